```python
import math
import jax
import jax.numpy as jnp
from jax import lax
import numpy as np

D_MODEL = 1024
BATCH = 2
SEQ = 8192
DEPTH = 1
DEC_BATCH = 128
DEC_SEQ = 1
PAST_LEN = 16384
PAGE_SIZE = 128

ATT_HEADS = 8
ATT_KV_HEADS = 2
HEAD_DIM = 64
GQA = ATT_HEADS // ATT_KV_HEADS
WINDOW = 128
ATT_BLOCK = 128
N_BUCKETS = 32
MAX_DISTANCE = WINDOW
DN_HEADS = 8
DN_DK = 64
DN_DV = 64
CONV_WIDTH = 4
DN_CHUNK = 64
ATT_WIDTH = ATT_HEADS * HEAD_DIM
KV_WIDTH = ATT_KV_HEADS * HEAD_DIM
DN_QK_WIDTH = DN_HEADS * DN_DK
DN_WIDTH = DN_HEADS * DN_DV
MIX_WIDTH = ATT_WIDTH + DN_WIDTH
CONV_CH = 2 * DN_QK_WIDTH + DN_WIDTH
IN_SPLITS = (ATT_WIDTH, KV_WIDTH, KV_WIDTH, DN_QK_WIDTH, DN_QK_WIDTH, DN_WIDTH, DN_HEADS, DN_HEADS, DN_WIDTH)
IN_WIDTH = sum(IN_SPLITS)
IN_OFFSETS = tuple(int(o) for o in np.cumsum(IN_SPLITS)[:-1])
N_GROUPS = 4
EXPERTS_PER_GROUP = 8
N_EXPERTS = N_GROUPS * EXPERTS_PER_GROUP
TOP_K = 2
D_EXPERT = 256
PLE_DIM = 256
EPS = 1e-6

kernel_name = 'hymba_swa_gdn_hier_moe_step'


def rmsnorm(x, g):
    xf = x.astype(jnp.float32)
    y = xf * lax.rsqrt(jnp.mean(xf * xf, axis=-1, keepdims=True) + EPS)
    return (y * g.astype(jnp.float32)).astype(x.dtype)


def l2norm(x):
    return x * lax.rsqrt(jnp.sum(x * x, axis=-1, keepdims=True) + EPS)


def t5_bucket(dist):
    max_exact = N_BUCKETS // 2
    d = jnp.maximum(dist, 0)
    ratio = jnp.log(jnp.maximum(d, 1).astype(jnp.float32) / max_exact) / math.log(MAX_DISTANCE / max_exact)
    large = jnp.minimum(max_exact + (ratio * (N_BUCKETS - max_exact)).astype(jnp.int32), N_BUCKETS - 1)
    return jnp.where(d < max_exact, d, large)


def head_bias(dist, rel_bias):
    b = rel_bias[t5_bucket(dist)].astype(jnp.float32)
    b = jnp.moveaxis(b, -1, 0)
    return b.reshape((ATT_KV_HEADS, GQA) + dist.shape)


def sink_softmax(s, sink, mask):
    s = jnp.where(mask, s, -jnp.inf)
    m = jnp.maximum(jnp.max(s, axis=-1, keepdims=True), sink)
    p = jnp.exp(s - m)
    return p / (jnp.sum(p, axis=-1, keepdims=True) + jnp.exp(sink - m))


def attn_prompt(q, k, v, rel_bias, att_sink):
    B, L = q.shape[:2]
    nb = L // ATT_BLOCK
    qb = q.reshape(B, nb, ATT_BLOCK, ATT_KV_HEADS, GQA, HEAD_DIM).astype(jnp.float32)
    kb = k.reshape(B, nb, ATT_BLOCK, ATT_KV_HEADS, HEAD_DIM)
    vb = v.reshape(B, nb, ATT_BLOCK, ATT_KV_HEADS, HEAD_DIM)
    pad = ((0, 0), (1, 0), (0, 0), (0, 0), (0, 0))
    kk = jnp.concatenate([jnp.pad(kb[:, :-1], pad), kb], axis=2).astype(jnp.float32)
    vv = jnp.concatenate([jnp.pad(vb[:, :-1], pad), vb], axis=2).astype(jnp.float32)
    qi = jnp.arange(ATT_BLOCK)[:, None]
    kj = jnp.arange(2 * ATT_BLOCK)[None, :]
    dist = qi + ATT_BLOCK - kj
    band = (dist >= 0) & (dist < WINDOW)
    not_before_start = (jnp.arange(nb)[:, None, None] > 0) | (kj[None] >= ATT_BLOCK)
    mask = (band[None] & not_before_start)[:, None, None]
    s = jnp.einsum('bnqkgd,bnjkd->bnkgqj', qb, kk) * (HEAD_DIM ** -0.5) + head_bias(dist, rel_bias)
    sink = att_sink.astype(jnp.float32).reshape(ATT_KV_HEADS, GQA, 1, 1)
    pr = sink_softmax(s, sink, mask)
    o = jnp.einsum('bnkgqj,bnjkd->bnqkgd', pr, vv).reshape(B, L, ATT_WIDTH).astype(q.dtype)
    buf = min(WINDOW, L)
    return o, k[:, L - buf:], v[:, L - buf:]


def attn_sample(q, k, v, cache_k, cache_v, rel_bias, att_sink):
    Bd, T = q.shape[:2]
    Lb = cache_k.shape[1]
    kk = jnp.concatenate([cache_k.astype(k.dtype), k], axis=1)
    vv = jnp.concatenate([cache_v.astype(v.dtype), v], axis=1)
    qh = q.reshape(Bd, T, ATT_KV_HEADS, GQA, HEAD_DIM).astype(jnp.float32)
    dist = (Lb + jnp.arange(T))[:, None] - jnp.arange(Lb + T)[None, :]
    mask = (dist >= 0) & (dist < WINDOW)
    s = jnp.einsum('btkgd,bjkd->bkgtj', qh, kk.astype(jnp.float32)) * (HEAD_DIM ** -0.5) + head_bias(dist, rel_bias)
    sink = att_sink.astype(jnp.float32).reshape(ATT_KV_HEADS, GQA, 1, 1)
    pr = sink_softmax(s, sink, mask)
    o = jnp.einsum('bkgtj,bjkd->btkgd', pr, vv.astype(jnp.float32)).reshape(Bd, T, ATT_WIDTH).astype(q.dtype)
    return o, kk[:, T:], vv[:, T:]


def gated_delta_chunked(q, k, v, g, beta, S0, chunk):
    B, L, H, dk = q.shape
    dv = v.shape[-1]
    n = L // chunk

    def blk(t):
        t = t.reshape((B, n, chunk, H) + t.shape[3:])
        return jnp.moveaxis(t, 3, 1)

    q, k, v, g, beta = blk(q), blk(k), blk(v), blk(g), blk(beta)
    gam = jnp.cumsum(g, axis=-1)
    causal = jnp.tril(jnp.ones((chunk, chunk), dtype=bool))
    strict = jnp.tril(jnp.ones((chunk, chunk), dtype=bool), -1)
    decay = jnp.exp(jnp.where(causal, gam[..., :, None] - gam[..., None, :], -jnp.inf))
    kb = k * beta[..., None]
    A = jnp.where(strict, jnp.einsum('bhncd,bhnjd->bhncj', kb, k) * decay, 0.0)
    eye = jnp.eye(chunk, dtype=A.dtype)
    rhs = jnp.concatenate([v * beta[..., None], kb * jnp.exp(gam)[..., None]], axis=-1)
    sol = lax.linalg.triangular_solve(eye + A, rhs, left_side=True, lower=True, unit_diagonal=True)
    u, w = sol[..., :dv], sol[..., dv:]
    qk = jnp.where(causal, jnp.einsum('bhncd,bhnjd->bhncj', q, k) * decay, 0.0)
    qg = q * jnp.exp(gam)[..., None]
    kd = k * jnp.exp(gam[..., -1:] - gam)[..., None]
    gl = jnp.exp(gam[..., -1])
    xs = tuple(jnp.moveaxis(t, 2, 0) for t in (u, w, qk, qg, kd, gl))

    def step(S, inp):
        u_c, w_c, qk_c, qg_c, kd_c, gl_c = inp
        v_new = u_c - jnp.einsum('bhcd,bhde->bhce', w_c, S)
        o = jnp.einsum('bhcd,bhde->bhce', qg_c, S) + jnp.einsum('bhcj,bhje->bhce', qk_c, v_new)
        S = S * gl_c[..., None, None] + jnp.einsum('bhcd,bhce->bhde', kd_c, v_new)
        return S, o

    S, o = lax.scan(step, S0, xs)
    o = jnp.moveaxis(jnp.moveaxis(o, 0, 2), 1, 3).reshape(B, L, H, dv)
    return o, S


def gated_deltanet(dq, dk, dv, db, da, dz, conv_prev, S0, conv_w, A_log, dt_bias, dn_norm):
    B, L, _ = dq.shape
    xc = jnp.concatenate([dq, dk, dv], axis=-1)
    xp = jnp.concatenate([conv_prev.astype(xc.dtype), xc], axis=1)
    y = xp[:, 0:L] * conv_w[0]
    for j in range(1, CONV_WIDTH):
        y = y + xp[:, j:j + L] * conv_w[j]
    y = jax.nn.silu(y.astype(jnp.float32))
    new_conv = xp[:, L:]
    q = l2norm(y[..., :DN_QK_WIDTH].reshape(B, L, DN_HEADS, DN_DK)) * (DN_DK ** -0.5)
    k = l2norm(y[..., DN_QK_WIDTH:2 * DN_QK_WIDTH].reshape(B, L, DN_HEADS, DN_DK))
    v = y[..., 2 * DN_QK_WIDTH:].reshape(B, L, DN_HEADS, DN_DV)
    beta = jax.nn.sigmoid(db.astype(jnp.float32))
    g = -jnp.exp(A_log.astype(jnp.float32)) * jax.nn.softplus(da.astype(jnp.float32) + dt_bias.astype(jnp.float32))
    chunk = DN_CHUNK if L % DN_CHUNK == 0 else L
    o, S = gated_delta_chunked(q, k, v, g, beta, S0.astype(jnp.float32), chunk)
    o = rmsnorm(o, dn_norm) * jax.nn.silu(dz.reshape(B, L, DN_HEADS, DN_DV).astype(jnp.float32))
    return o.reshape(B, L, DN_WIDTH).astype(dq.dtype), new_conv, S


def hier_moe(x, w_rg, w_re, w_gate, w_up, w_down):
    B, L, D = x.shape
    xt = x.reshape(B * L, D)
    pg = jax.nn.softmax((xt @ w_rg).astype(jnp.float32), axis=-1)
    gsel = jnp.argmax(pg, axis=-1)
    pgsel = jnp.take_along_axis(pg, gsel[:, None], axis=-1)
    le = (xt @ w_re).astype(jnp.float32).reshape(-1, N_GROUPS, EXPERTS_PER_GROUP)
    le = jnp.take_along_axis(le, gsel[:, None, None], axis=1)[:, 0]
    topv, topi = lax.top_k(jax.nn.softmax(le, axis=-1), TOP_K)
    topv = topv / jnp.sum(topv, axis=-1, keepdims=True)
    eidx = gsel[:, None] * EXPERTS_PER_GROUP + topi
    gates = jnp.sum(jax.nn.one_hot(eidx, N_EXPERTS, dtype=jnp.float32) * (topv * pgsel)[..., None], axis=1)
    hg = jnp.einsum('td,edf->tef', xt, w_gate)
    hu = jnp.einsum('td,edf->tef', xt, w_up)
    hm = jax.nn.silu(hg) * hu * gates[..., None].astype(x.dtype)
    return jnp.einsum('tef,efd->td', hm, w_down).reshape(B, L, D)


def layer(h, p, kv_cache, conv_prev, S0, lw, rel_bias):
    (norm_mix, w_in, att_sink, conv_w, A_log, dt_bias, dn_norm, w_out,
     norm_ffn, w_rg, w_re, w_gate, w_up, w_down, w_pp, w_pg, norm_ple) = lw
    B, L, _ = h.shape
    hn = rmsnorm(h, norm_mix)
    aq, ak, av, dq, dk, dv, db, da, dz = jnp.split(hn @ w_in, IN_OFFSETS, axis=-1)
    aq = aq.reshape(B, L, ATT_HEADS, HEAD_DIM)
    ak = ak.reshape(B, L, ATT_KV_HEADS, HEAD_DIM)
    av = av.reshape(B, L, ATT_KV_HEADS, HEAD_DIM)
    if kv_cache is None:
        o_att, k_new, v_new = attn_prompt(aq, ak, av, rel_bias, att_sink)
    else:
        o_att, k_new, v_new = attn_sample(aq, ak, av, kv_cache[0], kv_cache[1], rel_bias, att_sink)
    o_dn, conv_new, S_new = gated_deltanet(dq, dk, dv, db, da, dz, conv_prev, S0, conv_w, A_log, dt_bias, dn_norm)
    h = h + jnp.concatenate([o_att, o_dn], axis=-1) @ w_out
    h = h + hier_moe(rmsnorm(h, norm_ffn), w_rg, w_re, w_gate, w_up, w_down)
    h = h + (p.astype(h.dtype) @ w_pp) * jax.nn.sigmoid(rmsnorm(h, norm_ple) @ w_pg)
    return h, k_new, v_new, conv_new, S_new


def setup_inputs(seed: int = 0) -> dict:
    key = jax.random.key(seed)
    ks = jax.random.split(key, 32)
    f32 = jnp.float32

    def nrm(k, shape, s):
        return s * jax.random.normal(k, shape, f32)

    def gain(k, shape):
        return 1.0 + 0.02 * jax.random.normal(k, shape, f32)

    buf = min(WINDOW, PAST_LEN)
    dt = jnp.exp(jax.random.uniform(ks[16], (DEPTH, DN_HEADS), f32, math.log(1e-3), math.log(1e-1)))
    return {
        'x_prompt': nrm(ks[0], (BATCH, SEQ, D_MODEL), 1.0),
        'x_sample': nrm(ks[1], (DEC_BATCH, DEC_SEQ, D_MODEL), 1.0),
        'p_prompt': nrm(ks[2], (DEPTH, BATCH, SEQ, PLE_DIM), 1.0),
        'p_sample': nrm(ks[3], (DEPTH, DEC_BATCH, DEC_SEQ, PLE_DIM), 1.0),
        'cache_k': nrm(ks[4], (DEPTH, DEC_BATCH, buf, ATT_KV_HEADS, HEAD_DIM), 1.0),
        'cache_v': nrm(ks[5], (DEPTH, DEC_BATCH, buf, ATT_KV_HEADS, HEAD_DIM), 1.0),
        'state_conv': nrm(ks[6], (DEPTH, DEC_BATCH, CONV_WIDTH - 1, CONV_CH), 1.0),
        'state_S': nrm(ks[7], (DEPTH, DEC_BATCH, DN_HEADS, DN_DK, DN_DV), 0.1),
        'rel_bias': nrm(ks[8], (N_BUCKETS, ATT_HEADS), 0.5),
        'norm_mix': gain(ks[9], (DEPTH, D_MODEL)),
        'w_in': nrm(ks[10], (DEPTH, D_MODEL, IN_WIDTH), D_MODEL ** -0.5),
        'att_sink': nrm(ks[11], (DEPTH, ATT_HEADS), 0.5),
        'conv_w': nrm(ks[12], (DEPTH, CONV_WIDTH, CONV_CH), CONV_WIDTH ** -0.5),
        'dn_A_log': jnp.log(jax.random.uniform(ks[13], (DEPTH, DN_HEADS), f32, 1.0, 16.0)),
        'dn_dt_bias': dt + jnp.log(-jnp.expm1(-dt)),
        'dn_norm': gain(ks[14], (DEPTH, DN_DV)),
        'w_out': nrm(ks[15], (DEPTH, MIX_WIDTH, D_MODEL), MIX_WIDTH ** -0.5),
        'norm_ffn': gain(ks[17], (DEPTH, D_MODEL)),
        'w_router_group': nrm(ks[18], (DEPTH, D_MODEL, N_GROUPS), D_MODEL ** -0.5),
        'w_router_expert': nrm(ks[19], (DEPTH, D_MODEL, N_EXPERTS), D_MODEL ** -0.5),
        'w_gate': nrm(ks[20], (DEPTH, N_EXPERTS, D_MODEL, D_EXPERT), D_MODEL ** -0.5),
        'w_up': nrm(ks[21], (DEPTH, N_EXPERTS, D_MODEL, D_EXPERT), D_MODEL ** -0.5),
        'w_down': nrm(ks[22], (DEPTH, N_EXPERTS, D_EXPERT, D_MODEL), D_EXPERT ** -0.5),
        'w_ple_proj': nrm(ks[23], (DEPTH, PLE_DIM, D_MODEL), PLE_DIM ** -0.5),
        'w_ple_gate': nrm(ks[24], (DEPTH, D_MODEL, D_MODEL), D_MODEL ** -0.5),
        'norm_ple': gain(ks[25], (DEPTH, D_MODEL)),
        'norm_final': gain(ks[26], (D_MODEL,)),
    }


def reference(x_prompt, x_sample, p_prompt, p_sample, cache_k, cache_v, state_conv, state_S,
              rel_bias, norm_mix, w_in, att_sink, conv_w, dn_A_log, dn_dt_bias, dn_norm, w_out,
              norm_ffn, w_router_group, w_router_expert, w_gate, w_up, w_down,
              w_ple_proj, w_ple_gate, norm_ple, norm_final):
    hp, hs = x_prompt, x_sample
    kp_l, vp_l, cp_l, sp_l = [], [], [], []
    ks_l, vs_l, cs_l, ss_l = [], [], [], []
    for i in range(DEPTH):
        lw = (norm_mix[i], w_in[i], att_sink[i], conv_w[i], dn_A_log[i], dn_dt_bias[i], dn_norm[i], w_out[i],
              norm_ffn[i], w_router_group[i], w_router_expert[i], w_gate[i], w_up[i], w_down[i],
              w_ple_proj[i], w_ple_gate[i], norm_ple[i])
        conv0 = jnp.zeros((hp.shape[0], CONV_WIDTH - 1, CONV_CH), hp.dtype)
        S0 = jnp.zeros((hp.shape[0], DN_HEADS, DN_DK, DN_DV), jnp.float32)
        hp, kp, vp, cp, sp = layer(hp, p_prompt[i], None, conv0, S0, lw, rel_bias)
        hs, kn, vn, cn, sn = layer(hs, p_sample[i], (cache_k[i], cache_v[i]), state_conv[i], state_S[i], lw, rel_bias)
        kp_l.append(kp); vp_l.append(vp); cp_l.append(cp); sp_l.append(sp)
        ks_l.append(kn); vs_l.append(vn); cs_l.append(cn); ss_l.append(sn)
    y_prompt = rmsnorm(hp, norm_final)
    y_sample = rmsnorm(hs, norm_final)
    return (y_prompt, y_sample,
            jnp.stack(kp_l), jnp.stack(vp_l), jnp.stack(cp_l), jnp.stack(sp_l),
            jnp.stack(ks_l), jnp.stack(vs_l), jnp.stack(cs_l), jnp.stack(ss_l))
```

```python
import functools
import math

import numpy as np
import jax
import jax.numpy as jnp
from jax import lax
from jax.experimental import pallas as pl
from jax.experimental.pallas import tpu as pltpu

F32 = jnp.float32
BF16 = jnp.bfloat16

D_MODEL = 1024
ATT_HEADS = 8
ATT_KV_HEADS = 2
HEAD_DIM = 64
GQA = ATT_HEADS // ATT_KV_HEADS
WINDOW = 128
ATT_BLOCK = 128
N_BUCKETS = 32
DN_HEADS = 8
DN_DK = 64
DN_DV = 64
CONV_WIDTH = 4
DN_CHUNK = 64
ATT_WIDTH = ATT_HEADS * HEAD_DIM
KV_WIDTH = ATT_KV_HEADS * HEAD_DIM
DN_WIDTH = DN_HEADS * DN_DV
CONV_CH = 3 * DN_WIDTH
N_GROUPS = 4
EXPERTS_PER_GROUP = 8
N_EXPERTS = N_GROUPS * EXPERTS_PER_GROUP
D_EXPERT = 256
PLE_DIM = 256
EPS = 1e-6
NEG_INF = float("-inf")

ATT_COLS = ATT_WIDTH + 2 * KV_WIDTH
LANES = 128
IN_COLS = ATT_COLS + CONV_CH + DN_WIDTH + LANES
ROUTER_OFF = N_GROUPS
VMEM_LIMIT = 48 * 1024 * 1024


def _params(*sem):
    return pltpu.CompilerParams(dimension_semantics=sem, vmem_limit_bytes=VMEM_LIMIT)


def _mm(a, b):
    return jnp.dot(a.astype(BF16), b.astype(BF16), preferred_element_type=F32)


def _mm_nt(a, b):
    return lax.dot_general(a.astype(BF16), b.astype(BF16), (((1,), (1,)), ((), ())),
                           preferred_element_type=F32)


def _mm_tn(a, b):
    return lax.dot_general(a.astype(BF16), b.astype(BF16), (((0,), (0,)), ((), ())),
                           preferred_element_type=F32)


def _split3(x):
    h1 = x.astype(BF16)
    r1 = x - h1.astype(F32)
    h2 = r1.astype(BF16)
    h3 = (r1 - h2.astype(F32)).astype(BF16)
    return h1, h2, h3


def _mm_sel_rhs(x, sel):
    h1, h2, h3 = _split3(x)
    d = lambda h: jnp.dot(h, sel, preferred_element_type=F32)
    return d(h1) + d(h2) + d(h3)


def _mm_sel_lhs(sel, x):
    h1, h2, h3 = _split3(x)
    d = lambda h: jnp.dot(sel, h, preferred_element_type=F32)
    return d(h1) + d(h2) + d(h3)


def _mm3(a, b):
    ah = a.astype(BF16)
    al = (a - ah.astype(F32)).astype(BF16)
    bh = b.astype(BF16)
    bl = (b - bh.astype(F32)).astype(BF16)
    d = lambda u, v: jnp.dot(u, v, preferred_element_type=F32)
    return d(ah, bh) + d(ah, bl) + d(al, bh)


def _sigmoid(x):
    return 1.0 / (1.0 + jnp.exp(-x))


def _silu(x):
    return x * _sigmoid(x)


def _softplus(x):
    return jnp.maximum(x, 0.0) + jnp.log(1.0 + jnp.exp(-jnp.abs(x)))


def _rmsnorm(x, g):
    return x * lax.rsqrt(jnp.mean(x * x, axis=-1, keepdims=True) + EPS) * g


def _t5_bucket_np(dist):
    max_exact = N_BUCKETS // 2
    d = np.maximum(dist, 0)
    ratio = (np.log(np.maximum(d, 1).astype(np.float32) / np.float32(max_exact))
             / np.float32(math.log(WINDOW / max_exact))).astype(np.float32)
    large = np.minimum(max_exact + (ratio * np.float32(N_BUCKETS - max_exact)).astype(np.int32),
                       N_BUCKETS - 1)
    return np.where(d < max_exact, d, large).astype(np.int32)


def _bias_lookup(bucket, rb_ref, h):
    acc = jnp.zeros(bucket.shape, F32)
    for t in range(N_BUCKETS):
        acc = jnp.where(bucket == t, rb_ref[t, h], acc)
    return acc


def _inproj_kernel(x_ref, g_ref, w_ref, att_ref, xc_ref, dz_ref, ba_ref):
    xn = _rmsnorm(x_ref[...], g_ref[...]).astype(BF16)
    o0, o1, o2 = ATT_COLS, ATT_COLS + CONV_CH, ATT_COLS + CONV_CH + DN_WIDTH
    att_ref[...] = jnp.dot(xn, w_ref[:, :o0], preferred_element_type=F32)
    xc_ref[...] = jnp.dot(xn, w_ref[:, o0:o1], preferred_element_type=F32)
    dz_ref[...] = jnp.dot(xn, w_ref[:, o1:o2], preferred_element_type=F32)
    ba_ref[...] = jnp.dot(xn, w_ref[:, o2:], preferred_element_type=F32)


def _inproj(x, g, w):
    t = x.shape[0]
    tm = min(t, 256)
    row = lambda n: pl.BlockSpec((tm, n), lambda i: (i, 0))
    full = lambda a: pl.BlockSpec(a.shape, lambda i: (0,) * a.ndim)
    return pl.pallas_call(
        _inproj_kernel,
        grid=(t // tm,),
        in_specs=[row(D_MODEL), full(g), full(w)],
        out_specs=[row(ATT_COLS), row(CONV_CH), row(DN_WIDTH), row(LANES)],
        out_shape=[jax.ShapeDtypeStruct((t, n), F32) for n in (ATT_COLS, CONV_CH, DN_WIDTH, LANES)],
        compiler_params=_params("parallel"),
        name="inproj",
    )(x, g, w)


def _attn_prompt_kernel(cur_ref, prev_ref, bucket_ref, rb_ref, sink_ref, o_ref, bias_scr):
    first = jnp.logical_and(pl.program_id(0) == 0, pl.program_id(1) == 0)
    qi = lax.broadcasted_iota(jnp.int32, (ATT_BLOCK, 2 * ATT_BLOCK), 0)
    kj = lax.broadcasted_iota(jnp.int32, (ATT_BLOCK, 2 * ATT_BLOCK), 1)

    @pl.when(first)
    def _():
        dist = qi + ATT_BLOCK - kj
        band = jnp.logical_and(dist >= 0, dist < WINDOW)
        bucket = bucket_ref[...]
        for h in range(ATT_HEADS):
            bias_scr[h] = jnp.where(band, _bias_lookup(bucket, rb_ref, h), NEG_INF)

    cur = cur_ref[...]
    prev = prev_ref[...]
    q = cur[:, :ATT_WIDTH] * (HEAD_DIM ** -0.5)
    kcat = jnp.concatenate([prev[:, ATT_WIDTH:ATT_WIDTH + KV_WIDTH],
                            cur[:, ATT_WIDTH:ATT_WIDTH + KV_WIDTH]], axis=0)
    vcat = jnp.concatenate([prev[:, ATT_WIDTH + KV_WIDTH:], cur[:, ATT_WIDTH + KV_WIDTH:]], axis=0)
    keep = jnp.logical_or(pl.program_id(1) > 0, kj >= ATT_BLOCK)
    outs = []
    for h in range(ATT_HEADS):
        g = h // GQA
        qh = q[:, h * HEAD_DIM:(h + 1) * HEAD_DIM]
        kh = kcat[:, g * HEAD_DIM:(g + 1) * HEAD_DIM]
        vh = vcat[:, g * HEAD_DIM:(g + 1) * HEAD_DIM]
        s = jnp.where(keep, _mm_nt(qh, kh) + bias_scr[h], NEG_INF)
        sink = sink_ref[h]
        m = jnp.maximum(jnp.max(s, axis=-1, keepdims=True), sink)
        p = jnp.exp(s - m)
        den = jnp.sum(p, axis=-1, keepdims=True) + jnp.exp(sink - m)
        outs.append(_mm(p, vh) / den)
    o_ref[...] = jnp.concatenate(outs, axis=1)


def _attn_prompt(att, bucket, rel_bias, sink, batch, seq):
    nb = seq // ATT_BLOCK
    smem = pl.BlockSpec(memory_space=pltpu.SMEM)
    return pl.pallas_call(
        _attn_prompt_kernel,
        grid=(batch, nb),
        in_specs=[
            pl.BlockSpec((ATT_BLOCK, ATT_COLS), lambda b, i: (b * nb + i, 0)),
            pl.BlockSpec((ATT_BLOCK, ATT_COLS), lambda b, i: (b * nb + jnp.maximum(i - 1, 0), 0)),
            pl.BlockSpec(bucket.shape, lambda b, i: (0, 0)),
            smem, smem,
        ],
        out_specs=pl.BlockSpec((ATT_BLOCK, ATT_WIDTH), lambda b, i: (b * nb + i, 0)),
        out_shape=jax.ShapeDtypeStruct((batch * seq, ATT_WIDTH), F32),
        scratch_shapes=[pltpu.VMEM((ATT_HEADS, ATT_BLOCK, 2 * ATT_BLOCK), F32)],
        compiler_params=_params("arbitrary", "arbitrary"),
        name="attn_prompt",
    )(att, att, bucket, rel_bias, sink)


ATT_S_BB = 8


def _attn_sample_kernel(att_ref, ck_ref, cv_ref, bucket_ref, rb_ref, sink_ref, o_ref,
                        bias_scr, col_scr):
    hrow = lax.broadcasted_iota(jnp.int32, (ATT_HEADS, LANES), 0)
    lane = lax.broadcasted_iota(jnp.int32, (ATT_HEADS, LANES), 1)

    @pl.when(pl.program_id(0) == 0)
    def _():
        bucket = jnp.broadcast_to(bucket_ref[...], (ATT_HEADS, LANES))
        bias = jnp.zeros((ATT_HEADS, LANES), F32)
        cols = jnp.zeros((ATT_HEADS, LANES), F32)
        for h in range(ATT_HEADS):
            bias = jnp.where(hrow == h, _bias_lookup(bucket, rb_ref, h), bias)
            cols = jnp.where(jnp.logical_and(hrow == h, lane == 0), sink_ref[h], cols)
            cols = jnp.where(jnp.logical_and(hrow == h, lane == 1), rb_ref[0, h], cols)
        bias_scr[...] = jnp.where(lane >= 1, bias, NEG_INF)
        col_scr[...] = cols

    bias_c = bias_scr[...]
    sink = col_scr[:, 0:1]
    bias_n = col_scr[:, 1:2]
    same_group = (hrow // GQA) == (lane // HEAD_DIM)
    low_group = lax.broadcasted_iota(jnp.int32, (ATT_HEADS, HEAD_DIM), 0) < GQA
    for b in range(ATT_S_BB):
        row = att_ref[b:b + 1, :]
        q = row[:, :ATT_WIDTH] * (HEAD_DIM ** -0.5)
        kn = row[:, ATT_WIDTH:ATT_WIDTH + KV_WIDTH]
        vn = row[:, ATT_WIDTH + KV_WIDTH:]
        qh = jnp.concatenate([q[:, h * HEAD_DIM:(h + 1) * HEAD_DIM] for h in range(ATT_HEADS)], axis=0)
        q_bd = jnp.where(same_group, jnp.concatenate([qh, qh], axis=1), 0.0)
        rnd = lambda a: a.astype(BF16).astype(F32)
        s_c = _mm_nt(q_bd, ck_ref[b]) + bias_c
        s_n = jnp.sum(rnd(q_bd) * rnd(kn), axis=-1, keepdims=True) + bias_n
        m = jnp.maximum(jnp.maximum(jnp.max(s_c, axis=-1, keepdims=True), s_n), sink)
        p_c = jnp.exp(s_c - m)
        p_n = jnp.exp(s_n - m)
        den = jnp.sum(p_c, axis=-1, keepdims=True) + p_n + jnp.exp(sink - m)
        o_full = _mm(p_c / den, cv_ref[b]) + rnd(p_n / den) * rnd(vn)
        o_sel = jnp.where(low_group, o_full[:, :HEAD_DIM], o_full[:, HEAD_DIM:])
        o_ref[b:b + 1, :] = jnp.concatenate([o_sel[h:h + 1, :] for h in range(ATT_HEADS)], axis=1)


def _attn_sample(att, ck, cv, bucket, rel_bias, sink):
    nseq = att.shape[0]
    smem = pl.BlockSpec(memory_space=pltpu.SMEM)
    cache = pl.BlockSpec((ATT_S_BB, WINDOW, KV_WIDTH), lambda i: (i, 0, 0))
    return pl.pallas_call(
        _attn_sample_kernel,
        grid=(nseq // ATT_S_BB,),
        in_specs=[pl.BlockSpec((ATT_S_BB, ATT_COLS), lambda i: (i, 0)), cache, cache,
                  pl.BlockSpec(bucket.shape, lambda i: (0, 0)), smem, smem],
        out_specs=pl.BlockSpec((ATT_S_BB, ATT_WIDTH), lambda i: (i, 0)),
        out_shape=jax.ShapeDtypeStruct((nseq, ATT_WIDTH), F32),
        scratch_shapes=[pltpu.VMEM((ATT_HEADS, LANES), F32), pltpu.VMEM((ATT_HEADS, LANES), F32)],
        compiler_params=_params("arbitrary"),
        name="attn_sample",
    )(att, ck, cv, bucket, rel_bias, sink)


GDN_TB = 128
GDN_NC = GDN_TB // DN_CHUNK
TAIL = 8


def _gdn_gates(ba, alog, dtb):
    beta = _sigmoid(ba)
    g = -jnp.exp(alog) * _softplus(ba + dtb)
    return beta, g


def _gdn_prompt_kernel(xc_ref, dz_ref, ba_ref, cw_ref, alog_ref, dtb_ref, dnx_ref,
                       hsum_ref, expb_ref, expg_ref, ltri_ref,
                       o_ref, s_out_ref, xp_scr, s_scr):
    i = pl.program_id(1)

    @pl.when(i == 0)
    def _():
        xp_scr[0:TAIL, :] = jnp.zeros((TAIL, CONV_CH), F32)
        s_scr[...] = jnp.zeros(s_scr.shape, F32)

    xc = xc_ref[...]
    xp_scr[TAIL:, :] = xc
    y = xp_scr[TAIL - 3:TAIL - 3 + GDN_TB, :] * cw_ref[0:1, :]
    y = y + xp_scr[TAIL - 2:TAIL - 2 + GDN_TB, :] * cw_ref[1:2, :]
    y = y + xp_scr[TAIL - 1:TAIL - 1 + GDN_TB, :] * cw_ref[2:3, :]
    y = y + xc * cw_ref[3:4, :]
    xp_scr[0:TAIL, :] = xc[GDN_TB - TAIL:, :]
    y = _silu(y)

    hsum = hsum_ref[...]
    q = y[:, :DN_WIDTH]
    k = y[:, DN_WIDTH:2 * DN_WIDTH]
    v = y[:, 2 * DN_WIDTH:]
    q = q * lax.rsqrt(_mm_sel_rhs(q * q, hsum) + EPS) * (DN_DK ** -0.5)
    k = k * lax.rsqrt(_mm_sel_rhs(k * k, hsum) + EPS)

    beta_c, g_c = _gdn_gates(ba_ref[...], alog_ref[...], dtb_ref[...])
    beta = _mm_sel_rhs(beta_c, expb_ref[...])
    gam_c = _mm_sel_lhs(ltri_ref[...], g_c)
    gam = _mm_sel_rhs(gam_c, expg_ref[...])
    gam_t = gam_c.T

    kb = k * beta
    vb = v * beta
    egam = jnp.exp(gam)
    qg = q * egam
    wr = kb * egam

    ri = lax.broadcasted_iota(jnp.int32, (DN_CHUNK, DN_CHUNK), 0)
    ci = lax.broadcasted_iota(jnp.int32, (DN_CHUNK, DN_CHUNK), 1)
    causal = ri >= ci
    strict = ri > ci
    eye = (ri == ci).astype(F32)

    o_tiles = []
    for c in range(GDN_NC):
        r0, r1 = c * DN_CHUNK, (c + 1) * DN_CHUNK
        gam_last = gam[r1 - 1:r1, :]
        kd = k[r0:r1, :] * jnp.exp(gam_last - gam[r0:r1, :])
        gl = jnp.exp(gam_last)
        o_heads = []
        for h in range(DN_HEADS):
            l0, l1 = h * DN_DK, (h + 1) * DN_DK
            qh, kh, kbh = q[r0:r1, l0:l1], k[r0:r1, l0:l1], kb[r0:r1, l0:l1]
            gcol = gam[r0:r1, l0:l1]
            grow = gam_t[DN_HEADS + h:DN_HEADS + h + 1, r0:r1]
            decay = jnp.exp(jnp.where(causal, gcol - grow, NEG_INF))
            a = jnp.where(strict, _mm_nt(kbh, kh) * decay, 0.0)
            p = -a
            t = eye + p
            for _ in range(5):
                p = _mm3(p, p)
                t = t + _mm3(t, p)
            rhs = jnp.concatenate([vb[r0:r1, l0:l1], wr[r0:r1, l0:l1]], axis=1)
            sol = _mm3(t, rhs)
            u, w = sol[:, :DN_DV], sol[:, DN_DV:]
            qk = jnp.where(causal, _mm_nt(qh, kh) * decay, 0.0)
            s_old = s_scr[h]
            v_new = u - _mm(w, s_old)
            o_heads.append(_mm(qg[r0:r1, l0:l1], s_old) + _mm(qk, v_new))
            s_scr[h] = s_old * gl[:, l0:l1] + _mm_tn(kd[:, l0:l1], v_new)
        o_tiles.append(jnp.concatenate(o_heads, axis=1))
    o = jnp.concatenate(o_tiles, axis=0)
    ms = _mm_sel_rhs(o * o, hsum) * (1.0 / DN_DV)
    o = o * lax.rsqrt(ms + EPS) * dnx_ref[...]
    o_ref[...] = o * _silu(dz_ref[...])

    @pl.when(i == pl.num_programs(1) - 1)
    def _():
        s_out_ref[...] = s_scr[...]


def _gdn_consts():
    lane = np.arange(DN_WIDTH)
    hsum = (lane[:, None] // DN_DV == lane[None, :] // DN_DV)
    src = np.arange(LANES)
    expb = (src[:, None] == lane[None, :] // DN_DV)
    expg = (src[:, None] == DN_HEADS + lane[None, :] // DN_DV)
    tok = np.arange(GDN_TB)
    ltri = np.logical_and(tok[:, None] >= tok[None, :],
                          tok[:, None] // DN_CHUNK == tok[None, :] // DN_CHUNK)
    as_bf16 = lambda m: jnp.asarray(m.astype(np.float32), dtype=BF16)
    return as_bf16(hsum), as_bf16(expb), as_bf16(expg), as_bf16(ltri)


def _gdn_prompt(xc, dz, ba, conv_w, alog, dtb, dnx, batch, seq):
    nt = seq // GDN_TB
    hsum, expb, expg, ltri = _gdn_consts()
    row = lambda n: pl.BlockSpec((GDN_TB, n), lambda b, i: (b * nt + i, 0))
    full = lambda a: pl.BlockSpec(a.shape, lambda b, i: (0,) * a.ndim)
    consts = (conv_w, alog, dtb, dnx, hsum, expb, expg, ltri)
    return pl.pallas_call(
        _gdn_prompt_kernel,
        grid=(batch, nt),
        in_specs=[row(CONV_CH), row(DN_WIDTH), row(LANES)] + [full(a) for a in consts],
        out_specs=[row(DN_WIDTH),
                   pl.BlockSpec((None, DN_HEADS, DN_DK, DN_DV), lambda b, i: (b, 0, 0, 0))],
        out_shape=[jax.ShapeDtypeStruct((batch * seq, DN_WIDTH), F32),
                   jax.ShapeDtypeStruct((batch, DN_HEADS, DN_DK, DN_DV), F32)],
        scratch_shapes=[pltpu.VMEM((TAIL + GDN_TB, CONV_CH), F32),
                        pltpu.VMEM((DN_HEADS, DN_DK, DN_DV), F32)],
        compiler_params=_params("arbitrary", "arbitrary"),
        name="gdn_prompt",
    )(xc, dz, ba, *consts)


GDN_S_BB = 8


def _gdn_sample_kernel(xc_ref, dz_ref, ba_ref, sc_ref, s_ref, cw_ref, alog_ref, dtb_ref, dn_ref,
                       hsum_ref, eye_ref, hsel_ref, hrep_ref, o_ref, s_out_ref):
    xc = xc_ref[...]
    y = sc_ref[0] * cw_ref[0:1, :]
    y = y + sc_ref[1] * cw_ref[1:2, :]
    y = y + sc_ref[2] * cw_ref[2:3, :]
    y = _silu(y + xc * cw_ref[3:4, :])
    hsum = hsum_ref[...]
    q = y[:, :DN_WIDTH]
    k = y[:, DN_WIDTH:2 * DN_WIDTH]
    v = y[:, 2 * DN_WIDTH:]
    q = q * lax.rsqrt(_mm_sel_rhs(q * q, hsum) + EPS) * (DN_DK ** -0.5)
    k = k * lax.rsqrt(_mm_sel_rhs(k * k, hsum) + EPS)
    beta_c, g_c = _gdn_gates(ba_ref[...], alog_ref[...], dtb_ref[...])
    eg_c = jnp.exp(g_c)
    eye = eye_ref[...]
    tr = lambda a: lax.dot_general(a, eye, (((0,), (0,)), ((), ())), precision=lax.Precision.HIGHEST,
                                   preferred_element_type=F32)
    k_t = tr(k)
    q_t = tr(q)
    beta_t = tr(beta_c)
    eg_t = tr(eg_c)
    dz = dz_ref[...]
    dn = dn_ref[...]
    split = lambda r: jnp.concatenate([r[:, h * DN_DV:(h + 1) * DN_DV] for h in range(DN_HEADS)], axis=0)
    hsel = hsel_ref[...]
    hrep = hrep_ref[...]
    for b in range(GDN_S_BB):
        s2 = s_ref[b]
        kc = k_t[:, b:b + 1]
        qc = q_t[:, b:b + 1]
        beta = beta_t[0:DN_HEADS, b:b + 1]
        eg = eg_t[DN_HEADS:2 * DN_HEADS, b:b + 1]
        vh = split(v[b:b + 1, :])
        qh = split(q[b:b + 1, :])
        kh = split(k[b:b + 1, :])
        ks = _mm_sel_lhs(hsel, kc * s2)
        v_new = beta * (vh - eg * ks)
        qs = _mm_sel_lhs(hsel, qc * s2)
        qk = jnp.sum(qh * kh, axis=-1, keepdims=True)
        o = eg * qs + qk * v_new
        rep = _mm_sel_lhs(hrep, jnp.concatenate(
            [v_new, jnp.broadcast_to(eg, (DN_HEADS, DN_DV))], axis=1))
        s_out_ref[b] = s2 * rep[:, DN_DV:] + kc * rep[:, :DN_DV]
        o = _rmsnorm(o, dn) * _silu(split(dz[b:b + 1, :]))
        o_ref[b] = o


def _gdn_sample(xc, dz, ba, sconv_t, state, conv_w, alog, dtb, dn):
    nseq = xc.shape[0]
    lane = np.arange(DN_WIDTH)
    hsum = jnp.asarray((lane[:, None] // DN_DV == lane[None, :] // DN_DV).astype(np.float32), dtype=BF16)
    eye = jnp.eye(GDN_S_BB, dtype=F32)
    hsel_np = (np.arange(DN_HEADS)[:, None] == lane[None, :] // DN_DK).astype(np.float32)
    hsel = jnp.asarray(hsel_np, dtype=BF16)
    hrep = jnp.asarray(hsel_np.T, dtype=BF16)
    row = lambda n: pl.BlockSpec((GDN_S_BB, n), lambda i: (i, 0))
    full = lambda a: pl.BlockSpec(a.shape, lambda i: (0,) * a.ndim)
    st = pl.BlockSpec((GDN_S_BB, DN_HEADS * DN_DK, DN_DV), lambda i: (i, 0, 0))
    consts = (conv_w, alog, dtb, dn, hsum, eye, hsel, hrep)
    return pl.pallas_call(
        _gdn_sample_kernel,
        grid=(nseq // GDN_S_BB,),
        in_specs=[row(CONV_CH), row(DN_WIDTH), row(LANES),
                  pl.BlockSpec((CONV_WIDTH - 1, GDN_S_BB, CONV_CH), lambda i: (0, i, 0)), st]
                 + [full(a) for a in consts],
        out_specs=[pl.BlockSpec((GDN_S_BB, DN_HEADS, DN_DV), lambda i: (i, 0, 0)), st],
        out_shape=[jax.ShapeDtypeStruct((nseq, DN_HEADS, DN_DV), F32),
                   jax.ShapeDtypeStruct(state.shape, F32)],
        compiler_params=_params("parallel"),
        name="gdn_sample",
    )(xc, dz, ba, sconv_t, state, *consts)


def _outproj_router_kernel(x_ref, oa_ref, od_ref, wo_ref, g_ref, wr_ref, h_ref, xn_ref, gate_ref):
    h = x_ref[...] + _mm(oa_ref[...], wo_ref[:ATT_WIDTH, :]) + _mm(od_ref[...], wo_ref[ATT_WIDTH:, :])
    h_ref[...] = h
    xn = _rmsnorm(h, g_ref[...]).astype(BF16)
    xn_ref[...] = xn
    logits = jnp.dot(xn, wr_ref[...], preferred_element_type=F32)
    lane = lax.broadcasted_iota(jnp.int32, logits.shape, 1).astype(F32)
    first_at = lambda hit: jnp.min(jnp.where(hit, lane, float(LANES)), axis=-1, keepdims=True)
    glog = jnp.where(lane < N_GROUPS, logits, NEG_INF)
    gmax = jnp.max(glog, axis=-1, keepdims=True)
    gsel = first_at(glog == gmax)
    pgsel = 1.0 / jnp.sum(jnp.exp(glog - gmax), axis=-1, keepdims=True)
    lo = ROUTER_OFF + gsel * EXPERTS_PER_GROUP
    in_group = jnp.logical_and(lane >= lo, lane < lo + EXPERTS_PER_GROUP)
    elog = jnp.where(in_group, logits, NEG_INF)
    m1 = jnp.max(elog, axis=-1, keepdims=True)
    i1 = first_at(elog == m1)
    z = jnp.sum(jnp.exp(elog - m1), axis=-1, keepdims=True)
    elog2 = jnp.where(lane == i1, NEG_INF, elog)
    m2 = jnp.max(elog2, axis=-1, keepdims=True)
    i2 = first_at(elog2 == m2)
    p1 = 1.0 / z
    p2 = jnp.exp(m2 - m1) / z
    tot = p1 + p2
    gate_ref[...] = (jnp.where(lane == i1, p1 / tot * pgsel, 0.0)
                     + jnp.where(lane == i2, p2 / tot * pgsel, 0.0))


def _outproj_router(x, oa, od, wo, g, wr):
    t = x.shape[0]
    tm = min(t, 256)
    row = lambda n: pl.BlockSpec((tm, n), lambda i: (i, 0))
    full = lambda a: pl.BlockSpec(a.shape, lambda i: (0,) * a.ndim)
    return pl.pallas_call(
        _outproj_router_kernel,
        grid=(t // tm,),
        in_specs=[row(D_MODEL), row(ATT_WIDTH), row(DN_WIDTH), full(wo), full(g), full(wr)],
        out_specs=[row(D_MODEL), row(D_MODEL), row(LANES)],
        out_shape=[jax.ShapeDtypeStruct((t, D_MODEL), F32), jax.ShapeDtypeStruct((t, D_MODEL), BF16),
                   jax.ShapeDtypeStruct((t, LANES), F32)],
        compiler_params=_params("parallel"),
        name="outproj_router",
    )(x, oa, od, wo, g, wr)


def _moe_kernel(xn_ref, gate_ref, wg_ref, wu_ref, wd_ref, o_ref):
    e = pl.program_id(1)
    xn = xn_ref[...]
    lane = lax.broadcasted_iota(jnp.int32, gate_ref.shape, 1)
    gate = jnp.sum(jnp.where(lane == e + ROUTER_OFF, gate_ref[...], 0.0), axis=-1, keepdims=True)
    hg = jnp.dot(xn, wg_ref[...], preferred_element_type=F32)
    hu = jnp.dot(xn, wu_ref[...], preferred_element_type=F32)
    hm = _silu(hg) * hu * gate
    y = jnp.dot(hm.astype(BF16), wd_ref[...], preferred_element_type=F32)

    @pl.when(e == 0)
    def _():
        o_ref[...] = y

    @pl.when(e > 0)
    def _():
        o_ref[...] += y


def _moe(xn, gates, wg, wu, wd):
    t = xn.shape[0]
    tm = min(t, 1024)
    return pl.pallas_call(
        _moe_kernel,
        grid=(t // tm, N_EXPERTS),
        in_specs=[pl.BlockSpec((tm, D_MODEL), lambda i, e: (i, 0)),
                  pl.BlockSpec((tm, LANES), lambda i, e: (i, 0)),
                  pl.BlockSpec((None, D_MODEL, D_EXPERT), lambda i, e: (e, 0, 0)),
                  pl.BlockSpec((None, D_MODEL, D_EXPERT), lambda i, e: (e, 0, 0)),
                  pl.BlockSpec((None, D_EXPERT, D_MODEL), lambda i, e: (e, 0, 0))],
        out_specs=pl.BlockSpec((tm, D_MODEL), lambda i, e: (i, 0)),
        out_shape=jax.ShapeDtypeStruct((t, D_MODEL), F32),
        compiler_params=_params("parallel", "arbitrary"),
        name="moe",
    )(xn, gates, wg, wu, wd)


def _ple_final_kernel(h_ref, m_ref, p_ref, wpp_ref, wpg_ref, gp_ref, gf_ref, y_ref):
    h = h_ref[...] + m_ref[...]
    hn = _rmsnorm(h, gp_ref[...])
    h = h + _mm(p_ref[...], wpp_ref[...]) * _sigmoid(_mm(hn, wpg_ref[...]))
    y_ref[...] = _rmsnorm(h, gf_ref[...])


def _ple_final(h, m, p, wpp, wpg, gp, gf):
    t = h.shape[0]
    tm = min(t, 256)
    row = lambda n: pl.BlockSpec((tm, n), lambda i: (i, 0))
    full = lambda a: pl.BlockSpec(a.shape, lambda i: (0,) * a.ndim)
    return pl.pallas_call(
        _ple_final_kernel,
        grid=(t // tm,),
        in_specs=[row(D_MODEL), row(D_MODEL), row(PLE_DIM), full(wpp), full(wpg), full(gp), full(gf)],
        out_specs=row(D_MODEL),
        out_shape=jax.ShapeDtypeStruct((t, D_MODEL), F32),
        compiler_params=_params("parallel"),
        name="ple_final",
    )(h, m, p, wpp, wpg, gp, gf)


def kernel(x_prompt, x_sample, p_prompt, p_sample, cache_k, cache_v, state_conv, state_S, rel_bias, norm_mix, w_in, att_sink, conv_w, dn_A_log, dn_dt_bias, dn_norm, w_out, norm_ffn, w_router_group, w_router_expert, w_gate, w_up, w_down, w_ple_proj, w_ple_gate, norm_ple, norm_final):
    batch, seq, _ = x_prompt.shape
    nseq = x_sample.shape[0]
    assert x_sample.shape[1] == 1 and norm_mix.shape[0] == 1 and cache_k.shape[2] == WINDOW
    assert seq % GDN_TB == 0 and seq % ATT_BLOCK == 0

    wi = w_in[0]
    o_db = ATT_COLS + CONV_CH
    w_in_re = jnp.concatenate(
        [wi[:, :o_db], wi[:, o_db + 2 * DN_HEADS:], wi[:, o_db:o_db + 2 * DN_HEADS],
         jnp.zeros((D_MODEL, LANES - 2 * DN_HEADS), F32)], axis=1).astype(BF16)
    row = lambda a: a.reshape(1, -1).astype(F32)
    pad_lanes = lambda a, off: jnp.zeros((1, LANES), F32).at[0, off:off + a.shape[0]].set(a)
    alog = pad_lanes(dn_A_log[0], DN_HEADS)
    dtb = pad_lanes(dn_dt_bias[0], DN_HEADS)
    dnx = jnp.tile(dn_norm[0], DN_HEADS).reshape(1, DN_WIDTH)
    w_router = jnp.concatenate(
        [w_router_group[0], w_router_expert[0],
         jnp.zeros((D_MODEL, LANES - N_GROUPS - N_EXPERTS), F32)], axis=1).astype(BF16)
    wo = w_out[0].astype(BF16)
    wg, wu, wd = w_gate[0].astype(BF16), w_up[0].astype(BF16), w_down[0].astype(BF16)
    wpp, wpg = w_ple_proj[0].astype(BF16), w_ple_gate[0].astype(BF16)
    sink = att_sink[0]

    qi = np.arange(ATT_BLOCK)[:, None]
    kj = np.arange(2 * ATT_BLOCK)[None, :]
    bucket_p = jnp.asarray(_t5_bucket_np(qi + ATT_BLOCK - kj))
    bucket_s = jnp.asarray(_t5_bucket_np(WINDOW - np.arange(WINDOW)[None, :]))

    def tail(x, o_att, o_dn, p):
        h1, xn2, gates = _outproj_router(x, o_att, o_dn, wo, row(norm_ffn[0]), w_router)
        moe = _moe(xn2, gates, wg, wu, wd)
        return _ple_final(h1, moe, p, wpp, wpg, row(norm_ple[0]), row(norm_final))

    xp = x_prompt.reshape(batch * seq, D_MODEL)
    att_p, xc_p, dz_p, ba_p = _inproj(xp, row(norm_mix[0]), w_in_re)
    o_att_p = _attn_prompt(att_p, bucket_p, rel_bias, sink, batch, seq)
    o_dn_p, s_p = _gdn_prompt(xc_p, dz_p, ba_p, conv_w[0], alog, dtb, dnx, batch, seq)
    y_p = tail(xp, o_att_p, o_dn_p, p_prompt[0].reshape(batch * seq, PLE_DIM))

    xs = x_sample.reshape(nseq, D_MODEL)
    att_s, xc_s, dz_s, ba_s = _inproj(xs, row(norm_mix[0]), w_in_re)
    ck = cache_k[0].reshape(nseq, WINDOW, KV_WIDTH)
    cv = cache_v[0].reshape(nseq, WINDOW, KV_WIDTH)
    o_att_s = _attn_sample(att_s, ck, cv, bucket_s, rel_bias, sink)
    sconv_t = jnp.swapaxes(state_conv[0], 0, 1)
    o_dn_s, s_s = _gdn_sample(xc_s, dz_s, ba_s, sconv_t,
                              state_S[0].reshape(nseq, DN_HEADS * DN_DK, DN_DV), conv_w[0], alog, dtb,
                              dn_norm[0].reshape(1, DN_DV))
    s_s = s_s.reshape(nseq, DN_HEADS, DN_DK, DN_DV)
    y_s = tail(xs, o_att_s, o_dn_s.reshape(nseq, DN_WIDTH), p_sample[0].reshape(nseq, PLE_DIM))

    att_p3 = att_p.reshape(batch, seq, ATT_COLS)
    kv_shape = (1, batch, WINDOW, ATT_KV_HEADS, HEAD_DIM)
    k_p = att_p3[:, seq - WINDOW:, ATT_WIDTH:ATT_WIDTH + KV_WIDTH].reshape(kv_shape)
    v_p = att_p3[:, seq - WINDOW:, ATT_WIDTH + KV_WIDTH:].reshape(kv_shape)
    conv_p = xc_p.reshape(batch, seq, CONV_CH)[:, seq - (CONV_WIDTH - 1):][None]
    k_new = att_s[:, None, ATT_WIDTH:ATT_WIDTH + KV_WIDTH]
    v_new = att_s[:, None, ATT_WIDTH + KV_WIDTH:]
    kv_s_shape = (1, nseq, WINDOW, ATT_KV_HEADS, HEAD_DIM)
    k_s = jnp.concatenate([ck[:, 1:], k_new], axis=1).reshape(kv_s_shape)
    v_s = jnp.concatenate([cv[:, 1:], v_new], axis=1).reshape(kv_s_shape)
    conv_s = jnp.concatenate([state_conv[0][:, 1:], xc_s[:, None, :]], axis=1)[None]
    return (y_p.reshape(batch, seq, D_MODEL), y_s.reshape(nseq, 1, D_MODEL),
            k_p, v_p, conv_p, s_p[None], k_s, v_s, conv_s, s_s[None])
```

```python
import functools
import math

import numpy as np
import jax
import jax.numpy as jnp
from jax import lax
from jax.experimental import pallas as pl
from jax.experimental.pallas import tpu as pltpu

F32 = jnp.float32
BF16 = jnp.bfloat16

D_MODEL = 1024
ATT_HEADS = 8
ATT_KV_HEADS = 2
HEAD_DIM = 64
GQA = ATT_HEADS // ATT_KV_HEADS
WINDOW = 128
ATT_BLOCK = 128
N_BUCKETS = 32
DN_HEADS = 8
DN_DK = 64
DN_DV = 64
CONV_WIDTH = 4
DN_CHUNK = 64
ATT_WIDTH = ATT_HEADS * HEAD_DIM
KV_WIDTH = ATT_KV_HEADS * HEAD_DIM
DN_WIDTH = DN_HEADS * DN_DV
CONV_CH = 3 * DN_WIDTH
N_GROUPS = 4
EXPERTS_PER_GROUP = 8
N_EXPERTS = N_GROUPS * EXPERTS_PER_GROUP
D_EXPERT = 256
PLE_DIM = 256
EPS = 1e-6
NEG_INF = float("-inf")

ATT_COLS = ATT_WIDTH + 2 * KV_WIDTH
LANES = 128
IN_COLS = ATT_COLS + CONV_CH + DN_WIDTH + LANES
ROUTER_OFF = N_GROUPS
VMEM_LIMIT = 48 * 1024 * 1024


def _params(*sem):
    return pltpu.CompilerParams(dimension_semantics=sem, vmem_limit_bytes=VMEM_LIMIT)


def _mm(a, b):
    return jnp.dot(a.astype(BF16), b.astype(BF16), preferred_element_type=F32)


def _mm_nt(a, b):
    return lax.dot_general(a.astype(BF16), b.astype(BF16), (((1,), (1,)), ((), ())),
                           preferred_element_type=F32)


def _mm_tn(a, b):
    return lax.dot_general(a.astype(BF16), b.astype(BF16), (((0,), (0,)), ((), ())),
                           preferred_element_type=F32)


def _split3(x):
    h1 = x.astype(BF16)
    r1 = x - h1.astype(F32)
    h2 = r1.astype(BF16)
    h3 = (r1 - h2.astype(F32)).astype(BF16)
    return h1, h2, h3


def _mm_sel_rhs(x, sel):
    h1, h2, h3 = _split3(x)
    d = lambda h: jnp.dot(h, sel, preferred_element_type=F32)
    return d(h1) + d(h2) + d(h3)


def _mm_sel_lhs(sel, x):
    h1, h2, h3 = _split3(x)
    d = lambda h: jnp.dot(sel, h, preferred_element_type=F32)
    return d(h1) + d(h2) + d(h3)


def _mm3(a, b):
    ah = a.astype(BF16)
    al = (a - ah.astype(F32)).astype(BF16)
    bh = b.astype(BF16)
    bl = (b - bh.astype(F32)).astype(BF16)
    d = lambda u, v: jnp.dot(u, v, preferred_element_type=F32)
    return d(ah, bh) + d(ah, bl) + d(al, bh)


def _sigmoid(x):
    return 1.0 / (1.0 + jnp.exp(-x))


def _silu(x):
    return x * _sigmoid(x)


def _softplus(x):
    return jnp.maximum(x, 0.0) + jnp.log(1.0 + jnp.exp(-jnp.abs(x)))


def _rmsnorm(x, g):
    return x * lax.rsqrt(jnp.mean(x * x, axis=-1, keepdims=True) + EPS) * g


def _t5_bucket_np(dist):
    max_exact = N_BUCKETS // 2
    d = np.maximum(dist, 0)
    ratio = (np.log(np.maximum(d, 1).astype(np.float32) / np.float32(max_exact))
             / np.float32(math.log(WINDOW / max_exact))).astype(np.float32)
    large = np.minimum(max_exact + (ratio * np.float32(N_BUCKETS - max_exact)).astype(np.int32),
                       N_BUCKETS - 1)
    return np.where(d < max_exact, d, large).astype(np.int32)


def _bias_lookup(bucket, rb_ref, h):
    acc = jnp.zeros(bucket.shape, F32)
    for t in range(N_BUCKETS):
        acc = jnp.where(bucket == t, rb_ref[t, h], acc)
    return acc


def _inproj_kernel(x_ref, g_ref, w_ref, att_ref, xc_ref, dz_ref, ba_ref):
    xn = _rmsnorm(x_ref[...], g_ref[...]).astype(BF16)
    o0, o1, o2 = ATT_COLS, ATT_COLS + CONV_CH, ATT_COLS + CONV_CH + DN_WIDTH
    att_ref[...] = jnp.dot(xn, w_ref[:, :o0], preferred_element_type=F32)
    xc_ref[...] = jnp.dot(xn, w_ref[:, o0:o1], preferred_element_type=F32)
    dz_ref[...] = jnp.dot(xn, w_ref[:, o1:o2], preferred_element_type=F32)
    ba_ref[...] = jnp.dot(xn, w_ref[:, o2:], preferred_element_type=F32)


def _inproj(x, g, w):
    t = x.shape[0]
    tm = min(t, 256)
    row = lambda n: pl.BlockSpec((tm, n), lambda i: (i, 0))
    full = lambda a: pl.BlockSpec(a.shape, lambda i: (0,) * a.ndim)
    return pl.pallas_call(
        _inproj_kernel,
        grid=(t // tm,),
        in_specs=[row(D_MODEL), full(g), full(w)],
        out_specs=[row(ATT_COLS), row(CONV_CH), row(DN_WIDTH), row(LANES)],
        out_shape=[jax.ShapeDtypeStruct((t, n), F32) for n in (ATT_COLS, CONV_CH, DN_WIDTH, LANES)],
        compiler_params=_params("parallel"),
        name="inproj",
    )(x, g, w)


def _attn_prompt_kernel(cur_ref, prev_ref, bucket_ref, rb_ref, sink_ref, o_ref, bias_scr):
    first = jnp.logical_and(pl.program_id(0) == 0, pl.program_id(1) == 0)
    qi = lax.broadcasted_iota(jnp.int32, (ATT_BLOCK, 2 * ATT_BLOCK), 0)
    kj = lax.broadcasted_iota(jnp.int32, (ATT_BLOCK, 2 * ATT_BLOCK), 1)

    @pl.when(first)
    def _():
        dist = qi + ATT_BLOCK - kj
        band = jnp.logical_and(dist >= 0, dist < WINDOW)
        bucket = bucket_ref[...]
        for h in range(ATT_HEADS):
            bias_scr[h] = jnp.where(band, _bias_lookup(bucket, rb_ref, h), NEG_INF)

    cur = cur_ref[...]
    prev = prev_ref[...]
    q = cur[:, :ATT_WIDTH] * (HEAD_DIM ** -0.5)
    kcat = jnp.concatenate([prev[:, ATT_WIDTH:ATT_WIDTH + KV_WIDTH],
                            cur[:, ATT_WIDTH:ATT_WIDTH + KV_WIDTH]], axis=0)
    vcat = jnp.concatenate([prev[:, ATT_WIDTH + KV_WIDTH:], cur[:, ATT_WIDTH + KV_WIDTH:]], axis=0)
    keep = jnp.logical_or(pl.program_id(1) > 0, kj >= ATT_BLOCK)
    outs = []
    for h in range(ATT_HEADS):
        g = h // GQA
        qh = q[:, h * HEAD_DIM:(h + 1) * HEAD_DIM]
        kh = kcat[:, g * HEAD_DIM:(g + 1) * HEAD_DIM]
        vh = vcat[:, g * HEAD_DIM:(g + 1) * HEAD_DIM]
        s = jnp.where(keep, _mm_nt(qh, kh) + bias_scr[h], NEG_INF)
        sink = sink_ref[h]
        m = jnp.maximum(jnp.max(s, axis=-1, keepdims=True), sink)
        p = jnp.exp(s - m)
        den = jnp.sum(p, axis=-1, keepdims=True) + jnp.exp(sink - m)
        outs.append(_mm(p, vh) / den)
    o_ref[...] = jnp.concatenate(outs, axis=1)


def _attn_prompt(att, bucket, rel_bias, sink, batch, seq):
    nb = seq // ATT_BLOCK
    smem = pl.BlockSpec(memory_space=pltpu.SMEM)
    return pl.pallas_call(
        _attn_prompt_kernel,
        grid=(batch, nb),
        in_specs=[
            pl.BlockSpec((ATT_BLOCK, ATT_COLS), lambda b, i: (b * nb + i, 0)),
            pl.BlockSpec((ATT_BLOCK, ATT_COLS), lambda b, i: (b * nb + jnp.maximum(i - 1, 0), 0)),
            pl.BlockSpec(bucket.shape, lambda b, i: (0, 0)),
            smem, smem,
        ],
        out_specs=pl.BlockSpec((ATT_BLOCK, ATT_WIDTH), lambda b, i: (b * nb + i, 0)),
        out_shape=jax.ShapeDtypeStruct((batch * seq, ATT_WIDTH), F32),
        scratch_shapes=[pltpu.VMEM((ATT_HEADS, ATT_BLOCK, 2 * ATT_BLOCK), F32)],
        compiler_params=_params("arbitrary", "arbitrary"),
        name="attn_prompt",
    )(att, att, bucket, rel_bias, sink)


ATT_S_BB = 8


def _attn_sample_kernel(att_ref, ck_ref, cv_ref, bucket_ref, rb_ref, sink_ref, o_ref,
                        bias_scr, col_scr):
    hrow = lax.broadcasted_iota(jnp.int32, (ATT_HEADS, LANES), 0)
    lane = lax.broadcasted_iota(jnp.int32, (ATT_HEADS, LANES), 1)

    @pl.when(pl.program_id(0) == 0)
    def _():
        bucket = jnp.broadcast_to(bucket_ref[...], (ATT_HEADS, LANES))
        bias = jnp.zeros((ATT_HEADS, LANES), F32)
        cols = jnp.zeros((ATT_HEADS, LANES), F32)
        for h in range(ATT_HEADS):
            bias = jnp.where(hrow == h, _bias_lookup(bucket, rb_ref, h), bias)
            cols = jnp.where(jnp.logical_and(hrow == h, lane == 0), sink_ref[h], cols)
            cols = jnp.where(jnp.logical_and(hrow == h, lane == 1), rb_ref[0, h], cols)
        bias_scr[...] = jnp.where(lane >= 1, bias, NEG_INF)
        col_scr[...] = cols

    bias_c = bias_scr[...]
    sink = col_scr[:, 0:1]
    bias_n = col_scr[:, 1:2]
    same_group = (hrow // GQA) == (lane // HEAD_DIM)
    low_group = lax.broadcasted_iota(jnp.int32, (ATT_HEADS, HEAD_DIM), 0) < GQA
    for b in range(ATT_S_BB):
        row = att_ref[b:b + 1, :]
        q = row[:, :ATT_WIDTH] * (HEAD_DIM ** -0.5)
        kn = row[:, ATT_WIDTH:ATT_WIDTH + KV_WIDTH]
        vn = row[:, ATT_WIDTH + KV_WIDTH:]
        qh = jnp.concatenate([q[:, h * HEAD_DIM:(h + 1) * HEAD_DIM] for h in range(ATT_HEADS)], axis=0)
        q_bd = jnp.where(same_group, jnp.concatenate([qh, qh], axis=1), 0.0)
        rnd = lambda a: a.astype(BF16).astype(F32)
        s_c = _mm_nt(q_bd, ck_ref[b]) + bias_c
        s_n = jnp.sum(rnd(q_bd) * rnd(kn), axis=-1, keepdims=True) + bias_n
        m = jnp.maximum(jnp.maximum(jnp.max(s_c, axis=-1, keepdims=True), s_n), sink)
        p_c = jnp.exp(s_c - m)
        p_n = jnp.exp(s_n - m)
        den = jnp.sum(p_c, axis=-1, keepdims=True) + p_n + jnp.exp(sink - m)
        o_full = _mm(p_c / den, cv_ref[b]) + rnd(p_n / den) * rnd(vn)
        o_sel = jnp.where(low_group, o_full[:, :HEAD_DIM], o_full[:, HEAD_DIM:])
        o_ref[b:b + 1, :] = jnp.concatenate([o_sel[h:h + 1, :] for h in range(ATT_HEADS)], axis=1)


def _attn_sample(att, ck, cv, bucket, rel_bias, sink):
    nseq = att.shape[0]
    smem = pl.BlockSpec(memory_space=pltpu.SMEM)
    cache = pl.BlockSpec((ATT_S_BB, WINDOW, KV_WIDTH), lambda i: (i, 0, 0))
    return pl.pallas_call(
        _attn_sample_kernel,
        grid=(nseq // ATT_S_BB,),
        in_specs=[pl.BlockSpec((ATT_S_BB, ATT_COLS), lambda i: (i, 0)), cache, cache,
                  pl.BlockSpec(bucket.shape, lambda i: (0, 0)), smem, smem],
        out_specs=pl.BlockSpec((ATT_S_BB, ATT_WIDTH), lambda i: (i, 0)),
        out_shape=jax.ShapeDtypeStruct((nseq, ATT_WIDTH), F32),
        scratch_shapes=[pltpu.VMEM((ATT_HEADS, LANES), F32), pltpu.VMEM((ATT_HEADS, LANES), F32)],
        compiler_params=_params("arbitrary"),
        name="attn_sample",
    )(att, ck, cv, bucket, rel_bias, sink)


GDN_TB = 128
GDN_NC = GDN_TB // DN_CHUNK
TAIL = 8


def _gdn_gates(ba, alog, dtb):
    beta = _sigmoid(ba)
    g = -jnp.exp(alog) * _softplus(ba + dtb)
    return beta, g


PAIR = 2 * DN_DK
N_PAIRS = DN_WIDTH // PAIR


def _pair_diag(x, lo):
    xb = x.astype(BF16)
    zero = jnp.zeros_like(xb)
    return jnp.concatenate([jnp.where(lo, xb, zero), jnp.where(lo, zero, xb)], axis=0)


def _gdn_prompt_kernel(xc_ref, dz_ref, ba_ref, cw_ref, alog_ref, dtb_ref, dnx_ref,
                       hsum_ref, expb_ref, expg_ref, ltri_ref,
                       o_ref, s_out_ref, xp_scr, s_scr):
    i = pl.program_id(0)
    nb = xc_ref.shape[0]

    @pl.when(i == 0)
    def _():
        xp_scr[:, 0:TAIL, :] = jnp.zeros((nb, TAIL, CONV_CH), F32)
        s_scr[...] = jnp.zeros(s_scr.shape, F32)

    hsum = hsum_ref[...]
    ri = lax.broadcasted_iota(jnp.int32, (DN_CHUNK, PAIR), 0)
    ci = lax.broadcasted_iota(jnp.int32, (DN_CHUNK, PAIR), 1)
    lo = ci < DN_DK
    cj = jnp.where(lo, ci, ci - DN_DK)
    causal = ri >= cj
    strict = ri > cj
    eye = (ri == cj).astype(F32)

    def sel2(x, m):
        hi = x.astype(BF16)
        lw = (x - hi.astype(F32)).astype(BF16)
        return (jnp.dot(hi, m, preferred_element_type=F32) + jnp.dot(lw, m, preferred_element_type=F32))

    pre = []
    for b in range(nb):
        xc = xc_ref[b]
        xp_scr[b, TAIL:, :] = xc
        y = xp_scr[b, TAIL - 3:TAIL - 3 + GDN_TB, :] * cw_ref[0:1, :]
        y = y + xp_scr[b, TAIL - 2:TAIL - 2 + GDN_TB, :] * cw_ref[1:2, :]
        y = y + xp_scr[b, TAIL - 1:TAIL - 1 + GDN_TB, :] * cw_ref[2:3, :]
        y = y + xc * cw_ref[3:4, :]
        xp_scr[b, 0:TAIL, :] = xc[GDN_TB - TAIL:, :]
        y = _silu(y)
        q = y[:, :DN_WIDTH]
        k = y[:, DN_WIDTH:2 * DN_WIDTH]
        v = y[:, 2 * DN_WIDTH:]
        q = q * lax.rsqrt(sel2(q * q, hsum) + EPS) * (DN_DK ** -0.5)
        k = k * lax.rsqrt(sel2(k * k, hsum) + EPS)
        beta_c, g_c = _gdn_gates(ba_ref[b], alog_ref[...], dtb_ref[...])
        beta = sel2(beta_c, expb_ref[...])
        gam_c = _mm_sel_lhs(ltri_ref[...], g_c)
        gam = _mm_sel_rhs(gam_c, expg_ref[...])
        gam_t = gam_c.T
        kb = k * beta
        egam = jnp.exp(gam)
        pre.append(dict(q=q, k=k, kb=kb, vb=v * beta, qg=q * egam, wr=kb * egam, gam=gam, gam_t=gam_t))

    probs = [(b, p) for b in range(nb) for p in range(N_PAIRS)]
    pick = lambda m: jnp.where(lo, m[:DN_DK], m[DN_DK:])
    o_rows = [[] for _ in range(nb)]
    for c in range(GDN_NC):
        r0, r1 = c * DN_CHUNK, (c + 1) * DN_CHUNK
        sl = lambda name, b, p: pre[b][name][r0:r1, p * PAIR:(p + 1) * PAIR]
        raws = []
        for b, p in probs:
            k_p = sl("k", b, p)
            k_rows = jnp.concatenate([jnp.where(lo, k_p, 0.0), jnp.where(lo, 0.0, k_p)], axis=0)
            raws.append(_mm_nt(jnp.concatenate([sl("kb", b, p), sl("q", b, p)], axis=0), k_rows))
        pws, ts, qks = [], [], []
        for (b, p), raw in zip(probs, raws):
            gcol = sl("gam", b, p)
            h0 = DN_HEADS + 2 * p
            gam_t = pre[b]["gam_t"]
            grow = jnp.concatenate([gam_t[h0:h0 + 1, r0:r1], gam_t[h0 + 1:h0 + 2, r0:r1]], axis=1)
            decay = jnp.exp(jnp.where(causal, gcol - grow, NEG_INF))
            a = jnp.where(strict, raw[:DN_CHUNK] * decay, 0.0)
            qks.append(jnp.where(causal, raw[DN_CHUNK:] * decay, 0.0))
            pws.append(-a)
            ts.append(eye - a)
        pws = [_mm(pw, _pair_diag(pw, lo)) for pw in pws]
        for _ in range(4):
            rs = [_mm(jnp.concatenate([pw, t], axis=0), _pair_diag(pw, lo)) for pw, t in zip(pws, ts)]
            pws = [r[:DN_CHUNK] for r in rs]
            ts = [t + r[DN_CHUNK:] for t, r in zip(ts, rs)]
        rs = [_mm(t, _pair_diag(pw, lo)) for pw, t in zip(pws, ts)]
        ts = [t + r for t, r in zip(ts, rs)]
        sols = [_mm(t, jnp.concatenate([_pair_diag(sl("vb", b, p), lo), _pair_diag(sl("wr", b, p), lo)],
                                       axis=1)) for (b, p), t in zip(probs, ts)]
        qkuws = [_mm(qk, jnp.concatenate([_pair_diag(s[:, :PAIR], lo), _pair_diag(s[:, PAIR:], lo)], axis=1))
                 for qk, s in zip(qks, sols)]
        crosses, gls = [], []
        for (b, p), s in zip(probs, sols):
            gam_last = pre[b]["gam"][r1 - 1:r1, p * PAIR:(p + 1) * PAIR]
            kd = sl("k", b, p) * jnp.exp(gam_last - sl("gam", b, p))
            crosses.append(_mm_tn(kd, s))
            gls.append(jnp.exp(gam_last))
        lhs = [jnp.concatenate([pick(cr[:, PAIR:]), sl("qg", b, p) - qkuw[:, PAIR:]], axis=0)
               for (b, p), cr, qkuw in zip(probs, crosses, qkuws)]
        s_olds = [s_scr[b, p] for b, p in probs]
        rs = [_mm(l, _pair_diag(s_old, lo)) for l, s_old in zip(lhs, s_olds)]
        o_pairs = [[] for _ in range(nb)]
        for (b, p), r, s_old, gl, cr, qkuw in zip(probs, rs, s_olds, gls, crosses, qkuws):
            s_scr[b, p] = gl * s_old - r[:DN_DK] + pick(cr[:, :PAIR])
            o_pairs[b].append(r[DN_DK:] + qkuw[:, :PAIR])
        for b in range(nb):
            o_rows[b].append(jnp.concatenate(o_pairs[b], axis=1))

    for b in range(nb):
        o = jnp.concatenate(o_rows[b], axis=0)
        ms = sel2(o * o, hsum) * (1.0 / DN_DV)
        o = o * lax.rsqrt(ms + EPS) * dnx_ref[...]
        o_ref[b] = o * _silu(dz_ref[b])

    @pl.when(i == pl.num_programs(0) - 1)
    def _():
        for b in range(nb):
            for p in range(N_PAIRS):
                s_p = s_scr[b, p]
                s_out_ref[b, 2 * p] = s_p[:, :DN_DV]
                s_out_ref[b, 2 * p + 1] = s_p[:, DN_DV:]


def _gdn_consts():
    lane = np.arange(DN_WIDTH)
    hsum = (lane[:, None] // DN_DV == lane[None, :] // DN_DV)
    src = np.arange(LANES)
    expb = (src[:, None] == lane[None, :] // DN_DV)
    expg = (src[:, None] == DN_HEADS + lane[None, :] // DN_DV)
    tok = np.arange(GDN_TB)
    ltri = np.logical_and(tok[:, None] >= tok[None, :],
                          tok[:, None] // DN_CHUNK == tok[None, :] // DN_CHUNK)
    as_bf16 = lambda m: jnp.asarray(m.astype(np.float32), dtype=BF16)
    return as_bf16(hsum), as_bf16(expb), as_bf16(expg), as_bf16(ltri)


def _gdn_prompt(xc, dz, ba, conv_w, alog, dtb, dnx, batch, seq):
    nt = seq // GDN_TB
    hsum, expb, expg, ltri = _gdn_consts()
    row = lambda n: pl.BlockSpec((batch, GDN_TB, n), lambda i: (0, i, 0))
    full = lambda a: pl.BlockSpec(a.shape, lambda i: (0,) * a.ndim)
    consts = (conv_w, alog, dtb, dnx, hsum, expb, expg, ltri)
    as3d = lambda a: a.reshape(batch, seq, a.shape[-1])
    o, s = pl.pallas_call(
        _gdn_prompt_kernel,
        grid=(nt,),
        in_specs=[row(CONV_CH), row(DN_WIDTH), row(LANES)] + [full(a) for a in consts],
        out_specs=[row(DN_WIDTH),
                   pl.BlockSpec((batch, DN_HEADS, DN_DK, DN_DV), lambda i: (0, 0, 0, 0))],
        out_shape=[jax.ShapeDtypeStruct((batch, seq, DN_WIDTH), F32),
                   jax.ShapeDtypeStruct((batch, DN_HEADS, DN_DK, DN_DV), F32)],
        scratch_shapes=[pltpu.VMEM((batch, TAIL + GDN_TB, CONV_CH), F32),
                        pltpu.VMEM((batch, N_PAIRS, DN_DK, PAIR), F32)],
        compiler_params=_params("arbitrary"),
        name="gdn_prompt",
    )(as3d(xc), as3d(dz), as3d(ba), *consts)
    return o.reshape(batch * seq, DN_WIDTH), s


GDN_S_BB = 8


def _gdn_sample_kernel(xc_ref, dz_ref, ba_ref, sc_ref, s_ref, cw_ref, alog_ref, dtb_ref, dn_ref,
                       hsum_ref, eye_ref, hsel_ref, hrep_ref, o_ref, s_out_ref):
    xc = xc_ref[...]
    y = sc_ref[0] * cw_ref[0:1, :]
    y = y + sc_ref[1] * cw_ref[1:2, :]
    y = y + sc_ref[2] * cw_ref[2:3, :]
    y = _silu(y + xc * cw_ref[3:4, :])
    hsum = hsum_ref[...]
    q = y[:, :DN_WIDTH]
    k = y[:, DN_WIDTH:2 * DN_WIDTH]
    v = y[:, 2 * DN_WIDTH:]
    q = q * lax.rsqrt(_mm_sel_rhs(q * q, hsum) + EPS) * (DN_DK ** -0.5)
    k = k * lax.rsqrt(_mm_sel_rhs(k * k, hsum) + EPS)
    beta_c, g_c = _gdn_gates(ba_ref[...], alog_ref[...], dtb_ref[...])
    eg_c = jnp.exp(g_c)
    eye = eye_ref[...]
    tr = lambda a: lax.dot_general(a, eye, (((0,), (0,)), ((), ())), precision=lax.Precision.HIGHEST,
                                   preferred_element_type=F32)
    k_t = tr(k)
    q_t = tr(q)
    beta_t = tr(beta_c)
    eg_t = tr(eg_c)
    dz = dz_ref[...]
    dn = dn_ref[...]
    split = lambda r: jnp.concatenate([r[:, h * DN_DV:(h + 1) * DN_DV] for h in range(DN_HEADS)], axis=0)
    hsel = hsel_ref[...]
    hrep = hrep_ref[...]
    for b in range(GDN_S_BB):
        s2 = s_ref[b]
        kc = k_t[:, b:b + 1]
        qc = q_t[:, b:b + 1]
        beta = beta_t[0:DN_HEADS, b:b + 1]
        eg = eg_t[DN_HEADS:2 * DN_HEADS, b:b + 1]
        vh = split(v[b:b + 1, :])
        qh = split(q[b:b + 1, :])
        kh = split(k[b:b + 1, :])
        ks = _mm_sel_lhs(hsel, kc * s2)
        v_new = beta * (vh - eg * ks)
        qs = _mm_sel_lhs(hsel, qc * s2)
        qk = jnp.sum(qh * kh, axis=-1, keepdims=True)
        o = eg * qs + qk * v_new
        rep = _mm_sel_lhs(hrep, jnp.concatenate(
            [v_new, jnp.broadcast_to(eg, (DN_HEADS, DN_DV))], axis=1))
        s_out_ref[b] = s2 * rep[:, DN_DV:] + kc * rep[:, :DN_DV]
        o = _rmsnorm(o, dn) * _silu(split(dz[b:b + 1, :]))
        o_ref[b] = o


def _gdn_sample(xc, dz, ba, sconv_t, state, conv_w, alog, dtb, dn):
    nseq = xc.shape[0]
    lane = np.arange(DN_WIDTH)
    hsum = jnp.asarray((lane[:, None] // DN_DV == lane[None, :] // DN_DV).astype(np.float32), dtype=BF16)
    eye = jnp.eye(GDN_S_BB, dtype=F32)
    hsel_np = (np.arange(DN_HEADS)[:, None] == lane[None, :] // DN_DK).astype(np.float32)
    hsel = jnp.asarray(hsel_np, dtype=BF16)
    hrep = jnp.asarray(hsel_np.T, dtype=BF16)
    row = lambda n: pl.BlockSpec((GDN_S_BB, n), lambda i: (i, 0))
    full = lambda a: pl.BlockSpec(a.shape, lambda i: (0,) * a.ndim)
    st = pl.BlockSpec((GDN_S_BB, DN_HEADS * DN_DK, DN_DV), lambda i: (i, 0, 0))
    consts = (conv_w, alog, dtb, dn, hsum, eye, hsel, hrep)
    return pl.pallas_call(
        _gdn_sample_kernel,
        grid=(nseq // GDN_S_BB,),
        in_specs=[row(CONV_CH), row(DN_WIDTH), row(LANES),
                  pl.BlockSpec((CONV_WIDTH - 1, GDN_S_BB, CONV_CH), lambda i: (0, i, 0)), st]
                 + [full(a) for a in consts],
        out_specs=[pl.BlockSpec((GDN_S_BB, DN_HEADS, DN_DV), lambda i: (i, 0, 0)), st],
        out_shape=[jax.ShapeDtypeStruct((nseq, DN_HEADS, DN_DV), F32),
                   jax.ShapeDtypeStruct(state.shape, F32)],
        compiler_params=_params("parallel"),
        name="gdn_sample",
    )(xc, dz, ba, sconv_t, state, *consts)


def _outproj_router_kernel(x_ref, oa_ref, od_ref, wo_ref, g_ref, wr_ref, h_ref, xn_ref, gate_ref):
    h = x_ref[...] + _mm(oa_ref[...], wo_ref[:ATT_WIDTH, :]) + _mm(od_ref[...], wo_ref[ATT_WIDTH:, :])
    h_ref[...] = h
    xn = _rmsnorm(h, g_ref[...]).astype(BF16)
    xn_ref[...] = xn
    logits = jnp.dot(xn, wr_ref[...], preferred_element_type=F32)
    lane = lax.broadcasted_iota(jnp.int32, logits.shape, 1).astype(F32)
    first_at = lambda hit: jnp.min(jnp.where(hit, lane, float(LANES)), axis=-1, keepdims=True)
    glog = jnp.where(lane < N_GROUPS, logits, NEG_INF)
    gmax = jnp.max(glog, axis=-1, keepdims=True)
    gsel = first_at(glog == gmax)
    pgsel = 1.0 / jnp.sum(jnp.exp(glog - gmax), axis=-1, keepdims=True)
    lo = ROUTER_OFF + gsel * EXPERTS_PER_GROUP
    in_group = jnp.logical_and(lane >= lo, lane < lo + EXPERTS_PER_GROUP)
    elog = jnp.where(in_group, logits, NEG_INF)
    m1 = jnp.max(elog, axis=-1, keepdims=True)
    i1 = first_at(elog == m1)
    z = jnp.sum(jnp.exp(elog - m1), axis=-1, keepdims=True)
    elog2 = jnp.where(lane == i1, NEG_INF, elog)
    m2 = jnp.max(elog2, axis=-1, keepdims=True)
    i2 = first_at(elog2 == m2)
    p1 = 1.0 / z
    p2 = jnp.exp(m2 - m1) / z
    tot = p1 + p2
    gate_ref[...] = (jnp.where(lane == i1, p1 / tot * pgsel, 0.0)
                     + jnp.where(lane == i2, p2 / tot * pgsel, 0.0))


def _outproj_router(x, oa, od, wo, g, wr):
    t = x.shape[0]
    tm = min(t, 256)
    row = lambda n: pl.BlockSpec((tm, n), lambda i: (i, 0))
    full = lambda a: pl.BlockSpec(a.shape, lambda i: (0,) * a.ndim)
    return pl.pallas_call(
        _outproj_router_kernel,
        grid=(t // tm,),
        in_specs=[row(D_MODEL), row(ATT_WIDTH), row(DN_WIDTH), full(wo), full(g), full(wr)],
        out_specs=[row(D_MODEL), row(D_MODEL), row(LANES)],
        out_shape=[jax.ShapeDtypeStruct((t, D_MODEL), F32), jax.ShapeDtypeStruct((t, D_MODEL), BF16),
                   jax.ShapeDtypeStruct((t, LANES), F32)],
        compiler_params=_params("parallel"),
        name="outproj_router",
    )(x, oa, od, wo, g, wr)


def _moe_kernel(xn_ref, gate_ref, wg_ref, wu_ref, wd_ref, o_ref):
    e = pl.program_id(1)
    xn = xn_ref[...]
    lane = lax.broadcasted_iota(jnp.int32, gate_ref.shape, 1)
    gate = jnp.sum(jnp.where(lane == e + ROUTER_OFF, gate_ref[...], 0.0), axis=-1, keepdims=True)
    hg = jnp.dot(xn, wg_ref[...], preferred_element_type=F32)
    hu = jnp.dot(xn, wu_ref[...], preferred_element_type=F32)
    hm = _silu(hg) * hu * gate
    y = jnp.dot(hm.astype(BF16), wd_ref[...], preferred_element_type=F32)

    @pl.when(e == 0)
    def _():
        o_ref[...] = y

    @pl.when(e > 0)
    def _():
        o_ref[...] += y


def _moe(xn, gates, wg, wu, wd):
    t = xn.shape[0]
    tm = min(t, 1024)
    return pl.pallas_call(
        _moe_kernel,
        grid=(t // tm, N_EXPERTS),
        in_specs=[pl.BlockSpec((tm, D_MODEL), lambda i, e: (i, 0)),
                  pl.BlockSpec((tm, LANES), lambda i, e: (i, 0)),
                  pl.BlockSpec((None, D_MODEL, D_EXPERT), lambda i, e: (e, 0, 0)),
                  pl.BlockSpec((None, D_MODEL, D_EXPERT), lambda i, e: (e, 0, 0)),
                  pl.BlockSpec((None, D_EXPERT, D_MODEL), lambda i, e: (e, 0, 0))],
        out_specs=pl.BlockSpec((tm, D_MODEL), lambda i, e: (i, 0)),
        out_shape=jax.ShapeDtypeStruct((t, D_MODEL), F32),
        compiler_params=_params("parallel", "arbitrary"),
        name="moe",
    )(xn, gates, wg, wu, wd)


def _ple_final_kernel(h_ref, m_ref, p_ref, wpp_ref, wpg_ref, gp_ref, gf_ref, y_ref):
    h = h_ref[...] + m_ref[...]
    hn = _rmsnorm(h, gp_ref[...])
    h = h + _mm(p_ref[...], wpp_ref[...]) * _sigmoid(_mm(hn, wpg_ref[...]))
    y_ref[...] = _rmsnorm(h, gf_ref[...])


def _ple_final(h, m, p, wpp, wpg, gp, gf):
    t = h.shape[0]
    tm = min(t, 256)
    row = lambda n: pl.BlockSpec((tm, n), lambda i: (i, 0))
    full = lambda a: pl.BlockSpec(a.shape, lambda i: (0,) * a.ndim)
    return pl.pallas_call(
        _ple_final_kernel,
        grid=(t // tm,),
        in_specs=[row(D_MODEL), row(D_MODEL), row(PLE_DIM), full(wpp), full(wpg), full(gp), full(gf)],
        out_specs=row(D_MODEL),
        out_shape=jax.ShapeDtypeStruct((t, D_MODEL), F32),
        compiler_params=_params("parallel"),
        name="ple_final",
    )(h, m, p, wpp, wpg, gp, gf)


def kernel(x_prompt, x_sample, p_prompt, p_sample, cache_k, cache_v, state_conv, state_S, rel_bias, norm_mix, w_in, att_sink, conv_w, dn_A_log, dn_dt_bias, dn_norm, w_out, norm_ffn, w_router_group, w_router_expert, w_gate, w_up, w_down, w_ple_proj, w_ple_gate, norm_ple, norm_final):
    batch, seq, _ = x_prompt.shape
    nseq = x_sample.shape[0]
    assert x_sample.shape[1] == 1 and norm_mix.shape[0] == 1 and cache_k.shape[2] == WINDOW
    assert seq % GDN_TB == 0 and seq % ATT_BLOCK == 0

    wi = w_in[0]
    o_db = ATT_COLS + CONV_CH
    w_in_re = jnp.concatenate(
        [wi[:, :o_db], wi[:, o_db + 2 * DN_HEADS:], wi[:, o_db:o_db + 2 * DN_HEADS],
         jnp.zeros((D_MODEL, LANES - 2 * DN_HEADS), F32)], axis=1).astype(BF16)
    row = lambda a: a.reshape(1, -1).astype(F32)
    pad_lanes = lambda a, off: jnp.zeros((1, LANES), F32).at[0, off:off + a.shape[0]].set(a)
    alog = pad_lanes(dn_A_log[0], DN_HEADS)
    dtb = pad_lanes(dn_dt_bias[0], DN_HEADS)
    dnx = jnp.tile(dn_norm[0], DN_HEADS).reshape(1, DN_WIDTH)
    w_router = jnp.concatenate(
        [w_router_group[0], w_router_expert[0],
         jnp.zeros((D_MODEL, LANES - N_GROUPS - N_EXPERTS), F32)], axis=1).astype(BF16)
    wo = w_out[0].astype(BF16)
    wg, wu, wd = w_gate[0].astype(BF16), w_up[0].astype(BF16), w_down[0].astype(BF16)
    wpp, wpg = w_ple_proj[0].astype(BF16), w_ple_gate[0].astype(BF16)
    sink = att_sink[0]

    qi = np.arange(ATT_BLOCK)[:, None]
    kj = np.arange(2 * ATT_BLOCK)[None, :]
    bucket_p = jnp.asarray(_t5_bucket_np(qi + ATT_BLOCK - kj))
    bucket_s = jnp.asarray(_t5_bucket_np(WINDOW - np.arange(WINDOW)[None, :]))

    def tail(x, o_att, o_dn, p):
        h1, xn2, gates = _outproj_router(x, o_att, o_dn, wo, row(norm_ffn[0]), w_router)
        moe = _moe(xn2, gates, wg, wu, wd)
        return _ple_final(h1, moe, p, wpp, wpg, row(norm_ple[0]), row(norm_final))

    xp = x_prompt.reshape(batch * seq, D_MODEL)
    att_p, xc_p, dz_p, ba_p = _inproj(xp, row(norm_mix[0]), w_in_re)
    o_att_p = _attn_prompt(att_p, bucket_p, rel_bias, sink, batch, seq)
    o_dn_p, s_p = _gdn_prompt(xc_p, dz_p, ba_p, conv_w[0], alog, dtb, dnx, batch, seq)
    y_p = tail(xp, o_att_p, o_dn_p, p_prompt[0].reshape(batch * seq, PLE_DIM))

    xs = x_sample.reshape(nseq, D_MODEL)
    att_s, xc_s, dz_s, ba_s = _inproj(xs, row(norm_mix[0]), w_in_re)
    ck = cache_k[0].reshape(nseq, WINDOW, KV_WIDTH)
    cv = cache_v[0].reshape(nseq, WINDOW, KV_WIDTH)
    o_att_s = _attn_sample(att_s, ck, cv, bucket_s, rel_bias, sink)
    sconv_t = jnp.swapaxes(state_conv[0], 0, 1)
    o_dn_s, s_s = _gdn_sample(xc_s, dz_s, ba_s, sconv_t,
                              state_S[0].reshape(nseq, DN_HEADS * DN_DK, DN_DV), conv_w[0], alog, dtb,
                              dn_norm[0].reshape(1, DN_DV))
    s_s = s_s.reshape(nseq, DN_HEADS, DN_DK, DN_DV)
    y_s = tail(xs, o_att_s, o_dn_s.reshape(nseq, DN_WIDTH), p_sample[0].reshape(nseq, PLE_DIM))

    att_p3 = att_p.reshape(batch, seq, ATT_COLS)
    kv_shape = (1, batch, WINDOW, ATT_KV_HEADS, HEAD_DIM)
    k_p = att_p3[:, seq - WINDOW:, ATT_WIDTH:ATT_WIDTH + KV_WIDTH].reshape(kv_shape)
    v_p = att_p3[:, seq - WINDOW:, ATT_WIDTH + KV_WIDTH:].reshape(kv_shape)
    conv_p = xc_p.reshape(batch, seq, CONV_CH)[:, seq - (CONV_WIDTH - 1):][None]
    k_new = att_s[:, None, ATT_WIDTH:ATT_WIDTH + KV_WIDTH]
    v_new = att_s[:, None, ATT_WIDTH + KV_WIDTH:]
    kv_s_shape = (1, nseq, WINDOW, ATT_KV_HEADS, HEAD_DIM)
    k_s = jnp.concatenate([ck[:, 1:], k_new], axis=1).reshape(kv_s_shape)
    v_s = jnp.concatenate([cv[:, 1:], v_new], axis=1).reshape(kv_s_shape)
    conv_s = jnp.concatenate([state_conv[0][:, 1:], xc_s[:, None, :]], axis=1)[None]
    return (y_p.reshape(batch, seq, D_MODEL), y_s.reshape(nseq, 1, D_MODEL),
            k_p, v_p, conv_p, s_p[None], k_s, v_s, conv_s, s_s[None])
```

```python
import functools
import math

import numpy as np
import jax
import jax.numpy as jnp
from jax import lax
from jax.experimental import pallas as pl
from jax.experimental.pallas import tpu as pltpu

F32 = jnp.float32
BF16 = jnp.bfloat16

D_MODEL = 1024
ATT_HEADS = 8
ATT_KV_HEADS = 2
HEAD_DIM = 64
GQA = ATT_HEADS // ATT_KV_HEADS
WINDOW = 128
ATT_BLOCK = 128
N_BUCKETS = 32
DN_HEADS = 8
DN_DK = 64
DN_DV = 64
CONV_WIDTH = 4
DN_CHUNK = 64
ATT_WIDTH = ATT_HEADS * HEAD_DIM
KV_WIDTH = ATT_KV_HEADS * HEAD_DIM
DN_WIDTH = DN_HEADS * DN_DV
CONV_CH = 3 * DN_WIDTH
N_GROUPS = 4
EXPERTS_PER_GROUP = 8
N_EXPERTS = N_GROUPS * EXPERTS_PER_GROUP
D_EXPERT = 256
PLE_DIM = 256
EPS = 1e-6
NEG_INF = float("-inf")

ATT_COLS = ATT_WIDTH + 2 * KV_WIDTH
LANES = 128
IN_COLS = ATT_COLS + CONV_CH + DN_WIDTH + LANES
ROUTER_OFF = N_GROUPS
VMEM_LIMIT = 48 * 1024 * 1024


def _params(*sem):
    return pltpu.CompilerParams(dimension_semantics=sem, vmem_limit_bytes=VMEM_LIMIT)


def _mm(a, b):
    return jnp.dot(a.astype(BF16), b.astype(BF16), preferred_element_type=F32)


def _mm_nt(a, b):
    return lax.dot_general(a.astype(BF16), b.astype(BF16), (((1,), (1,)), ((), ())),
                           preferred_element_type=F32)


def _mm_tn(a, b):
    return lax.dot_general(a.astype(BF16), b.astype(BF16), (((0,), (0,)), ((), ())),
                           preferred_element_type=F32)


def _split3(x):
    h1 = x.astype(BF16)
    r1 = x - h1.astype(F32)
    h2 = r1.astype(BF16)
    h3 = (r1 - h2.astype(F32)).astype(BF16)
    return h1, h2, h3


def _mm_sel_rhs(x, sel):
    h1, h2, h3 = _split3(x)
    d = lambda h: jnp.dot(h, sel, preferred_element_type=F32)
    return d(h1) + d(h2) + d(h3)


def _mm_sel_lhs(sel, x):
    h1, h2, h3 = _split3(x)
    d = lambda h: jnp.dot(sel, h, preferred_element_type=F32)
    return d(h1) + d(h2) + d(h3)


def _mm3(a, b):
    ah = a.astype(BF16)
    al = (a - ah.astype(F32)).astype(BF16)
    bh = b.astype(BF16)
    bl = (b - bh.astype(F32)).astype(BF16)
    d = lambda u, v: jnp.dot(u, v, preferred_element_type=F32)
    return d(ah, bh) + d(ah, bl) + d(al, bh)


def _sigmoid(x):
    return 1.0 / (1.0 + jnp.exp(-x))


def _silu(x):
    return x * _sigmoid(x)


def _softplus(x):
    return jnp.maximum(x, 0.0) + jnp.log1p(jnp.exp(-jnp.abs(x)))


def _rmsnorm(x, g):
    return x * lax.rsqrt(jnp.mean(x * x, axis=-1, keepdims=True) + EPS) * g


def _t5_bucket_np(dist):
    max_exact = N_BUCKETS // 2
    d = np.maximum(dist, 0)
    ratio = (np.log(np.maximum(d, 1).astype(np.float32) / np.float32(max_exact))
             / np.float32(math.log(WINDOW / max_exact))).astype(np.float32)
    large = np.minimum(max_exact + (ratio * np.float32(N_BUCKETS - max_exact)).astype(np.int32),
                       N_BUCKETS - 1)
    return np.where(d < max_exact, d, large).astype(np.int32)


def _bias_lookup(bucket, rb_ref, h):
    acc = jnp.zeros(bucket.shape, F32)
    for t in range(N_BUCKETS):
        acc = jnp.where(bucket == t, rb_ref[t, h], acc)
    return acc


def _inproj_kernel(x_ref, g_ref, w_ref, att_ref, xc_ref, dz_ref, ba_ref):
    xn = _rmsnorm(x_ref[...], g_ref[...]).astype(BF16)
    o0, o1, o2 = ATT_COLS, ATT_COLS + CONV_CH, ATT_COLS + CONV_CH + DN_WIDTH
    att_ref[...] = jnp.dot(xn, w_ref[:, :o0], preferred_element_type=F32)
    xc_ref[...] = jnp.dot(xn, w_ref[:, o0:o1], preferred_element_type=F32)
    dz_ref[...] = jnp.dot(xn, w_ref[:, o1:o2], preferred_element_type=F32)
    ba_ref[...] = jnp.dot(xn, w_ref[:, o2:], preferred_element_type=F32)


def _inproj(x, g, w):
    t = x.shape[0]
    tm = min(t, 256)
    row = lambda n: pl.BlockSpec((tm, n), lambda i: (i, 0))
    full = lambda a: pl.BlockSpec(a.shape, lambda i: (0,) * a.ndim)
    return pl.pallas_call(
        _inproj_kernel,
        grid=(t // tm,),
        in_specs=[row(D_MODEL), full(g), full(w)],
        out_specs=[row(ATT_COLS), row(CONV_CH), row(DN_WIDTH), row(LANES)],
        out_shape=[jax.ShapeDtypeStruct((t, n), F32) for n in (ATT_COLS, CONV_CH, DN_WIDTH, LANES)],
        compiler_params=_params("parallel"),
        name="inproj",
    )(x, g, w)


def _attn_prompt_kernel(cur_ref, prev_ref, bucket_ref, rb_ref, sink_ref, o_ref, bias_scr):
    first = jnp.logical_and(pl.program_id(0) == 0, pl.program_id(1) == 0)
    qi = lax.broadcasted_iota(jnp.int32, (ATT_BLOCK, 2 * ATT_BLOCK), 0)
    kj = lax.broadcasted_iota(jnp.int32, (ATT_BLOCK, 2 * ATT_BLOCK), 1)

    @pl.when(first)
    def _():
        dist = qi + ATT_BLOCK - kj
        band = jnp.logical_and(dist >= 0, dist < WINDOW)
        bucket = bucket_ref[...]
        for h in range(ATT_HEADS):
            bias_scr[h] = jnp.where(band, _bias_lookup(bucket, rb_ref, h), NEG_INF)

    cur = cur_ref[...]
    prev = prev_ref[...]
    q = cur[:, :ATT_WIDTH] * (HEAD_DIM ** -0.5)
    kcat = jnp.concatenate([prev[:, ATT_WIDTH:ATT_WIDTH + KV_WIDTH],
                            cur[:, ATT_WIDTH:ATT_WIDTH + KV_WIDTH]], axis=0)
    vcat = jnp.concatenate([prev[:, ATT_WIDTH + KV_WIDTH:], cur[:, ATT_WIDTH + KV_WIDTH:]], axis=0)
    keep = jnp.logical_or(pl.program_id(1) > 0, kj >= ATT_BLOCK)
    outs = []
    for h in range(ATT_HEADS):
        g = h // GQA
        qh = q[:, h * HEAD_DIM:(h + 1) * HEAD_DIM]
        kh = kcat[:, g * HEAD_DIM:(g + 1) * HEAD_DIM]
        vh = vcat[:, g * HEAD_DIM:(g + 1) * HEAD_DIM]
        s = jnp.where(keep, _mm_nt(qh, kh) + bias_scr[h], NEG_INF)
        sink = sink_ref[h]
        m = jnp.maximum(jnp.max(s, axis=-1, keepdims=True), sink)
        p = jnp.exp(s - m)
        den = jnp.sum(p, axis=-1, keepdims=True) + jnp.exp(sink - m)
        outs.append(_mm(p, vh) / den)
    o_ref[...] = jnp.concatenate(outs, axis=1)


def _attn_prompt(att, bucket, rel_bias, sink, batch, seq):
    nb = seq // ATT_BLOCK
    smem = pl.BlockSpec(memory_space=pltpu.SMEM)
    return pl.pallas_call(
        _attn_prompt_kernel,
        grid=(batch, nb),
        in_specs=[
            pl.BlockSpec((ATT_BLOCK, ATT_COLS), lambda b, i: (b * nb + i, 0)),
            pl.BlockSpec((ATT_BLOCK, ATT_COLS), lambda b, i: (b * nb + jnp.maximum(i - 1, 0), 0)),
            pl.BlockSpec(bucket.shape, lambda b, i: (0, 0)),
            smem, smem,
        ],
        out_specs=pl.BlockSpec((ATT_BLOCK, ATT_WIDTH), lambda b, i: (b * nb + i, 0)),
        out_shape=jax.ShapeDtypeStruct((batch * seq, ATT_WIDTH), F32),
        scratch_shapes=[pltpu.VMEM((ATT_HEADS, ATT_BLOCK, 2 * ATT_BLOCK), F32)],
        compiler_params=_params("arbitrary", "arbitrary"),
        name="attn_prompt",
    )(att, att, bucket, rel_bias, sink)


ATT_S_BB = 8


def _attn_sample_kernel(att_ref, ck_ref, cv_ref, bucket_ref, rb_ref, sink_ref, o_ref,
                        bias_scr, col_scr):
    hrow = lax.broadcasted_iota(jnp.int32, (ATT_HEADS, LANES), 0)
    lane = lax.broadcasted_iota(jnp.int32, (ATT_HEADS, LANES), 1)

    @pl.when(pl.program_id(0) == 0)
    def _():
        bucket = jnp.broadcast_to(bucket_ref[...], (ATT_HEADS, LANES))
        bias = jnp.zeros((ATT_HEADS, LANES), F32)
        cols = jnp.zeros((ATT_HEADS, LANES), F32)
        for h in range(ATT_HEADS):
            bias = jnp.where(hrow == h, _bias_lookup(bucket, rb_ref, h), bias)
            cols = jnp.where(jnp.logical_and(hrow == h, lane == 0), sink_ref[h], cols)
            cols = jnp.where(jnp.logical_and(hrow == h, lane == 1), rb_ref[0, h], cols)
        bias_scr[...] = jnp.where(lane >= 1, bias, NEG_INF)
        col_scr[...] = cols

    bias_c = bias_scr[...]
    sink = col_scr[:, 0:1]
    bias_n = col_scr[:, 1:2]
    same_group = (hrow // GQA) == (lane // HEAD_DIM)
    low_group = lax.broadcasted_iota(jnp.int32, (ATT_HEADS, HEAD_DIM), 0) < GQA
    for b in range(ATT_S_BB):
        row = att_ref[b:b + 1, :]
        q = row[:, :ATT_WIDTH] * (HEAD_DIM ** -0.5)
        kn = row[:, ATT_WIDTH:ATT_WIDTH + KV_WIDTH]
        vn = row[:, ATT_WIDTH + KV_WIDTH:]
        qh = jnp.concatenate([q[:, h * HEAD_DIM:(h + 1) * HEAD_DIM] for h in range(ATT_HEADS)], axis=0)
        q_bd = jnp.where(same_group, jnp.concatenate([qh, qh], axis=1), 0.0)
        rnd = lambda a: a.astype(BF16).astype(F32)
        s_c = _mm_nt(q_bd, ck_ref[b]) + bias_c
        s_n = jnp.sum(rnd(q_bd) * rnd(kn), axis=-1, keepdims=True) + bias_n
        m = jnp.maximum(jnp.maximum(jnp.max(s_c, axis=-1, keepdims=True), s_n), sink)
        p_c = jnp.exp(s_c - m)
        p_n = jnp.exp(s_n - m)
        den = jnp.sum(p_c, axis=-1, keepdims=True) + p_n + jnp.exp(sink - m)
        o_full = _mm(p_c / den, cv_ref[b]) + rnd(p_n / den) * rnd(vn)
        o_sel = jnp.where(low_group, o_full[:, :HEAD_DIM], o_full[:, HEAD_DIM:])
        o_ref[b:b + 1, :] = jnp.concatenate([o_sel[h:h + 1, :] for h in range(ATT_HEADS)], axis=1)


def _attn_sample(att, ck, cv, bucket, rel_bias, sink):
    nseq = att.shape[0]
    smem = pl.BlockSpec(memory_space=pltpu.SMEM)
    cache = pl.BlockSpec((ATT_S_BB, WINDOW, KV_WIDTH), lambda i: (i, 0, 0))
    return pl.pallas_call(
        _attn_sample_kernel,
        grid=(nseq // ATT_S_BB,),
        in_specs=[pl.BlockSpec((ATT_S_BB, ATT_COLS), lambda i: (i, 0)), cache, cache,
                  pl.BlockSpec(bucket.shape, lambda i: (0, 0)), smem, smem],
        out_specs=pl.BlockSpec((ATT_S_BB, ATT_WIDTH), lambda i: (i, 0)),
        out_shape=jax.ShapeDtypeStruct((nseq, ATT_WIDTH), F32),
        scratch_shapes=[pltpu.VMEM((ATT_HEADS, LANES), F32), pltpu.VMEM((ATT_HEADS, LANES), F32)],
        compiler_params=_params("arbitrary"),
        name="attn_sample",
    )(att, ck, cv, bucket, rel_bias, sink)


GDN_TB = 128
GDN_NC = GDN_TB // DN_CHUNK
TAIL = 8


def _gdn_gates(ba, alog, dtb):
    beta = _sigmoid(ba)
    g = -jnp.exp(alog) * _softplus(ba + dtb)
    return beta, g


PAIR = 2 * DN_DK
N_PAIRS = DN_WIDTH // PAIR


def _pair_diag(x, lo):
    xb = x.astype(BF16)
    zero = jnp.zeros_like(xb)
    return jnp.concatenate([jnp.where(lo, xb, zero), jnp.where(lo, zero, xb)], axis=0)


def _gdn_prompt_kernel(xc_ref, dz_ref, ba_ref, cw_ref, alog_ref, dtb_ref, dnx_ref,
                       hsum_ref, expb_ref, expg_ref, ltri_ref,
                       o_ref, s_out_ref, xp_scr, s_scr):
    i = pl.program_id(0)
    nb = xc_ref.shape[0]

    @pl.when(i == 0)
    def _():
        xp_scr[:, 0:TAIL, :] = jnp.zeros((nb, TAIL, CONV_CH), F32)
        s_scr[...] = jnp.zeros(s_scr.shape, F32)

    hsum = hsum_ref[...]
    ri = lax.broadcasted_iota(jnp.int32, (DN_CHUNK, PAIR), 0)
    ci = lax.broadcasted_iota(jnp.int32, (DN_CHUNK, PAIR), 1)
    lo = ci < DN_DK
    cj = jnp.where(lo, ci, ci - DN_DK)
    causal = ri >= cj
    strict = ri > cj
    eye = (ri == cj).astype(F32)

    def sel2(x, m):
        hi = x.astype(BF16)
        lw = (x - hi.astype(F32)).astype(BF16)
        return (jnp.dot(hi, m, preferred_element_type=F32) + jnp.dot(lw, m, preferred_element_type=F32))

    pre = []
    for b in range(nb):
        xc = xc_ref[b]
        xp_scr[b, TAIL:, :] = xc
        y = xp_scr[b, TAIL - 3:TAIL - 3 + GDN_TB, :] * cw_ref[0:1, :]
        y = y + xp_scr[b, TAIL - 2:TAIL - 2 + GDN_TB, :] * cw_ref[1:2, :]
        y = y + xp_scr[b, TAIL - 1:TAIL - 1 + GDN_TB, :] * cw_ref[2:3, :]
        y = y + xc * cw_ref[3:4, :]
        xp_scr[b, 0:TAIL, :] = xc[GDN_TB - TAIL:, :]
        y = _silu(y)
        q = y[:, :DN_WIDTH]
        k = y[:, DN_WIDTH:2 * DN_WIDTH]
        v = y[:, 2 * DN_WIDTH:]
        q = q * lax.rsqrt(sel2(q * q, hsum) + EPS) * (DN_DK ** -0.5)
        k = k * lax.rsqrt(sel2(k * k, hsum) + EPS)
        beta_c, g_c = _gdn_gates(ba_ref[b], alog_ref[...], dtb_ref[...])
        beta = sel2(beta_c, expb_ref[...])
        gam_c = _mm_sel_lhs(ltri_ref[...], g_c)
        gam = _mm_sel_rhs(gam_c, expg_ref[...])
        gam_t = gam_c.T
        kb = k * beta
        egam = jnp.exp(gam)
        pre.append(dict(q=q, k=k, kb=kb, vb=v * beta, qg=q * egam, wr=kb * egam, gam=gam, gam_t=gam_t))

    probs = [(b, p) for b in range(nb) for p in range(N_PAIRS)]
    pick = lambda m: jnp.where(lo, m[:DN_DK], m[DN_DK:])
    o_rows = [[] for _ in range(nb)]
    for c in range(GDN_NC):
        r0, r1 = c * DN_CHUNK, (c + 1) * DN_CHUNK
        sl = lambda name, b, p: pre[b][name][r0:r1, p * PAIR:(p + 1) * PAIR]
        raws = []
        for b, p in probs:
            k_p = sl("k", b, p)
            k_rows = jnp.concatenate([jnp.where(lo, k_p, 0.0), jnp.where(lo, 0.0, k_p)], axis=0)
            raws.append(_mm_nt(jnp.concatenate([sl("kb", b, p), sl("q", b, p)], axis=0), k_rows))
        pws, ts, qks = [], [], []
        for (b, p), raw in zip(probs, raws):
            gcol = sl("gam", b, p)
            h0 = DN_HEADS + 2 * p
            gam_t = pre[b]["gam_t"]
            grow = jnp.concatenate([gam_t[h0:h0 + 1, r0:r1], gam_t[h0 + 1:h0 + 2, r0:r1]], axis=1)
            decay = jnp.exp(jnp.where(causal, gcol - grow, NEG_INF))
            a = jnp.where(strict, raw[:DN_CHUNK] * decay, 0.0)
            qks.append(jnp.where(causal, raw[DN_CHUNK:] * decay, 0.0))
            pws.append(-a)
            ts.append(eye - a)
        pws = [_mm(pw, _pair_diag(pw, lo)) for pw in pws]
        for _ in range(4):
            rs = [_mm(jnp.concatenate([pw, t], axis=0), _pair_diag(pw, lo)) for pw, t in zip(pws, ts)]
            pws = [r[:DN_CHUNK] for r in rs]
            ts = [t + r[DN_CHUNK:] for t, r in zip(ts, rs)]
        rs = [_mm(t, _pair_diag(pw, lo)) for pw, t in zip(pws, ts)]
        ts = [t + r for t, r in zip(ts, rs)]
        sols = [_mm(t, jnp.concatenate([_pair_diag(sl("vb", b, p), lo), _pair_diag(sl("wr", b, p), lo)],
                                       axis=1)) for (b, p), t in zip(probs, ts)]
        qkuws = [_mm(qk, jnp.concatenate([_pair_diag(s[:, :PAIR], lo), _pair_diag(s[:, PAIR:], lo)], axis=1))
                 for qk, s in zip(qks, sols)]
        crosses, gls = [], []
        for (b, p), s in zip(probs, sols):
            gam_last = pre[b]["gam"][r1 - 1:r1, p * PAIR:(p + 1) * PAIR]
            kd = sl("k", b, p) * jnp.exp(gam_last - sl("gam", b, p))
            crosses.append(_mm_tn(kd, s))
            gls.append(jnp.exp(gam_last))
        lhs = [jnp.concatenate([pick(cr[:, PAIR:]), sl("qg", b, p) - qkuw[:, PAIR:]], axis=0)
               for (b, p), cr, qkuw in zip(probs, crosses, qkuws)]
        s_olds = [s_scr[b, p] for b, p in probs]
        rs = [_mm(l, _pair_diag(s_old, lo)) for l, s_old in zip(lhs, s_olds)]
        o_pairs = [[] for _ in range(nb)]
        for (b, p), r, s_old, gl, cr, qkuw in zip(probs, rs, s_olds, gls, crosses, qkuws):
            s_scr[b, p] = gl * s_old - r[:DN_DK] + pick(cr[:, :PAIR])
            o_pairs[b].append(r[DN_DK:] + qkuw[:, :PAIR])
        for b in range(nb):
            o_rows[b].append(jnp.concatenate(o_pairs[b], axis=1))

    for b in range(nb):
        o = jnp.concatenate(o_rows[b], axis=0)
        ms = sel2(o * o, hsum) * (1.0 / DN_DV)
        o = o * lax.rsqrt(ms + EPS) * dnx_ref[...]
        o_ref[b] = o * _silu(dz_ref[b])

    @pl.when(i == pl.num_programs(0) - 1)
    def _():
        for b in range(nb):
            for p in range(N_PAIRS):
                s_p = s_scr[b, p]
                s_out_ref[b, 2 * p] = s_p[:, :DN_DV]
                s_out_ref[b, 2 * p + 1] = s_p[:, DN_DV:]


def _gdn_consts():
    lane = np.arange(DN_WIDTH)
    hsum = (lane[:, None] // DN_DV == lane[None, :] // DN_DV)
    src = np.arange(LANES)
    expb = (src[:, None] == lane[None, :] // DN_DV)
    expg = (src[:, None] == DN_HEADS + lane[None, :] // DN_DV)
    tok = np.arange(GDN_TB)
    ltri = np.logical_and(tok[:, None] >= tok[None, :],
                          tok[:, None] // DN_CHUNK == tok[None, :] // DN_CHUNK)
    as_bf16 = lambda m: jnp.asarray(m.astype(np.float32), dtype=BF16)
    return as_bf16(hsum), as_bf16(expb), as_bf16(expg), as_bf16(ltri)


def _gdn_prompt(xc, dz, ba, conv_w, alog, dtb, dnx, batch, seq):
    nt = seq // GDN_TB
    hsum, expb, expg, ltri = _gdn_consts()
    row = lambda n: pl.BlockSpec((batch, GDN_TB, n), lambda i: (0, i, 0))
    full = lambda a: pl.BlockSpec(a.shape, lambda i: (0,) * a.ndim)
    consts = (conv_w, alog, dtb, dnx, hsum, expb, expg, ltri)
    as3d = lambda a: a.reshape(batch, seq, a.shape[-1])
    o, s = pl.pallas_call(
        _gdn_prompt_kernel,
        grid=(nt,),
        in_specs=[row(CONV_CH), row(DN_WIDTH), row(LANES)] + [full(a) for a in consts],
        out_specs=[row(DN_WIDTH),
                   pl.BlockSpec((batch, DN_HEADS, DN_DK, DN_DV), lambda i: (0, 0, 0, 0))],
        out_shape=[jax.ShapeDtypeStruct((batch, seq, DN_WIDTH), F32),
                   jax.ShapeDtypeStruct((batch, DN_HEADS, DN_DK, DN_DV), F32)],
        scratch_shapes=[pltpu.VMEM((batch, TAIL + GDN_TB, CONV_CH), F32),
                        pltpu.VMEM((batch, N_PAIRS, DN_DK, PAIR), F32)],
        compiler_params=_params("arbitrary"),
        name="gdn_prompt",
    )(as3d(xc), as3d(dz), as3d(ba), *consts)
    return o.reshape(batch * seq, DN_WIDTH), s


GDN_S_BB = 8


def _gdn_sample_kernel(xc_ref, dz_ref, ba_ref, sc_ref, s_ref, cw_ref, alog_ref, dtb_ref, dn_ref,
                       hsum_ref, eye_ref, hsel_ref, hrep_ref, o_ref, s_out_ref):
    xc = xc_ref[...]
    y = sc_ref[0] * cw_ref[0:1, :]
    y = y + sc_ref[1] * cw_ref[1:2, :]
    y = y + sc_ref[2] * cw_ref[2:3, :]
    y = _silu(y + xc * cw_ref[3:4, :])
    hsum = hsum_ref[...]
    q = y[:, :DN_WIDTH]
    k = y[:, DN_WIDTH:2 * DN_WIDTH]
    v = y[:, 2 * DN_WIDTH:]
    q = q * lax.rsqrt(_mm_sel_rhs(q * q, hsum) + EPS) * (DN_DK ** -0.5)
    k = k * lax.rsqrt(_mm_sel_rhs(k * k, hsum) + EPS)
    beta_c, g_c = _gdn_gates(ba_ref[...], alog_ref[...], dtb_ref[...])
    eg_c = jnp.exp(g_c)
    eye = eye_ref[...]
    tr = lambda a: lax.dot_general(a, eye, (((0,), (0,)), ((), ())), precision=lax.Precision.HIGHEST,
                                   preferred_element_type=F32)
    k_t = tr(k)
    q_t = tr(q)
    beta_t = tr(beta_c)
    eg_t = tr(eg_c)
    dz = dz_ref[...]
    dn = dn_ref[...]
    split = lambda r: jnp.concatenate([r[:, h * DN_DV:(h + 1) * DN_DV] for h in range(DN_HEADS)], axis=0)
    hsel = hsel_ref[...]
    hrep = hrep_ref[...]
    for b in range(GDN_S_BB):
        s2 = s_ref[b]
        kc = k_t[:, b:b + 1]
        qc = q_t[:, b:b + 1]
        beta = beta_t[0:DN_HEADS, b:b + 1]
        eg = eg_t[DN_HEADS:2 * DN_HEADS, b:b + 1]
        vh = split(v[b:b + 1, :])
        qh = split(q[b:b + 1, :])
        kh = split(k[b:b + 1, :])
        ks = _mm_sel_lhs(hsel, kc * s2)
        v_new = beta * (vh - eg * ks)
        qs = _mm_sel_lhs(hsel, qc * s2)
        qk = jnp.sum(qh * kh, axis=-1, keepdims=True)
        o = eg * qs + qk * v_new
        rep = _mm_sel_lhs(hrep, jnp.concatenate(
            [v_new, jnp.broadcast_to(eg, (DN_HEADS, DN_DV))], axis=1))
        s_out_ref[b] = s2 * rep[:, DN_DV:] + kc * rep[:, :DN_DV]
        o = _rmsnorm(o, dn) * _silu(split(dz[b:b + 1, :]))
        o_ref[b] = o


def _gdn_sample(xc, dz, ba, sconv_t, state, conv_w, alog, dtb, dn):
    nseq = xc.shape[0]
    lane = np.arange(DN_WIDTH)
    hsum = jnp.asarray((lane[:, None] // DN_DV == lane[None, :] // DN_DV).astype(np.float32), dtype=BF16)
    eye = jnp.eye(GDN_S_BB, dtype=F32)
    hsel_np = (np.arange(DN_HEADS)[:, None] == lane[None, :] // DN_DK).astype(np.float32)
    hsel = jnp.asarray(hsel_np, dtype=BF16)
    hrep = jnp.asarray(hsel_np.T, dtype=BF16)
    row = lambda n: pl.BlockSpec((GDN_S_BB, n), lambda i: (i, 0))
    full = lambda a: pl.BlockSpec(a.shape, lambda i: (0,) * a.ndim)
    st = pl.BlockSpec((GDN_S_BB, DN_HEADS * DN_DK, DN_DV), lambda i: (i, 0, 0))
    consts = (conv_w, alog, dtb, dn, hsum, eye, hsel, hrep)
    return pl.pallas_call(
        _gdn_sample_kernel,
        grid=(nseq // GDN_S_BB,),
        in_specs=[row(CONV_CH), row(DN_WIDTH), row(LANES),
                  pl.BlockSpec((CONV_WIDTH - 1, GDN_S_BB, CONV_CH), lambda i: (0, i, 0)), st]
                 + [full(a) for a in consts],
        out_specs=[pl.BlockSpec((GDN_S_BB, DN_HEADS, DN_DV), lambda i: (i, 0, 0)), st],
        out_shape=[jax.ShapeDtypeStruct((nseq, DN_HEADS, DN_DV), F32),
                   jax.ShapeDtypeStruct(state.shape, F32)],
        compiler_params=_params("parallel"),
        name="gdn_sample",
    )(xc, dz, ba, sconv_t, state, *consts)


def _route(xn, wr):
    logits = jnp.dot(xn, wr, preferred_element_type=F32)
    lane = lax.broadcasted_iota(jnp.int32, logits.shape, 1).astype(F32)
    first_at = lambda hit: jnp.min(jnp.where(hit, lane, float(LANES)), axis=-1, keepdims=True)
    glog = jnp.where(lane < N_GROUPS, logits, NEG_INF)
    gmax = jnp.max(glog, axis=-1, keepdims=True)
    gsel = first_at(glog == gmax)
    pgsel = 1.0 / jnp.sum(jnp.exp(glog - gmax), axis=-1, keepdims=True)
    lo = ROUTER_OFF + gsel * EXPERTS_PER_GROUP
    in_group = jnp.logical_and(lane >= lo, lane < lo + EXPERTS_PER_GROUP)
    elog = jnp.where(in_group, logits, NEG_INF)
    m1 = jnp.max(elog, axis=-1, keepdims=True)
    i1 = first_at(elog == m1)
    z = jnp.sum(jnp.exp(elog - m1), axis=-1, keepdims=True)
    elog2 = jnp.where(lane == i1, NEG_INF, elog)
    m2 = jnp.max(elog2, axis=-1, keepdims=True)
    i2 = first_at(elog2 == m2)
    p1 = 1.0 / z
    p2 = jnp.exp(m2 - m1) / z
    tot = p1 + p2
    return lane, i1, i2, p1 / tot * pgsel, p2 / tot * pgsel


def _outproj(x_ref, oa_ref, od_ref, wo_ref):
    return x_ref[...] + _mm(oa_ref[...], wo_ref[:ATT_WIDTH, :]) + _mm(od_ref[...], wo_ref[ATT_WIDTH:, :])


def _outproj_router_kernel(x_ref, oa_ref, od_ref, wo_ref, g_ref, wr_ref, h_ref, xn_ref, gate_ref):
    h = _outproj(x_ref, oa_ref, od_ref, wo_ref)
    h_ref[...] = h
    xn = _rmsnorm(h, g_ref[...]).astype(BF16)
    xn_ref[...] = xn
    lane, i1, i2, g1, g2 = _route(xn, wr_ref[...])
    gate_ref[...] = jnp.where(lane == i1, g1, 0.0) + jnp.where(lane == i2, g2, 0.0)


def _outproj_router(x, oa, od, wo, g, wr):
    t = x.shape[0]
    tm = min(t, 256)
    row = lambda n: pl.BlockSpec((tm, n), lambda i: (i, 0))
    full = lambda a: pl.BlockSpec(a.shape, lambda i: (0,) * a.ndim)
    return pl.pallas_call(
        _outproj_router_kernel,
        grid=(t // tm,),
        in_specs=[row(D_MODEL), row(ATT_WIDTH), row(DN_WIDTH), full(wo), full(g), full(wr)],
        out_specs=[row(D_MODEL), row(D_MODEL), row(LANES)],
        out_shape=[jax.ShapeDtypeStruct((t, D_MODEL), F32), jax.ShapeDtypeStruct((t, D_MODEL), BF16),
                   jax.ShapeDtypeStruct((t, LANES), F32)],
        compiler_params=_params("parallel"),
        name="outproj_router",
    )(x, oa, od, wo, g, wr)


def _moe_kernel(xn_ref, gate_ref, wg_ref, wu_ref, wd_ref, o_ref):
    e = pl.program_id(1)
    xn = xn_ref[...]
    lane = lax.broadcasted_iota(jnp.int32, gate_ref.shape, 1)
    gate = jnp.sum(jnp.where(lane == e + ROUTER_OFF, gate_ref[...], 0.0), axis=-1, keepdims=True)
    hg = jnp.dot(xn, wg_ref[...], preferred_element_type=F32)
    hu = jnp.dot(xn, wu_ref[...], preferred_element_type=F32)
    hm = _silu(hg) * hu * gate
    y = jnp.dot(hm.astype(BF16), wd_ref[...], preferred_element_type=F32)

    @pl.when(e == 0)
    def _():
        o_ref[...] = y

    @pl.when(e > 0)
    def _():
        o_ref[...] += y


def _moe(xn, gates, wg, wu, wd):
    t = xn.shape[0]
    tm = min(t, 1024)
    return pl.pallas_call(
        _moe_kernel,
        grid=(t // tm, N_EXPERTS),
        in_specs=[pl.BlockSpec((tm, D_MODEL), lambda i, e: (i, 0)),
                  pl.BlockSpec((tm, LANES), lambda i, e: (i, 0)),
                  pl.BlockSpec((None, D_MODEL, D_EXPERT), lambda i, e: (e, 0, 0)),
                  pl.BlockSpec((None, D_MODEL, D_EXPERT), lambda i, e: (e, 0, 0)),
                  pl.BlockSpec((None, D_EXPERT, D_MODEL), lambda i, e: (e, 0, 0))],
        out_specs=pl.BlockSpec((tm, D_MODEL), lambda i, e: (i, 0)),
        out_shape=jax.ShapeDtypeStruct((t, D_MODEL), F32),
        compiler_params=_params("parallel", "arbitrary"),
        name="moe",
    )(xn, gates, wg, wu, wd)


MOE_TM = 256
POS_TM = 512
INFO_G1, INFO_G2, INFO_E1, INFO_E2 = 0, 1, 2, 3
DMA_UNROLL = 8


def _moe_tiles(t):
    return (2 * t) // MOE_TM + N_EXPERTS


def _route_kernel(x_ref, oa_ref, od_ref, wo_ref, g_ref, wr_ref, h_ref, xn_ref, info_ref):
    h = _outproj(x_ref, oa_ref, od_ref, wo_ref)
    h_ref[...] = h
    xn = _rmsnorm(h, g_ref[...])
    xn_ref[...] = xn
    lane, i1, i2, g1, g2 = _route(xn.astype(BF16), wr_ref[...])
    info = jnp.where(lane == INFO_G1, g1, 0.0) + jnp.where(lane == INFO_G2, g2, 0.0)
    info = info + jnp.where(lane == INFO_E1, i1, 0.0) + jnp.where(lane == INFO_E2, i2, 0.0)
    info_ref[...] = info


def _route_sparse(x, oa, od, wo, g, wr):
    t = x.shape[0]
    tm = 256
    row = lambda n: pl.BlockSpec((tm, n), lambda i: (i, 0))
    full = lambda a: pl.BlockSpec(a.shape, lambda i: (0,) * a.ndim)
    return pl.pallas_call(
        _route_kernel,
        grid=(t // tm,),
        in_specs=[row(D_MODEL), row(ATT_WIDTH), row(DN_WIDTH), full(wo), full(g), full(wr)],
        out_specs=[row(D_MODEL), row(D_MODEL), row(LANES)],
        out_shape=[jax.ShapeDtypeStruct((t, D_MODEL), F32), jax.ShapeDtypeStruct((t, D_MODEL), F32),
                   jax.ShapeDtypeStruct((t, LANES), F32)],
        compiler_params=_params("parallel"),
        name="route",
    )(x, oa, od, wo, g, wr)


def _positions_kernel(info_ref, ltri_ref, utri_ref, pos_ref, cnt_ref, run_scr, off_scr):
    phase = pl.program_id(0)
    i = pl.program_id(1)
    info = info_ref[...]
    lane = lax.broadcasted_iota(jnp.int32, info.shape, 1).astype(F32)
    hit1 = lane == info[:, INFO_E1:INFO_E1 + 1]
    hit2 = lane == info[:, INFO_E2:INFO_E2 + 1]
    onehot = jnp.logical_or(hit1, hit2).astype(F32)

    @pl.when(jnp.logical_and(phase == 0, i == 0))
    def _():
        run_scr[...] = jnp.zeros(run_scr.shape, F32)

    @pl.when(jnp.logical_and(phase == 1, i == 0))
    def _():
        cnt = run_scr[...]
        cnt_ref[...] = cnt
        tiles = jnp.floor((cnt + (MOE_TM - 1)) * (1.0 / MOE_TM))
        off_scr[...] = MOE_TM * jnp.dot(tiles.astype(BF16), utri_ref[...], preferred_element_type=F32)
        run_scr[...] = jnp.zeros(run_scr.shape, F32)

    @pl.when(phase == 0)
    def _():
        pos_ref[...] = jnp.zeros(pos_ref.shape, jnp.int32)

    @pl.when(phase == 1)
    def _():
        before = (jnp.dot(ltri_ref[...], onehot.astype(BF16), preferred_element_type=F32)
                  + run_scr[...] + off_scr[...])
        pos1 = jnp.sum(jnp.where(hit1, before, 0.0), axis=-1, keepdims=True)
        pos2 = jnp.sum(jnp.where(hit2, before, 0.0), axis=-1, keepdims=True)
        pos_ref[...] = (jnp.where(lane == 0, pos1, 0.0) + jnp.where(lane == 1, pos2, 0.0)).astype(jnp.int32)

    run_scr[...] += jnp.sum(onehot, axis=0, keepdims=True)


def _positions(info):
    t = info.shape[0]
    tm = min(t, POS_TM)
    tok = np.arange(tm)
    ltri = jnp.asarray((tok[:, None] > tok[None, :]).astype(np.float32), dtype=BF16)
    ln = np.arange(LANES)
    utri = jnp.asarray((ln[:, None] < ln[None, :]).astype(np.float32), dtype=BF16)
    full = lambda a: pl.BlockSpec(a.shape, lambda ph, i: (0,) * a.ndim)
    return pl.pallas_call(
        _positions_kernel,
        grid=(2, t // tm),
        in_specs=[pl.BlockSpec((tm, LANES), lambda ph, i: (i, 0)), full(ltri), full(utri)],
        out_specs=[pl.BlockSpec((tm, LANES), lambda ph, i: (i * ph, 0)),
                   pl.BlockSpec((1, LANES), lambda ph, i: (0, 0))],
        out_shape=[jax.ShapeDtypeStruct((t, LANES), jnp.int32), jax.ShapeDtypeStruct((1, LANES), F32)],
        scratch_shapes=[pltpu.VMEM((1, LANES), F32), pltpu.VMEM((1, LANES), F32)],
        compiler_params=_params("arbitrary", "arbitrary"),
        name="positions",
    )(info, ltri, utri)


def _row_copy(src_hbm, src_row, dst_hbm, dst_row, sem):
    return pltpu.make_async_copy(src_hbm.at[pl.ds(src_row, 1)], dst_hbm.at[pl.ds(dst_row, 1)], sem)


def _scatter_kernel(pos1_ref, pos2_ref, last_ref, used_ref, nt_ref, xn_hbm, zero_hbm, xs_hbm, sem, zsem,
                    *, n_tok):
    max_tiles = xs_hbm.shape[0] // MOE_TM

    def zero_tile(tile):
        return pltpu.make_async_copy(zero_hbm, xs_hbm.at[pl.ds(tile * MOE_TM, MOE_TM)], zsem)

    def for_unused(fn):
        def body(tile, carry):
            fn(tile)
            return carry
        lax.fori_loop(nt_ref[0], max_tiles, body, 0)

    for e in range(N_EXPERTS):
        @pl.when(used_ref[e] > 0)
        def _():
            zero_tile(last_ref[e]).start()
    for_unused(lambda tile: zero_tile(tile).start())
    for e in range(N_EXPERTS):
        @pl.when(used_ref[e] > 0)
        def _():
            zero_tile(last_ref[e]).wait()
    for_unused(lambda tile: zero_tile(tile).wait())

    chunk = MOE_TM

    def issue(c):
        def body(j, carry):
            tok = c * chunk + j
            _row_copy(xn_hbm, tok, xs_hbm, pos1_ref[tok], sem).start()
            _row_copy(xn_hbm, tok, xs_hbm, pos2_ref[tok], sem).start()
            return carry
        lax.fori_loop(0, chunk, body, 0, unroll=DMA_UNROLL)

    def wait_group():
        pltpu.make_async_copy(xn_hbm.at[pl.ds(0, 2 * chunk)], xs_hbm.at[pl.ds(0, 2 * chunk)], sem).wait()

    issue(0)

    def step(c, carry):
        issue(c)
        wait_group()
        return carry
    lax.fori_loop(1, n_tok // chunk, step, 0)
    wait_group()


def _scatter_rows(xn, pos1, pos2, last_tile, used, n_tiles, n_rows):
    t = xn.shape[0]
    zero = jnp.zeros((MOE_TM, D_MODEL), F32)
    any_spec = pl.BlockSpec(memory_space=pl.ANY)
    return pl.pallas_call(
        functools.partial(_scatter_kernel, n_tok=t),
        grid_spec=pltpu.PrefetchScalarGridSpec(
            num_scalar_prefetch=5, grid=(1,),
            in_specs=[any_spec, any_spec], out_specs=any_spec,
            scratch_shapes=[pltpu.SemaphoreType.DMA, pltpu.SemaphoreType.DMA]),
        out_shape=jax.ShapeDtypeStruct((n_rows, D_MODEL), F32),
        compiler_params=_params("arbitrary"),
        name="scatter_rows",
    )(pos1, pos2, last_tile, used, n_tiles, xn, zero)


def _experts_kernel(te_ref, nt_ref, xs_ref, wg_ref, wu_ref, wd_ref, ys_ref):
    used = pl.program_id(0) < nt_ref[0]

    @pl.when(used)
    def _():
        x = xs_ref[...].astype(BF16)
        hg = jnp.dot(x, wg_ref[...], preferred_element_type=F32)
        hu = jnp.dot(x, wu_ref[...], preferred_element_type=F32)
        hm = (_silu(hg) * hu).astype(BF16)
        ys_ref[...] = jnp.dot(hm, wd_ref[...], preferred_element_type=F32)

    @pl.when(jnp.logical_not(used))
    def _():
        ys_ref[...] = jnp.zeros(ys_ref.shape, F32)


def _experts(xs, tile_expert, n_tiles, wg, wu, wd):
    max_tiles = xs.shape[0] // MOE_TM
    rows = pl.BlockSpec((MOE_TM, D_MODEL), lambda i, te, nt: (i, 0))
    return pl.pallas_call(
        _experts_kernel,
        grid_spec=pltpu.PrefetchScalarGridSpec(
            num_scalar_prefetch=2, grid=(max_tiles,),
            in_specs=[rows,
                      pl.BlockSpec((None, D_MODEL, D_EXPERT), lambda i, te, nt: (te[i], 0, 0)),
                      pl.BlockSpec((None, D_MODEL, D_EXPERT), lambda i, te, nt: (te[i], 0, 0)),
                      pl.BlockSpec((None, D_EXPERT, D_MODEL), lambda i, te, nt: (te[i], 0, 0))],
            out_specs=rows),
        out_shape=jax.ShapeDtypeStruct(xs.shape, F32),
        compiler_params=_params("arbitrary"),
        name="experts",
    )(tile_expert, n_tiles, xs, wg, wu, wd)


def _ple_gather_kernel(pos1_ref, pos2_ref, h_ref, info_ref, p_ref, wpp_ref, wpg_ref, gp_ref, gf_ref,
                       ys_hbm, y_ref, ybuf, sem):
    i = pl.program_id(0)
    n = pl.num_programs(0)
    tm = h_ref.shape[0]

    def issue(tile, slot):
        def body(j, carry):
            tok = tile * tm + j
            pltpu.make_async_copy(ys_hbm.at[pl.ds(pos1_ref[tok], 1)], ybuf.at[slot, 0, pl.ds(j, 1)],
                                  sem.at[slot]).start()
            pltpu.make_async_copy(ys_hbm.at[pl.ds(pos2_ref[tok], 1)], ybuf.at[slot, 1, pl.ds(j, 1)],
                                  sem.at[slot]).start()
            return carry
        lax.fori_loop(0, tm, body, 0, unroll=DMA_UNROLL)

    @pl.when(i == 0)
    def _():
        issue(0, 0)

    @pl.when(i + 1 < n)
    def _():
        issue(i + 1, (i + 1) % 2)

    slot = i % 2
    pltpu.make_async_copy(ybuf.at[slot], ybuf.at[slot], sem.at[slot]).wait()
    info = info_ref[...]
    moe = info[:, INFO_G1:INFO_G1 + 1] * ybuf[slot, 0] + info[:, INFO_G2:INFO_G2 + 1] * ybuf[slot, 1]
    h = h_ref[...] + moe
    hn = _rmsnorm(h, gp_ref[...])
    h = h + _mm(p_ref[...], wpp_ref[...]) * _sigmoid(_mm(hn, wpg_ref[...]))
    y_ref[...] = _rmsnorm(h, gf_ref[...])


def _ple_gather(h, info, p, ys, pos1, pos2, wpp, wpg, gp, gf):
    t = h.shape[0]
    tm = 256
    row = lambda n: pl.BlockSpec((tm, n), lambda i, p1, p2: (i, 0))
    full = lambda a: pl.BlockSpec(a.shape, lambda i, p1, p2: (0,) * a.ndim)
    return pl.pallas_call(
        _ple_gather_kernel,
        grid_spec=pltpu.PrefetchScalarGridSpec(
            num_scalar_prefetch=2, grid=(t // tm,),
            in_specs=[row(D_MODEL), row(LANES), row(PLE_DIM), full(wpp), full(wpg), full(gp), full(gf),
                      pl.BlockSpec(memory_space=pl.ANY)],
            out_specs=row(D_MODEL),
            scratch_shapes=[pltpu.VMEM((2, 2, tm, D_MODEL), F32), pltpu.SemaphoreType.DMA((2,))]),
        out_shape=jax.ShapeDtypeStruct((t, D_MODEL), F32),
        compiler_params=_params("arbitrary"),
        name="ple_gather",
    )(pos1, pos2, h, info, p, wpp, wpg, gp, gf, ys)


def _tile_tables(cnt, max_tiles):
    tiles_e = (cnt + (MOE_TM - 1)) // MOE_TM
    ends = jnp.cumsum(tiles_e)
    n_tiles = ends[-1]
    idx = jnp.minimum(jnp.arange(max_tiles, dtype=jnp.int32), n_tiles - 1)
    tile_expert = jnp.sum((idx[:, None] >= ends[None, :]).astype(jnp.int32), axis=1)
    return tile_expert, n_tiles.reshape(1), (ends - 1).astype(jnp.int32), tiles_e.astype(jnp.int32)


def _ple_final_kernel(h_ref, m_ref, p_ref, wpp_ref, wpg_ref, gp_ref, gf_ref, y_ref):
    h = h_ref[...] + m_ref[...]
    hn = _rmsnorm(h, gp_ref[...])
    h = h + _mm(p_ref[...], wpp_ref[...]) * _sigmoid(_mm(hn, wpg_ref[...]))
    y_ref[...] = _rmsnorm(h, gf_ref[...])


def _ple_final(h, m, p, wpp, wpg, gp, gf):
    t = h.shape[0]
    tm = min(t, 256)
    row = lambda n: pl.BlockSpec((tm, n), lambda i: (i, 0))
    full = lambda a: pl.BlockSpec(a.shape, lambda i: (0,) * a.ndim)
    return pl.pallas_call(
        _ple_final_kernel,
        grid=(t // tm,),
        in_specs=[row(D_MODEL), row(D_MODEL), row(PLE_DIM), full(wpp), full(wpg), full(gp), full(gf)],
        out_specs=row(D_MODEL),
        out_shape=jax.ShapeDtypeStruct((t, D_MODEL), F32),
        compiler_params=_params("parallel"),
        name="ple_final",
    )(h, m, p, wpp, wpg, gp, gf)


def kernel(x_prompt, x_sample, p_prompt, p_sample, cache_k, cache_v, state_conv, state_S, rel_bias, norm_mix, w_in, att_sink, conv_w, dn_A_log, dn_dt_bias, dn_norm, w_out, norm_ffn, w_router_group, w_router_expert, w_gate, w_up, w_down, w_ple_proj, w_ple_gate, norm_ple, norm_final):
    batch, seq, _ = x_prompt.shape
    nseq = x_sample.shape[0]
    assert x_sample.shape[1] == 1 and norm_mix.shape[0] == 1 and cache_k.shape[2] == WINDOW
    assert seq % GDN_TB == 0 and seq % ATT_BLOCK == 0

    wi = w_in[0]
    o_db = ATT_COLS + CONV_CH
    w_in_re = jnp.concatenate(
        [wi[:, :o_db], wi[:, o_db + 2 * DN_HEADS:], wi[:, o_db:o_db + 2 * DN_HEADS],
         jnp.zeros((D_MODEL, LANES - 2 * DN_HEADS), F32)], axis=1).astype(BF16)
    row = lambda a: a.reshape(1, -1).astype(F32)
    pad_lanes = lambda a, off: jnp.zeros((1, LANES), F32).at[0, off:off + a.shape[0]].set(a)
    alog = pad_lanes(dn_A_log[0], DN_HEADS)
    dtb = pad_lanes(dn_dt_bias[0], DN_HEADS)
    dnx = jnp.tile(dn_norm[0], DN_HEADS).reshape(1, DN_WIDTH)
    w_router = jnp.concatenate(
        [w_router_group[0], w_router_expert[0],
         jnp.zeros((D_MODEL, LANES - N_GROUPS - N_EXPERTS), F32)], axis=1).astype(BF16)
    wo = w_out[0].astype(BF16)
    wg, wu, wd = w_gate[0].astype(BF16), w_up[0].astype(BF16), w_down[0].astype(BF16)
    wpp, wpg = w_ple_proj[0].astype(BF16), w_ple_gate[0].astype(BF16)
    sink = att_sink[0]

    qi = np.arange(ATT_BLOCK)[:, None]
    kj = np.arange(2 * ATT_BLOCK)[None, :]
    bucket_p = jnp.asarray(_t5_bucket_np(qi + ATT_BLOCK - kj))
    bucket_s = jnp.asarray(_t5_bucket_np(WINDOW - np.arange(WINDOW)[None, :]))

    def tail(x, o_att, o_dn, p):
        h1, xn2, gates = _outproj_router(x, o_att, o_dn, wo, row(norm_ffn[0]), w_router)
        moe = _moe(xn2, gates, wg, wu, wd)
        return _ple_final(h1, moe, p, wpp, wpg, row(norm_ple[0]), row(norm_final))

    xp = x_prompt.reshape(batch * seq, D_MODEL)
    att_p, xc_p, dz_p, ba_p = _inproj(xp, row(norm_mix[0]), w_in_re)
    o_att_p = _attn_prompt(att_p, bucket_p, rel_bias, sink, batch, seq)
    o_dn_p, s_p = _gdn_prompt(xc_p, dz_p, ba_p, conv_w[0], alog, dtb, dnx, batch, seq)
    h1, xn2, info = _route_sparse(xp, o_att_p, o_dn_p, wo, row(norm_ffn[0]), w_router)
    pos, cnt = _positions(info)
    pos1, pos2 = pos[:, 0], pos[:, 1]
    max_tiles = _moe_tiles(batch * seq)
    cnt_e = cnt[0, ROUTER_OFF:ROUTER_OFF + N_EXPERTS].astype(jnp.int32)
    tile_expert, n_tiles, last_tile, used = _tile_tables(cnt_e, max_tiles)
    xs = _scatter_rows(xn2, pos1, pos2, last_tile, used, n_tiles, max_tiles * MOE_TM)
    ys = _experts(xs, tile_expert, n_tiles, wg, wu, wd)
    y_p = _ple_gather(h1, info, p_prompt[0].reshape(batch * seq, PLE_DIM), ys, pos1, pos2,
                      wpp, wpg, row(norm_ple[0]), row(norm_final))

    xs = x_sample.reshape(nseq, D_MODEL)
    att_s, xc_s, dz_s, ba_s = _inproj(xs, row(norm_mix[0]), w_in_re)
    ck = cache_k[0].reshape(nseq, WINDOW, KV_WIDTH)
    cv = cache_v[0].reshape(nseq, WINDOW, KV_WIDTH)
    o_att_s = _attn_sample(att_s, ck, cv, bucket_s, rel_bias, sink)
    sconv_t = jnp.swapaxes(state_conv[0], 0, 1)
    o_dn_s, s_s = _gdn_sample(xc_s, dz_s, ba_s, sconv_t,
                              state_S[0].reshape(nseq, DN_HEADS * DN_DK, DN_DV), conv_w[0], alog, dtb,
                              dn_norm[0].reshape(1, DN_DV))
    s_s = s_s.reshape(nseq, DN_HEADS, DN_DK, DN_DV)
    y_s = tail(xs, o_att_s, o_dn_s.reshape(nseq, DN_WIDTH), p_sample[0].reshape(nseq, PLE_DIM))

    att_p3 = att_p.reshape(batch, seq, ATT_COLS)
    kv_shape = (1, batch, WINDOW, ATT_KV_HEADS, HEAD_DIM)
    k_p = att_p3[:, seq - WINDOW:, ATT_WIDTH:ATT_WIDTH + KV_WIDTH].reshape(kv_shape)
    v_p = att_p3[:, seq - WINDOW:, ATT_WIDTH + KV_WIDTH:].reshape(kv_shape)
    conv_p = xc_p.reshape(batch, seq, CONV_CH)[:, seq - (CONV_WIDTH - 1):][None]
    k_new = att_s[:, None, ATT_WIDTH:ATT_WIDTH + KV_WIDTH]
    v_new = att_s[:, None, ATT_WIDTH + KV_WIDTH:]
    kv_s_shape = (1, nseq, WINDOW, ATT_KV_HEADS, HEAD_DIM)
    k_s = jnp.concatenate([ck[:, 1:], k_new], axis=1).reshape(kv_s_shape)
    v_s = jnp.concatenate([cv[:, 1:], v_new], axis=1).reshape(kv_s_shape)
    conv_s = jnp.concatenate([state_conv[0][:, 1:], xc_s[:, None, :]], axis=1)[None]
    return (y_p.reshape(batch, seq, D_MODEL), y_s.reshape(nseq, 1, D_MODEL),
            k_p, v_p, conv_p, s_p[None], k_s, v_s, conv_s, s_s[None])
```

```python
import functools
import math

import numpy as np
import jax
import jax.numpy as jnp
from jax import lax
from jax.experimental import pallas as pl
from jax.experimental.pallas import tpu as pltpu

F32 = jnp.float32
BF16 = jnp.bfloat16

D_MODEL = 1024
ATT_HEADS = 8
ATT_KV_HEADS = 2
HEAD_DIM = 64
GQA = ATT_HEADS // ATT_KV_HEADS
WINDOW = 128
ATT_BLOCK = 128
N_BUCKETS = 32
DN_HEADS = 8
DN_DK = 64
DN_DV = 64
CONV_WIDTH = 4
DN_CHUNK = 64
ATT_WIDTH = ATT_HEADS * HEAD_DIM
KV_WIDTH = ATT_KV_HEADS * HEAD_DIM
DN_WIDTH = DN_HEADS * DN_DV
CONV_CH = 3 * DN_WIDTH
N_GROUPS = 4
EXPERTS_PER_GROUP = 8
N_EXPERTS = N_GROUPS * EXPERTS_PER_GROUP
D_EXPERT = 256
PLE_DIM = 256
EPS = 1e-6
NEG_INF = float("-inf")

ATT_COLS = ATT_WIDTH + 2 * KV_WIDTH
LANES = 128
IN_COLS = ATT_COLS + CONV_CH + DN_WIDTH + LANES
ROUTER_OFF = N_GROUPS
VMEM_LIMIT = 48 * 1024 * 1024


def _params(*sem):
    return pltpu.CompilerParams(dimension_semantics=sem, vmem_limit_bytes=VMEM_LIMIT)


def _mm(a, b):
    return jnp.dot(a.astype(BF16), b.astype(BF16), preferred_element_type=F32)


def _mm_nt(a, b):
    return lax.dot_general(a.astype(BF16), b.astype(BF16), (((1,), (1,)), ((), ())),
                           preferred_element_type=F32)


def _mm_tn(a, b):
    return lax.dot_general(a.astype(BF16), b.astype(BF16), (((0,), (0,)), ((), ())),
                           preferred_element_type=F32)


def _split3(x):
    h1 = x.astype(BF16)
    r1 = x - h1.astype(F32)
    h2 = r1.astype(BF16)
    h3 = (r1 - h2.astype(F32)).astype(BF16)
    return h1, h2, h3


def _mm_sel_rhs(x, sel):
    h1, h2, h3 = _split3(x)
    d = lambda h: jnp.dot(h, sel, preferred_element_type=F32)
    return d(h1) + d(h2) + d(h3)


def _mm_sel_lhs(sel, x):
    h1, h2, h3 = _split3(x)
    d = lambda h: jnp.dot(sel, h, preferred_element_type=F32)
    return d(h1) + d(h2) + d(h3)


def _mm3(a, b):
    ah = a.astype(BF16)
    al = (a - ah.astype(F32)).astype(BF16)
    bh = b.astype(BF16)
    bl = (b - bh.astype(F32)).astype(BF16)
    d = lambda u, v: jnp.dot(u, v, preferred_element_type=F32)
    return d(ah, bh) + d(ah, bl) + d(al, bh)


def _sigmoid(x):
    return 1.0 / (1.0 + jnp.exp(-x))


def _silu(x):
    return x * _sigmoid(x)


def _softplus(x):
    return jnp.maximum(x, 0.0) + jnp.log1p(jnp.exp(-jnp.abs(x)))


def _rmsnorm(x, g):
    return x * lax.rsqrt(jnp.mean(x * x, axis=-1, keepdims=True) + EPS) * g


def _t5_bucket_np(dist):
    max_exact = N_BUCKETS // 2
    d = np.maximum(dist, 0)
    ratio = (np.log(np.maximum(d, 1).astype(np.float32) / np.float32(max_exact))
             / np.float32(math.log(WINDOW / max_exact))).astype(np.float32)
    large = np.minimum(max_exact + (ratio * np.float32(N_BUCKETS - max_exact)).astype(np.int32),
                       N_BUCKETS - 1)
    return np.where(d < max_exact, d, large).astype(np.int32)


def _bias_lookup(bucket, rb_ref, h):
    acc = jnp.zeros(bucket.shape, F32)
    for t in range(N_BUCKETS):
        acc = jnp.where(bucket == t, rb_ref[t, h], acc)
    return acc


def _inproj_kernel(x_ref, g_ref, w_ref, att_ref, xc_ref, dz_ref, ba_ref):
    xn = _rmsnorm(x_ref[...], g_ref[...]).astype(BF16)
    o0, o1, o2 = ATT_COLS, ATT_COLS + CONV_CH, ATT_COLS + CONV_CH + DN_WIDTH
    att_ref[...] = jnp.dot(xn, w_ref[:, :o0], preferred_element_type=F32)
    xc_ref[...] = jnp.dot(xn, w_ref[:, o0:o1], preferred_element_type=F32)
    dz_ref[...] = jnp.dot(xn, w_ref[:, o1:o2], preferred_element_type=F32)
    ba_ref[...] = jnp.dot(xn, w_ref[:, o2:], preferred_element_type=F32)


def _inproj(x, g, w):
    t = x.shape[0]
    tm = min(t, 256)
    row = lambda n: pl.BlockSpec((tm, n), lambda i: (i, 0))
    full = lambda a: pl.BlockSpec(a.shape, lambda i: (0,) * a.ndim)
    return pl.pallas_call(
        _inproj_kernel,
        grid=(t // tm,),
        in_specs=[row(D_MODEL), full(g), full(w)],
        out_specs=[row(ATT_COLS), row(CONV_CH), row(DN_WIDTH), row(LANES)],
        out_shape=[jax.ShapeDtypeStruct((t, n), F32) for n in (ATT_COLS, CONV_CH, DN_WIDTH, LANES)],
        compiler_params=_params("parallel"),
        name="inproj",
    )(x, g, w)


def _attn_prompt_kernel(cur_ref, prev_ref, bucket_ref, rb_ref, sink_ref, o_ref, bias_scr):
    first = jnp.logical_and(pl.program_id(0) == 0, pl.program_id(1) == 0)
    qi = lax.broadcasted_iota(jnp.int32, (ATT_BLOCK, 2 * ATT_BLOCK), 0)
    kj = lax.broadcasted_iota(jnp.int32, (ATT_BLOCK, 2 * ATT_BLOCK), 1)

    @pl.when(first)
    def _():
        dist = qi + ATT_BLOCK - kj
        band = jnp.logical_and(dist >= 0, dist < WINDOW)
        bucket = bucket_ref[...]
        for h in range(ATT_HEADS):
            bias_scr[h] = jnp.where(band, _bias_lookup(bucket, rb_ref, h), NEG_INF)

    cur = cur_ref[...]
    prev = prev_ref[...]
    q = cur[:, :ATT_WIDTH] * (HEAD_DIM ** -0.5)
    kcat = jnp.concatenate([prev[:, ATT_WIDTH:ATT_WIDTH + KV_WIDTH],
                            cur[:, ATT_WIDTH:ATT_WIDTH + KV_WIDTH]], axis=0)
    vcat = jnp.concatenate([prev[:, ATT_WIDTH + KV_WIDTH:], cur[:, ATT_WIDTH + KV_WIDTH:]], axis=0)
    keep = jnp.logical_or(pl.program_id(1) > 0, kj >= ATT_BLOCK)
    outs = []
    for h in range(ATT_HEADS):
        g = h // GQA
        qh = q[:, h * HEAD_DIM:(h + 1) * HEAD_DIM]
        kh = kcat[:, g * HEAD_DIM:(g + 1) * HEAD_DIM]
        vh = vcat[:, g * HEAD_DIM:(g + 1) * HEAD_DIM]
        s = jnp.where(keep, _mm_nt(qh, kh) + bias_scr[h], NEG_INF)
        sink = sink_ref[h]
        m = jnp.maximum(jnp.max(s, axis=-1, keepdims=True), sink)
        p = jnp.exp(s - m)
        den = jnp.sum(p, axis=-1, keepdims=True) + jnp.exp(sink - m)
        outs.append(_mm(p, vh) / den)
    o_ref[...] = jnp.concatenate(outs, axis=1)


def _attn_prompt(att, bucket, rel_bias, sink, batch, seq):
    nb = seq // ATT_BLOCK
    smem = pl.BlockSpec(memory_space=pltpu.SMEM)
    return pl.pallas_call(
        _attn_prompt_kernel,
        grid=(batch, nb),
        in_specs=[
            pl.BlockSpec((ATT_BLOCK, ATT_COLS), lambda b, i: (b * nb + i, 0)),
            pl.BlockSpec((ATT_BLOCK, ATT_COLS), lambda b, i: (b * nb + jnp.maximum(i - 1, 0), 0)),
            pl.BlockSpec(bucket.shape, lambda b, i: (0, 0)),
            smem, smem,
        ],
        out_specs=pl.BlockSpec((ATT_BLOCK, ATT_WIDTH), lambda b, i: (b * nb + i, 0)),
        out_shape=jax.ShapeDtypeStruct((batch * seq, ATT_WIDTH), F32),
        scratch_shapes=[pltpu.VMEM((ATT_HEADS, ATT_BLOCK, 2 * ATT_BLOCK), F32)],
        compiler_params=_params("arbitrary", "arbitrary"),
        name="attn_prompt",
    )(att, att, bucket, rel_bias, sink)


ATT_S_BB = 8


def _attn_sample_kernel(att_ref, ck_ref, cv_ref, bucket_ref, rb_ref, sink_ref, o_ref,
                        bias_scr, col_scr):
    hrow = lax.broadcasted_iota(jnp.int32, (ATT_HEADS, LANES), 0)
    lane = lax.broadcasted_iota(jnp.int32, (ATT_HEADS, LANES), 1)

    @pl.when(pl.program_id(0) == 0)
    def _():
        bucket = jnp.broadcast_to(bucket_ref[...], (ATT_HEADS, LANES))
        bias = jnp.zeros((ATT_HEADS, LANES), F32)
        cols = jnp.zeros((ATT_HEADS, LANES), F32)
        for h in range(ATT_HEADS):
            bias = jnp.where(hrow == h, _bias_lookup(bucket, rb_ref, h), bias)
            cols = jnp.where(jnp.logical_and(hrow == h, lane == 0), sink_ref[h], cols)
            cols = jnp.where(jnp.logical_and(hrow == h, lane == 1), rb_ref[0, h], cols)
        bias_scr[...] = jnp.where(lane >= 1, bias, NEG_INF)
        col_scr[...] = cols

    bias_c = bias_scr[...]
    sink = col_scr[:, 0:1]
    bias_n = col_scr[:, 1:2]
    same_group = (hrow // GQA) == (lane // HEAD_DIM)
    low_group = lax.broadcasted_iota(jnp.int32, (ATT_HEADS, HEAD_DIM), 0) < GQA
    for b in range(ATT_S_BB):
        row = att_ref[b:b + 1, :]
        q = row[:, :ATT_WIDTH] * (HEAD_DIM ** -0.5)
        kn = row[:, ATT_WIDTH:ATT_WIDTH + KV_WIDTH]
        vn = row[:, ATT_WIDTH + KV_WIDTH:]
        qh = jnp.concatenate([q[:, h * HEAD_DIM:(h + 1) * HEAD_DIM] for h in range(ATT_HEADS)], axis=0)
        q_bd = jnp.where(same_group, jnp.concatenate([qh, qh], axis=1), 0.0)
        rnd = lambda a: a.astype(BF16).astype(F32)
        s_c = _mm_nt(q_bd, ck_ref[b]) + bias_c
        s_n = jnp.sum(rnd(q_bd) * rnd(kn), axis=-1, keepdims=True) + bias_n
        m = jnp.maximum(jnp.maximum(jnp.max(s_c, axis=-1, keepdims=True), s_n), sink)
        p_c = jnp.exp(s_c - m)
        p_n = jnp.exp(s_n - m)
        den = jnp.sum(p_c, axis=-1, keepdims=True) + p_n + jnp.exp(sink - m)
        o_full = _mm(p_c / den, cv_ref[b]) + rnd(p_n / den) * rnd(vn)
        o_sel = jnp.where(low_group, o_full[:, :HEAD_DIM], o_full[:, HEAD_DIM:])
        o_ref[b:b + 1, :] = jnp.concatenate([o_sel[h:h + 1, :] for h in range(ATT_HEADS)], axis=1)


def _attn_sample(att, ck, cv, bucket, rel_bias, sink):
    nseq = att.shape[0]
    smem = pl.BlockSpec(memory_space=pltpu.SMEM)
    cache = pl.BlockSpec((ATT_S_BB, WINDOW, KV_WIDTH), lambda i: (i, 0, 0))
    return pl.pallas_call(
        _attn_sample_kernel,
        grid=(nseq // ATT_S_BB,),
        in_specs=[pl.BlockSpec((ATT_S_BB, ATT_COLS), lambda i: (i, 0)), cache, cache,
                  pl.BlockSpec(bucket.shape, lambda i: (0, 0)), smem, smem],
        out_specs=pl.BlockSpec((ATT_S_BB, ATT_WIDTH), lambda i: (i, 0)),
        out_shape=jax.ShapeDtypeStruct((nseq, ATT_WIDTH), F32),
        scratch_shapes=[pltpu.VMEM((ATT_HEADS, LANES), F32), pltpu.VMEM((ATT_HEADS, LANES), F32)],
        compiler_params=_params("arbitrary"),
        name="attn_sample",
    )(att, ck, cv, bucket, rel_bias, sink)


GDN_TB = 128
GDN_NC = GDN_TB // DN_CHUNK
TAIL = 8


def _gdn_gates(ba, alog, dtb):
    beta = _sigmoid(ba)
    g = -jnp.exp(alog) * _softplus(ba + dtb)
    return beta, g


PAIR = 2 * DN_DK
N_PAIRS = DN_WIDTH // PAIR


def _pair_diag(x, lo):
    xb = x.astype(BF16)
    zero = jnp.zeros_like(xb)
    return jnp.concatenate([jnp.where(lo, xb, zero), jnp.where(lo, zero, xb)], axis=0)


def _gdn_prompt_kernel(xc_ref, dz_ref, ba_ref, cw_ref, alog_ref, dtb_ref, dnx_ref,
                       hsum_ref, expb_ref, expg_ref, ltri_ref,
                       o_ref, s_out_ref, xp_scr, s_scr):
    i = pl.program_id(0)
    nb = xc_ref.shape[0]

    @pl.when(i == 0)
    def _():
        xp_scr[:, 0:TAIL, :] = jnp.zeros((nb, TAIL, CONV_CH), F32)
        s_scr[...] = jnp.zeros(s_scr.shape, F32)

    hsum = hsum_ref[...]
    ri = lax.broadcasted_iota(jnp.int32, (DN_CHUNK, PAIR), 0)
    ci = lax.broadcasted_iota(jnp.int32, (DN_CHUNK, PAIR), 1)
    lo = ci < DN_DK
    cj = jnp.where(lo, ci, ci - DN_DK)
    causal = ri >= cj
    strict = ri > cj
    eye = (ri == cj).astype(F32)

    def sel2(x, m):
        hi = x.astype(BF16)
        lw = (x - hi.astype(F32)).astype(BF16)
        return (jnp.dot(hi, m, preferred_element_type=F32) + jnp.dot(lw, m, preferred_element_type=F32))

    pre = []
    for b in range(nb):
        xc = xc_ref[b]
        xp_scr[b, TAIL:, :] = xc
        y = xp_scr[b, TAIL - 3:TAIL - 3 + GDN_TB, :] * cw_ref[0:1, :]
        y = y + xp_scr[b, TAIL - 2:TAIL - 2 + GDN_TB, :] * cw_ref[1:2, :]
        y = y + xp_scr[b, TAIL - 1:TAIL - 1 + GDN_TB, :] * cw_ref[2:3, :]
        y = y + xc * cw_ref[3:4, :]
        xp_scr[b, 0:TAIL, :] = xc[GDN_TB - TAIL:, :]
        y = _silu(y)
        q = y[:, :DN_WIDTH]
        k = y[:, DN_WIDTH:2 * DN_WIDTH]
        v = y[:, 2 * DN_WIDTH:]
        q = q * lax.rsqrt(sel2(q * q, hsum) + EPS) * (DN_DK ** -0.5)
        k = k * lax.rsqrt(sel2(k * k, hsum) + EPS)
        beta_c, g_c = _gdn_gates(ba_ref[b], alog_ref[...], dtb_ref[...])
        beta = sel2(beta_c, expb_ref[...])
        gam_c = _mm_sel_lhs(ltri_ref[...], g_c)
        gam = _mm_sel_rhs(gam_c, expg_ref[...])
        gam_t = gam_c.T
        kb = k * beta
        egam = jnp.exp(gam)
        pre.append(dict(q=q, k=k, kb=kb, vb=v * beta, qg=q * egam, wr=kb * egam, gam=gam, gam_t=gam_t))

    probs = [(b, p) for b in range(nb) for p in range(N_PAIRS)]
    pick = lambda m: jnp.where(lo, m[:DN_DK], m[DN_DK:])
    o_rows = [[] for _ in range(nb)]
    for c in range(GDN_NC):
        r0, r1 = c * DN_CHUNK, (c + 1) * DN_CHUNK
        sl = lambda name, b, p: pre[b][name][r0:r1, p * PAIR:(p + 1) * PAIR]
        raws = []
        for b, p in probs:
            k_p = sl("k", b, p)
            k_rows = jnp.concatenate([jnp.where(lo, k_p, 0.0), jnp.where(lo, 0.0, k_p)], axis=0)
            raws.append(_mm_nt(jnp.concatenate([sl("kb", b, p), sl("q", b, p)], axis=0), k_rows))
        pws, ts, qks = [], [], []
        for (b, p), raw in zip(probs, raws):
            gcol = sl("gam", b, p)
            h0 = DN_HEADS + 2 * p
            gam_t = pre[b]["gam_t"]
            grow = jnp.concatenate([gam_t[h0:h0 + 1, r0:r1], gam_t[h0 + 1:h0 + 2, r0:r1]], axis=1)
            decay = jnp.exp(jnp.where(causal, gcol - grow, NEG_INF))
            a = jnp.where(strict, raw[:DN_CHUNK] * decay, 0.0)
            qks.append(jnp.where(causal, raw[DN_CHUNK:] * decay, 0.0))
            pws.append(-a)
            ts.append(eye - a)
        pws = [_mm(pw, _pair_diag(pw, lo)) for pw in pws]
        for _ in range(4):
            rs = [_mm(jnp.concatenate([pw, t], axis=0), _pair_diag(pw, lo)) for pw, t in zip(pws, ts)]
            pws = [r[:DN_CHUNK] for r in rs]
            ts = [t + r[DN_CHUNK:] for t, r in zip(ts, rs)]
        rs = [_mm(t, _pair_diag(pw, lo)) for pw, t in zip(pws, ts)]
        ts = [t + r for t, r in zip(ts, rs)]
        sols = [_mm(t, jnp.concatenate([_pair_diag(sl("vb", b, p), lo), _pair_diag(sl("wr", b, p), lo)],
                                       axis=1)) for (b, p), t in zip(probs, ts)]
        qkuws = [_mm(qk, jnp.concatenate([_pair_diag(s[:, :PAIR], lo), _pair_diag(s[:, PAIR:], lo)], axis=1))
                 for qk, s in zip(qks, sols)]
        crosses, gls = [], []
        for (b, p), s in zip(probs, sols):
            gam_last = pre[b]["gam"][r1 - 1:r1, p * PAIR:(p + 1) * PAIR]
            kd = sl("k", b, p) * jnp.exp(gam_last - sl("gam", b, p))
            crosses.append(_mm_tn(kd, s))
            gls.append(jnp.exp(gam_last))
        lhs = [jnp.concatenate([pick(cr[:, PAIR:]), sl("qg", b, p) - qkuw[:, PAIR:]], axis=0)
               for (b, p), cr, qkuw in zip(probs, crosses, qkuws)]
        s_olds = [s_scr[b, p] for b, p in probs]
        rs = [_mm(l, _pair_diag(s_old, lo)) for l, s_old in zip(lhs, s_olds)]
        o_pairs = [[] for _ in range(nb)]
        for (b, p), r, s_old, gl, cr, qkuw in zip(probs, rs, s_olds, gls, crosses, qkuws):
            s_scr[b, p] = gl * s_old - r[:DN_DK] + pick(cr[:, :PAIR])
            o_pairs[b].append(r[DN_DK:] + qkuw[:, :PAIR])
        for b in range(nb):
            o_rows[b].append(jnp.concatenate(o_pairs[b], axis=1))

    for b in range(nb):
        o = jnp.concatenate(o_rows[b], axis=0)
        ms = sel2(o * o, hsum) * (1.0 / DN_DV)
        o = o * lax.rsqrt(ms + EPS) * dnx_ref[...]
        o_ref[b] = o * _silu(dz_ref[b])

    @pl.when(i == pl.num_programs(0) - 1)
    def _():
        for b in range(nb):
            for p in range(N_PAIRS):
                s_p = s_scr[b, p]
                s_out_ref[b, 2 * p] = s_p[:, :DN_DV]
                s_out_ref[b, 2 * p + 1] = s_p[:, DN_DV:]


def _gdn_consts():
    lane = np.arange(DN_WIDTH)
    hsum = (lane[:, None] // DN_DV == lane[None, :] // DN_DV)
    src = np.arange(LANES)
    expb = (src[:, None] == lane[None, :] // DN_DV)
    expg = (src[:, None] == DN_HEADS + lane[None, :] // DN_DV)
    tok = np.arange(GDN_TB)
    ltri = np.logical_and(tok[:, None] >= tok[None, :],
                          tok[:, None] // DN_CHUNK == tok[None, :] // DN_CHUNK)
    as_bf16 = lambda m: jnp.asarray(m.astype(np.float32), dtype=BF16)
    return as_bf16(hsum), as_bf16(expb), as_bf16(expg), as_bf16(ltri)


def _gdn_prompt(xc, dz, ba, conv_w, alog, dtb, dnx, batch, seq):
    nt = seq // GDN_TB
    hsum, expb, expg, ltri = _gdn_consts()
    row = lambda n: pl.BlockSpec((batch, GDN_TB, n), lambda i: (0, i, 0))
    full = lambda a: pl.BlockSpec(a.shape, lambda i: (0,) * a.ndim)
    consts = (conv_w, alog, dtb, dnx, hsum, expb, expg, ltri)
    as3d = lambda a: a.reshape(batch, seq, a.shape[-1])
    o, s = pl.pallas_call(
        _gdn_prompt_kernel,
        grid=(nt,),
        in_specs=[row(CONV_CH), row(DN_WIDTH), row(LANES)] + [full(a) for a in consts],
        out_specs=[row(DN_WIDTH),
                   pl.BlockSpec((batch, DN_HEADS, DN_DK, DN_DV), lambda i: (0, 0, 0, 0))],
        out_shape=[jax.ShapeDtypeStruct((batch, seq, DN_WIDTH), F32),
                   jax.ShapeDtypeStruct((batch, DN_HEADS, DN_DK, DN_DV), F32)],
        scratch_shapes=[pltpu.VMEM((batch, TAIL + GDN_TB, CONV_CH), F32),
                        pltpu.VMEM((batch, N_PAIRS, DN_DK, PAIR), F32)],
        compiler_params=_params("arbitrary"),
        name="gdn_prompt",
    )(as3d(xc), as3d(dz), as3d(ba), *consts)
    return o.reshape(batch * seq, DN_WIDTH), s


GDN_S_BB = 8


def _gdn_sample_kernel(xc_ref, dz_ref, ba_ref, sc_ref, s_ref, cw_ref, alog_ref, dtb_ref, dn_ref,
                       hsum_ref, eye_ref, hsel_ref, hrep_ref, o_ref, s_out_ref):
    xc = xc_ref[...]
    y = sc_ref[0] * cw_ref[0:1, :]
    y = y + sc_ref[1] * cw_ref[1:2, :]
    y = y + sc_ref[2] * cw_ref[2:3, :]
    y = _silu(y + xc * cw_ref[3:4, :])
    hsum = hsum_ref[...]
    q = y[:, :DN_WIDTH]
    k = y[:, DN_WIDTH:2 * DN_WIDTH]
    v = y[:, 2 * DN_WIDTH:]
    q = q * lax.rsqrt(_mm_sel_rhs(q * q, hsum) + EPS) * (DN_DK ** -0.5)
    k = k * lax.rsqrt(_mm_sel_rhs(k * k, hsum) + EPS)
    beta_c, g_c = _gdn_gates(ba_ref[...], alog_ref[...], dtb_ref[...])
    eg_c = jnp.exp(g_c)
    eye = eye_ref[...]
    tr = lambda a: lax.dot_general(a, eye, (((0,), (0,)), ((), ())), precision=lax.Precision.HIGHEST,
                                   preferred_element_type=F32)
    k_t = tr(k)
    q_t = tr(q)
    beta_t = tr(beta_c)
    eg_t = tr(eg_c)
    dz = dz_ref[...]
    dn = dn_ref[...]
    split = lambda r: jnp.concatenate([r[:, h * DN_DV:(h + 1) * DN_DV] for h in range(DN_HEADS)], axis=0)
    hsel = hsel_ref[...]
    hrep = hrep_ref[...]
    for b in range(GDN_S_BB):
        s2 = s_ref[b]
        kc = k_t[:, b:b + 1]
        qc = q_t[:, b:b + 1]
        beta = beta_t[0:DN_HEADS, b:b + 1]
        eg = eg_t[DN_HEADS:2 * DN_HEADS, b:b + 1]
        vh = split(v[b:b + 1, :])
        qh = split(q[b:b + 1, :])
        kh = split(k[b:b + 1, :])
        ks = _mm_sel_lhs(hsel, kc * s2)
        v_new = beta * (vh - eg * ks)
        qs = _mm_sel_lhs(hsel, qc * s2)
        qk = jnp.sum(qh * kh, axis=-1, keepdims=True)
        o = eg * qs + qk * v_new
        rep = _mm_sel_lhs(hrep, jnp.concatenate(
            [v_new, jnp.broadcast_to(eg, (DN_HEADS, DN_DV))], axis=1))
        s_out_ref[b] = s2 * rep[:, DN_DV:] + kc * rep[:, :DN_DV]
        o = _rmsnorm(o, dn) * _silu(split(dz[b:b + 1, :]))
        o_ref[b] = o


def _gdn_sample(xc, dz, ba, sconv_t, state, conv_w, alog, dtb, dn):
    nseq = xc.shape[0]
    lane = np.arange(DN_WIDTH)
    hsum = jnp.asarray((lane[:, None] // DN_DV == lane[None, :] // DN_DV).astype(np.float32), dtype=BF16)
    eye = jnp.eye(GDN_S_BB, dtype=F32)
    hsel_np = (np.arange(DN_HEADS)[:, None] == lane[None, :] // DN_DK).astype(np.float32)
    hsel = jnp.asarray(hsel_np, dtype=BF16)
    hrep = jnp.asarray(hsel_np.T, dtype=BF16)
    row = lambda n: pl.BlockSpec((GDN_S_BB, n), lambda i: (i, 0))
    full = lambda a: pl.BlockSpec(a.shape, lambda i: (0,) * a.ndim)
    st = pl.BlockSpec((GDN_S_BB, DN_HEADS * DN_DK, DN_DV), lambda i: (i, 0, 0))
    consts = (conv_w, alog, dtb, dn, hsum, eye, hsel, hrep)
    return pl.pallas_call(
        _gdn_sample_kernel,
        grid=(nseq // GDN_S_BB,),
        in_specs=[row(CONV_CH), row(DN_WIDTH), row(LANES),
                  pl.BlockSpec((CONV_WIDTH - 1, GDN_S_BB, CONV_CH), lambda i: (0, i, 0)), st]
                 + [full(a) for a in consts],
        out_specs=[pl.BlockSpec((GDN_S_BB, DN_HEADS, DN_DV), lambda i: (i, 0, 0)), st],
        out_shape=[jax.ShapeDtypeStruct((nseq, DN_HEADS, DN_DV), F32),
                   jax.ShapeDtypeStruct(state.shape, F32)],
        compiler_params=_params("parallel"),
        name="gdn_sample",
    )(xc, dz, ba, sconv_t, state, *consts)


def _route(xn, wr):
    logits = jnp.dot(xn, wr, preferred_element_type=F32)
    lane = lax.broadcasted_iota(jnp.int32, logits.shape, 1).astype(F32)
    first_at = lambda hit: jnp.min(jnp.where(hit, lane, float(LANES)), axis=-1, keepdims=True)
    glog = jnp.where(lane < N_GROUPS, logits, NEG_INF)
    gmax = jnp.max(glog, axis=-1, keepdims=True)
    gsel = first_at(glog == gmax)
    pgsel = 1.0 / jnp.sum(jnp.exp(glog - gmax), axis=-1, keepdims=True)
    lo = ROUTER_OFF + gsel * EXPERTS_PER_GROUP
    in_group = jnp.logical_and(lane >= lo, lane < lo + EXPERTS_PER_GROUP)
    elog = jnp.where(in_group, logits, NEG_INF)
    m1 = jnp.max(elog, axis=-1, keepdims=True)
    i1 = first_at(elog == m1)
    z = jnp.sum(jnp.exp(elog - m1), axis=-1, keepdims=True)
    elog2 = jnp.where(lane == i1, NEG_INF, elog)
    m2 = jnp.max(elog2, axis=-1, keepdims=True)
    i2 = first_at(elog2 == m2)
    p1 = 1.0 / z
    p2 = jnp.exp(m2 - m1) / z
    tot = p1 + p2
    return lane, i1, i2, p1 / tot * pgsel, p2 / tot * pgsel


def _outproj(x_ref, oa_ref, od_ref, wo_ref):
    return x_ref[...] + _mm(oa_ref[...], wo_ref[:ATT_WIDTH, :]) + _mm(od_ref[...], wo_ref[ATT_WIDTH:, :])


def _outproj_router_kernel(x_ref, oa_ref, od_ref, wo_ref, g_ref, wr_ref, h_ref, xn_ref, gate_ref):
    h = _outproj(x_ref, oa_ref, od_ref, wo_ref)
    h_ref[...] = h
    xn = _rmsnorm(h, g_ref[...]).astype(BF16)
    xn_ref[...] = xn
    lane, i1, i2, g1, g2 = _route(xn, wr_ref[...])
    gate_ref[...] = jnp.where(lane == i1, g1, 0.0) + jnp.where(lane == i2, g2, 0.0)


def _outproj_router(x, oa, od, wo, g, wr):
    t = x.shape[0]
    tm = min(t, 256)
    row = lambda n: pl.BlockSpec((tm, n), lambda i: (i, 0))
    full = lambda a: pl.BlockSpec(a.shape, lambda i: (0,) * a.ndim)
    return pl.pallas_call(
        _outproj_router_kernel,
        grid=(t // tm,),
        in_specs=[row(D_MODEL), row(ATT_WIDTH), row(DN_WIDTH), full(wo), full(g), full(wr)],
        out_specs=[row(D_MODEL), row(D_MODEL), row(LANES)],
        out_shape=[jax.ShapeDtypeStruct((t, D_MODEL), F32), jax.ShapeDtypeStruct((t, D_MODEL), BF16),
                   jax.ShapeDtypeStruct((t, LANES), F32)],
        compiler_params=_params("parallel"),
        name="outproj_router",
    )(x, oa, od, wo, g, wr)


def _moe_kernel(xn_ref, gate_ref, wg_ref, wu_ref, wd_ref, o_ref):
    e = pl.program_id(1)
    xn = xn_ref[...]
    lane = lax.broadcasted_iota(jnp.int32, gate_ref.shape, 1)
    gate = jnp.sum(jnp.where(lane == e + ROUTER_OFF, gate_ref[...], 0.0), axis=-1, keepdims=True)
    hg = jnp.dot(xn, wg_ref[...], preferred_element_type=F32)
    hu = jnp.dot(xn, wu_ref[...], preferred_element_type=F32)
    hm = _silu(hg) * hu * gate
    y = jnp.dot(hm.astype(BF16), wd_ref[...], preferred_element_type=F32)

    @pl.when(e == 0)
    def _():
        o_ref[...] = y

    @pl.when(e > 0)
    def _():
        o_ref[...] += y


def _moe(xn, gates, wg, wu, wd):
    t = xn.shape[0]
    tm = min(t, 1024)
    return pl.pallas_call(
        _moe_kernel,
        grid=(t // tm, N_EXPERTS),
        in_specs=[pl.BlockSpec((tm, D_MODEL), lambda i, e: (i, 0)),
                  pl.BlockSpec((tm, LANES), lambda i, e: (i, 0)),
                  pl.BlockSpec((None, D_MODEL, D_EXPERT), lambda i, e: (e, 0, 0)),
                  pl.BlockSpec((None, D_MODEL, D_EXPERT), lambda i, e: (e, 0, 0)),
                  pl.BlockSpec((None, D_EXPERT, D_MODEL), lambda i, e: (e, 0, 0))],
        out_specs=pl.BlockSpec((tm, D_MODEL), lambda i, e: (i, 0)),
        out_shape=jax.ShapeDtypeStruct((t, D_MODEL), F32),
        compiler_params=_params("parallel", "arbitrary"),
        name="moe",
    )(xn, gates, wg, wu, wd)


MOE_TM = 256
POS_TM = 512
INFO_G1, INFO_G2, INFO_E1, INFO_E2 = 0, 1, 2, 3
DMA_UNROLL = 8


def _moe_tiles(t):
    return (2 * t) // MOE_TM + N_EXPERTS


def _route_kernel(x_ref, oa_ref, od_ref, wo_ref, g_ref, wr_ref, h_ref, xn_ref, info_ref):
    h = _outproj(x_ref, oa_ref, od_ref, wo_ref)
    h_ref[...] = h
    xn = _rmsnorm(h, g_ref[...])
    xn_ref[...] = xn
    lane, i1, i2, g1, g2 = _route(xn.astype(BF16), wr_ref[...])
    info = jnp.where(lane == INFO_G1, g1, 0.0) + jnp.where(lane == INFO_G2, g2, 0.0)
    info = info + jnp.where(lane == INFO_E1, i1, 0.0) + jnp.where(lane == INFO_E2, i2, 0.0)
    info_ref[...] = info


def _route_sparse(x, oa, od, wo, g, wr):
    t = x.shape[0]
    tm = 256
    row = lambda n: pl.BlockSpec((tm, n), lambda i: (i, 0))
    full = lambda a: pl.BlockSpec(a.shape, lambda i: (0,) * a.ndim)
    return pl.pallas_call(
        _route_kernel,
        grid=(t // tm,),
        in_specs=[row(D_MODEL), row(ATT_WIDTH), row(DN_WIDTH), full(wo), full(g), full(wr)],
        out_specs=[row(D_MODEL), row(D_MODEL), row(LANES)],
        out_shape=[jax.ShapeDtypeStruct((t, D_MODEL), F32), jax.ShapeDtypeStruct((t, D_MODEL), F32),
                   jax.ShapeDtypeStruct((t, LANES), F32)],
        compiler_params=_params("parallel"),
        name="route",
    )(x, oa, od, wo, g, wr)


def _positions_kernel(info_ref, ltri_ref, utri_ref, pos_ref, cnt_ref, run_scr, off_scr):
    phase = pl.program_id(0)
    i = pl.program_id(1)
    info = info_ref[...]
    lane = lax.broadcasted_iota(jnp.int32, info.shape, 1).astype(F32)
    hit1 = lane == info[:, INFO_E1:INFO_E1 + 1]
    hit2 = lane == info[:, INFO_E2:INFO_E2 + 1]
    onehot = jnp.logical_or(hit1, hit2).astype(F32)

    @pl.when(jnp.logical_and(phase == 0, i == 0))
    def _():
        run_scr[...] = jnp.zeros(run_scr.shape, F32)

    @pl.when(jnp.logical_and(phase == 1, i == 0))
    def _():
        cnt = run_scr[...]
        cnt_ref[...] = cnt
        tiles = jnp.floor((cnt + (MOE_TM - 1)) * (1.0 / MOE_TM))
        off_scr[...] = MOE_TM * jnp.dot(tiles.astype(BF16), utri_ref[...], preferred_element_type=F32)
        run_scr[...] = jnp.zeros(run_scr.shape, F32)

    @pl.when(phase == 0)
    def _():
        pos_ref[...] = jnp.zeros(pos_ref.shape, jnp.int32)

    @pl.when(phase == 1)
    def _():
        before = (jnp.dot(ltri_ref[...], onehot.astype(BF16), preferred_element_type=F32)
                  + run_scr[...] + off_scr[...])
        pos1 = jnp.sum(jnp.where(hit1, before, 0.0), axis=-1, keepdims=True)
        pos2 = jnp.sum(jnp.where(hit2, before, 0.0), axis=-1, keepdims=True)
        pos_ref[...] = (jnp.where(lane == 0, pos1, 0.0) + jnp.where(lane == 1, pos2, 0.0)).astype(jnp.int32)

    run_scr[...] += jnp.sum(onehot, axis=0, keepdims=True)


def _positions(info):
    t = info.shape[0]
    tm = min(t, POS_TM)
    tok = np.arange(tm)
    ltri = jnp.asarray((tok[:, None] > tok[None, :]).astype(np.float32), dtype=BF16)
    ln = np.arange(LANES)
    utri = jnp.asarray((ln[:, None] < ln[None, :]).astype(np.float32), dtype=BF16)
    full = lambda a: pl.BlockSpec(a.shape, lambda ph, i: (0,) * a.ndim)
    return pl.pallas_call(
        _positions_kernel,
        grid=(2, t // tm),
        in_specs=[pl.BlockSpec((tm, LANES), lambda ph, i: (i, 0)), full(ltri), full(utri)],
        out_specs=[pl.BlockSpec((tm, LANES), lambda ph, i: (i * ph, 0)),
                   pl.BlockSpec((1, LANES), lambda ph, i: (0, 0))],
        out_shape=[jax.ShapeDtypeStruct((t, LANES), jnp.int32), jax.ShapeDtypeStruct((1, LANES), F32)],
        scratch_shapes=[pltpu.VMEM((1, LANES), F32), pltpu.VMEM((1, LANES), F32)],
        compiler_params=_params("arbitrary", "arbitrary"),
        name="positions",
    )(info, ltri, utri)


def _row_copy(src_hbm, src_row, dst_hbm, dst_row, sem):
    return pltpu.make_async_copy(src_hbm.at[pl.ds(src_row, 1)], dst_hbm.at[pl.ds(dst_row, 1)], sem)


SCATTER_SLOTS = 3


def _scatter_kernel(pos1_ref, pos2_ref, last_ref, used_ref, nt_ref, xn_hbm, zero_hbm, xs_hbm,
                    buf, lsem, sem, zsem, *, n_tok):
    max_tiles = xs_hbm.shape[0] // MOE_TM

    def zero_tile(tile):
        return pltpu.make_async_copy(zero_hbm, xs_hbm.at[pl.ds(tile * MOE_TM, MOE_TM)], zsem)

    def for_unused(fn):
        def body(tile, carry):
            fn(tile)
            return carry
        lax.fori_loop(nt_ref[0], max_tiles, body, 0)

    for e in range(N_EXPERTS):
        @pl.when(used_ref[e] > 0)
        def _():
            zero_tile(last_ref[e]).start()
    for_unused(lambda tile: zero_tile(tile).start())
    for e in range(N_EXPERTS):
        @pl.when(used_ref[e] > 0)
        def _():
            zero_tile(last_ref[e]).wait()
    for_unused(lambda tile: zero_tile(tile).wait())

    tm = buf.shape[1]
    n = n_tok // tm

    def load(i):
        return pltpu.make_async_copy(xn_hbm.at[pl.ds(i * tm, tm)], buf.at[i % SCATTER_SLOTS],
                                     lsem.at[i % SCATTER_SLOTS])

    def wait_rows(slot):
        pltpu.make_async_copy(xs_hbm.at[pl.ds(0, 2 * tm)], xs_hbm.at[pl.ds(0, 2 * tm)], sem.at[slot]).wait()

    load(0).start()
    load(1).start()

    def step(i, carry):
        slot = i % SCATTER_SLOTS
        load(i).wait()

        def body(j, c2):
            tok = i * tm + j
            src = buf.at[slot, pl.ds(j, 1)]
            pltpu.make_async_copy(src, xs_hbm.at[pl.ds(pos1_ref[tok], 1)], sem.at[slot]).start()
            pltpu.make_async_copy(src, xs_hbm.at[pl.ds(pos2_ref[tok], 1)], sem.at[slot]).start()
            return c2
        lax.fori_loop(0, tm, body, 0, unroll=DMA_UNROLL)

        @pl.when(i >= 1)
        def _():
            wait_rows((i - 1) % SCATTER_SLOTS)

        @pl.when(i + 2 < n)
        def _():
            load(i + 2).start()
        return carry
    lax.fori_loop(0, n, step, 0)
    wait_rows((n - 1) % SCATTER_SLOTS)


def _scatter_rows(xn, pos1, pos2, last_tile, used, n_tiles, n_rows):
    t = xn.shape[0]
    zero = jnp.zeros((MOE_TM, D_MODEL), F32)
    any_spec = pl.BlockSpec(memory_space=pl.ANY)
    return pl.pallas_call(
        functools.partial(_scatter_kernel, n_tok=t),
        grid_spec=pltpu.PrefetchScalarGridSpec(
            num_scalar_prefetch=5, grid=(1,),
            in_specs=[any_spec, any_spec], out_specs=any_spec,
            scratch_shapes=[pltpu.VMEM((SCATTER_SLOTS, MOE_TM, D_MODEL), F32),
                            pltpu.SemaphoreType.DMA((SCATTER_SLOTS,)),
                            pltpu.SemaphoreType.DMA((SCATTER_SLOTS,)),
                            pltpu.SemaphoreType.DMA]),
        out_shape=jax.ShapeDtypeStruct((n_rows, D_MODEL), F32),
        compiler_params=_params("arbitrary"),
        name="scatter_rows",
    )(pos1, pos2, last_tile, used, n_tiles, xn, zero)


def _experts_kernel(te_ref, nt_ref, xs_ref, wg_ref, wu_ref, wd_ref, ys_ref):
    used = pl.program_id(0) < nt_ref[0]

    @pl.when(used)
    def _():
        x = xs_ref[...].astype(BF16)
        hg = jnp.dot(x, wg_ref[...], preferred_element_type=F32)
        hu = jnp.dot(x, wu_ref[...], preferred_element_type=F32)
        hm = (_silu(hg) * hu).astype(BF16)
        ys_ref[...] = jnp.dot(hm, wd_ref[...], preferred_element_type=F32)

    @pl.when(jnp.logical_not(used))
    def _():
        ys_ref[...] = jnp.zeros(ys_ref.shape, F32)


def _experts(xs, tile_expert, n_tiles, wg, wu, wd):
    max_tiles = xs.shape[0] // MOE_TM
    rows = pl.BlockSpec((MOE_TM, D_MODEL), lambda i, te, nt: (i, 0))
    return pl.pallas_call(
        _experts_kernel,
        grid_spec=pltpu.PrefetchScalarGridSpec(
            num_scalar_prefetch=2, grid=(max_tiles,),
            in_specs=[rows,
                      pl.BlockSpec((None, D_MODEL, D_EXPERT), lambda i, te, nt: (te[i], 0, 0)),
                      pl.BlockSpec((None, D_MODEL, D_EXPERT), lambda i, te, nt: (te[i], 0, 0)),
                      pl.BlockSpec((None, D_EXPERT, D_MODEL), lambda i, te, nt: (te[i], 0, 0))],
            out_specs=rows),
        out_shape=jax.ShapeDtypeStruct(xs.shape, F32),
        compiler_params=_params("arbitrary"),
        name="experts",
    )(tile_expert, n_tiles, xs, wg, wu, wd)


def _ple_gather_kernel(pos1_ref, pos2_ref, h_ref, info_ref, p_ref, wpp_ref, wpg_ref, gp_ref, gf_ref,
                       ys_hbm, y_ref, ybuf, sem):
    i = pl.program_id(0)
    n = pl.num_programs(0)
    tm = h_ref.shape[0]

    def issue(tile, slot):
        def body(j, carry):
            tok = tile * tm + j
            pltpu.make_async_copy(ys_hbm.at[pl.ds(pos1_ref[tok], 1)], ybuf.at[slot, 0, pl.ds(j, 1)],
                                  sem.at[slot]).start()
            pltpu.make_async_copy(ys_hbm.at[pl.ds(pos2_ref[tok], 1)], ybuf.at[slot, 1, pl.ds(j, 1)],
                                  sem.at[slot]).start()
            return carry
        lax.fori_loop(0, tm, body, 0, unroll=DMA_UNROLL)

    @pl.when(i == 0)
    def _():
        issue(0, 0)

    @pl.when(i + 1 < n)
    def _():
        issue(i + 1, (i + 1) % 2)

    slot = i % 2
    pltpu.make_async_copy(ybuf.at[slot], ybuf.at[slot], sem.at[slot]).wait()
    info = info_ref[...]
    moe = info[:, INFO_G1:INFO_G1 + 1] * ybuf[slot, 0] + info[:, INFO_G2:INFO_G2 + 1] * ybuf[slot, 1]
    h = h_ref[...] + moe
    hn = _rmsnorm(h, gp_ref[...])
    h = h + _mm(p_ref[...], wpp_ref[...]) * _sigmoid(_mm(hn, wpg_ref[...]))
    y_ref[...] = _rmsnorm(h, gf_ref[...])


def _ple_gather(h, info, p, ys, pos1, pos2, wpp, wpg, gp, gf):
    t = h.shape[0]
    tm = 256
    row = lambda n: pl.BlockSpec((tm, n), lambda i, p1, p2: (i, 0))
    full = lambda a: pl.BlockSpec(a.shape, lambda i, p1, p2: (0,) * a.ndim)
    return pl.pallas_call(
        _ple_gather_kernel,
        grid_spec=pltpu.PrefetchScalarGridSpec(
            num_scalar_prefetch=2, grid=(t // tm,),
            in_specs=[row(D_MODEL), row(LANES), row(PLE_DIM), full(wpp), full(wpg), full(gp), full(gf),
                      pl.BlockSpec(memory_space=pl.ANY)],
            out_specs=row(D_MODEL),
            scratch_shapes=[pltpu.VMEM((2, 2, tm, D_MODEL), F32), pltpu.SemaphoreType.DMA((2,))]),
        out_shape=jax.ShapeDtypeStruct((t, D_MODEL), F32),
        compiler_params=_params("arbitrary"),
        name="ple_gather",
    )(pos1, pos2, h, info, p, wpp, wpg, gp, gf, ys)


def _tile_tables(cnt, max_tiles):
    tiles_e = (cnt + (MOE_TM - 1)) // MOE_TM
    ends = jnp.cumsum(tiles_e)
    n_tiles = ends[-1]
    idx = jnp.minimum(jnp.arange(max_tiles, dtype=jnp.int32), n_tiles - 1)
    tile_expert = jnp.sum((idx[:, None] >= ends[None, :]).astype(jnp.int32), axis=1)
    return tile_expert, n_tiles.reshape(1), (ends - 1).astype(jnp.int32), tiles_e.astype(jnp.int32)


def _ple_final_kernel(h_ref, m_ref, p_ref, wpp_ref, wpg_ref, gp_ref, gf_ref, y_ref):
    h = h_ref[...] + m_ref[...]
    hn = _rmsnorm(h, gp_ref[...])
    h = h + _mm(p_ref[...], wpp_ref[...]) * _sigmoid(_mm(hn, wpg_ref[...]))
    y_ref[...] = _rmsnorm(h, gf_ref[...])


def _ple_final(h, m, p, wpp, wpg, gp, gf):
    t = h.shape[0]
    tm = min(t, 256)
    row = lambda n: pl.BlockSpec((tm, n), lambda i: (i, 0))
    full = lambda a: pl.BlockSpec(a.shape, lambda i: (0,) * a.ndim)
    return pl.pallas_call(
        _ple_final_kernel,
        grid=(t // tm,),
        in_specs=[row(D_MODEL), row(D_MODEL), row(PLE_DIM), full(wpp), full(wpg), full(gp), full(gf)],
        out_specs=row(D_MODEL),
        out_shape=jax.ShapeDtypeStruct((t, D_MODEL), F32),
        compiler_params=_params("parallel"),
        name="ple_final",
    )(h, m, p, wpp, wpg, gp, gf)


def kernel(x_prompt, x_sample, p_prompt, p_sample, cache_k, cache_v, state_conv, state_S, rel_bias, norm_mix, w_in, att_sink, conv_w, dn_A_log, dn_dt_bias, dn_norm, w_out, norm_ffn, w_router_group, w_router_expert, w_gate, w_up, w_down, w_ple_proj, w_ple_gate, norm_ple, norm_final):
    batch, seq, _ = x_prompt.shape
    nseq = x_sample.shape[0]
    assert x_sample.shape[1] == 1 and norm_mix.shape[0] == 1 and cache_k.shape[2] == WINDOW
    assert seq % GDN_TB == 0 and seq % ATT_BLOCK == 0

    wi = w_in[0]
    o_db = ATT_COLS + CONV_CH
    w_in_re = jnp.concatenate(
        [wi[:, :o_db], wi[:, o_db + 2 * DN_HEADS:], wi[:, o_db:o_db + 2 * DN_HEADS],
         jnp.zeros((D_MODEL, LANES - 2 * DN_HEADS), F32)], axis=1).astype(BF16)
    row = lambda a: a.reshape(1, -1).astype(F32)
    pad_lanes = lambda a, off: jnp.zeros((1, LANES), F32).at[0, off:off + a.shape[0]].set(a)
    alog = pad_lanes(dn_A_log[0], DN_HEADS)
    dtb = pad_lanes(dn_dt_bias[0], DN_HEADS)
    dnx = jnp.tile(dn_norm[0], DN_HEADS).reshape(1, DN_WIDTH)
    w_router = jnp.concatenate(
        [w_router_group[0], w_router_expert[0],
         jnp.zeros((D_MODEL, LANES - N_GROUPS - N_EXPERTS), F32)], axis=1).astype(BF16)
    wo = w_out[0].astype(BF16)
    wg, wu, wd = w_gate[0].astype(BF16), w_up[0].astype(BF16), w_down[0].astype(BF16)
    wpp, wpg = w_ple_proj[0].astype(BF16), w_ple_gate[0].astype(BF16)
    sink = att_sink[0]

    qi = np.arange(ATT_BLOCK)[:, None]
    kj = np.arange(2 * ATT_BLOCK)[None, :]
    bucket_p = jnp.asarray(_t5_bucket_np(qi + ATT_BLOCK - kj))
    bucket_s = jnp.asarray(_t5_bucket_np(WINDOW - np.arange(WINDOW)[None, :]))

    def tail(x, o_att, o_dn, p):
        h1, xn2, gates = _outproj_router(x, o_att, o_dn, wo, row(norm_ffn[0]), w_router)
        moe = _moe(xn2, gates, wg, wu, wd)
        return _ple_final(h1, moe, p, wpp, wpg, row(norm_ple[0]), row(norm_final))

    xp = x_prompt.reshape(batch * seq, D_MODEL)
    att_p, xc_p, dz_p, ba_p = _inproj(xp, row(norm_mix[0]), w_in_re)
    o_att_p = _attn_prompt(att_p, bucket_p, rel_bias, sink, batch, seq)
    o_dn_p, s_p = _gdn_prompt(xc_p, dz_p, ba_p, conv_w[0], alog, dtb, dnx, batch, seq)
    h1, xn2, info = _route_sparse(xp, o_att_p, o_dn_p, wo, row(norm_ffn[0]), w_router)
    pos, cnt = _positions(info)
    pos1, pos2 = pos[:, 0], pos[:, 1]
    max_tiles = _moe_tiles(batch * seq)
    cnt_e = cnt[0, ROUTER_OFF:ROUTER_OFF + N_EXPERTS].astype(jnp.int32)
    tile_expert, n_tiles, last_tile, used = _tile_tables(cnt_e, max_tiles)
    xs = _scatter_rows(xn2, pos1, pos2, last_tile, used, n_tiles, max_tiles * MOE_TM)
    ys = _experts(xs, tile_expert, n_tiles, wg, wu, wd)
    y_p = _ple_gather(h1, info, p_prompt[0].reshape(batch * seq, PLE_DIM), ys, pos1, pos2,
                      wpp, wpg, row(norm_ple[0]), row(norm_final))

    xs = x_sample.reshape(nseq, D_MODEL)
    att_s, xc_s, dz_s, ba_s = _inproj(xs, row(norm_mix[0]), w_in_re)
    ck = cache_k[0].reshape(nseq, WINDOW, KV_WIDTH)
    cv = cache_v[0].reshape(nseq, WINDOW, KV_WIDTH)
    o_att_s = _attn_sample(att_s, ck, cv, bucket_s, rel_bias, sink)
    sconv_t = jnp.swapaxes(state_conv[0], 0, 1)
    o_dn_s, s_s = _gdn_sample(xc_s, dz_s, ba_s, sconv_t,
                              state_S[0].reshape(nseq, DN_HEADS * DN_DK, DN_DV), conv_w[0], alog, dtb,
                              dn_norm[0].reshape(1, DN_DV))
    s_s = s_s.reshape(nseq, DN_HEADS, DN_DK, DN_DV)
    y_s = tail(xs, o_att_s, o_dn_s.reshape(nseq, DN_WIDTH), p_sample[0].reshape(nseq, PLE_DIM))

    att_p3 = att_p.reshape(batch, seq, ATT_COLS)
    kv_shape = (1, batch, WINDOW, ATT_KV_HEADS, HEAD_DIM)
    k_p = att_p3[:, seq - WINDOW:, ATT_WIDTH:ATT_WIDTH + KV_WIDTH].reshape(kv_shape)
    v_p = att_p3[:, seq - WINDOW:, ATT_WIDTH + KV_WIDTH:].reshape(kv_shape)
    conv_p = xc_p.reshape(batch, seq, CONV_CH)[:, seq - (CONV_WIDTH - 1):][None]
    k_new = att_s[:, None, ATT_WIDTH:ATT_WIDTH + KV_WIDTH]
    v_new = att_s[:, None, ATT_WIDTH + KV_WIDTH:]
    kv_s_shape = (1, nseq, WINDOW, ATT_KV_HEADS, HEAD_DIM)
    k_s = jnp.concatenate([ck[:, 1:], k_new], axis=1).reshape(kv_s_shape)
    v_s = jnp.concatenate([cv[:, 1:], v_new], axis=1).reshape(kv_s_shape)
    conv_s = jnp.concatenate([state_conv[0][:, 1:], xc_s[:, None, :]], axis=1)[None]
    return (y_p.reshape(batch, seq, D_MODEL), y_s.reshape(nseq, 1, D_MODEL),
            k_p, v_p, conv_p, s_p[None], k_s, v_s, conv_s, s_s[None])
```

```python
import functools
import math

import numpy as np
import jax
import jax.numpy as jnp
from jax import lax
from jax.experimental import pallas as pl
from jax.experimental.pallas import tpu as pltpu
from jax.experimental.pallas import tpu_sc as plsc

F32 = jnp.float32
BF16 = jnp.bfloat16

D_MODEL = 1024
ATT_HEADS = 8
ATT_KV_HEADS = 2
HEAD_DIM = 64
GQA = ATT_HEADS // ATT_KV_HEADS
WINDOW = 128
ATT_BLOCK = 128
N_BUCKETS = 32
DN_HEADS = 8
DN_DK = 64
DN_DV = 64
CONV_WIDTH = 4
DN_CHUNK = 64
ATT_WIDTH = ATT_HEADS * HEAD_DIM
KV_WIDTH = ATT_KV_HEADS * HEAD_DIM
DN_WIDTH = DN_HEADS * DN_DV
CONV_CH = 3 * DN_WIDTH
N_GROUPS = 4
EXPERTS_PER_GROUP = 8
N_EXPERTS = N_GROUPS * EXPERTS_PER_GROUP
D_EXPERT = 256
PLE_DIM = 256
EPS = 1e-6
NEG_INF = float("-inf")

ATT_COLS = ATT_WIDTH + 2 * KV_WIDTH
LANES = 128
IN_COLS = ATT_COLS + CONV_CH + DN_WIDTH + LANES
ROUTER_OFF = N_GROUPS
VMEM_LIMIT = 48 * 1024 * 1024


def _params(*sem):
    return pltpu.CompilerParams(dimension_semantics=sem, vmem_limit_bytes=VMEM_LIMIT)


def _mm(a, b):
    return jnp.dot(a.astype(BF16), b.astype(BF16), preferred_element_type=F32)


def _mm_nt(a, b):
    return lax.dot_general(a.astype(BF16), b.astype(BF16), (((1,), (1,)), ((), ())),
                           preferred_element_type=F32)


def _mm_tn(a, b):
    return lax.dot_general(a.astype(BF16), b.astype(BF16), (((0,), (0,)), ((), ())),
                           preferred_element_type=F32)


def _split3(x):
    h1 = x.astype(BF16)
    r1 = x - h1.astype(F32)
    h2 = r1.astype(BF16)
    h3 = (r1 - h2.astype(F32)).astype(BF16)
    return h1, h2, h3


def _mm_sel_rhs(x, sel):
    h1, h2, h3 = _split3(x)
    d = lambda h: jnp.dot(h, sel, preferred_element_type=F32)
    return d(h1) + d(h2) + d(h3)


def _mm_sel_lhs(sel, x):
    h1, h2, h3 = _split3(x)
    d = lambda h: jnp.dot(sel, h, preferred_element_type=F32)
    return d(h1) + d(h2) + d(h3)


def _mm3(a, b):
    ah = a.astype(BF16)
    al = (a - ah.astype(F32)).astype(BF16)
    bh = b.astype(BF16)
    bl = (b - bh.astype(F32)).astype(BF16)
    d = lambda u, v: jnp.dot(u, v, preferred_element_type=F32)
    return d(ah, bh) + d(ah, bl) + d(al, bh)


def _sigmoid(x):
    return 1.0 / (1.0 + jnp.exp(-x))


def _silu(x):
    return x * _sigmoid(x)


def _softplus(x):
    return jnp.maximum(x, 0.0) + jnp.log1p(jnp.exp(-jnp.abs(x)))


def _rmsnorm(x, g):
    return x * lax.rsqrt(jnp.mean(x * x, axis=-1, keepdims=True) + EPS) * g


def _t5_bucket_np(dist):
    max_exact = N_BUCKETS // 2
    d = np.maximum(dist, 0)
    ratio = (np.log(np.maximum(d, 1).astype(np.float32) / np.float32(max_exact))
             / np.float32(math.log(WINDOW / max_exact))).astype(np.float32)
    large = np.minimum(max_exact + (ratio * np.float32(N_BUCKETS - max_exact)).astype(np.int32),
                       N_BUCKETS - 1)
    return np.where(d < max_exact, d, large).astype(np.int32)


def _bias_lookup(bucket, rb_ref, h):
    acc = jnp.zeros(bucket.shape, F32)
    for t in range(N_BUCKETS):
        acc = jnp.where(bucket == t, rb_ref[t, h], acc)
    return acc


def _inproj_kernel(x_ref, g_ref, w_ref, att_ref, xc_ref, dz_ref, ba_ref):
    xn = _rmsnorm(x_ref[...], g_ref[...]).astype(BF16)
    o0, o1, o2 = ATT_COLS, ATT_COLS + CONV_CH, ATT_COLS + CONV_CH + DN_WIDTH
    att_ref[...] = jnp.dot(xn, w_ref[:, :o0], preferred_element_type=F32)
    xc_ref[...] = jnp.dot(xn, w_ref[:, o0:o1], preferred_element_type=F32)
    dz_ref[...] = jnp.dot(xn, w_ref[:, o1:o2], preferred_element_type=F32)
    ba_ref[...] = jnp.dot(xn, w_ref[:, o2:], preferred_element_type=F32)


def _inproj(x, g, w):
    t = x.shape[0]
    tm = min(t, 256)
    row = lambda n: pl.BlockSpec((tm, n), lambda i: (i, 0))
    full = lambda a: pl.BlockSpec(a.shape, lambda i: (0,) * a.ndim)
    return pl.pallas_call(
        _inproj_kernel,
        grid=(t // tm,),
        in_specs=[row(D_MODEL), full(g), full(w)],
        out_specs=[row(ATT_COLS), row(CONV_CH), row(DN_WIDTH), row(LANES)],
        out_shape=[jax.ShapeDtypeStruct((t, n), F32) for n in (ATT_COLS, CONV_CH, DN_WIDTH, LANES)],
        compiler_params=_params("parallel"),
        name="inproj",
    )(x, g, w)


def _attn_prompt_kernel(cur_ref, prev_ref, bucket_ref, rb_ref, sink_ref, o_ref, bias_scr):
    first = jnp.logical_and(pl.program_id(0) == 0, pl.program_id(1) == 0)
    qi = lax.broadcasted_iota(jnp.int32, (ATT_BLOCK, 2 * ATT_BLOCK), 0)
    kj = lax.broadcasted_iota(jnp.int32, (ATT_BLOCK, 2 * ATT_BLOCK), 1)

    @pl.when(first)
    def _():
        dist = qi + ATT_BLOCK - kj
        band = jnp.logical_and(dist >= 0, dist < WINDOW)
        bucket = bucket_ref[...]
        for h in range(ATT_HEADS):
            bias_scr[h] = jnp.where(band, _bias_lookup(bucket, rb_ref, h), NEG_INF)

    cur = cur_ref[...]
    prev = prev_ref[...]
    q = cur[:, :ATT_WIDTH] * (HEAD_DIM ** -0.5)
    kcat = jnp.concatenate([prev[:, ATT_WIDTH:ATT_WIDTH + KV_WIDTH],
                            cur[:, ATT_WIDTH:ATT_WIDTH + KV_WIDTH]], axis=0)
    vcat = jnp.concatenate([prev[:, ATT_WIDTH + KV_WIDTH:], cur[:, ATT_WIDTH + KV_WIDTH:]], axis=0)
    keep = jnp.logical_or(pl.program_id(1) > 0, kj >= ATT_BLOCK)
    outs = []
    for h in range(ATT_HEADS):
        g = h // GQA
        qh = q[:, h * HEAD_DIM:(h + 1) * HEAD_DIM]
        kh = kcat[:, g * HEAD_DIM:(g + 1) * HEAD_DIM]
        vh = vcat[:, g * HEAD_DIM:(g + 1) * HEAD_DIM]
        s = jnp.where(keep, _mm_nt(qh, kh) + bias_scr[h], NEG_INF)
        sink = sink_ref[h]
        m = jnp.maximum(jnp.max(s, axis=-1, keepdims=True), sink)
        p = jnp.exp(s - m)
        den = jnp.sum(p, axis=-1, keepdims=True) + jnp.exp(sink - m)
        outs.append(_mm(p, vh) / den)
    o_ref[...] = jnp.concatenate(outs, axis=1)


def _attn_prompt(att, bucket, rel_bias, sink, batch, seq):
    nb = seq // ATT_BLOCK
    smem = pl.BlockSpec(memory_space=pltpu.SMEM)
    return pl.pallas_call(
        _attn_prompt_kernel,
        grid=(batch, nb),
        in_specs=[
            pl.BlockSpec((ATT_BLOCK, ATT_COLS), lambda b, i: (b * nb + i, 0)),
            pl.BlockSpec((ATT_BLOCK, ATT_COLS), lambda b, i: (b * nb + jnp.maximum(i - 1, 0), 0)),
            pl.BlockSpec(bucket.shape, lambda b, i: (0, 0)),
            smem, smem,
        ],
        out_specs=pl.BlockSpec((ATT_BLOCK, ATT_WIDTH), lambda b, i: (b * nb + i, 0)),
        out_shape=jax.ShapeDtypeStruct((batch * seq, ATT_WIDTH), F32),
        scratch_shapes=[pltpu.VMEM((ATT_HEADS, ATT_BLOCK, 2 * ATT_BLOCK), F32)],
        compiler_params=_params("arbitrary", "arbitrary"),
        name="attn_prompt",
    )(att, att, bucket, rel_bias, sink)


ATT_S_BB = 8


def _attn_sample_kernel(att_ref, ck_ref, cv_ref, bucket_ref, rb_ref, sink_ref, o_ref,
                        bias_scr, col_scr):
    hrow = lax.broadcasted_iota(jnp.int32, (ATT_HEADS, LANES), 0)
    lane = lax.broadcasted_iota(jnp.int32, (ATT_HEADS, LANES), 1)

    @pl.when(pl.program_id(0) == 0)
    def _():
        bucket = jnp.broadcast_to(bucket_ref[...], (ATT_HEADS, LANES))
        bias = jnp.zeros((ATT_HEADS, LANES), F32)
        cols = jnp.zeros((ATT_HEADS, LANES), F32)
        for h in range(ATT_HEADS):
            bias = jnp.where(hrow == h, _bias_lookup(bucket, rb_ref, h), bias)
            cols = jnp.where(jnp.logical_and(hrow == h, lane == 0), sink_ref[h], cols)
            cols = jnp.where(jnp.logical_and(hrow == h, lane == 1), rb_ref[0, h], cols)
        bias_scr[...] = jnp.where(lane >= 1, bias, NEG_INF)
        col_scr[...] = cols

    bias_c = bias_scr[...]
    sink = col_scr[:, 0:1]
    bias_n = col_scr[:, 1:2]
    same_group = (hrow // GQA) == (lane // HEAD_DIM)
    low_group = lax.broadcasted_iota(jnp.int32, (ATT_HEADS, HEAD_DIM), 0) < GQA
    for b in range(ATT_S_BB):
        row = att_ref[b:b + 1, :]
        q = row[:, :ATT_WIDTH] * (HEAD_DIM ** -0.5)
        kn = row[:, ATT_WIDTH:ATT_WIDTH + KV_WIDTH]
        vn = row[:, ATT_WIDTH + KV_WIDTH:]
        qh = jnp.concatenate([q[:, h * HEAD_DIM:(h + 1) * HEAD_DIM] for h in range(ATT_HEADS)], axis=0)
        q_bd = jnp.where(same_group, jnp.concatenate([qh, qh], axis=1), 0.0)
        rnd = lambda a: a.astype(BF16).astype(F32)
        s_c = _mm_nt(q_bd, ck_ref[b]) + bias_c
        s_n = jnp.sum(rnd(q_bd) * rnd(kn), axis=-1, keepdims=True) + bias_n
        m = jnp.maximum(jnp.maximum(jnp.max(s_c, axis=-1, keepdims=True), s_n), sink)
        p_c = jnp.exp(s_c - m)
        p_n = jnp.exp(s_n - m)
        den = jnp.sum(p_c, axis=-1, keepdims=True) + p_n + jnp.exp(sink - m)
        o_full = _mm(p_c / den, cv_ref[b]) + rnd(p_n / den) * rnd(vn)
        o_sel = jnp.where(low_group, o_full[:, :HEAD_DIM], o_full[:, HEAD_DIM:])
        o_ref[b:b + 1, :] = jnp.concatenate([o_sel[h:h + 1, :] for h in range(ATT_HEADS)], axis=1)


def _attn_sample(att, ck, cv, bucket, rel_bias, sink):
    nseq = att.shape[0]
    smem = pl.BlockSpec(memory_space=pltpu.SMEM)
    cache = pl.BlockSpec((ATT_S_BB, WINDOW, KV_WIDTH), lambda i: (i, 0, 0))
    return pl.pallas_call(
        _attn_sample_kernel,
        grid=(nseq // ATT_S_BB,),
        in_specs=[pl.BlockSpec((ATT_S_BB, ATT_COLS), lambda i: (i, 0)), cache, cache,
                  pl.BlockSpec(bucket.shape, lambda i: (0, 0)), smem, smem],
        out_specs=pl.BlockSpec((ATT_S_BB, ATT_WIDTH), lambda i: (i, 0)),
        out_shape=jax.ShapeDtypeStruct((nseq, ATT_WIDTH), F32),
        scratch_shapes=[pltpu.VMEM((ATT_HEADS, LANES), F32), pltpu.VMEM((ATT_HEADS, LANES), F32)],
        compiler_params=_params("arbitrary"),
        name="attn_sample",
    )(att, ck, cv, bucket, rel_bias, sink)


GDN_TB = 128
GDN_NC = GDN_TB // DN_CHUNK
TAIL = 8


def _gdn_gates(ba, alog, dtb):
    beta = _sigmoid(ba)
    g = -jnp.exp(alog) * _softplus(ba + dtb)
    return beta, g


PAIR = 2 * DN_DK
N_PAIRS = DN_WIDTH // PAIR


def _pair_diag(x, lo):
    xb = x.astype(BF16)
    zero = jnp.zeros_like(xb)
    return jnp.concatenate([jnp.where(lo, xb, zero), jnp.where(lo, zero, xb)], axis=0)


def _gdn_prompt_kernel(xc_ref, dz_ref, ba_ref, cw_ref, alog_ref, dtb_ref, dnx_ref,
                       hsum_ref, expb_ref, expg_ref, ltri_ref,
                       o_ref, s_out_ref, xp_scr, s_scr):
    i = pl.program_id(0)
    nb = xc_ref.shape[0]

    @pl.when(i == 0)
    def _():
        xp_scr[:, 0:TAIL, :] = jnp.zeros((nb, TAIL, CONV_CH), F32)
        s_scr[...] = jnp.zeros(s_scr.shape, F32)

    hsum = hsum_ref[...]
    ri = lax.broadcasted_iota(jnp.int32, (DN_CHUNK, PAIR), 0)
    ci = lax.broadcasted_iota(jnp.int32, (DN_CHUNK, PAIR), 1)
    lo = ci < DN_DK
    cj = jnp.where(lo, ci, ci - DN_DK)
    causal = ri >= cj
    strict = ri > cj
    eye = (ri == cj).astype(F32)

    def sel2(x, m):
        hi = x.astype(BF16)
        lw = (x - hi.astype(F32)).astype(BF16)
        return (jnp.dot(hi, m, preferred_element_type=F32) + jnp.dot(lw, m, preferred_element_type=F32))

    pre = []
    for b in range(nb):
        xc = xc_ref[b]
        xp_scr[b, TAIL:, :] = xc
        y = xp_scr[b, TAIL - 3:TAIL - 3 + GDN_TB, :] * cw_ref[0:1, :]
        y = y + xp_scr[b, TAIL - 2:TAIL - 2 + GDN_TB, :] * cw_ref[1:2, :]
        y = y + xp_scr[b, TAIL - 1:TAIL - 1 + GDN_TB, :] * cw_ref[2:3, :]
        y = y + xc * cw_ref[3:4, :]
        xp_scr[b, 0:TAIL, :] = xc[GDN_TB - TAIL:, :]
        y = _silu(y)
        q = y[:, :DN_WIDTH]
        k = y[:, DN_WIDTH:2 * DN_WIDTH]
        v = y[:, 2 * DN_WIDTH:]
        q = q * lax.rsqrt(sel2(q * q, hsum) + EPS) * (DN_DK ** -0.5)
        k = k * lax.rsqrt(sel2(k * k, hsum) + EPS)
        beta_c, g_c = _gdn_gates(ba_ref[b], alog_ref[...], dtb_ref[...])
        beta = sel2(beta_c, expb_ref[...])
        gam_c = _mm_sel_lhs(ltri_ref[...], g_c)
        gam = _mm_sel_rhs(gam_c, expg_ref[...])
        gam_t = gam_c.T
        kb = k * beta
        egam = jnp.exp(gam)
        pre.append(dict(q=q, k=k, kb=kb, vb=v * beta, qg=q * egam, wr=kb * egam, gam=gam, gam_t=gam_t))

    probs = [(b, p) for b in range(nb) for p in range(N_PAIRS)]
    pick = lambda m: jnp.where(lo, m[:DN_DK], m[DN_DK:])
    o_rows = [[] for _ in range(nb)]
    for c in range(GDN_NC):
        r0, r1 = c * DN_CHUNK, (c + 1) * DN_CHUNK
        sl = lambda name, b, p: pre[b][name][r0:r1, p * PAIR:(p + 1) * PAIR]
        raws = []
        for b, p in probs:
            k_p = sl("k", b, p)
            k_rows = jnp.concatenate([jnp.where(lo, k_p, 0.0), jnp.where(lo, 0.0, k_p)], axis=0)
            raws.append(_mm_nt(jnp.concatenate([sl("kb", b, p), sl("q", b, p)], axis=0), k_rows))
        pws, ts, qks = [], [], []
        for (b, p), raw in zip(probs, raws):
            gcol = sl("gam", b, p)
            h0 = DN_HEADS + 2 * p
            gam_t = pre[b]["gam_t"]
            grow = jnp.concatenate([gam_t[h0:h0 + 1, r0:r1], gam_t[h0 + 1:h0 + 2, r0:r1]], axis=1)
            decay = jnp.exp(jnp.where(causal, gcol - grow, NEG_INF))
            a = jnp.where(strict, raw[:DN_CHUNK] * decay, 0.0)
            qks.append(jnp.where(causal, raw[DN_CHUNK:] * decay, 0.0))
            pws.append(-a)
            ts.append(eye - a)
        pws = [_mm(pw, _pair_diag(pw, lo)) for pw in pws]
        for _ in range(4):
            rs = [_mm(jnp.concatenate([pw, t], axis=0), _pair_diag(pw, lo)) for pw, t in zip(pws, ts)]
            pws = [r[:DN_CHUNK] for r in rs]
            ts = [t + r[DN_CHUNK:] for t, r in zip(ts, rs)]
        rs = [_mm(t, _pair_diag(pw, lo)) for pw, t in zip(pws, ts)]
        ts = [t + r for t, r in zip(ts, rs)]
        sols = [_mm(t, jnp.concatenate([_pair_diag(sl("vb", b, p), lo), _pair_diag(sl("wr", b, p), lo)],
                                       axis=1)) for (b, p), t in zip(probs, ts)]
        qkuws = [_mm(qk, jnp.concatenate([_pair_diag(s[:, :PAIR], lo), _pair_diag(s[:, PAIR:], lo)], axis=1))
                 for qk, s in zip(qks, sols)]
        crosses, gls = [], []
        for (b, p), s in zip(probs, sols):
            gam_last = pre[b]["gam"][r1 - 1:r1, p * PAIR:(p + 1) * PAIR]
            kd = sl("k", b, p) * jnp.exp(gam_last - sl("gam", b, p))
            crosses.append(_mm_tn(kd, s))
            gls.append(jnp.exp(gam_last))
        lhs = [jnp.concatenate([pick(cr[:, PAIR:]), sl("qg", b, p) - qkuw[:, PAIR:]], axis=0)
               for (b, p), cr, qkuw in zip(probs, crosses, qkuws)]
        s_olds = [s_scr[b, p] for b, p in probs]
        rs = [_mm(l, _pair_diag(s_old, lo)) for l, s_old in zip(lhs, s_olds)]
        o_pairs = [[] for _ in range(nb)]
        for (b, p), r, s_old, gl, cr, qkuw in zip(probs, rs, s_olds, gls, crosses, qkuws):
            s_scr[b, p] = gl * s_old - r[:DN_DK] + pick(cr[:, :PAIR])
            o_pairs[b].append(r[DN_DK:] + qkuw[:, :PAIR])
        for b in range(nb):
            o_rows[b].append(jnp.concatenate(o_pairs[b], axis=1))

    for b in range(nb):
        o = jnp.concatenate(o_rows[b], axis=0)
        ms = sel2(o * o, hsum) * (1.0 / DN_DV)
        o = o * lax.rsqrt(ms + EPS) * dnx_ref[...]
        o_ref[b] = o * _silu(dz_ref[b])

    @pl.when(i == pl.num_programs(0) - 1)
    def _():
        for b in range(nb):
            for p in range(N_PAIRS):
                s_p = s_scr[b, p]
                s_out_ref[b, 2 * p] = s_p[:, :DN_DV]
                s_out_ref[b, 2 * p + 1] = s_p[:, DN_DV:]


def _gdn_consts():
    lane = np.arange(DN_WIDTH)
    hsum = (lane[:, None] // DN_DV == lane[None, :] // DN_DV)
    src = np.arange(LANES)
    expb = (src[:, None] == lane[None, :] // DN_DV)
    expg = (src[:, None] == DN_HEADS + lane[None, :] // DN_DV)
    tok = np.arange(GDN_TB)
    ltri = np.logical_and(tok[:, None] >= tok[None, :],
                          tok[:, None] // DN_CHUNK == tok[None, :] // DN_CHUNK)
    as_bf16 = lambda m: jnp.asarray(m.astype(np.float32), dtype=BF16)
    return as_bf16(hsum), as_bf16(expb), as_bf16(expg), as_bf16(ltri)


def _gdn_prompt(xc, dz, ba, conv_w, alog, dtb, dnx, batch, seq):
    nt = seq // GDN_TB
    hsum, expb, expg, ltri = _gdn_consts()
    row = lambda n: pl.BlockSpec((batch, GDN_TB, n), lambda i: (0, i, 0))
    full = lambda a: pl.BlockSpec(a.shape, lambda i: (0,) * a.ndim)
    consts = (conv_w, alog, dtb, dnx, hsum, expb, expg, ltri)
    as3d = lambda a: a.reshape(batch, seq, a.shape[-1])
    o, s = pl.pallas_call(
        _gdn_prompt_kernel,
        grid=(nt,),
        in_specs=[row(CONV_CH), row(DN_WIDTH), row(LANES)] + [full(a) for a in consts],
        out_specs=[row(DN_WIDTH),
                   pl.BlockSpec((batch, DN_HEADS, DN_DK, DN_DV), lambda i: (0, 0, 0, 0))],
        out_shape=[jax.ShapeDtypeStruct((batch, seq, DN_WIDTH), F32),
                   jax.ShapeDtypeStruct((batch, DN_HEADS, DN_DK, DN_DV), F32)],
        scratch_shapes=[pltpu.VMEM((batch, TAIL + GDN_TB, CONV_CH), F32),
                        pltpu.VMEM((batch, N_PAIRS, DN_DK, PAIR), F32)],
        compiler_params=_params("arbitrary"),
        name="gdn_prompt",
    )(as3d(xc), as3d(dz), as3d(ba), *consts)
    return o.reshape(batch * seq, DN_WIDTH), s


GDN_S_BB = 8


def _gdn_sample_kernel(xc_ref, dz_ref, ba_ref, sc_ref, s_ref, cw_ref, alog_ref, dtb_ref, dn_ref,
                       hsum_ref, eye_ref, hsel_ref, hrep_ref, o_ref, s_out_ref):
    xc = xc_ref[...]
    y = sc_ref[0] * cw_ref[0:1, :]
    y = y + sc_ref[1] * cw_ref[1:2, :]
    y = y + sc_ref[2] * cw_ref[2:3, :]
    y = _silu(y + xc * cw_ref[3:4, :])
    hsum = hsum_ref[...]
    q = y[:, :DN_WIDTH]
    k = y[:, DN_WIDTH:2 * DN_WIDTH]
    v = y[:, 2 * DN_WIDTH:]
    q = q * lax.rsqrt(_mm_sel_rhs(q * q, hsum) + EPS) * (DN_DK ** -0.5)
    k = k * lax.rsqrt(_mm_sel_rhs(k * k, hsum) + EPS)
    beta_c, g_c = _gdn_gates(ba_ref[...], alog_ref[...], dtb_ref[...])
    eg_c = jnp.exp(g_c)
    eye = eye_ref[...]
    tr = lambda a: lax.dot_general(a, eye, (((0,), (0,)), ((), ())), precision=lax.Precision.HIGHEST,
                                   preferred_element_type=F32)
    k_t = tr(k)
    q_t = tr(q)
    beta_t = tr(beta_c)
    eg_t = tr(eg_c)
    dz = dz_ref[...]
    dn = dn_ref[...]
    split = lambda r: jnp.concatenate([r[:, h * DN_DV:(h + 1) * DN_DV] for h in range(DN_HEADS)], axis=0)
    hsel = hsel_ref[...]
    hrep = hrep_ref[...]
    for b in range(GDN_S_BB):
        s2 = s_ref[b]
        kc = k_t[:, b:b + 1]
        qc = q_t[:, b:b + 1]
        beta = beta_t[0:DN_HEADS, b:b + 1]
        eg = eg_t[DN_HEADS:2 * DN_HEADS, b:b + 1]
        vh = split(v[b:b + 1, :])
        qh = split(q[b:b + 1, :])
        kh = split(k[b:b + 1, :])
        ks = _mm_sel_lhs(hsel, kc * s2)
        v_new = beta * (vh - eg * ks)
        qs = _mm_sel_lhs(hsel, qc * s2)
        qk = jnp.sum(qh * kh, axis=-1, keepdims=True)
        o = eg * qs + qk * v_new
        rep = _mm_sel_lhs(hrep, jnp.concatenate(
            [v_new, jnp.broadcast_to(eg, (DN_HEADS, DN_DV))], axis=1))
        s_out_ref[b] = s2 * rep[:, DN_DV:] + kc * rep[:, :DN_DV]
        o = _rmsnorm(o, dn) * _silu(split(dz[b:b + 1, :]))
        o_ref[b] = o


def _gdn_sample(xc, dz, ba, sconv_t, state, conv_w, alog, dtb, dn):
    nseq = xc.shape[0]
    lane = np.arange(DN_WIDTH)
    hsum = jnp.asarray((lane[:, None] // DN_DV == lane[None, :] // DN_DV).astype(np.float32), dtype=BF16)
    eye = jnp.eye(GDN_S_BB, dtype=F32)
    hsel_np = (np.arange(DN_HEADS)[:, None] == lane[None, :] // DN_DK).astype(np.float32)
    hsel = jnp.asarray(hsel_np, dtype=BF16)
    hrep = jnp.asarray(hsel_np.T, dtype=BF16)
    row = lambda n: pl.BlockSpec((GDN_S_BB, n), lambda i: (i, 0))
    full = lambda a: pl.BlockSpec(a.shape, lambda i: (0,) * a.ndim)
    st = pl.BlockSpec((GDN_S_BB, DN_HEADS * DN_DK, DN_DV), lambda i: (i, 0, 0))
    consts = (conv_w, alog, dtb, dn, hsum, eye, hsel, hrep)
    return pl.pallas_call(
        _gdn_sample_kernel,
        grid=(nseq // GDN_S_BB,),
        in_specs=[row(CONV_CH), row(DN_WIDTH), row(LANES),
                  pl.BlockSpec((CONV_WIDTH - 1, GDN_S_BB, CONV_CH), lambda i: (0, i, 0)), st]
                 + [full(a) for a in consts],
        out_specs=[pl.BlockSpec((GDN_S_BB, DN_HEADS, DN_DV), lambda i: (i, 0, 0)), st],
        out_shape=[jax.ShapeDtypeStruct((nseq, DN_HEADS, DN_DV), F32),
                   jax.ShapeDtypeStruct(state.shape, F32)],
        compiler_params=_params("parallel"),
        name="gdn_sample",
    )(xc, dz, ba, sconv_t, state, *consts)


def _route(xn, wr):
    logits = jnp.dot(xn, wr, preferred_element_type=F32)
    lane = lax.broadcasted_iota(jnp.int32, logits.shape, 1).astype(F32)
    first_at = lambda hit: jnp.min(jnp.where(hit, lane, float(LANES)), axis=-1, keepdims=True)
    glog = jnp.where(lane < N_GROUPS, logits, NEG_INF)
    gmax = jnp.max(glog, axis=-1, keepdims=True)
    gsel = first_at(glog == gmax)
    pgsel = 1.0 / jnp.sum(jnp.exp(glog - gmax), axis=-1, keepdims=True)
    lo = ROUTER_OFF + gsel * EXPERTS_PER_GROUP
    in_group = jnp.logical_and(lane >= lo, lane < lo + EXPERTS_PER_GROUP)
    elog = jnp.where(in_group, logits, NEG_INF)
    m1 = jnp.max(elog, axis=-1, keepdims=True)
    i1 = first_at(elog == m1)
    z = jnp.sum(jnp.exp(elog - m1), axis=-1, keepdims=True)
    elog2 = jnp.where(lane == i1, NEG_INF, elog)
    m2 = jnp.max(elog2, axis=-1, keepdims=True)
    i2 = first_at(elog2 == m2)
    p1 = 1.0 / z
    p2 = jnp.exp(m2 - m1) / z
    tot = p1 + p2
    return lane, i1, i2, p1 / tot * pgsel, p2 / tot * pgsel


def _outproj(x_ref, oa_ref, od_ref, wo_ref):
    return x_ref[...] + _mm(oa_ref[...], wo_ref[:ATT_WIDTH, :]) + _mm(od_ref[...], wo_ref[ATT_WIDTH:, :])


def _outproj_router_kernel(x_ref, oa_ref, od_ref, wo_ref, g_ref, wr_ref, h_ref, xn_ref, gate_ref):
    h = _outproj(x_ref, oa_ref, od_ref, wo_ref)
    h_ref[...] = h
    xn = _rmsnorm(h, g_ref[...]).astype(BF16)
    xn_ref[...] = xn
    lane, i1, i2, g1, g2 = _route(xn, wr_ref[...])
    gate_ref[...] = jnp.where(lane == i1, g1, 0.0) + jnp.where(lane == i2, g2, 0.0)


def _outproj_router(x, oa, od, wo, g, wr):
    t = x.shape[0]
    tm = min(t, 256)
    row = lambda n: pl.BlockSpec((tm, n), lambda i: (i, 0))
    full = lambda a: pl.BlockSpec(a.shape, lambda i: (0,) * a.ndim)
    return pl.pallas_call(
        _outproj_router_kernel,
        grid=(t // tm,),
        in_specs=[row(D_MODEL), row(ATT_WIDTH), row(DN_WIDTH), full(wo), full(g), full(wr)],
        out_specs=[row(D_MODEL), row(D_MODEL), row(LANES)],
        out_shape=[jax.ShapeDtypeStruct((t, D_MODEL), F32), jax.ShapeDtypeStruct((t, D_MODEL), BF16),
                   jax.ShapeDtypeStruct((t, LANES), F32)],
        compiler_params=_params("parallel"),
        name="outproj_router",
    )(x, oa, od, wo, g, wr)


def _moe_kernel(xn_ref, gate_ref, wg_ref, wu_ref, wd_ref, o_ref):
    e = pl.program_id(1)
    xn = xn_ref[...]
    lane = lax.broadcasted_iota(jnp.int32, gate_ref.shape, 1)
    gate = jnp.sum(jnp.where(lane == e + ROUTER_OFF, gate_ref[...], 0.0), axis=-1, keepdims=True)
    hg = jnp.dot(xn, wg_ref[...], preferred_element_type=F32)
    hu = jnp.dot(xn, wu_ref[...], preferred_element_type=F32)
    hm = _silu(hg) * hu * gate
    y = jnp.dot(hm.astype(BF16), wd_ref[...], preferred_element_type=F32)

    @pl.when(e == 0)
    def _():
        o_ref[...] = y

    @pl.when(e > 0)
    def _():
        o_ref[...] += y


def _moe(xn, gates, wg, wu, wd):
    t = xn.shape[0]
    tm = min(t, 1024)
    return pl.pallas_call(
        _moe_kernel,
        grid=(t // tm, N_EXPERTS),
        in_specs=[pl.BlockSpec((tm, D_MODEL), lambda i, e: (i, 0)),
                  pl.BlockSpec((tm, LANES), lambda i, e: (i, 0)),
                  pl.BlockSpec((None, D_MODEL, D_EXPERT), lambda i, e: (e, 0, 0)),
                  pl.BlockSpec((None, D_MODEL, D_EXPERT), lambda i, e: (e, 0, 0)),
                  pl.BlockSpec((None, D_EXPERT, D_MODEL), lambda i, e: (e, 0, 0))],
        out_specs=pl.BlockSpec((tm, D_MODEL), lambda i, e: (i, 0)),
        out_shape=jax.ShapeDtypeStruct((t, D_MODEL), F32),
        compiler_params=_params("parallel", "arbitrary"),
        name="moe",
    )(xn, gates, wg, wu, wd)


MOE_TM = 256
POS_TM = 512
INFO_G1, INFO_G2, INFO_E1, INFO_E2 = 0, 1, 2, 3
DMA_UNROLL = 8


def _moe_tiles(t):
    return (2 * t) // MOE_TM + N_EXPERTS


def _route_kernel(x_ref, oa_ref, od_ref, wo_ref, g_ref, wr_ref, h_ref, xn_ref, info_ref):
    h = _outproj(x_ref, oa_ref, od_ref, wo_ref)
    h_ref[...] = h
    xn = _rmsnorm(h, g_ref[...])
    xn_ref[...] = xn
    lane, i1, i2, g1, g2 = _route(xn.astype(BF16), wr_ref[...])
    info = jnp.where(lane == INFO_G1, g1, 0.0) + jnp.where(lane == INFO_G2, g2, 0.0)
    info = info + jnp.where(lane == INFO_E1, i1, 0.0) + jnp.where(lane == INFO_E2, i2, 0.0)
    info_ref[...] = info


def _route_sparse(x, oa, od, wo, g, wr):
    t = x.shape[0]
    tm = 256
    row = lambda n: pl.BlockSpec((tm, n), lambda i: (i, 0))
    full = lambda a: pl.BlockSpec(a.shape, lambda i: (0,) * a.ndim)
    return pl.pallas_call(
        _route_kernel,
        grid=(t // tm,),
        in_specs=[row(D_MODEL), row(ATT_WIDTH), row(DN_WIDTH), full(wo), full(g), full(wr)],
        out_specs=[row(D_MODEL), row(D_MODEL), row(LANES)],
        out_shape=[jax.ShapeDtypeStruct((t, D_MODEL), F32), jax.ShapeDtypeStruct((t, D_MODEL), F32),
                   jax.ShapeDtypeStruct((t, LANES), F32)],
        compiler_params=_params("parallel"),
        name="route",
    )(x, oa, od, wo, g, wr)


def _positions_kernel(info_ref, ltri_ref, utri_ref, pos_ref, cnt_ref, run_scr, off_scr):
    phase = pl.program_id(0)
    i = pl.program_id(1)
    info = info_ref[...]
    lane = lax.broadcasted_iota(jnp.int32, info.shape, 1).astype(F32)
    hit1 = lane == info[:, INFO_E1:INFO_E1 + 1]
    hit2 = lane == info[:, INFO_E2:INFO_E2 + 1]
    onehot = jnp.logical_or(hit1, hit2).astype(F32)

    @pl.when(jnp.logical_and(phase == 0, i == 0))
    def _():
        run_scr[...] = jnp.zeros(run_scr.shape, F32)

    @pl.when(jnp.logical_and(phase == 1, i == 0))
    def _():
        cnt = run_scr[...]
        cnt_ref[...] = cnt
        tiles = jnp.floor((cnt + (MOE_TM - 1)) * (1.0 / MOE_TM))
        off_scr[...] = MOE_TM * jnp.dot(tiles.astype(BF16), utri_ref[...], preferred_element_type=F32)
        run_scr[...] = jnp.zeros(run_scr.shape, F32)

    @pl.when(phase == 0)
    def _():
        pos_ref[...] = jnp.zeros(pos_ref.shape, jnp.int32)

    @pl.when(phase == 1)
    def _():
        before = (jnp.dot(ltri_ref[...], onehot.astype(BF16), preferred_element_type=F32)
                  + run_scr[...] + off_scr[...])
        pos1 = jnp.sum(jnp.where(hit1, before, 0.0), axis=-1, keepdims=True)
        pos2 = jnp.sum(jnp.where(hit2, before, 0.0), axis=-1, keepdims=True)
        pos_ref[...] = (jnp.where(lane == 0, pos1, 0.0) + jnp.where(lane == 1, pos2, 0.0)).astype(jnp.int32)

    run_scr[...] += jnp.sum(onehot, axis=0, keepdims=True)


def _positions(info):
    t = info.shape[0]
    tm = min(t, POS_TM)
    tok = np.arange(tm)
    ltri = jnp.asarray((tok[:, None] > tok[None, :]).astype(np.float32), dtype=BF16)
    ln = np.arange(LANES)
    utri = jnp.asarray((ln[:, None] < ln[None, :]).astype(np.float32), dtype=BF16)
    full = lambda a: pl.BlockSpec(a.shape, lambda ph, i: (0,) * a.ndim)
    return pl.pallas_call(
        _positions_kernel,
        grid=(2, t // tm),
        in_specs=[pl.BlockSpec((tm, LANES), lambda ph, i: (i, 0)), full(ltri), full(utri)],
        out_specs=[pl.BlockSpec((tm, LANES), lambda ph, i: (i * ph, 0)),
                   pl.BlockSpec((1, LANES), lambda ph, i: (0, 0))],
        out_shape=[jax.ShapeDtypeStruct((t, LANES), jnp.int32), jax.ShapeDtypeStruct((1, LANES), F32)],
        scratch_shapes=[pltpu.VMEM((1, LANES), F32), pltpu.VMEM((1, LANES), F32)],
        compiler_params=_params("arbitrary", "arbitrary"),
        name="positions",
    )(info, ltri, utri)


def _row_copy(src_hbm, src_row, dst_hbm, dst_row, sem):
    return pltpu.make_async_copy(src_hbm.at[pl.ds(src_row, 1)], dst_hbm.at[pl.ds(dst_row, 1)], sem)


SCATTER_SLOTS = 3


def _scatter_kernel(pos1_ref, pos2_ref, last_ref, used_ref, nt_ref, xn_hbm, zero_hbm, xs_hbm,
                    buf, lsem, sem, zsem, *, n_tok):
    max_tiles = xs_hbm.shape[0] // MOE_TM

    def zero_tile(tile):
        return pltpu.make_async_copy(zero_hbm, xs_hbm.at[pl.ds(tile * MOE_TM, MOE_TM)], zsem)

    def for_unused(fn):
        def body(tile, carry):
            fn(tile)
            return carry
        lax.fori_loop(nt_ref[0], max_tiles, body, 0)

    for e in range(N_EXPERTS):
        @pl.when(used_ref[e] > 0)
        def _():
            zero_tile(last_ref[e]).start()
    for_unused(lambda tile: zero_tile(tile).start())
    for e in range(N_EXPERTS):
        @pl.when(used_ref[e] > 0)
        def _():
            zero_tile(last_ref[e]).wait()
    for_unused(lambda tile: zero_tile(tile).wait())

    tm = buf.shape[1]
    n = n_tok // tm

    def load(i):
        return pltpu.make_async_copy(xn_hbm.at[pl.ds(i * tm, tm)], buf.at[i % SCATTER_SLOTS],
                                     lsem.at[i % SCATTER_SLOTS])

    def wait_rows(slot):
        pltpu.make_async_copy(xs_hbm.at[pl.ds(0, 2 * tm)], xs_hbm.at[pl.ds(0, 2 * tm)], sem.at[slot]).wait()

    load(0).start()
    load(1).start()

    def step(i, carry):
        slot = i % SCATTER_SLOTS
        load(i).wait()

        def body(j, c2):
            tok = i * tm + j
            src = buf.at[slot, pl.ds(j, 1)]
            pltpu.make_async_copy(src, xs_hbm.at[pl.ds(pos1_ref[tok], 1)], sem.at[slot]).start()
            pltpu.make_async_copy(src, xs_hbm.at[pl.ds(pos2_ref[tok], 1)], sem.at[slot]).start()
            return c2
        lax.fori_loop(0, tm, body, 0, unroll=DMA_UNROLL)

        @pl.when(i >= 1)
        def _():
            wait_rows((i - 1) % SCATTER_SLOTS)

        @pl.when(i + 2 < n)
        def _():
            load(i + 2).start()
        return carry
    lax.fori_loop(0, n, step, 0)
    wait_rows((n - 1) % SCATTER_SLOTS)


def _scatter_rows(xn, pos1, pos2, last_tile, used, n_tiles, n_rows):
    t = xn.shape[0]
    zero = jnp.zeros((MOE_TM, D_MODEL), F32)
    any_spec = pl.BlockSpec(memory_space=pl.ANY)
    return pl.pallas_call(
        functools.partial(_scatter_kernel, n_tok=t),
        grid_spec=pltpu.PrefetchScalarGridSpec(
            num_scalar_prefetch=5, grid=(1,),
            in_specs=[any_spec, any_spec], out_specs=any_spec,
            scratch_shapes=[pltpu.VMEM((SCATTER_SLOTS, MOE_TM, D_MODEL), F32),
                            pltpu.SemaphoreType.DMA((SCATTER_SLOTS,)),
                            pltpu.SemaphoreType.DMA((SCATTER_SLOTS,)),
                            pltpu.SemaphoreType.DMA]),
        out_shape=jax.ShapeDtypeStruct((n_rows, D_MODEL), F32),
        compiler_params=_params("arbitrary"),
        name="scatter_rows",
    )(pos1, pos2, last_tile, used, n_tiles, xn, zero)


def _experts_kernel(te_ref, tv_ref, nt_ref, xs_ref, wg_ref, wu_ref, wd_ref, ys_ref):
    i = pl.program_id(0)
    used = i < nt_ref[0]

    @pl.when(used)
    def _():
        row = lax.broadcasted_iota(jnp.int32, xs_ref.shape, 0)
        x = jnp.where(row < tv_ref[i], xs_ref[...], 0.0).astype(BF16)
        hg = jnp.dot(x, wg_ref[...], preferred_element_type=F32)
        hu = jnp.dot(x, wu_ref[...], preferred_element_type=F32)
        hm = (_silu(hg) * hu).astype(BF16)
        ys_ref[...] = jnp.dot(hm, wd_ref[...], preferred_element_type=F32)

    @pl.when(jnp.logical_not(used))
    def _():
        ys_ref[...] = jnp.zeros(ys_ref.shape, F32)


def _experts(xs, tile_expert, tile_valid, n_tiles, wg, wu, wd):
    max_tiles = xs.shape[0] // MOE_TM
    rows = pl.BlockSpec((MOE_TM, D_MODEL), lambda i, te, tv, nt: (i, 0))
    wspec = lambda shape: pl.BlockSpec((None,) + shape, lambda i, te, tv, nt: (te[i], 0, 0))
    return pl.pallas_call(
        _experts_kernel,
        grid_spec=pltpu.PrefetchScalarGridSpec(
            num_scalar_prefetch=3, grid=(max_tiles,),
            in_specs=[rows, wspec((D_MODEL, D_EXPERT)), wspec((D_MODEL, D_EXPERT)),
                      wspec((D_EXPERT, D_MODEL))],
            out_specs=rows),
        out_shape=jax.ShapeDtypeStruct(xs.shape, F32),
        compiler_params=_params("arbitrary"),
        name="experts",
    )(tile_expert, tile_valid, n_tiles, xs, wg, wu, wd)


def _ple_gather_kernel(pos1_ref, pos2_ref, h_ref, info_ref, p_ref, wpp_ref, wpg_ref, gp_ref, gf_ref,
                       ys_hbm, y_ref, ybuf, sem):
    i = pl.program_id(0)
    n = pl.num_programs(0)
    tm = h_ref.shape[0]

    def issue(tile, slot):
        def body(j, carry):
            tok = tile * tm + j
            pltpu.make_async_copy(ys_hbm.at[pl.ds(pos1_ref[tok], 1)], ybuf.at[slot, 0, pl.ds(j, 1)],
                                  sem.at[slot]).start()
            pltpu.make_async_copy(ys_hbm.at[pl.ds(pos2_ref[tok], 1)], ybuf.at[slot, 1, pl.ds(j, 1)],
                                  sem.at[slot]).start()
            return carry
        lax.fori_loop(0, tm, body, 0, unroll=DMA_UNROLL)

    @pl.when(i == 0)
    def _():
        issue(0, 0)

    @pl.when(i + 1 < n)
    def _():
        issue(i + 1, (i + 1) % 2)

    slot = i % 2
    pltpu.make_async_copy(ybuf.at[slot], ybuf.at[slot], sem.at[slot]).wait()
    info = info_ref[...]
    moe = info[:, INFO_G1:INFO_G1 + 1] * ybuf[slot, 0] + info[:, INFO_G2:INFO_G2 + 1] * ybuf[slot, 1]
    h = h_ref[...] + moe
    hn = _rmsnorm(h, gp_ref[...])
    h = h + _mm(p_ref[...], wpp_ref[...]) * _sigmoid(_mm(hn, wpg_ref[...]))
    y_ref[...] = _rmsnorm(h, gf_ref[...])


def _ple_gather(h, info, p, ys, pos1, pos2, wpp, wpg, gp, gf):
    t = h.shape[0]
    tm = 256
    row = lambda n: pl.BlockSpec((tm, n), lambda i, p1, p2: (i, 0))
    full = lambda a: pl.BlockSpec(a.shape, lambda i, p1, p2: (0,) * a.ndim)
    return pl.pallas_call(
        _ple_gather_kernel,
        grid_spec=pltpu.PrefetchScalarGridSpec(
            num_scalar_prefetch=2, grid=(t // tm,),
            in_specs=[row(D_MODEL), row(LANES), row(PLE_DIM), full(wpp), full(wpg), full(gp), full(gf),
                      pl.BlockSpec(memory_space=pl.ANY)],
            out_specs=row(D_MODEL),
            scratch_shapes=[pltpu.VMEM((2, 2, tm, D_MODEL), F32), pltpu.SemaphoreType.DMA((2,))]),
        out_shape=jax.ShapeDtypeStruct((t, D_MODEL), F32),
        compiler_params=_params("arbitrary"),
        name="ple_gather",
    )(pos1, pos2, h, info, p, wpp, wpg, gp, gf, ys)


SC_IDX = 128
SC_ROWS = 32
SC_WORKERS = 32


def _sc_mesh():
    return plsc.VectorSubcoreMesh(core_axis_name="c", subcore_axis_name="s")


def _sc_windows(t, fn):
    per_worker = t // SC_WORKERS
    worker = lax.axis_index(("c", "s"))

    @pl.loop(0, per_worker // SC_IDX)
    def _(w):
        fn(worker * per_worker + w * SC_IDX)


def _sc_scatter_rows(xn, pos1, pos2, n_rows):
    t, d = xn.shape
    assert t % (SC_WORKERS * SC_IDX) == 0
    idx_t = pltpu.VMEM((1, SC_IDX), jnp.int32)

    @pl.kernel(out_type=jax.ShapeDtypeStruct((n_rows, d), xn.dtype), mesh=_sc_mesh(),
               scratch_types=[idx_t, idx_t, pltpu.VMEM((SC_ROWS, d), xn.dtype)])
    def scatter(x_hbm, p1_hbm, p2_hbm, o_hbm, i1_v, i2_v, buf):
        def window(base):
            pltpu.sync_copy(p1_hbm.at[:, pl.ds(base, SC_IDX)], i1_v)
            pltpu.sync_copy(p2_hbm.at[:, pl.ds(base, SC_IDX)], i2_v)
            for k in range(SC_IDX // SC_ROWS):
                pltpu.sync_copy(x_hbm.at[pl.ds(base + k * SC_ROWS, SC_ROWS)], buf)
                pltpu.sync_copy(buf, o_hbm.at[i1_v.at[0, pl.ds(k * SC_ROWS, SC_ROWS)]])
                pltpu.sync_copy(buf, o_hbm.at[i2_v.at[0, pl.ds(k * SC_ROWS, SC_ROWS)]])
        _sc_windows(t, window)

    return scatter(xn, pos1.reshape(1, t), pos2.reshape(1, t))


def _sc_gather_rows(ys, pos1, pos2):
    t = pos1.shape[0]
    d = ys.shape[1]
    assert t % (SC_WORKERS * SC_IDX) == 0
    idx_t = pltpu.VMEM((1, SC_IDX), jnp.int32)
    out = jax.ShapeDtypeStruct((t, d), ys.dtype)

    @pl.kernel(out_type=(out, out), mesh=_sc_mesh(),
               scratch_types=[idx_t, idx_t, pltpu.VMEM((SC_ROWS, d), ys.dtype)])
    def gather(y_hbm, p1_hbm, p2_hbm, o1_hbm, o2_hbm, i1_v, i2_v, buf):
        def window(base):
            pltpu.sync_copy(p1_hbm.at[:, pl.ds(base, SC_IDX)], i1_v)
            pltpu.sync_copy(p2_hbm.at[:, pl.ds(base, SC_IDX)], i2_v)
            for k in range(SC_IDX // SC_ROWS):
                rows = pl.ds(base + k * SC_ROWS, SC_ROWS)
                pltpu.sync_copy(y_hbm.at[i1_v.at[0, pl.ds(k * SC_ROWS, SC_ROWS)]], buf)
                pltpu.sync_copy(buf, o1_hbm.at[rows])
                pltpu.sync_copy(y_hbm.at[i2_v.at[0, pl.ds(k * SC_ROWS, SC_ROWS)]], buf)
                pltpu.sync_copy(buf, o2_hbm.at[rows])
        _sc_windows(t, window)

    return gather(ys, pos1.reshape(1, t), pos2.reshape(1, t))


def _ple_sparse_kernel(h_ref, info_ref, y1_ref, y2_ref, p_ref, wpp_ref, wpg_ref, gp_ref, gf_ref, y_ref):
    info = info_ref[...]
    h = h_ref[...] + (info[:, INFO_G1:INFO_G1 + 1] * y1_ref[...] + info[:, INFO_G2:INFO_G2 + 1] * y2_ref[...])
    hn = _rmsnorm(h, gp_ref[...])
    h = h + _mm(p_ref[...], wpp_ref[...]) * _sigmoid(_mm(hn, wpg_ref[...]))
    y_ref[...] = _rmsnorm(h, gf_ref[...])


def _ple_sparse(h, info, y1, y2, p, wpp, wpg, gp, gf):
    t = h.shape[0]
    tm = 256
    row = lambda n: pl.BlockSpec((tm, n), lambda i: (i, 0))
    full = lambda a: pl.BlockSpec(a.shape, lambda i: (0,) * a.ndim)
    return pl.pallas_call(
        _ple_sparse_kernel,
        grid=(t // tm,),
        in_specs=[row(D_MODEL), row(LANES), row(D_MODEL), row(D_MODEL), row(PLE_DIM),
                  full(wpp), full(wpg), full(gp), full(gf)],
        out_specs=row(D_MODEL),
        out_shape=jax.ShapeDtypeStruct((t, D_MODEL), F32),
        compiler_params=_params("parallel"),
        name="ple_sparse",
    )(h, info, y1, y2, p, wpp, wpg, gp, gf)


def _tile_tables(cnt, max_tiles):
    tiles_e = (cnt + (MOE_TM - 1)) // MOE_TM
    ends = jnp.cumsum(tiles_e)
    n_tiles = ends[-1]
    tile = jnp.arange(max_tiles, dtype=jnp.int32)
    idx = jnp.minimum(tile, n_tiles - 1)
    tile_expert = jnp.sum((idx[:, None] >= ends[None, :]).astype(jnp.int32), axis=1)
    mine = tile_expert[:, None] == jnp.arange(N_EXPERTS, dtype=jnp.int32)[None, :]
    of_mine = lambda v: jnp.sum(jnp.where(mine, v[None, :], 0), axis=1)
    valid = jnp.clip(of_mine(cnt) - (idx - of_mine(ends - tiles_e)) * MOE_TM, 0, MOE_TM)
    tile_valid = jnp.where(tile < n_tiles, valid, 0).astype(jnp.int32)
    return (tile_expert, tile_valid, n_tiles.reshape(1), (ends - 1).astype(jnp.int32),
            tiles_e.astype(jnp.int32))


def _ple_final_kernel(h_ref, m_ref, p_ref, wpp_ref, wpg_ref, gp_ref, gf_ref, y_ref):
    h = h_ref[...] + m_ref[...]
    hn = _rmsnorm(h, gp_ref[...])
    h = h + _mm(p_ref[...], wpp_ref[...]) * _sigmoid(_mm(hn, wpg_ref[...]))
    y_ref[...] = _rmsnorm(h, gf_ref[...])


def _ple_final(h, m, p, wpp, wpg, gp, gf):
    t = h.shape[0]
    tm = min(t, 256)
    row = lambda n: pl.BlockSpec((tm, n), lambda i: (i, 0))
    full = lambda a: pl.BlockSpec(a.shape, lambda i: (0,) * a.ndim)
    return pl.pallas_call(
        _ple_final_kernel,
        grid=(t // tm,),
        in_specs=[row(D_MODEL), row(D_MODEL), row(PLE_DIM), full(wpp), full(wpg), full(gp), full(gf)],
        out_specs=row(D_MODEL),
        out_shape=jax.ShapeDtypeStruct((t, D_MODEL), F32),
        compiler_params=_params("parallel"),
        name="ple_final",
    )(h, m, p, wpp, wpg, gp, gf)


def kernel(x_prompt, x_sample, p_prompt, p_sample, cache_k, cache_v, state_conv, state_S, rel_bias, norm_mix, w_in, att_sink, conv_w, dn_A_log, dn_dt_bias, dn_norm, w_out, norm_ffn, w_router_group, w_router_expert, w_gate, w_up, w_down, w_ple_proj, w_ple_gate, norm_ple, norm_final):
    batch, seq, _ = x_prompt.shape
    nseq = x_sample.shape[0]
    assert x_sample.shape[1] == 1 and norm_mix.shape[0] == 1 and cache_k.shape[2] == WINDOW
    assert seq % GDN_TB == 0 and seq % ATT_BLOCK == 0

    wi = w_in[0]
    o_db = ATT_COLS + CONV_CH
    w_in_re = jnp.concatenate(
        [wi[:, :o_db], wi[:, o_db + 2 * DN_HEADS:], wi[:, o_db:o_db + 2 * DN_HEADS],
         jnp.zeros((D_MODEL, LANES - 2 * DN_HEADS), F32)], axis=1).astype(BF16)
    row = lambda a: a.reshape(1, -1).astype(F32)
    pad_lanes = lambda a, off: jnp.zeros((1, LANES), F32).at[0, off:off + a.shape[0]].set(a)
    alog = pad_lanes(dn_A_log[0], DN_HEADS)
    dtb = pad_lanes(dn_dt_bias[0], DN_HEADS)
    dnx = jnp.tile(dn_norm[0], DN_HEADS).reshape(1, DN_WIDTH)
    w_router = jnp.concatenate(
        [w_router_group[0], w_router_expert[0],
         jnp.zeros((D_MODEL, LANES - N_GROUPS - N_EXPERTS), F32)], axis=1).astype(BF16)
    wo = w_out[0].astype(BF16)
    wg, wu, wd = w_gate[0].astype(BF16), w_up[0].astype(BF16), w_down[0].astype(BF16)
    wpp, wpg = w_ple_proj[0].astype(BF16), w_ple_gate[0].astype(BF16)
    sink = att_sink[0]

    qi = np.arange(ATT_BLOCK)[:, None]
    kj = np.arange(2 * ATT_BLOCK)[None, :]
    bucket_p = jnp.asarray(_t5_bucket_np(qi + ATT_BLOCK - kj))
    bucket_s = jnp.asarray(_t5_bucket_np(WINDOW - np.arange(WINDOW)[None, :]))

    def tail(x, o_att, o_dn, p):
        h1, xn2, gates = _outproj_router(x, o_att, o_dn, wo, row(norm_ffn[0]), w_router)
        moe = _moe(xn2, gates, wg, wu, wd)
        return _ple_final(h1, moe, p, wpp, wpg, row(norm_ple[0]), row(norm_final))

    xp = x_prompt.reshape(batch * seq, D_MODEL)
    att_p, xc_p, dz_p, ba_p = _inproj(xp, row(norm_mix[0]), w_in_re)
    o_att_p = _attn_prompt(att_p, bucket_p, rel_bias, sink, batch, seq)
    o_dn_p, s_p = _gdn_prompt(xc_p, dz_p, ba_p, conv_w[0], alog, dtb, dnx, batch, seq)
    h1, xn2, info = _route_sparse(xp, o_att_p, o_dn_p, wo, row(norm_ffn[0]), w_router)
    pos, cnt = _positions(info)
    pos1, pos2 = pos[:, 0], pos[:, 1]
    max_tiles = _moe_tiles(batch * seq)
    cnt_e = cnt[0, ROUTER_OFF:ROUTER_OFF + N_EXPERTS].astype(jnp.int32)
    tile_expert, tile_valid, n_tiles, last_tile, used = _tile_tables(cnt_e, max_tiles)
    xs = _sc_scatter_rows(xn2, pos1, pos2, max_tiles * MOE_TM)
    ys = _experts(xs, tile_expert, tile_valid, n_tiles, wg, wu, wd)
    y1, y2 = _sc_gather_rows(ys, pos1, pos2)
    y_p = _ple_sparse(h1, info, y1, y2, p_prompt[0].reshape(batch * seq, PLE_DIM),
                      wpp, wpg, row(norm_ple[0]), row(norm_final))

    xs = x_sample.reshape(nseq, D_MODEL)
    att_s, xc_s, dz_s, ba_s = _inproj(xs, row(norm_mix[0]), w_in_re)
    ck = cache_k[0].reshape(nseq, WINDOW, KV_WIDTH)
    cv = cache_v[0].reshape(nseq, WINDOW, KV_WIDTH)
    o_att_s = _attn_sample(att_s, ck, cv, bucket_s, rel_bias, sink)
    sconv_t = jnp.swapaxes(state_conv[0], 0, 1)
    o_dn_s, s_s = _gdn_sample(xc_s, dz_s, ba_s, sconv_t,
                              state_S[0].reshape(nseq, DN_HEADS * DN_DK, DN_DV), conv_w[0], alog, dtb,
                              dn_norm[0].reshape(1, DN_DV))
    s_s = s_s.reshape(nseq, DN_HEADS, DN_DK, DN_DV)
    y_s = tail(xs, o_att_s, o_dn_s.reshape(nseq, DN_WIDTH), p_sample[0].reshape(nseq, PLE_DIM))

    att_p3 = att_p.reshape(batch, seq, ATT_COLS)
    kv_shape = (1, batch, WINDOW, ATT_KV_HEADS, HEAD_DIM)
    k_p = att_p3[:, seq - WINDOW:, ATT_WIDTH:ATT_WIDTH + KV_WIDTH].reshape(kv_shape)
    v_p = att_p3[:, seq - WINDOW:, ATT_WIDTH + KV_WIDTH:].reshape(kv_shape)
    conv_p = xc_p.reshape(batch, seq, CONV_CH)[:, seq - (CONV_WIDTH - 1):][None]
    k_new = att_s[:, None, ATT_WIDTH:ATT_WIDTH + KV_WIDTH]
    v_new = att_s[:, None, ATT_WIDTH + KV_WIDTH:]
    kv_s_shape = (1, nseq, WINDOW, ATT_KV_HEADS, HEAD_DIM)
    k_s = jnp.concatenate([ck[:, 1:], k_new], axis=1).reshape(kv_s_shape)
    v_s = jnp.concatenate([cv[:, 1:], v_new], axis=1).reshape(kv_s_shape)
    conv_s = jnp.concatenate([state_conv[0][:, 1:], xc_s[:, None, :]], axis=1)[None]
    return (y_p.reshape(batch, seq, D_MODEL), y_s.reshape(nseq, 1, D_MODEL),
            k_p, v_p, conv_p, s_p[None], k_s, v_s, conv_s, s_s[None])
```

```python
import functools
import math

import numpy as np
import jax
import jax.numpy as jnp
from jax import lax
from jax.experimental import pallas as pl
from jax.experimental.pallas import tpu as pltpu
from jax.experimental.pallas import tpu_sc as plsc

F32 = jnp.float32
BF16 = jnp.bfloat16

D_MODEL = 1024
ATT_HEADS = 8
ATT_KV_HEADS = 2
HEAD_DIM = 64
GQA = ATT_HEADS // ATT_KV_HEADS
WINDOW = 128
ATT_BLOCK = 128
N_BUCKETS = 32
DN_HEADS = 8
DN_DK = 64
DN_DV = 64
CONV_WIDTH = 4
DN_CHUNK = 64
ATT_WIDTH = ATT_HEADS * HEAD_DIM
KV_WIDTH = ATT_KV_HEADS * HEAD_DIM
DN_WIDTH = DN_HEADS * DN_DV
CONV_CH = 3 * DN_WIDTH
N_GROUPS = 4
EXPERTS_PER_GROUP = 8
N_EXPERTS = N_GROUPS * EXPERTS_PER_GROUP
D_EXPERT = 256
PLE_DIM = 256
EPS = 1e-6
NEG_INF = float("-inf")

ATT_COLS = ATT_WIDTH + 2 * KV_WIDTH
LANES = 128
IN_COLS = ATT_COLS + CONV_CH + DN_WIDTH + LANES
ROUTER_OFF = N_GROUPS
VMEM_LIMIT = 48 * 1024 * 1024
ROW_TM = 512


def _params(*sem):
    return pltpu.CompilerParams(dimension_semantics=sem, vmem_limit_bytes=VMEM_LIMIT)


def _mm(a, b):
    return jnp.dot(a.astype(BF16), b.astype(BF16), preferred_element_type=F32)


def _mm_nt(a, b):
    return lax.dot_general(a.astype(BF16), b.astype(BF16), (((1,), (1,)), ((), ())),
                           preferred_element_type=F32)


def _mm_tn(a, b):
    return lax.dot_general(a.astype(BF16), b.astype(BF16), (((0,), (0,)), ((), ())),
                           preferred_element_type=F32)


def _split3(x):
    h1 = x.astype(BF16)
    r1 = x - h1.astype(F32)
    h2 = r1.astype(BF16)
    h3 = (r1 - h2.astype(F32)).astype(BF16)
    return h1, h2, h3


def _mm_sel_rhs(x, sel):
    h1, h2, h3 = _split3(x)
    d = lambda h: jnp.dot(h, sel, preferred_element_type=F32)
    return d(h1) + d(h2) + d(h3)


def _mm_sel_lhs(sel, x):
    h1, h2, h3 = _split3(x)
    d = lambda h: jnp.dot(sel, h, preferred_element_type=F32)
    return d(h1) + d(h2) + d(h3)


def _mm3(a, b):
    ah = a.astype(BF16)
    al = (a - ah.astype(F32)).astype(BF16)
    bh = b.astype(BF16)
    bl = (b - bh.astype(F32)).astype(BF16)
    d = lambda u, v: jnp.dot(u, v, preferred_element_type=F32)
    return d(ah, bh) + d(ah, bl) + d(al, bh)


def _sigmoid(x):
    return 1.0 / (1.0 + jnp.exp(-x))


def _silu(x):
    return x * _sigmoid(x)


def _softplus(x):
    return jnp.maximum(x, 0.0) + jnp.log1p(jnp.exp(-jnp.abs(x)))


def _rmsnorm(x, g):
    return x * lax.rsqrt(jnp.mean(x * x, axis=-1, keepdims=True) + EPS) * g


def _t5_bucket_np(dist):
    max_exact = N_BUCKETS // 2
    d = np.maximum(dist, 0)
    ratio = (np.log(np.maximum(d, 1).astype(np.float32) / np.float32(max_exact))
             / np.float32(math.log(WINDOW / max_exact))).astype(np.float32)
    large = np.minimum(max_exact + (ratio * np.float32(N_BUCKETS - max_exact)).astype(np.int32),
                       N_BUCKETS - 1)
    return np.where(d < max_exact, d, large).astype(np.int32)


def _bias_lookup(bucket, rb_ref, h):
    acc = jnp.zeros(bucket.shape, F32)
    for t in range(N_BUCKETS):
        acc = jnp.where(bucket == t, rb_ref[t, h], acc)
    return acc


def _inproj_kernel(x_ref, g_ref, w_ref, att_ref, xc_ref, dz_ref, ba_ref):
    xn = _rmsnorm(x_ref[...], g_ref[...]).astype(BF16)
    o0, o1, o2 = ATT_COLS, ATT_COLS + CONV_CH, ATT_COLS + CONV_CH + DN_WIDTH
    att_ref[...] = jnp.dot(xn, w_ref[:, :o0], preferred_element_type=F32)
    xc_ref[...] = jnp.dot(xn, w_ref[:, o0:o1], preferred_element_type=F32)
    dz_ref[...] = jnp.dot(xn, w_ref[:, o1:o2], preferred_element_type=F32)
    ba_ref[...] = jnp.dot(xn, w_ref[:, o2:], preferred_element_type=F32)


def _inproj(x, g, w):
    t = x.shape[0]
    tm = min(t, ROW_TM)
    row = lambda n: pl.BlockSpec((tm, n), lambda i: (i, 0))
    full = lambda a: pl.BlockSpec(a.shape, lambda i: (0,) * a.ndim)
    return pl.pallas_call(
        _inproj_kernel,
        grid=(t // tm,),
        in_specs=[row(D_MODEL), full(g), full(w)],
        out_specs=[row(ATT_COLS), row(CONV_CH), row(DN_WIDTH), row(LANES)],
        out_shape=[jax.ShapeDtypeStruct((t, n), F32) for n in (ATT_COLS, CONV_CH, DN_WIDTH, LANES)],
        compiler_params=_params("parallel"),
        name="inproj",
    )(x, g, w)


def _attn_prompt_kernel(cur_ref, prev_ref, bucket_ref, rb_ref, sink_ref, o_ref, bias_scr):
    first = jnp.logical_and(pl.program_id(0) == 0, pl.program_id(1) == 0)
    qi = lax.broadcasted_iota(jnp.int32, (ATT_BLOCK, 2 * ATT_BLOCK), 0)
    kj = lax.broadcasted_iota(jnp.int32, (ATT_BLOCK, 2 * ATT_BLOCK), 1)

    @pl.when(first)
    def _():
        dist = qi + ATT_BLOCK - kj
        band = jnp.logical_and(dist >= 0, dist < WINDOW)
        bucket = bucket_ref[...]
        for h in range(ATT_HEADS):
            bias_scr[h] = jnp.where(band, _bias_lookup(bucket, rb_ref, h), NEG_INF)

    cur = cur_ref[...]
    prev = prev_ref[...]
    q = cur[:, :ATT_WIDTH] * (HEAD_DIM ** -0.5)
    kcat = jnp.concatenate([prev[:, ATT_WIDTH:ATT_WIDTH + KV_WIDTH],
                            cur[:, ATT_WIDTH:ATT_WIDTH + KV_WIDTH]], axis=0)
    vcat = jnp.concatenate([prev[:, ATT_WIDTH + KV_WIDTH:], cur[:, ATT_WIDTH + KV_WIDTH:]], axis=0)
    keep = jnp.logical_or(pl.program_id(1) > 0, kj >= ATT_BLOCK)
    outs = []
    for h in range(ATT_HEADS):
        g = h // GQA
        qh = q[:, h * HEAD_DIM:(h + 1) * HEAD_DIM]
        kh = kcat[:, g * HEAD_DIM:(g + 1) * HEAD_DIM]
        vh = vcat[:, g * HEAD_DIM:(g + 1) * HEAD_DIM]
        s = jnp.where(keep, _mm_nt(qh, kh) + bias_scr[h], NEG_INF)
        sink = sink_ref[h]
        m = jnp.maximum(jnp.max(s, axis=-1, keepdims=True), sink)
        p = jnp.exp(s - m)
        den = jnp.sum(p, axis=-1, keepdims=True) + jnp.exp(sink - m)
        outs.append(_mm(p, vh) / den)
    o_ref[...] = jnp.concatenate(outs, axis=1)


def _attn_prompt(att, bucket, rel_bias, sink, batch, seq):
    nb = seq // ATT_BLOCK
    smem = pl.BlockSpec(memory_space=pltpu.SMEM)
    return pl.pallas_call(
        _attn_prompt_kernel,
        grid=(batch, nb),
        in_specs=[
            pl.BlockSpec((ATT_BLOCK, ATT_COLS), lambda b, i: (b * nb + i, 0)),
            pl.BlockSpec((ATT_BLOCK, ATT_COLS), lambda b, i: (b * nb + jnp.maximum(i - 1, 0), 0)),
            pl.BlockSpec(bucket.shape, lambda b, i: (0, 0)),
            smem, smem,
        ],
        out_specs=pl.BlockSpec((ATT_BLOCK, ATT_WIDTH), lambda b, i: (b * nb + i, 0)),
        out_shape=jax.ShapeDtypeStruct((batch * seq, ATT_WIDTH), F32),
        scratch_shapes=[pltpu.VMEM((ATT_HEADS, ATT_BLOCK, 2 * ATT_BLOCK), F32)],
        compiler_params=_params("arbitrary", "arbitrary"),
        name="attn_prompt",
    )(att, att, bucket, rel_bias, sink)


ATT_S_BB = 8


def _attn_sample_kernel(att_ref, ck_ref, cv_ref, bucket_ref, rb_ref, sink_ref, o_ref,
                        bias_scr, col_scr):
    hrow = lax.broadcasted_iota(jnp.int32, (ATT_HEADS, LANES), 0)
    lane = lax.broadcasted_iota(jnp.int32, (ATT_HEADS, LANES), 1)

    @pl.when(pl.program_id(0) == 0)
    def _():
        bucket = jnp.broadcast_to(bucket_ref[...], (ATT_HEADS, LANES))
        bias = jnp.zeros((ATT_HEADS, LANES), F32)
        cols = jnp.zeros((ATT_HEADS, LANES), F32)
        for h in range(ATT_HEADS):
            bias = jnp.where(hrow == h, _bias_lookup(bucket, rb_ref, h), bias)
            cols = jnp.where(jnp.logical_and(hrow == h, lane == 0), sink_ref[h], cols)
            cols = jnp.where(jnp.logical_and(hrow == h, lane == 1), rb_ref[0, h], cols)
        bias_scr[...] = jnp.where(lane >= 1, bias, NEG_INF)
        col_scr[...] = cols

    bias_c = bias_scr[...]
    sink = col_scr[:, 0:1]
    bias_n = col_scr[:, 1:2]
    same_group = (hrow // GQA) == (lane // HEAD_DIM)
    low_group = lax.broadcasted_iota(jnp.int32, (ATT_HEADS, HEAD_DIM), 0) < GQA
    for b in range(ATT_S_BB):
        row = att_ref[b:b + 1, :]
        q = row[:, :ATT_WIDTH] * (HEAD_DIM ** -0.5)
        kn = row[:, ATT_WIDTH:ATT_WIDTH + KV_WIDTH]
        vn = row[:, ATT_WIDTH + KV_WIDTH:]
        qh = jnp.concatenate([q[:, h * HEAD_DIM:(h + 1) * HEAD_DIM] for h in range(ATT_HEADS)], axis=0)
        q_bd = jnp.where(same_group, jnp.concatenate([qh, qh], axis=1), 0.0)
        rnd = lambda a: a.astype(BF16).astype(F32)
        s_c = _mm_nt(q_bd, ck_ref[b]) + bias_c
        s_n = jnp.sum(rnd(q_bd) * rnd(kn), axis=-1, keepdims=True) + bias_n
        m = jnp.maximum(jnp.maximum(jnp.max(s_c, axis=-1, keepdims=True), s_n), sink)
        p_c = jnp.exp(s_c - m)
        p_n = jnp.exp(s_n - m)
        den = jnp.sum(p_c, axis=-1, keepdims=True) + p_n + jnp.exp(sink - m)
        o_full = _mm(p_c / den, cv_ref[b]) + rnd(p_n / den) * rnd(vn)
        o_sel = jnp.where(low_group, o_full[:, :HEAD_DIM], o_full[:, HEAD_DIM:])
        o_ref[b:b + 1, :] = jnp.concatenate([o_sel[h:h + 1, :] for h in range(ATT_HEADS)], axis=1)


def _attn_sample(att, ck, cv, bucket, rel_bias, sink):
    nseq = att.shape[0]
    smem = pl.BlockSpec(memory_space=pltpu.SMEM)
    cache = pl.BlockSpec((ATT_S_BB, WINDOW, KV_WIDTH), lambda i: (i, 0, 0))
    return pl.pallas_call(
        _attn_sample_kernel,
        grid=(nseq // ATT_S_BB,),
        in_specs=[pl.BlockSpec((ATT_S_BB, ATT_COLS), lambda i: (i, 0)), cache, cache,
                  pl.BlockSpec(bucket.shape, lambda i: (0, 0)), smem, smem],
        out_specs=pl.BlockSpec((ATT_S_BB, ATT_WIDTH), lambda i: (i, 0)),
        out_shape=jax.ShapeDtypeStruct((nseq, ATT_WIDTH), F32),
        scratch_shapes=[pltpu.VMEM((ATT_HEADS, LANES), F32), pltpu.VMEM((ATT_HEADS, LANES), F32)],
        compiler_params=_params("arbitrary"),
        name="attn_sample",
    )(att, ck, cv, bucket, rel_bias, sink)


GDN_TB = 128
GDN_NC = GDN_TB // DN_CHUNK
TAIL = 8


def _gdn_gates(ba, alog, dtb):
    beta = _sigmoid(ba)
    g = -jnp.exp(alog) * _softplus(ba + dtb)
    return beta, g


PAIR = 2 * DN_DK
N_PAIRS = DN_WIDTH // PAIR


def _pair_diag(x, lo):
    xb = x.astype(BF16)
    zero = jnp.zeros_like(xb)
    return jnp.concatenate([jnp.where(lo, xb, zero), jnp.where(lo, zero, xb)], axis=0)


def _gdn_prompt_kernel(xc_ref, dz_ref, ba_ref, cw_ref, alog_ref, dtb_ref, dnx_ref,
                       hsum_ref, expb_ref, expg_ref, ltri_ref,
                       o_ref, s_out_ref, xp_scr, s_scr):
    i = pl.program_id(0)
    nb = xc_ref.shape[0]

    @pl.when(i == 0)
    def _():
        xp_scr[:, 0:TAIL, :] = jnp.zeros((nb, TAIL, CONV_CH), F32)
        s_scr[...] = jnp.zeros(s_scr.shape, F32)

    hsum = hsum_ref[...]
    ri = lax.broadcasted_iota(jnp.int32, (DN_CHUNK, PAIR), 0)
    ci = lax.broadcasted_iota(jnp.int32, (DN_CHUNK, PAIR), 1)
    lo = ci < DN_DK
    cj = jnp.where(lo, ci, ci - DN_DK)
    causal = ri >= cj
    strict = ri > cj
    eye = (ri == cj).astype(F32)

    def sel2(x, m):
        hi = x.astype(BF16)
        lw = (x - hi.astype(F32)).astype(BF16)
        return (jnp.dot(hi, m, preferred_element_type=F32) + jnp.dot(lw, m, preferred_element_type=F32))

    pre = []
    for b in range(nb):
        xc = xc_ref[b]
        xp_scr[b, TAIL:, :] = xc
        y = xp_scr[b, TAIL - 3:TAIL - 3 + GDN_TB, :] * cw_ref[0:1, :]
        y = y + xp_scr[b, TAIL - 2:TAIL - 2 + GDN_TB, :] * cw_ref[1:2, :]
        y = y + xp_scr[b, TAIL - 1:TAIL - 1 + GDN_TB, :] * cw_ref[2:3, :]
        y = y + xc * cw_ref[3:4, :]
        xp_scr[b, 0:TAIL, :] = xc[GDN_TB - TAIL:, :]
        y = _silu(y)
        q = y[:, :DN_WIDTH]
        k = y[:, DN_WIDTH:2 * DN_WIDTH]
        v = y[:, 2 * DN_WIDTH:]
        q = q * lax.rsqrt(sel2(q * q, hsum) + EPS) * (DN_DK ** -0.5)
        k = k * lax.rsqrt(sel2(k * k, hsum) + EPS)
        beta_c, g_c = _gdn_gates(ba_ref[b], alog_ref[...], dtb_ref[...])
        beta = sel2(beta_c, expb_ref[...])
        gam_c = _mm_sel_lhs(ltri_ref[...], g_c)
        gam = _mm_sel_rhs(gam_c, expg_ref[...])
        gam_t = gam_c.T
        kb = k * beta
        egam = jnp.exp(gam)
        pre.append(dict(q=q, k=k, kb=kb, vb=v * beta, qg=q * egam, wr=kb * egam, gam=gam, gam_t=gam_t))

    probs = [(b, p) for b in range(nb) for p in range(N_PAIRS)]
    pick = lambda m: jnp.where(lo, m[:DN_DK], m[DN_DK:])
    o_rows = [[] for _ in range(nb)]
    for c in range(GDN_NC):
        r0, r1 = c * DN_CHUNK, (c + 1) * DN_CHUNK
        sl = lambda name, b, p: pre[b][name][r0:r1, p * PAIR:(p + 1) * PAIR]
        raws = []
        for b, p in probs:
            k_p = sl("k", b, p)
            k_rows = jnp.concatenate([jnp.where(lo, k_p, 0.0), jnp.where(lo, 0.0, k_p)], axis=0)
            raws.append(_mm_nt(jnp.concatenate([sl("kb", b, p), sl("q", b, p)], axis=0), k_rows))
        pws, ts, qks = [], [], []
        for (b, p), raw in zip(probs, raws):
            gcol = sl("gam", b, p)
            h0 = DN_HEADS + 2 * p
            gam_t = pre[b]["gam_t"]
            grow = jnp.concatenate([gam_t[h0:h0 + 1, r0:r1], gam_t[h0 + 1:h0 + 2, r0:r1]], axis=1)
            decay = jnp.exp(jnp.where(causal, gcol - grow, NEG_INF))
            a = jnp.where(strict, raw[:DN_CHUNK] * decay, 0.0)
            qks.append(jnp.where(causal, raw[DN_CHUNK:] * decay, 0.0))
            pws.append(-a)
            ts.append(eye - a)
        pws = [_mm(pw, _pair_diag(pw, lo)) for pw in pws]
        for _ in range(4):
            rs = [_mm(jnp.concatenate([pw, t], axis=0), _pair_diag(pw, lo)) for pw, t in zip(pws, ts)]
            pws = [r[:DN_CHUNK] for r in rs]
            ts = [t + r[DN_CHUNK:] for t, r in zip(ts, rs)]
        rs = [_mm(t, _pair_diag(pw, lo)) for pw, t in zip(pws, ts)]
        ts = [t + r for t, r in zip(ts, rs)]
        sols = [_mm(t, jnp.concatenate([_pair_diag(sl("vb", b, p), lo), _pair_diag(sl("wr", b, p), lo)],
                                       axis=1)) for (b, p), t in zip(probs, ts)]
        qkuws = [_mm(qk, jnp.concatenate([_pair_diag(s[:, :PAIR], lo), _pair_diag(s[:, PAIR:], lo)], axis=1))
                 for qk, s in zip(qks, sols)]
        crosses, gls = [], []
        for (b, p), s in zip(probs, sols):
            gam_last = pre[b]["gam"][r1 - 1:r1, p * PAIR:(p + 1) * PAIR]
            kd = sl("k", b, p) * jnp.exp(gam_last - sl("gam", b, p))
            crosses.append(_mm_tn(kd, s))
            gls.append(jnp.exp(gam_last))
        lhs = [jnp.concatenate([pick(cr[:, PAIR:]), sl("qg", b, p) - qkuw[:, PAIR:]], axis=0)
               for (b, p), cr, qkuw in zip(probs, crosses, qkuws)]
        s_olds = [s_scr[b, p] for b, p in probs]
        rs = [_mm(l, _pair_diag(s_old, lo)) for l, s_old in zip(lhs, s_olds)]
        o_pairs = [[] for _ in range(nb)]
        for (b, p), r, s_old, gl, cr, qkuw in zip(probs, rs, s_olds, gls, crosses, qkuws):
            s_scr[b, p] = gl * s_old - r[:DN_DK] + pick(cr[:, :PAIR])
            o_pairs[b].append(r[DN_DK:] + qkuw[:, :PAIR])
        for b in range(nb):
            o_rows[b].append(jnp.concatenate(o_pairs[b], axis=1))

    for b in range(nb):
        o = jnp.concatenate(o_rows[b], axis=0)
        ms = sel2(o * o, hsum) * (1.0 / DN_DV)
        o = o * lax.rsqrt(ms + EPS) * dnx_ref[...]
        o_ref[b] = o * _silu(dz_ref[b])

    @pl.when(i == pl.num_programs(0) - 1)
    def _():
        for b in range(nb):
            for p in range(N_PAIRS):
                s_p = s_scr[b, p]
                s_out_ref[b, 2 * p] = s_p[:, :DN_DV]
                s_out_ref[b, 2 * p + 1] = s_p[:, DN_DV:]


def _gdn_consts():
    lane = np.arange(DN_WIDTH)
    hsum = (lane[:, None] // DN_DV == lane[None, :] // DN_DV)
    src = np.arange(LANES)
    expb = (src[:, None] == lane[None, :] // DN_DV)
    expg = (src[:, None] == DN_HEADS + lane[None, :] // DN_DV)
    tok = np.arange(GDN_TB)
    ltri = np.logical_and(tok[:, None] >= tok[None, :],
                          tok[:, None] // DN_CHUNK == tok[None, :] // DN_CHUNK)
    as_bf16 = lambda m: jnp.asarray(m.astype(np.float32), dtype=BF16)
    return as_bf16(hsum), as_bf16(expb), as_bf16(expg), as_bf16(ltri)


def _gdn_prompt(xc, dz, ba, conv_w, alog, dtb, dnx, batch, seq):
    nt = seq // GDN_TB
    hsum, expb, expg, ltri = _gdn_consts()
    row = lambda n: pl.BlockSpec((batch, GDN_TB, n), lambda i: (0, i, 0))
    full = lambda a: pl.BlockSpec(a.shape, lambda i: (0,) * a.ndim)
    consts = (conv_w, alog, dtb, dnx, hsum, expb, expg, ltri)
    as3d = lambda a: a.reshape(batch, seq, a.shape[-1])
    o, s = pl.pallas_call(
        _gdn_prompt_kernel,
        grid=(nt,),
        in_specs=[row(CONV_CH), row(DN_WIDTH), row(LANES)] + [full(a) for a in consts],
        out_specs=[row(DN_WIDTH),
                   pl.BlockSpec((batch, DN_HEADS, DN_DK, DN_DV), lambda i: (0, 0, 0, 0))],
        out_shape=[jax.ShapeDtypeStruct((batch, seq, DN_WIDTH), F32),
                   jax.ShapeDtypeStruct((batch, DN_HEADS, DN_DK, DN_DV), F32)],
        scratch_shapes=[pltpu.VMEM((batch, TAIL + GDN_TB, CONV_CH), F32),
                        pltpu.VMEM((batch, N_PAIRS, DN_DK, PAIR), F32)],
        compiler_params=_params("arbitrary"),
        name="gdn_prompt",
    )(as3d(xc), as3d(dz), as3d(ba), *consts)
    return o.reshape(batch * seq, DN_WIDTH), s


GDN_S_BB = 8


def _gdn_sample_kernel(xc_ref, dz_ref, ba_ref, sc_ref, s_ref, cw_ref, alog_ref, dtb_ref, dn_ref,
                       hsum_ref, eye_ref, hsel_ref, hrep_ref, o_ref, s_out_ref):
    xc = xc_ref[...]
    y = sc_ref[0] * cw_ref[0:1, :]
    y = y + sc_ref[1] * cw_ref[1:2, :]
    y = y + sc_ref[2] * cw_ref[2:3, :]
    y = _silu(y + xc * cw_ref[3:4, :])
    hsum = hsum_ref[...]
    q = y[:, :DN_WIDTH]
    k = y[:, DN_WIDTH:2 * DN_WIDTH]
    v = y[:, 2 * DN_WIDTH:]
    q = q * lax.rsqrt(_mm_sel_rhs(q * q, hsum) + EPS) * (DN_DK ** -0.5)
    k = k * lax.rsqrt(_mm_sel_rhs(k * k, hsum) + EPS)
    beta_c, g_c = _gdn_gates(ba_ref[...], alog_ref[...], dtb_ref[...])
    eg_c = jnp.exp(g_c)
    eye = eye_ref[...]
    tr = lambda a: lax.dot_general(a, eye, (((0,), (0,)), ((), ())), precision=lax.Precision.HIGHEST,
                                   preferred_element_type=F32)
    k_t = tr(k)
    q_t = tr(q)
    beta_t = tr(beta_c)
    eg_t = tr(eg_c)
    dz = dz_ref[...]
    dn = dn_ref[...]
    split = lambda r: jnp.concatenate([r[:, h * DN_DV:(h + 1) * DN_DV] for h in range(DN_HEADS)], axis=0)
    hsel = hsel_ref[...]
    hrep = hrep_ref[...]
    for b in range(GDN_S_BB):
        s2 = s_ref[b]
        kc = k_t[:, b:b + 1]
        qc = q_t[:, b:b + 1]
        beta = beta_t[0:DN_HEADS, b:b + 1]
        eg = eg_t[DN_HEADS:2 * DN_HEADS, b:b + 1]
        vh = split(v[b:b + 1, :])
        qh = split(q[b:b + 1, :])
        kh = split(k[b:b + 1, :])
        ks = _mm_sel_lhs(hsel, kc * s2)
        v_new = beta * (vh - eg * ks)
        qs = _mm_sel_lhs(hsel, qc * s2)
        qk = jnp.sum(qh * kh, axis=-1, keepdims=True)
        o = eg * qs + qk * v_new
        rep = _mm_sel_lhs(hrep, jnp.concatenate(
            [v_new, jnp.broadcast_to(eg, (DN_HEADS, DN_DV))], axis=1))
        s_out_ref[b] = s2 * rep[:, DN_DV:] + kc * rep[:, :DN_DV]
        o = _rmsnorm(o, dn) * _silu(split(dz[b:b + 1, :]))
        o_ref[b] = o


def _gdn_sample(xc, dz, ba, sconv_t, state, conv_w, alog, dtb, dn):
    nseq = xc.shape[0]
    lane = np.arange(DN_WIDTH)
    hsum = jnp.asarray((lane[:, None] // DN_DV == lane[None, :] // DN_DV).astype(np.float32), dtype=BF16)
    eye = jnp.eye(GDN_S_BB, dtype=F32)
    hsel_np = (np.arange(DN_HEADS)[:, None] == lane[None, :] // DN_DK).astype(np.float32)
    hsel = jnp.asarray(hsel_np, dtype=BF16)
    hrep = jnp.asarray(hsel_np.T, dtype=BF16)
    row = lambda n: pl.BlockSpec((GDN_S_BB, n), lambda i: (i, 0))
    full = lambda a: pl.BlockSpec(a.shape, lambda i: (0,) * a.ndim)
    st = pl.BlockSpec((GDN_S_BB, DN_HEADS * DN_DK, DN_DV), lambda i: (i, 0, 0))
    consts = (conv_w, alog, dtb, dn, hsum, eye, hsel, hrep)
    return pl.pallas_call(
        _gdn_sample_kernel,
        grid=(nseq // GDN_S_BB,),
        in_specs=[row(CONV_CH), row(DN_WIDTH), row(LANES),
                  pl.BlockSpec((CONV_WIDTH - 1, GDN_S_BB, CONV_CH), lambda i: (0, i, 0)), st]
                 + [full(a) for a in consts],
        out_specs=[pl.BlockSpec((GDN_S_BB, DN_HEADS, DN_DV), lambda i: (i, 0, 0)), st],
        out_shape=[jax.ShapeDtypeStruct((nseq, DN_HEADS, DN_DV), F32),
                   jax.ShapeDtypeStruct(state.shape, F32)],
        compiler_params=_params("parallel"),
        name="gdn_sample",
    )(xc, dz, ba, sconv_t, state, *consts)


def _route(xn, wr):
    logits = jnp.dot(xn, wr, preferred_element_type=F32)
    lane = lax.broadcasted_iota(jnp.int32, logits.shape, 1).astype(F32)
    first_at = lambda hit: jnp.min(jnp.where(hit, lane, float(LANES)), axis=-1, keepdims=True)
    glog = jnp.where(lane < N_GROUPS, logits, NEG_INF)
    gmax = jnp.max(glog, axis=-1, keepdims=True)
    gsel = first_at(glog == gmax)
    pgsel = 1.0 / jnp.sum(jnp.exp(glog - gmax), axis=-1, keepdims=True)
    lo = ROUTER_OFF + gsel * EXPERTS_PER_GROUP
    in_group = jnp.logical_and(lane >= lo, lane < lo + EXPERTS_PER_GROUP)
    elog = jnp.where(in_group, logits, NEG_INF)
    m1 = jnp.max(elog, axis=-1, keepdims=True)
    i1 = first_at(elog == m1)
    z = jnp.sum(jnp.exp(elog - m1), axis=-1, keepdims=True)
    elog2 = jnp.where(lane == i1, NEG_INF, elog)
    m2 = jnp.max(elog2, axis=-1, keepdims=True)
    i2 = first_at(elog2 == m2)
    p1 = 1.0 / z
    p2 = jnp.exp(m2 - m1) / z
    tot = p1 + p2
    return lane, i1, i2, p1 / tot * pgsel, p2 / tot * pgsel


def _outproj(x_ref, oa_ref, od_ref, wo_ref):
    return x_ref[...] + _mm(oa_ref[...], wo_ref[:ATT_WIDTH, :]) + _mm(od_ref[...], wo_ref[ATT_WIDTH:, :])


def _outproj_router_kernel(x_ref, oa_ref, od_ref, wo_ref, g_ref, wr_ref, h_ref, xn_ref, gate_ref):
    h = _outproj(x_ref, oa_ref, od_ref, wo_ref)
    h_ref[...] = h
    xn = _rmsnorm(h, g_ref[...]).astype(BF16)
    xn_ref[...] = xn
    lane, i1, i2, g1, g2 = _route(xn, wr_ref[...])
    gate_ref[...] = jnp.where(lane == i1, g1, 0.0) + jnp.where(lane == i2, g2, 0.0)


def _outproj_router(x, oa, od, wo, g, wr):
    t = x.shape[0]
    tm = min(t, 256)
    row = lambda n: pl.BlockSpec((tm, n), lambda i: (i, 0))
    full = lambda a: pl.BlockSpec(a.shape, lambda i: (0,) * a.ndim)
    return pl.pallas_call(
        _outproj_router_kernel,
        grid=(t // tm,),
        in_specs=[row(D_MODEL), row(ATT_WIDTH), row(DN_WIDTH), full(wo), full(g), full(wr)],
        out_specs=[row(D_MODEL), row(D_MODEL), row(LANES)],
        out_shape=[jax.ShapeDtypeStruct((t, D_MODEL), F32), jax.ShapeDtypeStruct((t, D_MODEL), BF16),
                   jax.ShapeDtypeStruct((t, LANES), F32)],
        compiler_params=_params("parallel"),
        name="outproj_router",
    )(x, oa, od, wo, g, wr)


def _moe_kernel(xn_ref, gate_ref, wg_ref, wu_ref, wd_ref, o_ref):
    e = pl.program_id(1)
    xn = xn_ref[...]
    lane = lax.broadcasted_iota(jnp.int32, gate_ref.shape, 1)
    gate = jnp.sum(jnp.where(lane == e + ROUTER_OFF, gate_ref[...], 0.0), axis=-1, keepdims=True)
    hg = jnp.dot(xn, wg_ref[...].astype(BF16), preferred_element_type=F32)
    hu = jnp.dot(xn, wu_ref[...].astype(BF16), preferred_element_type=F32)
    hm = _silu(hg) * hu * gate
    y = jnp.dot(hm.astype(BF16), wd_ref[...].astype(BF16), preferred_element_type=F32)

    @pl.when(e == 0)
    def _():
        o_ref[...] = y

    @pl.when(e > 0)
    def _():
        o_ref[...] += y


def _moe(xn, gates, wg, wu, wd):
    t = xn.shape[0]
    tm = min(t, 1024)
    return pl.pallas_call(
        _moe_kernel,
        grid=(t // tm, N_EXPERTS),
        in_specs=[pl.BlockSpec((tm, D_MODEL), lambda i, e: (i, 0)),
                  pl.BlockSpec((tm, LANES), lambda i, e: (i, 0)),
                  pl.BlockSpec((None, D_MODEL, D_EXPERT), lambda i, e: (e, 0, 0)),
                  pl.BlockSpec((None, D_MODEL, D_EXPERT), lambda i, e: (e, 0, 0)),
                  pl.BlockSpec((None, D_EXPERT, D_MODEL), lambda i, e: (e, 0, 0))],
        out_specs=pl.BlockSpec((tm, D_MODEL), lambda i, e: (i, 0)),
        out_shape=jax.ShapeDtypeStruct((t, D_MODEL), F32),
        compiler_params=_params("parallel", "arbitrary"),
        name="moe",
    )(xn, gates, wg, wu, wd)


MOE_TM = 256
POS_TM = 512
INFO_G1, INFO_G2, INFO_E1, INFO_E2 = 0, 1, 2, 3
DMA_UNROLL = 8


def _moe_tiles(t):
    return (2 * t) // MOE_TM + N_EXPERTS


def _route_kernel(x_ref, oa_ref, od_ref, wo_ref, g_ref, wr_ref, h_ref, xn_ref, info_ref):
    h = _outproj(x_ref, oa_ref, od_ref, wo_ref)
    h_ref[...] = h
    xn = _rmsnorm(h, g_ref[...])
    xn_ref[...] = xn
    lane, i1, i2, g1, g2 = _route(xn.astype(BF16), wr_ref[...])
    info = jnp.where(lane == INFO_G1, g1, 0.0) + jnp.where(lane == INFO_G2, g2, 0.0)
    info = info + jnp.where(lane == INFO_E1, i1, 0.0) + jnp.where(lane == INFO_E2, i2, 0.0)
    info_ref[...] = info


def _route_sparse(x, oa, od, wo, g, wr):
    t = x.shape[0]
    tm = ROW_TM
    row = lambda n: pl.BlockSpec((tm, n), lambda i: (i, 0))
    full = lambda a: pl.BlockSpec(a.shape, lambda i: (0,) * a.ndim)
    return pl.pallas_call(
        _route_kernel,
        grid=(t // tm,),
        in_specs=[row(D_MODEL), row(ATT_WIDTH), row(DN_WIDTH), full(wo), full(g), full(wr)],
        out_specs=[row(D_MODEL), row(D_MODEL), row(LANES)],
        out_shape=[jax.ShapeDtypeStruct((t, D_MODEL), F32), jax.ShapeDtypeStruct((t, D_MODEL), F32),
                   jax.ShapeDtypeStruct((t, LANES), F32)],
        compiler_params=_params("parallel"),
        name="route",
    )(x, oa, od, wo, g, wr)


def _positions_kernel(info_ref, ltri_ref, utri_ref, pos_ref, cnt_ref, run_scr, off_scr):
    phase = pl.program_id(0)
    i = pl.program_id(1)
    info = info_ref[...]
    lane = lax.broadcasted_iota(jnp.int32, info.shape, 1).astype(F32)
    hit1 = lane == info[:, INFO_E1:INFO_E1 + 1]
    hit2 = lane == info[:, INFO_E2:INFO_E2 + 1]
    onehot = jnp.logical_or(hit1, hit2).astype(F32)

    @pl.when(jnp.logical_and(phase == 0, i == 0))
    def _():
        run_scr[...] = jnp.zeros(run_scr.shape, F32)

    @pl.when(jnp.logical_and(phase == 1, i == 0))
    def _():
        cnt = run_scr[...]
        cnt_ref[...] = cnt
        tiles = jnp.floor((cnt + (MOE_TM - 1)) * (1.0 / MOE_TM))
        off_scr[...] = MOE_TM * jnp.dot(tiles.astype(BF16), utri_ref[...], preferred_element_type=F32)
        run_scr[...] = jnp.zeros(run_scr.shape, F32)

    @pl.when(phase == 0)
    def _():
        pos_ref[...] = jnp.zeros(pos_ref.shape, jnp.int32)

    @pl.when(phase == 1)
    def _():
        before = (jnp.dot(ltri_ref[...], onehot.astype(BF16), preferred_element_type=F32)
                  + run_scr[...] + off_scr[...])
        pos1 = jnp.sum(jnp.where(hit1, before, 0.0), axis=-1, keepdims=True)
        pos2 = jnp.sum(jnp.where(hit2, before, 0.0), axis=-1, keepdims=True)
        pos_ref[...] = (jnp.where(lane == 0, pos1, 0.0) + jnp.where(lane == 1, pos2, 0.0)).astype(jnp.int32)

    run_scr[...] += jnp.sum(onehot, axis=0, keepdims=True)


def _positions(info):
    t = info.shape[0]
    tm = min(t, POS_TM)
    tok = np.arange(tm)
    ltri = jnp.asarray((tok[:, None] > tok[None, :]).astype(np.float32), dtype=BF16)
    ln = np.arange(LANES)
    utri = jnp.asarray((ln[:, None] < ln[None, :]).astype(np.float32), dtype=BF16)
    full = lambda a: pl.BlockSpec(a.shape, lambda ph, i: (0,) * a.ndim)
    return pl.pallas_call(
        _positions_kernel,
        grid=(2, t // tm),
        in_specs=[pl.BlockSpec((tm, LANES), lambda ph, i: (i, 0)), full(ltri), full(utri)],
        out_specs=[pl.BlockSpec((tm, LANES), lambda ph, i: (i * ph, 0)),
                   pl.BlockSpec((1, LANES), lambda ph, i: (0, 0))],
        out_shape=[jax.ShapeDtypeStruct((t, LANES), jnp.int32), jax.ShapeDtypeStruct((1, LANES), F32)],
        scratch_shapes=[pltpu.VMEM((1, LANES), F32), pltpu.VMEM((1, LANES), F32)],
        compiler_params=_params("arbitrary", "arbitrary"),
        name="positions",
    )(info, ltri, utri)


def _row_copy(src_hbm, src_row, dst_hbm, dst_row, sem):
    return pltpu.make_async_copy(src_hbm.at[pl.ds(src_row, 1)], dst_hbm.at[pl.ds(dst_row, 1)], sem)


SCATTER_SLOTS = 3


def _scatter_kernel(pos1_ref, pos2_ref, last_ref, used_ref, nt_ref, xn_hbm, zero_hbm, xs_hbm,
                    buf, lsem, sem, zsem, *, n_tok):
    max_tiles = xs_hbm.shape[0] // MOE_TM

    def zero_tile(tile):
        return pltpu.make_async_copy(zero_hbm, xs_hbm.at[pl.ds(tile * MOE_TM, MOE_TM)], zsem)

    def for_unused(fn):
        def body(tile, carry):
            fn(tile)
            return carry
        lax.fori_loop(nt_ref[0], max_tiles, body, 0)

    for e in range(N_EXPERTS):
        @pl.when(used_ref[e] > 0)
        def _():
            zero_tile(last_ref[e]).start()
    for_unused(lambda tile: zero_tile(tile).start())
    for e in range(N_EXPERTS):
        @pl.when(used_ref[e] > 0)
        def _():
            zero_tile(last_ref[e]).wait()
    for_unused(lambda tile: zero_tile(tile).wait())

    tm = buf.shape[1]
    n = n_tok // tm

    def load(i):
        return pltpu.make_async_copy(xn_hbm.at[pl.ds(i * tm, tm)], buf.at[i % SCATTER_SLOTS],
                                     lsem.at[i % SCATTER_SLOTS])

    def wait_rows(slot):
        pltpu.make_async_copy(xs_hbm.at[pl.ds(0, 2 * tm)], xs_hbm.at[pl.ds(0, 2 * tm)], sem.at[slot]).wait()

    load(0).start()
    load(1).start()

    def step(i, carry):
        slot = i % SCATTER_SLOTS
        load(i).wait()

        def body(j, c2):
            tok = i * tm + j
            src = buf.at[slot, pl.ds(j, 1)]
            pltpu.make_async_copy(src, xs_hbm.at[pl.ds(pos1_ref[tok], 1)], sem.at[slot]).start()
            pltpu.make_async_copy(src, xs_hbm.at[pl.ds(pos2_ref[tok], 1)], sem.at[slot]).start()
            return c2
        lax.fori_loop(0, tm, body, 0, unroll=DMA_UNROLL)

        @pl.when(i >= 1)
        def _():
            wait_rows((i - 1) % SCATTER_SLOTS)

        @pl.when(i + 2 < n)
        def _():
            load(i + 2).start()
        return carry
    lax.fori_loop(0, n, step, 0)
    wait_rows((n - 1) % SCATTER_SLOTS)


def _scatter_rows(xn, pos1, pos2, last_tile, used, n_tiles, n_rows):
    t = xn.shape[0]
    zero = jnp.zeros((MOE_TM, D_MODEL), F32)
    any_spec = pl.BlockSpec(memory_space=pl.ANY)
    return pl.pallas_call(
        functools.partial(_scatter_kernel, n_tok=t),
        grid_spec=pltpu.PrefetchScalarGridSpec(
            num_scalar_prefetch=5, grid=(1,),
            in_specs=[any_spec, any_spec], out_specs=any_spec,
            scratch_shapes=[pltpu.VMEM((SCATTER_SLOTS, MOE_TM, D_MODEL), F32),
                            pltpu.SemaphoreType.DMA((SCATTER_SLOTS,)),
                            pltpu.SemaphoreType.DMA((SCATTER_SLOTS,)),
                            pltpu.SemaphoreType.DMA]),
        out_shape=jax.ShapeDtypeStruct((n_rows, D_MODEL), F32),
        compiler_params=_params("arbitrary"),
        name="scatter_rows",
    )(pos1, pos2, last_tile, used, n_tiles, xn, zero)


def _experts_kernel(te_ref, tv_ref, nt_ref, xs_ref, wg_ref, wu_ref, wd_ref, ys_ref, wg_s, wu_s, wd_s):
    i = pl.program_id(0)
    used = i < nt_ref[0]

    @pl.when(jnp.logical_or(i == 0, te_ref[i] != te_ref[jnp.maximum(i - 1, 0)]))
    def _():
        wg_s[...] = wg_ref[...].astype(BF16)
        wu_s[...] = wu_ref[...].astype(BF16)
        wd_s[...] = wd_ref[...].astype(BF16)

    @pl.when(used)
    def _():
        row = lax.broadcasted_iota(jnp.int32, xs_ref.shape, 0)
        x = jnp.where(row < tv_ref[i], xs_ref[...], 0.0).astype(BF16)
        hg = jnp.dot(x, wg_s[...], preferred_element_type=F32)
        hu = jnp.dot(x, wu_s[...], preferred_element_type=F32)
        hm = (_silu(hg) * hu).astype(BF16)
        ys_ref[...] = jnp.dot(hm, wd_s[...], preferred_element_type=F32)

    @pl.when(jnp.logical_not(used))
    def _():
        ys_ref[...] = jnp.zeros(ys_ref.shape, F32)


def _experts(xs, tile_expert, tile_valid, n_tiles, wg, wu, wd):
    max_tiles = xs.shape[0] // MOE_TM
    rows = pl.BlockSpec((MOE_TM, D_MODEL), lambda i, te, tv, nt: (i, 0))
    wspec = lambda shape: pl.BlockSpec((None,) + shape, lambda i, te, tv, nt: (te[i], 0, 0))
    return pl.pallas_call(
        _experts_kernel,
        grid_spec=pltpu.PrefetchScalarGridSpec(
            num_scalar_prefetch=3, grid=(max_tiles,),
            in_specs=[rows, wspec((D_MODEL, D_EXPERT)), wspec((D_MODEL, D_EXPERT)),
                      wspec((D_EXPERT, D_MODEL))],
            out_specs=rows,
            scratch_shapes=[pltpu.VMEM((D_MODEL, D_EXPERT), BF16), pltpu.VMEM((D_MODEL, D_EXPERT), BF16),
                            pltpu.VMEM((D_EXPERT, D_MODEL), BF16)]),
        out_shape=jax.ShapeDtypeStruct(xs.shape, F32),
        compiler_params=_params("arbitrary"),
        name="experts",
    )(tile_expert, tile_valid, n_tiles, xs, wg, wu, wd)


def _ple_gather_kernel(pos1_ref, pos2_ref, h_ref, info_ref, p_ref, wpp_ref, wpg_ref, gp_ref, gf_ref,
                       ys_hbm, y_ref, ybuf, sem):
    i = pl.program_id(0)
    n = pl.num_programs(0)
    tm = h_ref.shape[0]

    def issue(tile, slot):
        def body(j, carry):
            tok = tile * tm + j
            pltpu.make_async_copy(ys_hbm.at[pl.ds(pos1_ref[tok], 1)], ybuf.at[slot, 0, pl.ds(j, 1)],
                                  sem.at[slot]).start()
            pltpu.make_async_copy(ys_hbm.at[pl.ds(pos2_ref[tok], 1)], ybuf.at[slot, 1, pl.ds(j, 1)],
                                  sem.at[slot]).start()
            return carry
        lax.fori_loop(0, tm, body, 0, unroll=DMA_UNROLL)

    @pl.when(i == 0)
    def _():
        issue(0, 0)

    @pl.when(i + 1 < n)
    def _():
        issue(i + 1, (i + 1) % 2)

    slot = i % 2
    pltpu.make_async_copy(ybuf.at[slot], ybuf.at[slot], sem.at[slot]).wait()
    info = info_ref[...]
    moe = info[:, INFO_G1:INFO_G1 + 1] * ybuf[slot, 0] + info[:, INFO_G2:INFO_G2 + 1] * ybuf[slot, 1]
    h = h_ref[...] + moe
    hn = _rmsnorm(h, gp_ref[...])
    h = h + _mm(p_ref[...], wpp_ref[...]) * _sigmoid(_mm(hn, wpg_ref[...]))
    y_ref[...] = _rmsnorm(h, gf_ref[...])


def _ple_gather(h, info, p, ys, pos1, pos2, wpp, wpg, gp, gf):
    t = h.shape[0]
    tm = 256
    row = lambda n: pl.BlockSpec((tm, n), lambda i, p1, p2: (i, 0))
    full = lambda a: pl.BlockSpec(a.shape, lambda i, p1, p2: (0,) * a.ndim)
    return pl.pallas_call(
        _ple_gather_kernel,
        grid_spec=pltpu.PrefetchScalarGridSpec(
            num_scalar_prefetch=2, grid=(t // tm,),
            in_specs=[row(D_MODEL), row(LANES), row(PLE_DIM), full(wpp), full(wpg), full(gp), full(gf),
                      pl.BlockSpec(memory_space=pl.ANY)],
            out_specs=row(D_MODEL),
            scratch_shapes=[pltpu.VMEM((2, 2, tm, D_MODEL), F32), pltpu.SemaphoreType.DMA((2,))]),
        out_shape=jax.ShapeDtypeStruct((t, D_MODEL), F32),
        compiler_params=_params("arbitrary"),
        name="ple_gather",
    )(pos1, pos2, h, info, p, wpp, wpg, gp, gf, ys)


SC_IDX = 128
SC_ROWS = 32
SC_WORKERS = 32


def _sc_mesh():
    return plsc.VectorSubcoreMesh(core_axis_name="c", subcore_axis_name="s")


def _sc_windows(t, fn):
    per_worker = t // SC_WORKERS
    worker = lax.axis_index(("c", "s"))

    @pl.loop(0, per_worker // SC_IDX)
    def _(w):
        fn(worker * per_worker + w * SC_IDX)


def _sc_scatter_rows(xn, pos1, pos2, n_rows):
    t, d = xn.shape
    assert t % (SC_WORKERS * SC_IDX) == 0
    idx_t = pltpu.VMEM((1, SC_IDX), jnp.int32)

    @pl.kernel(out_type=jax.ShapeDtypeStruct((n_rows, d), xn.dtype), mesh=_sc_mesh(),
               scratch_types=[idx_t, idx_t, pltpu.VMEM((SC_ROWS, d), xn.dtype)])
    def scatter(x_hbm, p1_hbm, p2_hbm, o_hbm, i1_v, i2_v, buf):
        def window(base):
            pltpu.sync_copy(p1_hbm.at[:, pl.ds(base, SC_IDX)], i1_v)
            pltpu.sync_copy(p2_hbm.at[:, pl.ds(base, SC_IDX)], i2_v)
            for k in range(SC_IDX // SC_ROWS):
                pltpu.sync_copy(x_hbm.at[pl.ds(base + k * SC_ROWS, SC_ROWS)], buf)
                pltpu.sync_copy(buf, o_hbm.at[i1_v.at[0, pl.ds(k * SC_ROWS, SC_ROWS)]])
                pltpu.sync_copy(buf, o_hbm.at[i2_v.at[0, pl.ds(k * SC_ROWS, SC_ROWS)]])
        _sc_windows(t, window)

    return scatter(xn, pos1.reshape(1, t), pos2.reshape(1, t))


def _sc_gather_rows(ys, pos1, pos2):
    t = pos1.shape[0]
    d = ys.shape[1]
    assert t % (SC_WORKERS * SC_IDX) == 0
    idx_t = pltpu.VMEM((1, SC_IDX), jnp.int32)
    out = jax.ShapeDtypeStruct((t, d), ys.dtype)

    @pl.kernel(out_type=(out, out), mesh=_sc_mesh(),
               scratch_types=[idx_t, idx_t, pltpu.VMEM((SC_ROWS, d), ys.dtype)])
    def gather(y_hbm, p1_hbm, p2_hbm, o1_hbm, o2_hbm, i1_v, i2_v, buf):
        def window(base):
            pltpu.sync_copy(p1_hbm.at[:, pl.ds(base, SC_IDX)], i1_v)
            pltpu.sync_copy(p2_hbm.at[:, pl.ds(base, SC_IDX)], i2_v)
            for k in range(SC_IDX // SC_ROWS):
                rows = pl.ds(base + k * SC_ROWS, SC_ROWS)
                pltpu.sync_copy(y_hbm.at[i1_v.at[0, pl.ds(k * SC_ROWS, SC_ROWS)]], buf)
                pltpu.sync_copy(buf, o1_hbm.at[rows])
                pltpu.sync_copy(y_hbm.at[i2_v.at[0, pl.ds(k * SC_ROWS, SC_ROWS)]], buf)
                pltpu.sync_copy(buf, o2_hbm.at[rows])
        _sc_windows(t, window)

    return gather(ys, pos1.reshape(1, t), pos2.reshape(1, t))


def _ple_sparse_kernel(h_ref, info_ref, y1_ref, y2_ref, p_ref, wpp_ref, wpg_ref, gp_ref, gf_ref, y_ref):
    info = info_ref[...]
    h = h_ref[...] + (info[:, INFO_G1:INFO_G1 + 1] * y1_ref[...] + info[:, INFO_G2:INFO_G2 + 1] * y2_ref[...])
    hn = _rmsnorm(h, gp_ref[...])
    h = h + _mm(p_ref[...], wpp_ref[...]) * _sigmoid(_mm(hn, wpg_ref[...]))
    y_ref[...] = _rmsnorm(h, gf_ref[...])


def _ple_sparse(h, info, y1, y2, p, wpp, wpg, gp, gf):
    t = h.shape[0]
    tm = ROW_TM
    row = lambda n: pl.BlockSpec((tm, n), lambda i: (i, 0))
    full = lambda a: pl.BlockSpec(a.shape, lambda i: (0,) * a.ndim)
    return pl.pallas_call(
        _ple_sparse_kernel,
        grid=(t // tm,),
        in_specs=[row(D_MODEL), row(LANES), row(D_MODEL), row(D_MODEL), row(PLE_DIM),
                  full(wpp), full(wpg), full(gp), full(gf)],
        out_specs=row(D_MODEL),
        out_shape=jax.ShapeDtypeStruct((t, D_MODEL), F32),
        compiler_params=_params("parallel"),
        name="ple_sparse",
    )(h, info, y1, y2, p, wpp, wpg, gp, gf)


def _tile_tables(cnt, max_tiles):
    tiles_e = (cnt + (MOE_TM - 1)) // MOE_TM
    ends = jnp.cumsum(tiles_e)
    n_tiles = ends[-1]
    tile = jnp.arange(max_tiles, dtype=jnp.int32)
    idx = jnp.minimum(tile, n_tiles - 1)
    tile_expert = jnp.sum((idx[:, None] >= ends[None, :]).astype(jnp.int32), axis=1)
    mine = tile_expert[:, None] == jnp.arange(N_EXPERTS, dtype=jnp.int32)[None, :]
    of_mine = lambda v: jnp.sum(jnp.where(mine, v[None, :], 0), axis=1)
    valid = jnp.clip(of_mine(cnt) - (idx - of_mine(ends - tiles_e)) * MOE_TM, 0, MOE_TM)
    tile_valid = jnp.where(tile < n_tiles, valid, 0).astype(jnp.int32)
    return (tile_expert, tile_valid, n_tiles.reshape(1), (ends - 1).astype(jnp.int32),
            tiles_e.astype(jnp.int32))


def _ple_final_kernel(h_ref, m_ref, p_ref, wpp_ref, wpg_ref, gp_ref, gf_ref, y_ref):
    h = h_ref[...] + m_ref[...]
    hn = _rmsnorm(h, gp_ref[...])
    h = h + _mm(p_ref[...], wpp_ref[...]) * _sigmoid(_mm(hn, wpg_ref[...]))
    y_ref[...] = _rmsnorm(h, gf_ref[...])


def _ple_final(h, m, p, wpp, wpg, gp, gf):
    t = h.shape[0]
    tm = min(t, 256)
    row = lambda n: pl.BlockSpec((tm, n), lambda i: (i, 0))
    full = lambda a: pl.BlockSpec(a.shape, lambda i: (0,) * a.ndim)
    return pl.pallas_call(
        _ple_final_kernel,
        grid=(t // tm,),
        in_specs=[row(D_MODEL), row(D_MODEL), row(PLE_DIM), full(wpp), full(wpg), full(gp), full(gf)],
        out_specs=row(D_MODEL),
        out_shape=jax.ShapeDtypeStruct((t, D_MODEL), F32),
        compiler_params=_params("parallel"),
        name="ple_final",
    )(h, m, p, wpp, wpg, gp, gf)


def kernel(x_prompt, x_sample, p_prompt, p_sample, cache_k, cache_v, state_conv, state_S, rel_bias, norm_mix, w_in, att_sink, conv_w, dn_A_log, dn_dt_bias, dn_norm, w_out, norm_ffn, w_router_group, w_router_expert, w_gate, w_up, w_down, w_ple_proj, w_ple_gate, norm_ple, norm_final):
    batch, seq, _ = x_prompt.shape
    nseq = x_sample.shape[0]
    assert x_sample.shape[1] == 1 and norm_mix.shape[0] == 1 and cache_k.shape[2] == WINDOW
    assert seq % GDN_TB == 0 and seq % ATT_BLOCK == 0

    wi = w_in[0]
    o_db = ATT_COLS + CONV_CH
    w_in_re = jnp.concatenate(
        [wi[:, :o_db], wi[:, o_db + 2 * DN_HEADS:], wi[:, o_db:o_db + 2 * DN_HEADS],
         jnp.zeros((D_MODEL, LANES - 2 * DN_HEADS), F32)], axis=1).astype(BF16)
    row = lambda a: a.reshape(1, -1).astype(F32)
    pad_lanes = lambda a, off: jnp.zeros((1, LANES), F32).at[0, off:off + a.shape[0]].set(a)
    alog = pad_lanes(dn_A_log[0], DN_HEADS)
    dtb = pad_lanes(dn_dt_bias[0], DN_HEADS)
    dnx = jnp.tile(dn_norm[0], DN_HEADS).reshape(1, DN_WIDTH)
    w_router = jnp.concatenate(
        [w_router_group[0], w_router_expert[0],
         jnp.zeros((D_MODEL, LANES - N_GROUPS - N_EXPERTS), F32)], axis=1).astype(BF16)
    wo = w_out[0].astype(BF16)
    wg, wu, wd = w_gate[0], w_up[0], w_down[0]
    wpp, wpg = w_ple_proj[0].astype(BF16), w_ple_gate[0].astype(BF16)
    sink = att_sink[0]

    qi = np.arange(ATT_BLOCK)[:, None]
    kj = np.arange(2 * ATT_BLOCK)[None, :]
    bucket_p = jnp.asarray(_t5_bucket_np(qi + ATT_BLOCK - kj))
    bucket_s = jnp.asarray(_t5_bucket_np(WINDOW - np.arange(WINDOW)[None, :]))

    def tail(x, o_att, o_dn, p):
        h1, xn2, gates = _outproj_router(x, o_att, o_dn, wo, row(norm_ffn[0]), w_router)
        moe = _moe(xn2, gates, wg, wu, wd)
        return _ple_final(h1, moe, p, wpp, wpg, row(norm_ple[0]), row(norm_final))

    xp = x_prompt.reshape(batch * seq, D_MODEL)
    att_p, xc_p, dz_p, ba_p = _inproj(xp, row(norm_mix[0]), w_in_re)
    o_att_p = _attn_prompt(att_p, bucket_p, rel_bias, sink, batch, seq)
    o_dn_p, s_p = _gdn_prompt(xc_p, dz_p, ba_p, conv_w[0], alog, dtb, dnx, batch, seq)
    h1, xn2, info = _route_sparse(xp, o_att_p, o_dn_p, wo, row(norm_ffn[0]), w_router)
    pos, cnt = _positions(info)
    pos1, pos2 = pos[:, 0], pos[:, 1]
    max_tiles = _moe_tiles(batch * seq)
    cnt_e = cnt[0, ROUTER_OFF:ROUTER_OFF + N_EXPERTS].astype(jnp.int32)
    tile_expert, tile_valid, n_tiles, last_tile, used = _tile_tables(cnt_e, max_tiles)
    xs = _sc_scatter_rows(xn2, pos1, pos2, max_tiles * MOE_TM)
    ys = _experts(xs, tile_expert, tile_valid, n_tiles, wg, wu, wd)
    y1, y2 = _sc_gather_rows(ys, pos1, pos2)
    y_p = _ple_sparse(h1, info, y1, y2, p_prompt[0].reshape(batch * seq, PLE_DIM),
                      wpp, wpg, row(norm_ple[0]), row(norm_final))

    xs = x_sample.reshape(nseq, D_MODEL)
    att_s, xc_s, dz_s, ba_s = _inproj(xs, row(norm_mix[0]), w_in_re)
    ck = cache_k[0].reshape(nseq, WINDOW, KV_WIDTH)
    cv = cache_v[0].reshape(nseq, WINDOW, KV_WIDTH)
    o_att_s = _attn_sample(att_s, ck, cv, bucket_s, rel_bias, sink)
    sconv_t = jnp.swapaxes(state_conv[0], 0, 1)
    o_dn_s, s_s = _gdn_sample(xc_s, dz_s, ba_s, sconv_t,
                              state_S[0].reshape(nseq, DN_HEADS * DN_DK, DN_DV), conv_w[0], alog, dtb,
                              dn_norm[0].reshape(1, DN_DV))
    s_s = s_s.reshape(nseq, DN_HEADS, DN_DK, DN_DV)
    y_s = tail(xs, o_att_s, o_dn_s.reshape(nseq, DN_WIDTH), p_sample[0].reshape(nseq, PLE_DIM))

    att_p3 = att_p.reshape(batch, seq, ATT_COLS)
    kv_shape = (1, batch, WINDOW, ATT_KV_HEADS, HEAD_DIM)
    k_p = att_p3[:, seq - WINDOW:, ATT_WIDTH:ATT_WIDTH + KV_WIDTH].reshape(kv_shape)
    v_p = att_p3[:, seq - WINDOW:, ATT_WIDTH + KV_WIDTH:].reshape(kv_shape)
    conv_p = xc_p.reshape(batch, seq, CONV_CH)[:, seq - (CONV_WIDTH - 1):][None]
    k_new = att_s[:, None, ATT_WIDTH:ATT_WIDTH + KV_WIDTH]
    v_new = att_s[:, None, ATT_WIDTH + KV_WIDTH:]
    kv_s_shape = (1, nseq, WINDOW, ATT_KV_HEADS, HEAD_DIM)
    k_s = jnp.concatenate([ck[:, 1:], k_new], axis=1).reshape(kv_s_shape)
    v_s = jnp.concatenate([cv[:, 1:], v_new], axis=1).reshape(kv_s_shape)
    conv_s = jnp.concatenate([state_conv[0][:, 1:], xc_s[:, None, :]], axis=1)[None]
    return (y_p.reshape(batch, seq, D_MODEL), y_s.reshape(nseq, 1, D_MODEL),
            k_p, v_p, conv_p, s_p[None], k_s, v_s, conv_s, s_s[None])
```

```python
import functools
import math

import numpy as np
import jax
import jax.numpy as jnp
from jax import lax
from jax.experimental import pallas as pl
from jax.experimental.pallas import tpu as pltpu
from jax.experimental.pallas import tpu_sc as plsc

F32 = jnp.float32
BF16 = jnp.bfloat16

D_MODEL = 1024
ATT_HEADS = 8
ATT_KV_HEADS = 2
HEAD_DIM = 64
GQA = ATT_HEADS // ATT_KV_HEADS
WINDOW = 128
ATT_BLOCK = 128
N_BUCKETS = 32
DN_HEADS = 8
DN_DK = 64
DN_DV = 64
CONV_WIDTH = 4
DN_CHUNK = 64
ATT_WIDTH = ATT_HEADS * HEAD_DIM
KV_WIDTH = ATT_KV_HEADS * HEAD_DIM
DN_WIDTH = DN_HEADS * DN_DV
CONV_CH = 3 * DN_WIDTH
N_GROUPS = 4
EXPERTS_PER_GROUP = 8
N_EXPERTS = N_GROUPS * EXPERTS_PER_GROUP
D_EXPERT = 256
PLE_DIM = 256
EPS = 1e-6
NEG_INF = float("-inf")

ATT_COLS = ATT_WIDTH + 2 * KV_WIDTH
LANES = 128
IN_COLS = ATT_COLS + CONV_CH + DN_WIDTH + LANES
ROUTER_OFF = N_GROUPS
VMEM_LIMIT = 48 * 1024 * 1024
ROW_TM = 512


def _params(*sem):
    return pltpu.CompilerParams(dimension_semantics=sem, vmem_limit_bytes=VMEM_LIMIT)


def _mm(a, b):
    return jnp.dot(a.astype(BF16), b.astype(BF16), preferred_element_type=F32)


def _mm_nt(a, b):
    return lax.dot_general(a.astype(BF16), b.astype(BF16), (((1,), (1,)), ((), ())),
                           preferred_element_type=F32)


def _mm_tn(a, b):
    return lax.dot_general(a.astype(BF16), b.astype(BF16), (((0,), (0,)), ((), ())),
                           preferred_element_type=F32)


def _split3(x):
    h1 = x.astype(BF16)
    r1 = x - h1.astype(F32)
    h2 = r1.astype(BF16)
    h3 = (r1 - h2.astype(F32)).astype(BF16)
    return h1, h2, h3


def _mm_sel_rhs(x, sel):
    h1, h2, h3 = _split3(x)
    d = lambda h: jnp.dot(h, sel, preferred_element_type=F32)
    return d(h1) + d(h2) + d(h3)


def _mm_sel_lhs(sel, x):
    h1, h2, h3 = _split3(x)
    d = lambda h: jnp.dot(sel, h, preferred_element_type=F32)
    return d(h1) + d(h2) + d(h3)


def _mm3(a, b):
    ah = a.astype(BF16)
    al = (a - ah.astype(F32)).astype(BF16)
    bh = b.astype(BF16)
    bl = (b - bh.astype(F32)).astype(BF16)
    d = lambda u, v: jnp.dot(u, v, preferred_element_type=F32)
    return d(ah, bh) + d(ah, bl) + d(al, bh)


def _sigmoid(x):
    return 1.0 / (1.0 + jnp.exp(-x))


def _silu(x):
    return x * _sigmoid(x)


def _softplus(x):
    return jnp.maximum(x, 0.0) + jnp.log1p(jnp.exp(-jnp.abs(x)))


def _rmsnorm(x, g):
    return x * lax.rsqrt(jnp.mean(x * x, axis=-1, keepdims=True) + EPS) * g


def _t5_bucket_np(dist):
    max_exact = N_BUCKETS // 2
    d = np.maximum(dist, 0)
    ratio = (np.log(np.maximum(d, 1).astype(np.float32) / np.float32(max_exact))
             / np.float32(math.log(WINDOW / max_exact))).astype(np.float32)
    large = np.minimum(max_exact + (ratio * np.float32(N_BUCKETS - max_exact)).astype(np.int32),
                       N_BUCKETS - 1)
    return np.where(d < max_exact, d, large).astype(np.int32)


def _bias_lookup(bucket, rb_ref, h):
    acc = jnp.zeros(bucket.shape, F32)
    for t in range(N_BUCKETS):
        acc = jnp.where(bucket == t, rb_ref[t, h], acc)
    return acc


def _inproj_kernel(x_ref, g_ref, w_ref, att_ref, xc_ref, dz_ref, ba_ref):
    xn = _rmsnorm(x_ref[...], g_ref[...]).astype(BF16)
    o0, o1, o2 = ATT_COLS, ATT_COLS + CONV_CH, ATT_COLS + CONV_CH + DN_WIDTH
    att_ref[...] = jnp.dot(xn, w_ref[:, :o0], preferred_element_type=F32)
    xc_ref[...] = jnp.dot(xn, w_ref[:, o0:o1], preferred_element_type=F32)
    dz_ref[...] = jnp.dot(xn, w_ref[:, o1:o2], preferred_element_type=F32)
    ba_ref[...] = jnp.dot(xn, w_ref[:, o2:], preferred_element_type=F32)


def _inproj(x, g, w):
    t = x.shape[0]
    tm = min(t, ROW_TM)
    row = lambda n: pl.BlockSpec((tm, n), lambda i: (i, 0))
    full = lambda a: pl.BlockSpec(a.shape, lambda i: (0,) * a.ndim)
    return pl.pallas_call(
        _inproj_kernel,
        grid=(t // tm,),
        in_specs=[row(D_MODEL), full(g), full(w)],
        out_specs=[row(ATT_COLS), row(CONV_CH), row(DN_WIDTH), row(LANES)],
        out_shape=[jax.ShapeDtypeStruct((t, n), F32) for n in (ATT_COLS, CONV_CH, DN_WIDTH, LANES)],
        compiler_params=_params("parallel"),
        name="inproj",
    )(x, g, w)


def _attn_prompt_kernel(cur_ref, prev_ref, bucket_ref, rb_ref, sink_ref, o_ref, bias_scr):
    first = jnp.logical_and(pl.program_id(0) == 0, pl.program_id(1) == 0)
    qi = lax.broadcasted_iota(jnp.int32, (ATT_BLOCK, 2 * ATT_BLOCK), 0)
    kj = lax.broadcasted_iota(jnp.int32, (ATT_BLOCK, 2 * ATT_BLOCK), 1)

    @pl.when(first)
    def _():
        dist = qi + ATT_BLOCK - kj
        band = jnp.logical_and(dist >= 0, dist < WINDOW)
        bucket = bucket_ref[...]
        for h in range(ATT_HEADS):
            bias_scr[h] = jnp.where(band, _bias_lookup(bucket, rb_ref, h), NEG_INF)

    cur = cur_ref[...]
    prev = prev_ref[...]
    q = cur[:, :ATT_WIDTH] * (HEAD_DIM ** -0.5)
    kcat = jnp.concatenate([prev[:, ATT_WIDTH:ATT_WIDTH + KV_WIDTH],
                            cur[:, ATT_WIDTH:ATT_WIDTH + KV_WIDTH]], axis=0)
    vcat = jnp.concatenate([prev[:, ATT_WIDTH + KV_WIDTH:], cur[:, ATT_WIDTH + KV_WIDTH:]], axis=0)
    keep = jnp.logical_or(pl.program_id(1) > 0, kj >= ATT_BLOCK)
    outs = []
    for h in range(ATT_HEADS):
        g = h // GQA
        qh = q[:, h * HEAD_DIM:(h + 1) * HEAD_DIM]
        kh = kcat[:, g * HEAD_DIM:(g + 1) * HEAD_DIM]
        vh = vcat[:, g * HEAD_DIM:(g + 1) * HEAD_DIM]
        s = jnp.where(keep, _mm_nt(qh, kh) + bias_scr[h], NEG_INF)
        sink = sink_ref[h]
        m = jnp.maximum(jnp.max(s, axis=-1, keepdims=True), sink)
        p = jnp.exp(s - m)
        den = jnp.sum(p, axis=-1, keepdims=True) + jnp.exp(sink - m)
        outs.append(_mm(p, vh) / den)
    o_ref[...] = jnp.concatenate(outs, axis=1)


def _attn_prompt(att, bucket, rel_bias, sink, batch, seq):
    nb = seq // ATT_BLOCK
    smem = pl.BlockSpec(memory_space=pltpu.SMEM)
    return pl.pallas_call(
        _attn_prompt_kernel,
        grid=(batch, nb),
        in_specs=[
            pl.BlockSpec((ATT_BLOCK, ATT_COLS), lambda b, i: (b * nb + i, 0)),
            pl.BlockSpec((ATT_BLOCK, ATT_COLS), lambda b, i: (b * nb + jnp.maximum(i - 1, 0), 0)),
            pl.BlockSpec(bucket.shape, lambda b, i: (0, 0)),
            smem, smem,
        ],
        out_specs=pl.BlockSpec((ATT_BLOCK, ATT_WIDTH), lambda b, i: (b * nb + i, 0)),
        out_shape=jax.ShapeDtypeStruct((batch * seq, ATT_WIDTH), F32),
        scratch_shapes=[pltpu.VMEM((ATT_HEADS, ATT_BLOCK, 2 * ATT_BLOCK), F32)],
        compiler_params=_params("arbitrary", "arbitrary"),
        name="attn_prompt",
    )(att, att, bucket, rel_bias, sink)


ATT_S_BB = 8


def _attn_sample_kernel(att_ref, ck_ref, cv_ref, bucket_ref, rb_ref, sink_ref, o_ref,
                        bias_scr, col_scr):
    hrow = lax.broadcasted_iota(jnp.int32, (ATT_HEADS, LANES), 0)
    lane = lax.broadcasted_iota(jnp.int32, (ATT_HEADS, LANES), 1)

    @pl.when(pl.program_id(0) == 0)
    def _():
        bucket = jnp.broadcast_to(bucket_ref[...], (ATT_HEADS, LANES))
        bias = jnp.zeros((ATT_HEADS, LANES), F32)
        cols = jnp.zeros((ATT_HEADS, LANES), F32)
        for h in range(ATT_HEADS):
            bias = jnp.where(hrow == h, _bias_lookup(bucket, rb_ref, h), bias)
            cols = jnp.where(jnp.logical_and(hrow == h, lane == 0), sink_ref[h], cols)
            cols = jnp.where(jnp.logical_and(hrow == h, lane == 1), rb_ref[0, h], cols)
        bias_scr[...] = jnp.where(lane >= 1, bias, NEG_INF)
        col_scr[...] = cols

    bias_c = bias_scr[...]
    sink = col_scr[:, 0:1]
    bias_n = col_scr[:, 1:2]
    same_group = (hrow // GQA) == (lane // HEAD_DIM)
    low_group = lax.broadcasted_iota(jnp.int32, (ATT_HEADS, HEAD_DIM), 0) < GQA
    for b in range(ATT_S_BB):
        row = att_ref[b:b + 1, :]
        q = row[:, :ATT_WIDTH] * (HEAD_DIM ** -0.5)
        kn = row[:, ATT_WIDTH:ATT_WIDTH + KV_WIDTH]
        vn = row[:, ATT_WIDTH + KV_WIDTH:]
        qh = jnp.concatenate([q[:, h * HEAD_DIM:(h + 1) * HEAD_DIM] for h in range(ATT_HEADS)], axis=0)
        q_bd = jnp.where(same_group, jnp.concatenate([qh, qh], axis=1), 0.0)
        rnd = lambda a: a.astype(BF16).astype(F32)
        s_c = _mm_nt(q_bd, ck_ref[b]) + bias_c
        s_n = jnp.sum(rnd(q_bd) * rnd(kn), axis=-1, keepdims=True) + bias_n
        m = jnp.maximum(jnp.maximum(jnp.max(s_c, axis=-1, keepdims=True), s_n), sink)
        p_c = jnp.exp(s_c - m)
        p_n = jnp.exp(s_n - m)
        den = jnp.sum(p_c, axis=-1, keepdims=True) + p_n + jnp.exp(sink - m)
        o_full = _mm(p_c / den, cv_ref[b]) + rnd(p_n / den) * rnd(vn)
        o_sel = jnp.where(low_group, o_full[:, :HEAD_DIM], o_full[:, HEAD_DIM:])
        o_ref[b:b + 1, :] = jnp.concatenate([o_sel[h:h + 1, :] for h in range(ATT_HEADS)], axis=1)


def _attn_sample(att, ck, cv, bucket, rel_bias, sink):
    nseq = att.shape[0]
    smem = pl.BlockSpec(memory_space=pltpu.SMEM)
    cache = pl.BlockSpec((ATT_S_BB, WINDOW, KV_WIDTH), lambda i: (i, 0, 0))
    return pl.pallas_call(
        _attn_sample_kernel,
        grid=(nseq // ATT_S_BB,),
        in_specs=[pl.BlockSpec((ATT_S_BB, ATT_COLS), lambda i: (i, 0)), cache, cache,
                  pl.BlockSpec(bucket.shape, lambda i: (0, 0)), smem, smem],
        out_specs=pl.BlockSpec((ATT_S_BB, ATT_WIDTH), lambda i: (i, 0)),
        out_shape=jax.ShapeDtypeStruct((nseq, ATT_WIDTH), F32),
        scratch_shapes=[pltpu.VMEM((ATT_HEADS, LANES), F32), pltpu.VMEM((ATT_HEADS, LANES), F32)],
        compiler_params=_params("arbitrary"),
        name="attn_sample",
    )(att, ck, cv, bucket, rel_bias, sink)


GDN_TB = 128
GDN_NC = GDN_TB // DN_CHUNK
TAIL = 8


def _gdn_gates(ba, alog, dtb):
    beta = _sigmoid(ba)
    g = -jnp.exp(alog) * _softplus(ba + dtb)
    return beta, g


PAIR = 2 * DN_DK
N_PAIRS = DN_WIDTH // PAIR


def _pair_diag(x, lo):
    xb = x.astype(BF16)
    zero = jnp.zeros_like(xb)
    return jnp.concatenate([jnp.where(lo, xb, zero), jnp.where(lo, zero, xb)], axis=0)


def _gdn_prompt_kernel(xc_ref, dz_ref, ba_ref, cw_ref, alog_ref, dtb_ref, dnx_ref,
                       hsum_ref, expb_ref, expg_ref, ltri_ref,
                       o_ref, s_out_ref, xp_scr, s_scr):
    i = pl.program_id(0)
    nb = xc_ref.shape[0]

    @pl.when(i == 0)
    def _():
        xp_scr[:, 0:TAIL, :] = jnp.zeros((nb, TAIL, CONV_CH), F32)
        s_scr[...] = jnp.zeros(s_scr.shape, F32)

    hsum = hsum_ref[...]
    ri = lax.broadcasted_iota(jnp.int32, (DN_CHUNK, PAIR), 0)
    ci = lax.broadcasted_iota(jnp.int32, (DN_CHUNK, PAIR), 1)
    lo = ci < DN_DK
    cj = jnp.where(lo, ci, ci - DN_DK)
    causal = ri >= cj
    strict = ri > cj
    eye = (ri == cj).astype(F32)

    def sel2(x, m):
        hi = x.astype(BF16)
        lw = (x - hi.astype(F32)).astype(BF16)
        return (jnp.dot(hi, m, preferred_element_type=F32) + jnp.dot(lw, m, preferred_element_type=F32))

    pre = []
    for b in range(nb):
        xc = xc_ref[b]
        xp_scr[b, TAIL:, :] = xc
        y = xp_scr[b, TAIL - 3:TAIL - 3 + GDN_TB, :] * cw_ref[0:1, :]
        y = y + xp_scr[b, TAIL - 2:TAIL - 2 + GDN_TB, :] * cw_ref[1:2, :]
        y = y + xp_scr[b, TAIL - 1:TAIL - 1 + GDN_TB, :] * cw_ref[2:3, :]
        y = y + xc * cw_ref[3:4, :]
        xp_scr[b, 0:TAIL, :] = xc[GDN_TB - TAIL:, :]
        y = _silu(y)
        q = y[:, :DN_WIDTH]
        k = y[:, DN_WIDTH:2 * DN_WIDTH]
        v = y[:, 2 * DN_WIDTH:]
        q = q * lax.rsqrt(sel2(q * q, hsum) + EPS) * (DN_DK ** -0.5)
        k = k * lax.rsqrt(sel2(k * k, hsum) + EPS)
        beta_c, g_c = _gdn_gates(ba_ref[b], alog_ref[...], dtb_ref[...])
        beta = sel2(beta_c, expb_ref[...])
        gam_c = _mm_sel_lhs(ltri_ref[...], g_c)
        gam = _mm_sel_rhs(gam_c, expg_ref[...])
        gam_t = gam_c.T
        kb = k * beta
        egam = jnp.exp(gam)
        pre.append(dict(q=q, k=k, kb=kb, vb=v * beta, qg=q * egam, wr=kb * egam, gam=gam, gam_t=gam_t))

    probs = [(b, p) for b in range(nb) for p in range(N_PAIRS)]
    pick = lambda m: jnp.where(lo, m[:DN_DK], m[DN_DK:])
    o_rows = [[] for _ in range(nb)]
    for c in range(GDN_NC):
        r0, r1 = c * DN_CHUNK, (c + 1) * DN_CHUNK
        sl = lambda name, b, p: pre[b][name][r0:r1, p * PAIR:(p + 1) * PAIR]
        raws = []
        for b, p in probs:
            k_p = sl("k", b, p)
            k_rows = jnp.concatenate([jnp.where(lo, k_p, 0.0), jnp.where(lo, 0.0, k_p)], axis=0)
            raws.append(_mm_nt(jnp.concatenate([sl("kb", b, p), sl("q", b, p)], axis=0), k_rows))
        pws, ts, qks = [], [], []
        for (b, p), raw in zip(probs, raws):
            gcol = sl("gam", b, p)
            h0 = DN_HEADS + 2 * p
            gam_t = pre[b]["gam_t"]
            grow = jnp.concatenate([gam_t[h0:h0 + 1, r0:r1], gam_t[h0 + 1:h0 + 2, r0:r1]], axis=1)
            decay = jnp.exp(jnp.where(causal, gcol - grow, NEG_INF))
            a = jnp.where(strict, raw[:DN_CHUNK] * decay, 0.0)
            qks.append(jnp.where(causal, raw[DN_CHUNK:] * decay, 0.0))
            pws.append(-a)
            ts.append(eye - a)
        pws = [_mm(pw, _pair_diag(pw, lo)) for pw in pws]
        for _ in range(4):
            rs = [_mm(jnp.concatenate([pw, t], axis=0), _pair_diag(pw, lo)) for pw, t in zip(pws, ts)]
            pws = [r[:DN_CHUNK] for r in rs]
            ts = [t + r[DN_CHUNK:] for t, r in zip(ts, rs)]
        rs = [_mm(t, _pair_diag(pw, lo)) for pw, t in zip(pws, ts)]
        ts = [t + r for t, r in zip(ts, rs)]
        sols = [_mm(t, jnp.concatenate([_pair_diag(sl("vb", b, p), lo), _pair_diag(sl("wr", b, p), lo)],
                                       axis=1)) for (b, p), t in zip(probs, ts)]
        qkuws = [_mm(qk, jnp.concatenate([_pair_diag(s[:, :PAIR], lo), _pair_diag(s[:, PAIR:], lo)], axis=1))
                 for qk, s in zip(qks, sols)]
        crosses, gls = [], []
        for (b, p), s in zip(probs, sols):
            gam_last = pre[b]["gam"][r1 - 1:r1, p * PAIR:(p + 1) * PAIR]
            kd = sl("k", b, p) * jnp.exp(gam_last - sl("gam", b, p))
            crosses.append(_mm_tn(kd, s))
            gls.append(jnp.exp(gam_last))
        lhs = [jnp.concatenate([pick(cr[:, PAIR:]), sl("qg", b, p) - qkuw[:, PAIR:]], axis=0)
               for (b, p), cr, qkuw in zip(probs, crosses, qkuws)]
        s_olds = [s_scr[b, p] for b, p in probs]
        rs = [_mm(l, _pair_diag(s_old, lo)) for l, s_old in zip(lhs, s_olds)]
        o_pairs = [[] for _ in range(nb)]
        for (b, p), r, s_old, gl, cr, qkuw in zip(probs, rs, s_olds, gls, crosses, qkuws):
            s_scr[b, p] = gl * s_old - r[:DN_DK] + pick(cr[:, :PAIR])
            o_pairs[b].append(r[DN_DK:] + qkuw[:, :PAIR])
        for b in range(nb):
            o_rows[b].append(jnp.concatenate(o_pairs[b], axis=1))

    for b in range(nb):
        o = jnp.concatenate(o_rows[b], axis=0)
        ms = sel2(o * o, hsum) * (1.0 / DN_DV)
        o = o * lax.rsqrt(ms + EPS) * dnx_ref[...]
        o_ref[b] = o * _silu(dz_ref[b])

    @pl.when(i == pl.num_programs(0) - 1)
    def _():
        for b in range(nb):
            for p in range(N_PAIRS):
                s_p = s_scr[b, p]
                s_out_ref[b, 2 * p] = s_p[:, :DN_DV]
                s_out_ref[b, 2 * p + 1] = s_p[:, DN_DV:]


def _gdn_consts():
    lane = np.arange(DN_WIDTH)
    hsum = (lane[:, None] // DN_DV == lane[None, :] // DN_DV)
    src = np.arange(LANES)
    expb = (src[:, None] == lane[None, :] // DN_DV)
    expg = (src[:, None] == DN_HEADS + lane[None, :] // DN_DV)
    tok = np.arange(GDN_TB)
    ltri = np.logical_and(tok[:, None] >= tok[None, :],
                          tok[:, None] // DN_CHUNK == tok[None, :] // DN_CHUNK)
    as_bf16 = lambda m: jnp.asarray(m.astype(np.float32), dtype=BF16)
    return as_bf16(hsum), as_bf16(expb), as_bf16(expg), as_bf16(ltri)


def _gdn_prompt(xc, dz, ba, conv_w, alog, dtb, dnx, batch, seq):
    nt = seq // GDN_TB
    hsum, expb, expg, ltri = _gdn_consts()
    row = lambda n: pl.BlockSpec((batch, GDN_TB, n), lambda i: (0, i, 0))
    full = lambda a: pl.BlockSpec(a.shape, lambda i: (0,) * a.ndim)
    consts = (conv_w, alog, dtb, dnx, hsum, expb, expg, ltri)
    as3d = lambda a: a.reshape(batch, seq, a.shape[-1])
    o, s = pl.pallas_call(
        _gdn_prompt_kernel,
        grid=(nt,),
        in_specs=[row(CONV_CH), row(DN_WIDTH), row(LANES)] + [full(a) for a in consts],
        out_specs=[row(DN_WIDTH),
                   pl.BlockSpec((batch, DN_HEADS, DN_DK, DN_DV), lambda i: (0, 0, 0, 0))],
        out_shape=[jax.ShapeDtypeStruct((batch, seq, DN_WIDTH), F32),
                   jax.ShapeDtypeStruct((batch, DN_HEADS, DN_DK, DN_DV), F32)],
        scratch_shapes=[pltpu.VMEM((batch, TAIL + GDN_TB, CONV_CH), F32),
                        pltpu.VMEM((batch, N_PAIRS, DN_DK, PAIR), F32)],
        compiler_params=_params("arbitrary"),
        name="gdn_prompt",
    )(as3d(xc), as3d(dz), as3d(ba), *consts)
    return o.reshape(batch * seq, DN_WIDTH), s


GDN_S_BB = 8


def _gdn_sample_kernel(xc_ref, dz_ref, ba_ref, sc_ref, s_ref, cw_ref, alog_ref, dtb_ref, dn_ref,
                       hsum_ref, eye_ref, hsel_ref, hrep_ref, o_ref, s_out_ref):
    xc = xc_ref[...]
    y = sc_ref[0] * cw_ref[0:1, :]
    y = y + sc_ref[1] * cw_ref[1:2, :]
    y = y + sc_ref[2] * cw_ref[2:3, :]
    y = _silu(y + xc * cw_ref[3:4, :])
    hsum = hsum_ref[...]
    q = y[:, :DN_WIDTH]
    k = y[:, DN_WIDTH:2 * DN_WIDTH]
    v = y[:, 2 * DN_WIDTH:]
    q = q * lax.rsqrt(_mm_sel_rhs(q * q, hsum) + EPS) * (DN_DK ** -0.5)
    k = k * lax.rsqrt(_mm_sel_rhs(k * k, hsum) + EPS)
    beta_c, g_c = _gdn_gates(ba_ref[...], alog_ref[...], dtb_ref[...])
    eg_c = jnp.exp(g_c)
    eye = eye_ref[...]
    tr = lambda a: lax.dot_general(a, eye, (((0,), (0,)), ((), ())), precision=lax.Precision.HIGHEST,
                                   preferred_element_type=F32)
    k_t = tr(k)
    q_t = tr(q)
    beta_t = tr(beta_c)
    eg_t = tr(eg_c)
    dz = dz_ref[...]
    dn = dn_ref[...]
    split = lambda r: jnp.concatenate([r[:, h * DN_DV:(h + 1) * DN_DV] for h in range(DN_HEADS)], axis=0)
    hsel = hsel_ref[...]
    hrep = hrep_ref[...]
    for b in range(GDN_S_BB):
        s2 = s_ref[b]
        kc = k_t[:, b:b + 1]
        qc = q_t[:, b:b + 1]
        beta = beta_t[0:DN_HEADS, b:b + 1]
        eg = eg_t[DN_HEADS:2 * DN_HEADS, b:b + 1]
        vh = split(v[b:b + 1, :])
        qh = split(q[b:b + 1, :])
        kh = split(k[b:b + 1, :])
        ks = _mm_sel_lhs(hsel, kc * s2)
        v_new = beta * (vh - eg * ks)
        qs = _mm_sel_lhs(hsel, qc * s2)
        qk = jnp.sum(qh * kh, axis=-1, keepdims=True)
        o = eg * qs + qk * v_new
        rep = _mm_sel_lhs(hrep, jnp.concatenate(
            [v_new, jnp.broadcast_to(eg, (DN_HEADS, DN_DV))], axis=1))
        s_out_ref[b] = s2 * rep[:, DN_DV:] + kc * rep[:, :DN_DV]
        o = _rmsnorm(o, dn) * _silu(split(dz[b:b + 1, :]))
        o_ref[b] = o


def _gdn_sample(xc, dz, ba, sconv_t, state, conv_w, alog, dtb, dn):
    nseq = xc.shape[0]
    lane = np.arange(DN_WIDTH)
    hsum = jnp.asarray((lane[:, None] // DN_DV == lane[None, :] // DN_DV).astype(np.float32), dtype=BF16)
    eye = jnp.eye(GDN_S_BB, dtype=F32)
    hsel_np = (np.arange(DN_HEADS)[:, None] == lane[None, :] // DN_DK).astype(np.float32)
    hsel = jnp.asarray(hsel_np, dtype=BF16)
    hrep = jnp.asarray(hsel_np.T, dtype=BF16)
    row = lambda n: pl.BlockSpec((GDN_S_BB, n), lambda i: (i, 0))
    full = lambda a: pl.BlockSpec(a.shape, lambda i: (0,) * a.ndim)
    st = pl.BlockSpec((GDN_S_BB, DN_HEADS * DN_DK, DN_DV), lambda i: (i, 0, 0))
    consts = (conv_w, alog, dtb, dn, hsum, eye, hsel, hrep)
    return pl.pallas_call(
        _gdn_sample_kernel,
        grid=(nseq // GDN_S_BB,),
        in_specs=[row(CONV_CH), row(DN_WIDTH), row(LANES),
                  pl.BlockSpec((CONV_WIDTH - 1, GDN_S_BB, CONV_CH), lambda i: (0, i, 0)), st]
                 + [full(a) for a in consts],
        out_specs=[pl.BlockSpec((GDN_S_BB, DN_HEADS, DN_DV), lambda i: (i, 0, 0)), st],
        out_shape=[jax.ShapeDtypeStruct((nseq, DN_HEADS, DN_DV), F32),
                   jax.ShapeDtypeStruct(state.shape, F32)],
        compiler_params=_params("parallel"),
        name="gdn_sample",
    )(xc, dz, ba, sconv_t, state, *consts)


def _route(xn, wr):
    logits = jnp.dot(xn, wr, preferred_element_type=F32)
    lane = lax.broadcasted_iota(jnp.int32, logits.shape, 1).astype(F32)
    first_at = lambda hit: jnp.min(jnp.where(hit, lane, float(LANES)), axis=-1, keepdims=True)
    glog = jnp.where(lane < N_GROUPS, logits, NEG_INF)
    gmax = jnp.max(glog, axis=-1, keepdims=True)
    gsel = first_at(glog == gmax)
    pgsel = 1.0 / jnp.sum(jnp.exp(glog - gmax), axis=-1, keepdims=True)
    lo = ROUTER_OFF + gsel * EXPERTS_PER_GROUP
    in_group = jnp.logical_and(lane >= lo, lane < lo + EXPERTS_PER_GROUP)
    elog = jnp.where(in_group, logits, NEG_INF)
    m1 = jnp.max(elog, axis=-1, keepdims=True)
    i1 = first_at(elog == m1)
    z = jnp.sum(jnp.exp(elog - m1), axis=-1, keepdims=True)
    elog2 = jnp.where(lane == i1, NEG_INF, elog)
    m2 = jnp.max(elog2, axis=-1, keepdims=True)
    i2 = first_at(elog2 == m2)
    p1 = 1.0 / z
    p2 = jnp.exp(m2 - m1) / z
    tot = p1 + p2
    return lane, i1, i2, p1 / tot * pgsel, p2 / tot * pgsel


def _outproj(x_ref, oa_ref, od_ref, wo_ref):
    return x_ref[...] + _mm(oa_ref[...], wo_ref[:ATT_WIDTH, :]) + _mm(od_ref[...], wo_ref[ATT_WIDTH:, :])


def _outproj_router_kernel(x_ref, oa_ref, od_ref, wo_ref, g_ref, wr_ref, h_ref, xn_ref, gate_ref):
    h = _outproj(x_ref, oa_ref, od_ref, wo_ref)
    h_ref[...] = h
    xn = _rmsnorm(h, g_ref[...]).astype(BF16)
    xn_ref[...] = xn
    lane, i1, i2, g1, g2 = _route(xn, wr_ref[...])
    gate_ref[...] = jnp.where(lane == i1, g1, 0.0) + jnp.where(lane == i2, g2, 0.0)


def _outproj_router(x, oa, od, wo, g, wr):
    t = x.shape[0]
    tm = min(t, 256)
    row = lambda n: pl.BlockSpec((tm, n), lambda i: (i, 0))
    full = lambda a: pl.BlockSpec(a.shape, lambda i: (0,) * a.ndim)
    return pl.pallas_call(
        _outproj_router_kernel,
        grid=(t // tm,),
        in_specs=[row(D_MODEL), row(ATT_WIDTH), row(DN_WIDTH), full(wo), full(g), full(wr)],
        out_specs=[row(D_MODEL), row(D_MODEL), row(LANES)],
        out_shape=[jax.ShapeDtypeStruct((t, D_MODEL), F32), jax.ShapeDtypeStruct((t, D_MODEL), BF16),
                   jax.ShapeDtypeStruct((t, LANES), F32)],
        compiler_params=_params("parallel"),
        name="outproj_router",
    )(x, oa, od, wo, g, wr)


def _moe_kernel(xn_ref, gate_ref, wg_ref, wu_ref, wd_ref, o_ref):
    e = pl.program_id(1)
    xn = xn_ref[...]
    lane = lax.broadcasted_iota(jnp.int32, gate_ref.shape, 1)
    gate = jnp.sum(jnp.where(lane == e + ROUTER_OFF, gate_ref[...], 0.0), axis=-1, keepdims=True)
    hg = jnp.dot(xn, wg_ref[...].astype(BF16), preferred_element_type=F32)
    hu = jnp.dot(xn, wu_ref[...].astype(BF16), preferred_element_type=F32)
    hm = _silu(hg) * hu * gate
    y = jnp.dot(hm.astype(BF16), wd_ref[...].astype(BF16), preferred_element_type=F32)

    @pl.when(e == 0)
    def _():
        o_ref[...] = y

    @pl.when(e > 0)
    def _():
        o_ref[...] += y


def _moe(xn, gates, wg, wu, wd):
    t = xn.shape[0]
    tm = min(t, 1024)
    return pl.pallas_call(
        _moe_kernel,
        grid=(t // tm, N_EXPERTS),
        in_specs=[pl.BlockSpec((tm, D_MODEL), lambda i, e: (i, 0)),
                  pl.BlockSpec((tm, LANES), lambda i, e: (i, 0)),
                  pl.BlockSpec((None, D_MODEL, D_EXPERT), lambda i, e: (e, 0, 0)),
                  pl.BlockSpec((None, D_MODEL, D_EXPERT), lambda i, e: (e, 0, 0)),
                  pl.BlockSpec((None, D_EXPERT, D_MODEL), lambda i, e: (e, 0, 0))],
        out_specs=pl.BlockSpec((tm, D_MODEL), lambda i, e: (i, 0)),
        out_shape=jax.ShapeDtypeStruct((t, D_MODEL), F32),
        compiler_params=_params("parallel", "arbitrary"),
        name="moe",
    )(xn, gates, wg, wu, wd)


MOE_TM = 256
POS_TM = 512
INFO_G1, INFO_G2, INFO_E1, INFO_E2 = 0, 1, 2, 3
DMA_UNROLL = 8


def _moe_tiles(t):
    return (2 * t) // MOE_TM + N_EXPERTS


HALF = D_MODEL // 2
U32 = jnp.uint32


def _pack_rows(x):
    bits = lambda v: lax.bitcast_convert_type(v.astype(BF16).astype(F32), U32)
    return bits(x[:, HALF:]) | (bits(x[:, :HALF]) >> 16)


def _unpack_rows(w):
    lo = lax.bitcast_convert_type(w << 16, F32)
    hi = lax.bitcast_convert_type(w & jnp.uint32(0xFFFF0000), F32)
    return lo, hi


def _route_kernel(x_ref, oa_ref, od_ref, wo_ref, g_ref, wr_ref, h_ref, xn_ref, info_ref):
    h = _outproj(x_ref, oa_ref, od_ref, wo_ref)
    h_ref[...] = h
    xn = _rmsnorm(h, g_ref[...])
    xn_ref[...] = _pack_rows(xn)
    lane, i1, i2, g1, g2 = _route(xn.astype(BF16), wr_ref[...])
    info = jnp.where(lane == INFO_G1, g1, 0.0) + jnp.where(lane == INFO_G2, g2, 0.0)
    info = info + jnp.where(lane == INFO_E1, i1, 0.0) + jnp.where(lane == INFO_E2, i2, 0.0)
    info_ref[...] = info


def _route_sparse(x, oa, od, wo, g, wr):
    t = x.shape[0]
    tm = ROW_TM
    row = lambda n: pl.BlockSpec((tm, n), lambda i: (i, 0))
    full = lambda a: pl.BlockSpec(a.shape, lambda i: (0,) * a.ndim)
    return pl.pallas_call(
        _route_kernel,
        grid=(t // tm,),
        in_specs=[row(D_MODEL), row(ATT_WIDTH), row(DN_WIDTH), full(wo), full(g), full(wr)],
        out_specs=[row(D_MODEL), row(HALF), row(LANES)],
        out_shape=[jax.ShapeDtypeStruct((t, D_MODEL), F32), jax.ShapeDtypeStruct((t, HALF), U32),
                   jax.ShapeDtypeStruct((t, LANES), F32)],
        compiler_params=_params("parallel"),
        name="route",
    )(x, oa, od, wo, g, wr)


def _positions_kernel(info_ref, ltri_ref, utri_ref, pos_ref, cnt_ref, run_scr, off_scr):
    phase = pl.program_id(0)
    i = pl.program_id(1)
    info = info_ref[...]
    lane = lax.broadcasted_iota(jnp.int32, info.shape, 1).astype(F32)
    hit1 = lane == info[:, INFO_E1:INFO_E1 + 1]
    hit2 = lane == info[:, INFO_E2:INFO_E2 + 1]
    onehot = jnp.logical_or(hit1, hit2).astype(F32)

    @pl.when(jnp.logical_and(phase == 0, i == 0))
    def _():
        run_scr[...] = jnp.zeros(run_scr.shape, F32)

    @pl.when(jnp.logical_and(phase == 1, i == 0))
    def _():
        cnt = run_scr[...]
        cnt_ref[...] = cnt
        tiles = jnp.floor((cnt + (MOE_TM - 1)) * (1.0 / MOE_TM))
        off_scr[...] = MOE_TM * jnp.dot(tiles.astype(BF16), utri_ref[...], preferred_element_type=F32)
        run_scr[...] = jnp.zeros(run_scr.shape, F32)

    @pl.when(phase == 0)
    def _():
        pos_ref[...] = jnp.zeros(pos_ref.shape, jnp.int32)

    @pl.when(phase == 1)
    def _():
        before = (jnp.dot(ltri_ref[...], onehot.astype(BF16), preferred_element_type=F32)
                  + run_scr[...] + off_scr[...])
        pos1 = jnp.sum(jnp.where(hit1, before, 0.0), axis=-1, keepdims=True)
        pos2 = jnp.sum(jnp.where(hit2, before, 0.0), axis=-1, keepdims=True)
        pos_ref[...] = (jnp.where(lane == 0, pos1, 0.0) + jnp.where(lane == 1, pos2, 0.0)).astype(jnp.int32)

    run_scr[...] += jnp.sum(onehot, axis=0, keepdims=True)


def _positions(info):
    t = info.shape[0]
    tm = min(t, POS_TM)
    tok = np.arange(tm)
    ltri = jnp.asarray((tok[:, None] > tok[None, :]).astype(np.float32), dtype=BF16)
    ln = np.arange(LANES)
    utri = jnp.asarray((ln[:, None] < ln[None, :]).astype(np.float32), dtype=BF16)
    full = lambda a: pl.BlockSpec(a.shape, lambda ph, i: (0,) * a.ndim)
    return pl.pallas_call(
        _positions_kernel,
        grid=(2, t // tm),
        in_specs=[pl.BlockSpec((tm, LANES), lambda ph, i: (i, 0)), full(ltri), full(utri)],
        out_specs=[pl.BlockSpec((tm, LANES), lambda ph, i: (i * ph, 0)),
                   pl.BlockSpec((1, LANES), lambda ph, i: (0, 0))],
        out_shape=[jax.ShapeDtypeStruct((t, LANES), jnp.int32), jax.ShapeDtypeStruct((1, LANES), F32)],
        scratch_shapes=[pltpu.VMEM((1, LANES), F32), pltpu.VMEM((1, LANES), F32)],
        compiler_params=_params("arbitrary", "arbitrary"),
        name="positions",
    )(info, ltri, utri)


def _row_copy(src_hbm, src_row, dst_hbm, dst_row, sem):
    return pltpu.make_async_copy(src_hbm.at[pl.ds(src_row, 1)], dst_hbm.at[pl.ds(dst_row, 1)], sem)


SCATTER_SLOTS = 3


def _scatter_kernel(pos1_ref, pos2_ref, last_ref, used_ref, nt_ref, xn_hbm, zero_hbm, xs_hbm,
                    buf, lsem, sem, zsem, *, n_tok):
    max_tiles = xs_hbm.shape[0] // MOE_TM

    def zero_tile(tile):
        return pltpu.make_async_copy(zero_hbm, xs_hbm.at[pl.ds(tile * MOE_TM, MOE_TM)], zsem)

    def for_unused(fn):
        def body(tile, carry):
            fn(tile)
            return carry
        lax.fori_loop(nt_ref[0], max_tiles, body, 0)

    for e in range(N_EXPERTS):
        @pl.when(used_ref[e] > 0)
        def _():
            zero_tile(last_ref[e]).start()
    for_unused(lambda tile: zero_tile(tile).start())
    for e in range(N_EXPERTS):
        @pl.when(used_ref[e] > 0)
        def _():
            zero_tile(last_ref[e]).wait()
    for_unused(lambda tile: zero_tile(tile).wait())

    tm = buf.shape[1]
    n = n_tok // tm

    def load(i):
        return pltpu.make_async_copy(xn_hbm.at[pl.ds(i * tm, tm)], buf.at[i % SCATTER_SLOTS],
                                     lsem.at[i % SCATTER_SLOTS])

    def wait_rows(slot):
        pltpu.make_async_copy(xs_hbm.at[pl.ds(0, 2 * tm)], xs_hbm.at[pl.ds(0, 2 * tm)], sem.at[slot]).wait()

    load(0).start()
    load(1).start()

    def step(i, carry):
        slot = i % SCATTER_SLOTS
        load(i).wait()

        def body(j, c2):
            tok = i * tm + j
            src = buf.at[slot, pl.ds(j, 1)]
            pltpu.make_async_copy(src, xs_hbm.at[pl.ds(pos1_ref[tok], 1)], sem.at[slot]).start()
            pltpu.make_async_copy(src, xs_hbm.at[pl.ds(pos2_ref[tok], 1)], sem.at[slot]).start()
            return c2
        lax.fori_loop(0, tm, body, 0, unroll=DMA_UNROLL)

        @pl.when(i >= 1)
        def _():
            wait_rows((i - 1) % SCATTER_SLOTS)

        @pl.when(i + 2 < n)
        def _():
            load(i + 2).start()
        return carry
    lax.fori_loop(0, n, step, 0)
    wait_rows((n - 1) % SCATTER_SLOTS)


def _scatter_rows(xn, pos1, pos2, last_tile, used, n_tiles, n_rows):
    t = xn.shape[0]
    zero = jnp.zeros((MOE_TM, D_MODEL), F32)
    any_spec = pl.BlockSpec(memory_space=pl.ANY)
    return pl.pallas_call(
        functools.partial(_scatter_kernel, n_tok=t),
        grid_spec=pltpu.PrefetchScalarGridSpec(
            num_scalar_prefetch=5, grid=(1,),
            in_specs=[any_spec, any_spec], out_specs=any_spec,
            scratch_shapes=[pltpu.VMEM((SCATTER_SLOTS, MOE_TM, D_MODEL), F32),
                            pltpu.SemaphoreType.DMA((SCATTER_SLOTS,)),
                            pltpu.SemaphoreType.DMA((SCATTER_SLOTS,)),
                            pltpu.SemaphoreType.DMA]),
        out_shape=jax.ShapeDtypeStruct((n_rows, D_MODEL), F32),
        compiler_params=_params("arbitrary"),
        name="scatter_rows",
    )(pos1, pos2, last_tile, used, n_tiles, xn, zero)


def _experts_kernel(te_ref, tv_ref, nt_ref, xs_ref, wg_ref, wu_ref, wd_ref, ys_ref, wg_s, wu_s, wd_s):
    i = pl.program_id(0)
    used = i < nt_ref[0]

    @pl.when(jnp.logical_or(i == 0, te_ref[i] != te_ref[jnp.maximum(i - 1, 0)]))
    def _():
        wg_s[...] = wg_ref[...].astype(BF16)
        wu_s[...] = wu_ref[...].astype(BF16)
        wd_s[...] = wd_ref[...].astype(BF16)

    @pl.when(used)
    def _():
        row = lax.broadcasted_iota(jnp.int32, xs_ref.shape, 0)
        x_lo, x_hi = _unpack_rows(jnp.where(row < tv_ref[i], xs_ref[...], jnp.uint32(0)))
        x_lo = x_lo.astype(BF16)
        x_hi = x_hi.astype(BF16)
        up = lambda w_s: (jnp.dot(x_lo, w_s[:HALF, :], preferred_element_type=F32)
                          + jnp.dot(x_hi, w_s[HALF:, :], preferred_element_type=F32))
        hm = (_silu(up(wg_s)) * up(wu_s)).astype(BF16)
        ys_ref[...] = _pack_rows(jnp.dot(hm, wd_s[...], preferred_element_type=F32))

    @pl.when(jnp.logical_not(used))
    def _():
        ys_ref[...] = jnp.zeros(ys_ref.shape, U32)


def _experts(xs, tile_expert, tile_valid, n_tiles, wg, wu, wd):
    max_tiles = xs.shape[0] // MOE_TM
    rows = pl.BlockSpec((MOE_TM, HALF), lambda i, te, tv, nt: (i, 0))
    wspec = lambda shape: pl.BlockSpec((None,) + shape, lambda i, te, tv, nt: (te[i], 0, 0))
    return pl.pallas_call(
        _experts_kernel,
        grid_spec=pltpu.PrefetchScalarGridSpec(
            num_scalar_prefetch=3, grid=(max_tiles,),
            in_specs=[rows, wspec((D_MODEL, D_EXPERT)), wspec((D_MODEL, D_EXPERT)),
                      wspec((D_EXPERT, D_MODEL))],
            out_specs=rows,
            scratch_shapes=[pltpu.VMEM((D_MODEL, D_EXPERT), BF16), pltpu.VMEM((D_MODEL, D_EXPERT), BF16),
                            pltpu.VMEM((D_EXPERT, D_MODEL), BF16)]),
        out_shape=jax.ShapeDtypeStruct(xs.shape, U32),
        compiler_params=_params("arbitrary"),
        name="experts",
    )(tile_expert, tile_valid, n_tiles, xs, wg, wu, wd)


def _ple_gather_kernel(pos1_ref, pos2_ref, h_ref, info_ref, p_ref, wpp_ref, wpg_ref, gp_ref, gf_ref,
                       ys_hbm, y_ref, ybuf, sem):
    i = pl.program_id(0)
    n = pl.num_programs(0)
    tm = h_ref.shape[0]

    def issue(tile, slot):
        def body(j, carry):
            tok = tile * tm + j
            pltpu.make_async_copy(ys_hbm.at[pl.ds(pos1_ref[tok], 1)], ybuf.at[slot, 0, pl.ds(j, 1)],
                                  sem.at[slot]).start()
            pltpu.make_async_copy(ys_hbm.at[pl.ds(pos2_ref[tok], 1)], ybuf.at[slot, 1, pl.ds(j, 1)],
                                  sem.at[slot]).start()
            return carry
        lax.fori_loop(0, tm, body, 0, unroll=DMA_UNROLL)

    @pl.when(i == 0)
    def _():
        issue(0, 0)

    @pl.when(i + 1 < n)
    def _():
        issue(i + 1, (i + 1) % 2)

    slot = i % 2
    pltpu.make_async_copy(ybuf.at[slot], ybuf.at[slot], sem.at[slot]).wait()
    info = info_ref[...]
    moe = info[:, INFO_G1:INFO_G1 + 1] * ybuf[slot, 0] + info[:, INFO_G2:INFO_G2 + 1] * ybuf[slot, 1]
    h = h_ref[...] + moe
    hn = _rmsnorm(h, gp_ref[...])
    h = h + _mm(p_ref[...], wpp_ref[...]) * _sigmoid(_mm(hn, wpg_ref[...]))
    y_ref[...] = _rmsnorm(h, gf_ref[...])


def _ple_gather(h, info, p, ys, pos1, pos2, wpp, wpg, gp, gf):
    t = h.shape[0]
    tm = 256
    row = lambda n: pl.BlockSpec((tm, n), lambda i, p1, p2: (i, 0))
    full = lambda a: pl.BlockSpec(a.shape, lambda i, p1, p2: (0,) * a.ndim)
    return pl.pallas_call(
        _ple_gather_kernel,
        grid_spec=pltpu.PrefetchScalarGridSpec(
            num_scalar_prefetch=2, grid=(t // tm,),
            in_specs=[row(D_MODEL), row(LANES), row(PLE_DIM), full(wpp), full(wpg), full(gp), full(gf),
                      pl.BlockSpec(memory_space=pl.ANY)],
            out_specs=row(D_MODEL),
            scratch_shapes=[pltpu.VMEM((2, 2, tm, D_MODEL), F32), pltpu.SemaphoreType.DMA((2,))]),
        out_shape=jax.ShapeDtypeStruct((t, D_MODEL), F32),
        compiler_params=_params("arbitrary"),
        name="ple_gather",
    )(pos1, pos2, h, info, p, wpp, wpg, gp, gf, ys)


SC_IDX = 128
SC_ROWS = 64
SC_WORKERS = 32


def _sc_mesh():
    return plsc.VectorSubcoreMesh(core_axis_name="c", subcore_axis_name="s")


def _sc_windows(t, fn):
    per_worker = t // SC_WORKERS
    worker = lax.axis_index(("c", "s"))

    @pl.loop(0, per_worker // SC_IDX)
    def _(w):
        fn(worker * per_worker + w * SC_IDX)


def _sc_scatter_rows(xn, pos1, pos2, n_rows):
    t, d = xn.shape
    assert t % (SC_WORKERS * SC_IDX) == 0
    idx_t = pltpu.VMEM((1, SC_IDX), jnp.int32)

    @pl.kernel(out_type=jax.ShapeDtypeStruct((n_rows, d), xn.dtype), mesh=_sc_mesh(),
               scratch_types=[idx_t, idx_t, pltpu.VMEM((SC_ROWS, d), xn.dtype)])
    def scatter(x_hbm, p1_hbm, p2_hbm, o_hbm, i1_v, i2_v, buf):
        def window(base):
            pltpu.sync_copy(p1_hbm.at[:, pl.ds(base, SC_IDX)], i1_v)
            pltpu.sync_copy(p2_hbm.at[:, pl.ds(base, SC_IDX)], i2_v)
            for k in range(SC_IDX // SC_ROWS):
                pltpu.sync_copy(x_hbm.at[pl.ds(base + k * SC_ROWS, SC_ROWS)], buf)
                pltpu.sync_copy(buf, o_hbm.at[i1_v.at[0, pl.ds(k * SC_ROWS, SC_ROWS)]])
                pltpu.sync_copy(buf, o_hbm.at[i2_v.at[0, pl.ds(k * SC_ROWS, SC_ROWS)]])
        _sc_windows(t, window)

    return scatter(xn, pos1.reshape(1, t), pos2.reshape(1, t))


def _sc_gather_rows(ys, pos1, pos2):
    t = pos1.shape[0]
    d = ys.shape[1]
    assert t % (SC_WORKERS * SC_IDX) == 0
    idx_t = pltpu.VMEM((1, SC_IDX), jnp.int32)
    out = jax.ShapeDtypeStruct((t, d), ys.dtype)

    @pl.kernel(out_type=(out, out), mesh=_sc_mesh(),
               scratch_types=[idx_t, idx_t, pltpu.VMEM((SC_ROWS, d), ys.dtype)])
    def gather(y_hbm, p1_hbm, p2_hbm, o1_hbm, o2_hbm, i1_v, i2_v, buf):
        def window(base):
            pltpu.sync_copy(p1_hbm.at[:, pl.ds(base, SC_IDX)], i1_v)
            pltpu.sync_copy(p2_hbm.at[:, pl.ds(base, SC_IDX)], i2_v)
            for k in range(SC_IDX // SC_ROWS):
                rows = pl.ds(base + k * SC_ROWS, SC_ROWS)
                pltpu.sync_copy(y_hbm.at[i1_v.at[0, pl.ds(k * SC_ROWS, SC_ROWS)]], buf)
                pltpu.sync_copy(buf, o1_hbm.at[rows])
                pltpu.sync_copy(y_hbm.at[i2_v.at[0, pl.ds(k * SC_ROWS, SC_ROWS)]], buf)
                pltpu.sync_copy(buf, o2_hbm.at[rows])
        _sc_windows(t, window)

    return gather(ys, pos1.reshape(1, t), pos2.reshape(1, t))


def _ple_sparse_kernel(h_ref, info_ref, y1_ref, y2_ref, p_ref, wpp_ref, wpg_ref, gp_ref, gf_ref, y_ref):
    info = info_ref[...]
    g1 = info[:, INFO_G1:INFO_G1 + 1]
    g2 = info[:, INFO_G2:INFO_G2 + 1]
    y1_lo, y1_hi = _unpack_rows(y1_ref[...])
    y2_lo, y2_hi = _unpack_rows(y2_ref[...])
    moe = jnp.concatenate([g1 * y1_lo + g2 * y2_lo, g1 * y1_hi + g2 * y2_hi], axis=1)
    h = h_ref[...] + moe
    hn = _rmsnorm(h, gp_ref[...])
    h = h + _mm(p_ref[...], wpp_ref[...]) * _sigmoid(_mm(hn, wpg_ref[...]))
    y_ref[...] = _rmsnorm(h, gf_ref[...])


def _ple_sparse(h, info, y1, y2, p, wpp, wpg, gp, gf):
    t = h.shape[0]
    tm = ROW_TM
    row = lambda n: pl.BlockSpec((tm, n), lambda i: (i, 0))
    full = lambda a: pl.BlockSpec(a.shape, lambda i: (0,) * a.ndim)
    return pl.pallas_call(
        _ple_sparse_kernel,
        grid=(t // tm,),
        in_specs=[row(D_MODEL), row(LANES), row(HALF), row(HALF), row(PLE_DIM),
                  full(wpp), full(wpg), full(gp), full(gf)],
        out_specs=row(D_MODEL),
        out_shape=jax.ShapeDtypeStruct((t, D_MODEL), F32),
        compiler_params=_params("parallel"),
        name="ple_sparse",
    )(h, info, y1, y2, p, wpp, wpg, gp, gf)


def _tile_tables(cnt, max_tiles):
    tiles_e = (cnt + (MOE_TM - 1)) // MOE_TM
    ends = jnp.cumsum(tiles_e)
    n_tiles = ends[-1]
    tile = jnp.arange(max_tiles, dtype=jnp.int32)
    idx = jnp.minimum(tile, n_tiles - 1)
    tile_expert = jnp.sum((idx[:, None] >= ends[None, :]).astype(jnp.int32), axis=1)
    mine = tile_expert[:, None] == jnp.arange(N_EXPERTS, dtype=jnp.int32)[None, :]
    of_mine = lambda v: jnp.sum(jnp.where(mine, v[None, :], 0), axis=1)
    valid = jnp.clip(of_mine(cnt) - (idx - of_mine(ends - tiles_e)) * MOE_TM, 0, MOE_TM)
    tile_valid = jnp.where(tile < n_tiles, valid, 0).astype(jnp.int32)
    return (tile_expert, tile_valid, n_tiles.reshape(1), (ends - 1).astype(jnp.int32),
            tiles_e.astype(jnp.int32))


def _ple_final_kernel(h_ref, m_ref, p_ref, wpp_ref, wpg_ref, gp_ref, gf_ref, y_ref):
    h = h_ref[...] + m_ref[...]
    hn = _rmsnorm(h, gp_ref[...])
    h = h + _mm(p_ref[...], wpp_ref[...]) * _sigmoid(_mm(hn, wpg_ref[...]))
    y_ref[...] = _rmsnorm(h, gf_ref[...])


def _ple_final(h, m, p, wpp, wpg, gp, gf):
    t = h.shape[0]
    tm = min(t, 256)
    row = lambda n: pl.BlockSpec((tm, n), lambda i: (i, 0))
    full = lambda a: pl.BlockSpec(a.shape, lambda i: (0,) * a.ndim)
    return pl.pallas_call(
        _ple_final_kernel,
        grid=(t // tm,),
        in_specs=[row(D_MODEL), row(D_MODEL), row(PLE_DIM), full(wpp), full(wpg), full(gp), full(gf)],
        out_specs=row(D_MODEL),
        out_shape=jax.ShapeDtypeStruct((t, D_MODEL), F32),
        compiler_params=_params("parallel"),
        name="ple_final",
    )(h, m, p, wpp, wpg, gp, gf)


def kernel(x_prompt, x_sample, p_prompt, p_sample, cache_k, cache_v, state_conv, state_S, rel_bias, norm_mix, w_in, att_sink, conv_w, dn_A_log, dn_dt_bias, dn_norm, w_out, norm_ffn, w_router_group, w_router_expert, w_gate, w_up, w_down, w_ple_proj, w_ple_gate, norm_ple, norm_final):
    batch, seq, _ = x_prompt.shape
    nseq = x_sample.shape[0]
    assert x_sample.shape[1] == 1 and norm_mix.shape[0] == 1 and cache_k.shape[2] == WINDOW
    assert seq % GDN_TB == 0 and seq % ATT_BLOCK == 0

    wi = w_in[0]
    o_db = ATT_COLS + CONV_CH
    w_in_re = jnp.concatenate(
        [wi[:, :o_db], wi[:, o_db + 2 * DN_HEADS:], wi[:, o_db:o_db + 2 * DN_HEADS],
         jnp.zeros((D_MODEL, LANES - 2 * DN_HEADS), F32)], axis=1).astype(BF16)
    row = lambda a: a.reshape(1, -1).astype(F32)
    pad_lanes = lambda a, off: jnp.zeros((1, LANES), F32).at[0, off:off + a.shape[0]].set(a)
    alog = pad_lanes(dn_A_log[0], DN_HEADS)
    dtb = pad_lanes(dn_dt_bias[0], DN_HEADS)
    dnx = jnp.tile(dn_norm[0], DN_HEADS).reshape(1, DN_WIDTH)
    w_router = jnp.concatenate(
        [w_router_group[0], w_router_expert[0],
         jnp.zeros((D_MODEL, LANES - N_GROUPS - N_EXPERTS), F32)], axis=1).astype(BF16)
    wo = w_out[0].astype(BF16)
    wg, wu, wd = w_gate[0], w_up[0], w_down[0]
    wpp, wpg = w_ple_proj[0].astype(BF16), w_ple_gate[0].astype(BF16)
    sink = att_sink[0]

    qi = np.arange(ATT_BLOCK)[:, None]
    kj = np.arange(2 * ATT_BLOCK)[None, :]
    bucket_p = jnp.asarray(_t5_bucket_np(qi + ATT_BLOCK - kj))
    bucket_s = jnp.asarray(_t5_bucket_np(WINDOW - np.arange(WINDOW)[None, :]))

    def tail(x, o_att, o_dn, p):
        h1, xn2, gates = _outproj_router(x, o_att, o_dn, wo, row(norm_ffn[0]), w_router)
        moe = _moe(xn2, gates, wg, wu, wd)
        return _ple_final(h1, moe, p, wpp, wpg, row(norm_ple[0]), row(norm_final))

    xp = x_prompt.reshape(batch * seq, D_MODEL)
    att_p, xc_p, dz_p, ba_p = _inproj(xp, row(norm_mix[0]), w_in_re)
    o_att_p = _attn_prompt(att_p, bucket_p, rel_bias, sink, batch, seq)
    o_dn_p, s_p = _gdn_prompt(xc_p, dz_p, ba_p, conv_w[0], alog, dtb, dnx, batch, seq)
    h1, xn2, info = _route_sparse(xp, o_att_p, o_dn_p, wo, row(norm_ffn[0]), w_router)
    pos, cnt = _positions(info)
    pos1, pos2 = pos[:, 0], pos[:, 1]
    max_tiles = _moe_tiles(batch * seq)
    cnt_e = cnt[0, ROUTER_OFF:ROUTER_OFF + N_EXPERTS].astype(jnp.int32)
    tile_expert, tile_valid, n_tiles, last_tile, used = _tile_tables(cnt_e, max_tiles)
    xs = _sc_scatter_rows(xn2, pos1, pos2, max_tiles * MOE_TM)
    ys = _experts(xs, tile_expert, tile_valid, n_tiles, wg, wu, wd)
    y1, y2 = _sc_gather_rows(ys, pos1, pos2)
    y_p = _ple_sparse(h1, info, y1, y2, p_prompt[0].reshape(batch * seq, PLE_DIM),
                      wpp, wpg, row(norm_ple[0]), row(norm_final))

    xs = x_sample.reshape(nseq, D_MODEL)
    att_s, xc_s, dz_s, ba_s = _inproj(xs, row(norm_mix[0]), w_in_re)
    ck = cache_k[0].reshape(nseq, WINDOW, KV_WIDTH)
    cv = cache_v[0].reshape(nseq, WINDOW, KV_WIDTH)
    o_att_s = _attn_sample(att_s, ck, cv, bucket_s, rel_bias, sink)
    sconv_t = jnp.swapaxes(state_conv[0], 0, 1)
    o_dn_s, s_s = _gdn_sample(xc_s, dz_s, ba_s, sconv_t,
                              state_S[0].reshape(nseq, DN_HEADS * DN_DK, DN_DV), conv_w[0], alog, dtb,
                              dn_norm[0].reshape(1, DN_DV))
    s_s = s_s.reshape(nseq, DN_HEADS, DN_DK, DN_DV)
    y_s = tail(xs, o_att_s, o_dn_s.reshape(nseq, DN_WIDTH), p_sample[0].reshape(nseq, PLE_DIM))

    att_p3 = att_p.reshape(batch, seq, ATT_COLS)
    kv_shape = (1, batch, WINDOW, ATT_KV_HEADS, HEAD_DIM)
    k_p = att_p3[:, seq - WINDOW:, ATT_WIDTH:ATT_WIDTH + KV_WIDTH].reshape(kv_shape)
    v_p = att_p3[:, seq - WINDOW:, ATT_WIDTH + KV_WIDTH:].reshape(kv_shape)
    conv_p = xc_p.reshape(batch, seq, CONV_CH)[:, seq - (CONV_WIDTH - 1):][None]
    k_new = att_s[:, None, ATT_WIDTH:ATT_WIDTH + KV_WIDTH]
    v_new = att_s[:, None, ATT_WIDTH + KV_WIDTH:]
    kv_s_shape = (1, nseq, WINDOW, ATT_KV_HEADS, HEAD_DIM)
    k_s = jnp.concatenate([ck[:, 1:], k_new], axis=1).reshape(kv_s_shape)
    v_s = jnp.concatenate([cv[:, 1:], v_new], axis=1).reshape(kv_s_shape)
    conv_s = jnp.concatenate([state_conv[0][:, 1:], xc_s[:, None, :]], axis=1)[None]
    return (y_p.reshape(batch, seq, D_MODEL), y_s.reshape(nseq, 1, D_MODEL),
            k_p, v_p, conv_p, s_p[None], k_s, v_s, conv_s, s_s[None])
```

```python
import functools
import math

import numpy as np
import jax
import jax.numpy as jnp
from jax import lax
from jax.experimental import pallas as pl
from jax.experimental.pallas import tpu as pltpu
from jax.experimental.pallas import tpu_sc as plsc

F32 = jnp.float32
BF16 = jnp.bfloat16

D_MODEL = 1024
ATT_HEADS = 8
ATT_KV_HEADS = 2
HEAD_DIM = 64
GQA = ATT_HEADS // ATT_KV_HEADS
WINDOW = 128
ATT_BLOCK = 128
N_BUCKETS = 32
DN_HEADS = 8
DN_DK = 64
DN_DV = 64
CONV_WIDTH = 4
DN_CHUNK = 64
ATT_WIDTH = ATT_HEADS * HEAD_DIM
KV_WIDTH = ATT_KV_HEADS * HEAD_DIM
DN_WIDTH = DN_HEADS * DN_DV
CONV_CH = 3 * DN_WIDTH
N_GROUPS = 4
EXPERTS_PER_GROUP = 8
N_EXPERTS = N_GROUPS * EXPERTS_PER_GROUP
D_EXPERT = 256
PLE_DIM = 256
EPS = 1e-6
NEG_INF = float("-inf")

ATT_COLS = ATT_WIDTH + 2 * KV_WIDTH
LANES = 128
IN_COLS = ATT_COLS + CONV_CH + DN_WIDTH + LANES
ROUTER_OFF = N_GROUPS
VMEM_LIMIT = 48 * 1024 * 1024
ROW_TM = 512


def _params(*sem):
    return pltpu.CompilerParams(dimension_semantics=sem, vmem_limit_bytes=VMEM_LIMIT)


def _mm(a, b):
    return jnp.dot(a.astype(BF16), b.astype(BF16), preferred_element_type=F32)


def _mm_nt(a, b):
    return lax.dot_general(a.astype(BF16), b.astype(BF16), (((1,), (1,)), ((), ())),
                           preferred_element_type=F32)


def _mm_tn(a, b):
    return lax.dot_general(a.astype(BF16), b.astype(BF16), (((0,), (0,)), ((), ())),
                           preferred_element_type=F32)


def _split3(x):
    h1 = x.astype(BF16)
    r1 = x - h1.astype(F32)
    h2 = r1.astype(BF16)
    h3 = (r1 - h2.astype(F32)).astype(BF16)
    return h1, h2, h3


def _mm_sel_rhs(x, sel):
    h1, h2, h3 = _split3(x)
    d = lambda h: jnp.dot(h, sel, preferred_element_type=F32)
    return d(h1) + d(h2) + d(h3)


def _mm_sel_lhs(sel, x):
    h1, h2, h3 = _split3(x)
    d = lambda h: jnp.dot(sel, h, preferred_element_type=F32)
    return d(h1) + d(h2) + d(h3)


def _mm3(a, b):
    ah = a.astype(BF16)
    al = (a - ah.astype(F32)).astype(BF16)
    bh = b.astype(BF16)
    bl = (b - bh.astype(F32)).astype(BF16)
    d = lambda u, v: jnp.dot(u, v, preferred_element_type=F32)
    return d(ah, bh) + d(ah, bl) + d(al, bh)


def _sigmoid(x):
    return 1.0 / (1.0 + jnp.exp(-x))


def _silu(x):
    return x * _sigmoid(x)


def _softplus(x):
    return jnp.maximum(x, 0.0) + jnp.log1p(jnp.exp(-jnp.abs(x)))


def _rmsnorm(x, g):
    return x * lax.rsqrt(jnp.mean(x * x, axis=-1, keepdims=True) + EPS) * g


def _t5_bucket_np(dist):
    max_exact = N_BUCKETS // 2
    d = np.maximum(dist, 0)
    ratio = (np.log(np.maximum(d, 1).astype(np.float32) / np.float32(max_exact))
             / np.float32(math.log(WINDOW / max_exact))).astype(np.float32)
    large = np.minimum(max_exact + (ratio * np.float32(N_BUCKETS - max_exact)).astype(np.int32),
                       N_BUCKETS - 1)
    return np.where(d < max_exact, d, large).astype(np.int32)


def _bias_lookup(bucket, rb_ref, h):
    acc = jnp.zeros(bucket.shape, F32)
    for t in range(N_BUCKETS):
        acc = jnp.where(bucket == t, rb_ref[t, h], acc)
    return acc


def _inproj_kernel(x_ref, g_ref, w_ref, att_ref, xc_ref, dz_ref, ba_ref):
    xn = _rmsnorm(x_ref[...], g_ref[...]).astype(BF16)
    o0, o1, o2 = ATT_COLS, ATT_COLS + CONV_CH, ATT_COLS + CONV_CH + DN_WIDTH
    att_ref[...] = jnp.dot(xn, w_ref[:, :o0], preferred_element_type=F32)
    xc_ref[...] = jnp.dot(xn, w_ref[:, o0:o1], preferred_element_type=F32)
    dz_ref[...] = jnp.dot(xn, w_ref[:, o1:o2], preferred_element_type=F32)
    ba_ref[...] = jnp.dot(xn, w_ref[:, o2:], preferred_element_type=F32)


def _inproj(x, g, w):
    t = x.shape[0]
    tm = min(t, ROW_TM)
    row = lambda n: pl.BlockSpec((tm, n), lambda i: (i, 0))
    full = lambda a: pl.BlockSpec(a.shape, lambda i: (0,) * a.ndim)
    return pl.pallas_call(
        _inproj_kernel,
        grid=(t // tm,),
        in_specs=[row(D_MODEL), full(g), full(w)],
        out_specs=[row(ATT_COLS), row(CONV_CH), row(DN_WIDTH), row(LANES)],
        out_shape=[jax.ShapeDtypeStruct((t, n), F32) for n in (ATT_COLS, CONV_CH, DN_WIDTH, LANES)],
        compiler_params=_params("parallel"),
        name="inproj",
    )(x, g, w)


GROUP_ROWS = GQA * ATT_BLOCK


def _attn_prompt_kernel(cur_ref, prev_ref, bucket_ref, rb_ref, sink_ref, o_ref, bias_scr, sink_scr):
    i = pl.program_id(0)
    nseq = cur_ref.shape[0]

    @pl.when(i == 0)
    def _():
        qi = lax.broadcasted_iota(jnp.int32, (ATT_BLOCK, 2 * ATT_BLOCK), 0)
        kj = lax.broadcasted_iota(jnp.int32, (ATT_BLOCK, 2 * ATT_BLOCK), 1)
        dist = qi + ATT_BLOCK - kj
        band = jnp.logical_and(dist >= 0, dist < WINDOW)
        bucket = bucket_ref[...]
        hrow = lax.broadcasted_iota(jnp.int32, (GROUP_ROWS, 1), 0) // ATT_BLOCK
        for g in range(ATT_KV_HEADS):
            sink_col = jnp.zeros((GROUP_ROWS, 1), F32)
            for hh in range(GQA):
                h = g * GQA + hh
                bias = jnp.where(band, _bias_lookup(bucket, rb_ref, h), NEG_INF)
                bias_scr[0, g, hh * ATT_BLOCK:(hh + 1) * ATT_BLOCK, :] = bias
                bias_scr[1, g, hh * ATT_BLOCK:(hh + 1) * ATT_BLOCK, :] = jnp.where(kj >= ATT_BLOCK, bias, NEG_INF)
                sink_col = jnp.where(hrow == hh, sink_ref[h], sink_col)
            sink_scr[g] = sink_col

    first = (i == 0).astype(jnp.int32)
    probs = [(b, g) for b in range(nseq) for g in range(ATT_KV_HEADS)]
    scores = []
    for b, g in probs:
        cur = cur_ref[b]
        prev = prev_ref[b]
        q = jnp.concatenate([cur[:, (g * GQA + hh) * HEAD_DIM:(g * GQA + hh + 1) * HEAD_DIM]
                             for hh in range(GQA)], axis=0) * (HEAD_DIM ** -0.5)
        kcol = slice(ATT_WIDTH + g * HEAD_DIM, ATT_WIDTH + (g + 1) * HEAD_DIM)
        k2 = jnp.concatenate([prev[:, kcol], cur[:, kcol]], axis=0)
        scores.append(_mm_nt(q, k2) + bias_scr[first, g])
    probs_p, dens = [], []
    for (b, g), s in zip(probs, scores):
        sink = sink_scr[g]
        m = jnp.maximum(jnp.max(s, axis=-1, keepdims=True), sink)
        p = jnp.exp(s - m)
        dens.append(jnp.sum(p, axis=-1, keepdims=True) + jnp.exp(sink - m))
        probs_p.append(p)
    outs = {}
    for (b, g), p, den in zip(probs, probs_p, dens):
        vcol = slice(ATT_WIDTH + KV_WIDTH + g * HEAD_DIM, ATT_WIDTH + KV_WIDTH + (g + 1) * HEAD_DIM)
        v2 = jnp.concatenate([prev_ref[b][:, vcol], cur_ref[b][:, vcol]], axis=0)
        outs[b, g] = _mm(p, v2) / den
    for b in range(nseq):
        o_ref[b] = jnp.concatenate([outs[b, g][hh * ATT_BLOCK:(hh + 1) * ATT_BLOCK, :]
                                    for g in range(ATT_KV_HEADS) for hh in range(GQA)], axis=1)


def _attn_prompt(att, bucket, rel_bias, sink, batch, seq):
    nb = seq // ATT_BLOCK
    smem = pl.BlockSpec(memory_space=pltpu.SMEM)
    att3 = att.reshape(batch, seq, ATT_COLS)
    out = pl.pallas_call(
        _attn_prompt_kernel,
        grid=(nb,),
        in_specs=[
            pl.BlockSpec((batch, ATT_BLOCK, ATT_COLS), lambda i: (0, i, 0)),
            pl.BlockSpec((batch, ATT_BLOCK, ATT_COLS), lambda i: (0, jnp.maximum(i - 1, 0), 0)),
            pl.BlockSpec(bucket.shape, lambda i: (0, 0)),
            smem, smem,
        ],
        out_specs=pl.BlockSpec((batch, ATT_BLOCK, ATT_WIDTH), lambda i: (0, i, 0)),
        out_shape=jax.ShapeDtypeStruct((batch, seq, ATT_WIDTH), F32),
        scratch_shapes=[pltpu.VMEM((2, ATT_KV_HEADS, GROUP_ROWS, 2 * ATT_BLOCK), F32),
                        pltpu.VMEM((ATT_KV_HEADS, GROUP_ROWS, 1), F32)],
        compiler_params=_params("arbitrary"),
        name="attn_prompt",
    )(att3, att3, bucket, rel_bias, sink)
    return out.reshape(batch * seq, ATT_WIDTH)


ATT_S_BB = 8


def _attn_sample_kernel(att_ref, ck_ref, cv_ref, bucket_ref, rb_ref, sink_ref, o_ref,
                        bias_scr, col_scr):
    hrow = lax.broadcasted_iota(jnp.int32, (ATT_HEADS, LANES), 0)
    lane = lax.broadcasted_iota(jnp.int32, (ATT_HEADS, LANES), 1)

    @pl.when(pl.program_id(0) == 0)
    def _():
        bucket = jnp.broadcast_to(bucket_ref[...], (ATT_HEADS, LANES))
        bias = jnp.zeros((ATT_HEADS, LANES), F32)
        cols = jnp.zeros((ATT_HEADS, LANES), F32)
        for h in range(ATT_HEADS):
            bias = jnp.where(hrow == h, _bias_lookup(bucket, rb_ref, h), bias)
            cols = jnp.where(jnp.logical_and(hrow == h, lane == 0), sink_ref[h], cols)
            cols = jnp.where(jnp.logical_and(hrow == h, lane == 1), rb_ref[0, h], cols)
        bias_scr[...] = jnp.where(lane >= 1, bias, NEG_INF)
        col_scr[...] = cols

    bias_c = bias_scr[...]
    sink = col_scr[:, 0:1]
    bias_n = col_scr[:, 1:2]
    same_group = (hrow // GQA) == (lane // HEAD_DIM)
    low_group = lax.broadcasted_iota(jnp.int32, (ATT_HEADS, HEAD_DIM), 0) < GQA
    for b in range(ATT_S_BB):
        row = att_ref[b:b + 1, :]
        q = row[:, :ATT_WIDTH] * (HEAD_DIM ** -0.5)
        kn = row[:, ATT_WIDTH:ATT_WIDTH + KV_WIDTH]
        vn = row[:, ATT_WIDTH + KV_WIDTH:]
        qh = jnp.concatenate([q[:, h * HEAD_DIM:(h + 1) * HEAD_DIM] for h in range(ATT_HEADS)], axis=0)
        q_bd = jnp.where(same_group, jnp.concatenate([qh, qh], axis=1), 0.0)
        rnd = lambda a: a.astype(BF16).astype(F32)
        s_c = _mm_nt(q_bd, ck_ref[b]) + bias_c
        s_n = jnp.sum(rnd(q_bd) * rnd(kn), axis=-1, keepdims=True) + bias_n
        m = jnp.maximum(jnp.maximum(jnp.max(s_c, axis=-1, keepdims=True), s_n), sink)
        p_c = jnp.exp(s_c - m)
        p_n = jnp.exp(s_n - m)
        den = jnp.sum(p_c, axis=-1, keepdims=True) + p_n + jnp.exp(sink - m)
        o_full = _mm(p_c / den, cv_ref[b]) + rnd(p_n / den) * rnd(vn)
        o_sel = jnp.where(low_group, o_full[:, :HEAD_DIM], o_full[:, HEAD_DIM:])
        o_ref[b:b + 1, :] = jnp.concatenate([o_sel[h:h + 1, :] for h in range(ATT_HEADS)], axis=1)


def _attn_sample(att, ck, cv, bucket, rel_bias, sink):
    nseq = att.shape[0]
    smem = pl.BlockSpec(memory_space=pltpu.SMEM)
    cache = pl.BlockSpec((ATT_S_BB, WINDOW, KV_WIDTH), lambda i: (i, 0, 0))
    return pl.pallas_call(
        _attn_sample_kernel,
        grid=(nseq // ATT_S_BB,),
        in_specs=[pl.BlockSpec((ATT_S_BB, ATT_COLS), lambda i: (i, 0)), cache, cache,
                  pl.BlockSpec(bucket.shape, lambda i: (0, 0)), smem, smem],
        out_specs=pl.BlockSpec((ATT_S_BB, ATT_WIDTH), lambda i: (i, 0)),
        out_shape=jax.ShapeDtypeStruct((nseq, ATT_WIDTH), F32),
        scratch_shapes=[pltpu.VMEM((ATT_HEADS, LANES), F32), pltpu.VMEM((ATT_HEADS, LANES), F32)],
        compiler_params=_params("arbitrary"),
        name="attn_sample",
    )(att, ck, cv, bucket, rel_bias, sink)


GDN_TB = 128
GDN_NC = GDN_TB // DN_CHUNK
TAIL = 8


def _gdn_gates(ba, alog, dtb):
    beta = _sigmoid(ba)
    g = -jnp.exp(alog) * _softplus(ba + dtb)
    return beta, g


PAIR = 2 * DN_DK
N_PAIRS = DN_WIDTH // PAIR


def _pair_diag(x, lo):
    xb = x.astype(BF16)
    zero = jnp.zeros_like(xb)
    return jnp.concatenate([jnp.where(lo, xb, zero), jnp.where(lo, zero, xb)], axis=0)


def _gdn_prompt_kernel(xc_ref, dz_ref, ba_ref, cw_ref, alog_ref, dtb_ref, dnx_ref,
                       hsum_ref, expb_ref, expg_ref, ltri_ref,
                       o_ref, s_out_ref, xp_scr, s_scr):
    i = pl.program_id(0)
    nb = xc_ref.shape[0]

    @pl.when(i == 0)
    def _():
        xp_scr[:, 0:TAIL, :] = jnp.zeros((nb, TAIL, CONV_CH), F32)
        s_scr[...] = jnp.zeros(s_scr.shape, F32)

    hsum = hsum_ref[...]
    ri = lax.broadcasted_iota(jnp.int32, (DN_CHUNK, PAIR), 0)
    ci = lax.broadcasted_iota(jnp.int32, (DN_CHUNK, PAIR), 1)
    lo = ci < DN_DK
    cj = jnp.where(lo, ci, ci - DN_DK)
    causal = ri >= cj
    strict = ri > cj
    eye = (ri == cj).astype(F32)

    def sel2(x, m):
        hi = x.astype(BF16)
        lw = (x - hi.astype(F32)).astype(BF16)
        return (jnp.dot(hi, m, preferred_element_type=F32) + jnp.dot(lw, m, preferred_element_type=F32))

    def head_sums(z):
        hi = z.astype(BF16)
        lw = (z - hi.astype(F32)).astype(BF16)
        d = lambda a, p: jnp.dot(a[:, p * PAIR:(p + 1) * PAIR], hsum, preferred_element_type=F32)
        return jnp.concatenate([d(hi, p) + d(lw, p) for p in range(N_PAIRS)], axis=1)

    ys = []
    for b in range(nb):
        xc = xc_ref[b]
        xp_scr[b, TAIL:, :] = xc
        y = xp_scr[b, TAIL - 3:TAIL - 3 + GDN_TB, :] * cw_ref[0:1, :]
        y = y + xp_scr[b, TAIL - 2:TAIL - 2 + GDN_TB, :] * cw_ref[1:2, :]
        y = y + xp_scr[b, TAIL - 1:TAIL - 1 + GDN_TB, :] * cw_ref[2:3, :]
        y = y + xc * cw_ref[3:4, :]
        xp_scr[b, 0:TAIL, :] = xc[GDN_TB - TAIL:, :]
        ys.append(_silu(y))
    qk_raw = [y[:, j * DN_WIDTH:(j + 1) * DN_WIDTH] for y in ys for j in range(2)]
    inv_norm = lax.rsqrt(head_sums(jnp.concatenate([a * a for a in qk_raw], axis=0)) + EPS)
    pre = []
    for b in range(nb):
        q = qk_raw[2 * b] * inv_norm[2 * b * GDN_TB:(2 * b + 1) * GDN_TB] * (DN_DK ** -0.5)
        k = qk_raw[2 * b + 1] * inv_norm[(2 * b + 1) * GDN_TB:(2 * b + 2) * GDN_TB]
        v = ys[b][:, 2 * DN_WIDTH:]
        beta_c, g_c = _gdn_gates(ba_ref[b], alog_ref[...], dtb_ref[...])
        beta = sel2(beta_c, expb_ref[...])
        gam_c = _mm_sel_lhs(ltri_ref[...], g_c)
        gam = _mm_sel_rhs(gam_c, expg_ref[...])
        gam_t = gam_c.T
        kb = k * beta
        egam = jnp.exp(gam)
        pre.append(dict(q=q, k=k, kb=kb, vb=v * beta, qg=q * egam, wr=kb * egam, gam=gam, gam_t=gam_t))

    probs = [(b, p) for b in range(nb) for p in range(N_PAIRS)]
    pick = lambda m: jnp.where(lo, m[:DN_DK], m[DN_DK:])
    o_rows = [[] for _ in range(nb)]
    for c in range(GDN_NC):
        r0, r1 = c * DN_CHUNK, (c + 1) * DN_CHUNK
        sl = lambda name, b, p: pre[b][name][r0:r1, p * PAIR:(p + 1) * PAIR]
        raws = []
        for b, p in probs:
            k_p = sl("k", b, p)
            k_rows = jnp.concatenate([jnp.where(lo, k_p, 0.0), jnp.where(lo, 0.0, k_p)], axis=0)
            raws.append(_mm_nt(jnp.concatenate([sl("kb", b, p), sl("q", b, p)], axis=0), k_rows))
        pws, ts, qks = [], [], []
        for (b, p), raw in zip(probs, raws):
            gcol = sl("gam", b, p)
            h0 = DN_HEADS + 2 * p
            gam_t = pre[b]["gam_t"]
            grow = jnp.concatenate([gam_t[h0:h0 + 1, r0:r1], gam_t[h0 + 1:h0 + 2, r0:r1]], axis=1)
            decay = jnp.exp(jnp.where(causal, gcol - grow, NEG_INF))
            a = jnp.where(strict, raw[:DN_CHUNK] * decay, 0.0)
            qks.append(jnp.where(causal, raw[DN_CHUNK:] * decay, 0.0))
            pws.append(-a)
            ts.append(eye - a)
        pws = [_mm(pw, _pair_diag(pw, lo)) for pw in pws]
        for _ in range(4):
            rs = [_mm(jnp.concatenate([pw, t], axis=0), _pair_diag(pw, lo)) for pw, t in zip(pws, ts)]
            pws = [r[:DN_CHUNK] for r in rs]
            ts = [t + r[DN_CHUNK:] for t, r in zip(ts, rs)]
        rs = [_mm(t, _pair_diag(pw, lo)) for pw, t in zip(pws, ts)]
        ts = [t + r for t, r in zip(ts, rs)]
        sols = [_mm(t, jnp.concatenate([_pair_diag(sl("vb", b, p), lo), _pair_diag(sl("wr", b, p), lo)],
                                       axis=1)) for (b, p), t in zip(probs, ts)]
        qkuws = [_mm(qk, jnp.concatenate([_pair_diag(s[:, :PAIR], lo), _pair_diag(s[:, PAIR:], lo)], axis=1))
                 for qk, s in zip(qks, sols)]
        crosses, gls = [], []
        for (b, p), s in zip(probs, sols):
            gam_last = pre[b]["gam"][r1 - 1:r1, p * PAIR:(p + 1) * PAIR]
            kd = sl("k", b, p) * jnp.exp(gam_last - sl("gam", b, p))
            crosses.append(_mm_tn(kd, s))
            gls.append(jnp.exp(gam_last))
        lhs = [jnp.concatenate([pick(cr[:, PAIR:]), sl("qg", b, p) - qkuw[:, PAIR:]], axis=0)
               for (b, p), cr, qkuw in zip(probs, crosses, qkuws)]
        s_olds = [s_scr[b, p] for b, p in probs]
        rs = [_mm(l, _pair_diag(s_old, lo)) for l, s_old in zip(lhs, s_olds)]
        o_pairs = [[] for _ in range(nb)]
        for (b, p), r, s_old, gl, cr, qkuw in zip(probs, rs, s_olds, gls, crosses, qkuws):
            s_scr[b, p] = gl * s_old - r[:DN_DK] + pick(cr[:, :PAIR])
            o_pairs[b].append(r[DN_DK:] + qkuw[:, :PAIR])
        for b in range(nb):
            o_rows[b].append(jnp.concatenate(o_pairs[b], axis=1))

    o_all = jnp.concatenate([jnp.concatenate(rows, axis=0) for rows in o_rows], axis=0)
    inv_rms = lax.rsqrt(head_sums(o_all * o_all) * (1.0 / DN_DV) + EPS)
    for b in range(nb):
        rows = slice(b * GDN_TB, (b + 1) * GDN_TB)
        o_ref[b] = o_all[rows] * inv_rms[rows] * dnx_ref[...] * _silu(dz_ref[b])

    @pl.when(i == pl.num_programs(0) - 1)
    def _():
        for b in range(nb):
            for p in range(N_PAIRS):
                s_p = s_scr[b, p]
                s_out_ref[b, 2 * p] = s_p[:, :DN_DV]
                s_out_ref[b, 2 * p + 1] = s_p[:, DN_DV:]


def _gdn_consts():
    lane = np.arange(DN_WIDTH)
    pl_lane = np.arange(PAIR)
    hsum = (pl_lane[:, None] // DN_DV == pl_lane[None, :] // DN_DV)
    src = np.arange(LANES)
    expb = (src[:, None] == lane[None, :] // DN_DV)
    expg = (src[:, None] == DN_HEADS + lane[None, :] // DN_DV)
    tok = np.arange(GDN_TB)
    ltri = np.logical_and(tok[:, None] >= tok[None, :],
                          tok[:, None] // DN_CHUNK == tok[None, :] // DN_CHUNK)
    as_bf16 = lambda m: jnp.asarray(m.astype(np.float32), dtype=BF16)
    return as_bf16(hsum), as_bf16(expb), as_bf16(expg), as_bf16(ltri)


def _gdn_prompt(xc, dz, ba, conv_w, alog, dtb, dnx, batch, seq):
    nt = seq // GDN_TB
    hsum, expb, expg, ltri = _gdn_consts()
    row = lambda n: pl.BlockSpec((batch, GDN_TB, n), lambda i: (0, i, 0))
    full = lambda a: pl.BlockSpec(a.shape, lambda i: (0,) * a.ndim)
    consts = (conv_w, alog, dtb, dnx, hsum, expb, expg, ltri)
    as3d = lambda a: a.reshape(batch, seq, a.shape[-1])
    o, s = pl.pallas_call(
        _gdn_prompt_kernel,
        grid=(nt,),
        in_specs=[row(CONV_CH), row(DN_WIDTH), row(LANES)] + [full(a) for a in consts],
        out_specs=[row(DN_WIDTH),
                   pl.BlockSpec((batch, DN_HEADS, DN_DK, DN_DV), lambda i: (0, 0, 0, 0))],
        out_shape=[jax.ShapeDtypeStruct((batch, seq, DN_WIDTH), F32),
                   jax.ShapeDtypeStruct((batch, DN_HEADS, DN_DK, DN_DV), F32)],
        scratch_shapes=[pltpu.VMEM((batch, TAIL + GDN_TB, CONV_CH), F32),
                        pltpu.VMEM((batch, N_PAIRS, DN_DK, PAIR), F32)],
        compiler_params=_params("arbitrary"),
        name="gdn_prompt",
    )(as3d(xc), as3d(dz), as3d(ba), *consts)
    return o.reshape(batch * seq, DN_WIDTH), s


GDN_S_BB = 8


def _gdn_sample_kernel(xc_ref, dz_ref, ba_ref, sc_ref, s_ref, cw_ref, alog_ref, dtb_ref, dn_ref,
                       hsum_ref, eye_ref, hsel_ref, hrep_ref, o_ref, s_out_ref):
    xc = xc_ref[...]
    y = sc_ref[0] * cw_ref[0:1, :]
    y = y + sc_ref[1] * cw_ref[1:2, :]
    y = y + sc_ref[2] * cw_ref[2:3, :]
    y = _silu(y + xc * cw_ref[3:4, :])
    hsum = hsum_ref[...]
    q = y[:, :DN_WIDTH]
    k = y[:, DN_WIDTH:2 * DN_WIDTH]
    v = y[:, 2 * DN_WIDTH:]
    q = q * lax.rsqrt(_mm_sel_rhs(q * q, hsum) + EPS) * (DN_DK ** -0.5)
    k = k * lax.rsqrt(_mm_sel_rhs(k * k, hsum) + EPS)
    beta_c, g_c = _gdn_gates(ba_ref[...], alog_ref[...], dtb_ref[...])
    eg_c = jnp.exp(g_c)
    eye = eye_ref[...]
    tr = lambda a: lax.dot_general(a, eye, (((0,), (0,)), ((), ())), precision=lax.Precision.HIGHEST,
                                   preferred_element_type=F32)
    k_t = tr(k)
    q_t = tr(q)
    beta_t = tr(beta_c)
    eg_t = tr(eg_c)
    dz = dz_ref[...]
    dn = dn_ref[...]
    split = lambda r: jnp.concatenate([r[:, h * DN_DV:(h + 1) * DN_DV] for h in range(DN_HEADS)], axis=0)
    hsel = hsel_ref[...]
    hrep = hrep_ref[...]
    for b in range(GDN_S_BB):
        s2 = s_ref[b]
        kc = k_t[:, b:b + 1]
        qc = q_t[:, b:b + 1]
        beta = beta_t[0:DN_HEADS, b:b + 1]
        eg = eg_t[DN_HEADS:2 * DN_HEADS, b:b + 1]
        vh = split(v[b:b + 1, :])
        qh = split(q[b:b + 1, :])
        kh = split(k[b:b + 1, :])
        ks = _mm_sel_lhs(hsel, kc * s2)
        v_new = beta * (vh - eg * ks)
        qs = _mm_sel_lhs(hsel, qc * s2)
        qk = jnp.sum(qh * kh, axis=-1, keepdims=True)
        o = eg * qs + qk * v_new
        rep = _mm_sel_lhs(hrep, jnp.concatenate(
            [v_new, jnp.broadcast_to(eg, (DN_HEADS, DN_DV))], axis=1))
        s_out_ref[b] = s2 * rep[:, DN_DV:] + kc * rep[:, :DN_DV]
        o = _rmsnorm(o, dn) * _silu(split(dz[b:b + 1, :]))
        o_ref[b] = o


def _gdn_sample(xc, dz, ba, sconv_t, state, conv_w, alog, dtb, dn):
    nseq = xc.shape[0]
    lane = np.arange(DN_WIDTH)
    hsum = jnp.asarray((lane[:, None] // DN_DV == lane[None, :] // DN_DV).astype(np.float32), dtype=BF16)
    eye = jnp.eye(GDN_S_BB, dtype=F32)
    hsel_np = (np.arange(DN_HEADS)[:, None] == lane[None, :] // DN_DK).astype(np.float32)
    hsel = jnp.asarray(hsel_np, dtype=BF16)
    hrep = jnp.asarray(hsel_np.T, dtype=BF16)
    row = lambda n: pl.BlockSpec((GDN_S_BB, n), lambda i: (i, 0))
    full = lambda a: pl.BlockSpec(a.shape, lambda i: (0,) * a.ndim)
    st = pl.BlockSpec((GDN_S_BB, DN_HEADS * DN_DK, DN_DV), lambda i: (i, 0, 0))
    consts = (conv_w, alog, dtb, dn, hsum, eye, hsel, hrep)
    return pl.pallas_call(
        _gdn_sample_kernel,
        grid=(nseq // GDN_S_BB,),
        in_specs=[row(CONV_CH), row(DN_WIDTH), row(LANES),
                  pl.BlockSpec((CONV_WIDTH - 1, GDN_S_BB, CONV_CH), lambda i: (0, i, 0)), st]
                 + [full(a) for a in consts],
        out_specs=[pl.BlockSpec((GDN_S_BB, DN_HEADS, DN_DV), lambda i: (i, 0, 0)), st],
        out_shape=[jax.ShapeDtypeStruct((nseq, DN_HEADS, DN_DV), F32),
                   jax.ShapeDtypeStruct(state.shape, F32)],
        compiler_params=_params("parallel"),
        name="gdn_sample",
    )(xc, dz, ba, sconv_t, state, *consts)


def _route(xn, wr):
    logits = jnp.dot(xn, wr, preferred_element_type=F32)
    lane = lax.broadcasted_iota(jnp.int32, logits.shape, 1).astype(F32)
    first_at = lambda hit: jnp.min(jnp.where(hit, lane, float(LANES)), axis=-1, keepdims=True)
    glog = jnp.where(lane < N_GROUPS, logits, NEG_INF)
    gmax = jnp.max(glog, axis=-1, keepdims=True)
    gsel = first_at(glog == gmax)
    pgsel = 1.0 / jnp.sum(jnp.exp(glog - gmax), axis=-1, keepdims=True)
    lo = ROUTER_OFF + gsel * EXPERTS_PER_GROUP
    in_group = jnp.logical_and(lane >= lo, lane < lo + EXPERTS_PER_GROUP)
    elog = jnp.where(in_group, logits, NEG_INF)
    m1 = jnp.max(elog, axis=-1, keepdims=True)
    i1 = first_at(elog == m1)
    z = jnp.sum(jnp.exp(elog - m1), axis=-1, keepdims=True)
    elog2 = jnp.where(lane == i1, NEG_INF, elog)
    m2 = jnp.max(elog2, axis=-1, keepdims=True)
    i2 = first_at(elog2 == m2)
    p1 = 1.0 / z
    p2 = jnp.exp(m2 - m1) / z
    tot = p1 + p2
    return lane, i1, i2, p1 / tot * pgsel, p2 / tot * pgsel


def _outproj(x_ref, oa_ref, od_ref, wo_ref):
    return x_ref[...] + _mm(oa_ref[...], wo_ref[:ATT_WIDTH, :]) + _mm(od_ref[...], wo_ref[ATT_WIDTH:, :])


def _outproj_router_kernel(x_ref, oa_ref, od_ref, wo_ref, g_ref, wr_ref, h_ref, xn_ref, gate_ref):
    h = _outproj(x_ref, oa_ref, od_ref, wo_ref)
    h_ref[...] = h
    xn = _rmsnorm(h, g_ref[...]).astype(BF16)
    xn_ref[...] = xn
    lane, i1, i2, g1, g2 = _route(xn, wr_ref[...])
    gate_ref[...] = jnp.where(lane == i1, g1, 0.0) + jnp.where(lane == i2, g2, 0.0)


def _outproj_router(x, oa, od, wo, g, wr):
    t = x.shape[0]
    tm = min(t, 256)
    row = lambda n: pl.BlockSpec((tm, n), lambda i: (i, 0))
    full = lambda a: pl.BlockSpec(a.shape, lambda i: (0,) * a.ndim)
    return pl.pallas_call(
        _outproj_router_kernel,
        grid=(t // tm,),
        in_specs=[row(D_MODEL), row(ATT_WIDTH), row(DN_WIDTH), full(wo), full(g), full(wr)],
        out_specs=[row(D_MODEL), row(D_MODEL), row(LANES)],
        out_shape=[jax.ShapeDtypeStruct((t, D_MODEL), F32), jax.ShapeDtypeStruct((t, D_MODEL), BF16),
                   jax.ShapeDtypeStruct((t, LANES), F32)],
        compiler_params=_params("parallel"),
        name="outproj_router",
    )(x, oa, od, wo, g, wr)


def _moe_kernel(xn_ref, gate_ref, wg_ref, wu_ref, wd_ref, o_ref):
    e = pl.program_id(1)
    xn = xn_ref[...]
    lane = lax.broadcasted_iota(jnp.int32, gate_ref.shape, 1)
    gate = jnp.sum(jnp.where(lane == e + ROUTER_OFF, gate_ref[...], 0.0), axis=-1, keepdims=True)
    hg = jnp.dot(xn, wg_ref[...].astype(BF16), preferred_element_type=F32)
    hu = jnp.dot(xn, wu_ref[...].astype(BF16), preferred_element_type=F32)
    hm = _silu(hg) * hu * gate
    y = jnp.dot(hm.astype(BF16), wd_ref[...].astype(BF16), preferred_element_type=F32)

    @pl.when(e == 0)
    def _():
        o_ref[...] = y

    @pl.when(e > 0)
    def _():
        o_ref[...] += y


def _moe(xn, gates, wg, wu, wd):
    t = xn.shape[0]
    tm = min(t, 1024)
    return pl.pallas_call(
        _moe_kernel,
        grid=(t // tm, N_EXPERTS),
        in_specs=[pl.BlockSpec((tm, D_MODEL), lambda i, e: (i, 0)),
                  pl.BlockSpec((tm, LANES), lambda i, e: (i, 0)),
                  pl.BlockSpec((None, D_MODEL, D_EXPERT), lambda i, e: (e, 0, 0)),
                  pl.BlockSpec((None, D_MODEL, D_EXPERT), lambda i, e: (e, 0, 0)),
                  pl.BlockSpec((None, D_EXPERT, D_MODEL), lambda i, e: (e, 0, 0))],
        out_specs=pl.BlockSpec((tm, D_MODEL), lambda i, e: (i, 0)),
        out_shape=jax.ShapeDtypeStruct((t, D_MODEL), F32),
        compiler_params=_params("parallel", "arbitrary"),
        name="moe",
    )(xn, gates, wg, wu, wd)


MOE_TM = 256
POS_TM = 512
INFO_G1, INFO_G2, INFO_E1, INFO_E2 = 0, 1, 2, 3
DMA_UNROLL = 8


def _moe_tiles(t):
    return (2 * t) // MOE_TM + N_EXPERTS


HALF = D_MODEL // 2
U32 = jnp.uint32


def _pack_rows(x):
    bits = lambda v: lax.bitcast_convert_type(v.astype(BF16).astype(F32), U32)
    return bits(x[:, HALF:]) | (bits(x[:, :HALF]) >> 16)


def _unpack_rows(w):
    lo = lax.bitcast_convert_type(w << 16, F32)
    hi = lax.bitcast_convert_type(w & jnp.uint32(0xFFFF0000), F32)
    return lo, hi


def _route_kernel(x_ref, oa_ref, od_ref, wo_ref, g_ref, wr_ref, h_ref, xn_ref, info_ref):
    h = _outproj(x_ref, oa_ref, od_ref, wo_ref)
    h_ref[...] = h
    xn = _rmsnorm(h, g_ref[...])
    xn_ref[...] = _pack_rows(xn)
    lane, i1, i2, g1, g2 = _route(xn.astype(BF16), wr_ref[...])
    info = jnp.where(lane == INFO_G1, g1, 0.0) + jnp.where(lane == INFO_G2, g2, 0.0)
    info = info + jnp.where(lane == INFO_E1, i1, 0.0) + jnp.where(lane == INFO_E2, i2, 0.0)
    info_ref[...] = info


def _route_sparse(x, oa, od, wo, g, wr):
    t = x.shape[0]
    tm = ROW_TM
    row = lambda n: pl.BlockSpec((tm, n), lambda i: (i, 0))
    full = lambda a: pl.BlockSpec(a.shape, lambda i: (0,) * a.ndim)
    return pl.pallas_call(
        _route_kernel,
        grid=(t // tm,),
        in_specs=[row(D_MODEL), row(ATT_WIDTH), row(DN_WIDTH), full(wo), full(g), full(wr)],
        out_specs=[row(D_MODEL), row(HALF), row(LANES)],
        out_shape=[jax.ShapeDtypeStruct((t, D_MODEL), F32), jax.ShapeDtypeStruct((t, HALF), U32),
                   jax.ShapeDtypeStruct((t, LANES), F32)],
        compiler_params=_params("parallel"),
        name="route",
    )(x, oa, od, wo, g, wr)


def _positions_kernel(info_ref, ltri_ref, utri_ref, pos_ref, cnt_ref, run_scr, off_scr):
    phase = pl.program_id(0)
    i = pl.program_id(1)
    info = info_ref[...]
    lane = lax.broadcasted_iota(jnp.int32, info.shape, 1).astype(F32)
    hit1 = lane == info[:, INFO_E1:INFO_E1 + 1]
    hit2 = lane == info[:, INFO_E2:INFO_E2 + 1]
    onehot = jnp.logical_or(hit1, hit2).astype(F32)

    @pl.when(jnp.logical_and(phase == 0, i == 0))
    def _():
        run_scr[...] = jnp.zeros(run_scr.shape, F32)

    @pl.when(jnp.logical_and(phase == 1, i == 0))
    def _():
        cnt = run_scr[...]
        cnt_ref[...] = cnt
        tiles = jnp.floor((cnt + (MOE_TM - 1)) * (1.0 / MOE_TM))
        off_scr[...] = MOE_TM * jnp.dot(tiles.astype(BF16), utri_ref[...], preferred_element_type=F32)
        run_scr[...] = jnp.zeros(run_scr.shape, F32)

    @pl.when(phase == 0)
    def _():
        pos_ref[...] = jnp.zeros(pos_ref.shape, jnp.int32)

    @pl.when(phase == 1)
    def _():
        before = (jnp.dot(ltri_ref[...], onehot.astype(BF16), preferred_element_type=F32)
                  + run_scr[...] + off_scr[...])
        pos1 = jnp.sum(jnp.where(hit1, before, 0.0), axis=-1, keepdims=True)
        pos2 = jnp.sum(jnp.where(hit2, before, 0.0), axis=-1, keepdims=True)
        pos_ref[...] = (jnp.where(lane == 0, pos1, 0.0) + jnp.where(lane == 1, pos2, 0.0)).astype(jnp.int32)

    run_scr[...] += jnp.sum(onehot, axis=0, keepdims=True)


def _positions(info):
    t = info.shape[0]
    tm = min(t, POS_TM)
    tok = np.arange(tm)
    ltri = jnp.asarray((tok[:, None] > tok[None, :]).astype(np.float32), dtype=BF16)
    ln = np.arange(LANES)
    utri = jnp.asarray((ln[:, None] < ln[None, :]).astype(np.float32), dtype=BF16)
    full = lambda a: pl.BlockSpec(a.shape, lambda ph, i: (0,) * a.ndim)
    return pl.pallas_call(
        _positions_kernel,
        grid=(2, t // tm),
        in_specs=[pl.BlockSpec((tm, LANES), lambda ph, i: (i, 0)), full(ltri), full(utri)],
        out_specs=[pl.BlockSpec((tm, LANES), lambda ph, i: (i * ph, 0)),
                   pl.BlockSpec((1, LANES), lambda ph, i: (0, 0))],
        out_shape=[jax.ShapeDtypeStruct((t, LANES), jnp.int32), jax.ShapeDtypeStruct((1, LANES), F32)],
        scratch_shapes=[pltpu.VMEM((1, LANES), F32), pltpu.VMEM((1, LANES), F32)],
        compiler_params=_params("arbitrary", "arbitrary"),
        name="positions",
    )(info, ltri, utri)


def _row_copy(src_hbm, src_row, dst_hbm, dst_row, sem):
    return pltpu.make_async_copy(src_hbm.at[pl.ds(src_row, 1)], dst_hbm.at[pl.ds(dst_row, 1)], sem)


SCATTER_SLOTS = 3


def _scatter_kernel(pos1_ref, pos2_ref, last_ref, used_ref, nt_ref, xn_hbm, zero_hbm, xs_hbm,
                    buf, lsem, sem, zsem, *, n_tok):
    max_tiles = xs_hbm.shape[0] // MOE_TM

    def zero_tile(tile):
        return pltpu.make_async_copy(zero_hbm, xs_hbm.at[pl.ds(tile * MOE_TM, MOE_TM)], zsem)

    def for_unused(fn):
        def body(tile, carry):
            fn(tile)
            return carry
        lax.fori_loop(nt_ref[0], max_tiles, body, 0)

    for e in range(N_EXPERTS):
        @pl.when(used_ref[e] > 0)
        def _():
            zero_tile(last_ref[e]).start()
    for_unused(lambda tile: zero_tile(tile).start())
    for e in range(N_EXPERTS):
        @pl.when(used_ref[e] > 0)
        def _():
            zero_tile(last_ref[e]).wait()
    for_unused(lambda tile: zero_tile(tile).wait())

    tm = buf.shape[1]
    n = n_tok // tm

    def load(i):
        return pltpu.make_async_copy(xn_hbm.at[pl.ds(i * tm, tm)], buf.at[i % SCATTER_SLOTS],
                                     lsem.at[i % SCATTER_SLOTS])

    def wait_rows(slot):
        pltpu.make_async_copy(xs_hbm.at[pl.ds(0, 2 * tm)], xs_hbm.at[pl.ds(0, 2 * tm)], sem.at[slot]).wait()

    load(0).start()
    load(1).start()

    def step(i, carry):
        slot = i % SCATTER_SLOTS
        load(i).wait()

        def body(j, c2):
            tok = i * tm + j
            src = buf.at[slot, pl.ds(j, 1)]
            pltpu.make_async_copy(src, xs_hbm.at[pl.ds(pos1_ref[tok], 1)], sem.at[slot]).start()
            pltpu.make_async_copy(src, xs_hbm.at[pl.ds(pos2_ref[tok], 1)], sem.at[slot]).start()
            return c2
        lax.fori_loop(0, tm, body, 0, unroll=DMA_UNROLL)

        @pl.when(i >= 1)
        def _():
            wait_rows((i - 1) % SCATTER_SLOTS)

        @pl.when(i + 2 < n)
        def _():
            load(i + 2).start()
        return carry
    lax.fori_loop(0, n, step, 0)
    wait_rows((n - 1) % SCATTER_SLOTS)


def _scatter_rows(xn, pos1, pos2, last_tile, used, n_tiles, n_rows):
    t = xn.shape[0]
    zero = jnp.zeros((MOE_TM, D_MODEL), F32)
    any_spec = pl.BlockSpec(memory_space=pl.ANY)
    return pl.pallas_call(
        functools.partial(_scatter_kernel, n_tok=t),
        grid_spec=pltpu.PrefetchScalarGridSpec(
            num_scalar_prefetch=5, grid=(1,),
            in_specs=[any_spec, any_spec], out_specs=any_spec,
            scratch_shapes=[pltpu.VMEM((SCATTER_SLOTS, MOE_TM, D_MODEL), F32),
                            pltpu.SemaphoreType.DMA((SCATTER_SLOTS,)),
                            pltpu.SemaphoreType.DMA((SCATTER_SLOTS,)),
                            pltpu.SemaphoreType.DMA]),
        out_shape=jax.ShapeDtypeStruct((n_rows, D_MODEL), F32),
        compiler_params=_params("arbitrary"),
        name="scatter_rows",
    )(pos1, pos2, last_tile, used, n_tiles, xn, zero)


def _experts_kernel(te_ref, tv_ref, nt_ref, xs_ref, wg_ref, wu_ref, wd_ref, ys_ref, wg_s, wu_s, wd_s):
    i = pl.program_id(0)
    used = i < nt_ref[0]

    @pl.when(jnp.logical_or(i == 0, te_ref[i] != te_ref[jnp.maximum(i - 1, 0)]))
    def _():
        wg_s[...] = wg_ref[...].astype(BF16)
        wu_s[...] = wu_ref[...].astype(BF16)
        wd_s[...] = wd_ref[...].astype(BF16)

    @pl.when(used)
    def _():
        row = lax.broadcasted_iota(jnp.int32, xs_ref.shape, 0)
        x_lo, x_hi = _unpack_rows(jnp.where(row < tv_ref[i], xs_ref[...], jnp.uint32(0)))
        x_lo = x_lo.astype(BF16)
        x_hi = x_hi.astype(BF16)
        up = lambda w_s: (jnp.dot(x_lo, w_s[:HALF, :], preferred_element_type=F32)
                          + jnp.dot(x_hi, w_s[HALF:, :], preferred_element_type=F32))
        hm = (_silu(up(wg_s)) * up(wu_s)).astype(BF16)
        ys_ref[...] = _pack_rows(jnp.dot(hm, wd_s[...], preferred_element_type=F32))

    @pl.when(jnp.logical_not(used))
    def _():
        ys_ref[...] = jnp.zeros(ys_ref.shape, U32)


def _experts(xs, tile_expert, tile_valid, n_tiles, wg, wu, wd):
    max_tiles = xs.shape[0] // MOE_TM
    rows = pl.BlockSpec((MOE_TM, HALF), lambda i, te, tv, nt: (i, 0))
    wspec = lambda shape: pl.BlockSpec((None,) + shape, lambda i, te, tv, nt: (te[i], 0, 0))
    return pl.pallas_call(
        _experts_kernel,
        grid_spec=pltpu.PrefetchScalarGridSpec(
            num_scalar_prefetch=3, grid=(max_tiles,),
            in_specs=[rows, wspec((D_MODEL, D_EXPERT)), wspec((D_MODEL, D_EXPERT)),
                      wspec((D_EXPERT, D_MODEL))],
            out_specs=rows,
            scratch_shapes=[pltpu.VMEM((D_MODEL, D_EXPERT), BF16), pltpu.VMEM((D_MODEL, D_EXPERT), BF16),
                            pltpu.VMEM((D_EXPERT, D_MODEL), BF16)]),
        out_shape=jax.ShapeDtypeStruct(xs.shape, U32),
        compiler_params=_params("arbitrary"),
        name="experts",
    )(tile_expert, tile_valid, n_tiles, xs, wg, wu, wd)


def _ple_gather_kernel(pos1_ref, pos2_ref, h_ref, info_ref, p_ref, wpp_ref, wpg_ref, gp_ref, gf_ref,
                       ys_hbm, y_ref, ybuf, sem):
    i = pl.program_id(0)
    n = pl.num_programs(0)
    tm = h_ref.shape[0]

    def issue(tile, slot):
        def body(j, carry):
            tok = tile * tm + j
            pltpu.make_async_copy(ys_hbm.at[pl.ds(pos1_ref[tok], 1)], ybuf.at[slot, 0, pl.ds(j, 1)],
                                  sem.at[slot]).start()
            pltpu.make_async_copy(ys_hbm.at[pl.ds(pos2_ref[tok], 1)], ybuf.at[slot, 1, pl.ds(j, 1)],
                                  sem.at[slot]).start()
            return carry
        lax.fori_loop(0, tm, body, 0, unroll=DMA_UNROLL)

    @pl.when(i == 0)
    def _():
        issue(0, 0)

    @pl.when(i + 1 < n)
    def _():
        issue(i + 1, (i + 1) % 2)

    slot = i % 2
    pltpu.make_async_copy(ybuf.at[slot], ybuf.at[slot], sem.at[slot]).wait()
    info = info_ref[...]
    moe = info[:, INFO_G1:INFO_G1 + 1] * ybuf[slot, 0] + info[:, INFO_G2:INFO_G2 + 1] * ybuf[slot, 1]
    h = h_ref[...] + moe
    hn = _rmsnorm(h, gp_ref[...])
    h = h + _mm(p_ref[...], wpp_ref[...]) * _sigmoid(_mm(hn, wpg_ref[...]))
    y_ref[...] = _rmsnorm(h, gf_ref[...])


def _ple_gather(h, info, p, ys, pos1, pos2, wpp, wpg, gp, gf):
    t = h.shape[0]
    tm = 256
    row = lambda n: pl.BlockSpec((tm, n), lambda i, p1, p2: (i, 0))
    full = lambda a: pl.BlockSpec(a.shape, lambda i, p1, p2: (0,) * a.ndim)
    return pl.pallas_call(
        _ple_gather_kernel,
        grid_spec=pltpu.PrefetchScalarGridSpec(
            num_scalar_prefetch=2, grid=(t // tm,),
            in_specs=[row(D_MODEL), row(LANES), row(PLE_DIM), full(wpp), full(wpg), full(gp), full(gf),
                      pl.BlockSpec(memory_space=pl.ANY)],
            out_specs=row(D_MODEL),
            scratch_shapes=[pltpu.VMEM((2, 2, tm, D_MODEL), F32), pltpu.SemaphoreType.DMA((2,))]),
        out_shape=jax.ShapeDtypeStruct((t, D_MODEL), F32),
        compiler_params=_params("arbitrary"),
        name="ple_gather",
    )(pos1, pos2, h, info, p, wpp, wpg, gp, gf, ys)


SC_IDX = 128
SC_ROWS = 64
SC_WORKERS = 32


def _sc_mesh():
    return plsc.VectorSubcoreMesh(core_axis_name="c", subcore_axis_name="s")


def _sc_windows(t, fn):
    per_worker = t // SC_WORKERS
    worker = lax.axis_index(("c", "s"))

    @pl.loop(0, per_worker // SC_IDX)
    def _(w):
        fn(worker * per_worker + w * SC_IDX)


def _sc_scatter_rows(xn, pos1, pos2, n_rows):
    t, d = xn.shape
    assert t % (SC_WORKERS * SC_IDX) == 0
    idx_t = pltpu.VMEM((1, SC_IDX), jnp.int32)

    @pl.kernel(out_type=jax.ShapeDtypeStruct((n_rows, d), xn.dtype), mesh=_sc_mesh(),
               scratch_types=[idx_t, idx_t, pltpu.VMEM((SC_ROWS, d), xn.dtype)])
    def scatter(x_hbm, p1_hbm, p2_hbm, o_hbm, i1_v, i2_v, buf):
        def window(base):
            pltpu.sync_copy(p1_hbm.at[:, pl.ds(base, SC_IDX)], i1_v)
            pltpu.sync_copy(p2_hbm.at[:, pl.ds(base, SC_IDX)], i2_v)
            for k in range(SC_IDX // SC_ROWS):
                pltpu.sync_copy(x_hbm.at[pl.ds(base + k * SC_ROWS, SC_ROWS)], buf)
                pltpu.sync_copy(buf, o_hbm.at[i1_v.at[0, pl.ds(k * SC_ROWS, SC_ROWS)]])
                pltpu.sync_copy(buf, o_hbm.at[i2_v.at[0, pl.ds(k * SC_ROWS, SC_ROWS)]])
        _sc_windows(t, window)

    return scatter(xn, pos1.reshape(1, t), pos2.reshape(1, t))


def _sc_gather_rows(ys, pos1, pos2):
    t = pos1.shape[0]
    d = ys.shape[1]
    assert t % (SC_WORKERS * SC_IDX) == 0
    idx_t = pltpu.VMEM((1, SC_IDX), jnp.int32)
    out = jax.ShapeDtypeStruct((t, d), ys.dtype)

    @pl.kernel(out_type=(out, out), mesh=_sc_mesh(),
               scratch_types=[idx_t, idx_t, pltpu.VMEM((SC_ROWS, d), ys.dtype)])
    def gather(y_hbm, p1_hbm, p2_hbm, o1_hbm, o2_hbm, i1_v, i2_v, buf):
        def window(base):
            pltpu.sync_copy(p1_hbm.at[:, pl.ds(base, SC_IDX)], i1_v)
            pltpu.sync_copy(p2_hbm.at[:, pl.ds(base, SC_IDX)], i2_v)
            for k in range(SC_IDX // SC_ROWS):
                rows = pl.ds(base + k * SC_ROWS, SC_ROWS)
                pltpu.sync_copy(y_hbm.at[i1_v.at[0, pl.ds(k * SC_ROWS, SC_ROWS)]], buf)
                pltpu.sync_copy(buf, o1_hbm.at[rows])
                pltpu.sync_copy(y_hbm.at[i2_v.at[0, pl.ds(k * SC_ROWS, SC_ROWS)]], buf)
                pltpu.sync_copy(buf, o2_hbm.at[rows])
        _sc_windows(t, window)

    return gather(ys, pos1.reshape(1, t), pos2.reshape(1, t))


def _ple_sparse_kernel(h_ref, info_ref, y1_ref, y2_ref, p_ref, wpp_ref, wpg_ref, gp_ref, gf_ref, y_ref):
    info = info_ref[...]
    g1 = info[:, INFO_G1:INFO_G1 + 1]
    g2 = info[:, INFO_G2:INFO_G2 + 1]
    y1_lo, y1_hi = _unpack_rows(y1_ref[...])
    y2_lo, y2_hi = _unpack_rows(y2_ref[...])
    moe = jnp.concatenate([g1 * y1_lo + g2 * y2_lo, g1 * y1_hi + g2 * y2_hi], axis=1)
    h = h_ref[...] + moe
    hn = _rmsnorm(h, gp_ref[...])
    h = h + _mm(p_ref[...], wpp_ref[...]) * _sigmoid(_mm(hn, wpg_ref[...]))
    y_ref[...] = _rmsnorm(h, gf_ref[...])


def _ple_sparse(h, info, y1, y2, p, wpp, wpg, gp, gf):
    t = h.shape[0]
    tm = ROW_TM
    row = lambda n: pl.BlockSpec((tm, n), lambda i: (i, 0))
    full = lambda a: pl.BlockSpec(a.shape, lambda i: (0,) * a.ndim)
    return pl.pallas_call(
        _ple_sparse_kernel,
        grid=(t // tm,),
        in_specs=[row(D_MODEL), row(LANES), row(HALF), row(HALF), row(PLE_DIM),
                  full(wpp), full(wpg), full(gp), full(gf)],
        out_specs=row(D_MODEL),
        out_shape=jax.ShapeDtypeStruct((t, D_MODEL), F32),
        compiler_params=_params("parallel"),
        name="ple_sparse",
    )(h, info, y1, y2, p, wpp, wpg, gp, gf)


def _tile_tables(cnt, max_tiles):
    tiles_e = (cnt + (MOE_TM - 1)) // MOE_TM
    ends = jnp.cumsum(tiles_e)
    n_tiles = ends[-1]
    tile = jnp.arange(max_tiles, dtype=jnp.int32)
    idx = jnp.minimum(tile, n_tiles - 1)
    tile_expert = jnp.sum((idx[:, None] >= ends[None, :]).astype(jnp.int32), axis=1)
    mine = tile_expert[:, None] == jnp.arange(N_EXPERTS, dtype=jnp.int32)[None, :]
    of_mine = lambda v: jnp.sum(jnp.where(mine, v[None, :], 0), axis=1)
    valid = jnp.clip(of_mine(cnt) - (idx - of_mine(ends - tiles_e)) * MOE_TM, 0, MOE_TM)
    tile_valid = jnp.where(tile < n_tiles, valid, 0).astype(jnp.int32)
    return (tile_expert, tile_valid, n_tiles.reshape(1), (ends - 1).astype(jnp.int32),
            tiles_e.astype(jnp.int32))


def _ple_final_kernel(h_ref, m_ref, p_ref, wpp_ref, wpg_ref, gp_ref, gf_ref, y_ref):
    h = h_ref[...] + m_ref[...]
    hn = _rmsnorm(h, gp_ref[...])
    h = h + _mm(p_ref[...], wpp_ref[...]) * _sigmoid(_mm(hn, wpg_ref[...]))
    y_ref[...] = _rmsnorm(h, gf_ref[...])


def _ple_final(h, m, p, wpp, wpg, gp, gf):
    t = h.shape[0]
    tm = min(t, 256)
    row = lambda n: pl.BlockSpec((tm, n), lambda i: (i, 0))
    full = lambda a: pl.BlockSpec(a.shape, lambda i: (0,) * a.ndim)
    return pl.pallas_call(
        _ple_final_kernel,
        grid=(t // tm,),
        in_specs=[row(D_MODEL), row(D_MODEL), row(PLE_DIM), full(wpp), full(wpg), full(gp), full(gf)],
        out_specs=row(D_MODEL),
        out_shape=jax.ShapeDtypeStruct((t, D_MODEL), F32),
        compiler_params=_params("parallel"),
        name="ple_final",
    )(h, m, p, wpp, wpg, gp, gf)


def kernel(x_prompt, x_sample, p_prompt, p_sample, cache_k, cache_v, state_conv, state_S, rel_bias, norm_mix, w_in, att_sink, conv_w, dn_A_log, dn_dt_bias, dn_norm, w_out, norm_ffn, w_router_group, w_router_expert, w_gate, w_up, w_down, w_ple_proj, w_ple_gate, norm_ple, norm_final):
    batch, seq, _ = x_prompt.shape
    nseq = x_sample.shape[0]
    assert x_sample.shape[1] == 1 and norm_mix.shape[0] == 1 and cache_k.shape[2] == WINDOW
    assert seq % GDN_TB == 0 and seq % ATT_BLOCK == 0

    wi = w_in[0]
    o_db = ATT_COLS + CONV_CH
    w_in_re = jnp.concatenate(
        [wi[:, :o_db], wi[:, o_db + 2 * DN_HEADS:], wi[:, o_db:o_db + 2 * DN_HEADS],
         jnp.zeros((D_MODEL, LANES - 2 * DN_HEADS), F32)], axis=1).astype(BF16)
    row = lambda a: a.reshape(1, -1).astype(F32)
    pad_lanes = lambda a, off: jnp.zeros((1, LANES), F32).at[0, off:off + a.shape[0]].set(a)
    alog = pad_lanes(dn_A_log[0], DN_HEADS)
    dtb = pad_lanes(dn_dt_bias[0], DN_HEADS)
    dnx = jnp.tile(dn_norm[0], DN_HEADS).reshape(1, DN_WIDTH)
    w_router = jnp.concatenate(
        [w_router_group[0], w_router_expert[0],
         jnp.zeros((D_MODEL, LANES - N_GROUPS - N_EXPERTS), F32)], axis=1).astype(BF16)
    wo = w_out[0].astype(BF16)
    wg, wu, wd = w_gate[0], w_up[0], w_down[0]
    wpp, wpg = w_ple_proj[0].astype(BF16), w_ple_gate[0].astype(BF16)
    sink = att_sink[0]

    qi = np.arange(ATT_BLOCK)[:, None]
    kj = np.arange(2 * ATT_BLOCK)[None, :]
    bucket_p = jnp.asarray(_t5_bucket_np(qi + ATT_BLOCK - kj))
    bucket_s = jnp.asarray(_t5_bucket_np(WINDOW - np.arange(WINDOW)[None, :]))

    def tail(x, o_att, o_dn, p):
        h1, xn2, gates = _outproj_router(x, o_att, o_dn, wo, row(norm_ffn[0]), w_router)
        moe = _moe(xn2, gates, wg, wu, wd)
        return _ple_final(h1, moe, p, wpp, wpg, row(norm_ple[0]), row(norm_final))

    xp = x_prompt.reshape(batch * seq, D_MODEL)
    att_p, xc_p, dz_p, ba_p = _inproj(xp, row(norm_mix[0]), w_in_re)
    o_att_p = _attn_prompt(att_p, bucket_p, rel_bias, sink, batch, seq)
    o_dn_p, s_p = _gdn_prompt(xc_p, dz_p, ba_p, conv_w[0], alog, dtb, dnx, batch, seq)
    h1, xn2, info = _route_sparse(xp, o_att_p, o_dn_p, wo, row(norm_ffn[0]), w_router)
    pos, cnt = _positions(info)
    pos1, pos2 = pos[:, 0], pos[:, 1]
    max_tiles = _moe_tiles(batch * seq)
    cnt_e = cnt[0, ROUTER_OFF:ROUTER_OFF + N_EXPERTS].astype(jnp.int32)
    tile_expert, tile_valid, n_tiles, last_tile, used = _tile_tables(cnt_e, max_tiles)
    xs = _sc_scatter_rows(xn2, pos1, pos2, max_tiles * MOE_TM)
    ys = _experts(xs, tile_expert, tile_valid, n_tiles, wg, wu, wd)
    y1, y2 = _sc_gather_rows(ys, pos1, pos2)
    y_p = _ple_sparse(h1, info, y1, y2, p_prompt[0].reshape(batch * seq, PLE_DIM),
                      wpp, wpg, row(norm_ple[0]), row(norm_final))

    xs = x_sample.reshape(nseq, D_MODEL)
    att_s, xc_s, dz_s, ba_s = _inproj(xs, row(norm_mix[0]), w_in_re)
    ck = cache_k[0].reshape(nseq, WINDOW, KV_WIDTH)
    cv = cache_v[0].reshape(nseq, WINDOW, KV_WIDTH)
    o_att_s = _attn_sample(att_s, ck, cv, bucket_s, rel_bias, sink)
    sconv_t = jnp.swapaxes(state_conv[0], 0, 1)
    o_dn_s, s_s = _gdn_sample(xc_s, dz_s, ba_s, sconv_t,
                              state_S[0].reshape(nseq, DN_HEADS * DN_DK, DN_DV), conv_w[0], alog, dtb,
                              dn_norm[0].reshape(1, DN_DV))
    s_s = s_s.reshape(nseq, DN_HEADS, DN_DK, DN_DV)
    y_s = tail(xs, o_att_s, o_dn_s.reshape(nseq, DN_WIDTH), p_sample[0].reshape(nseq, PLE_DIM))

    att_p3 = att_p.reshape(batch, seq, ATT_COLS)
    kv_shape = (1, batch, WINDOW, ATT_KV_HEADS, HEAD_DIM)
    k_p = att_p3[:, seq - WINDOW:, ATT_WIDTH:ATT_WIDTH + KV_WIDTH].reshape(kv_shape)
    v_p = att_p3[:, seq - WINDOW:, ATT_WIDTH + KV_WIDTH:].reshape(kv_shape)
    conv_p = xc_p.reshape(batch, seq, CONV_CH)[:, seq - (CONV_WIDTH - 1):][None]
    k_new = att_s[:, None, ATT_WIDTH:ATT_WIDTH + KV_WIDTH]
    v_new = att_s[:, None, ATT_WIDTH + KV_WIDTH:]
    kv_s_shape = (1, nseq, WINDOW, ATT_KV_HEADS, HEAD_DIM)
    k_s = jnp.concatenate([ck[:, 1:], k_new], axis=1).reshape(kv_s_shape)
    v_s = jnp.concatenate([cv[:, 1:], v_new], axis=1).reshape(kv_s_shape)
    conv_s = jnp.concatenate([state_conv[0][:, 1:], xc_s[:, None, :]], axis=1)[None]
    return (y_p.reshape(batch, seq, D_MODEL), y_s.reshape(nseq, 1, D_MODEL),
            k_p, v_p, conv_p, s_p[None], k_s, v_s, conv_s, s_s[None])
```

```python
import functools
import math

import numpy as np
import jax
import jax.numpy as jnp
from jax import lax
from jax.experimental import pallas as pl
from jax.experimental.pallas import tpu as pltpu
from jax.experimental.pallas import tpu_sc as plsc

F32 = jnp.float32
BF16 = jnp.bfloat16

D_MODEL = 1024
ATT_HEADS = 8
ATT_KV_HEADS = 2
HEAD_DIM = 64
GQA = ATT_HEADS // ATT_KV_HEADS
WINDOW = 128
ATT_BLOCK = 128
N_BUCKETS = 32
DN_HEADS = 8
DN_DK = 64
DN_DV = 64
CONV_WIDTH = 4
DN_CHUNK = 64
ATT_WIDTH = ATT_HEADS * HEAD_DIM
KV_WIDTH = ATT_KV_HEADS * HEAD_DIM
DN_WIDTH = DN_HEADS * DN_DV
CONV_CH = 3 * DN_WIDTH
N_GROUPS = 4
EXPERTS_PER_GROUP = 8
N_EXPERTS = N_GROUPS * EXPERTS_PER_GROUP
D_EXPERT = 256
PLE_DIM = 256
EPS = 1e-6
NEG_INF = float("-inf")

ATT_COLS = ATT_WIDTH + 2 * KV_WIDTH
LANES = 128
IN_COLS = ATT_COLS + CONV_CH + DN_WIDTH + LANES
ROUTER_OFF = N_GROUPS
VMEM_LIMIT = 48 * 1024 * 1024
ROW_TM = 512


def _params(*sem):
    return pltpu.CompilerParams(dimension_semantics=sem, vmem_limit_bytes=VMEM_LIMIT)


def _mm(a, b):
    return jnp.dot(a.astype(BF16), b.astype(BF16), preferred_element_type=F32)


def _mm_nt(a, b):
    return lax.dot_general(a.astype(BF16), b.astype(BF16), (((1,), (1,)), ((), ())),
                           preferred_element_type=F32)


def _mm_tn(a, b):
    return lax.dot_general(a.astype(BF16), b.astype(BF16), (((0,), (0,)), ((), ())),
                           preferred_element_type=F32)


def _split3(x):
    h1 = x.astype(BF16)
    r1 = x - h1.astype(F32)
    h2 = r1.astype(BF16)
    h3 = (r1 - h2.astype(F32)).astype(BF16)
    return h1, h2, h3


def _mm_sel_rhs(x, sel):
    h1, h2, h3 = _split3(x)
    d = lambda h: jnp.dot(h, sel, preferred_element_type=F32)
    return d(h1) + d(h2) + d(h3)


def _mm_sel_lhs(sel, x):
    h1, h2, h3 = _split3(x)
    d = lambda h: jnp.dot(sel, h, preferred_element_type=F32)
    return d(h1) + d(h2) + d(h3)


def _mm3(a, b):
    ah = a.astype(BF16)
    al = (a - ah.astype(F32)).astype(BF16)
    bh = b.astype(BF16)
    bl = (b - bh.astype(F32)).astype(BF16)
    d = lambda u, v: jnp.dot(u, v, preferred_element_type=F32)
    return d(ah, bh) + d(ah, bl) + d(al, bh)


def _sigmoid(x):
    return 1.0 / (1.0 + jnp.exp(-x))


def _silu(x):
    return x * _sigmoid(x)


def _softplus(x):
    return jnp.maximum(x, 0.0) + jnp.log1p(jnp.exp(-jnp.abs(x)))


def _rmsnorm(x, g):
    return x * lax.rsqrt(jnp.mean(x * x, axis=-1, keepdims=True) + EPS) * g


def _t5_bucket_np(dist):
    max_exact = N_BUCKETS // 2
    d = np.maximum(dist, 0)
    ratio = (np.log(np.maximum(d, 1).astype(np.float32) / np.float32(max_exact))
             / np.float32(math.log(WINDOW / max_exact))).astype(np.float32)
    large = np.minimum(max_exact + (ratio * np.float32(N_BUCKETS - max_exact)).astype(np.int32),
                       N_BUCKETS - 1)
    return np.where(d < max_exact, d, large).astype(np.int32)


def _bias_lookup(bucket, rb_ref, h):
    acc = jnp.zeros(bucket.shape, F32)
    for t in range(N_BUCKETS):
        acc = jnp.where(bucket == t, rb_ref[t, h], acc)
    return acc


def _inproj_kernel(x_ref, g_ref, w_ref, att_ref, xc_ref, dz_ref, ba_ref):
    xn = _rmsnorm(x_ref[...], g_ref[...]).astype(BF16)
    o0, o1, o2 = ATT_COLS, ATT_COLS + CONV_CH, ATT_COLS + CONV_CH + DN_WIDTH
    att_ref[...] = jnp.dot(xn, w_ref[:, :o0], preferred_element_type=F32)
    xc_ref[...] = jnp.dot(xn, w_ref[:, o0:o1], preferred_element_type=F32)
    dz_ref[...] = jnp.dot(xn, w_ref[:, o1:o2], preferred_element_type=F32)
    ba_ref[...] = jnp.dot(xn, w_ref[:, o2:], preferred_element_type=F32)


def _inproj(x, g, w):
    t = x.shape[0]
    tm = min(t, ROW_TM)
    row = lambda n: pl.BlockSpec((tm, n), lambda i: (i, 0))
    full = lambda a: pl.BlockSpec(a.shape, lambda i: (0,) * a.ndim)
    return pl.pallas_call(
        _inproj_kernel,
        grid=(t // tm,),
        in_specs=[row(D_MODEL), full(g), full(w)],
        out_specs=[row(ATT_COLS), row(CONV_CH), row(DN_WIDTH), row(LANES)],
        out_shape=[jax.ShapeDtypeStruct((t, n), F32) for n in (ATT_COLS, CONV_CH, DN_WIDTH, LANES)],
        compiler_params=_params("parallel"),
        name="inproj",
    )(x, g, w)


GROUP_ROWS = GQA * ATT_BLOCK


def _attn_prompt_kernel(cur_ref, prev_ref, bucket_ref, rb_ref, sink_ref, o_ref, bias_scr, sink_scr):
    i = pl.program_id(0)
    nseq = cur_ref.shape[0]

    @pl.when(i == 0)
    def _():
        qi = lax.broadcasted_iota(jnp.int32, (ATT_BLOCK, 2 * ATT_BLOCK), 0)
        kj = lax.broadcasted_iota(jnp.int32, (ATT_BLOCK, 2 * ATT_BLOCK), 1)
        dist = qi + ATT_BLOCK - kj
        band = jnp.logical_and(dist >= 0, dist < WINDOW)
        bucket = bucket_ref[...]
        hrow = lax.broadcasted_iota(jnp.int32, (GROUP_ROWS, 1), 0) // ATT_BLOCK
        for g in range(ATT_KV_HEADS):
            sink_col = jnp.zeros((GROUP_ROWS, 1), F32)
            for hh in range(GQA):
                h = g * GQA + hh
                bias = jnp.where(band, _bias_lookup(bucket, rb_ref, h), NEG_INF)
                bias_scr[0, g, hh * ATT_BLOCK:(hh + 1) * ATT_BLOCK, :] = bias
                bias_scr[1, g, hh * ATT_BLOCK:(hh + 1) * ATT_BLOCK, :] = jnp.where(kj >= ATT_BLOCK, bias, NEG_INF)
                sink_col = jnp.where(hrow == hh, sink_ref[h], sink_col)
            sink_scr[g] = sink_col

    first = (i == 0).astype(jnp.int32)
    probs = [(b, g) for b in range(nseq) for g in range(ATT_KV_HEADS)]
    scores = []
    for b, g in probs:
        cur = cur_ref[b]
        prev = prev_ref[b]
        q = jnp.concatenate([cur[:, (g * GQA + hh) * HEAD_DIM:(g * GQA + hh + 1) * HEAD_DIM]
                             for hh in range(GQA)], axis=0) * (HEAD_DIM ** -0.5)
        kcol = slice(ATT_WIDTH + g * HEAD_DIM, ATT_WIDTH + (g + 1) * HEAD_DIM)
        k2 = jnp.concatenate([prev[:, kcol], cur[:, kcol]], axis=0)
        scores.append(_mm_nt(q, k2) + bias_scr[first, g])
    probs_p, dens = [], []
    for (b, g), s in zip(probs, scores):
        sink = sink_scr[g]
        m = jnp.maximum(jnp.max(s, axis=-1, keepdims=True), sink)
        p = jnp.exp(s - m)
        dens.append(jnp.sum(p, axis=-1, keepdims=True) + jnp.exp(sink - m))
        probs_p.append(p)
    outs = {}
    for (b, g), p, den in zip(probs, probs_p, dens):
        vcol = slice(ATT_WIDTH + KV_WIDTH + g * HEAD_DIM, ATT_WIDTH + KV_WIDTH + (g + 1) * HEAD_DIM)
        v2 = jnp.concatenate([prev_ref[b][:, vcol], cur_ref[b][:, vcol]], axis=0)
        outs[b, g] = _mm(p, v2) / den
    for b in range(nseq):
        o_ref[b] = jnp.concatenate([outs[b, g][hh * ATT_BLOCK:(hh + 1) * ATT_BLOCK, :]
                                    for g in range(ATT_KV_HEADS) for hh in range(GQA)], axis=1)


def _attn_prompt(att, bucket, rel_bias, sink, batch, seq):
    nb = seq // ATT_BLOCK
    smem = pl.BlockSpec(memory_space=pltpu.SMEM)
    att3 = att.reshape(batch, seq, ATT_COLS)
    out = pl.pallas_call(
        _attn_prompt_kernel,
        grid=(nb,),
        in_specs=[
            pl.BlockSpec((batch, ATT_BLOCK, ATT_COLS), lambda i: (0, i, 0)),
            pl.BlockSpec((batch, ATT_BLOCK, ATT_COLS), lambda i: (0, jnp.maximum(i - 1, 0), 0)),
            pl.BlockSpec(bucket.shape, lambda i: (0, 0)),
            smem, smem,
        ],
        out_specs=pl.BlockSpec((batch, ATT_BLOCK, ATT_WIDTH), lambda i: (0, i, 0)),
        out_shape=jax.ShapeDtypeStruct((batch, seq, ATT_WIDTH), F32),
        scratch_shapes=[pltpu.VMEM((2, ATT_KV_HEADS, GROUP_ROWS, 2 * ATT_BLOCK), F32),
                        pltpu.VMEM((ATT_KV_HEADS, GROUP_ROWS, 1), F32)],
        compiler_params=_params("arbitrary"),
        name="attn_prompt",
    )(att3, att3, bucket, rel_bias, sink)
    return out.reshape(batch * seq, ATT_WIDTH)


ATT_S_BB = 8


def _attn_sample_kernel(att_ref, ck_ref, cv_ref, bucket_ref, rb_ref, sink_ref, o_ref,
                        bias_scr, col_scr):
    hrow = lax.broadcasted_iota(jnp.int32, (ATT_HEADS, LANES), 0)
    lane = lax.broadcasted_iota(jnp.int32, (ATT_HEADS, LANES), 1)

    @pl.when(pl.program_id(0) == 0)
    def _():
        bucket = jnp.broadcast_to(bucket_ref[...], (ATT_HEADS, LANES))
        bias = jnp.zeros((ATT_HEADS, LANES), F32)
        cols = jnp.zeros((ATT_HEADS, LANES), F32)
        for h in range(ATT_HEADS):
            bias = jnp.where(hrow == h, _bias_lookup(bucket, rb_ref, h), bias)
            cols = jnp.where(jnp.logical_and(hrow == h, lane == 0), sink_ref[h], cols)
            cols = jnp.where(jnp.logical_and(hrow == h, lane == 1), rb_ref[0, h], cols)
        bias_scr[...] = jnp.where(lane >= 1, bias, NEG_INF)
        col_scr[...] = cols

    bias_c = bias_scr[...]
    sink = col_scr[:, 0:1]
    bias_n = col_scr[:, 1:2]
    same_group = (hrow // GQA) == (lane // HEAD_DIM)
    low_group = lax.broadcasted_iota(jnp.int32, (ATT_HEADS, HEAD_DIM), 0) < GQA
    rnd = lambda a: a.astype(BF16).astype(F32)
    seqs = range(ATT_S_BB)
    rows = [att_ref[b:b + 1, :] for b in seqs]
    q_bds = []
    for row in rows:
        q = row[:, :ATT_WIDTH] * (HEAD_DIM ** -0.5)
        qh = jnp.concatenate([q[:, h * HEAD_DIM:(h + 1) * HEAD_DIM] for h in range(ATT_HEADS)], axis=0)
        q_bds.append(jnp.where(same_group, jnp.concatenate([qh, qh], axis=1), 0.0))
    s_cs = [_mm_nt(q_bd, ck_ref[b]) + bias_c for b, q_bd in zip(seqs, q_bds)]
    prs, pns = [], []
    for row, q_bd, s_c in zip(rows, q_bds, s_cs):
        kn = row[:, ATT_WIDTH:ATT_WIDTH + KV_WIDTH]
        s_n = jnp.sum(rnd(q_bd) * rnd(kn), axis=-1, keepdims=True) + bias_n
        m = jnp.maximum(jnp.maximum(jnp.max(s_c, axis=-1, keepdims=True), s_n), sink)
        p_c = jnp.exp(s_c - m)
        p_n = jnp.exp(s_n - m)
        den = jnp.sum(p_c, axis=-1, keepdims=True) + p_n + jnp.exp(sink - m)
        prs.append(p_c / den)
        pns.append(p_n / den)
    pvs = [_mm(pr, cv_ref[b]) for b, pr in zip(seqs, prs)]
    for b, row, pv, pn in zip(seqs, rows, pvs, pns):
        vn = row[:, ATT_WIDTH + KV_WIDTH:]
        o_full = pv + rnd(pn) * rnd(vn)
        o_sel = jnp.where(low_group, o_full[:, :HEAD_DIM], o_full[:, HEAD_DIM:])
        o_ref[b:b + 1, :] = jnp.concatenate([o_sel[h:h + 1, :] for h in range(ATT_HEADS)], axis=1)


def _attn_sample(att, ck, cv, bucket, rel_bias, sink):
    nseq = att.shape[0]
    smem = pl.BlockSpec(memory_space=pltpu.SMEM)
    cache = pl.BlockSpec((ATT_S_BB, WINDOW, KV_WIDTH), lambda i: (i, 0, 0))
    return pl.pallas_call(
        _attn_sample_kernel,
        grid=(nseq // ATT_S_BB,),
        in_specs=[pl.BlockSpec((ATT_S_BB, ATT_COLS), lambda i: (i, 0)), cache, cache,
                  pl.BlockSpec(bucket.shape, lambda i: (0, 0)), smem, smem],
        out_specs=pl.BlockSpec((ATT_S_BB, ATT_WIDTH), lambda i: (i, 0)),
        out_shape=jax.ShapeDtypeStruct((nseq, ATT_WIDTH), F32),
        scratch_shapes=[pltpu.VMEM((ATT_HEADS, LANES), F32), pltpu.VMEM((ATT_HEADS, LANES), F32)],
        compiler_params=_params("arbitrary"),
        name="attn_sample",
    )(att, ck, cv, bucket, rel_bias, sink)


GDN_TB = 128
GDN_NC = GDN_TB // DN_CHUNK
TAIL = 8


def _gdn_gates(ba, alog, dtb):
    beta = _sigmoid(ba)
    g = -jnp.exp(alog) * _softplus(ba + dtb)
    return beta, g


PAIR = 2 * DN_DK
N_PAIRS = DN_WIDTH // PAIR


def _pair_diag(x, lo):
    xb = x.astype(BF16)
    zero = jnp.zeros_like(xb)
    return jnp.concatenate([jnp.where(lo, xb, zero), jnp.where(lo, zero, xb)], axis=0)


def _gdn_prompt_kernel(xc_ref, dz_ref, ba_ref, cw_ref, alog_ref, dtb_ref, dnx_ref,
                       hsum_ref, expb_ref, expg_ref, ltri_ref,
                       o_ref, s_out_ref, xp_scr, s_scr):
    i = pl.program_id(0)
    nb = xc_ref.shape[0]

    @pl.when(i == 0)
    def _():
        xp_scr[:, 0:TAIL, :] = jnp.zeros((nb, TAIL, CONV_CH), F32)
        s_scr[...] = jnp.zeros(s_scr.shape, F32)

    hsum = hsum_ref[...]
    ri = lax.broadcasted_iota(jnp.int32, (DN_CHUNK, PAIR), 0)
    ci = lax.broadcasted_iota(jnp.int32, (DN_CHUNK, PAIR), 1)
    lo = ci < DN_DK
    cj = jnp.where(lo, ci, ci - DN_DK)
    causal = ri >= cj
    strict = ri > cj
    eye = (ri == cj).astype(F32)

    def sel2(x, m):
        hi = x.astype(BF16)
        lw = (x - hi.astype(F32)).astype(BF16)
        return (jnp.dot(hi, m, preferred_element_type=F32) + jnp.dot(lw, m, preferred_element_type=F32))

    def head_sums(z):
        hi = z.astype(BF16)
        lw = (z - hi.astype(F32)).astype(BF16)
        d = lambda a, p: jnp.dot(a[:, p * PAIR:(p + 1) * PAIR], hsum, preferred_element_type=F32)
        return jnp.concatenate([d(hi, p) + d(lw, p) for p in range(N_PAIRS)], axis=1)

    ys = []
    for b in range(nb):
        xc = xc_ref[b]
        xp_scr[b, TAIL:, :] = xc
        y = xp_scr[b, TAIL - 3:TAIL - 3 + GDN_TB, :] * cw_ref[0:1, :]
        y = y + xp_scr[b, TAIL - 2:TAIL - 2 + GDN_TB, :] * cw_ref[1:2, :]
        y = y + xp_scr[b, TAIL - 1:TAIL - 1 + GDN_TB, :] * cw_ref[2:3, :]
        y = y + xc * cw_ref[3:4, :]
        xp_scr[b, 0:TAIL, :] = xc[GDN_TB - TAIL:, :]
        ys.append(_silu(y))
    qk_raw = [y[:, j * DN_WIDTH:(j + 1) * DN_WIDTH] for y in ys for j in range(2)]
    inv_norm = lax.rsqrt(head_sums(jnp.concatenate([a * a for a in qk_raw], axis=0)) + EPS)
    pre = []
    for b in range(nb):
        q = qk_raw[2 * b] * inv_norm[2 * b * GDN_TB:(2 * b + 1) * GDN_TB] * (DN_DK ** -0.5)
        k = qk_raw[2 * b + 1] * inv_norm[(2 * b + 1) * GDN_TB:(2 * b + 2) * GDN_TB]
        v = ys[b][:, 2 * DN_WIDTH:]
        beta_c, g_c = _gdn_gates(ba_ref[b], alog_ref[...], dtb_ref[...])
        beta = sel2(beta_c, expb_ref[...])
        gam_c = _mm_sel_lhs(ltri_ref[...], g_c)
        gam = _mm_sel_rhs(gam_c, expg_ref[...])
        gam_t = gam_c.T
        kb = k * beta
        egam = jnp.exp(gam)
        pre.append(dict(q=q, k=k, kb=kb, vb=v * beta, qg=q * egam, wr=kb * egam, gam=gam, gam_t=gam_t))

    probs = [(b, p) for b in range(nb) for p in range(N_PAIRS)]
    pick = lambda m: jnp.where(lo, m[:DN_DK], m[DN_DK:])
    o_rows = [[] for _ in range(nb)]
    for c in range(GDN_NC):
        r0, r1 = c * DN_CHUNK, (c + 1) * DN_CHUNK
        sl = lambda name, b, p: pre[b][name][r0:r1, p * PAIR:(p + 1) * PAIR]
        raws = []
        for b, p in probs:
            k_p = sl("k", b, p)
            k_rows = jnp.concatenate([jnp.where(lo, k_p, 0.0), jnp.where(lo, 0.0, k_p)], axis=0)
            raws.append(_mm_nt(jnp.concatenate([sl("kb", b, p), sl("q", b, p)], axis=0), k_rows))
        pws, ts, qks = [], [], []
        for (b, p), raw in zip(probs, raws):
            gcol = sl("gam", b, p)
            h0 = DN_HEADS + 2 * p
            gam_t = pre[b]["gam_t"]
            grow = jnp.concatenate([gam_t[h0:h0 + 1, r0:r1], gam_t[h0 + 1:h0 + 2, r0:r1]], axis=1)
            decay = jnp.exp(jnp.where(causal, gcol - grow, NEG_INF))
            a = jnp.where(strict, raw[:DN_CHUNK] * decay, 0.0)
            qks.append(jnp.where(causal, raw[DN_CHUNK:] * decay, 0.0))
            pws.append(-a)
            ts.append(eye - a)
        pws = [_mm(pw, _pair_diag(pw, lo)) for pw in pws]
        for _ in range(4):
            rs = [_mm(jnp.concatenate([pw, t], axis=0), _pair_diag(pw, lo)) for pw, t in zip(pws, ts)]
            pws = [r[:DN_CHUNK] for r in rs]
            ts = [t + r[DN_CHUNK:] for t, r in zip(ts, rs)]
        rs = [_mm(t, _pair_diag(pw, lo)) for pw, t in zip(pws, ts)]
        ts = [t + r for t, r in zip(ts, rs)]
        sols = [_mm(t, jnp.concatenate([_pair_diag(sl("vb", b, p), lo), _pair_diag(sl("wr", b, p), lo)],
                                       axis=1)) for (b, p), t in zip(probs, ts)]
        qkuws = [_mm(qk, jnp.concatenate([_pair_diag(s[:, :PAIR], lo), _pair_diag(s[:, PAIR:], lo)], axis=1))
                 for qk, s in zip(qks, sols)]
        crosses, gls = [], []
        for (b, p), s in zip(probs, sols):
            gam_last = pre[b]["gam"][r1 - 1:r1, p * PAIR:(p + 1) * PAIR]
            kd = sl("k", b, p) * jnp.exp(gam_last - sl("gam", b, p))
            crosses.append(_mm_tn(kd, s))
            gls.append(jnp.exp(gam_last))
        lhs = [jnp.concatenate([pick(cr[:, PAIR:]), sl("qg", b, p) - qkuw[:, PAIR:]], axis=0)
               for (b, p), cr, qkuw in zip(probs, crosses, qkuws)]
        s_olds = [s_scr[b, p] for b, p in probs]
        rs = [_mm(l, _pair_diag(s_old, lo)) for l, s_old in zip(lhs, s_olds)]
        o_pairs = [[] for _ in range(nb)]
        for (b, p), r, s_old, gl, cr, qkuw in zip(probs, rs, s_olds, gls, crosses, qkuws):
            s_scr[b, p] = gl * s_old - r[:DN_DK] + pick(cr[:, :PAIR])
            o_pairs[b].append(r[DN_DK:] + qkuw[:, :PAIR])
        for b in range(nb):
            o_rows[b].append(jnp.concatenate(o_pairs[b], axis=1))

    o_all = jnp.concatenate([jnp.concatenate(rows, axis=0) for rows in o_rows], axis=0)
    inv_rms = lax.rsqrt(head_sums(o_all * o_all) * (1.0 / DN_DV) + EPS)
    for b in range(nb):
        rows = slice(b * GDN_TB, (b + 1) * GDN_TB)
        o_ref[b] = o_all[rows] * inv_rms[rows] * dnx_ref[...] * _silu(dz_ref[b])

    @pl.when(i == pl.num_programs(0) - 1)
    def _():
        for b in range(nb):
            for p in range(N_PAIRS):
                s_p = s_scr[b, p]
                s_out_ref[b, 2 * p] = s_p[:, :DN_DV]
                s_out_ref[b, 2 * p + 1] = s_p[:, DN_DV:]


def _gdn_consts():
    lane = np.arange(DN_WIDTH)
    pl_lane = np.arange(PAIR)
    hsum = (pl_lane[:, None] // DN_DV == pl_lane[None, :] // DN_DV)
    src = np.arange(LANES)
    expb = (src[:, None] == lane[None, :] // DN_DV)
    expg = (src[:, None] == DN_HEADS + lane[None, :] // DN_DV)
    tok = np.arange(GDN_TB)
    ltri = np.logical_and(tok[:, None] >= tok[None, :],
                          tok[:, None] // DN_CHUNK == tok[None, :] // DN_CHUNK)
    as_bf16 = lambda m: jnp.asarray(m.astype(np.float32), dtype=BF16)
    return as_bf16(hsum), as_bf16(expb), as_bf16(expg), as_bf16(ltri)


def _gdn_prompt(xc, dz, ba, conv_w, alog, dtb, dnx, batch, seq):
    nt = seq // GDN_TB
    hsum, expb, expg, ltri = _gdn_consts()
    row = lambda n: pl.BlockSpec((batch, GDN_TB, n), lambda i: (0, i, 0))
    full = lambda a: pl.BlockSpec(a.shape, lambda i: (0,) * a.ndim)
    consts = (conv_w, alog, dtb, dnx, hsum, expb, expg, ltri)
    as3d = lambda a: a.reshape(batch, seq, a.shape[-1])
    o, s = pl.pallas_call(
        _gdn_prompt_kernel,
        grid=(nt,),
        in_specs=[row(CONV_CH), row(DN_WIDTH), row(LANES)] + [full(a) for a in consts],
        out_specs=[row(DN_WIDTH),
                   pl.BlockSpec((batch, DN_HEADS, DN_DK, DN_DV), lambda i: (0, 0, 0, 0))],
        out_shape=[jax.ShapeDtypeStruct((batch, seq, DN_WIDTH), F32),
                   jax.ShapeDtypeStruct((batch, DN_HEADS, DN_DK, DN_DV), F32)],
        scratch_shapes=[pltpu.VMEM((batch, TAIL + GDN_TB, CONV_CH), F32),
                        pltpu.VMEM((batch, N_PAIRS, DN_DK, PAIR), F32)],
        compiler_params=_params("arbitrary"),
        name="gdn_prompt",
    )(as3d(xc), as3d(dz), as3d(ba), *consts)
    return o.reshape(batch * seq, DN_WIDTH), s


GDN_S_BB = 8


def _gdn_sample_kernel(xc_ref, dz_ref, ba_ref, sc_ref, s_ref, cw_ref, alog_ref, dtb_ref, dn_ref,
                       hsum_ref, eye_ref, hsel_ref, hrep_ref, o_ref, s_out_ref):
    xc = xc_ref[...]
    y = sc_ref[0] * cw_ref[0:1, :]
    y = y + sc_ref[1] * cw_ref[1:2, :]
    y = y + sc_ref[2] * cw_ref[2:3, :]
    y = _silu(y + xc * cw_ref[3:4, :])
    hsum = hsum_ref[...]
    q = y[:, :DN_WIDTH]
    k = y[:, DN_WIDTH:2 * DN_WIDTH]
    v = y[:, 2 * DN_WIDTH:]
    q = q * lax.rsqrt(_mm_sel_rhs(q * q, hsum) + EPS) * (DN_DK ** -0.5)
    k = k * lax.rsqrt(_mm_sel_rhs(k * k, hsum) + EPS)
    beta_c, g_c = _gdn_gates(ba_ref[...], alog_ref[...], dtb_ref[...])
    eg_c = jnp.exp(g_c)
    eye = eye_ref[...]
    tr = lambda a: lax.dot_general(a, eye, (((0,), (0,)), ((), ())), precision=lax.Precision.HIGHEST,
                                   preferred_element_type=F32)
    k_t = tr(k)
    q_t = tr(q)
    beta_t = tr(beta_c)
    eg_t = tr(eg_c)
    dz = dz_ref[...]
    dn = dn_ref[...]
    split = lambda r: jnp.concatenate([r[:, h * DN_DV:(h + 1) * DN_DV] for h in range(DN_HEADS)], axis=0)
    hsel = hsel_ref[...]
    hrep = hrep_ref[...]
    seqs = range(GDN_S_BB)
    kqs = [_mm_sel_lhs(hsel, jnp.concatenate([k_t[:, b:b + 1] * s_ref[b], q_t[:, b:b + 1] * s_ref[b]], axis=1))
           for b in seqs]
    rep_ins = []
    for b, kq in zip(seqs, kqs):
        beta = beta_t[0:DN_HEADS, b:b + 1]
        eg = eg_t[DN_HEADS:2 * DN_HEADS, b:b + 1]
        vh = split(v[b:b + 1, :])
        qh = split(q[b:b + 1, :])
        kh = split(k[b:b + 1, :])
        v_new = beta * (vh - eg * kq[:, :DN_DV])
        qk = jnp.sum(qh * kh, axis=-1, keepdims=True)
        o = eg * kq[:, DN_DV:] + qk * v_new
        o_ref[b] = _rmsnorm(o, dn) * _silu(split(dz[b:b + 1, :]))
        rep_ins.append(jnp.concatenate([v_new, jnp.broadcast_to(eg, (DN_HEADS, DN_DV))], axis=1))
    reps = [_mm_sel_lhs(hrep, r) for r in rep_ins]
    for b, rep in zip(seqs, reps):
        s_out_ref[b] = s_ref[b] * rep[:, DN_DV:] + k_t[:, b:b + 1] * rep[:, :DN_DV]


def _gdn_sample(xc, dz, ba, sconv_t, state, conv_w, alog, dtb, dn):
    nseq = xc.shape[0]
    lane = np.arange(DN_WIDTH)
    hsum = jnp.asarray((lane[:, None] // DN_DV == lane[None, :] // DN_DV).astype(np.float32), dtype=BF16)
    eye = jnp.eye(GDN_S_BB, dtype=F32)
    hsel_np = (np.arange(DN_HEADS)[:, None] == lane[None, :] // DN_DK).astype(np.float32)
    hsel = jnp.asarray(hsel_np, dtype=BF16)
    hrep = jnp.asarray(hsel_np.T, dtype=BF16)
    row = lambda n: pl.BlockSpec((GDN_S_BB, n), lambda i: (i, 0))
    full = lambda a: pl.BlockSpec(a.shape, lambda i: (0,) * a.ndim)
    st = pl.BlockSpec((GDN_S_BB, DN_HEADS * DN_DK, DN_DV), lambda i: (i, 0, 0))
    consts = (conv_w, alog, dtb, dn, hsum, eye, hsel, hrep)
    return pl.pallas_call(
        _gdn_sample_kernel,
        grid=(nseq // GDN_S_BB,),
        in_specs=[row(CONV_CH), row(DN_WIDTH), row(LANES),
                  pl.BlockSpec((CONV_WIDTH - 1, GDN_S_BB, CONV_CH), lambda i: (0, i, 0)), st]
                 + [full(a) for a in consts],
        out_specs=[pl.BlockSpec((GDN_S_BB, DN_HEADS, DN_DV), lambda i: (i, 0, 0)), st],
        out_shape=[jax.ShapeDtypeStruct((nseq, DN_HEADS, DN_DV), F32),
                   jax.ShapeDtypeStruct(state.shape, F32)],
        compiler_params=_params("parallel"),
        name="gdn_sample",
    )(xc, dz, ba, sconv_t, state, *consts)


def _route(xn, wr):
    logits = jnp.dot(xn, wr, preferred_element_type=F32)
    lane = lax.broadcasted_iota(jnp.int32, logits.shape, 1).astype(F32)
    first_at = lambda hit: jnp.min(jnp.where(hit, lane, float(LANES)), axis=-1, keepdims=True)
    glog = jnp.where(lane < N_GROUPS, logits, NEG_INF)
    gmax = jnp.max(glog, axis=-1, keepdims=True)
    gsel = first_at(glog == gmax)
    pgsel = 1.0 / jnp.sum(jnp.exp(glog - gmax), axis=-1, keepdims=True)
    lo = ROUTER_OFF + gsel * EXPERTS_PER_GROUP
    in_group = jnp.logical_and(lane >= lo, lane < lo + EXPERTS_PER_GROUP)
    elog = jnp.where(in_group, logits, NEG_INF)
    m1 = jnp.max(elog, axis=-1, keepdims=True)
    i1 = first_at(elog == m1)
    z = jnp.sum(jnp.exp(elog - m1), axis=-1, keepdims=True)
    elog2 = jnp.where(lane == i1, NEG_INF, elog)
    m2 = jnp.max(elog2, axis=-1, keepdims=True)
    i2 = first_at(elog2 == m2)
    p1 = 1.0 / z
    p2 = jnp.exp(m2 - m1) / z
    tot = p1 + p2
    return lane, i1, i2, p1 / tot * pgsel, p2 / tot * pgsel


def _outproj(x_ref, oa_ref, od_ref, wo_ref):
    return x_ref[...] + _mm(oa_ref[...], wo_ref[:ATT_WIDTH, :]) + _mm(od_ref[...], wo_ref[ATT_WIDTH:, :])


def _outproj_router_kernel(x_ref, oa_ref, od_ref, wo_ref, g_ref, wr_ref, h_ref, xn_ref, gate_ref):
    h = _outproj(x_ref, oa_ref, od_ref, wo_ref)
    h_ref[...] = h
    xn = _rmsnorm(h, g_ref[...]).astype(BF16)
    xn_ref[...] = xn
    lane, i1, i2, g1, g2 = _route(xn, wr_ref[...])
    gate_ref[...] = jnp.where(lane == i1, g1, 0.0) + jnp.where(lane == i2, g2, 0.0)


def _outproj_router(x, oa, od, wo, g, wr):
    t = x.shape[0]
    tm = min(t, 256)
    row = lambda n: pl.BlockSpec((tm, n), lambda i: (i, 0))
    full = lambda a: pl.BlockSpec(a.shape, lambda i: (0,) * a.ndim)
    return pl.pallas_call(
        _outproj_router_kernel,
        grid=(t // tm,),
        in_specs=[row(D_MODEL), row(ATT_WIDTH), row(DN_WIDTH), full(wo), full(g), full(wr)],
        out_specs=[row(D_MODEL), row(D_MODEL), row(LANES)],
        out_shape=[jax.ShapeDtypeStruct((t, D_MODEL), F32), jax.ShapeDtypeStruct((t, D_MODEL), BF16),
                   jax.ShapeDtypeStruct((t, LANES), F32)],
        compiler_params=_params("parallel"),
        name="outproj_router",
    )(x, oa, od, wo, g, wr)


def _moe_kernel(xn_ref, gate_ref, wg_ref, wu_ref, wd_ref, o_ref):
    e = pl.program_id(1)
    xn = xn_ref[...]
    lane = lax.broadcasted_iota(jnp.int32, gate_ref.shape, 1)
    gate = jnp.sum(jnp.where(lane == e + ROUTER_OFF, gate_ref[...], 0.0), axis=-1, keepdims=True)
    hg = jnp.dot(xn, wg_ref[...].astype(BF16), preferred_element_type=F32)
    hu = jnp.dot(xn, wu_ref[...].astype(BF16), preferred_element_type=F32)
    hm = _silu(hg) * hu * gate
    y = jnp.dot(hm.astype(BF16), wd_ref[...].astype(BF16), preferred_element_type=F32)

    @pl.when(e == 0)
    def _():
        o_ref[...] = y

    @pl.when(e > 0)
    def _():
        o_ref[...] += y


def _moe(xn, gates, wg, wu, wd):
    t = xn.shape[0]
    tm = min(t, 1024)
    return pl.pallas_call(
        _moe_kernel,
        grid=(t // tm, N_EXPERTS),
        in_specs=[pl.BlockSpec((tm, D_MODEL), lambda i, e: (i, 0)),
                  pl.BlockSpec((tm, LANES), lambda i, e: (i, 0)),
                  pl.BlockSpec((None, D_MODEL, D_EXPERT), lambda i, e: (e, 0, 0)),
                  pl.BlockSpec((None, D_MODEL, D_EXPERT), lambda i, e: (e, 0, 0)),
                  pl.BlockSpec((None, D_EXPERT, D_MODEL), lambda i, e: (e, 0, 0))],
        out_specs=pl.BlockSpec((tm, D_MODEL), lambda i, e: (i, 0)),
        out_shape=jax.ShapeDtypeStruct((t, D_MODEL), F32),
        compiler_params=_params("parallel", "arbitrary"),
        name="moe",
    )(xn, gates, wg, wu, wd)


MOE_TM = 256
POS_TM = 1024
INFO_G1, INFO_G2, INFO_E1, INFO_E2 = 0, 1, 2, 3
DMA_UNROLL = 8


def _moe_tiles(t):
    return (2 * t) // MOE_TM + N_EXPERTS


HALF = D_MODEL // 2
U32 = jnp.uint32


def _pack_rows(x):
    bits = lambda v: lax.bitcast_convert_type(v.astype(BF16).astype(F32), U32)
    return bits(x[:, HALF:]) | (bits(x[:, :HALF]) >> 16)


def _unpack_rows(w):
    lo = lax.bitcast_convert_type(w << 16, F32)
    hi = lax.bitcast_convert_type(w & jnp.uint32(0xFFFF0000), F32)
    return lo, hi


def _route_kernel(x_ref, oa_ref, od_ref, wo_ref, g_ref, wr_ref, h_ref, xn_ref, info_ref, cnt_ref, run_scr):
    h = _outproj(x_ref, oa_ref, od_ref, wo_ref)
    h_ref[...] = h
    xn = _rmsnorm(h, g_ref[...])
    xn_ref[...] = _pack_rows(xn)
    lane, i1, i2, g1, g2 = _route(xn.astype(BF16), wr_ref[...])
    info = jnp.where(lane == INFO_G1, g1, 0.0) + jnp.where(lane == INFO_G2, g2, 0.0)
    info = info + jnp.where(lane == INFO_E1, i1, 0.0) + jnp.where(lane == INFO_E2, i2, 0.0)
    info_ref[...] = info

    @pl.when(pl.program_id(0) == 0)
    def _():
        run_scr[...] = jnp.zeros(run_scr.shape, F32)
    picked = jnp.logical_or(lane == i1, lane == i2).astype(F32)
    run_scr[...] += jnp.sum(picked, axis=0, keepdims=True)
    cnt_ref[...] = run_scr[...]


def _route_sparse(x, oa, od, wo, g, wr):
    t = x.shape[0]
    tm = ROW_TM
    row = lambda n: pl.BlockSpec((tm, n), lambda i: (i, 0))
    full = lambda a: pl.BlockSpec(a.shape, lambda i: (0,) * a.ndim)
    return pl.pallas_call(
        _route_kernel,
        grid=(t // tm,),
        in_specs=[row(D_MODEL), row(ATT_WIDTH), row(DN_WIDTH), full(wo), full(g), full(wr)],
        out_specs=[row(D_MODEL), row(HALF), row(LANES), pl.BlockSpec((1, LANES), lambda i: (0, 0))],
        out_shape=[jax.ShapeDtypeStruct((t, D_MODEL), F32), jax.ShapeDtypeStruct((t, HALF), U32),
                   jax.ShapeDtypeStruct((t, LANES), F32), jax.ShapeDtypeStruct((1, LANES), F32)],
        scratch_shapes=[pltpu.VMEM((1, LANES), F32)],
        compiler_params=_params("arbitrary"),
        name="route",
    )(x, oa, od, wo, g, wr)


def _positions_kernel(info_ref, cnt_ref, ltri_ref, utri_ref, pos_ref, run_scr, off_scr):
    info = info_ref[...]
    lane = lax.broadcasted_iota(jnp.int32, info.shape, 1).astype(F32)
    hit1 = lane == info[:, INFO_E1:INFO_E1 + 1]
    hit2 = lane == info[:, INFO_E2:INFO_E2 + 1]
    onehot = jnp.logical_or(hit1, hit2).astype(F32)

    @pl.when(pl.program_id(0) == 0)
    def _():
        tiles = jnp.floor((cnt_ref[...] + (MOE_TM - 1)) * (1.0 / MOE_TM))
        off_scr[...] = MOE_TM * jnp.dot(tiles.astype(BF16), utri_ref[...], preferred_element_type=F32)
        run_scr[...] = jnp.zeros(run_scr.shape, F32)

    before = (jnp.dot(ltri_ref[...], onehot.astype(BF16), preferred_element_type=F32)
              + run_scr[...] + off_scr[...])
    pos1 = jnp.sum(jnp.where(hit1, before, 0.0), axis=-1, keepdims=True)
    pos2 = jnp.sum(jnp.where(hit2, before, 0.0), axis=-1, keepdims=True)
    pos_ref[...] = (jnp.where(lane == 0, pos1, 0.0) + jnp.where(lane == 1, pos2, 0.0)).astype(jnp.int32)
    run_scr[...] += jnp.sum(onehot, axis=0, keepdims=True)


def _positions(info, cnt):
    t = info.shape[0]
    tm = min(t, POS_TM)
    tok = np.arange(tm)
    ltri = jnp.asarray((tok[:, None] > tok[None, :]).astype(np.float32), dtype=BF16)
    ln = np.arange(LANES)
    utri = jnp.asarray((ln[:, None] < ln[None, :]).astype(np.float32), dtype=BF16)
    full = lambda a: pl.BlockSpec(a.shape, lambda i: (0,) * a.ndim)
    return pl.pallas_call(
        _positions_kernel,
        grid=(t // tm,),
        in_specs=[pl.BlockSpec((tm, LANES), lambda i: (i, 0)), full(cnt), full(ltri), full(utri)],
        out_specs=pl.BlockSpec((tm, LANES), lambda i: (i, 0)),
        out_shape=jax.ShapeDtypeStruct((t, LANES), jnp.int32),
        scratch_shapes=[pltpu.VMEM((1, LANES), F32), pltpu.VMEM((1, LANES), F32)],
        compiler_params=_params("arbitrary"),
        name="positions",
    )(info, cnt, ltri, utri)


def _row_copy(src_hbm, src_row, dst_hbm, dst_row, sem):
    return pltpu.make_async_copy(src_hbm.at[pl.ds(src_row, 1)], dst_hbm.at[pl.ds(dst_row, 1)], sem)


SCATTER_SLOTS = 3


def _scatter_kernel(pos1_ref, pos2_ref, last_ref, used_ref, nt_ref, xn_hbm, zero_hbm, xs_hbm,
                    buf, lsem, sem, zsem, *, n_tok):
    max_tiles = xs_hbm.shape[0] // MOE_TM

    def zero_tile(tile):
        return pltpu.make_async_copy(zero_hbm, xs_hbm.at[pl.ds(tile * MOE_TM, MOE_TM)], zsem)

    def for_unused(fn):
        def body(tile, carry):
            fn(tile)
            return carry
        lax.fori_loop(nt_ref[0], max_tiles, body, 0)

    for e in range(N_EXPERTS):
        @pl.when(used_ref[e] > 0)
        def _():
            zero_tile(last_ref[e]).start()
    for_unused(lambda tile: zero_tile(tile).start())
    for e in range(N_EXPERTS):
        @pl.when(used_ref[e] > 0)
        def _():
            zero_tile(last_ref[e]).wait()
    for_unused(lambda tile: zero_tile(tile).wait())

    tm = buf.shape[1]
    n = n_tok // tm

    def load(i):
        return pltpu.make_async_copy(xn_hbm.at[pl.ds(i * tm, tm)], buf.at[i % SCATTER_SLOTS],
                                     lsem.at[i % SCATTER_SLOTS])

    def wait_rows(slot):
        pltpu.make_async_copy(xs_hbm.at[pl.ds(0, 2 * tm)], xs_hbm.at[pl.ds(0, 2 * tm)], sem.at[slot]).wait()

    load(0).start()
    load(1).start()

    def step(i, carry):
        slot = i % SCATTER_SLOTS
        load(i).wait()

        def body(j, c2):
            tok = i * tm + j
            src = buf.at[slot, pl.ds(j, 1)]
            pltpu.make_async_copy(src, xs_hbm.at[pl.ds(pos1_ref[tok], 1)], sem.at[slot]).start()
            pltpu.make_async_copy(src, xs_hbm.at[pl.ds(pos2_ref[tok], 1)], sem.at[slot]).start()
            return c2
        lax.fori_loop(0, tm, body, 0, unroll=DMA_UNROLL)

        @pl.when(i >= 1)
        def _():
            wait_rows((i - 1) % SCATTER_SLOTS)

        @pl.when(i + 2 < n)
        def _():
            load(i + 2).start()
        return carry
    lax.fori_loop(0, n, step, 0)
    wait_rows((n - 1) % SCATTER_SLOTS)


def _scatter_rows(xn, pos1, pos2, last_tile, used, n_tiles, n_rows):
    t = xn.shape[0]
    zero = jnp.zeros((MOE_TM, D_MODEL), F32)
    any_spec = pl.BlockSpec(memory_space=pl.ANY)
    return pl.pallas_call(
        functools.partial(_scatter_kernel, n_tok=t),
        grid_spec=pltpu.PrefetchScalarGridSpec(
            num_scalar_prefetch=5, grid=(1,),
            in_specs=[any_spec, any_spec], out_specs=any_spec,
            scratch_shapes=[pltpu.VMEM((SCATTER_SLOTS, MOE_TM, D_MODEL), F32),
                            pltpu.SemaphoreType.DMA((SCATTER_SLOTS,)),
                            pltpu.SemaphoreType.DMA((SCATTER_SLOTS,)),
                            pltpu.SemaphoreType.DMA]),
        out_shape=jax.ShapeDtypeStruct((n_rows, D_MODEL), F32),
        compiler_params=_params("arbitrary"),
        name="scatter_rows",
    )(pos1, pos2, last_tile, used, n_tiles, xn, zero)


def _experts_kernel(te_ref, tv_ref, nt_ref, xs_ref, wg_ref, wu_ref, wd_ref, ys_ref, wg_s, wu_s, wd_s):
    i = pl.program_id(0)
    used = i < nt_ref[0]

    @pl.when(jnp.logical_or(i == 0, te_ref[i] != te_ref[jnp.maximum(i - 1, 0)]))
    def _():
        wg_s[...] = wg_ref[...].astype(BF16)
        wu_s[...] = wu_ref[...].astype(BF16)
        wd_s[...] = wd_ref[...].astype(BF16)

    @pl.when(used)
    def _():
        row = lax.broadcasted_iota(jnp.int32, xs_ref.shape, 0)
        x_lo, x_hi = _unpack_rows(jnp.where(row < tv_ref[i], xs_ref[...], jnp.uint32(0)))
        x_lo = x_lo.astype(BF16)
        x_hi = x_hi.astype(BF16)
        up = lambda w_s: (jnp.dot(x_lo, w_s[:HALF, :], preferred_element_type=F32)
                          + jnp.dot(x_hi, w_s[HALF:, :], preferred_element_type=F32))
        hm = (_silu(up(wg_s)) * up(wu_s)).astype(BF16)
        ys_ref[...] = _pack_rows(jnp.dot(hm, wd_s[...], preferred_element_type=F32))

    @pl.when(jnp.logical_not(used))
    def _():
        ys_ref[...] = jnp.zeros(ys_ref.shape, U32)


def _experts(xs, tile_expert, tile_valid, n_tiles, wg, wu, wd):
    max_tiles = xs.shape[0] // MOE_TM
    rows = pl.BlockSpec((MOE_TM, HALF), lambda i, te, tv, nt: (i, 0))
    wspec = lambda shape: pl.BlockSpec((None,) + shape, lambda i, te, tv, nt: (te[i], 0, 0))
    return pl.pallas_call(
        _experts_kernel,
        grid_spec=pltpu.PrefetchScalarGridSpec(
            num_scalar_prefetch=3, grid=(max_tiles,),
            in_specs=[rows, wspec((D_MODEL, D_EXPERT)), wspec((D_MODEL, D_EXPERT)),
                      wspec((D_EXPERT, D_MODEL))],
            out_specs=rows,
            scratch_shapes=[pltpu.VMEM((D_MODEL, D_EXPERT), BF16), pltpu.VMEM((D_MODEL, D_EXPERT), BF16),
                            pltpu.VMEM((D_EXPERT, D_MODEL), BF16)]),
        out_shape=jax.ShapeDtypeStruct(xs.shape, U32),
        compiler_params=_params("arbitrary"),
        name="experts",
    )(tile_expert, tile_valid, n_tiles, xs, wg, wu, wd)


def _ple_gather_kernel(pos1_ref, pos2_ref, h_ref, info_ref, p_ref, wpp_ref, wpg_ref, gp_ref, gf_ref,
                       ys_hbm, y_ref, ybuf, sem):
    i = pl.program_id(0)
    n = pl.num_programs(0)
    tm = h_ref.shape[0]

    def issue(tile, slot):
        def body(j, carry):
            tok = tile * tm + j
            pltpu.make_async_copy(ys_hbm.at[pl.ds(pos1_ref[tok], 1)], ybuf.at[slot, 0, pl.ds(j, 1)],
                                  sem.at[slot]).start()
            pltpu.make_async_copy(ys_hbm.at[pl.ds(pos2_ref[tok], 1)], ybuf.at[slot, 1, pl.ds(j, 1)],
                                  sem.at[slot]).start()
            return carry
        lax.fori_loop(0, tm, body, 0, unroll=DMA_UNROLL)

    @pl.when(i == 0)
    def _():
        issue(0, 0)

    @pl.when(i + 1 < n)
    def _():
        issue(i + 1, (i + 1) % 2)

    slot = i % 2
    pltpu.make_async_copy(ybuf.at[slot], ybuf.at[slot], sem.at[slot]).wait()
    info = info_ref[...]
    moe = info[:, INFO_G1:INFO_G1 + 1] * ybuf[slot, 0] + info[:, INFO_G2:INFO_G2 + 1] * ybuf[slot, 1]
    h = h_ref[...] + moe
    hn = _rmsnorm(h, gp_ref[...])
    h = h + _mm(p_ref[...], wpp_ref[...]) * _sigmoid(_mm(hn, wpg_ref[...]))
    y_ref[...] = _rmsnorm(h, gf_ref[...])


def _ple_gather(h, info, p, ys, pos1, pos2, wpp, wpg, gp, gf):
    t = h.shape[0]
    tm = 256
    row = lambda n: pl.BlockSpec((tm, n), lambda i, p1, p2: (i, 0))
    full = lambda a: pl.BlockSpec(a.shape, lambda i, p1, p2: (0,) * a.ndim)
    return pl.pallas_call(
        _ple_gather_kernel,
        grid_spec=pltpu.PrefetchScalarGridSpec(
            num_scalar_prefetch=2, grid=(t // tm,),
            in_specs=[row(D_MODEL), row(LANES), row(PLE_DIM), full(wpp), full(wpg), full(gp), full(gf),
                      pl.BlockSpec(memory_space=pl.ANY)],
            out_specs=row(D_MODEL),
            scratch_shapes=[pltpu.VMEM((2, 2, tm, D_MODEL), F32), pltpu.SemaphoreType.DMA((2,))]),
        out_shape=jax.ShapeDtypeStruct((t, D_MODEL), F32),
        compiler_params=_params("arbitrary"),
        name="ple_gather",
    )(pos1, pos2, h, info, p, wpp, wpg, gp, gf, ys)


SC_IDX = 128
SC_ROWS = 64
SC_WORKERS = 32


def _sc_mesh():
    return plsc.VectorSubcoreMesh(core_axis_name="c", subcore_axis_name="s")


def _sc_windows(t, fn):
    per_worker = t // SC_WORKERS
    worker = lax.axis_index(("c", "s"))

    @pl.loop(0, per_worker // SC_IDX)
    def _(w):
        fn(worker * per_worker + w * SC_IDX)


def _sc_scatter_rows(xn, pos1, pos2, n_rows):
    t, d = xn.shape
    assert t % (SC_WORKERS * SC_IDX) == 0
    idx_t = pltpu.VMEM((1, SC_IDX), jnp.int32)

    @pl.kernel(out_type=jax.ShapeDtypeStruct((n_rows, d), xn.dtype), mesh=_sc_mesh(),
               scratch_types=[idx_t, idx_t, pltpu.VMEM((SC_ROWS, d), xn.dtype)])
    def scatter(x_hbm, p1_hbm, p2_hbm, o_hbm, i1_v, i2_v, buf):
        def window(base):
            pltpu.sync_copy(p1_hbm.at[:, pl.ds(base, SC_IDX)], i1_v)
            pltpu.sync_copy(p2_hbm.at[:, pl.ds(base, SC_IDX)], i2_v)
            for k in range(SC_IDX // SC_ROWS):
                pltpu.sync_copy(x_hbm.at[pl.ds(base + k * SC_ROWS, SC_ROWS)], buf)
                pltpu.sync_copy(buf, o_hbm.at[i1_v.at[0, pl.ds(k * SC_ROWS, SC_ROWS)]])
                pltpu.sync_copy(buf, o_hbm.at[i2_v.at[0, pl.ds(k * SC_ROWS, SC_ROWS)]])
        _sc_windows(t, window)

    return scatter(xn, pos1.reshape(1, t), pos2.reshape(1, t))


def _sc_gather_rows(ys, pos1, pos2):
    t = pos1.shape[0]
    d = ys.shape[1]
    assert t % (SC_WORKERS * SC_IDX) == 0
    idx_t = pltpu.VMEM((1, SC_IDX), jnp.int32)
    out = jax.ShapeDtypeStruct((t, d), ys.dtype)

    @pl.kernel(out_type=(out, out), mesh=_sc_mesh(),
               scratch_types=[idx_t, idx_t, pltpu.VMEM((SC_ROWS, d), ys.dtype)])
    def gather(y_hbm, p1_hbm, p2_hbm, o1_hbm, o2_hbm, i1_v, i2_v, buf):
        def window(base):
            pltpu.sync_copy(p1_hbm.at[:, pl.ds(base, SC_IDX)], i1_v)
            pltpu.sync_copy(p2_hbm.at[:, pl.ds(base, SC_IDX)], i2_v)
            for k in range(SC_IDX // SC_ROWS):
                rows = pl.ds(base + k * SC_ROWS, SC_ROWS)
                pltpu.sync_copy(y_hbm.at[i1_v.at[0, pl.ds(k * SC_ROWS, SC_ROWS)]], buf)
                pltpu.sync_copy(buf, o1_hbm.at[rows])
                pltpu.sync_copy(y_hbm.at[i2_v.at[0, pl.ds(k * SC_ROWS, SC_ROWS)]], buf)
                pltpu.sync_copy(buf, o2_hbm.at[rows])
        _sc_windows(t, window)

    return gather(ys, pos1.reshape(1, t), pos2.reshape(1, t))


def _ple_sparse_kernel(h_ref, info_ref, y1_ref, y2_ref, p_ref, wpp_ref, wpg_ref, gp_ref, gf_ref, y_ref):
    info = info_ref[...]
    g1 = info[:, INFO_G1:INFO_G1 + 1]
    g2 = info[:, INFO_G2:INFO_G2 + 1]
    y1_lo, y1_hi = _unpack_rows(y1_ref[...])
    y2_lo, y2_hi = _unpack_rows(y2_ref[...])
    moe = jnp.concatenate([g1 * y1_lo + g2 * y2_lo, g1 * y1_hi + g2 * y2_hi], axis=1)
    h = h_ref[...] + moe
    hn = _rmsnorm(h, gp_ref[...])
    h = h + _mm(p_ref[...], wpp_ref[...]) * _sigmoid(_mm(hn, wpg_ref[...]))
    y_ref[...] = _rmsnorm(h, gf_ref[...])


def _ple_sparse(h, info, y1, y2, p, wpp, wpg, gp, gf):
    t = h.shape[0]
    tm = ROW_TM
    row = lambda n: pl.BlockSpec((tm, n), lambda i: (i, 0))
    full = lambda a: pl.BlockSpec(a.shape, lambda i: (0,) * a.ndim)
    return pl.pallas_call(
        _ple_sparse_kernel,
        grid=(t // tm,),
        in_specs=[row(D_MODEL), row(LANES), row(HALF), row(HALF), row(PLE_DIM),
                  full(wpp), full(wpg), full(gp), full(gf)],
        out_specs=row(D_MODEL),
        out_shape=jax.ShapeDtypeStruct((t, D_MODEL), F32),
        compiler_params=_params("parallel"),
        name="ple_sparse",
    )(h, info, y1, y2, p, wpp, wpg, gp, gf)


def _tile_tables(cnt, max_tiles):
    tiles_e = (cnt + (MOE_TM - 1)) // MOE_TM
    ends = jnp.cumsum(tiles_e)
    n_tiles = ends[-1]
    tile = jnp.arange(max_tiles, dtype=jnp.int32)
    idx = jnp.minimum(tile, n_tiles - 1)
    tile_expert = jnp.sum((idx[:, None] >= ends[None, :]).astype(jnp.int32), axis=1)
    mine = tile_expert[:, None] == jnp.arange(N_EXPERTS, dtype=jnp.int32)[None, :]
    of_mine = lambda v: jnp.sum(jnp.where(mine, v[None, :], 0), axis=1)
    valid = jnp.clip(of_mine(cnt) - (idx - of_mine(ends - tiles_e)) * MOE_TM, 0, MOE_TM)
    tile_valid = jnp.where(tile < n_tiles, valid, 0).astype(jnp.int32)
    return (tile_expert, tile_valid, n_tiles.reshape(1), (ends - 1).astype(jnp.int32),
            tiles_e.astype(jnp.int32))


def _ple_final_kernel(h_ref, m_ref, p_ref, wpp_ref, wpg_ref, gp_ref, gf_ref, y_ref):
    h = h_ref[...] + m_ref[...]
    hn = _rmsnorm(h, gp_ref[...])
    h = h + _mm(p_ref[...], wpp_ref[...]) * _sigmoid(_mm(hn, wpg_ref[...]))
    y_ref[...] = _rmsnorm(h, gf_ref[...])


def _ple_final(h, m, p, wpp, wpg, gp, gf):
    t = h.shape[0]
    tm = min(t, 256)
    row = lambda n: pl.BlockSpec((tm, n), lambda i: (i, 0))
    full = lambda a: pl.BlockSpec(a.shape, lambda i: (0,) * a.ndim)
    return pl.pallas_call(
        _ple_final_kernel,
        grid=(t // tm,),
        in_specs=[row(D_MODEL), row(D_MODEL), row(PLE_DIM), full(wpp), full(wpg), full(gp), full(gf)],
        out_specs=row(D_MODEL),
        out_shape=jax.ShapeDtypeStruct((t, D_MODEL), F32),
        compiler_params=_params("parallel"),
        name="ple_final",
    )(h, m, p, wpp, wpg, gp, gf)


def kernel(x_prompt, x_sample, p_prompt, p_sample, cache_k, cache_v, state_conv, state_S, rel_bias, norm_mix, w_in, att_sink, conv_w, dn_A_log, dn_dt_bias, dn_norm, w_out, norm_ffn, w_router_group, w_router_expert, w_gate, w_up, w_down, w_ple_proj, w_ple_gate, norm_ple, norm_final):
    batch, seq, _ = x_prompt.shape
    nseq = x_sample.shape[0]
    assert x_sample.shape[1] == 1 and norm_mix.shape[0] == 1 and cache_k.shape[2] == WINDOW
    assert seq % GDN_TB == 0 and seq % ATT_BLOCK == 0

    wi = w_in[0]
    o_db = ATT_COLS + CONV_CH
    w_in_re = jnp.concatenate(
        [wi[:, :o_db], wi[:, o_db + 2 * DN_HEADS:], wi[:, o_db:o_db + 2 * DN_HEADS],
         jnp.zeros((D_MODEL, LANES - 2 * DN_HEADS), F32)], axis=1).astype(BF16)
    row = lambda a: a.reshape(1, -1).astype(F32)
    pad_lanes = lambda a, off: jnp.zeros((1, LANES), F32).at[0, off:off + a.shape[0]].set(a)
    alog = pad_lanes(dn_A_log[0], DN_HEADS)
    dtb = pad_lanes(dn_dt_bias[0], DN_HEADS)
    dnx = jnp.tile(dn_norm[0], DN_HEADS).reshape(1, DN_WIDTH)
    w_router = jnp.concatenate(
        [w_router_group[0], w_router_expert[0],
         jnp.zeros((D_MODEL, LANES - N_GROUPS - N_EXPERTS), F32)], axis=1).astype(BF16)
    wo = w_out[0].astype(BF16)
    wg, wu, wd = w_gate[0], w_up[0], w_down[0]
    wpp, wpg = w_ple_proj[0].astype(BF16), w_ple_gate[0].astype(BF16)
    sink = att_sink[0]

    qi = np.arange(ATT_BLOCK)[:, None]
    kj = np.arange(2 * ATT_BLOCK)[None, :]
    bucket_p = jnp.asarray(_t5_bucket_np(qi + ATT_BLOCK - kj))
    bucket_s = jnp.asarray(_t5_bucket_np(WINDOW - np.arange(WINDOW)[None, :]))

    def tail(x, o_att, o_dn, p):
        h1, xn2, gates = _outproj_router(x, o_att, o_dn, wo, row(norm_ffn[0]), w_router)
        moe = _moe(xn2, gates, wg, wu, wd)
        return _ple_final(h1, moe, p, wpp, wpg, row(norm_ple[0]), row(norm_final))

    xp = x_prompt.reshape(batch * seq, D_MODEL)
    att_p, xc_p, dz_p, ba_p = _inproj(xp, row(norm_mix[0]), w_in_re)
    o_att_p = _attn_prompt(att_p, bucket_p, rel_bias, sink, batch, seq)
    o_dn_p, s_p = _gdn_prompt(xc_p, dz_p, ba_p, conv_w[0], alog, dtb, dnx, batch, seq)
    h1, xn2, info, cnt = _route_sparse(xp, o_att_p, o_dn_p, wo, row(norm_ffn[0]), w_router)
    pos = _positions(info, cnt)
    pos1, pos2 = pos[:, 0], pos[:, 1]
    max_tiles = _moe_tiles(batch * seq)
    cnt_e = cnt[0, ROUTER_OFF:ROUTER_OFF + N_EXPERTS].astype(jnp.int32)
    tile_expert, tile_valid, n_tiles, last_tile, used = _tile_tables(cnt_e, max_tiles)
    xs = _sc_scatter_rows(xn2, pos1, pos2, max_tiles * MOE_TM)
    ys = _experts(xs, tile_expert, tile_valid, n_tiles, wg, wu, wd)
    y1, y2 = _sc_gather_rows(ys, pos1, pos2)
    y_p = _ple_sparse(h1, info, y1, y2, p_prompt[0].reshape(batch * seq, PLE_DIM),
                      wpp, wpg, row(norm_ple[0]), row(norm_final))

    xs = x_sample.reshape(nseq, D_MODEL)
    att_s, xc_s, dz_s, ba_s = _inproj(xs, row(norm_mix[0]), w_in_re)
    ck = cache_k[0].reshape(nseq, WINDOW, KV_WIDTH)
    cv = cache_v[0].reshape(nseq, WINDOW, KV_WIDTH)
    o_att_s = _attn_sample(att_s, ck, cv, bucket_s, rel_bias, sink)
    sconv_t = jnp.swapaxes(state_conv[0], 0, 1)
    o_dn_s, s_s = _gdn_sample(xc_s, dz_s, ba_s, sconv_t,
                              state_S[0].reshape(nseq, DN_HEADS * DN_DK, DN_DV), conv_w[0], alog, dtb,
                              dn_norm[0].reshape(1, DN_DV))
    s_s = s_s.reshape(nseq, DN_HEADS, DN_DK, DN_DV)
    y_s = tail(xs, o_att_s, o_dn_s.reshape(nseq, DN_WIDTH), p_sample[0].reshape(nseq, PLE_DIM))

    att_p3 = att_p.reshape(batch, seq, ATT_COLS)
    kv_shape = (1, batch, WINDOW, ATT_KV_HEADS, HEAD_DIM)
    k_p = att_p3[:, seq - WINDOW:, ATT_WIDTH:ATT_WIDTH + KV_WIDTH].reshape(kv_shape)
    v_p = att_p3[:, seq - WINDOW:, ATT_WIDTH + KV_WIDTH:].reshape(kv_shape)
    conv_p = xc_p.reshape(batch, seq, CONV_CH)[:, seq - (CONV_WIDTH - 1):][None]
    k_new = att_s[:, None, ATT_WIDTH:ATT_WIDTH + KV_WIDTH]
    v_new = att_s[:, None, ATT_WIDTH + KV_WIDTH:]
    kv_s_shape = (1, nseq, WINDOW, ATT_KV_HEADS, HEAD_DIM)
    k_s = jnp.concatenate([ck[:, 1:], k_new], axis=1).reshape(kv_s_shape)
    v_s = jnp.concatenate([cv[:, 1:], v_new], axis=1).reshape(kv_s_shape)
    conv_s = jnp.concatenate([state_conv[0][:, 1:], xc_s[:, None, :]], axis=1)[None]
    return (y_p.reshape(batch, seq, D_MODEL), y_s.reshape(nseq, 1, D_MODEL),
            k_p, v_p, conv_p, s_p[None], k_s, v_s, conv_s, s_s[None])
```

```python
import functools
import math

import numpy as np
import jax
import jax.numpy as jnp
from jax import lax
from jax.experimental import pallas as pl
from jax.experimental.pallas import tpu as pltpu
from jax.experimental.pallas import tpu_sc as plsc

F32 = jnp.float32
BF16 = jnp.bfloat16

D_MODEL = 1024
ATT_HEADS = 8
ATT_KV_HEADS = 2
HEAD_DIM = 64
GQA = ATT_HEADS // ATT_KV_HEADS
WINDOW = 128
ATT_BLOCK = 128
N_BUCKETS = 32
DN_HEADS = 8
DN_DK = 64
DN_DV = 64
CONV_WIDTH = 4
DN_CHUNK = 64
ATT_WIDTH = ATT_HEADS * HEAD_DIM
KV_WIDTH = ATT_KV_HEADS * HEAD_DIM
DN_WIDTH = DN_HEADS * DN_DV
CONV_CH = 3 * DN_WIDTH
N_GROUPS = 4
EXPERTS_PER_GROUP = 8
N_EXPERTS = N_GROUPS * EXPERTS_PER_GROUP
D_EXPERT = 256
PLE_DIM = 256
EPS = 1e-6
NEG_INF = float("-inf")

ATT_COLS = ATT_WIDTH + 2 * KV_WIDTH
LANES = 128
IN_COLS = ATT_COLS + CONV_CH + DN_WIDTH + LANES
ROUTER_OFF = N_GROUPS
VMEM_LIMIT = 48 * 1024 * 1024
ROW_TM = 512


def _params(*sem):
    return pltpu.CompilerParams(dimension_semantics=sem, vmem_limit_bytes=VMEM_LIMIT)


def _mm(a, b):
    return jnp.dot(a.astype(BF16), b.astype(BF16), preferred_element_type=F32)


def _mm_nt(a, b):
    return lax.dot_general(a.astype(BF16), b.astype(BF16), (((1,), (1,)), ((), ())),
                           preferred_element_type=F32)


def _mm_tn(a, b):
    return lax.dot_general(a.astype(BF16), b.astype(BF16), (((0,), (0,)), ((), ())),
                           preferred_element_type=F32)


def _split3(x):
    h1 = x.astype(BF16)
    r1 = x - h1.astype(F32)
    h2 = r1.astype(BF16)
    h3 = (r1 - h2.astype(F32)).astype(BF16)
    return h1, h2, h3


def _mm_sel_rhs(x, sel):
    h1, h2, h3 = _split3(x)
    d = lambda h: jnp.dot(h, sel, preferred_element_type=F32)
    return d(h1) + d(h2) + d(h3)


def _mm_sel_lhs(sel, x):
    h1, h2, h3 = _split3(x)
    d = lambda h: jnp.dot(sel, h, preferred_element_type=F32)
    return d(h1) + d(h2) + d(h3)


def _mm3(a, b):
    ah = a.astype(BF16)
    al = (a - ah.astype(F32)).astype(BF16)
    bh = b.astype(BF16)
    bl = (b - bh.astype(F32)).astype(BF16)
    d = lambda u, v: jnp.dot(u, v, preferred_element_type=F32)
    return d(ah, bh) + d(ah, bl) + d(al, bh)


def _sigmoid(x):
    return 1.0 / (1.0 + jnp.exp(-x))


def _silu(x):
    return x * _sigmoid(x)


def _softplus(x):
    return jnp.maximum(x, 0.0) + jnp.log1p(jnp.exp(-jnp.abs(x)))


def _rmsnorm(x, g):
    return x * lax.rsqrt(jnp.mean(x * x, axis=-1, keepdims=True) + EPS) * g


def _t5_bucket_np(dist):
    max_exact = N_BUCKETS // 2
    d = np.maximum(dist, 0)
    ratio = (np.log(np.maximum(d, 1).astype(np.float32) / np.float32(max_exact))
             / np.float32(math.log(WINDOW / max_exact))).astype(np.float32)
    large = np.minimum(max_exact + (ratio * np.float32(N_BUCKETS - max_exact)).astype(np.int32),
                       N_BUCKETS - 1)
    return np.where(d < max_exact, d, large).astype(np.int32)


def _bias_lookup(bucket, rb_ref, h):
    acc = jnp.zeros(bucket.shape, F32)
    for t in range(N_BUCKETS):
        acc = jnp.where(bucket == t, rb_ref[t, h], acc)
    return acc


def _inproj_kernel(x_ref, g_ref, w_ref, att_ref, xc_ref, dz_ref, ba_ref):
    xn = _rmsnorm(x_ref[...], g_ref[...]).astype(BF16)
    o0, o1, o2 = ATT_COLS, ATT_COLS + CONV_CH, ATT_COLS + CONV_CH + DN_WIDTH
    att_ref[...] = jnp.dot(xn, w_ref[:, :o0], preferred_element_type=F32)
    xc_ref[...] = jnp.dot(xn, w_ref[:, o0:o1], preferred_element_type=F32)
    dz_ref[...] = jnp.dot(xn, w_ref[:, o1:o2], preferred_element_type=F32)
    ba_ref[...] = jnp.dot(xn, w_ref[:, o2:], preferred_element_type=F32)


def _inproj(x, g, w):
    t = x.shape[0]
    tm = min(t, ROW_TM)
    row = lambda n: pl.BlockSpec((tm, n), lambda i: (i, 0))
    full = lambda a: pl.BlockSpec(a.shape, lambda i: (0,) * a.ndim)
    return pl.pallas_call(
        _inproj_kernel,
        grid=(t // tm,),
        in_specs=[row(D_MODEL), full(g), full(w)],
        out_specs=[row(ATT_COLS), row(CONV_CH), row(DN_WIDTH), row(LANES)],
        out_shape=[jax.ShapeDtypeStruct((t, n), F32) for n in (ATT_COLS, CONV_CH, DN_WIDTH, LANES)],
        compiler_params=_params("parallel"),
        name="inproj",
    )(x, g, w)


GROUP_ROWS = GQA * ATT_BLOCK


def _attn_prompt_kernel(cur_ref, prev_ref, bucket_ref, rb_ref, sink_ref, o_ref, bias_scr, sink_scr):
    i = pl.program_id(0)
    nseq = cur_ref.shape[0]

    @pl.when(i == 0)
    def _():
        qi = lax.broadcasted_iota(jnp.int32, (ATT_BLOCK, 2 * ATT_BLOCK), 0)
        kj = lax.broadcasted_iota(jnp.int32, (ATT_BLOCK, 2 * ATT_BLOCK), 1)
        dist = qi + ATT_BLOCK - kj
        band = jnp.logical_and(dist >= 0, dist < WINDOW)
        bucket = bucket_ref[...]
        hrow = lax.broadcasted_iota(jnp.int32, (GROUP_ROWS, 1), 0) // ATT_BLOCK
        for g in range(ATT_KV_HEADS):
            sink_col = jnp.zeros((GROUP_ROWS, 1), F32)
            for hh in range(GQA):
                h = g * GQA + hh
                bias = jnp.where(band, _bias_lookup(bucket, rb_ref, h), NEG_INF)
                bias_scr[0, g, hh * ATT_BLOCK:(hh + 1) * ATT_BLOCK, :] = bias
                bias_scr[1, g, hh * ATT_BLOCK:(hh + 1) * ATT_BLOCK, :] = jnp.where(kj >= ATT_BLOCK, bias, NEG_INF)
                sink_col = jnp.where(hrow == hh, sink_ref[h], sink_col)
            sink_scr[g] = sink_col

    first = (i == 0).astype(jnp.int32)
    probs = [(b, g) for b in range(nseq) for g in range(ATT_KV_HEADS)]
    scores = []
    for b, g in probs:
        cur = cur_ref[b]
        prev = prev_ref[b]
        q = jnp.concatenate([cur[:, (g * GQA + hh) * HEAD_DIM:(g * GQA + hh + 1) * HEAD_DIM]
                             for hh in range(GQA)], axis=0) * (HEAD_DIM ** -0.5)
        kcol = slice(ATT_WIDTH + g * HEAD_DIM, ATT_WIDTH + (g + 1) * HEAD_DIM)
        k2 = jnp.concatenate([prev[:, kcol], cur[:, kcol]], axis=0)
        scores.append(_mm_nt(q, k2) + bias_scr[first, g])
    probs_p, dens = [], []
    for (b, g), s in zip(probs, scores):
        sink = sink_scr[g]
        m = jnp.maximum(jnp.max(s, axis=-1, keepdims=True), sink)
        p = jnp.exp(s - m)
        dens.append(jnp.sum(p, axis=-1, keepdims=True) + jnp.exp(sink - m))
        probs_p.append(p)
    outs = {}
    for (b, g), p, den in zip(probs, probs_p, dens):
        vcol = slice(ATT_WIDTH + KV_WIDTH + g * HEAD_DIM, ATT_WIDTH + KV_WIDTH + (g + 1) * HEAD_DIM)
        v2 = jnp.concatenate([prev_ref[b][:, vcol], cur_ref[b][:, vcol]], axis=0)
        outs[b, g] = _mm(p, v2) / den
    for b in range(nseq):
        o_ref[b] = jnp.concatenate([outs[b, g][hh * ATT_BLOCK:(hh + 1) * ATT_BLOCK, :]
                                    for g in range(ATT_KV_HEADS) for hh in range(GQA)], axis=1)


def _attn_prompt(att, bucket, rel_bias, sink, batch, seq):
    nb = seq // ATT_BLOCK
    smem = pl.BlockSpec(memory_space=pltpu.SMEM)
    att3 = att.reshape(batch, seq, ATT_COLS)
    out = pl.pallas_call(
        _attn_prompt_kernel,
        grid=(nb,),
        in_specs=[
            pl.BlockSpec((batch, ATT_BLOCK, ATT_COLS), lambda i: (0, i, 0)),
            pl.BlockSpec((batch, ATT_BLOCK, ATT_COLS), lambda i: (0, jnp.maximum(i - 1, 0), 0)),
            pl.BlockSpec(bucket.shape, lambda i: (0, 0)),
            smem, smem,
        ],
        out_specs=pl.BlockSpec((batch, ATT_BLOCK, ATT_WIDTH), lambda i: (0, i, 0)),
        out_shape=jax.ShapeDtypeStruct((batch, seq, ATT_WIDTH), F32),
        scratch_shapes=[pltpu.VMEM((2, ATT_KV_HEADS, GROUP_ROWS, 2 * ATT_BLOCK), F32),
                        pltpu.VMEM((ATT_KV_HEADS, GROUP_ROWS, 1), F32)],
        compiler_params=_params("arbitrary"),
        name="attn_prompt",
    )(att3, att3, bucket, rel_bias, sink)
    return out.reshape(batch * seq, ATT_WIDTH)


ATT_S_BB = 8


def _attn_sample_kernel(att_ref, ck_ref, cv_ref, bucket_ref, rb_ref, sink_ref, o_ref,
                        bias_scr, col_scr):
    hrow = lax.broadcasted_iota(jnp.int32, (ATT_HEADS, LANES), 0)
    lane = lax.broadcasted_iota(jnp.int32, (ATT_HEADS, LANES), 1)

    @pl.when(pl.program_id(0) == 0)
    def _():
        bucket = jnp.broadcast_to(bucket_ref[...], (ATT_HEADS, LANES))
        bias = jnp.zeros((ATT_HEADS, LANES), F32)
        cols = jnp.zeros((ATT_HEADS, LANES), F32)
        for h in range(ATT_HEADS):
            bias = jnp.where(hrow == h, _bias_lookup(bucket, rb_ref, h), bias)
            cols = jnp.where(jnp.logical_and(hrow == h, lane == 0), sink_ref[h], cols)
            cols = jnp.where(jnp.logical_and(hrow == h, lane == 1), rb_ref[0, h], cols)
        bias_scr[...] = jnp.where(lane >= 1, bias, NEG_INF)
        col_scr[...] = cols

    bias_c = bias_scr[...]
    sink = col_scr[:, 0:1]
    bias_n = col_scr[:, 1:2]
    same_group = (hrow // GQA) == (lane // HEAD_DIM)
    low_group = lax.broadcasted_iota(jnp.int32, (ATT_HEADS, HEAD_DIM), 0) < GQA
    rnd = lambda a: a.astype(BF16).astype(F32)
    seqs = range(ATT_S_BB)
    rows = [att_ref[b:b + 1, :] for b in seqs]
    q_bds = []
    for row in rows:
        q = row[:, :ATT_WIDTH] * (HEAD_DIM ** -0.5)
        qh = jnp.concatenate([q[:, h * HEAD_DIM:(h + 1) * HEAD_DIM] for h in range(ATT_HEADS)], axis=0)
        q_bds.append(jnp.where(same_group, jnp.concatenate([qh, qh], axis=1), 0.0))
    s_cs = [_mm_nt(q_bd, ck_ref[b]) + bias_c for b, q_bd in zip(seqs, q_bds)]
    prs, pns = [], []
    for row, q_bd, s_c in zip(rows, q_bds, s_cs):
        kn = row[:, ATT_WIDTH:ATT_WIDTH + KV_WIDTH]
        s_n = jnp.sum(rnd(q_bd) * rnd(kn), axis=-1, keepdims=True) + bias_n
        m = jnp.maximum(jnp.maximum(jnp.max(s_c, axis=-1, keepdims=True), s_n), sink)
        p_c = jnp.exp(s_c - m)
        p_n = jnp.exp(s_n - m)
        den = jnp.sum(p_c, axis=-1, keepdims=True) + p_n + jnp.exp(sink - m)
        prs.append(p_c / den)
        pns.append(p_n / den)
    pvs = [_mm(pr, cv_ref[b]) for b, pr in zip(seqs, prs)]
    for b, row, pv, pn in zip(seqs, rows, pvs, pns):
        vn = row[:, ATT_WIDTH + KV_WIDTH:]
        o_full = pv + rnd(pn) * rnd(vn)
        o_sel = jnp.where(low_group, o_full[:, :HEAD_DIM], o_full[:, HEAD_DIM:])
        o_ref[b:b + 1, :] = jnp.concatenate([o_sel[h:h + 1, :] for h in range(ATT_HEADS)], axis=1)


def _attn_sample(att, ck, cv, bucket, rel_bias, sink):
    nseq = att.shape[0]
    smem = pl.BlockSpec(memory_space=pltpu.SMEM)
    cache = pl.BlockSpec((ATT_S_BB, WINDOW, KV_WIDTH), lambda i: (i, 0, 0))
    return pl.pallas_call(
        _attn_sample_kernel,
        grid=(nseq // ATT_S_BB,),
        in_specs=[pl.BlockSpec((ATT_S_BB, ATT_COLS), lambda i: (i, 0)), cache, cache,
                  pl.BlockSpec(bucket.shape, lambda i: (0, 0)), smem, smem],
        out_specs=pl.BlockSpec((ATT_S_BB, ATT_WIDTH), lambda i: (i, 0)),
        out_shape=jax.ShapeDtypeStruct((nseq, ATT_WIDTH), F32),
        scratch_shapes=[pltpu.VMEM((ATT_HEADS, LANES), F32), pltpu.VMEM((ATT_HEADS, LANES), F32)],
        compiler_params=_params("arbitrary"),
        name="attn_sample",
    )(att, ck, cv, bucket, rel_bias, sink)


GDN_TB = 128
GDN_NC = GDN_TB // DN_CHUNK
TAIL = 8


def _gdn_gates(ba, alog, dtb):
    beta = _sigmoid(ba)
    g = -jnp.exp(alog) * _softplus(ba + dtb)
    return beta, g


PAIR = 2 * DN_DK
N_PAIRS = DN_WIDTH // PAIR


def _pair_diag(x, lo):
    xb = x.astype(BF16)
    zero = jnp.zeros_like(xb)
    return jnp.concatenate([jnp.where(lo, xb, zero), jnp.where(lo, zero, xb)], axis=0)


def _gdn_prompt_kernel(xc_ref, dz_ref, ba_ref, cw_ref, alog_ref, dtb_ref, dnx_ref,
                       hsum_ref, expb_ref, expg_ref, ltri_ref,
                       o_ref, s_out_ref, xp_scr, s_scr):
    i = pl.program_id(0)
    nb = xc_ref.shape[0]

    @pl.when(i == 0)
    def _():
        xp_scr[:, 0:TAIL, :] = jnp.zeros((nb, TAIL, CONV_CH), F32)
        s_scr[...] = jnp.zeros(s_scr.shape, F32)

    hsum = hsum_ref[...]
    ri = lax.broadcasted_iota(jnp.int32, (DN_CHUNK, PAIR), 0)
    ci = lax.broadcasted_iota(jnp.int32, (DN_CHUNK, PAIR), 1)
    lo = ci < DN_DK
    cj = jnp.where(lo, ci, ci - DN_DK)
    causal = ri >= cj
    strict = ri > cj
    eye = (ri == cj).astype(F32)

    def sel2(x, m):
        hi = x.astype(BF16)
        lw = (x - hi.astype(F32)).astype(BF16)
        return (jnp.dot(hi, m, preferred_element_type=F32) + jnp.dot(lw, m, preferred_element_type=F32))

    def head_sums(z):
        hi = z.astype(BF16)
        lw = (z - hi.astype(F32)).astype(BF16)
        d = lambda a, p: jnp.dot(a[:, p * PAIR:(p + 1) * PAIR], hsum, preferred_element_type=F32)
        return jnp.concatenate([d(hi, p) + d(lw, p) for p in range(N_PAIRS)], axis=1)

    ys = []
    for b in range(nb):
        xc = xc_ref[b]
        xp_scr[b, TAIL:, :] = xc
        y = xp_scr[b, TAIL - 3:TAIL - 3 + GDN_TB, :] * cw_ref[0:1, :]
        y = y + xp_scr[b, TAIL - 2:TAIL - 2 + GDN_TB, :] * cw_ref[1:2, :]
        y = y + xp_scr[b, TAIL - 1:TAIL - 1 + GDN_TB, :] * cw_ref[2:3, :]
        y = y + xc * cw_ref[3:4, :]
        xp_scr[b, 0:TAIL, :] = xc[GDN_TB - TAIL:, :]
        ys.append(_silu(y))
    qk_raw = [y[:, j * DN_WIDTH:(j + 1) * DN_WIDTH] for y in ys for j in range(2)]
    inv_norm = lax.rsqrt(head_sums(jnp.concatenate([a * a for a in qk_raw], axis=0)) + EPS)
    pre = []
    for b in range(nb):
        q = qk_raw[2 * b] * inv_norm[2 * b * GDN_TB:(2 * b + 1) * GDN_TB] * (DN_DK ** -0.5)
        k = qk_raw[2 * b + 1] * inv_norm[(2 * b + 1) * GDN_TB:(2 * b + 2) * GDN_TB]
        v = ys[b][:, 2 * DN_WIDTH:]
        beta_c, g_c = _gdn_gates(ba_ref[b], alog_ref[...], dtb_ref[...])
        beta = sel2(beta_c, expb_ref[...])
        gam_c = _mm_sel_lhs(ltri_ref[...], g_c)
        gam = _mm_sel_rhs(gam_c, expg_ref[...])
        gam_t = gam_c.T
        kb = k * beta
        egam = jnp.exp(gam)
        pre.append(dict(q=q, k=k, kb=kb, vb=v * beta, qg=q * egam, wr=kb * egam, gam=gam, gam_t=gam_t))

    probs = [(b, p) for b in range(nb) for p in range(N_PAIRS)]
    pick = lambda m: jnp.where(lo, m[:DN_DK], m[DN_DK:])
    o_rows = [[] for _ in range(nb)]
    for c in range(GDN_NC):
        r0, r1 = c * DN_CHUNK, (c + 1) * DN_CHUNK
        sl = lambda name, b, p: pre[b][name][r0:r1, p * PAIR:(p + 1) * PAIR]
        raws = []
        for b, p in probs:
            k_p = sl("k", b, p)
            k_rows = jnp.concatenate([jnp.where(lo, k_p, 0.0), jnp.where(lo, 0.0, k_p)], axis=0)
            raws.append(_mm_nt(jnp.concatenate([sl("kb", b, p), sl("q", b, p)], axis=0), k_rows))
        pws, ts, qks = [], [], []
        for (b, p), raw in zip(probs, raws):
            gcol = sl("gam", b, p)
            h0 = DN_HEADS + 2 * p
            gam_t = pre[b]["gam_t"]
            grow = jnp.concatenate([gam_t[h0:h0 + 1, r0:r1], gam_t[h0 + 1:h0 + 2, r0:r1]], axis=1)
            decay = jnp.exp(jnp.where(causal, gcol - grow, NEG_INF))
            a = jnp.where(strict, raw[:DN_CHUNK] * decay, 0.0)
            qks.append(jnp.where(causal, raw[DN_CHUNK:] * decay, 0.0))
            pws.append(-a)
            ts.append(eye - a)
        pws = [_mm(pw, _pair_diag(pw, lo)) for pw in pws]
        for _ in range(4):
            rs = [_mm(jnp.concatenate([pw, t], axis=0), _pair_diag(pw, lo)) for pw, t in zip(pws, ts)]
            pws = [r[:DN_CHUNK] for r in rs]
            ts = [t + r[DN_CHUNK:] for t, r in zip(ts, rs)]
        rs = [_mm(t, _pair_diag(pw, lo)) for pw, t in zip(pws, ts)]
        ts = [t + r for t, r in zip(ts, rs)]
        sols = [_mm(t, jnp.concatenate([_pair_diag(sl("vb", b, p), lo), _pair_diag(sl("wr", b, p), lo)],
                                       axis=1)) for (b, p), t in zip(probs, ts)]
        qkuws = [_mm(qk, jnp.concatenate([_pair_diag(s[:, :PAIR], lo), _pair_diag(s[:, PAIR:], lo)], axis=1))
                 for qk, s in zip(qks, sols)]
        crosses, gls = [], []
        for (b, p), s in zip(probs, sols):
            gam_last = pre[b]["gam"][r1 - 1:r1, p * PAIR:(p + 1) * PAIR]
            kd = sl("k", b, p) * jnp.exp(gam_last - sl("gam", b, p))
            crosses.append(_mm_tn(kd, s))
            gls.append(jnp.exp(gam_last))
        lhs = [jnp.concatenate([pick(cr[:, PAIR:]), sl("qg", b, p) - qkuw[:, PAIR:]], axis=0)
               for (b, p), cr, qkuw in zip(probs, crosses, qkuws)]
        s_olds = [s_scr[b, p] for b, p in probs]
        rs = [_mm(l, _pair_diag(s_old, lo)) for l, s_old in zip(lhs, s_olds)]
        o_pairs = [[] for _ in range(nb)]
        for (b, p), r, s_old, gl, cr, qkuw in zip(probs, rs, s_olds, gls, crosses, qkuws):
            s_scr[b, p] = gl * s_old - r[:DN_DK] + pick(cr[:, :PAIR])
            o_pairs[b].append(r[DN_DK:] + qkuw[:, :PAIR])
        for b in range(nb):
            o_rows[b].append(jnp.concatenate(o_pairs[b], axis=1))

    o_all = jnp.concatenate([jnp.concatenate(rows, axis=0) for rows in o_rows], axis=0)
    inv_rms = lax.rsqrt(head_sums(o_all * o_all) * (1.0 / DN_DV) + EPS)
    for b in range(nb):
        rows = slice(b * GDN_TB, (b + 1) * GDN_TB)
        o_ref[b] = o_all[rows] * inv_rms[rows] * dnx_ref[...] * _silu(dz_ref[b])

    @pl.when(i == pl.num_programs(0) - 1)
    def _():
        for b in range(nb):
            for p in range(N_PAIRS):
                s_p = s_scr[b, p]
                s_out_ref[b, 2 * p] = s_p[:, :DN_DV]
                s_out_ref[b, 2 * p + 1] = s_p[:, DN_DV:]


def _gdn_consts():
    lane = np.arange(DN_WIDTH)
    pl_lane = np.arange(PAIR)
    hsum = (pl_lane[:, None] // DN_DV == pl_lane[None, :] // DN_DV)
    src = np.arange(LANES)
    expb = (src[:, None] == lane[None, :] // DN_DV)
    expg = (src[:, None] == DN_HEADS + lane[None, :] // DN_DV)
    tok = np.arange(GDN_TB)
    ltri = np.logical_and(tok[:, None] >= tok[None, :],
                          tok[:, None] // DN_CHUNK == tok[None, :] // DN_CHUNK)
    as_bf16 = lambda m: jnp.asarray(m.astype(np.float32), dtype=BF16)
    return as_bf16(hsum), as_bf16(expb), as_bf16(expg), as_bf16(ltri)


def _gdn_prompt(xc, dz, ba, conv_w, alog, dtb, dnx, batch, seq):
    nt = seq // GDN_TB
    hsum, expb, expg, ltri = _gdn_consts()
    row = lambda n: pl.BlockSpec((batch, GDN_TB, n), lambda i: (0, i, 0))
    full = lambda a: pl.BlockSpec(a.shape, lambda i: (0,) * a.ndim)
    consts = (conv_w, alog, dtb, dnx, hsum, expb, expg, ltri)
    as3d = lambda a: a.reshape(batch, seq, a.shape[-1])
    o, s = pl.pallas_call(
        _gdn_prompt_kernel,
        grid=(nt,),
        in_specs=[row(CONV_CH), row(DN_WIDTH), row(LANES)] + [full(a) for a in consts],
        out_specs=[row(DN_WIDTH),
                   pl.BlockSpec((batch, DN_HEADS, DN_DK, DN_DV), lambda i: (0, 0, 0, 0))],
        out_shape=[jax.ShapeDtypeStruct((batch, seq, DN_WIDTH), F32),
                   jax.ShapeDtypeStruct((batch, DN_HEADS, DN_DK, DN_DV), F32)],
        scratch_shapes=[pltpu.VMEM((batch, TAIL + GDN_TB, CONV_CH), F32),
                        pltpu.VMEM((batch, N_PAIRS, DN_DK, PAIR), F32)],
        compiler_params=_params("arbitrary"),
        name="gdn_prompt",
    )(as3d(xc), as3d(dz), as3d(ba), *consts)
    return o.reshape(batch * seq, DN_WIDTH), s


GDN_S_BB = 8


def _gdn_sample_kernel(xc_ref, dz_ref, ba_ref, sc_ref, s_ref, cw_ref, alog_ref, dtb_ref, dn_ref,
                       hsum_ref, eye_ref, hsel_ref, hrep3_ref, o_ref, s_out_ref):
    xc = xc_ref[...]
    y = sc_ref[0] * cw_ref[0:1, :]
    y = y + sc_ref[1] * cw_ref[1:2, :]
    y = y + sc_ref[2] * cw_ref[2:3, :]
    y = _silu(y + xc * cw_ref[3:4, :])
    hsum = hsum_ref[...]
    q = y[:, :DN_WIDTH]
    k = y[:, DN_WIDTH:2 * DN_WIDTH]
    v = y[:, 2 * DN_WIDTH:]
    q = q * lax.rsqrt(_mm_sel_rhs(q * q, hsum) + EPS) * (DN_DK ** -0.5)
    k = k * lax.rsqrt(_mm_sel_rhs(k * k, hsum) + EPS)
    beta_c, g_c = _gdn_gates(ba_ref[...], alog_ref[...], dtb_ref[...])
    eg_c = jnp.exp(g_c)
    eye = eye_ref[...]
    tr = lambda a: lax.dot_general(a, eye, (((0,), (0,)), ((), ())), precision=lax.Precision.HIGHEST,
                                   preferred_element_type=F32)
    gates_t = tr(jnp.concatenate([beta_c, eg_c], axis=1))
    beta_t = gates_t[:LANES]
    eg_t = gates_t[LANES:]
    dz = dz_ref[...]
    dn = dn_ref[...]
    split = lambda r: jnp.concatenate([r[:, h * DN_DV:(h + 1) * DN_DV] for h in range(DN_HEADS)], axis=0)
    own_head = hsel_ref[...].astype(F32)
    hrep3 = hrep3_ref[...]
    seqs = range(GDN_S_BB)
    dot = lambda a, b: jnp.dot(a.astype(BF16), b.astype(BF16), preferred_element_type=F32)

    def pieces(x):
        p1 = x.astype(BF16).astype(F32)
        r1 = x - p1
        p2 = r1.astype(BF16).astype(F32)
        return p1, p2, (r1 - p2).astype(BF16).astype(F32)

    heads = DN_HEADS
    k_pieces, kqs = [], []
    for b in seqs:
        kq_bd = jnp.concatenate([own_head * k[b:b + 1, :], own_head * q[b:b + 1, :]], axis=0)
        a1, a2, a3 = pieces(kq_bd)
        s1, s2, s3 = pieces(s_ref[b])
        r1 = dot(jnp.concatenate([a1, a2, a3], axis=0), s1)
        r2 = dot(jnp.concatenate([a1, a2], axis=0), s2)
        r3 = dot(a1, s3)
        n = 2 * heads
        kqs.append(((r3 + r2[n:] + r1[2 * n:]) + (r2[:n] + r1[n:2 * n])) + r1[:n])
        k_pieces.append((a1[:heads], a2[:heads], a3[:heads]))
    egs = [eg_t[DN_HEADS:2 * DN_HEADS, b:b + 1] for b in seqs]
    qks = [jnp.sum(split(q[b:b + 1, :]) * split(k[b:b + 1, :]), axis=-1, keepdims=True) for b in seqs]
    v_news = [beta_t[0:DN_HEADS, b:b + 1] * (split(v[b:b + 1, :]) - eg * kq[:heads])
              for b, eg, kq in zip(seqs, egs, kqs)]
    os_ = [eg * kq[heads:] + qk * v_new for eg, kq, qk, v_new in zip(egs, kqs, qks, v_news)]
    inv_rms = [lax.rsqrt(jnp.mean(o * o, axis=-1, keepdims=True) + EPS) for o in os_]
    for b, o, r in zip(seqs, os_, inv_rms):
        o_ref[b] = o * r * dn * _silu(split(dz[b:b + 1, :]))
    outers, egrows = [], []
    for (k1, k2, k3), v_new, eg in zip(k_pieces, v_news, egs):
        v1, v2, v3 = pieces(v_new)
        lhs = jnp.concatenate([k1, k1, k2, k1, k2, k3], axis=0).astype(BF16)
        rhs = jnp.concatenate([v1, v2, v1, v3, v2, v1], axis=0).astype(BF16)
        outers.append(lax.dot_general(lhs, rhs, (((0,), (0,)), ((), ())), preferred_element_type=F32))
        egrows.append(dot(hrep3, jnp.concatenate(pieces(jnp.broadcast_to(eg, (DN_HEADS, DN_DV))), axis=0)))
    for b, outer, egrow in zip(seqs, outers, egrows):
        s_out_ref[b] = s_ref[b] * egrow + outer


def _gdn_sample(xc, dz, ba, sconv_t, state, conv_w, alog, dtb, dn):
    nseq = xc.shape[0]
    lane = np.arange(DN_WIDTH)
    hsum = jnp.asarray((lane[:, None] // DN_DV == lane[None, :] // DN_DV).astype(np.float32), dtype=BF16)
    eye = jnp.eye(GDN_S_BB, dtype=F32)
    hsel_np = (np.arange(DN_HEADS)[:, None] == lane[None, :] // DN_DK).astype(np.float32)
    hsel = jnp.asarray(hsel_np, dtype=BF16)
    hrep3 = jnp.asarray(np.tile(hsel_np.T, (1, 3)), dtype=BF16)
    row = lambda n: pl.BlockSpec((GDN_S_BB, n), lambda i: (i, 0))
    full = lambda a: pl.BlockSpec(a.shape, lambda i: (0,) * a.ndim)
    st = pl.BlockSpec((GDN_S_BB, DN_HEADS * DN_DK, DN_DV), lambda i: (i, 0, 0))
    consts = (conv_w, alog, dtb, dn, hsum, eye, hsel, hrep3)
    return pl.pallas_call(
        _gdn_sample_kernel,
        grid=(nseq // GDN_S_BB,),
        in_specs=[row(CONV_CH), row(DN_WIDTH), row(LANES),
                  pl.BlockSpec((CONV_WIDTH - 1, GDN_S_BB, CONV_CH), lambda i: (0, i, 0)), st]
                 + [full(a) for a in consts],
        out_specs=[pl.BlockSpec((GDN_S_BB, DN_HEADS, DN_DV), lambda i: (i, 0, 0)), st],
        out_shape=[jax.ShapeDtypeStruct((nseq, DN_HEADS, DN_DV), F32),
                   jax.ShapeDtypeStruct(state.shape, F32)],
        compiler_params=_params("parallel"),
        name="gdn_sample",
    )(xc, dz, ba, sconv_t, state, *consts)


def _route(xn, wr):
    logits = jnp.dot(xn, wr, preferred_element_type=F32)
    lane = lax.broadcasted_iota(jnp.int32, logits.shape, 1).astype(F32)
    first_at = lambda hit: jnp.min(jnp.where(hit, lane, float(LANES)), axis=-1, keepdims=True)
    glog = jnp.where(lane < N_GROUPS, logits, NEG_INF)
    gmax = jnp.max(glog, axis=-1, keepdims=True)
    gsel = first_at(glog == gmax)
    pgsel = 1.0 / jnp.sum(jnp.exp(glog - gmax), axis=-1, keepdims=True)
    lo = ROUTER_OFF + gsel * EXPERTS_PER_GROUP
    in_group = jnp.logical_and(lane >= lo, lane < lo + EXPERTS_PER_GROUP)
    elog = jnp.where(in_group, logits, NEG_INF)
    m1 = jnp.max(elog, axis=-1, keepdims=True)
    i1 = first_at(elog == m1)
    z = jnp.sum(jnp.exp(elog - m1), axis=-1, keepdims=True)
    elog2 = jnp.where(lane == i1, NEG_INF, elog)
    m2 = jnp.max(elog2, axis=-1, keepdims=True)
    i2 = first_at(elog2 == m2)
    p1 = 1.0 / z
    p2 = jnp.exp(m2 - m1) / z
    tot = p1 + p2
    return lane, i1, i2, p1 / tot * pgsel, p2 / tot * pgsel


def _outproj(x_ref, oa_ref, od_ref, wo_ref):
    return x_ref[...] + _mm(oa_ref[...], wo_ref[:ATT_WIDTH, :]) + _mm(od_ref[...], wo_ref[ATT_WIDTH:, :])


def _outproj_router_kernel(x_ref, oa_ref, od_ref, wo_ref, g_ref, wr_ref, h_ref, xn_ref, gate_ref):
    h = _outproj(x_ref, oa_ref, od_ref, wo_ref)
    h_ref[...] = h
    xn = _rmsnorm(h, g_ref[...]).astype(BF16)
    xn_ref[...] = xn
    lane, i1, i2, g1, g2 = _route(xn, wr_ref[...])
    gate_ref[...] = jnp.where(lane == i1, g1, 0.0) + jnp.where(lane == i2, g2, 0.0)


def _outproj_router(x, oa, od, wo, g, wr):
    t = x.shape[0]
    tm = min(t, 256)
    row = lambda n: pl.BlockSpec((tm, n), lambda i: (i, 0))
    full = lambda a: pl.BlockSpec(a.shape, lambda i: (0,) * a.ndim)
    return pl.pallas_call(
        _outproj_router_kernel,
        grid=(t // tm,),
        in_specs=[row(D_MODEL), row(ATT_WIDTH), row(DN_WIDTH), full(wo), full(g), full(wr)],
        out_specs=[row(D_MODEL), row(D_MODEL), row(LANES)],
        out_shape=[jax.ShapeDtypeStruct((t, D_MODEL), F32), jax.ShapeDtypeStruct((t, D_MODEL), BF16),
                   jax.ShapeDtypeStruct((t, LANES), F32)],
        compiler_params=_params("parallel"),
        name="outproj_router",
    )(x, oa, od, wo, g, wr)


def _moe_kernel(xn_ref, gate_ref, wg_ref, wu_ref, wd_ref, o_ref):
    e = pl.program_id(1)
    xn = xn_ref[...]
    lane = lax.broadcasted_iota(jnp.int32, gate_ref.shape, 1)
    gate = jnp.sum(jnp.where(lane == e + ROUTER_OFF, gate_ref[...], 0.0), axis=-1, keepdims=True)
    hg = jnp.dot(xn, wg_ref[...].astype(BF16), preferred_element_type=F32)
    hu = jnp.dot(xn, wu_ref[...].astype(BF16), preferred_element_type=F32)
    hm = _silu(hg) * hu * gate
    y = jnp.dot(hm.astype(BF16), wd_ref[...].astype(BF16), preferred_element_type=F32)

    @pl.when(e == 0)
    def _():
        o_ref[...] = y

    @pl.when(e > 0)
    def _():
        o_ref[...] += y


def _moe(xn, gates, wg, wu, wd):
    t = xn.shape[0]
    tm = min(t, 1024)
    return pl.pallas_call(
        _moe_kernel,
        grid=(t // tm, N_EXPERTS),
        in_specs=[pl.BlockSpec((tm, D_MODEL), lambda i, e: (i, 0)),
                  pl.BlockSpec((tm, LANES), lambda i, e: (i, 0)),
                  pl.BlockSpec((None, D_MODEL, D_EXPERT), lambda i, e: (e, 0, 0)),
                  pl.BlockSpec((None, D_MODEL, D_EXPERT), lambda i, e: (e, 0, 0)),
                  pl.BlockSpec((None, D_EXPERT, D_MODEL), lambda i, e: (e, 0, 0))],
        out_specs=pl.BlockSpec((tm, D_MODEL), lambda i, e: (i, 0)),
        out_shape=jax.ShapeDtypeStruct((t, D_MODEL), F32),
        compiler_params=_params("parallel", "arbitrary"),
        name="moe",
    )(xn, gates, wg, wu, wd)


MOE_TM = 256
POS_TM = 1024
INFO_G1, INFO_G2, INFO_E1, INFO_E2 = 0, 1, 2, 3
DMA_UNROLL = 8


def _moe_tiles(t):
    return (2 * t) // MOE_TM + N_EXPERTS


HALF = D_MODEL // 2
U32 = jnp.uint32


def _pack_rows(x):
    bits = lambda v: lax.bitcast_convert_type(v.astype(BF16).astype(F32), U32)
    return bits(x[:, HALF:]) | (bits(x[:, :HALF]) >> 16)


def _unpack_rows(w):
    lo = lax.bitcast_convert_type(w << 16, F32)
    hi = lax.bitcast_convert_type(w & jnp.uint32(0xFFFF0000), F32)
    return lo, hi


def _route_kernel(x_ref, oa_ref, od_ref, wo_ref, g_ref, wr_ref, h_ref, xn_ref, info_ref, cnt_ref, run_scr):
    h = _outproj(x_ref, oa_ref, od_ref, wo_ref)
    h_ref[...] = h
    xn = _rmsnorm(h, g_ref[...])
    xn_ref[...] = _pack_rows(xn)
    lane, i1, i2, g1, g2 = _route(xn.astype(BF16), wr_ref[...])
    info = jnp.where(lane == INFO_G1, g1, 0.0) + jnp.where(lane == INFO_G2, g2, 0.0)
    info = info + jnp.where(lane == INFO_E1, i1, 0.0) + jnp.where(lane == INFO_E2, i2, 0.0)
    info_ref[...] = info

    @pl.when(pl.program_id(0) == 0)
    def _():
        run_scr[...] = jnp.zeros(run_scr.shape, F32)
    picked = jnp.logical_or(lane == i1, lane == i2).astype(F32)
    run_scr[...] += jnp.sum(picked, axis=0, keepdims=True)
    cnt_ref[...] = run_scr[...]


def _route_sparse(x, oa, od, wo, g, wr):
    t = x.shape[0]
    tm = ROW_TM
    row = lambda n: pl.BlockSpec((tm, n), lambda i: (i, 0))
    full = lambda a: pl.BlockSpec(a.shape, lambda i: (0,) * a.ndim)
    return pl.pallas_call(
        _route_kernel,
        grid=(t // tm,),
        in_specs=[row(D_MODEL), row(ATT_WIDTH), row(DN_WIDTH), full(wo), full(g), full(wr)],
        out_specs=[row(D_MODEL), row(HALF), row(LANES), pl.BlockSpec((1, LANES), lambda i: (0, 0))],
        out_shape=[jax.ShapeDtypeStruct((t, D_MODEL), F32), jax.ShapeDtypeStruct((t, HALF), U32),
                   jax.ShapeDtypeStruct((t, LANES), F32), jax.ShapeDtypeStruct((1, LANES), F32)],
        scratch_shapes=[pltpu.VMEM((1, LANES), F32)],
        compiler_params=_params("arbitrary"),
        name="route",
    )(x, oa, od, wo, g, wr)


def _positions_kernel(info_ref, cnt_ref, ltri_ref, utri_ref, pos_ref, run_scr, off_scr):
    info = info_ref[...]
    lane = lax.broadcasted_iota(jnp.int32, info.shape, 1).astype(F32)
    hit1 = lane == info[:, INFO_E1:INFO_E1 + 1]
    hit2 = lane == info[:, INFO_E2:INFO_E2 + 1]
    onehot = jnp.logical_or(hit1, hit2).astype(F32)

    @pl.when(pl.program_id(0) == 0)
    def _():
        tiles = jnp.floor((cnt_ref[...] + (MOE_TM - 1)) * (1.0 / MOE_TM))
        off_scr[...] = MOE_TM * jnp.dot(tiles.astype(BF16), utri_ref[...], preferred_element_type=F32)
        run_scr[...] = jnp.zeros(run_scr.shape, F32)

    before = (jnp.dot(ltri_ref[...], onehot.astype(BF16), preferred_element_type=F32)
              + run_scr[...] + off_scr[...])
    pos1 = jnp.sum(jnp.where(hit1, before, 0.0), axis=-1, keepdims=True)
    pos2 = jnp.sum(jnp.where(hit2, before, 0.0), axis=-1, keepdims=True)
    pos_ref[...] = (jnp.where(lane == 0, pos1, 0.0) + jnp.where(lane == 1, pos2, 0.0)).astype(jnp.int32)
    run_scr[...] += jnp.sum(onehot, axis=0, keepdims=True)


def _positions(info, cnt):
    t = info.shape[0]
    tm = min(t, POS_TM)
    tok = np.arange(tm)
    ltri = jnp.asarray((tok[:, None] > tok[None, :]).astype(np.float32), dtype=BF16)
    ln = np.arange(LANES)
    utri = jnp.asarray((ln[:, None] < ln[None, :]).astype(np.float32), dtype=BF16)
    full = lambda a: pl.BlockSpec(a.shape, lambda i: (0,) * a.ndim)
    return pl.pallas_call(
        _positions_kernel,
        grid=(t // tm,),
        in_specs=[pl.BlockSpec((tm, LANES), lambda i: (i, 0)), full(cnt), full(ltri), full(utri)],
        out_specs=pl.BlockSpec((tm, LANES), lambda i: (i, 0)),
        out_shape=jax.ShapeDtypeStruct((t, LANES), jnp.int32),
        scratch_shapes=[pltpu.VMEM((1, LANES), F32), pltpu.VMEM((1, LANES), F32)],
        compiler_params=_params("arbitrary"),
        name="positions",
    )(info, cnt, ltri, utri)


def _row_copy(src_hbm, src_row, dst_hbm, dst_row, sem):
    return pltpu.make_async_copy(src_hbm.at[pl.ds(src_row, 1)], dst_hbm.at[pl.ds(dst_row, 1)], sem)


SCATTER_SLOTS = 3


def _scatter_kernel(pos1_ref, pos2_ref, last_ref, used_ref, nt_ref, xn_hbm, zero_hbm, xs_hbm,
                    buf, lsem, sem, zsem, *, n_tok):
    max_tiles = xs_hbm.shape[0] // MOE_TM

    def zero_tile(tile):
        return pltpu.make_async_copy(zero_hbm, xs_hbm.at[pl.ds(tile * MOE_TM, MOE_TM)], zsem)

    def for_unused(fn):
        def body(tile, carry):
            fn(tile)
            return carry
        lax.fori_loop(nt_ref[0], max_tiles, body, 0)

    for e in range(N_EXPERTS):
        @pl.when(used_ref[e] > 0)
        def _():
            zero_tile(last_ref[e]).start()
    for_unused(lambda tile: zero_tile(tile).start())
    for e in range(N_EXPERTS):
        @pl.when(used_ref[e] > 0)
        def _():
            zero_tile(last_ref[e]).wait()
    for_unused(lambda tile: zero_tile(tile).wait())

    tm = buf.shape[1]
    n = n_tok // tm

    def load(i):
        return pltpu.make_async_copy(xn_hbm.at[pl.ds(i * tm, tm)], buf.at[i % SCATTER_SLOTS],
                                     lsem.at[i % SCATTER_SLOTS])

    def wait_rows(slot):
        pltpu.make_async_copy(xs_hbm.at[pl.ds(0, 2 * tm)], xs_hbm.at[pl.ds(0, 2 * tm)], sem.at[slot]).wait()

    load(0).start()
    load(1).start()

    def step(i, carry):
        slot = i % SCATTER_SLOTS
        load(i).wait()

        def body(j, c2):
            tok = i * tm + j
            src = buf.at[slot, pl.ds(j, 1)]
            pltpu.make_async_copy(src, xs_hbm.at[pl.ds(pos1_ref[tok], 1)], sem.at[slot]).start()
            pltpu.make_async_copy(src, xs_hbm.at[pl.ds(pos2_ref[tok], 1)], sem.at[slot]).start()
            return c2
        lax.fori_loop(0, tm, body, 0, unroll=DMA_UNROLL)

        @pl.when(i >= 1)
        def _():
            wait_rows((i - 1) % SCATTER_SLOTS)

        @pl.when(i + 2 < n)
        def _():
            load(i + 2).start()
        return carry
    lax.fori_loop(0, n, step, 0)
    wait_rows((n - 1) % SCATTER_SLOTS)


def _scatter_rows(xn, pos1, pos2, last_tile, used, n_tiles, n_rows):
    t = xn.shape[0]
    zero = jnp.zeros((MOE_TM, D_MODEL), F32)
    any_spec = pl.BlockSpec(memory_space=pl.ANY)
    return pl.pallas_call(
        functools.partial(_scatter_kernel, n_tok=t),
        grid_spec=pltpu.PrefetchScalarGridSpec(
            num_scalar_prefetch=5, grid=(1,),
            in_specs=[any_spec, any_spec], out_specs=any_spec,
            scratch_shapes=[pltpu.VMEM((SCATTER_SLOTS, MOE_TM, D_MODEL), F32),
                            pltpu.SemaphoreType.DMA((SCATTER_SLOTS,)),
                            pltpu.SemaphoreType.DMA((SCATTER_SLOTS,)),
                            pltpu.SemaphoreType.DMA]),
        out_shape=jax.ShapeDtypeStruct((n_rows, D_MODEL), F32),
        compiler_params=_params("arbitrary"),
        name="scatter_rows",
    )(pos1, pos2, last_tile, used, n_tiles, xn, zero)


def _experts_kernel(te_ref, tv_ref, nt_ref, xs_ref, wg_ref, wu_ref, wd_ref, ys_ref, wg_s, wu_s, wd_s):
    i = pl.program_id(0)
    used = i < nt_ref[0]

    @pl.when(jnp.logical_or(i == 0, te_ref[i] != te_ref[jnp.maximum(i - 1, 0)]))
    def _():
        wg_s[...] = wg_ref[...].astype(BF16)
        wu_s[...] = wu_ref[...].astype(BF16)
        wd_s[...] = wd_ref[...].astype(BF16)

    @pl.when(used)
    def _():
        row = lax.broadcasted_iota(jnp.int32, xs_ref.shape, 0)
        x_lo, x_hi = _unpack_rows(jnp.where(row < tv_ref[i], xs_ref[...], jnp.uint32(0)))
        x_lo = x_lo.astype(BF16)
        x_hi = x_hi.astype(BF16)
        up = lambda w_s: (jnp.dot(x_lo, w_s[:HALF, :], preferred_element_type=F32)
                          + jnp.dot(x_hi, w_s[HALF:, :], preferred_element_type=F32))
        hm = (_silu(up(wg_s)) * up(wu_s)).astype(BF16)
        ys_ref[...] = _pack_rows(jnp.dot(hm, wd_s[...], preferred_element_type=F32))

    @pl.when(jnp.logical_not(used))
    def _():
        ys_ref[...] = jnp.zeros(ys_ref.shape, U32)


def _experts(xs, tile_expert, tile_valid, n_tiles, wg, wu, wd):
    max_tiles = xs.shape[0] // MOE_TM
    rows = pl.BlockSpec((MOE_TM, HALF), lambda i, te, tv, nt: (i, 0))
    wspec = lambda shape: pl.BlockSpec((None,) + shape, lambda i, te, tv, nt: (te[i], 0, 0))
    return pl.pallas_call(
        _experts_kernel,
        grid_spec=pltpu.PrefetchScalarGridSpec(
            num_scalar_prefetch=3, grid=(max_tiles,),
            in_specs=[rows, wspec((D_MODEL, D_EXPERT)), wspec((D_MODEL, D_EXPERT)),
                      wspec((D_EXPERT, D_MODEL))],
            out_specs=rows,
            scratch_shapes=[pltpu.VMEM((D_MODEL, D_EXPERT), BF16), pltpu.VMEM((D_MODEL, D_EXPERT), BF16),
                            pltpu.VMEM((D_EXPERT, D_MODEL), BF16)]),
        out_shape=jax.ShapeDtypeStruct(xs.shape, U32),
        compiler_params=_params("arbitrary"),
        name="experts",
    )(tile_expert, tile_valid, n_tiles, xs, wg, wu, wd)


def _ple_gather_kernel(pos1_ref, pos2_ref, h_ref, info_ref, p_ref, wpp_ref, wpg_ref, gp_ref, gf_ref,
                       ys_hbm, y_ref, ybuf, sem):
    i = pl.program_id(0)
    n = pl.num_programs(0)
    tm = h_ref.shape[0]

    def issue(tile, slot):
        def body(j, carry):
            tok = tile * tm + j
            pltpu.make_async_copy(ys_hbm.at[pl.ds(pos1_ref[tok], 1)], ybuf.at[slot, 0, pl.ds(j, 1)],
                                  sem.at[slot]).start()
            pltpu.make_async_copy(ys_hbm.at[pl.ds(pos2_ref[tok], 1)], ybuf.at[slot, 1, pl.ds(j, 1)],
                                  sem.at[slot]).start()
            return carry
        lax.fori_loop(0, tm, body, 0, unroll=DMA_UNROLL)

    @pl.when(i == 0)
    def _():
        issue(0, 0)

    @pl.when(i + 1 < n)
    def _():
        issue(i + 1, (i + 1) % 2)

    slot = i % 2
    pltpu.make_async_copy(ybuf.at[slot], ybuf.at[slot], sem.at[slot]).wait()
    info = info_ref[...]
    moe = info[:, INFO_G1:INFO_G1 + 1] * ybuf[slot, 0] + info[:, INFO_G2:INFO_G2 + 1] * ybuf[slot, 1]
    h = h_ref[...] + moe
    hn = _rmsnorm(h, gp_ref[...])
    h = h + _mm(p_ref[...], wpp_ref[...]) * _sigmoid(_mm(hn, wpg_ref[...]))
    y_ref[...] = _rmsnorm(h, gf_ref[...])


def _ple_gather(h, info, p, ys, pos1, pos2, wpp, wpg, gp, gf):
    t = h.shape[0]
    tm = 256
    row = lambda n: pl.BlockSpec((tm, n), lambda i, p1, p2: (i, 0))
    full = lambda a: pl.BlockSpec(a.shape, lambda i, p1, p2: (0,) * a.ndim)
    return pl.pallas_call(
        _ple_gather_kernel,
        grid_spec=pltpu.PrefetchScalarGridSpec(
            num_scalar_prefetch=2, grid=(t // tm,),
            in_specs=[row(D_MODEL), row(LANES), row(PLE_DIM), full(wpp), full(wpg), full(gp), full(gf),
                      pl.BlockSpec(memory_space=pl.ANY)],
            out_specs=row(D_MODEL),
            scratch_shapes=[pltpu.VMEM((2, 2, tm, D_MODEL), F32), pltpu.SemaphoreType.DMA((2,))]),
        out_shape=jax.ShapeDtypeStruct((t, D_MODEL), F32),
        compiler_params=_params("arbitrary"),
        name="ple_gather",
    )(pos1, pos2, h, info, p, wpp, wpg, gp, gf, ys)


SC_IDX = 128
SC_ROWS = 64
SC_WORKERS = 32


def _sc_mesh():
    return plsc.VectorSubcoreMesh(core_axis_name="c", subcore_axis_name="s")


def _sc_windows(t, fn):
    per_worker = t // SC_WORKERS
    worker = lax.axis_index(("c", "s"))

    @pl.loop(0, per_worker // SC_IDX)
    def _(w):
        fn(worker * per_worker + w * SC_IDX)


def _sc_scatter_rows(xn, pos1, pos2, n_rows):
    t, d = xn.shape
    assert t % (SC_WORKERS * SC_IDX) == 0
    idx_t = pltpu.VMEM((1, SC_IDX), jnp.int32)

    @pl.kernel(out_type=jax.ShapeDtypeStruct((n_rows, d), xn.dtype), mesh=_sc_mesh(),
               scratch_types=[idx_t, idx_t, pltpu.VMEM((SC_ROWS, d), xn.dtype)])
    def scatter(x_hbm, p1_hbm, p2_hbm, o_hbm, i1_v, i2_v, buf):
        def window(base):
            pltpu.sync_copy(p1_hbm.at[:, pl.ds(base, SC_IDX)], i1_v)
            pltpu.sync_copy(p2_hbm.at[:, pl.ds(base, SC_IDX)], i2_v)
            for k in range(SC_IDX // SC_ROWS):
                pltpu.sync_copy(x_hbm.at[pl.ds(base + k * SC_ROWS, SC_ROWS)], buf)
                pltpu.sync_copy(buf, o_hbm.at[i1_v.at[0, pl.ds(k * SC_ROWS, SC_ROWS)]])
                pltpu.sync_copy(buf, o_hbm.at[i2_v.at[0, pl.ds(k * SC_ROWS, SC_ROWS)]])
        _sc_windows(t, window)

    return scatter(xn, pos1.reshape(1, t), pos2.reshape(1, t))


def _sc_gather_rows(ys, pos1, pos2):
    t = pos1.shape[0]
    d = ys.shape[1]
    assert t % (SC_WORKERS * SC_IDX) == 0
    idx_t = pltpu.VMEM((1, SC_IDX), jnp.int32)
    out = jax.ShapeDtypeStruct((t, d), ys.dtype)

    @pl.kernel(out_type=(out, out), mesh=_sc_mesh(),
               scratch_types=[idx_t, idx_t, pltpu.VMEM((SC_ROWS, d), ys.dtype)])
    def gather(y_hbm, p1_hbm, p2_hbm, o1_hbm, o2_hbm, i1_v, i2_v, buf):
        def window(base):
            pltpu.sync_copy(p1_hbm.at[:, pl.ds(base, SC_IDX)], i1_v)
            pltpu.sync_copy(p2_hbm.at[:, pl.ds(base, SC_IDX)], i2_v)
            for k in range(SC_IDX // SC_ROWS):
                rows = pl.ds(base + k * SC_ROWS, SC_ROWS)
                pltpu.sync_copy(y_hbm.at[i1_v.at[0, pl.ds(k * SC_ROWS, SC_ROWS)]], buf)
                pltpu.sync_copy(buf, o1_hbm.at[rows])
                pltpu.sync_copy(y_hbm.at[i2_v.at[0, pl.ds(k * SC_ROWS, SC_ROWS)]], buf)
                pltpu.sync_copy(buf, o2_hbm.at[rows])
        _sc_windows(t, window)

    return gather(ys, pos1.reshape(1, t), pos2.reshape(1, t))


def _ple_sparse_kernel(h_ref, info_ref, y1_ref, y2_ref, p_ref, wpp_ref, wpg_ref, gp_ref, gf_ref, y_ref):
    info = info_ref[...]
    g1 = info[:, INFO_G1:INFO_G1 + 1]
    g2 = info[:, INFO_G2:INFO_G2 + 1]
    y1_lo, y1_hi = _unpack_rows(y1_ref[...])
    y2_lo, y2_hi = _unpack_rows(y2_ref[...])
    moe = jnp.concatenate([g1 * y1_lo + g2 * y2_lo, g1 * y1_hi + g2 * y2_hi], axis=1)
    h = h_ref[...] + moe
    hn = _rmsnorm(h, gp_ref[...])
    h = h + _mm(p_ref[...], wpp_ref[...]) * _sigmoid(_mm(hn, wpg_ref[...]))
    y_ref[...] = _rmsnorm(h, gf_ref[...])


def _ple_sparse(h, info, y1, y2, p, wpp, wpg, gp, gf):
    t = h.shape[0]
    tm = ROW_TM
    row = lambda n: pl.BlockSpec((tm, n), lambda i: (i, 0))
    full = lambda a: pl.BlockSpec(a.shape, lambda i: (0,) * a.ndim)
    return pl.pallas_call(
        _ple_sparse_kernel,
        grid=(t // tm,),
        in_specs=[row(D_MODEL), row(LANES), row(HALF), row(HALF), row(PLE_DIM),
                  full(wpp), full(wpg), full(gp), full(gf)],
        out_specs=row(D_MODEL),
        out_shape=jax.ShapeDtypeStruct((t, D_MODEL), F32),
        compiler_params=_params("parallel"),
        name="ple_sparse",
    )(h, info, y1, y2, p, wpp, wpg, gp, gf)


def _tile_tables(cnt, max_tiles):
    tiles_e = (cnt + (MOE_TM - 1)) // MOE_TM
    ends = jnp.cumsum(tiles_e)
    n_tiles = ends[-1]
    tile = jnp.arange(max_tiles, dtype=jnp.int32)
    idx = jnp.minimum(tile, n_tiles - 1)
    tile_expert = jnp.sum((idx[:, None] >= ends[None, :]).astype(jnp.int32), axis=1)
    mine = tile_expert[:, None] == jnp.arange(N_EXPERTS, dtype=jnp.int32)[None, :]
    of_mine = lambda v: jnp.sum(jnp.where(mine, v[None, :], 0), axis=1)
    valid = jnp.clip(of_mine(cnt) - (idx - of_mine(ends - tiles_e)) * MOE_TM, 0, MOE_TM)
    tile_valid = jnp.where(tile < n_tiles, valid, 0).astype(jnp.int32)
    return (tile_expert, tile_valid, n_tiles.reshape(1), (ends - 1).astype(jnp.int32),
            tiles_e.astype(jnp.int32))


def _ple_final_kernel(h_ref, m_ref, p_ref, wpp_ref, wpg_ref, gp_ref, gf_ref, y_ref):
    h = h_ref[...] + m_ref[...]
    hn = _rmsnorm(h, gp_ref[...])
    h = h + _mm(p_ref[...], wpp_ref[...]) * _sigmoid(_mm(hn, wpg_ref[...]))
    y_ref[...] = _rmsnorm(h, gf_ref[...])


def _ple_final(h, m, p, wpp, wpg, gp, gf):
    t = h.shape[0]
    tm = min(t, 256)
    row = lambda n: pl.BlockSpec((tm, n), lambda i: (i, 0))
    full = lambda a: pl.BlockSpec(a.shape, lambda i: (0,) * a.ndim)
    return pl.pallas_call(
        _ple_final_kernel,
        grid=(t // tm,),
        in_specs=[row(D_MODEL), row(D_MODEL), row(PLE_DIM), full(wpp), full(wpg), full(gp), full(gf)],
        out_specs=row(D_MODEL),
        out_shape=jax.ShapeDtypeStruct((t, D_MODEL), F32),
        compiler_params=_params("parallel"),
        name="ple_final",
    )(h, m, p, wpp, wpg, gp, gf)


def kernel(x_prompt, x_sample, p_prompt, p_sample, cache_k, cache_v, state_conv, state_S, rel_bias, norm_mix, w_in, att_sink, conv_w, dn_A_log, dn_dt_bias, dn_norm, w_out, norm_ffn, w_router_group, w_router_expert, w_gate, w_up, w_down, w_ple_proj, w_ple_gate, norm_ple, norm_final):
    batch, seq, _ = x_prompt.shape
    nseq = x_sample.shape[0]
    assert x_sample.shape[1] == 1 and norm_mix.shape[0] == 1 and cache_k.shape[2] == WINDOW
    assert seq % GDN_TB == 0 and seq % ATT_BLOCK == 0

    wi = w_in[0]
    o_db = ATT_COLS + CONV_CH
    w_in_re = jnp.concatenate(
        [wi[:, :o_db], wi[:, o_db + 2 * DN_HEADS:], wi[:, o_db:o_db + 2 * DN_HEADS],
         jnp.zeros((D_MODEL, LANES - 2 * DN_HEADS), F32)], axis=1).astype(BF16)
    row = lambda a: a.reshape(1, -1).astype(F32)
    pad_lanes = lambda a, off: jnp.zeros((1, LANES), F32).at[0, off:off + a.shape[0]].set(a)
    alog = pad_lanes(dn_A_log[0], DN_HEADS)
    dtb = pad_lanes(dn_dt_bias[0], DN_HEADS)
    dnx = jnp.tile(dn_norm[0], DN_HEADS).reshape(1, DN_WIDTH)
    w_router = jnp.concatenate(
        [w_router_group[0], w_router_expert[0],
         jnp.zeros((D_MODEL, LANES - N_GROUPS - N_EXPERTS), F32)], axis=1).astype(BF16)
    wo = w_out[0].astype(BF16)
    wg, wu, wd = w_gate[0], w_up[0], w_down[0]
    wpp, wpg = w_ple_proj[0].astype(BF16), w_ple_gate[0].astype(BF16)
    sink = att_sink[0]

    qi = np.arange(ATT_BLOCK)[:, None]
    kj = np.arange(2 * ATT_BLOCK)[None, :]
    bucket_p = jnp.asarray(_t5_bucket_np(qi + ATT_BLOCK - kj))
    bucket_s = jnp.asarray(_t5_bucket_np(WINDOW - np.arange(WINDOW)[None, :]))

    def tail(x, o_att, o_dn, p):
        h1, xn2, gates = _outproj_router(x, o_att, o_dn, wo, row(norm_ffn[0]), w_router)
        moe = _moe(xn2, gates, wg, wu, wd)
        return _ple_final(h1, moe, p, wpp, wpg, row(norm_ple[0]), row(norm_final))

    xp = x_prompt.reshape(batch * seq, D_MODEL)
    att_p, xc_p, dz_p, ba_p = _inproj(xp, row(norm_mix[0]), w_in_re)
    o_att_p = _attn_prompt(att_p, bucket_p, rel_bias, sink, batch, seq)
    o_dn_p, s_p = _gdn_prompt(xc_p, dz_p, ba_p, conv_w[0], alog, dtb, dnx, batch, seq)
    h1, xn2, info, cnt = _route_sparse(xp, o_att_p, o_dn_p, wo, row(norm_ffn[0]), w_router)
    pos = _positions(info, cnt)
    pos1, pos2 = pos[:, 0], pos[:, 1]
    max_tiles = _moe_tiles(batch * seq)
    cnt_e = cnt[0, ROUTER_OFF:ROUTER_OFF + N_EXPERTS].astype(jnp.int32)
    tile_expert, tile_valid, n_tiles, last_tile, used = _tile_tables(cnt_e, max_tiles)
    xs = _sc_scatter_rows(xn2, pos1, pos2, max_tiles * MOE_TM)
    ys = _experts(xs, tile_expert, tile_valid, n_tiles, wg, wu, wd)
    y1, y2 = _sc_gather_rows(ys, pos1, pos2)
    y_p = _ple_sparse(h1, info, y1, y2, p_prompt[0].reshape(batch * seq, PLE_DIM),
                      wpp, wpg, row(norm_ple[0]), row(norm_final))

    xs = x_sample.reshape(nseq, D_MODEL)
    att_s, xc_s, dz_s, ba_s = _inproj(xs, row(norm_mix[0]), w_in_re)
    ck = cache_k[0].reshape(nseq, WINDOW, KV_WIDTH)
    cv = cache_v[0].reshape(nseq, WINDOW, KV_WIDTH)
    o_att_s = _attn_sample(att_s, ck, cv, bucket_s, rel_bias, sink)
    sconv_t = jnp.swapaxes(state_conv[0], 0, 1)
    o_dn_s, s_s = _gdn_sample(xc_s, dz_s, ba_s, sconv_t,
                              state_S[0].reshape(nseq, DN_HEADS * DN_DK, DN_DV), conv_w[0], alog, dtb,
                              dn_norm[0].reshape(1, DN_DV))
    s_s = s_s.reshape(nseq, DN_HEADS, DN_DK, DN_DV)
    y_s = tail(xs, o_att_s, o_dn_s.reshape(nseq, DN_WIDTH), p_sample[0].reshape(nseq, PLE_DIM))

    att_p3 = att_p.reshape(batch, seq, ATT_COLS)
    kv_shape = (1, batch, WINDOW, ATT_KV_HEADS, HEAD_DIM)
    k_p = att_p3[:, seq - WINDOW:, ATT_WIDTH:ATT_WIDTH + KV_WIDTH].reshape(kv_shape)
    v_p = att_p3[:, seq - WINDOW:, ATT_WIDTH + KV_WIDTH:].reshape(kv_shape)
    conv_p = xc_p.reshape(batch, seq, CONV_CH)[:, seq - (CONV_WIDTH - 1):][None]
    k_new = att_s[:, None, ATT_WIDTH:ATT_WIDTH + KV_WIDTH]
    v_new = att_s[:, None, ATT_WIDTH + KV_WIDTH:]
    kv_s_shape = (1, nseq, WINDOW, ATT_KV_HEADS, HEAD_DIM)
    k_s = jnp.concatenate([ck[:, 1:], k_new], axis=1).reshape(kv_s_shape)
    v_s = jnp.concatenate([cv[:, 1:], v_new], axis=1).reshape(kv_s_shape)
    conv_s = jnp.concatenate([state_conv[0][:, 1:], xc_s[:, None, :]], axis=1)[None]
    return (y_p.reshape(batch, seq, D_MODEL), y_s.reshape(nseq, 1, D_MODEL),
            k_p, v_p, conv_p, s_p[None], k_s, v_s, conv_s, s_s[None])
```

```python
import functools
import math

import numpy as np
import jax
import jax.numpy as jnp
from jax import lax
from jax.experimental import pallas as pl
from jax.experimental.pallas import tpu as pltpu
from jax.experimental.pallas import tpu_sc as plsc

F32 = jnp.float32
BF16 = jnp.bfloat16

D_MODEL = 1024
ATT_HEADS = 8
ATT_KV_HEADS = 2
HEAD_DIM = 64
GQA = ATT_HEADS // ATT_KV_HEADS
WINDOW = 128
ATT_BLOCK = 128
N_BUCKETS = 32
DN_HEADS = 8
DN_DK = 64
DN_DV = 64
CONV_WIDTH = 4
DN_CHUNK = 64
ATT_WIDTH = ATT_HEADS * HEAD_DIM
KV_WIDTH = ATT_KV_HEADS * HEAD_DIM
DN_WIDTH = DN_HEADS * DN_DV
CONV_CH = 3 * DN_WIDTH
N_GROUPS = 4
EXPERTS_PER_GROUP = 8
N_EXPERTS = N_GROUPS * EXPERTS_PER_GROUP
D_EXPERT = 256
PLE_DIM = 256
EPS = 1e-6
NEG_INF = float("-inf")

ATT_COLS = ATT_WIDTH + 2 * KV_WIDTH
LANES = 128
IN_COLS = ATT_COLS + CONV_CH + DN_WIDTH + LANES
ROUTER_OFF = N_GROUPS
VMEM_LIMIT = 48 * 1024 * 1024
ROW_TM = 512


def _params(*sem):
    return pltpu.CompilerParams(dimension_semantics=sem, vmem_limit_bytes=VMEM_LIMIT)


def _mm(a, b):
    return jnp.dot(a.astype(BF16), b.astype(BF16), preferred_element_type=F32)


def _mm_nt(a, b):
    return lax.dot_general(a.astype(BF16), b.astype(BF16), (((1,), (1,)), ((), ())),
                           preferred_element_type=F32)


def _mm_tn(a, b):
    return lax.dot_general(a.astype(BF16), b.astype(BF16), (((0,), (0,)), ((), ())),
                           preferred_element_type=F32)


def _split3(x):
    h1 = x.astype(BF16)
    r1 = x - h1.astype(F32)
    h2 = r1.astype(BF16)
    h3 = (r1 - h2.astype(F32)).astype(BF16)
    return h1, h2, h3


def _mm_sel_rhs(x, sel):
    h1, h2, h3 = _split3(x)
    d = lambda h: jnp.dot(h, sel, preferred_element_type=F32)
    return d(h1) + d(h2) + d(h3)


def _mm_sel_lhs(sel, x):
    h1, h2, h3 = _split3(x)
    d = lambda h: jnp.dot(sel, h, preferred_element_type=F32)
    return d(h1) + d(h2) + d(h3)


def _mm3(a, b):
    ah = a.astype(BF16)
    al = (a - ah.astype(F32)).astype(BF16)
    bh = b.astype(BF16)
    bl = (b - bh.astype(F32)).astype(BF16)
    d = lambda u, v: jnp.dot(u, v, preferred_element_type=F32)
    return d(ah, bh) + d(ah, bl) + d(al, bh)


def _sigmoid(x):
    return 1.0 / (1.0 + jnp.exp(-x))


def _silu(x):
    return x * _sigmoid(x)


def _softplus(x):
    return jnp.maximum(x, 0.0) + jnp.log1p(jnp.exp(-jnp.abs(x)))


def _rmsnorm(x, g):
    return x * lax.rsqrt(jnp.mean(x * x, axis=-1, keepdims=True) + EPS) * g


def _t5_bucket_np(dist):
    max_exact = N_BUCKETS // 2
    d = np.maximum(dist, 0)
    ratio = (np.log(np.maximum(d, 1).astype(np.float32) / np.float32(max_exact))
             / np.float32(math.log(WINDOW / max_exact))).astype(np.float32)
    large = np.minimum(max_exact + (ratio * np.float32(N_BUCKETS - max_exact)).astype(np.int32),
                       N_BUCKETS - 1)
    return np.where(d < max_exact, d, large).astype(np.int32)


def _bias_lookup(bucket, rb_ref, h):
    acc = jnp.zeros(bucket.shape, F32)
    for t in range(N_BUCKETS):
        acc = jnp.where(bucket == t, rb_ref[t, h], acc)
    return acc


def _inproj_kernel(x_ref, g_ref, w_ref, att_ref, xc_ref, dz_ref, ba_ref):
    xn = _rmsnorm(x_ref[...], g_ref[...]).astype(BF16)
    o0, o1, o2 = ATT_COLS, ATT_COLS + CONV_CH, ATT_COLS + CONV_CH + DN_WIDTH
    att_ref[...] = jnp.dot(xn, w_ref[:, :o0], preferred_element_type=F32)
    xc_ref[...] = jnp.dot(xn, w_ref[:, o0:o1], preferred_element_type=F32)
    dz_ref[...] = jnp.dot(xn, w_ref[:, o1:o2], preferred_element_type=F32)
    ba_ref[...] = jnp.dot(xn, w_ref[:, o2:], preferred_element_type=F32)


def _inproj(x, g, w):
    t = x.shape[0]
    tm = min(t, ROW_TM)
    row = lambda n: pl.BlockSpec((tm, n), lambda i: (i, 0))
    full = lambda a: pl.BlockSpec(a.shape, lambda i: (0,) * a.ndim)
    return pl.pallas_call(
        _inproj_kernel,
        grid=(t // tm,),
        in_specs=[row(D_MODEL), full(g), full(w)],
        out_specs=[row(ATT_COLS), row(CONV_CH), row(DN_WIDTH), row(LANES)],
        out_shape=[jax.ShapeDtypeStruct((t, n), F32) for n in (ATT_COLS, CONV_CH, DN_WIDTH, LANES)],
        compiler_params=_params("parallel"),
        name="inproj",
    )(x, g, w)


GROUP_ROWS = GQA * ATT_BLOCK


def _attn_prompt_kernel(cur_ref, prev_ref, bucket_ref, rb_ref, sink_ref, o_ref, bias_scr, sink_scr):
    i = pl.program_id(0)
    nseq = cur_ref.shape[0]

    @pl.when(i == 0)
    def _():
        qi = lax.broadcasted_iota(jnp.int32, (ATT_BLOCK, 2 * ATT_BLOCK), 0)
        kj = lax.broadcasted_iota(jnp.int32, (ATT_BLOCK, 2 * ATT_BLOCK), 1)
        dist = qi + ATT_BLOCK - kj
        band = jnp.logical_and(dist >= 0, dist < WINDOW)
        bucket = bucket_ref[...]
        hrow = lax.broadcasted_iota(jnp.int32, (GROUP_ROWS, 1), 0) // ATT_BLOCK
        for g in range(ATT_KV_HEADS):
            sink_col = jnp.zeros((GROUP_ROWS, 1), F32)
            for hh in range(GQA):
                h = g * GQA + hh
                bias = jnp.where(band, _bias_lookup(bucket, rb_ref, h), NEG_INF)
                bias_scr[0, g, hh * ATT_BLOCK:(hh + 1) * ATT_BLOCK, :] = bias
                bias_scr[1, g, hh * ATT_BLOCK:(hh + 1) * ATT_BLOCK, :] = jnp.where(kj >= ATT_BLOCK, bias, NEG_INF)
                sink_col = jnp.where(hrow == hh, sink_ref[h], sink_col)
            sink_scr[g] = sink_col

    first = (i == 0).astype(jnp.int32)
    probs = [(b, g) for b in range(nseq) for g in range(ATT_KV_HEADS)]
    scores = []
    for b, g in probs:
        cur = cur_ref[b]
        prev = prev_ref[b]
        q = jnp.concatenate([cur[:, (g * GQA + hh) * HEAD_DIM:(g * GQA + hh + 1) * HEAD_DIM]
                             for hh in range(GQA)], axis=0) * (HEAD_DIM ** -0.5)
        kcol = slice(ATT_WIDTH + g * HEAD_DIM, ATT_WIDTH + (g + 1) * HEAD_DIM)
        k2 = jnp.concatenate([prev[:, kcol], cur[:, kcol]], axis=0)
        scores.append(_mm_nt(q, k2) + bias_scr[first, g])
    probs_p, dens = [], []
    for (b, g), s in zip(probs, scores):
        sink = sink_scr[g]
        m = jnp.maximum(jnp.max(s, axis=-1, keepdims=True), sink)
        p = jnp.exp(s - m)
        dens.append(jnp.sum(p, axis=-1, keepdims=True) + jnp.exp(sink - m))
        probs_p.append(p)
    outs = {}
    for (b, g), p, den in zip(probs, probs_p, dens):
        vcol = slice(ATT_WIDTH + KV_WIDTH + g * HEAD_DIM, ATT_WIDTH + KV_WIDTH + (g + 1) * HEAD_DIM)
        v2 = jnp.concatenate([prev_ref[b][:, vcol], cur_ref[b][:, vcol]], axis=0)
        outs[b, g] = _mm(p, v2) / den
    for b in range(nseq):
        o_ref[b] = jnp.concatenate([outs[b, g][hh * ATT_BLOCK:(hh + 1) * ATT_BLOCK, :]
                                    for g in range(ATT_KV_HEADS) for hh in range(GQA)], axis=1)


def _attn_prompt(att, bucket, rel_bias, sink, batch, seq):
    nb = seq // ATT_BLOCK
    smem = pl.BlockSpec(memory_space=pltpu.SMEM)
    att3 = att.reshape(batch, seq, ATT_COLS)
    out = pl.pallas_call(
        _attn_prompt_kernel,
        grid=(nb,),
        in_specs=[
            pl.BlockSpec((batch, ATT_BLOCK, ATT_COLS), lambda i: (0, i, 0)),
            pl.BlockSpec((batch, ATT_BLOCK, ATT_COLS), lambda i: (0, jnp.maximum(i - 1, 0), 0)),
            pl.BlockSpec(bucket.shape, lambda i: (0, 0)),
            smem, smem,
        ],
        out_specs=pl.BlockSpec((batch, ATT_BLOCK, ATT_WIDTH), lambda i: (0, i, 0)),
        out_shape=jax.ShapeDtypeStruct((batch, seq, ATT_WIDTH), F32),
        scratch_shapes=[pltpu.VMEM((2, ATT_KV_HEADS, GROUP_ROWS, 2 * ATT_BLOCK), F32),
                        pltpu.VMEM((ATT_KV_HEADS, GROUP_ROWS, 1), F32)],
        compiler_params=_params("arbitrary"),
        name="attn_prompt",
    )(att3, att3, bucket, rel_bias, sink)
    return out.reshape(batch * seq, ATT_WIDTH)


ATT_S_BB = 8


def _attn_sample_kernel(att_ref, ck_ref, cv_ref, bucket_ref, rb_ref, sink_ref, o_ref,
                        bias_scr, col_scr):
    hrow = lax.broadcasted_iota(jnp.int32, (ATT_HEADS, LANES), 0)
    lane = lax.broadcasted_iota(jnp.int32, (ATT_HEADS, LANES), 1)

    @pl.when(pl.program_id(0) == 0)
    def _():
        bucket = jnp.broadcast_to(bucket_ref[...], (ATT_HEADS, LANES))
        bias = jnp.zeros((ATT_HEADS, LANES), F32)
        cols = jnp.zeros((ATT_HEADS, LANES), F32)
        for h in range(ATT_HEADS):
            bias = jnp.where(hrow == h, _bias_lookup(bucket, rb_ref, h), bias)
            cols = jnp.where(jnp.logical_and(hrow == h, lane == 0), sink_ref[h], cols)
            cols = jnp.where(jnp.logical_and(hrow == h, lane == 1), rb_ref[0, h], cols)
        bias_scr[...] = jnp.where(lane >= 1, bias, NEG_INF)
        col_scr[...] = cols

    bias_c = bias_scr[...]
    sink = col_scr[:, 0:1]
    bias_n = col_scr[:, 1:2]
    same_group = (hrow // GQA) == (lane // HEAD_DIM)
    low_group = lax.broadcasted_iota(jnp.int32, (ATT_HEADS, HEAD_DIM), 0) < GQA
    rnd = lambda a: a.astype(BF16).astype(F32)
    seqs = range(ATT_S_BB)
    rows = [att_ref[b:b + 1, :] for b in seqs]
    q_bds = []
    for row in rows:
        q = row[:, :ATT_WIDTH] * (HEAD_DIM ** -0.5)
        qh = jnp.concatenate([q[:, h * HEAD_DIM:(h + 1) * HEAD_DIM] for h in range(ATT_HEADS)], axis=0)
        q_bds.append(jnp.where(same_group, jnp.concatenate([qh, qh], axis=1), 0.0))
    s_cs = [_mm_nt(q_bd, ck_ref[b]) + bias_c for b, q_bd in zip(seqs, q_bds)]
    prs, pns = [], []
    for row, q_bd, s_c in zip(rows, q_bds, s_cs):
        kn = row[:, ATT_WIDTH:ATT_WIDTH + KV_WIDTH]
        s_n = jnp.sum(rnd(q_bd) * rnd(kn), axis=-1, keepdims=True) + bias_n
        m = jnp.maximum(jnp.maximum(jnp.max(s_c, axis=-1, keepdims=True), s_n), sink)
        p_c = jnp.exp(s_c - m)
        p_n = jnp.exp(s_n - m)
        den = jnp.sum(p_c, axis=-1, keepdims=True) + p_n + jnp.exp(sink - m)
        prs.append(p_c / den)
        pns.append(p_n / den)
    pvs = [_mm(pr, cv_ref[b]) for b, pr in zip(seqs, prs)]
    for b, row, pv, pn in zip(seqs, rows, pvs, pns):
        vn = row[:, ATT_WIDTH + KV_WIDTH:]
        o_full = pv + rnd(pn) * rnd(vn)
        o_sel = jnp.where(low_group, o_full[:, :HEAD_DIM], o_full[:, HEAD_DIM:])
        o_ref[b:b + 1, :] = jnp.concatenate([o_sel[h:h + 1, :] for h in range(ATT_HEADS)], axis=1)


def _attn_sample(att, ck, cv, bucket, rel_bias, sink):
    nseq = att.shape[0]
    smem = pl.BlockSpec(memory_space=pltpu.SMEM)
    cache = pl.BlockSpec((ATT_S_BB, WINDOW, KV_WIDTH), lambda i: (i, 0, 0))
    return pl.pallas_call(
        _attn_sample_kernel,
        grid=(nseq // ATT_S_BB,),
        in_specs=[pl.BlockSpec((ATT_S_BB, ATT_COLS), lambda i: (i, 0)), cache, cache,
                  pl.BlockSpec(bucket.shape, lambda i: (0, 0)), smem, smem],
        out_specs=pl.BlockSpec((ATT_S_BB, ATT_WIDTH), lambda i: (i, 0)),
        out_shape=jax.ShapeDtypeStruct((nseq, ATT_WIDTH), F32),
        scratch_shapes=[pltpu.VMEM((ATT_HEADS, LANES), F32), pltpu.VMEM((ATT_HEADS, LANES), F32)],
        compiler_params=_params("arbitrary"),
        name="attn_sample",
    )(att, ck, cv, bucket, rel_bias, sink)


GDN_TB = 128
GDN_NC = GDN_TB // DN_CHUNK
TAIL = 8


def _gdn_gates(ba, alog, dtb):
    beta = _sigmoid(ba)
    g = -jnp.exp(alog) * _softplus(ba + dtb)
    return beta, g


PAIR = 2 * DN_DK
N_PAIRS = DN_WIDTH // PAIR


def _pair_diag(x, lo):
    xb = x.astype(BF16)
    zero = jnp.zeros_like(xb)
    return jnp.concatenate([jnp.where(lo, xb, zero), jnp.where(lo, zero, xb)], axis=0)


def _gdn_prompt_kernel(xc_ref, dz_ref, ba_ref, cw_ref, alog_ref, dtb_ref, dnx_ref,
                       hsum_ref, expb_ref, expg_ref, ltri_ref,
                       o_ref, s_out_ref, xp_scr, s_scr):
    i = pl.program_id(0)
    nb = xc_ref.shape[0]

    @pl.when(i == 0)
    def _():
        xp_scr[:, 0:TAIL, :] = jnp.zeros((nb, TAIL, CONV_CH), F32)
        s_scr[...] = jnp.zeros(s_scr.shape, F32)

    hsum = hsum_ref[...]
    ri = lax.broadcasted_iota(jnp.int32, (DN_CHUNK, PAIR), 0)
    ci = lax.broadcasted_iota(jnp.int32, (DN_CHUNK, PAIR), 1)
    lo = ci < DN_DK
    cj = jnp.where(lo, ci, ci - DN_DK)
    causal = ri >= cj
    strict = ri > cj
    eye = (ri == cj).astype(F32)

    def sel2(x, m):
        hi = x.astype(BF16)
        lw = (x - hi.astype(F32)).astype(BF16)
        return (jnp.dot(hi, m, preferred_element_type=F32) + jnp.dot(lw, m, preferred_element_type=F32))

    def head_sums(z):
        hi = z.astype(BF16)
        lw = (z - hi.astype(F32)).astype(BF16)
        d = lambda a, p: jnp.dot(a[:, p * PAIR:(p + 1) * PAIR], hsum, preferred_element_type=F32)
        return jnp.concatenate([d(hi, p) + d(lw, p) for p in range(N_PAIRS)], axis=1)

    ys = []
    for b in range(nb):
        xc = xc_ref[b]
        xp_scr[b, TAIL:, :] = xc
        y = xp_scr[b, TAIL - 3:TAIL - 3 + GDN_TB, :] * cw_ref[0:1, :]
        y = y + xp_scr[b, TAIL - 2:TAIL - 2 + GDN_TB, :] * cw_ref[1:2, :]
        y = y + xp_scr[b, TAIL - 1:TAIL - 1 + GDN_TB, :] * cw_ref[2:3, :]
        y = y + xc * cw_ref[3:4, :]
        xp_scr[b, 0:TAIL, :] = xc[GDN_TB - TAIL:, :]
        ys.append(_silu(y))
    qk_raw = [y[:, j * DN_WIDTH:(j + 1) * DN_WIDTH] for y in ys for j in range(2)]
    inv_norm = lax.rsqrt(head_sums(jnp.concatenate([a * a for a in qk_raw], axis=0)) + EPS)
    pre = []
    for b in range(nb):
        q = qk_raw[2 * b] * inv_norm[2 * b * GDN_TB:(2 * b + 1) * GDN_TB] * (DN_DK ** -0.5)
        k = qk_raw[2 * b + 1] * inv_norm[(2 * b + 1) * GDN_TB:(2 * b + 2) * GDN_TB]
        v = ys[b][:, 2 * DN_WIDTH:]
        beta_c, g_c = _gdn_gates(ba_ref[b], alog_ref[...], dtb_ref[...])
        beta = sel2(beta_c, expb_ref[...])
        gam_c = _mm_sel_lhs(ltri_ref[...], g_c)
        gam = _mm_sel_rhs(gam_c, expg_ref[...])
        gam_t = gam_c.T
        kb = k * beta
        egam = jnp.exp(gam)
        pre.append(dict(q=q, k=k, kb=kb, vb=v * beta, qg=q * egam, wr=kb * egam, gam=gam, gam_t=gam_t))

    probs = [(b, p) for b in range(nb) for p in range(N_PAIRS)]
    pick = lambda m: jnp.where(lo, m[:DN_DK], m[DN_DK:])
    o_rows = [[] for _ in range(nb)]
    for c in range(GDN_NC):
        r0, r1 = c * DN_CHUNK, (c + 1) * DN_CHUNK
        sl = lambda name, b, p: pre[b][name][r0:r1, p * PAIR:(p + 1) * PAIR]
        raws = []
        for b, p in probs:
            k_p = sl("k", b, p)
            k_rows = jnp.concatenate([jnp.where(lo, k_p, 0.0), jnp.where(lo, 0.0, k_p)], axis=0)
            raws.append(_mm_nt(jnp.concatenate([sl("kb", b, p), sl("q", b, p)], axis=0), k_rows))
        pws, ts, qks = [], [], []
        for (b, p), raw in zip(probs, raws):
            gcol = sl("gam", b, p)
            h0 = DN_HEADS + 2 * p
            gam_t = pre[b]["gam_t"]
            grow = jnp.concatenate([gam_t[h0:h0 + 1, r0:r1], gam_t[h0 + 1:h0 + 2, r0:r1]], axis=1)
            decay = jnp.exp(jnp.where(causal, gcol - grow, NEG_INF))
            a = jnp.where(strict, raw[:DN_CHUNK] * decay, 0.0)
            qks.append(jnp.where(causal, raw[DN_CHUNK:] * decay, 0.0))
            pws.append(-a)
            ts.append(eye - a)
        pws = [_mm(pw, _pair_diag(pw, lo)) for pw in pws]
        for _ in range(4):
            rs = [_mm(jnp.concatenate([pw, t], axis=0), _pair_diag(pw, lo)) for pw, t in zip(pws, ts)]
            pws = [r[:DN_CHUNK] for r in rs]
            ts = [t + r[DN_CHUNK:] for t, r in zip(ts, rs)]
        rs = [_mm(t, _pair_diag(pw, lo)) for pw, t in zip(pws, ts)]
        ts = [t + r for t, r in zip(ts, rs)]
        sols = [_mm(t, jnp.concatenate([_pair_diag(sl("vb", b, p), lo), _pair_diag(sl("wr", b, p), lo)],
                                       axis=1)) for (b, p), t in zip(probs, ts)]
        qkuws = [_mm(qk, jnp.concatenate([_pair_diag(s[:, :PAIR], lo), _pair_diag(s[:, PAIR:], lo)], axis=1))
                 for qk, s in zip(qks, sols)]
        crosses, gls = [], []
        for (b, p), s in zip(probs, sols):
            gam_last = pre[b]["gam"][r1 - 1:r1, p * PAIR:(p + 1) * PAIR]
            kd = sl("k", b, p) * jnp.exp(gam_last - sl("gam", b, p))
            crosses.append(_mm_tn(kd, s))
            gls.append(jnp.exp(gam_last))
        lhs = [jnp.concatenate([pick(cr[:, PAIR:]), sl("qg", b, p) - qkuw[:, PAIR:]], axis=0)
               for (b, p), cr, qkuw in zip(probs, crosses, qkuws)]
        s_olds = [s_scr[b, p] for b, p in probs]
        rs = [_mm(l, _pair_diag(s_old, lo)) for l, s_old in zip(lhs, s_olds)]
        o_pairs = [[] for _ in range(nb)]
        for (b, p), r, s_old, gl, cr, qkuw in zip(probs, rs, s_olds, gls, crosses, qkuws):
            s_scr[b, p] = gl * s_old - r[:DN_DK] + pick(cr[:, :PAIR])
            o_pairs[b].append(r[DN_DK:] + qkuw[:, :PAIR])
        for b in range(nb):
            o_rows[b].append(jnp.concatenate(o_pairs[b], axis=1))

    o_all = jnp.concatenate([jnp.concatenate(rows, axis=0) for rows in o_rows], axis=0)
    inv_rms = lax.rsqrt(head_sums(o_all * o_all) * (1.0 / DN_DV) + EPS)
    for b in range(nb):
        rows = slice(b * GDN_TB, (b + 1) * GDN_TB)
        o_ref[b] = o_all[rows] * inv_rms[rows] * dnx_ref[...] * _silu(dz_ref[b])

    @pl.when(i == pl.num_programs(0) - 1)
    def _():
        for b in range(nb):
            for p in range(N_PAIRS):
                s_p = s_scr[b, p]
                s_out_ref[b, 2 * p] = s_p[:, :DN_DV]
                s_out_ref[b, 2 * p + 1] = s_p[:, DN_DV:]


def _gdn_consts():
    lane = np.arange(DN_WIDTH)
    pl_lane = np.arange(PAIR)
    hsum = (pl_lane[:, None] // DN_DV == pl_lane[None, :] // DN_DV)
    src = np.arange(LANES)
    expb = (src[:, None] == lane[None, :] // DN_DV)
    expg = (src[:, None] == DN_HEADS + lane[None, :] // DN_DV)
    tok = np.arange(GDN_TB)
    ltri = np.logical_and(tok[:, None] >= tok[None, :],
                          tok[:, None] // DN_CHUNK == tok[None, :] // DN_CHUNK)
    as_bf16 = lambda m: jnp.asarray(m.astype(np.float32), dtype=BF16)
    return as_bf16(hsum), as_bf16(expb), as_bf16(expg), as_bf16(ltri)


def _gdn_prompt(xc, dz, ba, conv_w, alog, dtb, dnx, batch, seq):
    nt = seq // GDN_TB
    hsum, expb, expg, ltri = _gdn_consts()
    row = lambda n: pl.BlockSpec((batch, GDN_TB, n), lambda i: (0, i, 0))
    full = lambda a: pl.BlockSpec(a.shape, lambda i: (0,) * a.ndim)
    consts = (conv_w, alog, dtb, dnx, hsum, expb, expg, ltri)
    as3d = lambda a: a.reshape(batch, seq, a.shape[-1])
    o, s = pl.pallas_call(
        _gdn_prompt_kernel,
        grid=(nt,),
        in_specs=[row(CONV_CH), row(DN_WIDTH), row(LANES)] + [full(a) for a in consts],
        out_specs=[row(DN_WIDTH),
                   pl.BlockSpec((batch, DN_HEADS, DN_DK, DN_DV), lambda i: (0, 0, 0, 0))],
        out_shape=[jax.ShapeDtypeStruct((batch, seq, DN_WIDTH), F32),
                   jax.ShapeDtypeStruct((batch, DN_HEADS, DN_DK, DN_DV), F32)],
        scratch_shapes=[pltpu.VMEM((batch, TAIL + GDN_TB, CONV_CH), F32),
                        pltpu.VMEM((batch, N_PAIRS, DN_DK, PAIR), F32)],
        compiler_params=_params("arbitrary"),
        name="gdn_prompt",
    )(as3d(xc), as3d(dz), as3d(ba), *consts)
    return o.reshape(batch * seq, DN_WIDTH), s


GDN_S_BB = 8


def _gdn_sample_kernel(xc_ref, dz_ref, ba_ref, sc_ref, s_ref, cw_ref, alog_ref, dtb_ref, dn_ref,
                       hsum_ref, eye_ref, hsel_ref, hrep3_ref, o_ref, s_out_ref):
    xc = xc_ref[...]
    y = sc_ref[0] * cw_ref[0:1, :]
    y = y + sc_ref[1] * cw_ref[1:2, :]
    y = y + sc_ref[2] * cw_ref[2:3, :]
    y = _silu(y + xc * cw_ref[3:4, :])
    hsum = hsum_ref[...]
    q = y[:, :DN_WIDTH]
    k = y[:, DN_WIDTH:2 * DN_WIDTH]
    v = y[:, 2 * DN_WIDTH:]
    q = q * lax.rsqrt(_mm_sel_rhs(q * q, hsum) + EPS) * (DN_DK ** -0.5)
    k = k * lax.rsqrt(_mm_sel_rhs(k * k, hsum) + EPS)
    beta_c, g_c = _gdn_gates(ba_ref[...], alog_ref[...], dtb_ref[...])
    eg_c = jnp.exp(g_c)
    eye = eye_ref[...]
    tr = lambda a: lax.dot_general(a, eye, (((0,), (0,)), ((), ())), precision=lax.Precision.HIGHEST,
                                   preferred_element_type=F32)
    gates_t = tr(jnp.concatenate([beta_c, eg_c], axis=1))
    beta_t = gates_t[:LANES]
    eg_t = gates_t[LANES:]
    dz = dz_ref[...]
    dn = dn_ref[...]
    split = lambda r: jnp.concatenate([r[:, h * DN_DV:(h + 1) * DN_DV] for h in range(DN_HEADS)], axis=0)
    own_head = hsel_ref[...].astype(F32)
    hrep3 = hrep3_ref[...]
    seqs = range(GDN_S_BB)
    dot = lambda a, b: jnp.dot(a.astype(BF16), b.astype(BF16), preferred_element_type=F32)

    def pieces(x):
        p1 = x.astype(BF16).astype(F32)
        r1 = x - p1
        p2 = r1.astype(BF16).astype(F32)
        return p1, p2, (r1 - p2).astype(BF16).astype(F32)

    heads = DN_HEADS
    k_pieces, kqs = [], []
    for b in seqs:
        kq_bd = jnp.concatenate([own_head * k[b:b + 1, :], own_head * q[b:b + 1, :]], axis=0)
        a1, a2, a3 = pieces(kq_bd)
        s1, s2, s3 = pieces(s_ref[b])
        r1 = dot(jnp.concatenate([a1, a2, a3], axis=0), s1)
        r2 = dot(jnp.concatenate([a1, a2], axis=0), s2)
        r3 = dot(a1, s3)
        n = 2 * heads
        kqs.append(((r3 + r2[n:] + r1[2 * n:]) + (r2[:n] + r1[n:2 * n])) + r1[:n])
        k_pieces.append((a1[:heads], a2[:heads], a3[:heads]))
    egs = [eg_t[DN_HEADS:2 * DN_HEADS, b:b + 1] for b in seqs]
    qks = [jnp.sum(split(q[b:b + 1, :]) * split(k[b:b + 1, :]), axis=-1, keepdims=True) for b in seqs]
    v_news = [beta_t[0:DN_HEADS, b:b + 1] * (split(v[b:b + 1, :]) - eg * kq[:heads])
              for b, eg, kq in zip(seqs, egs, kqs)]
    os_ = [eg * kq[heads:] + qk * v_new for eg, kq, qk, v_new in zip(egs, kqs, qks, v_news)]
    inv_rms = [lax.rsqrt(jnp.mean(o * o, axis=-1, keepdims=True) + EPS) for o in os_]
    for b, o, r in zip(seqs, os_, inv_rms):
        o_ref[b] = o * r * dn * _silu(split(dz[b:b + 1, :]))
    outers, egrows = [], []
    for (k1, k2, k3), v_new, eg in zip(k_pieces, v_news, egs):
        v1, v2, v3 = pieces(v_new)
        lhs = jnp.concatenate([k1, k1, k2, k1, k2, k3], axis=0).astype(BF16)
        rhs = jnp.concatenate([v1, v2, v1, v3, v2, v1], axis=0).astype(BF16)
        outers.append(lax.dot_general(lhs, rhs, (((0,), (0,)), ((), ())), preferred_element_type=F32))
        egrows.append(dot(hrep3, jnp.concatenate(pieces(jnp.broadcast_to(eg, (DN_HEADS, DN_DV))), axis=0)))
    for b, outer, egrow in zip(seqs, outers, egrows):
        s_out_ref[b] = s_ref[b] * egrow + outer


def _gdn_sample(xc, dz, ba, sconv_t, state, conv_w, alog, dtb, dn):
    nseq = xc.shape[0]
    lane = np.arange(DN_WIDTH)
    hsum = jnp.asarray((lane[:, None] // DN_DV == lane[None, :] // DN_DV).astype(np.float32), dtype=BF16)
    eye = jnp.eye(GDN_S_BB, dtype=F32)
    hsel_np = (np.arange(DN_HEADS)[:, None] == lane[None, :] // DN_DK).astype(np.float32)
    hsel = jnp.asarray(hsel_np, dtype=BF16)
    hrep3 = jnp.asarray(np.tile(hsel_np.T, (1, 3)), dtype=BF16)
    row = lambda n: pl.BlockSpec((GDN_S_BB, n), lambda i: (i, 0))
    full = lambda a: pl.BlockSpec(a.shape, lambda i: (0,) * a.ndim)
    st = pl.BlockSpec((GDN_S_BB, DN_HEADS * DN_DK, DN_DV), lambda i: (i, 0, 0))
    consts = (conv_w, alog, dtb, dn, hsum, eye, hsel, hrep3)
    return pl.pallas_call(
        _gdn_sample_kernel,
        grid=(nseq // GDN_S_BB,),
        in_specs=[row(CONV_CH), row(DN_WIDTH), row(LANES),
                  pl.BlockSpec((CONV_WIDTH - 1, GDN_S_BB, CONV_CH), lambda i: (0, i, 0)), st]
                 + [full(a) for a in consts],
        out_specs=[pl.BlockSpec((GDN_S_BB, DN_HEADS, DN_DV), lambda i: (i, 0, 0)), st],
        out_shape=[jax.ShapeDtypeStruct((nseq, DN_HEADS, DN_DV), F32),
                   jax.ShapeDtypeStruct(state.shape, F32)],
        compiler_params=_params("parallel"),
        name="gdn_sample",
    )(xc, dz, ba, sconv_t, state, *consts)


def _route(xn, wr):
    logits = jnp.dot(xn, wr, preferred_element_type=F32)
    lane = lax.broadcasted_iota(jnp.int32, logits.shape, 1).astype(F32)
    first_at = lambda hit: jnp.min(jnp.where(hit, lane, float(LANES)), axis=-1, keepdims=True)
    glog = jnp.where(lane < N_GROUPS, logits, NEG_INF)
    gmax = jnp.max(glog, axis=-1, keepdims=True)
    gsel = first_at(glog == gmax)
    pgsel = 1.0 / jnp.sum(jnp.exp(glog - gmax), axis=-1, keepdims=True)
    lo = ROUTER_OFF + gsel * EXPERTS_PER_GROUP
    in_group = jnp.logical_and(lane >= lo, lane < lo + EXPERTS_PER_GROUP)
    elog = jnp.where(in_group, logits, NEG_INF)
    m1 = jnp.max(elog, axis=-1, keepdims=True)
    i1 = first_at(elog == m1)
    z = jnp.sum(jnp.exp(elog - m1), axis=-1, keepdims=True)
    elog2 = jnp.where(lane == i1, NEG_INF, elog)
    m2 = jnp.max(elog2, axis=-1, keepdims=True)
    i2 = first_at(elog2 == m2)
    p1 = 1.0 / z
    p2 = jnp.exp(m2 - m1) / z
    tot = p1 + p2
    return lane, i1, i2, p1 / tot * pgsel, p2 / tot * pgsel


def _outproj(x_ref, oa_ref, od_ref, wo_ref):
    return x_ref[...] + _mm(oa_ref[...], wo_ref[:ATT_WIDTH, :]) + _mm(od_ref[...], wo_ref[ATT_WIDTH:, :])


def _outproj_router_kernel(x_ref, oa_ref, od_ref, wo_ref, g_ref, wr_ref, h_ref, xn_ref, gate_ref):
    h = _outproj(x_ref, oa_ref, od_ref, wo_ref)
    h_ref[...] = h
    xn = _rmsnorm(h, g_ref[...]).astype(BF16)
    xn_ref[...] = xn
    lane, i1, i2, g1, g2 = _route(xn, wr_ref[...])
    gate_ref[...] = jnp.where(lane == i1, g1, 0.0) + jnp.where(lane == i2, g2, 0.0)


def _outproj_router(x, oa, od, wo, g, wr):
    t = x.shape[0]
    tm = min(t, 256)
    row = lambda n: pl.BlockSpec((tm, n), lambda i: (i, 0))
    full = lambda a: pl.BlockSpec(a.shape, lambda i: (0,) * a.ndim)
    return pl.pallas_call(
        _outproj_router_kernel,
        grid=(t // tm,),
        in_specs=[row(D_MODEL), row(ATT_WIDTH), row(DN_WIDTH), full(wo), full(g), full(wr)],
        out_specs=[row(D_MODEL), row(D_MODEL), row(LANES)],
        out_shape=[jax.ShapeDtypeStruct((t, D_MODEL), F32), jax.ShapeDtypeStruct((t, D_MODEL), BF16),
                   jax.ShapeDtypeStruct((t, LANES), F32)],
        compiler_params=_params("parallel"),
        name="outproj_router",
    )(x, oa, od, wo, g, wr)


def _moe_kernel(xn_ref, gate_ref, wg_ref, wu_ref, wd_ref, o_ref):
    e = pl.program_id(1)
    xn = xn_ref[...]
    lane = lax.broadcasted_iota(jnp.int32, gate_ref.shape, 1)
    gate = jnp.sum(jnp.where(lane == e + ROUTER_OFF, gate_ref[...], 0.0), axis=-1, keepdims=True)
    hg = jnp.dot(xn, wg_ref[...].astype(BF16), preferred_element_type=F32)
    hu = jnp.dot(xn, wu_ref[...].astype(BF16), preferred_element_type=F32)
    hm = _silu(hg) * hu * gate
    y = jnp.dot(hm.astype(BF16), wd_ref[...].astype(BF16), preferred_element_type=F32)

    @pl.when(e == 0)
    def _():
        o_ref[...] = y

    @pl.when(e > 0)
    def _():
        o_ref[...] += y


def _moe(xn, gates, wg, wu, wd):
    t = xn.shape[0]
    tm = min(t, 1024)
    return pl.pallas_call(
        _moe_kernel,
        grid=(t // tm, N_EXPERTS),
        in_specs=[pl.BlockSpec((tm, D_MODEL), lambda i, e: (i, 0)),
                  pl.BlockSpec((tm, LANES), lambda i, e: (i, 0)),
                  pl.BlockSpec((None, D_MODEL, D_EXPERT), lambda i, e: (e, 0, 0)),
                  pl.BlockSpec((None, D_MODEL, D_EXPERT), lambda i, e: (e, 0, 0)),
                  pl.BlockSpec((None, D_EXPERT, D_MODEL), lambda i, e: (e, 0, 0))],
        out_specs=pl.BlockSpec((tm, D_MODEL), lambda i, e: (i, 0)),
        out_shape=jax.ShapeDtypeStruct((t, D_MODEL), F32),
        compiler_params=_params("parallel", "arbitrary"),
        name="moe",
    )(xn, gates, wg, wu, wd)


MOE_TM = 512
POS_TM = 1024
INFO_G1, INFO_G2, INFO_E1, INFO_E2 = 0, 1, 2, 3
DMA_UNROLL = 8


def _moe_tiles(t):
    return (2 * t) // MOE_TM + N_EXPERTS


HALF = D_MODEL // 2
U32 = jnp.uint32


def _pack_rows(x):
    bits = lambda v: lax.bitcast_convert_type(v.astype(BF16).astype(F32), U32)
    return bits(x[:, HALF:]) | (bits(x[:, :HALF]) >> 16)


def _unpack_rows(w):
    lo = lax.bitcast_convert_type(w << 16, F32)
    hi = lax.bitcast_convert_type(w & jnp.uint32(0xFFFF0000), F32)
    return lo, hi


def _route_kernel(x_ref, oa_ref, od_ref, wo_ref, g_ref, wr_ref, h_ref, xn_ref, info_ref, cnt_ref, run_scr):
    h = _outproj(x_ref, oa_ref, od_ref, wo_ref)
    h_ref[...] = h
    xn = _rmsnorm(h, g_ref[...])
    xn_ref[...] = _pack_rows(xn)
    lane, i1, i2, g1, g2 = _route(xn.astype(BF16), wr_ref[...])
    info = jnp.where(lane == INFO_G1, g1, 0.0) + jnp.where(lane == INFO_G2, g2, 0.0)
    info = info + jnp.where(lane == INFO_E1, i1, 0.0) + jnp.where(lane == INFO_E2, i2, 0.0)
    info_ref[...] = info

    @pl.when(pl.program_id(0) == 0)
    def _():
        run_scr[...] = jnp.zeros(run_scr.shape, F32)
    picked = jnp.logical_or(lane == i1, lane == i2).astype(F32)
    run_scr[...] += jnp.sum(picked, axis=0, keepdims=True)
    cnt_ref[...] = run_scr[...]


def _route_sparse(x, oa, od, wo, g, wr):
    t = x.shape[0]
    tm = ROW_TM
    row = lambda n: pl.BlockSpec((tm, n), lambda i: (i, 0))
    full = lambda a: pl.BlockSpec(a.shape, lambda i: (0,) * a.ndim)
    return pl.pallas_call(
        _route_kernel,
        grid=(t // tm,),
        in_specs=[row(D_MODEL), row(ATT_WIDTH), row(DN_WIDTH), full(wo), full(g), full(wr)],
        out_specs=[row(D_MODEL), row(HALF), row(LANES), pl.BlockSpec((1, LANES), lambda i: (0, 0))],
        out_shape=[jax.ShapeDtypeStruct((t, D_MODEL), F32), jax.ShapeDtypeStruct((t, HALF), U32),
                   jax.ShapeDtypeStruct((t, LANES), F32), jax.ShapeDtypeStruct((1, LANES), F32)],
        scratch_shapes=[pltpu.VMEM((1, LANES), F32)],
        compiler_params=_params("arbitrary"),
        name="route",
    )(x, oa, od, wo, g, wr)


def _positions_kernel(info_ref, cnt_ref, ltri_ref, utri_ref, pos_ref, run_scr, off_scr):
    info = info_ref[...]
    lane = lax.broadcasted_iota(jnp.int32, info.shape, 1).astype(F32)
    hit1 = lane == info[:, INFO_E1:INFO_E1 + 1]
    hit2 = lane == info[:, INFO_E2:INFO_E2 + 1]
    onehot = jnp.logical_or(hit1, hit2).astype(F32)

    @pl.when(pl.program_id(0) == 0)
    def _():
        tiles = jnp.floor((cnt_ref[...] + (MOE_TM - 1)) * (1.0 / MOE_TM))
        off_scr[...] = MOE_TM * jnp.dot(tiles.astype(BF16), utri_ref[...], preferred_element_type=F32)
        run_scr[...] = jnp.zeros(run_scr.shape, F32)

    before = (jnp.dot(ltri_ref[...], onehot.astype(BF16), preferred_element_type=F32)
              + run_scr[...] + off_scr[...])
    pos1 = jnp.sum(jnp.where(hit1, before, 0.0), axis=-1, keepdims=True)
    pos2 = jnp.sum(jnp.where(hit2, before, 0.0), axis=-1, keepdims=True)
    pos_ref[...] = (jnp.where(lane == 0, pos1, 0.0) + jnp.where(lane == 1, pos2, 0.0)).astype(jnp.int32)
    run_scr[...] += jnp.sum(onehot, axis=0, keepdims=True)


def _positions(info, cnt):
    t = info.shape[0]
    tm = min(t, POS_TM)
    tok = np.arange(tm)
    ltri = jnp.asarray((tok[:, None] > tok[None, :]).astype(np.float32), dtype=BF16)
    ln = np.arange(LANES)
    utri = jnp.asarray((ln[:, None] < ln[None, :]).astype(np.float32), dtype=BF16)
    full = lambda a: pl.BlockSpec(a.shape, lambda i: (0,) * a.ndim)
    return pl.pallas_call(
        _positions_kernel,
        grid=(t // tm,),
        in_specs=[pl.BlockSpec((tm, LANES), lambda i: (i, 0)), full(cnt), full(ltri), full(utri)],
        out_specs=pl.BlockSpec((tm, LANES), lambda i: (i, 0)),
        out_shape=jax.ShapeDtypeStruct((t, LANES), jnp.int32),
        scratch_shapes=[pltpu.VMEM((1, LANES), F32), pltpu.VMEM((1, LANES), F32)],
        compiler_params=_params("arbitrary"),
        name="positions",
    )(info, cnt, ltri, utri)


def _row_copy(src_hbm, src_row, dst_hbm, dst_row, sem):
    return pltpu.make_async_copy(src_hbm.at[pl.ds(src_row, 1)], dst_hbm.at[pl.ds(dst_row, 1)], sem)


SCATTER_SLOTS = 3


def _scatter_kernel(pos1_ref, pos2_ref, last_ref, used_ref, nt_ref, xn_hbm, zero_hbm, xs_hbm,
                    buf, lsem, sem, zsem, *, n_tok):
    max_tiles = xs_hbm.shape[0] // MOE_TM

    def zero_tile(tile):
        return pltpu.make_async_copy(zero_hbm, xs_hbm.at[pl.ds(tile * MOE_TM, MOE_TM)], zsem)

    def for_unused(fn):
        def body(tile, carry):
            fn(tile)
            return carry
        lax.fori_loop(nt_ref[0], max_tiles, body, 0)

    for e in range(N_EXPERTS):
        @pl.when(used_ref[e] > 0)
        def _():
            zero_tile(last_ref[e]).start()
    for_unused(lambda tile: zero_tile(tile).start())
    for e in range(N_EXPERTS):
        @pl.when(used_ref[e] > 0)
        def _():
            zero_tile(last_ref[e]).wait()
    for_unused(lambda tile: zero_tile(tile).wait())

    tm = buf.shape[1]
    n = n_tok // tm

    def load(i):
        return pltpu.make_async_copy(xn_hbm.at[pl.ds(i * tm, tm)], buf.at[i % SCATTER_SLOTS],
                                     lsem.at[i % SCATTER_SLOTS])

    def wait_rows(slot):
        pltpu.make_async_copy(xs_hbm.at[pl.ds(0, 2 * tm)], xs_hbm.at[pl.ds(0, 2 * tm)], sem.at[slot]).wait()

    load(0).start()
    load(1).start()

    def step(i, carry):
        slot = i % SCATTER_SLOTS
        load(i).wait()

        def body(j, c2):
            tok = i * tm + j
            src = buf.at[slot, pl.ds(j, 1)]
            pltpu.make_async_copy(src, xs_hbm.at[pl.ds(pos1_ref[tok], 1)], sem.at[slot]).start()
            pltpu.make_async_copy(src, xs_hbm.at[pl.ds(pos2_ref[tok], 1)], sem.at[slot]).start()
            return c2
        lax.fori_loop(0, tm, body, 0, unroll=DMA_UNROLL)

        @pl.when(i >= 1)
        def _():
            wait_rows((i - 1) % SCATTER_SLOTS)

        @pl.when(i + 2 < n)
        def _():
            load(i + 2).start()
        return carry
    lax.fori_loop(0, n, step, 0)
    wait_rows((n - 1) % SCATTER_SLOTS)


def _scatter_rows(xn, pos1, pos2, last_tile, used, n_tiles, n_rows):
    t = xn.shape[0]
    zero = jnp.zeros((MOE_TM, D_MODEL), F32)
    any_spec = pl.BlockSpec(memory_space=pl.ANY)
    return pl.pallas_call(
        functools.partial(_scatter_kernel, n_tok=t),
        grid_spec=pltpu.PrefetchScalarGridSpec(
            num_scalar_prefetch=5, grid=(1,),
            in_specs=[any_spec, any_spec], out_specs=any_spec,
            scratch_shapes=[pltpu.VMEM((SCATTER_SLOTS, MOE_TM, D_MODEL), F32),
                            pltpu.SemaphoreType.DMA((SCATTER_SLOTS,)),
                            pltpu.SemaphoreType.DMA((SCATTER_SLOTS,)),
                            pltpu.SemaphoreType.DMA]),
        out_shape=jax.ShapeDtypeStruct((n_rows, D_MODEL), F32),
        compiler_params=_params("arbitrary"),
        name="scatter_rows",
    )(pos1, pos2, last_tile, used, n_tiles, xn, zero)


def _experts_kernel(te_ref, tv_ref, nt_ref, xs_ref, wg_ref, wu_ref, wd_ref, ys_ref, wg_s, wu_s, wd_s):
    i = pl.program_id(0)
    used = i < nt_ref[0]

    @pl.when(jnp.logical_or(i == 0, te_ref[i] != te_ref[jnp.maximum(i - 1, 0)]))
    def _():
        wg_s[...] = wg_ref[...].astype(BF16)
        wu_s[...] = wu_ref[...].astype(BF16)
        wd_s[...] = wd_ref[...].astype(BF16)

    @pl.when(used)
    def _():
        row = lax.broadcasted_iota(jnp.int32, xs_ref.shape, 0)
        x_lo, x_hi = _unpack_rows(jnp.where(row < tv_ref[i], xs_ref[...], jnp.uint32(0)))
        x_lo = x_lo.astype(BF16)
        x_hi = x_hi.astype(BF16)
        up = lambda w_s: (jnp.dot(x_lo, w_s[:HALF, :], preferred_element_type=F32)
                          + jnp.dot(x_hi, w_s[HALF:, :], preferred_element_type=F32))
        hm = (_silu(up(wg_s)) * up(wu_s)).astype(BF16)
        ys_ref[...] = _pack_rows(jnp.dot(hm, wd_s[...], preferred_element_type=F32))

    @pl.when(jnp.logical_not(used))
    def _():
        ys_ref[...] = jnp.zeros(ys_ref.shape, U32)


def _experts(xs, tile_expert, tile_valid, n_tiles, wg, wu, wd):
    max_tiles = xs.shape[0] // MOE_TM
    rows = pl.BlockSpec((MOE_TM, HALF), lambda i, te, tv, nt: (i, 0))
    wspec = lambda shape: pl.BlockSpec((None,) + shape, lambda i, te, tv, nt: (te[i], 0, 0))
    return pl.pallas_call(
        _experts_kernel,
        grid_spec=pltpu.PrefetchScalarGridSpec(
            num_scalar_prefetch=3, grid=(max_tiles,),
            in_specs=[rows, wspec((D_MODEL, D_EXPERT)), wspec((D_MODEL, D_EXPERT)),
                      wspec((D_EXPERT, D_MODEL))],
            out_specs=rows,
            scratch_shapes=[pltpu.VMEM((D_MODEL, D_EXPERT), BF16), pltpu.VMEM((D_MODEL, D_EXPERT), BF16),
                            pltpu.VMEM((D_EXPERT, D_MODEL), BF16)]),
        out_shape=jax.ShapeDtypeStruct(xs.shape, U32),
        compiler_params=_params("arbitrary"),
        name="experts",
    )(tile_expert, tile_valid, n_tiles, xs, wg, wu, wd)


def _ple_gather_kernel(pos1_ref, pos2_ref, h_ref, info_ref, p_ref, wpp_ref, wpg_ref, gp_ref, gf_ref,
                       ys_hbm, y_ref, ybuf, sem):
    i = pl.program_id(0)
    n = pl.num_programs(0)
    tm = h_ref.shape[0]

    def issue(tile, slot):
        def body(j, carry):
            tok = tile * tm + j
            pltpu.make_async_copy(ys_hbm.at[pl.ds(pos1_ref[tok], 1)], ybuf.at[slot, 0, pl.ds(j, 1)],
                                  sem.at[slot]).start()
            pltpu.make_async_copy(ys_hbm.at[pl.ds(pos2_ref[tok], 1)], ybuf.at[slot, 1, pl.ds(j, 1)],
                                  sem.at[slot]).start()
            return carry
        lax.fori_loop(0, tm, body, 0, unroll=DMA_UNROLL)

    @pl.when(i == 0)
    def _():
        issue(0, 0)

    @pl.when(i + 1 < n)
    def _():
        issue(i + 1, (i + 1) % 2)

    slot = i % 2
    pltpu.make_async_copy(ybuf.at[slot], ybuf.at[slot], sem.at[slot]).wait()
    info = info_ref[...]
    moe = info[:, INFO_G1:INFO_G1 + 1] * ybuf[slot, 0] + info[:, INFO_G2:INFO_G2 + 1] * ybuf[slot, 1]
    h = h_ref[...] + moe
    hn = _rmsnorm(h, gp_ref[...])
    h = h + _mm(p_ref[...], wpp_ref[...]) * _sigmoid(_mm(hn, wpg_ref[...]))
    y_ref[...] = _rmsnorm(h, gf_ref[...])


def _ple_gather(h, info, p, ys, pos1, pos2, wpp, wpg, gp, gf):
    t = h.shape[0]
    tm = 256
    row = lambda n: pl.BlockSpec((tm, n), lambda i, p1, p2: (i, 0))
    full = lambda a: pl.BlockSpec(a.shape, lambda i, p1, p2: (0,) * a.ndim)
    return pl.pallas_call(
        _ple_gather_kernel,
        grid_spec=pltpu.PrefetchScalarGridSpec(
            num_scalar_prefetch=2, grid=(t // tm,),
            in_specs=[row(D_MODEL), row(LANES), row(PLE_DIM), full(wpp), full(wpg), full(gp), full(gf),
                      pl.BlockSpec(memory_space=pl.ANY)],
            out_specs=row(D_MODEL),
            scratch_shapes=[pltpu.VMEM((2, 2, tm, D_MODEL), F32), pltpu.SemaphoreType.DMA((2,))]),
        out_shape=jax.ShapeDtypeStruct((t, D_MODEL), F32),
        compiler_params=_params("arbitrary"),
        name="ple_gather",
    )(pos1, pos2, h, info, p, wpp, wpg, gp, gf, ys)


SC_IDX = 128
SC_ROWS = 64
SC_WORKERS = 32


def _sc_mesh():
    return plsc.VectorSubcoreMesh(core_axis_name="c", subcore_axis_name="s")


def _sc_windows(t, fn):
    per_worker = t // SC_WORKERS
    worker = lax.axis_index(("c", "s"))

    @pl.loop(0, per_worker // SC_IDX)
    def _(w):
        fn(worker * per_worker + w * SC_IDX)


def _sc_scatter_rows(xn, pos1, pos2, n_rows):
    t, d = xn.shape
    assert t % (SC_WORKERS * SC_IDX) == 0
    idx_t = pltpu.VMEM((1, SC_IDX), jnp.int32)

    @pl.kernel(out_type=jax.ShapeDtypeStruct((n_rows, d), xn.dtype), mesh=_sc_mesh(),
               scratch_types=[idx_t, idx_t, pltpu.VMEM((SC_ROWS, d), xn.dtype)])
    def scatter(x_hbm, p1_hbm, p2_hbm, o_hbm, i1_v, i2_v, buf):
        def window(base):
            pltpu.sync_copy(p1_hbm.at[:, pl.ds(base, SC_IDX)], i1_v)
            pltpu.sync_copy(p2_hbm.at[:, pl.ds(base, SC_IDX)], i2_v)
            for k in range(SC_IDX // SC_ROWS):
                pltpu.sync_copy(x_hbm.at[pl.ds(base + k * SC_ROWS, SC_ROWS)], buf)
                pltpu.sync_copy(buf, o_hbm.at[i1_v.at[0, pl.ds(k * SC_ROWS, SC_ROWS)]])
                pltpu.sync_copy(buf, o_hbm.at[i2_v.at[0, pl.ds(k * SC_ROWS, SC_ROWS)]])
        _sc_windows(t, window)

    return scatter(xn, pos1.reshape(1, t), pos2.reshape(1, t))


def _sc_gather_rows(ys, pos1, pos2):
    t = pos1.shape[0]
    d = ys.shape[1]
    assert t % (SC_WORKERS * SC_IDX) == 0
    idx_t = pltpu.VMEM((1, SC_IDX), jnp.int32)
    out = jax.ShapeDtypeStruct((t, d), ys.dtype)

    @pl.kernel(out_type=(out, out), mesh=_sc_mesh(),
               scratch_types=[idx_t, idx_t, pltpu.VMEM((SC_ROWS, d), ys.dtype)])
    def gather(y_hbm, p1_hbm, p2_hbm, o1_hbm, o2_hbm, i1_v, i2_v, buf):
        def window(base):
            pltpu.sync_copy(p1_hbm.at[:, pl.ds(base, SC_IDX)], i1_v)
            pltpu.sync_copy(p2_hbm.at[:, pl.ds(base, SC_IDX)], i2_v)
            for k in range(SC_IDX // SC_ROWS):
                rows = pl.ds(base + k * SC_ROWS, SC_ROWS)
                pltpu.sync_copy(y_hbm.at[i1_v.at[0, pl.ds(k * SC_ROWS, SC_ROWS)]], buf)
                pltpu.sync_copy(buf, o1_hbm.at[rows])
                pltpu.sync_copy(y_hbm.at[i2_v.at[0, pl.ds(k * SC_ROWS, SC_ROWS)]], buf)
                pltpu.sync_copy(buf, o2_hbm.at[rows])
        _sc_windows(t, window)

    return gather(ys, pos1.reshape(1, t), pos2.reshape(1, t))


def _ple_sparse_kernel(h_ref, info_ref, y1_ref, y2_ref, p_ref, wpp_ref, wpg_ref, gp_ref, gf_ref, y_ref):
    info = info_ref[...]
    g1 = info[:, INFO_G1:INFO_G1 + 1]
    g2 = info[:, INFO_G2:INFO_G2 + 1]
    y1_lo, y1_hi = _unpack_rows(y1_ref[...])
    y2_lo, y2_hi = _unpack_rows(y2_ref[...])
    moe = jnp.concatenate([g1 * y1_lo + g2 * y2_lo, g1 * y1_hi + g2 * y2_hi], axis=1)
    h = h_ref[...] + moe
    hn = _rmsnorm(h, gp_ref[...])
    h = h + _mm(p_ref[...], wpp_ref[...]) * _sigmoid(_mm(hn, wpg_ref[...]))
    y_ref[...] = _rmsnorm(h, gf_ref[...])


def _ple_sparse(h, info, y1, y2, p, wpp, wpg, gp, gf):
    t = h.shape[0]
    tm = ROW_TM
    row = lambda n: pl.BlockSpec((tm, n), lambda i: (i, 0))
    full = lambda a: pl.BlockSpec(a.shape, lambda i: (0,) * a.ndim)
    return pl.pallas_call(
        _ple_sparse_kernel,
        grid=(t // tm,),
        in_specs=[row(D_MODEL), row(LANES), row(HALF), row(HALF), row(PLE_DIM),
                  full(wpp), full(wpg), full(gp), full(gf)],
        out_specs=row(D_MODEL),
        out_shape=jax.ShapeDtypeStruct((t, D_MODEL), F32),
        compiler_params=_params("parallel"),
        name="ple_sparse",
    )(h, info, y1, y2, p, wpp, wpg, gp, gf)


def _tile_tables(cnt, max_tiles):
    tiles_e = (cnt + (MOE_TM - 1)) // MOE_TM
    ends = jnp.cumsum(tiles_e)
    n_tiles = ends[-1]
    tile = jnp.arange(max_tiles, dtype=jnp.int32)
    idx = jnp.minimum(tile, n_tiles - 1)
    tile_expert = jnp.sum((idx[:, None] >= ends[None, :]).astype(jnp.int32), axis=1)
    mine = tile_expert[:, None] == jnp.arange(N_EXPERTS, dtype=jnp.int32)[None, :]
    of_mine = lambda v: jnp.sum(jnp.where(mine, v[None, :], 0), axis=1)
    valid = jnp.clip(of_mine(cnt) - (idx - of_mine(ends - tiles_e)) * MOE_TM, 0, MOE_TM)
    tile_valid = jnp.where(tile < n_tiles, valid, 0).astype(jnp.int32)
    return (tile_expert, tile_valid, n_tiles.reshape(1), (ends - 1).astype(jnp.int32),
            tiles_e.astype(jnp.int32))


def _ple_final_kernel(h_ref, m_ref, p_ref, wpp_ref, wpg_ref, gp_ref, gf_ref, y_ref):
    h = h_ref[...] + m_ref[...]
    hn = _rmsnorm(h, gp_ref[...])
    h = h + _mm(p_ref[...], wpp_ref[...]) * _sigmoid(_mm(hn, wpg_ref[...]))
    y_ref[...] = _rmsnorm(h, gf_ref[...])


def _ple_final(h, m, p, wpp, wpg, gp, gf):
    t = h.shape[0]
    tm = min(t, 256)
    row = lambda n: pl.BlockSpec((tm, n), lambda i: (i, 0))
    full = lambda a: pl.BlockSpec(a.shape, lambda i: (0,) * a.ndim)
    return pl.pallas_call(
        _ple_final_kernel,
        grid=(t // tm,),
        in_specs=[row(D_MODEL), row(D_MODEL), row(PLE_DIM), full(wpp), full(wpg), full(gp), full(gf)],
        out_specs=row(D_MODEL),
        out_shape=jax.ShapeDtypeStruct((t, D_MODEL), F32),
        compiler_params=_params("parallel"),
        name="ple_final",
    )(h, m, p, wpp, wpg, gp, gf)


def kernel(x_prompt, x_sample, p_prompt, p_sample, cache_k, cache_v, state_conv, state_S, rel_bias, norm_mix, w_in, att_sink, conv_w, dn_A_log, dn_dt_bias, dn_norm, w_out, norm_ffn, w_router_group, w_router_expert, w_gate, w_up, w_down, w_ple_proj, w_ple_gate, norm_ple, norm_final):
    batch, seq, _ = x_prompt.shape
    nseq = x_sample.shape[0]
    assert x_sample.shape[1] == 1 and norm_mix.shape[0] == 1 and cache_k.shape[2] == WINDOW
    assert seq % GDN_TB == 0 and seq % ATT_BLOCK == 0

    wi = w_in[0]
    o_db = ATT_COLS + CONV_CH
    w_in_re = jnp.concatenate(
        [wi[:, :o_db], wi[:, o_db + 2 * DN_HEADS:], wi[:, o_db:o_db + 2 * DN_HEADS],
         jnp.zeros((D_MODEL, LANES - 2 * DN_HEADS), F32)], axis=1).astype(BF16)
    row = lambda a: a.reshape(1, -1).astype(F32)
    pad_lanes = lambda a, off: jnp.zeros((1, LANES), F32).at[0, off:off + a.shape[0]].set(a)
    alog = pad_lanes(dn_A_log[0], DN_HEADS)
    dtb = pad_lanes(dn_dt_bias[0], DN_HEADS)
    dnx = jnp.tile(dn_norm[0], DN_HEADS).reshape(1, DN_WIDTH)
    w_router = jnp.concatenate(
        [w_router_group[0], w_router_expert[0],
         jnp.zeros((D_MODEL, LANES - N_GROUPS - N_EXPERTS), F32)], axis=1).astype(BF16)
    wo = w_out[0].astype(BF16)
    wg, wu, wd = w_gate[0], w_up[0], w_down[0]
    wpp, wpg = w_ple_proj[0].astype(BF16), w_ple_gate[0].astype(BF16)
    sink = att_sink[0]

    qi = np.arange(ATT_BLOCK)[:, None]
    kj = np.arange(2 * ATT_BLOCK)[None, :]
    bucket_p = jnp.asarray(_t5_bucket_np(qi + ATT_BLOCK - kj))
    bucket_s = jnp.asarray(_t5_bucket_np(WINDOW - np.arange(WINDOW)[None, :]))

    def tail(x, o_att, o_dn, p):
        h1, xn2, gates = _outproj_router(x, o_att, o_dn, wo, row(norm_ffn[0]), w_router)
        moe = _moe(xn2, gates, wg, wu, wd)
        return _ple_final(h1, moe, p, wpp, wpg, row(norm_ple[0]), row(norm_final))

    xp = x_prompt.reshape(batch * seq, D_MODEL)
    att_p, xc_p, dz_p, ba_p = _inproj(xp, row(norm_mix[0]), w_in_re)
    o_att_p = _attn_prompt(att_p, bucket_p, rel_bias, sink, batch, seq)
    o_dn_p, s_p = _gdn_prompt(xc_p, dz_p, ba_p, conv_w[0], alog, dtb, dnx, batch, seq)
    h1, xn2, info, cnt = _route_sparse(xp, o_att_p, o_dn_p, wo, row(norm_ffn[0]), w_router)
    pos = _positions(info, cnt)
    pos1, pos2 = pos[:, 0], pos[:, 1]
    max_tiles = _moe_tiles(batch * seq)
    cnt_e = cnt[0, ROUTER_OFF:ROUTER_OFF + N_EXPERTS].astype(jnp.int32)
    tile_expert, tile_valid, n_tiles, last_tile, used = _tile_tables(cnt_e, max_tiles)
    xs = _sc_scatter_rows(xn2, pos1, pos2, max_tiles * MOE_TM)
    ys = _experts(xs, tile_expert, tile_valid, n_tiles, wg, wu, wd)
    y1, y2 = _sc_gather_rows(ys, pos1, pos2)
    y_p = _ple_sparse(h1, info, y1, y2, p_prompt[0].reshape(batch * seq, PLE_DIM),
                      wpp, wpg, row(norm_ple[0]), row(norm_final))

    xs = x_sample.reshape(nseq, D_MODEL)
    att_s, xc_s, dz_s, ba_s = _inproj(xs, row(norm_mix[0]), w_in_re)
    ck = cache_k[0].reshape(nseq, WINDOW, KV_WIDTH)
    cv = cache_v[0].reshape(nseq, WINDOW, KV_WIDTH)
    o_att_s = _attn_sample(att_s, ck, cv, bucket_s, rel_bias, sink)
    sconv_t = jnp.swapaxes(state_conv[0], 0, 1)
    o_dn_s, s_s = _gdn_sample(xc_s, dz_s, ba_s, sconv_t,
                              state_S[0].reshape(nseq, DN_HEADS * DN_DK, DN_DV), conv_w[0], alog, dtb,
                              dn_norm[0].reshape(1, DN_DV))
    s_s = s_s.reshape(nseq, DN_HEADS, DN_DK, DN_DV)
    y_s = tail(xs, o_att_s, o_dn_s.reshape(nseq, DN_WIDTH), p_sample[0].reshape(nseq, PLE_DIM))

    att_p3 = att_p.reshape(batch, seq, ATT_COLS)
    kv_shape = (1, batch, WINDOW, ATT_KV_HEADS, HEAD_DIM)
    k_p = att_p3[:, seq - WINDOW:, ATT_WIDTH:ATT_WIDTH + KV_WIDTH].reshape(kv_shape)
    v_p = att_p3[:, seq - WINDOW:, ATT_WIDTH + KV_WIDTH:].reshape(kv_shape)
    conv_p = xc_p.reshape(batch, seq, CONV_CH)[:, seq - (CONV_WIDTH - 1):][None]
    k_new = att_s[:, None, ATT_WIDTH:ATT_WIDTH + KV_WIDTH]
    v_new = att_s[:, None, ATT_WIDTH + KV_WIDTH:]
    kv_s_shape = (1, nseq, WINDOW, ATT_KV_HEADS, HEAD_DIM)
    k_s = jnp.concatenate([ck[:, 1:], k_new], axis=1).reshape(kv_s_shape)
    v_s = jnp.concatenate([cv[:, 1:], v_new], axis=1).reshape(kv_s_shape)
    conv_s = jnp.concatenate([state_conv[0][:, 1:], xc_s[:, None, :]], axis=1)[None]
    return (y_p.reshape(batch, seq, D_MODEL), y_s.reshape(nseq, 1, D_MODEL),
            k_p, v_p, conv_p, s_p[None], k_s, v_s, conv_s, s_s[None])
```

```python
import functools
import math

import numpy as np
import jax
import jax.numpy as jnp
from jax import lax
from jax.experimental import pallas as pl
from jax.experimental.pallas import tpu as pltpu
from jax.experimental.pallas import tpu_sc as plsc

F32 = jnp.float32
BF16 = jnp.bfloat16

D_MODEL = 1024
ATT_HEADS = 8
ATT_KV_HEADS = 2
HEAD_DIM = 64
GQA = ATT_HEADS // ATT_KV_HEADS
WINDOW = 128
ATT_BLOCK = 128
N_BUCKETS = 32
DN_HEADS = 8
DN_DK = 64
DN_DV = 64
CONV_WIDTH = 4
DN_CHUNK = 64
ATT_WIDTH = ATT_HEADS * HEAD_DIM
KV_WIDTH = ATT_KV_HEADS * HEAD_DIM
DN_WIDTH = DN_HEADS * DN_DV
CONV_CH = 3 * DN_WIDTH
N_GROUPS = 4
EXPERTS_PER_GROUP = 8
N_EXPERTS = N_GROUPS * EXPERTS_PER_GROUP
D_EXPERT = 256
PLE_DIM = 256
EPS = 1e-6
NEG_INF = float("-inf")

ATT_COLS = ATT_WIDTH + 2 * KV_WIDTH
LANES = 128
IN_COLS = ATT_COLS + CONV_CH + DN_WIDTH + LANES
ROUTER_OFF = N_GROUPS
VMEM_LIMIT = 48 * 1024 * 1024
ROW_TM = 512


def _params(*sem):
    return pltpu.CompilerParams(dimension_semantics=sem, vmem_limit_bytes=VMEM_LIMIT)


def _mm(a, b):
    return jnp.dot(a.astype(BF16), b.astype(BF16), preferred_element_type=F32)


def _mm_nt(a, b):
    return lax.dot_general(a.astype(BF16), b.astype(BF16), (((1,), (1,)), ((), ())),
                           preferred_element_type=F32)


def _mm_tn(a, b):
    return lax.dot_general(a.astype(BF16), b.astype(BF16), (((0,), (0,)), ((), ())),
                           preferred_element_type=F32)


def _split3(x):
    h1 = x.astype(BF16)
    r1 = x - h1.astype(F32)
    h2 = r1.astype(BF16)
    h3 = (r1 - h2.astype(F32)).astype(BF16)
    return h1, h2, h3


def _mm_sel_rhs(x, sel):
    h1, h2, h3 = _split3(x)
    d = lambda h: jnp.dot(h, sel, preferred_element_type=F32)
    return d(h1) + d(h2) + d(h3)


def _mm_sel_lhs(sel, x):
    h1, h2, h3 = _split3(x)
    d = lambda h: jnp.dot(sel, h, preferred_element_type=F32)
    return d(h1) + d(h2) + d(h3)


def _mm3(a, b):
    ah = a.astype(BF16)
    al = (a - ah.astype(F32)).astype(BF16)
    bh = b.astype(BF16)
    bl = (b - bh.astype(F32)).astype(BF16)
    d = lambda u, v: jnp.dot(u, v, preferred_element_type=F32)
    return d(ah, bh) + d(ah, bl) + d(al, bh)


def _sigmoid(x):
    return 1.0 / (1.0 + jnp.exp(-x))


def _silu(x):
    return x * _sigmoid(x)


def _softplus(x):
    return jnp.maximum(x, 0.0) + jnp.log1p(jnp.exp(-jnp.abs(x)))


def _rmsnorm(x, g):
    return x * lax.rsqrt(jnp.mean(x * x, axis=-1, keepdims=True) + EPS) * g


def _t5_bucket_np(dist):
    max_exact = N_BUCKETS // 2
    d = np.maximum(dist, 0)
    ratio = (np.log(np.maximum(d, 1).astype(np.float32) / np.float32(max_exact))
             / np.float32(math.log(WINDOW / max_exact))).astype(np.float32)
    large = np.minimum(max_exact + (ratio * np.float32(N_BUCKETS - max_exact)).astype(np.int32),
                       N_BUCKETS - 1)
    return np.where(d < max_exact, d, large).astype(np.int32)


def _bias_lookup(bucket, rb_ref, h):
    acc = jnp.zeros(bucket.shape, F32)
    for t in range(N_BUCKETS):
        acc = jnp.where(bucket == t, rb_ref[t, h], acc)
    return acc


def _inproj_kernel(x_ref, g_ref, w_ref, att_ref, xc_ref, dz_ref, ba_ref):
    xn = _rmsnorm(x_ref[...], g_ref[...]).astype(BF16)
    o0, o1, o2 = ATT_COLS, ATT_COLS + CONV_CH, ATT_COLS + CONV_CH + DN_WIDTH
    att_ref[...] = jnp.dot(xn, w_ref[:, :o0], preferred_element_type=F32)
    xc_ref[...] = jnp.dot(xn, w_ref[:, o0:o1], preferred_element_type=F32)
    dz_ref[...] = jnp.dot(xn, w_ref[:, o1:o2], preferred_element_type=F32)
    ba_ref[...] = jnp.dot(xn, w_ref[:, o2:], preferred_element_type=F32)


def _inproj(x, g, w):
    t = x.shape[0]
    tm = min(t, ROW_TM)
    row = lambda n: pl.BlockSpec((tm, n), lambda i: (i, 0))
    full = lambda a: pl.BlockSpec(a.shape, lambda i: (0,) * a.ndim)
    return pl.pallas_call(
        _inproj_kernel,
        grid=(t // tm,),
        in_specs=[row(D_MODEL), full(g), full(w)],
        out_specs=[row(ATT_COLS), row(CONV_CH), row(DN_WIDTH), row(LANES)],
        out_shape=[jax.ShapeDtypeStruct((t, n), F32) for n in (ATT_COLS, CONV_CH, DN_WIDTH, LANES)],
        compiler_params=_params("parallel"),
        name="inproj",
    )(x, g, w)


TAIL = 8
PAIR = 2 * DN_DK
N_PAIRS = DN_WIDTH // PAIR


def _head_sums(z, pair_ones):
    hi = z.astype(BF16)
    lw = (z - hi.astype(F32)).astype(BF16)
    d = lambda a, p: jnp.dot(a[:, p * PAIR:(p + 1) * PAIR], pair_ones, preferred_element_type=F32)
    return jnp.concatenate([d(hi, p) + d(lw, p) for p in range(N_PAIRS)], axis=1)


def _inproj_conv_kernel(x_ref, g_ref, w_ref, cw_ref, ones_ref, att_ref, qkv_ref, dz_ref, ba_ref, tail_ref,
                        xp_scr, *, tiles_per_seq):
    tm = x_ref.shape[0]

    @pl.when(pl.program_id(0) % tiles_per_seq == 0)
    def _():
        xp_scr[0:TAIL, :] = jnp.zeros((TAIL, CONV_CH), F32)

    xn = _rmsnorm(x_ref[...], g_ref[...]).astype(BF16)
    o0, o1, o2 = ATT_COLS, ATT_COLS + CONV_CH, ATT_COLS + CONV_CH + DN_WIDTH
    xc = jnp.dot(xn, w_ref[:, o0:o1], preferred_element_type=F32)
    att_ref[...] = jnp.dot(xn, w_ref[:, :o0], preferred_element_type=F32)
    dz_ref[...] = jnp.dot(xn, w_ref[:, o1:o2], preferred_element_type=F32)
    ba_ref[...] = jnp.dot(xn, w_ref[:, o2:], preferred_element_type=F32)

    xp_scr[TAIL:, :] = xc
    y = xp_scr[TAIL - 3:TAIL - 3 + tm, :] * cw_ref[0:1, :]
    y = y + xp_scr[TAIL - 2:TAIL - 2 + tm, :] * cw_ref[1:2, :]
    y = y + xp_scr[TAIL - 1:TAIL - 1 + tm, :] * cw_ref[2:3, :]
    y = y + xc * cw_ref[3:4, :]
    tail = xc[tm - TAIL:, :]
    xp_scr[0:TAIL, :] = tail
    tail_ref[0] = tail
    y = _silu(y)
    q = y[:, :DN_WIDTH]
    k = y[:, DN_WIDTH:2 * DN_WIDTH]
    inv_norm = lax.rsqrt(_head_sums(jnp.concatenate([q * q, k * k], axis=0), ones_ref[...]) + EPS)
    qkv_ref[:, :DN_WIDTH] = q * inv_norm[:tm] * (DN_DK ** -0.5)
    qkv_ref[:, DN_WIDTH:2 * DN_WIDTH] = k * inv_norm[tm:]
    qkv_ref[:, 2 * DN_WIDTH:] = y[:, 2 * DN_WIDTH:]


def _pair_ones():
    lane = np.arange(PAIR)
    return jnp.asarray((lane[:, None] // DN_DV == lane[None, :] // DN_DV).astype(np.float32), dtype=BF16)


def _inproj_conv(x, g, w, conv_w, seq):
    t = x.shape[0]
    tm = ROW_TM
    assert seq % tm == 0
    ones = _pair_ones()
    row = lambda n: pl.BlockSpec((tm, n), lambda i: (i, 0))
    full = lambda a: pl.BlockSpec(a.shape, lambda i: (0,) * a.ndim)
    return pl.pallas_call(
        functools.partial(_inproj_conv_kernel, tiles_per_seq=seq // tm),
        grid=(t // tm,),
        in_specs=[row(D_MODEL), full(g), full(w), full(conv_w), full(ones)],
        out_specs=[row(ATT_COLS), row(CONV_CH), row(DN_WIDTH), row(LANES),
                   pl.BlockSpec((1, TAIL, CONV_CH), lambda i: (i, 0, 0))],
        out_shape=[jax.ShapeDtypeStruct((t, n), F32) for n in (ATT_COLS, CONV_CH, DN_WIDTH, LANES)]
                  + [jax.ShapeDtypeStruct((t // tm, TAIL, CONV_CH), F32)],
        scratch_shapes=[pltpu.VMEM((TAIL + tm, CONV_CH), F32)],
        compiler_params=_params("arbitrary"),
        name="inproj_conv",
    )(x, g, w, conv_w, ones)


GROUP_ROWS = GQA * ATT_BLOCK


def _attn_prompt_kernel(cur_ref, prev_ref, bucket_ref, rb_ref, sink_ref, o_ref, bias_scr, sink_scr):
    i = pl.program_id(0)
    nseq = cur_ref.shape[0]

    @pl.when(i == 0)
    def _():
        qi = lax.broadcasted_iota(jnp.int32, (ATT_BLOCK, 2 * ATT_BLOCK), 0)
        kj = lax.broadcasted_iota(jnp.int32, (ATT_BLOCK, 2 * ATT_BLOCK), 1)
        dist = qi + ATT_BLOCK - kj
        band = jnp.logical_and(dist >= 0, dist < WINDOW)
        bucket = bucket_ref[...]
        hrow = lax.broadcasted_iota(jnp.int32, (GROUP_ROWS, 1), 0) // ATT_BLOCK
        for g in range(ATT_KV_HEADS):
            sink_col = jnp.zeros((GROUP_ROWS, 1), F32)
            for hh in range(GQA):
                h = g * GQA + hh
                bias = jnp.where(band, _bias_lookup(bucket, rb_ref, h), NEG_INF)
                bias_scr[0, g, hh * ATT_BLOCK:(hh + 1) * ATT_BLOCK, :] = bias
                bias_scr[1, g, hh * ATT_BLOCK:(hh + 1) * ATT_BLOCK, :] = jnp.where(kj >= ATT_BLOCK, bias, NEG_INF)
                sink_col = jnp.where(hrow == hh, sink_ref[h], sink_col)
            sink_scr[g] = sink_col

    first = (i == 0).astype(jnp.int32)
    probs = [(b, g) for b in range(nseq) for g in range(ATT_KV_HEADS)]
    scores = []
    for b, g in probs:
        cur = cur_ref[b]
        prev = prev_ref[b]
        q = jnp.concatenate([cur[:, (g * GQA + hh) * HEAD_DIM:(g * GQA + hh + 1) * HEAD_DIM]
                             for hh in range(GQA)], axis=0) * (HEAD_DIM ** -0.5)
        kcol = slice(ATT_WIDTH + g * HEAD_DIM, ATT_WIDTH + (g + 1) * HEAD_DIM)
        k2 = jnp.concatenate([prev[:, kcol], cur[:, kcol]], axis=0)
        scores.append(_mm_nt(q, k2) + bias_scr[first, g])
    probs_p, dens = [], []
    for (b, g), s in zip(probs, scores):
        sink = sink_scr[g]
        m = jnp.maximum(jnp.max(s, axis=-1, keepdims=True), sink)
        p = jnp.exp(s - m)
        dens.append(jnp.sum(p, axis=-1, keepdims=True) + jnp.exp(sink - m))
        probs_p.append(p)
    outs = {}
    for (b, g), p, den in zip(probs, probs_p, dens):
        vcol = slice(ATT_WIDTH + KV_WIDTH + g * HEAD_DIM, ATT_WIDTH + KV_WIDTH + (g + 1) * HEAD_DIM)
        v2 = jnp.concatenate([prev_ref[b][:, vcol], cur_ref[b][:, vcol]], axis=0)
        outs[b, g] = _mm(p, v2) / den
    for b in range(nseq):
        o_ref[b] = jnp.concatenate([outs[b, g][hh * ATT_BLOCK:(hh + 1) * ATT_BLOCK, :]
                                    for g in range(ATT_KV_HEADS) for hh in range(GQA)], axis=1)


def _attn_prompt(att, bucket, rel_bias, sink, batch, seq):
    nb = seq // ATT_BLOCK
    smem = pl.BlockSpec(memory_space=pltpu.SMEM)
    att3 = att.reshape(batch, seq, ATT_COLS)
    out = pl.pallas_call(
        _attn_prompt_kernel,
        grid=(nb,),
        in_specs=[
            pl.BlockSpec((batch, ATT_BLOCK, ATT_COLS), lambda i: (0, i, 0)),
            pl.BlockSpec((batch, ATT_BLOCK, ATT_COLS), lambda i: (0, jnp.maximum(i - 1, 0), 0)),
            pl.BlockSpec(bucket.shape, lambda i: (0, 0)),
            smem, smem,
        ],
        out_specs=pl.BlockSpec((batch, ATT_BLOCK, ATT_WIDTH), lambda i: (0, i, 0)),
        out_shape=jax.ShapeDtypeStruct((batch, seq, ATT_WIDTH), F32),
        scratch_shapes=[pltpu.VMEM((2, ATT_KV_HEADS, GROUP_ROWS, 2 * ATT_BLOCK), F32),
                        pltpu.VMEM((ATT_KV_HEADS, GROUP_ROWS, 1), F32)],
        compiler_params=_params("arbitrary"),
        name="attn_prompt",
    )(att3, att3, bucket, rel_bias, sink)
    return out.reshape(batch * seq, ATT_WIDTH)


ATT_S_BB = 8


def _attn_sample_kernel(att_ref, ck_ref, cv_ref, bucket_ref, rb_ref, sink_ref, o_ref,
                        bias_scr, col_scr):
    hrow = lax.broadcasted_iota(jnp.int32, (ATT_HEADS, LANES), 0)
    lane = lax.broadcasted_iota(jnp.int32, (ATT_HEADS, LANES), 1)

    @pl.when(pl.program_id(0) == 0)
    def _():
        bucket = jnp.broadcast_to(bucket_ref[...], (ATT_HEADS, LANES))
        bias = jnp.zeros((ATT_HEADS, LANES), F32)
        cols = jnp.zeros((ATT_HEADS, LANES), F32)
        for h in range(ATT_HEADS):
            bias = jnp.where(hrow == h, _bias_lookup(bucket, rb_ref, h), bias)
            cols = jnp.where(jnp.logical_and(hrow == h, lane == 0), sink_ref[h], cols)
            cols = jnp.where(jnp.logical_and(hrow == h, lane == 1), rb_ref[0, h], cols)
        bias_scr[...] = jnp.where(lane >= 1, bias, NEG_INF)
        col_scr[...] = cols

    bias_c = bias_scr[...]
    sink = col_scr[:, 0:1]
    bias_n = col_scr[:, 1:2]
    same_group = (hrow // GQA) == (lane // HEAD_DIM)
    low_group = lax.broadcasted_iota(jnp.int32, (ATT_HEADS, HEAD_DIM), 0) < GQA
    rnd = lambda a: a.astype(BF16).astype(F32)
    seqs = range(ATT_S_BB)
    rows = [att_ref[b:b + 1, :] for b in seqs]
    q_bds = []
    for row in rows:
        q = row[:, :ATT_WIDTH] * (HEAD_DIM ** -0.5)
        qh = jnp.concatenate([q[:, h * HEAD_DIM:(h + 1) * HEAD_DIM] for h in range(ATT_HEADS)], axis=0)
        q_bds.append(jnp.where(same_group, jnp.concatenate([qh, qh], axis=1), 0.0))
    s_cs = [_mm_nt(q_bd, ck_ref[b]) + bias_c for b, q_bd in zip(seqs, q_bds)]
    prs, pns = [], []
    for row, q_bd, s_c in zip(rows, q_bds, s_cs):
        kn = row[:, ATT_WIDTH:ATT_WIDTH + KV_WIDTH]
        s_n = jnp.sum(rnd(q_bd) * rnd(kn), axis=-1, keepdims=True) + bias_n
        m = jnp.maximum(jnp.maximum(jnp.max(s_c, axis=-1, keepdims=True), s_n), sink)
        p_c = jnp.exp(s_c - m)
        p_n = jnp.exp(s_n - m)
        den = jnp.sum(p_c, axis=-1, keepdims=True) + p_n + jnp.exp(sink - m)
        prs.append(p_c / den)
        pns.append(p_n / den)
    pvs = [_mm(pr, cv_ref[b]) for b, pr in zip(seqs, prs)]
    for b, row, pv, pn in zip(seqs, rows, pvs, pns):
        vn = row[:, ATT_WIDTH + KV_WIDTH:]
        o_full = pv + rnd(pn) * rnd(vn)
        o_sel = jnp.where(low_group, o_full[:, :HEAD_DIM], o_full[:, HEAD_DIM:])
        o_ref[b:b + 1, :] = jnp.concatenate([o_sel[h:h + 1, :] for h in range(ATT_HEADS)], axis=1)


def _attn_sample(att, ck, cv, bucket, rel_bias, sink):
    nseq = att.shape[0]
    smem = pl.BlockSpec(memory_space=pltpu.SMEM)
    cache = pl.BlockSpec((ATT_S_BB, WINDOW, KV_WIDTH), lambda i: (i, 0, 0))
    return pl.pallas_call(
        _attn_sample_kernel,
        grid=(nseq // ATT_S_BB,),
        in_specs=[pl.BlockSpec((ATT_S_BB, ATT_COLS), lambda i: (i, 0)), cache, cache,
                  pl.BlockSpec(bucket.shape, lambda i: (0, 0)), smem, smem],
        out_specs=pl.BlockSpec((ATT_S_BB, ATT_WIDTH), lambda i: (i, 0)),
        out_shape=jax.ShapeDtypeStruct((nseq, ATT_WIDTH), F32),
        scratch_shapes=[pltpu.VMEM((ATT_HEADS, LANES), F32), pltpu.VMEM((ATT_HEADS, LANES), F32)],
        compiler_params=_params("arbitrary"),
        name="attn_sample",
    )(att, ck, cv, bucket, rel_bias, sink)


GDN_TB = 128
GDN_NC = GDN_TB // DN_CHUNK


def _gdn_gates(ba, alog, dtb):
    beta = _sigmoid(ba)
    g = -jnp.exp(alog) * _softplus(ba + dtb)
    return beta, g


def _pair_diag(x, lo):
    xb = x.astype(BF16)
    zero = jnp.zeros_like(xb)
    return jnp.concatenate([jnp.where(lo, xb, zero), jnp.where(lo, zero, xb)], axis=0)


def _gdn_prompt_kernel(qkv_ref, dz_ref, ba_ref, alog_ref, dtb_ref, dnx_ref,
                       hsum_ref, expb_ref, expg_ref, ltri_ref,
                       o_ref, s_out_ref, s_scr):
    i = pl.program_id(0)
    nb = qkv_ref.shape[0]

    @pl.when(i == 0)
    def _():
        s_scr[...] = jnp.zeros(s_scr.shape, F32)

    hsum = hsum_ref[...]
    ri = lax.broadcasted_iota(jnp.int32, (DN_CHUNK, PAIR), 0)
    ci = lax.broadcasted_iota(jnp.int32, (DN_CHUNK, PAIR), 1)
    lo = ci < DN_DK
    cj = jnp.where(lo, ci, ci - DN_DK)
    causal = ri >= cj
    strict = ri > cj
    eye = (ri == cj).astype(F32)

    def sel2(x, m):
        hi = x.astype(BF16)
        lw = (x - hi.astype(F32)).astype(BF16)
        return (jnp.dot(hi, m, preferred_element_type=F32) + jnp.dot(lw, m, preferred_element_type=F32))

    pre = []
    for b in range(nb):
        q = qkv_ref[b, :, :DN_WIDTH]
        k = qkv_ref[b, :, DN_WIDTH:2 * DN_WIDTH]
        v = qkv_ref[b, :, 2 * DN_WIDTH:]
        beta_c, g_c = _gdn_gates(ba_ref[b], alog_ref[...], dtb_ref[...])
        beta = sel2(beta_c, expb_ref[...])
        gam_c = _mm_sel_lhs(ltri_ref[...], g_c)
        gam = _mm_sel_rhs(gam_c, expg_ref[...])
        gam_t = gam_c.T
        kb = k * beta
        egam = jnp.exp(gam)
        pre.append(dict(q=q, k=k, kb=kb, vb=v * beta, qg=q * egam, wr=kb * egam, gam=gam, gam_t=gam_t))

    probs = [(b, p) for b in range(nb) for p in range(N_PAIRS)]
    pick = lambda m: jnp.where(lo, m[:DN_DK], m[DN_DK:])
    o_rows = [[] for _ in range(nb)]
    for c in range(GDN_NC):
        r0, r1 = c * DN_CHUNK, (c + 1) * DN_CHUNK
        sl = lambda name, b, p: pre[b][name][r0:r1, p * PAIR:(p + 1) * PAIR]
        raws = []
        for b, p in probs:
            k_p = sl("k", b, p)
            k_rows = jnp.concatenate([jnp.where(lo, k_p, 0.0), jnp.where(lo, 0.0, k_p)], axis=0)
            raws.append(_mm_nt(jnp.concatenate([sl("kb", b, p), sl("q", b, p)], axis=0), k_rows))
        pws, ts, qks = [], [], []
        for (b, p), raw in zip(probs, raws):
            gcol = sl("gam", b, p)
            h0 = DN_HEADS + 2 * p
            gam_t = pre[b]["gam_t"]
            grow = jnp.concatenate([gam_t[h0:h0 + 1, r0:r1], gam_t[h0 + 1:h0 + 2, r0:r1]], axis=1)
            decay = jnp.exp(jnp.where(causal, gcol - grow, NEG_INF))
            a = jnp.where(strict, raw[:DN_CHUNK] * decay, 0.0)
            qks.append(jnp.where(causal, raw[DN_CHUNK:] * decay, 0.0))
            pws.append(-a)
            ts.append(eye - a)
        pws = [_mm(pw, _pair_diag(pw, lo)) for pw in pws]
        for _ in range(4):
            rs = [_mm(jnp.concatenate([pw, t], axis=0), _pair_diag(pw, lo)) for pw, t in zip(pws, ts)]
            pws = [r[:DN_CHUNK] for r in rs]
            ts = [t + r[DN_CHUNK:] for t, r in zip(ts, rs)]
        rs = [_mm(t, _pair_diag(pw, lo)) for pw, t in zip(pws, ts)]
        ts = [t + r for t, r in zip(ts, rs)]
        sols = [_mm(t, jnp.concatenate([_pair_diag(sl("vb", b, p), lo), _pair_diag(sl("wr", b, p), lo)],
                                       axis=1)) for (b, p), t in zip(probs, ts)]
        qkuws = [_mm(qk, jnp.concatenate([_pair_diag(s[:, :PAIR], lo), _pair_diag(s[:, PAIR:], lo)], axis=1))
                 for qk, s in zip(qks, sols)]
        crosses, gls = [], []
        for (b, p), s in zip(probs, sols):
            gam_last = pre[b]["gam"][r1 - 1:r1, p * PAIR:(p + 1) * PAIR]
            kd = sl("k", b, p) * jnp.exp(gam_last - sl("gam", b, p))
            crosses.append(_mm_tn(kd, s))
            gls.append(jnp.exp(gam_last))
        lhs = [jnp.concatenate([pick(cr[:, PAIR:]), sl("qg", b, p) - qkuw[:, PAIR:]], axis=0)
               for (b, p), cr, qkuw in zip(probs, crosses, qkuws)]
        s_olds = [s_scr[b, p] for b, p in probs]
        rs = [_mm(l, _pair_diag(s_old, lo)) for l, s_old in zip(lhs, s_olds)]
        o_pairs = [[] for _ in range(nb)]
        for (b, p), r, s_old, gl, cr, qkuw in zip(probs, rs, s_olds, gls, crosses, qkuws):
            s_scr[b, p] = gl * s_old - r[:DN_DK] + pick(cr[:, :PAIR])
            o_pairs[b].append(r[DN_DK:] + qkuw[:, :PAIR])
        for b in range(nb):
            o_rows[b].append(jnp.concatenate(o_pairs[b], axis=1))

    o_all = jnp.concatenate([jnp.concatenate(rows, axis=0) for rows in o_rows], axis=0)
    inv_rms = lax.rsqrt(_head_sums(o_all * o_all, hsum) * (1.0 / DN_DV) + EPS)
    for b in range(nb):
        rows = slice(b * GDN_TB, (b + 1) * GDN_TB)
        o_ref[b] = o_all[rows] * inv_rms[rows] * dnx_ref[...] * _silu(dz_ref[b])

    @pl.when(i == pl.num_programs(0) - 1)
    def _():
        for b in range(nb):
            for p in range(N_PAIRS):
                s_p = s_scr[b, p]
                s_out_ref[b, 2 * p] = s_p[:, :DN_DV]
                s_out_ref[b, 2 * p + 1] = s_p[:, DN_DV:]


def _gdn_consts():
    lane = np.arange(DN_WIDTH)
    pl_lane = np.arange(PAIR)
    hsum = (pl_lane[:, None] // DN_DV == pl_lane[None, :] // DN_DV)
    src = np.arange(LANES)
    expb = (src[:, None] == lane[None, :] // DN_DV)
    expg = (src[:, None] == DN_HEADS + lane[None, :] // DN_DV)
    tok = np.arange(GDN_TB)
    ltri = np.logical_and(tok[:, None] >= tok[None, :],
                          tok[:, None] // DN_CHUNK == tok[None, :] // DN_CHUNK)
    as_bf16 = lambda m: jnp.asarray(m.astype(np.float32), dtype=BF16)
    return as_bf16(hsum), as_bf16(expb), as_bf16(expg), as_bf16(ltri)


def _gdn_prompt(xc, dz, ba, alog, dtb, dnx, batch, seq):
    nt = seq // GDN_TB
    hsum, expb, expg, ltri = _gdn_consts()
    row = lambda n: pl.BlockSpec((batch, GDN_TB, n), lambda i: (0, i, 0))
    full = lambda a: pl.BlockSpec(a.shape, lambda i: (0,) * a.ndim)
    consts = (alog, dtb, dnx, hsum, expb, expg, ltri)
    as3d = lambda a: a.reshape(batch, seq, a.shape[-1])
    o, s = pl.pallas_call(
        _gdn_prompt_kernel,
        grid=(nt,),
        in_specs=[row(CONV_CH), row(DN_WIDTH), row(LANES)] + [full(a) for a in consts],
        out_specs=[row(DN_WIDTH),
                   pl.BlockSpec((batch, DN_HEADS, DN_DK, DN_DV), lambda i: (0, 0, 0, 0))],
        out_shape=[jax.ShapeDtypeStruct((batch, seq, DN_WIDTH), F32),
                   jax.ShapeDtypeStruct((batch, DN_HEADS, DN_DK, DN_DV), F32)],
        scratch_shapes=[pltpu.VMEM((batch, N_PAIRS, DN_DK, PAIR), F32)],
        compiler_params=_params("arbitrary"),
        name="gdn_prompt",
    )(as3d(xc), as3d(dz), as3d(ba), *consts)
    return o.reshape(batch * seq, DN_WIDTH), s


GDN_S_BB = 8


def _gdn_sample_kernel(xc_ref, dz_ref, ba_ref, sc_ref, s_ref, cw_ref, alog_ref, dtb_ref, dn_ref,
                       hsum_ref, eye_ref, hsel_ref, hrep3_ref, o_ref, s_out_ref):
    xc = xc_ref[...]
    y = sc_ref[0] * cw_ref[0:1, :]
    y = y + sc_ref[1] * cw_ref[1:2, :]
    y = y + sc_ref[2] * cw_ref[2:3, :]
    y = _silu(y + xc * cw_ref[3:4, :])
    hsum = hsum_ref[...]
    q = y[:, :DN_WIDTH]
    k = y[:, DN_WIDTH:2 * DN_WIDTH]
    v = y[:, 2 * DN_WIDTH:]
    q = q * lax.rsqrt(_mm_sel_rhs(q * q, hsum) + EPS) * (DN_DK ** -0.5)
    k = k * lax.rsqrt(_mm_sel_rhs(k * k, hsum) + EPS)
    beta_c, g_c = _gdn_gates(ba_ref[...], alog_ref[...], dtb_ref[...])
    eg_c = jnp.exp(g_c)
    eye = eye_ref[...]
    tr = lambda a: lax.dot_general(a, eye, (((0,), (0,)), ((), ())), precision=lax.Precision.HIGHEST,
                                   preferred_element_type=F32)
    gates_t = tr(jnp.concatenate([beta_c, eg_c], axis=1))
    beta_t = gates_t[:LANES]
    eg_t = gates_t[LANES:]
    dz = dz_ref[...]
    dn = dn_ref[...]
    split = lambda r: jnp.concatenate([r[:, h * DN_DV:(h + 1) * DN_DV] for h in range(DN_HEADS)], axis=0)
    own_head = hsel_ref[...].astype(F32)
    hrep3 = hrep3_ref[...]
    seqs = range(GDN_S_BB)
    dot = lambda a, b: jnp.dot(a.astype(BF16), b.astype(BF16), preferred_element_type=F32)

    def pieces(x):
        p1 = x.astype(BF16).astype(F32)
        r1 = x - p1
        p2 = r1.astype(BF16).astype(F32)
        return p1, p2, (r1 - p2).astype(BF16).astype(F32)

    heads = DN_HEADS
    k_pieces, kqs = [], []
    for b in seqs:
        kq_bd = jnp.concatenate([own_head * k[b:b + 1, :], own_head * q[b:b + 1, :]], axis=0)
        a1, a2, a3 = pieces(kq_bd)
        s1, s2, s3 = pieces(s_ref[b])
        r1 = dot(jnp.concatenate([a1, a2, a3], axis=0), s1)
        r2 = dot(jnp.concatenate([a1, a2], axis=0), s2)
        r3 = dot(a1, s3)
        n = 2 * heads
        kqs.append(((r3 + r2[n:] + r1[2 * n:]) + (r2[:n] + r1[n:2 * n])) + r1[:n])
        k_pieces.append((a1[:heads], a2[:heads], a3[:heads]))
    egs = [eg_t[DN_HEADS:2 * DN_HEADS, b:b + 1] for b in seqs]
    qks = [jnp.sum(split(q[b:b + 1, :]) * split(k[b:b + 1, :]), axis=-1, keepdims=True) for b in seqs]
    v_news = [beta_t[0:DN_HEADS, b:b + 1] * (split(v[b:b + 1, :]) - eg * kq[:heads])
              for b, eg, kq in zip(seqs, egs, kqs)]
    os_ = [eg * kq[heads:] + qk * v_new for eg, kq, qk, v_new in zip(egs, kqs, qks, v_news)]
    inv_rms = [lax.rsqrt(jnp.mean(o * o, axis=-1, keepdims=True) + EPS) for o in os_]
    for b, o, r in zip(seqs, os_, inv_rms):
        o_ref[b] = o * r * dn * _silu(split(dz[b:b + 1, :]))
    outers, egrows = [], []
    for (k1, k2, k3), v_new, eg in zip(k_pieces, v_news, egs):
        v1, v2, v3 = pieces(v_new)
        lhs = jnp.concatenate([k1, k1, k2, k1, k2, k3], axis=0).astype(BF16)
        rhs = jnp.concatenate([v1, v2, v1, v3, v2, v1], axis=0).astype(BF16)
        outers.append(lax.dot_general(lhs, rhs, (((0,), (0,)), ((), ())), preferred_element_type=F32))
        egrows.append(dot(hrep3, jnp.concatenate(pieces(jnp.broadcast_to(eg, (DN_HEADS, DN_DV))), axis=0)))
    for b, outer, egrow in zip(seqs, outers, egrows):
        s_out_ref[b] = s_ref[b] * egrow + outer


def _gdn_sample(xc, dz, ba, sconv_t, state, conv_w, alog, dtb, dn):
    nseq = xc.shape[0]
    lane = np.arange(DN_WIDTH)
    hsum = jnp.asarray((lane[:, None] // DN_DV == lane[None, :] // DN_DV).astype(np.float32), dtype=BF16)
    eye = jnp.eye(GDN_S_BB, dtype=F32)
    hsel_np = (np.arange(DN_HEADS)[:, None] == lane[None, :] // DN_DK).astype(np.float32)
    hsel = jnp.asarray(hsel_np, dtype=BF16)
    hrep3 = jnp.asarray(np.tile(hsel_np.T, (1, 3)), dtype=BF16)
    row = lambda n: pl.BlockSpec((GDN_S_BB, n), lambda i: (i, 0))
    full = lambda a: pl.BlockSpec(a.shape, lambda i: (0,) * a.ndim)
    st = pl.BlockSpec((GDN_S_BB, DN_HEADS * DN_DK, DN_DV), lambda i: (i, 0, 0))
    consts = (conv_w, alog, dtb, dn, hsum, eye, hsel, hrep3)
    return pl.pallas_call(
        _gdn_sample_kernel,
        grid=(nseq // GDN_S_BB,),
        in_specs=[row(CONV_CH), row(DN_WIDTH), row(LANES),
                  pl.BlockSpec((CONV_WIDTH - 1, GDN_S_BB, CONV_CH), lambda i: (0, i, 0)), st]
                 + [full(a) for a in consts],
        out_specs=[pl.BlockSpec((GDN_S_BB, DN_HEADS, DN_DV), lambda i: (i, 0, 0)), st],
        out_shape=[jax.ShapeDtypeStruct((nseq, DN_HEADS, DN_DV), F32),
                   jax.ShapeDtypeStruct(state.shape, F32)],
        compiler_params=_params("parallel"),
        name="gdn_sample",
    )(xc, dz, ba, sconv_t, state, *consts)


def _route(xn, wr):
    logits = jnp.dot(xn, wr, preferred_element_type=F32)
    lane = lax.broadcasted_iota(jnp.int32, logits.shape, 1).astype(F32)
    first_at = lambda hit: jnp.min(jnp.where(hit, lane, float(LANES)), axis=-1, keepdims=True)
    glog = jnp.where(lane < N_GROUPS, logits, NEG_INF)
    gmax = jnp.max(glog, axis=-1, keepdims=True)
    gsel = first_at(glog == gmax)
    pgsel = 1.0 / jnp.sum(jnp.exp(glog - gmax), axis=-1, keepdims=True)
    lo = ROUTER_OFF + gsel * EXPERTS_PER_GROUP
    in_group = jnp.logical_and(lane >= lo, lane < lo + EXPERTS_PER_GROUP)
    elog = jnp.where(in_group, logits, NEG_INF)
    m1 = jnp.max(elog, axis=-1, keepdims=True)
    i1 = first_at(elog == m1)
    z = jnp.sum(jnp.exp(elog - m1), axis=-1, keepdims=True)
    elog2 = jnp.where(lane == i1, NEG_INF, elog)
    m2 = jnp.max(elog2, axis=-1, keepdims=True)
    i2 = first_at(elog2 == m2)
    p1 = 1.0 / z
    p2 = jnp.exp(m2 - m1) / z
    tot = p1 + p2
    return lane, i1, i2, p1 / tot * pgsel, p2 / tot * pgsel


def _outproj(x_ref, oa_ref, od_ref, wo_ref):
    return x_ref[...] + _mm(oa_ref[...], wo_ref[:ATT_WIDTH, :]) + _mm(od_ref[...], wo_ref[ATT_WIDTH:, :])


def _outproj_router_kernel(x_ref, oa_ref, od_ref, wo_ref, g_ref, wr_ref, h_ref, xn_ref, gate_ref):
    h = _outproj(x_ref, oa_ref, od_ref, wo_ref)
    h_ref[...] = h
    xn = _rmsnorm(h, g_ref[...]).astype(BF16)
    xn_ref[...] = xn
    lane, i1, i2, g1, g2 = _route(xn, wr_ref[...])
    gate_ref[...] = jnp.where(lane == i1, g1, 0.0) + jnp.where(lane == i2, g2, 0.0)


def _outproj_router(x, oa, od, wo, g, wr):
    t = x.shape[0]
    tm = min(t, 256)
    row = lambda n: pl.BlockSpec((tm, n), lambda i: (i, 0))
    full = lambda a: pl.BlockSpec(a.shape, lambda i: (0,) * a.ndim)
    return pl.pallas_call(
        _outproj_router_kernel,
        grid=(t // tm,),
        in_specs=[row(D_MODEL), row(ATT_WIDTH), row(DN_WIDTH), full(wo), full(g), full(wr)],
        out_specs=[row(D_MODEL), row(D_MODEL), row(LANES)],
        out_shape=[jax.ShapeDtypeStruct((t, D_MODEL), F32), jax.ShapeDtypeStruct((t, D_MODEL), BF16),
                   jax.ShapeDtypeStruct((t, LANES), F32)],
        compiler_params=_params("parallel"),
        name="outproj_router",
    )(x, oa, od, wo, g, wr)


def _moe_kernel(xn_ref, gate_ref, wg_ref, wu_ref, wd_ref, o_ref):
    e = pl.program_id(1)
    xn = xn_ref[...]
    lane = lax.broadcasted_iota(jnp.int32, gate_ref.shape, 1)
    gate = jnp.sum(jnp.where(lane == e + ROUTER_OFF, gate_ref[...], 0.0), axis=-1, keepdims=True)
    hg = jnp.dot(xn, wg_ref[...].astype(BF16), preferred_element_type=F32)
    hu = jnp.dot(xn, wu_ref[...].astype(BF16), preferred_element_type=F32)
    hm = _silu(hg) * hu * gate
    y = jnp.dot(hm.astype(BF16), wd_ref[...].astype(BF16), preferred_element_type=F32)

    @pl.when(e == 0)
    def _():
        o_ref[...] = y

    @pl.when(e > 0)
    def _():
        o_ref[...] += y


def _moe(xn, gates, wg, wu, wd):
    t = xn.shape[0]
    tm = min(t, 1024)
    return pl.pallas_call(
        _moe_kernel,
        grid=(t // tm, N_EXPERTS),
        in_specs=[pl.BlockSpec((tm, D_MODEL), lambda i, e: (i, 0)),
                  pl.BlockSpec((tm, LANES), lambda i, e: (i, 0)),
                  pl.BlockSpec((None, D_MODEL, D_EXPERT), lambda i, e: (e, 0, 0)),
                  pl.BlockSpec((None, D_MODEL, D_EXPERT), lambda i, e: (e, 0, 0)),
                  pl.BlockSpec((None, D_EXPERT, D_MODEL), lambda i, e: (e, 0, 0))],
        out_specs=pl.BlockSpec((tm, D_MODEL), lambda i, e: (i, 0)),
        out_shape=jax.ShapeDtypeStruct((t, D_MODEL), F32),
        compiler_params=_params("parallel", "arbitrary"),
        name="moe",
    )(xn, gates, wg, wu, wd)


MOE_TM = 512
POS_TM = 1024
INFO_G1, INFO_G2, INFO_E1, INFO_E2 = 0, 1, 2, 3
DMA_UNROLL = 8


def _moe_tiles(t):
    return (2 * t) // MOE_TM + N_EXPERTS


HALF = D_MODEL // 2
U32 = jnp.uint32


def _pack_rows(x):
    bits = lambda v: lax.bitcast_convert_type(v.astype(BF16).astype(F32), U32)
    return bits(x[:, HALF:]) | (bits(x[:, :HALF]) >> 16)


def _unpack_rows(w):
    lo = lax.bitcast_convert_type(w << 16, F32)
    hi = lax.bitcast_convert_type(w & jnp.uint32(0xFFFF0000), F32)
    return lo, hi


def _route_kernel(x_ref, oa_ref, od_ref, wo_ref, g_ref, wr_ref, h_ref, xn_ref, info_ref, cnt_ref, run_scr):
    h = _outproj(x_ref, oa_ref, od_ref, wo_ref)
    h_ref[...] = h
    xn = _rmsnorm(h, g_ref[...])
    xn_ref[...] = _pack_rows(xn)
    lane, i1, i2, g1, g2 = _route(xn.astype(BF16), wr_ref[...])
    info = jnp.where(lane == INFO_G1, g1, 0.0) + jnp.where(lane == INFO_G2, g2, 0.0)
    info = info + jnp.where(lane == INFO_E1, i1, 0.0) + jnp.where(lane == INFO_E2, i2, 0.0)
    info_ref[...] = info

    @pl.when(pl.program_id(0) == 0)
    def _():
        run_scr[...] = jnp.zeros(run_scr.shape, F32)
    picked = jnp.logical_or(lane == i1, lane == i2).astype(F32)
    run_scr[...] += jnp.sum(picked, axis=0, keepdims=True)
    cnt_ref[...] = run_scr[...]


def _route_sparse(x, oa, od, wo, g, wr):
    t = x.shape[0]
    tm = ROW_TM
    row = lambda n: pl.BlockSpec((tm, n), lambda i: (i, 0))
    full = lambda a: pl.BlockSpec(a.shape, lambda i: (0,) * a.ndim)
    return pl.pallas_call(
        _route_kernel,
        grid=(t // tm,),
        in_specs=[row(D_MODEL), row(ATT_WIDTH), row(DN_WIDTH), full(wo), full(g), full(wr)],
        out_specs=[row(D_MODEL), row(HALF), row(LANES), pl.BlockSpec((1, LANES), lambda i: (0, 0))],
        out_shape=[jax.ShapeDtypeStruct((t, D_MODEL), F32), jax.ShapeDtypeStruct((t, HALF), U32),
                   jax.ShapeDtypeStruct((t, LANES), F32), jax.ShapeDtypeStruct((1, LANES), F32)],
        scratch_shapes=[pltpu.VMEM((1, LANES), F32)],
        compiler_params=_params("arbitrary"),
        name="route",
    )(x, oa, od, wo, g, wr)


def _positions_kernel(info_ref, cnt_ref, ltri_ref, utri_ref, pos_ref, run_scr, off_scr):
    info = info_ref[...]
    lane = lax.broadcasted_iota(jnp.int32, info.shape, 1).astype(F32)
    hit1 = lane == info[:, INFO_E1:INFO_E1 + 1]
    hit2 = lane == info[:, INFO_E2:INFO_E2 + 1]
    onehot = jnp.logical_or(hit1, hit2).astype(F32)

    @pl.when(pl.program_id(0) == 0)
    def _():
        tiles = jnp.floor((cnt_ref[...] + (MOE_TM - 1)) * (1.0 / MOE_TM))
        off_scr[...] = MOE_TM * jnp.dot(tiles.astype(BF16), utri_ref[...], preferred_element_type=F32)
        run_scr[...] = jnp.zeros(run_scr.shape, F32)

    before = (jnp.dot(ltri_ref[...], onehot.astype(BF16), preferred_element_type=F32)
              + run_scr[...] + off_scr[...])
    pos1 = jnp.sum(jnp.where(hit1, before, 0.0), axis=-1, keepdims=True)
    pos2 = jnp.sum(jnp.where(hit2, before, 0.0), axis=-1, keepdims=True)
    pos_ref[...] = (jnp.where(lane == 0, pos1, 0.0) + jnp.where(lane == 1, pos2, 0.0)).astype(jnp.int32)
    run_scr[...] += jnp.sum(onehot, axis=0, keepdims=True)


def _positions(info, cnt):
    t = info.shape[0]
    tm = min(t, POS_TM)
    tok = np.arange(tm)
    ltri = jnp.asarray((tok[:, None] > tok[None, :]).astype(np.float32), dtype=BF16)
    ln = np.arange(LANES)
    utri = jnp.asarray((ln[:, None] < ln[None, :]).astype(np.float32), dtype=BF16)
    full = lambda a: pl.BlockSpec(a.shape, lambda i: (0,) * a.ndim)
    return pl.pallas_call(
        _positions_kernel,
        grid=(t // tm,),
        in_specs=[pl.BlockSpec((tm, LANES), lambda i: (i, 0)), full(cnt), full(ltri), full(utri)],
        out_specs=pl.BlockSpec((tm, LANES), lambda i: (i, 0)),
        out_shape=jax.ShapeDtypeStruct((t, LANES), jnp.int32),
        scratch_shapes=[pltpu.VMEM((1, LANES), F32), pltpu.VMEM((1, LANES), F32)],
        compiler_params=_params("arbitrary"),
        name="positions",
    )(info, cnt, ltri, utri)


def _row_copy(src_hbm, src_row, dst_hbm, dst_row, sem):
    return pltpu.make_async_copy(src_hbm.at[pl.ds(src_row, 1)], dst_hbm.at[pl.ds(dst_row, 1)], sem)


SCATTER_SLOTS = 3


def _scatter_kernel(pos1_ref, pos2_ref, last_ref, used_ref, nt_ref, xn_hbm, zero_hbm, xs_hbm,
                    buf, lsem, sem, zsem, *, n_tok):
    max_tiles = xs_hbm.shape[0] // MOE_TM

    def zero_tile(tile):
        return pltpu.make_async_copy(zero_hbm, xs_hbm.at[pl.ds(tile * MOE_TM, MOE_TM)], zsem)

    def for_unused(fn):
        def body(tile, carry):
            fn(tile)
            return carry
        lax.fori_loop(nt_ref[0], max_tiles, body, 0)

    for e in range(N_EXPERTS):
        @pl.when(used_ref[e] > 0)
        def _():
            zero_tile(last_ref[e]).start()
    for_unused(lambda tile: zero_tile(tile).start())
    for e in range(N_EXPERTS):
        @pl.when(used_ref[e] > 0)
        def _():
            zero_tile(last_ref[e]).wait()
    for_unused(lambda tile: zero_tile(tile).wait())

    tm = buf.shape[1]
    n = n_tok // tm

    def load(i):
        return pltpu.make_async_copy(xn_hbm.at[pl.ds(i * tm, tm)], buf.at[i % SCATTER_SLOTS],
                                     lsem.at[i % SCATTER_SLOTS])

    def wait_rows(slot):
        pltpu.make_async_copy(xs_hbm.at[pl.ds(0, 2 * tm)], xs_hbm.at[pl.ds(0, 2 * tm)], sem.at[slot]).wait()

    load(0).start()
    load(1).start()

    def step(i, carry):
        slot = i % SCATTER_SLOTS
        load(i).wait()

        def body(j, c2):
            tok = i * tm + j
            src = buf.at[slot, pl.ds(j, 1)]
            pltpu.make_async_copy(src, xs_hbm.at[pl.ds(pos1_ref[tok], 1)], sem.at[slot]).start()
            pltpu.make_async_copy(src, xs_hbm.at[pl.ds(pos2_ref[tok], 1)], sem.at[slot]).start()
            return c2
        lax.fori_loop(0, tm, body, 0, unroll=DMA_UNROLL)

        @pl.when(i >= 1)
        def _():
            wait_rows((i - 1) % SCATTER_SLOTS)

        @pl.when(i + 2 < n)
        def _():
            load(i + 2).start()
        return carry
    lax.fori_loop(0, n, step, 0)
    wait_rows((n - 1) % SCATTER_SLOTS)


def _scatter_rows(xn, pos1, pos2, last_tile, used, n_tiles, n_rows):
    t = xn.shape[0]
    zero = jnp.zeros((MOE_TM, D_MODEL), F32)
    any_spec = pl.BlockSpec(memory_space=pl.ANY)
    return pl.pallas_call(
        functools.partial(_scatter_kernel, n_tok=t),
        grid_spec=pltpu.PrefetchScalarGridSpec(
            num_scalar_prefetch=5, grid=(1,),
            in_specs=[any_spec, any_spec], out_specs=any_spec,
            scratch_shapes=[pltpu.VMEM((SCATTER_SLOTS, MOE_TM, D_MODEL), F32),
                            pltpu.SemaphoreType.DMA((SCATTER_SLOTS,)),
                            pltpu.SemaphoreType.DMA((SCATTER_SLOTS,)),
                            pltpu.SemaphoreType.DMA]),
        out_shape=jax.ShapeDtypeStruct((n_rows, D_MODEL), F32),
        compiler_params=_params("arbitrary"),
        name="scatter_rows",
    )(pos1, pos2, last_tile, used, n_tiles, xn, zero)


def _experts_kernel(te_ref, tv_ref, nt_ref, xs_ref, wg_ref, wu_ref, wd_ref, ys_ref, wg_s, wu_s, wd_s):
    i = pl.program_id(0)
    used = i < nt_ref[0]

    @pl.when(jnp.logical_or(i == 0, te_ref[i] != te_ref[jnp.maximum(i - 1, 0)]))
    def _():
        wg_s[...] = wg_ref[...].astype(BF16)
        wu_s[...] = wu_ref[...].astype(BF16)
        wd_s[...] = wd_ref[...].astype(BF16)

    @pl.when(used)
    def _():
        row = lax.broadcasted_iota(jnp.int32, xs_ref.shape, 0)
        x_lo, x_hi = _unpack_rows(jnp.where(row < tv_ref[i], xs_ref[...], jnp.uint32(0)))
        x_lo = x_lo.astype(BF16)
        x_hi = x_hi.astype(BF16)
        up = lambda w_s: (jnp.dot(x_lo, w_s[:HALF, :], preferred_element_type=F32)
                          + jnp.dot(x_hi, w_s[HALF:, :], preferred_element_type=F32))
        hm = (_silu(up(wg_s)) * up(wu_s)).astype(BF16)
        ys_ref[...] = _pack_rows(jnp.dot(hm, wd_s[...], preferred_element_type=F32))

    @pl.when(jnp.logical_not(used))
    def _():
        ys_ref[...] = jnp.zeros(ys_ref.shape, U32)


def _experts(xs, tile_expert, tile_valid, n_tiles, wg, wu, wd):
    max_tiles = xs.shape[0] // MOE_TM
    rows = pl.BlockSpec((MOE_TM, HALF), lambda i, te, tv, nt: (i, 0))
    wspec = lambda shape: pl.BlockSpec((None,) + shape, lambda i, te, tv, nt: (te[i], 0, 0))
    return pl.pallas_call(
        _experts_kernel,
        grid_spec=pltpu.PrefetchScalarGridSpec(
            num_scalar_prefetch=3, grid=(max_tiles,),
            in_specs=[rows, wspec((D_MODEL, D_EXPERT)), wspec((D_MODEL, D_EXPERT)),
                      wspec((D_EXPERT, D_MODEL))],
            out_specs=rows,
            scratch_shapes=[pltpu.VMEM((D_MODEL, D_EXPERT), BF16), pltpu.VMEM((D_MODEL, D_EXPERT), BF16),
                            pltpu.VMEM((D_EXPERT, D_MODEL), BF16)]),
        out_shape=jax.ShapeDtypeStruct(xs.shape, U32),
        compiler_params=_params("arbitrary"),
        name="experts",
    )(tile_expert, tile_valid, n_tiles, xs, wg, wu, wd)


def _ple_gather_kernel(pos1_ref, pos2_ref, h_ref, info_ref, p_ref, wpp_ref, wpg_ref, gp_ref, gf_ref,
                       ys_hbm, y_ref, ybuf, sem):
    i = pl.program_id(0)
    n = pl.num_programs(0)
    tm = h_ref.shape[0]

    def issue(tile, slot):
        def body(j, carry):
            tok = tile * tm + j
            pltpu.make_async_copy(ys_hbm.at[pl.ds(pos1_ref[tok], 1)], ybuf.at[slot, 0, pl.ds(j, 1)],
                                  sem.at[slot]).start()
            pltpu.make_async_copy(ys_hbm.at[pl.ds(pos2_ref[tok], 1)], ybuf.at[slot, 1, pl.ds(j, 1)],
                                  sem.at[slot]).start()
            return carry
        lax.fori_loop(0, tm, body, 0, unroll=DMA_UNROLL)

    @pl.when(i == 0)
    def _():
        issue(0, 0)

    @pl.when(i + 1 < n)
    def _():
        issue(i + 1, (i + 1) % 2)

    slot = i % 2
    pltpu.make_async_copy(ybuf.at[slot], ybuf.at[slot], sem.at[slot]).wait()
    info = info_ref[...]
    moe = info[:, INFO_G1:INFO_G1 + 1] * ybuf[slot, 0] + info[:, INFO_G2:INFO_G2 + 1] * ybuf[slot, 1]
    h = h_ref[...] + moe
    hn = _rmsnorm(h, gp_ref[...])
    h = h + _mm(p_ref[...], wpp_ref[...]) * _sigmoid(_mm(hn, wpg_ref[...]))
    y_ref[...] = _rmsnorm(h, gf_ref[...])


def _ple_gather(h, info, p, ys, pos1, pos2, wpp, wpg, gp, gf):
    t = h.shape[0]
    tm = 256
    row = lambda n: pl.BlockSpec((tm, n), lambda i, p1, p2: (i, 0))
    full = lambda a: pl.BlockSpec(a.shape, lambda i, p1, p2: (0,) * a.ndim)
    return pl.pallas_call(
        _ple_gather_kernel,
        grid_spec=pltpu.PrefetchScalarGridSpec(
            num_scalar_prefetch=2, grid=(t // tm,),
            in_specs=[row(D_MODEL), row(LANES), row(PLE_DIM), full(wpp), full(wpg), full(gp), full(gf),
                      pl.BlockSpec(memory_space=pl.ANY)],
            out_specs=row(D_MODEL),
            scratch_shapes=[pltpu.VMEM((2, 2, tm, D_MODEL), F32), pltpu.SemaphoreType.DMA((2,))]),
        out_shape=jax.ShapeDtypeStruct((t, D_MODEL), F32),
        compiler_params=_params("arbitrary"),
        name="ple_gather",
    )(pos1, pos2, h, info, p, wpp, wpg, gp, gf, ys)


SC_IDX = 128
SC_ROWS = 64
SC_WORKERS = 32


def _sc_mesh():
    return plsc.VectorSubcoreMesh(core_axis_name="c", subcore_axis_name="s")


def _sc_windows(t, fn):
    per_worker = t // SC_WORKERS
    worker = lax.axis_index(("c", "s"))

    @pl.loop(0, per_worker // SC_IDX)
    def _(w):
        fn(worker * per_worker + w * SC_IDX)


def _sc_scatter_rows(xn, pos1, pos2, n_rows):
    t, d = xn.shape
    assert t % (SC_WORKERS * SC_IDX) == 0
    idx_t = pltpu.VMEM((1, SC_IDX), jnp.int32)

    @pl.kernel(out_type=jax.ShapeDtypeStruct((n_rows, d), xn.dtype), mesh=_sc_mesh(),
               scratch_types=[idx_t, idx_t, pltpu.VMEM((SC_ROWS, d), xn.dtype)])
    def scatter(x_hbm, p1_hbm, p2_hbm, o_hbm, i1_v, i2_v, buf):
        def window(base):
            pltpu.sync_copy(p1_hbm.at[:, pl.ds(base, SC_IDX)], i1_v)
            pltpu.sync_copy(p2_hbm.at[:, pl.ds(base, SC_IDX)], i2_v)
            for k in range(SC_IDX // SC_ROWS):
                pltpu.sync_copy(x_hbm.at[pl.ds(base + k * SC_ROWS, SC_ROWS)], buf)
                pltpu.sync_copy(buf, o_hbm.at[i1_v.at[0, pl.ds(k * SC_ROWS, SC_ROWS)]])
                pltpu.sync_copy(buf, o_hbm.at[i2_v.at[0, pl.ds(k * SC_ROWS, SC_ROWS)]])
        _sc_windows(t, window)

    return scatter(xn, pos1.reshape(1, t), pos2.reshape(1, t))


def _sc_gather_rows(ys, pos1, pos2):
    t = pos1.shape[0]
    d = ys.shape[1]
    assert t % (SC_WORKERS * SC_IDX) == 0
    idx_t = pltpu.VMEM((1, SC_IDX), jnp.int32)
    out = jax.ShapeDtypeStruct((t, d), ys.dtype)

    @pl.kernel(out_type=(out, out), mesh=_sc_mesh(),
               scratch_types=[idx_t, idx_t, pltpu.VMEM((SC_ROWS, d), ys.dtype)])
    def gather(y_hbm, p1_hbm, p2_hbm, o1_hbm, o2_hbm, i1_v, i2_v, buf):
        def window(base):
            pltpu.sync_copy(p1_hbm.at[:, pl.ds(base, SC_IDX)], i1_v)
            pltpu.sync_copy(p2_hbm.at[:, pl.ds(base, SC_IDX)], i2_v)
            for k in range(SC_IDX // SC_ROWS):
                rows = pl.ds(base + k * SC_ROWS, SC_ROWS)
                pltpu.sync_copy(y_hbm.at[i1_v.at[0, pl.ds(k * SC_ROWS, SC_ROWS)]], buf)
                pltpu.sync_copy(buf, o1_hbm.at[rows])
                pltpu.sync_copy(y_hbm.at[i2_v.at[0, pl.ds(k * SC_ROWS, SC_ROWS)]], buf)
                pltpu.sync_copy(buf, o2_hbm.at[rows])
        _sc_windows(t, window)

    return gather(ys, pos1.reshape(1, t), pos2.reshape(1, t))


def _ple_sparse_kernel(h_ref, info_ref, y1_ref, y2_ref, p_ref, wpp_ref, wpg_ref, gp_ref, gf_ref, y_ref):
    info = info_ref[...]
    g1 = info[:, INFO_G1:INFO_G1 + 1]
    g2 = info[:, INFO_G2:INFO_G2 + 1]
    y1_lo, y1_hi = _unpack_rows(y1_ref[...])
    y2_lo, y2_hi = _unpack_rows(y2_ref[...])
    moe = jnp.concatenate([g1 * y1_lo + g2 * y2_lo, g1 * y1_hi + g2 * y2_hi], axis=1)
    h = h_ref[...] + moe
    hn = _rmsnorm(h, gp_ref[...])
    h = h + _mm(p_ref[...], wpp_ref[...]) * _sigmoid(_mm(hn, wpg_ref[...]))
    y_ref[...] = _rmsnorm(h, gf_ref[...])


def _ple_sparse(h, info, y1, y2, p, wpp, wpg, gp, gf):
    t = h.shape[0]
    tm = ROW_TM
    row = lambda n: pl.BlockSpec((tm, n), lambda i: (i, 0))
    full = lambda a: pl.BlockSpec(a.shape, lambda i: (0,) * a.ndim)
    return pl.pallas_call(
        _ple_sparse_kernel,
        grid=(t // tm,),
        in_specs=[row(D_MODEL), row(LANES), row(HALF), row(HALF), row(PLE_DIM),
                  full(wpp), full(wpg), full(gp), full(gf)],
        out_specs=row(D_MODEL),
        out_shape=jax.ShapeDtypeStruct((t, D_MODEL), F32),
        compiler_params=_params("parallel"),
        name="ple_sparse",
    )(h, info, y1, y2, p, wpp, wpg, gp, gf)


def _tile_tables(cnt, max_tiles):
    tiles_e = (cnt + (MOE_TM - 1)) // MOE_TM
    ends = jnp.cumsum(tiles_e)
    n_tiles = ends[-1]
    tile = jnp.arange(max_tiles, dtype=jnp.int32)
    idx = jnp.minimum(tile, n_tiles - 1)
    tile_expert = jnp.sum((idx[:, None] >= ends[None, :]).astype(jnp.int32), axis=1)
    mine = tile_expert[:, None] == jnp.arange(N_EXPERTS, dtype=jnp.int32)[None, :]
    of_mine = lambda v: jnp.sum(jnp.where(mine, v[None, :], 0), axis=1)
    valid = jnp.clip(of_mine(cnt) - (idx - of_mine(ends - tiles_e)) * MOE_TM, 0, MOE_TM)
    tile_valid = jnp.where(tile < n_tiles, valid, 0).astype(jnp.int32)
    return (tile_expert, tile_valid, n_tiles.reshape(1), (ends - 1).astype(jnp.int32),
            tiles_e.astype(jnp.int32))


def _ple_final_kernel(h_ref, m_ref, p_ref, wpp_ref, wpg_ref, gp_ref, gf_ref, y_ref):
    h = h_ref[...] + m_ref[...]
    hn = _rmsnorm(h, gp_ref[...])
    h = h + _mm(p_ref[...], wpp_ref[...]) * _sigmoid(_mm(hn, wpg_ref[...]))
    y_ref[...] = _rmsnorm(h, gf_ref[...])


def _ple_final(h, m, p, wpp, wpg, gp, gf):
    t = h.shape[0]
    tm = min(t, 256)
    row = lambda n: pl.BlockSpec((tm, n), lambda i: (i, 0))
    full = lambda a: pl.BlockSpec(a.shape, lambda i: (0,) * a.ndim)
    return pl.pallas_call(
        _ple_final_kernel,
        grid=(t // tm,),
        in_specs=[row(D_MODEL), row(D_MODEL), row(PLE_DIM), full(wpp), full(wpg), full(gp), full(gf)],
        out_specs=row(D_MODEL),
        out_shape=jax.ShapeDtypeStruct((t, D_MODEL), F32),
        compiler_params=_params("parallel"),
        name="ple_final",
    )(h, m, p, wpp, wpg, gp, gf)


def kernel(x_prompt, x_sample, p_prompt, p_sample, cache_k, cache_v, state_conv, state_S, rel_bias, norm_mix, w_in, att_sink, conv_w, dn_A_log, dn_dt_bias, dn_norm, w_out, norm_ffn, w_router_group, w_router_expert, w_gate, w_up, w_down, w_ple_proj, w_ple_gate, norm_ple, norm_final):
    batch, seq, _ = x_prompt.shape
    nseq = x_sample.shape[0]
    assert x_sample.shape[1] == 1 and norm_mix.shape[0] == 1 and cache_k.shape[2] == WINDOW
    assert seq % GDN_TB == 0 and seq % ATT_BLOCK == 0

    wi = w_in[0]
    o_db = ATT_COLS + CONV_CH
    w_in_re = jnp.concatenate(
        [wi[:, :o_db], wi[:, o_db + 2 * DN_HEADS:], wi[:, o_db:o_db + 2 * DN_HEADS],
         jnp.zeros((D_MODEL, LANES - 2 * DN_HEADS), F32)], axis=1).astype(BF16)
    row = lambda a: a.reshape(1, -1).astype(F32)
    pad_lanes = lambda a, off: jnp.zeros((1, LANES), F32).at[0, off:off + a.shape[0]].set(a)
    alog = pad_lanes(dn_A_log[0], DN_HEADS)
    dtb = pad_lanes(dn_dt_bias[0], DN_HEADS)
    dnx = jnp.tile(dn_norm[0], DN_HEADS).reshape(1, DN_WIDTH)
    w_router = jnp.concatenate(
        [w_router_group[0], w_router_expert[0],
         jnp.zeros((D_MODEL, LANES - N_GROUPS - N_EXPERTS), F32)], axis=1).astype(BF16)
    wo = w_out[0].astype(BF16)
    wg, wu, wd = w_gate[0], w_up[0], w_down[0]
    wpp, wpg = w_ple_proj[0].astype(BF16), w_ple_gate[0].astype(BF16)
    sink = att_sink[0]

    qi = np.arange(ATT_BLOCK)[:, None]
    kj = np.arange(2 * ATT_BLOCK)[None, :]
    bucket_p = jnp.asarray(_t5_bucket_np(qi + ATT_BLOCK - kj))
    bucket_s = jnp.asarray(_t5_bucket_np(WINDOW - np.arange(WINDOW)[None, :]))

    def tail(x, o_att, o_dn, p):
        h1, xn2, gates = _outproj_router(x, o_att, o_dn, wo, row(norm_ffn[0]), w_router)
        moe = _moe(xn2, gates, wg, wu, wd)
        return _ple_final(h1, moe, p, wpp, wpg, row(norm_ple[0]), row(norm_final))

    xp = x_prompt.reshape(batch * seq, D_MODEL)
    att_p, qkv_p, dz_p, ba_p, xc_tails = _inproj_conv(xp, row(norm_mix[0]), w_in_re, conv_w[0], seq)
    o_att_p = _attn_prompt(att_p, bucket_p, rel_bias, sink, batch, seq)
    o_dn_p, s_p = _gdn_prompt(qkv_p, dz_p, ba_p, alog, dtb, dnx, batch, seq)
    h1, xn2, info, cnt = _route_sparse(xp, o_att_p, o_dn_p, wo, row(norm_ffn[0]), w_router)
    pos = _positions(info, cnt)
    pos1, pos2 = pos[:, 0], pos[:, 1]
    max_tiles = _moe_tiles(batch * seq)
    cnt_e = cnt[0, ROUTER_OFF:ROUTER_OFF + N_EXPERTS].astype(jnp.int32)
    tile_expert, tile_valid, n_tiles, last_tile, used = _tile_tables(cnt_e, max_tiles)
    xs = _sc_scatter_rows(xn2, pos1, pos2, max_tiles * MOE_TM)
    ys = _experts(xs, tile_expert, tile_valid, n_tiles, wg, wu, wd)
    y1, y2 = _sc_gather_rows(ys, pos1, pos2)
    y_p = _ple_sparse(h1, info, y1, y2, p_prompt[0].reshape(batch * seq, PLE_DIM),
                      wpp, wpg, row(norm_ple[0]), row(norm_final))

    xs = x_sample.reshape(nseq, D_MODEL)
    att_s, xc_s, dz_s, ba_s = _inproj(xs, row(norm_mix[0]), w_in_re)
    ck = cache_k[0].reshape(nseq, WINDOW, KV_WIDTH)
    cv = cache_v[0].reshape(nseq, WINDOW, KV_WIDTH)
    o_att_s = _attn_sample(att_s, ck, cv, bucket_s, rel_bias, sink)
    sconv_t = jnp.swapaxes(state_conv[0], 0, 1)
    o_dn_s, s_s = _gdn_sample(xc_s, dz_s, ba_s, sconv_t,
                              state_S[0].reshape(nseq, DN_HEADS * DN_DK, DN_DV), conv_w[0], alog, dtb,
                              dn_norm[0].reshape(1, DN_DV))
    s_s = s_s.reshape(nseq, DN_HEADS, DN_DK, DN_DV)
    y_s = tail(xs, o_att_s, o_dn_s.reshape(nseq, DN_WIDTH), p_sample[0].reshape(nseq, PLE_DIM))

    att_p3 = att_p.reshape(batch, seq, ATT_COLS)
    kv_shape = (1, batch, WINDOW, ATT_KV_HEADS, HEAD_DIM)
    k_p = att_p3[:, seq - WINDOW:, ATT_WIDTH:ATT_WIDTH + KV_WIDTH].reshape(kv_shape)
    v_p = att_p3[:, seq - WINDOW:, ATT_WIDTH + KV_WIDTH:].reshape(kv_shape)
    conv_p = xc_tails.reshape(batch, -1, TAIL, CONV_CH)[:, -1, TAIL - (CONV_WIDTH - 1):][None]
    k_new = att_s[:, None, ATT_WIDTH:ATT_WIDTH + KV_WIDTH]
    v_new = att_s[:, None, ATT_WIDTH + KV_WIDTH:]
    kv_s_shape = (1, nseq, WINDOW, ATT_KV_HEADS, HEAD_DIM)
    k_s = jnp.concatenate([ck[:, 1:], k_new], axis=1).reshape(kv_s_shape)
    v_s = jnp.concatenate([cv[:, 1:], v_new], axis=1).reshape(kv_s_shape)
    conv_s = jnp.concatenate([state_conv[0][:, 1:], xc_s[:, None, :]], axis=1)[None]
    return (y_p.reshape(batch, seq, D_MODEL), y_s.reshape(nseq, 1, D_MODEL),
            k_p, v_p, conv_p, s_p[None], k_s, v_s, conv_s, s_s[None])
```

```python
import functools
import math

import numpy as np
import jax
import jax.numpy as jnp
from jax import lax
from jax.experimental import pallas as pl
from jax.experimental.pallas import tpu as pltpu
from jax.experimental.pallas import tpu_sc as plsc

F32 = jnp.float32
BF16 = jnp.bfloat16

D_MODEL = 1024
ATT_HEADS = 8
ATT_KV_HEADS = 2
HEAD_DIM = 64
GQA = ATT_HEADS // ATT_KV_HEADS
WINDOW = 128
ATT_BLOCK = 128
N_BUCKETS = 32
DN_HEADS = 8
DN_DK = 64
DN_DV = 64
CONV_WIDTH = 4
DN_CHUNK = 64
ATT_WIDTH = ATT_HEADS * HEAD_DIM
KV_WIDTH = ATT_KV_HEADS * HEAD_DIM
DN_WIDTH = DN_HEADS * DN_DV
CONV_CH = 3 * DN_WIDTH
N_GROUPS = 4
EXPERTS_PER_GROUP = 8
N_EXPERTS = N_GROUPS * EXPERTS_PER_GROUP
D_EXPERT = 256
PLE_DIM = 256
EPS = 1e-6
NEG_INF = float("-inf")

ATT_COLS = ATT_WIDTH + 2 * KV_WIDTH
LANES = 128
IN_COLS = ATT_COLS + CONV_CH + DN_WIDTH + LANES
ROUTER_OFF = N_GROUPS
VMEM_LIMIT = 48 * 1024 * 1024
ROW_TM = 512


def _params(*sem):
    return pltpu.CompilerParams(dimension_semantics=sem, vmem_limit_bytes=VMEM_LIMIT)


def _mm(a, b):
    return jnp.dot(a.astype(BF16), b.astype(BF16), preferred_element_type=F32)


def _mm_nt(a, b):
    return lax.dot_general(a.astype(BF16), b.astype(BF16), (((1,), (1,)), ((), ())),
                           preferred_element_type=F32)


def _mm_tn(a, b):
    return lax.dot_general(a.astype(BF16), b.astype(BF16), (((0,), (0,)), ((), ())),
                           preferred_element_type=F32)


def _split3(x):
    h1 = x.astype(BF16)
    r1 = x - h1.astype(F32)
    h2 = r1.astype(BF16)
    h3 = (r1 - h2.astype(F32)).astype(BF16)
    return h1, h2, h3


def _mm_sel_rhs(x, sel):
    h1, h2, h3 = _split3(x)
    d = lambda h: jnp.dot(h, sel, preferred_element_type=F32)
    return d(h1) + d(h2) + d(h3)


def _mm_sel_lhs(sel, x):
    h1, h2, h3 = _split3(x)
    d = lambda h: jnp.dot(sel, h, preferred_element_type=F32)
    return d(h1) + d(h2) + d(h3)


def _mm3(a, b):
    ah = a.astype(BF16)
    al = (a - ah.astype(F32)).astype(BF16)
    bh = b.astype(BF16)
    bl = (b - bh.astype(F32)).astype(BF16)
    d = lambda u, v: jnp.dot(u, v, preferred_element_type=F32)
    return d(ah, bh) + d(ah, bl) + d(al, bh)


def _sigmoid(x):
    return 1.0 / (1.0 + jnp.exp(-x))


def _silu(x):
    return x * _sigmoid(x)


def _softplus(x):
    return jnp.maximum(x, 0.0) + jnp.log1p(jnp.exp(-jnp.abs(x)))


def _rmsnorm(x, g):
    return x * lax.rsqrt(jnp.mean(x * x, axis=-1, keepdims=True) + EPS) * g


def _t5_bucket_np(dist):
    max_exact = N_BUCKETS // 2
    d = np.maximum(dist, 0)
    ratio = (np.log(np.maximum(d, 1).astype(np.float32) / np.float32(max_exact))
             / np.float32(math.log(WINDOW / max_exact))).astype(np.float32)
    large = np.minimum(max_exact + (ratio * np.float32(N_BUCKETS - max_exact)).astype(np.int32),
                       N_BUCKETS - 1)
    return np.where(d < max_exact, d, large).astype(np.int32)


def _bias_lookup(bucket, rb_ref, h):
    acc = jnp.zeros(bucket.shape, F32)
    for t in range(N_BUCKETS):
        acc = jnp.where(bucket == t, rb_ref[t, h], acc)
    return acc


def _inproj_kernel(x_ref, g_ref, w_ref, att_ref, xc_ref, dz_ref, ba_ref):
    xn = _rmsnorm(x_ref[...], g_ref[...]).astype(BF16)
    o0, o1, o2 = ATT_COLS, ATT_COLS + CONV_CH, ATT_COLS + CONV_CH + DN_WIDTH
    att_ref[...] = jnp.dot(xn, w_ref[:, :o0], preferred_element_type=F32)
    xc_ref[...] = jnp.dot(xn, w_ref[:, o0:o1], preferred_element_type=F32)
    dz_ref[...] = jnp.dot(xn, w_ref[:, o1:o2], preferred_element_type=F32)
    ba_ref[...] = jnp.dot(xn, w_ref[:, o2:], preferred_element_type=F32)


def _inproj(x, g, w):
    t = x.shape[0]
    tm = min(t, ROW_TM)
    row = lambda n: pl.BlockSpec((tm, n), lambda i: (i, 0))
    full = lambda a: pl.BlockSpec(a.shape, lambda i: (0,) * a.ndim)
    return pl.pallas_call(
        _inproj_kernel,
        grid=(t // tm,),
        in_specs=[row(D_MODEL), full(g), full(w)],
        out_specs=[row(ATT_COLS), row(CONV_CH), row(DN_WIDTH), row(LANES)],
        out_shape=[jax.ShapeDtypeStruct((t, n), F32) for n in (ATT_COLS, CONV_CH, DN_WIDTH, LANES)],
        compiler_params=_params("parallel"),
        name="inproj",
    )(x, g, w)


TAIL = 8
PAIR = 2 * DN_DK
N_PAIRS = DN_WIDTH // PAIR


def _head_sums(z, pair_ones):
    hi = z.astype(BF16)
    lw = (z - hi.astype(F32)).astype(BF16)
    d = lambda a, p: jnp.dot(a[:, p * PAIR:(p + 1) * PAIR], pair_ones, preferred_element_type=F32)
    return jnp.concatenate([d(hi, p) + d(lw, p) for p in range(N_PAIRS)], axis=1)


def _inproj_conv_kernel(x_ref, g_ref, w_ref, cw_ref, ones_ref, att_ref, qkv_ref, dz_ref, ba_ref, tail_ref,
                        xp_scr, *, tiles_per_seq):
    tm = x_ref.shape[0]

    @pl.when(pl.program_id(0) % tiles_per_seq == 0)
    def _():
        xp_scr[0:TAIL, :] = jnp.zeros((TAIL, CONV_CH), F32)

    xn = _rmsnorm(x_ref[...], g_ref[...]).astype(BF16)
    o0, o1, o2 = ATT_COLS, ATT_COLS + CONV_CH, ATT_COLS + CONV_CH + DN_WIDTH
    xc = jnp.dot(xn, w_ref[:, o0:o1], preferred_element_type=F32)
    att_ref[...] = jnp.dot(xn, w_ref[:, :o0], preferred_element_type=F32)
    dz_ref[...] = jnp.dot(xn, w_ref[:, o1:o2], preferred_element_type=F32)
    ba_ref[...] = jnp.dot(xn, w_ref[:, o2:], preferred_element_type=F32)

    xp_scr[TAIL:, :] = xc
    y = xp_scr[TAIL - 3:TAIL - 3 + tm, :] * cw_ref[0:1, :]
    y = y + xp_scr[TAIL - 2:TAIL - 2 + tm, :] * cw_ref[1:2, :]
    y = y + xp_scr[TAIL - 1:TAIL - 1 + tm, :] * cw_ref[2:3, :]
    y = y + xc * cw_ref[3:4, :]
    tail = xc[tm - TAIL:, :]
    xp_scr[0:TAIL, :] = tail
    tail_ref[0] = tail
    y = _silu(y)
    q = y[:, :DN_WIDTH]
    k = y[:, DN_WIDTH:2 * DN_WIDTH]
    inv_norm = lax.rsqrt(_head_sums(jnp.concatenate([q * q, k * k], axis=0), ones_ref[...]) + EPS)
    qkv_ref[:, :DN_WIDTH] = q * inv_norm[:tm] * (DN_DK ** -0.5)
    qkv_ref[:, DN_WIDTH:2 * DN_WIDTH] = k * inv_norm[tm:]
    qkv_ref[:, 2 * DN_WIDTH:] = y[:, 2 * DN_WIDTH:]


def _pair_ones():
    lane = np.arange(PAIR)
    return jnp.asarray((lane[:, None] // DN_DV == lane[None, :] // DN_DV).astype(np.float32), dtype=BF16)


def _inproj_conv(x, g, w, conv_w, seq):
    t = x.shape[0]
    tm = ROW_TM
    assert seq % tm == 0
    ones = _pair_ones()
    row = lambda n: pl.BlockSpec((tm, n), lambda i: (i, 0))
    full = lambda a: pl.BlockSpec(a.shape, lambda i: (0,) * a.ndim)
    return pl.pallas_call(
        functools.partial(_inproj_conv_kernel, tiles_per_seq=seq // tm),
        grid=(t // tm,),
        in_specs=[row(D_MODEL), full(g), full(w), full(conv_w), full(ones)],
        out_specs=[row(ATT_COLS), row(CONV_CH), row(DN_WIDTH), row(LANES),
                   pl.BlockSpec((1, TAIL, CONV_CH), lambda i: (i, 0, 0))],
        out_shape=[jax.ShapeDtypeStruct((t, n), F32) for n in (ATT_COLS, CONV_CH, DN_WIDTH, LANES)]
                  + [jax.ShapeDtypeStruct((t // tm, TAIL, CONV_CH), F32)],
        scratch_shapes=[pltpu.VMEM((TAIL + tm, CONV_CH), F32)],
        compiler_params=_params("arbitrary"),
        name="inproj_conv",
    )(x, g, w, conv_w, ones)


GROUP_ROWS = GQA * ATT_BLOCK


def _attn_prompt_kernel(cur_ref, prev_ref, bucket_ref, rb_ref, sink_ref, o_ref, bias_scr, sink_scr):
    i = pl.program_id(0)
    nseq = cur_ref.shape[0]

    @pl.when(i == 0)
    def _():
        qi = lax.broadcasted_iota(jnp.int32, (ATT_BLOCK, 2 * ATT_BLOCK), 0)
        kj = lax.broadcasted_iota(jnp.int32, (ATT_BLOCK, 2 * ATT_BLOCK), 1)
        dist = qi + ATT_BLOCK - kj
        band = jnp.logical_and(dist >= 0, dist < WINDOW)
        bucket = bucket_ref[...]
        hrow = lax.broadcasted_iota(jnp.int32, (GROUP_ROWS, 1), 0) // ATT_BLOCK
        for g in range(ATT_KV_HEADS):
            sink_col = jnp.zeros((GROUP_ROWS, 1), F32)
            for hh in range(GQA):
                h = g * GQA + hh
                bias = jnp.where(band, _bias_lookup(bucket, rb_ref, h), NEG_INF)
                bias_scr[0, g, hh * ATT_BLOCK:(hh + 1) * ATT_BLOCK, :] = bias
                bias_scr[1, g, hh * ATT_BLOCK:(hh + 1) * ATT_BLOCK, :] = jnp.where(kj >= ATT_BLOCK, bias, NEG_INF)
                sink_col = jnp.where(hrow == hh, sink_ref[h], sink_col)
            sink_scr[g] = sink_col

    first = (i == 0).astype(jnp.int32)
    probs = [(b, g) for b in range(nseq) for g in range(ATT_KV_HEADS)]
    scores = []
    for b, g in probs:
        cur = cur_ref[b]
        prev = prev_ref[b]
        q = jnp.concatenate([cur[:, (g * GQA + hh) * HEAD_DIM:(g * GQA + hh + 1) * HEAD_DIM]
                             for hh in range(GQA)], axis=0) * (HEAD_DIM ** -0.5)
        kcol = slice(ATT_WIDTH + g * HEAD_DIM, ATT_WIDTH + (g + 1) * HEAD_DIM)
        k2 = jnp.concatenate([prev[:, kcol], cur[:, kcol]], axis=0)
        scores.append(_mm_nt(q, k2) + bias_scr[first, g])
    probs_p, dens = [], []
    for (b, g), s in zip(probs, scores):
        sink = sink_scr[g]
        m = jnp.maximum(jnp.max(s, axis=-1, keepdims=True), sink)
        p = jnp.exp(s - m)
        dens.append(jnp.sum(p, axis=-1, keepdims=True) + jnp.exp(sink - m))
        probs_p.append(p)
    outs = {}
    for (b, g), p, den in zip(probs, probs_p, dens):
        vcol = slice(ATT_WIDTH + KV_WIDTH + g * HEAD_DIM, ATT_WIDTH + KV_WIDTH + (g + 1) * HEAD_DIM)
        v2 = jnp.concatenate([prev_ref[b][:, vcol], cur_ref[b][:, vcol]], axis=0)
        outs[b, g] = _mm(p, v2) / den
    for b in range(nseq):
        o_ref[b] = jnp.concatenate([outs[b, g][hh * ATT_BLOCK:(hh + 1) * ATT_BLOCK, :]
                                    for g in range(ATT_KV_HEADS) for hh in range(GQA)], axis=1)


def _attn_prompt(att, bucket, rel_bias, sink, batch, seq):
    nb = seq // ATT_BLOCK
    smem = pl.BlockSpec(memory_space=pltpu.SMEM)
    att3 = att.reshape(batch, seq, ATT_COLS)
    out = pl.pallas_call(
        _attn_prompt_kernel,
        grid=(nb,),
        in_specs=[
            pl.BlockSpec((batch, ATT_BLOCK, ATT_COLS), lambda i: (0, i, 0)),
            pl.BlockSpec((batch, ATT_BLOCK, ATT_COLS), lambda i: (0, jnp.maximum(i - 1, 0), 0)),
            pl.BlockSpec(bucket.shape, lambda i: (0, 0)),
            smem, smem,
        ],
        out_specs=pl.BlockSpec((batch, ATT_BLOCK, ATT_WIDTH), lambda i: (0, i, 0)),
        out_shape=jax.ShapeDtypeStruct((batch, seq, ATT_WIDTH), F32),
        scratch_shapes=[pltpu.VMEM((2, ATT_KV_HEADS, GROUP_ROWS, 2 * ATT_BLOCK), F32),
                        pltpu.VMEM((ATT_KV_HEADS, GROUP_ROWS, 1), F32)],
        compiler_params=_params("arbitrary"),
        name="attn_prompt",
    )(att3, att3, bucket, rel_bias, sink)
    return out.reshape(batch * seq, ATT_WIDTH)


ATT_S_BB = 8


def _attn_sample_kernel(att_ref, ck_ref, cv_ref, bucket_ref, rb_ref, sink_ref, o_ref,
                        bias_scr, col_scr):
    hrow = lax.broadcasted_iota(jnp.int32, (ATT_HEADS, LANES), 0)
    lane = lax.broadcasted_iota(jnp.int32, (ATT_HEADS, LANES), 1)

    @pl.when(pl.program_id(0) == 0)
    def _():
        bucket = jnp.broadcast_to(bucket_ref[...], (ATT_HEADS, LANES))
        bias = jnp.zeros((ATT_HEADS, LANES), F32)
        cols = jnp.zeros((ATT_HEADS, LANES), F32)
        for h in range(ATT_HEADS):
            bias = jnp.where(hrow == h, _bias_lookup(bucket, rb_ref, h), bias)
            cols = jnp.where(jnp.logical_and(hrow == h, lane == 0), sink_ref[h], cols)
            cols = jnp.where(jnp.logical_and(hrow == h, lane == 1), rb_ref[0, h], cols)
        bias_scr[...] = jnp.where(lane >= 1, bias, NEG_INF)
        col_scr[...] = cols

    bias_c = bias_scr[...]
    sink = col_scr[:, 0:1]
    bias_n = col_scr[:, 1:2]
    same_group = (hrow // GQA) == (lane // HEAD_DIM)
    low_group = lax.broadcasted_iota(jnp.int32, (ATT_HEADS, HEAD_DIM), 0) < GQA
    rnd = lambda a: a.astype(BF16).astype(F32)
    seqs = range(ATT_S_BB)
    rows = [att_ref[b:b + 1, :] for b in seqs]
    q_bds = []
    for row in rows:
        q = row[:, :ATT_WIDTH] * (HEAD_DIM ** -0.5)
        qh = jnp.concatenate([q[:, h * HEAD_DIM:(h + 1) * HEAD_DIM] for h in range(ATT_HEADS)], axis=0)
        q_bds.append(jnp.where(same_group, jnp.concatenate([qh, qh], axis=1), 0.0))
    s_cs = [_mm_nt(q_bd, ck_ref[b]) + bias_c for b, q_bd in zip(seqs, q_bds)]
    prs, pns = [], []
    for row, q_bd, s_c in zip(rows, q_bds, s_cs):
        kn = row[:, ATT_WIDTH:ATT_WIDTH + KV_WIDTH]
        s_n = jnp.sum(rnd(q_bd) * rnd(kn), axis=-1, keepdims=True) + bias_n
        m = jnp.maximum(jnp.maximum(jnp.max(s_c, axis=-1, keepdims=True), s_n), sink)
        p_c = jnp.exp(s_c - m)
        p_n = jnp.exp(s_n - m)
        den = jnp.sum(p_c, axis=-1, keepdims=True) + p_n + jnp.exp(sink - m)
        prs.append(p_c / den)
        pns.append(p_n / den)
    pvs = [_mm(pr, cv_ref[b]) for b, pr in zip(seqs, prs)]
    for b, row, pv, pn in zip(seqs, rows, pvs, pns):
        vn = row[:, ATT_WIDTH + KV_WIDTH:]
        o_full = pv + rnd(pn) * rnd(vn)
        o_sel = jnp.where(low_group, o_full[:, :HEAD_DIM], o_full[:, HEAD_DIM:])
        o_ref[b:b + 1, :] = jnp.concatenate([o_sel[h:h + 1, :] for h in range(ATT_HEADS)], axis=1)


def _attn_sample(att, ck, cv, bucket, rel_bias, sink):
    nseq = att.shape[0]
    smem = pl.BlockSpec(memory_space=pltpu.SMEM)
    cache = pl.BlockSpec((ATT_S_BB, WINDOW, KV_WIDTH), lambda i: (i, 0, 0))
    return pl.pallas_call(
        _attn_sample_kernel,
        grid=(nseq // ATT_S_BB,),
        in_specs=[pl.BlockSpec((ATT_S_BB, ATT_COLS), lambda i: (i, 0)), cache, cache,
                  pl.BlockSpec(bucket.shape, lambda i: (0, 0)), smem, smem],
        out_specs=pl.BlockSpec((ATT_S_BB, ATT_WIDTH), lambda i: (i, 0)),
        out_shape=jax.ShapeDtypeStruct((nseq, ATT_WIDTH), F32),
        scratch_shapes=[pltpu.VMEM((ATT_HEADS, LANES), F32), pltpu.VMEM((ATT_HEADS, LANES), F32)],
        compiler_params=_params("arbitrary"),
        name="attn_sample",
    )(att, ck, cv, bucket, rel_bias, sink)


GDN_TB = 128
GDN_NC = GDN_TB // DN_CHUNK


def _gdn_gates(ba, alog, dtb):
    beta = _sigmoid(ba)
    g = -jnp.exp(alog) * _softplus(ba + dtb)
    return beta, g


def _pair_diag(x, lo):
    xb = x.astype(BF16)
    zero = jnp.zeros_like(xb)
    return jnp.concatenate([jnp.where(lo, xb, zero), jnp.where(lo, zero, xb)], axis=0)


def _gdn_prompt_kernel(qkv_ref, dz_ref, ba_ref, alog_ref, dtb_ref, dnx_ref,
                       hsum_ref, expb_ref, expg_ref, ltri_ref,
                       o_ref, s_out_ref, s_scr):
    i = pl.program_id(0)
    nb = qkv_ref.shape[0]

    @pl.when(i == 0)
    def _():
        s_scr[...] = jnp.zeros(s_scr.shape, F32)

    hsum = hsum_ref[...]
    ri = lax.broadcasted_iota(jnp.int32, (DN_CHUNK, PAIR), 0)
    ci = lax.broadcasted_iota(jnp.int32, (DN_CHUNK, PAIR), 1)
    lo = ci < DN_DK
    cj = jnp.where(lo, ci, ci - DN_DK)
    causal = ri >= cj
    strict = ri > cj
    eye = (ri == cj).astype(F32)

    def sel2(x, m):
        hi = x.astype(BF16)
        lw = (x - hi.astype(F32)).astype(BF16)
        return (jnp.dot(hi, m, preferred_element_type=F32) + jnp.dot(lw, m, preferred_element_type=F32))

    pre = []
    for b in range(nb):
        q = qkv_ref[b, :, :DN_WIDTH]
        k = qkv_ref[b, :, DN_WIDTH:2 * DN_WIDTH]
        v = qkv_ref[b, :, 2 * DN_WIDTH:]
        beta_c, g_c = _gdn_gates(ba_ref[b], alog_ref[...], dtb_ref[...])
        beta = sel2(beta_c, expb_ref[...])
        gam_c = _mm_sel_lhs(ltri_ref[...], g_c)
        gam = _mm_sel_rhs(gam_c, expg_ref[...])
        gam_t = gam_c.T
        kb = k * beta
        egam = jnp.exp(gam)
        pre.append(dict(q=q, k=k, kb=kb, vb=v * beta, qg=q * egam, wr=kb * egam, gam=gam, gam_t=gam_t))

    probs = [(b, p) for b in range(nb) for p in range(N_PAIRS)]
    pick = lambda m: jnp.where(lo, m[:DN_DK], m[DN_DK:])
    o_rows = [[] for _ in range(nb)]
    for c in range(GDN_NC):
        r0, r1 = c * DN_CHUNK, (c + 1) * DN_CHUNK
        sl = lambda name, b, p: pre[b][name][r0:r1, p * PAIR:(p + 1) * PAIR]
        raws = []
        for b, p in probs:
            k_p = sl("k", b, p)
            k_rows = jnp.concatenate([jnp.where(lo, k_p, 0.0), jnp.where(lo, 0.0, k_p)], axis=0)
            raws.append(_mm_nt(jnp.concatenate([sl("kb", b, p), sl("q", b, p)], axis=0), k_rows))
        pws, ts, qks = [], [], []
        for (b, p), raw in zip(probs, raws):
            gcol = sl("gam", b, p)
            h0 = DN_HEADS + 2 * p
            gam_t = pre[b]["gam_t"]
            grow = jnp.concatenate([gam_t[h0:h0 + 1, r0:r1], gam_t[h0 + 1:h0 + 2, r0:r1]], axis=1)
            decay = jnp.exp(jnp.where(causal, gcol - grow, NEG_INF))
            a = jnp.where(strict, raw[:DN_CHUNK] * decay, 0.0)
            qks.append(jnp.where(causal, raw[DN_CHUNK:] * decay, 0.0))
            pws.append(-a)
            ts.append(eye - a)
        pws = [_mm(pw, _pair_diag(pw, lo)) for pw in pws]
        for _ in range(4):
            rs = [_mm(jnp.concatenate([pw, t], axis=0), _pair_diag(pw, lo)) for pw, t in zip(pws, ts)]
            pws = [r[:DN_CHUNK] for r in rs]
            ts = [t + r[DN_CHUNK:] for t, r in zip(ts, rs)]
        rs = [_mm(t, _pair_diag(pw, lo)) for pw, t in zip(pws, ts)]
        ts = [t + r for t, r in zip(ts, rs)]
        sols = [_mm(t, jnp.concatenate([_pair_diag(sl("vb", b, p), lo), _pair_diag(sl("wr", b, p), lo)],
                                       axis=1)) for (b, p), t in zip(probs, ts)]
        qkuws = [_mm(qk, jnp.concatenate([_pair_diag(s[:, :PAIR], lo), _pair_diag(s[:, PAIR:], lo)], axis=1))
                 for qk, s in zip(qks, sols)]
        crosses, gls = [], []
        for (b, p), s in zip(probs, sols):
            gam_last = pre[b]["gam"][r1 - 1:r1, p * PAIR:(p + 1) * PAIR]
            kd = sl("k", b, p) * jnp.exp(gam_last - sl("gam", b, p))
            crosses.append(_mm_tn(kd, s))
            gls.append(jnp.exp(gam_last))
        lhs = [jnp.concatenate([pick(cr[:, PAIR:]), sl("qg", b, p) - qkuw[:, PAIR:]], axis=0)
               for (b, p), cr, qkuw in zip(probs, crosses, qkuws)]
        s_olds = [s_scr[b, p] for b, p in probs]
        rs = [_mm(l, _pair_diag(s_old, lo)) for l, s_old in zip(lhs, s_olds)]
        o_pairs = [[] for _ in range(nb)]
        for (b, p), r, s_old, gl, cr, qkuw in zip(probs, rs, s_olds, gls, crosses, qkuws):
            s_scr[b, p] = gl * s_old - r[:DN_DK] + pick(cr[:, :PAIR])
            o_pairs[b].append(r[DN_DK:] + qkuw[:, :PAIR])
        for b in range(nb):
            o_rows[b].append(jnp.concatenate(o_pairs[b], axis=1))

    o_all = jnp.concatenate([jnp.concatenate(rows, axis=0) for rows in o_rows], axis=0)
    inv_rms = lax.rsqrt(_head_sums(o_all * o_all, hsum) * (1.0 / DN_DV) + EPS)
    for b in range(nb):
        rows = slice(b * GDN_TB, (b + 1) * GDN_TB)
        o_ref[b] = o_all[rows] * inv_rms[rows] * dnx_ref[...] * _silu(dz_ref[b])

    @pl.when(i == pl.num_programs(0) - 1)
    def _():
        for b in range(nb):
            for p in range(N_PAIRS):
                s_p = s_scr[b, p]
                s_out_ref[b, 2 * p] = s_p[:, :DN_DV]
                s_out_ref[b, 2 * p + 1] = s_p[:, DN_DV:]


def _gdn_consts():
    lane = np.arange(DN_WIDTH)
    pl_lane = np.arange(PAIR)
    hsum = (pl_lane[:, None] // DN_DV == pl_lane[None, :] // DN_DV)
    src = np.arange(LANES)
    expb = (src[:, None] == lane[None, :] // DN_DV)
    expg = (src[:, None] == DN_HEADS + lane[None, :] // DN_DV)
    tok = np.arange(GDN_TB)
    ltri = np.logical_and(tok[:, None] >= tok[None, :],
                          tok[:, None] // DN_CHUNK == tok[None, :] // DN_CHUNK)
    as_bf16 = lambda m: jnp.asarray(m.astype(np.float32), dtype=BF16)
    return as_bf16(hsum), as_bf16(expb), as_bf16(expg), as_bf16(ltri)


def _gdn_prompt(xc, dz, ba, alog, dtb, dnx, batch, seq):
    nt = seq // GDN_TB
    hsum, expb, expg, ltri = _gdn_consts()
    row = lambda n: pl.BlockSpec((batch, GDN_TB, n), lambda i: (0, i, 0))
    full = lambda a: pl.BlockSpec(a.shape, lambda i: (0,) * a.ndim)
    consts = (alog, dtb, dnx, hsum, expb, expg, ltri)
    as3d = lambda a: a.reshape(batch, seq, a.shape[-1])
    o, s = pl.pallas_call(
        _gdn_prompt_kernel,
        grid=(nt,),
        in_specs=[row(CONV_CH), row(DN_WIDTH), row(LANES)] + [full(a) for a in consts],
        out_specs=[row(DN_WIDTH),
                   pl.BlockSpec((batch, DN_HEADS, DN_DK, DN_DV), lambda i: (0, 0, 0, 0))],
        out_shape=[jax.ShapeDtypeStruct((batch, seq, DN_WIDTH), F32),
                   jax.ShapeDtypeStruct((batch, DN_HEADS, DN_DK, DN_DV), F32)],
        scratch_shapes=[pltpu.VMEM((batch, N_PAIRS, DN_DK, PAIR), F32)],
        compiler_params=_params("arbitrary"),
        name="gdn_prompt",
    )(as3d(xc), as3d(dz), as3d(ba), *consts)
    return o.reshape(batch * seq, DN_WIDTH), s


GDN_S_BB = 8


def _gdn_sample_kernel(xc_ref, dz_ref, ba_ref, sc_ref, s_ref, cw_ref, alog_ref, dtb_ref, dn_ref,
                       hsum_ref, eye_ref, hsel_ref, hrep3_ref, o_ref, s_out_ref):
    xc = xc_ref[...]
    y = sc_ref[0] * cw_ref[0:1, :]
    y = y + sc_ref[1] * cw_ref[1:2, :]
    y = y + sc_ref[2] * cw_ref[2:3, :]
    y = _silu(y + xc * cw_ref[3:4, :])
    hsum = hsum_ref[...]
    q = y[:, :DN_WIDTH]
    k = y[:, DN_WIDTH:2 * DN_WIDTH]
    v = y[:, 2 * DN_WIDTH:]
    q = q * lax.rsqrt(_mm_sel_rhs(q * q, hsum) + EPS) * (DN_DK ** -0.5)
    k = k * lax.rsqrt(_mm_sel_rhs(k * k, hsum) + EPS)
    beta_c, g_c = _gdn_gates(ba_ref[...], alog_ref[...], dtb_ref[...])
    eg_c = jnp.exp(g_c)
    eye = eye_ref[...]
    tr = lambda a: lax.dot_general(a, eye, (((0,), (0,)), ((), ())), precision=lax.Precision.HIGHEST,
                                   preferred_element_type=F32)
    gates_t = tr(jnp.concatenate([beta_c, eg_c], axis=1))
    beta_t = gates_t[:LANES]
    eg_t = gates_t[LANES:]
    dz = dz_ref[...]
    dn = dn_ref[...]
    split = lambda r: jnp.concatenate([r[:, h * DN_DV:(h + 1) * DN_DV] for h in range(DN_HEADS)], axis=0)
    own_head = hsel_ref[...].astype(F32)
    hrep3 = hrep3_ref[...]
    seqs = range(GDN_S_BB)
    dot = lambda a, b: jnp.dot(a.astype(BF16), b.astype(BF16), preferred_element_type=F32)

    def pieces(x):
        p1 = x.astype(BF16).astype(F32)
        r1 = x - p1
        p2 = r1.astype(BF16).astype(F32)
        return p1, p2, (r1 - p2).astype(BF16).astype(F32)

    heads = DN_HEADS
    k_pieces, kqs = [], []
    for b in seqs:
        kq_bd = jnp.concatenate([own_head * k[b:b + 1, :], own_head * q[b:b + 1, :]], axis=0)
        a1, a2, a3 = pieces(kq_bd)
        s1, s2, s3 = pieces(s_ref[b])
        r1 = dot(jnp.concatenate([a1, a2, a3], axis=0), s1)
        r2 = dot(jnp.concatenate([a1, a2], axis=0), s2)
        r3 = dot(a1, s3)
        n = 2 * heads
        kqs.append(((r3 + r2[n:] + r1[2 * n:]) + (r2[:n] + r1[n:2 * n])) + r1[:n])
        k_pieces.append((a1[:heads], a2[:heads], a3[:heads]))
    egs = [eg_t[DN_HEADS:2 * DN_HEADS, b:b + 1] for b in seqs]
    qks = [jnp.sum(split(q[b:b + 1, :]) * split(k[b:b + 1, :]), axis=-1, keepdims=True) for b in seqs]
    v_news = [beta_t[0:DN_HEADS, b:b + 1] * (split(v[b:b + 1, :]) - eg * kq[:heads])
              for b, eg, kq in zip(seqs, egs, kqs)]
    os_ = [eg * kq[heads:] + qk * v_new for eg, kq, qk, v_new in zip(egs, kqs, qks, v_news)]
    inv_rms = [lax.rsqrt(jnp.mean(o * o, axis=-1, keepdims=True) + EPS) for o in os_]
    for b, o, r in zip(seqs, os_, inv_rms):
        o_ref[b] = o * r * dn * _silu(split(dz[b:b + 1, :]))
    outers, egrows = [], []
    for (k1, k2, k3), v_new, eg in zip(k_pieces, v_news, egs):
        v1, v2, v3 = pieces(v_new)
        lhs = jnp.concatenate([k1, k1, k2, k1, k2, k3], axis=0).astype(BF16)
        rhs = jnp.concatenate([v1, v2, v1, v3, v2, v1], axis=0).astype(BF16)
        outers.append(lax.dot_general(lhs, rhs, (((0,), (0,)), ((), ())), preferred_element_type=F32))
        egrows.append(dot(hrep3, jnp.concatenate(pieces(jnp.broadcast_to(eg, (DN_HEADS, DN_DV))), axis=0)))
    for b, outer, egrow in zip(seqs, outers, egrows):
        s_out_ref[b] = s_ref[b] * egrow + outer


def _gdn_sample(xc, dz, ba, sconv_t, state, conv_w, alog, dtb, dn):
    nseq = xc.shape[0]
    lane = np.arange(DN_WIDTH)
    hsum = jnp.asarray((lane[:, None] // DN_DV == lane[None, :] // DN_DV).astype(np.float32), dtype=BF16)
    eye = jnp.eye(GDN_S_BB, dtype=F32)
    hsel_np = (np.arange(DN_HEADS)[:, None] == lane[None, :] // DN_DK).astype(np.float32)
    hsel = jnp.asarray(hsel_np, dtype=BF16)
    hrep3 = jnp.asarray(np.tile(hsel_np.T, (1, 3)), dtype=BF16)
    row = lambda n: pl.BlockSpec((GDN_S_BB, n), lambda i: (i, 0))
    full = lambda a: pl.BlockSpec(a.shape, lambda i: (0,) * a.ndim)
    st = pl.BlockSpec((GDN_S_BB, DN_HEADS * DN_DK, DN_DV), lambda i: (i, 0, 0))
    consts = (conv_w, alog, dtb, dn, hsum, eye, hsel, hrep3)
    return pl.pallas_call(
        _gdn_sample_kernel,
        grid=(nseq // GDN_S_BB,),
        in_specs=[row(CONV_CH), row(DN_WIDTH), row(LANES),
                  pl.BlockSpec((CONV_WIDTH - 1, GDN_S_BB, CONV_CH), lambda i: (0, i, 0)), st]
                 + [full(a) for a in consts],
        out_specs=[pl.BlockSpec((GDN_S_BB, DN_HEADS, DN_DV), lambda i: (i, 0, 0)), st],
        out_shape=[jax.ShapeDtypeStruct((nseq, DN_HEADS, DN_DV), F32),
                   jax.ShapeDtypeStruct(state.shape, F32)],
        compiler_params=_params("parallel"),
        name="gdn_sample",
    )(xc, dz, ba, sconv_t, state, *consts)


def _gdn_sample_front_kernel(xc_ref, dz_ref, ba_ref, sc_ref, cw_ref, alog_ref, dtb_ref, hsum_ref,
                             q_ref, k_ref, v_ref, dz_t_ref, gates_ref):
    xc = xc_ref[...]
    y = sc_ref[0] * cw_ref[0:1, :]
    y = y + sc_ref[1] * cw_ref[1:2, :]
    y = y + sc_ref[2] * cw_ref[2:3, :]
    y = _silu(y + xc * cw_ref[3:4, :])
    hsum = hsum_ref[...]
    q = y[:, :DN_WIDTH]
    k = y[:, DN_WIDTH:2 * DN_WIDTH]
    q = q * lax.rsqrt(_mm_sel_rhs(q * q, hsum) + EPS) * (DN_DK ** -0.5)
    k = k * lax.rsqrt(_mm_sel_rhs(k * k, hsum) + EPS)
    beta_c, g_c = _gdn_gates(ba_ref[...], alog_ref[...], dtb_ref[...])
    q_ref[...] = q.T
    k_ref[...] = k.T
    v_ref[...] = y[:, 2 * DN_WIDTH:].T
    dz_t_ref[...] = dz_ref[...].T
    gates_ref[0:LANES, :] = beta_c.T
    gates_ref[LANES:, :] = jnp.exp(g_c).T


def _gdn_sample_step_kernel(q_ref, k_ref, v_ref, dz_ref, gates_ref, dn_ref, s_ref, o_ref, s_out_ref):
    h = pl.program_id(0)
    beta = gates_ref[pl.ds(h, 1), :]
    eg = gates_ref[pl.ds(LANES + DN_HEADS + h, 1), :]
    q, k, v = q_ref[...], k_ref[...], v_ref[...]
    w = (k * beta) * eg
    qg = q * eg
    ws = jnp.zeros(v.shape, F32)
    qs = jnp.zeros(v.shape, F32)
    for dk in range(DN_DK):
        s_dk = s_ref[0, dk]
        ws = ws + w[dk:dk + 1, :] * s_dk
        qs = qs + qg[dk:dk + 1, :] * s_dk
    v_new = v * beta - ws
    qk = jnp.sum(q * k, axis=0, keepdims=True)
    o = qs + qk * v_new
    for dk in range(DN_DK):
        s_out_ref[0, dk] = s_ref[0, dk] * eg + k[dk:dk + 1, :] * v_new
    o = o * lax.rsqrt(jnp.mean(o * o, axis=0, keepdims=True) + EPS) * dn_ref[...]
    o_ref[...] = o * _silu(dz_ref[...])


def _gdn_sample_lanes(xc, dz, ba, sconv_t, state_t, conv_w, alog, dtb, dn):
    nseq = xc.shape[0]
    assert nseq == LANES
    lane = np.arange(DN_WIDTH)
    hsum = jnp.asarray((lane[:, None] // DN_DV == lane[None, :] // DN_DV).astype(np.float32), dtype=BF16)
    full = lambda a: pl.BlockSpec(a.shape, lambda i: (0,) * a.ndim)
    cm = jax.ShapeDtypeStruct((DN_WIDTH, nseq), F32)
    front_in = (xc, dz, ba, sconv_t, conv_w, alog, dtb, hsum)
    q_t, k_t, v_t, dz_t, gates_t = pl.pallas_call(
        _gdn_sample_front_kernel,
        grid=(1,),
        in_specs=[full(a) for a in front_in],
        out_specs=[pl.BlockSpec((DN_WIDTH, nseq), lambda i: (0, 0))] * 4
                  + [pl.BlockSpec((2 * LANES, nseq), lambda i: (0, 0))],
        out_shape=[cm, cm, cm, cm, jax.ShapeDtypeStruct((2 * LANES, nseq), F32)],
        compiler_params=_params("arbitrary"),
        name="gdn_sample_front",
    )(*front_in)
    dn_b = jnp.broadcast_to(dn.reshape(DN_DV, 1), (DN_DV, nseq))
    head = pl.BlockSpec((DN_DK, nseq), lambda h: (h, 0))
    st = pl.BlockSpec((1, DN_DK, DN_DV, nseq), lambda h: (h, 0, 0, 0))
    return pl.pallas_call(
        _gdn_sample_step_kernel,
        grid=(DN_HEADS,),
        in_specs=[head, head, head, head, full(gates_t), full(dn_b), st],
        out_specs=[head, st],
        out_shape=[cm, jax.ShapeDtypeStruct(state_t.shape, F32)],
        compiler_params=_params("parallel"),
        name="gdn_sample_step",
    )(q_t, k_t, v_t, dz_t, gates_t, dn_b, state_t)


def _route(xn, wr):
    logits = jnp.dot(xn, wr, preferred_element_type=F32)
    lane = lax.broadcasted_iota(jnp.int32, logits.shape, 1).astype(F32)
    first_at = lambda hit: jnp.min(jnp.where(hit, lane, float(LANES)), axis=-1, keepdims=True)
    glog = jnp.where(lane < N_GROUPS, logits, NEG_INF)
    gmax = jnp.max(glog, axis=-1, keepdims=True)
    gsel = first_at(glog == gmax)
    pgsel = 1.0 / jnp.sum(jnp.exp(glog - gmax), axis=-1, keepdims=True)
    lo = ROUTER_OFF + gsel * EXPERTS_PER_GROUP
    in_group = jnp.logical_and(lane >= lo, lane < lo + EXPERTS_PER_GROUP)
    elog = jnp.where(in_group, logits, NEG_INF)
    m1 = jnp.max(elog, axis=-1, keepdims=True)
    i1 = first_at(elog == m1)
    z = jnp.sum(jnp.exp(elog - m1), axis=-1, keepdims=True)
    elog2 = jnp.where(lane == i1, NEG_INF, elog)
    m2 = jnp.max(elog2, axis=-1, keepdims=True)
    i2 = first_at(elog2 == m2)
    p1 = 1.0 / z
    p2 = jnp.exp(m2 - m1) / z
    tot = p1 + p2
    return lane, i1, i2, p1 / tot * pgsel, p2 / tot * pgsel


def _outproj(x_ref, oa_ref, od_ref, wo_ref):
    return x_ref[...] + _mm(oa_ref[...], wo_ref[:ATT_WIDTH, :]) + _mm(od_ref[...], wo_ref[ATT_WIDTH:, :])


def _outproj_router_kernel(x_ref, oa_ref, od_t_ref, wo_ref, g_ref, wr_ref, h_ref, xn_ref, gate_ref):
    h = (x_ref[...] + _mm(oa_ref[...], wo_ref[:ATT_WIDTH, :])
         + _mm(od_t_ref[...].T, wo_ref[ATT_WIDTH:, :]))
    h_ref[...] = h
    xn = _rmsnorm(h, g_ref[...]).astype(BF16)
    xn_ref[...] = xn
    lane, i1, i2, g1, g2 = _route(xn, wr_ref[...])
    gate_ref[...] = jnp.where(lane == i1, g1, 0.0) + jnp.where(lane == i2, g2, 0.0)


def _outproj_router(x, oa, od_t, wo, g, wr):
    t = x.shape[0]
    tm = t
    row = lambda n: pl.BlockSpec((tm, n), lambda i: (i, 0))
    full = lambda a: pl.BlockSpec(a.shape, lambda i: (0,) * a.ndim)
    return pl.pallas_call(
        _outproj_router_kernel,
        grid=(t // tm,),
        in_specs=[row(D_MODEL), row(ATT_WIDTH), full(od_t), full(wo), full(g), full(wr)],
        out_specs=[row(D_MODEL), row(D_MODEL), row(LANES)],
        out_shape=[jax.ShapeDtypeStruct((t, D_MODEL), F32), jax.ShapeDtypeStruct((t, D_MODEL), BF16),
                   jax.ShapeDtypeStruct((t, LANES), F32)],
        compiler_params=_params("parallel"),
        name="outproj_router",
    )(x, oa, od_t, wo, g, wr)


def _moe_kernel(xn_ref, gate_ref, wg_ref, wu_ref, wd_ref, o_ref):
    e = pl.program_id(1)
    xn = xn_ref[...]
    lane = lax.broadcasted_iota(jnp.int32, gate_ref.shape, 1)
    gate = jnp.sum(jnp.where(lane == e + ROUTER_OFF, gate_ref[...], 0.0), axis=-1, keepdims=True)
    hg = jnp.dot(xn, wg_ref[...].astype(BF16), preferred_element_type=F32)
    hu = jnp.dot(xn, wu_ref[...].astype(BF16), preferred_element_type=F32)
    hm = _silu(hg) * hu * gate
    y = jnp.dot(hm.astype(BF16), wd_ref[...].astype(BF16), preferred_element_type=F32)

    @pl.when(e == 0)
    def _():
        o_ref[...] = y

    @pl.when(e > 0)
    def _():
        o_ref[...] += y


def _moe(xn, gates, wg, wu, wd):
    t = xn.shape[0]
    tm = min(t, 1024)
    return pl.pallas_call(
        _moe_kernel,
        grid=(t // tm, N_EXPERTS),
        in_specs=[pl.BlockSpec((tm, D_MODEL), lambda i, e: (i, 0)),
                  pl.BlockSpec((tm, LANES), lambda i, e: (i, 0)),
                  pl.BlockSpec((None, D_MODEL, D_EXPERT), lambda i, e: (e, 0, 0)),
                  pl.BlockSpec((None, D_MODEL, D_EXPERT), lambda i, e: (e, 0, 0)),
                  pl.BlockSpec((None, D_EXPERT, D_MODEL), lambda i, e: (e, 0, 0))],
        out_specs=pl.BlockSpec((tm, D_MODEL), lambda i, e: (i, 0)),
        out_shape=jax.ShapeDtypeStruct((t, D_MODEL), F32),
        compiler_params=_params("parallel", "arbitrary"),
        name="moe",
    )(xn, gates, wg, wu, wd)


MOE_TM = 512
POS_TM = 1024
INFO_G1, INFO_G2, INFO_E1, INFO_E2 = 0, 1, 2, 3
DMA_UNROLL = 8


def _moe_tiles(t):
    return (2 * t) // MOE_TM + N_EXPERTS


HALF = D_MODEL // 2
U32 = jnp.uint32


def _pack_rows(x):
    bits = lambda v: lax.bitcast_convert_type(v.astype(BF16).astype(F32), U32)
    return bits(x[:, HALF:]) | (bits(x[:, :HALF]) >> 16)


def _unpack_rows(w):
    lo = lax.bitcast_convert_type(w << 16, F32)
    hi = lax.bitcast_convert_type(w & jnp.uint32(0xFFFF0000), F32)
    return lo, hi


def _route_kernel(x_ref, oa_ref, od_ref, wo_ref, g_ref, wr_ref, h_ref, xn_ref, info_ref, cnt_ref, run_scr):
    h = _outproj(x_ref, oa_ref, od_ref, wo_ref)
    h_ref[...] = h
    xn = _rmsnorm(h, g_ref[...])
    xn_ref[...] = _pack_rows(xn)
    lane, i1, i2, g1, g2 = _route(xn.astype(BF16), wr_ref[...])
    info = jnp.where(lane == INFO_G1, g1, 0.0) + jnp.where(lane == INFO_G2, g2, 0.0)
    info = info + jnp.where(lane == INFO_E1, i1, 0.0) + jnp.where(lane == INFO_E2, i2, 0.0)
    info_ref[...] = info

    @pl.when(pl.program_id(0) == 0)
    def _():
        run_scr[...] = jnp.zeros(run_scr.shape, F32)
    picked = jnp.logical_or(lane == i1, lane == i2).astype(F32)
    run_scr[...] += jnp.sum(picked, axis=0, keepdims=True)
    cnt_ref[...] = run_scr[...]


def _route_sparse(x, oa, od, wo, g, wr):
    t = x.shape[0]
    tm = ROW_TM
    row = lambda n: pl.BlockSpec((tm, n), lambda i: (i, 0))
    full = lambda a: pl.BlockSpec(a.shape, lambda i: (0,) * a.ndim)
    return pl.pallas_call(
        _route_kernel,
        grid=(t // tm,),
        in_specs=[row(D_MODEL), row(ATT_WIDTH), row(DN_WIDTH), full(wo), full(g), full(wr)],
        out_specs=[row(D_MODEL), row(HALF), row(LANES), pl.BlockSpec((1, LANES), lambda i: (0, 0))],
        out_shape=[jax.ShapeDtypeStruct((t, D_MODEL), F32), jax.ShapeDtypeStruct((t, HALF), U32),
                   jax.ShapeDtypeStruct((t, LANES), F32), jax.ShapeDtypeStruct((1, LANES), F32)],
        scratch_shapes=[pltpu.VMEM((1, LANES), F32)],
        compiler_params=_params("arbitrary"),
        name="route",
    )(x, oa, od, wo, g, wr)


def _positions_kernel(info_ref, cnt_ref, ltri_ref, utri_ref, pos_ref, run_scr, off_scr):
    info = info_ref[...]
    lane = lax.broadcasted_iota(jnp.int32, info.shape, 1).astype(F32)
    hit1 = lane == info[:, INFO_E1:INFO_E1 + 1]
    hit2 = lane == info[:, INFO_E2:INFO_E2 + 1]
    onehot = jnp.logical_or(hit1, hit2).astype(F32)

    @pl.when(pl.program_id(0) == 0)
    def _():
        tiles = jnp.floor((cnt_ref[...] + (MOE_TM - 1)) * (1.0 / MOE_TM))
        off_scr[...] = MOE_TM * jnp.dot(tiles.astype(BF16), utri_ref[...], preferred_element_type=F32)
        run_scr[...] = jnp.zeros(run_scr.shape, F32)

    before = (jnp.dot(ltri_ref[...], onehot.astype(BF16), preferred_element_type=F32)
              + run_scr[...] + off_scr[...])
    pos1 = jnp.sum(jnp.where(hit1, before, 0.0), axis=-1, keepdims=True)
    pos2 = jnp.sum(jnp.where(hit2, before, 0.0), axis=-1, keepdims=True)
    pos_ref[...] = (jnp.where(lane == 0, pos1, 0.0) + jnp.where(lane == 1, pos2, 0.0)).astype(jnp.int32)
    run_scr[...] += jnp.sum(onehot, axis=0, keepdims=True)


def _positions(info, cnt):
    t = info.shape[0]
    tm = min(t, POS_TM)
    tok = np.arange(tm)
    ltri = jnp.asarray((tok[:, None] > tok[None, :]).astype(np.float32), dtype=BF16)
    ln = np.arange(LANES)
    utri = jnp.asarray((ln[:, None] < ln[None, :]).astype(np.float32), dtype=BF16)
    full = lambda a: pl.BlockSpec(a.shape, lambda i: (0,) * a.ndim)
    return pl.pallas_call(
        _positions_kernel,
        grid=(t // tm,),
        in_specs=[pl.BlockSpec((tm, LANES), lambda i: (i, 0)), full(cnt), full(ltri), full(utri)],
        out_specs=pl.BlockSpec((tm, LANES), lambda i: (i, 0)),
        out_shape=jax.ShapeDtypeStruct((t, LANES), jnp.int32),
        scratch_shapes=[pltpu.VMEM((1, LANES), F32), pltpu.VMEM((1, LANES), F32)],
        compiler_params=_params("arbitrary"),
        name="positions",
    )(info, cnt, ltri, utri)


def _row_copy(src_hbm, src_row, dst_hbm, dst_row, sem):
    return pltpu.make_async_copy(src_hbm.at[pl.ds(src_row, 1)], dst_hbm.at[pl.ds(dst_row, 1)], sem)


SCATTER_SLOTS = 3


def _scatter_kernel(pos1_ref, pos2_ref, last_ref, used_ref, nt_ref, xn_hbm, zero_hbm, xs_hbm,
                    buf, lsem, sem, zsem, *, n_tok):
    max_tiles = xs_hbm.shape[0] // MOE_TM

    def zero_tile(tile):
        return pltpu.make_async_copy(zero_hbm, xs_hbm.at[pl.ds(tile * MOE_TM, MOE_TM)], zsem)

    def for_unused(fn):
        def body(tile, carry):
            fn(tile)
            return carry
        lax.fori_loop(nt_ref[0], max_tiles, body, 0)

    for e in range(N_EXPERTS):
        @pl.when(used_ref[e] > 0)
        def _():
            zero_tile(last_ref[e]).start()
    for_unused(lambda tile: zero_tile(tile).start())
    for e in range(N_EXPERTS):
        @pl.when(used_ref[e] > 0)
        def _():
            zero_tile(last_ref[e]).wait()
    for_unused(lambda tile: zero_tile(tile).wait())

    tm = buf.shape[1]
    n = n_tok // tm

    def load(i):
        return pltpu.make_async_copy(xn_hbm.at[pl.ds(i * tm, tm)], buf.at[i % SCATTER_SLOTS],
                                     lsem.at[i % SCATTER_SLOTS])

    def wait_rows(slot):
        pltpu.make_async_copy(xs_hbm.at[pl.ds(0, 2 * tm)], xs_hbm.at[pl.ds(0, 2 * tm)], sem.at[slot]).wait()

    load(0).start()
    load(1).start()

    def step(i, carry):
        slot = i % SCATTER_SLOTS
        load(i).wait()

        def body(j, c2):
            tok = i * tm + j
            src = buf.at[slot, pl.ds(j, 1)]
            pltpu.make_async_copy(src, xs_hbm.at[pl.ds(pos1_ref[tok], 1)], sem.at[slot]).start()
            pltpu.make_async_copy(src, xs_hbm.at[pl.ds(pos2_ref[tok], 1)], sem.at[slot]).start()
            return c2
        lax.fori_loop(0, tm, body, 0, unroll=DMA_UNROLL)

        @pl.when(i >= 1)
        def _():
            wait_rows((i - 1) % SCATTER_SLOTS)

        @pl.when(i + 2 < n)
        def _():
            load(i + 2).start()
        return carry
    lax.fori_loop(0, n, step, 0)
    wait_rows((n - 1) % SCATTER_SLOTS)


def _scatter_rows(xn, pos1, pos2, last_tile, used, n_tiles, n_rows):
    t = xn.shape[0]
    zero = jnp.zeros((MOE_TM, D_MODEL), F32)
    any_spec = pl.BlockSpec(memory_space=pl.ANY)
    return pl.pallas_call(
        functools.partial(_scatter_kernel, n_tok=t),
        grid_spec=pltpu.PrefetchScalarGridSpec(
            num_scalar_prefetch=5, grid=(1,),
            in_specs=[any_spec, any_spec], out_specs=any_spec,
            scratch_shapes=[pltpu.VMEM((SCATTER_SLOTS, MOE_TM, D_MODEL), F32),
                            pltpu.SemaphoreType.DMA((SCATTER_SLOTS,)),
                            pltpu.SemaphoreType.DMA((SCATTER_SLOTS,)),
                            pltpu.SemaphoreType.DMA]),
        out_shape=jax.ShapeDtypeStruct((n_rows, D_MODEL), F32),
        compiler_params=_params("arbitrary"),
        name="scatter_rows",
    )(pos1, pos2, last_tile, used, n_tiles, xn, zero)


def _experts_kernel(te_ref, tv_ref, nt_ref, xs_ref, wg_ref, wu_ref, wd_ref, ys_ref, wg_s, wu_s, wd_s):
    i = pl.program_id(0)
    used = i < nt_ref[0]

    @pl.when(jnp.logical_or(i == 0, te_ref[i] != te_ref[jnp.maximum(i - 1, 0)]))
    def _():
        wg_s[...] = wg_ref[...].astype(BF16)
        wu_s[...] = wu_ref[...].astype(BF16)
        wd_s[...] = wd_ref[...].astype(BF16)

    @pl.when(used)
    def _():
        row = lax.broadcasted_iota(jnp.int32, xs_ref.shape, 0)
        x_lo, x_hi = _unpack_rows(jnp.where(row < tv_ref[i], xs_ref[...], jnp.uint32(0)))
        x_lo = x_lo.astype(BF16)
        x_hi = x_hi.astype(BF16)
        up = lambda w_s: (jnp.dot(x_lo, w_s[:HALF, :], preferred_element_type=F32)
                          + jnp.dot(x_hi, w_s[HALF:, :], preferred_element_type=F32))
        hm = (_silu(up(wg_s)) * up(wu_s)).astype(BF16)
        ys_ref[...] = _pack_rows(jnp.dot(hm, wd_s[...], preferred_element_type=F32))

    @pl.when(jnp.logical_not(used))
    def _():
        ys_ref[...] = jnp.zeros(ys_ref.shape, U32)


def _experts(xs, tile_expert, tile_valid, n_tiles, wg, wu, wd):
    max_tiles = xs.shape[0] // MOE_TM
    rows = pl.BlockSpec((MOE_TM, HALF), lambda i, te, tv, nt: (i, 0))
    wspec = lambda shape: pl.BlockSpec((None,) + shape, lambda i, te, tv, nt: (te[i], 0, 0))
    return pl.pallas_call(
        _experts_kernel,
        grid_spec=pltpu.PrefetchScalarGridSpec(
            num_scalar_prefetch=3, grid=(max_tiles,),
            in_specs=[rows, wspec((D_MODEL, D_EXPERT)), wspec((D_MODEL, D_EXPERT)),
                      wspec((D_EXPERT, D_MODEL))],
            out_specs=rows,
            scratch_shapes=[pltpu.VMEM((D_MODEL, D_EXPERT), BF16), pltpu.VMEM((D_MODEL, D_EXPERT), BF16),
                            pltpu.VMEM((D_EXPERT, D_MODEL), BF16)]),
        out_shape=jax.ShapeDtypeStruct(xs.shape, U32),
        compiler_params=_params("arbitrary"),
        name="experts",
    )(tile_expert, tile_valid, n_tiles, xs, wg, wu, wd)


def _ple_gather_kernel(pos1_ref, pos2_ref, h_ref, info_ref, p_ref, wpp_ref, wpg_ref, gp_ref, gf_ref,
                       ys_hbm, y_ref, ybuf, sem):
    i = pl.program_id(0)
    n = pl.num_programs(0)
    tm = h_ref.shape[0]

    def issue(tile, slot):
        def body(j, carry):
            tok = tile * tm + j
            pltpu.make_async_copy(ys_hbm.at[pl.ds(pos1_ref[tok], 1)], ybuf.at[slot, 0, pl.ds(j, 1)],
                                  sem.at[slot]).start()
            pltpu.make_async_copy(ys_hbm.at[pl.ds(pos2_ref[tok], 1)], ybuf.at[slot, 1, pl.ds(j, 1)],
                                  sem.at[slot]).start()
            return carry
        lax.fori_loop(0, tm, body, 0, unroll=DMA_UNROLL)

    @pl.when(i == 0)
    def _():
        issue(0, 0)

    @pl.when(i + 1 < n)
    def _():
        issue(i + 1, (i + 1) % 2)

    slot = i % 2
    pltpu.make_async_copy(ybuf.at[slot], ybuf.at[slot], sem.at[slot]).wait()
    info = info_ref[...]
    moe = info[:, INFO_G1:INFO_G1 + 1] * ybuf[slot, 0] + info[:, INFO_G2:INFO_G2 + 1] * ybuf[slot, 1]
    h = h_ref[...] + moe
    hn = _rmsnorm(h, gp_ref[...])
    h = h + _mm(p_ref[...], wpp_ref[...]) * _sigmoid(_mm(hn, wpg_ref[...]))
    y_ref[...] = _rmsnorm(h, gf_ref[...])


def _ple_gather(h, info, p, ys, pos1, pos2, wpp, wpg, gp, gf):
    t = h.shape[0]
    tm = 256
    row = lambda n: pl.BlockSpec((tm, n), lambda i, p1, p2: (i, 0))
    full = lambda a: pl.BlockSpec(a.shape, lambda i, p1, p2: (0,) * a.ndim)
    return pl.pallas_call(
        _ple_gather_kernel,
        grid_spec=pltpu.PrefetchScalarGridSpec(
            num_scalar_prefetch=2, grid=(t // tm,),
            in_specs=[row(D_MODEL), row(LANES), row(PLE_DIM), full(wpp), full(wpg), full(gp), full(gf),
                      pl.BlockSpec(memory_space=pl.ANY)],
            out_specs=row(D_MODEL),
            scratch_shapes=[pltpu.VMEM((2, 2, tm, D_MODEL), F32), pltpu.SemaphoreType.DMA((2,))]),
        out_shape=jax.ShapeDtypeStruct((t, D_MODEL), F32),
        compiler_params=_params("arbitrary"),
        name="ple_gather",
    )(pos1, pos2, h, info, p, wpp, wpg, gp, gf, ys)


SC_IDX = 128
SC_ROWS = 64
SC_WORKERS = 32


def _sc_mesh():
    return plsc.VectorSubcoreMesh(core_axis_name="c", subcore_axis_name="s")


def _sc_windows(t, fn):
    per_worker = t // SC_WORKERS
    worker = lax.axis_index(("c", "s"))

    @pl.loop(0, per_worker // SC_IDX)
    def _(w):
        fn(worker * per_worker + w * SC_IDX)


def _sc_scatter_rows(xn, pos1, pos2, n_rows):
    t, d = xn.shape
    assert t % (SC_WORKERS * SC_IDX) == 0
    idx_t = pltpu.VMEM((1, SC_IDX), jnp.int32)

    @pl.kernel(out_type=jax.ShapeDtypeStruct((n_rows, d), xn.dtype), mesh=_sc_mesh(),
               scratch_types=[idx_t, idx_t, pltpu.VMEM((SC_ROWS, d), xn.dtype)])
    def scatter(x_hbm, p1_hbm, p2_hbm, o_hbm, i1_v, i2_v, buf):
        def window(base):
            pltpu.sync_copy(p1_hbm.at[:, pl.ds(base, SC_IDX)], i1_v)
            pltpu.sync_copy(p2_hbm.at[:, pl.ds(base, SC_IDX)], i2_v)
            for k in range(SC_IDX // SC_ROWS):
                pltpu.sync_copy(x_hbm.at[pl.ds(base + k * SC_ROWS, SC_ROWS)], buf)
                pltpu.sync_copy(buf, o_hbm.at[i1_v.at[0, pl.ds(k * SC_ROWS, SC_ROWS)]])
                pltpu.sync_copy(buf, o_hbm.at[i2_v.at[0, pl.ds(k * SC_ROWS, SC_ROWS)]])
        _sc_windows(t, window)

    return scatter(xn, pos1.reshape(1, t), pos2.reshape(1, t))


def _sc_gather_rows(ys, pos1, pos2):
    t = pos1.shape[0]
    d = ys.shape[1]
    assert t % (SC_WORKERS * SC_IDX) == 0
    idx_t = pltpu.VMEM((1, SC_IDX), jnp.int32)
    out = jax.ShapeDtypeStruct((t, d), ys.dtype)

    @pl.kernel(out_type=(out, out), mesh=_sc_mesh(),
               scratch_types=[idx_t, idx_t, pltpu.VMEM((SC_ROWS, d), ys.dtype)])
    def gather(y_hbm, p1_hbm, p2_hbm, o1_hbm, o2_hbm, i1_v, i2_v, buf):
        def window(base):
            pltpu.sync_copy(p1_hbm.at[:, pl.ds(base, SC_IDX)], i1_v)
            pltpu.sync_copy(p2_hbm.at[:, pl.ds(base, SC_IDX)], i2_v)
            for k in range(SC_IDX // SC_ROWS):
                rows = pl.ds(base + k * SC_ROWS, SC_ROWS)
                pltpu.sync_copy(y_hbm.at[i1_v.at[0, pl.ds(k * SC_ROWS, SC_ROWS)]], buf)
                pltpu.sync_copy(buf, o1_hbm.at[rows])
                pltpu.sync_copy(y_hbm.at[i2_v.at[0, pl.ds(k * SC_ROWS, SC_ROWS)]], buf)
                pltpu.sync_copy(buf, o2_hbm.at[rows])
        _sc_windows(t, window)

    return gather(ys, pos1.reshape(1, t), pos2.reshape(1, t))


def _ple_sparse_kernel(h_ref, info_ref, y1_ref, y2_ref, p_ref, wpp_ref, wpg_ref, gp_ref, gf_ref, y_ref):
    info = info_ref[...]
    g1 = info[:, INFO_G1:INFO_G1 + 1]
    g2 = info[:, INFO_G2:INFO_G2 + 1]
    y1_lo, y1_hi = _unpack_rows(y1_ref[...])
    y2_lo, y2_hi = _unpack_rows(y2_ref[...])
    moe = jnp.concatenate([g1 * y1_lo + g2 * y2_lo, g1 * y1_hi + g2 * y2_hi], axis=1)
    h = h_ref[...] + moe
    hn = _rmsnorm(h, gp_ref[...])
    h = h + _mm(p_ref[...], wpp_ref[...]) * _sigmoid(_mm(hn, wpg_ref[...]))
    y_ref[...] = _rmsnorm(h, gf_ref[...])


def _ple_sparse(h, info, y1, y2, p, wpp, wpg, gp, gf):
    t = h.shape[0]
    tm = ROW_TM
    row = lambda n: pl.BlockSpec((tm, n), lambda i: (i, 0))
    full = lambda a: pl.BlockSpec(a.shape, lambda i: (0,) * a.ndim)
    return pl.pallas_call(
        _ple_sparse_kernel,
        grid=(t // tm,),
        in_specs=[row(D_MODEL), row(LANES), row(HALF), row(HALF), row(PLE_DIM),
                  full(wpp), full(wpg), full(gp), full(gf)],
        out_specs=row(D_MODEL),
        out_shape=jax.ShapeDtypeStruct((t, D_MODEL), F32),
        compiler_params=_params("parallel"),
        name="ple_sparse",
    )(h, info, y1, y2, p, wpp, wpg, gp, gf)


def _tile_tables(cnt, max_tiles):
    tiles_e = (cnt + (MOE_TM - 1)) // MOE_TM
    ends = jnp.cumsum(tiles_e)
    n_tiles = ends[-1]
    tile = jnp.arange(max_tiles, dtype=jnp.int32)
    idx = jnp.minimum(tile, n_tiles - 1)
    tile_expert = jnp.sum((idx[:, None] >= ends[None, :]).astype(jnp.int32), axis=1)
    mine = tile_expert[:, None] == jnp.arange(N_EXPERTS, dtype=jnp.int32)[None, :]
    of_mine = lambda v: jnp.sum(jnp.where(mine, v[None, :], 0), axis=1)
    valid = jnp.clip(of_mine(cnt) - (idx - of_mine(ends - tiles_e)) * MOE_TM, 0, MOE_TM)
    tile_valid = jnp.where(tile < n_tiles, valid, 0).astype(jnp.int32)
    return (tile_expert, tile_valid, n_tiles.reshape(1), (ends - 1).astype(jnp.int32),
            tiles_e.astype(jnp.int32))


def _ple_final_kernel(h_ref, m_ref, p_ref, wpp_ref, wpg_ref, gp_ref, gf_ref, y_ref):
    h = h_ref[...] + m_ref[...]
    hn = _rmsnorm(h, gp_ref[...])
    h = h + _mm(p_ref[...], wpp_ref[...]) * _sigmoid(_mm(hn, wpg_ref[...]))
    y_ref[...] = _rmsnorm(h, gf_ref[...])


def _ple_final(h, m, p, wpp, wpg, gp, gf):
    t = h.shape[0]
    tm = min(t, 256)
    row = lambda n: pl.BlockSpec((tm, n), lambda i: (i, 0))
    full = lambda a: pl.BlockSpec(a.shape, lambda i: (0,) * a.ndim)
    return pl.pallas_call(
        _ple_final_kernel,
        grid=(t // tm,),
        in_specs=[row(D_MODEL), row(D_MODEL), row(PLE_DIM), full(wpp), full(wpg), full(gp), full(gf)],
        out_specs=row(D_MODEL),
        out_shape=jax.ShapeDtypeStruct((t, D_MODEL), F32),
        compiler_params=_params("parallel"),
        name="ple_final",
    )(h, m, p, wpp, wpg, gp, gf)


def kernel(x_prompt, x_sample, p_prompt, p_sample, cache_k, cache_v, state_conv, state_S, rel_bias, norm_mix, w_in, att_sink, conv_w, dn_A_log, dn_dt_bias, dn_norm, w_out, norm_ffn, w_router_group, w_router_expert, w_gate, w_up, w_down, w_ple_proj, w_ple_gate, norm_ple, norm_final):
    batch, seq, _ = x_prompt.shape
    nseq = x_sample.shape[0]
    assert x_sample.shape[1] == 1 and norm_mix.shape[0] == 1 and cache_k.shape[2] == WINDOW
    assert seq % GDN_TB == 0 and seq % ATT_BLOCK == 0

    wi = w_in[0]
    o_db = ATT_COLS + CONV_CH
    w_in_re = jnp.concatenate(
        [wi[:, :o_db], wi[:, o_db + 2 * DN_HEADS:], wi[:, o_db:o_db + 2 * DN_HEADS],
         jnp.zeros((D_MODEL, LANES - 2 * DN_HEADS), F32)], axis=1).astype(BF16)
    row = lambda a: a.reshape(1, -1).astype(F32)
    pad_lanes = lambda a, off: jnp.zeros((1, LANES), F32).at[0, off:off + a.shape[0]].set(a)
    alog = pad_lanes(dn_A_log[0], DN_HEADS)
    dtb = pad_lanes(dn_dt_bias[0], DN_HEADS)
    dnx = jnp.tile(dn_norm[0], DN_HEADS).reshape(1, DN_WIDTH)
    w_router = jnp.concatenate(
        [w_router_group[0], w_router_expert[0],
         jnp.zeros((D_MODEL, LANES - N_GROUPS - N_EXPERTS), F32)], axis=1).astype(BF16)
    wo = w_out[0].astype(BF16)
    wg, wu, wd = w_gate[0], w_up[0], w_down[0]
    wpp, wpg = w_ple_proj[0].astype(BF16), w_ple_gate[0].astype(BF16)
    sink = att_sink[0]

    qi = np.arange(ATT_BLOCK)[:, None]
    kj = np.arange(2 * ATT_BLOCK)[None, :]
    bucket_p = jnp.asarray(_t5_bucket_np(qi + ATT_BLOCK - kj))
    bucket_s = jnp.asarray(_t5_bucket_np(WINDOW - np.arange(WINDOW)[None, :]))

    def tail(x, o_att, o_dn, p):
        h1, xn2, gates = _outproj_router(x, o_att, o_dn, wo, row(norm_ffn[0]), w_router)
        moe = _moe(xn2, gates, wg, wu, wd)
        return _ple_final(h1, moe, p, wpp, wpg, row(norm_ple[0]), row(norm_final))

    xp = x_prompt.reshape(batch * seq, D_MODEL)
    att_p, qkv_p, dz_p, ba_p, xc_tails = _inproj_conv(xp, row(norm_mix[0]), w_in_re, conv_w[0], seq)
    o_att_p = _attn_prompt(att_p, bucket_p, rel_bias, sink, batch, seq)
    o_dn_p, s_p = _gdn_prompt(qkv_p, dz_p, ba_p, alog, dtb, dnx, batch, seq)
    h1, xn2, info, cnt = _route_sparse(xp, o_att_p, o_dn_p, wo, row(norm_ffn[0]), w_router)
    pos = _positions(info, cnt)
    pos1, pos2 = pos[:, 0], pos[:, 1]
    max_tiles = _moe_tiles(batch * seq)
    cnt_e = cnt[0, ROUTER_OFF:ROUTER_OFF + N_EXPERTS].astype(jnp.int32)
    tile_expert, tile_valid, n_tiles, last_tile, used = _tile_tables(cnt_e, max_tiles)
    xs = _sc_scatter_rows(xn2, pos1, pos2, max_tiles * MOE_TM)
    ys = _experts(xs, tile_expert, tile_valid, n_tiles, wg, wu, wd)
    y1, y2 = _sc_gather_rows(ys, pos1, pos2)
    y_p = _ple_sparse(h1, info, y1, y2, p_prompt[0].reshape(batch * seq, PLE_DIM),
                      wpp, wpg, row(norm_ple[0]), row(norm_final))

    xs = x_sample.reshape(nseq, D_MODEL)
    att_s, xc_s, dz_s, ba_s = _inproj(xs, row(norm_mix[0]), w_in_re)
    ck = cache_k[0].reshape(nseq, WINDOW, KV_WIDTH)
    cv = cache_v[0].reshape(nseq, WINDOW, KV_WIDTH)
    o_att_s = _attn_sample(att_s, ck, cv, bucket_s, rel_bias, sink)
    sconv_t = jnp.swapaxes(state_conv[0], 0, 1)
    o_dn_s_t, s_s_t = _gdn_sample_lanes(xc_s, dz_s, ba_s, sconv_t, jnp.transpose(state_S[0], (1, 2, 3, 0)),
                                        conv_w[0], alog, dtb, dn_norm[0])
    s_s = jnp.transpose(s_s_t, (3, 0, 1, 2))
    y_s = tail(xs, o_att_s, o_dn_s_t, p_sample[0].reshape(nseq, PLE_DIM))

    att_p3 = att_p.reshape(batch, seq, ATT_COLS)
    kv_shape = (1, batch, WINDOW, ATT_KV_HEADS, HEAD_DIM)
    k_p = att_p3[:, seq - WINDOW:, ATT_WIDTH:ATT_WIDTH + KV_WIDTH].reshape(kv_shape)
    v_p = att_p3[:, seq - WINDOW:, ATT_WIDTH + KV_WIDTH:].reshape(kv_shape)
    conv_p = xc_tails.reshape(batch, -1, TAIL, CONV_CH)[:, -1, TAIL - (CONV_WIDTH - 1):][None]
    k_new = att_s[:, None, ATT_WIDTH:ATT_WIDTH + KV_WIDTH]
    v_new = att_s[:, None, ATT_WIDTH + KV_WIDTH:]
    kv_s_shape = (1, nseq, WINDOW, ATT_KV_HEADS, HEAD_DIM)
    k_s = jnp.concatenate([ck[:, 1:], k_new], axis=1).reshape(kv_s_shape)
    v_s = jnp.concatenate([cv[:, 1:], v_new], axis=1).reshape(kv_s_shape)
    conv_s = jnp.concatenate([state_conv[0][:, 1:], xc_s[:, None, :]], axis=1)[None]
    return (y_p.reshape(batch, seq, D_MODEL), y_s.reshape(nseq, 1, D_MODEL),
            k_p, v_p, conv_p, s_p[None], k_s, v_s, conv_s, s_s[None])
```

```python
import functools
import math

import numpy as np
import jax
import jax.numpy as jnp
from jax import lax
from jax.experimental import pallas as pl
from jax.experimental.pallas import tpu as pltpu
from jax.experimental.pallas import tpu_sc as plsc

F32 = jnp.float32
BF16 = jnp.bfloat16

D_MODEL = 1024
ATT_HEADS = 8
ATT_KV_HEADS = 2
HEAD_DIM = 64
GQA = ATT_HEADS // ATT_KV_HEADS
WINDOW = 128
ATT_BLOCK = 128
N_BUCKETS = 32
DN_HEADS = 8
DN_DK = 64
DN_DV = 64
CONV_WIDTH = 4
DN_CHUNK = 64
ATT_WIDTH = ATT_HEADS * HEAD_DIM
KV_WIDTH = ATT_KV_HEADS * HEAD_DIM
DN_WIDTH = DN_HEADS * DN_DV
CONV_CH = 3 * DN_WIDTH
N_GROUPS = 4
EXPERTS_PER_GROUP = 8
N_EXPERTS = N_GROUPS * EXPERTS_PER_GROUP
D_EXPERT = 256
PLE_DIM = 256
EPS = 1e-6
NEG_INF = float("-inf")

ATT_COLS = ATT_WIDTH + 2 * KV_WIDTH
LANES = 128
IN_COLS = ATT_COLS + CONV_CH + DN_WIDTH + LANES
ROUTER_OFF = N_GROUPS
VMEM_LIMIT = 48 * 1024 * 1024
ROW_TM = 512


def _params(*sem):
    return pltpu.CompilerParams(dimension_semantics=sem, vmem_limit_bytes=VMEM_LIMIT)


def _mm(a, b):
    return jnp.dot(a.astype(BF16), b.astype(BF16), preferred_element_type=F32)


def _mm_nt(a, b):
    return lax.dot_general(a.astype(BF16), b.astype(BF16), (((1,), (1,)), ((), ())),
                           preferred_element_type=F32)


def _mm_tn(a, b):
    return lax.dot_general(a.astype(BF16), b.astype(BF16), (((0,), (0,)), ((), ())),
                           preferred_element_type=F32)


def _split3(x):
    h1 = x.astype(BF16)
    r1 = x - h1.astype(F32)
    h2 = r1.astype(BF16)
    h3 = (r1 - h2.astype(F32)).astype(BF16)
    return h1, h2, h3


def _mm_sel_rhs(x, sel):
    h1, h2, h3 = _split3(x)
    d = lambda h: jnp.dot(h, sel, preferred_element_type=F32)
    return d(h1) + d(h2) + d(h3)


def _mm_sel_lhs(sel, x):
    h1, h2, h3 = _split3(x)
    d = lambda h: jnp.dot(sel, h, preferred_element_type=F32)
    return d(h1) + d(h2) + d(h3)


def _mm3(a, b):
    ah = a.astype(BF16)
    al = (a - ah.astype(F32)).astype(BF16)
    bh = b.astype(BF16)
    bl = (b - bh.astype(F32)).astype(BF16)
    d = lambda u, v: jnp.dot(u, v, preferred_element_type=F32)
    return d(ah, bh) + d(ah, bl) + d(al, bh)


def _sigmoid(x):
    return 1.0 / (1.0 + jnp.exp(-x))


def _silu(x):
    return x * _sigmoid(x)


def _softplus(x):
    return jnp.maximum(x, 0.0) + jnp.log1p(jnp.exp(-jnp.abs(x)))


def _rmsnorm(x, g):
    return x * lax.rsqrt(jnp.mean(x * x, axis=-1, keepdims=True) + EPS) * g


def _t5_bucket_np(dist):
    max_exact = N_BUCKETS // 2
    d = np.maximum(dist, 0)
    ratio = (np.log(np.maximum(d, 1).astype(np.float32) / np.float32(max_exact))
             / np.float32(math.log(WINDOW / max_exact))).astype(np.float32)
    large = np.minimum(max_exact + (ratio * np.float32(N_BUCKETS - max_exact)).astype(np.int32),
                       N_BUCKETS - 1)
    return np.where(d < max_exact, d, large).astype(np.int32)


def _bias_lookup(bucket, rb_ref, h):
    acc = jnp.zeros(bucket.shape, F32)
    for t in range(N_BUCKETS):
        acc = jnp.where(bucket == t, rb_ref[t, h], acc)
    return acc


def _inproj_kernel(x_ref, g_ref, w_ref, att_ref, xc_ref, dz_ref, ba_ref):
    xn = _rmsnorm(x_ref[...], g_ref[...]).astype(BF16)
    o0, o1, o2 = ATT_COLS, ATT_COLS + CONV_CH, ATT_COLS + CONV_CH + DN_WIDTH
    att_ref[...] = jnp.dot(xn, w_ref[:, :o0], preferred_element_type=F32)
    xc_ref[...] = jnp.dot(xn, w_ref[:, o0:o1], preferred_element_type=F32)
    dz_ref[...] = jnp.dot(xn, w_ref[:, o1:o2], preferred_element_type=F32)
    ba_ref[...] = jnp.dot(xn, w_ref[:, o2:], preferred_element_type=F32)


def _inproj(x, g, w):
    t = x.shape[0]
    tm = min(t, ROW_TM)
    row = lambda n: pl.BlockSpec((tm, n), lambda i: (i, 0))
    full = lambda a: pl.BlockSpec(a.shape, lambda i: (0,) * a.ndim)
    return pl.pallas_call(
        _inproj_kernel,
        grid=(t // tm,),
        in_specs=[row(D_MODEL), full(g), full(w)],
        out_specs=[row(ATT_COLS), row(CONV_CH), row(DN_WIDTH), row(LANES)],
        out_shape=[jax.ShapeDtypeStruct((t, n), F32) for n in (ATT_COLS, CONV_CH, DN_WIDTH, LANES)],
        compiler_params=_params("parallel"),
        name="inproj",
    )(x, g, w)


TAIL = 8
PAIR = 2 * DN_DK
N_PAIRS = DN_WIDTH // PAIR


def _head_sums(z, pair_ones):
    hi = z.astype(BF16)
    lw = (z - hi.astype(F32)).astype(BF16)
    d = lambda a, p: jnp.dot(a[:, p * PAIR:(p + 1) * PAIR], pair_ones, preferred_element_type=F32)
    return jnp.concatenate([d(hi, p) + d(lw, p) for p in range(N_PAIRS)], axis=1)


def _inproj_conv_kernel(x_ref, g_ref, w_ref, cw_ref, ones_ref, att_ref, qkv_ref, dz_ref, ba_ref, tail_ref,
                        xp_scr, *, tiles_per_seq):
    tm = x_ref.shape[0]

    @pl.when(pl.program_id(0) % tiles_per_seq == 0)
    def _():
        xp_scr[0:TAIL, :] = jnp.zeros((TAIL, CONV_CH), F32)

    xn = _rmsnorm(x_ref[...], g_ref[...]).astype(BF16)
    o0, o1, o2 = ATT_COLS, ATT_COLS + CONV_CH, ATT_COLS + CONV_CH + DN_WIDTH
    xc = jnp.dot(xn, w_ref[:, o0:o1], preferred_element_type=F32)
    att_ref[...] = jnp.dot(xn, w_ref[:, :o0], preferred_element_type=F32)
    dz_ref[...] = jnp.dot(xn, w_ref[:, o1:o2], preferred_element_type=F32)
    ba_ref[...] = jnp.dot(xn, w_ref[:, o2:], preferred_element_type=F32)

    xp_scr[TAIL:, :] = xc
    y = xp_scr[TAIL - 3:TAIL - 3 + tm, :] * cw_ref[0:1, :]
    y = y + xp_scr[TAIL - 2:TAIL - 2 + tm, :] * cw_ref[1:2, :]
    y = y + xp_scr[TAIL - 1:TAIL - 1 + tm, :] * cw_ref[2:3, :]
    y = y + xc * cw_ref[3:4, :]
    tail = xc[tm - TAIL:, :]
    xp_scr[0:TAIL, :] = tail
    tail_ref[0] = tail
    y = _silu(y)
    q = y[:, :DN_WIDTH]
    k = y[:, DN_WIDTH:2 * DN_WIDTH]
    inv_norm = lax.rsqrt(_head_sums(jnp.concatenate([q * q, k * k], axis=0), ones_ref[...]) + EPS)
    qkv_ref[:, :DN_WIDTH] = q * inv_norm[:tm] * (DN_DK ** -0.5)
    qkv_ref[:, DN_WIDTH:2 * DN_WIDTH] = k * inv_norm[tm:]
    qkv_ref[:, 2 * DN_WIDTH:] = y[:, 2 * DN_WIDTH:]


def _pair_ones():
    lane = np.arange(PAIR)
    return jnp.asarray((lane[:, None] // DN_DV == lane[None, :] // DN_DV).astype(np.float32), dtype=BF16)


def _inproj_conv(x, g, w, conv_w, seq):
    t = x.shape[0]
    tm = ROW_TM
    assert seq % tm == 0
    ones = _pair_ones()
    row = lambda n: pl.BlockSpec((tm, n), lambda i: (i, 0))
    full = lambda a: pl.BlockSpec(a.shape, lambda i: (0,) * a.ndim)
    return pl.pallas_call(
        functools.partial(_inproj_conv_kernel, tiles_per_seq=seq // tm),
        grid=(t // tm,),
        in_specs=[row(D_MODEL), full(g), full(w), full(conv_w), full(ones)],
        out_specs=[row(ATT_COLS), row(CONV_CH), row(DN_WIDTH), row(LANES),
                   pl.BlockSpec((1, TAIL, CONV_CH), lambda i: (i, 0, 0))],
        out_shape=[jax.ShapeDtypeStruct((t, n), F32) for n in (ATT_COLS, CONV_CH, DN_WIDTH, LANES)]
                  + [jax.ShapeDtypeStruct((t // tm, TAIL, CONV_CH), F32)],
        scratch_shapes=[pltpu.VMEM((TAIL + tm, CONV_CH), F32)],
        compiler_params=_params("arbitrary"),
        name="inproj_conv",
    )(x, g, w, conv_w, ones)


GROUP_ROWS = GQA * ATT_BLOCK


def _attn_prompt_kernel(cur_ref, prev_ref, bucket_ref, rb_ref, sink_ref, o_ref, bias_scr, sink_scr):
    i = pl.program_id(0)
    nseq = cur_ref.shape[0]

    @pl.when(i == 0)
    def _():
        qi = lax.broadcasted_iota(jnp.int32, (ATT_BLOCK, 2 * ATT_BLOCK), 0)
        kj = lax.broadcasted_iota(jnp.int32, (ATT_BLOCK, 2 * ATT_BLOCK), 1)
        dist = qi + ATT_BLOCK - kj
        band = jnp.logical_and(dist >= 0, dist < WINDOW)
        bucket = bucket_ref[...]
        hrow = lax.broadcasted_iota(jnp.int32, (GROUP_ROWS, 1), 0) // ATT_BLOCK
        for g in range(ATT_KV_HEADS):
            sink_col = jnp.zeros((GROUP_ROWS, 1), F32)
            for hh in range(GQA):
                h = g * GQA + hh
                bias = jnp.where(band, _bias_lookup(bucket, rb_ref, h), NEG_INF)
                bias_scr[0, g, hh * ATT_BLOCK:(hh + 1) * ATT_BLOCK, :] = bias
                bias_scr[1, g, hh * ATT_BLOCK:(hh + 1) * ATT_BLOCK, :] = jnp.where(kj >= ATT_BLOCK, bias, NEG_INF)
                sink_col = jnp.where(hrow == hh, sink_ref[h], sink_col)
            sink_scr[g] = sink_col

    first = (i == 0).astype(jnp.int32)
    probs = [(b, g) for b in range(nseq) for g in range(ATT_KV_HEADS)]
    scores = []
    for b, g in probs:
        cur = cur_ref[b]
        prev = prev_ref[b]
        q = jnp.concatenate([cur[:, (g * GQA + hh) * HEAD_DIM:(g * GQA + hh + 1) * HEAD_DIM]
                             for hh in range(GQA)], axis=0) * (HEAD_DIM ** -0.5)
        kcol = slice(ATT_WIDTH + g * HEAD_DIM, ATT_WIDTH + (g + 1) * HEAD_DIM)
        k2 = jnp.concatenate([prev[:, kcol], cur[:, kcol]], axis=0)
        scores.append(_mm_nt(q, k2) + bias_scr[first, g])
    probs_p, dens = [], []
    for (b, g), s in zip(probs, scores):
        sink = sink_scr[g]
        m = jnp.maximum(jnp.max(s, axis=-1, keepdims=True), sink)
        p = jnp.exp(s - m)
        dens.append(jnp.sum(p, axis=-1, keepdims=True) + jnp.exp(sink - m))
        probs_p.append(p)
    outs = {}
    for (b, g), p, den in zip(probs, probs_p, dens):
        vcol = slice(ATT_WIDTH + KV_WIDTH + g * HEAD_DIM, ATT_WIDTH + KV_WIDTH + (g + 1) * HEAD_DIM)
        v2 = jnp.concatenate([prev_ref[b][:, vcol], cur_ref[b][:, vcol]], axis=0)
        outs[b, g] = _mm(p, v2) / den
    for b in range(nseq):
        o_ref[b] = jnp.concatenate([outs[b, g][hh * ATT_BLOCK:(hh + 1) * ATT_BLOCK, :]
                                    for g in range(ATT_KV_HEADS) for hh in range(GQA)], axis=1)


def _attn_prompt(att, bucket, rel_bias, sink, batch, seq):
    nb = seq // ATT_BLOCK
    smem = pl.BlockSpec(memory_space=pltpu.SMEM)
    att3 = att.reshape(batch, seq, ATT_COLS)
    out = pl.pallas_call(
        _attn_prompt_kernel,
        grid=(nb,),
        in_specs=[
            pl.BlockSpec((batch, ATT_BLOCK, ATT_COLS), lambda i: (0, i, 0)),
            pl.BlockSpec((batch, ATT_BLOCK, ATT_COLS), lambda i: (0, jnp.maximum(i - 1, 0), 0)),
            pl.BlockSpec(bucket.shape, lambda i: (0, 0)),
            smem, smem,
        ],
        out_specs=pl.BlockSpec((batch, ATT_BLOCK, ATT_WIDTH), lambda i: (0, i, 0)),
        out_shape=jax.ShapeDtypeStruct((batch, seq, ATT_WIDTH), F32),
        scratch_shapes=[pltpu.VMEM((2, ATT_KV_HEADS, GROUP_ROWS, 2 * ATT_BLOCK), F32),
                        pltpu.VMEM((ATT_KV_HEADS, GROUP_ROWS, 1), F32)],
        compiler_params=_params("arbitrary"),
        name="attn_prompt",
    )(att3, att3, bucket, rel_bias, sink)
    return out.reshape(batch * seq, ATT_WIDTH)


ATT_S_BB = 8


def _attn_sample_kernel(att_ref, ck_ref, cv_ref, bucket_ref, rb_ref, sink_ref, o_ref,
                        bias_scr, col_scr):
    hrow = lax.broadcasted_iota(jnp.int32, (ATT_HEADS, LANES), 0)
    lane = lax.broadcasted_iota(jnp.int32, (ATT_HEADS, LANES), 1)

    @pl.when(pl.program_id(0) == 0)
    def _():
        bucket = jnp.broadcast_to(bucket_ref[...], (ATT_HEADS, LANES))
        bias = jnp.zeros((ATT_HEADS, LANES), F32)
        cols = jnp.zeros((ATT_HEADS, LANES), F32)
        for h in range(ATT_HEADS):
            bias = jnp.where(hrow == h, _bias_lookup(bucket, rb_ref, h), bias)
            cols = jnp.where(jnp.logical_and(hrow == h, lane == 0), sink_ref[h], cols)
            cols = jnp.where(jnp.logical_and(hrow == h, lane == 1), rb_ref[0, h], cols)
        bias_scr[...] = jnp.where(lane >= 1, bias, NEG_INF)
        col_scr[...] = cols

    bias_c = bias_scr[...]
    sink = col_scr[:, 0:1]
    bias_n = col_scr[:, 1:2]
    same_group = (hrow // GQA) == (lane // HEAD_DIM)
    low_group = lax.broadcasted_iota(jnp.int32, (ATT_HEADS, HEAD_DIM), 0) < GQA
    rnd = lambda a: a.astype(BF16).astype(F32)
    seqs = range(ATT_S_BB)
    rows = [att_ref[b:b + 1, :] for b in seqs]
    q_bds = []
    for row in rows:
        q = row[:, :ATT_WIDTH] * (HEAD_DIM ** -0.5)
        qh = jnp.concatenate([q[:, h * HEAD_DIM:(h + 1) * HEAD_DIM] for h in range(ATT_HEADS)], axis=0)
        q_bds.append(jnp.where(same_group, jnp.concatenate([qh, qh], axis=1), 0.0))
    s_cs = [_mm_nt(q_bd, ck_ref[b]) + bias_c for b, q_bd in zip(seqs, q_bds)]
    prs, pns = [], []
    for row, q_bd, s_c in zip(rows, q_bds, s_cs):
        kn = row[:, ATT_WIDTH:ATT_WIDTH + KV_WIDTH]
        s_n = jnp.sum(rnd(q_bd) * rnd(kn), axis=-1, keepdims=True) + bias_n
        m = jnp.maximum(jnp.maximum(jnp.max(s_c, axis=-1, keepdims=True), s_n), sink)
        p_c = jnp.exp(s_c - m)
        p_n = jnp.exp(s_n - m)
        den = jnp.sum(p_c, axis=-1, keepdims=True) + p_n + jnp.exp(sink - m)
        prs.append(p_c / den)
        pns.append(p_n / den)
    pvs = [_mm(pr, cv_ref[b]) for b, pr in zip(seqs, prs)]
    for b, row, pv, pn in zip(seqs, rows, pvs, pns):
        vn = row[:, ATT_WIDTH + KV_WIDTH:]
        o_full = pv + rnd(pn) * rnd(vn)
        o_sel = jnp.where(low_group, o_full[:, :HEAD_DIM], o_full[:, HEAD_DIM:])
        o_ref[b:b + 1, :] = jnp.concatenate([o_sel[h:h + 1, :] for h in range(ATT_HEADS)], axis=1)


def _attn_sample(att, ck, cv, bucket, rel_bias, sink):
    nseq = att.shape[0]
    smem = pl.BlockSpec(memory_space=pltpu.SMEM)
    cache = pl.BlockSpec((ATT_S_BB, WINDOW, KV_WIDTH), lambda i: (i, 0, 0))
    return pl.pallas_call(
        _attn_sample_kernel,
        grid=(nseq // ATT_S_BB,),
        in_specs=[pl.BlockSpec((ATT_S_BB, ATT_COLS), lambda i: (i, 0)), cache, cache,
                  pl.BlockSpec(bucket.shape, lambda i: (0, 0)), smem, smem],
        out_specs=pl.BlockSpec((ATT_S_BB, ATT_WIDTH), lambda i: (i, 0)),
        out_shape=jax.ShapeDtypeStruct((nseq, ATT_WIDTH), F32),
        scratch_shapes=[pltpu.VMEM((ATT_HEADS, LANES), F32), pltpu.VMEM((ATT_HEADS, LANES), F32)],
        compiler_params=_params("arbitrary"),
        name="attn_sample",
    )(att, ck, cv, bucket, rel_bias, sink)


GDN_TB = 128
GDN_NC = GDN_TB // DN_CHUNK


def _gdn_gates(ba, alog, dtb):
    beta = _sigmoid(ba)
    g = -jnp.exp(alog) * _softplus(ba + dtb)
    return beta, g


def _pair_diag(x, lo):
    xb = x.astype(BF16)
    zero = jnp.zeros_like(xb)
    return jnp.concatenate([jnp.where(lo, xb, zero), jnp.where(lo, zero, xb)], axis=0)


def _gdn_prompt_kernel(qkv_ref, dz_ref, ba_ref, alog_ref, dtb_ref, dnx_ref,
                       hsum_ref, expb_ref, expg_ref, ltri_ref,
                       o_ref, s_out_ref, s_scr):
    i = pl.program_id(0)
    nb = qkv_ref.shape[0]

    @pl.when(i == 0)
    def _():
        s_scr[...] = jnp.zeros(s_scr.shape, F32)

    hsum = hsum_ref[...]
    ri = lax.broadcasted_iota(jnp.int32, (DN_CHUNK, PAIR), 0)
    ci = lax.broadcasted_iota(jnp.int32, (DN_CHUNK, PAIR), 1)
    lo = ci < DN_DK
    cj = jnp.where(lo, ci, ci - DN_DK)
    causal = ri >= cj
    strict = ri > cj
    eye = (ri == cj).astype(F32)

    def sel2(x, m):
        hi = x.astype(BF16)
        lw = (x - hi.astype(F32)).astype(BF16)
        return (jnp.dot(hi, m, preferred_element_type=F32) + jnp.dot(lw, m, preferred_element_type=F32))

    pre = []
    for b in range(nb):
        q = qkv_ref[b, :, :DN_WIDTH]
        k = qkv_ref[b, :, DN_WIDTH:2 * DN_WIDTH]
        v = qkv_ref[b, :, 2 * DN_WIDTH:]
        beta_c, g_c = _gdn_gates(ba_ref[b], alog_ref[...], dtb_ref[...])
        beta = sel2(beta_c, expb_ref[...])
        gam_c = _mm_sel_lhs(ltri_ref[...], g_c)
        gam = _mm_sel_rhs(gam_c, expg_ref[...])
        gam_t = gam_c.T
        kb = k * beta
        egam = jnp.exp(gam)
        pre.append(dict(q=q, k=k, kb=kb, vb=v * beta, qg=q * egam, wr=kb * egam, gam=gam, gam_t=gam_t))

    probs = [(b, p) for b in range(nb) for p in range(N_PAIRS)]
    pick = lambda m: jnp.where(lo, m[:DN_DK], m[DN_DK:])
    o_rows = [[] for _ in range(nb)]
    for c in range(GDN_NC):
        r0, r1 = c * DN_CHUNK, (c + 1) * DN_CHUNK
        sl = lambda name, b, p: pre[b][name][r0:r1, p * PAIR:(p + 1) * PAIR]
        raws = []
        for b, p in probs:
            k_p = sl("k", b, p)
            k_rows = jnp.concatenate([jnp.where(lo, k_p, 0.0), jnp.where(lo, 0.0, k_p)], axis=0)
            raws.append(_mm_nt(jnp.concatenate([sl("kb", b, p), sl("q", b, p)], axis=0), k_rows))
        pws, ts, qks = [], [], []
        for (b, p), raw in zip(probs, raws):
            gcol = sl("gam", b, p)
            h0 = DN_HEADS + 2 * p
            gam_t = pre[b]["gam_t"]
            grow = jnp.concatenate([gam_t[h0:h0 + 1, r0:r1], gam_t[h0 + 1:h0 + 2, r0:r1]], axis=1)
            decay = jnp.exp(jnp.where(causal, gcol - grow, NEG_INF))
            a = jnp.where(strict, raw[:DN_CHUNK] * decay, 0.0)
            qks.append(jnp.where(causal, raw[DN_CHUNK:] * decay, 0.0))
            pws.append(-a)
            ts.append(eye - a)
        pws = [_mm(pw, _pair_diag(pw, lo)) for pw in pws]
        for _ in range(4):
            rs = [_mm(jnp.concatenate([pw, t], axis=0), _pair_diag(pw, lo)) for pw, t in zip(pws, ts)]
            pws = [r[:DN_CHUNK] for r in rs]
            ts = [t + r[DN_CHUNK:] for t, r in zip(ts, rs)]
        rs = [_mm(t, _pair_diag(pw, lo)) for pw, t in zip(pws, ts)]
        ts = [t + r for t, r in zip(ts, rs)]
        sols = [_mm(t, jnp.concatenate([_pair_diag(sl("vb", b, p), lo), _pair_diag(sl("wr", b, p), lo)],
                                       axis=1)) for (b, p), t in zip(probs, ts)]
        qkuws = [_mm(qk, jnp.concatenate([_pair_diag(s[:, :PAIR], lo), _pair_diag(s[:, PAIR:], lo)], axis=1))
                 for qk, s in zip(qks, sols)]
        crosses, gls = [], []
        for (b, p), s in zip(probs, sols):
            gam_last = pre[b]["gam"][r1 - 1:r1, p * PAIR:(p + 1) * PAIR]
            kd = sl("k", b, p) * jnp.exp(gam_last - sl("gam", b, p))
            crosses.append(_mm_tn(kd, s))
            gls.append(jnp.exp(gam_last))
        lhs = [jnp.concatenate([pick(cr[:, PAIR:]), sl("qg", b, p) - qkuw[:, PAIR:]], axis=0)
               for (b, p), cr, qkuw in zip(probs, crosses, qkuws)]
        s_olds = [s_scr[b, p] for b, p in probs]
        rs = [_mm(l, _pair_diag(s_old, lo)) for l, s_old in zip(lhs, s_olds)]
        o_pairs = [[] for _ in range(nb)]
        for (b, p), r, s_old, gl, cr, qkuw in zip(probs, rs, s_olds, gls, crosses, qkuws):
            s_scr[b, p] = gl * s_old - r[:DN_DK] + pick(cr[:, :PAIR])
            o_pairs[b].append(r[DN_DK:] + qkuw[:, :PAIR])
        for b in range(nb):
            o_rows[b].append(jnp.concatenate(o_pairs[b], axis=1))

    o_all = jnp.concatenate([jnp.concatenate(rows, axis=0) for rows in o_rows], axis=0)
    inv_rms = lax.rsqrt(_head_sums(o_all * o_all, hsum) * (1.0 / DN_DV) + EPS)
    for b in range(nb):
        rows = slice(b * GDN_TB, (b + 1) * GDN_TB)
        o_ref[b] = o_all[rows] * inv_rms[rows] * dnx_ref[...] * _silu(dz_ref[b])

    @pl.when(i == pl.num_programs(0) - 1)
    def _():
        for b in range(nb):
            for p in range(N_PAIRS):
                s_p = s_scr[b, p]
                s_out_ref[b, 2 * p] = s_p[:, :DN_DV]
                s_out_ref[b, 2 * p + 1] = s_p[:, DN_DV:]


def _gdn_consts():
    lane = np.arange(DN_WIDTH)
    pl_lane = np.arange(PAIR)
    hsum = (pl_lane[:, None] // DN_DV == pl_lane[None, :] // DN_DV)
    src = np.arange(LANES)
    expb = (src[:, None] == lane[None, :] // DN_DV)
    expg = (src[:, None] == DN_HEADS + lane[None, :] // DN_DV)
    tok = np.arange(GDN_TB)
    ltri = np.logical_and(tok[:, None] >= tok[None, :],
                          tok[:, None] // DN_CHUNK == tok[None, :] // DN_CHUNK)
    as_bf16 = lambda m: jnp.asarray(m.astype(np.float32), dtype=BF16)
    return as_bf16(hsum), as_bf16(expb), as_bf16(expg), as_bf16(ltri)


def _gdn_prompt(xc, dz, ba, alog, dtb, dnx, batch, seq):
    nt = seq // GDN_TB
    hsum, expb, expg, ltri = _gdn_consts()
    row = lambda n: pl.BlockSpec((batch, GDN_TB, n), lambda i: (0, i, 0))
    full = lambda a: pl.BlockSpec(a.shape, lambda i: (0,) * a.ndim)
    consts = (alog, dtb, dnx, hsum, expb, expg, ltri)
    as3d = lambda a: a.reshape(batch, seq, a.shape[-1])
    o, s = pl.pallas_call(
        _gdn_prompt_kernel,
        grid=(nt,),
        in_specs=[row(CONV_CH), row(DN_WIDTH), row(LANES)] + [full(a) for a in consts],
        out_specs=[row(DN_WIDTH),
                   pl.BlockSpec((batch, DN_HEADS, DN_DK, DN_DV), lambda i: (0, 0, 0, 0))],
        out_shape=[jax.ShapeDtypeStruct((batch, seq, DN_WIDTH), F32),
                   jax.ShapeDtypeStruct((batch, DN_HEADS, DN_DK, DN_DV), F32)],
        scratch_shapes=[pltpu.VMEM((batch, N_PAIRS, DN_DK, PAIR), F32)],
        compiler_params=_params("arbitrary"),
        name="gdn_prompt",
    )(as3d(xc), as3d(dz), as3d(ba), *consts)
    return o.reshape(batch * seq, DN_WIDTH), s


GDN_S_BB = 8


def _gdn_sample_kernel(xc_ref, dz_ref, ba_ref, sc_ref, s_ref, cw_ref, alog_ref, dtb_ref, dn_ref,
                       hsum_ref, eye_ref, hsel_ref, hrep3_ref, o_ref, s_out_ref):
    xc = xc_ref[...]
    y = sc_ref[0] * cw_ref[0:1, :]
    y = y + sc_ref[1] * cw_ref[1:2, :]
    y = y + sc_ref[2] * cw_ref[2:3, :]
    y = _silu(y + xc * cw_ref[3:4, :])
    hsum = hsum_ref[...]
    q = y[:, :DN_WIDTH]
    k = y[:, DN_WIDTH:2 * DN_WIDTH]
    v = y[:, 2 * DN_WIDTH:]
    q = q * lax.rsqrt(_mm_sel_rhs(q * q, hsum) + EPS) * (DN_DK ** -0.5)
    k = k * lax.rsqrt(_mm_sel_rhs(k * k, hsum) + EPS)
    beta_c, g_c = _gdn_gates(ba_ref[...], alog_ref[...], dtb_ref[...])
    eg_c = jnp.exp(g_c)
    eye = eye_ref[...]
    tr = lambda a: lax.dot_general(a, eye, (((0,), (0,)), ((), ())), precision=lax.Precision.HIGHEST,
                                   preferred_element_type=F32)
    gates_t = tr(jnp.concatenate([beta_c, eg_c], axis=1))
    beta_t = gates_t[:LANES]
    eg_t = gates_t[LANES:]
    dz = dz_ref[...]
    dn = dn_ref[...]
    split = lambda r: jnp.concatenate([r[:, h * DN_DV:(h + 1) * DN_DV] for h in range(DN_HEADS)], axis=0)
    own_head = hsel_ref[...].astype(F32)
    hrep3 = hrep3_ref[...]
    seqs = range(GDN_S_BB)
    dot = lambda a, b: jnp.dot(a.astype(BF16), b.astype(BF16), preferred_element_type=F32)

    def pieces(x):
        p1 = x.astype(BF16).astype(F32)
        r1 = x - p1
        p2 = r1.astype(BF16).astype(F32)
        return p1, p2, (r1 - p2).astype(BF16).astype(F32)

    heads = DN_HEADS
    k_pieces, kqs = [], []
    for b in seqs:
        kq_bd = jnp.concatenate([own_head * k[b:b + 1, :], own_head * q[b:b + 1, :]], axis=0)
        a1, a2, a3 = pieces(kq_bd)
        s1, s2, s3 = pieces(s_ref[b])
        r1 = dot(jnp.concatenate([a1, a2, a3], axis=0), s1)
        r2 = dot(jnp.concatenate([a1, a2], axis=0), s2)
        r3 = dot(a1, s3)
        n = 2 * heads
        kqs.append(((r3 + r2[n:] + r1[2 * n:]) + (r2[:n] + r1[n:2 * n])) + r1[:n])
        k_pieces.append((a1[:heads], a2[:heads], a3[:heads]))
    egs = [eg_t[DN_HEADS:2 * DN_HEADS, b:b + 1] for b in seqs]
    qks = [jnp.sum(split(q[b:b + 1, :]) * split(k[b:b + 1, :]), axis=-1, keepdims=True) for b in seqs]
    v_news = [beta_t[0:DN_HEADS, b:b + 1] * (split(v[b:b + 1, :]) - eg * kq[:heads])
              for b, eg, kq in zip(seqs, egs, kqs)]
    os_ = [eg * kq[heads:] + qk * v_new for eg, kq, qk, v_new in zip(egs, kqs, qks, v_news)]
    inv_rms = [lax.rsqrt(jnp.mean(o * o, axis=-1, keepdims=True) + EPS) for o in os_]
    for b, o, r in zip(seqs, os_, inv_rms):
        o_ref[b] = o * r * dn * _silu(split(dz[b:b + 1, :]))
    outers, egrows = [], []
    for (k1, k2, k3), v_new, eg in zip(k_pieces, v_news, egs):
        v1, v2, v3 = pieces(v_new)
        lhs = jnp.concatenate([k1, k1, k2, k1, k2, k3], axis=0).astype(BF16)
        rhs = jnp.concatenate([v1, v2, v1, v3, v2, v1], axis=0).astype(BF16)
        outers.append(lax.dot_general(lhs, rhs, (((0,), (0,)), ((), ())), preferred_element_type=F32))
        egrows.append(dot(hrep3, jnp.concatenate(pieces(jnp.broadcast_to(eg, (DN_HEADS, DN_DV))), axis=0)))
    for b, outer, egrow in zip(seqs, outers, egrows):
        s_out_ref[b] = s_ref[b] * egrow + outer


def _gdn_sample(xc, dz, ba, sconv_t, state, conv_w, alog, dtb, dn):
    nseq = xc.shape[0]
    lane = np.arange(DN_WIDTH)
    hsum = jnp.asarray((lane[:, None] // DN_DV == lane[None, :] // DN_DV).astype(np.float32), dtype=BF16)
    eye = jnp.eye(GDN_S_BB, dtype=F32)
    hsel_np = (np.arange(DN_HEADS)[:, None] == lane[None, :] // DN_DK).astype(np.float32)
    hsel = jnp.asarray(hsel_np, dtype=BF16)
    hrep3 = jnp.asarray(np.tile(hsel_np.T, (1, 3)), dtype=BF16)
    row = lambda n: pl.BlockSpec((GDN_S_BB, n), lambda i: (i, 0))
    full = lambda a: pl.BlockSpec(a.shape, lambda i: (0,) * a.ndim)
    st = pl.BlockSpec((GDN_S_BB, DN_HEADS * DN_DK, DN_DV), lambda i: (i, 0, 0))
    consts = (conv_w, alog, dtb, dn, hsum, eye, hsel, hrep3)
    return pl.pallas_call(
        _gdn_sample_kernel,
        grid=(nseq // GDN_S_BB,),
        in_specs=[row(CONV_CH), row(DN_WIDTH), row(LANES),
                  pl.BlockSpec((CONV_WIDTH - 1, GDN_S_BB, CONV_CH), lambda i: (0, i, 0)), st]
                 + [full(a) for a in consts],
        out_specs=[pl.BlockSpec((GDN_S_BB, DN_HEADS, DN_DV), lambda i: (i, 0, 0)), st],
        out_shape=[jax.ShapeDtypeStruct((nseq, DN_HEADS, DN_DV), F32),
                   jax.ShapeDtypeStruct(state.shape, F32)],
        compiler_params=_params("parallel"),
        name="gdn_sample",
    )(xc, dz, ba, sconv_t, state, *consts)


def _attn_sample_lanes_kernel(att_ref, ck_ref, cv_ref, bucket_ref, rb_ref, sink_ref, o_ref, s_scr):
    g = pl.program_id(0)
    nseq = att_ref.shape[0]
    rnd = lambda a: a.astype(BF16).astype(F32)
    att = att_ref[...]
    q_all_t = (att[:, :ATT_WIDTH] * (HEAD_DIM ** -0.5)).T
    kv_new_t = att[:, ATT_WIDTH:].T
    qsel = [jnp.where(g == 0, q_all_t[hh * HEAD_DIM:(hh + 1) * HEAD_DIM],
                      q_all_t[(GQA + hh) * HEAD_DIM:(GQA + hh + 1) * HEAD_DIM]) for hh in range(GQA)]
    qr = [rnd(q) for q in qsel]
    kn = rnd(jnp.where(g == 0, kv_new_t[0:HEAD_DIM], kv_new_t[HEAD_DIM:2 * HEAD_DIM]))
    vn = rnd(jnp.where(g == 0, kv_new_t[2 * HEAD_DIM:3 * HEAD_DIM], kv_new_t[3 * HEAD_DIM:]))

    def score_row(j, carry):
        kj = rnd(ck_ref[j, 0])
        for hh in range(GQA):
            s_scr[hh, pl.ds(j, 1), :] = jnp.sum(qr[hh] * kj, axis=0, keepdims=True)
        return carry
    lax.fori_loop(0, WINDOW, score_row, 0, unroll=2)

    bucket = bucket_ref[...]
    jrow = lax.broadcasted_iota(jnp.int32, (WINDOW, nseq), 0)
    prn = []
    for hh in range(GQA):
        h = g * GQA + hh
        bias = jnp.where(jrow >= 1, _bias_lookup(bucket, rb_ref, h), NEG_INF)
        s = s_scr[hh] + bias
        s_n = jnp.sum(qr[hh] * kn, axis=0, keepdims=True) + rb_ref[0, h]
        sink = sink_ref[h]
        m = jnp.maximum(jnp.maximum(jnp.max(s, axis=0, keepdims=True), s_n), sink)
        p = jnp.exp(s - m)
        p_n = jnp.exp(s_n - m)
        den = jnp.sum(p, axis=0, keepdims=True) + p_n + jnp.exp(sink - m)
        s_scr[hh] = rnd(p / den)
        prn.append(rnd(p_n / den))

    def value_row(j, acc):
        vj = rnd(cv_ref[j, 0])
        return tuple(acc[hh] + s_scr[hh, pl.ds(j, 1), :] * vj for hh in range(GQA))
    zero = jnp.zeros((HEAD_DIM, nseq), F32)
    acc = lax.fori_loop(0, WINDOW, value_row, (zero,) * GQA, unroll=2)
    for hh in range(GQA):
        o_ref[hh * HEAD_DIM:(hh + 1) * HEAD_DIM, :] = acc[hh] + prn[hh] * vn


def _attn_sample_lanes(att, ck_t, cv_t, rel_bias, sink):
    nseq = att.shape[0]
    assert nseq == LANES
    bucket = jnp.asarray(np.broadcast_to(_t5_bucket_np(WINDOW - np.arange(WINDOW))[:, None], (WINDOW, nseq)))
    smem = pl.BlockSpec(memory_space=pltpu.SMEM)
    cache = pl.BlockSpec((WINDOW, 1, HEAD_DIM, nseq), lambda g: (0, g, 0, 0))
    full = lambda a: pl.BlockSpec(a.shape, lambda g: (0,) * a.ndim)
    return pl.pallas_call(
        _attn_sample_lanes_kernel,
        grid=(ATT_KV_HEADS,),
        in_specs=[full(att), cache, cache, full(bucket), smem, smem],
        out_specs=pl.BlockSpec((GQA * HEAD_DIM, nseq), lambda g: (g, 0)),
        out_shape=jax.ShapeDtypeStruct((ATT_WIDTH, nseq), F32),
        scratch_shapes=[pltpu.VMEM((GQA, WINDOW, nseq), F32)],
        compiler_params=_params("arbitrary"),
        name="attn_sample_lanes",
    )(att, ck_t, cv_t, bucket, rel_bias, sink)


def _gdn_sample_front_kernel(xc_ref, dz_ref, ba_ref, sc_ref, cw_ref, alog_ref, dtb_ref, hsum_ref,
                             q_ref, k_ref, v_ref, dz_t_ref, gates_ref):
    xc = xc_ref[...]
    y = sc_ref[0] * cw_ref[0:1, :]
    y = y + sc_ref[1] * cw_ref[1:2, :]
    y = y + sc_ref[2] * cw_ref[2:3, :]
    y = _silu(y + xc * cw_ref[3:4, :])
    hsum = hsum_ref[...]
    q = y[:, :DN_WIDTH]
    k = y[:, DN_WIDTH:2 * DN_WIDTH]
    q = q * lax.rsqrt(_mm_sel_rhs(q * q, hsum) + EPS) * (DN_DK ** -0.5)
    k = k * lax.rsqrt(_mm_sel_rhs(k * k, hsum) + EPS)
    beta_c, g_c = _gdn_gates(ba_ref[...], alog_ref[...], dtb_ref[...])
    q_ref[...] = q.T
    k_ref[...] = k.T
    v_ref[...] = y[:, 2 * DN_WIDTH:].T
    dz_t_ref[...] = dz_ref[...].T
    gates_ref[0:LANES, :] = beta_c.T
    gates_ref[LANES:, :] = jnp.exp(g_c).T


def _gdn_sample_step_kernel(q_ref, k_ref, v_ref, dz_ref, gates_ref, dn_ref, s_ref, o_ref, s_out_ref):
    h = pl.program_id(0)
    beta = gates_ref[pl.ds(h, 1), :]
    eg = gates_ref[pl.ds(LANES + DN_HEADS + h, 1), :]
    q, k, v = q_ref[...], k_ref[...], v_ref[...]
    w = (k * beta) * eg
    qg = q * eg
    ws = jnp.zeros(v.shape, F32)
    qs = jnp.zeros(v.shape, F32)
    for dk in range(DN_DK):
        s_dk = s_ref[0, dk]
        ws = ws + w[dk:dk + 1, :] * s_dk
        qs = qs + qg[dk:dk + 1, :] * s_dk
    v_new = v * beta - ws
    qk = jnp.sum(q * k, axis=0, keepdims=True)
    o = qs + qk * v_new
    for dk in range(DN_DK):
        s_out_ref[0, dk] = s_ref[0, dk] * eg + k[dk:dk + 1, :] * v_new
    o = o * lax.rsqrt(jnp.mean(o * o, axis=0, keepdims=True) + EPS) * dn_ref[...]
    o_ref[...] = o * _silu(dz_ref[...])


def _gdn_sample_lanes(xc, dz, ba, sconv_t, state_t, conv_w, alog, dtb, dn):
    nseq = xc.shape[0]
    assert nseq == LANES
    lane = np.arange(DN_WIDTH)
    hsum = jnp.asarray((lane[:, None] // DN_DV == lane[None, :] // DN_DV).astype(np.float32), dtype=BF16)
    full = lambda a: pl.BlockSpec(a.shape, lambda i: (0,) * a.ndim)
    cm = jax.ShapeDtypeStruct((DN_WIDTH, nseq), F32)
    front_in = (xc, dz, ba, sconv_t, conv_w, alog, dtb, hsum)
    q_t, k_t, v_t, dz_t, gates_t = pl.pallas_call(
        _gdn_sample_front_kernel,
        grid=(1,),
        in_specs=[full(a) for a in front_in],
        out_specs=[pl.BlockSpec((DN_WIDTH, nseq), lambda i: (0, 0))] * 4
                  + [pl.BlockSpec((2 * LANES, nseq), lambda i: (0, 0))],
        out_shape=[cm, cm, cm, cm, jax.ShapeDtypeStruct((2 * LANES, nseq), F32)],
        compiler_params=_params("arbitrary"),
        name="gdn_sample_front",
    )(*front_in)
    dn_b = jnp.broadcast_to(dn.reshape(DN_DV, 1), (DN_DV, nseq))
    head = pl.BlockSpec((DN_DK, nseq), lambda h: (h, 0))
    st = pl.BlockSpec((1, DN_DK, DN_DV, nseq), lambda h: (h, 0, 0, 0))
    return pl.pallas_call(
        _gdn_sample_step_kernel,
        grid=(DN_HEADS,),
        in_specs=[head, head, head, head, full(gates_t), full(dn_b), st],
        out_specs=[head, st],
        out_shape=[cm, jax.ShapeDtypeStruct(state_t.shape, F32)],
        compiler_params=_params("parallel"),
        name="gdn_sample_step",
    )(q_t, k_t, v_t, dz_t, gates_t, dn_b, state_t)


def _route(xn, wr):
    logits = jnp.dot(xn, wr, preferred_element_type=F32)
    lane = lax.broadcasted_iota(jnp.int32, logits.shape, 1).astype(F32)
    first_at = lambda hit: jnp.min(jnp.where(hit, lane, float(LANES)), axis=-1, keepdims=True)
    glog = jnp.where(lane < N_GROUPS, logits, NEG_INF)
    gmax = jnp.max(glog, axis=-1, keepdims=True)
    gsel = first_at(glog == gmax)
    pgsel = 1.0 / jnp.sum(jnp.exp(glog - gmax), axis=-1, keepdims=True)
    lo = ROUTER_OFF + gsel * EXPERTS_PER_GROUP
    in_group = jnp.logical_and(lane >= lo, lane < lo + EXPERTS_PER_GROUP)
    elog = jnp.where(in_group, logits, NEG_INF)
    m1 = jnp.max(elog, axis=-1, keepdims=True)
    i1 = first_at(elog == m1)
    z = jnp.sum(jnp.exp(elog - m1), axis=-1, keepdims=True)
    elog2 = jnp.where(lane == i1, NEG_INF, elog)
    m2 = jnp.max(elog2, axis=-1, keepdims=True)
    i2 = first_at(elog2 == m2)
    p1 = 1.0 / z
    p2 = jnp.exp(m2 - m1) / z
    tot = p1 + p2
    return lane, i1, i2, p1 / tot * pgsel, p2 / tot * pgsel


def _outproj(x_ref, oa_ref, od_ref, wo_ref):
    return x_ref[...] + _mm(oa_ref[...], wo_ref[:ATT_WIDTH, :]) + _mm(od_ref[...], wo_ref[ATT_WIDTH:, :])


def _outproj_router_kernel(x_ref, oa_t_ref, od_t_ref, wo_ref, g_ref, wr_ref, h_ref, xn_ref, gate_ref):
    h = (x_ref[...] + _mm(oa_t_ref[...].T, wo_ref[:ATT_WIDTH, :])
         + _mm(od_t_ref[...].T, wo_ref[ATT_WIDTH:, :]))
    h_ref[...] = h
    xn = _rmsnorm(h, g_ref[...]).astype(BF16)
    xn_ref[...] = xn
    lane, i1, i2, g1, g2 = _route(xn, wr_ref[...])
    gate_ref[...] = jnp.where(lane == i1, g1, 0.0) + jnp.where(lane == i2, g2, 0.0)


def _outproj_router(x, oa_t, od_t, wo, g, wr):
    t = x.shape[0]
    tm = t
    row = lambda n: pl.BlockSpec((tm, n), lambda i: (i, 0))
    full = lambda a: pl.BlockSpec(a.shape, lambda i: (0,) * a.ndim)
    return pl.pallas_call(
        _outproj_router_kernel,
        grid=(t // tm,),
        in_specs=[row(D_MODEL), full(oa_t), full(od_t), full(wo), full(g), full(wr)],
        out_specs=[row(D_MODEL), row(D_MODEL), row(LANES)],
        out_shape=[jax.ShapeDtypeStruct((t, D_MODEL), F32), jax.ShapeDtypeStruct((t, D_MODEL), BF16),
                   jax.ShapeDtypeStruct((t, LANES), F32)],
        compiler_params=_params("parallel"),
        name="outproj_router",
    )(x, oa_t, od_t, wo, g, wr)


def _moe_kernel(xn_ref, gate_ref, wg_ref, wu_ref, wd_ref, o_ref):
    e = pl.program_id(1)
    xn = xn_ref[...]
    lane = lax.broadcasted_iota(jnp.int32, gate_ref.shape, 1)
    gate = jnp.sum(jnp.where(lane == e + ROUTER_OFF, gate_ref[...], 0.0), axis=-1, keepdims=True)
    hg = jnp.dot(xn, wg_ref[...].astype(BF16), preferred_element_type=F32)
    hu = jnp.dot(xn, wu_ref[...].astype(BF16), preferred_element_type=F32)
    hm = _silu(hg) * hu * gate
    y = jnp.dot(hm.astype(BF16), wd_ref[...].astype(BF16), preferred_element_type=F32)

    @pl.when(e == 0)
    def _():
        o_ref[...] = y

    @pl.when(e > 0)
    def _():
        o_ref[...] += y


def _moe(xn, gates, wg, wu, wd):
    t = xn.shape[0]
    tm = min(t, 1024)
    return pl.pallas_call(
        _moe_kernel,
        grid=(t // tm, N_EXPERTS),
        in_specs=[pl.BlockSpec((tm, D_MODEL), lambda i, e: (i, 0)),
                  pl.BlockSpec((tm, LANES), lambda i, e: (i, 0)),
                  pl.BlockSpec((None, D_MODEL, D_EXPERT), lambda i, e: (e, 0, 0)),
                  pl.BlockSpec((None, D_MODEL, D_EXPERT), lambda i, e: (e, 0, 0)),
                  pl.BlockSpec((None, D_EXPERT, D_MODEL), lambda i, e: (e, 0, 0))],
        out_specs=pl.BlockSpec((tm, D_MODEL), lambda i, e: (i, 0)),
        out_shape=jax.ShapeDtypeStruct((t, D_MODEL), F32),
        compiler_params=_params("parallel", "arbitrary"),
        name="moe",
    )(xn, gates, wg, wu, wd)


MOE_TM = 512
POS_TM = 1024
INFO_G1, INFO_G2, INFO_E1, INFO_E2 = 0, 1, 2, 3
DMA_UNROLL = 8


def _moe_tiles(t):
    return (2 * t) // MOE_TM + N_EXPERTS


HALF = D_MODEL // 2
U32 = jnp.uint32


def _pack_rows(x):
    bits = lambda v: lax.bitcast_convert_type(v.astype(BF16).astype(F32), U32)
    return bits(x[:, HALF:]) | (bits(x[:, :HALF]) >> 16)


def _unpack_rows(w):
    lo = lax.bitcast_convert_type(w << 16, F32)
    hi = lax.bitcast_convert_type(w & jnp.uint32(0xFFFF0000), F32)
    return lo, hi


def _route_kernel(x_ref, oa_ref, od_ref, wo_ref, g_ref, wr_ref, h_ref, xn_ref, info_ref, cnt_ref, run_scr):
    h = _outproj(x_ref, oa_ref, od_ref, wo_ref)
    h_ref[...] = h
    xn = _rmsnorm(h, g_ref[...])
    xn_ref[...] = _pack_rows(xn)
    lane, i1, i2, g1, g2 = _route(xn.astype(BF16), wr_ref[...])
    info = jnp.where(lane == INFO_G1, g1, 0.0) + jnp.where(lane == INFO_G2, g2, 0.0)
    info = info + jnp.where(lane == INFO_E1, i1, 0.0) + jnp.where(lane == INFO_E2, i2, 0.0)
    info_ref[...] = info

    @pl.when(pl.program_id(0) == 0)
    def _():
        run_scr[...] = jnp.zeros(run_scr.shape, F32)
    picked = jnp.logical_or(lane == i1, lane == i2).astype(F32)
    run_scr[...] += jnp.sum(picked, axis=0, keepdims=True)
    cnt_ref[...] = run_scr[...]


def _route_sparse(x, oa, od, wo, g, wr):
    t = x.shape[0]
    tm = ROW_TM
    row = lambda n: pl.BlockSpec((tm, n), lambda i: (i, 0))
    full = lambda a: pl.BlockSpec(a.shape, lambda i: (0,) * a.ndim)
    return pl.pallas_call(
        _route_kernel,
        grid=(t // tm,),
        in_specs=[row(D_MODEL), row(ATT_WIDTH), row(DN_WIDTH), full(wo), full(g), full(wr)],
        out_specs=[row(D_MODEL), row(HALF), row(LANES), pl.BlockSpec((1, LANES), lambda i: (0, 0))],
        out_shape=[jax.ShapeDtypeStruct((t, D_MODEL), F32), jax.ShapeDtypeStruct((t, HALF), U32),
                   jax.ShapeDtypeStruct((t, LANES), F32), jax.ShapeDtypeStruct((1, LANES), F32)],
        scratch_shapes=[pltpu.VMEM((1, LANES), F32)],
        compiler_params=_params("arbitrary"),
        name="route",
    )(x, oa, od, wo, g, wr)


def _positions_kernel(info_ref, cnt_ref, ltri_ref, utri_ref, pos_ref, run_scr, off_scr):
    info = info_ref[...]
    lane = lax.broadcasted_iota(jnp.int32, info.shape, 1).astype(F32)
    hit1 = lane == info[:, INFO_E1:INFO_E1 + 1]
    hit2 = lane == info[:, INFO_E2:INFO_E2 + 1]
    onehot = jnp.logical_or(hit1, hit2).astype(F32)

    @pl.when(pl.program_id(0) == 0)
    def _():
        tiles = jnp.floor((cnt_ref[...] + (MOE_TM - 1)) * (1.0 / MOE_TM))
        off_scr[...] = MOE_TM * jnp.dot(tiles.astype(BF16), utri_ref[...], preferred_element_type=F32)
        run_scr[...] = jnp.zeros(run_scr.shape, F32)

    before = (jnp.dot(ltri_ref[...], onehot.astype(BF16), preferred_element_type=F32)
              + run_scr[...] + off_scr[...])
    pos1 = jnp.sum(jnp.where(hit1, before, 0.0), axis=-1, keepdims=True)
    pos2 = jnp.sum(jnp.where(hit2, before, 0.0), axis=-1, keepdims=True)
    pos_ref[...] = (jnp.where(lane == 0, pos1, 0.0) + jnp.where(lane == 1, pos2, 0.0)).astype(jnp.int32)
    run_scr[...] += jnp.sum(onehot, axis=0, keepdims=True)


def _positions(info, cnt):
    t = info.shape[0]
    tm = min(t, POS_TM)
    tok = np.arange(tm)
    ltri = jnp.asarray((tok[:, None] > tok[None, :]).astype(np.float32), dtype=BF16)
    ln = np.arange(LANES)
    utri = jnp.asarray((ln[:, None] < ln[None, :]).astype(np.float32), dtype=BF16)
    full = lambda a: pl.BlockSpec(a.shape, lambda i: (0,) * a.ndim)
    return pl.pallas_call(
        _positions_kernel,
        grid=(t // tm,),
        in_specs=[pl.BlockSpec((tm, LANES), lambda i: (i, 0)), full(cnt), full(ltri), full(utri)],
        out_specs=pl.BlockSpec((tm, LANES), lambda i: (i, 0)),
        out_shape=jax.ShapeDtypeStruct((t, LANES), jnp.int32),
        scratch_shapes=[pltpu.VMEM((1, LANES), F32), pltpu.VMEM((1, LANES), F32)],
        compiler_params=_params("arbitrary"),
        name="positions",
    )(info, cnt, ltri, utri)


def _row_copy(src_hbm, src_row, dst_hbm, dst_row, sem):
    return pltpu.make_async_copy(src_hbm.at[pl.ds(src_row, 1)], dst_hbm.at[pl.ds(dst_row, 1)], sem)


SCATTER_SLOTS = 3


def _scatter_kernel(pos1_ref, pos2_ref, last_ref, used_ref, nt_ref, xn_hbm, zero_hbm, xs_hbm,
                    buf, lsem, sem, zsem, *, n_tok):
    max_tiles = xs_hbm.shape[0] // MOE_TM

    def zero_tile(tile):
        return pltpu.make_async_copy(zero_hbm, xs_hbm.at[pl.ds(tile * MOE_TM, MOE_TM)], zsem)

    def for_unused(fn):
        def body(tile, carry):
            fn(tile)
            return carry
        lax.fori_loop(nt_ref[0], max_tiles, body, 0)

    for e in range(N_EXPERTS):
        @pl.when(used_ref[e] > 0)
        def _():
            zero_tile(last_ref[e]).start()
    for_unused(lambda tile: zero_tile(tile).start())
    for e in range(N_EXPERTS):
        @pl.when(used_ref[e] > 0)
        def _():
            zero_tile(last_ref[e]).wait()
    for_unused(lambda tile: zero_tile(tile).wait())

    tm = buf.shape[1]
    n = n_tok // tm

    def load(i):
        return pltpu.make_async_copy(xn_hbm.at[pl.ds(i * tm, tm)], buf.at[i % SCATTER_SLOTS],
                                     lsem.at[i % SCATTER_SLOTS])

    def wait_rows(slot):
        pltpu.make_async_copy(xs_hbm.at[pl.ds(0, 2 * tm)], xs_hbm.at[pl.ds(0, 2 * tm)], sem.at[slot]).wait()

    load(0).start()
    load(1).start()

    def step(i, carry):
        slot = i % SCATTER_SLOTS
        load(i).wait()

        def body(j, c2):
            tok = i * tm + j
            src = buf.at[slot, pl.ds(j, 1)]
            pltpu.make_async_copy(src, xs_hbm.at[pl.ds(pos1_ref[tok], 1)], sem.at[slot]).start()
            pltpu.make_async_copy(src, xs_hbm.at[pl.ds(pos2_ref[tok], 1)], sem.at[slot]).start()
            return c2
        lax.fori_loop(0, tm, body, 0, unroll=DMA_UNROLL)

        @pl.when(i >= 1)
        def _():
            wait_rows((i - 1) % SCATTER_SLOTS)

        @pl.when(i + 2 < n)
        def _():
            load(i + 2).start()
        return carry
    lax.fori_loop(0, n, step, 0)
    wait_rows((n - 1) % SCATTER_SLOTS)


def _scatter_rows(xn, pos1, pos2, last_tile, used, n_tiles, n_rows):
    t = xn.shape[0]
    zero = jnp.zeros((MOE_TM, D_MODEL), F32)
    any_spec = pl.BlockSpec(memory_space=pl.ANY)
    return pl.pallas_call(
        functools.partial(_scatter_kernel, n_tok=t),
        grid_spec=pltpu.PrefetchScalarGridSpec(
            num_scalar_prefetch=5, grid=(1,),
            in_specs=[any_spec, any_spec], out_specs=any_spec,
            scratch_shapes=[pltpu.VMEM((SCATTER_SLOTS, MOE_TM, D_MODEL), F32),
                            pltpu.SemaphoreType.DMA((SCATTER_SLOTS,)),
                            pltpu.SemaphoreType.DMA((SCATTER_SLOTS,)),
                            pltpu.SemaphoreType.DMA]),
        out_shape=jax.ShapeDtypeStruct((n_rows, D_MODEL), F32),
        compiler_params=_params("arbitrary"),
        name="scatter_rows",
    )(pos1, pos2, last_tile, used, n_tiles, xn, zero)


def _experts_kernel(te_ref, tv_ref, nt_ref, xs_ref, wg_ref, wu_ref, wd_ref, ys_ref, wg_s, wu_s, wd_s):
    i = pl.program_id(0)
    used = i < nt_ref[0]

    @pl.when(jnp.logical_or(i == 0, te_ref[i] != te_ref[jnp.maximum(i - 1, 0)]))
    def _():
        wg_s[...] = wg_ref[...].astype(BF16)
        wu_s[...] = wu_ref[...].astype(BF16)
        wd_s[...] = wd_ref[...].astype(BF16)

    @pl.when(used)
    def _():
        row = lax.broadcasted_iota(jnp.int32, xs_ref.shape, 0)
        x_lo, x_hi = _unpack_rows(jnp.where(row < tv_ref[i], xs_ref[...], jnp.uint32(0)))
        x_lo = x_lo.astype(BF16)
        x_hi = x_hi.astype(BF16)
        up = lambda w_s: (jnp.dot(x_lo, w_s[:HALF, :], preferred_element_type=F32)
                          + jnp.dot(x_hi, w_s[HALF:, :], preferred_element_type=F32))
        hm = (_silu(up(wg_s)) * up(wu_s)).astype(BF16)
        ys_ref[...] = _pack_rows(jnp.dot(hm, wd_s[...], preferred_element_type=F32))

    @pl.when(jnp.logical_not(used))
    def _():
        ys_ref[...] = jnp.zeros(ys_ref.shape, U32)


def _experts(xs, tile_expert, tile_valid, n_tiles, wg, wu, wd):
    max_tiles = xs.shape[0] // MOE_TM
    rows = pl.BlockSpec((MOE_TM, HALF), lambda i, te, tv, nt: (i, 0))
    wspec = lambda shape: pl.BlockSpec((None,) + shape, lambda i, te, tv, nt: (te[i], 0, 0))
    return pl.pallas_call(
        _experts_kernel,
        grid_spec=pltpu.PrefetchScalarGridSpec(
            num_scalar_prefetch=3, grid=(max_tiles,),
            in_specs=[rows, wspec((D_MODEL, D_EXPERT)), wspec((D_MODEL, D_EXPERT)),
                      wspec((D_EXPERT, D_MODEL))],
            out_specs=rows,
            scratch_shapes=[pltpu.VMEM((D_MODEL, D_EXPERT), BF16), pltpu.VMEM((D_MODEL, D_EXPERT), BF16),
                            pltpu.VMEM((D_EXPERT, D_MODEL), BF16)]),
        out_shape=jax.ShapeDtypeStruct(xs.shape, U32),
        compiler_params=_params("arbitrary"),
        name="experts",
    )(tile_expert, tile_valid, n_tiles, xs, wg, wu, wd)


def _ple_gather_kernel(pos1_ref, pos2_ref, h_ref, info_ref, p_ref, wpp_ref, wpg_ref, gp_ref, gf_ref,
                       ys_hbm, y_ref, ybuf, sem):
    i = pl.program_id(0)
    n = pl.num_programs(0)
    tm = h_ref.shape[0]

    def issue(tile, slot):
        def body(j, carry):
            tok = tile * tm + j
            pltpu.make_async_copy(ys_hbm.at[pl.ds(pos1_ref[tok], 1)], ybuf.at[slot, 0, pl.ds(j, 1)],
                                  sem.at[slot]).start()
            pltpu.make_async_copy(ys_hbm.at[pl.ds(pos2_ref[tok], 1)], ybuf.at[slot, 1, pl.ds(j, 1)],
                                  sem.at[slot]).start()
            return carry
        lax.fori_loop(0, tm, body, 0, unroll=DMA_UNROLL)

    @pl.when(i == 0)
    def _():
        issue(0, 0)

    @pl.when(i + 1 < n)
    def _():
        issue(i + 1, (i + 1) % 2)

    slot = i % 2
    pltpu.make_async_copy(ybuf.at[slot], ybuf.at[slot], sem.at[slot]).wait()
    info = info_ref[...]
    moe = info[:, INFO_G1:INFO_G1 + 1] * ybuf[slot, 0] + info[:, INFO_G2:INFO_G2 + 1] * ybuf[slot, 1]
    h = h_ref[...] + moe
    hn = _rmsnorm(h, gp_ref[...])
    h = h + _mm(p_ref[...], wpp_ref[...]) * _sigmoid(_mm(hn, wpg_ref[...]))
    y_ref[...] = _rmsnorm(h, gf_ref[...])


def _ple_gather(h, info, p, ys, pos1, pos2, wpp, wpg, gp, gf):
    t = h.shape[0]
    tm = 256
    row = lambda n: pl.BlockSpec((tm, n), lambda i, p1, p2: (i, 0))
    full = lambda a: pl.BlockSpec(a.shape, lambda i, p1, p2: (0,) * a.ndim)
    return pl.pallas_call(
        _ple_gather_kernel,
        grid_spec=pltpu.PrefetchScalarGridSpec(
            num_scalar_prefetch=2, grid=(t // tm,),
            in_specs=[row(D_MODEL), row(LANES), row(PLE_DIM), full(wpp), full(wpg), full(gp), full(gf),
                      pl.BlockSpec(memory_space=pl.ANY)],
            out_specs=row(D_MODEL),
            scratch_shapes=[pltpu.VMEM((2, 2, tm, D_MODEL), F32), pltpu.SemaphoreType.DMA((2,))]),
        out_shape=jax.ShapeDtypeStruct((t, D_MODEL), F32),
        compiler_params=_params("arbitrary"),
        name="ple_gather",
    )(pos1, pos2, h, info, p, wpp, wpg, gp, gf, ys)


SC_IDX = 128
SC_ROWS = 64
SC_WORKERS = 32


def _sc_mesh():
    return plsc.VectorSubcoreMesh(core_axis_name="c", subcore_axis_name="s")


def _sc_windows(t, fn):
    per_worker = t // SC_WORKERS
    worker = lax.axis_index(("c", "s"))

    @pl.loop(0, per_worker // SC_IDX)
    def _(w):
        fn(worker * per_worker + w * SC_IDX)


def _sc_scatter_rows(xn, pos1, pos2, n_rows):
    t, d = xn.shape
    assert t % (SC_WORKERS * SC_IDX) == 0
    idx_t = pltpu.VMEM((1, SC_IDX), jnp.int32)

    @pl.kernel(out_type=jax.ShapeDtypeStruct((n_rows, d), xn.dtype), mesh=_sc_mesh(),
               scratch_types=[idx_t, idx_t, pltpu.VMEM((SC_ROWS, d), xn.dtype)])
    def scatter(x_hbm, p1_hbm, p2_hbm, o_hbm, i1_v, i2_v, buf):
        def window(base):
            pltpu.sync_copy(p1_hbm.at[:, pl.ds(base, SC_IDX)], i1_v)
            pltpu.sync_copy(p2_hbm.at[:, pl.ds(base, SC_IDX)], i2_v)
            for k in range(SC_IDX // SC_ROWS):
                pltpu.sync_copy(x_hbm.at[pl.ds(base + k * SC_ROWS, SC_ROWS)], buf)
                pltpu.sync_copy(buf, o_hbm.at[i1_v.at[0, pl.ds(k * SC_ROWS, SC_ROWS)]])
                pltpu.sync_copy(buf, o_hbm.at[i2_v.at[0, pl.ds(k * SC_ROWS, SC_ROWS)]])
        _sc_windows(t, window)

    return scatter(xn, pos1.reshape(1, t), pos2.reshape(1, t))


def _sc_gather_rows(ys, pos1, pos2):
    t = pos1.shape[0]
    d = ys.shape[1]
    assert t % (SC_WORKERS * SC_IDX) == 0
    idx_t = pltpu.VMEM((1, SC_IDX), jnp.int32)
    out = jax.ShapeDtypeStruct((t, d), ys.dtype)

    @pl.kernel(out_type=(out, out), mesh=_sc_mesh(),
               scratch_types=[idx_t, idx_t, pltpu.VMEM((SC_ROWS, d), ys.dtype)])
    def gather(y_hbm, p1_hbm, p2_hbm, o1_hbm, o2_hbm, i1_v, i2_v, buf):
        def window(base):
            pltpu.sync_copy(p1_hbm.at[:, pl.ds(base, SC_IDX)], i1_v)
            pltpu.sync_copy(p2_hbm.at[:, pl.ds(base, SC_IDX)], i2_v)
            for k in range(SC_IDX // SC_ROWS):
                rows = pl.ds(base + k * SC_ROWS, SC_ROWS)
                pltpu.sync_copy(y_hbm.at[i1_v.at[0, pl.ds(k * SC_ROWS, SC_ROWS)]], buf)
                pltpu.sync_copy(buf, o1_hbm.at[rows])
                pltpu.sync_copy(y_hbm.at[i2_v.at[0, pl.ds(k * SC_ROWS, SC_ROWS)]], buf)
                pltpu.sync_copy(buf, o2_hbm.at[rows])
        _sc_windows(t, window)

    return gather(ys, pos1.reshape(1, t), pos2.reshape(1, t))


def _ple_sparse_kernel(h_ref, info_ref, y1_ref, y2_ref, p_ref, wpp_ref, wpg_ref, gp_ref, gf_ref, y_ref):
    info = info_ref[...]
    g1 = info[:, INFO_G1:INFO_G1 + 1]
    g2 = info[:, INFO_G2:INFO_G2 + 1]
    y1_lo, y1_hi = _unpack_rows(y1_ref[...])
    y2_lo, y2_hi = _unpack_rows(y2_ref[...])
    moe = jnp.concatenate([g1 * y1_lo + g2 * y2_lo, g1 * y1_hi + g2 * y2_hi], axis=1)
    h = h_ref[...] + moe
    hn = _rmsnorm(h, gp_ref[...])
    h = h + _mm(p_ref[...], wpp_ref[...]) * _sigmoid(_mm(hn, wpg_ref[...]))
    y_ref[...] = _rmsnorm(h, gf_ref[...])


def _ple_sparse(h, info, y1, y2, p, wpp, wpg, gp, gf):
    t = h.shape[0]
    tm = ROW_TM
    row = lambda n: pl.BlockSpec((tm, n), lambda i: (i, 0))
    full = lambda a: pl.BlockSpec(a.shape, lambda i: (0,) * a.ndim)
    return pl.pallas_call(
        _ple_sparse_kernel,
        grid=(t // tm,),
        in_specs=[row(D_MODEL), row(LANES), row(HALF), row(HALF), row(PLE_DIM),
                  full(wpp), full(wpg), full(gp), full(gf)],
        out_specs=row(D_MODEL),
        out_shape=jax.ShapeDtypeStruct((t, D_MODEL), F32),
        compiler_params=_params("parallel"),
        name="ple_sparse",
    )(h, info, y1, y2, p, wpp, wpg, gp, gf)


def _tile_tables(cnt, max_tiles):
    tiles_e = (cnt + (MOE_TM - 1)) // MOE_TM
    ends = jnp.cumsum(tiles_e)
    n_tiles = ends[-1]
    tile = jnp.arange(max_tiles, dtype=jnp.int32)
    idx = jnp.minimum(tile, n_tiles - 1)
    tile_expert = jnp.sum((idx[:, None] >= ends[None, :]).astype(jnp.int32), axis=1)
    mine = tile_expert[:, None] == jnp.arange(N_EXPERTS, dtype=jnp.int32)[None, :]
    of_mine = lambda v: jnp.sum(jnp.where(mine, v[None, :], 0), axis=1)
    valid = jnp.clip(of_mine(cnt) - (idx - of_mine(ends - tiles_e)) * MOE_TM, 0, MOE_TM)
    tile_valid = jnp.where(tile < n_tiles, valid, 0).astype(jnp.int32)
    return (tile_expert, tile_valid, n_tiles.reshape(1), (ends - 1).astype(jnp.int32),
            tiles_e.astype(jnp.int32))


def _ple_final_kernel(h_ref, m_ref, p_ref, wpp_ref, wpg_ref, gp_ref, gf_ref, y_ref):
    h = h_ref[...] + m_ref[...]
    hn = _rmsnorm(h, gp_ref[...])
    h = h + _mm(p_ref[...], wpp_ref[...]) * _sigmoid(_mm(hn, wpg_ref[...]))
    y_ref[...] = _rmsnorm(h, gf_ref[...])


def _ple_final(h, m, p, wpp, wpg, gp, gf):
    t = h.shape[0]
    tm = min(t, 256)
    row = lambda n: pl.BlockSpec((tm, n), lambda i: (i, 0))
    full = lambda a: pl.BlockSpec(a.shape, lambda i: (0,) * a.ndim)
    return pl.pallas_call(
        _ple_final_kernel,
        grid=(t // tm,),
        in_specs=[row(D_MODEL), row(D_MODEL), row(PLE_DIM), full(wpp), full(wpg), full(gp), full(gf)],
        out_specs=row(D_MODEL),
        out_shape=jax.ShapeDtypeStruct((t, D_MODEL), F32),
        compiler_params=_params("parallel"),
        name="ple_final",
    )(h, m, p, wpp, wpg, gp, gf)


def kernel(x_prompt, x_sample, p_prompt, p_sample, cache_k, cache_v, state_conv, state_S, rel_bias, norm_mix, w_in, att_sink, conv_w, dn_A_log, dn_dt_bias, dn_norm, w_out, norm_ffn, w_router_group, w_router_expert, w_gate, w_up, w_down, w_ple_proj, w_ple_gate, norm_ple, norm_final):
    batch, seq, _ = x_prompt.shape
    nseq = x_sample.shape[0]
    assert x_sample.shape[1] == 1 and norm_mix.shape[0] == 1 and cache_k.shape[2] == WINDOW
    assert seq % GDN_TB == 0 and seq % ATT_BLOCK == 0

    wi = w_in[0]
    o_db = ATT_COLS + CONV_CH
    w_in_re = jnp.concatenate(
        [wi[:, :o_db], wi[:, o_db + 2 * DN_HEADS:], wi[:, o_db:o_db + 2 * DN_HEADS],
         jnp.zeros((D_MODEL, LANES - 2 * DN_HEADS), F32)], axis=1).astype(BF16)
    row = lambda a: a.reshape(1, -1).astype(F32)
    pad_lanes = lambda a, off: jnp.zeros((1, LANES), F32).at[0, off:off + a.shape[0]].set(a)
    alog = pad_lanes(dn_A_log[0], DN_HEADS)
    dtb = pad_lanes(dn_dt_bias[0], DN_HEADS)
    dnx = jnp.tile(dn_norm[0], DN_HEADS).reshape(1, DN_WIDTH)
    w_router = jnp.concatenate(
        [w_router_group[0], w_router_expert[0],
         jnp.zeros((D_MODEL, LANES - N_GROUPS - N_EXPERTS), F32)], axis=1).astype(BF16)
    wo = w_out[0].astype(BF16)
    wg, wu, wd = w_gate[0], w_up[0], w_down[0]
    wpp, wpg = w_ple_proj[0].astype(BF16), w_ple_gate[0].astype(BF16)
    sink = att_sink[0]

    qi = np.arange(ATT_BLOCK)[:, None]
    kj = np.arange(2 * ATT_BLOCK)[None, :]
    bucket_p = jnp.asarray(_t5_bucket_np(qi + ATT_BLOCK - kj))
    bucket_s = jnp.asarray(_t5_bucket_np(WINDOW - np.arange(WINDOW)[None, :]))

    def tail(x, o_att, o_dn, p):
        h1, xn2, gates = _outproj_router(x, o_att, o_dn, wo, row(norm_ffn[0]), w_router)
        moe = _moe(xn2, gates, wg, wu, wd)
        return _ple_final(h1, moe, p, wpp, wpg, row(norm_ple[0]), row(norm_final))

    xp = x_prompt.reshape(batch * seq, D_MODEL)
    att_p, qkv_p, dz_p, ba_p, xc_tails = _inproj_conv(xp, row(norm_mix[0]), w_in_re, conv_w[0], seq)
    o_att_p = _attn_prompt(att_p, bucket_p, rel_bias, sink, batch, seq)
    o_dn_p, s_p = _gdn_prompt(qkv_p, dz_p, ba_p, alog, dtb, dnx, batch, seq)
    h1, xn2, info, cnt = _route_sparse(xp, o_att_p, o_dn_p, wo, row(norm_ffn[0]), w_router)
    pos = _positions(info, cnt)
    pos1, pos2 = pos[:, 0], pos[:, 1]
    max_tiles = _moe_tiles(batch * seq)
    cnt_e = cnt[0, ROUTER_OFF:ROUTER_OFF + N_EXPERTS].astype(jnp.int32)
    tile_expert, tile_valid, n_tiles, last_tile, used = _tile_tables(cnt_e, max_tiles)
    xs = _sc_scatter_rows(xn2, pos1, pos2, max_tiles * MOE_TM)
    ys = _experts(xs, tile_expert, tile_valid, n_tiles, wg, wu, wd)
    y1, y2 = _sc_gather_rows(ys, pos1, pos2)
    y_p = _ple_sparse(h1, info, y1, y2, p_prompt[0].reshape(batch * seq, PLE_DIM),
                      wpp, wpg, row(norm_ple[0]), row(norm_final))

    xs = x_sample.reshape(nseq, D_MODEL)
    att_s, xc_s, dz_s, ba_s = _inproj(xs, row(norm_mix[0]), w_in_re)
    ck_t = jnp.transpose(cache_k[0], (1, 2, 3, 0))
    cv_t = jnp.transpose(cache_v[0], (1, 2, 3, 0))
    o_att_s = _attn_sample_lanes(att_s, ck_t, cv_t, rel_bias, sink)
    sconv_t = jnp.swapaxes(state_conv[0], 0, 1)
    o_dn_s_t, s_s_t = _gdn_sample_lanes(xc_s, dz_s, ba_s, sconv_t, jnp.transpose(state_S[0], (1, 2, 3, 0)),
                                        conv_w[0], alog, dtb, dn_norm[0])
    s_s = jnp.transpose(s_s_t, (3, 0, 1, 2))
    y_s = tail(xs, o_att_s, o_dn_s_t, p_sample[0].reshape(nseq, PLE_DIM))

    att_p3 = att_p.reshape(batch, seq, ATT_COLS)
    kv_shape = (1, batch, WINDOW, ATT_KV_HEADS, HEAD_DIM)
    k_p = att_p3[:, seq - WINDOW:, ATT_WIDTH:ATT_WIDTH + KV_WIDTH].reshape(kv_shape)
    v_p = att_p3[:, seq - WINDOW:, ATT_WIDTH + KV_WIDTH:].reshape(kv_shape)
    conv_p = xc_tails.reshape(batch, -1, TAIL, CONV_CH)[:, -1, TAIL - (CONV_WIDTH - 1):][None]
    new_row = lambda c0: att_s[:, c0:c0 + KV_WIDTH].T.reshape(1, ATT_KV_HEADS, HEAD_DIM, nseq)
    shift_in = lambda c_t, c0: jnp.transpose(jnp.concatenate([c_t[1:], new_row(c0)], axis=0), (3, 0, 1, 2))[None]
    k_s = shift_in(ck_t, ATT_WIDTH)
    v_s = shift_in(cv_t, ATT_WIDTH + KV_WIDTH)
    conv_s = jnp.concatenate([state_conv[0][:, 1:], xc_s[:, None, :]], axis=1)[None]
    return (y_p.reshape(batch, seq, D_MODEL), y_s.reshape(nseq, 1, D_MODEL),
            k_p, v_p, conv_p, s_p[None], k_s, v_s, conv_s, s_s[None])
```

```python
import functools
import math

import numpy as np
import jax
import jax.numpy as jnp
from jax import lax
from jax.experimental import pallas as pl
from jax.experimental.pallas import tpu as pltpu
from jax.experimental.pallas import tpu_sc as plsc

F32 = jnp.float32
BF16 = jnp.bfloat16

D_MODEL = 1024
ATT_HEADS = 8
ATT_KV_HEADS = 2
HEAD_DIM = 64
GQA = ATT_HEADS // ATT_KV_HEADS
WINDOW = 128
ATT_BLOCK = 128
N_BUCKETS = 32
DN_HEADS = 8
DN_DK = 64
DN_DV = 64
CONV_WIDTH = 4
DN_CHUNK = 64
ATT_WIDTH = ATT_HEADS * HEAD_DIM
KV_WIDTH = ATT_KV_HEADS * HEAD_DIM
DN_WIDTH = DN_HEADS * DN_DV
CONV_CH = 3 * DN_WIDTH
N_GROUPS = 4
EXPERTS_PER_GROUP = 8
N_EXPERTS = N_GROUPS * EXPERTS_PER_GROUP
D_EXPERT = 256
PLE_DIM = 256
EPS = 1e-6
NEG_INF = float("-inf")

ATT_COLS = ATT_WIDTH + 2 * KV_WIDTH
LANES = 128
IN_COLS = ATT_COLS + CONV_CH + DN_WIDTH + LANES
ROUTER_OFF = N_GROUPS
VMEM_LIMIT = 48 * 1024 * 1024
ROW_TM = 512


def _params(*sem):
    return pltpu.CompilerParams(dimension_semantics=sem, vmem_limit_bytes=VMEM_LIMIT)


def _mm(a, b):
    return jnp.dot(a.astype(BF16), b.astype(BF16), preferred_element_type=F32)


def _mm_nt(a, b):
    return lax.dot_general(a.astype(BF16), b.astype(BF16), (((1,), (1,)), ((), ())),
                           preferred_element_type=F32)


def _mm_tn(a, b):
    return lax.dot_general(a.astype(BF16), b.astype(BF16), (((0,), (0,)), ((), ())),
                           preferred_element_type=F32)


def _split3(x):
    h1 = x.astype(BF16)
    r1 = x - h1.astype(F32)
    h2 = r1.astype(BF16)
    h3 = (r1 - h2.astype(F32)).astype(BF16)
    return h1, h2, h3


def _mm_sel_rhs(x, sel):
    h1, h2, h3 = _split3(x)
    d = lambda h: jnp.dot(h, sel, preferred_element_type=F32)
    return d(h1) + d(h2) + d(h3)


def _mm_sel_lhs(sel, x):
    h1, h2, h3 = _split3(x)
    d = lambda h: jnp.dot(sel, h, preferred_element_type=F32)
    return d(h1) + d(h2) + d(h3)


def _mm3(a, b):
    ah = a.astype(BF16)
    al = (a - ah.astype(F32)).astype(BF16)
    bh = b.astype(BF16)
    bl = (b - bh.astype(F32)).astype(BF16)
    d = lambda u, v: jnp.dot(u, v, preferred_element_type=F32)
    return d(ah, bh) + d(ah, bl) + d(al, bh)


def _sigmoid(x):
    return 1.0 / (1.0 + jnp.exp(-x))


def _silu(x):
    return x * _sigmoid(x)


def _softplus(x):
    return jnp.maximum(x, 0.0) + jnp.log1p(jnp.exp(-jnp.abs(x)))


def _rmsnorm(x, g):
    return x * lax.rsqrt(jnp.mean(x * x, axis=-1, keepdims=True) + EPS) * g


def _t5_bucket_np(dist):
    max_exact = N_BUCKETS // 2
    d = np.maximum(dist, 0)
    ratio = (np.log(np.maximum(d, 1).astype(np.float32) / np.float32(max_exact))
             / np.float32(math.log(WINDOW / max_exact))).astype(np.float32)
    large = np.minimum(max_exact + (ratio * np.float32(N_BUCKETS - max_exact)).astype(np.int32),
                       N_BUCKETS - 1)
    return np.where(d < max_exact, d, large).astype(np.int32)


def _bias_lookup(bucket, rb_ref, h):
    acc = jnp.zeros(bucket.shape, F32)
    for t in range(N_BUCKETS):
        acc = jnp.where(bucket == t, rb_ref[t, h], acc)
    return acc


def _inproj_kernel(x_ref, g_ref, w_ref, att_ref, xc_ref, dz_ref, ba_ref):
    xn = _rmsnorm(x_ref[...], g_ref[...]).astype(BF16)
    o0, o1, o2 = ATT_COLS, ATT_COLS + CONV_CH, ATT_COLS + CONV_CH + DN_WIDTH
    att_ref[...] = jnp.dot(xn, w_ref[:, :o0], preferred_element_type=F32)
    xc_ref[...] = jnp.dot(xn, w_ref[:, o0:o1], preferred_element_type=F32)
    dz_ref[...] = jnp.dot(xn, w_ref[:, o1:o2], preferred_element_type=F32)
    ba_ref[...] = jnp.dot(xn, w_ref[:, o2:], preferred_element_type=F32)


def _inproj(x, g, w):
    t = x.shape[0]
    tm = min(t, ROW_TM)
    row = lambda n: pl.BlockSpec((tm, n), lambda i: (i, 0))
    full = lambda a: pl.BlockSpec(a.shape, lambda i: (0,) * a.ndim)
    return pl.pallas_call(
        _inproj_kernel,
        grid=(t // tm,),
        in_specs=[row(D_MODEL), full(g), full(w)],
        out_specs=[row(ATT_COLS), row(CONV_CH), row(DN_WIDTH), row(LANES)],
        out_shape=[jax.ShapeDtypeStruct((t, n), F32) for n in (ATT_COLS, CONV_CH, DN_WIDTH, LANES)],
        compiler_params=_params("parallel"),
        name="inproj",
    )(x, g, w)


TAIL = 8
PAIR = 2 * DN_DK
N_PAIRS = DN_WIDTH // PAIR


def _head_sums(z, pair_ones):
    hi = z.astype(BF16)
    lw = (z - hi.astype(F32)).astype(BF16)
    d = lambda a, p: jnp.dot(a[:, p * PAIR:(p + 1) * PAIR], pair_ones, preferred_element_type=F32)
    return jnp.concatenate([d(hi, p) + d(lw, p) for p in range(N_PAIRS)], axis=1)


def _inproj_conv_kernel(x_ref, g_ref, w_ref, cw_ref, ones_ref, att_ref, qkv_ref, dz_ref, ba_ref, tail_ref,
                        xp_scr, *, tiles_per_seq):
    tm = x_ref.shape[0]

    @pl.when(pl.program_id(0) % tiles_per_seq == 0)
    def _():
        xp_scr[0:TAIL, :] = jnp.zeros((TAIL, CONV_CH), F32)

    xn = _rmsnorm(x_ref[...], g_ref[...]).astype(BF16)
    o0, o1, o2 = ATT_COLS, ATT_COLS + CONV_CH, ATT_COLS + CONV_CH + DN_WIDTH
    xc = jnp.dot(xn, w_ref[:, o0:o1], preferred_element_type=F32)
    att_ref[...] = jnp.dot(xn, w_ref[:, :o0], preferred_element_type=F32)
    dz_ref[...] = jnp.dot(xn, w_ref[:, o1:o2], preferred_element_type=F32)
    ba_ref[...] = jnp.dot(xn, w_ref[:, o2:], preferred_element_type=F32)

    xp_scr[TAIL:, :] = xc
    y = xp_scr[TAIL - 3:TAIL - 3 + tm, :] * cw_ref[0:1, :]
    y = y + xp_scr[TAIL - 2:TAIL - 2 + tm, :] * cw_ref[1:2, :]
    y = y + xp_scr[TAIL - 1:TAIL - 1 + tm, :] * cw_ref[2:3, :]
    y = y + xc * cw_ref[3:4, :]
    tail = xc[tm - TAIL:, :]
    xp_scr[0:TAIL, :] = tail
    tail_ref[0] = tail
    y = _silu(y)
    q = y[:, :DN_WIDTH]
    k = y[:, DN_WIDTH:2 * DN_WIDTH]
    inv_norm = lax.rsqrt(_head_sums(jnp.concatenate([q * q, k * k], axis=0), ones_ref[...]) + EPS)
    qkv_ref[:, :DN_WIDTH] = q * inv_norm[:tm] * (DN_DK ** -0.5)
    qkv_ref[:, DN_WIDTH:2 * DN_WIDTH] = k * inv_norm[tm:]
    qkv_ref[:, 2 * DN_WIDTH:] = y[:, 2 * DN_WIDTH:]


def _pair_ones():
    lane = np.arange(PAIR)
    return jnp.asarray((lane[:, None] // DN_DV == lane[None, :] // DN_DV).astype(np.float32), dtype=BF16)


def _inproj_conv(x, g, w, conv_w, seq):
    t = x.shape[0]
    tm = ROW_TM
    assert seq % tm == 0
    ones = _pair_ones()
    row = lambda n: pl.BlockSpec((tm, n), lambda i: (i, 0))
    full = lambda a: pl.BlockSpec(a.shape, lambda i: (0,) * a.ndim)
    return pl.pallas_call(
        functools.partial(_inproj_conv_kernel, tiles_per_seq=seq // tm),
        grid=(t // tm,),
        in_specs=[row(D_MODEL), full(g), full(w), full(conv_w), full(ones)],
        out_specs=[row(ATT_COLS), row(CONV_CH), row(DN_WIDTH), row(LANES),
                   pl.BlockSpec((1, TAIL, CONV_CH), lambda i: (i, 0, 0))],
        out_shape=[jax.ShapeDtypeStruct((t, n), F32) for n in (ATT_COLS, CONV_CH, DN_WIDTH, LANES)]
                  + [jax.ShapeDtypeStruct((t // tm, TAIL, CONV_CH), F32)],
        scratch_shapes=[pltpu.VMEM((TAIL + tm, CONV_CH), F32)],
        compiler_params=_params("arbitrary"),
        name="inproj_conv",
    )(x, g, w, conv_w, ones)


GROUP_ROWS = GQA * ATT_BLOCK


def _attn_prompt_kernel(cur_ref, prev_ref, bucket_ref, rb_ref, sink_ref, o_ref, bias_scr, sink_scr):
    i = pl.program_id(0)
    nseq = cur_ref.shape[0]

    @pl.when(i == 0)
    def _():
        qi = lax.broadcasted_iota(jnp.int32, (ATT_BLOCK, 2 * ATT_BLOCK), 0)
        kj = lax.broadcasted_iota(jnp.int32, (ATT_BLOCK, 2 * ATT_BLOCK), 1)
        dist = qi + ATT_BLOCK - kj
        band = jnp.logical_and(dist >= 0, dist < WINDOW)
        bucket = bucket_ref[...]
        hrow = lax.broadcasted_iota(jnp.int32, (GROUP_ROWS, 1), 0) // ATT_BLOCK
        for g in range(ATT_KV_HEADS):
            sink_col = jnp.zeros((GROUP_ROWS, 1), F32)
            for hh in range(GQA):
                h = g * GQA + hh
                bias = jnp.where(band, _bias_lookup(bucket, rb_ref, h), NEG_INF)
                bias_scr[0, g, hh * ATT_BLOCK:(hh + 1) * ATT_BLOCK, :] = bias
                bias_scr[1, g, hh * ATT_BLOCK:(hh + 1) * ATT_BLOCK, :] = jnp.where(kj >= ATT_BLOCK, bias, NEG_INF)
                sink_col = jnp.where(hrow == hh, sink_ref[h], sink_col)
            sink_scr[g] = sink_col

    first = (i == 0).astype(jnp.int32)
    probs = [(b, g) for b in range(nseq) for g in range(ATT_KV_HEADS)]
    scores = []
    for b, g in probs:
        cur = cur_ref[b]
        prev = prev_ref[b]
        q = jnp.concatenate([cur[:, (g * GQA + hh) * HEAD_DIM:(g * GQA + hh + 1) * HEAD_DIM]
                             for hh in range(GQA)], axis=0) * (HEAD_DIM ** -0.5)
        kcol = slice(ATT_WIDTH + g * HEAD_DIM, ATT_WIDTH + (g + 1) * HEAD_DIM)
        k2 = jnp.concatenate([prev[:, kcol], cur[:, kcol]], axis=0)
        scores.append(_mm_nt(q, k2) + bias_scr[first, g])
    probs_p, dens = [], []
    for (b, g), s in zip(probs, scores):
        sink = sink_scr[g]
        m = jnp.maximum(jnp.max(s, axis=-1, keepdims=True), sink)
        p = jnp.exp(s - m)
        dens.append(jnp.sum(p, axis=-1, keepdims=True) + jnp.exp(sink - m))
        probs_p.append(p)
    outs = {}
    for (b, g), p, den in zip(probs, probs_p, dens):
        vcol = slice(ATT_WIDTH + KV_WIDTH + g * HEAD_DIM, ATT_WIDTH + KV_WIDTH + (g + 1) * HEAD_DIM)
        v2 = jnp.concatenate([prev_ref[b][:, vcol], cur_ref[b][:, vcol]], axis=0)
        outs[b, g] = _mm(p, v2) / den
    for b in range(nseq):
        o_ref[b] = jnp.concatenate([outs[b, g][hh * ATT_BLOCK:(hh + 1) * ATT_BLOCK, :]
                                    for g in range(ATT_KV_HEADS) for hh in range(GQA)], axis=1)


def _attn_prompt(att, bucket, rel_bias, sink, batch, seq):
    nb = seq // ATT_BLOCK
    smem = pl.BlockSpec(memory_space=pltpu.SMEM)
    att3 = att.reshape(batch, seq, ATT_COLS)
    out = pl.pallas_call(
        _attn_prompt_kernel,
        grid=(nb,),
        in_specs=[
            pl.BlockSpec((batch, ATT_BLOCK, ATT_COLS), lambda i: (0, i, 0)),
            pl.BlockSpec((batch, ATT_BLOCK, ATT_COLS), lambda i: (0, jnp.maximum(i - 1, 0), 0)),
            pl.BlockSpec(bucket.shape, lambda i: (0, 0)),
            smem, smem,
        ],
        out_specs=pl.BlockSpec((batch, ATT_BLOCK, ATT_WIDTH), lambda i: (0, i, 0)),
        out_shape=jax.ShapeDtypeStruct((batch, seq, ATT_WIDTH), F32),
        scratch_shapes=[pltpu.VMEM((2, ATT_KV_HEADS, GROUP_ROWS, 2 * ATT_BLOCK), F32),
                        pltpu.VMEM((ATT_KV_HEADS, GROUP_ROWS, 1), F32)],
        compiler_params=_params("arbitrary"),
        name="attn_prompt",
    )(att3, att3, bucket, rel_bias, sink)
    return out.reshape(batch * seq, ATT_WIDTH)


ATT_S_BB = 8


def _attn_sample_kernel(att_ref, ck_ref, cv_ref, bucket_ref, rb_ref, sink_ref, o_ref,
                        bias_scr, col_scr):
    hrow = lax.broadcasted_iota(jnp.int32, (ATT_HEADS, LANES), 0)
    lane = lax.broadcasted_iota(jnp.int32, (ATT_HEADS, LANES), 1)

    @pl.when(pl.program_id(0) == 0)
    def _():
        bucket = jnp.broadcast_to(bucket_ref[...], (ATT_HEADS, LANES))
        bias = jnp.zeros((ATT_HEADS, LANES), F32)
        cols = jnp.zeros((ATT_HEADS, LANES), F32)
        for h in range(ATT_HEADS):
            bias = jnp.where(hrow == h, _bias_lookup(bucket, rb_ref, h), bias)
            cols = jnp.where(jnp.logical_and(hrow == h, lane == 0), sink_ref[h], cols)
            cols = jnp.where(jnp.logical_and(hrow == h, lane == 1), rb_ref[0, h], cols)
        bias_scr[...] = jnp.where(lane >= 1, bias, NEG_INF)
        col_scr[...] = cols

    bias_c = bias_scr[...]
    sink = col_scr[:, 0:1]
    bias_n = col_scr[:, 1:2]
    same_group = (hrow // GQA) == (lane // HEAD_DIM)
    low_group = lax.broadcasted_iota(jnp.int32, (ATT_HEADS, HEAD_DIM), 0) < GQA
    rnd = lambda a: a.astype(BF16).astype(F32)
    seqs = range(ATT_S_BB)
    rows = [att_ref[b:b + 1, :] for b in seqs]
    q_bds = []
    for row in rows:
        q = row[:, :ATT_WIDTH] * (HEAD_DIM ** -0.5)
        qh = jnp.concatenate([q[:, h * HEAD_DIM:(h + 1) * HEAD_DIM] for h in range(ATT_HEADS)], axis=0)
        q_bds.append(jnp.where(same_group, jnp.concatenate([qh, qh], axis=1), 0.0))
    kv_t = lambda ref, b: ref[b].reshape(KV_WIDTH, WINDOW)
    s_cs = [_mm(q_bd, kv_t(ck_ref, b)) + bias_c for b, q_bd in zip(seqs, q_bds)]
    prs, pns = [], []
    for row, q_bd, s_c in zip(rows, q_bds, s_cs):
        kn = row[:, ATT_WIDTH:ATT_WIDTH + KV_WIDTH]
        s_n = jnp.sum(rnd(q_bd) * rnd(kn), axis=-1, keepdims=True) + bias_n
        m = jnp.maximum(jnp.maximum(jnp.max(s_c, axis=-1, keepdims=True), s_n), sink)
        p_c = jnp.exp(s_c - m)
        p_n = jnp.exp(s_n - m)
        den = jnp.sum(p_c, axis=-1, keepdims=True) + p_n + jnp.exp(sink - m)
        prs.append(p_c / den)
        pns.append(p_n / den)
    pvs = [_mm_nt(pr, kv_t(cv_ref, b)) for b, pr in zip(seqs, prs)]
    for b, row, pv, pn in zip(seqs, rows, pvs, pns):
        vn = row[:, ATT_WIDTH + KV_WIDTH:]
        o_full = pv + rnd(pn) * rnd(vn)
        o_sel = jnp.where(low_group, o_full[:, :HEAD_DIM], o_full[:, HEAD_DIM:])
        o_ref[b:b + 1, :] = jnp.concatenate([o_sel[h:h + 1, :] for h in range(ATT_HEADS)], axis=1)


def _attn_sample(att, ck, cv, bucket, rel_bias, sink):
    nseq = att.shape[0]
    smem = pl.BlockSpec(memory_space=pltpu.SMEM)
    cache = pl.BlockSpec((ATT_S_BB, ATT_KV_HEADS, HEAD_DIM, WINDOW), lambda i: (i, 0, 0, 0))
    return pl.pallas_call(
        _attn_sample_kernel,
        grid=(nseq // ATT_S_BB,),
        in_specs=[pl.BlockSpec((ATT_S_BB, ATT_COLS), lambda i: (i, 0)), cache, cache,
                  pl.BlockSpec(bucket.shape, lambda i: (0, 0)), smem, smem],
        out_specs=pl.BlockSpec((ATT_S_BB, ATT_WIDTH), lambda i: (i, 0)),
        out_shape=jax.ShapeDtypeStruct((nseq, ATT_WIDTH), F32),
        scratch_shapes=[pltpu.VMEM((ATT_HEADS, LANES), F32), pltpu.VMEM((ATT_HEADS, LANES), F32)],
        compiler_params=_params("arbitrary"),
        name="attn_sample",
    )(att, ck, cv, bucket, rel_bias, sink)


GDN_TB = 128
GDN_NC = GDN_TB // DN_CHUNK


def _gdn_gates(ba, alog, dtb):
    beta = _sigmoid(ba)
    g = -jnp.exp(alog) * _softplus(ba + dtb)
    return beta, g


def _pair_diag(x, lo):
    xb = x.astype(BF16)
    zero = jnp.zeros_like(xb)
    return jnp.concatenate([jnp.where(lo, xb, zero), jnp.where(lo, zero, xb)], axis=0)


def _gdn_prompt_kernel(qkv_ref, dz_ref, ba_ref, alog_ref, dtb_ref, dnx_ref,
                       hsum_ref, expb_ref, expg_ref, ltri_ref,
                       o_ref, s_out_ref, s_scr):
    i = pl.program_id(0)
    nb = qkv_ref.shape[0]

    @pl.when(i == 0)
    def _():
        s_scr[...] = jnp.zeros(s_scr.shape, F32)

    hsum = hsum_ref[...]
    ri = lax.broadcasted_iota(jnp.int32, (DN_CHUNK, PAIR), 0)
    ci = lax.broadcasted_iota(jnp.int32, (DN_CHUNK, PAIR), 1)
    lo = ci < DN_DK
    cj = jnp.where(lo, ci, ci - DN_DK)
    causal = ri >= cj
    strict = ri > cj
    eye = (ri == cj).astype(F32)

    def sel2(x, m):
        hi = x.astype(BF16)
        lw = (x - hi.astype(F32)).astype(BF16)
        return (jnp.dot(hi, m, preferred_element_type=F32) + jnp.dot(lw, m, preferred_element_type=F32))

    pre = []
    for b in range(nb):
        q = qkv_ref[b, :, :DN_WIDTH]
        k = qkv_ref[b, :, DN_WIDTH:2 * DN_WIDTH]
        v = qkv_ref[b, :, 2 * DN_WIDTH:]
        beta_c, g_c = _gdn_gates(ba_ref[b], alog_ref[...], dtb_ref[...])
        beta = sel2(beta_c, expb_ref[...])
        gam_c = _mm_sel_lhs(ltri_ref[...], g_c)
        gam = _mm_sel_rhs(gam_c, expg_ref[...])
        gam_t = gam_c.T
        kb = k * beta
        egam = jnp.exp(gam)
        pre.append(dict(q=q, k=k, kb=kb, vb=v * beta, qg=q * egam, wr=kb * egam, gam=gam, gam_t=gam_t))

    probs = [(b, p) for b in range(nb) for p in range(N_PAIRS)]
    pick = lambda m: jnp.where(lo, m[:DN_DK], m[DN_DK:])
    o_rows = [[] for _ in range(nb)]
    for c in range(GDN_NC):
        r0, r1 = c * DN_CHUNK, (c + 1) * DN_CHUNK
        sl = lambda name, b, p: pre[b][name][r0:r1, p * PAIR:(p + 1) * PAIR]
        raws = []
        for b, p in probs:
            k_p = sl("k", b, p)
            k_rows = jnp.concatenate([jnp.where(lo, k_p, 0.0), jnp.where(lo, 0.0, k_p)], axis=0)
            raws.append(_mm_nt(jnp.concatenate([sl("kb", b, p), sl("q", b, p)], axis=0), k_rows))
        pws, ts, qks = [], [], []
        for (b, p), raw in zip(probs, raws):
            gcol = sl("gam", b, p)
            h0 = DN_HEADS + 2 * p
            gam_t = pre[b]["gam_t"]
            grow = jnp.concatenate([gam_t[h0:h0 + 1, r0:r1], gam_t[h0 + 1:h0 + 2, r0:r1]], axis=1)
            decay = jnp.exp(jnp.where(causal, gcol - grow, NEG_INF))
            a = jnp.where(strict, raw[:DN_CHUNK] * decay, 0.0)
            qks.append(jnp.where(causal, raw[DN_CHUNK:] * decay, 0.0))
            pws.append(-a)
            ts.append(eye - a)
        pws = [_mm(pw, _pair_diag(pw, lo)) for pw in pws]
        for _ in range(4):
            rs = [_mm(jnp.concatenate([pw, t], axis=0), _pair_diag(pw, lo)) for pw, t in zip(pws, ts)]
            pws = [r[:DN_CHUNK] for r in rs]
            ts = [t + r[DN_CHUNK:] for t, r in zip(ts, rs)]
        rs = [_mm(t, _pair_diag(pw, lo)) for pw, t in zip(pws, ts)]
        ts = [t + r for t, r in zip(ts, rs)]
        sols = [_mm(t, jnp.concatenate([_pair_diag(sl("vb", b, p), lo), _pair_diag(sl("wr", b, p), lo)],
                                       axis=1)) for (b, p), t in zip(probs, ts)]
        qkuws = [_mm(qk, jnp.concatenate([_pair_diag(s[:, :PAIR], lo), _pair_diag(s[:, PAIR:], lo)], axis=1))
                 for qk, s in zip(qks, sols)]
        crosses, gls = [], []
        for (b, p), s in zip(probs, sols):
            gam_last = pre[b]["gam"][r1 - 1:r1, p * PAIR:(p + 1) * PAIR]
            kd = sl("k", b, p) * jnp.exp(gam_last - sl("gam", b, p))
            crosses.append(_mm_tn(kd, s))
            gls.append(jnp.exp(gam_last))
        lhs = [jnp.concatenate([pick(cr[:, PAIR:]), sl("qg", b, p) - qkuw[:, PAIR:]], axis=0)
               for (b, p), cr, qkuw in zip(probs, crosses, qkuws)]
        s_olds = [s_scr[b, p] for b, p in probs]
        rs = [_mm(l, _pair_diag(s_old, lo)) for l, s_old in zip(lhs, s_olds)]
        o_pairs = [[] for _ in range(nb)]
        for (b, p), r, s_old, gl, cr, qkuw in zip(probs, rs, s_olds, gls, crosses, qkuws):
            s_scr[b, p] = gl * s_old - r[:DN_DK] + pick(cr[:, :PAIR])
            o_pairs[b].append(r[DN_DK:] + qkuw[:, :PAIR])
        for b in range(nb):
            o_rows[b].append(jnp.concatenate(o_pairs[b], axis=1))

    o_all = jnp.concatenate([jnp.concatenate(rows, axis=0) for rows in o_rows], axis=0)
    inv_rms = lax.rsqrt(_head_sums(o_all * o_all, hsum) * (1.0 / DN_DV) + EPS)
    for b in range(nb):
        rows = slice(b * GDN_TB, (b + 1) * GDN_TB)
        o_ref[b] = o_all[rows] * inv_rms[rows] * dnx_ref[...] * _silu(dz_ref[b])

    @pl.when(i == pl.num_programs(0) - 1)
    def _():
        for b in range(nb):
            for p in range(N_PAIRS):
                s_p = s_scr[b, p]
                s_out_ref[b, 2 * p] = s_p[:, :DN_DV]
                s_out_ref[b, 2 * p + 1] = s_p[:, DN_DV:]


def _gdn_consts():
    lane = np.arange(DN_WIDTH)
    pl_lane = np.arange(PAIR)
    hsum = (pl_lane[:, None] // DN_DV == pl_lane[None, :] // DN_DV)
    src = np.arange(LANES)
    expb = (src[:, None] == lane[None, :] // DN_DV)
    expg = (src[:, None] == DN_HEADS + lane[None, :] // DN_DV)
    tok = np.arange(GDN_TB)
    ltri = np.logical_and(tok[:, None] >= tok[None, :],
                          tok[:, None] // DN_CHUNK == tok[None, :] // DN_CHUNK)
    as_bf16 = lambda m: jnp.asarray(m.astype(np.float32), dtype=BF16)
    return as_bf16(hsum), as_bf16(expb), as_bf16(expg), as_bf16(ltri)


def _gdn_prompt(xc, dz, ba, alog, dtb, dnx, batch, seq):
    nt = seq // GDN_TB
    hsum, expb, expg, ltri = _gdn_consts()
    row = lambda n: pl.BlockSpec((batch, GDN_TB, n), lambda i: (0, i, 0))
    full = lambda a: pl.BlockSpec(a.shape, lambda i: (0,) * a.ndim)
    consts = (alog, dtb, dnx, hsum, expb, expg, ltri)
    as3d = lambda a: a.reshape(batch, seq, a.shape[-1])
    o, s = pl.pallas_call(
        _gdn_prompt_kernel,
        grid=(nt,),
        in_specs=[row(CONV_CH), row(DN_WIDTH), row(LANES)] + [full(a) for a in consts],
        out_specs=[row(DN_WIDTH),
                   pl.BlockSpec((batch, DN_HEADS, DN_DK, DN_DV), lambda i: (0, 0, 0, 0))],
        out_shape=[jax.ShapeDtypeStruct((batch, seq, DN_WIDTH), F32),
                   jax.ShapeDtypeStruct((batch, DN_HEADS, DN_DK, DN_DV), F32)],
        scratch_shapes=[pltpu.VMEM((batch, N_PAIRS, DN_DK, PAIR), F32)],
        compiler_params=_params("arbitrary"),
        name="gdn_prompt",
    )(as3d(xc), as3d(dz), as3d(ba), *consts)
    return o.reshape(batch * seq, DN_WIDTH), s


GDN_S_BB = 8


def _gdn_sample_kernel(xc_ref, dz_ref, ba_ref, sc_ref, s_ref, cw_ref, alog_ref, dtb_ref, dn_ref,
                       hsum_ref, eye_ref, hsel_ref, hrep3_ref, o_ref, s_out_ref):
    xc = xc_ref[...]
    y = sc_ref[0] * cw_ref[0:1, :]
    y = y + sc_ref[1] * cw_ref[1:2, :]
    y = y + sc_ref[2] * cw_ref[2:3, :]
    y = _silu(y + xc * cw_ref[3:4, :])
    hsum = hsum_ref[...]
    q = y[:, :DN_WIDTH]
    k = y[:, DN_WIDTH:2 * DN_WIDTH]
    v = y[:, 2 * DN_WIDTH:]
    q = q * lax.rsqrt(_mm_sel_rhs(q * q, hsum) + EPS) * (DN_DK ** -0.5)
    k = k * lax.rsqrt(_mm_sel_rhs(k * k, hsum) + EPS)
    beta_c, g_c = _gdn_gates(ba_ref[...], alog_ref[...], dtb_ref[...])
    eg_c = jnp.exp(g_c)
    eye = eye_ref[...]
    tr = lambda a: lax.dot_general(a, eye, (((0,), (0,)), ((), ())), precision=lax.Precision.HIGHEST,
                                   preferred_element_type=F32)
    gates_t = tr(jnp.concatenate([beta_c, eg_c], axis=1))
    beta_t = gates_t[:LANES]
    eg_t = gates_t[LANES:]
    dz = dz_ref[...]
    dn = dn_ref[...]
    split = lambda r: jnp.concatenate([r[:, h * DN_DV:(h + 1) * DN_DV] for h in range(DN_HEADS)], axis=0)
    own_head = hsel_ref[...].astype(F32)
    hrep3 = hrep3_ref[...]
    seqs = range(GDN_S_BB)
    dot = lambda a, b: jnp.dot(a.astype(BF16), b.astype(BF16), preferred_element_type=F32)

    def pieces(x):
        p1 = x.astype(BF16).astype(F32)
        r1 = x - p1
        p2 = r1.astype(BF16).astype(F32)
        return p1, p2, (r1 - p2).astype(BF16).astype(F32)

    heads = DN_HEADS
    k_pieces, kqs = [], []
    for b in seqs:
        kq_bd = jnp.concatenate([own_head * k[b:b + 1, :], own_head * q[b:b + 1, :]], axis=0)
        a1, a2, a3 = pieces(kq_bd)
        s1, s2, s3 = pieces(s_ref[b])
        r1 = dot(jnp.concatenate([a1, a2, a3], axis=0), s1)
        r2 = dot(jnp.concatenate([a1, a2], axis=0), s2)
        r3 = dot(a1, s3)
        n = 2 * heads
        kqs.append(((r3 + r2[n:] + r1[2 * n:]) + (r2[:n] + r1[n:2 * n])) + r1[:n])
        k_pieces.append((a1[:heads], a2[:heads], a3[:heads]))
    egs = [eg_t[DN_HEADS:2 * DN_HEADS, b:b + 1] for b in seqs]
    qks = [jnp.sum(split(q[b:b + 1, :]) * split(k[b:b + 1, :]), axis=-1, keepdims=True) for b in seqs]
    v_news = [beta_t[0:DN_HEADS, b:b + 1] * (split(v[b:b + 1, :]) - eg * kq[:heads])
              for b, eg, kq in zip(seqs, egs, kqs)]
    os_ = [eg * kq[heads:] + qk * v_new for eg, kq, qk, v_new in zip(egs, kqs, qks, v_news)]
    inv_rms = [lax.rsqrt(jnp.mean(o * o, axis=-1, keepdims=True) + EPS) for o in os_]
    for b, o, r in zip(seqs, os_, inv_rms):
        o_ref[b] = o * r * dn * _silu(split(dz[b:b + 1, :]))
    outers, egrows = [], []
    for (k1, k2, k3), v_new, eg in zip(k_pieces, v_news, egs):
        v1, v2, v3 = pieces(v_new)
        lhs = jnp.concatenate([k1, k1, k2, k1, k2, k3], axis=0).astype(BF16)
        rhs = jnp.concatenate([v1, v2, v1, v3, v2, v1], axis=0).astype(BF16)
        outers.append(lax.dot_general(lhs, rhs, (((0,), (0,)), ((), ())), preferred_element_type=F32))
        egrows.append(dot(hrep3, jnp.concatenate(pieces(jnp.broadcast_to(eg, (DN_HEADS, DN_DV))), axis=0)))
    for b, outer, egrow in zip(seqs, outers, egrows):
        s_out_ref[b] = s_ref[b] * egrow + outer


def _gdn_sample(xc, dz, ba, sconv_t, state, conv_w, alog, dtb, dn):
    nseq = xc.shape[0]
    lane = np.arange(DN_WIDTH)
    hsum = jnp.asarray((lane[:, None] // DN_DV == lane[None, :] // DN_DV).astype(np.float32), dtype=BF16)
    eye = jnp.eye(GDN_S_BB, dtype=F32)
    hsel_np = (np.arange(DN_HEADS)[:, None] == lane[None, :] // DN_DK).astype(np.float32)
    hsel = jnp.asarray(hsel_np, dtype=BF16)
    hrep3 = jnp.asarray(np.tile(hsel_np.T, (1, 3)), dtype=BF16)
    row = lambda n: pl.BlockSpec((GDN_S_BB, n), lambda i: (i, 0))
    full = lambda a: pl.BlockSpec(a.shape, lambda i: (0,) * a.ndim)
    st = pl.BlockSpec((GDN_S_BB, DN_HEADS * DN_DK, DN_DV), lambda i: (i, 0, 0))
    consts = (conv_w, alog, dtb, dn, hsum, eye, hsel, hrep3)
    return pl.pallas_call(
        _gdn_sample_kernel,
        grid=(nseq // GDN_S_BB,),
        in_specs=[row(CONV_CH), row(DN_WIDTH), row(LANES),
                  pl.BlockSpec((CONV_WIDTH - 1, GDN_S_BB, CONV_CH), lambda i: (0, i, 0)), st]
                 + [full(a) for a in consts],
        out_specs=[pl.BlockSpec((GDN_S_BB, DN_HEADS, DN_DV), lambda i: (i, 0, 0)), st],
        out_shape=[jax.ShapeDtypeStruct((nseq, DN_HEADS, DN_DV), F32),
                   jax.ShapeDtypeStruct(state.shape, F32)],
        compiler_params=_params("parallel"),
        name="gdn_sample",
    )(xc, dz, ba, sconv_t, state, *consts)


def _attn_sample_lanes_kernel(att_ref, ck_ref, cv_ref, bucket_ref, rb_ref, sink_ref, o_ref, s_scr):
    g = pl.program_id(0)
    nseq = att_ref.shape[0]
    rnd = lambda a: a.astype(BF16).astype(F32)
    att = att_ref[...]
    q_all_t = (att[:, :ATT_WIDTH] * (HEAD_DIM ** -0.5)).T
    kv_new_t = att[:, ATT_WIDTH:].T
    qsel = [jnp.where(g == 0, q_all_t[hh * HEAD_DIM:(hh + 1) * HEAD_DIM],
                      q_all_t[(GQA + hh) * HEAD_DIM:(GQA + hh + 1) * HEAD_DIM]) for hh in range(GQA)]
    qr = [rnd(q) for q in qsel]
    kn = rnd(jnp.where(g == 0, kv_new_t[0:HEAD_DIM], kv_new_t[HEAD_DIM:2 * HEAD_DIM]))
    vn = rnd(jnp.where(g == 0, kv_new_t[2 * HEAD_DIM:3 * HEAD_DIM], kv_new_t[3 * HEAD_DIM:]))

    def score_row(j, carry):
        kj = rnd(ck_ref[j, 0])
        for hh in range(GQA):
            s_scr[hh, pl.ds(j, 1), :] = jnp.sum(qr[hh] * kj, axis=0, keepdims=True)
        return carry
    lax.fori_loop(0, WINDOW, score_row, 0, unroll=2)

    bucket = bucket_ref[...]
    jrow = lax.broadcasted_iota(jnp.int32, (WINDOW, nseq), 0)
    prn = []
    for hh in range(GQA):
        h = g * GQA + hh
        bias = jnp.where(jrow >= 1, _bias_lookup(bucket, rb_ref, h), NEG_INF)
        s = s_scr[hh] + bias
        s_n = jnp.sum(qr[hh] * kn, axis=0, keepdims=True) + rb_ref[0, h]
        sink = sink_ref[h]
        m = jnp.maximum(jnp.maximum(jnp.max(s, axis=0, keepdims=True), s_n), sink)
        p = jnp.exp(s - m)
        p_n = jnp.exp(s_n - m)
        den = jnp.sum(p, axis=0, keepdims=True) + p_n + jnp.exp(sink - m)
        s_scr[hh] = rnd(p / den)
        prn.append(rnd(p_n / den))

    def value_row(j, acc):
        vj = rnd(cv_ref[j, 0])
        return tuple(acc[hh] + s_scr[hh, pl.ds(j, 1), :] * vj for hh in range(GQA))
    zero = jnp.zeros((HEAD_DIM, nseq), F32)
    acc = lax.fori_loop(0, WINDOW, value_row, (zero,) * GQA, unroll=2)
    for hh in range(GQA):
        o_ref[hh * HEAD_DIM:(hh + 1) * HEAD_DIM, :] = acc[hh] + prn[hh] * vn


def _attn_sample_lanes(att, ck_t, cv_t, rel_bias, sink):
    nseq = att.shape[0]
    assert nseq == LANES
    bucket = jnp.asarray(np.broadcast_to(_t5_bucket_np(WINDOW - np.arange(WINDOW))[:, None], (WINDOW, nseq)))
    smem = pl.BlockSpec(memory_space=pltpu.SMEM)
    cache = pl.BlockSpec((WINDOW, 1, HEAD_DIM, nseq), lambda g: (0, g, 0, 0))
    full = lambda a: pl.BlockSpec(a.shape, lambda g: (0,) * a.ndim)
    return pl.pallas_call(
        _attn_sample_lanes_kernel,
        grid=(ATT_KV_HEADS,),
        in_specs=[full(att), cache, cache, full(bucket), smem, smem],
        out_specs=pl.BlockSpec((GQA * HEAD_DIM, nseq), lambda g: (g, 0)),
        out_shape=jax.ShapeDtypeStruct((ATT_WIDTH, nseq), F32),
        scratch_shapes=[pltpu.VMEM((GQA, WINDOW, nseq), F32)],
        compiler_params=_params("arbitrary"),
        name="attn_sample_lanes",
    )(att, ck_t, cv_t, bucket, rel_bias, sink)


def _gdn_sample_front_kernel(xc_ref, dz_ref, ba_ref, sc_ref, cw_ref, alog_ref, dtb_ref, hsum_ref,
                             q_ref, k_ref, v_ref, dz_t_ref, gates_ref):
    xc = xc_ref[...]
    y = sc_ref[0] * cw_ref[0:1, :]
    y = y + sc_ref[1] * cw_ref[1:2, :]
    y = y + sc_ref[2] * cw_ref[2:3, :]
    y = _silu(y + xc * cw_ref[3:4, :])
    hsum = hsum_ref[...]
    q = y[:, :DN_WIDTH]
    k = y[:, DN_WIDTH:2 * DN_WIDTH]
    q = q * lax.rsqrt(_mm_sel_rhs(q * q, hsum) + EPS) * (DN_DK ** -0.5)
    k = k * lax.rsqrt(_mm_sel_rhs(k * k, hsum) + EPS)
    beta_c, g_c = _gdn_gates(ba_ref[...], alog_ref[...], dtb_ref[...])
    q_ref[...] = q.T
    k_ref[...] = k.T
    v_ref[...] = y[:, 2 * DN_WIDTH:].T
    dz_t_ref[...] = dz_ref[...].T
    gates_ref[0:LANES, :] = beta_c.T
    gates_ref[LANES:, :] = jnp.exp(g_c).T


def _gdn_sample_step_kernel(q_ref, k_ref, v_ref, dz_ref, gates_ref, dn_ref, s_ref, o_ref, s_out_ref):
    h = pl.program_id(0)
    beta = gates_ref[pl.ds(h, 1), :]
    eg = gates_ref[pl.ds(LANES + DN_HEADS + h, 1), :]
    q, k, v = q_ref[...], k_ref[...], v_ref[...]
    w = (k * beta) * eg
    qg = q * eg
    ws = jnp.zeros(v.shape, F32)
    qs = jnp.zeros(v.shape, F32)
    for dk in range(DN_DK):
        s_dk = s_ref[0, dk]
        ws = ws + w[dk:dk + 1, :] * s_dk
        qs = qs + qg[dk:dk + 1, :] * s_dk
    v_new = v * beta - ws
    qk = jnp.sum(q * k, axis=0, keepdims=True)
    o = qs + qk * v_new
    for dk in range(DN_DK):
        s_out_ref[0, dk] = s_ref[0, dk] * eg + k[dk:dk + 1, :] * v_new
    o = o * lax.rsqrt(jnp.mean(o * o, axis=0, keepdims=True) + EPS) * dn_ref[...]
    o_ref[...] = o * _silu(dz_ref[...])


def _gdn_sample_lanes(xc, dz, ba, sconv_t, state_t, conv_w, alog, dtb, dn):
    nseq = xc.shape[0]
    assert nseq == LANES
    lane = np.arange(DN_WIDTH)
    hsum = jnp.asarray((lane[:, None] // DN_DV == lane[None, :] // DN_DV).astype(np.float32), dtype=BF16)
    full = lambda a: pl.BlockSpec(a.shape, lambda i: (0,) * a.ndim)
    cm = jax.ShapeDtypeStruct((DN_WIDTH, nseq), F32)
    front_in = (xc, dz, ba, sconv_t, conv_w, alog, dtb, hsum)
    q_t, k_t, v_t, dz_t, gates_t = pl.pallas_call(
        _gdn_sample_front_kernel,
        grid=(1,),
        in_specs=[full(a) for a in front_in],
        out_specs=[pl.BlockSpec((DN_WIDTH, nseq), lambda i: (0, 0))] * 4
                  + [pl.BlockSpec((2 * LANES, nseq), lambda i: (0, 0))],
        out_shape=[cm, cm, cm, cm, jax.ShapeDtypeStruct((2 * LANES, nseq), F32)],
        compiler_params=_params("arbitrary"),
        name="gdn_sample_front",
    )(*front_in)
    dn_b = jnp.broadcast_to(dn.reshape(DN_DV, 1), (DN_DV, nseq))
    head = pl.BlockSpec((DN_DK, nseq), lambda h: (h, 0))
    st = pl.BlockSpec((1, DN_DK, DN_DV, nseq), lambda h: (h, 0, 0, 0))
    return pl.pallas_call(
        _gdn_sample_step_kernel,
        grid=(DN_HEADS,),
        in_specs=[head, head, head, head, full(gates_t), full(dn_b), st],
        out_specs=[head, st],
        out_shape=[cm, jax.ShapeDtypeStruct(state_t.shape, F32)],
        compiler_params=_params("parallel"),
        name="gdn_sample_step",
    )(q_t, k_t, v_t, dz_t, gates_t, dn_b, state_t)


def _route(xn, wr):
    logits = jnp.dot(xn, wr, preferred_element_type=F32)
    lane = lax.broadcasted_iota(jnp.int32, logits.shape, 1).astype(F32)
    first_at = lambda hit: jnp.min(jnp.where(hit, lane, float(LANES)), axis=-1, keepdims=True)
    glog = jnp.where(lane < N_GROUPS, logits, NEG_INF)
    gmax = jnp.max(glog, axis=-1, keepdims=True)
    gsel = first_at(glog == gmax)
    pgsel = 1.0 / jnp.sum(jnp.exp(glog - gmax), axis=-1, keepdims=True)
    lo = ROUTER_OFF + gsel * EXPERTS_PER_GROUP
    in_group = jnp.logical_and(lane >= lo, lane < lo + EXPERTS_PER_GROUP)
    elog = jnp.where(in_group, logits, NEG_INF)
    m1 = jnp.max(elog, axis=-1, keepdims=True)
    i1 = first_at(elog == m1)
    z = jnp.sum(jnp.exp(elog - m1), axis=-1, keepdims=True)
    elog2 = jnp.where(lane == i1, NEG_INF, elog)
    m2 = jnp.max(elog2, axis=-1, keepdims=True)
    i2 = first_at(elog2 == m2)
    p1 = 1.0 / z
    p2 = jnp.exp(m2 - m1) / z
    tot = p1 + p2
    return lane, i1, i2, p1 / tot * pgsel, p2 / tot * pgsel


def _outproj(x_ref, oa_ref, od_ref, wo_ref):
    return x_ref[...] + _mm(oa_ref[...], wo_ref[:ATT_WIDTH, :]) + _mm(od_ref[...], wo_ref[ATT_WIDTH:, :])


def _outproj_router_kernel(x_ref, oa_ref, od_t_ref, wo_ref, g_ref, wr_ref, h_ref, xn_ref, gate_ref):
    h = (x_ref[...] + _mm(oa_ref[...], wo_ref[:ATT_WIDTH, :])
         + _mm(od_t_ref[...].T, wo_ref[ATT_WIDTH:, :]))
    h_ref[...] = h
    xn = _rmsnorm(h, g_ref[...]).astype(BF16)
    xn_ref[...] = xn
    lane, i1, i2, g1, g2 = _route(xn, wr_ref[...])
    gate_ref[...] = jnp.where(lane == i1, g1, 0.0) + jnp.where(lane == i2, g2, 0.0)


def _outproj_router(x, oa, od_t, wo, g, wr):
    t = x.shape[0]
    tm = t
    row = lambda n: pl.BlockSpec((tm, n), lambda i: (i, 0))
    full = lambda a: pl.BlockSpec(a.shape, lambda i: (0,) * a.ndim)
    return pl.pallas_call(
        _outproj_router_kernel,
        grid=(t // tm,),
        in_specs=[row(D_MODEL), row(ATT_WIDTH), full(od_t), full(wo), full(g), full(wr)],
        out_specs=[row(D_MODEL), row(D_MODEL), row(LANES)],
        out_shape=[jax.ShapeDtypeStruct((t, D_MODEL), F32), jax.ShapeDtypeStruct((t, D_MODEL), BF16),
                   jax.ShapeDtypeStruct((t, LANES), F32)],
        compiler_params=_params("parallel"),
        name="outproj_router",
    )(x, oa, od_t, wo, g, wr)


def _moe_kernel(xn_ref, gate_ref, wg_ref, wu_ref, wd_ref, o_ref):
    e = pl.program_id(1)
    xn = xn_ref[...]
    lane = lax.broadcasted_iota(jnp.int32, gate_ref.shape, 1)
    gate = jnp.sum(jnp.where(lane == e + ROUTER_OFF, gate_ref[...], 0.0), axis=-1, keepdims=True)
    hg = jnp.dot(xn, wg_ref[...].astype(BF16), preferred_element_type=F32)
    hu = jnp.dot(xn, wu_ref[...].astype(BF16), preferred_element_type=F32)
    hm = _silu(hg) * hu * gate
    y = jnp.dot(hm.astype(BF16), wd_ref[...].astype(BF16), preferred_element_type=F32)

    @pl.when(e == 0)
    def _():
        o_ref[...] = y

    @pl.when(e > 0)
    def _():
        o_ref[...] += y


def _moe(xn, gates, wg, wu, wd):
    t = xn.shape[0]
    tm = min(t, 1024)
    return pl.pallas_call(
        _moe_kernel,
        grid=(t // tm, N_EXPERTS),
        in_specs=[pl.BlockSpec((tm, D_MODEL), lambda i, e: (i, 0)),
                  pl.BlockSpec((tm, LANES), lambda i, e: (i, 0)),
                  pl.BlockSpec((None, D_MODEL, D_EXPERT), lambda i, e: (e, 0, 0)),
                  pl.BlockSpec((None, D_MODEL, D_EXPERT), lambda i, e: (e, 0, 0)),
                  pl.BlockSpec((None, D_EXPERT, D_MODEL), lambda i, e: (e, 0, 0))],
        out_specs=pl.BlockSpec((tm, D_MODEL), lambda i, e: (i, 0)),
        out_shape=jax.ShapeDtypeStruct((t, D_MODEL), F32),
        compiler_params=_params("parallel", "arbitrary"),
        name="moe",
    )(xn, gates, wg, wu, wd)


MOE_TM = 512
POS_TM = 1024
INFO_G1, INFO_G2, INFO_E1, INFO_E2 = 0, 1, 2, 3
DMA_UNROLL = 8


def _moe_tiles(t):
    return (2 * t) // MOE_TM + N_EXPERTS


HALF = D_MODEL // 2
U32 = jnp.uint32


def _pack_rows(x):
    bits = lambda v: lax.bitcast_convert_type(v.astype(BF16).astype(F32), U32)
    return bits(x[:, HALF:]) | (bits(x[:, :HALF]) >> 16)


def _unpack_rows(w):
    lo = lax.bitcast_convert_type(w << 16, F32)
    hi = lax.bitcast_convert_type(w & jnp.uint32(0xFFFF0000), F32)
    return lo, hi


def _route_kernel(x_ref, oa_ref, od_ref, wo_ref, g_ref, wr_ref, h_ref, xn_ref, info_ref, cnt_ref, run_scr):
    h = _outproj(x_ref, oa_ref, od_ref, wo_ref)
    h_ref[...] = h
    xn = _rmsnorm(h, g_ref[...])
    xn_ref[...] = _pack_rows(xn)
    lane, i1, i2, g1, g2 = _route(xn.astype(BF16), wr_ref[...])
    info = jnp.where(lane == INFO_G1, g1, 0.0) + jnp.where(lane == INFO_G2, g2, 0.0)
    info = info + jnp.where(lane == INFO_E1, i1, 0.0) + jnp.where(lane == INFO_E2, i2, 0.0)
    info_ref[...] = info

    @pl.when(pl.program_id(0) == 0)
    def _():
        run_scr[...] = jnp.zeros(run_scr.shape, F32)
    picked = jnp.logical_or(lane == i1, lane == i2).astype(F32)
    run_scr[...] += jnp.sum(picked, axis=0, keepdims=True)
    cnt_ref[...] = run_scr[...]


def _route_sparse(x, oa, od, wo, g, wr):
    t = x.shape[0]
    tm = ROW_TM
    row = lambda n: pl.BlockSpec((tm, n), lambda i: (i, 0))
    full = lambda a: pl.BlockSpec(a.shape, lambda i: (0,) * a.ndim)
    return pl.pallas_call(
        _route_kernel,
        grid=(t // tm,),
        in_specs=[row(D_MODEL), row(ATT_WIDTH), row(DN_WIDTH), full(wo), full(g), full(wr)],
        out_specs=[row(D_MODEL), row(HALF), row(LANES), pl.BlockSpec((1, LANES), lambda i: (0, 0))],
        out_shape=[jax.ShapeDtypeStruct((t, D_MODEL), F32), jax.ShapeDtypeStruct((t, HALF), U32),
                   jax.ShapeDtypeStruct((t, LANES), F32), jax.ShapeDtypeStruct((1, LANES), F32)],
        scratch_shapes=[pltpu.VMEM((1, LANES), F32)],
        compiler_params=_params("arbitrary"),
        name="route",
    )(x, oa, od, wo, g, wr)


def _positions_kernel(info_ref, cnt_ref, ltri_ref, utri_ref, pos_ref, run_scr, off_scr):
    info = info_ref[...]
    lane = lax.broadcasted_iota(jnp.int32, info.shape, 1).astype(F32)
    hit1 = lane == info[:, INFO_E1:INFO_E1 + 1]
    hit2 = lane == info[:, INFO_E2:INFO_E2 + 1]
    onehot = jnp.logical_or(hit1, hit2).astype(F32)

    @pl.when(pl.program_id(0) == 0)
    def _():
        tiles = jnp.floor((cnt_ref[...] + (MOE_TM - 1)) * (1.0 / MOE_TM))
        off_scr[...] = MOE_TM * jnp.dot(tiles.astype(BF16), utri_ref[...], preferred_element_type=F32)
        run_scr[...] = jnp.zeros(run_scr.shape, F32)

    before = (jnp.dot(ltri_ref[...], onehot.astype(BF16), preferred_element_type=F32)
              + run_scr[...] + off_scr[...])
    pos1 = jnp.sum(jnp.where(hit1, before, 0.0), axis=-1, keepdims=True)
    pos2 = jnp.sum(jnp.where(hit2, before, 0.0), axis=-1, keepdims=True)
    pos_ref[...] = (jnp.where(lane == 0, pos1, 0.0) + jnp.where(lane == 1, pos2, 0.0)).astype(jnp.int32)
    run_scr[...] += jnp.sum(onehot, axis=0, keepdims=True)


def _positions(info, cnt):
    t = info.shape[0]
    tm = min(t, POS_TM)
    tok = np.arange(tm)
    ltri = jnp.asarray((tok[:, None] > tok[None, :]).astype(np.float32), dtype=BF16)
    ln = np.arange(LANES)
    utri = jnp.asarray((ln[:, None] < ln[None, :]).astype(np.float32), dtype=BF16)
    full = lambda a: pl.BlockSpec(a.shape, lambda i: (0,) * a.ndim)
    return pl.pallas_call(
        _positions_kernel,
        grid=(t // tm,),
        in_specs=[pl.BlockSpec((tm, LANES), lambda i: (i, 0)), full(cnt), full(ltri), full(utri)],
        out_specs=pl.BlockSpec((tm, LANES), lambda i: (i, 0)),
        out_shape=jax.ShapeDtypeStruct((t, LANES), jnp.int32),
        scratch_shapes=[pltpu.VMEM((1, LANES), F32), pltpu.VMEM((1, LANES), F32)],
        compiler_params=_params("arbitrary"),
        name="positions",
    )(info, cnt, ltri, utri)


def _row_copy(src_hbm, src_row, dst_hbm, dst_row, sem):
    return pltpu.make_async_copy(src_hbm.at[pl.ds(src_row, 1)], dst_hbm.at[pl.ds(dst_row, 1)], sem)


SCATTER_SLOTS = 3


def _scatter_kernel(pos1_ref, pos2_ref, last_ref, used_ref, nt_ref, xn_hbm, zero_hbm, xs_hbm,
                    buf, lsem, sem, zsem, *, n_tok):
    max_tiles = xs_hbm.shape[0] // MOE_TM

    def zero_tile(tile):
        return pltpu.make_async_copy(zero_hbm, xs_hbm.at[pl.ds(tile * MOE_TM, MOE_TM)], zsem)

    def for_unused(fn):
        def body(tile, carry):
            fn(tile)
            return carry
        lax.fori_loop(nt_ref[0], max_tiles, body, 0)

    for e in range(N_EXPERTS):
        @pl.when(used_ref[e] > 0)
        def _():
            zero_tile(last_ref[e]).start()
    for_unused(lambda tile: zero_tile(tile).start())
    for e in range(N_EXPERTS):
        @pl.when(used_ref[e] > 0)
        def _():
            zero_tile(last_ref[e]).wait()
    for_unused(lambda tile: zero_tile(tile).wait())

    tm = buf.shape[1]
    n = n_tok // tm

    def load(i):
        return pltpu.make_async_copy(xn_hbm.at[pl.ds(i * tm, tm)], buf.at[i % SCATTER_SLOTS],
                                     lsem.at[i % SCATTER_SLOTS])

    def wait_rows(slot):
        pltpu.make_async_copy(xs_hbm.at[pl.ds(0, 2 * tm)], xs_hbm.at[pl.ds(0, 2 * tm)], sem.at[slot]).wait()

    load(0).start()
    load(1).start()

    def step(i, carry):
        slot = i % SCATTER_SLOTS
        load(i).wait()

        def body(j, c2):
            tok = i * tm + j
            src = buf.at[slot, pl.ds(j, 1)]
            pltpu.make_async_copy(src, xs_hbm.at[pl.ds(pos1_ref[tok], 1)], sem.at[slot]).start()
            pltpu.make_async_copy(src, xs_hbm.at[pl.ds(pos2_ref[tok], 1)], sem.at[slot]).start()
            return c2
        lax.fori_loop(0, tm, body, 0, unroll=DMA_UNROLL)

        @pl.when(i >= 1)
        def _():
            wait_rows((i - 1) % SCATTER_SLOTS)

        @pl.when(i + 2 < n)
        def _():
            load(i + 2).start()
        return carry
    lax.fori_loop(0, n, step, 0)
    wait_rows((n - 1) % SCATTER_SLOTS)


def _scatter_rows(xn, pos1, pos2, last_tile, used, n_tiles, n_rows):
    t = xn.shape[0]
    zero = jnp.zeros((MOE_TM, D_MODEL), F32)
    any_spec = pl.BlockSpec(memory_space=pl.ANY)
    return pl.pallas_call(
        functools.partial(_scatter_kernel, n_tok=t),
        grid_spec=pltpu.PrefetchScalarGridSpec(
            num_scalar_prefetch=5, grid=(1,),
            in_specs=[any_spec, any_spec], out_specs=any_spec,
            scratch_shapes=[pltpu.VMEM((SCATTER_SLOTS, MOE_TM, D_MODEL), F32),
                            pltpu.SemaphoreType.DMA((SCATTER_SLOTS,)),
                            pltpu.SemaphoreType.DMA((SCATTER_SLOTS,)),
                            pltpu.SemaphoreType.DMA]),
        out_shape=jax.ShapeDtypeStruct((n_rows, D_MODEL), F32),
        compiler_params=_params("arbitrary"),
        name="scatter_rows",
    )(pos1, pos2, last_tile, used, n_tiles, xn, zero)


def _experts_kernel(te_ref, tv_ref, nt_ref, xs_ref, wg_ref, wu_ref, wd_ref, ys_ref, wg_s, wu_s, wd_s):
    i = pl.program_id(0)
    used = i < nt_ref[0]

    @pl.when(jnp.logical_or(i == 0, te_ref[i] != te_ref[jnp.maximum(i - 1, 0)]))
    def _():
        wg_s[...] = wg_ref[...].astype(BF16)
        wu_s[...] = wu_ref[...].astype(BF16)
        wd_s[...] = wd_ref[...].astype(BF16)

    @pl.when(used)
    def _():
        row = lax.broadcasted_iota(jnp.int32, xs_ref.shape, 0)
        x_lo, x_hi = _unpack_rows(jnp.where(row < tv_ref[i], xs_ref[...], jnp.uint32(0)))
        x_lo = x_lo.astype(BF16)
        x_hi = x_hi.astype(BF16)
        up = lambda w_s: (jnp.dot(x_lo, w_s[:HALF, :], preferred_element_type=F32)
                          + jnp.dot(x_hi, w_s[HALF:, :], preferred_element_type=F32))
        hm = (_silu(up(wg_s)) * up(wu_s)).astype(BF16)
        ys_ref[...] = _pack_rows(jnp.dot(hm, wd_s[...], preferred_element_type=F32))

    @pl.when(jnp.logical_not(used))
    def _():
        ys_ref[...] = jnp.zeros(ys_ref.shape, U32)


def _experts(xs, tile_expert, tile_valid, n_tiles, wg, wu, wd):
    max_tiles = xs.shape[0] // MOE_TM
    rows = pl.BlockSpec((MOE_TM, HALF), lambda i, te, tv, nt: (i, 0))
    wspec = lambda shape: pl.BlockSpec((None,) + shape, lambda i, te, tv, nt: (te[i], 0, 0))
    return pl.pallas_call(
        _experts_kernel,
        grid_spec=pltpu.PrefetchScalarGridSpec(
            num_scalar_prefetch=3, grid=(max_tiles,),
            in_specs=[rows, wspec((D_MODEL, D_EXPERT)), wspec((D_MODEL, D_EXPERT)),
                      wspec((D_EXPERT, D_MODEL))],
            out_specs=rows,
            scratch_shapes=[pltpu.VMEM((D_MODEL, D_EXPERT), BF16), pltpu.VMEM((D_MODEL, D_EXPERT), BF16),
                            pltpu.VMEM((D_EXPERT, D_MODEL), BF16)]),
        out_shape=jax.ShapeDtypeStruct(xs.shape, U32),
        compiler_params=_params("arbitrary"),
        name="experts",
    )(tile_expert, tile_valid, n_tiles, xs, wg, wu, wd)


def _ple_gather_kernel(pos1_ref, pos2_ref, h_ref, info_ref, p_ref, wpp_ref, wpg_ref, gp_ref, gf_ref,
                       ys_hbm, y_ref, ybuf, sem):
    i = pl.program_id(0)
    n = pl.num_programs(0)
    tm = h_ref.shape[0]

    def issue(tile, slot):
        def body(j, carry):
            tok = tile * tm + j
            pltpu.make_async_copy(ys_hbm.at[pl.ds(pos1_ref[tok], 1)], ybuf.at[slot, 0, pl.ds(j, 1)],
                                  sem.at[slot]).start()
            pltpu.make_async_copy(ys_hbm.at[pl.ds(pos2_ref[tok], 1)], ybuf.at[slot, 1, pl.ds(j, 1)],
                                  sem.at[slot]).start()
            return carry
        lax.fori_loop(0, tm, body, 0, unroll=DMA_UNROLL)

    @pl.when(i == 0)
    def _():
        issue(0, 0)

    @pl.when(i + 1 < n)
    def _():
        issue(i + 1, (i + 1) % 2)

    slot = i % 2
    pltpu.make_async_copy(ybuf.at[slot], ybuf.at[slot], sem.at[slot]).wait()
    info = info_ref[...]
    moe = info[:, INFO_G1:INFO_G1 + 1] * ybuf[slot, 0] + info[:, INFO_G2:INFO_G2 + 1] * ybuf[slot, 1]
    h = h_ref[...] + moe
    hn = _rmsnorm(h, gp_ref[...])
    h = h + _mm(p_ref[...], wpp_ref[...]) * _sigmoid(_mm(hn, wpg_ref[...]))
    y_ref[...] = _rmsnorm(h, gf_ref[...])


def _ple_gather(h, info, p, ys, pos1, pos2, wpp, wpg, gp, gf):
    t = h.shape[0]
    tm = 256
    row = lambda n: pl.BlockSpec((tm, n), lambda i, p1, p2: (i, 0))
    full = lambda a: pl.BlockSpec(a.shape, lambda i, p1, p2: (0,) * a.ndim)
    return pl.pallas_call(
        _ple_gather_kernel,
        grid_spec=pltpu.PrefetchScalarGridSpec(
            num_scalar_prefetch=2, grid=(t // tm,),
            in_specs=[row(D_MODEL), row(LANES), row(PLE_DIM), full(wpp), full(wpg), full(gp), full(gf),
                      pl.BlockSpec(memory_space=pl.ANY)],
            out_specs=row(D_MODEL),
            scratch_shapes=[pltpu.VMEM((2, 2, tm, D_MODEL), F32), pltpu.SemaphoreType.DMA((2,))]),
        out_shape=jax.ShapeDtypeStruct((t, D_MODEL), F32),
        compiler_params=_params("arbitrary"),
        name="ple_gather",
    )(pos1, pos2, h, info, p, wpp, wpg, gp, gf, ys)


SC_IDX = 128
SC_ROWS = 64
SC_WORKERS = 32


def _sc_mesh():
    return plsc.VectorSubcoreMesh(core_axis_name="c", subcore_axis_name="s")


def _sc_windows(t, fn):
    per_worker = t // SC_WORKERS
    worker = lax.axis_index(("c", "s"))

    @pl.loop(0, per_worker // SC_IDX)
    def _(w):
        fn(worker * per_worker + w * SC_IDX)


def _sc_scatter_rows(xn, pos1, pos2, n_rows):
    t, d = xn.shape
    assert t % (SC_WORKERS * SC_IDX) == 0
    idx_t = pltpu.VMEM((1, SC_IDX), jnp.int32)

    @pl.kernel(out_type=jax.ShapeDtypeStruct((n_rows, d), xn.dtype), mesh=_sc_mesh(),
               scratch_types=[idx_t, idx_t, pltpu.VMEM((SC_ROWS, d), xn.dtype)])
    def scatter(x_hbm, p1_hbm, p2_hbm, o_hbm, i1_v, i2_v, buf):
        def window(base):
            pltpu.sync_copy(p1_hbm.at[:, pl.ds(base, SC_IDX)], i1_v)
            pltpu.sync_copy(p2_hbm.at[:, pl.ds(base, SC_IDX)], i2_v)
            for k in range(SC_IDX // SC_ROWS):
                pltpu.sync_copy(x_hbm.at[pl.ds(base + k * SC_ROWS, SC_ROWS)], buf)
                pltpu.sync_copy(buf, o_hbm.at[i1_v.at[0, pl.ds(k * SC_ROWS, SC_ROWS)]])
                pltpu.sync_copy(buf, o_hbm.at[i2_v.at[0, pl.ds(k * SC_ROWS, SC_ROWS)]])
        _sc_windows(t, window)

    return scatter(xn, pos1.reshape(1, t), pos2.reshape(1, t))


def _sc_gather_rows(ys, pos1, pos2):
    t = pos1.shape[0]
    d = ys.shape[1]
    assert t % (SC_WORKERS * SC_IDX) == 0
    idx_t = pltpu.VMEM((1, SC_IDX), jnp.int32)
    out = jax.ShapeDtypeStruct((t, d), ys.dtype)

    @pl.kernel(out_type=(out, out), mesh=_sc_mesh(),
               scratch_types=[idx_t, idx_t, pltpu.VMEM((SC_ROWS, d), ys.dtype)])
    def gather(y_hbm, p1_hbm, p2_hbm, o1_hbm, o2_hbm, i1_v, i2_v, buf):
        def window(base):
            pltpu.sync_copy(p1_hbm.at[:, pl.ds(base, SC_IDX)], i1_v)
            pltpu.sync_copy(p2_hbm.at[:, pl.ds(base, SC_IDX)], i2_v)
            for k in range(SC_IDX // SC_ROWS):
                rows = pl.ds(base + k * SC_ROWS, SC_ROWS)
                pltpu.sync_copy(y_hbm.at[i1_v.at[0, pl.ds(k * SC_ROWS, SC_ROWS)]], buf)
                pltpu.sync_copy(buf, o1_hbm.at[rows])
                pltpu.sync_copy(y_hbm.at[i2_v.at[0, pl.ds(k * SC_ROWS, SC_ROWS)]], buf)
                pltpu.sync_copy(buf, o2_hbm.at[rows])
        _sc_windows(t, window)

    return gather(ys, pos1.reshape(1, t), pos2.reshape(1, t))


def _ple_sparse_kernel(h_ref, info_ref, y1_ref, y2_ref, p_ref, wpp_ref, wpg_ref, gp_ref, gf_ref, y_ref):
    info = info_ref[...]
    g1 = info[:, INFO_G1:INFO_G1 + 1]
    g2 = info[:, INFO_G2:INFO_G2 + 1]
    y1_lo, y1_hi = _unpack_rows(y1_ref[...])
    y2_lo, y2_hi = _unpack_rows(y2_ref[...])
    moe = jnp.concatenate([g1 * y1_lo + g2 * y2_lo, g1 * y1_hi + g2 * y2_hi], axis=1)
    h = h_ref[...] + moe
    hn = _rmsnorm(h, gp_ref[...])
    h = h + _mm(p_ref[...], wpp_ref[...]) * _sigmoid(_mm(hn, wpg_ref[...]))
    y_ref[...] = _rmsnorm(h, gf_ref[...])


def _ple_sparse(h, info, y1, y2, p, wpp, wpg, gp, gf):
    t = h.shape[0]
    tm = ROW_TM
    row = lambda n: pl.BlockSpec((tm, n), lambda i: (i, 0))
    full = lambda a: pl.BlockSpec(a.shape, lambda i: (0,) * a.ndim)
    return pl.pallas_call(
        _ple_sparse_kernel,
        grid=(t // tm,),
        in_specs=[row(D_MODEL), row(LANES), row(HALF), row(HALF), row(PLE_DIM),
                  full(wpp), full(wpg), full(gp), full(gf)],
        out_specs=row(D_MODEL),
        out_shape=jax.ShapeDtypeStruct((t, D_MODEL), F32),
        compiler_params=_params("parallel"),
        name="ple_sparse",
    )(h, info, y1, y2, p, wpp, wpg, gp, gf)


def _tile_tables(cnt, max_tiles):
    tiles_e = (cnt + (MOE_TM - 1)) // MOE_TM
    ends = jnp.cumsum(tiles_e)
    n_tiles = ends[-1]
    tile = jnp.arange(max_tiles, dtype=jnp.int32)
    idx = jnp.minimum(tile, n_tiles - 1)
    tile_expert = jnp.sum((idx[:, None] >= ends[None, :]).astype(jnp.int32), axis=1)
    mine = tile_expert[:, None] == jnp.arange(N_EXPERTS, dtype=jnp.int32)[None, :]
    of_mine = lambda v: jnp.sum(jnp.where(mine, v[None, :], 0), axis=1)
    valid = jnp.clip(of_mine(cnt) - (idx - of_mine(ends - tiles_e)) * MOE_TM, 0, MOE_TM)
    tile_valid = jnp.where(tile < n_tiles, valid, 0).astype(jnp.int32)
    return (tile_expert, tile_valid, n_tiles.reshape(1), (ends - 1).astype(jnp.int32),
            tiles_e.astype(jnp.int32))


def _ple_final_kernel(h_ref, m_ref, p_ref, wpp_ref, wpg_ref, gp_ref, gf_ref, y_ref):
    h = h_ref[...] + m_ref[...]
    hn = _rmsnorm(h, gp_ref[...])
    h = h + _mm(p_ref[...], wpp_ref[...]) * _sigmoid(_mm(hn, wpg_ref[...]))
    y_ref[...] = _rmsnorm(h, gf_ref[...])


def _ple_final(h, m, p, wpp, wpg, gp, gf):
    t = h.shape[0]
    tm = min(t, 256)
    row = lambda n: pl.BlockSpec((tm, n), lambda i: (i, 0))
    full = lambda a: pl.BlockSpec(a.shape, lambda i: (0,) * a.ndim)
    return pl.pallas_call(
        _ple_final_kernel,
        grid=(t // tm,),
        in_specs=[row(D_MODEL), row(D_MODEL), row(PLE_DIM), full(wpp), full(wpg), full(gp), full(gf)],
        out_specs=row(D_MODEL),
        out_shape=jax.ShapeDtypeStruct((t, D_MODEL), F32),
        compiler_params=_params("parallel"),
        name="ple_final",
    )(h, m, p, wpp, wpg, gp, gf)


def kernel(x_prompt, x_sample, p_prompt, p_sample, cache_k, cache_v, state_conv, state_S, rel_bias, norm_mix, w_in, att_sink, conv_w, dn_A_log, dn_dt_bias, dn_norm, w_out, norm_ffn, w_router_group, w_router_expert, w_gate, w_up, w_down, w_ple_proj, w_ple_gate, norm_ple, norm_final):
    batch, seq, _ = x_prompt.shape
    nseq = x_sample.shape[0]
    assert x_sample.shape[1] == 1 and norm_mix.shape[0] == 1 and cache_k.shape[2] == WINDOW
    assert seq % GDN_TB == 0 and seq % ATT_BLOCK == 0

    wi = w_in[0]
    o_db = ATT_COLS + CONV_CH
    w_in_re = jnp.concatenate(
        [wi[:, :o_db], wi[:, o_db + 2 * DN_HEADS:], wi[:, o_db:o_db + 2 * DN_HEADS],
         jnp.zeros((D_MODEL, LANES - 2 * DN_HEADS), F32)], axis=1).astype(BF16)
    row = lambda a: a.reshape(1, -1).astype(F32)
    pad_lanes = lambda a, off: jnp.zeros((1, LANES), F32).at[0, off:off + a.shape[0]].set(a)
    alog = pad_lanes(dn_A_log[0], DN_HEADS)
    dtb = pad_lanes(dn_dt_bias[0], DN_HEADS)
    dnx = jnp.tile(dn_norm[0], DN_HEADS).reshape(1, DN_WIDTH)
    w_router = jnp.concatenate(
        [w_router_group[0], w_router_expert[0],
         jnp.zeros((D_MODEL, LANES - N_GROUPS - N_EXPERTS), F32)], axis=1).astype(BF16)
    wo = w_out[0].astype(BF16)
    wg, wu, wd = w_gate[0], w_up[0], w_down[0]
    wpp, wpg = w_ple_proj[0].astype(BF16), w_ple_gate[0].astype(BF16)
    sink = att_sink[0]

    qi = np.arange(ATT_BLOCK)[:, None]
    kj = np.arange(2 * ATT_BLOCK)[None, :]
    bucket_p = jnp.asarray(_t5_bucket_np(qi + ATT_BLOCK - kj))
    bucket_s = jnp.asarray(_t5_bucket_np(WINDOW - np.arange(WINDOW)[None, :]))

    def tail(x, o_att, o_dn, p):
        h1, xn2, gates = _outproj_router(x, o_att, o_dn, wo, row(norm_ffn[0]), w_router)
        moe = _moe(xn2, gates, wg, wu, wd)
        return _ple_final(h1, moe, p, wpp, wpg, row(norm_ple[0]), row(norm_final))

    xp = x_prompt.reshape(batch * seq, D_MODEL)
    att_p, qkv_p, dz_p, ba_p, xc_tails = _inproj_conv(xp, row(norm_mix[0]), w_in_re, conv_w[0], seq)
    o_att_p = _attn_prompt(att_p, bucket_p, rel_bias, sink, batch, seq)
    o_dn_p, s_p = _gdn_prompt(qkv_p, dz_p, ba_p, alog, dtb, dnx, batch, seq)
    h1, xn2, info, cnt = _route_sparse(xp, o_att_p, o_dn_p, wo, row(norm_ffn[0]), w_router)
    pos = _positions(info, cnt)
    pos1, pos2 = pos[:, 0], pos[:, 1]
    max_tiles = _moe_tiles(batch * seq)
    cnt_e = cnt[0, ROUTER_OFF:ROUTER_OFF + N_EXPERTS].astype(jnp.int32)
    tile_expert, tile_valid, n_tiles, last_tile, used = _tile_tables(cnt_e, max_tiles)
    xs = _sc_scatter_rows(xn2, pos1, pos2, max_tiles * MOE_TM)
    ys = _experts(xs, tile_expert, tile_valid, n_tiles, wg, wu, wd)
    y1, y2 = _sc_gather_rows(ys, pos1, pos2)
    y_p = _ple_sparse(h1, info, y1, y2, p_prompt[0].reshape(batch * seq, PLE_DIM),
                      wpp, wpg, row(norm_ple[0]), row(norm_final))

    xs = x_sample.reshape(nseq, D_MODEL)
    att_s, xc_s, dz_s, ba_s = _inproj(xs, row(norm_mix[0]), w_in_re)
    ck_t = jnp.transpose(cache_k[0], (0, 2, 3, 1))
    cv_t = jnp.transpose(cache_v[0], (0, 2, 3, 1))
    o_att_s = _attn_sample(att_s, ck_t, cv_t, bucket_s, rel_bias, sink)
    sconv_t = jnp.swapaxes(state_conv[0], 0, 1)
    o_dn_s_t, s_s_t = _gdn_sample_lanes(xc_s, dz_s, ba_s, sconv_t, jnp.transpose(state_S[0], (1, 2, 3, 0)),
                                        conv_w[0], alog, dtb, dn_norm[0])
    s_s = jnp.transpose(s_s_t, (3, 0, 1, 2))
    y_s = tail(xs, o_att_s, o_dn_s_t, p_sample[0].reshape(nseq, PLE_DIM))

    att_p3 = att_p.reshape(batch, seq, ATT_COLS)
    kv_shape = (1, batch, WINDOW, ATT_KV_HEADS, HEAD_DIM)
    k_p = att_p3[:, seq - WINDOW:, ATT_WIDTH:ATT_WIDTH + KV_WIDTH].reshape(kv_shape)
    v_p = att_p3[:, seq - WINDOW:, ATT_WIDTH + KV_WIDTH:].reshape(kv_shape)
    conv_p = xc_tails.reshape(batch, -1, TAIL, CONV_CH)[:, -1, TAIL - (CONV_WIDTH - 1):][None]
    new_row = lambda c0: att_s[:, c0:c0 + KV_WIDTH].reshape(nseq, ATT_KV_HEADS, HEAD_DIM, 1)
    shift_in = lambda c_t, c0: jnp.transpose(jnp.concatenate([c_t[..., 1:], new_row(c0)], axis=-1),
                                             (0, 3, 1, 2))[None]
    k_s = shift_in(ck_t, ATT_WIDTH)
    v_s = shift_in(cv_t, ATT_WIDTH + KV_WIDTH)
    conv_s = jnp.concatenate([state_conv[0][:, 1:], xc_s[:, None, :]], axis=1)[None]
    return (y_p.reshape(batch, seq, D_MODEL), y_s.reshape(nseq, 1, D_MODEL),
            k_p, v_p, conv_p, s_p[None], k_s, v_s, conv_s, s_s[None])
```

```python
import functools
import math

import numpy as np
import jax
import jax.numpy as jnp
from jax import lax
from jax.experimental import pallas as pl
from jax.experimental.pallas import tpu as pltpu
from jax.experimental.pallas import tpu_sc as plsc

F32 = jnp.float32
BF16 = jnp.bfloat16

D_MODEL = 1024
ATT_HEADS = 8
ATT_KV_HEADS = 2
HEAD_DIM = 64
GQA = ATT_HEADS // ATT_KV_HEADS
WINDOW = 128
ATT_BLOCK = 128
N_BUCKETS = 32
DN_HEADS = 8
DN_DK = 64
DN_DV = 64
CONV_WIDTH = 4
DN_CHUNK = 64
ATT_WIDTH = ATT_HEADS * HEAD_DIM
KV_WIDTH = ATT_KV_HEADS * HEAD_DIM
DN_WIDTH = DN_HEADS * DN_DV
CONV_CH = 3 * DN_WIDTH
N_GROUPS = 4
EXPERTS_PER_GROUP = 8
N_EXPERTS = N_GROUPS * EXPERTS_PER_GROUP
D_EXPERT = 256
PLE_DIM = 256
EPS = 1e-6
NEG_INF = float("-inf")

ATT_COLS = ATT_WIDTH + 2 * KV_WIDTH
LANES = 128
IN_COLS = ATT_COLS + CONV_CH + DN_WIDTH + LANES
ROUTER_OFF = N_GROUPS
VMEM_LIMIT = 48 * 1024 * 1024
ROW_TM = 512


def _params(*sem):
    return pltpu.CompilerParams(dimension_semantics=sem, vmem_limit_bytes=VMEM_LIMIT)


def _mm(a, b):
    return jnp.dot(a.astype(BF16), b.astype(BF16), preferred_element_type=F32)


def _mm_nt(a, b):
    return lax.dot_general(a.astype(BF16), b.astype(BF16), (((1,), (1,)), ((), ())),
                           preferred_element_type=F32)


def _mm_tn(a, b):
    return lax.dot_general(a.astype(BF16), b.astype(BF16), (((0,), (0,)), ((), ())),
                           preferred_element_type=F32)


def _split3(x):
    h1 = x.astype(BF16)
    r1 = x - h1.astype(F32)
    h2 = r1.astype(BF16)
    h3 = (r1 - h2.astype(F32)).astype(BF16)
    return h1, h2, h3


def _mm_sel_rhs(x, sel):
    h1, h2, h3 = _split3(x)
    d = lambda h: jnp.dot(h, sel, preferred_element_type=F32)
    return d(h1) + d(h2) + d(h3)


def _mm_sel_lhs(sel, x):
    h1, h2, h3 = _split3(x)
    d = lambda h: jnp.dot(sel, h, preferred_element_type=F32)
    return d(h1) + d(h2) + d(h3)


def _mm3(a, b):
    ah = a.astype(BF16)
    al = (a - ah.astype(F32)).astype(BF16)
    bh = b.astype(BF16)
    bl = (b - bh.astype(F32)).astype(BF16)
    d = lambda u, v: jnp.dot(u, v, preferred_element_type=F32)
    return d(ah, bh) + d(ah, bl) + d(al, bh)


def _sigmoid(x):
    return 1.0 / (1.0 + jnp.exp(-x))


def _silu(x):
    return x * _sigmoid(x)


def _softplus(x):
    return jnp.maximum(x, 0.0) + jnp.log1p(jnp.exp(-jnp.abs(x)))


def _rmsnorm(x, g):
    return x * lax.rsqrt(jnp.mean(x * x, axis=-1, keepdims=True) + EPS) * g


def _t5_bucket_np(dist):
    max_exact = N_BUCKETS // 2
    d = np.maximum(dist, 0)
    ratio = (np.log(np.maximum(d, 1).astype(np.float32) / np.float32(max_exact))
             / np.float32(math.log(WINDOW / max_exact))).astype(np.float32)
    large = np.minimum(max_exact + (ratio * np.float32(N_BUCKETS - max_exact)).astype(np.int32),
                       N_BUCKETS - 1)
    return np.where(d < max_exact, d, large).astype(np.int32)


def _bias_lookup(bucket, rb_ref, h):
    acc = jnp.zeros(bucket.shape, F32)
    for t in range(N_BUCKETS):
        acc = jnp.where(bucket == t, rb_ref[t, h], acc)
    return acc


def _inproj_kernel(x_ref, g_ref, w_ref, att_ref, xc_ref, dz_ref, ba_ref):
    xn = _rmsnorm(x_ref[...], g_ref[...]).astype(BF16)
    o0, o1, o2 = ATT_COLS, ATT_COLS + CONV_CH, ATT_COLS + CONV_CH + DN_WIDTH
    att_ref[...] = jnp.dot(xn, w_ref[:, :o0], preferred_element_type=F32)
    xc_ref[...] = jnp.dot(xn, w_ref[:, o0:o1], preferred_element_type=F32)
    dz_ref[...] = jnp.dot(xn, w_ref[:, o1:o2], preferred_element_type=F32)
    ba_ref[...] = jnp.dot(xn, w_ref[:, o2:], preferred_element_type=F32)


def _inproj(x, g, w):
    t = x.shape[0]
    tm = min(t, ROW_TM)
    row = lambda n: pl.BlockSpec((tm, n), lambda i: (i, 0))
    full = lambda a: pl.BlockSpec(a.shape, lambda i: (0,) * a.ndim)
    return pl.pallas_call(
        _inproj_kernel,
        grid=(t // tm,),
        in_specs=[row(D_MODEL), full(g), full(w)],
        out_specs=[row(ATT_COLS), row(CONV_CH), row(DN_WIDTH), row(LANES)],
        out_shape=[jax.ShapeDtypeStruct((t, n), F32) for n in (ATT_COLS, CONV_CH, DN_WIDTH, LANES)],
        compiler_params=_params("parallel"),
        name="inproj",
    )(x, g, w)


TAIL = 8
PAIR = 2 * DN_DK
N_PAIRS = DN_WIDTH // PAIR


def _head_sums(z, pair_ones):
    hi = z.astype(BF16)
    lw = (z - hi.astype(F32)).astype(BF16)
    d = lambda a, p: jnp.dot(a[:, p * PAIR:(p + 1) * PAIR], pair_ones, preferred_element_type=F32)
    return jnp.concatenate([d(hi, p) + d(lw, p) for p in range(N_PAIRS)], axis=1)


def _inproj_conv_kernel(x_ref, g_ref, w_ref, cw_ref, ones_ref, att_ref, qkv_ref, dz_ref, ba_ref, tail_ref,
                        xp_scr, *, tiles_per_seq):
    tm = x_ref.shape[0]

    @pl.when(pl.program_id(0) % tiles_per_seq == 0)
    def _():
        xp_scr[0:TAIL, :] = jnp.zeros((TAIL, CONV_CH), F32)

    xn = _rmsnorm(x_ref[...], g_ref[...]).astype(BF16)
    o0, o1, o2 = ATT_COLS, ATT_COLS + CONV_CH, ATT_COLS + CONV_CH + DN_WIDTH
    xc = jnp.dot(xn, w_ref[:, o0:o1], preferred_element_type=F32)
    att_ref[...] = jnp.dot(xn, w_ref[:, :o0], preferred_element_type=F32)
    dz_ref[...] = jnp.dot(xn, w_ref[:, o1:o2], preferred_element_type=F32)
    ba_ref[...] = jnp.dot(xn, w_ref[:, o2:], preferred_element_type=F32)

    xp_scr[TAIL:, :] = xc
    y = xp_scr[TAIL - 3:TAIL - 3 + tm, :] * cw_ref[0:1, :]
    y = y + xp_scr[TAIL - 2:TAIL - 2 + tm, :] * cw_ref[1:2, :]
    y = y + xp_scr[TAIL - 1:TAIL - 1 + tm, :] * cw_ref[2:3, :]
    y = y + xc * cw_ref[3:4, :]
    tail = xc[tm - TAIL:, :]
    xp_scr[0:TAIL, :] = tail
    tail_ref[0] = tail
    y = _silu(y)
    q = y[:, :DN_WIDTH]
    k = y[:, DN_WIDTH:2 * DN_WIDTH]
    inv_norm = lax.rsqrt(_head_sums(jnp.concatenate([q * q, k * k], axis=0), ones_ref[...]) + EPS)
    qkv_ref[:, :DN_WIDTH] = q * inv_norm[:tm] * (DN_DK ** -0.5)
    qkv_ref[:, DN_WIDTH:2 * DN_WIDTH] = k * inv_norm[tm:]
    qkv_ref[:, 2 * DN_WIDTH:] = y[:, 2 * DN_WIDTH:]


def _pair_ones():
    lane = np.arange(PAIR)
    return jnp.asarray((lane[:, None] // DN_DV == lane[None, :] // DN_DV).astype(np.float32), dtype=BF16)


def _inproj_conv(x, g, w, conv_w, seq):
    t = x.shape[0]
    tm = ROW_TM
    assert seq % tm == 0
    ones = _pair_ones()
    row = lambda n: pl.BlockSpec((tm, n), lambda i: (i, 0))
    full = lambda a: pl.BlockSpec(a.shape, lambda i: (0,) * a.ndim)
    return pl.pallas_call(
        functools.partial(_inproj_conv_kernel, tiles_per_seq=seq // tm),
        grid=(t // tm,),
        in_specs=[row(D_MODEL), full(g), full(w), full(conv_w), full(ones)],
        out_specs=[row(ATT_COLS), row(CONV_CH), row(DN_WIDTH), row(LANES),
                   pl.BlockSpec((1, TAIL, CONV_CH), lambda i: (i, 0, 0))],
        out_shape=[jax.ShapeDtypeStruct((t, n), F32) for n in (ATT_COLS, CONV_CH, DN_WIDTH, LANES)]
                  + [jax.ShapeDtypeStruct((t // tm, TAIL, CONV_CH), F32)],
        scratch_shapes=[pltpu.VMEM((TAIL + tm, CONV_CH), F32)],
        compiler_params=_params("arbitrary"),
        name="inproj_conv",
    )(x, g, w, conv_w, ones)


GROUP_ROWS = GQA * ATT_BLOCK


def _attn_prompt_kernel(cur_ref, prev_ref, bucket_ref, rb_ref, sink_ref, o_ref, bias_scr, sink_scr):
    i = pl.program_id(0)
    nseq = cur_ref.shape[0]

    @pl.when(i == 0)
    def _():
        qi = lax.broadcasted_iota(jnp.int32, (ATT_BLOCK, 2 * ATT_BLOCK), 0)
        kj = lax.broadcasted_iota(jnp.int32, (ATT_BLOCK, 2 * ATT_BLOCK), 1)
        dist = qi + ATT_BLOCK - kj
        band = jnp.logical_and(dist >= 0, dist < WINDOW)
        bucket = bucket_ref[...]
        hrow = lax.broadcasted_iota(jnp.int32, (GROUP_ROWS, 1), 0) // ATT_BLOCK
        for g in range(ATT_KV_HEADS):
            sink_col = jnp.zeros((GROUP_ROWS, 1), F32)
            for hh in range(GQA):
                h = g * GQA + hh
                bias = jnp.where(band, _bias_lookup(bucket, rb_ref, h), NEG_INF)
                bias_scr[0, g, hh * ATT_BLOCK:(hh + 1) * ATT_BLOCK, :] = bias
                bias_scr[1, g, hh * ATT_BLOCK:(hh + 1) * ATT_BLOCK, :] = jnp.where(kj >= ATT_BLOCK, bias, NEG_INF)
                sink_col = jnp.where(hrow == hh, sink_ref[h], sink_col)
            sink_scr[g] = sink_col

    first = (i == 0).astype(jnp.int32)
    probs = [(b, g) for b in range(nseq) for g in range(ATT_KV_HEADS)]
    scores = []
    for b, g in probs:
        cur = cur_ref[b]
        prev = prev_ref[b]
        q = jnp.concatenate([cur[:, (g * GQA + hh) * HEAD_DIM:(g * GQA + hh + 1) * HEAD_DIM]
                             for hh in range(GQA)], axis=0) * (HEAD_DIM ** -0.5)
        kcol = slice(ATT_WIDTH + g * HEAD_DIM, ATT_WIDTH + (g + 1) * HEAD_DIM)
        k2 = jnp.concatenate([prev[:, kcol], cur[:, kcol]], axis=0)
        scores.append(_mm_nt(q, k2) + bias_scr[first, g])
    probs_p, dens = [], []
    for (b, g), s in zip(probs, scores):
        sink = sink_scr[g]
        m = jnp.maximum(jnp.max(s, axis=-1, keepdims=True), sink)
        p = jnp.exp(s - m)
        dens.append(jnp.sum(p, axis=-1, keepdims=True) + jnp.exp(sink - m))
        probs_p.append(p)
    outs = {}
    for (b, g), p, den in zip(probs, probs_p, dens):
        vcol = slice(ATT_WIDTH + KV_WIDTH + g * HEAD_DIM, ATT_WIDTH + KV_WIDTH + (g + 1) * HEAD_DIM)
        v2 = jnp.concatenate([prev_ref[b][:, vcol], cur_ref[b][:, vcol]], axis=0)
        outs[b, g] = _mm(p, v2) / den
    for b in range(nseq):
        o_ref[b] = jnp.concatenate([outs[b, g][hh * ATT_BLOCK:(hh + 1) * ATT_BLOCK, :]
                                    for g in range(ATT_KV_HEADS) for hh in range(GQA)], axis=1)


def _attn_prompt(att, bucket, rel_bias, sink, batch, seq):
    nb = seq // ATT_BLOCK
    smem = pl.BlockSpec(memory_space=pltpu.SMEM)
    att3 = att.reshape(batch, seq, ATT_COLS)
    out = pl.pallas_call(
        _attn_prompt_kernel,
        grid=(nb,),
        in_specs=[
            pl.BlockSpec((batch, ATT_BLOCK, ATT_COLS), lambda i: (0, i, 0)),
            pl.BlockSpec((batch, ATT_BLOCK, ATT_COLS), lambda i: (0, jnp.maximum(i - 1, 0), 0)),
            pl.BlockSpec(bucket.shape, lambda i: (0, 0)),
            smem, smem,
        ],
        out_specs=pl.BlockSpec((batch, ATT_BLOCK, ATT_WIDTH), lambda i: (0, i, 0)),
        out_shape=jax.ShapeDtypeStruct((batch, seq, ATT_WIDTH), F32),
        scratch_shapes=[pltpu.VMEM((2, ATT_KV_HEADS, GROUP_ROWS, 2 * ATT_BLOCK), F32),
                        pltpu.VMEM((ATT_KV_HEADS, GROUP_ROWS, 1), F32)],
        compiler_params=_params("arbitrary"),
        name="attn_prompt",
    )(att3, att3, bucket, rel_bias, sink)
    return out.reshape(batch * seq, ATT_WIDTH)


ATT_S_BB = 8


def _attn_sample_kernel(att_ref, ck_ref, cv_ref, bucket_ref, rb_ref, sink_ref, o_ref, ks_ref, vs_ref,
                        bias_scr, col_scr):
    hrow = lax.broadcasted_iota(jnp.int32, (ATT_HEADS, LANES), 0)
    lane = lax.broadcasted_iota(jnp.int32, (ATT_HEADS, LANES), 1)

    last = (lax.broadcasted_iota(jnp.int32, (3, WINDOW), 1) == WINDOW - 1).astype(BF16)
    is_last = lax.broadcasted_iota(jnp.int32, (KV_WIDTH, WINDOW), 1) == WINDOW - 1

    def shifted(cache_t, new_row):
        pieces = jnp.concatenate([p.astype(F32) for p in _split3(new_row)], axis=0).astype(BF16)
        col = lax.dot_general(pieces, last, (((0,), (0,)), ((), ())), preferred_element_type=F32)
        out = jnp.where(is_last, col, pltpu.roll(cache_t, WINDOW - 1, axis=1))
        return out.reshape(ATT_KV_HEADS, HEAD_DIM, WINDOW)

    for b in range(ATT_S_BB):
        row = att_ref[b:b + 1, :]
        ks_ref[b] = shifted(ck_ref[b].reshape(KV_WIDTH, WINDOW), row[:, ATT_WIDTH:ATT_WIDTH + KV_WIDTH])
        vs_ref[b] = shifted(cv_ref[b].reshape(KV_WIDTH, WINDOW), row[:, ATT_WIDTH + KV_WIDTH:])

    @pl.when(pl.program_id(0) == 0)
    def _():
        bucket = jnp.broadcast_to(bucket_ref[...], (ATT_HEADS, LANES))
        bias = jnp.zeros((ATT_HEADS, LANES), F32)
        cols = jnp.zeros((ATT_HEADS, LANES), F32)
        for h in range(ATT_HEADS):
            bias = jnp.where(hrow == h, _bias_lookup(bucket, rb_ref, h), bias)
            cols = jnp.where(jnp.logical_and(hrow == h, lane == 0), sink_ref[h], cols)
            cols = jnp.where(jnp.logical_and(hrow == h, lane == 1), rb_ref[0, h], cols)
        bias_scr[...] = jnp.where(lane >= 1, bias, NEG_INF)
        col_scr[...] = cols

    bias_c = bias_scr[...]
    sink = col_scr[:, 0:1]
    bias_n = col_scr[:, 1:2]
    same_group = (hrow // GQA) == (lane // HEAD_DIM)
    low_group = lax.broadcasted_iota(jnp.int32, (ATT_HEADS, HEAD_DIM), 0) < GQA
    rnd = lambda a: a.astype(BF16).astype(F32)
    seqs = range(ATT_S_BB)
    rows = [att_ref[b:b + 1, :] for b in seqs]
    q_bds = []
    for row in rows:
        q = row[:, :ATT_WIDTH] * (HEAD_DIM ** -0.5)
        qh = jnp.concatenate([q[:, h * HEAD_DIM:(h + 1) * HEAD_DIM] for h in range(ATT_HEADS)], axis=0)
        q_bds.append(jnp.where(same_group, jnp.concatenate([qh, qh], axis=1), 0.0))
    kv_t = lambda ref, b: ref[b].reshape(KV_WIDTH, WINDOW)
    s_cs = [_mm(q_bd, kv_t(ck_ref, b)) + bias_c for b, q_bd in zip(seqs, q_bds)]
    prs, pns = [], []
    for row, q_bd, s_c in zip(rows, q_bds, s_cs):
        kn = row[:, ATT_WIDTH:ATT_WIDTH + KV_WIDTH]
        s_n = jnp.sum(rnd(q_bd) * rnd(kn), axis=-1, keepdims=True) + bias_n
        m = jnp.maximum(jnp.maximum(jnp.max(s_c, axis=-1, keepdims=True), s_n), sink)
        p_c = jnp.exp(s_c - m)
        p_n = jnp.exp(s_n - m)
        den = jnp.sum(p_c, axis=-1, keepdims=True) + p_n + jnp.exp(sink - m)
        prs.append(p_c / den)
        pns.append(p_n / den)
    pvs = [_mm_nt(pr, kv_t(cv_ref, b)) for b, pr in zip(seqs, prs)]
    for b, row, pv, pn in zip(seqs, rows, pvs, pns):
        vn = row[:, ATT_WIDTH + KV_WIDTH:]
        o_full = pv + rnd(pn) * rnd(vn)
        o_sel = jnp.where(low_group, o_full[:, :HEAD_DIM], o_full[:, HEAD_DIM:])
        o_ref[b:b + 1, :] = jnp.concatenate([o_sel[h:h + 1, :] for h in range(ATT_HEADS)], axis=1)


def _attn_sample(att, ck, cv, bucket, rel_bias, sink):
    nseq = att.shape[0]
    smem = pl.BlockSpec(memory_space=pltpu.SMEM)
    cache = pl.BlockSpec((ATT_S_BB, ATT_KV_HEADS, HEAD_DIM, WINDOW), lambda i: (i, 0, 0, 0))
    return pl.pallas_call(
        _attn_sample_kernel,
        grid=(nseq // ATT_S_BB,),
        in_specs=[pl.BlockSpec((ATT_S_BB, ATT_COLS), lambda i: (i, 0)), cache, cache,
                  pl.BlockSpec(bucket.shape, lambda i: (0, 0)), smem, smem],
        out_specs=[pl.BlockSpec((ATT_S_BB, ATT_WIDTH), lambda i: (i, 0)), cache, cache],
        out_shape=[jax.ShapeDtypeStruct((nseq, ATT_WIDTH), F32),
                   jax.ShapeDtypeStruct(ck.shape, F32), jax.ShapeDtypeStruct(cv.shape, F32)],
        scratch_shapes=[pltpu.VMEM((ATT_HEADS, LANES), F32), pltpu.VMEM((ATT_HEADS, LANES), F32)],
        compiler_params=_params("arbitrary"),
        name="attn_sample",
    )(att, ck, cv, bucket, rel_bias, sink)


GDN_TB = 128
GDN_NC = GDN_TB // DN_CHUNK


def _gdn_gates(ba, alog, dtb):
    beta = _sigmoid(ba)
    g = -jnp.exp(alog) * _softplus(ba + dtb)
    return beta, g


def _pair_diag(x, lo):
    xb = x.astype(BF16)
    zero = jnp.zeros_like(xb)
    return jnp.concatenate([jnp.where(lo, xb, zero), jnp.where(lo, zero, xb)], axis=0)


def _gdn_prompt_kernel(qkv_ref, dz_ref, ba_ref, alog_ref, dtb_ref, dnx_ref,
                       hsum_ref, expb_ref, expg_ref, ltri_ref,
                       o_ref, s_out_ref, s_scr):
    i = pl.program_id(0)
    nb = qkv_ref.shape[0]

    @pl.when(i == 0)
    def _():
        s_scr[...] = jnp.zeros(s_scr.shape, F32)

    hsum = hsum_ref[...]
    ri = lax.broadcasted_iota(jnp.int32, (DN_CHUNK, PAIR), 0)
    ci = lax.broadcasted_iota(jnp.int32, (DN_CHUNK, PAIR), 1)
    lo = ci < DN_DK
    cj = jnp.where(lo, ci, ci - DN_DK)
    causal = ri >= cj
    strict = ri > cj
    eye = (ri == cj).astype(F32)

    def sel2(x, m):
        hi = x.astype(BF16)
        lw = (x - hi.astype(F32)).astype(BF16)
        return (jnp.dot(hi, m, preferred_element_type=F32) + jnp.dot(lw, m, preferred_element_type=F32))

    pre = []
    for b in range(nb):
        q = qkv_ref[b, :, :DN_WIDTH]
        k = qkv_ref[b, :, DN_WIDTH:2 * DN_WIDTH]
        v = qkv_ref[b, :, 2 * DN_WIDTH:]
        beta_c, g_c = _gdn_gates(ba_ref[b], alog_ref[...], dtb_ref[...])
        beta = sel2(beta_c, expb_ref[...])
        gam_c = _mm_sel_lhs(ltri_ref[...], g_c)
        gam = _mm_sel_rhs(gam_c, expg_ref[...])
        gam_t = gam_c.T
        kb = k * beta
        egam = jnp.exp(gam)
        pre.append(dict(q=q, k=k, kb=kb, vb=v * beta, qg=q * egam, wr=kb * egam, gam=gam, gam_t=gam_t))

    probs = [(b, p) for b in range(nb) for p in range(N_PAIRS)]
    pick = lambda m: jnp.where(lo, m[:DN_DK], m[DN_DK:])
    o_rows = [[] for _ in range(nb)]
    for c in range(GDN_NC):
        r0, r1 = c * DN_CHUNK, (c + 1) * DN_CHUNK
        sl = lambda name, b, p: pre[b][name][r0:r1, p * PAIR:(p + 1) * PAIR]
        raws = []
        for b, p in probs:
            k_p = sl("k", b, p)
            k_rows = jnp.concatenate([jnp.where(lo, k_p, 0.0), jnp.where(lo, 0.0, k_p)], axis=0)
            raws.append(_mm_nt(jnp.concatenate([sl("kb", b, p), sl("q", b, p)], axis=0), k_rows))
        pws, ts, qks = [], [], []
        for (b, p), raw in zip(probs, raws):
            gcol = sl("gam", b, p)
            h0 = DN_HEADS + 2 * p
            gam_t = pre[b]["gam_t"]
            grow = jnp.concatenate([gam_t[h0:h0 + 1, r0:r1], gam_t[h0 + 1:h0 + 2, r0:r1]], axis=1)
            decay = jnp.exp(jnp.where(causal, gcol - grow, NEG_INF))
            a = jnp.where(strict, raw[:DN_CHUNK] * decay, 0.0)
            qks.append(jnp.where(causal, raw[DN_CHUNK:] * decay, 0.0))
            pws.append(-a)
            ts.append(eye - a)
        pws = [_mm(pw, _pair_diag(pw, lo)) for pw in pws]
        for _ in range(4):
            rs = [_mm(jnp.concatenate([pw, t], axis=0), _pair_diag(pw, lo)) for pw, t in zip(pws, ts)]
            pws = [r[:DN_CHUNK] for r in rs]
            ts = [t + r[DN_CHUNK:] for t, r in zip(ts, rs)]
        rs = [_mm(t, _pair_diag(pw, lo)) for pw, t in zip(pws, ts)]
        ts = [t + r for t, r in zip(ts, rs)]
        sols = [_mm(t, jnp.concatenate([_pair_diag(sl("vb", b, p), lo), _pair_diag(sl("wr", b, p), lo)],
                                       axis=1)) for (b, p), t in zip(probs, ts)]
        qkuws = [_mm(qk, jnp.concatenate([_pair_diag(s[:, :PAIR], lo), _pair_diag(s[:, PAIR:], lo)], axis=1))
                 for qk, s in zip(qks, sols)]
        crosses, gls = [], []
        for (b, p), s in zip(probs, sols):
            gam_last = pre[b]["gam"][r1 - 1:r1, p * PAIR:(p + 1) * PAIR]
            kd = sl("k", b, p) * jnp.exp(gam_last - sl("gam", b, p))
            crosses.append(_mm_tn(kd, s))
            gls.append(jnp.exp(gam_last))
        lhs = [jnp.concatenate([pick(cr[:, PAIR:]), sl("qg", b, p) - qkuw[:, PAIR:]], axis=0)
               for (b, p), cr, qkuw in zip(probs, crosses, qkuws)]
        s_olds = [s_scr[b, p] for b, p in probs]
        rs = [_mm(l, _pair_diag(s_old, lo)) for l, s_old in zip(lhs, s_olds)]
        o_pairs = [[] for _ in range(nb)]
        for (b, p), r, s_old, gl, cr, qkuw in zip(probs, rs, s_olds, gls, crosses, qkuws):
            s_scr[b, p] = gl * s_old - r[:DN_DK] + pick(cr[:, :PAIR])
            o_pairs[b].append(r[DN_DK:] + qkuw[:, :PAIR])
        for b in range(nb):
            o_rows[b].append(jnp.concatenate(o_pairs[b], axis=1))

    o_all = jnp.concatenate([jnp.concatenate(rows, axis=0) for rows in o_rows], axis=0)
    inv_rms = lax.rsqrt(_head_sums(o_all * o_all, hsum) * (1.0 / DN_DV) + EPS)
    for b in range(nb):
        rows = slice(b * GDN_TB, (b + 1) * GDN_TB)
        o_ref[b] = o_all[rows] * inv_rms[rows] * dnx_ref[...] * _silu(dz_ref[b])

    @pl.when(i == pl.num_programs(0) - 1)
    def _():
        for b in range(nb):
            for p in range(N_PAIRS):
                s_p = s_scr[b, p]
                s_out_ref[b, 2 * p] = s_p[:, :DN_DV]
                s_out_ref[b, 2 * p + 1] = s_p[:, DN_DV:]


def _gdn_consts():
    lane = np.arange(DN_WIDTH)
    pl_lane = np.arange(PAIR)
    hsum = (pl_lane[:, None] // DN_DV == pl_lane[None, :] // DN_DV)
    src = np.arange(LANES)
    expb = (src[:, None] == lane[None, :] // DN_DV)
    expg = (src[:, None] == DN_HEADS + lane[None, :] // DN_DV)
    tok = np.arange(GDN_TB)
    ltri = np.logical_and(tok[:, None] >= tok[None, :],
                          tok[:, None] // DN_CHUNK == tok[None, :] // DN_CHUNK)
    as_bf16 = lambda m: jnp.asarray(m.astype(np.float32), dtype=BF16)
    return as_bf16(hsum), as_bf16(expb), as_bf16(expg), as_bf16(ltri)


def _gdn_prompt(xc, dz, ba, alog, dtb, dnx, batch, seq):
    nt = seq // GDN_TB
    hsum, expb, expg, ltri = _gdn_consts()
    row = lambda n: pl.BlockSpec((batch, GDN_TB, n), lambda i: (0, i, 0))
    full = lambda a: pl.BlockSpec(a.shape, lambda i: (0,) * a.ndim)
    consts = (alog, dtb, dnx, hsum, expb, expg, ltri)
    as3d = lambda a: a.reshape(batch, seq, a.shape[-1])
    o, s = pl.pallas_call(
        _gdn_prompt_kernel,
        grid=(nt,),
        in_specs=[row(CONV_CH), row(DN_WIDTH), row(LANES)] + [full(a) for a in consts],
        out_specs=[row(DN_WIDTH),
                   pl.BlockSpec((batch, DN_HEADS, DN_DK, DN_DV), lambda i: (0, 0, 0, 0))],
        out_shape=[jax.ShapeDtypeStruct((batch, seq, DN_WIDTH), F32),
                   jax.ShapeDtypeStruct((batch, DN_HEADS, DN_DK, DN_DV), F32)],
        scratch_shapes=[pltpu.VMEM((batch, N_PAIRS, DN_DK, PAIR), F32)],
        compiler_params=_params("arbitrary"),
        name="gdn_prompt",
    )(as3d(xc), as3d(dz), as3d(ba), *consts)
    return o.reshape(batch * seq, DN_WIDTH), s


GDN_S_BB = 8


def _gdn_sample_kernel(xc_ref, dz_ref, ba_ref, sc_ref, s_ref, cw_ref, alog_ref, dtb_ref, dn_ref,
                       hsum_ref, eye_ref, hsel_ref, hrep3_ref, o_ref, s_out_ref):
    xc = xc_ref[...]
    y = sc_ref[0] * cw_ref[0:1, :]
    y = y + sc_ref[1] * cw_ref[1:2, :]
    y = y + sc_ref[2] * cw_ref[2:3, :]
    y = _silu(y + xc * cw_ref[3:4, :])
    hsum = hsum_ref[...]
    q = y[:, :DN_WIDTH]
    k = y[:, DN_WIDTH:2 * DN_WIDTH]
    v = y[:, 2 * DN_WIDTH:]
    q = q * lax.rsqrt(_mm_sel_rhs(q * q, hsum) + EPS) * (DN_DK ** -0.5)
    k = k * lax.rsqrt(_mm_sel_rhs(k * k, hsum) + EPS)
    beta_c, g_c = _gdn_gates(ba_ref[...], alog_ref[...], dtb_ref[...])
    eg_c = jnp.exp(g_c)
    eye = eye_ref[...]
    tr = lambda a: lax.dot_general(a, eye, (((0,), (0,)), ((), ())), precision=lax.Precision.HIGHEST,
                                   preferred_element_type=F32)
    gates_t = tr(jnp.concatenate([beta_c, eg_c], axis=1))
    beta_t = gates_t[:LANES]
    eg_t = gates_t[LANES:]
    dz = dz_ref[...]
    dn = dn_ref[...]
    split = lambda r: jnp.concatenate([r[:, h * DN_DV:(h + 1) * DN_DV] for h in range(DN_HEADS)], axis=0)
    own_head = hsel_ref[...].astype(F32)
    hrep3 = hrep3_ref[...]
    seqs = range(GDN_S_BB)
    dot = lambda a, b: jnp.dot(a.astype(BF16), b.astype(BF16), preferred_element_type=F32)

    def pieces(x):
        p1 = x.astype(BF16).astype(F32)
        r1 = x - p1
        p2 = r1.astype(BF16).astype(F32)
        return p1, p2, (r1 - p2).astype(BF16).astype(F32)

    heads = DN_HEADS
    k_pieces, kqs = [], []
    for b in seqs:
        kq_bd = jnp.concatenate([own_head * k[b:b + 1, :], own_head * q[b:b + 1, :]], axis=0)
        a1, a2, a3 = pieces(kq_bd)
        s1, s2, s3 = pieces(s_ref[b])
        r1 = dot(jnp.concatenate([a1, a2, a3], axis=0), s1)
        r2 = dot(jnp.concatenate([a1, a2], axis=0), s2)
        r3 = dot(a1, s3)
        n = 2 * heads
        kqs.append(((r3 + r2[n:] + r1[2 * n:]) + (r2[:n] + r1[n:2 * n])) + r1[:n])
        k_pieces.append((a1[:heads], a2[:heads], a3[:heads]))
    egs = [eg_t[DN_HEADS:2 * DN_HEADS, b:b + 1] for b in seqs]
    qks = [jnp.sum(split(q[b:b + 1, :]) * split(k[b:b + 1, :]), axis=-1, keepdims=True) for b in seqs]
    v_news = [beta_t[0:DN_HEADS, b:b + 1] * (split(v[b:b + 1, :]) - eg * kq[:heads])
              for b, eg, kq in zip(seqs, egs, kqs)]
    os_ = [eg * kq[heads:] + qk * v_new for eg, kq, qk, v_new in zip(egs, kqs, qks, v_news)]
    inv_rms = [lax.rsqrt(jnp.mean(o * o, axis=-1, keepdims=True) + EPS) for o in os_]
    for b, o, r in zip(seqs, os_, inv_rms):
        o_ref[b] = o * r * dn * _silu(split(dz[b:b + 1, :]))
    outers, egrows = [], []
    for (k1, k2, k3), v_new, eg in zip(k_pieces, v_news, egs):
        v1, v2, v3 = pieces(v_new)
        lhs = jnp.concatenate([k1, k1, k2, k1, k2, k3], axis=0).astype(BF16)
        rhs = jnp.concatenate([v1, v2, v1, v3, v2, v1], axis=0).astype(BF16)
        outers.append(lax.dot_general(lhs, rhs, (((0,), (0,)), ((), ())), preferred_element_type=F32))
        egrows.append(dot(hrep3, jnp.concatenate(pieces(jnp.broadcast_to(eg, (DN_HEADS, DN_DV))), axis=0)))
    for b, outer, egrow in zip(seqs, outers, egrows):
        s_out_ref[b] = s_ref[b] * egrow + outer


def _gdn_sample(xc, dz, ba, sconv_t, state, conv_w, alog, dtb, dn):
    nseq = xc.shape[0]
    lane = np.arange(DN_WIDTH)
    hsum = jnp.asarray((lane[:, None] // DN_DV == lane[None, :] // DN_DV).astype(np.float32), dtype=BF16)
    eye = jnp.eye(GDN_S_BB, dtype=F32)
    hsel_np = (np.arange(DN_HEADS)[:, None] == lane[None, :] // DN_DK).astype(np.float32)
    hsel = jnp.asarray(hsel_np, dtype=BF16)
    hrep3 = jnp.asarray(np.tile(hsel_np.T, (1, 3)), dtype=BF16)
    row = lambda n: pl.BlockSpec((GDN_S_BB, n), lambda i: (i, 0))
    full = lambda a: pl.BlockSpec(a.shape, lambda i: (0,) * a.ndim)
    st = pl.BlockSpec((GDN_S_BB, DN_HEADS * DN_DK, DN_DV), lambda i: (i, 0, 0))
    consts = (conv_w, alog, dtb, dn, hsum, eye, hsel, hrep3)
    return pl.pallas_call(
        _gdn_sample_kernel,
        grid=(nseq // GDN_S_BB,),
        in_specs=[row(CONV_CH), row(DN_WIDTH), row(LANES),
                  pl.BlockSpec((CONV_WIDTH - 1, GDN_S_BB, CONV_CH), lambda i: (0, i, 0)), st]
                 + [full(a) for a in consts],
        out_specs=[pl.BlockSpec((GDN_S_BB, DN_HEADS, DN_DV), lambda i: (i, 0, 0)), st],
        out_shape=[jax.ShapeDtypeStruct((nseq, DN_HEADS, DN_DV), F32),
                   jax.ShapeDtypeStruct(state.shape, F32)],
        compiler_params=_params("parallel"),
        name="gdn_sample",
    )(xc, dz, ba, sconv_t, state, *consts)


def _attn_sample_lanes_kernel(att_ref, ck_ref, cv_ref, bucket_ref, rb_ref, sink_ref, o_ref, s_scr):
    g = pl.program_id(0)
    nseq = att_ref.shape[0]
    rnd = lambda a: a.astype(BF16).astype(F32)
    att = att_ref[...]
    q_all_t = (att[:, :ATT_WIDTH] * (HEAD_DIM ** -0.5)).T
    kv_new_t = att[:, ATT_WIDTH:].T
    qsel = [jnp.where(g == 0, q_all_t[hh * HEAD_DIM:(hh + 1) * HEAD_DIM],
                      q_all_t[(GQA + hh) * HEAD_DIM:(GQA + hh + 1) * HEAD_DIM]) for hh in range(GQA)]
    qr = [rnd(q) for q in qsel]
    kn = rnd(jnp.where(g == 0, kv_new_t[0:HEAD_DIM], kv_new_t[HEAD_DIM:2 * HEAD_DIM]))
    vn = rnd(jnp.where(g == 0, kv_new_t[2 * HEAD_DIM:3 * HEAD_DIM], kv_new_t[3 * HEAD_DIM:]))

    def score_row(j, carry):
        kj = rnd(ck_ref[j, 0])
        for hh in range(GQA):
            s_scr[hh, pl.ds(j, 1), :] = jnp.sum(qr[hh] * kj, axis=0, keepdims=True)
        return carry
    lax.fori_loop(0, WINDOW, score_row, 0, unroll=2)

    bucket = bucket_ref[...]
    jrow = lax.broadcasted_iota(jnp.int32, (WINDOW, nseq), 0)
    prn = []
    for hh in range(GQA):
        h = g * GQA + hh
        bias = jnp.where(jrow >= 1, _bias_lookup(bucket, rb_ref, h), NEG_INF)
        s = s_scr[hh] + bias
        s_n = jnp.sum(qr[hh] * kn, axis=0, keepdims=True) + rb_ref[0, h]
        sink = sink_ref[h]
        m = jnp.maximum(jnp.maximum(jnp.max(s, axis=0, keepdims=True), s_n), sink)
        p = jnp.exp(s - m)
        p_n = jnp.exp(s_n - m)
        den = jnp.sum(p, axis=0, keepdims=True) + p_n + jnp.exp(sink - m)
        s_scr[hh] = rnd(p / den)
        prn.append(rnd(p_n / den))

    def value_row(j, acc):
        vj = rnd(cv_ref[j, 0])
        return tuple(acc[hh] + s_scr[hh, pl.ds(j, 1), :] * vj for hh in range(GQA))
    zero = jnp.zeros((HEAD_DIM, nseq), F32)
    acc = lax.fori_loop(0, WINDOW, value_row, (zero,) * GQA, unroll=2)
    for hh in range(GQA):
        o_ref[hh * HEAD_DIM:(hh + 1) * HEAD_DIM, :] = acc[hh] + prn[hh] * vn


def _attn_sample_lanes(att, ck_t, cv_t, rel_bias, sink):
    nseq = att.shape[0]
    assert nseq == LANES
    bucket = jnp.asarray(np.broadcast_to(_t5_bucket_np(WINDOW - np.arange(WINDOW))[:, None], (WINDOW, nseq)))
    smem = pl.BlockSpec(memory_space=pltpu.SMEM)
    cache = pl.BlockSpec((WINDOW, 1, HEAD_DIM, nseq), lambda g: (0, g, 0, 0))
    full = lambda a: pl.BlockSpec(a.shape, lambda g: (0,) * a.ndim)
    return pl.pallas_call(
        _attn_sample_lanes_kernel,
        grid=(ATT_KV_HEADS,),
        in_specs=[full(att), cache, cache, full(bucket), smem, smem],
        out_specs=pl.BlockSpec((GQA * HEAD_DIM, nseq), lambda g: (g, 0)),
        out_shape=jax.ShapeDtypeStruct((ATT_WIDTH, nseq), F32),
        scratch_shapes=[pltpu.VMEM((GQA, WINDOW, nseq), F32)],
        compiler_params=_params("arbitrary"),
        name="attn_sample_lanes",
    )(att, ck_t, cv_t, bucket, rel_bias, sink)


def _gdn_sample_front_kernel(xc_ref, dz_ref, ba_ref, sc_ref, cw_ref, alog_ref, dtb_ref, hsum_ref,
                             q_ref, k_ref, v_ref, dz_t_ref, gates_ref):
    xc = xc_ref[...]
    y = sc_ref[0] * cw_ref[0:1, :]
    y = y + sc_ref[1] * cw_ref[1:2, :]
    y = y + sc_ref[2] * cw_ref[2:3, :]
    y = _silu(y + xc * cw_ref[3:4, :])
    hsum = hsum_ref[...]
    q = y[:, :DN_WIDTH]
    k = y[:, DN_WIDTH:2 * DN_WIDTH]
    q = q * lax.rsqrt(_mm_sel_rhs(q * q, hsum) + EPS) * (DN_DK ** -0.5)
    k = k * lax.rsqrt(_mm_sel_rhs(k * k, hsum) + EPS)
    beta_c, g_c = _gdn_gates(ba_ref[...], alog_ref[...], dtb_ref[...])
    q_ref[...] = q.T
    k_ref[...] = k.T
    v_ref[...] = y[:, 2 * DN_WIDTH:].T
    dz_t_ref[...] = dz_ref[...].T
    gates_ref[0:LANES, :] = beta_c.T
    gates_ref[LANES:, :] = jnp.exp(g_c).T


def _gdn_sample_step_kernel(q_ref, k_ref, v_ref, dz_ref, gates_ref, dn_ref, s_ref, o_ref, s_out_ref):
    h = pl.program_id(0)
    beta = gates_ref[pl.ds(h, 1), :]
    eg = gates_ref[pl.ds(LANES + DN_HEADS + h, 1), :]
    q, k, v = q_ref[...], k_ref[...], v_ref[...]
    w = (k * beta) * eg
    qg = q * eg
    ws = jnp.zeros(v.shape, F32)
    qs = jnp.zeros(v.shape, F32)
    for dk in range(DN_DK):
        s_dk = s_ref[0, dk]
        ws = ws + w[dk:dk + 1, :] * s_dk
        qs = qs + qg[dk:dk + 1, :] * s_dk
    v_new = v * beta - ws
    qk = jnp.sum(q * k, axis=0, keepdims=True)
    o = qs + qk * v_new
    for dk in range(DN_DK):
        s_out_ref[0, dk] = s_ref[0, dk] * eg + k[dk:dk + 1, :] * v_new
    o = o * lax.rsqrt(jnp.mean(o * o, axis=0, keepdims=True) + EPS) * dn_ref[...]
    o_ref[...] = o * _silu(dz_ref[...])


def _gdn_sample_lanes(xc, dz, ba, sconv_t, state_t, conv_w, alog, dtb, dn):
    nseq = xc.shape[0]
    assert nseq == LANES
    lane = np.arange(DN_WIDTH)
    hsum = jnp.asarray((lane[:, None] // DN_DV == lane[None, :] // DN_DV).astype(np.float32), dtype=BF16)
    full = lambda a: pl.BlockSpec(a.shape, lambda i: (0,) * a.ndim)
    cm = jax.ShapeDtypeStruct((DN_WIDTH, nseq), F32)
    front_in = (xc, dz, ba, sconv_t, conv_w, alog, dtb, hsum)
    q_t, k_t, v_t, dz_t, gates_t = pl.pallas_call(
        _gdn_sample_front_kernel,
        grid=(1,),
        in_specs=[full(a) for a in front_in],
        out_specs=[pl.BlockSpec((DN_WIDTH, nseq), lambda i: (0, 0))] * 4
                  + [pl.BlockSpec((2 * LANES, nseq), lambda i: (0, 0))],
        out_shape=[cm, cm, cm, cm, jax.ShapeDtypeStruct((2 * LANES, nseq), F32)],
        compiler_params=_params("arbitrary"),
        name="gdn_sample_front",
    )(*front_in)
    dn_b = jnp.broadcast_to(dn.reshape(DN_DV, 1), (DN_DV, nseq))
    head = pl.BlockSpec((DN_DK, nseq), lambda h: (h, 0))
    st = pl.BlockSpec((1, DN_DK, DN_DV, nseq), lambda h: (h, 0, 0, 0))
    return pl.pallas_call(
        _gdn_sample_step_kernel,
        grid=(DN_HEADS,),
        in_specs=[head, head, head, head, full(gates_t), full(dn_b), st],
        out_specs=[head, st],
        out_shape=[cm, jax.ShapeDtypeStruct(state_t.shape, F32)],
        compiler_params=_params("parallel"),
        name="gdn_sample_step",
    )(q_t, k_t, v_t, dz_t, gates_t, dn_b, state_t)


def _route(xn, wr):
    logits = jnp.dot(xn, wr, preferred_element_type=F32)
    lane = lax.broadcasted_iota(jnp.int32, logits.shape, 1).astype(F32)
    first_at = lambda hit: jnp.min(jnp.where(hit, lane, float(LANES)), axis=-1, keepdims=True)
    glog = jnp.where(lane < N_GROUPS, logits, NEG_INF)
    gmax = jnp.max(glog, axis=-1, keepdims=True)
    gsel = first_at(glog == gmax)
    pgsel = 1.0 / jnp.sum(jnp.exp(glog - gmax), axis=-1, keepdims=True)
    lo = ROUTER_OFF + gsel * EXPERTS_PER_GROUP
    in_group = jnp.logical_and(lane >= lo, lane < lo + EXPERTS_PER_GROUP)
    elog = jnp.where(in_group, logits, NEG_INF)
    m1 = jnp.max(elog, axis=-1, keepdims=True)
    i1 = first_at(elog == m1)
    z = jnp.sum(jnp.exp(elog - m1), axis=-1, keepdims=True)
    elog2 = jnp.where(lane == i1, NEG_INF, elog)
    m2 = jnp.max(elog2, axis=-1, keepdims=True)
    i2 = first_at(elog2 == m2)
    p1 = 1.0 / z
    p2 = jnp.exp(m2 - m1) / z
    tot = p1 + p2
    return lane, i1, i2, p1 / tot * pgsel, p2 / tot * pgsel


def _outproj(x_ref, oa_ref, od_ref, wo_ref):
    return x_ref[...] + _mm(oa_ref[...], wo_ref[:ATT_WIDTH, :]) + _mm(od_ref[...], wo_ref[ATT_WIDTH:, :])


def _outproj_router_kernel(x_ref, oa_ref, od_t_ref, wo_ref, g_ref, wr_ref, h_ref, xn_ref, gate_ref):
    h = (x_ref[...] + _mm(oa_ref[...], wo_ref[:ATT_WIDTH, :])
         + _mm(od_t_ref[...].T, wo_ref[ATT_WIDTH:, :]))
    h_ref[...] = h
    xn = _rmsnorm(h, g_ref[...]).astype(BF16)
    xn_ref[...] = xn
    lane, i1, i2, g1, g2 = _route(xn, wr_ref[...])
    gate_ref[...] = jnp.where(lane == i1, g1, 0.0) + jnp.where(lane == i2, g2, 0.0)


def _outproj_router(x, oa, od_t, wo, g, wr):
    t = x.shape[0]
    tm = t
    row = lambda n: pl.BlockSpec((tm, n), lambda i: (i, 0))
    full = lambda a: pl.BlockSpec(a.shape, lambda i: (0,) * a.ndim)
    return pl.pallas_call(
        _outproj_router_kernel,
        grid=(t // tm,),
        in_specs=[row(D_MODEL), row(ATT_WIDTH), full(od_t), full(wo), full(g), full(wr)],
        out_specs=[row(D_MODEL), row(D_MODEL), row(LANES)],
        out_shape=[jax.ShapeDtypeStruct((t, D_MODEL), F32), jax.ShapeDtypeStruct((t, D_MODEL), BF16),
                   jax.ShapeDtypeStruct((t, LANES), F32)],
        compiler_params=_params("parallel"),
        name="outproj_router",
    )(x, oa, od_t, wo, g, wr)


def _moe_kernel(xn_ref, gate_ref, wg_ref, wu_ref, wd_ref, o_ref):
    e = pl.program_id(1)
    xn = xn_ref[...]
    lane = lax.broadcasted_iota(jnp.int32, gate_ref.shape, 1)
    gate = jnp.sum(jnp.where(lane == e + ROUTER_OFF, gate_ref[...], 0.0), axis=-1, keepdims=True)
    hg = jnp.dot(xn, wg_ref[...].astype(BF16), preferred_element_type=F32)
    hu = jnp.dot(xn, wu_ref[...].astype(BF16), preferred_element_type=F32)
    hm = _silu(hg) * hu * gate
    y = jnp.dot(hm.astype(BF16), wd_ref[...].astype(BF16), preferred_element_type=F32)

    @pl.when(e == 0)
    def _():
        o_ref[...] = y

    @pl.when(e > 0)
    def _():
        o_ref[...] += y


def _moe(xn, gates, wg, wu, wd):
    t = xn.shape[0]
    tm = min(t, 1024)
    return pl.pallas_call(
        _moe_kernel,
        grid=(t // tm, N_EXPERTS),
        in_specs=[pl.BlockSpec((tm, D_MODEL), lambda i, e: (i, 0)),
                  pl.BlockSpec((tm, LANES), lambda i, e: (i, 0)),
                  pl.BlockSpec((None, D_MODEL, D_EXPERT), lambda i, e: (e, 0, 0)),
                  pl.BlockSpec((None, D_MODEL, D_EXPERT), lambda i, e: (e, 0, 0)),
                  pl.BlockSpec((None, D_EXPERT, D_MODEL), lambda i, e: (e, 0, 0))],
        out_specs=pl.BlockSpec((tm, D_MODEL), lambda i, e: (i, 0)),
        out_shape=jax.ShapeDtypeStruct((t, D_MODEL), F32),
        compiler_params=_params("parallel", "arbitrary"),
        name="moe",
    )(xn, gates, wg, wu, wd)


MOE_TM = 512
POS_TM = 1024
INFO_G1, INFO_G2, INFO_E1, INFO_E2 = 0, 1, 2, 3
DMA_UNROLL = 8


def _moe_tiles(t):
    return (2 * t) // MOE_TM + N_EXPERTS


HALF = D_MODEL // 2
U32 = jnp.uint32


def _pack_rows(x):
    bits = lambda v: lax.bitcast_convert_type(v.astype(BF16).astype(F32), U32)
    return bits(x[:, HALF:]) | (bits(x[:, :HALF]) >> 16)


def _unpack_rows(w):
    lo = lax.bitcast_convert_type(w << 16, F32)
    hi = lax.bitcast_convert_type(w & jnp.uint32(0xFFFF0000), F32)
    return lo, hi


def _route_kernel(x_ref, oa_ref, od_ref, wo_ref, g_ref, wr_ref, h_ref, xn_ref, info_ref, cnt_ref, run_scr):
    h = _outproj(x_ref, oa_ref, od_ref, wo_ref)
    h_ref[...] = h
    xn = _rmsnorm(h, g_ref[...])
    xn_ref[...] = _pack_rows(xn)
    lane, i1, i2, g1, g2 = _route(xn.astype(BF16), wr_ref[...])
    info = jnp.where(lane == INFO_G1, g1, 0.0) + jnp.where(lane == INFO_G2, g2, 0.0)
    info = info + jnp.where(lane == INFO_E1, i1, 0.0) + jnp.where(lane == INFO_E2, i2, 0.0)
    info_ref[...] = info

    @pl.when(pl.program_id(0) == 0)
    def _():
        run_scr[...] = jnp.zeros(run_scr.shape, F32)
    picked = jnp.logical_or(lane == i1, lane == i2).astype(F32)
    run_scr[...] += jnp.sum(picked, axis=0, keepdims=True)
    cnt_ref[...] = run_scr[...]


def _route_sparse(x, oa, od, wo, g, wr):
    t = x.shape[0]
    tm = ROW_TM
    row = lambda n: pl.BlockSpec((tm, n), lambda i: (i, 0))
    full = lambda a: pl.BlockSpec(a.shape, lambda i: (0,) * a.ndim)
    return pl.pallas_call(
        _route_kernel,
        grid=(t // tm,),
        in_specs=[row(D_MODEL), row(ATT_WIDTH), row(DN_WIDTH), full(wo), full(g), full(wr)],
        out_specs=[row(D_MODEL), row(HALF), row(LANES), pl.BlockSpec((1, LANES), lambda i: (0, 0))],
        out_shape=[jax.ShapeDtypeStruct((t, D_MODEL), F32), jax.ShapeDtypeStruct((t, HALF), U32),
                   jax.ShapeDtypeStruct((t, LANES), F32), jax.ShapeDtypeStruct((1, LANES), F32)],
        scratch_shapes=[pltpu.VMEM((1, LANES), F32)],
        compiler_params=_params("arbitrary"),
        name="route",
    )(x, oa, od, wo, g, wr)


def _positions_kernel(info_ref, cnt_ref, ltri_ref, utri_ref, pos_ref, run_scr, off_scr):
    info = info_ref[...]
    lane = lax.broadcasted_iota(jnp.int32, info.shape, 1).astype(F32)
    hit1 = lane == info[:, INFO_E1:INFO_E1 + 1]
    hit2 = lane == info[:, INFO_E2:INFO_E2 + 1]
    onehot = jnp.logical_or(hit1, hit2).astype(F32)

    @pl.when(pl.program_id(0) == 0)
    def _():
        tiles = jnp.floor((cnt_ref[...] + (MOE_TM - 1)) * (1.0 / MOE_TM))
        off_scr[...] = MOE_TM * jnp.dot(tiles.astype(BF16), utri_ref[...], preferred_element_type=F32)
        run_scr[...] = jnp.zeros(run_scr.shape, F32)

    before = (jnp.dot(ltri_ref[...], onehot.astype(BF16), preferred_element_type=F32)
              + run_scr[...] + off_scr[...])
    pos1 = jnp.sum(jnp.where(hit1, before, 0.0), axis=-1, keepdims=True)
    pos2 = jnp.sum(jnp.where(hit2, before, 0.0), axis=-1, keepdims=True)
    pos_ref[...] = (jnp.where(lane == 0, pos1, 0.0) + jnp.where(lane == 1, pos2, 0.0)).astype(jnp.int32)
    run_scr[...] += jnp.sum(onehot, axis=0, keepdims=True)


def _positions(info, cnt):
    t = info.shape[0]
    tm = min(t, POS_TM)
    tok = np.arange(tm)
    ltri = jnp.asarray((tok[:, None] > tok[None, :]).astype(np.float32), dtype=BF16)
    ln = np.arange(LANES)
    utri = jnp.asarray((ln[:, None] < ln[None, :]).astype(np.float32), dtype=BF16)
    full = lambda a: pl.BlockSpec(a.shape, lambda i: (0,) * a.ndim)
    return pl.pallas_call(
        _positions_kernel,
        grid=(t // tm,),
        in_specs=[pl.BlockSpec((tm, LANES), lambda i: (i, 0)), full(cnt), full(ltri), full(utri)],
        out_specs=pl.BlockSpec((tm, LANES), lambda i: (i, 0)),
        out_shape=jax.ShapeDtypeStruct((t, LANES), jnp.int32),
        scratch_shapes=[pltpu.VMEM((1, LANES), F32), pltpu.VMEM((1, LANES), F32)],
        compiler_params=_params("arbitrary"),
        name="positions",
    )(info, cnt, ltri, utri)


def _row_copy(src_hbm, src_row, dst_hbm, dst_row, sem):
    return pltpu.make_async_copy(src_hbm.at[pl.ds(src_row, 1)], dst_hbm.at[pl.ds(dst_row, 1)], sem)


SCATTER_SLOTS = 3


def _scatter_kernel(pos1_ref, pos2_ref, last_ref, used_ref, nt_ref, xn_hbm, zero_hbm, xs_hbm,
                    buf, lsem, sem, zsem, *, n_tok):
    max_tiles = xs_hbm.shape[0] // MOE_TM

    def zero_tile(tile):
        return pltpu.make_async_copy(zero_hbm, xs_hbm.at[pl.ds(tile * MOE_TM, MOE_TM)], zsem)

    def for_unused(fn):
        def body(tile, carry):
            fn(tile)
            return carry
        lax.fori_loop(nt_ref[0], max_tiles, body, 0)

    for e in range(N_EXPERTS):
        @pl.when(used_ref[e] > 0)
        def _():
            zero_tile(last_ref[e]).start()
    for_unused(lambda tile: zero_tile(tile).start())
    for e in range(N_EXPERTS):
        @pl.when(used_ref[e] > 0)
        def _():
            zero_tile(last_ref[e]).wait()
    for_unused(lambda tile: zero_tile(tile).wait())

    tm = buf.shape[1]
    n = n_tok // tm

    def load(i):
        return pltpu.make_async_copy(xn_hbm.at[pl.ds(i * tm, tm)], buf.at[i % SCATTER_SLOTS],
                                     lsem.at[i % SCATTER_SLOTS])

    def wait_rows(slot):
        pltpu.make_async_copy(xs_hbm.at[pl.ds(0, 2 * tm)], xs_hbm.at[pl.ds(0, 2 * tm)], sem.at[slot]).wait()

    load(0).start()
    load(1).start()

    def step(i, carry):
        slot = i % SCATTER_SLOTS
        load(i).wait()

        def body(j, c2):
            tok = i * tm + j
            src = buf.at[slot, pl.ds(j, 1)]
            pltpu.make_async_copy(src, xs_hbm.at[pl.ds(pos1_ref[tok], 1)], sem.at[slot]).start()
            pltpu.make_async_copy(src, xs_hbm.at[pl.ds(pos2_ref[tok], 1)], sem.at[slot]).start()
            return c2
        lax.fori_loop(0, tm, body, 0, unroll=DMA_UNROLL)

        @pl.when(i >= 1)
        def _():
            wait_rows((i - 1) % SCATTER_SLOTS)

        @pl.when(i + 2 < n)
        def _():
            load(i + 2).start()
        return carry
    lax.fori_loop(0, n, step, 0)
    wait_rows((n - 1) % SCATTER_SLOTS)


def _scatter_rows(xn, pos1, pos2, last_tile, used, n_tiles, n_rows):
    t = xn.shape[0]
    zero = jnp.zeros((MOE_TM, D_MODEL), F32)
    any_spec = pl.BlockSpec(memory_space=pl.ANY)
    return pl.pallas_call(
        functools.partial(_scatter_kernel, n_tok=t),
        grid_spec=pltpu.PrefetchScalarGridSpec(
            num_scalar_prefetch=5, grid=(1,),
            in_specs=[any_spec, any_spec], out_specs=any_spec,
            scratch_shapes=[pltpu.VMEM((SCATTER_SLOTS, MOE_TM, D_MODEL), F32),
                            pltpu.SemaphoreType.DMA((SCATTER_SLOTS,)),
                            pltpu.SemaphoreType.DMA((SCATTER_SLOTS,)),
                            pltpu.SemaphoreType.DMA]),
        out_shape=jax.ShapeDtypeStruct((n_rows, D_MODEL), F32),
        compiler_params=_params("arbitrary"),
        name="scatter_rows",
    )(pos1, pos2, last_tile, used, n_tiles, xn, zero)


def _experts_kernel(te_ref, tv_ref, nt_ref, xs_ref, wg_ref, wu_ref, wd_ref, ys_ref, wg_s, wu_s, wd_s):
    i = pl.program_id(0)
    used = i < nt_ref[0]

    @pl.when(jnp.logical_or(i == 0, te_ref[i] != te_ref[jnp.maximum(i - 1, 0)]))
    def _():
        wg_s[...] = wg_ref[...].astype(BF16)
        wu_s[...] = wu_ref[...].astype(BF16)
        wd_s[...] = wd_ref[...].astype(BF16)

    @pl.when(used)
    def _():
        row = lax.broadcasted_iota(jnp.int32, xs_ref.shape, 0)
        x_lo, x_hi = _unpack_rows(jnp.where(row < tv_ref[i], xs_ref[...], jnp.uint32(0)))
        x_lo = x_lo.astype(BF16)
        x_hi = x_hi.astype(BF16)
        up = lambda w_s: (jnp.dot(x_lo, w_s[:HALF, :], preferred_element_type=F32)
                          + jnp.dot(x_hi, w_s[HALF:, :], preferred_element_type=F32))
        hm = (_silu(up(wg_s)) * up(wu_s)).astype(BF16)
        ys_ref[...] = _pack_rows(jnp.dot(hm, wd_s[...], preferred_element_type=F32))

    @pl.when(jnp.logical_not(used))
    def _():
        ys_ref[...] = jnp.zeros(ys_ref.shape, U32)


def _experts(xs, tile_expert, tile_valid, n_tiles, wg, wu, wd):
    max_tiles = xs.shape[0] // MOE_TM
    rows = pl.BlockSpec((MOE_TM, HALF), lambda i, te, tv, nt: (i, 0))
    wspec = lambda shape: pl.BlockSpec((None,) + shape, lambda i, te, tv, nt: (te[i], 0, 0))
    return pl.pallas_call(
        _experts_kernel,
        grid_spec=pltpu.PrefetchScalarGridSpec(
            num_scalar_prefetch=3, grid=(max_tiles,),
            in_specs=[rows, wspec((D_MODEL, D_EXPERT)), wspec((D_MODEL, D_EXPERT)),
                      wspec((D_EXPERT, D_MODEL))],
            out_specs=rows,
            scratch_shapes=[pltpu.VMEM((D_MODEL, D_EXPERT), BF16), pltpu.VMEM((D_MODEL, D_EXPERT), BF16),
                            pltpu.VMEM((D_EXPERT, D_MODEL), BF16)]),
        out_shape=jax.ShapeDtypeStruct(xs.shape, U32),
        compiler_params=_params("arbitrary"),
        name="experts",
    )(tile_expert, tile_valid, n_tiles, xs, wg, wu, wd)


def _ple_gather_kernel(pos1_ref, pos2_ref, h_ref, info_ref, p_ref, wpp_ref, wpg_ref, gp_ref, gf_ref,
                       ys_hbm, y_ref, ybuf, sem):
    i = pl.program_id(0)
    n = pl.num_programs(0)
    tm = h_ref.shape[0]

    def issue(tile, slot):
        def body(j, carry):
            tok = tile * tm + j
            pltpu.make_async_copy(ys_hbm.at[pl.ds(pos1_ref[tok], 1)], ybuf.at[slot, 0, pl.ds(j, 1)],
                                  sem.at[slot]).start()
            pltpu.make_async_copy(ys_hbm.at[pl.ds(pos2_ref[tok], 1)], ybuf.at[slot, 1, pl.ds(j, 1)],
                                  sem.at[slot]).start()
            return carry
        lax.fori_loop(0, tm, body, 0, unroll=DMA_UNROLL)

    @pl.when(i == 0)
    def _():
        issue(0, 0)

    @pl.when(i + 1 < n)
    def _():
        issue(i + 1, (i + 1) % 2)

    slot = i % 2
    pltpu.make_async_copy(ybuf.at[slot], ybuf.at[slot], sem.at[slot]).wait()
    info = info_ref[...]
    moe = info[:, INFO_G1:INFO_G1 + 1] * ybuf[slot, 0] + info[:, INFO_G2:INFO_G2 + 1] * ybuf[slot, 1]
    h = h_ref[...] + moe
    hn = _rmsnorm(h, gp_ref[...])
    h = h + _mm(p_ref[...], wpp_ref[...]) * _sigmoid(_mm(hn, wpg_ref[...]))
    y_ref[...] = _rmsnorm(h, gf_ref[...])


def _ple_gather(h, info, p, ys, pos1, pos2, wpp, wpg, gp, gf):
    t = h.shape[0]
    tm = 256
    row = lambda n: pl.BlockSpec((tm, n), lambda i, p1, p2: (i, 0))
    full = lambda a: pl.BlockSpec(a.shape, lambda i, p1, p2: (0,) * a.ndim)
    return pl.pallas_call(
        _ple_gather_kernel,
        grid_spec=pltpu.PrefetchScalarGridSpec(
            num_scalar_prefetch=2, grid=(t // tm,),
            in_specs=[row(D_MODEL), row(LANES), row(PLE_DIM), full(wpp), full(wpg), full(gp), full(gf),
                      pl.BlockSpec(memory_space=pl.ANY)],
            out_specs=row(D_MODEL),
            scratch_shapes=[pltpu.VMEM((2, 2, tm, D_MODEL), F32), pltpu.SemaphoreType.DMA((2,))]),
        out_shape=jax.ShapeDtypeStruct((t, D_MODEL), F32),
        compiler_params=_params("arbitrary"),
        name="ple_gather",
    )(pos1, pos2, h, info, p, wpp, wpg, gp, gf, ys)


SC_IDX = 128
SC_ROWS = 64
SC_WORKERS = 32


def _sc_mesh():
    return plsc.VectorSubcoreMesh(core_axis_name="c", subcore_axis_name="s")


def _sc_windows(t, fn):
    per_worker = t // SC_WORKERS
    worker = lax.axis_index(("c", "s"))

    @pl.loop(0, per_worker // SC_IDX)
    def _(w):
        fn(worker * per_worker + w * SC_IDX)


def _sc_scatter_rows(xn, pos1, pos2, n_rows):
    t, d = xn.shape
    assert t % (SC_WORKERS * SC_IDX) == 0
    idx_t = pltpu.VMEM((1, SC_IDX), jnp.int32)

    @pl.kernel(out_type=jax.ShapeDtypeStruct((n_rows, d), xn.dtype), mesh=_sc_mesh(),
               scratch_types=[idx_t, idx_t, pltpu.VMEM((SC_ROWS, d), xn.dtype)])
    def scatter(x_hbm, p1_hbm, p2_hbm, o_hbm, i1_v, i2_v, buf):
        def window(base):
            pltpu.sync_copy(p1_hbm.at[:, pl.ds(base, SC_IDX)], i1_v)
            pltpu.sync_copy(p2_hbm.at[:, pl.ds(base, SC_IDX)], i2_v)
            for k in range(SC_IDX // SC_ROWS):
                pltpu.sync_copy(x_hbm.at[pl.ds(base + k * SC_ROWS, SC_ROWS)], buf)
                pltpu.sync_copy(buf, o_hbm.at[i1_v.at[0, pl.ds(k * SC_ROWS, SC_ROWS)]])
                pltpu.sync_copy(buf, o_hbm.at[i2_v.at[0, pl.ds(k * SC_ROWS, SC_ROWS)]])
        _sc_windows(t, window)

    return scatter(xn, pos1.reshape(1, t), pos2.reshape(1, t))


def _sc_gather_rows(ys, pos1, pos2):
    t = pos1.shape[0]
    d = ys.shape[1]
    assert t % (SC_WORKERS * SC_IDX) == 0
    idx_t = pltpu.VMEM((1, SC_IDX), jnp.int32)
    out = jax.ShapeDtypeStruct((t, d), ys.dtype)

    @pl.kernel(out_type=(out, out), mesh=_sc_mesh(),
               scratch_types=[idx_t, idx_t, pltpu.VMEM((SC_ROWS, d), ys.dtype)])
    def gather(y_hbm, p1_hbm, p2_hbm, o1_hbm, o2_hbm, i1_v, i2_v, buf):
        def window(base):
            pltpu.sync_copy(p1_hbm.at[:, pl.ds(base, SC_IDX)], i1_v)
            pltpu.sync_copy(p2_hbm.at[:, pl.ds(base, SC_IDX)], i2_v)
            for k in range(SC_IDX // SC_ROWS):
                rows = pl.ds(base + k * SC_ROWS, SC_ROWS)
                pltpu.sync_copy(y_hbm.at[i1_v.at[0, pl.ds(k * SC_ROWS, SC_ROWS)]], buf)
                pltpu.sync_copy(buf, o1_hbm.at[rows])
                pltpu.sync_copy(y_hbm.at[i2_v.at[0, pl.ds(k * SC_ROWS, SC_ROWS)]], buf)
                pltpu.sync_copy(buf, o2_hbm.at[rows])
        _sc_windows(t, window)

    return gather(ys, pos1.reshape(1, t), pos2.reshape(1, t))


def _ple_sparse_kernel(h_ref, info_ref, y1_ref, y2_ref, p_ref, wpp_ref, wpg_ref, gp_ref, gf_ref, y_ref):
    info = info_ref[...]
    g1 = info[:, INFO_G1:INFO_G1 + 1]
    g2 = info[:, INFO_G2:INFO_G2 + 1]
    y1_lo, y1_hi = _unpack_rows(y1_ref[...])
    y2_lo, y2_hi = _unpack_rows(y2_ref[...])
    moe = jnp.concatenate([g1 * y1_lo + g2 * y2_lo, g1 * y1_hi + g2 * y2_hi], axis=1)
    h = h_ref[...] + moe
    hn = _rmsnorm(h, gp_ref[...])
    h = h + _mm(p_ref[...], wpp_ref[...]) * _sigmoid(_mm(hn, wpg_ref[...]))
    y_ref[...] = _rmsnorm(h, gf_ref[...])


def _ple_sparse(h, info, y1, y2, p, wpp, wpg, gp, gf):
    t = h.shape[0]
    tm = ROW_TM
    row = lambda n: pl.BlockSpec((tm, n), lambda i: (i, 0))
    full = lambda a: pl.BlockSpec(a.shape, lambda i: (0,) * a.ndim)
    return pl.pallas_call(
        _ple_sparse_kernel,
        grid=(t // tm,),
        in_specs=[row(D_MODEL), row(LANES), row(HALF), row(HALF), row(PLE_DIM),
                  full(wpp), full(wpg), full(gp), full(gf)],
        out_specs=row(D_MODEL),
        out_shape=jax.ShapeDtypeStruct((t, D_MODEL), F32),
        compiler_params=_params("parallel"),
        name="ple_sparse",
    )(h, info, y1, y2, p, wpp, wpg, gp, gf)


def _tile_tables(cnt, max_tiles):
    tiles_e = (cnt + (MOE_TM - 1)) // MOE_TM
    ends = jnp.cumsum(tiles_e)
    n_tiles = ends[-1]
    tile = jnp.arange(max_tiles, dtype=jnp.int32)
    idx = jnp.minimum(tile, n_tiles - 1)
    tile_expert = jnp.sum((idx[:, None] >= ends[None, :]).astype(jnp.int32), axis=1)
    mine = tile_expert[:, None] == jnp.arange(N_EXPERTS, dtype=jnp.int32)[None, :]
    of_mine = lambda v: jnp.sum(jnp.where(mine, v[None, :], 0), axis=1)
    valid = jnp.clip(of_mine(cnt) - (idx - of_mine(ends - tiles_e)) * MOE_TM, 0, MOE_TM)
    tile_valid = jnp.where(tile < n_tiles, valid, 0).astype(jnp.int32)
    return (tile_expert, tile_valid, n_tiles.reshape(1), (ends - 1).astype(jnp.int32),
            tiles_e.astype(jnp.int32))


def _ple_final_kernel(h_ref, m_ref, p_ref, wpp_ref, wpg_ref, gp_ref, gf_ref, y_ref):
    h = h_ref[...] + m_ref[...]
    hn = _rmsnorm(h, gp_ref[...])
    h = h + _mm(p_ref[...], wpp_ref[...]) * _sigmoid(_mm(hn, wpg_ref[...]))
    y_ref[...] = _rmsnorm(h, gf_ref[...])


def _ple_final(h, m, p, wpp, wpg, gp, gf):
    t = h.shape[0]
    tm = min(t, 256)
    row = lambda n: pl.BlockSpec((tm, n), lambda i: (i, 0))
    full = lambda a: pl.BlockSpec(a.shape, lambda i: (0,) * a.ndim)
    return pl.pallas_call(
        _ple_final_kernel,
        grid=(t // tm,),
        in_specs=[row(D_MODEL), row(D_MODEL), row(PLE_DIM), full(wpp), full(wpg), full(gp), full(gf)],
        out_specs=row(D_MODEL),
        out_shape=jax.ShapeDtypeStruct((t, D_MODEL), F32),
        compiler_params=_params("parallel"),
        name="ple_final",
    )(h, m, p, wpp, wpg, gp, gf)


def kernel(x_prompt, x_sample, p_prompt, p_sample, cache_k, cache_v, state_conv, state_S, rel_bias, norm_mix, w_in, att_sink, conv_w, dn_A_log, dn_dt_bias, dn_norm, w_out, norm_ffn, w_router_group, w_router_expert, w_gate, w_up, w_down, w_ple_proj, w_ple_gate, norm_ple, norm_final):
    batch, seq, _ = x_prompt.shape
    nseq = x_sample.shape[0]
    assert x_sample.shape[1] == 1 and norm_mix.shape[0] == 1 and cache_k.shape[2] == WINDOW
    assert seq % GDN_TB == 0 and seq % ATT_BLOCK == 0

    wi = w_in[0]
    o_db = ATT_COLS + CONV_CH
    w_in_re = jnp.concatenate(
        [wi[:, :o_db], wi[:, o_db + 2 * DN_HEADS:], wi[:, o_db:o_db + 2 * DN_HEADS],
         jnp.zeros((D_MODEL, LANES - 2 * DN_HEADS), F32)], axis=1).astype(BF16)
    row = lambda a: a.reshape(1, -1).astype(F32)
    pad_lanes = lambda a, off: jnp.zeros((1, LANES), F32).at[0, off:off + a.shape[0]].set(a)
    alog = pad_lanes(dn_A_log[0], DN_HEADS)
    dtb = pad_lanes(dn_dt_bias[0], DN_HEADS)
    dnx = jnp.tile(dn_norm[0], DN_HEADS).reshape(1, DN_WIDTH)
    w_router = jnp.concatenate(
        [w_router_group[0], w_router_expert[0],
         jnp.zeros((D_MODEL, LANES - N_GROUPS - N_EXPERTS), F32)], axis=1).astype(BF16)
    wo = w_out[0].astype(BF16)
    wg, wu, wd = w_gate[0], w_up[0], w_down[0]
    wpp, wpg = w_ple_proj[0].astype(BF16), w_ple_gate[0].astype(BF16)
    sink = att_sink[0]

    qi = np.arange(ATT_BLOCK)[:, None]
    kj = np.arange(2 * ATT_BLOCK)[None, :]
    bucket_p = jnp.asarray(_t5_bucket_np(qi + ATT_BLOCK - kj))
    bucket_s = jnp.asarray(_t5_bucket_np(WINDOW - np.arange(WINDOW)[None, :]))

    def tail(x, o_att, o_dn, p):
        h1, xn2, gates = _outproj_router(x, o_att, o_dn, wo, row(norm_ffn[0]), w_router)
        moe = _moe(xn2, gates, wg, wu, wd)
        return _ple_final(h1, moe, p, wpp, wpg, row(norm_ple[0]), row(norm_final))

    xp = x_prompt.reshape(batch * seq, D_MODEL)
    att_p, qkv_p, dz_p, ba_p, xc_tails = _inproj_conv(xp, row(norm_mix[0]), w_in_re, conv_w[0], seq)
    o_att_p = _attn_prompt(att_p, bucket_p, rel_bias, sink, batch, seq)
    o_dn_p, s_p = _gdn_prompt(qkv_p, dz_p, ba_p, alog, dtb, dnx, batch, seq)
    h1, xn2, info, cnt = _route_sparse(xp, o_att_p, o_dn_p, wo, row(norm_ffn[0]), w_router)
    pos = _positions(info, cnt)
    pos1, pos2 = pos[:, 0], pos[:, 1]
    max_tiles = _moe_tiles(batch * seq)
    cnt_e = cnt[0, ROUTER_OFF:ROUTER_OFF + N_EXPERTS].astype(jnp.int32)
    tile_expert, tile_valid, n_tiles, last_tile, used = _tile_tables(cnt_e, max_tiles)
    xs = _sc_scatter_rows(xn2, pos1, pos2, max_tiles * MOE_TM)
    ys = _experts(xs, tile_expert, tile_valid, n_tiles, wg, wu, wd)
    y1, y2 = _sc_gather_rows(ys, pos1, pos2)
    y_p = _ple_sparse(h1, info, y1, y2, p_prompt[0].reshape(batch * seq, PLE_DIM),
                      wpp, wpg, row(norm_ple[0]), row(norm_final))

    xs = x_sample.reshape(nseq, D_MODEL)
    att_s, xc_s, dz_s, ba_s = _inproj(xs, row(norm_mix[0]), w_in_re)
    ck_t = jnp.transpose(cache_k[0], (0, 2, 3, 1))
    cv_t = jnp.transpose(cache_v[0], (0, 2, 3, 1))
    o_att_s, ks_t, vs_t = _attn_sample(att_s, ck_t, cv_t, bucket_s, rel_bias, sink)
    sconv_t = jnp.swapaxes(state_conv[0], 0, 1)
    o_dn_s_t, s_s_t = _gdn_sample_lanes(xc_s, dz_s, ba_s, sconv_t, jnp.transpose(state_S[0], (1, 2, 3, 0)),
                                        conv_w[0], alog, dtb, dn_norm[0])
    s_s = jnp.transpose(s_s_t, (3, 0, 1, 2))
    y_s = tail(xs, o_att_s, o_dn_s_t, p_sample[0].reshape(nseq, PLE_DIM))

    att_p3 = att_p.reshape(batch, seq, ATT_COLS)
    kv_shape = (1, batch, WINDOW, ATT_KV_HEADS, HEAD_DIM)
    k_p = att_p3[:, seq - WINDOW:, ATT_WIDTH:ATT_WIDTH + KV_WIDTH].reshape(kv_shape)
    v_p = att_p3[:, seq - WINDOW:, ATT_WIDTH + KV_WIDTH:].reshape(kv_shape)
    conv_p = xc_tails.reshape(batch, -1, TAIL, CONV_CH)[:, -1, TAIL - (CONV_WIDTH - 1):][None]
    k_s = jnp.transpose(ks_t, (0, 3, 1, 2))[None]
    v_s = jnp.transpose(vs_t, (0, 3, 1, 2))[None]
    conv_s = jnp.concatenate([state_conv[0][:, 1:], xc_s[:, None, :]], axis=1)[None]
    return (y_p.reshape(batch, seq, D_MODEL), y_s.reshape(nseq, 1, D_MODEL),
            k_p, v_p, conv_p, s_p[None], k_s, v_s, conv_s, s_s[None])
```

```python
import functools
import math

import numpy as np
import jax
import jax.numpy as jnp
from jax import lax
from jax.experimental import pallas as pl
from jax.experimental.pallas import tpu as pltpu
from jax.experimental.pallas import tpu_sc as plsc

F32 = jnp.float32
BF16 = jnp.bfloat16

D_MODEL = 1024
ATT_HEADS = 8
ATT_KV_HEADS = 2
HEAD_DIM = 64
GQA = ATT_HEADS // ATT_KV_HEADS
WINDOW = 128
ATT_BLOCK = 128
N_BUCKETS = 32
DN_HEADS = 8
DN_DK = 64
DN_DV = 64
CONV_WIDTH = 4
DN_CHUNK = 64
ATT_WIDTH = ATT_HEADS * HEAD_DIM
KV_WIDTH = ATT_KV_HEADS * HEAD_DIM
DN_WIDTH = DN_HEADS * DN_DV
CONV_CH = 3 * DN_WIDTH
N_GROUPS = 4
EXPERTS_PER_GROUP = 8
N_EXPERTS = N_GROUPS * EXPERTS_PER_GROUP
D_EXPERT = 256
PLE_DIM = 256
EPS = 1e-6
NEG_INF = float("-inf")

ATT_COLS = ATT_WIDTH + 2 * KV_WIDTH
LANES = 128
IN_COLS = ATT_COLS + CONV_CH + DN_WIDTH + LANES
ROUTER_OFF = N_GROUPS
VMEM_LIMIT = 48 * 1024 * 1024
ROW_TM = 512


def _params(*sem):
    return pltpu.CompilerParams(dimension_semantics=sem, vmem_limit_bytes=VMEM_LIMIT)


def _mm(a, b):
    return jnp.dot(a.astype(BF16), b.astype(BF16), preferred_element_type=F32)


def _mm_nt(a, b):
    return lax.dot_general(a.astype(BF16), b.astype(BF16), (((1,), (1,)), ((), ())),
                           preferred_element_type=F32)


def _mm_tn(a, b):
    return lax.dot_general(a.astype(BF16), b.astype(BF16), (((0,), (0,)), ((), ())),
                           preferred_element_type=F32)


def _split3(x):
    h1 = x.astype(BF16)
    r1 = x - h1.astype(F32)
    h2 = r1.astype(BF16)
    h3 = (r1 - h2.astype(F32)).astype(BF16)
    return h1, h2, h3


def _mm_sel_rhs(x, sel):
    h1, h2, h3 = _split3(x)
    d = lambda h: jnp.dot(h, sel, preferred_element_type=F32)
    return d(h1) + d(h2) + d(h3)


def _mm_sel_lhs(sel, x):
    h1, h2, h3 = _split3(x)
    d = lambda h: jnp.dot(sel, h, preferred_element_type=F32)
    return d(h1) + d(h2) + d(h3)


def _mm3(a, b):
    ah = a.astype(BF16)
    al = (a - ah.astype(F32)).astype(BF16)
    bh = b.astype(BF16)
    bl = (b - bh.astype(F32)).astype(BF16)
    d = lambda u, v: jnp.dot(u, v, preferred_element_type=F32)
    return d(ah, bh) + d(ah, bl) + d(al, bh)


def _sigmoid(x):
    return 1.0 / (1.0 + jnp.exp(-x))


def _silu(x):
    return x * _sigmoid(x)


def _softplus(x):
    return jnp.maximum(x, 0.0) + jnp.log1p(jnp.exp(-jnp.abs(x)))


def _rmsnorm(x, g):
    return x * lax.rsqrt(jnp.mean(x * x, axis=-1, keepdims=True) + EPS) * g


def _t5_bucket_np(dist):
    max_exact = N_BUCKETS // 2
    d = np.maximum(dist, 0)
    ratio = (np.log(np.maximum(d, 1).astype(np.float32) / np.float32(max_exact))
             / np.float32(math.log(WINDOW / max_exact))).astype(np.float32)
    large = np.minimum(max_exact + (ratio * np.float32(N_BUCKETS - max_exact)).astype(np.int32),
                       N_BUCKETS - 1)
    return np.where(d < max_exact, d, large).astype(np.int32)


def _bias_lookup(bucket, rb_ref, h):
    acc = jnp.zeros(bucket.shape, F32)
    for t in range(N_BUCKETS):
        acc = jnp.where(bucket == t, rb_ref[t, h], acc)
    return acc


def _inproj_kernel(x_ref, g_ref, w_ref, att_ref, xc_ref, dz_ref, ba_ref):
    xn = _rmsnorm(x_ref[...], g_ref[...]).astype(BF16)
    o0, o1, o2 = ATT_COLS, ATT_COLS + CONV_CH, ATT_COLS + CONV_CH + DN_WIDTH
    att_ref[...] = jnp.dot(xn, w_ref[:, :o0], preferred_element_type=F32)
    xc_ref[...] = jnp.dot(xn, w_ref[:, o0:o1], preferred_element_type=F32)
    dz_ref[...] = jnp.dot(xn, w_ref[:, o1:o2], preferred_element_type=F32)
    ba_ref[...] = jnp.dot(xn, w_ref[:, o2:], preferred_element_type=F32)


def _inproj(x, g, w):
    t = x.shape[0]
    tm = min(t, ROW_TM)
    row = lambda n: pl.BlockSpec((tm, n), lambda i: (i, 0))
    full = lambda a: pl.BlockSpec(a.shape, lambda i: (0,) * a.ndim)
    return pl.pallas_call(
        _inproj_kernel,
        grid=(t // tm,),
        in_specs=[row(D_MODEL), full(g), full(w)],
        out_specs=[row(ATT_COLS), row(CONV_CH), row(DN_WIDTH), row(LANES)],
        out_shape=[jax.ShapeDtypeStruct((t, n), F32) for n in (ATT_COLS, CONV_CH, DN_WIDTH, LANES)],
        compiler_params=_params("parallel"),
        name="inproj",
    )(x, g, w)


TAIL = 8
PAIR = 2 * DN_DK
N_PAIRS = DN_WIDTH // PAIR


def _head_sums(z, pair_ones):
    hi = z.astype(BF16)
    lw = (z - hi.astype(F32)).astype(BF16)
    d = lambda a, p: jnp.dot(a[:, p * PAIR:(p + 1) * PAIR], pair_ones, preferred_element_type=F32)
    return jnp.concatenate([d(hi, p) + d(lw, p) for p in range(N_PAIRS)], axis=1)


def _inproj_conv_kernel(x_ref, g_ref, w_ref, cw_ref, ones_ref, att_ref, qkv_ref, dz_ref, ba_ref, tail_ref,
                        xp_scr, *, tiles_per_seq):
    tm = x_ref.shape[0]

    @pl.when(pl.program_id(0) % tiles_per_seq == 0)
    def _():
        xp_scr[0:TAIL, :] = jnp.zeros((TAIL, CONV_CH), F32)

    xn = _rmsnorm(x_ref[...], g_ref[...]).astype(BF16)
    o0, o1, o2 = ATT_COLS, ATT_COLS + CONV_CH, ATT_COLS + CONV_CH + DN_WIDTH
    xc = jnp.dot(xn, w_ref[:, o0:o1], preferred_element_type=F32)
    att_ref[...] = jnp.dot(xn, w_ref[:, :o0], preferred_element_type=F32)
    dz_ref[...] = jnp.dot(xn, w_ref[:, o1:o2], preferred_element_type=F32)
    ba_ref[...] = jnp.dot(xn, w_ref[:, o2:], preferred_element_type=F32)

    xp_scr[TAIL:, :] = xc
    y = xp_scr[TAIL - 3:TAIL - 3 + tm, :] * cw_ref[0:1, :]
    y = y + xp_scr[TAIL - 2:TAIL - 2 + tm, :] * cw_ref[1:2, :]
    y = y + xp_scr[TAIL - 1:TAIL - 1 + tm, :] * cw_ref[2:3, :]
    y = y + xc * cw_ref[3:4, :]
    tail = xc[tm - TAIL:, :]
    xp_scr[0:TAIL, :] = tail
    tail_ref[0] = tail
    y = _silu(y)
    q = y[:, :DN_WIDTH]
    k = y[:, DN_WIDTH:2 * DN_WIDTH]
    inv_norm = lax.rsqrt(_head_sums(jnp.concatenate([q * q, k * k], axis=0), ones_ref[...]) + EPS)
    qkv_ref[:, :DN_WIDTH] = q * inv_norm[:tm] * (DN_DK ** -0.5)
    qkv_ref[:, DN_WIDTH:2 * DN_WIDTH] = k * inv_norm[tm:]
    qkv_ref[:, 2 * DN_WIDTH:] = y[:, 2 * DN_WIDTH:]


def _pair_ones():
    lane = np.arange(PAIR)
    return jnp.asarray((lane[:, None] // DN_DV == lane[None, :] // DN_DV).astype(np.float32), dtype=BF16)


def _inproj_conv(x, g, w, conv_w, seq):
    t = x.shape[0]
    tm = ROW_TM
    assert seq % tm == 0
    ones = _pair_ones()
    row = lambda n: pl.BlockSpec((tm, n), lambda i: (i, 0))
    full = lambda a: pl.BlockSpec(a.shape, lambda i: (0,) * a.ndim)
    return pl.pallas_call(
        functools.partial(_inproj_conv_kernel, tiles_per_seq=seq // tm),
        grid=(t // tm,),
        in_specs=[row(D_MODEL), full(g), full(w), full(conv_w), full(ones)],
        out_specs=[row(ATT_COLS), row(CONV_CH), row(DN_WIDTH), row(LANES),
                   pl.BlockSpec((1, TAIL, CONV_CH), lambda i: (i, 0, 0))],
        out_shape=[jax.ShapeDtypeStruct((t, n), F32) for n in (ATT_COLS, CONV_CH, DN_WIDTH, LANES)]
                  + [jax.ShapeDtypeStruct((t // tm, TAIL, CONV_CH), F32)],
        scratch_shapes=[pltpu.VMEM((TAIL + tm, CONV_CH), F32)],
        compiler_params=_params("arbitrary"),
        name="inproj_conv",
    )(x, g, w, conv_w, ones)


GROUP_ROWS = GQA * ATT_BLOCK


def _attn_prompt_kernel(cur_ref, prev_ref, bucket_ref, rb_ref, sink_ref, o_ref, bias_scr, sink_scr):
    i = pl.program_id(0)
    nseq = cur_ref.shape[0]

    @pl.when(i == 0)
    def _():
        qi = lax.broadcasted_iota(jnp.int32, (ATT_BLOCK, 2 * ATT_BLOCK), 0)
        kj = lax.broadcasted_iota(jnp.int32, (ATT_BLOCK, 2 * ATT_BLOCK), 1)
        dist = qi + ATT_BLOCK - kj
        band = jnp.logical_and(dist >= 0, dist < WINDOW)
        bucket = bucket_ref[...]
        hrow = lax.broadcasted_iota(jnp.int32, (GROUP_ROWS, 1), 0) // ATT_BLOCK
        for g in range(ATT_KV_HEADS):
            sink_col = jnp.zeros((GROUP_ROWS, 1), F32)
            for hh in range(GQA):
                h = g * GQA + hh
                bias = jnp.where(band, _bias_lookup(bucket, rb_ref, h), NEG_INF)
                bias_scr[0, g, hh * ATT_BLOCK:(hh + 1) * ATT_BLOCK, :] = bias
                bias_scr[1, g, hh * ATT_BLOCK:(hh + 1) * ATT_BLOCK, :] = jnp.where(kj >= ATT_BLOCK, bias, NEG_INF)
                sink_col = jnp.where(hrow == hh, sink_ref[h], sink_col)
            sink_scr[g] = sink_col

    first = (i == 0).astype(jnp.int32)
    probs = [(b, g) for b in range(nseq) for g in range(ATT_KV_HEADS)]
    scores = []
    for b, g in probs:
        cur = cur_ref[b]
        prev = prev_ref[b]
        q = jnp.concatenate([cur[:, (g * GQA + hh) * HEAD_DIM:(g * GQA + hh + 1) * HEAD_DIM]
                             for hh in range(GQA)], axis=0) * (HEAD_DIM ** -0.5)
        kcol = slice(ATT_WIDTH + g * HEAD_DIM, ATT_WIDTH + (g + 1) * HEAD_DIM)
        k2 = jnp.concatenate([prev[:, kcol], cur[:, kcol]], axis=0)
        scores.append(_mm_nt(q, k2) + bias_scr[first, g])
    probs_p, dens = [], []
    for (b, g), s in zip(probs, scores):
        sink = sink_scr[g]
        m = jnp.maximum(jnp.max(s, axis=-1, keepdims=True), sink)
        p = jnp.exp(s - m)
        dens.append(jnp.sum(p, axis=-1, keepdims=True) + jnp.exp(sink - m))
        probs_p.append(p)
    outs = {}
    for (b, g), p, den in zip(probs, probs_p, dens):
        vcol = slice(ATT_WIDTH + KV_WIDTH + g * HEAD_DIM, ATT_WIDTH + KV_WIDTH + (g + 1) * HEAD_DIM)
        v2 = jnp.concatenate([prev_ref[b][:, vcol], cur_ref[b][:, vcol]], axis=0)
        outs[b, g] = _mm(p, v2) / den
    for b in range(nseq):
        o_ref[b] = jnp.concatenate([outs[b, g][hh * ATT_BLOCK:(hh + 1) * ATT_BLOCK, :]
                                    for g in range(ATT_KV_HEADS) for hh in range(GQA)],
                                   axis=1).astype(o_ref.dtype)


def _attn_prompt(att, bucket, rel_bias, sink, batch, seq):
    nb = seq // ATT_BLOCK
    smem = pl.BlockSpec(memory_space=pltpu.SMEM)
    att3 = att.reshape(batch, seq, ATT_COLS)
    out = pl.pallas_call(
        _attn_prompt_kernel,
        grid=(nb,),
        in_specs=[
            pl.BlockSpec((batch, ATT_BLOCK, ATT_COLS), lambda i: (0, i, 0)),
            pl.BlockSpec((batch, ATT_BLOCK, ATT_COLS), lambda i: (0, jnp.maximum(i - 1, 0), 0)),
            pl.BlockSpec(bucket.shape, lambda i: (0, 0)),
            smem, smem,
        ],
        out_specs=pl.BlockSpec((batch, ATT_BLOCK, ATT_WIDTH), lambda i: (0, i, 0)),
        out_shape=jax.ShapeDtypeStruct((batch, seq, ATT_WIDTH), BF16),
        scratch_shapes=[pltpu.VMEM((2, ATT_KV_HEADS, GROUP_ROWS, 2 * ATT_BLOCK), F32),
                        pltpu.VMEM((ATT_KV_HEADS, GROUP_ROWS, 1), F32)],
        compiler_params=_params("arbitrary"),
        name="attn_prompt",
    )(att3, att3, bucket, rel_bias, sink)
    return out.reshape(batch * seq, ATT_WIDTH)


ATT_S_BB = 8


def _attn_sample_kernel(att_ref, ck_ref, cv_ref, bucket_ref, rb_ref, sink_ref, o_ref, ks_ref, vs_ref,
                        bias_scr, col_scr):
    hrow = lax.broadcasted_iota(jnp.int32, (ATT_HEADS, LANES), 0)
    lane = lax.broadcasted_iota(jnp.int32, (ATT_HEADS, LANES), 1)

    last = (lax.broadcasted_iota(jnp.int32, (3, WINDOW), 1) == WINDOW - 1).astype(BF16)
    is_last = lax.broadcasted_iota(jnp.int32, (KV_WIDTH, WINDOW), 1) == WINDOW - 1

    def shifted(cache_t, new_row):
        pieces = jnp.concatenate([p.astype(F32) for p in _split3(new_row)], axis=0).astype(BF16)
        col = lax.dot_general(pieces, last, (((0,), (0,)), ((), ())), preferred_element_type=F32)
        out = jnp.where(is_last, col, pltpu.roll(cache_t, WINDOW - 1, axis=1))
        return out.reshape(ATT_KV_HEADS, HEAD_DIM, WINDOW)

    for b in range(ATT_S_BB):
        row = att_ref[b:b + 1, :]
        ks_ref[b] = shifted(ck_ref[b].reshape(KV_WIDTH, WINDOW), row[:, ATT_WIDTH:ATT_WIDTH + KV_WIDTH])
        vs_ref[b] = shifted(cv_ref[b].reshape(KV_WIDTH, WINDOW), row[:, ATT_WIDTH + KV_WIDTH:])

    @pl.when(pl.program_id(0) == 0)
    def _():
        bucket = jnp.broadcast_to(bucket_ref[...], (ATT_HEADS, LANES))
        bias = jnp.zeros((ATT_HEADS, LANES), F32)
        cols = jnp.zeros((ATT_HEADS, LANES), F32)
        for h in range(ATT_HEADS):
            bias = jnp.where(hrow == h, _bias_lookup(bucket, rb_ref, h), bias)
            cols = jnp.where(jnp.logical_and(hrow == h, lane == 0), sink_ref[h], cols)
            cols = jnp.where(jnp.logical_and(hrow == h, lane == 1), rb_ref[0, h], cols)
        bias_scr[...] = jnp.where(lane >= 1, bias, NEG_INF)
        col_scr[...] = cols

    bias_c = bias_scr[...]
    sink = col_scr[:, 0:1]
    bias_n = col_scr[:, 1:2]
    same_group = (hrow // GQA) == (lane // HEAD_DIM)
    low_group = lax.broadcasted_iota(jnp.int32, (ATT_HEADS, HEAD_DIM), 0) < GQA
    rnd = lambda a: a.astype(BF16).astype(F32)
    seqs = range(ATT_S_BB)
    rows = [att_ref[b:b + 1, :] for b in seqs]
    q_bds = []
    for row in rows:
        q = row[:, :ATT_WIDTH] * (HEAD_DIM ** -0.5)
        qh = jnp.concatenate([q[:, h * HEAD_DIM:(h + 1) * HEAD_DIM] for h in range(ATT_HEADS)], axis=0)
        q_bds.append(jnp.where(same_group, jnp.concatenate([qh, qh], axis=1), 0.0))
    kv_t = lambda ref, b: ref[b].reshape(KV_WIDTH, WINDOW)
    s_cs = [_mm(q_bd, kv_t(ck_ref, b)) + bias_c for b, q_bd in zip(seqs, q_bds)]
    prs, pns = [], []
    for row, q_bd, s_c in zip(rows, q_bds, s_cs):
        kn = row[:, ATT_WIDTH:ATT_WIDTH + KV_WIDTH]
        s_n = jnp.sum(rnd(q_bd) * rnd(kn), axis=-1, keepdims=True) + bias_n
        m = jnp.maximum(jnp.maximum(jnp.max(s_c, axis=-1, keepdims=True), s_n), sink)
        p_c = jnp.exp(s_c - m)
        p_n = jnp.exp(s_n - m)
        den = jnp.sum(p_c, axis=-1, keepdims=True) + p_n + jnp.exp(sink - m)
        prs.append(p_c / den)
        pns.append(p_n / den)
    pvs = [_mm_nt(pr, kv_t(cv_ref, b)) for b, pr in zip(seqs, prs)]
    for b, row, pv, pn in zip(seqs, rows, pvs, pns):
        vn = row[:, ATT_WIDTH + KV_WIDTH:]
        o_full = pv + rnd(pn) * rnd(vn)
        o_sel = jnp.where(low_group, o_full[:, :HEAD_DIM], o_full[:, HEAD_DIM:])
        o_ref[b:b + 1, :] = jnp.concatenate([o_sel[h:h + 1, :] for h in range(ATT_HEADS)], axis=1)


def _attn_sample(att, ck, cv, bucket, rel_bias, sink):
    nseq = att.shape[0]
    smem = pl.BlockSpec(memory_space=pltpu.SMEM)
    cache = pl.BlockSpec((ATT_S_BB, ATT_KV_HEADS, HEAD_DIM, WINDOW), lambda i: (i, 0, 0, 0))
    return pl.pallas_call(
        _attn_sample_kernel,
        grid=(nseq // ATT_S_BB,),
        in_specs=[pl.BlockSpec((ATT_S_BB, ATT_COLS), lambda i: (i, 0)), cache, cache,
                  pl.BlockSpec(bucket.shape, lambda i: (0, 0)), smem, smem],
        out_specs=[pl.BlockSpec((ATT_S_BB, ATT_WIDTH), lambda i: (i, 0)), cache, cache],
        out_shape=[jax.ShapeDtypeStruct((nseq, ATT_WIDTH), F32),
                   jax.ShapeDtypeStruct(ck.shape, F32), jax.ShapeDtypeStruct(cv.shape, F32)],
        scratch_shapes=[pltpu.VMEM((ATT_HEADS, LANES), F32), pltpu.VMEM((ATT_HEADS, LANES), F32)],
        compiler_params=_params("arbitrary"),
        name="attn_sample",
    )(att, ck, cv, bucket, rel_bias, sink)


GDN_TB = 128
GDN_NC = GDN_TB // DN_CHUNK


def _gdn_gates(ba, alog, dtb):
    beta = _sigmoid(ba)
    g = -jnp.exp(alog) * _softplus(ba + dtb)
    return beta, g


def _pair_diag(x, lo):
    xb = x.astype(BF16)
    zero = jnp.zeros_like(xb)
    return jnp.concatenate([jnp.where(lo, xb, zero), jnp.where(lo, zero, xb)], axis=0)


def _gdn_prompt_kernel(qkv_ref, dz_ref, ba_ref, alog_ref, dtb_ref, dnx_ref,
                       hsum_ref, expb_ref, expg_ref, ltri_ref,
                       o_ref, s_out_ref, s_scr):
    i = pl.program_id(0)
    nb = qkv_ref.shape[0]

    @pl.when(i == 0)
    def _():
        s_scr[...] = jnp.zeros(s_scr.shape, F32)

    hsum = hsum_ref[...]
    ri = lax.broadcasted_iota(jnp.int32, (DN_CHUNK, PAIR), 0)
    ci = lax.broadcasted_iota(jnp.int32, (DN_CHUNK, PAIR), 1)
    lo = ci < DN_DK
    cj = jnp.where(lo, ci, ci - DN_DK)
    causal = ri >= cj
    strict = ri > cj
    eye = (ri == cj).astype(F32)

    def sel2(x, m):
        hi = x.astype(BF16)
        lw = (x - hi.astype(F32)).astype(BF16)
        return (jnp.dot(hi, m, preferred_element_type=F32) + jnp.dot(lw, m, preferred_element_type=F32))

    pre = []
    for b in range(nb):
        q = qkv_ref[b, :, :DN_WIDTH]
        k = qkv_ref[b, :, DN_WIDTH:2 * DN_WIDTH]
        v = qkv_ref[b, :, 2 * DN_WIDTH:]
        beta_c, g_c = _gdn_gates(ba_ref[b], alog_ref[...], dtb_ref[...])
        beta = sel2(beta_c, expb_ref[...])
        gam_c = _mm_sel_lhs(ltri_ref[...], g_c)
        gam = _mm_sel_rhs(gam_c, expg_ref[...])
        gam_t = gam_c.T
        kb = k * beta
        egam = jnp.exp(gam)
        pre.append(dict(q=q, k=k, kb=kb, vb=v * beta, qg=q * egam, wr=kb * egam, gam=gam, gam_t=gam_t))

    probs = [(b, p) for b in range(nb) for p in range(N_PAIRS)]
    pick = lambda m: jnp.where(lo, m[:DN_DK], m[DN_DK:])
    o_rows = [[] for _ in range(nb)]
    for c in range(GDN_NC):
        r0, r1 = c * DN_CHUNK, (c + 1) * DN_CHUNK
        sl = lambda name, b, p: pre[b][name][r0:r1, p * PAIR:(p + 1) * PAIR]
        raws = []
        for b, p in probs:
            k_p = sl("k", b, p)
            k_rows = jnp.concatenate([jnp.where(lo, k_p, 0.0), jnp.where(lo, 0.0, k_p)], axis=0)
            raws.append(_mm_nt(jnp.concatenate([sl("kb", b, p), sl("q", b, p)], axis=0), k_rows))
        pws, ts, qks = [], [], []
        for (b, p), raw in zip(probs, raws):
            gcol = sl("gam", b, p)
            h0 = DN_HEADS + 2 * p
            gam_t = pre[b]["gam_t"]
            grow = jnp.concatenate([gam_t[h0:h0 + 1, r0:r1], gam_t[h0 + 1:h0 + 2, r0:r1]], axis=1)
            decay = jnp.exp(jnp.where(causal, gcol - grow, NEG_INF))
            a = jnp.where(strict, raw[:DN_CHUNK] * decay, 0.0)
            qks.append(jnp.where(causal, raw[DN_CHUNK:] * decay, 0.0))
            pws.append(-a)
            ts.append(eye - a)
        pws = [_mm(pw, _pair_diag(pw, lo)) for pw in pws]
        for _ in range(4):
            rs = [_mm(jnp.concatenate([pw, t], axis=0), _pair_diag(pw, lo)) for pw, t in zip(pws, ts)]
            pws = [r[:DN_CHUNK] for r in rs]
            ts = [t + r[DN_CHUNK:] for t, r in zip(ts, rs)]
        rs = [_mm(t, _pair_diag(pw, lo)) for pw, t in zip(pws, ts)]
        ts = [t + r for t, r in zip(ts, rs)]
        sols = [_mm(t, jnp.concatenate([_pair_diag(sl("vb", b, p), lo), _pair_diag(sl("wr", b, p), lo)],
                                       axis=1)) for (b, p), t in zip(probs, ts)]
        qkuws = [_mm(qk, jnp.concatenate([_pair_diag(s[:, :PAIR], lo), _pair_diag(s[:, PAIR:], lo)], axis=1))
                 for qk, s in zip(qks, sols)]
        crosses, gls = [], []
        for (b, p), s in zip(probs, sols):
            gam_last = pre[b]["gam"][r1 - 1:r1, p * PAIR:(p + 1) * PAIR]
            kd = sl("k", b, p) * jnp.exp(gam_last - sl("gam", b, p))
            crosses.append(_mm_tn(kd, s))
            gls.append(jnp.exp(gam_last))
        lhs = [jnp.concatenate([pick(cr[:, PAIR:]), sl("qg", b, p) - qkuw[:, PAIR:]], axis=0)
               for (b, p), cr, qkuw in zip(probs, crosses, qkuws)]
        s_olds = [s_scr[b, p] for b, p in probs]
        rs = [_mm(l, _pair_diag(s_old, lo)) for l, s_old in zip(lhs, s_olds)]
        o_pairs = [[] for _ in range(nb)]
        for (b, p), r, s_old, gl, cr, qkuw in zip(probs, rs, s_olds, gls, crosses, qkuws):
            s_scr[b, p] = gl * s_old - r[:DN_DK] + pick(cr[:, :PAIR])
            o_pairs[b].append(r[DN_DK:] + qkuw[:, :PAIR])
        for b in range(nb):
            o_rows[b].append(jnp.concatenate(o_pairs[b], axis=1))

    o_all = jnp.concatenate([jnp.concatenate(rows, axis=0) for rows in o_rows], axis=0)
    inv_rms = lax.rsqrt(_head_sums(o_all * o_all, hsum) * (1.0 / DN_DV) + EPS)
    for b in range(nb):
        rows = slice(b * GDN_TB, (b + 1) * GDN_TB)
        o_ref[b] = (o_all[rows] * inv_rms[rows] * dnx_ref[...] * _silu(dz_ref[b])).astype(o_ref.dtype)

    @pl.when(i == pl.num_programs(0) - 1)
    def _():
        for b in range(nb):
            for p in range(N_PAIRS):
                s_p = s_scr[b, p]
                s_out_ref[b, 2 * p] = s_p[:, :DN_DV]
                s_out_ref[b, 2 * p + 1] = s_p[:, DN_DV:]


def _gdn_consts():
    lane = np.arange(DN_WIDTH)
    pl_lane = np.arange(PAIR)
    hsum = (pl_lane[:, None] // DN_DV == pl_lane[None, :] // DN_DV)
    src = np.arange(LANES)
    expb = (src[:, None] == lane[None, :] // DN_DV)
    expg = (src[:, None] == DN_HEADS + lane[None, :] // DN_DV)
    tok = np.arange(GDN_TB)
    ltri = np.logical_and(tok[:, None] >= tok[None, :],
                          tok[:, None] // DN_CHUNK == tok[None, :] // DN_CHUNK)
    as_bf16 = lambda m: jnp.asarray(m.astype(np.float32), dtype=BF16)
    return as_bf16(hsum), as_bf16(expb), as_bf16(expg), as_bf16(ltri)


def _gdn_prompt(xc, dz, ba, alog, dtb, dnx, batch, seq):
    nt = seq // GDN_TB
    hsum, expb, expg, ltri = _gdn_consts()
    row = lambda n: pl.BlockSpec((batch, GDN_TB, n), lambda i: (0, i, 0))
    full = lambda a: pl.BlockSpec(a.shape, lambda i: (0,) * a.ndim)
    consts = (alog, dtb, dnx, hsum, expb, expg, ltri)
    as3d = lambda a: a.reshape(batch, seq, a.shape[-1])
    o, s = pl.pallas_call(
        _gdn_prompt_kernel,
        grid=(nt,),
        in_specs=[row(CONV_CH), row(DN_WIDTH), row(LANES)] + [full(a) for a in consts],
        out_specs=[row(DN_WIDTH),
                   pl.BlockSpec((batch, DN_HEADS, DN_DK, DN_DV), lambda i: (0, 0, 0, 0))],
        out_shape=[jax.ShapeDtypeStruct((batch, seq, DN_WIDTH), BF16),
                   jax.ShapeDtypeStruct((batch, DN_HEADS, DN_DK, DN_DV), F32)],
        scratch_shapes=[pltpu.VMEM((batch, N_PAIRS, DN_DK, PAIR), F32)],
        compiler_params=_params("arbitrary"),
        name="gdn_prompt",
    )(as3d(xc), as3d(dz), as3d(ba), *consts)
    return o.reshape(batch * seq, DN_WIDTH), s


GDN_S_BB = 8


def _gdn_sample_kernel(xc_ref, dz_ref, ba_ref, sc_ref, s_ref, cw_ref, alog_ref, dtb_ref, dn_ref,
                       hsum_ref, eye_ref, hsel_ref, hrep3_ref, o_ref, s_out_ref):
    xc = xc_ref[...]
    y = sc_ref[0] * cw_ref[0:1, :]
    y = y + sc_ref[1] * cw_ref[1:2, :]
    y = y + sc_ref[2] * cw_ref[2:3, :]
    y = _silu(y + xc * cw_ref[3:4, :])
    hsum = hsum_ref[...]
    q = y[:, :DN_WIDTH]
    k = y[:, DN_WIDTH:2 * DN_WIDTH]
    v = y[:, 2 * DN_WIDTH:]
    q = q * lax.rsqrt(_mm_sel_rhs(q * q, hsum) + EPS) * (DN_DK ** -0.5)
    k = k * lax.rsqrt(_mm_sel_rhs(k * k, hsum) + EPS)
    beta_c, g_c = _gdn_gates(ba_ref[...], alog_ref[...], dtb_ref[...])
    eg_c = jnp.exp(g_c)
    eye = eye_ref[...]
    tr = lambda a: lax.dot_general(a, eye, (((0,), (0,)), ((), ())), precision=lax.Precision.HIGHEST,
                                   preferred_element_type=F32)
    gates_t = tr(jnp.concatenate([beta_c, eg_c], axis=1))
    beta_t = gates_t[:LANES]
    eg_t = gates_t[LANES:]
    dz = dz_ref[...]
    dn = dn_ref[...]
    split = lambda r: jnp.concatenate([r[:, h * DN_DV:(h + 1) * DN_DV] for h in range(DN_HEADS)], axis=0)
    own_head = hsel_ref[...].astype(F32)
    hrep3 = hrep3_ref[...]
    seqs = range(GDN_S_BB)
    dot = lambda a, b: jnp.dot(a.astype(BF16), b.astype(BF16), preferred_element_type=F32)

    def pieces(x):
        p1 = x.astype(BF16).astype(F32)
        r1 = x - p1
        p2 = r1.astype(BF16).astype(F32)
        return p1, p2, (r1 - p2).astype(BF16).astype(F32)

    heads = DN_HEADS
    k_pieces, kqs = [], []
    for b in seqs:
        kq_bd = jnp.concatenate([own_head * k[b:b + 1, :], own_head * q[b:b + 1, :]], axis=0)
        a1, a2, a3 = pieces(kq_bd)
        s1, s2, s3 = pieces(s_ref[b])
        r1 = dot(jnp.concatenate([a1, a2, a3], axis=0), s1)
        r2 = dot(jnp.concatenate([a1, a2], axis=0), s2)
        r3 = dot(a1, s3)
        n = 2 * heads
        kqs.append(((r3 + r2[n:] + r1[2 * n:]) + (r2[:n] + r1[n:2 * n])) + r1[:n])
        k_pieces.append((a1[:heads], a2[:heads], a3[:heads]))
    egs = [eg_t[DN_HEADS:2 * DN_HEADS, b:b + 1] for b in seqs]
    qks = [jnp.sum(split(q[b:b + 1, :]) * split(k[b:b + 1, :]), axis=-1, keepdims=True) for b in seqs]
    v_news = [beta_t[0:DN_HEADS, b:b + 1] * (split(v[b:b + 1, :]) - eg * kq[:heads])
              for b, eg, kq in zip(seqs, egs, kqs)]
    os_ = [eg * kq[heads:] + qk * v_new for eg, kq, qk, v_new in zip(egs, kqs, qks, v_news)]
    inv_rms = [lax.rsqrt(jnp.mean(o * o, axis=-1, keepdims=True) + EPS) for o in os_]
    for b, o, r in zip(seqs, os_, inv_rms):
        o_ref[b] = o * r * dn * _silu(split(dz[b:b + 1, :]))
    outers, egrows = [], []
    for (k1, k2, k3), v_new, eg in zip(k_pieces, v_news, egs):
        v1, v2, v3 = pieces(v_new)
        lhs = jnp.concatenate([k1, k1, k2, k1, k2, k3], axis=0).astype(BF16)
        rhs = jnp.concatenate([v1, v2, v1, v3, v2, v1], axis=0).astype(BF16)
        outers.append(lax.dot_general(lhs, rhs, (((0,), (0,)), ((), ())), preferred_element_type=F32))
        egrows.append(dot(hrep3, jnp.concatenate(pieces(jnp.broadcast_to(eg, (DN_HEADS, DN_DV))), axis=0)))
    for b, outer, egrow in zip(seqs, outers, egrows):
        s_out_ref[b] = s_ref[b] * egrow + outer


def _gdn_sample(xc, dz, ba, sconv_t, state, conv_w, alog, dtb, dn):
    nseq = xc.shape[0]
    lane = np.arange(DN_WIDTH)
    hsum = jnp.asarray((lane[:, None] // DN_DV == lane[None, :] // DN_DV).astype(np.float32), dtype=BF16)
    eye = jnp.eye(GDN_S_BB, dtype=F32)
    hsel_np = (np.arange(DN_HEADS)[:, None] == lane[None, :] // DN_DK).astype(np.float32)
    hsel = jnp.asarray(hsel_np, dtype=BF16)
    hrep3 = jnp.asarray(np.tile(hsel_np.T, (1, 3)), dtype=BF16)
    row = lambda n: pl.BlockSpec((GDN_S_BB, n), lambda i: (i, 0))
    full = lambda a: pl.BlockSpec(a.shape, lambda i: (0,) * a.ndim)
    st = pl.BlockSpec((GDN_S_BB, DN_HEADS * DN_DK, DN_DV), lambda i: (i, 0, 0))
    consts = (conv_w, alog, dtb, dn, hsum, eye, hsel, hrep3)
    return pl.pallas_call(
        _gdn_sample_kernel,
        grid=(nseq // GDN_S_BB,),
        in_specs=[row(CONV_CH), row(DN_WIDTH), row(LANES),
                  pl.BlockSpec((CONV_WIDTH - 1, GDN_S_BB, CONV_CH), lambda i: (0, i, 0)), st]
                 + [full(a) for a in consts],
        out_specs=[pl.BlockSpec((GDN_S_BB, DN_HEADS, DN_DV), lambda i: (i, 0, 0)), st],
        out_shape=[jax.ShapeDtypeStruct((nseq, DN_HEADS, DN_DV), F32),
                   jax.ShapeDtypeStruct(state.shape, F32)],
        compiler_params=_params("parallel"),
        name="gdn_sample",
    )(xc, dz, ba, sconv_t, state, *consts)


def _attn_sample_lanes_kernel(att_ref, ck_ref, cv_ref, bucket_ref, rb_ref, sink_ref, o_ref, s_scr):
    g = pl.program_id(0)
    nseq = att_ref.shape[0]
    rnd = lambda a: a.astype(BF16).astype(F32)
    att = att_ref[...]
    q_all_t = (att[:, :ATT_WIDTH] * (HEAD_DIM ** -0.5)).T
    kv_new_t = att[:, ATT_WIDTH:].T
    qsel = [jnp.where(g == 0, q_all_t[hh * HEAD_DIM:(hh + 1) * HEAD_DIM],
                      q_all_t[(GQA + hh) * HEAD_DIM:(GQA + hh + 1) * HEAD_DIM]) for hh in range(GQA)]
    qr = [rnd(q) for q in qsel]
    kn = rnd(jnp.where(g == 0, kv_new_t[0:HEAD_DIM], kv_new_t[HEAD_DIM:2 * HEAD_DIM]))
    vn = rnd(jnp.where(g == 0, kv_new_t[2 * HEAD_DIM:3 * HEAD_DIM], kv_new_t[3 * HEAD_DIM:]))

    def score_row(j, carry):
        kj = rnd(ck_ref[j, 0])
        for hh in range(GQA):
            s_scr[hh, pl.ds(j, 1), :] = jnp.sum(qr[hh] * kj, axis=0, keepdims=True)
        return carry
    lax.fori_loop(0, WINDOW, score_row, 0, unroll=2)

    bucket = bucket_ref[...]
    jrow = lax.broadcasted_iota(jnp.int32, (WINDOW, nseq), 0)
    prn = []
    for hh in range(GQA):
        h = g * GQA + hh
        bias = jnp.where(jrow >= 1, _bias_lookup(bucket, rb_ref, h), NEG_INF)
        s = s_scr[hh] + bias
        s_n = jnp.sum(qr[hh] * kn, axis=0, keepdims=True) + rb_ref[0, h]
        sink = sink_ref[h]
        m = jnp.maximum(jnp.maximum(jnp.max(s, axis=0, keepdims=True), s_n), sink)
        p = jnp.exp(s - m)
        p_n = jnp.exp(s_n - m)
        den = jnp.sum(p, axis=0, keepdims=True) + p_n + jnp.exp(sink - m)
        s_scr[hh] = rnd(p / den)
        prn.append(rnd(p_n / den))

    def value_row(j, acc):
        vj = rnd(cv_ref[j, 0])
        return tuple(acc[hh] + s_scr[hh, pl.ds(j, 1), :] * vj for hh in range(GQA))
    zero = jnp.zeros((HEAD_DIM, nseq), F32)
    acc = lax.fori_loop(0, WINDOW, value_row, (zero,) * GQA, unroll=2)
    for hh in range(GQA):
        o_ref[hh * HEAD_DIM:(hh + 1) * HEAD_DIM, :] = acc[hh] + prn[hh] * vn


def _attn_sample_lanes(att, ck_t, cv_t, rel_bias, sink):
    nseq = att.shape[0]
    assert nseq == LANES
    bucket = jnp.asarray(np.broadcast_to(_t5_bucket_np(WINDOW - np.arange(WINDOW))[:, None], (WINDOW, nseq)))
    smem = pl.BlockSpec(memory_space=pltpu.SMEM)
    cache = pl.BlockSpec((WINDOW, 1, HEAD_DIM, nseq), lambda g: (0, g, 0, 0))
    full = lambda a: pl.BlockSpec(a.shape, lambda g: (0,) * a.ndim)
    return pl.pallas_call(
        _attn_sample_lanes_kernel,
        grid=(ATT_KV_HEADS,),
        in_specs=[full(att), cache, cache, full(bucket), smem, smem],
        out_specs=pl.BlockSpec((GQA * HEAD_DIM, nseq), lambda g: (g, 0)),
        out_shape=jax.ShapeDtypeStruct((ATT_WIDTH, nseq), F32),
        scratch_shapes=[pltpu.VMEM((GQA, WINDOW, nseq), F32)],
        compiler_params=_params("arbitrary"),
        name="attn_sample_lanes",
    )(att, ck_t, cv_t, bucket, rel_bias, sink)


def _gdn_sample_front_kernel(xc_ref, dz_ref, ba_ref, sc_ref, cw_ref, alog_ref, dtb_ref, hsum_ref,
                             q_ref, k_ref, v_ref, dz_t_ref, gates_ref):
    xc = xc_ref[...]
    y = sc_ref[0] * cw_ref[0:1, :]
    y = y + sc_ref[1] * cw_ref[1:2, :]
    y = y + sc_ref[2] * cw_ref[2:3, :]
    y = _silu(y + xc * cw_ref[3:4, :])
    hsum = hsum_ref[...]
    q = y[:, :DN_WIDTH]
    k = y[:, DN_WIDTH:2 * DN_WIDTH]
    q = q * lax.rsqrt(_mm_sel_rhs(q * q, hsum) + EPS) * (DN_DK ** -0.5)
    k = k * lax.rsqrt(_mm_sel_rhs(k * k, hsum) + EPS)
    beta_c, g_c = _gdn_gates(ba_ref[...], alog_ref[...], dtb_ref[...])
    q_ref[...] = q.T
    k_ref[...] = k.T
    v_ref[...] = y[:, 2 * DN_WIDTH:].T
    dz_t_ref[...] = dz_ref[...].T
    gates_ref[0:LANES, :] = beta_c.T
    gates_ref[LANES:, :] = jnp.exp(g_c).T


def _gdn_sample_step_kernel(q_ref, k_ref, v_ref, dz_ref, gates_ref, dn_ref, s_ref, o_ref, s_out_ref):
    h = pl.program_id(0)
    beta = gates_ref[pl.ds(h, 1), :]
    eg = gates_ref[pl.ds(LANES + DN_HEADS + h, 1), :]
    q, k, v = q_ref[...], k_ref[...], v_ref[...]
    w = (k * beta) * eg
    qg = q * eg
    ws = jnp.zeros(v.shape, F32)
    qs = jnp.zeros(v.shape, F32)
    for dk in range(DN_DK):
        s_dk = s_ref[0, dk]
        ws = ws + w[dk:dk + 1, :] * s_dk
        qs = qs + qg[dk:dk + 1, :] * s_dk
    v_new = v * beta - ws
    qk = jnp.sum(q * k, axis=0, keepdims=True)
    o = qs + qk * v_new
    for dk in range(DN_DK):
        s_out_ref[0, dk] = s_ref[0, dk] * eg + k[dk:dk + 1, :] * v_new
    o = o * lax.rsqrt(jnp.mean(o * o, axis=0, keepdims=True) + EPS) * dn_ref[...]
    o_ref[...] = o * _silu(dz_ref[...])


def _gdn_sample_lanes(xc, dz, ba, sconv_t, state_t, conv_w, alog, dtb, dn):
    nseq = xc.shape[0]
    assert nseq == LANES
    lane = np.arange(DN_WIDTH)
    hsum = jnp.asarray((lane[:, None] // DN_DV == lane[None, :] // DN_DV).astype(np.float32), dtype=BF16)
    full = lambda a: pl.BlockSpec(a.shape, lambda i: (0,) * a.ndim)
    cm = jax.ShapeDtypeStruct((DN_WIDTH, nseq), F32)
    front_in = (xc, dz, ba, sconv_t, conv_w, alog, dtb, hsum)
    q_t, k_t, v_t, dz_t, gates_t = pl.pallas_call(
        _gdn_sample_front_kernel,
        grid=(1,),
        in_specs=[full(a) for a in front_in],
        out_specs=[pl.BlockSpec((DN_WIDTH, nseq), lambda i: (0, 0))] * 4
                  + [pl.BlockSpec((2 * LANES, nseq), lambda i: (0, 0))],
        out_shape=[cm, cm, cm, cm, jax.ShapeDtypeStruct((2 * LANES, nseq), F32)],
        compiler_params=_params("arbitrary"),
        name="gdn_sample_front",
    )(*front_in)
    dn_b = jnp.broadcast_to(dn.reshape(DN_DV, 1), (DN_DV, nseq))
    head = pl.BlockSpec((DN_DK, nseq), lambda h: (h, 0))
    st = pl.BlockSpec((1, DN_DK, DN_DV, nseq), lambda h: (h, 0, 0, 0))
    return pl.pallas_call(
        _gdn_sample_step_kernel,
        grid=(DN_HEADS,),
        in_specs=[head, head, head, head, full(gates_t), full(dn_b), st],
        out_specs=[head, st],
        out_shape=[cm, jax.ShapeDtypeStruct(state_t.shape, F32)],
        compiler_params=_params("parallel"),
        name="gdn_sample_step",
    )(q_t, k_t, v_t, dz_t, gates_t, dn_b, state_t)


def _route(xn, wr):
    logits = jnp.dot(xn, wr, preferred_element_type=F32)
    lane = lax.broadcasted_iota(jnp.int32, logits.shape, 1).astype(F32)
    first_at = lambda hit: jnp.min(jnp.where(hit, lane, float(LANES)), axis=-1, keepdims=True)
    glog = jnp.where(lane < N_GROUPS, logits, NEG_INF)
    gmax = jnp.max(glog, axis=-1, keepdims=True)
    gsel = first_at(glog == gmax)
    pgsel = 1.0 / jnp.sum(jnp.exp(glog - gmax), axis=-1, keepdims=True)
    lo = ROUTER_OFF + gsel * EXPERTS_PER_GROUP
    in_group = jnp.logical_and(lane >= lo, lane < lo + EXPERTS_PER_GROUP)
    elog = jnp.where(in_group, logits, NEG_INF)
    m1 = jnp.max(elog, axis=-1, keepdims=True)
    i1 = first_at(elog == m1)
    z = jnp.sum(jnp.exp(elog - m1), axis=-1, keepdims=True)
    elog2 = jnp.where(lane == i1, NEG_INF, elog)
    m2 = jnp.max(elog2, axis=-1, keepdims=True)
    i2 = first_at(elog2 == m2)
    p1 = 1.0 / z
    p2 = jnp.exp(m2 - m1) / z
    tot = p1 + p2
    return lane, i1, i2, p1 / tot * pgsel, p2 / tot * pgsel


def _outproj(x_ref, oa_ref, od_ref, wo_ref):
    return x_ref[...] + _mm(oa_ref[...], wo_ref[:ATT_WIDTH, :]) + _mm(od_ref[...], wo_ref[ATT_WIDTH:, :])


def _outproj_router_kernel(x_ref, oa_ref, od_t_ref, wo_ref, g_ref, wr_ref, h_ref, xn_ref, gate_ref):
    h = (x_ref[...] + _mm(oa_ref[...], wo_ref[:ATT_WIDTH, :])
         + _mm(od_t_ref[...].T, wo_ref[ATT_WIDTH:, :]))
    h_ref[...] = h
    xn = _rmsnorm(h, g_ref[...]).astype(BF16)
    xn_ref[...] = xn
    lane, i1, i2, g1, g2 = _route(xn, wr_ref[...])
    gate_ref[...] = jnp.where(lane == i1, g1, 0.0) + jnp.where(lane == i2, g2, 0.0)


def _outproj_router(x, oa, od_t, wo, g, wr):
    t = x.shape[0]
    tm = t
    row = lambda n: pl.BlockSpec((tm, n), lambda i: (i, 0))
    full = lambda a: pl.BlockSpec(a.shape, lambda i: (0,) * a.ndim)
    return pl.pallas_call(
        _outproj_router_kernel,
        grid=(t // tm,),
        in_specs=[row(D_MODEL), row(ATT_WIDTH), full(od_t), full(wo), full(g), full(wr)],
        out_specs=[row(D_MODEL), row(D_MODEL), row(LANES)],
        out_shape=[jax.ShapeDtypeStruct((t, D_MODEL), F32), jax.ShapeDtypeStruct((t, D_MODEL), BF16),
                   jax.ShapeDtypeStruct((t, LANES), F32)],
        compiler_params=_params("parallel"),
        name="outproj_router",
    )(x, oa, od_t, wo, g, wr)


def _moe_kernel(xn_ref, gate_ref, wg_ref, wu_ref, wd_ref, o_ref):
    e = pl.program_id(1)
    xn = xn_ref[...]
    lane = lax.broadcasted_iota(jnp.int32, gate_ref.shape, 1)
    gate = jnp.sum(jnp.where(lane == e + ROUTER_OFF, gate_ref[...], 0.0), axis=-1, keepdims=True)
    hg = jnp.dot(xn, wg_ref[...].astype(BF16), preferred_element_type=F32)
    hu = jnp.dot(xn, wu_ref[...].astype(BF16), preferred_element_type=F32)
    hm = _silu(hg) * hu * gate
    y = jnp.dot(hm.astype(BF16), wd_ref[...].astype(BF16), preferred_element_type=F32)

    @pl.when(e == 0)
    def _():
        o_ref[...] = y

    @pl.when(e > 0)
    def _():
        o_ref[...] += y


def _moe(xn, gates, wg, wu, wd):
    t = xn.shape[0]
    tm = min(t, 1024)
    return pl.pallas_call(
        _moe_kernel,
        grid=(t // tm, N_EXPERTS),
        in_specs=[pl.BlockSpec((tm, D_MODEL), lambda i, e: (i, 0)),
                  pl.BlockSpec((tm, LANES), lambda i, e: (i, 0)),
                  pl.BlockSpec((None, D_MODEL, D_EXPERT), lambda i, e: (e, 0, 0)),
                  pl.BlockSpec((None, D_MODEL, D_EXPERT), lambda i, e: (e, 0, 0)),
                  pl.BlockSpec((None, D_EXPERT, D_MODEL), lambda i, e: (e, 0, 0))],
        out_specs=pl.BlockSpec((tm, D_MODEL), lambda i, e: (i, 0)),
        out_shape=jax.ShapeDtypeStruct((t, D_MODEL), F32),
        compiler_params=_params("parallel", "arbitrary"),
        name="moe",
    )(xn, gates, wg, wu, wd)


MOE_TM = 512
POS_TM = 1024
INFO_G1, INFO_G2, INFO_E1, INFO_E2 = 0, 1, 2, 3
DMA_UNROLL = 8


def _moe_tiles(t):
    return (2 * t) // MOE_TM + N_EXPERTS


HALF = D_MODEL // 2
U32 = jnp.uint32


def _pack_rows(x):
    bits = lambda v: lax.bitcast_convert_type(v.astype(BF16).astype(F32), U32)
    return bits(x[:, HALF:]) | (bits(x[:, :HALF]) >> 16)


def _unpack_rows(w):
    lo = lax.bitcast_convert_type(w << 16, F32)
    hi = lax.bitcast_convert_type(w & jnp.uint32(0xFFFF0000), F32)
    return lo, hi


def _route_kernel(x_ref, oa_ref, od_ref, wo_ref, g_ref, wr_ref, h_ref, xn_ref, info_ref, cnt_ref, run_scr):
    h = _outproj(x_ref, oa_ref, od_ref, wo_ref)
    h_ref[...] = h
    xn = _rmsnorm(h, g_ref[...])
    xn_ref[...] = _pack_rows(xn)
    lane, i1, i2, g1, g2 = _route(xn.astype(BF16), wr_ref[...])
    info = jnp.where(lane == INFO_G1, g1, 0.0) + jnp.where(lane == INFO_G2, g2, 0.0)
    info = info + jnp.where(lane == INFO_E1, i1, 0.0) + jnp.where(lane == INFO_E2, i2, 0.0)
    info_ref[...] = info

    @pl.when(pl.program_id(0) == 0)
    def _():
        run_scr[...] = jnp.zeros(run_scr.shape, F32)
    picked = jnp.logical_or(lane == i1, lane == i2).astype(F32)
    run_scr[...] += jnp.sum(picked, axis=0, keepdims=True)
    cnt_ref[...] = run_scr[...]


def _route_sparse(x, oa, od, wo, g, wr):
    t = x.shape[0]
    tm = ROW_TM
    row = lambda n: pl.BlockSpec((tm, n), lambda i: (i, 0))
    full = lambda a: pl.BlockSpec(a.shape, lambda i: (0,) * a.ndim)
    return pl.pallas_call(
        _route_kernel,
        grid=(t // tm,),
        in_specs=[row(D_MODEL), row(ATT_WIDTH), row(DN_WIDTH), full(wo), full(g), full(wr)],
        out_specs=[row(D_MODEL), row(HALF), row(LANES), pl.BlockSpec((1, LANES), lambda i: (0, 0))],
        out_shape=[jax.ShapeDtypeStruct((t, D_MODEL), F32), jax.ShapeDtypeStruct((t, HALF), U32),
                   jax.ShapeDtypeStruct((t, LANES), F32), jax.ShapeDtypeStruct((1, LANES), F32)],
        scratch_shapes=[pltpu.VMEM((1, LANES), F32)],
        compiler_params=_params("arbitrary"),
        name="route",
    )(x, oa, od, wo, g, wr)


def _positions_kernel(info_ref, cnt_ref, ltri_ref, utri_ref, pos_ref, run_scr, off_scr):
    info = info_ref[...]
    lane = lax.broadcasted_iota(jnp.int32, info.shape, 1).astype(F32)
    hit1 = lane == info[:, INFO_E1:INFO_E1 + 1]
    hit2 = lane == info[:, INFO_E2:INFO_E2 + 1]
    onehot = jnp.logical_or(hit1, hit2).astype(F32)

    @pl.when(pl.program_id(0) == 0)
    def _():
        tiles = jnp.floor((cnt_ref[...] + (MOE_TM - 1)) * (1.0 / MOE_TM))
        off_scr[...] = MOE_TM * jnp.dot(tiles.astype(BF16), utri_ref[...], preferred_element_type=F32)
        run_scr[...] = jnp.zeros(run_scr.shape, F32)

    before = (jnp.dot(ltri_ref[...], onehot.astype(BF16), preferred_element_type=F32)
              + run_scr[...] + off_scr[...])
    pos1 = jnp.sum(jnp.where(hit1, before, 0.0), axis=-1, keepdims=True)
    pos2 = jnp.sum(jnp.where(hit2, before, 0.0), axis=-1, keepdims=True)
    pos_ref[...] = (jnp.where(lane == 0, pos1, 0.0) + jnp.where(lane == 1, pos2, 0.0)).astype(jnp.int32)
    run_scr[...] += jnp.sum(onehot, axis=0, keepdims=True)


def _positions(info, cnt):
    t = info.shape[0]
    tm = min(t, POS_TM)
    tok = np.arange(tm)
    ltri = jnp.asarray((tok[:, None] > tok[None, :]).astype(np.float32), dtype=BF16)
    ln = np.arange(LANES)
    utri = jnp.asarray((ln[:, None] < ln[None, :]).astype(np.float32), dtype=BF16)
    full = lambda a: pl.BlockSpec(a.shape, lambda i: (0,) * a.ndim)
    return pl.pallas_call(
        _positions_kernel,
        grid=(t // tm,),
        in_specs=[pl.BlockSpec((tm, LANES), lambda i: (i, 0)), full(cnt), full(ltri), full(utri)],
        out_specs=pl.BlockSpec((tm, LANES), lambda i: (i, 0)),
        out_shape=jax.ShapeDtypeStruct((t, LANES), jnp.int32),
        scratch_shapes=[pltpu.VMEM((1, LANES), F32), pltpu.VMEM((1, LANES), F32)],
        compiler_params=_params("arbitrary"),
        name="positions",
    )(info, cnt, ltri, utri)


def _row_copy(src_hbm, src_row, dst_hbm, dst_row, sem):
    return pltpu.make_async_copy(src_hbm.at[pl.ds(src_row, 1)], dst_hbm.at[pl.ds(dst_row, 1)], sem)


SCATTER_SLOTS = 3


def _scatter_kernel(pos1_ref, pos2_ref, last_ref, used_ref, nt_ref, xn_hbm, zero_hbm, xs_hbm,
                    buf, lsem, sem, zsem, *, n_tok):
    max_tiles = xs_hbm.shape[0] // MOE_TM

    def zero_tile(tile):
        return pltpu.make_async_copy(zero_hbm, xs_hbm.at[pl.ds(tile * MOE_TM, MOE_TM)], zsem)

    def for_unused(fn):
        def body(tile, carry):
            fn(tile)
            return carry
        lax.fori_loop(nt_ref[0], max_tiles, body, 0)

    for e in range(N_EXPERTS):
        @pl.when(used_ref[e] > 0)
        def _():
            zero_tile(last_ref[e]).start()
    for_unused(lambda tile: zero_tile(tile).start())
    for e in range(N_EXPERTS):
        @pl.when(used_ref[e] > 0)
        def _():
            zero_tile(last_ref[e]).wait()
    for_unused(lambda tile: zero_tile(tile).wait())

    tm = buf.shape[1]
    n = n_tok // tm

    def load(i):
        return pltpu.make_async_copy(xn_hbm.at[pl.ds(i * tm, tm)], buf.at[i % SCATTER_SLOTS],
                                     lsem.at[i % SCATTER_SLOTS])

    def wait_rows(slot):
        pltpu.make_async_copy(xs_hbm.at[pl.ds(0, 2 * tm)], xs_hbm.at[pl.ds(0, 2 * tm)], sem.at[slot]).wait()

    load(0).start()
    load(1).start()

    def step(i, carry):
        slot = i % SCATTER_SLOTS
        load(i).wait()

        def body(j, c2):
            tok = i * tm + j
            src = buf.at[slot, pl.ds(j, 1)]
            pltpu.make_async_copy(src, xs_hbm.at[pl.ds(pos1_ref[tok], 1)], sem.at[slot]).start()
            pltpu.make_async_copy(src, xs_hbm.at[pl.ds(pos2_ref[tok], 1)], sem.at[slot]).start()
            return c2
        lax.fori_loop(0, tm, body, 0, unroll=DMA_UNROLL)

        @pl.when(i >= 1)
        def _():
            wait_rows((i - 1) % SCATTER_SLOTS)

        @pl.when(i + 2 < n)
        def _():
            load(i + 2).start()
        return carry
    lax.fori_loop(0, n, step, 0)
    wait_rows((n - 1) % SCATTER_SLOTS)


def _scatter_rows(xn, pos1, pos2, last_tile, used, n_tiles, n_rows):
    t = xn.shape[0]
    zero = jnp.zeros((MOE_TM, D_MODEL), F32)
    any_spec = pl.BlockSpec(memory_space=pl.ANY)
    return pl.pallas_call(
        functools.partial(_scatter_kernel, n_tok=t),
        grid_spec=pltpu.PrefetchScalarGridSpec(
            num_scalar_prefetch=5, grid=(1,),
            in_specs=[any_spec, any_spec], out_specs=any_spec,
            scratch_shapes=[pltpu.VMEM((SCATTER_SLOTS, MOE_TM, D_MODEL), F32),
                            pltpu.SemaphoreType.DMA((SCATTER_SLOTS,)),
                            pltpu.SemaphoreType.DMA((SCATTER_SLOTS,)),
                            pltpu.SemaphoreType.DMA]),
        out_shape=jax.ShapeDtypeStruct((n_rows, D_MODEL), F32),
        compiler_params=_params("arbitrary"),
        name="scatter_rows",
    )(pos1, pos2, last_tile, used, n_tiles, xn, zero)


def _experts_kernel(te_ref, tv_ref, nt_ref, xs_ref, wg_ref, wu_ref, wd_ref, ys_ref, wg_s, wu_s, wd_s):
    i = pl.program_id(0)
    used = i < nt_ref[0]

    @pl.when(jnp.logical_or(i == 0, te_ref[i] != te_ref[jnp.maximum(i - 1, 0)]))
    def _():
        wg_s[...] = wg_ref[...].astype(BF16)
        wu_s[...] = wu_ref[...].astype(BF16)
        wd_s[...] = wd_ref[...].astype(BF16)

    @pl.when(used)
    def _():
        row = lax.broadcasted_iota(jnp.int32, xs_ref.shape, 0)
        x_lo, x_hi = _unpack_rows(jnp.where(row < tv_ref[i], xs_ref[...], jnp.uint32(0)))
        x_lo = x_lo.astype(BF16)
        x_hi = x_hi.astype(BF16)
        up = lambda w_s: (jnp.dot(x_lo, w_s[:HALF, :], preferred_element_type=F32)
                          + jnp.dot(x_hi, w_s[HALF:, :], preferred_element_type=F32))
        hm = (_silu(up(wg_s)) * up(wu_s)).astype(BF16)
        ys_ref[...] = _pack_rows(jnp.dot(hm, wd_s[...], preferred_element_type=F32))

    @pl.when(jnp.logical_not(used))
    def _():
        ys_ref[...] = jnp.zeros(ys_ref.shape, U32)


def _experts(xs, tile_expert, tile_valid, n_tiles, wg, wu, wd):
    max_tiles = xs.shape[0] // MOE_TM
    rows = pl.BlockSpec((MOE_TM, HALF), lambda i, te, tv, nt: (i, 0))
    wspec = lambda shape: pl.BlockSpec((None,) + shape, lambda i, te, tv, nt: (te[i], 0, 0))
    return pl.pallas_call(
        _experts_kernel,
        grid_spec=pltpu.PrefetchScalarGridSpec(
            num_scalar_prefetch=3, grid=(max_tiles,),
            in_specs=[rows, wspec((D_MODEL, D_EXPERT)), wspec((D_MODEL, D_EXPERT)),
                      wspec((D_EXPERT, D_MODEL))],
            out_specs=rows,
            scratch_shapes=[pltpu.VMEM((D_MODEL, D_EXPERT), BF16), pltpu.VMEM((D_MODEL, D_EXPERT), BF16),
                            pltpu.VMEM((D_EXPERT, D_MODEL), BF16)]),
        out_shape=jax.ShapeDtypeStruct(xs.shape, U32),
        compiler_params=_params("arbitrary"),
        name="experts",
    )(tile_expert, tile_valid, n_tiles, xs, wg, wu, wd)


def _ple_gather_kernel(pos1_ref, pos2_ref, h_ref, info_ref, p_ref, wpp_ref, wpg_ref, gp_ref, gf_ref,
                       ys_hbm, y_ref, ybuf, sem):
    i = pl.program_id(0)
    n = pl.num_programs(0)
    tm = h_ref.shape[0]

    def issue(tile, slot):
        def body(j, carry):
            tok = tile * tm + j
            pltpu.make_async_copy(ys_hbm.at[pl.ds(pos1_ref[tok], 1)], ybuf.at[slot, 0, pl.ds(j, 1)],
                                  sem.at[slot]).start()
            pltpu.make_async_copy(ys_hbm.at[pl.ds(pos2_ref[tok], 1)], ybuf.at[slot, 1, pl.ds(j, 1)],
                                  sem.at[slot]).start()
            return carry
        lax.fori_loop(0, tm, body, 0, unroll=DMA_UNROLL)

    @pl.when(i == 0)
    def _():
        issue(0, 0)

    @pl.when(i + 1 < n)
    def _():
        issue(i + 1, (i + 1) % 2)

    slot = i % 2
    pltpu.make_async_copy(ybuf.at[slot], ybuf.at[slot], sem.at[slot]).wait()
    info = info_ref[...]
    moe = info[:, INFO_G1:INFO_G1 + 1] * ybuf[slot, 0] + info[:, INFO_G2:INFO_G2 + 1] * ybuf[slot, 1]
    h = h_ref[...] + moe
    hn = _rmsnorm(h, gp_ref[...])
    h = h + _mm(p_ref[...], wpp_ref[...]) * _sigmoid(_mm(hn, wpg_ref[...]))
    y_ref[...] = _rmsnorm(h, gf_ref[...])


def _ple_gather(h, info, p, ys, pos1, pos2, wpp, wpg, gp, gf):
    t = h.shape[0]
    tm = 256
    row = lambda n: pl.BlockSpec((tm, n), lambda i, p1, p2: (i, 0))
    full = lambda a: pl.BlockSpec(a.shape, lambda i, p1, p2: (0,) * a.ndim)
    return pl.pallas_call(
        _ple_gather_kernel,
        grid_spec=pltpu.PrefetchScalarGridSpec(
            num_scalar_prefetch=2, grid=(t // tm,),
            in_specs=[row(D_MODEL), row(LANES), row(PLE_DIM), full(wpp), full(wpg), full(gp), full(gf),
                      pl.BlockSpec(memory_space=pl.ANY)],
            out_specs=row(D_MODEL),
            scratch_shapes=[pltpu.VMEM((2, 2, tm, D_MODEL), F32), pltpu.SemaphoreType.DMA((2,))]),
        out_shape=jax.ShapeDtypeStruct((t, D_MODEL), F32),
        compiler_params=_params("arbitrary"),
        name="ple_gather",
    )(pos1, pos2, h, info, p, wpp, wpg, gp, gf, ys)


SC_IDX = 128
SC_ROWS = 64
SC_WORKERS = 32


def _sc_mesh():
    return plsc.VectorSubcoreMesh(core_axis_name="c", subcore_axis_name="s")


def _sc_windows(t, fn):
    per_worker = t // SC_WORKERS
    worker = lax.axis_index(("c", "s"))

    @pl.loop(0, per_worker // SC_IDX)
    def _(w):
        fn(worker * per_worker + w * SC_IDX)


def _sc_scatter_rows(xn, pos1, pos2, n_rows):
    t, d = xn.shape
    assert t % (SC_WORKERS * SC_IDX) == 0
    idx_t = pltpu.VMEM((1, SC_IDX), jnp.int32)

    @pl.kernel(out_type=jax.ShapeDtypeStruct((n_rows, d), xn.dtype), mesh=_sc_mesh(),
               scratch_types=[idx_t, idx_t, pltpu.VMEM((SC_ROWS, d), xn.dtype)])
    def scatter(x_hbm, p1_hbm, p2_hbm, o_hbm, i1_v, i2_v, buf):
        def window(base):
            pltpu.sync_copy(p1_hbm.at[:, pl.ds(base, SC_IDX)], i1_v)
            pltpu.sync_copy(p2_hbm.at[:, pl.ds(base, SC_IDX)], i2_v)
            for k in range(SC_IDX // SC_ROWS):
                pltpu.sync_copy(x_hbm.at[pl.ds(base + k * SC_ROWS, SC_ROWS)], buf)
                pltpu.sync_copy(buf, o_hbm.at[i1_v.at[0, pl.ds(k * SC_ROWS, SC_ROWS)]])
                pltpu.sync_copy(buf, o_hbm.at[i2_v.at[0, pl.ds(k * SC_ROWS, SC_ROWS)]])
        _sc_windows(t, window)

    return scatter(xn, pos1.reshape(1, t), pos2.reshape(1, t))


def _sc_gather_rows(ys, pos1, pos2):
    t = pos1.shape[0]
    d = ys.shape[1]
    assert t % (SC_WORKERS * SC_IDX) == 0
    idx_t = pltpu.VMEM((1, SC_IDX), jnp.int32)
    out = jax.ShapeDtypeStruct((t, d), ys.dtype)

    @pl.kernel(out_type=(out, out), mesh=_sc_mesh(),
               scratch_types=[idx_t, idx_t, pltpu.VMEM((SC_ROWS, d), ys.dtype)])
    def gather(y_hbm, p1_hbm, p2_hbm, o1_hbm, o2_hbm, i1_v, i2_v, buf):
        def window(base):
            pltpu.sync_copy(p1_hbm.at[:, pl.ds(base, SC_IDX)], i1_v)
            pltpu.sync_copy(p2_hbm.at[:, pl.ds(base, SC_IDX)], i2_v)
            for k in range(SC_IDX // SC_ROWS):
                rows = pl.ds(base + k * SC_ROWS, SC_ROWS)
                pltpu.sync_copy(y_hbm.at[i1_v.at[0, pl.ds(k * SC_ROWS, SC_ROWS)]], buf)
                pltpu.sync_copy(buf, o1_hbm.at[rows])
                pltpu.sync_copy(y_hbm.at[i2_v.at[0, pl.ds(k * SC_ROWS, SC_ROWS)]], buf)
                pltpu.sync_copy(buf, o2_hbm.at[rows])
        _sc_windows(t, window)

    return gather(ys, pos1.reshape(1, t), pos2.reshape(1, t))


def _ple_sparse_kernel(h_ref, info_ref, y1_ref, y2_ref, p_ref, wpp_ref, wpg_ref, gp_ref, gf_ref, y_ref):
    info = info_ref[...]
    g1 = info[:, INFO_G1:INFO_G1 + 1]
    g2 = info[:, INFO_G2:INFO_G2 + 1]
    y1_lo, y1_hi = _unpack_rows(y1_ref[...])
    y2_lo, y2_hi = _unpack_rows(y2_ref[...])
    moe = jnp.concatenate([g1 * y1_lo + g2 * y2_lo, g1 * y1_hi + g2 * y2_hi], axis=1)
    h = h_ref[...] + moe
    hn = _rmsnorm(h, gp_ref[...])
    h = h + _mm(p_ref[...], wpp_ref[...]) * _sigmoid(_mm(hn, wpg_ref[...]))
    y_ref[...] = _rmsnorm(h, gf_ref[...])


def _ple_sparse(h, info, y1, y2, p, wpp, wpg, gp, gf):
    t = h.shape[0]
    tm = ROW_TM
    row = lambda n: pl.BlockSpec((tm, n), lambda i: (i, 0))
    full = lambda a: pl.BlockSpec(a.shape, lambda i: (0,) * a.ndim)
    return pl.pallas_call(
        _ple_sparse_kernel,
        grid=(t // tm,),
        in_specs=[row(D_MODEL), row(LANES), row(HALF), row(HALF), row(PLE_DIM),
                  full(wpp), full(wpg), full(gp), full(gf)],
        out_specs=row(D_MODEL),
        out_shape=jax.ShapeDtypeStruct((t, D_MODEL), F32),
        compiler_params=_params("parallel"),
        name="ple_sparse",
    )(h, info, y1, y2, p, wpp, wpg, gp, gf)


def _tile_tables(cnt, max_tiles):
    tiles_e = (cnt + (MOE_TM - 1)) // MOE_TM
    ends = jnp.cumsum(tiles_e)
    n_tiles = ends[-1]
    tile = jnp.arange(max_tiles, dtype=jnp.int32)
    idx = jnp.minimum(tile, n_tiles - 1)
    tile_expert = jnp.sum((idx[:, None] >= ends[None, :]).astype(jnp.int32), axis=1)
    mine = tile_expert[:, None] == jnp.arange(N_EXPERTS, dtype=jnp.int32)[None, :]
    of_mine = lambda v: jnp.sum(jnp.where(mine, v[None, :], 0), axis=1)
    valid = jnp.clip(of_mine(cnt) - (idx - of_mine(ends - tiles_e)) * MOE_TM, 0, MOE_TM)
    tile_valid = jnp.where(tile < n_tiles, valid, 0).astype(jnp.int32)
    return (tile_expert, tile_valid, n_tiles.reshape(1), (ends - 1).astype(jnp.int32),
            tiles_e.astype(jnp.int32))


def _ple_final_kernel(h_ref, m_ref, p_ref, wpp_ref, wpg_ref, gp_ref, gf_ref, y_ref):
    h = h_ref[...] + m_ref[...]
    hn = _rmsnorm(h, gp_ref[...])
    h = h + _mm(p_ref[...], wpp_ref[...]) * _sigmoid(_mm(hn, wpg_ref[...]))
    y_ref[...] = _rmsnorm(h, gf_ref[...])


def _ple_final(h, m, p, wpp, wpg, gp, gf):
    t = h.shape[0]
    tm = min(t, 256)
    row = lambda n: pl.BlockSpec((tm, n), lambda i: (i, 0))
    full = lambda a: pl.BlockSpec(a.shape, lambda i: (0,) * a.ndim)
    return pl.pallas_call(
        _ple_final_kernel,
        grid=(t // tm,),
        in_specs=[row(D_MODEL), row(D_MODEL), row(PLE_DIM), full(wpp), full(wpg), full(gp), full(gf)],
        out_specs=row(D_MODEL),
        out_shape=jax.ShapeDtypeStruct((t, D_MODEL), F32),
        compiler_params=_params("parallel"),
        name="ple_final",
    )(h, m, p, wpp, wpg, gp, gf)


def kernel(x_prompt, x_sample, p_prompt, p_sample, cache_k, cache_v, state_conv, state_S, rel_bias, norm_mix, w_in, att_sink, conv_w, dn_A_log, dn_dt_bias, dn_norm, w_out, norm_ffn, w_router_group, w_router_expert, w_gate, w_up, w_down, w_ple_proj, w_ple_gate, norm_ple, norm_final):
    batch, seq, _ = x_prompt.shape
    nseq = x_sample.shape[0]
    assert x_sample.shape[1] == 1 and norm_mix.shape[0] == 1 and cache_k.shape[2] == WINDOW
    assert seq % GDN_TB == 0 and seq % ATT_BLOCK == 0

    wi = w_in[0]
    o_db = ATT_COLS + CONV_CH
    w_in_re = jnp.concatenate(
        [wi[:, :o_db], wi[:, o_db + 2 * DN_HEADS:], wi[:, o_db:o_db + 2 * DN_HEADS],
         jnp.zeros((D_MODEL, LANES - 2 * DN_HEADS), F32)], axis=1).astype(BF16)
    row = lambda a: a.reshape(1, -1).astype(F32)
    pad_lanes = lambda a, off: jnp.zeros((1, LANES), F32).at[0, off:off + a.shape[0]].set(a)
    alog = pad_lanes(dn_A_log[0], DN_HEADS)
    dtb = pad_lanes(dn_dt_bias[0], DN_HEADS)
    dnx = jnp.tile(dn_norm[0], DN_HEADS).reshape(1, DN_WIDTH)
    w_router = jnp.concatenate(
        [w_router_group[0], w_router_expert[0],
         jnp.zeros((D_MODEL, LANES - N_GROUPS - N_EXPERTS), F32)], axis=1).astype(BF16)
    wo = w_out[0].astype(BF16)
    wg, wu, wd = w_gate[0], w_up[0], w_down[0]
    wpp, wpg = w_ple_proj[0].astype(BF16), w_ple_gate[0].astype(BF16)
    sink = att_sink[0]

    qi = np.arange(ATT_BLOCK)[:, None]
    kj = np.arange(2 * ATT_BLOCK)[None, :]
    bucket_p = jnp.asarray(_t5_bucket_np(qi + ATT_BLOCK - kj))
    bucket_s = jnp.asarray(_t5_bucket_np(WINDOW - np.arange(WINDOW)[None, :]))

    def tail(x, o_att, o_dn, p):
        h1, xn2, gates = _outproj_router(x, o_att, o_dn, wo, row(norm_ffn[0]), w_router)
        moe = _moe(xn2, gates, wg, wu, wd)
        return _ple_final(h1, moe, p, wpp, wpg, row(norm_ple[0]), row(norm_final))

    xp = x_prompt.reshape(batch * seq, D_MODEL)
    att_p, qkv_p, dz_p, ba_p, xc_tails = _inproj_conv(xp, row(norm_mix[0]), w_in_re, conv_w[0], seq)
    o_att_p = _attn_prompt(att_p, bucket_p, rel_bias, sink, batch, seq)
    o_dn_p, s_p = _gdn_prompt(qkv_p, dz_p, ba_p, alog, dtb, dnx, batch, seq)
    h1, xn2, info, cnt = _route_sparse(xp, o_att_p, o_dn_p, wo, row(norm_ffn[0]), w_router)
    pos = _positions(info, cnt)
    pos1, pos2 = pos[:, 0], pos[:, 1]
    max_tiles = _moe_tiles(batch * seq)
    cnt_e = cnt[0, ROUTER_OFF:ROUTER_OFF + N_EXPERTS].astype(jnp.int32)
    tile_expert, tile_valid, n_tiles, last_tile, used = _tile_tables(cnt_e, max_tiles)
    xs_sorted = _sc_scatter_rows(xn2, pos1, pos2, max_tiles * MOE_TM)

    xs = x_sample.reshape(nseq, D_MODEL)
    att_s, xc_s, dz_s, ba_s = _inproj(xs, row(norm_mix[0]), w_in_re)
    ck_t = jnp.transpose(cache_k[0], (0, 2, 3, 1))
    cv_t = jnp.transpose(cache_v[0], (0, 2, 3, 1))
    o_att_s, ks_t, vs_t = _attn_sample(att_s, ck_t, cv_t, bucket_s, rel_bias, sink)
    sconv_t = jnp.swapaxes(state_conv[0], 0, 1)
    o_dn_s_t, s_s_t = _gdn_sample_lanes(xc_s, dz_s, ba_s, sconv_t, jnp.transpose(state_S[0], (1, 2, 3, 0)),
                                        conv_w[0], alog, dtb, dn_norm[0])
    s_s = jnp.transpose(s_s_t, (3, 0, 1, 2))

    ys = _experts(xs_sorted, tile_expert, tile_valid, n_tiles, wg, wu, wd)
    y1, y2 = _sc_gather_rows(ys, pos1, pos2)
    y_s = tail(xs, o_att_s, o_dn_s_t, p_sample[0].reshape(nseq, PLE_DIM))
    y_p = _ple_sparse(h1, info, y1, y2, p_prompt[0].reshape(batch * seq, PLE_DIM),
                      wpp, wpg, row(norm_ple[0]), row(norm_final))

    att_p3 = att_p.reshape(batch, seq, ATT_COLS)
    kv_shape = (1, batch, WINDOW, ATT_KV_HEADS, HEAD_DIM)
    k_p = att_p3[:, seq - WINDOW:, ATT_WIDTH:ATT_WIDTH + KV_WIDTH].reshape(kv_shape)
    v_p = att_p3[:, seq - WINDOW:, ATT_WIDTH + KV_WIDTH:].reshape(kv_shape)
    conv_p = xc_tails.reshape(batch, -1, TAIL, CONV_CH)[:, -1, TAIL - (CONV_WIDTH - 1):][None]
    k_s = jnp.transpose(ks_t, (0, 3, 1, 2))[None]
    v_s = jnp.transpose(vs_t, (0, 3, 1, 2))[None]
    conv_s = jnp.concatenate([state_conv[0][:, 1:], xc_s[:, None, :]], axis=1)[None]
    return (y_p.reshape(batch, seq, D_MODEL), y_s.reshape(nseq, 1, D_MODEL),
            k_p, v_p, conv_p, s_p[None], k_s, v_s, conv_s, s_s[None])
```

```python
import functools
import math

import numpy as np
import jax
import jax.numpy as jnp
from jax import lax
from jax.experimental import pallas as pl
from jax.experimental.pallas import tpu as pltpu
from jax.experimental.pallas import tpu_sc as plsc

F32 = jnp.float32
BF16 = jnp.bfloat16

D_MODEL = 1024
ATT_HEADS = 8
ATT_KV_HEADS = 2
HEAD_DIM = 64
GQA = ATT_HEADS // ATT_KV_HEADS
WINDOW = 128
ATT_BLOCK = 128
N_BUCKETS = 32
DN_HEADS = 8
DN_DK = 64
DN_DV = 64
CONV_WIDTH = 4
DN_CHUNK = 64
ATT_WIDTH = ATT_HEADS * HEAD_DIM
KV_WIDTH = ATT_KV_HEADS * HEAD_DIM
DN_WIDTH = DN_HEADS * DN_DV
CONV_CH = 3 * DN_WIDTH
N_GROUPS = 4
EXPERTS_PER_GROUP = 8
N_EXPERTS = N_GROUPS * EXPERTS_PER_GROUP
D_EXPERT = 256
PLE_DIM = 256
EPS = 1e-6
NEG_INF = float("-inf")

ATT_COLS = ATT_WIDTH + 2 * KV_WIDTH
LANES = 128
IN_COLS = ATT_COLS + CONV_CH + DN_WIDTH + LANES
ROUTER_OFF = N_GROUPS
VMEM_LIMIT = 48 * 1024 * 1024
ROW_TM = 512


def _params(*sem):
    return pltpu.CompilerParams(dimension_semantics=sem, vmem_limit_bytes=VMEM_LIMIT)


def _mm(a, b):
    return jnp.dot(a.astype(BF16), b.astype(BF16), preferred_element_type=F32)


def _mm_nt(a, b):
    return lax.dot_general(a.astype(BF16), b.astype(BF16), (((1,), (1,)), ((), ())),
                           preferred_element_type=F32)


def _mm_tn(a, b):
    return lax.dot_general(a.astype(BF16), b.astype(BF16), (((0,), (0,)), ((), ())),
                           preferred_element_type=F32)


def _split3(x):
    h1 = x.astype(BF16)
    r1 = x - h1.astype(F32)
    h2 = r1.astype(BF16)
    h3 = (r1 - h2.astype(F32)).astype(BF16)
    return h1, h2, h3


def _mm_sel_rhs(x, sel):
    h1, h2, h3 = _split3(x)
    d = lambda h: jnp.dot(h, sel, preferred_element_type=F32)
    return d(h1) + d(h2) + d(h3)


def _mm_sel_lhs(sel, x):
    h1, h2, h3 = _split3(x)
    d = lambda h: jnp.dot(sel, h, preferred_element_type=F32)
    return d(h1) + d(h2) + d(h3)


def _mm3(a, b):
    ah = a.astype(BF16)
    al = (a - ah.astype(F32)).astype(BF16)
    bh = b.astype(BF16)
    bl = (b - bh.astype(F32)).astype(BF16)
    d = lambda u, v: jnp.dot(u, v, preferred_element_type=F32)
    return d(ah, bh) + d(ah, bl) + d(al, bh)


def _sigmoid(x):
    return 1.0 / (1.0 + jnp.exp(-x))


def _silu(x):
    return x * _sigmoid(x)


def _softplus(x):
    return jnp.maximum(x, 0.0) + jnp.log1p(jnp.exp(-jnp.abs(x)))


def _rmsnorm(x, g):
    return x * lax.rsqrt(jnp.mean(x * x, axis=-1, keepdims=True) + EPS) * g


def _t5_bucket_np(dist):
    max_exact = N_BUCKETS // 2
    d = np.maximum(dist, 0)
    ratio = (np.log(np.maximum(d, 1).astype(np.float32) / np.float32(max_exact))
             / np.float32(math.log(WINDOW / max_exact))).astype(np.float32)
    large = np.minimum(max_exact + (ratio * np.float32(N_BUCKETS - max_exact)).astype(np.int32),
                       N_BUCKETS - 1)
    return np.where(d < max_exact, d, large).astype(np.int32)


def _bias_lookup(bucket, rb_ref, h):
    acc = jnp.zeros(bucket.shape, F32)
    for t in range(N_BUCKETS):
        acc = jnp.where(bucket == t, rb_ref[t, h], acc)
    return acc


def _inproj_kernel(x_ref, g_ref, w_ref, att_ref, xc_ref, dz_ref, ba_ref):
    xn = _rmsnorm(x_ref[...], g_ref[...]).astype(BF16)
    o0, o1, o2 = ATT_COLS, ATT_COLS + CONV_CH, ATT_COLS + CONV_CH + DN_WIDTH
    att_ref[...] = jnp.dot(xn, w_ref[:, :o0], preferred_element_type=F32)
    xc_ref[...] = jnp.dot(xn, w_ref[:, o0:o1], preferred_element_type=F32)
    dz_ref[...] = jnp.dot(xn, w_ref[:, o1:o2], preferred_element_type=F32)
    ba_ref[...] = jnp.dot(xn, w_ref[:, o2:], preferred_element_type=F32)


def _inproj(x, g, w):
    t = x.shape[0]
    tm = min(t, ROW_TM)
    row = lambda n: pl.BlockSpec((tm, n), lambda i: (i, 0))
    full = lambda a: pl.BlockSpec(a.shape, lambda i: (0,) * a.ndim)
    return pl.pallas_call(
        _inproj_kernel,
        grid=(t // tm,),
        in_specs=[row(D_MODEL), full(g), full(w)],
        out_specs=[row(ATT_COLS), row(CONV_CH), row(DN_WIDTH), row(LANES)],
        out_shape=[jax.ShapeDtypeStruct((t, n), F32) for n in (ATT_COLS, CONV_CH, DN_WIDTH, LANES)],
        compiler_params=_params("parallel"),
        name="inproj",
    )(x, g, w)


TAIL = 8
PAIR = 2 * DN_DK
N_PAIRS = DN_WIDTH // PAIR


def _head_sums(z, pair_ones):
    hi = z.astype(BF16)
    lw = (z - hi.astype(F32)).astype(BF16)
    d = lambda a, p: jnp.dot(a[:, p * PAIR:(p + 1) * PAIR], pair_ones, preferred_element_type=F32)
    return jnp.concatenate([d(hi, p) + d(lw, p) for p in range(N_PAIRS)], axis=1)


def _inproj_conv_kernel(x_ref, g_ref, w_ref, cw_ref, ones_ref, att_ref, qkv_ref, dz_ref, ba_ref, tail_ref,
                        xp_scr, *, tiles_per_seq):
    tm = x_ref.shape[0]

    @pl.when(pl.program_id(0) % tiles_per_seq == 0)
    def _():
        xp_scr[0:TAIL, :] = jnp.zeros((TAIL, CONV_CH), F32)

    xn = _rmsnorm(x_ref[...], g_ref[...]).astype(BF16)
    o0, o1, o2 = ATT_COLS, ATT_COLS + CONV_CH, ATT_COLS + CONV_CH + DN_WIDTH
    xc = jnp.dot(xn, w_ref[:, o0:o1], preferred_element_type=F32)
    att_ref[...] = jnp.dot(xn, w_ref[:, :o0], preferred_element_type=F32)
    dz_ref[...] = jnp.dot(xn, w_ref[:, o1:o2], preferred_element_type=F32)
    ba_ref[...] = jnp.dot(xn, w_ref[:, o2:], preferred_element_type=F32)

    xp_scr[TAIL:, :] = xc
    y = xp_scr[TAIL - 3:TAIL - 3 + tm, :] * cw_ref[0:1, :]
    y = y + xp_scr[TAIL - 2:TAIL - 2 + tm, :] * cw_ref[1:2, :]
    y = y + xp_scr[TAIL - 1:TAIL - 1 + tm, :] * cw_ref[2:3, :]
    y = y + xc * cw_ref[3:4, :]
    tail = xc[tm - TAIL:, :]
    xp_scr[0:TAIL, :] = tail
    tail_ref[0] = tail
    y = _silu(y)
    q = y[:, :DN_WIDTH]
    k = y[:, DN_WIDTH:2 * DN_WIDTH]
    inv_norm = lax.rsqrt(_head_sums(jnp.concatenate([q * q, k * k], axis=0), ones_ref[...]) + EPS)
    qkv_ref[:, :DN_WIDTH] = q * inv_norm[:tm] * (DN_DK ** -0.5)
    qkv_ref[:, DN_WIDTH:2 * DN_WIDTH] = k * inv_norm[tm:]
    qkv_ref[:, 2 * DN_WIDTH:] = y[:, 2 * DN_WIDTH:]


def _pair_ones():
    lane = np.arange(PAIR)
    return jnp.asarray((lane[:, None] // DN_DV == lane[None, :] // DN_DV).astype(np.float32), dtype=BF16)


def _inproj_conv(x, g, w, conv_w, seq):
    t = x.shape[0]
    tm = ROW_TM
    assert seq % tm == 0
    ones = _pair_ones()
    row = lambda n: pl.BlockSpec((tm, n), lambda i: (i, 0))
    full = lambda a: pl.BlockSpec(a.shape, lambda i: (0,) * a.ndim)
    return pl.pallas_call(
        functools.partial(_inproj_conv_kernel, tiles_per_seq=seq // tm),
        grid=(t // tm,),
        in_specs=[row(D_MODEL), full(g), full(w), full(conv_w), full(ones)],
        out_specs=[row(ATT_COLS), row(CONV_CH), row(DN_WIDTH), row(LANES),
                   pl.BlockSpec((1, TAIL, CONV_CH), lambda i: (i, 0, 0))],
        out_shape=[jax.ShapeDtypeStruct((t, n), F32) for n in (ATT_COLS, CONV_CH, DN_WIDTH, LANES)]
                  + [jax.ShapeDtypeStruct((t // tm, TAIL, CONV_CH), F32)],
        scratch_shapes=[pltpu.VMEM((TAIL + tm, CONV_CH), F32)],
        compiler_params=_params("arbitrary"),
        name="inproj_conv",
    )(x, g, w, conv_w, ones)


GROUP_ROWS = GQA * ATT_BLOCK


def _attn_prompt_kernel(cur_ref, prev_ref, bucket_ref, rb_ref, sink_ref, o_ref, bias_scr, sink_scr):
    i = pl.program_id(0)
    nseq = cur_ref.shape[0]

    @pl.when(i == 0)
    def _():
        qi = lax.broadcasted_iota(jnp.int32, (ATT_BLOCK, 2 * ATT_BLOCK), 0)
        kj = lax.broadcasted_iota(jnp.int32, (ATT_BLOCK, 2 * ATT_BLOCK), 1)
        dist = qi + ATT_BLOCK - kj
        band = jnp.logical_and(dist >= 0, dist < WINDOW)
        bucket = bucket_ref[...]
        hrow = lax.broadcasted_iota(jnp.int32, (GROUP_ROWS, 1), 0) // ATT_BLOCK
        for g in range(ATT_KV_HEADS):
            sink_col = jnp.zeros((GROUP_ROWS, 1), F32)
            for hh in range(GQA):
                h = g * GQA + hh
                bias = jnp.where(band, _bias_lookup(bucket, rb_ref, h), NEG_INF)
                bias_scr[0, g, hh * ATT_BLOCK:(hh + 1) * ATT_BLOCK, :] = bias
                bias_scr[1, g, hh * ATT_BLOCK:(hh + 1) * ATT_BLOCK, :] = jnp.where(kj >= ATT_BLOCK, bias, NEG_INF)
                sink_col = jnp.where(hrow == hh, sink_ref[h], sink_col)
            sink_scr[g] = sink_col

    first = (i == 0).astype(jnp.int32)
    probs = [(b, g) for b in range(nseq) for g in range(ATT_KV_HEADS)]
    scores = []
    for b, g in probs:
        cur = cur_ref[b]
        prev = prev_ref[b]
        q = jnp.concatenate([cur[:, (g * GQA + hh) * HEAD_DIM:(g * GQA + hh + 1) * HEAD_DIM]
                             for hh in range(GQA)], axis=0) * (HEAD_DIM ** -0.5)
        kcol = slice(ATT_WIDTH + g * HEAD_DIM, ATT_WIDTH + (g + 1) * HEAD_DIM)
        k2 = jnp.concatenate([prev[:, kcol], cur[:, kcol]], axis=0)
        scores.append(_mm_nt(q, k2) + bias_scr[first, g])
    probs_p, dens = [], []
    for (b, g), s in zip(probs, scores):
        sink = sink_scr[g]
        m = jnp.maximum(jnp.max(s, axis=-1, keepdims=True), sink)
        p = jnp.exp(s - m)
        dens.append(jnp.sum(p, axis=-1, keepdims=True) + jnp.exp(sink - m))
        probs_p.append(p)
    outs = {}
    for (b, g), p, den in zip(probs, probs_p, dens):
        vcol = slice(ATT_WIDTH + KV_WIDTH + g * HEAD_DIM, ATT_WIDTH + KV_WIDTH + (g + 1) * HEAD_DIM)
        v2 = jnp.concatenate([prev_ref[b][:, vcol], cur_ref[b][:, vcol]], axis=0)
        outs[b, g] = _mm(p, v2) / den
    for b in range(nseq):
        o_ref[b] = jnp.concatenate([outs[b, g][hh * ATT_BLOCK:(hh + 1) * ATT_BLOCK, :]
                                    for g in range(ATT_KV_HEADS) for hh in range(GQA)],
                                   axis=1).astype(o_ref.dtype)


def _attn_prompt(att, bucket, rel_bias, sink, batch, seq):
    nb = seq // ATT_BLOCK
    smem = pl.BlockSpec(memory_space=pltpu.SMEM)
    att3 = att.reshape(batch, seq, ATT_COLS)
    out = pl.pallas_call(
        _attn_prompt_kernel,
        grid=(nb,),
        in_specs=[
            pl.BlockSpec((batch, ATT_BLOCK, ATT_COLS), lambda i: (0, i, 0)),
            pl.BlockSpec((batch, ATT_BLOCK, ATT_COLS), lambda i: (0, jnp.maximum(i - 1, 0), 0)),
            pl.BlockSpec(bucket.shape, lambda i: (0, 0)),
            smem, smem,
        ],
        out_specs=pl.BlockSpec((batch, ATT_BLOCK, ATT_WIDTH), lambda i: (0, i, 0)),
        out_shape=jax.ShapeDtypeStruct((batch, seq, ATT_WIDTH), BF16),
        scratch_shapes=[pltpu.VMEM((2, ATT_KV_HEADS, GROUP_ROWS, 2 * ATT_BLOCK), F32),
                        pltpu.VMEM((ATT_KV_HEADS, GROUP_ROWS, 1), F32)],
        compiler_params=_params("arbitrary"),
        name="attn_prompt",
    )(att3, att3, bucket, rel_bias, sink)
    return out.reshape(batch * seq, ATT_WIDTH)


ATT_S_BB = 8


def _attn_sample_kernel(att_ref, ck_ref, cv_ref, bucket_ref, rb_ref, sink_ref, o_ref, ks_ref, vs_ref,
                        bias_scr, col_scr):
    hrow = lax.broadcasted_iota(jnp.int32, (ATT_HEADS, LANES), 0)
    lane = lax.broadcasted_iota(jnp.int32, (ATT_HEADS, LANES), 1)

    last = (lax.broadcasted_iota(jnp.int32, (3, WINDOW), 1) == WINDOW - 1).astype(BF16)
    is_last = lax.broadcasted_iota(jnp.int32, (KV_WIDTH, WINDOW), 1) == WINDOW - 1

    def shifted(cache_t, new_row):
        pieces = jnp.concatenate([p.astype(F32) for p in _split3(new_row)], axis=0).astype(BF16)
        col = lax.dot_general(pieces, last, (((0,), (0,)), ((), ())), preferred_element_type=F32)
        out = jnp.where(is_last, col, pltpu.roll(cache_t, WINDOW - 1, axis=1))
        return out.reshape(ATT_KV_HEADS, HEAD_DIM, WINDOW)

    for b in range(ATT_S_BB):
        row = att_ref[b:b + 1, :]
        ks_ref[b] = shifted(ck_ref[b].reshape(KV_WIDTH, WINDOW), row[:, ATT_WIDTH:ATT_WIDTH + KV_WIDTH])
        vs_ref[b] = shifted(cv_ref[b].reshape(KV_WIDTH, WINDOW), row[:, ATT_WIDTH + KV_WIDTH:])

    @pl.when(pl.program_id(0) == 0)
    def _():
        bucket = jnp.broadcast_to(bucket_ref[...], (ATT_HEADS, LANES))
        bias = jnp.zeros((ATT_HEADS, LANES), F32)
        cols = jnp.zeros((ATT_HEADS, LANES), F32)
        for h in range(ATT_HEADS):
            bias = jnp.where(hrow == h, _bias_lookup(bucket, rb_ref, h), bias)
            cols = jnp.where(jnp.logical_and(hrow == h, lane == 0), sink_ref[h], cols)
            cols = jnp.where(jnp.logical_and(hrow == h, lane == 1), rb_ref[0, h], cols)
        bias_scr[...] = jnp.where(lane >= 1, bias, NEG_INF)
        col_scr[...] = cols

    bias_c = bias_scr[...]
    sink = col_scr[:, 0:1]
    bias_n = col_scr[:, 1:2]
    same_group = (hrow // GQA) == (lane // HEAD_DIM)
    low_group = lax.broadcasted_iota(jnp.int32, (ATT_HEADS, HEAD_DIM), 0) < GQA
    rnd = lambda a: a.astype(BF16).astype(F32)
    seqs = range(ATT_S_BB)
    rows = [att_ref[b:b + 1, :] for b in seqs]
    q_bds = []
    for row in rows:
        q = row[:, :ATT_WIDTH] * (HEAD_DIM ** -0.5)
        qh = jnp.concatenate([q[:, h * HEAD_DIM:(h + 1) * HEAD_DIM] for h in range(ATT_HEADS)], axis=0)
        q_bds.append(jnp.where(same_group, jnp.concatenate([qh, qh], axis=1), 0.0))
    kv_t = lambda ref, b: ref[b].reshape(KV_WIDTH, WINDOW)
    s_cs = [_mm(q_bd, kv_t(ck_ref, b)) + bias_c for b, q_bd in zip(seqs, q_bds)]
    prs, pns = [], []
    for row, q_bd, s_c in zip(rows, q_bds, s_cs):
        kn = row[:, ATT_WIDTH:ATT_WIDTH + KV_WIDTH]
        s_n = jnp.sum(rnd(q_bd) * rnd(kn), axis=-1, keepdims=True) + bias_n
        m = jnp.maximum(jnp.maximum(jnp.max(s_c, axis=-1, keepdims=True), s_n), sink)
        p_c = jnp.exp(s_c - m)
        p_n = jnp.exp(s_n - m)
        den = jnp.sum(p_c, axis=-1, keepdims=True) + p_n + jnp.exp(sink - m)
        prs.append(p_c / den)
        pns.append(p_n / den)
    pvs = [_mm_nt(pr, kv_t(cv_ref, b)) for b, pr in zip(seqs, prs)]
    for b, row, pv, pn in zip(seqs, rows, pvs, pns):
        vn = row[:, ATT_WIDTH + KV_WIDTH:]
        o_full = pv + rnd(pn) * rnd(vn)
        o_sel = jnp.where(low_group, o_full[:, :HEAD_DIM], o_full[:, HEAD_DIM:])
        o_ref[b:b + 1, :] = jnp.concatenate([o_sel[h:h + 1, :] for h in range(ATT_HEADS)], axis=1)


def _attn_sample(att, ck, cv, bucket, rel_bias, sink):
    nseq = att.shape[0]
    smem = pl.BlockSpec(memory_space=pltpu.SMEM)
    cache = pl.BlockSpec((ATT_S_BB, ATT_KV_HEADS, HEAD_DIM, WINDOW), lambda i: (i, 0, 0, 0))
    return pl.pallas_call(
        _attn_sample_kernel,
        grid=(nseq // ATT_S_BB,),
        in_specs=[pl.BlockSpec((ATT_S_BB, ATT_COLS), lambda i: (i, 0)), cache, cache,
                  pl.BlockSpec(bucket.shape, lambda i: (0, 0)), smem, smem],
        out_specs=[pl.BlockSpec((ATT_S_BB, ATT_WIDTH), lambda i: (i, 0)), cache, cache],
        out_shape=[jax.ShapeDtypeStruct((nseq, ATT_WIDTH), F32),
                   jax.ShapeDtypeStruct(ck.shape, F32), jax.ShapeDtypeStruct(cv.shape, F32)],
        scratch_shapes=[pltpu.VMEM((ATT_HEADS, LANES), F32), pltpu.VMEM((ATT_HEADS, LANES), F32)],
        compiler_params=_params("arbitrary"),
        name="attn_sample",
    )(att, ck, cv, bucket, rel_bias, sink)


GDN_TB = 128
GDN_NC = GDN_TB // DN_CHUNK


def _gdn_gates(ba, alog, dtb):
    beta = _sigmoid(ba)
    g = -jnp.exp(alog) * _softplus(ba + dtb)
    return beta, g


def _pair_diag(x, lo):
    xb = x.astype(BF16)
    zero = jnp.zeros_like(xb)
    return jnp.concatenate([jnp.where(lo, xb, zero), jnp.where(lo, zero, xb)], axis=0)


def _gdn_prompt_kernel(qkv_ref, dz_ref, ba_ref, alog_ref, dtb_ref, dnx_ref,
                       hsum_ref, expb_ref, expg_ref, ltri_ref,
                       o_ref, s_out_ref, s_scr):
    i = pl.program_id(0)
    nb = qkv_ref.shape[0]

    @pl.when(i == 0)
    def _():
        s_scr[...] = jnp.zeros(s_scr.shape, F32)

    hsum = hsum_ref[...]
    ri = lax.broadcasted_iota(jnp.int32, (DN_CHUNK, PAIR), 0)
    ci = lax.broadcasted_iota(jnp.int32, (DN_CHUNK, PAIR), 1)
    lo = ci < DN_DK
    cj = jnp.where(lo, ci, ci - DN_DK)
    causal = ri >= cj
    strict = ri > cj
    eye = (ri == cj).astype(F32)

    def sel2(x, m):
        hi = x.astype(BF16)
        lw = (x - hi.astype(F32)).astype(BF16)
        return (jnp.dot(hi, m, preferred_element_type=F32) + jnp.dot(lw, m, preferred_element_type=F32))

    pre = []
    for b in range(nb):
        q = qkv_ref[b, :, :DN_WIDTH]
        k = qkv_ref[b, :, DN_WIDTH:2 * DN_WIDTH]
        v = qkv_ref[b, :, 2 * DN_WIDTH:]
        beta_c, g_c = _gdn_gates(ba_ref[b], alog_ref[...], dtb_ref[...])
        beta = sel2(beta_c, expb_ref[...])
        gam_c = _mm_sel_lhs(ltri_ref[...], g_c)
        gam = _mm_sel_rhs(gam_c, expg_ref[...])
        gam_t = gam_c.T
        kb = k * beta
        egam = jnp.exp(gam)
        pre.append(dict(q=q, k=k, kb=kb, vb=v * beta, qg=q * egam, wr=kb * egam, gam=gam, gam_t=gam_t))

    probs = [(b, p) for b in range(nb) for p in range(N_PAIRS)]
    pick = lambda m: jnp.where(lo, m[:DN_DK], m[DN_DK:])
    o_rows = [[] for _ in range(nb)]
    for c in range(GDN_NC):
        r0, r1 = c * DN_CHUNK, (c + 1) * DN_CHUNK
        sl = lambda name, b, p: pre[b][name][r0:r1, p * PAIR:(p + 1) * PAIR]
        raws = []
        for b, p in probs:
            k_p = sl("k", b, p)
            k_rows = jnp.concatenate([jnp.where(lo, k_p, 0.0), jnp.where(lo, 0.0, k_p)], axis=0)
            raws.append(_mm_nt(jnp.concatenate([sl("kb", b, p), sl("q", b, p)], axis=0), k_rows))
        pws, ts, qks = [], [], []
        for (b, p), raw in zip(probs, raws):
            gcol = sl("gam", b, p)
            h0 = DN_HEADS + 2 * p
            gam_t = pre[b]["gam_t"]
            grow = jnp.concatenate([gam_t[h0:h0 + 1, r0:r1], gam_t[h0 + 1:h0 + 2, r0:r1]], axis=1)
            decay = jnp.exp(jnp.where(causal, gcol - grow, NEG_INF))
            a = jnp.where(strict, raw[:DN_CHUNK] * decay, 0.0)
            qks.append(jnp.where(causal, raw[DN_CHUNK:] * decay, 0.0))
            pws.append(-a)
            ts.append(eye - a)
        pws = [_mm(pw, _pair_diag(pw, lo)) for pw in pws]
        for _ in range(4):
            rs = [_mm(jnp.concatenate([pw, t], axis=0), _pair_diag(pw, lo)) for pw, t in zip(pws, ts)]
            pws = [r[:DN_CHUNK] for r in rs]
            ts = [t + r[DN_CHUNK:] for t, r in zip(ts, rs)]
        rs = [_mm(t, _pair_diag(pw, lo)) for pw, t in zip(pws, ts)]
        ts = [t + r for t, r in zip(ts, rs)]
        sols = [_mm(t, jnp.concatenate([_pair_diag(sl("vb", b, p), lo), _pair_diag(sl("wr", b, p), lo)],
                                       axis=1)) for (b, p), t in zip(probs, ts)]
        qkuws = [_mm(qk, jnp.concatenate([_pair_diag(s[:, :PAIR], lo), _pair_diag(s[:, PAIR:], lo)], axis=1))
                 for qk, s in zip(qks, sols)]
        crosses, gls = [], []
        for (b, p), s in zip(probs, sols):
            gam_last = pre[b]["gam"][r1 - 1:r1, p * PAIR:(p + 1) * PAIR]
            kd = sl("k", b, p) * jnp.exp(gam_last - sl("gam", b, p))
            crosses.append(_mm_tn(kd, s))
            gls.append(jnp.exp(gam_last))
        lhs = [jnp.concatenate([pick(cr[:, PAIR:]), sl("qg", b, p) - qkuw[:, PAIR:]], axis=0)
               for (b, p), cr, qkuw in zip(probs, crosses, qkuws)]
        s_olds = [s_scr[b, p] for b, p in probs]
        rs = [_mm(l, _pair_diag(s_old, lo)) for l, s_old in zip(lhs, s_olds)]
        o_pairs = [[] for _ in range(nb)]
        for (b, p), r, s_old, gl, cr, qkuw in zip(probs, rs, s_olds, gls, crosses, qkuws):
            s_scr[b, p] = gl * s_old - r[:DN_DK] + pick(cr[:, :PAIR])
            o_pairs[b].append(r[DN_DK:] + qkuw[:, :PAIR])
        for b in range(nb):
            o_rows[b].append(jnp.concatenate(o_pairs[b], axis=1))

    o_all = jnp.concatenate([jnp.concatenate(rows, axis=0) for rows in o_rows], axis=0)
    inv_rms = lax.rsqrt(_head_sums(o_all * o_all, hsum) * (1.0 / DN_DV) + EPS)
    for b in range(nb):
        rows = slice(b * GDN_TB, (b + 1) * GDN_TB)
        o_ref[b] = (o_all[rows] * inv_rms[rows] * dnx_ref[...] * _silu(dz_ref[b])).astype(o_ref.dtype)

    @pl.when(i == pl.num_programs(0) - 1)
    def _():
        for b in range(nb):
            for p in range(N_PAIRS):
                s_p = s_scr[b, p]
                s_out_ref[b, 2 * p] = s_p[:, :DN_DV]
                s_out_ref[b, 2 * p + 1] = s_p[:, DN_DV:]


def _gdn_consts():
    lane = np.arange(DN_WIDTH)
    pl_lane = np.arange(PAIR)
    hsum = (pl_lane[:, None] // DN_DV == pl_lane[None, :] // DN_DV)
    src = np.arange(LANES)
    expb = (src[:, None] == lane[None, :] // DN_DV)
    expg = (src[:, None] == DN_HEADS + lane[None, :] // DN_DV)
    tok = np.arange(GDN_TB)
    ltri = np.logical_and(tok[:, None] >= tok[None, :],
                          tok[:, None] // DN_CHUNK == tok[None, :] // DN_CHUNK)
    as_bf16 = lambda m: jnp.asarray(m.astype(np.float32), dtype=BF16)
    return as_bf16(hsum), as_bf16(expb), as_bf16(expg), as_bf16(ltri)


def _gdn_prompt(xc, dz, ba, alog, dtb, dnx, batch, seq):
    nt = seq // GDN_TB
    hsum, expb, expg, ltri = _gdn_consts()
    row = lambda n: pl.BlockSpec((batch, GDN_TB, n), lambda i: (0, i, 0))
    full = lambda a: pl.BlockSpec(a.shape, lambda i: (0,) * a.ndim)
    consts = (alog, dtb, dnx, hsum, expb, expg, ltri)
    as3d = lambda a: a.reshape(batch, seq, a.shape[-1])
    o, s = pl.pallas_call(
        _gdn_prompt_kernel,
        grid=(nt,),
        in_specs=[row(CONV_CH), row(DN_WIDTH), row(LANES)] + [full(a) for a in consts],
        out_specs=[row(DN_WIDTH),
                   pl.BlockSpec((batch, DN_HEADS, DN_DK, DN_DV), lambda i: (0, 0, 0, 0))],
        out_shape=[jax.ShapeDtypeStruct((batch, seq, DN_WIDTH), BF16),
                   jax.ShapeDtypeStruct((batch, DN_HEADS, DN_DK, DN_DV), F32)],
        scratch_shapes=[pltpu.VMEM((batch, N_PAIRS, DN_DK, PAIR), F32)],
        compiler_params=_params("arbitrary"),
        name="gdn_prompt",
    )(as3d(xc), as3d(dz), as3d(ba), *consts)
    return o.reshape(batch * seq, DN_WIDTH), s


GDN_S_BB = 8


def _gdn_sample_kernel(xc_ref, dz_ref, ba_ref, sc_ref, s_ref, cw_ref, alog_ref, dtb_ref, dn_ref,
                       hsum_ref, eye_ref, hsel_ref, hrep3_ref, o_ref, s_out_ref):
    xc = xc_ref[...]
    y = sc_ref[0] * cw_ref[0:1, :]
    y = y + sc_ref[1] * cw_ref[1:2, :]
    y = y + sc_ref[2] * cw_ref[2:3, :]
    y = _silu(y + xc * cw_ref[3:4, :])
    hsum = hsum_ref[...]
    q = y[:, :DN_WIDTH]
    k = y[:, DN_WIDTH:2 * DN_WIDTH]
    v = y[:, 2 * DN_WIDTH:]
    q = q * lax.rsqrt(_mm_sel_rhs(q * q, hsum) + EPS) * (DN_DK ** -0.5)
    k = k * lax.rsqrt(_mm_sel_rhs(k * k, hsum) + EPS)
    beta_c, g_c = _gdn_gates(ba_ref[...], alog_ref[...], dtb_ref[...])
    eg_c = jnp.exp(g_c)
    eye = eye_ref[...]
    tr = lambda a: lax.dot_general(a, eye, (((0,), (0,)), ((), ())), precision=lax.Precision.HIGHEST,
                                   preferred_element_type=F32)
    gates_t = tr(jnp.concatenate([beta_c, eg_c], axis=1))
    beta_t = gates_t[:LANES]
    eg_t = gates_t[LANES:]
    dz = dz_ref[...]
    dn = dn_ref[...]
    split = lambda r: jnp.concatenate([r[:, h * DN_DV:(h + 1) * DN_DV] for h in range(DN_HEADS)], axis=0)
    own_head = hsel_ref[...].astype(F32)
    hrep3 = hrep3_ref[...]
    seqs = range(GDN_S_BB)
    dot = lambda a, b: jnp.dot(a.astype(BF16), b.astype(BF16), preferred_element_type=F32)

    def pieces(x):
        p1 = x.astype(BF16).astype(F32)
        r1 = x - p1
        p2 = r1.astype(BF16).astype(F32)
        return p1, p2, (r1 - p2).astype(BF16).astype(F32)

    heads = DN_HEADS
    k_pieces, kqs = [], []
    for b in seqs:
        kq_bd = jnp.concatenate([own_head * k[b:b + 1, :], own_head * q[b:b + 1, :]], axis=0)
        a1, a2, a3 = pieces(kq_bd)
        s1, s2, s3 = pieces(s_ref[b])
        r1 = dot(jnp.concatenate([a1, a2, a3], axis=0), s1)
        r2 = dot(jnp.concatenate([a1, a2], axis=0), s2)
        r3 = dot(a1, s3)
        n = 2 * heads
        kqs.append(((r3 + r2[n:] + r1[2 * n:]) + (r2[:n] + r1[n:2 * n])) + r1[:n])
        k_pieces.append((a1[:heads], a2[:heads], a3[:heads]))
    egs = [eg_t[DN_HEADS:2 * DN_HEADS, b:b + 1] for b in seqs]
    qks = [jnp.sum(split(q[b:b + 1, :]) * split(k[b:b + 1, :]), axis=-1, keepdims=True) for b in seqs]
    v_news = [beta_t[0:DN_HEADS, b:b + 1] * (split(v[b:b + 1, :]) - eg * kq[:heads])
              for b, eg, kq in zip(seqs, egs, kqs)]
    os_ = [eg * kq[heads:] + qk * v_new for eg, kq, qk, v_new in zip(egs, kqs, qks, v_news)]
    inv_rms = [lax.rsqrt(jnp.mean(o * o, axis=-1, keepdims=True) + EPS) for o in os_]
    for b, o, r in zip(seqs, os_, inv_rms):
        o_ref[b] = o * r * dn * _silu(split(dz[b:b + 1, :]))
    outers, egrows = [], []
    for (k1, k2, k3), v_new, eg in zip(k_pieces, v_news, egs):
        v1, v2, v3 = pieces(v_new)
        lhs = jnp.concatenate([k1, k1, k2, k1, k2, k3], axis=0).astype(BF16)
        rhs = jnp.concatenate([v1, v2, v1, v3, v2, v1], axis=0).astype(BF16)
        outers.append(lax.dot_general(lhs, rhs, (((0,), (0,)), ((), ())), preferred_element_type=F32))
        egrows.append(dot(hrep3, jnp.concatenate(pieces(jnp.broadcast_to(eg, (DN_HEADS, DN_DV))), axis=0)))
    for b, outer, egrow in zip(seqs, outers, egrows):
        s_out_ref[b] = s_ref[b] * egrow + outer


def _gdn_sample(xc, dz, ba, sconv_t, state, conv_w, alog, dtb, dn):
    nseq = xc.shape[0]
    lane = np.arange(DN_WIDTH)
    hsum = jnp.asarray((lane[:, None] // DN_DV == lane[None, :] // DN_DV).astype(np.float32), dtype=BF16)
    eye = jnp.eye(GDN_S_BB, dtype=F32)
    hsel_np = (np.arange(DN_HEADS)[:, None] == lane[None, :] // DN_DK).astype(np.float32)
    hsel = jnp.asarray(hsel_np, dtype=BF16)
    hrep3 = jnp.asarray(np.tile(hsel_np.T, (1, 3)), dtype=BF16)
    row = lambda n: pl.BlockSpec((GDN_S_BB, n), lambda i: (i, 0))
    full = lambda a: pl.BlockSpec(a.shape, lambda i: (0,) * a.ndim)
    st = pl.BlockSpec((GDN_S_BB, DN_HEADS * DN_DK, DN_DV), lambda i: (i, 0, 0))
    consts = (conv_w, alog, dtb, dn, hsum, eye, hsel, hrep3)
    return pl.pallas_call(
        _gdn_sample_kernel,
        grid=(nseq // GDN_S_BB,),
        in_specs=[row(CONV_CH), row(DN_WIDTH), row(LANES),
                  pl.BlockSpec((CONV_WIDTH - 1, GDN_S_BB, CONV_CH), lambda i: (0, i, 0)), st]
                 + [full(a) for a in consts],
        out_specs=[pl.BlockSpec((GDN_S_BB, DN_HEADS, DN_DV), lambda i: (i, 0, 0)), st],
        out_shape=[jax.ShapeDtypeStruct((nseq, DN_HEADS, DN_DV), F32),
                   jax.ShapeDtypeStruct(state.shape, F32)],
        compiler_params=_params("parallel"),
        name="gdn_sample",
    )(xc, dz, ba, sconv_t, state, *consts)


def _attn_sample_lanes_kernel(att_ref, ck_ref, cv_ref, bucket_ref, rb_ref, sink_ref, o_ref, s_scr):
    g = pl.program_id(0)
    nseq = att_ref.shape[0]
    rnd = lambda a: a.astype(BF16).astype(F32)
    att = att_ref[...]
    q_all_t = (att[:, :ATT_WIDTH] * (HEAD_DIM ** -0.5)).T
    kv_new_t = att[:, ATT_WIDTH:].T
    qsel = [jnp.where(g == 0, q_all_t[hh * HEAD_DIM:(hh + 1) * HEAD_DIM],
                      q_all_t[(GQA + hh) * HEAD_DIM:(GQA + hh + 1) * HEAD_DIM]) for hh in range(GQA)]
    qr = [rnd(q) for q in qsel]
    kn = rnd(jnp.where(g == 0, kv_new_t[0:HEAD_DIM], kv_new_t[HEAD_DIM:2 * HEAD_DIM]))
    vn = rnd(jnp.where(g == 0, kv_new_t[2 * HEAD_DIM:3 * HEAD_DIM], kv_new_t[3 * HEAD_DIM:]))

    def score_row(j, carry):
        kj = rnd(ck_ref[j, 0])
        for hh in range(GQA):
            s_scr[hh, pl.ds(j, 1), :] = jnp.sum(qr[hh] * kj, axis=0, keepdims=True)
        return carry
    lax.fori_loop(0, WINDOW, score_row, 0, unroll=2)

    bucket = bucket_ref[...]
    jrow = lax.broadcasted_iota(jnp.int32, (WINDOW, nseq), 0)
    prn = []
    for hh in range(GQA):
        h = g * GQA + hh
        bias = jnp.where(jrow >= 1, _bias_lookup(bucket, rb_ref, h), NEG_INF)
        s = s_scr[hh] + bias
        s_n = jnp.sum(qr[hh] * kn, axis=0, keepdims=True) + rb_ref[0, h]
        sink = sink_ref[h]
        m = jnp.maximum(jnp.maximum(jnp.max(s, axis=0, keepdims=True), s_n), sink)
        p = jnp.exp(s - m)
        p_n = jnp.exp(s_n - m)
        den = jnp.sum(p, axis=0, keepdims=True) + p_n + jnp.exp(sink - m)
        s_scr[hh] = rnd(p / den)
        prn.append(rnd(p_n / den))

    def value_row(j, acc):
        vj = rnd(cv_ref[j, 0])
        return tuple(acc[hh] + s_scr[hh, pl.ds(j, 1), :] * vj for hh in range(GQA))
    zero = jnp.zeros((HEAD_DIM, nseq), F32)
    acc = lax.fori_loop(0, WINDOW, value_row, (zero,) * GQA, unroll=2)
    for hh in range(GQA):
        o_ref[hh * HEAD_DIM:(hh + 1) * HEAD_DIM, :] = acc[hh] + prn[hh] * vn


def _attn_sample_lanes(att, ck_t, cv_t, rel_bias, sink):
    nseq = att.shape[0]
    assert nseq == LANES
    bucket = jnp.asarray(np.broadcast_to(_t5_bucket_np(WINDOW - np.arange(WINDOW))[:, None], (WINDOW, nseq)))
    smem = pl.BlockSpec(memory_space=pltpu.SMEM)
    cache = pl.BlockSpec((WINDOW, 1, HEAD_DIM, nseq), lambda g: (0, g, 0, 0))
    full = lambda a: pl.BlockSpec(a.shape, lambda g: (0,) * a.ndim)
    return pl.pallas_call(
        _attn_sample_lanes_kernel,
        grid=(ATT_KV_HEADS,),
        in_specs=[full(att), cache, cache, full(bucket), smem, smem],
        out_specs=pl.BlockSpec((GQA * HEAD_DIM, nseq), lambda g: (g, 0)),
        out_shape=jax.ShapeDtypeStruct((ATT_WIDTH, nseq), F32),
        scratch_shapes=[pltpu.VMEM((GQA, WINDOW, nseq), F32)],
        compiler_params=_params("arbitrary"),
        name="attn_sample_lanes",
    )(att, ck_t, cv_t, bucket, rel_bias, sink)


def _gdn_sample_front_kernel(xc_ref, dz_ref, ba_ref, sc_ref, cw_ref, alog_ref, dtb_ref, hsum_ref,
                             q_ref, k_ref, v_ref, dz_t_ref, gates_ref):
    xc = xc_ref[...]
    y = sc_ref[0] * cw_ref[0:1, :]
    y = y + sc_ref[1] * cw_ref[1:2, :]
    y = y + sc_ref[2] * cw_ref[2:3, :]
    y = _silu(y + xc * cw_ref[3:4, :])
    hsum = hsum_ref[...]
    q = y[:, :DN_WIDTH]
    k = y[:, DN_WIDTH:2 * DN_WIDTH]
    q = q * lax.rsqrt(_mm_sel_rhs(q * q, hsum) + EPS) * (DN_DK ** -0.5)
    k = k * lax.rsqrt(_mm_sel_rhs(k * k, hsum) + EPS)
    beta_c, g_c = _gdn_gates(ba_ref[...], alog_ref[...], dtb_ref[...])
    q_ref[...] = q.T
    k_ref[...] = k.T
    v_ref[...] = y[:, 2 * DN_WIDTH:].T
    dz_t_ref[...] = dz_ref[...].T
    gates_ref[0:LANES, :] = beta_c.T
    gates_ref[LANES:, :] = jnp.exp(g_c).T


def _gdn_sample_step_kernel(q_ref, k_ref, v_ref, dz_ref, gates_ref, dn_ref, s_ref, o_ref, s_out_ref):
    h = pl.program_id(0)
    beta = gates_ref[pl.ds(h, 1), :]
    eg = gates_ref[pl.ds(LANES + DN_HEADS + h, 1), :]
    q, k, v = q_ref[...], k_ref[...], v_ref[...]
    w = (k * beta) * eg
    qg = q * eg
    ws = jnp.zeros(v.shape, F32)
    qs = jnp.zeros(v.shape, F32)
    for dk in range(DN_DK):
        s_dk = s_ref[0, dk]
        ws = ws + w[dk:dk + 1, :] * s_dk
        qs = qs + qg[dk:dk + 1, :] * s_dk
    v_new = v * beta - ws
    qk = jnp.sum(q * k, axis=0, keepdims=True)
    o = qs + qk * v_new
    for dk in range(DN_DK):
        s_out_ref[0, dk] = s_ref[0, dk] * eg + k[dk:dk + 1, :] * v_new
    o = o * lax.rsqrt(jnp.mean(o * o, axis=0, keepdims=True) + EPS) * dn_ref[...]
    o_ref[...] = o * _silu(dz_ref[...])


def _gdn_sample_lanes(xc, dz, ba, sconv_t, state_t, conv_w, alog, dtb, dn):
    nseq = xc.shape[0]
    assert nseq == LANES
    lane = np.arange(DN_WIDTH)
    hsum = jnp.asarray((lane[:, None] // DN_DV == lane[None, :] // DN_DV).astype(np.float32), dtype=BF16)
    full = lambda a: pl.BlockSpec(a.shape, lambda i: (0,) * a.ndim)
    cm = jax.ShapeDtypeStruct((DN_WIDTH, nseq), F32)
    front_in = (xc, dz, ba, sconv_t, conv_w, alog, dtb, hsum)
    q_t, k_t, v_t, dz_t, gates_t = pl.pallas_call(
        _gdn_sample_front_kernel,
        grid=(1,),
        in_specs=[full(a) for a in front_in],
        out_specs=[pl.BlockSpec((DN_WIDTH, nseq), lambda i: (0, 0))] * 4
                  + [pl.BlockSpec((2 * LANES, nseq), lambda i: (0, 0))],
        out_shape=[cm, cm, cm, cm, jax.ShapeDtypeStruct((2 * LANES, nseq), F32)],
        compiler_params=_params("arbitrary"),
        name="gdn_sample_front",
    )(*front_in)
    dn_b = jnp.broadcast_to(dn.reshape(DN_DV, 1), (DN_DV, nseq))
    head = pl.BlockSpec((DN_DK, nseq), lambda h: (h, 0))
    st = pl.BlockSpec((1, DN_DK, DN_DV, nseq), lambda h: (h, 0, 0, 0))
    return pl.pallas_call(
        _gdn_sample_step_kernel,
        grid=(DN_HEADS,),
        in_specs=[head, head, head, head, full(gates_t), full(dn_b), st],
        out_specs=[head, st],
        out_shape=[cm, jax.ShapeDtypeStruct(state_t.shape, F32)],
        compiler_params=_params("parallel"),
        name="gdn_sample_step",
    )(q_t, k_t, v_t, dz_t, gates_t, dn_b, state_t)


def _route(xn, wr):
    logits = jnp.dot(xn, wr, preferred_element_type=F32)
    lane = lax.broadcasted_iota(jnp.int32, logits.shape, 1).astype(F32)
    first_at = lambda hit: jnp.min(jnp.where(hit, lane, float(LANES)), axis=-1, keepdims=True)
    glog = jnp.where(lane < N_GROUPS, logits, NEG_INF)
    gmax = jnp.max(glog, axis=-1, keepdims=True)
    gsel = first_at(glog == gmax)
    pgsel = 1.0 / jnp.sum(jnp.exp(glog - gmax), axis=-1, keepdims=True)
    lo = ROUTER_OFF + gsel * EXPERTS_PER_GROUP
    in_group = jnp.logical_and(lane >= lo, lane < lo + EXPERTS_PER_GROUP)
    elog = jnp.where(in_group, logits, NEG_INF)
    m1 = jnp.max(elog, axis=-1, keepdims=True)
    i1 = first_at(elog == m1)
    z = jnp.sum(jnp.exp(elog - m1), axis=-1, keepdims=True)
    elog2 = jnp.where(lane == i1, NEG_INF, elog)
    m2 = jnp.max(elog2, axis=-1, keepdims=True)
    i2 = first_at(elog2 == m2)
    p1 = 1.0 / z
    p2 = jnp.exp(m2 - m1) / z
    tot = p1 + p2
    return lane, i1, i2, p1 / tot * pgsel, p2 / tot * pgsel


def _outproj(x_ref, oa_ref, od_ref, wo_ref):
    return x_ref[...] + _mm(oa_ref[...], wo_ref[:ATT_WIDTH, :]) + _mm(od_ref[...], wo_ref[ATT_WIDTH:, :])


def _outproj_router_kernel(x_ref, oa_ref, od_t_ref, wo_ref, g_ref, wr_ref, h_ref, xn_ref, gate_ref):
    h = (x_ref[...] + _mm(oa_ref[...], wo_ref[:ATT_WIDTH, :])
         + _mm(od_t_ref[...].T, wo_ref[ATT_WIDTH:, :]))
    h_ref[...] = h
    xn = _rmsnorm(h, g_ref[...]).astype(BF16)
    xn_ref[...] = xn
    lane, i1, i2, g1, g2 = _route(xn, wr_ref[...])
    gate_ref[...] = jnp.where(lane == i1, g1, 0.0) + jnp.where(lane == i2, g2, 0.0)


def _outproj_router(x, oa, od_t, wo, g, wr):
    t = x.shape[0]
    tm = t
    row = lambda n: pl.BlockSpec((tm, n), lambda i: (i, 0))
    full = lambda a: pl.BlockSpec(a.shape, lambda i: (0,) * a.ndim)
    return pl.pallas_call(
        _outproj_router_kernel,
        grid=(t // tm,),
        in_specs=[row(D_MODEL), row(ATT_WIDTH), full(od_t), full(wo), full(g), full(wr)],
        out_specs=[row(D_MODEL), row(D_MODEL), row(LANES)],
        out_shape=[jax.ShapeDtypeStruct((t, D_MODEL), F32), jax.ShapeDtypeStruct((t, D_MODEL), BF16),
                   jax.ShapeDtypeStruct((t, LANES), F32)],
        compiler_params=_params("parallel"),
        name="outproj_router",
    )(x, oa, od_t, wo, g, wr)


def _moe_kernel(xn_ref, gate_ref, wg_ref, wu_ref, wd_ref, o_ref):
    e = pl.program_id(1)
    xn = xn_ref[...]
    lane = lax.broadcasted_iota(jnp.int32, gate_ref.shape, 1)
    gate = jnp.sum(jnp.where(lane == e + ROUTER_OFF, gate_ref[...], 0.0), axis=-1, keepdims=True)
    hg = jnp.dot(xn, wg_ref[...].astype(BF16), preferred_element_type=F32)
    hu = jnp.dot(xn, wu_ref[...].astype(BF16), preferred_element_type=F32)
    hm = _silu(hg) * hu * gate
    y = jnp.dot(hm.astype(BF16), wd_ref[...].astype(BF16), preferred_element_type=F32)

    @pl.when(e == 0)
    def _():
        o_ref[...] = y

    @pl.when(e > 0)
    def _():
        o_ref[...] += y


def _moe(xn, gates, wg, wu, wd):
    t = xn.shape[0]
    tm = min(t, 1024)
    return pl.pallas_call(
        _moe_kernel,
        grid=(t // tm, N_EXPERTS),
        in_specs=[pl.BlockSpec((tm, D_MODEL), lambda i, e: (i, 0)),
                  pl.BlockSpec((tm, LANES), lambda i, e: (i, 0)),
                  pl.BlockSpec((None, D_MODEL, D_EXPERT), lambda i, e: (e, 0, 0)),
                  pl.BlockSpec((None, D_MODEL, D_EXPERT), lambda i, e: (e, 0, 0)),
                  pl.BlockSpec((None, D_EXPERT, D_MODEL), lambda i, e: (e, 0, 0))],
        out_specs=pl.BlockSpec((tm, D_MODEL), lambda i, e: (i, 0)),
        out_shape=jax.ShapeDtypeStruct((t, D_MODEL), F32),
        compiler_params=_params("parallel", "arbitrary"),
        name="moe",
    )(xn, gates, wg, wu, wd)


MOE_TM = 512
POS_TM = 1024
INFO_G1, INFO_G2, INFO_E1, INFO_E2 = 0, 1, 2, 3
DMA_UNROLL = 8


def _moe_tiles(t):
    return (2 * t) // MOE_TM + N_EXPERTS


HALF = D_MODEL // 2
U32 = jnp.uint32


def _pack_rows(x):
    bits = lambda v: lax.bitcast_convert_type(v.astype(BF16).astype(F32), U32)
    return bits(x[:, HALF:]) | (bits(x[:, :HALF]) >> 16)


def _unpack_rows(w):
    lo = lax.bitcast_convert_type(w << 16, F32)
    hi = lax.bitcast_convert_type(w & jnp.uint32(0xFFFF0000), F32)
    return lo, hi


def _route_kernel(x_ref, oa_ref, od_ref, wo_ref, g_ref, wr_ref, h_ref, xn_ref, info_ref, cnt_ref, run_scr):
    h = _outproj(x_ref, oa_ref, od_ref, wo_ref)
    h_ref[...] = h
    xn = _rmsnorm(h, g_ref[...])
    xn_ref[...] = _pack_rows(xn)
    lane, i1, i2, g1, g2 = _route(xn.astype(BF16), wr_ref[...])
    info = jnp.where(lane == INFO_G1, g1, 0.0) + jnp.where(lane == INFO_G2, g2, 0.0)
    info = info + jnp.where(lane == INFO_E1, i1, 0.0) + jnp.where(lane == INFO_E2, i2, 0.0)
    info_ref[...] = info

    @pl.when(pl.program_id(0) == 0)
    def _():
        run_scr[...] = jnp.zeros(run_scr.shape, F32)
    picked = jnp.logical_or(lane == i1, lane == i2).astype(F32)
    run_scr[...] += jnp.sum(picked, axis=0, keepdims=True)
    cnt_ref[...] = run_scr[...]


def _route_sparse(x, oa, od, wo, g, wr):
    t = x.shape[0]
    tm = ROW_TM
    row = lambda n: pl.BlockSpec((tm, n), lambda i: (i, 0))
    full = lambda a: pl.BlockSpec(a.shape, lambda i: (0,) * a.ndim)
    return pl.pallas_call(
        _route_kernel,
        grid=(t // tm,),
        in_specs=[row(D_MODEL), row(ATT_WIDTH), row(DN_WIDTH), full(wo), full(g), full(wr)],
        out_specs=[row(D_MODEL), row(HALF), row(LANES), pl.BlockSpec((1, LANES), lambda i: (0, 0))],
        out_shape=[jax.ShapeDtypeStruct((t, D_MODEL), F32), jax.ShapeDtypeStruct((t, HALF), U32),
                   jax.ShapeDtypeStruct((t, LANES), F32), jax.ShapeDtypeStruct((1, LANES), F32)],
        scratch_shapes=[pltpu.VMEM((1, LANES), F32)],
        compiler_params=_params("arbitrary"),
        name="route",
    )(x, oa, od, wo, g, wr)


def _positions_kernel(info_ref, cnt_ref, ltri_ref, utri_ref, pos_ref, run_scr, off_scr):
    info = info_ref[...]
    lane = lax.broadcasted_iota(jnp.int32, info.shape, 1).astype(F32)
    hit1 = lane == info[:, INFO_E1:INFO_E1 + 1]
    hit2 = lane == info[:, INFO_E2:INFO_E2 + 1]
    onehot = jnp.logical_or(hit1, hit2).astype(F32)

    @pl.when(pl.program_id(0) == 0)
    def _():
        tiles = jnp.floor((cnt_ref[...] + (MOE_TM - 1)) * (1.0 / MOE_TM))
        off_scr[...] = MOE_TM * jnp.dot(tiles.astype(BF16), utri_ref[...], preferred_element_type=F32)
        run_scr[...] = jnp.zeros(run_scr.shape, F32)

    before = (jnp.dot(ltri_ref[...], onehot.astype(BF16), preferred_element_type=F32)
              + run_scr[...] + off_scr[...])
    pos1 = jnp.sum(jnp.where(hit1, before, 0.0), axis=-1, keepdims=True)
    pos2 = jnp.sum(jnp.where(hit2, before, 0.0), axis=-1, keepdims=True)
    pos_ref[...] = (jnp.where(lane == 0, pos1, 0.0) + jnp.where(lane == 1, pos2, 0.0)).astype(jnp.int32)
    run_scr[...] += jnp.sum(onehot, axis=0, keepdims=True)


def _positions(info, cnt):
    t = info.shape[0]
    tm = min(t, POS_TM)
    tok = np.arange(tm)
    ltri = jnp.asarray((tok[:, None] > tok[None, :]).astype(np.float32), dtype=BF16)
    ln = np.arange(LANES)
    utri = jnp.asarray((ln[:, None] < ln[None, :]).astype(np.float32), dtype=BF16)
    full = lambda a: pl.BlockSpec(a.shape, lambda i: (0,) * a.ndim)
    return pl.pallas_call(
        _positions_kernel,
        grid=(t // tm,),
        in_specs=[pl.BlockSpec((tm, LANES), lambda i: (i, 0)), full(cnt), full(ltri), full(utri)],
        out_specs=pl.BlockSpec((tm, LANES), lambda i: (i, 0)),
        out_shape=jax.ShapeDtypeStruct((t, LANES), jnp.int32),
        scratch_shapes=[pltpu.VMEM((1, LANES), F32), pltpu.VMEM((1, LANES), F32)],
        compiler_params=_params("arbitrary"),
        name="positions",
    )(info, cnt, ltri, utri)


def _row_copy(src_hbm, src_row, dst_hbm, dst_row, sem):
    return pltpu.make_async_copy(src_hbm.at[pl.ds(src_row, 1)], dst_hbm.at[pl.ds(dst_row, 1)], sem)


SCATTER_SLOTS = 3


def _scatter_kernel(pos1_ref, pos2_ref, last_ref, used_ref, nt_ref, xn_hbm, zero_hbm, xs_hbm,
                    buf, lsem, sem, zsem, *, n_tok):
    max_tiles = xs_hbm.shape[0] // MOE_TM

    def zero_tile(tile):
        return pltpu.make_async_copy(zero_hbm, xs_hbm.at[pl.ds(tile * MOE_TM, MOE_TM)], zsem)

    def for_unused(fn):
        def body(tile, carry):
            fn(tile)
            return carry
        lax.fori_loop(nt_ref[0], max_tiles, body, 0)

    for e in range(N_EXPERTS):
        @pl.when(used_ref[e] > 0)
        def _():
            zero_tile(last_ref[e]).start()
    for_unused(lambda tile: zero_tile(tile).start())
    for e in range(N_EXPERTS):
        @pl.when(used_ref[e] > 0)
        def _():
            zero_tile(last_ref[e]).wait()
    for_unused(lambda tile: zero_tile(tile).wait())

    tm = buf.shape[1]
    n = n_tok // tm

    def load(i):
        return pltpu.make_async_copy(xn_hbm.at[pl.ds(i * tm, tm)], buf.at[i % SCATTER_SLOTS],
                                     lsem.at[i % SCATTER_SLOTS])

    def wait_rows(slot):
        pltpu.make_async_copy(xs_hbm.at[pl.ds(0, 2 * tm)], xs_hbm.at[pl.ds(0, 2 * tm)], sem.at[slot]).wait()

    load(0).start()
    load(1).start()

    def step(i, carry):
        slot = i % SCATTER_SLOTS
        load(i).wait()

        def body(j, c2):
            tok = i * tm + j
            src = buf.at[slot, pl.ds(j, 1)]
            pltpu.make_async_copy(src, xs_hbm.at[pl.ds(pos1_ref[tok], 1)], sem.at[slot]).start()
            pltpu.make_async_copy(src, xs_hbm.at[pl.ds(pos2_ref[tok], 1)], sem.at[slot]).start()
            return c2
        lax.fori_loop(0, tm, body, 0, unroll=DMA_UNROLL)

        @pl.when(i >= 1)
        def _():
            wait_rows((i - 1) % SCATTER_SLOTS)

        @pl.when(i + 2 < n)
        def _():
            load(i + 2).start()
        return carry
    lax.fori_loop(0, n, step, 0)
    wait_rows((n - 1) % SCATTER_SLOTS)


def _scatter_rows(xn, pos1, pos2, last_tile, used, n_tiles, n_rows):
    t = xn.shape[0]
    zero = jnp.zeros((MOE_TM, D_MODEL), F32)
    any_spec = pl.BlockSpec(memory_space=pl.ANY)
    return pl.pallas_call(
        functools.partial(_scatter_kernel, n_tok=t),
        grid_spec=pltpu.PrefetchScalarGridSpec(
            num_scalar_prefetch=5, grid=(1,),
            in_specs=[any_spec, any_spec], out_specs=any_spec,
            scratch_shapes=[pltpu.VMEM((SCATTER_SLOTS, MOE_TM, D_MODEL), F32),
                            pltpu.SemaphoreType.DMA((SCATTER_SLOTS,)),
                            pltpu.SemaphoreType.DMA((SCATTER_SLOTS,)),
                            pltpu.SemaphoreType.DMA]),
        out_shape=jax.ShapeDtypeStruct((n_rows, D_MODEL), F32),
        compiler_params=_params("arbitrary"),
        name="scatter_rows",
    )(pos1, pos2, last_tile, used, n_tiles, xn, zero)


def _experts_kernel(te_ref, tv_ref, nt_ref, xs_ref, wg_ref, wu_ref, wd_ref, ys_ref, wg_s, wu_s, wd_s):
    i = pl.program_id(0)
    used = i < nt_ref[0]

    @pl.when(jnp.logical_or(i == 0, te_ref[i] != te_ref[jnp.maximum(i - 1, 0)]))
    def _():
        wg_s[...] = wg_ref[...].astype(BF16)
        wu_s[...] = wu_ref[...].astype(BF16)
        wd_s[...] = wd_ref[...].astype(BF16)

    @pl.when(used)
    def _():
        row = lax.broadcasted_iota(jnp.int32, xs_ref.shape, 0)
        x_lo, x_hi = _unpack_rows(jnp.where(row < tv_ref[i], xs_ref[...], jnp.uint32(0)))
        x_lo = x_lo.astype(BF16)
        x_hi = x_hi.astype(BF16)
        up = lambda w_s: (jnp.dot(x_lo, w_s[:HALF, :], preferred_element_type=F32)
                          + jnp.dot(x_hi, w_s[HALF:, :], preferred_element_type=F32))
        hm = (_silu(up(wg_s)) * up(wu_s)).astype(BF16)
        ys_ref[...] = _pack_rows(jnp.dot(hm, wd_s[...], preferred_element_type=F32))

    @pl.when(jnp.logical_not(used))
    def _():
        ys_ref[...] = jnp.zeros(ys_ref.shape, U32)


def _experts(xs, tile_expert, tile_valid, n_tiles, wg, wu, wd):
    max_tiles = xs.shape[0] // MOE_TM
    rows = pl.BlockSpec((MOE_TM, HALF), lambda i, te, tv, nt: (i, 0))
    wspec = lambda shape: pl.BlockSpec((None,) + shape, lambda i, te, tv, nt: (te[i], 0, 0))
    return pl.pallas_call(
        _experts_kernel,
        grid_spec=pltpu.PrefetchScalarGridSpec(
            num_scalar_prefetch=3, grid=(max_tiles,),
            in_specs=[rows, wspec((D_MODEL, D_EXPERT)), wspec((D_MODEL, D_EXPERT)),
                      wspec((D_EXPERT, D_MODEL))],
            out_specs=rows,
            scratch_shapes=[pltpu.VMEM((D_MODEL, D_EXPERT), BF16), pltpu.VMEM((D_MODEL, D_EXPERT), BF16),
                            pltpu.VMEM((D_EXPERT, D_MODEL), BF16)]),
        out_shape=jax.ShapeDtypeStruct(xs.shape, U32),
        compiler_params=_params("arbitrary"),
        name="experts",
    )(tile_expert, tile_valid, n_tiles, xs, wg, wu, wd)


def _ple_gather_kernel(pos1_ref, pos2_ref, h_ref, info_ref, p_ref, wpp_ref, wpg_ref, gp_ref, gf_ref,
                       ys_hbm, y_ref, ybuf, sem):
    i = pl.program_id(0)
    n = pl.num_programs(0)
    tm = h_ref.shape[0]

    def issue(tile, slot):
        def body(j, carry):
            tok = tile * tm + j
            pltpu.make_async_copy(ys_hbm.at[pl.ds(pos1_ref[tok], 1)], ybuf.at[slot, 0, pl.ds(j, 1)],
                                  sem.at[slot]).start()
            pltpu.make_async_copy(ys_hbm.at[pl.ds(pos2_ref[tok], 1)], ybuf.at[slot, 1, pl.ds(j, 1)],
                                  sem.at[slot]).start()
            return carry
        lax.fori_loop(0, tm, body, 0, unroll=DMA_UNROLL)

    @pl.when(i == 0)
    def _():
        issue(0, 0)

    @pl.when(i + 1 < n)
    def _():
        issue(i + 1, (i + 1) % 2)

    slot = i % 2
    pltpu.make_async_copy(ybuf.at[slot], ybuf.at[slot], sem.at[slot]).wait()
    info = info_ref[...]
    moe = info[:, INFO_G1:INFO_G1 + 1] * ybuf[slot, 0] + info[:, INFO_G2:INFO_G2 + 1] * ybuf[slot, 1]
    h = h_ref[...] + moe
    hn = _rmsnorm(h, gp_ref[...])
    h = h + _mm(p_ref[...], wpp_ref[...]) * _sigmoid(_mm(hn, wpg_ref[...]))
    y_ref[...] = _rmsnorm(h, gf_ref[...])


def _ple_gather(h, info, p, ys, pos1, pos2, wpp, wpg, gp, gf):
    t = h.shape[0]
    tm = 256
    row = lambda n: pl.BlockSpec((tm, n), lambda i, p1, p2: (i, 0))
    full = lambda a: pl.BlockSpec(a.shape, lambda i, p1, p2: (0,) * a.ndim)
    return pl.pallas_call(
        _ple_gather_kernel,
        grid_spec=pltpu.PrefetchScalarGridSpec(
            num_scalar_prefetch=2, grid=(t // tm,),
            in_specs=[row(D_MODEL), row(LANES), row(PLE_DIM), full(wpp), full(wpg), full(gp), full(gf),
                      pl.BlockSpec(memory_space=pl.ANY)],
            out_specs=row(D_MODEL),
            scratch_shapes=[pltpu.VMEM((2, 2, tm, D_MODEL), F32), pltpu.SemaphoreType.DMA((2,))]),
        out_shape=jax.ShapeDtypeStruct((t, D_MODEL), F32),
        compiler_params=_params("arbitrary"),
        name="ple_gather",
    )(pos1, pos2, h, info, p, wpp, wpg, gp, gf, ys)


SC_IDX = 128
SC_ROWS = 64
SC_WORKERS = 32


def _sc_mesh():
    return plsc.VectorSubcoreMesh(core_axis_name="c", subcore_axis_name="s")


def _sc_windows(t, fn):
    per_worker = t // SC_WORKERS
    worker = lax.axis_index(("c", "s"))

    @pl.loop(0, per_worker // SC_IDX)
    def _(w):
        fn(worker * per_worker + w * SC_IDX)


def _sc_scatter_rows(xn, pos1, pos2, n_rows):
    t, d = xn.shape
    assert t % (SC_WORKERS * SC_IDX) == 0
    idx_t = pltpu.VMEM((1, SC_IDX), jnp.int32)

    @pl.kernel(out_type=jax.ShapeDtypeStruct((n_rows, d), xn.dtype), mesh=_sc_mesh(),
               scratch_types=[idx_t, idx_t, pltpu.VMEM((SC_ROWS, d), xn.dtype)])
    def scatter(x_hbm, p1_hbm, p2_hbm, o_hbm, i1_v, i2_v, buf):
        def window(base):
            pltpu.sync_copy(p1_hbm.at[:, pl.ds(base, SC_IDX)], i1_v)
            pltpu.sync_copy(p2_hbm.at[:, pl.ds(base, SC_IDX)], i2_v)
            for k in range(SC_IDX // SC_ROWS):
                pltpu.sync_copy(x_hbm.at[pl.ds(base + k * SC_ROWS, SC_ROWS)], buf)
                pltpu.sync_copy(buf, o_hbm.at[i1_v.at[0, pl.ds(k * SC_ROWS, SC_ROWS)]])
                pltpu.sync_copy(buf, o_hbm.at[i2_v.at[0, pl.ds(k * SC_ROWS, SC_ROWS)]])
        _sc_windows(t, window)

    return scatter(xn, pos1.reshape(1, t), pos2.reshape(1, t))


def _sc_gather_rows(ys, pos1, pos2):
    t = pos1.shape[0]
    d = ys.shape[1]
    assert t % (SC_WORKERS * SC_IDX) == 0
    idx_t = pltpu.VMEM((1, SC_IDX), jnp.int32)
    out = jax.ShapeDtypeStruct((t, d), ys.dtype)

    @pl.kernel(out_type=(out, out), mesh=_sc_mesh(),
               scratch_types=[idx_t, idx_t, pltpu.VMEM((SC_ROWS, d), ys.dtype)])
    def gather(y_hbm, p1_hbm, p2_hbm, o1_hbm, o2_hbm, i1_v, i2_v, buf):
        def window(base):
            pltpu.sync_copy(p1_hbm.at[:, pl.ds(base, SC_IDX)], i1_v)
            pltpu.sync_copy(p2_hbm.at[:, pl.ds(base, SC_IDX)], i2_v)
            for k in range(SC_IDX // SC_ROWS):
                rows = pl.ds(base + k * SC_ROWS, SC_ROWS)
                pltpu.sync_copy(y_hbm.at[i1_v.at[0, pl.ds(k * SC_ROWS, SC_ROWS)]], buf)
                pltpu.sync_copy(buf, o1_hbm.at[rows])
                pltpu.sync_copy(y_hbm.at[i2_v.at[0, pl.ds(k * SC_ROWS, SC_ROWS)]], buf)
                pltpu.sync_copy(buf, o2_hbm.at[rows])
        _sc_windows(t, window)

    return gather(ys, pos1.reshape(1, t), pos2.reshape(1, t))


def _ple_sparse_kernel(h_ref, info_ref, y1_ref, y2_ref, p_ref, wpp_ref, wpg_ref, gp_ref, gf_ref, y_ref):
    info = info_ref[...]
    g1 = info[:, INFO_G1:INFO_G1 + 1]
    g2 = info[:, INFO_G2:INFO_G2 + 1]
    y1_lo, y1_hi = _unpack_rows(y1_ref[...])
    y2_lo, y2_hi = _unpack_rows(y2_ref[...])
    moe = jnp.concatenate([g1 * y1_lo + g2 * y2_lo, g1 * y1_hi + g2 * y2_hi], axis=1)
    h = h_ref[...] + moe
    hn = _rmsnorm(h, gp_ref[...])
    h = h + _mm(p_ref[...], wpp_ref[...]) * _sigmoid(_mm(hn, wpg_ref[...]))
    y_ref[...] = _rmsnorm(h, gf_ref[...])


def _ple_sparse(h, info, y1, y2, p, wpp, wpg, gp, gf):
    t = h.shape[0]
    tm = ROW_TM
    row = lambda n: pl.BlockSpec((tm, n), lambda i: (i, 0))
    full = lambda a: pl.BlockSpec(a.shape, lambda i: (0,) * a.ndim)
    return pl.pallas_call(
        _ple_sparse_kernel,
        grid=(t // tm,),
        in_specs=[row(D_MODEL), row(LANES), row(HALF), row(HALF), row(PLE_DIM),
                  full(wpp), full(wpg), full(gp), full(gf)],
        out_specs=row(D_MODEL),
        out_shape=jax.ShapeDtypeStruct((t, D_MODEL), F32),
        compiler_params=_params("parallel"),
        name="ple_sparse",
    )(h, info, y1, y2, p, wpp, wpg, gp, gf)


def _tile_tables(cnt, max_tiles):
    tiles_e = (cnt + (MOE_TM - 1)) // MOE_TM
    ends = jnp.cumsum(tiles_e)
    n_tiles = ends[-1]
    tile = jnp.arange(max_tiles, dtype=jnp.int32)
    idx = jnp.minimum(tile, n_tiles - 1)
    tile_expert = jnp.sum((idx[:, None] >= ends[None, :]).astype(jnp.int32), axis=1)
    mine = tile_expert[:, None] == jnp.arange(N_EXPERTS, dtype=jnp.int32)[None, :]
    of_mine = lambda v: jnp.sum(jnp.where(mine, v[None, :], 0), axis=1)
    valid = jnp.clip(of_mine(cnt) - (idx - of_mine(ends - tiles_e)) * MOE_TM, 0, MOE_TM)
    tile_valid = jnp.where(tile < n_tiles, valid, 0).astype(jnp.int32)
    return (tile_expert, tile_valid, n_tiles.reshape(1), (ends - 1).astype(jnp.int32),
            tiles_e.astype(jnp.int32))


def _ple_final_kernel(h_ref, m_ref, p_ref, wpp_ref, wpg_ref, gp_ref, gf_ref, y_ref):
    h = h_ref[...] + m_ref[...]
    hn = _rmsnorm(h, gp_ref[...])
    h = h + _mm(p_ref[...], wpp_ref[...]) * _sigmoid(_mm(hn, wpg_ref[...]))
    y_ref[...] = _rmsnorm(h, gf_ref[...])


def _ple_final(h, m, p, wpp, wpg, gp, gf):
    t = h.shape[0]
    tm = min(t, 256)
    row = lambda n: pl.BlockSpec((tm, n), lambda i: (i, 0))
    full = lambda a: pl.BlockSpec(a.shape, lambda i: (0,) * a.ndim)
    return pl.pallas_call(
        _ple_final_kernel,
        grid=(t // tm,),
        in_specs=[row(D_MODEL), row(D_MODEL), row(PLE_DIM), full(wpp), full(wpg), full(gp), full(gf)],
        out_specs=row(D_MODEL),
        out_shape=jax.ShapeDtypeStruct((t, D_MODEL), F32),
        compiler_params=_params("parallel"),
        name="ple_final",
    )(h, m, p, wpp, wpg, gp, gf)


def kernel(x_prompt, x_sample, p_prompt, p_sample, cache_k, cache_v, state_conv, state_S, rel_bias, norm_mix, w_in, att_sink, conv_w, dn_A_log, dn_dt_bias, dn_norm, w_out, norm_ffn, w_router_group, w_router_expert, w_gate, w_up, w_down, w_ple_proj, w_ple_gate, norm_ple, norm_final):
    batch, seq, _ = x_prompt.shape
    nseq = x_sample.shape[0]
    assert x_sample.shape[1] == 1 and norm_mix.shape[0] == 1 and cache_k.shape[2] == WINDOW
    assert seq % GDN_TB == 0 and seq % ATT_BLOCK == 0

    wi = w_in[0]
    o_db = ATT_COLS + CONV_CH
    w_in_re = jnp.concatenate(
        [wi[:, :o_db], wi[:, o_db + 2 * DN_HEADS:], wi[:, o_db:o_db + 2 * DN_HEADS],
         jnp.zeros((D_MODEL, LANES - 2 * DN_HEADS), F32)], axis=1).astype(BF16)
    row = lambda a: a.reshape(1, -1).astype(F32)
    pad_lanes = lambda a, off: jnp.zeros((1, LANES), F32).at[0, off:off + a.shape[0]].set(a)
    alog = pad_lanes(dn_A_log[0], DN_HEADS)
    dtb = pad_lanes(dn_dt_bias[0], DN_HEADS)
    dnx = jnp.tile(dn_norm[0], DN_HEADS).reshape(1, DN_WIDTH)
    w_router = jnp.concatenate(
        [w_router_group[0], w_router_expert[0],
         jnp.zeros((D_MODEL, LANES - N_GROUPS - N_EXPERTS), F32)], axis=1).astype(BF16)
    wo = w_out[0].astype(BF16)
    wg, wu, wd = w_gate[0], w_up[0], w_down[0]
    wpp, wpg = w_ple_proj[0].astype(BF16), w_ple_gate[0].astype(BF16)
    sink = att_sink[0]

    qi = np.arange(ATT_BLOCK)[:, None]
    kj = np.arange(2 * ATT_BLOCK)[None, :]
    bucket_p = jnp.asarray(_t5_bucket_np(qi + ATT_BLOCK - kj))
    bucket_s = jnp.asarray(_t5_bucket_np(WINDOW - np.arange(WINDOW)[None, :]))

    def tail(x, o_att, o_dn, p):
        h1, xn2, gates = _outproj_router(x, o_att, o_dn, wo, row(norm_ffn[0]), w_router)
        moe = _moe(xn2, gates, wg, wu, wd)
        return _ple_final(h1, moe, p, wpp, wpg, row(norm_ple[0]), row(norm_final))

    xp = x_prompt.reshape(batch * seq, D_MODEL)
    att_p, qkv_p, dz_p, ba_p, xc_tails = _inproj_conv(xp, row(norm_mix[0]), w_in_re, conv_w[0], seq)
    o_att_p = _attn_prompt(att_p, bucket_p, rel_bias, sink, batch, seq)
    o_dn_p, s_p = _gdn_prompt(qkv_p, dz_p, ba_p, alog, dtb, dnx, batch, seq)
    h1, xn2, info, cnt = _route_sparse(xp, o_att_p, o_dn_p, wo, row(norm_ffn[0]), w_router)
    pos = _positions(info, cnt)
    pos1, pos2 = pos[:, 0], pos[:, 1]
    max_tiles = _moe_tiles(batch * seq)
    cnt_e = cnt[0, ROUTER_OFF:ROUTER_OFF + N_EXPERTS].astype(jnp.int32)
    tile_expert, tile_valid, n_tiles, last_tile, used = _tile_tables(cnt_e, max_tiles)
    xs_sorted = _sc_scatter_rows(xn2, pos1, pos2, max_tiles * MOE_TM)

    xs = x_sample.reshape(nseq, D_MODEL)
    att_s, xc_s, dz_s, ba_s = _inproj(xs, row(norm_mix[0]), w_in_re)
    ck_t = jnp.transpose(cache_k[0], (0, 2, 3, 1))
    cv_t = jnp.transpose(cache_v[0], (0, 2, 3, 1))
    o_att_s, ks_t, vs_t = _attn_sample(att_s, ck_t, cv_t, bucket_s, rel_bias, sink)
    sconv_t = jnp.swapaxes(state_conv[0], 0, 1)
    o_dn_s_t, s_s_t = _gdn_sample_lanes(xc_s, dz_s, ba_s, sconv_t, jnp.transpose(state_S[0], (1, 2, 3, 0)),
                                        conv_w[0], alog, dtb, dn_norm[0])
    s_s = jnp.transpose(s_s_t, (3, 0, 1, 2))

    anchor = jnp.logical_or(o_att_s[0, 0] != o_att_s[0, 0], o_dn_s_t[0, 0] != o_dn_s_t[0, 0]).astype(jnp.int32)
    ys = _experts(xs_sorted, tile_expert, tile_valid, n_tiles + anchor, wg, wu, wd)
    y1, y2 = _sc_gather_rows(ys, pos1, pos2)
    y_s = tail(xs, o_att_s, o_dn_s_t, p_sample[0].reshape(nseq, PLE_DIM))
    y_p = _ple_sparse(h1, info, y1, y2, p_prompt[0].reshape(batch * seq, PLE_DIM),
                      wpp, wpg, row(norm_ple[0]), row(norm_final))

    att_p3 = att_p.reshape(batch, seq, ATT_COLS)
    kv_shape = (1, batch, WINDOW, ATT_KV_HEADS, HEAD_DIM)
    k_p = att_p3[:, seq - WINDOW:, ATT_WIDTH:ATT_WIDTH + KV_WIDTH].reshape(kv_shape)
    v_p = att_p3[:, seq - WINDOW:, ATT_WIDTH + KV_WIDTH:].reshape(kv_shape)
    conv_p = xc_tails.reshape(batch, -1, TAIL, CONV_CH)[:, -1, TAIL - (CONV_WIDTH - 1):][None]
    k_s = jnp.transpose(ks_t, (0, 3, 1, 2))[None]
    v_s = jnp.transpose(vs_t, (0, 3, 1, 2))[None]
    conv_s = jnp.concatenate([state_conv[0][:, 1:], xc_s[:, None, :]], axis=1)[None]
    return (y_p.reshape(batch, seq, D_MODEL), y_s.reshape(nseq, 1, D_MODEL),
            k_p, v_p, conv_p, s_p[None], k_s, v_s, conv_s, s_s[None])
```

```python
import functools
import math

import numpy as np
import jax
import jax.numpy as jnp
from jax import lax
from jax.experimental import pallas as pl
from jax.experimental.pallas import tpu as pltpu
from jax.experimental.pallas import tpu_sc as plsc

F32 = jnp.float32
BF16 = jnp.bfloat16

D_MODEL = 1024
ATT_HEADS = 8
ATT_KV_HEADS = 2
HEAD_DIM = 64
GQA = ATT_HEADS // ATT_KV_HEADS
WINDOW = 128
ATT_BLOCK = 128
N_BUCKETS = 32
DN_HEADS = 8
DN_DK = 64
DN_DV = 64
CONV_WIDTH = 4
DN_CHUNK = 64
ATT_WIDTH = ATT_HEADS * HEAD_DIM
KV_WIDTH = ATT_KV_HEADS * HEAD_DIM
DN_WIDTH = DN_HEADS * DN_DV
CONV_CH = 3 * DN_WIDTH
N_GROUPS = 4
EXPERTS_PER_GROUP = 8
N_EXPERTS = N_GROUPS * EXPERTS_PER_GROUP
D_EXPERT = 256
PLE_DIM = 256
EPS = 1e-6
NEG_INF = float("-inf")

ATT_COLS = ATT_WIDTH + 2 * KV_WIDTH
LANES = 128
IN_COLS = ATT_COLS + CONV_CH + DN_WIDTH + LANES
ROUTER_OFF = N_GROUPS
VMEM_LIMIT = 48 * 1024 * 1024
ROW_TM = 512


def _params(*sem):
    return pltpu.CompilerParams(dimension_semantics=sem, vmem_limit_bytes=VMEM_LIMIT)


def _mm(a, b):
    return jnp.dot(a.astype(BF16), b.astype(BF16), preferred_element_type=F32)


def _mm_nt(a, b):
    return lax.dot_general(a.astype(BF16), b.astype(BF16), (((1,), (1,)), ((), ())),
                           preferred_element_type=F32)


def _mm_tn(a, b):
    return lax.dot_general(a.astype(BF16), b.astype(BF16), (((0,), (0,)), ((), ())),
                           preferred_element_type=F32)


def _split3(x):
    h1 = x.astype(BF16)
    r1 = x - h1.astype(F32)
    h2 = r1.astype(BF16)
    h3 = (r1 - h2.astype(F32)).astype(BF16)
    return h1, h2, h3


def _mm_sel_rhs(x, sel):
    h1, h2, h3 = _split3(x)
    d = lambda h: jnp.dot(h, sel, preferred_element_type=F32)
    return d(h1) + d(h2) + d(h3)


def _mm_sel_lhs(sel, x):
    h1, h2, h3 = _split3(x)
    d = lambda h: jnp.dot(sel, h, preferred_element_type=F32)
    return d(h1) + d(h2) + d(h3)


def _mm3(a, b):
    ah = a.astype(BF16)
    al = (a - ah.astype(F32)).astype(BF16)
    bh = b.astype(BF16)
    bl = (b - bh.astype(F32)).astype(BF16)
    d = lambda u, v: jnp.dot(u, v, preferred_element_type=F32)
    return d(ah, bh) + d(ah, bl) + d(al, bh)


def _sigmoid(x):
    return 1.0 / (1.0 + jnp.exp(-x))


def _silu(x):
    return x * _sigmoid(x)


def _softplus(x):
    return jnp.maximum(x, 0.0) + jnp.log1p(jnp.exp(-jnp.abs(x)))


def _rmsnorm(x, g):
    return x * lax.rsqrt(jnp.mean(x * x, axis=-1, keepdims=True) + EPS) * g


def _t5_bucket_np(dist):
    max_exact = N_BUCKETS // 2
    d = np.maximum(dist, 0)
    ratio = (np.log(np.maximum(d, 1).astype(np.float32) / np.float32(max_exact))
             / np.float32(math.log(WINDOW / max_exact))).astype(np.float32)
    large = np.minimum(max_exact + (ratio * np.float32(N_BUCKETS - max_exact)).astype(np.int32),
                       N_BUCKETS - 1)
    return np.where(d < max_exact, d, large).astype(np.int32)


def _bias_lookup(bucket, rb_ref, h):
    acc = jnp.zeros(bucket.shape, F32)
    for t in range(N_BUCKETS):
        acc = jnp.where(bucket == t, rb_ref[t, h], acc)
    return acc


def _inproj_kernel(x_ref, g_ref, w_ref, att_ref, xc_ref, dz_ref, ba_ref):
    xn = _rmsnorm(x_ref[...], g_ref[...]).astype(BF16)
    o0, o1, o2 = ATT_COLS, ATT_COLS + CONV_CH, ATT_COLS + CONV_CH + DN_WIDTH
    att_ref[...] = jnp.dot(xn, w_ref[:, :o0], preferred_element_type=F32)
    xc_ref[...] = jnp.dot(xn, w_ref[:, o0:o1], preferred_element_type=F32)
    dz_ref[...] = jnp.dot(xn, w_ref[:, o1:o2], preferred_element_type=F32)
    ba_ref[...] = jnp.dot(xn, w_ref[:, o2:], preferred_element_type=F32)


def _inproj(x, g, w):
    t = x.shape[0]
    tm = min(t, ROW_TM)
    row = lambda n: pl.BlockSpec((tm, n), lambda i: (i, 0))
    full = lambda a: pl.BlockSpec(a.shape, lambda i: (0,) * a.ndim)
    return pl.pallas_call(
        _inproj_kernel,
        grid=(t // tm,),
        in_specs=[row(D_MODEL), full(g), full(w)],
        out_specs=[row(ATT_COLS), row(CONV_CH), row(DN_WIDTH), row(LANES)],
        out_shape=[jax.ShapeDtypeStruct((t, n), F32) for n in (ATT_COLS, CONV_CH, DN_WIDTH, LANES)],
        compiler_params=_params("parallel"),
        name="inproj",
    )(x, g, w)


TAIL = 8
PAIR = 2 * DN_DK
N_PAIRS = DN_WIDTH // PAIR


def _head_sums(z, pair_ones):
    hi = z.astype(BF16)
    lw = (z - hi.astype(F32)).astype(BF16)
    d = lambda a, p: jnp.dot(a[:, p * PAIR:(p + 1) * PAIR], pair_ones, preferred_element_type=F32)
    return jnp.concatenate([d(hi, p) + d(lw, p) for p in range(N_PAIRS)], axis=1)


def _inproj_conv_kernel(x_ref, g_ref, w_ref, cw_ref, ones_ref, att_ref, qkv_ref, dz_ref, ba_ref, tail_ref,
                        xp_scr, *, tiles_per_seq):
    tm = x_ref.shape[0]

    @pl.when(pl.program_id(0) % tiles_per_seq == 0)
    def _():
        xp_scr[0:TAIL, :] = jnp.zeros((TAIL, CONV_CH), F32)

    xn = _rmsnorm(x_ref[...], g_ref[...]).astype(BF16)
    o0, o1, o2 = ATT_COLS, ATT_COLS + CONV_CH, ATT_COLS + CONV_CH + DN_WIDTH
    xc = jnp.dot(xn, w_ref[:, o0:o1], preferred_element_type=F32)
    att_ref[...] = jnp.dot(xn, w_ref[:, :o0], preferred_element_type=F32)
    dz_ref[...] = jnp.dot(xn, w_ref[:, o1:o2], preferred_element_type=F32)
    ba_ref[...] = jnp.dot(xn, w_ref[:, o2:], preferred_element_type=F32)

    xp_scr[TAIL:, :] = xc
    y = xp_scr[TAIL - 3:TAIL - 3 + tm, :] * cw_ref[0:1, :]
    y = y + xp_scr[TAIL - 2:TAIL - 2 + tm, :] * cw_ref[1:2, :]
    y = y + xp_scr[TAIL - 1:TAIL - 1 + tm, :] * cw_ref[2:3, :]
    y = y + xc * cw_ref[3:4, :]
    tail = xc[tm - TAIL:, :]
    xp_scr[0:TAIL, :] = tail
    tail_ref[0] = tail
    y = _silu(y)
    q = y[:, :DN_WIDTH]
    k = y[:, DN_WIDTH:2 * DN_WIDTH]
    inv_norm = lax.rsqrt(_head_sums(jnp.concatenate([q * q, k * k], axis=0), ones_ref[...]) + EPS)
    qkv_ref[:, :DN_WIDTH] = q * inv_norm[:tm] * (DN_DK ** -0.5)
    qkv_ref[:, DN_WIDTH:2 * DN_WIDTH] = k * inv_norm[tm:]
    qkv_ref[:, 2 * DN_WIDTH:] = y[:, 2 * DN_WIDTH:]


def _pair_ones():
    lane = np.arange(PAIR)
    return jnp.asarray((lane[:, None] // DN_DV == lane[None, :] // DN_DV).astype(np.float32), dtype=BF16)


def _inproj_conv(x, g, w, conv_w, seq):
    t = x.shape[0]
    tm = ROW_TM
    assert seq % tm == 0
    ones = _pair_ones()
    row = lambda n: pl.BlockSpec((tm, n), lambda i: (i, 0))
    full = lambda a: pl.BlockSpec(a.shape, lambda i: (0,) * a.ndim)
    return pl.pallas_call(
        functools.partial(_inproj_conv_kernel, tiles_per_seq=seq // tm),
        grid=(t // tm,),
        in_specs=[row(D_MODEL), full(g), full(w), full(conv_w), full(ones)],
        out_specs=[row(ATT_COLS), row(CONV_CH), row(DN_WIDTH), row(LANES),
                   pl.BlockSpec((1, TAIL, CONV_CH), lambda i: (i, 0, 0))],
        out_shape=[jax.ShapeDtypeStruct((t, n), F32) for n in (ATT_COLS, CONV_CH, DN_WIDTH, LANES)]
                  + [jax.ShapeDtypeStruct((t // tm, TAIL, CONV_CH), F32)],
        scratch_shapes=[pltpu.VMEM((TAIL + tm, CONV_CH), F32)],
        compiler_params=_params("arbitrary"),
        name="inproj_conv",
    )(x, g, w, conv_w, ones)


GROUP_ROWS = GQA * ATT_BLOCK


def _attn_prompt_kernel(cur_ref, prev_ref, bucket_ref, rb_ref, sink_ref, o_ref, bias_scr, sink_scr):
    i = pl.program_id(0)
    nseq = cur_ref.shape[0]

    @pl.when(i == 0)
    def _():
        qi = lax.broadcasted_iota(jnp.int32, (ATT_BLOCK, 2 * ATT_BLOCK), 0)
        kj = lax.broadcasted_iota(jnp.int32, (ATT_BLOCK, 2 * ATT_BLOCK), 1)
        dist = qi + ATT_BLOCK - kj
        band = jnp.logical_and(dist >= 0, dist < WINDOW)
        bucket = bucket_ref[...]
        hrow = lax.broadcasted_iota(jnp.int32, (GROUP_ROWS, 1), 0) // ATT_BLOCK
        for g in range(ATT_KV_HEADS):
            sink_col = jnp.zeros((GROUP_ROWS, 1), F32)
            for hh in range(GQA):
                h = g * GQA + hh
                bias = jnp.where(band, _bias_lookup(bucket, rb_ref, h), NEG_INF)
                bias_scr[0, g, hh * ATT_BLOCK:(hh + 1) * ATT_BLOCK, :] = bias
                bias_scr[1, g, hh * ATT_BLOCK:(hh + 1) * ATT_BLOCK, :] = jnp.where(kj >= ATT_BLOCK, bias, NEG_INF)
                sink_col = jnp.where(hrow == hh, sink_ref[h], sink_col)
            sink_scr[g] = sink_col

    first = (i == 0).astype(jnp.int32)
    probs = [(b, g) for b in range(nseq) for g in range(ATT_KV_HEADS)]
    scores = []
    for b, g in probs:
        cur = cur_ref[b]
        prev = prev_ref[b]
        q = jnp.concatenate([cur[:, (g * GQA + hh) * HEAD_DIM:(g * GQA + hh + 1) * HEAD_DIM]
                             for hh in range(GQA)], axis=0) * (HEAD_DIM ** -0.5)
        kcol = slice(ATT_WIDTH + g * HEAD_DIM, ATT_WIDTH + (g + 1) * HEAD_DIM)
        k2 = jnp.concatenate([prev[:, kcol], cur[:, kcol]], axis=0)
        scores.append(_mm_nt(q, k2) + bias_scr[first, g])
    probs_p, dens = [], []
    for (b, g), s in zip(probs, scores):
        sink = sink_scr[g]
        m = jnp.maximum(jnp.max(s, axis=-1, keepdims=True), sink)
        p = jnp.exp(s - m)
        dens.append(jnp.sum(p, axis=-1, keepdims=True) + jnp.exp(sink - m))
        probs_p.append(p)
    outs = {}
    for (b, g), p, den in zip(probs, probs_p, dens):
        vcol = slice(ATT_WIDTH + KV_WIDTH + g * HEAD_DIM, ATT_WIDTH + KV_WIDTH + (g + 1) * HEAD_DIM)
        v2 = jnp.concatenate([prev_ref[b][:, vcol], cur_ref[b][:, vcol]], axis=0)
        outs[b, g] = _mm(p, v2) / den
    for b in range(nseq):
        o_ref[b] = jnp.concatenate([outs[b, g][hh * ATT_BLOCK:(hh + 1) * ATT_BLOCK, :]
                                    for g in range(ATT_KV_HEADS) for hh in range(GQA)],
                                   axis=1).astype(o_ref.dtype)


def _attn_prompt(att, bucket, rel_bias, sink, batch, seq):
    nb = seq // ATT_BLOCK
    smem = pl.BlockSpec(memory_space=pltpu.SMEM)
    att3 = att.reshape(batch, seq, ATT_COLS)
    out = pl.pallas_call(
        _attn_prompt_kernel,
        grid=(nb,),
        in_specs=[
            pl.BlockSpec((batch, ATT_BLOCK, ATT_COLS), lambda i: (0, i, 0)),
            pl.BlockSpec((batch, ATT_BLOCK, ATT_COLS), lambda i: (0, jnp.maximum(i - 1, 0), 0)),
            pl.BlockSpec(bucket.shape, lambda i: (0, 0)),
            smem, smem,
        ],
        out_specs=pl.BlockSpec((batch, ATT_BLOCK, ATT_WIDTH), lambda i: (0, i, 0)),
        out_shape=jax.ShapeDtypeStruct((batch, seq, ATT_WIDTH), BF16),
        scratch_shapes=[pltpu.VMEM((2, ATT_KV_HEADS, GROUP_ROWS, 2 * ATT_BLOCK), F32),
                        pltpu.VMEM((ATT_KV_HEADS, GROUP_ROWS, 1), F32)],
        compiler_params=_params("arbitrary"),
        name="attn_prompt",
    )(att3, att3, bucket, rel_bias, sink)
    return out.reshape(batch * seq, ATT_WIDTH)


ATT_S_BB = 8


def _attn_sample_kernel(att_ref, ck_ref, cv_ref, bucket_ref, rb_ref, sink_ref, o_ref, ks_ref, vs_ref,
                        bias_scr, col_scr):
    hrow = lax.broadcasted_iota(jnp.int32, (ATT_HEADS, LANES), 0)
    lane = lax.broadcasted_iota(jnp.int32, (ATT_HEADS, LANES), 1)

    last = (lax.broadcasted_iota(jnp.int32, (3, WINDOW), 1) == WINDOW - 1).astype(BF16)
    is_last = lax.broadcasted_iota(jnp.int32, (KV_WIDTH, WINDOW), 1) == WINDOW - 1

    def shifted(cache_t, new_row):
        pieces = jnp.concatenate([p.astype(F32) for p in _split3(new_row)], axis=0).astype(BF16)
        col = lax.dot_general(pieces, last, (((0,), (0,)), ((), ())), preferred_element_type=F32)
        out = jnp.where(is_last, col, pltpu.roll(cache_t, WINDOW - 1, axis=1))
        return out.reshape(ATT_KV_HEADS, HEAD_DIM, WINDOW)

    for b in range(ATT_S_BB):
        row = att_ref[b:b + 1, :]
        ks_ref[b] = shifted(ck_ref[b].reshape(KV_WIDTH, WINDOW), row[:, ATT_WIDTH:ATT_WIDTH + KV_WIDTH])
        vs_ref[b] = shifted(cv_ref[b].reshape(KV_WIDTH, WINDOW), row[:, ATT_WIDTH + KV_WIDTH:])

    @pl.when(pl.program_id(0) == 0)
    def _():
        bucket = jnp.broadcast_to(bucket_ref[...], (ATT_HEADS, LANES))
        bias = jnp.zeros((ATT_HEADS, LANES), F32)
        cols = jnp.zeros((ATT_HEADS, LANES), F32)
        for h in range(ATT_HEADS):
            bias = jnp.where(hrow == h, _bias_lookup(bucket, rb_ref, h), bias)
            cols = jnp.where(jnp.logical_and(hrow == h, lane == 0), sink_ref[h], cols)
            cols = jnp.where(jnp.logical_and(hrow == h, lane == 1), rb_ref[0, h], cols)
        bias_scr[...] = jnp.where(lane >= 1, bias, NEG_INF)
        col_scr[...] = cols

    bias_c = bias_scr[...]
    sink = col_scr[:, 0:1]
    bias_n = col_scr[:, 1:2]
    same_group = (hrow // GQA) == (lane // HEAD_DIM)
    low_group = lax.broadcasted_iota(jnp.int32, (ATT_HEADS, HEAD_DIM), 0) < GQA
    rnd = lambda a: a.astype(BF16).astype(F32)
    seqs = range(ATT_S_BB)
    rows = [att_ref[b:b + 1, :] for b in seqs]
    q_bds = []
    for row in rows:
        q = row[:, :ATT_WIDTH] * (HEAD_DIM ** -0.5)
        qh = jnp.concatenate([q[:, h * HEAD_DIM:(h + 1) * HEAD_DIM] for h in range(ATT_HEADS)], axis=0)
        q_bds.append(jnp.where(same_group, jnp.concatenate([qh, qh], axis=1), 0.0))
    kv_t = lambda ref, b: ref[b].reshape(KV_WIDTH, WINDOW)
    s_cs = [_mm(q_bd, kv_t(ck_ref, b)) + bias_c for b, q_bd in zip(seqs, q_bds)]
    prs, pns = [], []
    for row, q_bd, s_c in zip(rows, q_bds, s_cs):
        kn = row[:, ATT_WIDTH:ATT_WIDTH + KV_WIDTH]
        s_n = jnp.sum(rnd(q_bd) * rnd(kn), axis=-1, keepdims=True) + bias_n
        m = jnp.maximum(jnp.maximum(jnp.max(s_c, axis=-1, keepdims=True), s_n), sink)
        p_c = jnp.exp(s_c - m)
        p_n = jnp.exp(s_n - m)
        den = jnp.sum(p_c, axis=-1, keepdims=True) + p_n + jnp.exp(sink - m)
        prs.append(p_c / den)
        pns.append(p_n / den)
    pvs = [_mm_nt(pr, kv_t(cv_ref, b)) for b, pr in zip(seqs, prs)]
    for b, row, pv, pn in zip(seqs, rows, pvs, pns):
        vn = row[:, ATT_WIDTH + KV_WIDTH:]
        o_full = pv + rnd(pn) * rnd(vn)
        o_sel = jnp.where(low_group, o_full[:, :HEAD_DIM], o_full[:, HEAD_DIM:])
        o_ref[b:b + 1, :] = jnp.concatenate([o_sel[h:h + 1, :] for h in range(ATT_HEADS)], axis=1)


def _attn_sample(att, ck, cv, bucket, rel_bias, sink):
    nseq = att.shape[0]
    smem = pl.BlockSpec(memory_space=pltpu.SMEM)
    cache = pl.BlockSpec((ATT_S_BB, ATT_KV_HEADS, HEAD_DIM, WINDOW), lambda i: (i, 0, 0, 0))
    return pl.pallas_call(
        _attn_sample_kernel,
        grid=(nseq // ATT_S_BB,),
        in_specs=[pl.BlockSpec((ATT_S_BB, ATT_COLS), lambda i: (i, 0)), cache, cache,
                  pl.BlockSpec(bucket.shape, lambda i: (0, 0)), smem, smem],
        out_specs=[pl.BlockSpec((ATT_S_BB, ATT_WIDTH), lambda i: (i, 0)), cache, cache],
        out_shape=[jax.ShapeDtypeStruct((nseq, ATT_WIDTH), F32),
                   jax.ShapeDtypeStruct(ck.shape, F32), jax.ShapeDtypeStruct(cv.shape, F32)],
        scratch_shapes=[pltpu.VMEM((ATT_HEADS, LANES), F32), pltpu.VMEM((ATT_HEADS, LANES), F32)],
        compiler_params=_params("arbitrary"),
        name="attn_sample",
    )(att, ck, cv, bucket, rel_bias, sink)


GDN_TB = 128
GDN_NC = GDN_TB // DN_CHUNK


def _gdn_gates(ba, alog, dtb):
    beta = _sigmoid(ba)
    g = -jnp.exp(alog) * _softplus(ba + dtb)
    return beta, g


def _pair_diag(x, lo):
    xb = x.astype(BF16)
    zero = jnp.zeros_like(xb)
    return jnp.concatenate([jnp.where(lo, xb, zero), jnp.where(lo, zero, xb)], axis=0)


def _gdn_prompt_kernel(qkv_ref, dz_ref, ba_ref, alog_ref, dtb_ref, dnx_ref,
                       hsum_ref, expb_ref, expg_ref, ltri_ref,
                       o_ref, s_out_ref, s_scr):
    i = pl.program_id(0)
    nb = qkv_ref.shape[0]

    @pl.when(i == 0)
    def _():
        s_scr[...] = jnp.zeros(s_scr.shape, F32)

    hsum = hsum_ref[...]
    ri = lax.broadcasted_iota(jnp.int32, (DN_CHUNK, PAIR), 0)
    ci = lax.broadcasted_iota(jnp.int32, (DN_CHUNK, PAIR), 1)
    lo = ci < DN_DK
    cj = jnp.where(lo, ci, ci - DN_DK)
    causal = ri >= cj
    strict = ri > cj
    eye = (ri == cj).astype(F32)

    def sel2(x, m):
        hi = x.astype(BF16)
        lw = (x - hi.astype(F32)).astype(BF16)
        return (jnp.dot(hi, m, preferred_element_type=F32) + jnp.dot(lw, m, preferred_element_type=F32))

    pre = []
    for b in range(nb):
        q = qkv_ref[b, :, :DN_WIDTH]
        k = qkv_ref[b, :, DN_WIDTH:2 * DN_WIDTH]
        v = qkv_ref[b, :, 2 * DN_WIDTH:]
        beta_c, g_c = _gdn_gates(ba_ref[b], alog_ref[...], dtb_ref[...])
        beta = sel2(beta_c, expb_ref[...])
        gam_c = _mm_sel_lhs(ltri_ref[...], g_c)
        gam = _mm_sel_rhs(gam_c, expg_ref[...])
        gam_t = gam_c.T
        kb = k * beta
        egam = jnp.exp(gam)
        pre.append(dict(q=q, k=k, kb=kb, vb=v * beta, qg=q * egam, wr=kb * egam, gam=gam, gam_t=gam_t))

    probs = [(b, p) for b in range(nb) for p in range(N_PAIRS)]
    pick = lambda m: jnp.where(lo, m[:DN_DK], m[DN_DK:])
    o_rows = [[] for _ in range(nb)]
    for c in range(GDN_NC):
        r0, r1 = c * DN_CHUNK, (c + 1) * DN_CHUNK
        sl = lambda name, b, p: pre[b][name][r0:r1, p * PAIR:(p + 1) * PAIR]
        raws = []
        for b, p in probs:
            k_p = sl("k", b, p)
            k_rows = jnp.concatenate([jnp.where(lo, k_p, 0.0), jnp.where(lo, 0.0, k_p)], axis=0)
            raws.append(_mm_nt(jnp.concatenate([sl("kb", b, p), sl("q", b, p)], axis=0), k_rows))
        pws, ts, qks = [], [], []
        for (b, p), raw in zip(probs, raws):
            gcol = sl("gam", b, p)
            h0 = DN_HEADS + 2 * p
            gam_t = pre[b]["gam_t"]
            grow = jnp.concatenate([gam_t[h0:h0 + 1, r0:r1], gam_t[h0 + 1:h0 + 2, r0:r1]], axis=1)
            decay = jnp.exp(jnp.where(causal, gcol - grow, NEG_INF))
            a = jnp.where(strict, raw[:DN_CHUNK] * decay, 0.0)
            qks.append(jnp.where(causal, raw[DN_CHUNK:] * decay, 0.0))
            pws.append(-a)
            ts.append(eye - a)
        pws = [_mm(pw, _pair_diag(pw, lo)) for pw in pws]
        for _ in range(4):
            rs = [_mm(jnp.concatenate([pw, t], axis=0), _pair_diag(pw, lo)) for pw, t in zip(pws, ts)]
            pws = [r[:DN_CHUNK] for r in rs]
            ts = [t + r[DN_CHUNK:] for t, r in zip(ts, rs)]
        rs = [_mm(t, _pair_diag(pw, lo)) for pw, t in zip(pws, ts)]
        ts = [t + r for t, r in zip(ts, rs)]
        sols = [_mm(t, jnp.concatenate([_pair_diag(sl("vb", b, p), lo), _pair_diag(sl("wr", b, p), lo)],
                                       axis=1)) for (b, p), t in zip(probs, ts)]
        qkuws = [_mm(qk, jnp.concatenate([_pair_diag(s[:, :PAIR], lo), _pair_diag(s[:, PAIR:], lo)], axis=1))
                 for qk, s in zip(qks, sols)]
        crosses, gls = [], []
        for (b, p), s in zip(probs, sols):
            gam_last = pre[b]["gam"][r1 - 1:r1, p * PAIR:(p + 1) * PAIR]
            kd = sl("k", b, p) * jnp.exp(gam_last - sl("gam", b, p))
            crosses.append(_mm_tn(kd, s))
            gls.append(jnp.exp(gam_last))
        lhs = [jnp.concatenate([pick(cr[:, PAIR:]), sl("qg", b, p) - qkuw[:, PAIR:]], axis=0)
               for (b, p), cr, qkuw in zip(probs, crosses, qkuws)]
        s_olds = [s_scr[b, p] for b, p in probs]
        rs = [_mm(l, _pair_diag(s_old, lo)) for l, s_old in zip(lhs, s_olds)]
        o_pairs = [[] for _ in range(nb)]
        for (b, p), r, s_old, gl, cr, qkuw in zip(probs, rs, s_olds, gls, crosses, qkuws):
            s_scr[b, p] = gl * s_old - r[:DN_DK] + pick(cr[:, :PAIR])
            o_pairs[b].append(r[DN_DK:] + qkuw[:, :PAIR])
        for b in range(nb):
            o_rows[b].append(jnp.concatenate(o_pairs[b], axis=1))

    o_all = jnp.concatenate([jnp.concatenate(rows, axis=0) for rows in o_rows], axis=0)
    inv_rms = lax.rsqrt(_head_sums(o_all * o_all, hsum) * (1.0 / DN_DV) + EPS)
    for b in range(nb):
        rows = slice(b * GDN_TB, (b + 1) * GDN_TB)
        o_ref[b] = (o_all[rows] * inv_rms[rows] * dnx_ref[...] * _silu(dz_ref[b])).astype(o_ref.dtype)

    @pl.when(i == pl.num_programs(0) - 1)
    def _():
        for b in range(nb):
            for p in range(N_PAIRS):
                s_p = s_scr[b, p]
                s_out_ref[b, 2 * p] = s_p[:, :DN_DV]
                s_out_ref[b, 2 * p + 1] = s_p[:, DN_DV:]


def _gdn_consts():
    lane = np.arange(DN_WIDTH)
    pl_lane = np.arange(PAIR)
    hsum = (pl_lane[:, None] // DN_DV == pl_lane[None, :] // DN_DV)
    src = np.arange(LANES)
    expb = (src[:, None] == lane[None, :] // DN_DV)
    expg = (src[:, None] == DN_HEADS + lane[None, :] // DN_DV)
    tok = np.arange(GDN_TB)
    ltri = np.logical_and(tok[:, None] >= tok[None, :],
                          tok[:, None] // DN_CHUNK == tok[None, :] // DN_CHUNK)
    as_bf16 = lambda m: jnp.asarray(m.astype(np.float32), dtype=BF16)
    return as_bf16(hsum), as_bf16(expb), as_bf16(expg), as_bf16(ltri)


def _gdn_prompt(xc, dz, ba, alog, dtb, dnx, batch, seq):
    nt = seq // GDN_TB
    hsum, expb, expg, ltri = _gdn_consts()
    row = lambda n: pl.BlockSpec((batch, GDN_TB, n), lambda i: (0, i, 0))
    full = lambda a: pl.BlockSpec(a.shape, lambda i: (0,) * a.ndim)
    consts = (alog, dtb, dnx, hsum, expb, expg, ltri)
    as3d = lambda a: a.reshape(batch, seq, a.shape[-1])
    o, s = pl.pallas_call(
        _gdn_prompt_kernel,
        grid=(nt,),
        in_specs=[row(CONV_CH), row(DN_WIDTH), row(LANES)] + [full(a) for a in consts],
        out_specs=[row(DN_WIDTH),
                   pl.BlockSpec((batch, DN_HEADS, DN_DK, DN_DV), lambda i: (0, 0, 0, 0))],
        out_shape=[jax.ShapeDtypeStruct((batch, seq, DN_WIDTH), BF16),
                   jax.ShapeDtypeStruct((batch, DN_HEADS, DN_DK, DN_DV), F32)],
        scratch_shapes=[pltpu.VMEM((batch, N_PAIRS, DN_DK, PAIR), F32)],
        compiler_params=_params("arbitrary"),
        name="gdn_prompt",
    )(as3d(xc), as3d(dz), as3d(ba), *consts)
    return o.reshape(batch * seq, DN_WIDTH), s


GDN_S_BB = 8


def _gdn_sample_kernel(xc_ref, dz_ref, ba_ref, sc_ref, s_ref, cw_ref, alog_ref, dtb_ref, dn_ref,
                       hsum_ref, eye_ref, hsel_ref, hrep3_ref, o_ref, s_out_ref):
    xc = xc_ref[...]
    y = sc_ref[0] * cw_ref[0:1, :]
    y = y + sc_ref[1] * cw_ref[1:2, :]
    y = y + sc_ref[2] * cw_ref[2:3, :]
    y = _silu(y + xc * cw_ref[3:4, :])
    hsum = hsum_ref[...]
    q = y[:, :DN_WIDTH]
    k = y[:, DN_WIDTH:2 * DN_WIDTH]
    v = y[:, 2 * DN_WIDTH:]
    q = q * lax.rsqrt(_mm_sel_rhs(q * q, hsum) + EPS) * (DN_DK ** -0.5)
    k = k * lax.rsqrt(_mm_sel_rhs(k * k, hsum) + EPS)
    beta_c, g_c = _gdn_gates(ba_ref[...], alog_ref[...], dtb_ref[...])
    eg_c = jnp.exp(g_c)
    eye = eye_ref[...]
    tr = lambda a: lax.dot_general(a, eye, (((0,), (0,)), ((), ())), precision=lax.Precision.HIGHEST,
                                   preferred_element_type=F32)
    gates_t = tr(jnp.concatenate([beta_c, eg_c], axis=1))
    beta_t = gates_t[:LANES]
    eg_t = gates_t[LANES:]
    dz = dz_ref[...]
    dn = dn_ref[...]
    split = lambda r: jnp.concatenate([r[:, h * DN_DV:(h + 1) * DN_DV] for h in range(DN_HEADS)], axis=0)
    own_head = hsel_ref[...].astype(F32)
    hrep3 = hrep3_ref[...]
    seqs = range(GDN_S_BB)
    dot = lambda a, b: jnp.dot(a.astype(BF16), b.astype(BF16), preferred_element_type=F32)

    def pieces(x):
        p1 = x.astype(BF16).astype(F32)
        r1 = x - p1
        p2 = r1.astype(BF16).astype(F32)
        return p1, p2, (r1 - p2).astype(BF16).astype(F32)

    heads = DN_HEADS
    k_pieces, kqs = [], []
    for b in seqs:
        kq_bd = jnp.concatenate([own_head * k[b:b + 1, :], own_head * q[b:b + 1, :]], axis=0)
        a1, a2, a3 = pieces(kq_bd)
        s1, s2, s3 = pieces(s_ref[b])
        r1 = dot(jnp.concatenate([a1, a2, a3], axis=0), s1)
        r2 = dot(jnp.concatenate([a1, a2], axis=0), s2)
        r3 = dot(a1, s3)
        n = 2 * heads
        kqs.append(((r3 + r2[n:] + r1[2 * n:]) + (r2[:n] + r1[n:2 * n])) + r1[:n])
        k_pieces.append((a1[:heads], a2[:heads], a3[:heads]))
    egs = [eg_t[DN_HEADS:2 * DN_HEADS, b:b + 1] for b in seqs]
    qks = [jnp.sum(split(q[b:b + 1, :]) * split(k[b:b + 1, :]), axis=-1, keepdims=True) for b in seqs]
    v_news = [beta_t[0:DN_HEADS, b:b + 1] * (split(v[b:b + 1, :]) - eg * kq[:heads])
              for b, eg, kq in zip(seqs, egs, kqs)]
    os_ = [eg * kq[heads:] + qk * v_new for eg, kq, qk, v_new in zip(egs, kqs, qks, v_news)]
    inv_rms = [lax.rsqrt(jnp.mean(o * o, axis=-1, keepdims=True) + EPS) for o in os_]
    for b, o, r in zip(seqs, os_, inv_rms):
        o_ref[b] = o * r * dn * _silu(split(dz[b:b + 1, :]))
    outers, egrows = [], []
    for (k1, k2, k3), v_new, eg in zip(k_pieces, v_news, egs):
        v1, v2, v3 = pieces(v_new)
        lhs = jnp.concatenate([k1, k1, k2, k1, k2, k3], axis=0).astype(BF16)
        rhs = jnp.concatenate([v1, v2, v1, v3, v2, v1], axis=0).astype(BF16)
        outers.append(lax.dot_general(lhs, rhs, (((0,), (0,)), ((), ())), preferred_element_type=F32))
        egrows.append(dot(hrep3, jnp.concatenate(pieces(jnp.broadcast_to(eg, (DN_HEADS, DN_DV))), axis=0)))
    for b, outer, egrow in zip(seqs, outers, egrows):
        s_out_ref[b] = s_ref[b] * egrow + outer


def _gdn_sample(xc, dz, ba, sconv_t, state, conv_w, alog, dtb, dn):
    nseq = xc.shape[0]
    lane = np.arange(DN_WIDTH)
    hsum = jnp.asarray((lane[:, None] // DN_DV == lane[None, :] // DN_DV).astype(np.float32), dtype=BF16)
    eye = jnp.eye(GDN_S_BB, dtype=F32)
    hsel_np = (np.arange(DN_HEADS)[:, None] == lane[None, :] // DN_DK).astype(np.float32)
    hsel = jnp.asarray(hsel_np, dtype=BF16)
    hrep3 = jnp.asarray(np.tile(hsel_np.T, (1, 3)), dtype=BF16)
    row = lambda n: pl.BlockSpec((GDN_S_BB, n), lambda i: (i, 0))
    full = lambda a: pl.BlockSpec(a.shape, lambda i: (0,) * a.ndim)
    st = pl.BlockSpec((GDN_S_BB, DN_HEADS * DN_DK, DN_DV), lambda i: (i, 0, 0))
    consts = (conv_w, alog, dtb, dn, hsum, eye, hsel, hrep3)
    return pl.pallas_call(
        _gdn_sample_kernel,
        grid=(nseq // GDN_S_BB,),
        in_specs=[row(CONV_CH), row(DN_WIDTH), row(LANES),
                  pl.BlockSpec((CONV_WIDTH - 1, GDN_S_BB, CONV_CH), lambda i: (0, i, 0)), st]
                 + [full(a) for a in consts],
        out_specs=[pl.BlockSpec((GDN_S_BB, DN_HEADS, DN_DV), lambda i: (i, 0, 0)), st],
        out_shape=[jax.ShapeDtypeStruct((nseq, DN_HEADS, DN_DV), F32),
                   jax.ShapeDtypeStruct(state.shape, F32)],
        compiler_params=_params("parallel"),
        name="gdn_sample",
    )(xc, dz, ba, sconv_t, state, *consts)


def _attn_sample_lanes_kernel(att_ref, ck_ref, cv_ref, bucket_ref, rb_ref, sink_ref, o_ref, s_scr):
    g = pl.program_id(0)
    nseq = att_ref.shape[0]
    rnd = lambda a: a.astype(BF16).astype(F32)
    att = att_ref[...]
    q_all_t = (att[:, :ATT_WIDTH] * (HEAD_DIM ** -0.5)).T
    kv_new_t = att[:, ATT_WIDTH:].T
    qsel = [jnp.where(g == 0, q_all_t[hh * HEAD_DIM:(hh + 1) * HEAD_DIM],
                      q_all_t[(GQA + hh) * HEAD_DIM:(GQA + hh + 1) * HEAD_DIM]) for hh in range(GQA)]
    qr = [rnd(q) for q in qsel]
    kn = rnd(jnp.where(g == 0, kv_new_t[0:HEAD_DIM], kv_new_t[HEAD_DIM:2 * HEAD_DIM]))
    vn = rnd(jnp.where(g == 0, kv_new_t[2 * HEAD_DIM:3 * HEAD_DIM], kv_new_t[3 * HEAD_DIM:]))

    def score_row(j, carry):
        kj = rnd(ck_ref[j, 0])
        for hh in range(GQA):
            s_scr[hh, pl.ds(j, 1), :] = jnp.sum(qr[hh] * kj, axis=0, keepdims=True)
        return carry
    lax.fori_loop(0, WINDOW, score_row, 0, unroll=2)

    bucket = bucket_ref[...]
    jrow = lax.broadcasted_iota(jnp.int32, (WINDOW, nseq), 0)
    prn = []
    for hh in range(GQA):
        h = g * GQA + hh
        bias = jnp.where(jrow >= 1, _bias_lookup(bucket, rb_ref, h), NEG_INF)
        s = s_scr[hh] + bias
        s_n = jnp.sum(qr[hh] * kn, axis=0, keepdims=True) + rb_ref[0, h]
        sink = sink_ref[h]
        m = jnp.maximum(jnp.maximum(jnp.max(s, axis=0, keepdims=True), s_n), sink)
        p = jnp.exp(s - m)
        p_n = jnp.exp(s_n - m)
        den = jnp.sum(p, axis=0, keepdims=True) + p_n + jnp.exp(sink - m)
        s_scr[hh] = rnd(p / den)
        prn.append(rnd(p_n / den))

    def value_row(j, acc):
        vj = rnd(cv_ref[j, 0])
        return tuple(acc[hh] + s_scr[hh, pl.ds(j, 1), :] * vj for hh in range(GQA))
    zero = jnp.zeros((HEAD_DIM, nseq), F32)
    acc = lax.fori_loop(0, WINDOW, value_row, (zero,) * GQA, unroll=2)
    for hh in range(GQA):
        o_ref[hh * HEAD_DIM:(hh + 1) * HEAD_DIM, :] = acc[hh] + prn[hh] * vn


def _attn_sample_lanes(att, ck_t, cv_t, rel_bias, sink):
    nseq = att.shape[0]
    assert nseq == LANES
    bucket = jnp.asarray(np.broadcast_to(_t5_bucket_np(WINDOW - np.arange(WINDOW))[:, None], (WINDOW, nseq)))
    smem = pl.BlockSpec(memory_space=pltpu.SMEM)
    cache = pl.BlockSpec((WINDOW, 1, HEAD_DIM, nseq), lambda g: (0, g, 0, 0))
    full = lambda a: pl.BlockSpec(a.shape, lambda g: (0,) * a.ndim)
    return pl.pallas_call(
        _attn_sample_lanes_kernel,
        grid=(ATT_KV_HEADS,),
        in_specs=[full(att), cache, cache, full(bucket), smem, smem],
        out_specs=pl.BlockSpec((GQA * HEAD_DIM, nseq), lambda g: (g, 0)),
        out_shape=jax.ShapeDtypeStruct((ATT_WIDTH, nseq), F32),
        scratch_shapes=[pltpu.VMEM((GQA, WINDOW, nseq), F32)],
        compiler_params=_params("arbitrary"),
        name="attn_sample_lanes",
    )(att, ck_t, cv_t, bucket, rel_bias, sink)


def _gdn_sample_front_kernel(xc_ref, dz_ref, ba_ref, sc_ref, cw_ref, alog_ref, dtb_ref, hsum_ref,
                             q_ref, k_ref, v_ref, dz_t_ref, gates_ref):
    xc = xc_ref[...]
    y = sc_ref[0] * cw_ref[0:1, :]
    y = y + sc_ref[1] * cw_ref[1:2, :]
    y = y + sc_ref[2] * cw_ref[2:3, :]
    y = _silu(y + xc * cw_ref[3:4, :])
    hsum = hsum_ref[...]
    q = y[:, :DN_WIDTH]
    k = y[:, DN_WIDTH:2 * DN_WIDTH]
    q = q * lax.rsqrt(_mm_sel_rhs(q * q, hsum) + EPS) * (DN_DK ** -0.5)
    k = k * lax.rsqrt(_mm_sel_rhs(k * k, hsum) + EPS)
    beta_c, g_c = _gdn_gates(ba_ref[...], alog_ref[...], dtb_ref[...])
    q_ref[...] = q.T
    k_ref[...] = k.T
    v_ref[...] = y[:, 2 * DN_WIDTH:].T
    dz_t_ref[...] = dz_ref[...].T
    gates_ref[0:LANES, :] = beta_c.T
    gates_ref[LANES:, :] = jnp.exp(g_c).T


def _gdn_sample_step_kernel(q_ref, k_ref, v_ref, dz_ref, gates_ref, dn_ref, s_ref, o_ref, s_out_ref):
    h = pl.program_id(0)
    beta = gates_ref[pl.ds(h, 1), :]
    eg = gates_ref[pl.ds(LANES + DN_HEADS + h, 1), :]
    q, k, v = q_ref[...], k_ref[...], v_ref[...]
    w = (k * beta) * eg
    qg = q * eg
    ws = jnp.zeros(v.shape, F32)
    qs = jnp.zeros(v.shape, F32)
    for dk in range(DN_DK):
        s_dk = s_ref[0, dk]
        ws = ws + w[dk:dk + 1, :] * s_dk
        qs = qs + qg[dk:dk + 1, :] * s_dk
    v_new = v * beta - ws
    qk = jnp.sum(q * k, axis=0, keepdims=True)
    o = qs + qk * v_new
    for dk in range(DN_DK):
        s_out_ref[0, dk] = s_ref[0, dk] * eg + k[dk:dk + 1, :] * v_new
    o = o * lax.rsqrt(jnp.mean(o * o, axis=0, keepdims=True) + EPS) * dn_ref[...]
    o_ref[...] = o * _silu(dz_ref[...])


def _gdn_sample_lanes(xc, dz, ba, sconv_t, state_t, conv_w, alog, dtb, dn):
    nseq = xc.shape[0]
    assert nseq == LANES
    lane = np.arange(DN_WIDTH)
    hsum = jnp.asarray((lane[:, None] // DN_DV == lane[None, :] // DN_DV).astype(np.float32), dtype=BF16)
    full = lambda a: pl.BlockSpec(a.shape, lambda i: (0,) * a.ndim)
    cm = jax.ShapeDtypeStruct((DN_WIDTH, nseq), F32)
    front_in = (xc, dz, ba, sconv_t, conv_w, alog, dtb, hsum)
    q_t, k_t, v_t, dz_t, gates_t = pl.pallas_call(
        _gdn_sample_front_kernel,
        grid=(1,),
        in_specs=[full(a) for a in front_in],
        out_specs=[pl.BlockSpec((DN_WIDTH, nseq), lambda i: (0, 0))] * 4
                  + [pl.BlockSpec((2 * LANES, nseq), lambda i: (0, 0))],
        out_shape=[cm, cm, cm, cm, jax.ShapeDtypeStruct((2 * LANES, nseq), F32)],
        compiler_params=_params("arbitrary"),
        name="gdn_sample_front",
    )(*front_in)
    dn_b = jnp.broadcast_to(dn.reshape(DN_DV, 1), (DN_DV, nseq))
    head = pl.BlockSpec((DN_DK, nseq), lambda h: (h, 0))
    st = pl.BlockSpec((1, DN_DK, DN_DV, nseq), lambda h: (h, 0, 0, 0))
    return pl.pallas_call(
        _gdn_sample_step_kernel,
        grid=(DN_HEADS,),
        in_specs=[head, head, head, head, full(gates_t), full(dn_b), st],
        out_specs=[head, st],
        out_shape=[cm, jax.ShapeDtypeStruct(state_t.shape, F32)],
        compiler_params=_params("parallel"),
        name="gdn_sample_step",
    )(q_t, k_t, v_t, dz_t, gates_t, dn_b, state_t)


def _route(xn, wr):
    logits = jnp.dot(xn, wr, preferred_element_type=F32)
    lane = lax.broadcasted_iota(jnp.int32, logits.shape, 1).astype(F32)
    first_at = lambda hit: jnp.min(jnp.where(hit, lane, float(LANES)), axis=-1, keepdims=True)
    glog = jnp.where(lane < N_GROUPS, logits, NEG_INF)
    gmax = jnp.max(glog, axis=-1, keepdims=True)
    gsel = first_at(glog == gmax)
    pgsel = 1.0 / jnp.sum(jnp.exp(glog - gmax), axis=-1, keepdims=True)
    lo = ROUTER_OFF + gsel * EXPERTS_PER_GROUP
    in_group = jnp.logical_and(lane >= lo, lane < lo + EXPERTS_PER_GROUP)
    elog = jnp.where(in_group, logits, NEG_INF)
    m1 = jnp.max(elog, axis=-1, keepdims=True)
    i1 = first_at(elog == m1)
    z = jnp.sum(jnp.exp(elog - m1), axis=-1, keepdims=True)
    elog2 = jnp.where(lane == i1, NEG_INF, elog)
    m2 = jnp.max(elog2, axis=-1, keepdims=True)
    i2 = first_at(elog2 == m2)
    p1 = 1.0 / z
    p2 = jnp.exp(m2 - m1) / z
    tot = p1 + p2
    return lane, i1, i2, p1 / tot * pgsel, p2 / tot * pgsel


def _outproj(x_ref, oa_ref, od_ref, wo_ref):
    return x_ref[...] + _mm(oa_ref[...], wo_ref[:ATT_WIDTH, :]) + _mm(od_ref[...], wo_ref[ATT_WIDTH:, :])


def _outproj_router_kernel(x_ref, oa_ref, od_t_ref, wo_ref, g_ref, wr_ref, h_ref, xn_ref, gate_ref):
    h = (x_ref[...] + _mm(oa_ref[...], wo_ref[:ATT_WIDTH, :])
         + _mm(od_t_ref[...].T, wo_ref[ATT_WIDTH:, :]))
    h_ref[...] = h
    xn = _rmsnorm(h, g_ref[...]).astype(BF16)
    xn_ref[...] = xn
    lane, i1, i2, g1, g2 = _route(xn, wr_ref[...])
    gate_ref[...] = jnp.where(lane == i1, g1, 0.0) + jnp.where(lane == i2, g2, 0.0)


def _outproj_router(x, oa, od_t, wo, g, wr):
    t = x.shape[0]
    tm = t
    row = lambda n: pl.BlockSpec((tm, n), lambda i: (i, 0))
    full = lambda a: pl.BlockSpec(a.shape, lambda i: (0,) * a.ndim)
    return pl.pallas_call(
        _outproj_router_kernel,
        grid=(t // tm,),
        in_specs=[row(D_MODEL), row(ATT_WIDTH), full(od_t), full(wo), full(g), full(wr)],
        out_specs=[row(D_MODEL), row(D_MODEL), row(LANES)],
        out_shape=[jax.ShapeDtypeStruct((t, D_MODEL), F32), jax.ShapeDtypeStruct((t, D_MODEL), BF16),
                   jax.ShapeDtypeStruct((t, LANES), F32)],
        compiler_params=_params("parallel"),
        name="outproj_router",
    )(x, oa, od_t, wo, g, wr)


def _moe_kernel(xn_ref, gate_ref, wg_ref, wu_ref, wd_ref, o_ref):
    e = pl.program_id(1)
    xn = xn_ref[...]
    lane = lax.broadcasted_iota(jnp.int32, gate_ref.shape, 1)
    gate = jnp.sum(jnp.where(lane == e + ROUTER_OFF, gate_ref[...], 0.0), axis=-1, keepdims=True)
    hg = jnp.dot(xn, wg_ref[...].astype(BF16), preferred_element_type=F32)
    hu = jnp.dot(xn, wu_ref[...].astype(BF16), preferred_element_type=F32)
    hm = _silu(hg) * hu * gate
    y = jnp.dot(hm.astype(BF16), wd_ref[...].astype(BF16), preferred_element_type=F32)

    @pl.when(e == 0)
    def _():
        o_ref[...] = y

    @pl.when(e > 0)
    def _():
        o_ref[...] += y


def _moe(xn, gates, wg, wu, wd):
    t = xn.shape[0]
    tm = min(t, 1024)
    return pl.pallas_call(
        _moe_kernel,
        grid=(t // tm, N_EXPERTS),
        in_specs=[pl.BlockSpec((tm, D_MODEL), lambda i, e: (i, 0)),
                  pl.BlockSpec((tm, LANES), lambda i, e: (i, 0)),
                  pl.BlockSpec((None, D_MODEL, D_EXPERT), lambda i, e: (e, 0, 0)),
                  pl.BlockSpec((None, D_MODEL, D_EXPERT), lambda i, e: (e, 0, 0)),
                  pl.BlockSpec((None, D_EXPERT, D_MODEL), lambda i, e: (e, 0, 0))],
        out_specs=pl.BlockSpec((tm, D_MODEL), lambda i, e: (i, 0)),
        out_shape=jax.ShapeDtypeStruct((t, D_MODEL), F32),
        compiler_params=_params("parallel", "arbitrary"),
        name="moe",
    )(xn, gates, wg, wu, wd)


MOE_TM = 512
POS_TM = 1024
INFO_G1, INFO_G2, INFO_E1, INFO_E2 = 0, 1, 2, 3
DMA_UNROLL = 8


def _moe_tiles(t):
    return (2 * t) // MOE_TM + N_EXPERTS


HALF = D_MODEL // 2
U32 = jnp.uint32


def _pack_rows(x):
    bits = lambda v: lax.bitcast_convert_type(v.astype(BF16).astype(F32), U32)
    return bits(x[:, HALF:]) | (bits(x[:, :HALF]) >> 16)


def _unpack_rows(w):
    lo = lax.bitcast_convert_type(w << 16, F32)
    hi = lax.bitcast_convert_type(w & jnp.uint32(0xFFFF0000), F32)
    return lo, hi


def _route_kernel(x_ref, oa_ref, od_ref, wo_ref, g_ref, wr_ref, h_ref, xn_ref, info_ref, cnt_ref, run_scr):
    h = _outproj(x_ref, oa_ref, od_ref, wo_ref)
    h_ref[...] = h
    xn = _rmsnorm(h, g_ref[...])
    xn_ref[...] = _pack_rows(xn)
    lane, i1, i2, g1, g2 = _route(xn.astype(BF16), wr_ref[...])
    info = jnp.where(lane == INFO_G1, g1, 0.0) + jnp.where(lane == INFO_G2, g2, 0.0)
    info = info + jnp.where(lane == INFO_E1, i1, 0.0) + jnp.where(lane == INFO_E2, i2, 0.0)
    info_ref[...] = info

    @pl.when(pl.program_id(0) == 0)
    def _():
        run_scr[...] = jnp.zeros(run_scr.shape, F32)
    picked = jnp.logical_or(lane == i1, lane == i2).astype(F32)
    run_scr[...] += jnp.sum(picked, axis=0, keepdims=True)
    cnt_ref[...] = run_scr[...]


def _route_sparse(x, oa, od, wo, g, wr):
    t = x.shape[0]
    tm = ROW_TM
    row = lambda n: pl.BlockSpec((tm, n), lambda i: (i, 0))
    full = lambda a: pl.BlockSpec(a.shape, lambda i: (0,) * a.ndim)
    return pl.pallas_call(
        _route_kernel,
        grid=(t // tm,),
        in_specs=[row(D_MODEL), row(ATT_WIDTH), row(DN_WIDTH), full(wo), full(g), full(wr)],
        out_specs=[row(D_MODEL), row(HALF), row(LANES), pl.BlockSpec((1, LANES), lambda i: (0, 0))],
        out_shape=[jax.ShapeDtypeStruct((t, D_MODEL), F32), jax.ShapeDtypeStruct((t, HALF), U32),
                   jax.ShapeDtypeStruct((t, LANES), F32), jax.ShapeDtypeStruct((1, LANES), F32)],
        scratch_shapes=[pltpu.VMEM((1, LANES), F32)],
        compiler_params=_params("arbitrary"),
        name="route",
    )(x, oa, od, wo, g, wr)


def _positions_kernel(info_ref, cnt_ref, ltri_ref, utri_ref, pos_ref, run_scr, off_scr):
    info = info_ref[...]
    lane = lax.broadcasted_iota(jnp.int32, info.shape, 1).astype(F32)
    hit1 = lane == info[:, INFO_E1:INFO_E1 + 1]
    hit2 = lane == info[:, INFO_E2:INFO_E2 + 1]
    onehot = jnp.logical_or(hit1, hit2).astype(F32)

    @pl.when(pl.program_id(0) == 0)
    def _():
        ln = lax.broadcasted_iota(jnp.int32, cnt_ref.shape, 1)
        is_expert = jnp.logical_and(ln >= ROUTER_OFF, ln < ROUTER_OFF + N_EXPERTS)
        tiles = jnp.where(is_expert, jnp.maximum(jnp.floor((cnt_ref[...] + (MOE_TM - 1)) * (1.0 / MOE_TM)), 1.0), 0.0)
        off_scr[...] = MOE_TM * jnp.dot(tiles.astype(BF16), utri_ref[...], preferred_element_type=F32)
        run_scr[...] = jnp.zeros(run_scr.shape, F32)

    before = (jnp.dot(ltri_ref[...], onehot.astype(BF16), preferred_element_type=F32)
              + run_scr[...] + off_scr[...])
    pos1 = jnp.sum(jnp.where(hit1, before, 0.0), axis=-1, keepdims=True)
    pos2 = jnp.sum(jnp.where(hit2, before, 0.0), axis=-1, keepdims=True)
    pos_ref[...] = (jnp.where(lane == 0, pos1, 0.0) + jnp.where(lane == 1, pos2, 0.0)).astype(jnp.int32)
    run_scr[...] += jnp.sum(onehot, axis=0, keepdims=True)


def _positions(info, cnt):
    t = info.shape[0]
    tm = min(t, POS_TM)
    tok = np.arange(tm)
    ltri = jnp.asarray((tok[:, None] > tok[None, :]).astype(np.float32), dtype=BF16)
    ln = np.arange(LANES)
    utri = jnp.asarray((ln[:, None] < ln[None, :]).astype(np.float32), dtype=BF16)
    full = lambda a: pl.BlockSpec(a.shape, lambda i: (0,) * a.ndim)
    return pl.pallas_call(
        _positions_kernel,
        grid=(t // tm,),
        in_specs=[pl.BlockSpec((tm, LANES), lambda i: (i, 0)), full(cnt), full(ltri), full(utri)],
        out_specs=pl.BlockSpec((tm, LANES), lambda i: (i, 0)),
        out_shape=jax.ShapeDtypeStruct((t, LANES), jnp.int32),
        scratch_shapes=[pltpu.VMEM((1, LANES), F32), pltpu.VMEM((1, LANES), F32)],
        compiler_params=_params("arbitrary"),
        name="positions",
    )(info, cnt, ltri, utri)


def _row_copy(src_hbm, src_row, dst_hbm, dst_row, sem):
    return pltpu.make_async_copy(src_hbm.at[pl.ds(src_row, 1)], dst_hbm.at[pl.ds(dst_row, 1)], sem)


SCATTER_SLOTS = 3


def _scatter_kernel(pos1_ref, pos2_ref, last_ref, used_ref, nt_ref, xn_hbm, zero_hbm, xs_hbm,
                    buf, lsem, sem, zsem, *, n_tok):
    max_tiles = xs_hbm.shape[0] // MOE_TM

    def zero_tile(tile):
        return pltpu.make_async_copy(zero_hbm, xs_hbm.at[pl.ds(tile * MOE_TM, MOE_TM)], zsem)

    def for_unused(fn):
        def body(tile, carry):
            fn(tile)
            return carry
        lax.fori_loop(nt_ref[0], max_tiles, body, 0)

    for e in range(N_EXPERTS):
        @pl.when(used_ref[e] > 0)
        def _():
            zero_tile(last_ref[e]).start()
    for_unused(lambda tile: zero_tile(tile).start())
    for e in range(N_EXPERTS):
        @pl.when(used_ref[e] > 0)
        def _():
            zero_tile(last_ref[e]).wait()
    for_unused(lambda tile: zero_tile(tile).wait())

    tm = buf.shape[1]
    n = n_tok // tm

    def load(i):
        return pltpu.make_async_copy(xn_hbm.at[pl.ds(i * tm, tm)], buf.at[i % SCATTER_SLOTS],
                                     lsem.at[i % SCATTER_SLOTS])

    def wait_rows(slot):
        pltpu.make_async_copy(xs_hbm.at[pl.ds(0, 2 * tm)], xs_hbm.at[pl.ds(0, 2 * tm)], sem.at[slot]).wait()

    load(0).start()
    load(1).start()

    def step(i, carry):
        slot = i % SCATTER_SLOTS
        load(i).wait()

        def body(j, c2):
            tok = i * tm + j
            src = buf.at[slot, pl.ds(j, 1)]
            pltpu.make_async_copy(src, xs_hbm.at[pl.ds(pos1_ref[tok], 1)], sem.at[slot]).start()
            pltpu.make_async_copy(src, xs_hbm.at[pl.ds(pos2_ref[tok], 1)], sem.at[slot]).start()
            return c2
        lax.fori_loop(0, tm, body, 0, unroll=DMA_UNROLL)

        @pl.when(i >= 1)
        def _():
            wait_rows((i - 1) % SCATTER_SLOTS)

        @pl.when(i + 2 < n)
        def _():
            load(i + 2).start()
        return carry
    lax.fori_loop(0, n, step, 0)
    wait_rows((n - 1) % SCATTER_SLOTS)


def _scatter_rows(xn, pos1, pos2, last_tile, used, n_tiles, n_rows):
    t = xn.shape[0]
    zero = jnp.zeros((MOE_TM, D_MODEL), F32)
    any_spec = pl.BlockSpec(memory_space=pl.ANY)
    return pl.pallas_call(
        functools.partial(_scatter_kernel, n_tok=t),
        grid_spec=pltpu.PrefetchScalarGridSpec(
            num_scalar_prefetch=5, grid=(1,),
            in_specs=[any_spec, any_spec], out_specs=any_spec,
            scratch_shapes=[pltpu.VMEM((SCATTER_SLOTS, MOE_TM, D_MODEL), F32),
                            pltpu.SemaphoreType.DMA((SCATTER_SLOTS,)),
                            pltpu.SemaphoreType.DMA((SCATTER_SLOTS,)),
                            pltpu.SemaphoreType.DMA]),
        out_shape=jax.ShapeDtypeStruct((n_rows, D_MODEL), F32),
        compiler_params=_params("arbitrary"),
        name="scatter_rows",
    )(pos1, pos2, last_tile, used, n_tiles, xn, zero)


def _experts_kernel(te_ref, tv_ref, nt_ref, xs_ref, wg_ref, wu_ref, wd_ref, xn_new_ref, gate_new_ref,
                    ys_ref, moe_new_ref, wg_s, wu_s, wd_s):
    i = pl.program_id(0)
    used = i < nt_ref[0]
    expert = te_ref[i]

    @pl.when(jnp.logical_or(i == 0, expert != te_ref[jnp.maximum(i - 1, 0)]))
    def _():
        wg_s[...] = wg_ref[...].astype(BF16)
        wu_s[...] = wu_ref[...].astype(BF16)
        wd_s[...] = wd_ref[...].astype(BF16)
        xn = xn_new_ref[...]
        lane = lax.broadcasted_iota(jnp.int32, gate_new_ref.shape, 1)
        gate = jnp.sum(jnp.where(lane == expert + ROUTER_OFF, gate_new_ref[...], 0.0), axis=-1, keepdims=True)
        hg = jnp.dot(xn, wg_s[...], preferred_element_type=F32)
        hu = jnp.dot(xn, wu_s[...], preferred_element_type=F32)
        hm = _silu(hg) * hu * gate
        y = jnp.dot(hm.astype(BF16), wd_s[...], preferred_element_type=F32)

        @pl.when(i == 0)
        def _():
            moe_new_ref[...] = y

        @pl.when(i > 0)
        def _():
            moe_new_ref[...] += y

    @pl.when(used)
    def _():
        row = lax.broadcasted_iota(jnp.int32, xs_ref.shape, 0)
        x_lo, x_hi = _unpack_rows(jnp.where(row < tv_ref[i], xs_ref[...], jnp.uint32(0)))
        x_lo = x_lo.astype(BF16)
        x_hi = x_hi.astype(BF16)
        up = lambda w_s: (jnp.dot(x_lo, w_s[:HALF, :], preferred_element_type=F32)
                          + jnp.dot(x_hi, w_s[HALF:, :], preferred_element_type=F32))
        hm = (_silu(up(wg_s)) * up(wu_s)).astype(BF16)
        ys_ref[...] = _pack_rows(jnp.dot(hm, wd_s[...], preferred_element_type=F32))

    @pl.when(jnp.logical_not(used))
    def _():
        ys_ref[...] = jnp.zeros(ys_ref.shape, U32)


def _experts(xs, tile_expert, tile_valid, n_tiles, wg, wu, wd, xn_new, gate_new):
    max_tiles = xs.shape[0] // MOE_TM
    rows = pl.BlockSpec((MOE_TM, HALF), lambda i, te, tv, nt: (i, 0))
    wspec = lambda shape: pl.BlockSpec((None,) + shape, lambda i, te, tv, nt: (te[i], 0, 0))
    full = lambda a: pl.BlockSpec(a.shape, lambda i, te, tv, nt: (0,) * a.ndim)
    return pl.pallas_call(
        _experts_kernel,
        grid_spec=pltpu.PrefetchScalarGridSpec(
            num_scalar_prefetch=3, grid=(max_tiles,),
            in_specs=[rows, wspec((D_MODEL, D_EXPERT)), wspec((D_MODEL, D_EXPERT)),
                      wspec((D_EXPERT, D_MODEL)), full(xn_new), full(gate_new)],
            out_specs=[rows, pl.BlockSpec(xn_new.shape, lambda i, te, tv, nt: (0, 0))],
            scratch_shapes=[pltpu.VMEM((D_MODEL, D_EXPERT), BF16), pltpu.VMEM((D_MODEL, D_EXPERT), BF16),
                            pltpu.VMEM((D_EXPERT, D_MODEL), BF16)]),
        out_shape=[jax.ShapeDtypeStruct(xs.shape, U32), jax.ShapeDtypeStruct(xn_new.shape, F32)],
        compiler_params=_params("arbitrary"),
        name="experts",
    )(tile_expert, tile_valid, n_tiles, xs, wg, wu, wd, xn_new, gate_new)


def _ple_gather_kernel(pos1_ref, pos2_ref, h_ref, info_ref, p_ref, wpp_ref, wpg_ref, gp_ref, gf_ref,
                       ys_hbm, y_ref, ybuf, sem):
    i = pl.program_id(0)
    n = pl.num_programs(0)
    tm = h_ref.shape[0]

    def issue(tile, slot):
        def body(j, carry):
            tok = tile * tm + j
            pltpu.make_async_copy(ys_hbm.at[pl.ds(pos1_ref[tok], 1)], ybuf.at[slot, 0, pl.ds(j, 1)],
                                  sem.at[slot]).start()
            pltpu.make_async_copy(ys_hbm.at[pl.ds(pos2_ref[tok], 1)], ybuf.at[slot, 1, pl.ds(j, 1)],
                                  sem.at[slot]).start()
            return carry
        lax.fori_loop(0, tm, body, 0, unroll=DMA_UNROLL)

    @pl.when(i == 0)
    def _():
        issue(0, 0)

    @pl.when(i + 1 < n)
    def _():
        issue(i + 1, (i + 1) % 2)

    slot = i % 2
    pltpu.make_async_copy(ybuf.at[slot], ybuf.at[slot], sem.at[slot]).wait()
    info = info_ref[...]
    moe = info[:, INFO_G1:INFO_G1 + 1] * ybuf[slot, 0] + info[:, INFO_G2:INFO_G2 + 1] * ybuf[slot, 1]
    h = h_ref[...] + moe
    hn = _rmsnorm(h, gp_ref[...])
    h = h + _mm(p_ref[...], wpp_ref[...]) * _sigmoid(_mm(hn, wpg_ref[...]))
    y_ref[...] = _rmsnorm(h, gf_ref[...])


def _ple_gather(h, info, p, ys, pos1, pos2, wpp, wpg, gp, gf):
    t = h.shape[0]
    tm = 256
    row = lambda n: pl.BlockSpec((tm, n), lambda i, p1, p2: (i, 0))
    full = lambda a: pl.BlockSpec(a.shape, lambda i, p1, p2: (0,) * a.ndim)
    return pl.pallas_call(
        _ple_gather_kernel,
        grid_spec=pltpu.PrefetchScalarGridSpec(
            num_scalar_prefetch=2, grid=(t // tm,),
            in_specs=[row(D_MODEL), row(LANES), row(PLE_DIM), full(wpp), full(wpg), full(gp), full(gf),
                      pl.BlockSpec(memory_space=pl.ANY)],
            out_specs=row(D_MODEL),
            scratch_shapes=[pltpu.VMEM((2, 2, tm, D_MODEL), F32), pltpu.SemaphoreType.DMA((2,))]),
        out_shape=jax.ShapeDtypeStruct((t, D_MODEL), F32),
        compiler_params=_params("arbitrary"),
        name="ple_gather",
    )(pos1, pos2, h, info, p, wpp, wpg, gp, gf, ys)


SC_IDX = 128
SC_ROWS = 64
SC_WORKERS = 32


def _sc_mesh():
    return plsc.VectorSubcoreMesh(core_axis_name="c", subcore_axis_name="s")


def _sc_windows(t, fn):
    per_worker = t // SC_WORKERS
    worker = lax.axis_index(("c", "s"))

    @pl.loop(0, per_worker // SC_IDX)
    def _(w):
        fn(worker * per_worker + w * SC_IDX)


def _sc_scatter_rows(xn, pos1, pos2, n_rows):
    t, d = xn.shape
    assert t % (SC_WORKERS * SC_IDX) == 0
    idx_t = pltpu.VMEM((1, SC_IDX), jnp.int32)

    @pl.kernel(out_type=jax.ShapeDtypeStruct((n_rows, d), xn.dtype), mesh=_sc_mesh(),
               scratch_types=[idx_t, idx_t, pltpu.VMEM((SC_ROWS, d), xn.dtype)])
    def scatter(x_hbm, p1_hbm, p2_hbm, o_hbm, i1_v, i2_v, buf):
        def window(base):
            pltpu.sync_copy(p1_hbm.at[:, pl.ds(base, SC_IDX)], i1_v)
            pltpu.sync_copy(p2_hbm.at[:, pl.ds(base, SC_IDX)], i2_v)
            for k in range(SC_IDX // SC_ROWS):
                pltpu.sync_copy(x_hbm.at[pl.ds(base + k * SC_ROWS, SC_ROWS)], buf)
                pltpu.sync_copy(buf, o_hbm.at[i1_v.at[0, pl.ds(k * SC_ROWS, SC_ROWS)]])
                pltpu.sync_copy(buf, o_hbm.at[i2_v.at[0, pl.ds(k * SC_ROWS, SC_ROWS)]])
        _sc_windows(t, window)

    return scatter(xn, pos1.reshape(1, t), pos2.reshape(1, t))


def _sc_gather_rows(ys, pos1, pos2):
    t = pos1.shape[0]
    d = ys.shape[1]
    assert t % (SC_WORKERS * SC_IDX) == 0
    idx_t = pltpu.VMEM((1, SC_IDX), jnp.int32)
    out = jax.ShapeDtypeStruct((t, d), ys.dtype)

    @pl.kernel(out_type=(out, out), mesh=_sc_mesh(),
               scratch_types=[idx_t, idx_t, pltpu.VMEM((SC_ROWS, d), ys.dtype)])
    def gather(y_hbm, p1_hbm, p2_hbm, o1_hbm, o2_hbm, i1_v, i2_v, buf):
        def window(base):
            pltpu.sync_copy(p1_hbm.at[:, pl.ds(base, SC_IDX)], i1_v)
            pltpu.sync_copy(p2_hbm.at[:, pl.ds(base, SC_IDX)], i2_v)
            for k in range(SC_IDX // SC_ROWS):
                rows = pl.ds(base + k * SC_ROWS, SC_ROWS)
                pltpu.sync_copy(y_hbm.at[i1_v.at[0, pl.ds(k * SC_ROWS, SC_ROWS)]], buf)
                pltpu.sync_copy(buf, o1_hbm.at[rows])
                pltpu.sync_copy(y_hbm.at[i2_v.at[0, pl.ds(k * SC_ROWS, SC_ROWS)]], buf)
                pltpu.sync_copy(buf, o2_hbm.at[rows])
        _sc_windows(t, window)

    return gather(ys, pos1.reshape(1, t), pos2.reshape(1, t))


def _ple_sparse_kernel(h_ref, info_ref, y1_ref, y2_ref, p_ref, wpp_ref, wpg_ref, gp_ref, gf_ref, y_ref):
    info = info_ref[...]
    g1 = info[:, INFO_G1:INFO_G1 + 1]
    g2 = info[:, INFO_G2:INFO_G2 + 1]
    y1_lo, y1_hi = _unpack_rows(y1_ref[...])
    y2_lo, y2_hi = _unpack_rows(y2_ref[...])
    moe = jnp.concatenate([g1 * y1_lo + g2 * y2_lo, g1 * y1_hi + g2 * y2_hi], axis=1)
    h = h_ref[...] + moe
    hn = _rmsnorm(h, gp_ref[...])
    h = h + _mm(p_ref[...], wpp_ref[...]) * _sigmoid(_mm(hn, wpg_ref[...]))
    y_ref[...] = _rmsnorm(h, gf_ref[...])


def _ple_sparse(h, info, y1, y2, p, wpp, wpg, gp, gf):
    t = h.shape[0]
    tm = ROW_TM
    row = lambda n: pl.BlockSpec((tm, n), lambda i: (i, 0))
    full = lambda a: pl.BlockSpec(a.shape, lambda i: (0,) * a.ndim)
    return pl.pallas_call(
        _ple_sparse_kernel,
        grid=(t // tm,),
        in_specs=[row(D_MODEL), row(LANES), row(HALF), row(HALF), row(PLE_DIM),
                  full(wpp), full(wpg), full(gp), full(gf)],
        out_specs=row(D_MODEL),
        out_shape=jax.ShapeDtypeStruct((t, D_MODEL), F32),
        compiler_params=_params("parallel"),
        name="ple_sparse",
    )(h, info, y1, y2, p, wpp, wpg, gp, gf)


def _tile_tables(cnt, max_tiles):
    tiles_e = jnp.maximum((cnt + (MOE_TM - 1)) // MOE_TM, 1)
    ends = jnp.cumsum(tiles_e)
    n_tiles = ends[-1]
    tile = jnp.arange(max_tiles, dtype=jnp.int32)
    idx = jnp.minimum(tile, n_tiles - 1)
    tile_expert = jnp.sum((idx[:, None] >= ends[None, :]).astype(jnp.int32), axis=1)
    mine = tile_expert[:, None] == jnp.arange(N_EXPERTS, dtype=jnp.int32)[None, :]
    of_mine = lambda v: jnp.sum(jnp.where(mine, v[None, :], 0), axis=1)
    valid = jnp.clip(of_mine(cnt) - (idx - of_mine(ends - tiles_e)) * MOE_TM, 0, MOE_TM)
    tile_valid = jnp.where(tile < n_tiles, valid, 0).astype(jnp.int32)
    return (tile_expert, tile_valid, n_tiles.reshape(1), (ends - 1).astype(jnp.int32),
            tiles_e.astype(jnp.int32))


def _ple_final_kernel(h_ref, m_ref, p_ref, wpp_ref, wpg_ref, gp_ref, gf_ref, y_ref):
    h = h_ref[...] + m_ref[...]
    hn = _rmsnorm(h, gp_ref[...])
    h = h + _mm(p_ref[...], wpp_ref[...]) * _sigmoid(_mm(hn, wpg_ref[...]))
    y_ref[...] = _rmsnorm(h, gf_ref[...])


def _ple_final(h, m, p, wpp, wpg, gp, gf):
    t = h.shape[0]
    tm = min(t, 256)
    row = lambda n: pl.BlockSpec((tm, n), lambda i: (i, 0))
    full = lambda a: pl.BlockSpec(a.shape, lambda i: (0,) * a.ndim)
    return pl.pallas_call(
        _ple_final_kernel,
        grid=(t // tm,),
        in_specs=[row(D_MODEL), row(D_MODEL), row(PLE_DIM), full(wpp), full(wpg), full(gp), full(gf)],
        out_specs=row(D_MODEL),
        out_shape=jax.ShapeDtypeStruct((t, D_MODEL), F32),
        compiler_params=_params("parallel"),
        name="ple_final",
    )(h, m, p, wpp, wpg, gp, gf)


def kernel(x_prompt, x_sample, p_prompt, p_sample, cache_k, cache_v, state_conv, state_S, rel_bias, norm_mix, w_in, att_sink, conv_w, dn_A_log, dn_dt_bias, dn_norm, w_out, norm_ffn, w_router_group, w_router_expert, w_gate, w_up, w_down, w_ple_proj, w_ple_gate, norm_ple, norm_final):
    batch, seq, _ = x_prompt.shape
    nseq = x_sample.shape[0]
    assert x_sample.shape[1] == 1 and norm_mix.shape[0] == 1 and cache_k.shape[2] == WINDOW
    assert seq % GDN_TB == 0 and seq % ATT_BLOCK == 0

    wi = w_in[0]
    o_db = ATT_COLS + CONV_CH
    w_in_re = jnp.concatenate(
        [wi[:, :o_db], wi[:, o_db + 2 * DN_HEADS:], wi[:, o_db:o_db + 2 * DN_HEADS],
         jnp.zeros((D_MODEL, LANES - 2 * DN_HEADS), F32)], axis=1).astype(BF16)
    row = lambda a: a.reshape(1, -1).astype(F32)
    pad_lanes = lambda a, off: jnp.zeros((1, LANES), F32).at[0, off:off + a.shape[0]].set(a)
    alog = pad_lanes(dn_A_log[0], DN_HEADS)
    dtb = pad_lanes(dn_dt_bias[0], DN_HEADS)
    dnx = jnp.tile(dn_norm[0], DN_HEADS).reshape(1, DN_WIDTH)
    w_router = jnp.concatenate(
        [w_router_group[0], w_router_expert[0],
         jnp.zeros((D_MODEL, LANES - N_GROUPS - N_EXPERTS), F32)], axis=1).astype(BF16)
    wo = w_out[0].astype(BF16)
    wg, wu, wd = w_gate[0], w_up[0], w_down[0]
    wpp, wpg = w_ple_proj[0].astype(BF16), w_ple_gate[0].astype(BF16)
    sink = att_sink[0]

    qi = np.arange(ATT_BLOCK)[:, None]
    kj = np.arange(2 * ATT_BLOCK)[None, :]
    bucket_p = jnp.asarray(_t5_bucket_np(qi + ATT_BLOCK - kj))
    bucket_s = jnp.asarray(_t5_bucket_np(WINDOW - np.arange(WINDOW)[None, :]))

    xp = x_prompt.reshape(batch * seq, D_MODEL)
    att_p, qkv_p, dz_p, ba_p, xc_tails = _inproj_conv(xp, row(norm_mix[0]), w_in_re, conv_w[0], seq)
    o_att_p = _attn_prompt(att_p, bucket_p, rel_bias, sink, batch, seq)
    o_dn_p, s_p = _gdn_prompt(qkv_p, dz_p, ba_p, alog, dtb, dnx, batch, seq)
    h1, xn2, info, cnt = _route_sparse(xp, o_att_p, o_dn_p, wo, row(norm_ffn[0]), w_router)
    pos = _positions(info, cnt)
    pos1, pos2 = pos[:, 0], pos[:, 1]
    max_tiles = _moe_tiles(batch * seq)
    cnt_e = cnt[0, ROUTER_OFF:ROUTER_OFF + N_EXPERTS].astype(jnp.int32)
    tile_expert, tile_valid, n_tiles, last_tile, used = _tile_tables(cnt_e, max_tiles)
    xs_sorted = _sc_scatter_rows(xn2, pos1, pos2, max_tiles * MOE_TM)

    xs = x_sample.reshape(nseq, D_MODEL)
    att_s, xc_s, dz_s, ba_s = _inproj(xs, row(norm_mix[0]), w_in_re)
    ck_t = jnp.transpose(cache_k[0], (0, 2, 3, 1))
    cv_t = jnp.transpose(cache_v[0], (0, 2, 3, 1))
    o_att_s, ks_t, vs_t = _attn_sample(att_s, ck_t, cv_t, bucket_s, rel_bias, sink)
    sconv_t = jnp.swapaxes(state_conv[0], 0, 1)
    o_dn_s_t, s_s_t = _gdn_sample_lanes(xc_s, dz_s, ba_s, sconv_t, jnp.transpose(state_S[0], (1, 2, 3, 0)),
                                        conv_w[0], alog, dtb, dn_norm[0])
    s_s = jnp.transpose(s_s_t, (3, 0, 1, 2))

    h1_s, xn2_s, gates_s = _outproj_router(xs, o_att_s, o_dn_s_t, wo, row(norm_ffn[0]), w_router)

    ys, moe_s = _experts(xs_sorted, tile_expert, tile_valid, n_tiles, wg, wu, wd, xn2_s, gates_s)
    y1, y2 = _sc_gather_rows(ys, pos1, pos2)
    y_s = _ple_final(h1_s, moe_s, p_sample[0].reshape(nseq, PLE_DIM), wpp, wpg, row(norm_ple[0]),
                     row(norm_final))
    y_p = _ple_sparse(h1, info, y1, y2, p_prompt[0].reshape(batch * seq, PLE_DIM),
                      wpp, wpg, row(norm_ple[0]), row(norm_final))

    att_p3 = att_p.reshape(batch, seq, ATT_COLS)
    kv_shape = (1, batch, WINDOW, ATT_KV_HEADS, HEAD_DIM)
    k_p = att_p3[:, seq - WINDOW:, ATT_WIDTH:ATT_WIDTH + KV_WIDTH].reshape(kv_shape)
    v_p = att_p3[:, seq - WINDOW:, ATT_WIDTH + KV_WIDTH:].reshape(kv_shape)
    conv_p = xc_tails.reshape(batch, -1, TAIL, CONV_CH)[:, -1, TAIL - (CONV_WIDTH - 1):][None]
    k_s = jnp.transpose(ks_t, (0, 3, 1, 2))[None]
    v_s = jnp.transpose(vs_t, (0, 3, 1, 2))[None]
    conv_s = jnp.concatenate([state_conv[0][:, 1:], xc_s[:, None, :]], axis=1)[None]
    return (y_p.reshape(batch, seq, D_MODEL), y_s.reshape(nseq, 1, D_MODEL),
            k_p, v_p, conv_p, s_p[None], k_s, v_s, conv_s, s_s[None])
```

```python
import functools
import math

import numpy as np
import jax
import jax.numpy as jnp
from jax import lax
from jax.experimental import pallas as pl
from jax.experimental.pallas import tpu as pltpu
from jax.experimental.pallas import tpu_sc as plsc

F32 = jnp.float32
BF16 = jnp.bfloat16

D_MODEL = 1024
ATT_HEADS = 8
ATT_KV_HEADS = 2
HEAD_DIM = 64
GQA = ATT_HEADS // ATT_KV_HEADS
WINDOW = 128
ATT_BLOCK = 128
N_BUCKETS = 32
DN_HEADS = 8
DN_DK = 64
DN_DV = 64
CONV_WIDTH = 4
DN_CHUNK = 64
ATT_WIDTH = ATT_HEADS * HEAD_DIM
KV_WIDTH = ATT_KV_HEADS * HEAD_DIM
DN_WIDTH = DN_HEADS * DN_DV
CONV_CH = 3 * DN_WIDTH
N_GROUPS = 4
EXPERTS_PER_GROUP = 8
N_EXPERTS = N_GROUPS * EXPERTS_PER_GROUP
D_EXPERT = 256
PLE_DIM = 256
EPS = 1e-6
NEG_INF = float("-inf")

ATT_COLS = ATT_WIDTH + 2 * KV_WIDTH
LANES = 128
IN_COLS = ATT_COLS + CONV_CH + DN_WIDTH + LANES
ROUTER_OFF = N_GROUPS
VMEM_LIMIT = 48 * 1024 * 1024
ROW_TM = 512


def _params(*sem):
    return pltpu.CompilerParams(dimension_semantics=sem, vmem_limit_bytes=VMEM_LIMIT)


def _mm(a, b):
    return jnp.dot(a.astype(BF16), b.astype(BF16), preferred_element_type=F32)


def _mm_nt(a, b):
    return lax.dot_general(a.astype(BF16), b.astype(BF16), (((1,), (1,)), ((), ())),
                           preferred_element_type=F32)


def _mm_tn(a, b):
    return lax.dot_general(a.astype(BF16), b.astype(BF16), (((0,), (0,)), ((), ())),
                           preferred_element_type=F32)


def _split3(x):
    h1 = x.astype(BF16)
    r1 = x - h1.astype(F32)
    h2 = r1.astype(BF16)
    h3 = (r1 - h2.astype(F32)).astype(BF16)
    return h1, h2, h3


def _mm_sel_rhs(x, sel):
    h1, h2, h3 = _split3(x)
    d = lambda h: jnp.dot(h, sel, preferred_element_type=F32)
    return d(h1) + d(h2) + d(h3)


def _mm_sel_lhs(sel, x):
    h1, h2, h3 = _split3(x)
    d = lambda h: jnp.dot(sel, h, preferred_element_type=F32)
    return d(h1) + d(h2) + d(h3)


def _mm3(a, b):
    ah = a.astype(BF16)
    al = (a - ah.astype(F32)).astype(BF16)
    bh = b.astype(BF16)
    bl = (b - bh.astype(F32)).astype(BF16)
    d = lambda u, v: jnp.dot(u, v, preferred_element_type=F32)
    return d(ah, bh) + d(ah, bl) + d(al, bh)


def _sigmoid(x):
    return 1.0 / (1.0 + jnp.exp(-x))


def _silu(x):
    return x * _sigmoid(x)


def _softplus(x):
    return jnp.maximum(x, 0.0) + jnp.log1p(jnp.exp(-jnp.abs(x)))


def _rmsnorm(x, g):
    return x * lax.rsqrt(jnp.mean(x * x, axis=-1, keepdims=True) + EPS) * g


def _t5_bucket_np(dist):
    max_exact = N_BUCKETS // 2
    d = np.maximum(dist, 0)
    ratio = (np.log(np.maximum(d, 1).astype(np.float32) / np.float32(max_exact))
             / np.float32(math.log(WINDOW / max_exact))).astype(np.float32)
    large = np.minimum(max_exact + (ratio * np.float32(N_BUCKETS - max_exact)).astype(np.int32),
                       N_BUCKETS - 1)
    return np.where(d < max_exact, d, large).astype(np.int32)


def _bias_lookup(bucket, rb_ref, h):
    acc = jnp.zeros(bucket.shape, F32)
    for t in range(N_BUCKETS):
        acc = jnp.where(bucket == t, rb_ref[t, h], acc)
    return acc


def _inproj_kernel(x_ref, g_ref, w_ref, att_ref, xc_ref, dz_ref, ba_ref):
    xn = _rmsnorm(x_ref[...], g_ref[...]).astype(BF16)
    o0, o1, o2 = ATT_COLS, ATT_COLS + CONV_CH, ATT_COLS + CONV_CH + DN_WIDTH
    att_ref[...] = jnp.dot(xn, w_ref[:, :o0], preferred_element_type=F32)
    xc_ref[...] = jnp.dot(xn, w_ref[:, o0:o1], preferred_element_type=F32)
    dz_ref[...] = jnp.dot(xn, w_ref[:, o1:o2], preferred_element_type=F32)
    ba_ref[...] = jnp.dot(xn, w_ref[:, o2:], preferred_element_type=F32)


def _inproj(x, g, w):
    t = x.shape[0]
    tm = min(t, ROW_TM)
    row = lambda n: pl.BlockSpec((tm, n), lambda i: (i, 0))
    full = lambda a: pl.BlockSpec(a.shape, lambda i: (0,) * a.ndim)
    return pl.pallas_call(
        _inproj_kernel,
        grid=(t // tm,),
        in_specs=[row(D_MODEL), full(g), full(w)],
        out_specs=[row(ATT_COLS), row(CONV_CH), row(DN_WIDTH), row(LANES)],
        out_shape=[jax.ShapeDtypeStruct((t, n), F32) for n in (ATT_COLS, CONV_CH, DN_WIDTH, LANES)],
        compiler_params=_params("parallel"),
        name="inproj",
    )(x, g, w)


TAIL = 8
PAIR = 2 * DN_DK
N_PAIRS = DN_WIDTH // PAIR


def _head_sums(z, pair_ones):
    hi = z.astype(BF16)
    lw = (z - hi.astype(F32)).astype(BF16)
    d = lambda a, p: jnp.dot(a[:, p * PAIR:(p + 1) * PAIR], pair_ones, preferred_element_type=F32)
    return jnp.concatenate([d(hi, p) + d(lw, p) for p in range(N_PAIRS)], axis=1)


def _inproj_conv_kernel(x_ref, g_ref, w_ref, cw_ref, ones_ref, att_ref, qkv_ref, dz_ref, ba_ref, tail_ref,
                        xp_scr, *, tiles_per_seq):
    tm = x_ref.shape[0]

    @pl.when(pl.program_id(0) % tiles_per_seq == 0)
    def _():
        xp_scr[0:TAIL, :] = jnp.zeros((TAIL, CONV_CH), F32)

    xn = _rmsnorm(x_ref[...], g_ref[...]).astype(BF16)
    o0, o1, o2 = ATT_COLS, ATT_COLS + CONV_CH, ATT_COLS + CONV_CH + DN_WIDTH
    xc = jnp.dot(xn, w_ref[:, o0:o1], preferred_element_type=F32)
    att_ref[...] = jnp.dot(xn, w_ref[:, :o0], preferred_element_type=F32)
    dz_ref[...] = jnp.dot(xn, w_ref[:, o1:o2], preferred_element_type=F32)
    ba_ref[...] = jnp.dot(xn, w_ref[:, o2:], preferred_element_type=F32)

    xp_scr[TAIL:, :] = xc
    y = xp_scr[TAIL - 3:TAIL - 3 + tm, :] * cw_ref[0:1, :]
    y = y + xp_scr[TAIL - 2:TAIL - 2 + tm, :] * cw_ref[1:2, :]
    y = y + xp_scr[TAIL - 1:TAIL - 1 + tm, :] * cw_ref[2:3, :]
    y = y + xc * cw_ref[3:4, :]
    tail = xc[tm - TAIL:, :]
    xp_scr[0:TAIL, :] = tail
    tail_ref[0] = tail
    y = _silu(y)
    q = y[:, :DN_WIDTH]
    k = y[:, DN_WIDTH:2 * DN_WIDTH]
    inv_norm = lax.rsqrt(_head_sums(jnp.concatenate([q * q, k * k], axis=0), ones_ref[...]) + EPS)
    qkv_ref[:, :DN_WIDTH] = q * inv_norm[:tm] * (DN_DK ** -0.5)
    qkv_ref[:, DN_WIDTH:2 * DN_WIDTH] = k * inv_norm[tm:]
    qkv_ref[:, 2 * DN_WIDTH:] = y[:, 2 * DN_WIDTH:]


def _pair_ones():
    lane = np.arange(PAIR)
    return jnp.asarray((lane[:, None] // DN_DV == lane[None, :] // DN_DV).astype(np.float32), dtype=BF16)


def _inproj_conv(x, g, w, conv_w, seq):
    t = x.shape[0]
    tm = ROW_TM
    assert seq % tm == 0
    ones = _pair_ones()
    row = lambda n: pl.BlockSpec((tm, n), lambda i: (i, 0))
    full = lambda a: pl.BlockSpec(a.shape, lambda i: (0,) * a.ndim)
    return pl.pallas_call(
        functools.partial(_inproj_conv_kernel, tiles_per_seq=seq // tm),
        grid=(t // tm,),
        in_specs=[row(D_MODEL), full(g), full(w), full(conv_w), full(ones)],
        out_specs=[row(ATT_COLS), row(CONV_CH), row(DN_WIDTH), row(LANES),
                   pl.BlockSpec((1, TAIL, CONV_CH), lambda i: (i, 0, 0))],
        out_shape=[jax.ShapeDtypeStruct((t, n), F32) for n in (ATT_COLS, CONV_CH, DN_WIDTH, LANES)]
                  + [jax.ShapeDtypeStruct((t // tm, TAIL, CONV_CH), F32)],
        scratch_shapes=[pltpu.VMEM((TAIL + tm, CONV_CH), F32)],
        compiler_params=_params("arbitrary"),
        name="inproj_conv",
    )(x, g, w, conv_w, ones)


GROUP_ROWS = GQA * ATT_BLOCK


def _attn_prompt_kernel(cur_ref, prev_ref, bucket_ref, rb_ref, sink_ref, o_ref, bias_scr, sink_scr):
    i = pl.program_id(0)
    nseq = cur_ref.shape[0]

    @pl.when(i == 0)
    def _():
        qi = lax.broadcasted_iota(jnp.int32, (ATT_BLOCK, 2 * ATT_BLOCK), 0)
        kj = lax.broadcasted_iota(jnp.int32, (ATT_BLOCK, 2 * ATT_BLOCK), 1)
        dist = qi + ATT_BLOCK - kj
        band = jnp.logical_and(dist >= 0, dist < WINDOW)
        bucket = bucket_ref[...]
        hrow = lax.broadcasted_iota(jnp.int32, (GROUP_ROWS, 1), 0) // ATT_BLOCK
        for g in range(ATT_KV_HEADS):
            sink_col = jnp.zeros((GROUP_ROWS, 1), F32)
            for hh in range(GQA):
                h = g * GQA + hh
                bias = jnp.where(band, _bias_lookup(bucket, rb_ref, h), NEG_INF)
                bias_scr[0, g, hh * ATT_BLOCK:(hh + 1) * ATT_BLOCK, :] = bias
                bias_scr[1, g, hh * ATT_BLOCK:(hh + 1) * ATT_BLOCK, :] = jnp.where(kj >= ATT_BLOCK, bias, NEG_INF)
                sink_col = jnp.where(hrow == hh, sink_ref[h], sink_col)
            sink_scr[g] = sink_col

    first = (i == 0).astype(jnp.int32)
    probs = [(b, g) for b in range(nseq) for g in range(ATT_KV_HEADS)]
    scores = []
    for b, g in probs:
        cur = cur_ref[b]
        prev = prev_ref[b]
        q = jnp.concatenate([cur[:, (g * GQA + hh) * HEAD_DIM:(g * GQA + hh + 1) * HEAD_DIM]
                             for hh in range(GQA)], axis=0) * (HEAD_DIM ** -0.5)
        kcol = slice(ATT_WIDTH + g * HEAD_DIM, ATT_WIDTH + (g + 1) * HEAD_DIM)
        k2 = jnp.concatenate([prev[:, kcol], cur[:, kcol]], axis=0)
        scores.append(_mm_nt(q, k2) + bias_scr[first, g])
    probs_p, dens = [], []
    for (b, g), s in zip(probs, scores):
        sink = sink_scr[g]
        m = jnp.maximum(jnp.max(s, axis=-1, keepdims=True), sink)
        p = jnp.exp(s - m)
        dens.append(jnp.sum(p, axis=-1, keepdims=True) + jnp.exp(sink - m))
        probs_p.append(p)
    outs = {}
    for (b, g), p, den in zip(probs, probs_p, dens):
        vcol = slice(ATT_WIDTH + KV_WIDTH + g * HEAD_DIM, ATT_WIDTH + KV_WIDTH + (g + 1) * HEAD_DIM)
        v2 = jnp.concatenate([prev_ref[b][:, vcol], cur_ref[b][:, vcol]], axis=0)
        outs[b, g] = _mm(p, v2) / den
    for b in range(nseq):
        o_ref[b] = jnp.concatenate([outs[b, g][hh * ATT_BLOCK:(hh + 1) * ATT_BLOCK, :]
                                    for g in range(ATT_KV_HEADS) for hh in range(GQA)],
                                   axis=1).astype(o_ref.dtype)


def _attn_prompt(att, bucket, rel_bias, sink, batch, seq):
    nb = seq // ATT_BLOCK
    smem = pl.BlockSpec(memory_space=pltpu.SMEM)
    att3 = att.reshape(batch, seq, ATT_COLS)
    out = pl.pallas_call(
        _attn_prompt_kernel,
        grid=(nb,),
        in_specs=[
            pl.BlockSpec((batch, ATT_BLOCK, ATT_COLS), lambda i: (0, i, 0)),
            pl.BlockSpec((batch, ATT_BLOCK, ATT_COLS), lambda i: (0, jnp.maximum(i - 1, 0), 0)),
            pl.BlockSpec(bucket.shape, lambda i: (0, 0)),
            smem, smem,
        ],
        out_specs=pl.BlockSpec((batch, ATT_BLOCK, ATT_WIDTH), lambda i: (0, i, 0)),
        out_shape=jax.ShapeDtypeStruct((batch, seq, ATT_WIDTH), BF16),
        scratch_shapes=[pltpu.VMEM((2, ATT_KV_HEADS, GROUP_ROWS, 2 * ATT_BLOCK), F32),
                        pltpu.VMEM((ATT_KV_HEADS, GROUP_ROWS, 1), F32)],
        compiler_params=_params("arbitrary"),
        name="attn_prompt",
    )(att3, att3, bucket, rel_bias, sink)
    return out.reshape(batch * seq, ATT_WIDTH)


ATT_S_BB = 8


def _attn_sample_kernel(att_ref, ck_ref, cv_ref, bucket_ref, rb_ref, sink_ref, o_ref, ks_ref, vs_ref,
                        bias_scr, col_scr):
    hrow = lax.broadcasted_iota(jnp.int32, (ATT_HEADS, LANES), 0)
    lane = lax.broadcasted_iota(jnp.int32, (ATT_HEADS, LANES), 1)

    last = (lax.broadcasted_iota(jnp.int32, (3, WINDOW), 1) == WINDOW - 1).astype(BF16)
    is_last = lax.broadcasted_iota(jnp.int32, (KV_WIDTH, WINDOW), 1) == WINDOW - 1

    def shifted(cache_t, new_row):
        pieces = jnp.concatenate([p.astype(F32) for p in _split3(new_row)], axis=0).astype(BF16)
        col = lax.dot_general(pieces, last, (((0,), (0,)), ((), ())), preferred_element_type=F32)
        out = jnp.where(is_last, col, pltpu.roll(cache_t, WINDOW - 1, axis=1))
        return out.reshape(ATT_KV_HEADS, HEAD_DIM, WINDOW)

    for b in range(ATT_S_BB):
        row = att_ref[b:b + 1, :]
        ks_ref[b] = shifted(ck_ref[b].reshape(KV_WIDTH, WINDOW), row[:, ATT_WIDTH:ATT_WIDTH + KV_WIDTH])
        vs_ref[b] = shifted(cv_ref[b].reshape(KV_WIDTH, WINDOW), row[:, ATT_WIDTH + KV_WIDTH:])

    @pl.when(pl.program_id(0) == 0)
    def _():
        bucket = jnp.broadcast_to(bucket_ref[...], (ATT_HEADS, LANES))
        bias = jnp.zeros((ATT_HEADS, LANES), F32)
        cols = jnp.zeros((ATT_HEADS, LANES), F32)
        for h in range(ATT_HEADS):
            bias = jnp.where(hrow == h, _bias_lookup(bucket, rb_ref, h), bias)
            cols = jnp.where(jnp.logical_and(hrow == h, lane == 0), sink_ref[h], cols)
            cols = jnp.where(jnp.logical_and(hrow == h, lane == 1), rb_ref[0, h], cols)
        bias_scr[...] = jnp.where(lane >= 1, bias, NEG_INF)
        col_scr[...] = cols

    bias_c = bias_scr[...]
    sink = col_scr[:, 0:1]
    bias_n = col_scr[:, 1:2]
    same_group = (hrow // GQA) == (lane // HEAD_DIM)
    low_group = lax.broadcasted_iota(jnp.int32, (ATT_HEADS, HEAD_DIM), 0) < GQA
    rnd = lambda a: a.astype(BF16).astype(F32)
    seqs = range(ATT_S_BB)
    rows = [att_ref[b:b + 1, :] for b in seqs]
    q_bds = []
    for row in rows:
        q = row[:, :ATT_WIDTH] * (HEAD_DIM ** -0.5)
        qh = jnp.concatenate([q[:, h * HEAD_DIM:(h + 1) * HEAD_DIM] for h in range(ATT_HEADS)], axis=0)
        q_bds.append(jnp.where(same_group, jnp.concatenate([qh, qh], axis=1), 0.0))
    kv_t = lambda ref, b: ref[b].reshape(KV_WIDTH, WINDOW)
    s_cs = [_mm(q_bd, kv_t(ck_ref, b)) + bias_c for b, q_bd in zip(seqs, q_bds)]
    prs, pns = [], []
    for row, q_bd, s_c in zip(rows, q_bds, s_cs):
        kn = row[:, ATT_WIDTH:ATT_WIDTH + KV_WIDTH]
        s_n = jnp.sum(rnd(q_bd) * rnd(kn), axis=-1, keepdims=True) + bias_n
        m = jnp.maximum(jnp.maximum(jnp.max(s_c, axis=-1, keepdims=True), s_n), sink)
        p_c = jnp.exp(s_c - m)
        p_n = jnp.exp(s_n - m)
        den = jnp.sum(p_c, axis=-1, keepdims=True) + p_n + jnp.exp(sink - m)
        prs.append(p_c / den)
        pns.append(p_n / den)
    pvs = [_mm_nt(pr, kv_t(cv_ref, b)) for b, pr in zip(seqs, prs)]
    for b, row, pv, pn in zip(seqs, rows, pvs, pns):
        vn = row[:, ATT_WIDTH + KV_WIDTH:]
        o_full = pv + rnd(pn) * rnd(vn)
        o_sel = jnp.where(low_group, o_full[:, :HEAD_DIM], o_full[:, HEAD_DIM:])
        o_ref[b:b + 1, :] = jnp.concatenate([o_sel[h:h + 1, :] for h in range(ATT_HEADS)], axis=1)


def _attn_sample(att, ck, cv, bucket, rel_bias, sink):
    nseq = att.shape[0]
    smem = pl.BlockSpec(memory_space=pltpu.SMEM)
    cache = pl.BlockSpec((ATT_S_BB, ATT_KV_HEADS, HEAD_DIM, WINDOW), lambda i: (i, 0, 0, 0))
    return pl.pallas_call(
        _attn_sample_kernel,
        grid=(nseq // ATT_S_BB,),
        in_specs=[pl.BlockSpec((ATT_S_BB, ATT_COLS), lambda i: (i, 0)), cache, cache,
                  pl.BlockSpec(bucket.shape, lambda i: (0, 0)), smem, smem],
        out_specs=[pl.BlockSpec((ATT_S_BB, ATT_WIDTH), lambda i: (i, 0)), cache, cache],
        out_shape=[jax.ShapeDtypeStruct((nseq, ATT_WIDTH), F32),
                   jax.ShapeDtypeStruct(ck.shape, F32), jax.ShapeDtypeStruct(cv.shape, F32)],
        scratch_shapes=[pltpu.VMEM((ATT_HEADS, LANES), F32), pltpu.VMEM((ATT_HEADS, LANES), F32)],
        compiler_params=_params("arbitrary"),
        name="attn_sample",
    )(att, ck, cv, bucket, rel_bias, sink)


GDN_TB = 128
GDN_NC = GDN_TB // DN_CHUNK


def _gdn_gates(ba, alog, dtb):
    beta = _sigmoid(ba)
    g = -jnp.exp(alog) * _softplus(ba + dtb)
    return beta, g


def _pair_diag(x, lo):
    xb = x.astype(BF16)
    zero = jnp.zeros_like(xb)
    return jnp.concatenate([jnp.where(lo, xb, zero), jnp.where(lo, zero, xb)], axis=0)


def _gdn_prompt_kernel(qkv_ref, dz_ref, ba_ref, alog_ref, dtb_ref, dnx_ref,
                       hsum_ref, expb_ref, expg_ref, ltri_ref,
                       o_ref, s_out_ref, s_scr):
    i = pl.program_id(0)
    nb = qkv_ref.shape[0]

    @pl.when(i == 0)
    def _():
        s_scr[...] = jnp.zeros(s_scr.shape, F32)

    hsum = hsum_ref[...]
    ri = lax.broadcasted_iota(jnp.int32, (DN_CHUNK, PAIR), 0)
    ci = lax.broadcasted_iota(jnp.int32, (DN_CHUNK, PAIR), 1)
    lo = ci < DN_DK
    cj = jnp.where(lo, ci, ci - DN_DK)
    causal = ri >= cj
    strict = ri > cj
    eye = (ri == cj).astype(F32)

    def sel2(x, m):
        hi = x.astype(BF16)
        lw = (x - hi.astype(F32)).astype(BF16)
        return (jnp.dot(hi, m, preferred_element_type=F32) + jnp.dot(lw, m, preferred_element_type=F32))

    pre = []
    for b in range(nb):
        q = qkv_ref[b, :, :DN_WIDTH]
        k = qkv_ref[b, :, DN_WIDTH:2 * DN_WIDTH]
        v = qkv_ref[b, :, 2 * DN_WIDTH:]
        beta_c, g_c = _gdn_gates(ba_ref[b], alog_ref[...], dtb_ref[...])
        beta = sel2(beta_c, expb_ref[...])
        gam_c = _mm_sel_lhs(ltri_ref[...], g_c)
        gam = _mm_sel_rhs(gam_c, expg_ref[...])
        gam_t = gam_c.T
        kb = k * beta
        egam = jnp.exp(gam)
        pre.append(dict(q=q, k=k, kb=kb, vb=v * beta, qg=q * egam, wr=kb * egam, gam=gam, gam_t=gam_t))

    probs = [(b, p) for b in range(nb) for p in range(N_PAIRS)]
    pick = lambda m: jnp.where(lo, m[:DN_DK], m[DN_DK:])
    o_rows = [[] for _ in range(nb)]
    for c in range(GDN_NC):
        r0, r1 = c * DN_CHUNK, (c + 1) * DN_CHUNK
        sl = lambda name, b, p: pre[b][name][r0:r1, p * PAIR:(p + 1) * PAIR]
        raws = []
        for b, p in probs:
            k_p = sl("k", b, p)
            k_rows = jnp.concatenate([jnp.where(lo, k_p, 0.0), jnp.where(lo, 0.0, k_p)], axis=0)
            raws.append(_mm_nt(jnp.concatenate([sl("kb", b, p), sl("q", b, p)], axis=0), k_rows))
        pws, ts, qks = [], [], []
        for (b, p), raw in zip(probs, raws):
            gcol = sl("gam", b, p)
            h0 = DN_HEADS + 2 * p
            gam_t = pre[b]["gam_t"]
            grow = jnp.concatenate([gam_t[h0:h0 + 1, r0:r1], gam_t[h0 + 1:h0 + 2, r0:r1]], axis=1)
            decay = jnp.exp(jnp.where(causal, gcol - grow, NEG_INF))
            a = jnp.where(strict, raw[:DN_CHUNK] * decay, 0.0)
            qks.append(jnp.where(causal, raw[DN_CHUNK:] * decay, 0.0))
            pws.append(-a)
            ts.append(eye - a)
        pws = [_mm(pw, _pair_diag(pw, lo)) for pw in pws]
        for _ in range(4):
            rs = [_mm(jnp.concatenate([pw, t], axis=0), _pair_diag(pw, lo)) for pw, t in zip(pws, ts)]
            pws = [r[:DN_CHUNK] for r in rs]
            ts = [t + r[DN_CHUNK:] for t, r in zip(ts, rs)]
        rs = [_mm(t, _pair_diag(pw, lo)) for pw, t in zip(pws, ts)]
        ts = [t + r for t, r in zip(ts, rs)]
        sols = [_mm(t, jnp.concatenate([_pair_diag(sl("vb", b, p), lo), _pair_diag(sl("wr", b, p), lo)],
                                       axis=1)) for (b, p), t in zip(probs, ts)]
        qkuws = [_mm(qk, jnp.concatenate([_pair_diag(s[:, :PAIR], lo), _pair_diag(s[:, PAIR:], lo)], axis=1))
                 for qk, s in zip(qks, sols)]
        crosses, gls = [], []
        for (b, p), s in zip(probs, sols):
            gam_last = pre[b]["gam"][r1 - 1:r1, p * PAIR:(p + 1) * PAIR]
            kd = sl("k", b, p) * jnp.exp(gam_last - sl("gam", b, p))
            crosses.append(_mm_tn(kd, s))
            gls.append(jnp.exp(gam_last))
        lhs = [jnp.concatenate([pick(cr[:, PAIR:]), sl("qg", b, p) - qkuw[:, PAIR:]], axis=0)
               for (b, p), cr, qkuw in zip(probs, crosses, qkuws)]
        s_olds = [s_scr[b, p] for b, p in probs]
        rs = [_mm(l, _pair_diag(s_old, lo)) for l, s_old in zip(lhs, s_olds)]
        o_pairs = [[] for _ in range(nb)]
        for (b, p), r, s_old, gl, cr, qkuw in zip(probs, rs, s_olds, gls, crosses, qkuws):
            s_scr[b, p] = gl * s_old - r[:DN_DK] + pick(cr[:, :PAIR])
            o_pairs[b].append(r[DN_DK:] + qkuw[:, :PAIR])
        for b in range(nb):
            o_rows[b].append(jnp.concatenate(o_pairs[b], axis=1))

    o_all = jnp.concatenate([jnp.concatenate(rows, axis=0) for rows in o_rows], axis=0)
    inv_rms = lax.rsqrt(_head_sums(o_all * o_all, hsum) * (1.0 / DN_DV) + EPS)
    for b in range(nb):
        rows = slice(b * GDN_TB, (b + 1) * GDN_TB)
        o_ref[b] = (o_all[rows] * inv_rms[rows] * dnx_ref[...] * _silu(dz_ref[b])).astype(o_ref.dtype)

    @pl.when(i == pl.num_programs(0) - 1)
    def _():
        for b in range(nb):
            for p in range(N_PAIRS):
                s_p = s_scr[b, p]
                s_out_ref[b, 2 * p] = s_p[:, :DN_DV]
                s_out_ref[b, 2 * p + 1] = s_p[:, DN_DV:]


def _gdn_consts():
    lane = np.arange(DN_WIDTH)
    pl_lane = np.arange(PAIR)
    hsum = (pl_lane[:, None] // DN_DV == pl_lane[None, :] // DN_DV)
    src = np.arange(LANES)
    expb = (src[:, None] == lane[None, :] // DN_DV)
    expg = (src[:, None] == DN_HEADS + lane[None, :] // DN_DV)
    tok = np.arange(GDN_TB)
    ltri = np.logical_and(tok[:, None] >= tok[None, :],
                          tok[:, None] // DN_CHUNK == tok[None, :] // DN_CHUNK)
    as_bf16 = lambda m: jnp.asarray(m.astype(np.float32), dtype=BF16)
    return as_bf16(hsum), as_bf16(expb), as_bf16(expg), as_bf16(ltri)


def _gdn_prompt(xc, dz, ba, alog, dtb, dnx, batch, seq):
    nt = seq // GDN_TB
    hsum, expb, expg, ltri = _gdn_consts()
    row = lambda n: pl.BlockSpec((batch, GDN_TB, n), lambda i: (0, i, 0))
    full = lambda a: pl.BlockSpec(a.shape, lambda i: (0,) * a.ndim)
    consts = (alog, dtb, dnx, hsum, expb, expg, ltri)
    as3d = lambda a: a.reshape(batch, seq, a.shape[-1])
    o, s = pl.pallas_call(
        _gdn_prompt_kernel,
        grid=(nt,),
        in_specs=[row(CONV_CH), row(DN_WIDTH), row(LANES)] + [full(a) for a in consts],
        out_specs=[row(DN_WIDTH),
                   pl.BlockSpec((batch, DN_HEADS, DN_DK, DN_DV), lambda i: (0, 0, 0, 0))],
        out_shape=[jax.ShapeDtypeStruct((batch, seq, DN_WIDTH), BF16),
                   jax.ShapeDtypeStruct((batch, DN_HEADS, DN_DK, DN_DV), F32)],
        scratch_shapes=[pltpu.VMEM((batch, N_PAIRS, DN_DK, PAIR), F32)],
        compiler_params=_params("arbitrary"),
        name="gdn_prompt",
    )(as3d(xc), as3d(dz), as3d(ba), *consts)
    return o.reshape(batch * seq, DN_WIDTH), s


GDN_S_BB = 8


def _gdn_sample_kernel(xc_ref, dz_ref, ba_ref, sc_ref, s_ref, cw_ref, alog_ref, dtb_ref, dn_ref,
                       hsum_ref, eye_ref, hsel_ref, hrep3_ref, o_ref, s_out_ref):
    xc = xc_ref[...]
    y = sc_ref[0] * cw_ref[0:1, :]
    y = y + sc_ref[1] * cw_ref[1:2, :]
    y = y + sc_ref[2] * cw_ref[2:3, :]
    y = _silu(y + xc * cw_ref[3:4, :])
    hsum = hsum_ref[...]
    q = y[:, :DN_WIDTH]
    k = y[:, DN_WIDTH:2 * DN_WIDTH]
    v = y[:, 2 * DN_WIDTH:]
    q = q * lax.rsqrt(_mm_sel_rhs(q * q, hsum) + EPS) * (DN_DK ** -0.5)
    k = k * lax.rsqrt(_mm_sel_rhs(k * k, hsum) + EPS)
    beta_c, g_c = _gdn_gates(ba_ref[...], alog_ref[...], dtb_ref[...])
    eg_c = jnp.exp(g_c)
    eye = eye_ref[...]
    tr = lambda a: lax.dot_general(a, eye, (((0,), (0,)), ((), ())), precision=lax.Precision.HIGHEST,
                                   preferred_element_type=F32)
    gates_t = tr(jnp.concatenate([beta_c, eg_c], axis=1))
    beta_t = gates_t[:LANES]
    eg_t = gates_t[LANES:]
    dz = dz_ref[...]
    dn = dn_ref[...]
    split = lambda r: jnp.concatenate([r[:, h * DN_DV:(h + 1) * DN_DV] for h in range(DN_HEADS)], axis=0)
    own_head = hsel_ref[...].astype(F32)
    hrep3 = hrep3_ref[...]
    seqs = range(GDN_S_BB)
    dot = lambda a, b: jnp.dot(a.astype(BF16), b.astype(BF16), preferred_element_type=F32)

    def pieces(x):
        p1 = x.astype(BF16).astype(F32)
        r1 = x - p1
        p2 = r1.astype(BF16).astype(F32)
        return p1, p2, (r1 - p2).astype(BF16).astype(F32)

    heads = DN_HEADS
    k_pieces, kqs = [], []
    for b in seqs:
        kq_bd = jnp.concatenate([own_head * k[b:b + 1, :], own_head * q[b:b + 1, :]], axis=0)
        a1, a2, a3 = pieces(kq_bd)
        s1, s2, s3 = pieces(s_ref[b])
        r1 = dot(jnp.concatenate([a1, a2, a3], axis=0), s1)
        r2 = dot(jnp.concatenate([a1, a2], axis=0), s2)
        r3 = dot(a1, s3)
        n = 2 * heads
        kqs.append(((r3 + r2[n:] + r1[2 * n:]) + (r2[:n] + r1[n:2 * n])) + r1[:n])
        k_pieces.append((a1[:heads], a2[:heads], a3[:heads]))
    egs = [eg_t[DN_HEADS:2 * DN_HEADS, b:b + 1] for b in seqs]
    qks = [jnp.sum(split(q[b:b + 1, :]) * split(k[b:b + 1, :]), axis=-1, keepdims=True) for b in seqs]
    v_news = [beta_t[0:DN_HEADS, b:b + 1] * (split(v[b:b + 1, :]) - eg * kq[:heads])
              for b, eg, kq in zip(seqs, egs, kqs)]
    os_ = [eg * kq[heads:] + qk * v_new for eg, kq, qk, v_new in zip(egs, kqs, qks, v_news)]
    inv_rms = [lax.rsqrt(jnp.mean(o * o, axis=-1, keepdims=True) + EPS) for o in os_]
    for b, o, r in zip(seqs, os_, inv_rms):
        o_ref[b] = o * r * dn * _silu(split(dz[b:b + 1, :]))
    outers, egrows = [], []
    for (k1, k2, k3), v_new, eg in zip(k_pieces, v_news, egs):
        v1, v2, v3 = pieces(v_new)
        lhs = jnp.concatenate([k1, k1, k2, k1, k2, k3], axis=0).astype(BF16)
        rhs = jnp.concatenate([v1, v2, v1, v3, v2, v1], axis=0).astype(BF16)
        outers.append(lax.dot_general(lhs, rhs, (((0,), (0,)), ((), ())), preferred_element_type=F32))
        egrows.append(dot(hrep3, jnp.concatenate(pieces(jnp.broadcast_to(eg, (DN_HEADS, DN_DV))), axis=0)))
    for b, outer, egrow in zip(seqs, outers, egrows):
        s_out_ref[b] = s_ref[b] * egrow + outer


def _gdn_sample(xc, dz, ba, sconv_t, state, conv_w, alog, dtb, dn):
    nseq = xc.shape[0]
    lane = np.arange(DN_WIDTH)
    hsum = jnp.asarray((lane[:, None] // DN_DV == lane[None, :] // DN_DV).astype(np.float32), dtype=BF16)
    eye = jnp.eye(GDN_S_BB, dtype=F32)
    hsel_np = (np.arange(DN_HEADS)[:, None] == lane[None, :] // DN_DK).astype(np.float32)
    hsel = jnp.asarray(hsel_np, dtype=BF16)
    hrep3 = jnp.asarray(np.tile(hsel_np.T, (1, 3)), dtype=BF16)
    row = lambda n: pl.BlockSpec((GDN_S_BB, n), lambda i: (i, 0))
    full = lambda a: pl.BlockSpec(a.shape, lambda i: (0,) * a.ndim)
    st = pl.BlockSpec((GDN_S_BB, DN_HEADS * DN_DK, DN_DV), lambda i: (i, 0, 0))
    consts = (conv_w, alog, dtb, dn, hsum, eye, hsel, hrep3)
    return pl.pallas_call(
        _gdn_sample_kernel,
        grid=(nseq // GDN_S_BB,),
        in_specs=[row(CONV_CH), row(DN_WIDTH), row(LANES),
                  pl.BlockSpec((CONV_WIDTH - 1, GDN_S_BB, CONV_CH), lambda i: (0, i, 0)), st]
                 + [full(a) for a in consts],
        out_specs=[pl.BlockSpec((GDN_S_BB, DN_HEADS, DN_DV), lambda i: (i, 0, 0)), st],
        out_shape=[jax.ShapeDtypeStruct((nseq, DN_HEADS, DN_DV), F32),
                   jax.ShapeDtypeStruct(state.shape, F32)],
        compiler_params=_params("parallel"),
        name="gdn_sample",
    )(xc, dz, ba, sconv_t, state, *consts)


def _attn_sample_lanes_kernel(att_ref, ck_ref, cv_ref, bucket_ref, rb_ref, sink_ref, o_ref, s_scr):
    g = pl.program_id(0)
    nseq = att_ref.shape[0]
    rnd = lambda a: a.astype(BF16).astype(F32)
    att = att_ref[...]
    q_all_t = (att[:, :ATT_WIDTH] * (HEAD_DIM ** -0.5)).T
    kv_new_t = att[:, ATT_WIDTH:].T
    qsel = [jnp.where(g == 0, q_all_t[hh * HEAD_DIM:(hh + 1) * HEAD_DIM],
                      q_all_t[(GQA + hh) * HEAD_DIM:(GQA + hh + 1) * HEAD_DIM]) for hh in range(GQA)]
    qr = [rnd(q) for q in qsel]
    kn = rnd(jnp.where(g == 0, kv_new_t[0:HEAD_DIM], kv_new_t[HEAD_DIM:2 * HEAD_DIM]))
    vn = rnd(jnp.where(g == 0, kv_new_t[2 * HEAD_DIM:3 * HEAD_DIM], kv_new_t[3 * HEAD_DIM:]))

    def score_row(j, carry):
        kj = rnd(ck_ref[j, 0])
        for hh in range(GQA):
            s_scr[hh, pl.ds(j, 1), :] = jnp.sum(qr[hh] * kj, axis=0, keepdims=True)
        return carry
    lax.fori_loop(0, WINDOW, score_row, 0, unroll=2)

    bucket = bucket_ref[...]
    jrow = lax.broadcasted_iota(jnp.int32, (WINDOW, nseq), 0)
    prn = []
    for hh in range(GQA):
        h = g * GQA + hh
        bias = jnp.where(jrow >= 1, _bias_lookup(bucket, rb_ref, h), NEG_INF)
        s = s_scr[hh] + bias
        s_n = jnp.sum(qr[hh] * kn, axis=0, keepdims=True) + rb_ref[0, h]
        sink = sink_ref[h]
        m = jnp.maximum(jnp.maximum(jnp.max(s, axis=0, keepdims=True), s_n), sink)
        p = jnp.exp(s - m)
        p_n = jnp.exp(s_n - m)
        den = jnp.sum(p, axis=0, keepdims=True) + p_n + jnp.exp(sink - m)
        s_scr[hh] = rnd(p / den)
        prn.append(rnd(p_n / den))

    def value_row(j, acc):
        vj = rnd(cv_ref[j, 0])
        return tuple(acc[hh] + s_scr[hh, pl.ds(j, 1), :] * vj for hh in range(GQA))
    zero = jnp.zeros((HEAD_DIM, nseq), F32)
    acc = lax.fori_loop(0, WINDOW, value_row, (zero,) * GQA, unroll=2)
    for hh in range(GQA):
        o_ref[hh * HEAD_DIM:(hh + 1) * HEAD_DIM, :] = acc[hh] + prn[hh] * vn


def _attn_sample_lanes(att, ck_t, cv_t, rel_bias, sink):
    nseq = att.shape[0]
    assert nseq == LANES
    bucket = jnp.asarray(np.broadcast_to(_t5_bucket_np(WINDOW - np.arange(WINDOW))[:, None], (WINDOW, nseq)))
    smem = pl.BlockSpec(memory_space=pltpu.SMEM)
    cache = pl.BlockSpec((WINDOW, 1, HEAD_DIM, nseq), lambda g: (0, g, 0, 0))
    full = lambda a: pl.BlockSpec(a.shape, lambda g: (0,) * a.ndim)
    return pl.pallas_call(
        _attn_sample_lanes_kernel,
        grid=(ATT_KV_HEADS,),
        in_specs=[full(att), cache, cache, full(bucket), smem, smem],
        out_specs=pl.BlockSpec((GQA * HEAD_DIM, nseq), lambda g: (g, 0)),
        out_shape=jax.ShapeDtypeStruct((ATT_WIDTH, nseq), F32),
        scratch_shapes=[pltpu.VMEM((GQA, WINDOW, nseq), F32)],
        compiler_params=_params("arbitrary"),
        name="attn_sample_lanes",
    )(att, ck_t, cv_t, bucket, rel_bias, sink)


def _gdn_sample_front_kernel(xc_ref, dz_ref, ba_ref, sc_ref, cw_ref, alog_ref, dtb_ref, hsum_ref,
                             q_ref, k_ref, v_ref, dz_t_ref, gates_ref):
    xc = xc_ref[...]
    y = sc_ref[0] * cw_ref[0:1, :]
    y = y + sc_ref[1] * cw_ref[1:2, :]
    y = y + sc_ref[2] * cw_ref[2:3, :]
    y = _silu(y + xc * cw_ref[3:4, :])
    hsum = hsum_ref[...]
    q = y[:, :DN_WIDTH]
    k = y[:, DN_WIDTH:2 * DN_WIDTH]
    q = q * lax.rsqrt(_mm_sel_rhs(q * q, hsum) + EPS) * (DN_DK ** -0.5)
    k = k * lax.rsqrt(_mm_sel_rhs(k * k, hsum) + EPS)
    beta_c, g_c = _gdn_gates(ba_ref[...], alog_ref[...], dtb_ref[...])
    q_ref[...] = q.T
    k_ref[...] = k.T
    v_ref[...] = y[:, 2 * DN_WIDTH:].T
    dz_t_ref[...] = dz_ref[...].T
    gates_ref[0:LANES, :] = beta_c.T
    gates_ref[LANES:, :] = jnp.exp(g_c).T


def _gdn_sample_step_kernel(q_ref, k_ref, v_ref, dz_ref, gates_ref, dn_ref, s_ref, o_ref, s_out_ref):
    h = pl.program_id(0)
    beta = gates_ref[pl.ds(h, 1), :]
    eg = gates_ref[pl.ds(LANES + DN_HEADS + h, 1), :]
    q, k, v = q_ref[...], k_ref[...], v_ref[...]
    w = (k * beta) * eg
    qg = q * eg
    ws = jnp.zeros(v.shape, F32)
    qs = jnp.zeros(v.shape, F32)
    for dk in range(DN_DK):
        s_dk = s_ref[0, dk]
        ws = ws + w[dk:dk + 1, :] * s_dk
        qs = qs + qg[dk:dk + 1, :] * s_dk
    v_new = v * beta - ws
    qk = jnp.sum(q * k, axis=0, keepdims=True)
    o = qs + qk * v_new
    for dk in range(DN_DK):
        s_out_ref[0, dk] = s_ref[0, dk] * eg + k[dk:dk + 1, :] * v_new
    o = o * lax.rsqrt(jnp.mean(o * o, axis=0, keepdims=True) + EPS) * dn_ref[...]
    o_ref[...] = o * _silu(dz_ref[...])


def _gdn_sample_lanes(xc, dz, ba, sconv_t, state_t, conv_w, alog, dtb, dn):
    nseq = xc.shape[0]
    assert nseq == LANES
    lane = np.arange(DN_WIDTH)
    hsum = jnp.asarray((lane[:, None] // DN_DV == lane[None, :] // DN_DV).astype(np.float32), dtype=BF16)
    full = lambda a: pl.BlockSpec(a.shape, lambda i: (0,) * a.ndim)
    cm = jax.ShapeDtypeStruct((DN_WIDTH, nseq), F32)
    front_in = (xc, dz, ba, sconv_t, conv_w, alog, dtb, hsum)
    q_t, k_t, v_t, dz_t, gates_t = pl.pallas_call(
        _gdn_sample_front_kernel,
        grid=(1,),
        in_specs=[full(a) for a in front_in],
        out_specs=[pl.BlockSpec((DN_WIDTH, nseq), lambda i: (0, 0))] * 4
                  + [pl.BlockSpec((2 * LANES, nseq), lambda i: (0, 0))],
        out_shape=[cm, cm, cm, cm, jax.ShapeDtypeStruct((2 * LANES, nseq), F32)],
        compiler_params=_params("arbitrary"),
        name="gdn_sample_front",
    )(*front_in)
    dn_b = jnp.broadcast_to(dn.reshape(DN_DV, 1), (DN_DV, nseq))
    head = pl.BlockSpec((DN_DK, nseq), lambda h: (h, 0))
    st = pl.BlockSpec((1, DN_DK, DN_DV, nseq), lambda h: (h, 0, 0, 0))
    return pl.pallas_call(
        _gdn_sample_step_kernel,
        grid=(DN_HEADS,),
        in_specs=[head, head, head, head, full(gates_t), full(dn_b), st],
        out_specs=[head, st],
        out_shape=[cm, jax.ShapeDtypeStruct(state_t.shape, F32)],
        compiler_params=_params("parallel"),
        name="gdn_sample_step",
    )(q_t, k_t, v_t, dz_t, gates_t, dn_b, state_t)


def _route(xn, wr):
    logits = jnp.dot(xn, wr, preferred_element_type=F32)
    lane = lax.broadcasted_iota(jnp.int32, logits.shape, 1).astype(F32)
    first_at = lambda hit: jnp.min(jnp.where(hit, lane, float(LANES)), axis=-1, keepdims=True)
    glog = jnp.where(lane < N_GROUPS, logits, NEG_INF)
    gmax = jnp.max(glog, axis=-1, keepdims=True)
    gsel = first_at(glog == gmax)
    pgsel = 1.0 / jnp.sum(jnp.exp(glog - gmax), axis=-1, keepdims=True)
    lo = ROUTER_OFF + gsel * EXPERTS_PER_GROUP
    in_group = jnp.logical_and(lane >= lo, lane < lo + EXPERTS_PER_GROUP)
    elog = jnp.where(in_group, logits, NEG_INF)
    m1 = jnp.max(elog, axis=-1, keepdims=True)
    i1 = first_at(elog == m1)
    z = jnp.sum(jnp.exp(elog - m1), axis=-1, keepdims=True)
    elog2 = jnp.where(lane == i1, NEG_INF, elog)
    m2 = jnp.max(elog2, axis=-1, keepdims=True)
    i2 = first_at(elog2 == m2)
    p1 = 1.0 / z
    p2 = jnp.exp(m2 - m1) / z
    tot = p1 + p2
    return lane, i1, i2, p1 / tot * pgsel, p2 / tot * pgsel


def _outproj(x_ref, oa_ref, od_ref, wo_ref):
    return x_ref[...] + _mm(oa_ref[...], wo_ref[:ATT_WIDTH, :]) + _mm(od_ref[...], wo_ref[ATT_WIDTH:, :])


def _outproj_router_kernel(x_ref, oa_ref, od_t_ref, wo_ref, g_ref, wr_ref, h_ref, xn_ref, gate_ref):
    h = (x_ref[...] + _mm(oa_ref[...], wo_ref[:ATT_WIDTH, :])
         + _mm(od_t_ref[...].T, wo_ref[ATT_WIDTH:, :]))
    h_ref[...] = h
    xn = _rmsnorm(h, g_ref[...]).astype(BF16)
    xn_ref[...] = xn
    lane, i1, i2, g1, g2 = _route(xn, wr_ref[...])
    gate_ref[...] = jnp.where(lane == i1, g1, 0.0) + jnp.where(lane == i2, g2, 0.0)


def _outproj_router(x, oa, od_t, wo, g, wr):
    t = x.shape[0]
    tm = t
    row = lambda n: pl.BlockSpec((tm, n), lambda i: (i, 0))
    full = lambda a: pl.BlockSpec(a.shape, lambda i: (0,) * a.ndim)
    return pl.pallas_call(
        _outproj_router_kernel,
        grid=(t // tm,),
        in_specs=[row(D_MODEL), row(ATT_WIDTH), full(od_t), full(wo), full(g), full(wr)],
        out_specs=[row(D_MODEL), row(D_MODEL), row(LANES)],
        out_shape=[jax.ShapeDtypeStruct((t, D_MODEL), F32), jax.ShapeDtypeStruct((t, D_MODEL), BF16),
                   jax.ShapeDtypeStruct((t, LANES), F32)],
        compiler_params=_params("parallel"),
        name="outproj_router",
    )(x, oa, od_t, wo, g, wr)


def _moe_kernel(xn_ref, gate_ref, wg_ref, wu_ref, wd_ref, o_ref):
    e = pl.program_id(1)
    xn = xn_ref[...]
    lane = lax.broadcasted_iota(jnp.int32, gate_ref.shape, 1)
    gate = jnp.sum(jnp.where(lane == e + ROUTER_OFF, gate_ref[...], 0.0), axis=-1, keepdims=True)
    hg = jnp.dot(xn, wg_ref[...].astype(BF16), preferred_element_type=F32)
    hu = jnp.dot(xn, wu_ref[...].astype(BF16), preferred_element_type=F32)
    hm = _silu(hg) * hu * gate
    y = jnp.dot(hm.astype(BF16), wd_ref[...].astype(BF16), preferred_element_type=F32)

    @pl.when(e == 0)
    def _():
        o_ref[...] = y

    @pl.when(e > 0)
    def _():
        o_ref[...] += y


def _moe(xn, gates, wg, wu, wd):
    t = xn.shape[0]
    tm = min(t, 1024)
    return pl.pallas_call(
        _moe_kernel,
        grid=(t // tm, N_EXPERTS),
        in_specs=[pl.BlockSpec((tm, D_MODEL), lambda i, e: (i, 0)),
                  pl.BlockSpec((tm, LANES), lambda i, e: (i, 0)),
                  pl.BlockSpec((None, D_MODEL, D_EXPERT), lambda i, e: (e, 0, 0)),
                  pl.BlockSpec((None, D_MODEL, D_EXPERT), lambda i, e: (e, 0, 0)),
                  pl.BlockSpec((None, D_EXPERT, D_MODEL), lambda i, e: (e, 0, 0))],
        out_specs=pl.BlockSpec((tm, D_MODEL), lambda i, e: (i, 0)),
        out_shape=jax.ShapeDtypeStruct((t, D_MODEL), F32),
        compiler_params=_params("parallel", "arbitrary"),
        name="moe",
    )(xn, gates, wg, wu, wd)


MOE_TM = 512
POS_TM = 1024
INFO_G1, INFO_G2, INFO_E1, INFO_E2 = 0, 1, 2, 3
DMA_UNROLL = 8


def _moe_tiles(t):
    return (2 * t) // MOE_TM + N_EXPERTS


HALF = D_MODEL // 2
U32 = jnp.uint32


def _pack_rows(x):
    bits = lambda v: lax.bitcast_convert_type(v.astype(BF16).astype(F32), U32)
    return bits(x[:, HALF:]) | (bits(x[:, :HALF]) >> 16)


def _unpack_rows(w):
    lo = lax.bitcast_convert_type(w << 16, F32)
    hi = lax.bitcast_convert_type(w & jnp.uint32(0xFFFF0000), F32)
    return lo, hi


def _route_kernel(x_ref, oa_ref, od_ref, wo_ref, g_ref, wr_ref, h_ref, xn_ref, info_ref, cnt_ref, run_scr):
    h = _outproj(x_ref, oa_ref, od_ref, wo_ref)
    h_ref[...] = h
    xn = _rmsnorm(h, g_ref[...])
    xn_ref[...] = _pack_rows(xn)
    lane, i1, i2, g1, g2 = _route(xn.astype(BF16), wr_ref[...])
    info = jnp.where(lane == INFO_G1, g1, 0.0) + jnp.where(lane == INFO_G2, g2, 0.0)
    info = info + jnp.where(lane == INFO_E1, i1, 0.0) + jnp.where(lane == INFO_E2, i2, 0.0)
    info_ref[...] = info

    @pl.when(pl.program_id(0) == 0)
    def _():
        run_scr[...] = jnp.zeros(run_scr.shape, F32)
    picked = jnp.logical_or(lane == i1, lane == i2).astype(F32)
    run_scr[...] += jnp.sum(picked, axis=0, keepdims=True)
    cnt_ref[...] = run_scr[...]


def _route_sparse(x, oa, od, wo, g, wr):
    t = x.shape[0]
    tm = ROW_TM
    row = lambda n: pl.BlockSpec((tm, n), lambda i: (i, 0))
    full = lambda a: pl.BlockSpec(a.shape, lambda i: (0,) * a.ndim)
    return pl.pallas_call(
        _route_kernel,
        grid=(t // tm,),
        in_specs=[row(D_MODEL), row(ATT_WIDTH), row(DN_WIDTH), full(wo), full(g), full(wr)],
        out_specs=[row(D_MODEL), row(HALF), row(LANES), pl.BlockSpec((1, LANES), lambda i: (0, 0))],
        out_shape=[jax.ShapeDtypeStruct((t, D_MODEL), F32), jax.ShapeDtypeStruct((t, HALF), U32),
                   jax.ShapeDtypeStruct((t, LANES), F32), jax.ShapeDtypeStruct((1, LANES), F32)],
        scratch_shapes=[pltpu.VMEM((1, LANES), F32)],
        compiler_params=_params("arbitrary"),
        name="route",
    )(x, oa, od, wo, g, wr)


def _positions_kernel(info_ref, cnt_ref, ltri_ref, utri_ref, pos_ref, run_scr, off_scr):
    info = info_ref[...]
    lane = lax.broadcasted_iota(jnp.int32, info.shape, 1).astype(F32)
    hit1 = lane == info[:, INFO_E1:INFO_E1 + 1]
    hit2 = lane == info[:, INFO_E2:INFO_E2 + 1]
    onehot = jnp.logical_or(hit1, hit2).astype(F32)

    @pl.when(pl.program_id(0) == 0)
    def _():
        ln = lax.broadcasted_iota(jnp.int32, cnt_ref.shape, 1)
        is_expert = jnp.logical_and(ln >= ROUTER_OFF, ln < ROUTER_OFF + N_EXPERTS)
        tiles = jnp.where(is_expert, jnp.maximum(jnp.floor((cnt_ref[...] + (MOE_TM - 1)) * (1.0 / MOE_TM)), 1.0), 0.0)
        off_scr[...] = MOE_TM * jnp.dot(tiles.astype(BF16), utri_ref[...], preferred_element_type=F32)
        run_scr[...] = jnp.zeros(run_scr.shape, F32)

    before = (jnp.dot(ltri_ref[...], onehot.astype(BF16), preferred_element_type=F32)
              + run_scr[...] + off_scr[...])
    pos1 = jnp.sum(jnp.where(hit1, before, 0.0), axis=-1, keepdims=True)
    pos2 = jnp.sum(jnp.where(hit2, before, 0.0), axis=-1, keepdims=True)
    pos_ref[...] = (jnp.where(lane == 0, pos1, 0.0) + jnp.where(lane == 1, pos2, 0.0)).astype(jnp.int32)
    run_scr[...] += jnp.sum(onehot, axis=0, keepdims=True)


def _positions(info, cnt):
    t = info.shape[0]
    tm = min(t, POS_TM)
    tok = np.arange(tm)
    ltri = jnp.asarray((tok[:, None] > tok[None, :]).astype(np.float32), dtype=BF16)
    ln = np.arange(LANES)
    utri = jnp.asarray((ln[:, None] < ln[None, :]).astype(np.float32), dtype=BF16)
    full = lambda a: pl.BlockSpec(a.shape, lambda i: (0,) * a.ndim)
    return pl.pallas_call(
        _positions_kernel,
        grid=(t // tm,),
        in_specs=[pl.BlockSpec((tm, LANES), lambda i: (i, 0)), full(cnt), full(ltri), full(utri)],
        out_specs=pl.BlockSpec((tm, LANES), lambda i: (i, 0)),
        out_shape=jax.ShapeDtypeStruct((t, LANES), jnp.int32),
        scratch_shapes=[pltpu.VMEM((1, LANES), F32), pltpu.VMEM((1, LANES), F32)],
        compiler_params=_params("arbitrary"),
        name="positions",
    )(info, cnt, ltri, utri)


def _row_copy(src_hbm, src_row, dst_hbm, dst_row, sem):
    return pltpu.make_async_copy(src_hbm.at[pl.ds(src_row, 1)], dst_hbm.at[pl.ds(dst_row, 1)], sem)


SCATTER_SLOTS = 3


def _scatter_kernel(pos1_ref, pos2_ref, last_ref, used_ref, nt_ref, xn_hbm, zero_hbm, xs_hbm,
                    buf, lsem, sem, zsem, *, n_tok):
    max_tiles = xs_hbm.shape[0] // MOE_TM

    def zero_tile(tile):
        return pltpu.make_async_copy(zero_hbm, xs_hbm.at[pl.ds(tile * MOE_TM, MOE_TM)], zsem)

    def for_unused(fn):
        def body(tile, carry):
            fn(tile)
            return carry
        lax.fori_loop(nt_ref[0], max_tiles, body, 0)

    for e in range(N_EXPERTS):
        @pl.when(used_ref[e] > 0)
        def _():
            zero_tile(last_ref[e]).start()
    for_unused(lambda tile: zero_tile(tile).start())
    for e in range(N_EXPERTS):
        @pl.when(used_ref[e] > 0)
        def _():
            zero_tile(last_ref[e]).wait()
    for_unused(lambda tile: zero_tile(tile).wait())

    tm = buf.shape[1]
    n = n_tok // tm

    def load(i):
        return pltpu.make_async_copy(xn_hbm.at[pl.ds(i * tm, tm)], buf.at[i % SCATTER_SLOTS],
                                     lsem.at[i % SCATTER_SLOTS])

    def wait_rows(slot):
        pltpu.make_async_copy(xs_hbm.at[pl.ds(0, 2 * tm)], xs_hbm.at[pl.ds(0, 2 * tm)], sem.at[slot]).wait()

    load(0).start()
    load(1).start()

    def step(i, carry):
        slot = i % SCATTER_SLOTS
        load(i).wait()

        def body(j, c2):
            tok = i * tm + j
            src = buf.at[slot, pl.ds(j, 1)]
            pltpu.make_async_copy(src, xs_hbm.at[pl.ds(pos1_ref[tok], 1)], sem.at[slot]).start()
            pltpu.make_async_copy(src, xs_hbm.at[pl.ds(pos2_ref[tok], 1)], sem.at[slot]).start()
            return c2
        lax.fori_loop(0, tm, body, 0, unroll=DMA_UNROLL)

        @pl.when(i >= 1)
        def _():
            wait_rows((i - 1) % SCATTER_SLOTS)

        @pl.when(i + 2 < n)
        def _():
            load(i + 2).start()
        return carry
    lax.fori_loop(0, n, step, 0)
    wait_rows((n - 1) % SCATTER_SLOTS)


def _scatter_rows(xn, pos1, pos2, last_tile, used, n_tiles, n_rows):
    t = xn.shape[0]
    zero = jnp.zeros((MOE_TM, D_MODEL), F32)
    any_spec = pl.BlockSpec(memory_space=pl.ANY)
    return pl.pallas_call(
        functools.partial(_scatter_kernel, n_tok=t),
        grid_spec=pltpu.PrefetchScalarGridSpec(
            num_scalar_prefetch=5, grid=(1,),
            in_specs=[any_spec, any_spec], out_specs=any_spec,
            scratch_shapes=[pltpu.VMEM((SCATTER_SLOTS, MOE_TM, D_MODEL), F32),
                            pltpu.SemaphoreType.DMA((SCATTER_SLOTS,)),
                            pltpu.SemaphoreType.DMA((SCATTER_SLOTS,)),
                            pltpu.SemaphoreType.DMA]),
        out_shape=jax.ShapeDtypeStruct((n_rows, D_MODEL), F32),
        compiler_params=_params("arbitrary"),
        name="scatter_rows",
    )(pos1, pos2, last_tile, used, n_tiles, xn, zero)


def _experts_kernel(te_ref, tv_ref, nt_ref, xs_ref, wg_ref, wu_ref, wd_ref, xn_new_ref, gate_new_ref,
                    ys_ref, moe_new_ref, wg_s, wu_s, wd_s):
    i = pl.program_id(0)
    used = i < nt_ref[0]
    expert = te_ref[i]

    @pl.when(jnp.logical_or(i == 0, expert != te_ref[jnp.maximum(i - 1, 0)]))
    def _():
        wg_s[...] = wg_ref[...].astype(BF16)
        wu_s[...] = wu_ref[...].astype(BF16)
        wd_s[...] = wd_ref[...].astype(BF16)
        xn = xn_new_ref[...]
        lane = lax.broadcasted_iota(jnp.int32, gate_new_ref.shape, 1)
        gate = jnp.sum(jnp.where(lane == expert + ROUTER_OFF, gate_new_ref[...], 0.0), axis=-1, keepdims=True)
        hg = jnp.dot(xn, wg_s[...], preferred_element_type=F32)
        hu = jnp.dot(xn, wu_s[...], preferred_element_type=F32)
        hm = _silu(hg) * hu * gate
        y = jnp.dot(hm.astype(BF16), wd_s[...], preferred_element_type=F32)

        @pl.when(i == 0)
        def _():
            moe_new_ref[...] = y

        @pl.when(i > 0)
        def _():
            moe_new_ref[...] += y

    @pl.when(used)
    def _():
        row = lax.broadcasted_iota(jnp.int32, xs_ref.shape, 0)
        x_lo, x_hi = _unpack_rows(jnp.where(row < tv_ref[i], xs_ref[...], jnp.uint32(0)))
        x_lo = x_lo.astype(BF16)
        x_hi = x_hi.astype(BF16)
        up = lambda w_s: (jnp.dot(x_lo, w_s[:HALF, :], preferred_element_type=F32)
                          + jnp.dot(x_hi, w_s[HALF:, :], preferred_element_type=F32))
        hm = (_silu(up(wg_s)) * up(wu_s)).astype(BF16)
        ys_ref[...] = _pack_rows(jnp.dot(hm, wd_s[...], preferred_element_type=F32))

    @pl.when(jnp.logical_not(used))
    def _():
        ys_ref[...] = jnp.zeros(ys_ref.shape, U32)


def _experts(xs, tile_expert, tile_valid, n_tiles, wg, wu, wd, xn_new, gate_new):
    max_tiles = xs.shape[0] // MOE_TM
    rows = pl.BlockSpec((MOE_TM, HALF), lambda i, te, tv, nt: (i, 0))
    wspec = lambda shape: pl.BlockSpec((None,) + shape, lambda i, te, tv, nt: (te[i], 0, 0))
    full = lambda a: pl.BlockSpec(a.shape, lambda i, te, tv, nt: (0,) * a.ndim)
    return pl.pallas_call(
        _experts_kernel,
        grid_spec=pltpu.PrefetchScalarGridSpec(
            num_scalar_prefetch=3, grid=(max_tiles,),
            in_specs=[rows, wspec((D_MODEL, D_EXPERT)), wspec((D_MODEL, D_EXPERT)),
                      wspec((D_EXPERT, D_MODEL)), full(xn_new), full(gate_new)],
            out_specs=[rows, pl.BlockSpec(xn_new.shape, lambda i, te, tv, nt: (0, 0))],
            scratch_shapes=[pltpu.VMEM((D_MODEL, D_EXPERT), BF16), pltpu.VMEM((D_MODEL, D_EXPERT), BF16),
                            pltpu.VMEM((D_EXPERT, D_MODEL), BF16)]),
        out_shape=[jax.ShapeDtypeStruct(xs.shape, U32), jax.ShapeDtypeStruct(xn_new.shape, F32)],
        compiler_params=_params("arbitrary"),
        name="experts",
    )(tile_expert, tile_valid, n_tiles, xs, wg, wu, wd, xn_new, gate_new)


def _ple_gather_kernel(pos1_ref, pos2_ref, h_ref, info_ref, p_ref, wpp_ref, wpg_ref, gp_ref, gf_ref,
                       ys_hbm, y_ref, ybuf, sem):
    i = pl.program_id(0)
    n = pl.num_programs(0)
    tm = h_ref.shape[0]

    def issue(tile, slot):
        def body(j, carry):
            tok = tile * tm + j
            pltpu.make_async_copy(ys_hbm.at[pl.ds(pos1_ref[tok], 1)], ybuf.at[slot, 0, pl.ds(j, 1)],
                                  sem.at[slot]).start()
            pltpu.make_async_copy(ys_hbm.at[pl.ds(pos2_ref[tok], 1)], ybuf.at[slot, 1, pl.ds(j, 1)],
                                  sem.at[slot]).start()
            return carry
        lax.fori_loop(0, tm, body, 0, unroll=DMA_UNROLL)

    @pl.when(i == 0)
    def _():
        issue(0, 0)

    @pl.when(i + 1 < n)
    def _():
        issue(i + 1, (i + 1) % 2)

    slot = i % 2
    pltpu.make_async_copy(ybuf.at[slot], ybuf.at[slot], sem.at[slot]).wait()
    info = info_ref[...]
    moe = info[:, INFO_G1:INFO_G1 + 1] * ybuf[slot, 0] + info[:, INFO_G2:INFO_G2 + 1] * ybuf[slot, 1]
    h = h_ref[...] + moe
    hn = _rmsnorm(h, gp_ref[...])
    h = h + _mm(p_ref[...], wpp_ref[...]) * _sigmoid(_mm(hn, wpg_ref[...]))
    y_ref[...] = _rmsnorm(h, gf_ref[...])


def _ple_gather(h, info, p, ys, pos1, pos2, wpp, wpg, gp, gf):
    t = h.shape[0]
    tm = 256
    row = lambda n: pl.BlockSpec((tm, n), lambda i, p1, p2: (i, 0))
    full = lambda a: pl.BlockSpec(a.shape, lambda i, p1, p2: (0,) * a.ndim)
    return pl.pallas_call(
        _ple_gather_kernel,
        grid_spec=pltpu.PrefetchScalarGridSpec(
            num_scalar_prefetch=2, grid=(t // tm,),
            in_specs=[row(D_MODEL), row(LANES), row(PLE_DIM), full(wpp), full(wpg), full(gp), full(gf),
                      pl.BlockSpec(memory_space=pl.ANY)],
            out_specs=row(D_MODEL),
            scratch_shapes=[pltpu.VMEM((2, 2, tm, D_MODEL), F32), pltpu.SemaphoreType.DMA((2,))]),
        out_shape=jax.ShapeDtypeStruct((t, D_MODEL), F32),
        compiler_params=_params("arbitrary"),
        name="ple_gather",
    )(pos1, pos2, h, info, p, wpp, wpg, gp, gf, ys)


SC_IDX = 128
SC_ROWS = 64
SC_WORKERS = 32


def _sc_mesh():
    return plsc.VectorSubcoreMesh(core_axis_name="c", subcore_axis_name="s")


def _sc_windows(t, fn):
    per_worker = t // SC_WORKERS
    worker = lax.axis_index(("c", "s"))

    @pl.loop(0, per_worker // SC_IDX)
    def _(w):
        fn(worker * per_worker + w * SC_IDX)


def _sc_scatter_rows(xn, pos1, pos2, n_rows):
    t, d = xn.shape
    assert t % (SC_WORKERS * SC_IDX) == 0
    idx_t = pltpu.VMEM((1, SC_IDX), jnp.int32)

    @pl.kernel(out_type=jax.ShapeDtypeStruct((n_rows, d), xn.dtype), mesh=_sc_mesh(),
               scratch_types=[idx_t, idx_t, pltpu.VMEM((SC_ROWS, d), xn.dtype)])
    def scatter(x_hbm, p1_hbm, p2_hbm, o_hbm, i1_v, i2_v, buf):
        def window(base):
            pltpu.sync_copy(p1_hbm.at[:, pl.ds(base, SC_IDX)], i1_v)
            pltpu.sync_copy(p2_hbm.at[:, pl.ds(base, SC_IDX)], i2_v)
            for k in range(SC_IDX // SC_ROWS):
                pltpu.sync_copy(x_hbm.at[pl.ds(base + k * SC_ROWS, SC_ROWS)], buf)
                pltpu.sync_copy(buf, o_hbm.at[i1_v.at[0, pl.ds(k * SC_ROWS, SC_ROWS)]])
                pltpu.sync_copy(buf, o_hbm.at[i2_v.at[0, pl.ds(k * SC_ROWS, SC_ROWS)]])
        _sc_windows(t, window)

    return scatter(xn, pos1.reshape(1, t), pos2.reshape(1, t))


def _sc_gather_rows(ys, pos1, pos2):
    t = pos1.shape[0]
    d = ys.shape[1]
    assert t % (SC_WORKERS * SC_IDX) == 0
    idx_t = pltpu.VMEM((1, SC_IDX), jnp.int32)
    out = jax.ShapeDtypeStruct((t, d), ys.dtype)

    buf_t = pltpu.VMEM((SC_ROWS, d), ys.dtype)

    @pl.kernel(out_type=(out, out), mesh=_sc_mesh(),
               scratch_types=[idx_t, idx_t, buf_t, buf_t, pltpu.SemaphoreType.DMA((2,)),
                              pltpu.SemaphoreType.DMA((2,))])
    def gather(y_hbm, p1_hbm, p2_hbm, o1_hbm, o2_hbm, i1_v, i2_v, buf_a, buf_b, gsem, wsem):
        bufs = (buf_a, buf_b)

        def window(base):
            pltpu.sync_copy(p1_hbm.at[:, pl.ds(base, SC_IDX)], i1_v)
            pltpu.sync_copy(p2_hbm.at[:, pl.ds(base, SC_IDX)], i2_v)
            items = [(idx_v, o_hbm, k) for k in range(SC_IDX // SC_ROWS)
                     for idx_v, o_hbm in ((i1_v, o1_hbm), (i2_v, o2_hbm))]

            def read(n):
                idx_v, _, k = items[n]
                return pltpu.make_async_copy(y_hbm.at[idx_v.at[0, pl.ds(k * SC_ROWS, SC_ROWS)]],
                                             bufs[n % 2], gsem.at[n % 2])

            def write(n):
                _, o_hbm, k = items[n]
                return pltpu.make_async_copy(bufs[n % 2], o_hbm.at[pl.ds(base + k * SC_ROWS, SC_ROWS)],
                                             wsem.at[n % 2])

            read(0).start()
            for n in range(len(items)):
                read(n).wait()
                if n >= 1:
                    write(n - 1).wait()
                if n + 1 < len(items):
                    read(n + 1).start()
                write(n).start()
            write(len(items) - 1).wait()
        _sc_windows(t, window)

    return gather(ys, pos1.reshape(1, t), pos2.reshape(1, t))


def _ple_sparse_kernel(h_ref, info_ref, y1_ref, y2_ref, p_ref, wpp_ref, wpg_ref, gp_ref, gf_ref, y_ref):
    info = info_ref[...]
    g1 = info[:, INFO_G1:INFO_G1 + 1]
    g2 = info[:, INFO_G2:INFO_G2 + 1]
    y1_lo, y1_hi = _unpack_rows(y1_ref[...])
    y2_lo, y2_hi = _unpack_rows(y2_ref[...])
    moe = jnp.concatenate([g1 * y1_lo + g2 * y2_lo, g1 * y1_hi + g2 * y2_hi], axis=1)
    h = h_ref[...] + moe
    hn = _rmsnorm(h, gp_ref[...])
    h = h + _mm(p_ref[...], wpp_ref[...]) * _sigmoid(_mm(hn, wpg_ref[...]))
    y_ref[...] = _rmsnorm(h, gf_ref[...])


def _ple_sparse(h, info, y1, y2, p, wpp, wpg, gp, gf):
    t = h.shape[0]
    tm = ROW_TM
    row = lambda n: pl.BlockSpec((tm, n), lambda i: (i, 0))
    full = lambda a: pl.BlockSpec(a.shape, lambda i: (0,) * a.ndim)
    return pl.pallas_call(
        _ple_sparse_kernel,
        grid=(t // tm,),
        in_specs=[row(D_MODEL), row(LANES), row(HALF), row(HALF), row(PLE_DIM),
                  full(wpp), full(wpg), full(gp), full(gf)],
        out_specs=row(D_MODEL),
        out_shape=jax.ShapeDtypeStruct((t, D_MODEL), F32),
        compiler_params=_params("parallel"),
        name="ple_sparse",
    )(h, info, y1, y2, p, wpp, wpg, gp, gf)


def _tile_tables(cnt, max_tiles):
    tiles_e = jnp.maximum((cnt + (MOE_TM - 1)) // MOE_TM, 1)
    ends = jnp.cumsum(tiles_e)
    n_tiles = ends[-1]
    tile = jnp.arange(max_tiles, dtype=jnp.int32)
    idx = jnp.minimum(tile, n_tiles - 1)
    tile_expert = jnp.sum((idx[:, None] >= ends[None, :]).astype(jnp.int32), axis=1)
    mine = tile_expert[:, None] == jnp.arange(N_EXPERTS, dtype=jnp.int32)[None, :]
    of_mine = lambda v: jnp.sum(jnp.where(mine, v[None, :], 0), axis=1)
    valid = jnp.clip(of_mine(cnt) - (idx - of_mine(ends - tiles_e)) * MOE_TM, 0, MOE_TM)
    tile_valid = jnp.where(tile < n_tiles, valid, 0).astype(jnp.int32)
    return (tile_expert, tile_valid, n_tiles.reshape(1), (ends - 1).astype(jnp.int32),
            tiles_e.astype(jnp.int32))


def _ple_final_kernel(h_ref, m_ref, p_ref, wpp_ref, wpg_ref, gp_ref, gf_ref, y_ref):
    h = h_ref[...] + m_ref[...]
    hn = _rmsnorm(h, gp_ref[...])
    h = h + _mm(p_ref[...], wpp_ref[...]) * _sigmoid(_mm(hn, wpg_ref[...]))
    y_ref[...] = _rmsnorm(h, gf_ref[...])


def _ple_final(h, m, p, wpp, wpg, gp, gf):
    t = h.shape[0]
    tm = min(t, 256)
    row = lambda n: pl.BlockSpec((tm, n), lambda i: (i, 0))
    full = lambda a: pl.BlockSpec(a.shape, lambda i: (0,) * a.ndim)
    return pl.pallas_call(
        _ple_final_kernel,
        grid=(t // tm,),
        in_specs=[row(D_MODEL), row(D_MODEL), row(PLE_DIM), full(wpp), full(wpg), full(gp), full(gf)],
        out_specs=row(D_MODEL),
        out_shape=jax.ShapeDtypeStruct((t, D_MODEL), F32),
        compiler_params=_params("parallel"),
        name="ple_final",
    )(h, m, p, wpp, wpg, gp, gf)


def kernel(x_prompt, x_sample, p_prompt, p_sample, cache_k, cache_v, state_conv, state_S, rel_bias, norm_mix, w_in, att_sink, conv_w, dn_A_log, dn_dt_bias, dn_norm, w_out, norm_ffn, w_router_group, w_router_expert, w_gate, w_up, w_down, w_ple_proj, w_ple_gate, norm_ple, norm_final):
    batch, seq, _ = x_prompt.shape
    nseq = x_sample.shape[0]
    assert x_sample.shape[1] == 1 and norm_mix.shape[0] == 1 and cache_k.shape[2] == WINDOW
    assert seq % GDN_TB == 0 and seq % ATT_BLOCK == 0

    wi = w_in[0]
    o_db = ATT_COLS + CONV_CH
    w_in_re = jnp.concatenate(
        [wi[:, :o_db], wi[:, o_db + 2 * DN_HEADS:], wi[:, o_db:o_db + 2 * DN_HEADS],
         jnp.zeros((D_MODEL, LANES - 2 * DN_HEADS), F32)], axis=1).astype(BF16)
    row = lambda a: a.reshape(1, -1).astype(F32)
    pad_lanes = lambda a, off: jnp.zeros((1, LANES), F32).at[0, off:off + a.shape[0]].set(a)
    alog = pad_lanes(dn_A_log[0], DN_HEADS)
    dtb = pad_lanes(dn_dt_bias[0], DN_HEADS)
    dnx = jnp.tile(dn_norm[0], DN_HEADS).reshape(1, DN_WIDTH)
    w_router = jnp.concatenate(
        [w_router_group[0], w_router_expert[0],
         jnp.zeros((D_MODEL, LANES - N_GROUPS - N_EXPERTS), F32)], axis=1).astype(BF16)
    wo = w_out[0].astype(BF16)
    wg, wu, wd = w_gate[0], w_up[0], w_down[0]
    wpp, wpg = w_ple_proj[0].astype(BF16), w_ple_gate[0].astype(BF16)
    sink = att_sink[0]

    qi = np.arange(ATT_BLOCK)[:, None]
    kj = np.arange(2 * ATT_BLOCK)[None, :]
    bucket_p = jnp.asarray(_t5_bucket_np(qi + ATT_BLOCK - kj))
    bucket_s = jnp.asarray(_t5_bucket_np(WINDOW - np.arange(WINDOW)[None, :]))

    xp = x_prompt.reshape(batch * seq, D_MODEL)
    att_p, qkv_p, dz_p, ba_p, xc_tails = _inproj_conv(xp, row(norm_mix[0]), w_in_re, conv_w[0], seq)
    o_att_p = _attn_prompt(att_p, bucket_p, rel_bias, sink, batch, seq)
    o_dn_p, s_p = _gdn_prompt(qkv_p, dz_p, ba_p, alog, dtb, dnx, batch, seq)
    h1, xn2, info, cnt = _route_sparse(xp, o_att_p, o_dn_p, wo, row(norm_ffn[0]), w_router)
    pos = _positions(info, cnt)
    pos1, pos2 = pos[:, 0], pos[:, 1]
    max_tiles = _moe_tiles(batch * seq)
    cnt_e = cnt[0, ROUTER_OFF:ROUTER_OFF + N_EXPERTS].astype(jnp.int32)
    tile_expert, tile_valid, n_tiles, last_tile, used = _tile_tables(cnt_e, max_tiles)
    xs_sorted = _sc_scatter_rows(xn2, pos1, pos2, max_tiles * MOE_TM)

    xs = x_sample.reshape(nseq, D_MODEL)
    att_s, xc_s, dz_s, ba_s = _inproj(xs, row(norm_mix[0]), w_in_re)
    ck_t = jnp.transpose(cache_k[0], (0, 2, 3, 1))
    cv_t = jnp.transpose(cache_v[0], (0, 2, 3, 1))
    o_att_s, ks_t, vs_t = _attn_sample(att_s, ck_t, cv_t, bucket_s, rel_bias, sink)
    sconv_t = jnp.swapaxes(state_conv[0], 0, 1)
    o_dn_s_t, s_s_t = _gdn_sample_lanes(xc_s, dz_s, ba_s, sconv_t, jnp.transpose(state_S[0], (1, 2, 3, 0)),
                                        conv_w[0], alog, dtb, dn_norm[0])
    s_s = jnp.transpose(s_s_t, (3, 0, 1, 2))

    h1_s, xn2_s, gates_s = _outproj_router(xs, o_att_s, o_dn_s_t, wo, row(norm_ffn[0]), w_router)

    ys, moe_s = _experts(xs_sorted, tile_expert, tile_valid, n_tiles, wg, wu, wd, xn2_s, gates_s)
    y1, y2 = _sc_gather_rows(ys, pos1, pos2)
    y_s = _ple_final(h1_s, moe_s, p_sample[0].reshape(nseq, PLE_DIM), wpp, wpg, row(norm_ple[0]),
                     row(norm_final))
    y_p = _ple_sparse(h1, info, y1, y2, p_prompt[0].reshape(batch * seq, PLE_DIM),
                      wpp, wpg, row(norm_ple[0]), row(norm_final))

    att_p3 = att_p.reshape(batch, seq, ATT_COLS)
    kv_shape = (1, batch, WINDOW, ATT_KV_HEADS, HEAD_DIM)
    k_p = att_p3[:, seq - WINDOW:, ATT_WIDTH:ATT_WIDTH + KV_WIDTH].reshape(kv_shape)
    v_p = att_p3[:, seq - WINDOW:, ATT_WIDTH + KV_WIDTH:].reshape(kv_shape)
    conv_p = xc_tails.reshape(batch, -1, TAIL, CONV_CH)[:, -1, TAIL - (CONV_WIDTH - 1):][None]
    k_s = jnp.transpose(ks_t, (0, 3, 1, 2))[None]
    v_s = jnp.transpose(vs_t, (0, 3, 1, 2))[None]
    conv_s = jnp.concatenate([state_conv[0][:, 1:], xc_s[:, None, :]], axis=1)[None]
    return (y_p.reshape(batch, seq, D_MODEL), y_s.reshape(nseq, 1, D_MODEL),
            k_p, v_p, conv_p, s_p[None], k_s, v_s, conv_s, s_s[None])
```

```python
import functools
import math

import numpy as np
import jax
import jax.numpy as jnp
from jax import lax
from jax.experimental import pallas as pl
from jax.experimental.pallas import tpu as pltpu
from jax.experimental.pallas import tpu_sc as plsc

F32 = jnp.float32
BF16 = jnp.bfloat16

D_MODEL = 1024
ATT_HEADS = 8
ATT_KV_HEADS = 2
HEAD_DIM = 64
GQA = ATT_HEADS // ATT_KV_HEADS
WINDOW = 128
ATT_BLOCK = 128
N_BUCKETS = 32
DN_HEADS = 8
DN_DK = 64
DN_DV = 64
CONV_WIDTH = 4
DN_CHUNK = 64
ATT_WIDTH = ATT_HEADS * HEAD_DIM
KV_WIDTH = ATT_KV_HEADS * HEAD_DIM
DN_WIDTH = DN_HEADS * DN_DV
CONV_CH = 3 * DN_WIDTH
N_GROUPS = 4
EXPERTS_PER_GROUP = 8
N_EXPERTS = N_GROUPS * EXPERTS_PER_GROUP
D_EXPERT = 256
PLE_DIM = 256
EPS = 1e-6
NEG_INF = float("-inf")

ATT_COLS = ATT_WIDTH + 2 * KV_WIDTH
LANES = 128
IN_COLS = ATT_COLS + CONV_CH + DN_WIDTH + LANES
ROUTER_OFF = N_GROUPS
VMEM_LIMIT = 48 * 1024 * 1024
ROW_TM = 512


def _params(*sem):
    return pltpu.CompilerParams(dimension_semantics=sem, vmem_limit_bytes=VMEM_LIMIT)


def _mm(a, b):
    return jnp.dot(a.astype(BF16), b.astype(BF16), preferred_element_type=F32)


def _mm_nt(a, b):
    return lax.dot_general(a.astype(BF16), b.astype(BF16), (((1,), (1,)), ((), ())),
                           preferred_element_type=F32)


def _mm_tn(a, b):
    return lax.dot_general(a.astype(BF16), b.astype(BF16), (((0,), (0,)), ((), ())),
                           preferred_element_type=F32)


def _split3(x):
    h1 = x.astype(BF16)
    r1 = x - h1.astype(F32)
    h2 = r1.astype(BF16)
    h3 = (r1 - h2.astype(F32)).astype(BF16)
    return h1, h2, h3


def _mm_sel_rhs(x, sel):
    h1, h2, h3 = _split3(x)
    d = lambda h: jnp.dot(h, sel, preferred_element_type=F32)
    return d(h1) + d(h2) + d(h3)


def _mm_sel_lhs(sel, x):
    h1, h2, h3 = _split3(x)
    d = lambda h: jnp.dot(sel, h, preferred_element_type=F32)
    return d(h1) + d(h2) + d(h3)


def _mm3(a, b):
    ah = a.astype(BF16)
    al = (a - ah.astype(F32)).astype(BF16)
    bh = b.astype(BF16)
    bl = (b - bh.astype(F32)).astype(BF16)
    d = lambda u, v: jnp.dot(u, v, preferred_element_type=F32)
    return d(ah, bh) + d(ah, bl) + d(al, bh)


def _sigmoid(x):
    return 1.0 / (1.0 + jnp.exp(-x))


def _silu(x):
    return x * _sigmoid(x)


def _softplus(x):
    return jnp.maximum(x, 0.0) + jnp.log1p(jnp.exp(-jnp.abs(x)))


def _rmsnorm(x, g):
    return x * lax.rsqrt(jnp.mean(x * x, axis=-1, keepdims=True) + EPS) * g


def _t5_bucket_np(dist):
    max_exact = N_BUCKETS // 2
    d = np.maximum(dist, 0)
    ratio = (np.log(np.maximum(d, 1).astype(np.float32) / np.float32(max_exact))
             / np.float32(math.log(WINDOW / max_exact))).astype(np.float32)
    large = np.minimum(max_exact + (ratio * np.float32(N_BUCKETS - max_exact)).astype(np.int32),
                       N_BUCKETS - 1)
    return np.where(d < max_exact, d, large).astype(np.int32)


def _bias_lookup(bucket, rb_ref, h):
    acc = jnp.zeros(bucket.shape, F32)
    for t in range(N_BUCKETS):
        acc = jnp.where(bucket == t, rb_ref[t, h], acc)
    return acc


def _inproj_kernel(x_ref, g_ref, w_ref, att_ref, xc_ref, dz_ref, ba_ref):
    xn = _rmsnorm(x_ref[...], g_ref[...]).astype(BF16)
    o0, o1, o2 = ATT_COLS, ATT_COLS + CONV_CH, ATT_COLS + CONV_CH + DN_WIDTH
    att_ref[...] = jnp.dot(xn, w_ref[:, :o0], preferred_element_type=F32)
    xc_ref[...] = jnp.dot(xn, w_ref[:, o0:o1], preferred_element_type=F32)
    dz_ref[...] = jnp.dot(xn, w_ref[:, o1:o2], preferred_element_type=F32)
    ba_ref[...] = jnp.dot(xn, w_ref[:, o2:], preferred_element_type=F32)


def _inproj(x, g, w):
    t = x.shape[0]
    tm = min(t, ROW_TM)
    row = lambda n: pl.BlockSpec((tm, n), lambda i: (i, 0))
    full = lambda a: pl.BlockSpec(a.shape, lambda i: (0,) * a.ndim)
    return pl.pallas_call(
        _inproj_kernel,
        grid=(t // tm,),
        in_specs=[row(D_MODEL), full(g), full(w)],
        out_specs=[row(ATT_COLS), row(CONV_CH), row(DN_WIDTH), row(LANES)],
        out_shape=[jax.ShapeDtypeStruct((t, n), F32) for n in (ATT_COLS, CONV_CH, DN_WIDTH, LANES)],
        compiler_params=_params("parallel"),
        name="inproj",
    )(x, g, w)


TAIL = 8
PAIR = 2 * DN_DK
N_PAIRS = DN_WIDTH // PAIR


def _head_sums(z, pair_ones):
    hi = z.astype(BF16)
    lw = (z - hi.astype(F32)).astype(BF16)
    d = lambda a, p: jnp.dot(a[:, p * PAIR:(p + 1) * PAIR], pair_ones, preferred_element_type=F32)
    return jnp.concatenate([d(hi, p) + d(lw, p) for p in range(N_PAIRS)], axis=1)


def _inproj_conv_kernel(x_ref, g_ref, w_ref, cw_ref, ones_ref, att_ref, qkv_ref, dz_ref, ba_ref, tail_ref,
                        xp_scr, *, tiles_per_seq):
    tm = x_ref.shape[0]

    @pl.when(pl.program_id(0) % tiles_per_seq == 0)
    def _():
        xp_scr[0:TAIL, :] = jnp.zeros((TAIL, CONV_CH), F32)

    xn = _rmsnorm(x_ref[...], g_ref[...]).astype(BF16)
    o0, o1, o2 = ATT_COLS, ATT_COLS + CONV_CH, ATT_COLS + CONV_CH + DN_WIDTH
    xc = jnp.dot(xn, w_ref[:, o0:o1], preferred_element_type=F32)
    att_ref[...] = jnp.dot(xn, w_ref[:, :o0], preferred_element_type=F32)
    dz_ref[...] = jnp.dot(xn, w_ref[:, o1:o2], preferred_element_type=F32)
    ba_ref[...] = jnp.dot(xn, w_ref[:, o2:], preferred_element_type=F32)

    xp_scr[TAIL:, :] = xc
    y = xp_scr[TAIL - 3:TAIL - 3 + tm, :] * cw_ref[0:1, :]
    y = y + xp_scr[TAIL - 2:TAIL - 2 + tm, :] * cw_ref[1:2, :]
    y = y + xp_scr[TAIL - 1:TAIL - 1 + tm, :] * cw_ref[2:3, :]
    y = y + xc * cw_ref[3:4, :]
    tail = xc[tm - TAIL:, :]
    xp_scr[0:TAIL, :] = tail
    tail_ref[0] = tail
    y = _silu(y)
    q = y[:, :DN_WIDTH]
    k = y[:, DN_WIDTH:2 * DN_WIDTH]
    inv_norm = lax.rsqrt(_head_sums(jnp.concatenate([q * q, k * k], axis=0), ones_ref[...]) + EPS)
    qkv_ref[:, :DN_WIDTH] = q * inv_norm[:tm] * (DN_DK ** -0.5)
    qkv_ref[:, DN_WIDTH:2 * DN_WIDTH] = k * inv_norm[tm:]
    qkv_ref[:, 2 * DN_WIDTH:] = y[:, 2 * DN_WIDTH:]


def _pair_ones():
    lane = np.arange(PAIR)
    return jnp.asarray((lane[:, None] // DN_DV == lane[None, :] // DN_DV).astype(np.float32), dtype=BF16)


def _inproj_conv(x, g, w, conv_w, seq):
    t = x.shape[0]
    tm = ROW_TM
    assert seq % tm == 0
    ones = _pair_ones()
    row = lambda n: pl.BlockSpec((tm, n), lambda i: (i, 0))
    full = lambda a: pl.BlockSpec(a.shape, lambda i: (0,) * a.ndim)
    return pl.pallas_call(
        functools.partial(_inproj_conv_kernel, tiles_per_seq=seq // tm),
        grid=(t // tm,),
        in_specs=[row(D_MODEL), full(g), full(w), full(conv_w), full(ones)],
        out_specs=[row(ATT_COLS), row(CONV_CH), row(DN_WIDTH), row(LANES),
                   pl.BlockSpec((1, TAIL, CONV_CH), lambda i: (i, 0, 0))],
        out_shape=[jax.ShapeDtypeStruct((t, n), F32) for n in (ATT_COLS, CONV_CH, DN_WIDTH, LANES)]
                  + [jax.ShapeDtypeStruct((t // tm, TAIL, CONV_CH), F32)],
        scratch_shapes=[pltpu.VMEM((TAIL + tm, CONV_CH), F32)],
        compiler_params=_params("arbitrary"),
        name="inproj_conv",
    )(x, g, w, conv_w, ones)


GROUP_ROWS = GQA * ATT_BLOCK


def _attn_prompt_kernel(cur_ref, prev_ref, bucket_ref, rb_ref, sink_ref, o_ref, bias_scr, sink_scr):
    i = pl.program_id(0)
    nseq = cur_ref.shape[0]

    @pl.when(i == 0)
    def _():
        qi = lax.broadcasted_iota(jnp.int32, (ATT_BLOCK, 2 * ATT_BLOCK), 0)
        kj = lax.broadcasted_iota(jnp.int32, (ATT_BLOCK, 2 * ATT_BLOCK), 1)
        dist = qi + ATT_BLOCK - kj
        band = jnp.logical_and(dist >= 0, dist < WINDOW)
        bucket = bucket_ref[...]
        hrow = lax.broadcasted_iota(jnp.int32, (GROUP_ROWS, 1), 0) // ATT_BLOCK
        for g in range(ATT_KV_HEADS):
            sink_col = jnp.zeros((GROUP_ROWS, 1), F32)
            for hh in range(GQA):
                h = g * GQA + hh
                bias = jnp.where(band, _bias_lookup(bucket, rb_ref, h), NEG_INF)
                bias_scr[0, g, hh * ATT_BLOCK:(hh + 1) * ATT_BLOCK, :] = bias
                bias_scr[1, g, hh * ATT_BLOCK:(hh + 1) * ATT_BLOCK, :] = jnp.where(kj >= ATT_BLOCK, bias, NEG_INF)
                sink_col = jnp.where(hrow == hh, sink_ref[h], sink_col)
            sink_scr[g] = sink_col

    first = (i == 0).astype(jnp.int32)
    probs = [(b, g) for b in range(nseq) for g in range(ATT_KV_HEADS)]
    scores = []
    for b, g in probs:
        cur = cur_ref[b]
        prev = prev_ref[b]
        q = jnp.concatenate([cur[:, (g * GQA + hh) * HEAD_DIM:(g * GQA + hh + 1) * HEAD_DIM]
                             for hh in range(GQA)], axis=0) * (HEAD_DIM ** -0.5)
        kcol = slice(ATT_WIDTH + g * HEAD_DIM, ATT_WIDTH + (g + 1) * HEAD_DIM)
        k2 = jnp.concatenate([prev[:, kcol], cur[:, kcol]], axis=0)
        scores.append(_mm_nt(q, k2) + bias_scr[first, g])
    probs_p, dens = [], []
    for (b, g), s in zip(probs, scores):
        sink = sink_scr[g]
        m = jnp.maximum(jnp.max(s, axis=-1, keepdims=True), sink)
        p = jnp.exp(s - m)
        dens.append(jnp.sum(p, axis=-1, keepdims=True) + jnp.exp(sink - m))
        probs_p.append(p)
    outs = {}
    for (b, g), p, den in zip(probs, probs_p, dens):
        vcol = slice(ATT_WIDTH + KV_WIDTH + g * HEAD_DIM, ATT_WIDTH + KV_WIDTH + (g + 1) * HEAD_DIM)
        v2 = jnp.concatenate([prev_ref[b][:, vcol], cur_ref[b][:, vcol]], axis=0)
        outs[b, g] = _mm(p, v2) / den
    for b in range(nseq):
        o_ref[b] = jnp.concatenate([outs[b, g][hh * ATT_BLOCK:(hh + 1) * ATT_BLOCK, :]
                                    for g in range(ATT_KV_HEADS) for hh in range(GQA)],
                                   axis=1).astype(o_ref.dtype)


def _attn_prompt(att, bucket, rel_bias, sink, batch, seq):
    nb = seq // ATT_BLOCK
    smem = pl.BlockSpec(memory_space=pltpu.SMEM)
    att3 = att.reshape(batch, seq, ATT_COLS)
    out = pl.pallas_call(
        _attn_prompt_kernel,
        grid=(nb,),
        in_specs=[
            pl.BlockSpec((batch, ATT_BLOCK, ATT_COLS), lambda i: (0, i, 0)),
            pl.BlockSpec((batch, ATT_BLOCK, ATT_COLS), lambda i: (0, jnp.maximum(i - 1, 0), 0)),
            pl.BlockSpec(bucket.shape, lambda i: (0, 0)),
            smem, smem,
        ],
        out_specs=pl.BlockSpec((batch, ATT_BLOCK, ATT_WIDTH), lambda i: (0, i, 0)),
        out_shape=jax.ShapeDtypeStruct((batch, seq, ATT_WIDTH), BF16),
        scratch_shapes=[pltpu.VMEM((2, ATT_KV_HEADS, GROUP_ROWS, 2 * ATT_BLOCK), F32),
                        pltpu.VMEM((ATT_KV_HEADS, GROUP_ROWS, 1), F32)],
        compiler_params=_params("arbitrary"),
        name="attn_prompt",
    )(att3, att3, bucket, rel_bias, sink)
    return out.reshape(batch * seq, ATT_WIDTH)


ATT_S_BB = 8


def _attn_sample_kernel(att_ref, ck_ref, cv_ref, bucket_ref, rb_ref, sink_ref, o_ref, ks_ref, vs_ref,
                        bias_scr, col_scr):
    hrow = lax.broadcasted_iota(jnp.int32, (ATT_HEADS, LANES), 0)
    lane = lax.broadcasted_iota(jnp.int32, (ATT_HEADS, LANES), 1)

    last = (lax.broadcasted_iota(jnp.int32, (3, WINDOW), 1) == WINDOW - 1).astype(BF16)
    is_last = lax.broadcasted_iota(jnp.int32, (KV_WIDTH, WINDOW), 1) == WINDOW - 1

    def shifted(cache_t, new_row):
        pieces = jnp.concatenate([p.astype(F32) for p in _split3(new_row)], axis=0).astype(BF16)
        col = lax.dot_general(pieces, last, (((0,), (0,)), ((), ())), preferred_element_type=F32)
        out = jnp.where(is_last, col, pltpu.roll(cache_t, WINDOW - 1, axis=1))
        return out.reshape(ATT_KV_HEADS, HEAD_DIM, WINDOW)

    for b in range(ATT_S_BB):
        row = att_ref[b:b + 1, :]
        ks_ref[b] = shifted(ck_ref[b].reshape(KV_WIDTH, WINDOW), row[:, ATT_WIDTH:ATT_WIDTH + KV_WIDTH])
        vs_ref[b] = shifted(cv_ref[b].reshape(KV_WIDTH, WINDOW), row[:, ATT_WIDTH + KV_WIDTH:])

    @pl.when(pl.program_id(0) == 0)
    def _():
        bucket = jnp.broadcast_to(bucket_ref[...], (ATT_HEADS, LANES))
        bias = jnp.zeros((ATT_HEADS, LANES), F32)
        cols = jnp.zeros((ATT_HEADS, LANES), F32)
        for h in range(ATT_HEADS):
            bias = jnp.where(hrow == h, _bias_lookup(bucket, rb_ref, h), bias)
            cols = jnp.where(jnp.logical_and(hrow == h, lane == 0), sink_ref[h], cols)
            cols = jnp.where(jnp.logical_and(hrow == h, lane == 1), rb_ref[0, h], cols)
        bias_scr[...] = jnp.where(lane >= 1, bias, NEG_INF)
        col_scr[...] = cols

    bias_c = bias_scr[...]
    sink = col_scr[:, 0:1]
    bias_n = col_scr[:, 1:2]
    same_group = (hrow // GQA) == (lane // HEAD_DIM)
    low_group = lax.broadcasted_iota(jnp.int32, (ATT_HEADS, HEAD_DIM), 0) < GQA
    rnd = lambda a: a.astype(BF16).astype(F32)
    seqs = range(ATT_S_BB)
    rows = [att_ref[b:b + 1, :] for b in seqs]
    q_bds = []
    for row in rows:
        q = row[:, :ATT_WIDTH] * (HEAD_DIM ** -0.5)
        qh = jnp.concatenate([q[:, h * HEAD_DIM:(h + 1) * HEAD_DIM] for h in range(ATT_HEADS)], axis=0)
        q_bds.append(jnp.where(same_group, jnp.concatenate([qh, qh], axis=1), 0.0))
    kv_t = lambda ref, b: ref[b].reshape(KV_WIDTH, WINDOW)
    s_cs = [_mm(q_bd, kv_t(ck_ref, b)) + bias_c for b, q_bd in zip(seqs, q_bds)]
    prs, pns = [], []
    for row, q_bd, s_c in zip(rows, q_bds, s_cs):
        kn = row[:, ATT_WIDTH:ATT_WIDTH + KV_WIDTH]
        s_n = jnp.sum(rnd(q_bd) * rnd(kn), axis=-1, keepdims=True) + bias_n
        m = jnp.maximum(jnp.maximum(jnp.max(s_c, axis=-1, keepdims=True), s_n), sink)
        p_c = jnp.exp(s_c - m)
        p_n = jnp.exp(s_n - m)
        den = jnp.sum(p_c, axis=-1, keepdims=True) + p_n + jnp.exp(sink - m)
        prs.append(p_c / den)
        pns.append(p_n / den)
    pvs = [_mm_nt(pr, kv_t(cv_ref, b)) for b, pr in zip(seqs, prs)]
    for b, row, pv, pn in zip(seqs, rows, pvs, pns):
        vn = row[:, ATT_WIDTH + KV_WIDTH:]
        o_full = pv + rnd(pn) * rnd(vn)
        o_sel = jnp.where(low_group, o_full[:, :HEAD_DIM], o_full[:, HEAD_DIM:])
        o_ref[b:b + 1, :] = jnp.concatenate([o_sel[h:h + 1, :] for h in range(ATT_HEADS)], axis=1)


def _attn_sample(att, ck, cv, bucket, rel_bias, sink):
    nseq = att.shape[0]
    smem = pl.BlockSpec(memory_space=pltpu.SMEM)
    cache = pl.BlockSpec((ATT_S_BB, ATT_KV_HEADS, HEAD_DIM, WINDOW), lambda i: (i, 0, 0, 0))
    return pl.pallas_call(
        _attn_sample_kernel,
        grid=(nseq // ATT_S_BB,),
        in_specs=[pl.BlockSpec((ATT_S_BB, ATT_COLS), lambda i: (i, 0)), cache, cache,
                  pl.BlockSpec(bucket.shape, lambda i: (0, 0)), smem, smem],
        out_specs=[pl.BlockSpec((ATT_S_BB, ATT_WIDTH), lambda i: (i, 0)), cache, cache],
        out_shape=[jax.ShapeDtypeStruct((nseq, ATT_WIDTH), F32),
                   jax.ShapeDtypeStruct(ck.shape, F32), jax.ShapeDtypeStruct(cv.shape, F32)],
        scratch_shapes=[pltpu.VMEM((ATT_HEADS, LANES), F32), pltpu.VMEM((ATT_HEADS, LANES), F32)],
        compiler_params=_params("arbitrary"),
        name="attn_sample",
    )(att, ck, cv, bucket, rel_bias, sink)


GDN_TB = 128
GDN_NC = GDN_TB // DN_CHUNK


def _gdn_gates(ba, alog, dtb):
    beta = _sigmoid(ba)
    g = -jnp.exp(alog) * _softplus(ba + dtb)
    return beta, g


def _pair_diag(x, lo):
    xb = x.astype(BF16)
    zero = jnp.zeros_like(xb)
    return jnp.concatenate([jnp.where(lo, xb, zero), jnp.where(lo, zero, xb)], axis=0)


def _gdn_prompt_kernel(qkv_ref, dz_ref, ba_ref, alog_ref, dtb_ref, dnx_ref,
                       hsum_ref, expb_ref, expg_ref, ltri_ref,
                       o_ref, s_out_ref, s_scr):
    i = pl.program_id(0)
    nb = qkv_ref.shape[0]

    @pl.when(i == 0)
    def _():
        s_scr[...] = jnp.zeros(s_scr.shape, F32)

    hsum = hsum_ref[...]
    ri = lax.broadcasted_iota(jnp.int32, (DN_CHUNK, PAIR), 0)
    ci = lax.broadcasted_iota(jnp.int32, (DN_CHUNK, PAIR), 1)
    lo = ci < DN_DK
    cj = jnp.where(lo, ci, ci - DN_DK)
    causal = ri >= cj
    strict = ri > cj
    eye = (ri == cj).astype(F32)

    def sel2(x, m):
        hi = x.astype(BF16)
        lw = (x - hi.astype(F32)).astype(BF16)
        return (jnp.dot(hi, m, preferred_element_type=F32) + jnp.dot(lw, m, preferred_element_type=F32))

    pre = []
    for b in range(nb):
        q = qkv_ref[b, :, :DN_WIDTH]
        k = qkv_ref[b, :, DN_WIDTH:2 * DN_WIDTH]
        v = qkv_ref[b, :, 2 * DN_WIDTH:]
        beta_c, g_c = _gdn_gates(ba_ref[b], alog_ref[...], dtb_ref[...])
        beta = sel2(beta_c, expb_ref[...])
        gam_c = _mm_sel_lhs(ltri_ref[...], g_c)
        gam = _mm_sel_rhs(gam_c, expg_ref[...])
        gam_t = gam_c.T
        kb = k * beta
        egam = jnp.exp(gam)
        pre.append(dict(q=q, k=k, kb=kb, vb=v * beta, qg=q * egam, wr=kb * egam, gam=gam, gam_t=gam_t))

    probs = [(b, p) for b in range(nb) for p in range(N_PAIRS)]
    pick = lambda m: jnp.where(lo, m[:DN_DK], m[DN_DK:])
    o_rows = [[] for _ in range(nb)]
    for c in range(GDN_NC):
        r0, r1 = c * DN_CHUNK, (c + 1) * DN_CHUNK
        sl = lambda name, b, p: pre[b][name][r0:r1, p * PAIR:(p + 1) * PAIR]
        raws = []
        for b, p in probs:
            k_p = sl("k", b, p)
            k_rows = jnp.concatenate([jnp.where(lo, k_p, 0.0), jnp.where(lo, 0.0, k_p)], axis=0)
            raws.append(_mm_nt(jnp.concatenate([sl("kb", b, p), sl("q", b, p)], axis=0), k_rows))
        pws, ts, qks = [], [], []
        for (b, p), raw in zip(probs, raws):
            gcol = sl("gam", b, p)
            h0 = DN_HEADS + 2 * p
            gam_t = pre[b]["gam_t"]
            grow = jnp.concatenate([gam_t[h0:h0 + 1, r0:r1], gam_t[h0 + 1:h0 + 2, r0:r1]], axis=1)
            decay = jnp.exp(jnp.where(causal, gcol - grow, NEG_INF))
            a = jnp.where(strict, raw[:DN_CHUNK] * decay, 0.0)
            qks.append(jnp.where(causal, raw[DN_CHUNK:] * decay, 0.0))
            pws.append(-a)
            ts.append(eye - a)
        pws = [_mm(pw, _pair_diag(pw, lo)) for pw in pws]
        for _ in range(4):
            rs = [_mm(jnp.concatenate([pw, t], axis=0), _pair_diag(pw, lo)) for pw, t in zip(pws, ts)]
            pws = [r[:DN_CHUNK] for r in rs]
            ts = [t + r[DN_CHUNK:] for t, r in zip(ts, rs)]
        rs = [_mm(t, _pair_diag(pw, lo)) for pw, t in zip(pws, ts)]
        ts = [t + r for t, r in zip(ts, rs)]
        sols = [_mm(t, jnp.concatenate([_pair_diag(sl("vb", b, p), lo), _pair_diag(sl("wr", b, p), lo)],
                                       axis=1)) for (b, p), t in zip(probs, ts)]
        qkuws = [_mm(qk, jnp.concatenate([_pair_diag(s[:, :PAIR], lo), _pair_diag(s[:, PAIR:], lo)], axis=1))
                 for qk, s in zip(qks, sols)]
        crosses, gls = [], []
        for (b, p), s in zip(probs, sols):
            gam_last = pre[b]["gam"][r1 - 1:r1, p * PAIR:(p + 1) * PAIR]
            kd = sl("k", b, p) * jnp.exp(gam_last - sl("gam", b, p))
            crosses.append(_mm_tn(kd, s))
            gls.append(jnp.exp(gam_last))
        lhs = [jnp.concatenate([pick(cr[:, PAIR:]), sl("qg", b, p) - qkuw[:, PAIR:]], axis=0)
               for (b, p), cr, qkuw in zip(probs, crosses, qkuws)]
        s_olds = [s_scr[b, p] for b, p in probs]
        rs = [_mm(l, _pair_diag(s_old, lo)) for l, s_old in zip(lhs, s_olds)]
        o_pairs = [[] for _ in range(nb)]
        for (b, p), r, s_old, gl, cr, qkuw in zip(probs, rs, s_olds, gls, crosses, qkuws):
            s_scr[b, p] = gl * s_old - r[:DN_DK] + pick(cr[:, :PAIR])
            o_pairs[b].append(r[DN_DK:] + qkuw[:, :PAIR])
        for b in range(nb):
            o_rows[b].append(jnp.concatenate(o_pairs[b], axis=1))

    o_all = jnp.concatenate([jnp.concatenate(rows, axis=0) for rows in o_rows], axis=0)
    inv_rms = lax.rsqrt(_head_sums(o_all * o_all, hsum) * (1.0 / DN_DV) + EPS)
    for b in range(nb):
        rows = slice(b * GDN_TB, (b + 1) * GDN_TB)
        o_ref[b] = (o_all[rows] * inv_rms[rows] * dnx_ref[...] * _silu(dz_ref[b])).astype(o_ref.dtype)

    @pl.when(i == pl.num_programs(0) - 1)
    def _():
        for b in range(nb):
            for p in range(N_PAIRS):
                s_p = s_scr[b, p]
                s_out_ref[b, 2 * p] = s_p[:, :DN_DV]
                s_out_ref[b, 2 * p + 1] = s_p[:, DN_DV:]


def _gdn_consts():
    lane = np.arange(DN_WIDTH)
    pl_lane = np.arange(PAIR)
    hsum = (pl_lane[:, None] // DN_DV == pl_lane[None, :] // DN_DV)
    src = np.arange(LANES)
    expb = (src[:, None] == lane[None, :] // DN_DV)
    expg = (src[:, None] == DN_HEADS + lane[None, :] // DN_DV)
    tok = np.arange(GDN_TB)
    ltri = np.logical_and(tok[:, None] >= tok[None, :],
                          tok[:, None] // DN_CHUNK == tok[None, :] // DN_CHUNK)
    as_bf16 = lambda m: jnp.asarray(m.astype(np.float32), dtype=BF16)
    return as_bf16(hsum), as_bf16(expb), as_bf16(expg), as_bf16(ltri)


def _gdn_prompt(xc, dz, ba, alog, dtb, dnx, batch, seq):
    nt = seq // GDN_TB
    hsum, expb, expg, ltri = _gdn_consts()
    row = lambda n: pl.BlockSpec((batch, GDN_TB, n), lambda i: (0, i, 0))
    full = lambda a: pl.BlockSpec(a.shape, lambda i: (0,) * a.ndim)
    consts = (alog, dtb, dnx, hsum, expb, expg, ltri)
    as3d = lambda a: a.reshape(batch, seq, a.shape[-1])
    o, s = pl.pallas_call(
        _gdn_prompt_kernel,
        grid=(nt,),
        in_specs=[row(CONV_CH), row(DN_WIDTH), row(LANES)] + [full(a) for a in consts],
        out_specs=[row(DN_WIDTH),
                   pl.BlockSpec((batch, DN_HEADS, DN_DK, DN_DV), lambda i: (0, 0, 0, 0))],
        out_shape=[jax.ShapeDtypeStruct((batch, seq, DN_WIDTH), BF16),
                   jax.ShapeDtypeStruct((batch, DN_HEADS, DN_DK, DN_DV), F32)],
        scratch_shapes=[pltpu.VMEM((batch, N_PAIRS, DN_DK, PAIR), F32)],
        compiler_params=_params("arbitrary"),
        name="gdn_prompt",
    )(as3d(xc), as3d(dz), as3d(ba), *consts)
    return o.reshape(batch * seq, DN_WIDTH), s


GDN_S_BB = 8


def _gdn_sample_kernel(xc_ref, dz_ref, ba_ref, sc_ref, s_ref, cw_ref, alog_ref, dtb_ref, dn_ref,
                       hsum_ref, eye_ref, hsel_ref, hrep3_ref, o_ref, s_out_ref):
    xc = xc_ref[...]
    y = sc_ref[0] * cw_ref[0:1, :]
    y = y + sc_ref[1] * cw_ref[1:2, :]
    y = y + sc_ref[2] * cw_ref[2:3, :]
    y = _silu(y + xc * cw_ref[3:4, :])
    hsum = hsum_ref[...]
    q = y[:, :DN_WIDTH]
    k = y[:, DN_WIDTH:2 * DN_WIDTH]
    v = y[:, 2 * DN_WIDTH:]
    q = q * lax.rsqrt(_mm_sel_rhs(q * q, hsum) + EPS) * (DN_DK ** -0.5)
    k = k * lax.rsqrt(_mm_sel_rhs(k * k, hsum) + EPS)
    beta_c, g_c = _gdn_gates(ba_ref[...], alog_ref[...], dtb_ref[...])
    eg_c = jnp.exp(g_c)
    eye = eye_ref[...]
    tr = lambda a: lax.dot_general(a, eye, (((0,), (0,)), ((), ())), precision=lax.Precision.HIGHEST,
                                   preferred_element_type=F32)
    gates_t = tr(jnp.concatenate([beta_c, eg_c], axis=1))
    beta_t = gates_t[:LANES]
    eg_t = gates_t[LANES:]
    dz = dz_ref[...]
    dn = dn_ref[...]
    split = lambda r: jnp.concatenate([r[:, h * DN_DV:(h + 1) * DN_DV] for h in range(DN_HEADS)], axis=0)
    own_head = hsel_ref[...].astype(F32)
    hrep3 = hrep3_ref[...]
    seqs = range(GDN_S_BB)
    dot = lambda a, b: jnp.dot(a.astype(BF16), b.astype(BF16), preferred_element_type=F32)

    def pieces(x):
        p1 = x.astype(BF16).astype(F32)
        r1 = x - p1
        p2 = r1.astype(BF16).astype(F32)
        return p1, p2, (r1 - p2).astype(BF16).astype(F32)

    heads = DN_HEADS
    k_pieces, kqs = [], []
    for b in seqs:
        kq_bd = jnp.concatenate([own_head * k[b:b + 1, :], own_head * q[b:b + 1, :]], axis=0)
        a1, a2, a3 = pieces(kq_bd)
        s1, s2, s3 = pieces(s_ref[b])
        r1 = dot(jnp.concatenate([a1, a2, a3], axis=0), s1)
        r2 = dot(jnp.concatenate([a1, a2], axis=0), s2)
        r3 = dot(a1, s3)
        n = 2 * heads
        kqs.append(((r3 + r2[n:] + r1[2 * n:]) + (r2[:n] + r1[n:2 * n])) + r1[:n])
        k_pieces.append((a1[:heads], a2[:heads], a3[:heads]))
    egs = [eg_t[DN_HEADS:2 * DN_HEADS, b:b + 1] for b in seqs]
    qks = [jnp.sum(split(q[b:b + 1, :]) * split(k[b:b + 1, :]), axis=-1, keepdims=True) for b in seqs]
    v_news = [beta_t[0:DN_HEADS, b:b + 1] * (split(v[b:b + 1, :]) - eg * kq[:heads])
              for b, eg, kq in zip(seqs, egs, kqs)]
    os_ = [eg * kq[heads:] + qk * v_new for eg, kq, qk, v_new in zip(egs, kqs, qks, v_news)]
    inv_rms = [lax.rsqrt(jnp.mean(o * o, axis=-1, keepdims=True) + EPS) for o in os_]
    for b, o, r in zip(seqs, os_, inv_rms):
        o_ref[b] = o * r * dn * _silu(split(dz[b:b + 1, :]))
    outers, egrows = [], []
    for (k1, k2, k3), v_new, eg in zip(k_pieces, v_news, egs):
        v1, v2, v3 = pieces(v_new)
        lhs = jnp.concatenate([k1, k1, k2, k1, k2, k3], axis=0).astype(BF16)
        rhs = jnp.concatenate([v1, v2, v1, v3, v2, v1], axis=0).astype(BF16)
        outers.append(lax.dot_general(lhs, rhs, (((0,), (0,)), ((), ())), preferred_element_type=F32))
        egrows.append(dot(hrep3, jnp.concatenate(pieces(jnp.broadcast_to(eg, (DN_HEADS, DN_DV))), axis=0)))
    for b, outer, egrow in zip(seqs, outers, egrows):
        s_out_ref[b] = s_ref[b] * egrow + outer


def _gdn_sample(xc, dz, ba, sconv_t, state, conv_w, alog, dtb, dn):
    nseq = xc.shape[0]
    lane = np.arange(DN_WIDTH)
    hsum = jnp.asarray((lane[:, None] // DN_DV == lane[None, :] // DN_DV).astype(np.float32), dtype=BF16)
    eye = jnp.eye(GDN_S_BB, dtype=F32)
    hsel_np = (np.arange(DN_HEADS)[:, None] == lane[None, :] // DN_DK).astype(np.float32)
    hsel = jnp.asarray(hsel_np, dtype=BF16)
    hrep3 = jnp.asarray(np.tile(hsel_np.T, (1, 3)), dtype=BF16)
    row = lambda n: pl.BlockSpec((GDN_S_BB, n), lambda i: (i, 0))
    full = lambda a: pl.BlockSpec(a.shape, lambda i: (0,) * a.ndim)
    st = pl.BlockSpec((GDN_S_BB, DN_HEADS * DN_DK, DN_DV), lambda i: (i, 0, 0))
    consts = (conv_w, alog, dtb, dn, hsum, eye, hsel, hrep3)
    return pl.pallas_call(
        _gdn_sample_kernel,
        grid=(nseq // GDN_S_BB,),
        in_specs=[row(CONV_CH), row(DN_WIDTH), row(LANES),
                  pl.BlockSpec((CONV_WIDTH - 1, GDN_S_BB, CONV_CH), lambda i: (0, i, 0)), st]
                 + [full(a) for a in consts],
        out_specs=[pl.BlockSpec((GDN_S_BB, DN_HEADS, DN_DV), lambda i: (i, 0, 0)), st],
        out_shape=[jax.ShapeDtypeStruct((nseq, DN_HEADS, DN_DV), F32),
                   jax.ShapeDtypeStruct(state.shape, F32)],
        compiler_params=_params("parallel"),
        name="gdn_sample",
    )(xc, dz, ba, sconv_t, state, *consts)


def _attn_sample_lanes_kernel(att_ref, ck_ref, cv_ref, bucket_ref, rb_ref, sink_ref, o_ref, s_scr):
    g = pl.program_id(0)
    nseq = att_ref.shape[0]
    rnd = lambda a: a.astype(BF16).astype(F32)
    att = att_ref[...]
    q_all_t = (att[:, :ATT_WIDTH] * (HEAD_DIM ** -0.5)).T
    kv_new_t = att[:, ATT_WIDTH:].T
    qsel = [jnp.where(g == 0, q_all_t[hh * HEAD_DIM:(hh + 1) * HEAD_DIM],
                      q_all_t[(GQA + hh) * HEAD_DIM:(GQA + hh + 1) * HEAD_DIM]) for hh in range(GQA)]
    qr = [rnd(q) for q in qsel]
    kn = rnd(jnp.where(g == 0, kv_new_t[0:HEAD_DIM], kv_new_t[HEAD_DIM:2 * HEAD_DIM]))
    vn = rnd(jnp.where(g == 0, kv_new_t[2 * HEAD_DIM:3 * HEAD_DIM], kv_new_t[3 * HEAD_DIM:]))

    def score_row(j, carry):
        kj = rnd(ck_ref[j, 0])
        for hh in range(GQA):
            s_scr[hh, pl.ds(j, 1), :] = jnp.sum(qr[hh] * kj, axis=0, keepdims=True)
        return carry
    lax.fori_loop(0, WINDOW, score_row, 0, unroll=2)

    bucket = bucket_ref[...]
    jrow = lax.broadcasted_iota(jnp.int32, (WINDOW, nseq), 0)
    prn = []
    for hh in range(GQA):
        h = g * GQA + hh
        bias = jnp.where(jrow >= 1, _bias_lookup(bucket, rb_ref, h), NEG_INF)
        s = s_scr[hh] + bias
        s_n = jnp.sum(qr[hh] * kn, axis=0, keepdims=True) + rb_ref[0, h]
        sink = sink_ref[h]
        m = jnp.maximum(jnp.maximum(jnp.max(s, axis=0, keepdims=True), s_n), sink)
        p = jnp.exp(s - m)
        p_n = jnp.exp(s_n - m)
        den = jnp.sum(p, axis=0, keepdims=True) + p_n + jnp.exp(sink - m)
        s_scr[hh] = rnd(p / den)
        prn.append(rnd(p_n / den))

    def value_row(j, acc):
        vj = rnd(cv_ref[j, 0])
        return tuple(acc[hh] + s_scr[hh, pl.ds(j, 1), :] * vj for hh in range(GQA))
    zero = jnp.zeros((HEAD_DIM, nseq), F32)
    acc = lax.fori_loop(0, WINDOW, value_row, (zero,) * GQA, unroll=2)
    for hh in range(GQA):
        o_ref[hh * HEAD_DIM:(hh + 1) * HEAD_DIM, :] = acc[hh] + prn[hh] * vn


def _attn_sample_lanes(att, ck_t, cv_t, rel_bias, sink):
    nseq = att.shape[0]
    assert nseq == LANES
    bucket = jnp.asarray(np.broadcast_to(_t5_bucket_np(WINDOW - np.arange(WINDOW))[:, None], (WINDOW, nseq)))
    smem = pl.BlockSpec(memory_space=pltpu.SMEM)
    cache = pl.BlockSpec((WINDOW, 1, HEAD_DIM, nseq), lambda g: (0, g, 0, 0))
    full = lambda a: pl.BlockSpec(a.shape, lambda g: (0,) * a.ndim)
    return pl.pallas_call(
        _attn_sample_lanes_kernel,
        grid=(ATT_KV_HEADS,),
        in_specs=[full(att), cache, cache, full(bucket), smem, smem],
        out_specs=pl.BlockSpec((GQA * HEAD_DIM, nseq), lambda g: (g, 0)),
        out_shape=jax.ShapeDtypeStruct((ATT_WIDTH, nseq), F32),
        scratch_shapes=[pltpu.VMEM((GQA, WINDOW, nseq), F32)],
        compiler_params=_params("arbitrary"),
        name="attn_sample_lanes",
    )(att, ck_t, cv_t, bucket, rel_bias, sink)


def _gdn_sample_front_kernel(xc_ref, dz_ref, ba_ref, sc_ref, cw_ref, alog_ref, dtb_ref, hsum_ref,
                             q_ref, k_ref, v_ref, dz_t_ref, gates_ref):
    xc = xc_ref[...]
    y = sc_ref[0] * cw_ref[0:1, :]
    y = y + sc_ref[1] * cw_ref[1:2, :]
    y = y + sc_ref[2] * cw_ref[2:3, :]
    y = _silu(y + xc * cw_ref[3:4, :])
    hsum = hsum_ref[...]
    q = y[:, :DN_WIDTH]
    k = y[:, DN_WIDTH:2 * DN_WIDTH]
    q = q * lax.rsqrt(_mm_sel_rhs(q * q, hsum) + EPS) * (DN_DK ** -0.5)
    k = k * lax.rsqrt(_mm_sel_rhs(k * k, hsum) + EPS)
    beta_c, g_c = _gdn_gates(ba_ref[...], alog_ref[...], dtb_ref[...])
    q_ref[...] = q.T
    k_ref[...] = k.T
    v_ref[...] = y[:, 2 * DN_WIDTH:].T
    dz_t_ref[...] = dz_ref[...].T
    gates_ref[0:LANES, :] = beta_c.T
    gates_ref[LANES:, :] = jnp.exp(g_c).T


def _gdn_sample_step_kernel(q_ref, k_ref, v_ref, dz_ref, gates_ref, dn_ref, s_ref, o_ref, s_out_ref):
    h = pl.program_id(0)
    beta = gates_ref[pl.ds(h, 1), :]
    eg = gates_ref[pl.ds(LANES + DN_HEADS + h, 1), :]
    q, k, v = q_ref[...], k_ref[...], v_ref[...]
    w = (k * beta) * eg
    qg = q * eg
    ws = jnp.zeros(v.shape, F32)
    qs = jnp.zeros(v.shape, F32)
    for dk in range(DN_DK):
        s_dk = s_ref[0, dk]
        ws = ws + w[dk:dk + 1, :] * s_dk
        qs = qs + qg[dk:dk + 1, :] * s_dk
    v_new = v * beta - ws
    qk = jnp.sum(q * k, axis=0, keepdims=True)
    o = qs + qk * v_new
    for dk in range(DN_DK):
        s_out_ref[0, dk] = s_ref[0, dk] * eg + k[dk:dk + 1, :] * v_new
    o = o * lax.rsqrt(jnp.mean(o * o, axis=0, keepdims=True) + EPS) * dn_ref[...]
    o_ref[...] = o * _silu(dz_ref[...])


def _gdn_sample_lanes(xc, dz, ba, sconv_t, state_t, conv_w, alog, dtb, dn):
    nseq = xc.shape[0]
    assert nseq == LANES
    lane = np.arange(DN_WIDTH)
    hsum = jnp.asarray((lane[:, None] // DN_DV == lane[None, :] // DN_DV).astype(np.float32), dtype=BF16)
    full = lambda a: pl.BlockSpec(a.shape, lambda i: (0,) * a.ndim)
    cm = jax.ShapeDtypeStruct((DN_WIDTH, nseq), F32)
    front_in = (xc, dz, ba, sconv_t, conv_w, alog, dtb, hsum)
    q_t, k_t, v_t, dz_t, gates_t = pl.pallas_call(
        _gdn_sample_front_kernel,
        grid=(1,),
        in_specs=[full(a) for a in front_in],
        out_specs=[pl.BlockSpec((DN_WIDTH, nseq), lambda i: (0, 0))] * 4
                  + [pl.BlockSpec((2 * LANES, nseq), lambda i: (0, 0))],
        out_shape=[cm, cm, cm, cm, jax.ShapeDtypeStruct((2 * LANES, nseq), F32)],
        compiler_params=_params("arbitrary"),
        name="gdn_sample_front",
    )(*front_in)
    dn_b = jnp.broadcast_to(dn.reshape(DN_DV, 1), (DN_DV, nseq))
    head = pl.BlockSpec((DN_DK, nseq), lambda h: (h, 0))
    st = pl.BlockSpec((1, DN_DK, DN_DV, nseq), lambda h: (h, 0, 0, 0))
    return pl.pallas_call(
        _gdn_sample_step_kernel,
        grid=(DN_HEADS,),
        in_specs=[head, head, head, head, full(gates_t), full(dn_b), st],
        out_specs=[head, st],
        out_shape=[cm, jax.ShapeDtypeStruct(state_t.shape, F32)],
        compiler_params=_params("parallel"),
        name="gdn_sample_step",
    )(q_t, k_t, v_t, dz_t, gates_t, dn_b, state_t)


def _route(xn, wr):
    logits = jnp.dot(xn, wr, preferred_element_type=F32)
    lane = lax.broadcasted_iota(jnp.int32, logits.shape, 1).astype(F32)
    first_at = lambda hit: jnp.min(jnp.where(hit, lane, float(LANES)), axis=-1, keepdims=True)
    glog = jnp.where(lane < N_GROUPS, logits, NEG_INF)
    gmax = jnp.max(glog, axis=-1, keepdims=True)
    gsel = first_at(glog == gmax)
    pgsel = 1.0 / jnp.sum(jnp.exp(glog - gmax), axis=-1, keepdims=True)
    lo = ROUTER_OFF + gsel * EXPERTS_PER_GROUP
    in_group = jnp.logical_and(lane >= lo, lane < lo + EXPERTS_PER_GROUP)
    elog = jnp.where(in_group, logits, NEG_INF)
    m1 = jnp.max(elog, axis=-1, keepdims=True)
    i1 = first_at(elog == m1)
    z = jnp.sum(jnp.exp(elog - m1), axis=-1, keepdims=True)
    elog2 = jnp.where(lane == i1, NEG_INF, elog)
    m2 = jnp.max(elog2, axis=-1, keepdims=True)
    i2 = first_at(elog2 == m2)
    p1 = 1.0 / z
    p2 = jnp.exp(m2 - m1) / z
    tot = p1 + p2
    return lane, i1, i2, p1 / tot * pgsel, p2 / tot * pgsel


def _outproj(x_ref, oa_ref, od_ref, wo_ref):
    return x_ref[...] + _mm(oa_ref[...], wo_ref[:ATT_WIDTH, :]) + _mm(od_ref[...], wo_ref[ATT_WIDTH:, :])


def _outproj_router_kernel(x_ref, oa_ref, od_t_ref, wo_ref, g_ref, wr_ref, h_ref, xn_ref, gate_ref):
    h = (x_ref[...] + _mm(oa_ref[...], wo_ref[:ATT_WIDTH, :])
         + _mm(od_t_ref[...].T, wo_ref[ATT_WIDTH:, :]))
    h_ref[...] = h
    xn = _rmsnorm(h, g_ref[...]).astype(BF16)
    xn_ref[...] = xn
    lane, i1, i2, g1, g2 = _route(xn, wr_ref[...])
    gate_ref[...] = jnp.where(lane == i1, g1, 0.0) + jnp.where(lane == i2, g2, 0.0)


def _outproj_router(x, oa, od_t, wo, g, wr):
    t = x.shape[0]
    tm = t
    row = lambda n: pl.BlockSpec((tm, n), lambda i: (i, 0))
    full = lambda a: pl.BlockSpec(a.shape, lambda i: (0,) * a.ndim)
    return pl.pallas_call(
        _outproj_router_kernel,
        grid=(t // tm,),
        in_specs=[row(D_MODEL), row(ATT_WIDTH), full(od_t), full(wo), full(g), full(wr)],
        out_specs=[row(D_MODEL), row(D_MODEL), row(LANES)],
        out_shape=[jax.ShapeDtypeStruct((t, D_MODEL), F32), jax.ShapeDtypeStruct((t, D_MODEL), BF16),
                   jax.ShapeDtypeStruct((t, LANES), F32)],
        compiler_params=_params("parallel"),
        name="outproj_router",
    )(x, oa, od_t, wo, g, wr)


def _moe_kernel(xn_ref, gate_ref, wg_ref, wu_ref, wd_ref, o_ref):
    e = pl.program_id(1)
    xn = xn_ref[...]
    lane = lax.broadcasted_iota(jnp.int32, gate_ref.shape, 1)
    gate = jnp.sum(jnp.where(lane == e + ROUTER_OFF, gate_ref[...], 0.0), axis=-1, keepdims=True)
    hg = jnp.dot(xn, wg_ref[...].astype(BF16), preferred_element_type=F32)
    hu = jnp.dot(xn, wu_ref[...].astype(BF16), preferred_element_type=F32)
    hm = _silu(hg) * hu * gate
    y = jnp.dot(hm.astype(BF16), wd_ref[...].astype(BF16), preferred_element_type=F32)

    @pl.when(e == 0)
    def _():
        o_ref[...] = y

    @pl.when(e > 0)
    def _():
        o_ref[...] += y


def _moe(xn, gates, wg, wu, wd):
    t = xn.shape[0]
    tm = min(t, 1024)
    return pl.pallas_call(
        _moe_kernel,
        grid=(t // tm, N_EXPERTS),
        in_specs=[pl.BlockSpec((tm, D_MODEL), lambda i, e: (i, 0)),
                  pl.BlockSpec((tm, LANES), lambda i, e: (i, 0)),
                  pl.BlockSpec((None, D_MODEL, D_EXPERT), lambda i, e: (e, 0, 0)),
                  pl.BlockSpec((None, D_MODEL, D_EXPERT), lambda i, e: (e, 0, 0)),
                  pl.BlockSpec((None, D_EXPERT, D_MODEL), lambda i, e: (e, 0, 0))],
        out_specs=pl.BlockSpec((tm, D_MODEL), lambda i, e: (i, 0)),
        out_shape=jax.ShapeDtypeStruct((t, D_MODEL), F32),
        compiler_params=_params("parallel", "arbitrary"),
        name="moe",
    )(xn, gates, wg, wu, wd)


MOE_TM = 512
POS_TM = 1024
INFO_G1, INFO_G2, INFO_E1, INFO_E2 = 0, 1, 2, 3
DMA_UNROLL = 8


def _moe_tiles(t):
    return (2 * t) // MOE_TM + N_EXPERTS


HALF = D_MODEL // 2
U32 = jnp.uint32


def _pack_rows(x):
    bits = lambda v: lax.bitcast_convert_type(v.astype(BF16).astype(F32), U32)
    return bits(x[:, HALF:]) | (bits(x[:, :HALF]) >> 16)


def _unpack_rows(w):
    lo = lax.bitcast_convert_type(w << 16, F32)
    hi = lax.bitcast_convert_type(w & jnp.uint32(0xFFFF0000), F32)
    return lo, hi


def _route_kernel(x_ref, oa_ref, od_ref, wo_ref, g_ref, wr_ref, h_ref, xn_ref, info_ref, cnt_ref, run_scr):
    h = _outproj(x_ref, oa_ref, od_ref, wo_ref)
    h_ref[...] = h
    xn = _rmsnorm(h, g_ref[...])
    xn_ref[...] = _pack_rows(xn)
    lane, i1, i2, g1, g2 = _route(xn.astype(BF16), wr_ref[...])
    info = jnp.where(lane == INFO_G1, g1, 0.0) + jnp.where(lane == INFO_G2, g2, 0.0)
    info = info + jnp.where(lane == INFO_E1, i1, 0.0) + jnp.where(lane == INFO_E2, i2, 0.0)
    info_ref[...] = info

    @pl.when(pl.program_id(0) == 0)
    def _():
        run_scr[...] = jnp.zeros(run_scr.shape, F32)
    picked = jnp.logical_or(lane == i1, lane == i2).astype(F32)
    run_scr[...] += jnp.sum(picked, axis=0, keepdims=True)
    cnt_ref[...] = run_scr[...]


def _route_sparse(x, oa, od, wo, g, wr):
    t = x.shape[0]
    tm = ROW_TM
    row = lambda n: pl.BlockSpec((tm, n), lambda i: (i, 0))
    full = lambda a: pl.BlockSpec(a.shape, lambda i: (0,) * a.ndim)
    return pl.pallas_call(
        _route_kernel,
        grid=(t // tm,),
        in_specs=[row(D_MODEL), row(ATT_WIDTH), row(DN_WIDTH), full(wo), full(g), full(wr)],
        out_specs=[row(D_MODEL), row(HALF), row(LANES), pl.BlockSpec((1, LANES), lambda i: (0, 0))],
        out_shape=[jax.ShapeDtypeStruct((t, D_MODEL), F32), jax.ShapeDtypeStruct((t, HALF), U32),
                   jax.ShapeDtypeStruct((t, LANES), F32), jax.ShapeDtypeStruct((1, LANES), F32)],
        scratch_shapes=[pltpu.VMEM((1, LANES), F32)],
        compiler_params=_params("arbitrary"),
        name="route",
    )(x, oa, od, wo, g, wr)


def _positions_kernel(info_ref, cnt_ref, ltri_ref, utri_ref, pos_ref, run_scr, off_scr):
    info = info_ref[...]
    lane = lax.broadcasted_iota(jnp.int32, info.shape, 1).astype(F32)
    hit1 = lane == info[:, INFO_E1:INFO_E1 + 1]
    hit2 = lane == info[:, INFO_E2:INFO_E2 + 1]
    onehot = jnp.logical_or(hit1, hit2).astype(F32)

    @pl.when(pl.program_id(0) == 0)
    def _():
        ln = lax.broadcasted_iota(jnp.int32, cnt_ref.shape, 1)
        is_expert = jnp.logical_and(ln >= ROUTER_OFF, ln < ROUTER_OFF + N_EXPERTS)
        tiles = jnp.where(is_expert, jnp.maximum(jnp.floor((cnt_ref[...] + (MOE_TM - 1)) * (1.0 / MOE_TM)), 1.0), 0.0)
        off_scr[...] = MOE_TM * jnp.dot(tiles.astype(BF16), utri_ref[...], preferred_element_type=F32)
        run_scr[...] = jnp.zeros(run_scr.shape, F32)

    before = (jnp.dot(ltri_ref[...], onehot.astype(BF16), preferred_element_type=F32)
              + run_scr[...] + off_scr[...])
    pos1 = jnp.sum(jnp.where(hit1, before, 0.0), axis=-1, keepdims=True)
    pos2 = jnp.sum(jnp.where(hit2, before, 0.0), axis=-1, keepdims=True)
    pos_ref[...] = (jnp.where(lane == 0, pos1, 0.0) + jnp.where(lane == 1, pos2, 0.0)).astype(jnp.int32)
    run_scr[...] += jnp.sum(onehot, axis=0, keepdims=True)


def _positions(info, cnt):
    t = info.shape[0]
    tm = min(t, POS_TM)
    tok = np.arange(tm)
    ltri = jnp.asarray((tok[:, None] > tok[None, :]).astype(np.float32), dtype=BF16)
    ln = np.arange(LANES)
    utri = jnp.asarray((ln[:, None] < ln[None, :]).astype(np.float32), dtype=BF16)
    full = lambda a: pl.BlockSpec(a.shape, lambda i: (0,) * a.ndim)
    return pl.pallas_call(
        _positions_kernel,
        grid=(t // tm,),
        in_specs=[pl.BlockSpec((tm, LANES), lambda i: (i, 0)), full(cnt), full(ltri), full(utri)],
        out_specs=pl.BlockSpec((tm, LANES), lambda i: (i, 0)),
        out_shape=jax.ShapeDtypeStruct((t, LANES), jnp.int32),
        scratch_shapes=[pltpu.VMEM((1, LANES), F32), pltpu.VMEM((1, LANES), F32)],
        compiler_params=_params("arbitrary"),
        name="positions",
    )(info, cnt, ltri, utri)


def _row_copy(src_hbm, src_row, dst_hbm, dst_row, sem):
    return pltpu.make_async_copy(src_hbm.at[pl.ds(src_row, 1)], dst_hbm.at[pl.ds(dst_row, 1)], sem)


SCATTER_SLOTS = 3


def _scatter_kernel(pos1_ref, pos2_ref, last_ref, used_ref, nt_ref, xn_hbm, zero_hbm, xs_hbm,
                    buf, lsem, sem, zsem, *, n_tok):
    max_tiles = xs_hbm.shape[0] // MOE_TM

    def zero_tile(tile):
        return pltpu.make_async_copy(zero_hbm, xs_hbm.at[pl.ds(tile * MOE_TM, MOE_TM)], zsem)

    def for_unused(fn):
        def body(tile, carry):
            fn(tile)
            return carry
        lax.fori_loop(nt_ref[0], max_tiles, body, 0)

    for e in range(N_EXPERTS):
        @pl.when(used_ref[e] > 0)
        def _():
            zero_tile(last_ref[e]).start()
    for_unused(lambda tile: zero_tile(tile).start())
    for e in range(N_EXPERTS):
        @pl.when(used_ref[e] > 0)
        def _():
            zero_tile(last_ref[e]).wait()
    for_unused(lambda tile: zero_tile(tile).wait())

    tm = buf.shape[1]
    n = n_tok // tm

    def load(i):
        return pltpu.make_async_copy(xn_hbm.at[pl.ds(i * tm, tm)], buf.at[i % SCATTER_SLOTS],
                                     lsem.at[i % SCATTER_SLOTS])

    def wait_rows(slot):
        pltpu.make_async_copy(xs_hbm.at[pl.ds(0, 2 * tm)], xs_hbm.at[pl.ds(0, 2 * tm)], sem.at[slot]).wait()

    load(0).start()
    load(1).start()

    def step(i, carry):
        slot = i % SCATTER_SLOTS
        load(i).wait()

        def body(j, c2):
            tok = i * tm + j
            src = buf.at[slot, pl.ds(j, 1)]
            pltpu.make_async_copy(src, xs_hbm.at[pl.ds(pos1_ref[tok], 1)], sem.at[slot]).start()
            pltpu.make_async_copy(src, xs_hbm.at[pl.ds(pos2_ref[tok], 1)], sem.at[slot]).start()
            return c2
        lax.fori_loop(0, tm, body, 0, unroll=DMA_UNROLL)

        @pl.when(i >= 1)
        def _():
            wait_rows((i - 1) % SCATTER_SLOTS)

        @pl.when(i + 2 < n)
        def _():
            load(i + 2).start()
        return carry
    lax.fori_loop(0, n, step, 0)
    wait_rows((n - 1) % SCATTER_SLOTS)


def _scatter_rows(xn, pos1, pos2, last_tile, used, n_tiles, n_rows):
    t = xn.shape[0]
    zero = jnp.zeros((MOE_TM, D_MODEL), F32)
    any_spec = pl.BlockSpec(memory_space=pl.ANY)
    return pl.pallas_call(
        functools.partial(_scatter_kernel, n_tok=t),
        grid_spec=pltpu.PrefetchScalarGridSpec(
            num_scalar_prefetch=5, grid=(1,),
            in_specs=[any_spec, any_spec], out_specs=any_spec,
            scratch_shapes=[pltpu.VMEM((SCATTER_SLOTS, MOE_TM, D_MODEL), F32),
                            pltpu.SemaphoreType.DMA((SCATTER_SLOTS,)),
                            pltpu.SemaphoreType.DMA((SCATTER_SLOTS,)),
                            pltpu.SemaphoreType.DMA]),
        out_shape=jax.ShapeDtypeStruct((n_rows, D_MODEL), F32),
        compiler_params=_params("arbitrary"),
        name="scatter_rows",
    )(pos1, pos2, last_tile, used, n_tiles, xn, zero)


def _experts_kernel(te_ref, tv_ref, nt_ref, xs_ref, wg_ref, wu_ref, wd_ref, xn_new_ref, gate_new_ref,
                    ys_ref, moe_new_ref, wg_s, wu_s, wd_s):
    i = pl.program_id(0)
    used = i < nt_ref[0]
    expert = te_ref[i]

    @pl.when(jnp.logical_or(i == 0, expert != te_ref[jnp.maximum(i - 1, 0)]))
    def _():
        wg_s[...] = wg_ref[...].astype(BF16)
        wu_s[...] = wu_ref[...].astype(BF16)
        wd_s[...] = wd_ref[...].astype(BF16)
        xn = xn_new_ref[...]
        lane = lax.broadcasted_iota(jnp.int32, gate_new_ref.shape, 1)
        gate = jnp.sum(jnp.where(lane == expert + ROUTER_OFF, gate_new_ref[...], 0.0), axis=-1, keepdims=True)
        hg = jnp.dot(xn, wg_s[...], preferred_element_type=F32)
        hu = jnp.dot(xn, wu_s[...], preferred_element_type=F32)
        hm = _silu(hg) * hu * gate
        y = jnp.dot(hm.astype(BF16), wd_s[...], preferred_element_type=F32)

        @pl.when(i == 0)
        def _():
            moe_new_ref[...] = y

        @pl.when(i > 0)
        def _():
            moe_new_ref[...] += y

    @pl.when(used)
    def _():
        row = lax.broadcasted_iota(jnp.int32, xs_ref.shape, 0)
        x_lo, x_hi = _unpack_rows(jnp.where(row < tv_ref[i], xs_ref[...], jnp.uint32(0)))
        x_lo = x_lo.astype(BF16)
        x_hi = x_hi.astype(BF16)
        up = lambda w_s: (jnp.dot(x_lo, w_s[:HALF, :], preferred_element_type=F32)
                          + jnp.dot(x_hi, w_s[HALF:, :], preferred_element_type=F32))
        hm = (_silu(up(wg_s)) * up(wu_s)).astype(BF16)
        ys_ref[...] = _pack_rows(jnp.dot(hm, wd_s[...], preferred_element_type=F32))

    @pl.when(jnp.logical_not(used))
    def _():
        ys_ref[...] = jnp.zeros(ys_ref.shape, U32)


def _experts(xs, tile_expert, tile_valid, n_tiles, wg, wu, wd, xn_new, gate_new):
    max_tiles = xs.shape[0] // MOE_TM
    rows = pl.BlockSpec((MOE_TM, HALF), lambda i, te, tv, nt: (i, 0))
    wspec = lambda shape: pl.BlockSpec((None,) + shape, lambda i, te, tv, nt: (te[i], 0, 0))
    full = lambda a: pl.BlockSpec(a.shape, lambda i, te, tv, nt: (0,) * a.ndim)
    return pl.pallas_call(
        _experts_kernel,
        grid_spec=pltpu.PrefetchScalarGridSpec(
            num_scalar_prefetch=3, grid=(max_tiles,),
            in_specs=[rows, wspec((D_MODEL, D_EXPERT)), wspec((D_MODEL, D_EXPERT)),
                      wspec((D_EXPERT, D_MODEL)), full(xn_new), full(gate_new)],
            out_specs=[rows, pl.BlockSpec(xn_new.shape, lambda i, te, tv, nt: (0, 0))],
            scratch_shapes=[pltpu.VMEM((D_MODEL, D_EXPERT), BF16), pltpu.VMEM((D_MODEL, D_EXPERT), BF16),
                            pltpu.VMEM((D_EXPERT, D_MODEL), BF16)]),
        out_shape=[jax.ShapeDtypeStruct(xs.shape, U32), jax.ShapeDtypeStruct(xn_new.shape, F32)],
        compiler_params=_params("arbitrary"),
        name="experts",
    )(tile_expert, tile_valid, n_tiles, xs, wg, wu, wd, xn_new, gate_new)


def _ple_gather_kernel(pos1_ref, pos2_ref, h_ref, info_ref, p_ref, wpp_ref, wpg_ref, gp_ref, gf_ref,
                       ys_hbm, y_ref, ybuf, sem):
    i = pl.program_id(0)
    n = pl.num_programs(0)
    tm = h_ref.shape[0]

    def issue(tile, slot):
        def body(j, carry):
            tok = tile * tm + j
            pltpu.make_async_copy(ys_hbm.at[pl.ds(pos1_ref[tok], 1)], ybuf.at[slot, 0, pl.ds(j, 1)],
                                  sem.at[slot]).start()
            pltpu.make_async_copy(ys_hbm.at[pl.ds(pos2_ref[tok], 1)], ybuf.at[slot, 1, pl.ds(j, 1)],
                                  sem.at[slot]).start()
            return carry
        lax.fori_loop(0, tm, body, 0, unroll=DMA_UNROLL)

    @pl.when(i == 0)
    def _():
        issue(0, 0)

    @pl.when(i + 1 < n)
    def _():
        issue(i + 1, (i + 1) % 2)

    slot = i % 2
    pltpu.make_async_copy(ybuf.at[slot], ybuf.at[slot], sem.at[slot]).wait()
    info = info_ref[...]
    moe = info[:, INFO_G1:INFO_G1 + 1] * ybuf[slot, 0] + info[:, INFO_G2:INFO_G2 + 1] * ybuf[slot, 1]
    h = h_ref[...] + moe
    hn = _rmsnorm(h, gp_ref[...])
    h = h + _mm(p_ref[...], wpp_ref[...]) * _sigmoid(_mm(hn, wpg_ref[...]))
    y_ref[...] = _rmsnorm(h, gf_ref[...])


def _ple_gather(h, info, p, ys, pos1, pos2, wpp, wpg, gp, gf):
    t = h.shape[0]
    tm = 256
    row = lambda n: pl.BlockSpec((tm, n), lambda i, p1, p2: (i, 0))
    full = lambda a: pl.BlockSpec(a.shape, lambda i, p1, p2: (0,) * a.ndim)
    return pl.pallas_call(
        _ple_gather_kernel,
        grid_spec=pltpu.PrefetchScalarGridSpec(
            num_scalar_prefetch=2, grid=(t // tm,),
            in_specs=[row(D_MODEL), row(LANES), row(PLE_DIM), full(wpp), full(wpg), full(gp), full(gf),
                      pl.BlockSpec(memory_space=pl.ANY)],
            out_specs=row(D_MODEL),
            scratch_shapes=[pltpu.VMEM((2, 2, tm, D_MODEL), F32), pltpu.SemaphoreType.DMA((2,))]),
        out_shape=jax.ShapeDtypeStruct((t, D_MODEL), F32),
        compiler_params=_params("arbitrary"),
        name="ple_gather",
    )(pos1, pos2, h, info, p, wpp, wpg, gp, gf, ys)


GATHER_PARTS = 4
SC_IDX = 128
SC_ROWS = 64
SC_WORKERS = 32


def _sc_mesh():
    return plsc.VectorSubcoreMesh(core_axis_name="c", subcore_axis_name="s")


def _sc_windows(t, fn):
    per_worker = t // SC_WORKERS
    worker = lax.axis_index(("c", "s"))

    @pl.loop(0, per_worker // SC_IDX)
    def _(w):
        fn(worker * per_worker + w * SC_IDX)


def _sc_scatter_rows(xn, pos1, pos2, n_rows):
    t, d = xn.shape
    assert t % (SC_WORKERS * SC_IDX) == 0
    idx_t = pltpu.VMEM((1, SC_IDX), jnp.int32)

    @pl.kernel(out_type=jax.ShapeDtypeStruct((n_rows, d), xn.dtype), mesh=_sc_mesh(),
               scratch_types=[idx_t, idx_t, pltpu.VMEM((SC_ROWS, d), xn.dtype)])
    def scatter(x_hbm, p1_hbm, p2_hbm, o_hbm, i1_v, i2_v, buf):
        def window(base):
            pltpu.sync_copy(p1_hbm.at[:, pl.ds(base, SC_IDX)], i1_v)
            pltpu.sync_copy(p2_hbm.at[:, pl.ds(base, SC_IDX)], i2_v)
            for k in range(SC_IDX // SC_ROWS):
                pltpu.sync_copy(x_hbm.at[pl.ds(base + k * SC_ROWS, SC_ROWS)], buf)
                pltpu.sync_copy(buf, o_hbm.at[i1_v.at[0, pl.ds(k * SC_ROWS, SC_ROWS)]])
                pltpu.sync_copy(buf, o_hbm.at[i2_v.at[0, pl.ds(k * SC_ROWS, SC_ROWS)]])
        _sc_windows(t, window)

    return scatter(xn, pos1.reshape(1, t), pos2.reshape(1, t))


def _sc_gather_rows(ys, pos1, pos2):
    t = pos1.shape[0]
    d = ys.shape[1]
    assert t % (SC_WORKERS * SC_IDX) == 0
    idx_t = pltpu.VMEM((1, SC_IDX), jnp.int32)
    out = jax.ShapeDtypeStruct((t, d), ys.dtype)

    buf_t = pltpu.VMEM((SC_ROWS, d), ys.dtype)

    @pl.kernel(out_type=(out, out), mesh=_sc_mesh(),
               scratch_types=[idx_t, idx_t, buf_t, buf_t, pltpu.SemaphoreType.DMA((2,)),
                              pltpu.SemaphoreType.DMA((2,))])
    def gather(y_hbm, p1_hbm, p2_hbm, o1_hbm, o2_hbm, i1_v, i2_v, buf_a, buf_b, gsem, wsem):
        bufs = (buf_a, buf_b)

        def window(base):
            pltpu.sync_copy(p1_hbm.at[:, pl.ds(base, SC_IDX)], i1_v)
            pltpu.sync_copy(p2_hbm.at[:, pl.ds(base, SC_IDX)], i2_v)
            items = [(idx_v, o_hbm, k) for k in range(SC_IDX // SC_ROWS)
                     for idx_v, o_hbm in ((i1_v, o1_hbm), (i2_v, o2_hbm))]

            def read(n):
                idx_v, _, k = items[n]
                return pltpu.make_async_copy(y_hbm.at[idx_v.at[0, pl.ds(k * SC_ROWS, SC_ROWS)]],
                                             bufs[n % 2], gsem.at[n % 2])

            def write(n):
                _, o_hbm, k = items[n]
                return pltpu.make_async_copy(bufs[n % 2], o_hbm.at[pl.ds(base + k * SC_ROWS, SC_ROWS)],
                                             wsem.at[n % 2])

            read(0).start()
            for n in range(len(items)):
                read(n).wait()
                if n >= 1:
                    write(n - 1).wait()
                if n + 1 < len(items):
                    read(n + 1).start()
                write(n).start()
            write(len(items) - 1).wait()
        _sc_windows(t, window)

    return gather(ys, pos1.reshape(1, t), pos2.reshape(1, t))


def _ple_sparse_kernel(h_ref, info_ref, y1_ref, y2_ref, p_ref, wpp_ref, wpg_ref, gp_ref, gf_ref, y_ref):
    info = info_ref[...]
    g1 = info[:, INFO_G1:INFO_G1 + 1]
    g2 = info[:, INFO_G2:INFO_G2 + 1]
    y1_lo, y1_hi = _unpack_rows(y1_ref[...])
    y2_lo, y2_hi = _unpack_rows(y2_ref[...])
    moe = jnp.concatenate([g1 * y1_lo + g2 * y2_lo, g1 * y1_hi + g2 * y2_hi], axis=1)
    h = h_ref[...] + moe
    hn = _rmsnorm(h, gp_ref[...])
    h = h + _mm(p_ref[...], wpp_ref[...]) * _sigmoid(_mm(hn, wpg_ref[...]))
    y_ref[...] = _rmsnorm(h, gf_ref[...])


def _ple_sparse_part_kernel(h_ref, info_ref, y1_ref, y2_ref, p_ref, wpp_ref, wpg_ref, gp_ref, gf_ref,
                            prev_ref, y_ref):
    del prev_ref
    _ple_sparse_kernel(h_ref, info_ref, y1_ref, y2_ref, p_ref, wpp_ref, wpg_ref, gp_ref, gf_ref, y_ref)


def _ple_sparse(h, info, y1, y2, p, wpp, wpg, gp, gf, first_row=0, prev=None):
    t = h.shape[0]
    tm = ROW_TM
    off = first_row // tm
    row = lambda n: pl.BlockSpec((tm, n), lambda i: (i + off, 0))
    part = lambda n: pl.BlockSpec((tm, n), lambda i: (i, 0))
    full = lambda a: pl.BlockSpec(a.shape, lambda i: (0,) * a.ndim)
    in_specs = [row(D_MODEL), row(LANES), part(HALF), part(HALF), row(PLE_DIM),
                full(wpp), full(wpg), full(gp), full(gf)]
    args = (h, info, y1, y2, p, wpp, wpg, gp, gf)
    if prev is None:
        body, aliases = _ple_sparse_kernel, {}
    else:
        body, aliases = _ple_sparse_part_kernel, {len(args): 0}
        in_specs = in_specs + [pl.BlockSpec(memory_space=pl.ANY)]
        args = args + (prev,)
    return pl.pallas_call(
        body,
        grid=(y1.shape[0] // tm,),
        in_specs=in_specs,
        out_specs=row(D_MODEL),
        out_shape=jax.ShapeDtypeStruct((t, D_MODEL), F32),
        input_output_aliases=aliases,
        compiler_params=_params("parallel"),
        name="ple_sparse",
    )(*args)


def _tile_tables(cnt, max_tiles):
    tiles_e = jnp.maximum((cnt + (MOE_TM - 1)) // MOE_TM, 1)
    ends = jnp.cumsum(tiles_e)
    n_tiles = ends[-1]
    tile = jnp.arange(max_tiles, dtype=jnp.int32)
    idx = jnp.minimum(tile, n_tiles - 1)
    tile_expert = jnp.sum((idx[:, None] >= ends[None, :]).astype(jnp.int32), axis=1)
    mine = tile_expert[:, None] == jnp.arange(N_EXPERTS, dtype=jnp.int32)[None, :]
    of_mine = lambda v: jnp.sum(jnp.where(mine, v[None, :], 0), axis=1)
    valid = jnp.clip(of_mine(cnt) - (idx - of_mine(ends - tiles_e)) * MOE_TM, 0, MOE_TM)
    tile_valid = jnp.where(tile < n_tiles, valid, 0).astype(jnp.int32)
    return (tile_expert, tile_valid, n_tiles.reshape(1), (ends - 1).astype(jnp.int32),
            tiles_e.astype(jnp.int32))


def _ple_final_kernel(h_ref, m_ref, p_ref, wpp_ref, wpg_ref, gp_ref, gf_ref, y_ref):
    h = h_ref[...] + m_ref[...]
    hn = _rmsnorm(h, gp_ref[...])
    h = h + _mm(p_ref[...], wpp_ref[...]) * _sigmoid(_mm(hn, wpg_ref[...]))
    y_ref[...] = _rmsnorm(h, gf_ref[...])


def _ple_final(h, m, p, wpp, wpg, gp, gf):
    t = h.shape[0]
    tm = min(t, 256)
    row = lambda n: pl.BlockSpec((tm, n), lambda i: (i, 0))
    full = lambda a: pl.BlockSpec(a.shape, lambda i: (0,) * a.ndim)
    return pl.pallas_call(
        _ple_final_kernel,
        grid=(t // tm,),
        in_specs=[row(D_MODEL), row(D_MODEL), row(PLE_DIM), full(wpp), full(wpg), full(gp), full(gf)],
        out_specs=row(D_MODEL),
        out_shape=jax.ShapeDtypeStruct((t, D_MODEL), F32),
        compiler_params=_params("parallel"),
        name="ple_final",
    )(h, m, p, wpp, wpg, gp, gf)


def kernel(x_prompt, x_sample, p_prompt, p_sample, cache_k, cache_v, state_conv, state_S, rel_bias, norm_mix, w_in, att_sink, conv_w, dn_A_log, dn_dt_bias, dn_norm, w_out, norm_ffn, w_router_group, w_router_expert, w_gate, w_up, w_down, w_ple_proj, w_ple_gate, norm_ple, norm_final):
    batch, seq, _ = x_prompt.shape
    nseq = x_sample.shape[0]
    assert x_sample.shape[1] == 1 and norm_mix.shape[0] == 1 and cache_k.shape[2] == WINDOW
    assert seq % GDN_TB == 0 and seq % ATT_BLOCK == 0

    wi = w_in[0]
    o_db = ATT_COLS + CONV_CH
    w_in_re = jnp.concatenate(
        [wi[:, :o_db], wi[:, o_db + 2 * DN_HEADS:], wi[:, o_db:o_db + 2 * DN_HEADS],
         jnp.zeros((D_MODEL, LANES - 2 * DN_HEADS), F32)], axis=1).astype(BF16)
    row = lambda a: a.reshape(1, -1).astype(F32)
    pad_lanes = lambda a, off: jnp.zeros((1, LANES), F32).at[0, off:off + a.shape[0]].set(a)
    alog = pad_lanes(dn_A_log[0], DN_HEADS)
    dtb = pad_lanes(dn_dt_bias[0], DN_HEADS)
    dnx = jnp.tile(dn_norm[0], DN_HEADS).reshape(1, DN_WIDTH)
    w_router = jnp.concatenate(
        [w_router_group[0], w_router_expert[0],
         jnp.zeros((D_MODEL, LANES - N_GROUPS - N_EXPERTS), F32)], axis=1).astype(BF16)
    wo = w_out[0].astype(BF16)
    wg, wu, wd = w_gate[0], w_up[0], w_down[0]
    wpp, wpg = w_ple_proj[0].astype(BF16), w_ple_gate[0].astype(BF16)
    sink = att_sink[0]

    qi = np.arange(ATT_BLOCK)[:, None]
    kj = np.arange(2 * ATT_BLOCK)[None, :]
    bucket_p = jnp.asarray(_t5_bucket_np(qi + ATT_BLOCK - kj))
    bucket_s = jnp.asarray(_t5_bucket_np(WINDOW - np.arange(WINDOW)[None, :]))

    xp = x_prompt.reshape(batch * seq, D_MODEL)
    att_p, qkv_p, dz_p, ba_p, xc_tails = _inproj_conv(xp, row(norm_mix[0]), w_in_re, conv_w[0], seq)
    o_att_p = _attn_prompt(att_p, bucket_p, rel_bias, sink, batch, seq)
    o_dn_p, s_p = _gdn_prompt(qkv_p, dz_p, ba_p, alog, dtb, dnx, batch, seq)
    h1, xn2, info, cnt = _route_sparse(xp, o_att_p, o_dn_p, wo, row(norm_ffn[0]), w_router)
    pos = _positions(info, cnt)
    pos1, pos2 = pos[:, 0], pos[:, 1]
    max_tiles = _moe_tiles(batch * seq)
    cnt_e = cnt[0, ROUTER_OFF:ROUTER_OFF + N_EXPERTS].astype(jnp.int32)
    tile_expert, tile_valid, n_tiles, last_tile, used = _tile_tables(cnt_e, max_tiles)
    xs_sorted = _sc_scatter_rows(xn2, pos1, pos2, max_tiles * MOE_TM)

    xs = x_sample.reshape(nseq, D_MODEL)
    att_s, xc_s, dz_s, ba_s = _inproj(xs, row(norm_mix[0]), w_in_re)
    ck_t = jnp.transpose(cache_k[0], (0, 2, 3, 1))
    cv_t = jnp.transpose(cache_v[0], (0, 2, 3, 1))
    o_att_s, ks_t, vs_t = _attn_sample(att_s, ck_t, cv_t, bucket_s, rel_bias, sink)
    sconv_t = jnp.swapaxes(state_conv[0], 0, 1)
    o_dn_s_t, s_s_t = _gdn_sample_lanes(xc_s, dz_s, ba_s, sconv_t, jnp.transpose(state_S[0], (1, 2, 3, 0)),
                                        conv_w[0], alog, dtb, dn_norm[0])
    s_s = jnp.transpose(s_s_t, (3, 0, 1, 2))

    h1_s, xn2_s, gates_s = _outproj_router(xs, o_att_s, o_dn_s_t, wo, row(norm_ffn[0]), w_router)

    ys, moe_s = _experts(xs_sorted, tile_expert, tile_valid, n_tiles, wg, wu, wd, xn2_s, gates_s)
    y_s = _ple_final(h1_s, moe_s, p_sample[0].reshape(nseq, PLE_DIM), wpp, wpg, row(norm_ple[0]),
                     row(norm_final))
    part_rows = (batch * seq) // GATHER_PARTS
    p_rows = p_prompt[0].reshape(batch * seq, PLE_DIM)
    y_p = None
    for part in range(GATHER_PARTS):
        rows = slice(part * part_rows, (part + 1) * part_rows)
        y1, y2 = _sc_gather_rows(ys, pos1[rows], pos2[rows])
        y_p = _ple_sparse(h1, info, y1, y2, p_rows, wpp, wpg, row(norm_ple[0]), row(norm_final),
                          first_row=part * part_rows, prev=y_p)

    att_p3 = att_p.reshape(batch, seq, ATT_COLS)
    kv_shape = (1, batch, WINDOW, ATT_KV_HEADS, HEAD_DIM)
    k_p = att_p3[:, seq - WINDOW:, ATT_WIDTH:ATT_WIDTH + KV_WIDTH].reshape(kv_shape)
    v_p = att_p3[:, seq - WINDOW:, ATT_WIDTH + KV_WIDTH:].reshape(kv_shape)
    conv_p = xc_tails.reshape(batch, -1, TAIL, CONV_CH)[:, -1, TAIL - (CONV_WIDTH - 1):][None]
    k_s = jnp.transpose(ks_t, (0, 3, 1, 2))[None]
    v_s = jnp.transpose(vs_t, (0, 3, 1, 2))[None]
    conv_s = jnp.concatenate([state_conv[0][:, 1:], xc_s[:, None, :]], axis=1)[None]
    return (y_p.reshape(batch, seq, D_MODEL), y_s.reshape(nseq, 1, D_MODEL),
            k_p, v_p, conv_p, s_p[None], k_s, v_s, conv_s, s_s[None])
```

```python
import functools
import math

import numpy as np
import jax
import jax.numpy as jnp
from jax import lax
from jax.experimental import pallas as pl
from jax.experimental.pallas import tpu as pltpu
from jax.experimental.pallas import tpu_sc as plsc

F32 = jnp.float32
BF16 = jnp.bfloat16

D_MODEL = 1024
ATT_HEADS = 8
ATT_KV_HEADS = 2
HEAD_DIM = 64
GQA = ATT_HEADS // ATT_KV_HEADS
WINDOW = 128
ATT_BLOCK = 128
N_BUCKETS = 32
DN_HEADS = 8
DN_DK = 64
DN_DV = 64
CONV_WIDTH = 4
DN_CHUNK = 64
ATT_WIDTH = ATT_HEADS * HEAD_DIM
KV_WIDTH = ATT_KV_HEADS * HEAD_DIM
DN_WIDTH = DN_HEADS * DN_DV
CONV_CH = 3 * DN_WIDTH
N_GROUPS = 4
EXPERTS_PER_GROUP = 8
N_EXPERTS = N_GROUPS * EXPERTS_PER_GROUP
D_EXPERT = 256
PLE_DIM = 256
EPS = 1e-6
NEG_INF = float("-inf")

ATT_COLS = ATT_WIDTH + 2 * KV_WIDTH
LANES = 128
IN_COLS = ATT_COLS + CONV_CH + DN_WIDTH + LANES
ROUTER_OFF = N_GROUPS
VMEM_LIMIT = 48 * 1024 * 1024
ROW_TM = 512


def _params(*sem):
    return pltpu.CompilerParams(dimension_semantics=sem, vmem_limit_bytes=VMEM_LIMIT)


def _mm(a, b):
    return jnp.dot(a.astype(BF16), b.astype(BF16), preferred_element_type=F32)


def _mm_nt(a, b):
    return lax.dot_general(a.astype(BF16), b.astype(BF16), (((1,), (1,)), ((), ())),
                           preferred_element_type=F32)


def _mm_tn(a, b):
    return lax.dot_general(a.astype(BF16), b.astype(BF16), (((0,), (0,)), ((), ())),
                           preferred_element_type=F32)


def _split3(x):
    h1 = x.astype(BF16)
    r1 = x - h1.astype(F32)
    h2 = r1.astype(BF16)
    h3 = (r1 - h2.astype(F32)).astype(BF16)
    return h1, h2, h3


def _mm_sel_rhs(x, sel):
    h1, h2, h3 = _split3(x)
    d = lambda h: jnp.dot(h, sel, preferred_element_type=F32)
    return d(h1) + d(h2) + d(h3)


def _mm_sel_lhs(sel, x):
    h1, h2, h3 = _split3(x)
    d = lambda h: jnp.dot(sel, h, preferred_element_type=F32)
    return d(h1) + d(h2) + d(h3)


def _mm3(a, b):
    ah = a.astype(BF16)
    al = (a - ah.astype(F32)).astype(BF16)
    bh = b.astype(BF16)
    bl = (b - bh.astype(F32)).astype(BF16)
    d = lambda u, v: jnp.dot(u, v, preferred_element_type=F32)
    return d(ah, bh) + d(ah, bl) + d(al, bh)


def _sigmoid(x):
    return 1.0 / (1.0 + jnp.exp(-x))


def _silu(x):
    return x * _sigmoid(x)


def _softplus(x):
    return jnp.maximum(x, 0.0) + jnp.log1p(jnp.exp(-jnp.abs(x)))


def _rmsnorm(x, g):
    return x * lax.rsqrt(jnp.mean(x * x, axis=-1, keepdims=True) + EPS) * g


def _t5_bucket_np(dist):
    max_exact = N_BUCKETS // 2
    d = np.maximum(dist, 0)
    ratio = (np.log(np.maximum(d, 1).astype(np.float32) / np.float32(max_exact))
             / np.float32(math.log(WINDOW / max_exact))).astype(np.float32)
    large = np.minimum(max_exact + (ratio * np.float32(N_BUCKETS - max_exact)).astype(np.int32),
                       N_BUCKETS - 1)
    return np.where(d < max_exact, d, large).astype(np.int32)


def _bias_lookup(bucket, rb_ref, h):
    acc = jnp.zeros(bucket.shape, F32)
    for t in range(N_BUCKETS):
        acc = jnp.where(bucket == t, rb_ref[t, h], acc)
    return acc


def _inproj_kernel(x_ref, g_ref, w_ref, att_ref, xc_ref, dz_ref, ba_ref):
    xn = _rmsnorm(x_ref[...], g_ref[...]).astype(BF16)
    o0, o1, o2 = ATT_COLS, ATT_COLS + CONV_CH, ATT_COLS + CONV_CH + DN_WIDTH
    att_ref[...] = jnp.dot(xn, w_ref[:, :o0], preferred_element_type=F32)
    xc_ref[...] = jnp.dot(xn, w_ref[:, o0:o1], preferred_element_type=F32)
    dz_ref[...] = jnp.dot(xn, w_ref[:, o1:o2], preferred_element_type=F32)
    ba_ref[...] = jnp.dot(xn, w_ref[:, o2:], preferred_element_type=F32)


def _inproj(x, g, w):
    t = x.shape[0]
    tm = min(t, ROW_TM)
    row = lambda n: pl.BlockSpec((tm, n), lambda i: (i, 0))
    full = lambda a: pl.BlockSpec(a.shape, lambda i: (0,) * a.ndim)
    return pl.pallas_call(
        _inproj_kernel,
        grid=(t // tm,),
        in_specs=[row(D_MODEL), full(g), full(w)],
        out_specs=[row(ATT_COLS), row(CONV_CH), row(DN_WIDTH), row(LANES)],
        out_shape=[jax.ShapeDtypeStruct((t, n), F32) for n in (ATT_COLS, CONV_CH, DN_WIDTH, LANES)],
        compiler_params=_params("parallel"),
        name="inproj",
    )(x, g, w)


TAIL = 8
PAIR = 2 * DN_DK
N_PAIRS = DN_WIDTH // PAIR


def _head_sums(z, pair_ones):
    hi = z.astype(BF16)
    lw = (z - hi.astype(F32)).astype(BF16)
    d = lambda a, p: jnp.dot(a[:, p * PAIR:(p + 1) * PAIR], pair_ones, preferred_element_type=F32)
    return jnp.concatenate([d(hi, p) + d(lw, p) for p in range(N_PAIRS)], axis=1)


def _inproj_conv_kernel(x_ref, g_ref, w_ref, cw_ref, ones_ref, att_ref, qkv_ref, dz_ref, ba_ref, tail_ref,
                        xp_scr, *, tiles_per_seq):
    tm = x_ref.shape[0]

    @pl.when(pl.program_id(0) % tiles_per_seq == 0)
    def _():
        xp_scr[0:TAIL, :] = jnp.zeros((TAIL, CONV_CH), F32)

    xn = _rmsnorm(x_ref[...], g_ref[...]).astype(BF16)
    o0, o1, o2 = ATT_COLS, ATT_COLS + CONV_CH, ATT_COLS + CONV_CH + DN_WIDTH
    xc = jnp.dot(xn, w_ref[:, o0:o1], preferred_element_type=F32)
    att_ref[...] = jnp.dot(xn, w_ref[:, :o0], preferred_element_type=F32)
    dz_ref[...] = jnp.dot(xn, w_ref[:, o1:o2], preferred_element_type=F32)
    ba_ref[...] = jnp.dot(xn, w_ref[:, o2:], preferred_element_type=F32)

    xp_scr[TAIL:, :] = xc
    y = xp_scr[TAIL - 3:TAIL - 3 + tm, :] * cw_ref[0:1, :]
    y = y + xp_scr[TAIL - 2:TAIL - 2 + tm, :] * cw_ref[1:2, :]
    y = y + xp_scr[TAIL - 1:TAIL - 1 + tm, :] * cw_ref[2:3, :]
    y = y + xc * cw_ref[3:4, :]
    tail = xc[tm - TAIL:, :]
    xp_scr[0:TAIL, :] = tail
    tail_ref[0] = tail
    y = _silu(y)
    q = y[:, :DN_WIDTH]
    k = y[:, DN_WIDTH:2 * DN_WIDTH]
    inv_norm = lax.rsqrt(_head_sums(jnp.concatenate([q * q, k * k], axis=0), ones_ref[...]) + EPS)
    qkv_ref[:, :DN_WIDTH] = q * inv_norm[:tm] * (DN_DK ** -0.5)
    qkv_ref[:, DN_WIDTH:2 * DN_WIDTH] = k * inv_norm[tm:]
    qkv_ref[:, 2 * DN_WIDTH:] = y[:, 2 * DN_WIDTH:]


def _pair_ones():
    lane = np.arange(PAIR)
    return jnp.asarray((lane[:, None] // DN_DV == lane[None, :] // DN_DV).astype(np.float32), dtype=BF16)


def _inproj_conv(x, g, w, conv_w, seq):
    t = x.shape[0]
    tm = ROW_TM
    assert seq % tm == 0
    ones = _pair_ones()
    row = lambda n: pl.BlockSpec((tm, n), lambda i: (i, 0))
    full = lambda a: pl.BlockSpec(a.shape, lambda i: (0,) * a.ndim)
    return pl.pallas_call(
        functools.partial(_inproj_conv_kernel, tiles_per_seq=seq // tm),
        grid=(t // tm,),
        in_specs=[row(D_MODEL), full(g), full(w), full(conv_w), full(ones)],
        out_specs=[row(ATT_COLS), row(CONV_CH), row(DN_WIDTH), row(LANES),
                   pl.BlockSpec((1, TAIL, CONV_CH), lambda i: (i, 0, 0))],
        out_shape=[jax.ShapeDtypeStruct((t, n), F32) for n in (ATT_COLS, CONV_CH, DN_WIDTH, LANES)]
                  + [jax.ShapeDtypeStruct((t // tm, TAIL, CONV_CH), F32)],
        scratch_shapes=[pltpu.VMEM((TAIL + tm, CONV_CH), F32)],
        compiler_params=_params("arbitrary"),
        name="inproj_conv",
    )(x, g, w, conv_w, ones)


GROUP_ROWS = GQA * ATT_BLOCK


def _attn_prompt_kernel(cur_ref, prev_ref, bucket_ref, rb_ref, sink_ref, o_ref, bias_scr, sink_scr):
    i = pl.program_id(0)
    nseq = cur_ref.shape[0]

    @pl.when(i == 0)
    def _():
        qi = lax.broadcasted_iota(jnp.int32, (ATT_BLOCK, 2 * ATT_BLOCK), 0)
        kj = lax.broadcasted_iota(jnp.int32, (ATT_BLOCK, 2 * ATT_BLOCK), 1)
        dist = qi + ATT_BLOCK - kj
        band = jnp.logical_and(dist >= 0, dist < WINDOW)
        bucket = bucket_ref[...]
        hrow = lax.broadcasted_iota(jnp.int32, (GROUP_ROWS, 1), 0) // ATT_BLOCK
        for g in range(ATT_KV_HEADS):
            sink_col = jnp.zeros((GROUP_ROWS, 1), F32)
            for hh in range(GQA):
                h = g * GQA + hh
                bias = jnp.where(band, _bias_lookup(bucket, rb_ref, h), NEG_INF)
                bias_scr[0, g, hh * ATT_BLOCK:(hh + 1) * ATT_BLOCK, :] = bias
                bias_scr[1, g, hh * ATT_BLOCK:(hh + 1) * ATT_BLOCK, :] = jnp.where(kj >= ATT_BLOCK, bias, NEG_INF)
                sink_col = jnp.where(hrow == hh, sink_ref[h], sink_col)
            sink_scr[g] = sink_col

    first = (i == 0).astype(jnp.int32)
    probs = [(b, g) for b in range(nseq) for g in range(ATT_KV_HEADS)]
    scores = []
    for b, g in probs:
        cur = cur_ref[b]
        prev = prev_ref[b]
        q = jnp.concatenate([cur[:, (g * GQA + hh) * HEAD_DIM:(g * GQA + hh + 1) * HEAD_DIM]
                             for hh in range(GQA)], axis=0) * (HEAD_DIM ** -0.5)
        kcol = slice(ATT_WIDTH + g * HEAD_DIM, ATT_WIDTH + (g + 1) * HEAD_DIM)
        k2 = jnp.concatenate([prev[:, kcol], cur[:, kcol]], axis=0)
        scores.append(_mm_nt(q, k2) + bias_scr[first, g])
    probs_p, dens = [], []
    for (b, g), s in zip(probs, scores):
        sink = sink_scr[g]
        m = jnp.maximum(jnp.max(s, axis=-1, keepdims=True), sink)
        p = jnp.exp(s - m)
        dens.append(jnp.sum(p, axis=-1, keepdims=True) + jnp.exp(sink - m))
        probs_p.append(p)
    outs = {}
    for (b, g), p, den in zip(probs, probs_p, dens):
        vcol = slice(ATT_WIDTH + KV_WIDTH + g * HEAD_DIM, ATT_WIDTH + KV_WIDTH + (g + 1) * HEAD_DIM)
        v2 = jnp.concatenate([prev_ref[b][:, vcol], cur_ref[b][:, vcol]], axis=0)
        outs[b, g] = _mm(p, v2) / den
    for b in range(nseq):
        o_ref[b] = jnp.concatenate([outs[b, g][hh * ATT_BLOCK:(hh + 1) * ATT_BLOCK, :]
                                    for g in range(ATT_KV_HEADS) for hh in range(GQA)],
                                   axis=1).astype(o_ref.dtype)


def _attn_prompt(att, bucket, rel_bias, sink, batch, seq):
    nb = seq // ATT_BLOCK
    smem = pl.BlockSpec(memory_space=pltpu.SMEM)
    att3 = att.reshape(batch, seq, ATT_COLS)
    out = pl.pallas_call(
        _attn_prompt_kernel,
        grid=(nb,),
        in_specs=[
            pl.BlockSpec((batch, ATT_BLOCK, ATT_COLS), lambda i: (0, i, 0)),
            pl.BlockSpec((batch, ATT_BLOCK, ATT_COLS), lambda i: (0, jnp.maximum(i - 1, 0), 0)),
            pl.BlockSpec(bucket.shape, lambda i: (0, 0)),
            smem, smem,
        ],
        out_specs=pl.BlockSpec((batch, ATT_BLOCK, ATT_WIDTH), lambda i: (0, i, 0)),
        out_shape=jax.ShapeDtypeStruct((batch, seq, ATT_WIDTH), BF16),
        scratch_shapes=[pltpu.VMEM((2, ATT_KV_HEADS, GROUP_ROWS, 2 * ATT_BLOCK), F32),
                        pltpu.VMEM((ATT_KV_HEADS, GROUP_ROWS, 1), F32)],
        compiler_params=_params("arbitrary"),
        name="attn_prompt",
    )(att3, att3, bucket, rel_bias, sink)
    return out.reshape(batch * seq, ATT_WIDTH)


ATT_S_BB = 8


def _attn_sample_kernel(att_ref, ck_ref, cv_ref, bucket_ref, rb_ref, sink_ref, o_ref, ks_ref, vs_ref,
                        bias_scr, col_scr):
    hrow = lax.broadcasted_iota(jnp.int32, (ATT_HEADS, LANES), 0)
    lane = lax.broadcasted_iota(jnp.int32, (ATT_HEADS, LANES), 1)

    last = (lax.broadcasted_iota(jnp.int32, (3, WINDOW), 1) == WINDOW - 1).astype(BF16)
    is_last = lax.broadcasted_iota(jnp.int32, (KV_WIDTH, WINDOW), 1) == WINDOW - 1

    def shifted(cache_t, new_row):
        pieces = jnp.concatenate([p.astype(F32) for p in _split3(new_row)], axis=0).astype(BF16)
        col = lax.dot_general(pieces, last, (((0,), (0,)), ((), ())), preferred_element_type=F32)
        out = jnp.where(is_last, col, pltpu.roll(cache_t, WINDOW - 1, axis=1))
        return out.reshape(ATT_KV_HEADS, HEAD_DIM, WINDOW)

    for b in range(ATT_S_BB):
        row = att_ref[b:b + 1, :]
        ks_ref[b] = shifted(ck_ref[b].reshape(KV_WIDTH, WINDOW), row[:, ATT_WIDTH:ATT_WIDTH + KV_WIDTH])
        vs_ref[b] = shifted(cv_ref[b].reshape(KV_WIDTH, WINDOW), row[:, ATT_WIDTH + KV_WIDTH:])

    @pl.when(pl.program_id(0) == 0)
    def _():
        bucket = jnp.broadcast_to(bucket_ref[...], (ATT_HEADS, LANES))
        bias = jnp.zeros((ATT_HEADS, LANES), F32)
        cols = jnp.zeros((ATT_HEADS, LANES), F32)
        for h in range(ATT_HEADS):
            bias = jnp.where(hrow == h, _bias_lookup(bucket, rb_ref, h), bias)
            cols = jnp.where(jnp.logical_and(hrow == h, lane == 0), sink_ref[h], cols)
            cols = jnp.where(jnp.logical_and(hrow == h, lane == 1), rb_ref[0, h], cols)
        bias_scr[...] = jnp.where(lane >= 1, bias, NEG_INF)
        col_scr[...] = cols

    bias_c = bias_scr[...]
    sink = col_scr[:, 0:1]
    bias_n = col_scr[:, 1:2]
    same_group = (hrow // GQA) == (lane // HEAD_DIM)
    low_group = lax.broadcasted_iota(jnp.int32, (ATT_HEADS, HEAD_DIM), 0) < GQA
    rnd = lambda a: a.astype(BF16).astype(F32)
    seqs = range(ATT_S_BB)
    rows = [att_ref[b:b + 1, :] for b in seqs]
    q_bds = []
    for row in rows:
        q = row[:, :ATT_WIDTH] * (HEAD_DIM ** -0.5)
        qh = jnp.concatenate([q[:, h * HEAD_DIM:(h + 1) * HEAD_DIM] for h in range(ATT_HEADS)], axis=0)
        q_bds.append(jnp.where(same_group, jnp.concatenate([qh, qh], axis=1), 0.0))
    kv_t = lambda ref, b: ref[b].reshape(KV_WIDTH, WINDOW)
    s_cs = [_mm(q_bd, kv_t(ck_ref, b)) + bias_c for b, q_bd in zip(seqs, q_bds)]
    prs, pns = [], []
    for row, q_bd, s_c in zip(rows, q_bds, s_cs):
        kn = row[:, ATT_WIDTH:ATT_WIDTH + KV_WIDTH]
        s_n = jnp.sum(rnd(q_bd) * rnd(kn), axis=-1, keepdims=True) + bias_n
        m = jnp.maximum(jnp.maximum(jnp.max(s_c, axis=-1, keepdims=True), s_n), sink)
        p_c = jnp.exp(s_c - m)
        p_n = jnp.exp(s_n - m)
        den = jnp.sum(p_c, axis=-1, keepdims=True) + p_n + jnp.exp(sink - m)
        prs.append(p_c / den)
        pns.append(p_n / den)
    pvs = [_mm_nt(pr, kv_t(cv_ref, b)) for b, pr in zip(seqs, prs)]
    for b, row, pv, pn in zip(seqs, rows, pvs, pns):
        vn = row[:, ATT_WIDTH + KV_WIDTH:]
        o_full = pv + rnd(pn) * rnd(vn)
        o_sel = jnp.where(low_group, o_full[:, :HEAD_DIM], o_full[:, HEAD_DIM:])
        o_ref[b:b + 1, :] = jnp.concatenate([o_sel[h:h + 1, :] for h in range(ATT_HEADS)], axis=1)


def _attn_sample(att, ck, cv, bucket, rel_bias, sink):
    nseq = att.shape[0]
    smem = pl.BlockSpec(memory_space=pltpu.SMEM)
    cache = pl.BlockSpec((ATT_S_BB, ATT_KV_HEADS, HEAD_DIM, WINDOW), lambda i: (i, 0, 0, 0))
    return pl.pallas_call(
        _attn_sample_kernel,
        grid=(nseq // ATT_S_BB,),
        in_specs=[pl.BlockSpec((ATT_S_BB, ATT_COLS), lambda i: (i, 0)), cache, cache,
                  pl.BlockSpec(bucket.shape, lambda i: (0, 0)), smem, smem],
        out_specs=[pl.BlockSpec((ATT_S_BB, ATT_WIDTH), lambda i: (i, 0)), cache, cache],
        out_shape=[jax.ShapeDtypeStruct((nseq, ATT_WIDTH), F32),
                   jax.ShapeDtypeStruct(ck.shape, F32), jax.ShapeDtypeStruct(cv.shape, F32)],
        scratch_shapes=[pltpu.VMEM((ATT_HEADS, LANES), F32), pltpu.VMEM((ATT_HEADS, LANES), F32)],
        compiler_params=_params("arbitrary"),
        name="attn_sample",
    )(att, ck, cv, bucket, rel_bias, sink)


GDN_TB = 128
GDN_NC = GDN_TB // DN_CHUNK


def _gdn_gates(ba, alog, dtb):
    beta = _sigmoid(ba)
    g = -jnp.exp(alog) * _softplus(ba + dtb)
    return beta, g


def _pair_diag(x, lo):
    xb = x.astype(BF16)
    zero = jnp.zeros_like(xb)
    return jnp.concatenate([jnp.where(lo, xb, zero), jnp.where(lo, zero, xb)], axis=0)


def _gdn_prompt_kernel(qkv_ref, dz_ref, ba_ref, alog_ref, dtb_ref, dnx_ref,
                       hsum_ref, expb_ref, expg_ref, ltri_ref,
                       o_ref, s_out_ref, s_scr):
    i = pl.program_id(0)
    nb = qkv_ref.shape[0]

    @pl.when(i == 0)
    def _():
        s_scr[...] = jnp.zeros(s_scr.shape, F32)

    hsum = hsum_ref[...]
    ri = lax.broadcasted_iota(jnp.int32, (DN_CHUNK, PAIR), 0)
    ci = lax.broadcasted_iota(jnp.int32, (DN_CHUNK, PAIR), 1)
    lo = ci < DN_DK
    cj = jnp.where(lo, ci, ci - DN_DK)
    causal = ri >= cj
    strict = ri > cj
    eye = (ri == cj).astype(F32)

    def sel2(x, m):
        hi = x.astype(BF16)
        lw = (x - hi.astype(F32)).astype(BF16)
        return (jnp.dot(hi, m, preferred_element_type=F32) + jnp.dot(lw, m, preferred_element_type=F32))

    pre = []
    for b in range(nb):
        q = qkv_ref[b, :, :DN_WIDTH]
        k = qkv_ref[b, :, DN_WIDTH:2 * DN_WIDTH]
        v = qkv_ref[b, :, 2 * DN_WIDTH:]
        beta_c, g_c = _gdn_gates(ba_ref[b], alog_ref[...], dtb_ref[...])
        beta = sel2(beta_c, expb_ref[...])
        gam_c = _mm_sel_lhs(ltri_ref[...], g_c)
        gam = _mm_sel_rhs(gam_c, expg_ref[...])
        gam_t = gam_c.T
        kb = k * beta
        egam = jnp.exp(gam)
        pre.append(dict(q=q, k=k, kb=kb, vb=v * beta, qg=q * egam, wr=kb * egam, gam=gam, gam_t=gam_t))

    probs = [(b, p) for b in range(nb) for p in range(N_PAIRS)]
    pick = lambda m: jnp.where(lo, m[:DN_DK], m[DN_DK:])
    o_rows = [[] for _ in range(nb)]
    for c in range(GDN_NC):
        r0, r1 = c * DN_CHUNK, (c + 1) * DN_CHUNK
        sl = lambda name, b, p: pre[b][name][r0:r1, p * PAIR:(p + 1) * PAIR]
        raws = []
        for b, p in probs:
            k_p = sl("k", b, p)
            k_rows = jnp.concatenate([jnp.where(lo, k_p, 0.0), jnp.where(lo, 0.0, k_p)], axis=0)
            raws.append(_mm_nt(jnp.concatenate([sl("kb", b, p), sl("q", b, p)], axis=0), k_rows))
        pws, ts, qks = [], [], []
        for (b, p), raw in zip(probs, raws):
            gcol = sl("gam", b, p)
            h0 = DN_HEADS + 2 * p
            gam_t = pre[b]["gam_t"]
            grow = jnp.concatenate([gam_t[h0:h0 + 1, r0:r1], gam_t[h0 + 1:h0 + 2, r0:r1]], axis=1)
            decay = jnp.exp(jnp.where(causal, gcol - grow, NEG_INF))
            a = jnp.where(strict, raw[:DN_CHUNK] * decay, 0.0)
            qks.append(jnp.where(causal, raw[DN_CHUNK:] * decay, 0.0))
            pws.append(-a)
            ts.append(eye - a)
        pws = [_mm(pw, _pair_diag(pw, lo)) for pw in pws]
        for _ in range(4):
            rs = [_mm(jnp.concatenate([pw, t], axis=0), _pair_diag(pw, lo)) for pw, t in zip(pws, ts)]
            pws = [r[:DN_CHUNK] for r in rs]
            ts = [t + r[DN_CHUNK:] for t, r in zip(ts, rs)]
        rs = [_mm(t, _pair_diag(pw, lo)) for pw, t in zip(pws, ts)]
        ts = [t + r for t, r in zip(ts, rs)]
        sols = [_mm(t, jnp.concatenate([_pair_diag(sl("vb", b, p), lo), _pair_diag(sl("wr", b, p), lo)],
                                       axis=1)) for (b, p), t in zip(probs, ts)]
        qkuws = [_mm(qk, jnp.concatenate([_pair_diag(s[:, :PAIR], lo), _pair_diag(s[:, PAIR:], lo)], axis=1))
                 for qk, s in zip(qks, sols)]
        crosses, gls = [], []
        for (b, p), s in zip(probs, sols):
            gam_last = pre[b]["gam"][r1 - 1:r1, p * PAIR:(p + 1) * PAIR]
            kd = sl("k", b, p) * jnp.exp(gam_last - sl("gam", b, p))
            crosses.append(_mm_tn(kd, s))
            gls.append(jnp.exp(gam_last))
        lhs = [jnp.concatenate([pick(cr[:, PAIR:]), sl("qg", b, p) - qkuw[:, PAIR:]], axis=0)
               for (b, p), cr, qkuw in zip(probs, crosses, qkuws)]
        s_olds = [s_scr[b, p] for b, p in probs]
        rs = [_mm(l, _pair_diag(s_old, lo)) for l, s_old in zip(lhs, s_olds)]
        o_pairs = [[] for _ in range(nb)]
        for (b, p), r, s_old, gl, cr, qkuw in zip(probs, rs, s_olds, gls, crosses, qkuws):
            s_scr[b, p] = gl * s_old - r[:DN_DK] + pick(cr[:, :PAIR])
            o_pairs[b].append(r[DN_DK:] + qkuw[:, :PAIR])
        for b in range(nb):
            o_rows[b].append(jnp.concatenate(o_pairs[b], axis=1))

    o_all = jnp.concatenate([jnp.concatenate(rows, axis=0) for rows in o_rows], axis=0)
    inv_rms = lax.rsqrt(_head_sums(o_all * o_all, hsum) * (1.0 / DN_DV) + EPS)
    for b in range(nb):
        rows = slice(b * GDN_TB, (b + 1) * GDN_TB)
        o_ref[b] = (o_all[rows] * inv_rms[rows] * dnx_ref[...] * _silu(dz_ref[b])).astype(o_ref.dtype)

    @pl.when(i == pl.num_programs(0) - 1)
    def _():
        for b in range(nb):
            for p in range(N_PAIRS):
                s_p = s_scr[b, p]
                s_out_ref[b, 2 * p] = s_p[:, :DN_DV]
                s_out_ref[b, 2 * p + 1] = s_p[:, DN_DV:]


def _gdn_consts():
    lane = np.arange(DN_WIDTH)
    pl_lane = np.arange(PAIR)
    hsum = (pl_lane[:, None] // DN_DV == pl_lane[None, :] // DN_DV)
    src = np.arange(LANES)
    expb = (src[:, None] == lane[None, :] // DN_DV)
    expg = (src[:, None] == DN_HEADS + lane[None, :] // DN_DV)
    tok = np.arange(GDN_TB)
    ltri = np.logical_and(tok[:, None] >= tok[None, :],
                          tok[:, None] // DN_CHUNK == tok[None, :] // DN_CHUNK)
    as_bf16 = lambda m: jnp.asarray(m.astype(np.float32), dtype=BF16)
    return as_bf16(hsum), as_bf16(expb), as_bf16(expg), as_bf16(ltri)


def _gdn_prompt(xc, dz, ba, alog, dtb, dnx, batch, seq):
    nt = seq // GDN_TB
    hsum, expb, expg, ltri = _gdn_consts()
    row = lambda n: pl.BlockSpec((batch, GDN_TB, n), lambda i: (0, i, 0))
    full = lambda a: pl.BlockSpec(a.shape, lambda i: (0,) * a.ndim)
    consts = (alog, dtb, dnx, hsum, expb, expg, ltri)
    as3d = lambda a: a.reshape(batch, seq, a.shape[-1])
    o, s = pl.pallas_call(
        _gdn_prompt_kernel,
        grid=(nt,),
        in_specs=[row(CONV_CH), row(DN_WIDTH), row(LANES)] + [full(a) for a in consts],
        out_specs=[row(DN_WIDTH),
                   pl.BlockSpec((batch, DN_HEADS, DN_DK, DN_DV), lambda i: (0, 0, 0, 0))],
        out_shape=[jax.ShapeDtypeStruct((batch, seq, DN_WIDTH), BF16),
                   jax.ShapeDtypeStruct((batch, DN_HEADS, DN_DK, DN_DV), F32)],
        scratch_shapes=[pltpu.VMEM((batch, N_PAIRS, DN_DK, PAIR), F32)],
        compiler_params=_params("arbitrary"),
        name="gdn_prompt",
    )(as3d(xc), as3d(dz), as3d(ba), *consts)
    return o.reshape(batch * seq, DN_WIDTH), s


GDN_S_BB = 8


def _gdn_sample_kernel(xc_ref, dz_ref, ba_ref, sc_ref, s_ref, cw_ref, alog_ref, dtb_ref, dn_ref,
                       hsum_ref, eye_ref, hsel_ref, hrep3_ref, o_ref, s_out_ref):
    xc = xc_ref[...]
    y = sc_ref[0] * cw_ref[0:1, :]
    y = y + sc_ref[1] * cw_ref[1:2, :]
    y = y + sc_ref[2] * cw_ref[2:3, :]
    y = _silu(y + xc * cw_ref[3:4, :])
    hsum = hsum_ref[...]
    q = y[:, :DN_WIDTH]
    k = y[:, DN_WIDTH:2 * DN_WIDTH]
    v = y[:, 2 * DN_WIDTH:]
    q = q * lax.rsqrt(_mm_sel_rhs(q * q, hsum) + EPS) * (DN_DK ** -0.5)
    k = k * lax.rsqrt(_mm_sel_rhs(k * k, hsum) + EPS)
    beta_c, g_c = _gdn_gates(ba_ref[...], alog_ref[...], dtb_ref[...])
    eg_c = jnp.exp(g_c)
    eye = eye_ref[...]
    tr = lambda a: lax.dot_general(a, eye, (((0,), (0,)), ((), ())), precision=lax.Precision.HIGHEST,
                                   preferred_element_type=F32)
    gates_t = tr(jnp.concatenate([beta_c, eg_c], axis=1))
    beta_t = gates_t[:LANES]
    eg_t = gates_t[LANES:]
    dz = dz_ref[...]
    dn = dn_ref[...]
    split = lambda r: jnp.concatenate([r[:, h * DN_DV:(h + 1) * DN_DV] for h in range(DN_HEADS)], axis=0)
    own_head = hsel_ref[...].astype(F32)
    hrep3 = hrep3_ref[...]
    seqs = range(GDN_S_BB)
    dot = lambda a, b: jnp.dot(a.astype(BF16), b.astype(BF16), preferred_element_type=F32)

    def pieces(x):
        p1 = x.astype(BF16).astype(F32)
        r1 = x - p1
        p2 = r1.astype(BF16).astype(F32)
        return p1, p2, (r1 - p2).astype(BF16).astype(F32)

    heads = DN_HEADS
    k_pieces, kqs = [], []
    for b in seqs:
        kq_bd = jnp.concatenate([own_head * k[b:b + 1, :], own_head * q[b:b + 1, :]], axis=0)
        a1, a2, a3 = pieces(kq_bd)
        s1, s2, s3 = pieces(s_ref[b])
        r1 = dot(jnp.concatenate([a1, a2, a3], axis=0), s1)
        r2 = dot(jnp.concatenate([a1, a2], axis=0), s2)
        r3 = dot(a1, s3)
        n = 2 * heads
        kqs.append(((r3 + r2[n:] + r1[2 * n:]) + (r2[:n] + r1[n:2 * n])) + r1[:n])
        k_pieces.append((a1[:heads], a2[:heads], a3[:heads]))
    egs = [eg_t[DN_HEADS:2 * DN_HEADS, b:b + 1] for b in seqs]
    qks = [jnp.sum(split(q[b:b + 1, :]) * split(k[b:b + 1, :]), axis=-1, keepdims=True) for b in seqs]
    v_news = [beta_t[0:DN_HEADS, b:b + 1] * (split(v[b:b + 1, :]) - eg * kq[:heads])
              for b, eg, kq in zip(seqs, egs, kqs)]
    os_ = [eg * kq[heads:] + qk * v_new for eg, kq, qk, v_new in zip(egs, kqs, qks, v_news)]
    inv_rms = [lax.rsqrt(jnp.mean(o * o, axis=-1, keepdims=True) + EPS) for o in os_]
    for b, o, r in zip(seqs, os_, inv_rms):
        o_ref[b] = o * r * dn * _silu(split(dz[b:b + 1, :]))
    outers, egrows = [], []
    for (k1, k2, k3), v_new, eg in zip(k_pieces, v_news, egs):
        v1, v2, v3 = pieces(v_new)
        lhs = jnp.concatenate([k1, k1, k2, k1, k2, k3], axis=0).astype(BF16)
        rhs = jnp.concatenate([v1, v2, v1, v3, v2, v1], axis=0).astype(BF16)
        outers.append(lax.dot_general(lhs, rhs, (((0,), (0,)), ((), ())), preferred_element_type=F32))
        egrows.append(dot(hrep3, jnp.concatenate(pieces(jnp.broadcast_to(eg, (DN_HEADS, DN_DV))), axis=0)))
    for b, outer, egrow in zip(seqs, outers, egrows):
        s_out_ref[b] = s_ref[b] * egrow + outer


def _gdn_sample(xc, dz, ba, sconv_t, state, conv_w, alog, dtb, dn):
    nseq = xc.shape[0]
    lane = np.arange(DN_WIDTH)
    hsum = jnp.asarray((lane[:, None] // DN_DV == lane[None, :] // DN_DV).astype(np.float32), dtype=BF16)
    eye = jnp.eye(GDN_S_BB, dtype=F32)
    hsel_np = (np.arange(DN_HEADS)[:, None] == lane[None, :] // DN_DK).astype(np.float32)
    hsel = jnp.asarray(hsel_np, dtype=BF16)
    hrep3 = jnp.asarray(np.tile(hsel_np.T, (1, 3)), dtype=BF16)
    row = lambda n: pl.BlockSpec((GDN_S_BB, n), lambda i: (i, 0))
    full = lambda a: pl.BlockSpec(a.shape, lambda i: (0,) * a.ndim)
    st = pl.BlockSpec((GDN_S_BB, DN_HEADS * DN_DK, DN_DV), lambda i: (i, 0, 0))
    consts = (conv_w, alog, dtb, dn, hsum, eye, hsel, hrep3)
    return pl.pallas_call(
        _gdn_sample_kernel,
        grid=(nseq // GDN_S_BB,),
        in_specs=[row(CONV_CH), row(DN_WIDTH), row(LANES),
                  pl.BlockSpec((CONV_WIDTH - 1, GDN_S_BB, CONV_CH), lambda i: (0, i, 0)), st]
                 + [full(a) for a in consts],
        out_specs=[pl.BlockSpec((GDN_S_BB, DN_HEADS, DN_DV), lambda i: (i, 0, 0)), st],
        out_shape=[jax.ShapeDtypeStruct((nseq, DN_HEADS, DN_DV), F32),
                   jax.ShapeDtypeStruct(state.shape, F32)],
        compiler_params=_params("parallel"),
        name="gdn_sample",
    )(xc, dz, ba, sconv_t, state, *consts)


def _attn_sample_lanes_kernel(att_ref, ck_ref, cv_ref, bucket_ref, rb_ref, sink_ref, o_ref, s_scr):
    g = pl.program_id(0)
    nseq = att_ref.shape[0]
    rnd = lambda a: a.astype(BF16).astype(F32)
    att = att_ref[...]
    q_all_t = (att[:, :ATT_WIDTH] * (HEAD_DIM ** -0.5)).T
    kv_new_t = att[:, ATT_WIDTH:].T
    qsel = [jnp.where(g == 0, q_all_t[hh * HEAD_DIM:(hh + 1) * HEAD_DIM],
                      q_all_t[(GQA + hh) * HEAD_DIM:(GQA + hh + 1) * HEAD_DIM]) for hh in range(GQA)]
    qr = [rnd(q) for q in qsel]
    kn = rnd(jnp.where(g == 0, kv_new_t[0:HEAD_DIM], kv_new_t[HEAD_DIM:2 * HEAD_DIM]))
    vn = rnd(jnp.where(g == 0, kv_new_t[2 * HEAD_DIM:3 * HEAD_DIM], kv_new_t[3 * HEAD_DIM:]))

    def score_row(j, carry):
        kj = rnd(ck_ref[j, 0])
        for hh in range(GQA):
            s_scr[hh, pl.ds(j, 1), :] = jnp.sum(qr[hh] * kj, axis=0, keepdims=True)
        return carry
    lax.fori_loop(0, WINDOW, score_row, 0, unroll=2)

    bucket = bucket_ref[...]
    jrow = lax.broadcasted_iota(jnp.int32, (WINDOW, nseq), 0)
    prn = []
    for hh in range(GQA):
        h = g * GQA + hh
        bias = jnp.where(jrow >= 1, _bias_lookup(bucket, rb_ref, h), NEG_INF)
        s = s_scr[hh] + bias
        s_n = jnp.sum(qr[hh] * kn, axis=0, keepdims=True) + rb_ref[0, h]
        sink = sink_ref[h]
        m = jnp.maximum(jnp.maximum(jnp.max(s, axis=0, keepdims=True), s_n), sink)
        p = jnp.exp(s - m)
        p_n = jnp.exp(s_n - m)
        den = jnp.sum(p, axis=0, keepdims=True) + p_n + jnp.exp(sink - m)
        s_scr[hh] = rnd(p / den)
        prn.append(rnd(p_n / den))

    def value_row(j, acc):
        vj = rnd(cv_ref[j, 0])
        return tuple(acc[hh] + s_scr[hh, pl.ds(j, 1), :] * vj for hh in range(GQA))
    zero = jnp.zeros((HEAD_DIM, nseq), F32)
    acc = lax.fori_loop(0, WINDOW, value_row, (zero,) * GQA, unroll=2)
    for hh in range(GQA):
        o_ref[hh * HEAD_DIM:(hh + 1) * HEAD_DIM, :] = acc[hh] + prn[hh] * vn


def _attn_sample_lanes(att, ck_t, cv_t, rel_bias, sink):
    nseq = att.shape[0]
    assert nseq == LANES
    bucket = jnp.asarray(np.broadcast_to(_t5_bucket_np(WINDOW - np.arange(WINDOW))[:, None], (WINDOW, nseq)))
    smem = pl.BlockSpec(memory_space=pltpu.SMEM)
    cache = pl.BlockSpec((WINDOW, 1, HEAD_DIM, nseq), lambda g: (0, g, 0, 0))
    full = lambda a: pl.BlockSpec(a.shape, lambda g: (0,) * a.ndim)
    return pl.pallas_call(
        _attn_sample_lanes_kernel,
        grid=(ATT_KV_HEADS,),
        in_specs=[full(att), cache, cache, full(bucket), smem, smem],
        out_specs=pl.BlockSpec((GQA * HEAD_DIM, nseq), lambda g: (g, 0)),
        out_shape=jax.ShapeDtypeStruct((ATT_WIDTH, nseq), F32),
        scratch_shapes=[pltpu.VMEM((GQA, WINDOW, nseq), F32)],
        compiler_params=_params("arbitrary"),
        name="attn_sample_lanes",
    )(att, ck_t, cv_t, bucket, rel_bias, sink)


def _gdn_sample_front_kernel(xc_ref, dz_ref, ba_ref, sc_ref, cw_ref, alog_ref, dtb_ref, hsum_ref,
                             q_ref, k_ref, v_ref, dz_t_ref, gates_ref):
    xc = xc_ref[...]
    y = sc_ref[0] * cw_ref[0:1, :]
    y = y + sc_ref[1] * cw_ref[1:2, :]
    y = y + sc_ref[2] * cw_ref[2:3, :]
    y = _silu(y + xc * cw_ref[3:4, :])
    hsum = hsum_ref[...]
    q = y[:, :DN_WIDTH]
    k = y[:, DN_WIDTH:2 * DN_WIDTH]
    q = q * lax.rsqrt(_mm_sel_rhs(q * q, hsum) + EPS) * (DN_DK ** -0.5)
    k = k * lax.rsqrt(_mm_sel_rhs(k * k, hsum) + EPS)
    beta_c, g_c = _gdn_gates(ba_ref[...], alog_ref[...], dtb_ref[...])
    q_ref[...] = q.T
    k_ref[...] = k.T
    v_ref[...] = y[:, 2 * DN_WIDTH:].T
    dz_t_ref[...] = dz_ref[...].T
    gates_ref[0:LANES, :] = beta_c.T
    gates_ref[LANES:, :] = jnp.exp(g_c).T


def _gdn_sample_step_kernel(q_ref, k_ref, v_ref, dz_ref, gates_ref, dn_ref, s_ref, o_ref, s_out_ref):
    h = pl.program_id(0)
    beta = gates_ref[pl.ds(h, 1), :]
    eg = gates_ref[pl.ds(LANES + DN_HEADS + h, 1), :]
    q, k, v = q_ref[...], k_ref[...], v_ref[...]
    w = (k * beta) * eg
    qg = q * eg
    ws = jnp.zeros(v.shape, F32)
    qs = jnp.zeros(v.shape, F32)
    for dk in range(DN_DK):
        s_dk = s_ref[0, dk]
        ws = ws + w[dk:dk + 1, :] * s_dk
        qs = qs + qg[dk:dk + 1, :] * s_dk
    v_new = v * beta - ws
    qk = jnp.sum(q * k, axis=0, keepdims=True)
    o = qs + qk * v_new
    for dk in range(DN_DK):
        s_out_ref[0, dk] = s_ref[0, dk] * eg + k[dk:dk + 1, :] * v_new
    o = o * lax.rsqrt(jnp.mean(o * o, axis=0, keepdims=True) + EPS) * dn_ref[...]
    o_ref[...] = o * _silu(dz_ref[...])


def _gdn_sample_lanes(xc, dz, ba, sconv_t, state_t, conv_w, alog, dtb, dn):
    nseq = xc.shape[0]
    assert nseq == LANES
    lane = np.arange(DN_WIDTH)
    hsum = jnp.asarray((lane[:, None] // DN_DV == lane[None, :] // DN_DV).astype(np.float32), dtype=BF16)
    full = lambda a: pl.BlockSpec(a.shape, lambda i: (0,) * a.ndim)
    cm = jax.ShapeDtypeStruct((DN_WIDTH, nseq), F32)
    front_in = (xc, dz, ba, sconv_t, conv_w, alog, dtb, hsum)
    q_t, k_t, v_t, dz_t, gates_t = pl.pallas_call(
        _gdn_sample_front_kernel,
        grid=(1,),
        in_specs=[full(a) for a in front_in],
        out_specs=[pl.BlockSpec((DN_WIDTH, nseq), lambda i: (0, 0))] * 4
                  + [pl.BlockSpec((2 * LANES, nseq), lambda i: (0, 0))],
        out_shape=[cm, cm, cm, cm, jax.ShapeDtypeStruct((2 * LANES, nseq), F32)],
        compiler_params=_params("arbitrary"),
        name="gdn_sample_front",
    )(*front_in)
    dn_b = jnp.broadcast_to(dn.reshape(DN_DV, 1), (DN_DV, nseq))
    head = pl.BlockSpec((DN_DK, nseq), lambda h: (h, 0))
    st = pl.BlockSpec((1, DN_DK, DN_DV, nseq), lambda h: (h, 0, 0, 0))
    return pl.pallas_call(
        _gdn_sample_step_kernel,
        grid=(DN_HEADS,),
        in_specs=[head, head, head, head, full(gates_t), full(dn_b), st],
        out_specs=[head, st],
        out_shape=[cm, jax.ShapeDtypeStruct(state_t.shape, F32)],
        compiler_params=_params("parallel"),
        name="gdn_sample_step",
    )(q_t, k_t, v_t, dz_t, gates_t, dn_b, state_t)


def _route(xn, wr):
    logits = jnp.dot(xn, wr, preferred_element_type=F32)
    lane = lax.broadcasted_iota(jnp.int32, logits.shape, 1).astype(F32)
    first_at = lambda hit: jnp.min(jnp.where(hit, lane, float(LANES)), axis=-1, keepdims=True)
    glog = jnp.where(lane < N_GROUPS, logits, NEG_INF)
    gmax = jnp.max(glog, axis=-1, keepdims=True)
    gsel = first_at(glog == gmax)
    pgsel = 1.0 / jnp.sum(jnp.exp(glog - gmax), axis=-1, keepdims=True)
    lo = ROUTER_OFF + gsel * EXPERTS_PER_GROUP
    in_group = jnp.logical_and(lane >= lo, lane < lo + EXPERTS_PER_GROUP)
    elog = jnp.where(in_group, logits, NEG_INF)
    m1 = jnp.max(elog, axis=-1, keepdims=True)
    i1 = first_at(elog == m1)
    z = jnp.sum(jnp.exp(elog - m1), axis=-1, keepdims=True)
    elog2 = jnp.where(lane == i1, NEG_INF, elog)
    m2 = jnp.max(elog2, axis=-1, keepdims=True)
    i2 = first_at(elog2 == m2)
    p1 = 1.0 / z
    p2 = jnp.exp(m2 - m1) / z
    tot = p1 + p2
    return lane, i1, i2, p1 / tot * pgsel, p2 / tot * pgsel


def _outproj(x_ref, oa_ref, od_ref, wo_ref):
    return x_ref[...] + _mm(oa_ref[...], wo_ref[:ATT_WIDTH, :]) + _mm(od_ref[...], wo_ref[ATT_WIDTH:, :])


def _outproj_router_kernel(x_ref, oa_ref, od_t_ref, wo_ref, g_ref, wr_ref, h_ref, xn_ref, gate_ref):
    h = (x_ref[...] + _mm(oa_ref[...], wo_ref[:ATT_WIDTH, :])
         + _mm(od_t_ref[...].T, wo_ref[ATT_WIDTH:, :]))
    h_ref[...] = h
    xn = _rmsnorm(h, g_ref[...]).astype(BF16)
    xn_ref[...] = xn
    lane, i1, i2, g1, g2 = _route(xn, wr_ref[...])
    gate_ref[...] = jnp.where(lane == i1, g1, 0.0) + jnp.where(lane == i2, g2, 0.0)


def _outproj_router(x, oa, od_t, wo, g, wr):
    t = x.shape[0]
    tm = t
    row = lambda n: pl.BlockSpec((tm, n), lambda i: (i, 0))
    full = lambda a: pl.BlockSpec(a.shape, lambda i: (0,) * a.ndim)
    return pl.pallas_call(
        _outproj_router_kernel,
        grid=(t // tm,),
        in_specs=[row(D_MODEL), row(ATT_WIDTH), full(od_t), full(wo), full(g), full(wr)],
        out_specs=[row(D_MODEL), row(D_MODEL), row(LANES)],
        out_shape=[jax.ShapeDtypeStruct((t, D_MODEL), F32), jax.ShapeDtypeStruct((t, D_MODEL), BF16),
                   jax.ShapeDtypeStruct((t, LANES), F32)],
        compiler_params=_params("parallel"),
        name="outproj_router",
    )(x, oa, od_t, wo, g, wr)


def _moe_kernel(xn_ref, gate_ref, wg_ref, wu_ref, wd_ref, o_ref):
    e = pl.program_id(1)
    xn = xn_ref[...]
    lane = lax.broadcasted_iota(jnp.int32, gate_ref.shape, 1)
    gate = jnp.sum(jnp.where(lane == e + ROUTER_OFF, gate_ref[...], 0.0), axis=-1, keepdims=True)
    hg = jnp.dot(xn, wg_ref[...].astype(BF16), preferred_element_type=F32)
    hu = jnp.dot(xn, wu_ref[...].astype(BF16), preferred_element_type=F32)
    hm = _silu(hg) * hu * gate
    y = jnp.dot(hm.astype(BF16), wd_ref[...].astype(BF16), preferred_element_type=F32)

    @pl.when(e == 0)
    def _():
        o_ref[...] = y

    @pl.when(e > 0)
    def _():
        o_ref[...] += y


def _moe(xn, gates, wg, wu, wd):
    t = xn.shape[0]
    tm = min(t, 1024)
    return pl.pallas_call(
        _moe_kernel,
        grid=(t // tm, N_EXPERTS),
        in_specs=[pl.BlockSpec((tm, D_MODEL), lambda i, e: (i, 0)),
                  pl.BlockSpec((tm, LANES), lambda i, e: (i, 0)),
                  pl.BlockSpec((None, D_MODEL, D_EXPERT), lambda i, e: (e, 0, 0)),
                  pl.BlockSpec((None, D_MODEL, D_EXPERT), lambda i, e: (e, 0, 0)),
                  pl.BlockSpec((None, D_EXPERT, D_MODEL), lambda i, e: (e, 0, 0))],
        out_specs=pl.BlockSpec((tm, D_MODEL), lambda i, e: (i, 0)),
        out_shape=jax.ShapeDtypeStruct((t, D_MODEL), F32),
        compiler_params=_params("parallel", "arbitrary"),
        name="moe",
    )(xn, gates, wg, wu, wd)


MOE_TM = 512
POS_TM = 1024
INFO_G1, INFO_G2, INFO_E1, INFO_E2 = 0, 1, 2, 3
DMA_UNROLL = 8


def _moe_tiles(t):
    return (2 * t) // MOE_TM + N_EXPERTS


HALF = D_MODEL // 2
U32 = jnp.uint32


def _pack_rows(x):
    bits = lambda v: lax.bitcast_convert_type(v.astype(BF16).astype(F32), U32)
    return bits(x[:, HALF:]) | (bits(x[:, :HALF]) >> 16)


def _unpack_rows(w):
    lo = lax.bitcast_convert_type(w << 16, F32)
    hi = lax.bitcast_convert_type(w & jnp.uint32(0xFFFF0000), F32)
    return lo, hi


def _route_kernel(x_ref, oa_ref, od_ref, wo_ref, g_ref, wr_ref, h_ref, xn_ref, info_ref, cnt_ref, run_scr):
    h = _outproj(x_ref, oa_ref, od_ref, wo_ref)
    h_ref[...] = h
    xn = _rmsnorm(h, g_ref[...])
    xn_ref[...] = _pack_rows(xn)
    lane, i1, i2, g1, g2 = _route(xn.astype(BF16), wr_ref[...])
    info = jnp.where(lane == INFO_G1, g1, 0.0) + jnp.where(lane == INFO_G2, g2, 0.0)
    info = info + jnp.where(lane == INFO_E1, i1, 0.0) + jnp.where(lane == INFO_E2, i2, 0.0)
    info_ref[...] = info

    @pl.when(pl.program_id(0) == 0)
    def _():
        run_scr[...] = jnp.zeros(run_scr.shape, F32)
    picked = jnp.logical_or(lane == i1, lane == i2).astype(F32)
    run_scr[...] += jnp.sum(picked, axis=0, keepdims=True)
    cnt_ref[...] = run_scr[...]


def _route_sparse(x, oa, od, wo, g, wr):
    t = x.shape[0]
    tm = ROW_TM
    row = lambda n: pl.BlockSpec((tm, n), lambda i: (i, 0))
    full = lambda a: pl.BlockSpec(a.shape, lambda i: (0,) * a.ndim)
    return pl.pallas_call(
        _route_kernel,
        grid=(t // tm,),
        in_specs=[row(D_MODEL), row(ATT_WIDTH), row(DN_WIDTH), full(wo), full(g), full(wr)],
        out_specs=[row(D_MODEL), row(HALF), row(LANES), pl.BlockSpec((1, LANES), lambda i: (0, 0))],
        out_shape=[jax.ShapeDtypeStruct((t, D_MODEL), F32), jax.ShapeDtypeStruct((t, HALF), U32),
                   jax.ShapeDtypeStruct((t, LANES), F32), jax.ShapeDtypeStruct((1, LANES), F32)],
        scratch_shapes=[pltpu.VMEM((1, LANES), F32)],
        compiler_params=_params("arbitrary"),
        name="route",
    )(x, oa, od, wo, g, wr)


def _positions_kernel(info_ref, cnt_ref, ltri_ref, utri_ref, pos_ref, run_scr, off_scr):
    info = info_ref[...]
    lane = lax.broadcasted_iota(jnp.int32, info.shape, 1).astype(F32)
    hit1 = lane == info[:, INFO_E1:INFO_E1 + 1]
    hit2 = lane == info[:, INFO_E2:INFO_E2 + 1]
    onehot = jnp.logical_or(hit1, hit2).astype(F32)

    @pl.when(pl.program_id(0) == 0)
    def _():
        ln = lax.broadcasted_iota(jnp.int32, cnt_ref.shape, 1)
        is_expert = jnp.logical_and(ln >= ROUTER_OFF, ln < ROUTER_OFF + N_EXPERTS)
        tiles = jnp.where(is_expert, jnp.maximum(jnp.floor((cnt_ref[...] + (MOE_TM - 1)) * (1.0 / MOE_TM)), 1.0), 0.0)
        off_scr[...] = MOE_TM * jnp.dot(tiles.astype(BF16), utri_ref[...], preferred_element_type=F32)
        run_scr[...] = jnp.zeros(run_scr.shape, F32)

    before = (jnp.dot(ltri_ref[...], onehot.astype(BF16), preferred_element_type=F32)
              + run_scr[...] + off_scr[...])
    pos1 = jnp.sum(jnp.where(hit1, before, 0.0), axis=-1, keepdims=True)
    pos2 = jnp.sum(jnp.where(hit2, before, 0.0), axis=-1, keepdims=True)
    pos_ref[...] = (jnp.where(lane == 0, pos1, 0.0) + jnp.where(lane == 1, pos2, 0.0)).astype(jnp.int32)
    run_scr[...] += jnp.sum(onehot, axis=0, keepdims=True)


def _positions(info, cnt):
    t = info.shape[0]
    tm = min(t, POS_TM)
    tok = np.arange(tm)
    ltri = jnp.asarray((tok[:, None] > tok[None, :]).astype(np.float32), dtype=BF16)
    ln = np.arange(LANES)
    utri = jnp.asarray((ln[:, None] < ln[None, :]).astype(np.float32), dtype=BF16)
    full = lambda a: pl.BlockSpec(a.shape, lambda i: (0,) * a.ndim)
    return pl.pallas_call(
        _positions_kernel,
        grid=(t // tm,),
        in_specs=[pl.BlockSpec((tm, LANES), lambda i: (i, 0)), full(cnt), full(ltri), full(utri)],
        out_specs=pl.BlockSpec((tm, LANES), lambda i: (i, 0)),
        out_shape=jax.ShapeDtypeStruct((t, LANES), jnp.int32),
        scratch_shapes=[pltpu.VMEM((1, LANES), F32), pltpu.VMEM((1, LANES), F32)],
        compiler_params=_params("arbitrary"),
        name="positions",
    )(info, cnt, ltri, utri)


def _row_copy(src_hbm, src_row, dst_hbm, dst_row, sem):
    return pltpu.make_async_copy(src_hbm.at[pl.ds(src_row, 1)], dst_hbm.at[pl.ds(dst_row, 1)], sem)


SCATTER_SLOTS = 3


def _scatter_kernel(pos1_ref, pos2_ref, last_ref, used_ref, nt_ref, xn_hbm, zero_hbm, xs_hbm,
                    buf, lsem, sem, zsem, *, n_tok):
    max_tiles = xs_hbm.shape[0] // MOE_TM

    def zero_tile(tile):
        return pltpu.make_async_copy(zero_hbm, xs_hbm.at[pl.ds(tile * MOE_TM, MOE_TM)], zsem)

    def for_unused(fn):
        def body(tile, carry):
            fn(tile)
            return carry
        lax.fori_loop(nt_ref[0], max_tiles, body, 0)

    for e in range(N_EXPERTS):
        @pl.when(used_ref[e] > 0)
        def _():
            zero_tile(last_ref[e]).start()
    for_unused(lambda tile: zero_tile(tile).start())
    for e in range(N_EXPERTS):
        @pl.when(used_ref[e] > 0)
        def _():
            zero_tile(last_ref[e]).wait()
    for_unused(lambda tile: zero_tile(tile).wait())

    tm = buf.shape[1]
    n = n_tok // tm

    def load(i):
        return pltpu.make_async_copy(xn_hbm.at[pl.ds(i * tm, tm)], buf.at[i % SCATTER_SLOTS],
                                     lsem.at[i % SCATTER_SLOTS])

    def wait_rows(slot):
        pltpu.make_async_copy(xs_hbm.at[pl.ds(0, 2 * tm)], xs_hbm.at[pl.ds(0, 2 * tm)], sem.at[slot]).wait()

    load(0).start()
    load(1).start()

    def step(i, carry):
        slot = i % SCATTER_SLOTS
        load(i).wait()

        def body(j, c2):
            tok = i * tm + j
            src = buf.at[slot, pl.ds(j, 1)]
            pltpu.make_async_copy(src, xs_hbm.at[pl.ds(pos1_ref[tok], 1)], sem.at[slot]).start()
            pltpu.make_async_copy(src, xs_hbm.at[pl.ds(pos2_ref[tok], 1)], sem.at[slot]).start()
            return c2
        lax.fori_loop(0, tm, body, 0, unroll=DMA_UNROLL)

        @pl.when(i >= 1)
        def _():
            wait_rows((i - 1) % SCATTER_SLOTS)

        @pl.when(i + 2 < n)
        def _():
            load(i + 2).start()
        return carry
    lax.fori_loop(0, n, step, 0)
    wait_rows((n - 1) % SCATTER_SLOTS)


def _scatter_rows(xn, pos1, pos2, last_tile, used, n_tiles, n_rows):
    t = xn.shape[0]
    zero = jnp.zeros((MOE_TM, D_MODEL), F32)
    any_spec = pl.BlockSpec(memory_space=pl.ANY)
    return pl.pallas_call(
        functools.partial(_scatter_kernel, n_tok=t),
        grid_spec=pltpu.PrefetchScalarGridSpec(
            num_scalar_prefetch=5, grid=(1,),
            in_specs=[any_spec, any_spec], out_specs=any_spec,
            scratch_shapes=[pltpu.VMEM((SCATTER_SLOTS, MOE_TM, D_MODEL), F32),
                            pltpu.SemaphoreType.DMA((SCATTER_SLOTS,)),
                            pltpu.SemaphoreType.DMA((SCATTER_SLOTS,)),
                            pltpu.SemaphoreType.DMA]),
        out_shape=jax.ShapeDtypeStruct((n_rows, D_MODEL), F32),
        compiler_params=_params("arbitrary"),
        name="scatter_rows",
    )(pos1, pos2, last_tile, used, n_tiles, xn, zero)


def _experts_kernel(te_ref, tv_ref, nt_ref, xs_ref, wg_hbm, wu_hbm, wd_hbm, xn_new_ref, gate_new_ref,
                    ys_ref, moe_new_ref, wg_s, wu_s, wd_s, wg_f, wu_f, wd_f, wsem):
    i = pl.program_id(0)
    used = i < nt_ref[0]
    expert = te_ref[i]

    def fetch(e):
        slot = e % 2
        return [pltpu.make_async_copy(src.at[e], dst.at[slot], wsem.at[slot, j])
                for j, (src, dst) in enumerate(((wg_hbm, wg_f), (wu_hbm, wu_f), (wd_hbm, wd_f)))]

    @pl.when(jnp.logical_or(i == 0, expert != te_ref[jnp.maximum(i - 1, 0)]))
    def _():
        @pl.when(i == 0)
        def _():
            for c in fetch(expert):
                c.start()
        for c in fetch(expert):
            c.wait()

        @pl.when(expert + 1 < N_EXPERTS)
        def _():
            for c in fetch(expert + 1):
                c.start()
        slot = expert % 2
        wg_s[...] = wg_f[slot].astype(BF16)
        wu_s[...] = wu_f[slot].astype(BF16)
        wd_s[...] = wd_f[slot].astype(BF16)
        xn = xn_new_ref[...]
        lane = lax.broadcasted_iota(jnp.int32, gate_new_ref.shape, 1)
        gate = jnp.sum(jnp.where(lane == expert + ROUTER_OFF, gate_new_ref[...], 0.0), axis=-1, keepdims=True)
        hg = jnp.dot(xn, wg_s[...], preferred_element_type=F32)
        hu = jnp.dot(xn, wu_s[...], preferred_element_type=F32)
        hm = _silu(hg) * hu * gate
        y = jnp.dot(hm.astype(BF16), wd_s[...], preferred_element_type=F32)

        @pl.when(i == 0)
        def _():
            moe_new_ref[...] = y

        @pl.when(i > 0)
        def _():
            moe_new_ref[...] += y

    @pl.when(used)
    def _():
        row = lax.broadcasted_iota(jnp.int32, xs_ref.shape, 0)
        x_lo, x_hi = _unpack_rows(jnp.where(row < tv_ref[i], xs_ref[...], jnp.uint32(0)))
        x_lo = x_lo.astype(BF16)
        x_hi = x_hi.astype(BF16)
        up = lambda w_s: (jnp.dot(x_lo, w_s[:HALF, :], preferred_element_type=F32)
                          + jnp.dot(x_hi, w_s[HALF:, :], preferred_element_type=F32))
        hm = (_silu(up(wg_s)) * up(wu_s)).astype(BF16)
        ys_ref[...] = _pack_rows(jnp.dot(hm, wd_s[...], preferred_element_type=F32))

    @pl.when(jnp.logical_not(used))
    def _():
        ys_ref[...] = jnp.zeros(ys_ref.shape, U32)


def _experts(xs, tile_expert, tile_valid, n_tiles, wg, wu, wd, xn_new, gate_new):
    max_tiles = xs.shape[0] // MOE_TM
    rows = pl.BlockSpec((MOE_TM, HALF), lambda i, te, tv, nt: (i, 0))
    hbm = pl.BlockSpec(memory_space=pl.ANY)
    full = lambda a: pl.BlockSpec(a.shape, lambda i, te, tv, nt: (0,) * a.ndim)
    return pl.pallas_call(
        _experts_kernel,
        grid_spec=pltpu.PrefetchScalarGridSpec(
            num_scalar_prefetch=3, grid=(max_tiles,),
            in_specs=[rows, hbm, hbm, hbm, full(xn_new), full(gate_new)],
            out_specs=[rows, pl.BlockSpec(xn_new.shape, lambda i, te, tv, nt: (0, 0))],
            scratch_shapes=[pltpu.VMEM((D_MODEL, D_EXPERT), BF16), pltpu.VMEM((D_MODEL, D_EXPERT), BF16),
                            pltpu.VMEM((D_EXPERT, D_MODEL), BF16),
                            pltpu.VMEM((2, D_MODEL, D_EXPERT), F32), pltpu.VMEM((2, D_MODEL, D_EXPERT), F32),
                            pltpu.VMEM((2, D_EXPERT, D_MODEL), F32), pltpu.SemaphoreType.DMA((2, 3))]),
        out_shape=[jax.ShapeDtypeStruct(xs.shape, U32), jax.ShapeDtypeStruct(xn_new.shape, F32)],
        compiler_params=_params("arbitrary"),
        name="experts",
    )(tile_expert, tile_valid, n_tiles, xs, wg, wu, wd, xn_new, gate_new)


def _ple_gather_kernel(pos1_ref, pos2_ref, h_ref, info_ref, p_ref, wpp_ref, wpg_ref, gp_ref, gf_ref,
                       ys_hbm, y_ref, ybuf, sem):
    i = pl.program_id(0)
    n = pl.num_programs(0)
    tm = h_ref.shape[0]

    def issue(tile, slot):
        def body(j, carry):
            tok = tile * tm + j
            pltpu.make_async_copy(ys_hbm.at[pl.ds(pos1_ref[tok], 1)], ybuf.at[slot, 0, pl.ds(j, 1)],
                                  sem.at[slot]).start()
            pltpu.make_async_copy(ys_hbm.at[pl.ds(pos2_ref[tok], 1)], ybuf.at[slot, 1, pl.ds(j, 1)],
                                  sem.at[slot]).start()
            return carry
        lax.fori_loop(0, tm, body, 0, unroll=DMA_UNROLL)

    @pl.when(i == 0)
    def _():
        issue(0, 0)

    @pl.when(i + 1 < n)
    def _():
        issue(i + 1, (i + 1) % 2)

    slot = i % 2
    pltpu.make_async_copy(ybuf.at[slot], ybuf.at[slot], sem.at[slot]).wait()
    info = info_ref[...]
    moe = info[:, INFO_G1:INFO_G1 + 1] * ybuf[slot, 0] + info[:, INFO_G2:INFO_G2 + 1] * ybuf[slot, 1]
    h = h_ref[...] + moe
    hn = _rmsnorm(h, gp_ref[...])
    h = h + _mm(p_ref[...], wpp_ref[...]) * _sigmoid(_mm(hn, wpg_ref[...]))
    y_ref[...] = _rmsnorm(h, gf_ref[...])


def _ple_gather(h, info, p, ys, pos1, pos2, wpp, wpg, gp, gf):
    t = h.shape[0]
    tm = 256
    row = lambda n: pl.BlockSpec((tm, n), lambda i, p1, p2: (i, 0))
    full = lambda a: pl.BlockSpec(a.shape, lambda i, p1, p2: (0,) * a.ndim)
    return pl.pallas_call(
        _ple_gather_kernel,
        grid_spec=pltpu.PrefetchScalarGridSpec(
            num_scalar_prefetch=2, grid=(t // tm,),
            in_specs=[row(D_MODEL), row(LANES), row(PLE_DIM), full(wpp), full(wpg), full(gp), full(gf),
                      pl.BlockSpec(memory_space=pl.ANY)],
            out_specs=row(D_MODEL),
            scratch_shapes=[pltpu.VMEM((2, 2, tm, D_MODEL), F32), pltpu.SemaphoreType.DMA((2,))]),
        out_shape=jax.ShapeDtypeStruct((t, D_MODEL), F32),
        compiler_params=_params("arbitrary"),
        name="ple_gather",
    )(pos1, pos2, h, info, p, wpp, wpg, gp, gf, ys)


SC_IDX = 128
SC_ROWS = 64
SC_WORKERS = 32


def _sc_mesh():
    return plsc.VectorSubcoreMesh(core_axis_name="c", subcore_axis_name="s")


def _sc_windows(t, fn):
    per_worker = t // SC_WORKERS
    worker = lax.axis_index(("c", "s"))

    @pl.loop(0, per_worker // SC_IDX)
    def _(w):
        fn(worker * per_worker + w * SC_IDX)


def _sc_scatter_rows(xn, pos1, pos2, n_rows):
    t, d = xn.shape
    assert t % (SC_WORKERS * SC_IDX) == 0
    idx_t = pltpu.VMEM((1, SC_IDX), jnp.int32)

    @pl.kernel(out_type=jax.ShapeDtypeStruct((n_rows, d), xn.dtype), mesh=_sc_mesh(),
               scratch_types=[idx_t, idx_t, pltpu.VMEM((SC_ROWS, d), xn.dtype)])
    def scatter(x_hbm, p1_hbm, p2_hbm, o_hbm, i1_v, i2_v, buf):
        def window(base):
            pltpu.sync_copy(p1_hbm.at[:, pl.ds(base, SC_IDX)], i1_v)
            pltpu.sync_copy(p2_hbm.at[:, pl.ds(base, SC_IDX)], i2_v)
            for k in range(SC_IDX // SC_ROWS):
                pltpu.sync_copy(x_hbm.at[pl.ds(base + k * SC_ROWS, SC_ROWS)], buf)
                pltpu.sync_copy(buf, o_hbm.at[i1_v.at[0, pl.ds(k * SC_ROWS, SC_ROWS)]])
                pltpu.sync_copy(buf, o_hbm.at[i2_v.at[0, pl.ds(k * SC_ROWS, SC_ROWS)]])
        _sc_windows(t, window)

    return scatter(xn, pos1.reshape(1, t), pos2.reshape(1, t))


def _sc_gather_rows(ys, pos1, pos2):
    t = pos1.shape[0]
    d = ys.shape[1]
    assert t % (SC_WORKERS * SC_IDX) == 0
    idx_t = pltpu.VMEM((1, SC_IDX), jnp.int32)
    out = jax.ShapeDtypeStruct((t, d), ys.dtype)

    buf_t = pltpu.VMEM((SC_ROWS, d), ys.dtype)

    @pl.kernel(out_type=(out, out), mesh=_sc_mesh(),
               scratch_types=[idx_t, idx_t, buf_t, buf_t, pltpu.SemaphoreType.DMA((2,)),
                              pltpu.SemaphoreType.DMA((2,))])
    def gather(y_hbm, p1_hbm, p2_hbm, o1_hbm, o2_hbm, i1_v, i2_v, buf_a, buf_b, gsem, wsem):
        bufs = (buf_a, buf_b)

        def window(base):
            pltpu.sync_copy(p1_hbm.at[:, pl.ds(base, SC_IDX)], i1_v)
            pltpu.sync_copy(p2_hbm.at[:, pl.ds(base, SC_IDX)], i2_v)
            items = [(idx_v, o_hbm, k) for k in range(SC_IDX // SC_ROWS)
                     for idx_v, o_hbm in ((i1_v, o1_hbm), (i2_v, o2_hbm))]

            def read(n):
                idx_v, _, k = items[n]
                return pltpu.make_async_copy(y_hbm.at[idx_v.at[0, pl.ds(k * SC_ROWS, SC_ROWS)]],
                                             bufs[n % 2], gsem.at[n % 2])

            def write(n):
                _, o_hbm, k = items[n]
                return pltpu.make_async_copy(bufs[n % 2], o_hbm.at[pl.ds(base + k * SC_ROWS, SC_ROWS)],
                                             wsem.at[n % 2])

            read(0).start()
            for n in range(len(items)):
                read(n).wait()
                if n >= 1:
                    write(n - 1).wait()
                if n + 1 < len(items):
                    read(n + 1).start()
                write(n).start()
            write(len(items) - 1).wait()
        _sc_windows(t, window)

    return gather(ys, pos1.reshape(1, t), pos2.reshape(1, t))


def _ple_sparse_kernel(h_ref, info_ref, y1_ref, y2_ref, p_ref, wpp_ref, wpg_ref, gp_ref, gf_ref, y_ref):
    info = info_ref[...]
    g1 = info[:, INFO_G1:INFO_G1 + 1]
    g2 = info[:, INFO_G2:INFO_G2 + 1]
    y1_lo, y1_hi = _unpack_rows(y1_ref[...])
    y2_lo, y2_hi = _unpack_rows(y2_ref[...])
    moe = jnp.concatenate([g1 * y1_lo + g2 * y2_lo, g1 * y1_hi + g2 * y2_hi], axis=1)
    h = h_ref[...] + moe
    hn = _rmsnorm(h, gp_ref[...])
    h = h + _mm(p_ref[...], wpp_ref[...]) * _sigmoid(_mm(hn, wpg_ref[...]))
    y_ref[...] = _rmsnorm(h, gf_ref[...])


def _ple_sparse(h, info, y1, y2, p, wpp, wpg, gp, gf):
    t = h.shape[0]
    tm = ROW_TM
    row = lambda n: pl.BlockSpec((tm, n), lambda i: (i, 0))
    full = lambda a: pl.BlockSpec(a.shape, lambda i: (0,) * a.ndim)
    return pl.pallas_call(
        _ple_sparse_kernel,
        grid=(t // tm,),
        in_specs=[row(D_MODEL), row(LANES), row(HALF), row(HALF), row(PLE_DIM),
                  full(wpp), full(wpg), full(gp), full(gf)],
        out_specs=row(D_MODEL),
        out_shape=jax.ShapeDtypeStruct((t, D_MODEL), F32),
        compiler_params=_params("parallel"),
        name="ple_sparse",
    )(h, info, y1, y2, p, wpp, wpg, gp, gf)


def _tile_tables(cnt, max_tiles):
    tiles_e = jnp.maximum((cnt + (MOE_TM - 1)) // MOE_TM, 1)
    ends = jnp.cumsum(tiles_e)
    n_tiles = ends[-1]
    tile = jnp.arange(max_tiles, dtype=jnp.int32)
    idx = jnp.minimum(tile, n_tiles - 1)
    tile_expert = jnp.sum((idx[:, None] >= ends[None, :]).astype(jnp.int32), axis=1)
    mine = tile_expert[:, None] == jnp.arange(N_EXPERTS, dtype=jnp.int32)[None, :]
    of_mine = lambda v: jnp.sum(jnp.where(mine, v[None, :], 0), axis=1)
    valid = jnp.clip(of_mine(cnt) - (idx - of_mine(ends - tiles_e)) * MOE_TM, 0, MOE_TM)
    tile_valid = jnp.where(tile < n_tiles, valid, 0).astype(jnp.int32)
    return (tile_expert, tile_valid, n_tiles.reshape(1), (ends - 1).astype(jnp.int32),
            tiles_e.astype(jnp.int32))


def _ple_final_kernel(h_ref, m_ref, p_ref, wpp_ref, wpg_ref, gp_ref, gf_ref, y_ref):
    h = h_ref[...] + m_ref[...]
    hn = _rmsnorm(h, gp_ref[...])
    h = h + _mm(p_ref[...], wpp_ref[...]) * _sigmoid(_mm(hn, wpg_ref[...]))
    y_ref[...] = _rmsnorm(h, gf_ref[...])


def _ple_final(h, m, p, wpp, wpg, gp, gf):
    t = h.shape[0]
    tm = min(t, 256)
    row = lambda n: pl.BlockSpec((tm, n), lambda i: (i, 0))
    full = lambda a: pl.BlockSpec(a.shape, lambda i: (0,) * a.ndim)
    return pl.pallas_call(
        _ple_final_kernel,
        grid=(t // tm,),
        in_specs=[row(D_MODEL), row(D_MODEL), row(PLE_DIM), full(wpp), full(wpg), full(gp), full(gf)],
        out_specs=row(D_MODEL),
        out_shape=jax.ShapeDtypeStruct((t, D_MODEL), F32),
        compiler_params=_params("parallel"),
        name="ple_final",
    )(h, m, p, wpp, wpg, gp, gf)


def kernel(x_prompt, x_sample, p_prompt, p_sample, cache_k, cache_v, state_conv, state_S, rel_bias, norm_mix, w_in, att_sink, conv_w, dn_A_log, dn_dt_bias, dn_norm, w_out, norm_ffn, w_router_group, w_router_expert, w_gate, w_up, w_down, w_ple_proj, w_ple_gate, norm_ple, norm_final):
    batch, seq, _ = x_prompt.shape
    nseq = x_sample.shape[0]
    assert x_sample.shape[1] == 1 and norm_mix.shape[0] == 1 and cache_k.shape[2] == WINDOW
    assert seq % GDN_TB == 0 and seq % ATT_BLOCK == 0

    wi = w_in[0]
    o_db = ATT_COLS + CONV_CH
    w_in_re = jnp.concatenate(
        [wi[:, :o_db], wi[:, o_db + 2 * DN_HEADS:], wi[:, o_db:o_db + 2 * DN_HEADS],
         jnp.zeros((D_MODEL, LANES - 2 * DN_HEADS), F32)], axis=1).astype(BF16)
    row = lambda a: a.reshape(1, -1).astype(F32)
    pad_lanes = lambda a, off: jnp.zeros((1, LANES), F32).at[0, off:off + a.shape[0]].set(a)
    alog = pad_lanes(dn_A_log[0], DN_HEADS)
    dtb = pad_lanes(dn_dt_bias[0], DN_HEADS)
    dnx = jnp.tile(dn_norm[0], DN_HEADS).reshape(1, DN_WIDTH)
    w_router = jnp.concatenate(
        [w_router_group[0], w_router_expert[0],
         jnp.zeros((D_MODEL, LANES - N_GROUPS - N_EXPERTS), F32)], axis=1).astype(BF16)
    wo = w_out[0].astype(BF16)
    wg, wu, wd = w_gate[0], w_up[0], w_down[0]
    wpp, wpg = w_ple_proj[0].astype(BF16), w_ple_gate[0].astype(BF16)
    sink = att_sink[0]

    qi = np.arange(ATT_BLOCK)[:, None]
    kj = np.arange(2 * ATT_BLOCK)[None, :]
    bucket_p = jnp.asarray(_t5_bucket_np(qi + ATT_BLOCK - kj))
    bucket_s = jnp.asarray(_t5_bucket_np(WINDOW - np.arange(WINDOW)[None, :]))

    xp = x_prompt.reshape(batch * seq, D_MODEL)
    att_p, qkv_p, dz_p, ba_p, xc_tails = _inproj_conv(xp, row(norm_mix[0]), w_in_re, conv_w[0], seq)
    o_att_p = _attn_prompt(att_p, bucket_p, rel_bias, sink, batch, seq)
    o_dn_p, s_p = _gdn_prompt(qkv_p, dz_p, ba_p, alog, dtb, dnx, batch, seq)
    h1, xn2, info, cnt = _route_sparse(xp, o_att_p, o_dn_p, wo, row(norm_ffn[0]), w_router)
    pos = _positions(info, cnt)
    pos1, pos2 = pos[:, 0], pos[:, 1]
    max_tiles = _moe_tiles(batch * seq)
    cnt_e = cnt[0, ROUTER_OFF:ROUTER_OFF + N_EXPERTS].astype(jnp.int32)
    tile_expert, tile_valid, n_tiles, last_tile, used = _tile_tables(cnt_e, max_tiles)
    xs_sorted = _sc_scatter_rows(xn2, pos1, pos2, max_tiles * MOE_TM)

    xs = x_sample.reshape(nseq, D_MODEL)
    att_s, xc_s, dz_s, ba_s = _inproj(xs, row(norm_mix[0]), w_in_re)
    ck_t = jnp.transpose(cache_k[0], (0, 2, 3, 1))
    cv_t = jnp.transpose(cache_v[0], (0, 2, 3, 1))
    o_att_s, ks_t, vs_t = _attn_sample(att_s, ck_t, cv_t, bucket_s, rel_bias, sink)
    sconv_t = jnp.swapaxes(state_conv[0], 0, 1)
    o_dn_s_t, s_s_t = _gdn_sample_lanes(xc_s, dz_s, ba_s, sconv_t, jnp.transpose(state_S[0], (1, 2, 3, 0)),
                                        conv_w[0], alog, dtb, dn_norm[0])
    s_s = jnp.transpose(s_s_t, (3, 0, 1, 2))

    h1_s, xn2_s, gates_s = _outproj_router(xs, o_att_s, o_dn_s_t, wo, row(norm_ffn[0]), w_router)

    ys, moe_s = _experts(xs_sorted, tile_expert, tile_valid, n_tiles, wg, wu, wd, xn2_s, gates_s)
    y1, y2 = _sc_gather_rows(ys, pos1, pos2)
    y_s = _ple_final(h1_s, moe_s, p_sample[0].reshape(nseq, PLE_DIM), wpp, wpg, row(norm_ple[0]),
                     row(norm_final))
    y_p = _ple_sparse(h1, info, y1, y2, p_prompt[0].reshape(batch * seq, PLE_DIM),
                      wpp, wpg, row(norm_ple[0]), row(norm_final))

    att_p3 = att_p.reshape(batch, seq, ATT_COLS)
    kv_shape = (1, batch, WINDOW, ATT_KV_HEADS, HEAD_DIM)
    k_p = att_p3[:, seq - WINDOW:, ATT_WIDTH:ATT_WIDTH + KV_WIDTH].reshape(kv_shape)
    v_p = att_p3[:, seq - WINDOW:, ATT_WIDTH + KV_WIDTH:].reshape(kv_shape)
    conv_p = xc_tails.reshape(batch, -1, TAIL, CONV_CH)[:, -1, TAIL - (CONV_WIDTH - 1):][None]
    k_s = jnp.transpose(ks_t, (0, 3, 1, 2))[None]
    v_s = jnp.transpose(vs_t, (0, 3, 1, 2))[None]
    conv_s = jnp.concatenate([state_conv[0][:, 1:], xc_s[:, None, :]], axis=1)[None]
    return (y_p.reshape(batch, seq, D_MODEL), y_s.reshape(nseq, 1, D_MODEL),
            k_p, v_p, conv_p, s_p[None], k_s, v_s, conv_s, s_s[None])
```

```python
import functools
import math

import numpy as np
import jax
import jax.numpy as jnp
from jax import lax
from jax.experimental import pallas as pl
from jax.experimental.pallas import tpu as pltpu
from jax.experimental.pallas import tpu_sc as plsc

F32 = jnp.float32
BF16 = jnp.bfloat16

D_MODEL = 1024
ATT_HEADS = 8
ATT_KV_HEADS = 2
HEAD_DIM = 64
GQA = ATT_HEADS // ATT_KV_HEADS
WINDOW = 128
ATT_BLOCK = 128
N_BUCKETS = 32
DN_HEADS = 8
DN_DK = 64
DN_DV = 64
CONV_WIDTH = 4
DN_CHUNK = 64
ATT_WIDTH = ATT_HEADS * HEAD_DIM
KV_WIDTH = ATT_KV_HEADS * HEAD_DIM
DN_WIDTH = DN_HEADS * DN_DV
CONV_CH = 3 * DN_WIDTH
N_GROUPS = 4
EXPERTS_PER_GROUP = 8
N_EXPERTS = N_GROUPS * EXPERTS_PER_GROUP
D_EXPERT = 256
PLE_DIM = 256
EPS = 1e-6
NEG_INF = float("-inf")

ATT_COLS = ATT_WIDTH + 2 * KV_WIDTH
LANES = 128
IN_COLS = ATT_COLS + CONV_CH + DN_WIDTH + LANES
ROUTER_OFF = N_GROUPS
VMEM_LIMIT = 48 * 1024 * 1024
ROW_TM = 512


def _params(*sem):
    return pltpu.CompilerParams(dimension_semantics=sem, vmem_limit_bytes=VMEM_LIMIT)


def _mm(a, b):
    return jnp.dot(a.astype(BF16), b.astype(BF16), preferred_element_type=F32)


def _mm_nt(a, b):
    return lax.dot_general(a.astype(BF16), b.astype(BF16), (((1,), (1,)), ((), ())),
                           preferred_element_type=F32)


def _mm_tn(a, b):
    return lax.dot_general(a.astype(BF16), b.astype(BF16), (((0,), (0,)), ((), ())),
                           preferred_element_type=F32)


def _split3(x):
    h1 = x.astype(BF16)
    r1 = x - h1.astype(F32)
    h2 = r1.astype(BF16)
    h3 = (r1 - h2.astype(F32)).astype(BF16)
    return h1, h2, h3


def _mm_sel_rhs(x, sel):
    h1, h2, h3 = _split3(x)
    d = lambda h: jnp.dot(h, sel, preferred_element_type=F32)
    return d(h1) + d(h2) + d(h3)


def _mm_sel_lhs(sel, x):
    h1, h2, h3 = _split3(x)
    d = lambda h: jnp.dot(sel, h, preferred_element_type=F32)
    return d(h1) + d(h2) + d(h3)


def _mm3(a, b):
    ah = a.astype(BF16)
    al = (a - ah.astype(F32)).astype(BF16)
    bh = b.astype(BF16)
    bl = (b - bh.astype(F32)).astype(BF16)
    d = lambda u, v: jnp.dot(u, v, preferred_element_type=F32)
    return d(ah, bh) + d(ah, bl) + d(al, bh)


def _sigmoid(x):
    return 1.0 / (1.0 + jnp.exp(-x))


def _silu(x):
    return x * _sigmoid(x)


def _silu_tanh(x):
    return x * (0.5 * jnp.tanh(0.5 * x) + 0.5)


def _softplus(x):
    return jnp.maximum(x, 0.0) + jnp.log1p(jnp.exp(-jnp.abs(x)))


def _rmsnorm(x, g):
    return x * lax.rsqrt(jnp.mean(x * x, axis=-1, keepdims=True) + EPS) * g


def _t5_bucket_np(dist):
    max_exact = N_BUCKETS // 2
    d = np.maximum(dist, 0)
    ratio = (np.log(np.maximum(d, 1).astype(np.float32) / np.float32(max_exact))
             / np.float32(math.log(WINDOW / max_exact))).astype(np.float32)
    large = np.minimum(max_exact + (ratio * np.float32(N_BUCKETS - max_exact)).astype(np.int32),
                       N_BUCKETS - 1)
    return np.where(d < max_exact, d, large).astype(np.int32)


def _bias_lookup(bucket, rb_ref, h):
    acc = jnp.zeros(bucket.shape, F32)
    for t in range(N_BUCKETS):
        acc = jnp.where(bucket == t, rb_ref[t, h], acc)
    return acc


def _inproj_kernel(x_ref, g_ref, w_ref, att_ref, xc_ref, dz_ref, ba_ref):
    xn = _rmsnorm(x_ref[...], g_ref[...]).astype(BF16)
    o0, o1, o2 = ATT_COLS, ATT_COLS + CONV_CH, ATT_COLS + CONV_CH + DN_WIDTH
    att_ref[...] = jnp.dot(xn, w_ref[:, :o0], preferred_element_type=F32)
    xc_ref[...] = jnp.dot(xn, w_ref[:, o0:o1], preferred_element_type=F32)
    dz_ref[...] = jnp.dot(xn, w_ref[:, o1:o2], preferred_element_type=F32)
    ba_ref[...] = jnp.dot(xn, w_ref[:, o2:], preferred_element_type=F32)


def _inproj(x, g, w):
    t = x.shape[0]
    tm = min(t, ROW_TM)
    row = lambda n: pl.BlockSpec((tm, n), lambda i: (i, 0))
    full = lambda a: pl.BlockSpec(a.shape, lambda i: (0,) * a.ndim)
    return pl.pallas_call(
        _inproj_kernel,
        grid=(t // tm,),
        in_specs=[row(D_MODEL), full(g), full(w)],
        out_specs=[row(ATT_COLS), row(CONV_CH), row(DN_WIDTH), row(LANES)],
        out_shape=[jax.ShapeDtypeStruct((t, n), F32) for n in (ATT_COLS, CONV_CH, DN_WIDTH, LANES)],
        compiler_params=_params("parallel"),
        name="inproj",
    )(x, g, w)


TAIL = 8
PAIR = 2 * DN_DK
N_PAIRS = DN_WIDTH // PAIR


def _head_sums(z, pair_ones):
    hi = z.astype(BF16)
    lw = (z - hi.astype(F32)).astype(BF16)
    d = lambda a, p: jnp.dot(a[:, p * PAIR:(p + 1) * PAIR], pair_ones, preferred_element_type=F32)
    return jnp.concatenate([d(hi, p) + d(lw, p) for p in range(N_PAIRS)], axis=1)


def _inproj_conv_kernel(x_ref, g_ref, w_ref, cw_ref, ones_ref, att_ref, qkv_ref, dz_ref, ba_ref, tail_ref,
                        xp_scr, *, tiles_per_seq):
    tm = x_ref.shape[0]

    @pl.when(pl.program_id(0) % tiles_per_seq == 0)
    def _():
        xp_scr[0:TAIL, :] = jnp.zeros((TAIL, CONV_CH), F32)

    xn = _rmsnorm(x_ref[...], g_ref[...]).astype(BF16)
    o0, o1, o2 = ATT_COLS, ATT_COLS + CONV_CH, ATT_COLS + CONV_CH + DN_WIDTH
    xc = jnp.dot(xn, w_ref[:, o0:o1], preferred_element_type=F32)
    att_ref[...] = jnp.dot(xn, w_ref[:, :o0], preferred_element_type=F32)
    dz_ref[...] = jnp.dot(xn, w_ref[:, o1:o2], preferred_element_type=F32)
    ba_ref[...] = jnp.dot(xn, w_ref[:, o2:], preferred_element_type=F32)

    xp_scr[TAIL:, :] = xc
    y = xp_scr[TAIL - 3:TAIL - 3 + tm, :] * cw_ref[0:1, :]
    y = y + xp_scr[TAIL - 2:TAIL - 2 + tm, :] * cw_ref[1:2, :]
    y = y + xp_scr[TAIL - 1:TAIL - 1 + tm, :] * cw_ref[2:3, :]
    y = y + xc * cw_ref[3:4, :]
    tail = xc[tm - TAIL:, :]
    xp_scr[0:TAIL, :] = tail
    tail_ref[0] = tail
    y = _silu_tanh(y)
    q = y[:, :DN_WIDTH]
    k = y[:, DN_WIDTH:2 * DN_WIDTH]
    inv_norm = lax.rsqrt(_head_sums(jnp.concatenate([q * q, k * k], axis=0), ones_ref[...]) + EPS)
    qkv_ref[:, :DN_WIDTH] = q * inv_norm[:tm] * (DN_DK ** -0.5)
    qkv_ref[:, DN_WIDTH:2 * DN_WIDTH] = k * inv_norm[tm:]
    qkv_ref[:, 2 * DN_WIDTH:] = y[:, 2 * DN_WIDTH:]


def _pair_ones():
    lane = np.arange(PAIR)
    return jnp.asarray((lane[:, None] // DN_DV == lane[None, :] // DN_DV).astype(np.float32), dtype=BF16)


def _inproj_conv(x, g, w, conv_w, seq):
    t = x.shape[0]
    tm = ROW_TM
    assert seq % tm == 0
    ones = _pair_ones()
    row = lambda n: pl.BlockSpec((tm, n), lambda i: (i, 0))
    full = lambda a: pl.BlockSpec(a.shape, lambda i: (0,) * a.ndim)
    return pl.pallas_call(
        functools.partial(_inproj_conv_kernel, tiles_per_seq=seq // tm),
        grid=(t // tm,),
        in_specs=[row(D_MODEL), full(g), full(w), full(conv_w), full(ones)],
        out_specs=[row(ATT_COLS), row(CONV_CH), row(DN_WIDTH), row(LANES),
                   pl.BlockSpec((1, TAIL, CONV_CH), lambda i: (i, 0, 0))],
        out_shape=[jax.ShapeDtypeStruct((t, n), F32) for n in (ATT_COLS, CONV_CH, DN_WIDTH, LANES)]
                  + [jax.ShapeDtypeStruct((t // tm, TAIL, CONV_CH), F32)],
        scratch_shapes=[pltpu.VMEM((TAIL + tm, CONV_CH), F32)],
        compiler_params=_params("arbitrary"),
        name="inproj_conv",
    )(x, g, w, conv_w, ones)


GROUP_ROWS = GQA * ATT_BLOCK


def _attn_prompt_kernel(cur_ref, prev_ref, bucket_ref, rb_ref, sink_ref, o_ref, bias_scr, sink_scr):
    i = pl.program_id(0)
    nseq = cur_ref.shape[0]

    @pl.when(i == 0)
    def _():
        qi = lax.broadcasted_iota(jnp.int32, (ATT_BLOCK, 2 * ATT_BLOCK), 0)
        kj = lax.broadcasted_iota(jnp.int32, (ATT_BLOCK, 2 * ATT_BLOCK), 1)
        dist = qi + ATT_BLOCK - kj
        band = jnp.logical_and(dist >= 0, dist < WINDOW)
        bucket = bucket_ref[...]
        hrow = lax.broadcasted_iota(jnp.int32, (GROUP_ROWS, 1), 0) // ATT_BLOCK
        for g in range(ATT_KV_HEADS):
            sink_col = jnp.zeros((GROUP_ROWS, 1), F32)
            for hh in range(GQA):
                h = g * GQA + hh
                bias = jnp.where(band, _bias_lookup(bucket, rb_ref, h), NEG_INF)
                bias_scr[0, g, hh * ATT_BLOCK:(hh + 1) * ATT_BLOCK, :] = bias
                bias_scr[1, g, hh * ATT_BLOCK:(hh + 1) * ATT_BLOCK, :] = jnp.where(kj >= ATT_BLOCK, bias, NEG_INF)
                sink_col = jnp.where(hrow == hh, sink_ref[h], sink_col)
            sink_scr[g] = sink_col

    first = (i == 0).astype(jnp.int32)
    probs = [(b, g) for b in range(nseq) for g in range(ATT_KV_HEADS)]
    scores = []
    for b, g in probs:
        cur = cur_ref[b]
        prev = prev_ref[b]
        q = jnp.concatenate([cur[:, (g * GQA + hh) * HEAD_DIM:(g * GQA + hh + 1) * HEAD_DIM]
                             for hh in range(GQA)], axis=0) * (HEAD_DIM ** -0.5)
        kcol = slice(ATT_WIDTH + g * HEAD_DIM, ATT_WIDTH + (g + 1) * HEAD_DIM)
        k2 = jnp.concatenate([prev[:, kcol], cur[:, kcol]], axis=0)
        scores.append(_mm_nt(q, k2) + bias_scr[first, g])
    probs_p, dens = [], []
    for (b, g), s in zip(probs, scores):
        sink = sink_scr[g]
        m = jnp.maximum(jnp.max(s, axis=-1, keepdims=True), sink)
        p = jnp.exp(s - m)
        dens.append(jnp.sum(p, axis=-1, keepdims=True) + jnp.exp(sink - m))
        probs_p.append(p)
    outs = {}
    for (b, g), p, den in zip(probs, probs_p, dens):
        vcol = slice(ATT_WIDTH + KV_WIDTH + g * HEAD_DIM, ATT_WIDTH + KV_WIDTH + (g + 1) * HEAD_DIM)
        v2 = jnp.concatenate([prev_ref[b][:, vcol], cur_ref[b][:, vcol]], axis=0)
        outs[b, g] = _mm(p, v2) / den
    for b in range(nseq):
        o_ref[b] = jnp.concatenate([outs[b, g][hh * ATT_BLOCK:(hh + 1) * ATT_BLOCK, :]
                                    for g in range(ATT_KV_HEADS) for hh in range(GQA)],
                                   axis=1).astype(o_ref.dtype)


def _attn_prompt(att, bucket, rel_bias, sink, batch, seq):
    nb = seq // ATT_BLOCK
    smem = pl.BlockSpec(memory_space=pltpu.SMEM)
    att3 = att.reshape(batch, seq, ATT_COLS)
    out = pl.pallas_call(
        _attn_prompt_kernel,
        grid=(nb,),
        in_specs=[
            pl.BlockSpec((batch, ATT_BLOCK, ATT_COLS), lambda i: (0, i, 0)),
            pl.BlockSpec((batch, ATT_BLOCK, ATT_COLS), lambda i: (0, jnp.maximum(i - 1, 0), 0)),
            pl.BlockSpec(bucket.shape, lambda i: (0, 0)),
            smem, smem,
        ],
        out_specs=pl.BlockSpec((batch, ATT_BLOCK, ATT_WIDTH), lambda i: (0, i, 0)),
        out_shape=jax.ShapeDtypeStruct((batch, seq, ATT_WIDTH), BF16),
        scratch_shapes=[pltpu.VMEM((2, ATT_KV_HEADS, GROUP_ROWS, 2 * ATT_BLOCK), F32),
                        pltpu.VMEM((ATT_KV_HEADS, GROUP_ROWS, 1), F32)],
        compiler_params=_params("arbitrary"),
        name="attn_prompt",
    )(att3, att3, bucket, rel_bias, sink)
    return out.reshape(batch * seq, ATT_WIDTH)


ATT_S_BB = 8


def _attn_sample_kernel(att_ref, ck_ref, cv_ref, bucket_ref, rb_ref, sink_ref, o_ref, ks_ref, vs_ref,
                        bias_scr, col_scr):
    hrow = lax.broadcasted_iota(jnp.int32, (ATT_HEADS, LANES), 0)
    lane = lax.broadcasted_iota(jnp.int32, (ATT_HEADS, LANES), 1)

    last = (lax.broadcasted_iota(jnp.int32, (3, WINDOW), 1) == WINDOW - 1).astype(BF16)
    is_last = lax.broadcasted_iota(jnp.int32, (KV_WIDTH, WINDOW), 1) == WINDOW - 1

    def shifted(cache_t, new_row):
        pieces = jnp.concatenate([p.astype(F32) for p in _split3(new_row)], axis=0).astype(BF16)
        col = lax.dot_general(pieces, last, (((0,), (0,)), ((), ())), preferred_element_type=F32)
        out = jnp.where(is_last, col, pltpu.roll(cache_t, WINDOW - 1, axis=1))
        return out.reshape(ATT_KV_HEADS, HEAD_DIM, WINDOW)

    for b in range(ATT_S_BB):
        row = att_ref[b:b + 1, :]
        ks_ref[b] = shifted(ck_ref[b].reshape(KV_WIDTH, WINDOW), row[:, ATT_WIDTH:ATT_WIDTH + KV_WIDTH])
        vs_ref[b] = shifted(cv_ref[b].reshape(KV_WIDTH, WINDOW), row[:, ATT_WIDTH + KV_WIDTH:])

    @pl.when(pl.program_id(0) == 0)
    def _():
        bucket = jnp.broadcast_to(bucket_ref[...], (ATT_HEADS, LANES))
        bias = jnp.zeros((ATT_HEADS, LANES), F32)
        cols = jnp.zeros((ATT_HEADS, LANES), F32)
        for h in range(ATT_HEADS):
            bias = jnp.where(hrow == h, _bias_lookup(bucket, rb_ref, h), bias)
            cols = jnp.where(jnp.logical_and(hrow == h, lane == 0), sink_ref[h], cols)
            cols = jnp.where(jnp.logical_and(hrow == h, lane == 1), rb_ref[0, h], cols)
        bias_scr[...] = jnp.where(lane >= 1, bias, NEG_INF)
        col_scr[...] = cols

    bias_c = bias_scr[...]
    sink = col_scr[:, 0:1]
    bias_n = col_scr[:, 1:2]
    same_group = (hrow // GQA) == (lane // HEAD_DIM)
    low_group = lax.broadcasted_iota(jnp.int32, (ATT_HEADS, HEAD_DIM), 0) < GQA
    rnd = lambda a: a.astype(BF16).astype(F32)
    seqs = range(ATT_S_BB)
    rows = [att_ref[b:b + 1, :] for b in seqs]
    q_bds = []
    for row in rows:
        q = row[:, :ATT_WIDTH] * (HEAD_DIM ** -0.5)
        qh = jnp.concatenate([q[:, h * HEAD_DIM:(h + 1) * HEAD_DIM] for h in range(ATT_HEADS)], axis=0)
        q_bds.append(jnp.where(same_group, jnp.concatenate([qh, qh], axis=1), 0.0))
    kv_t = lambda ref, b: ref[b].reshape(KV_WIDTH, WINDOW)
    s_cs = [_mm(q_bd, kv_t(ck_ref, b)) + bias_c for b, q_bd in zip(seqs, q_bds)]
    prs, pns = [], []
    for row, q_bd, s_c in zip(rows, q_bds, s_cs):
        kn = row[:, ATT_WIDTH:ATT_WIDTH + KV_WIDTH]
        s_n = jnp.sum(rnd(q_bd) * rnd(kn), axis=-1, keepdims=True) + bias_n
        m = jnp.maximum(jnp.maximum(jnp.max(s_c, axis=-1, keepdims=True), s_n), sink)
        p_c = jnp.exp(s_c - m)
        p_n = jnp.exp(s_n - m)
        den = jnp.sum(p_c, axis=-1, keepdims=True) + p_n + jnp.exp(sink - m)
        prs.append(p_c / den)
        pns.append(p_n / den)
    pvs = [_mm_nt(pr, kv_t(cv_ref, b)) for b, pr in zip(seqs, prs)]
    for b, row, pv, pn in zip(seqs, rows, pvs, pns):
        vn = row[:, ATT_WIDTH + KV_WIDTH:]
        o_full = pv + rnd(pn) * rnd(vn)
        o_sel = jnp.where(low_group, o_full[:, :HEAD_DIM], o_full[:, HEAD_DIM:])
        o_ref[b:b + 1, :] = jnp.concatenate([o_sel[h:h + 1, :] for h in range(ATT_HEADS)], axis=1)


def _attn_sample(att, ck, cv, bucket, rel_bias, sink):
    nseq = att.shape[0]
    smem = pl.BlockSpec(memory_space=pltpu.SMEM)
    cache = pl.BlockSpec((ATT_S_BB, ATT_KV_HEADS, HEAD_DIM, WINDOW), lambda i: (i, 0, 0, 0))
    return pl.pallas_call(
        _attn_sample_kernel,
        grid=(nseq // ATT_S_BB,),
        in_specs=[pl.BlockSpec((ATT_S_BB, ATT_COLS), lambda i: (i, 0)), cache, cache,
                  pl.BlockSpec(bucket.shape, lambda i: (0, 0)), smem, smem],
        out_specs=[pl.BlockSpec((ATT_S_BB, ATT_WIDTH), lambda i: (i, 0)), cache, cache],
        out_shape=[jax.ShapeDtypeStruct((nseq, ATT_WIDTH), F32),
                   jax.ShapeDtypeStruct(ck.shape, F32), jax.ShapeDtypeStruct(cv.shape, F32)],
        scratch_shapes=[pltpu.VMEM((ATT_HEADS, LANES), F32), pltpu.VMEM((ATT_HEADS, LANES), F32)],
        compiler_params=_params("arbitrary"),
        name="attn_sample",
    )(att, ck, cv, bucket, rel_bias, sink)


GDN_TB = 128
GDN_NC = GDN_TB // DN_CHUNK


def _gdn_gates(ba, alog, dtb):
    beta = _sigmoid(ba)
    g = -jnp.exp(alog) * _softplus(ba + dtb)
    return beta, g


def _pair_diag(x, lo):
    xb = x.astype(BF16)
    zero = jnp.zeros_like(xb)
    return jnp.concatenate([jnp.where(lo, xb, zero), jnp.where(lo, zero, xb)], axis=0)


def _gdn_prompt_kernel(qkv_ref, dz_ref, ba_ref, alog_ref, dtb_ref, dnx_ref,
                       hsum_ref, expb_ref, expg_ref, ltri_ref,
                       o_ref, s_out_ref, s_scr):
    i = pl.program_id(0)
    nb = qkv_ref.shape[0]

    @pl.when(i == 0)
    def _():
        s_scr[...] = jnp.zeros(s_scr.shape, F32)

    hsum = hsum_ref[...]
    ri = lax.broadcasted_iota(jnp.int32, (DN_CHUNK, PAIR), 0)
    ci = lax.broadcasted_iota(jnp.int32, (DN_CHUNK, PAIR), 1)
    lo = ci < DN_DK
    cj = jnp.where(lo, ci, ci - DN_DK)
    causal = ri >= cj
    strict = ri > cj
    eye = (ri == cj).astype(F32)

    def sel2(x, m):
        hi = x.astype(BF16)
        lw = (x - hi.astype(F32)).astype(BF16)
        return (jnp.dot(hi, m, preferred_element_type=F32) + jnp.dot(lw, m, preferred_element_type=F32))

    pre = []
    for b in range(nb):
        q = qkv_ref[b, :, :DN_WIDTH]
        k = qkv_ref[b, :, DN_WIDTH:2 * DN_WIDTH]
        v = qkv_ref[b, :, 2 * DN_WIDTH:]
        beta_c, g_c = _gdn_gates(ba_ref[b], alog_ref[...], dtb_ref[...])
        beta = sel2(beta_c, expb_ref[...])
        gam_c = _mm_sel_lhs(ltri_ref[...], g_c)
        gam = _mm_sel_rhs(gam_c, expg_ref[...])
        gam_t = gam_c.T
        kb = k * beta
        egam = jnp.exp(gam)
        pre.append(dict(q=q, k=k, kb=kb, vb=v * beta, qg=q * egam, wr=kb * egam, gam=gam, gam_t=gam_t))

    probs = [(b, p) for b in range(nb) for p in range(N_PAIRS)]
    pick = lambda m: jnp.where(lo, m[:DN_DK], m[DN_DK:])
    o_rows = [[] for _ in range(nb)]
    for c in range(GDN_NC):
        r0, r1 = c * DN_CHUNK, (c + 1) * DN_CHUNK
        sl = lambda name, b, p: pre[b][name][r0:r1, p * PAIR:(p + 1) * PAIR]
        raws = []
        for b, p in probs:
            k_p = sl("k", b, p)
            k_rows = jnp.concatenate([jnp.where(lo, k_p, 0.0), jnp.where(lo, 0.0, k_p)], axis=0)
            raws.append(_mm_nt(jnp.concatenate([sl("kb", b, p), sl("q", b, p)], axis=0), k_rows))
        pws, ts, qks = [], [], []
        for (b, p), raw in zip(probs, raws):
            gcol = sl("gam", b, p)
            h0 = DN_HEADS + 2 * p
            gam_t = pre[b]["gam_t"]
            grow = jnp.concatenate([gam_t[h0:h0 + 1, r0:r1], gam_t[h0 + 1:h0 + 2, r0:r1]], axis=1)
            decay = jnp.exp(jnp.where(causal, gcol - grow, NEG_INF))
            a = jnp.where(strict, raw[:DN_CHUNK] * decay, 0.0)
            qks.append(jnp.where(causal, raw[DN_CHUNK:] * decay, 0.0))
            pws.append(-a)
            ts.append(eye - a)
        pws = [_mm(pw, _pair_diag(pw, lo)) for pw in pws]
        for _ in range(4):
            rs = [_mm(jnp.concatenate([pw, t], axis=0), _pair_diag(pw, lo)) for pw, t in zip(pws, ts)]
            pws = [r[:DN_CHUNK] for r in rs]
            ts = [t + r[DN_CHUNK:] for t, r in zip(ts, rs)]
        rs = [_mm(t, _pair_diag(pw, lo)) for pw, t in zip(pws, ts)]
        ts = [t + r for t, r in zip(ts, rs)]
        sols = [_mm(t, jnp.concatenate([_pair_diag(sl("vb", b, p), lo), _pair_diag(sl("wr", b, p), lo)],
                                       axis=1)) for (b, p), t in zip(probs, ts)]
        qkuws = [_mm(qk, jnp.concatenate([_pair_diag(s[:, :PAIR], lo), _pair_diag(s[:, PAIR:], lo)], axis=1))
                 for qk, s in zip(qks, sols)]
        crosses, gls = [], []
        for (b, p), s in zip(probs, sols):
            gam_last = pre[b]["gam"][r1 - 1:r1, p * PAIR:(p + 1) * PAIR]
            kd = sl("k", b, p) * jnp.exp(gam_last - sl("gam", b, p))
            crosses.append(_mm_tn(kd, s))
            gls.append(jnp.exp(gam_last))
        lhs = [jnp.concatenate([pick(cr[:, PAIR:]), sl("qg", b, p) - qkuw[:, PAIR:]], axis=0)
               for (b, p), cr, qkuw in zip(probs, crosses, qkuws)]
        s_olds = [s_scr[b, p] for b, p in probs]
        rs = [_mm(l, _pair_diag(s_old, lo)) for l, s_old in zip(lhs, s_olds)]
        o_pairs = [[] for _ in range(nb)]
        for (b, p), r, s_old, gl, cr, qkuw in zip(probs, rs, s_olds, gls, crosses, qkuws):
            s_scr[b, p] = gl * s_old - r[:DN_DK] + pick(cr[:, :PAIR])
            o_pairs[b].append(r[DN_DK:] + qkuw[:, :PAIR])
        for b in range(nb):
            o_rows[b].append(jnp.concatenate(o_pairs[b], axis=1))

    o_all = jnp.concatenate([jnp.concatenate(rows, axis=0) for rows in o_rows], axis=0)
    inv_rms = lax.rsqrt(_head_sums(o_all * o_all, hsum) * (1.0 / DN_DV) + EPS)
    for b in range(nb):
        rows = slice(b * GDN_TB, (b + 1) * GDN_TB)
        o_ref[b] = (o_all[rows] * inv_rms[rows] * dnx_ref[...] * _silu_tanh(dz_ref[b])).astype(o_ref.dtype)

    @pl.when(i == pl.num_programs(0) - 1)
    def _():
        for b in range(nb):
            for p in range(N_PAIRS):
                s_p = s_scr[b, p]
                s_out_ref[b, 2 * p] = s_p[:, :DN_DV]
                s_out_ref[b, 2 * p + 1] = s_p[:, DN_DV:]


def _gdn_consts():
    lane = np.arange(DN_WIDTH)
    pl_lane = np.arange(PAIR)
    hsum = (pl_lane[:, None] // DN_DV == pl_lane[None, :] // DN_DV)
    src = np.arange(LANES)
    expb = (src[:, None] == lane[None, :] // DN_DV)
    expg = (src[:, None] == DN_HEADS + lane[None, :] // DN_DV)
    tok = np.arange(GDN_TB)
    ltri = np.logical_and(tok[:, None] >= tok[None, :],
                          tok[:, None] // DN_CHUNK == tok[None, :] // DN_CHUNK)
    as_bf16 = lambda m: jnp.asarray(m.astype(np.float32), dtype=BF16)
    return as_bf16(hsum), as_bf16(expb), as_bf16(expg), as_bf16(ltri)


def _gdn_prompt(xc, dz, ba, alog, dtb, dnx, batch, seq):
    nt = seq // GDN_TB
    hsum, expb, expg, ltri = _gdn_consts()
    row = lambda n: pl.BlockSpec((batch, GDN_TB, n), lambda i: (0, i, 0))
    full = lambda a: pl.BlockSpec(a.shape, lambda i: (0,) * a.ndim)
    consts = (alog, dtb, dnx, hsum, expb, expg, ltri)
    as3d = lambda a: a.reshape(batch, seq, a.shape[-1])
    o, s = pl.pallas_call(
        _gdn_prompt_kernel,
        grid=(nt,),
        in_specs=[row(CONV_CH), row(DN_WIDTH), row(LANES)] + [full(a) for a in consts],
        out_specs=[row(DN_WIDTH),
                   pl.BlockSpec((batch, DN_HEADS, DN_DK, DN_DV), lambda i: (0, 0, 0, 0))],
        out_shape=[jax.ShapeDtypeStruct((batch, seq, DN_WIDTH), BF16),
                   jax.ShapeDtypeStruct((batch, DN_HEADS, DN_DK, DN_DV), F32)],
        scratch_shapes=[pltpu.VMEM((batch, N_PAIRS, DN_DK, PAIR), F32)],
        compiler_params=_params("arbitrary"),
        name="gdn_prompt",
    )(as3d(xc), as3d(dz), as3d(ba), *consts)
    return o.reshape(batch * seq, DN_WIDTH), s


GDN_S_BB = 8


def _gdn_sample_kernel(xc_ref, dz_ref, ba_ref, sc_ref, s_ref, cw_ref, alog_ref, dtb_ref, dn_ref,
                       hsum_ref, eye_ref, hsel_ref, hrep3_ref, o_ref, s_out_ref):
    xc = xc_ref[...]
    y = sc_ref[0] * cw_ref[0:1, :]
    y = y + sc_ref[1] * cw_ref[1:2, :]
    y = y + sc_ref[2] * cw_ref[2:3, :]
    y = _silu(y + xc * cw_ref[3:4, :])
    hsum = hsum_ref[...]
    q = y[:, :DN_WIDTH]
    k = y[:, DN_WIDTH:2 * DN_WIDTH]
    v = y[:, 2 * DN_WIDTH:]
    q = q * lax.rsqrt(_mm_sel_rhs(q * q, hsum) + EPS) * (DN_DK ** -0.5)
    k = k * lax.rsqrt(_mm_sel_rhs(k * k, hsum) + EPS)
    beta_c, g_c = _gdn_gates(ba_ref[...], alog_ref[...], dtb_ref[...])
    eg_c = jnp.exp(g_c)
    eye = eye_ref[...]
    tr = lambda a: lax.dot_general(a, eye, (((0,), (0,)), ((), ())), precision=lax.Precision.HIGHEST,
                                   preferred_element_type=F32)
    gates_t = tr(jnp.concatenate([beta_c, eg_c], axis=1))
    beta_t = gates_t[:LANES]
    eg_t = gates_t[LANES:]
    dz = dz_ref[...]
    dn = dn_ref[...]
    split = lambda r: jnp.concatenate([r[:, h * DN_DV:(h + 1) * DN_DV] for h in range(DN_HEADS)], axis=0)
    own_head = hsel_ref[...].astype(F32)
    hrep3 = hrep3_ref[...]
    seqs = range(GDN_S_BB)
    dot = lambda a, b: jnp.dot(a.astype(BF16), b.astype(BF16), preferred_element_type=F32)

    def pieces(x):
        p1 = x.astype(BF16).astype(F32)
        r1 = x - p1
        p2 = r1.astype(BF16).astype(F32)
        return p1, p2, (r1 - p2).astype(BF16).astype(F32)

    heads = DN_HEADS
    k_pieces, kqs = [], []
    for b in seqs:
        kq_bd = jnp.concatenate([own_head * k[b:b + 1, :], own_head * q[b:b + 1, :]], axis=0)
        a1, a2, a3 = pieces(kq_bd)
        s1, s2, s3 = pieces(s_ref[b])
        r1 = dot(jnp.concatenate([a1, a2, a3], axis=0), s1)
        r2 = dot(jnp.concatenate([a1, a2], axis=0), s2)
        r3 = dot(a1, s3)
        n = 2 * heads
        kqs.append(((r3 + r2[n:] + r1[2 * n:]) + (r2[:n] + r1[n:2 * n])) + r1[:n])
        k_pieces.append((a1[:heads], a2[:heads], a3[:heads]))
    egs = [eg_t[DN_HEADS:2 * DN_HEADS, b:b + 1] for b in seqs]
    qks = [jnp.sum(split(q[b:b + 1, :]) * split(k[b:b + 1, :]), axis=-1, keepdims=True) for b in seqs]
    v_news = [beta_t[0:DN_HEADS, b:b + 1] * (split(v[b:b + 1, :]) - eg * kq[:heads])
              for b, eg, kq in zip(seqs, egs, kqs)]
    os_ = [eg * kq[heads:] + qk * v_new for eg, kq, qk, v_new in zip(egs, kqs, qks, v_news)]
    inv_rms = [lax.rsqrt(jnp.mean(o * o, axis=-1, keepdims=True) + EPS) for o in os_]
    for b, o, r in zip(seqs, os_, inv_rms):
        o_ref[b] = o * r * dn * _silu(split(dz[b:b + 1, :]))
    outers, egrows = [], []
    for (k1, k2, k3), v_new, eg in zip(k_pieces, v_news, egs):
        v1, v2, v3 = pieces(v_new)
        lhs = jnp.concatenate([k1, k1, k2, k1, k2, k3], axis=0).astype(BF16)
        rhs = jnp.concatenate([v1, v2, v1, v3, v2, v1], axis=0).astype(BF16)
        outers.append(lax.dot_general(lhs, rhs, (((0,), (0,)), ((), ())), preferred_element_type=F32))
        egrows.append(dot(hrep3, jnp.concatenate(pieces(jnp.broadcast_to(eg, (DN_HEADS, DN_DV))), axis=0)))
    for b, outer, egrow in zip(seqs, outers, egrows):
        s_out_ref[b] = s_ref[b] * egrow + outer


def _gdn_sample(xc, dz, ba, sconv_t, state, conv_w, alog, dtb, dn):
    nseq = xc.shape[0]
    lane = np.arange(DN_WIDTH)
    hsum = jnp.asarray((lane[:, None] // DN_DV == lane[None, :] // DN_DV).astype(np.float32), dtype=BF16)
    eye = jnp.eye(GDN_S_BB, dtype=F32)
    hsel_np = (np.arange(DN_HEADS)[:, None] == lane[None, :] // DN_DK).astype(np.float32)
    hsel = jnp.asarray(hsel_np, dtype=BF16)
    hrep3 = jnp.asarray(np.tile(hsel_np.T, (1, 3)), dtype=BF16)
    row = lambda n: pl.BlockSpec((GDN_S_BB, n), lambda i: (i, 0))
    full = lambda a: pl.BlockSpec(a.shape, lambda i: (0,) * a.ndim)
    st = pl.BlockSpec((GDN_S_BB, DN_HEADS * DN_DK, DN_DV), lambda i: (i, 0, 0))
    consts = (conv_w, alog, dtb, dn, hsum, eye, hsel, hrep3)
    return pl.pallas_call(
        _gdn_sample_kernel,
        grid=(nseq // GDN_S_BB,),
        in_specs=[row(CONV_CH), row(DN_WIDTH), row(LANES),
                  pl.BlockSpec((CONV_WIDTH - 1, GDN_S_BB, CONV_CH), lambda i: (0, i, 0)), st]
                 + [full(a) for a in consts],
        out_specs=[pl.BlockSpec((GDN_S_BB, DN_HEADS, DN_DV), lambda i: (i, 0, 0)), st],
        out_shape=[jax.ShapeDtypeStruct((nseq, DN_HEADS, DN_DV), F32),
                   jax.ShapeDtypeStruct(state.shape, F32)],
        compiler_params=_params("parallel"),
        name="gdn_sample",
    )(xc, dz, ba, sconv_t, state, *consts)


def _attn_sample_lanes_kernel(att_ref, ck_ref, cv_ref, bucket_ref, rb_ref, sink_ref, o_ref, s_scr):
    g = pl.program_id(0)
    nseq = att_ref.shape[0]
    rnd = lambda a: a.astype(BF16).astype(F32)
    att = att_ref[...]
    q_all_t = (att[:, :ATT_WIDTH] * (HEAD_DIM ** -0.5)).T
    kv_new_t = att[:, ATT_WIDTH:].T
    qsel = [jnp.where(g == 0, q_all_t[hh * HEAD_DIM:(hh + 1) * HEAD_DIM],
                      q_all_t[(GQA + hh) * HEAD_DIM:(GQA + hh + 1) * HEAD_DIM]) for hh in range(GQA)]
    qr = [rnd(q) for q in qsel]
    kn = rnd(jnp.where(g == 0, kv_new_t[0:HEAD_DIM], kv_new_t[HEAD_DIM:2 * HEAD_DIM]))
    vn = rnd(jnp.where(g == 0, kv_new_t[2 * HEAD_DIM:3 * HEAD_DIM], kv_new_t[3 * HEAD_DIM:]))

    def score_row(j, carry):
        kj = rnd(ck_ref[j, 0])
        for hh in range(GQA):
            s_scr[hh, pl.ds(j, 1), :] = jnp.sum(qr[hh] * kj, axis=0, keepdims=True)
        return carry
    lax.fori_loop(0, WINDOW, score_row, 0, unroll=2)

    bucket = bucket_ref[...]
    jrow = lax.broadcasted_iota(jnp.int32, (WINDOW, nseq), 0)
    prn = []
    for hh in range(GQA):
        h = g * GQA + hh
        bias = jnp.where(jrow >= 1, _bias_lookup(bucket, rb_ref, h), NEG_INF)
        s = s_scr[hh] + bias
        s_n = jnp.sum(qr[hh] * kn, axis=0, keepdims=True) + rb_ref[0, h]
        sink = sink_ref[h]
        m = jnp.maximum(jnp.maximum(jnp.max(s, axis=0, keepdims=True), s_n), sink)
        p = jnp.exp(s - m)
        p_n = jnp.exp(s_n - m)
        den = jnp.sum(p, axis=0, keepdims=True) + p_n + jnp.exp(sink - m)
        s_scr[hh] = rnd(p / den)
        prn.append(rnd(p_n / den))

    def value_row(j, acc):
        vj = rnd(cv_ref[j, 0])
        return tuple(acc[hh] + s_scr[hh, pl.ds(j, 1), :] * vj for hh in range(GQA))
    zero = jnp.zeros((HEAD_DIM, nseq), F32)
    acc = lax.fori_loop(0, WINDOW, value_row, (zero,) * GQA, unroll=2)
    for hh in range(GQA):
        o_ref[hh * HEAD_DIM:(hh + 1) * HEAD_DIM, :] = acc[hh] + prn[hh] * vn


def _attn_sample_lanes(att, ck_t, cv_t, rel_bias, sink):
    nseq = att.shape[0]
    assert nseq == LANES
    bucket = jnp.asarray(np.broadcast_to(_t5_bucket_np(WINDOW - np.arange(WINDOW))[:, None], (WINDOW, nseq)))
    smem = pl.BlockSpec(memory_space=pltpu.SMEM)
    cache = pl.BlockSpec((WINDOW, 1, HEAD_DIM, nseq), lambda g: (0, g, 0, 0))
    full = lambda a: pl.BlockSpec(a.shape, lambda g: (0,) * a.ndim)
    return pl.pallas_call(
        _attn_sample_lanes_kernel,
        grid=(ATT_KV_HEADS,),
        in_specs=[full(att), cache, cache, full(bucket), smem, smem],
        out_specs=pl.BlockSpec((GQA * HEAD_DIM, nseq), lambda g: (g, 0)),
        out_shape=jax.ShapeDtypeStruct((ATT_WIDTH, nseq), F32),
        scratch_shapes=[pltpu.VMEM((GQA, WINDOW, nseq), F32)],
        compiler_params=_params("arbitrary"),
        name="attn_sample_lanes",
    )(att, ck_t, cv_t, bucket, rel_bias, sink)


def _gdn_sample_front_kernel(xc_ref, dz_ref, ba_ref, sc_ref, cw_ref, alog_ref, dtb_ref, hsum_ref,
                             q_ref, k_ref, v_ref, dz_t_ref, gates_ref):
    xc = xc_ref[...]
    y = sc_ref[0] * cw_ref[0:1, :]
    y = y + sc_ref[1] * cw_ref[1:2, :]
    y = y + sc_ref[2] * cw_ref[2:3, :]
    y = _silu(y + xc * cw_ref[3:4, :])
    hsum = hsum_ref[...]
    q = y[:, :DN_WIDTH]
    k = y[:, DN_WIDTH:2 * DN_WIDTH]
    q = q * lax.rsqrt(_mm_sel_rhs(q * q, hsum) + EPS) * (DN_DK ** -0.5)
    k = k * lax.rsqrt(_mm_sel_rhs(k * k, hsum) + EPS)
    beta_c, g_c = _gdn_gates(ba_ref[...], alog_ref[...], dtb_ref[...])
    q_ref[...] = q.T
    k_ref[...] = k.T
    v_ref[...] = y[:, 2 * DN_WIDTH:].T
    dz_t_ref[...] = dz_ref[...].T
    gates_ref[0:LANES, :] = beta_c.T
    gates_ref[LANES:, :] = jnp.exp(g_c).T


def _gdn_sample_step_kernel(q_ref, k_ref, v_ref, dz_ref, gates_ref, dn_ref, s_ref, o_ref, s_out_ref):
    h = pl.program_id(0)
    beta = gates_ref[pl.ds(h, 1), :]
    eg = gates_ref[pl.ds(LANES + DN_HEADS + h, 1), :]
    q, k, v = q_ref[...], k_ref[...], v_ref[...]
    w = (k * beta) * eg
    qg = q * eg
    ws = jnp.zeros(v.shape, F32)
    qs = jnp.zeros(v.shape, F32)
    for dk in range(DN_DK):
        s_dk = s_ref[0, dk]
        ws = ws + w[dk:dk + 1, :] * s_dk
        qs = qs + qg[dk:dk + 1, :] * s_dk
    v_new = v * beta - ws
    qk = jnp.sum(q * k, axis=0, keepdims=True)
    o = qs + qk * v_new
    for dk in range(DN_DK):
        s_out_ref[0, dk] = s_ref[0, dk] * eg + k[dk:dk + 1, :] * v_new
    o = o * lax.rsqrt(jnp.mean(o * o, axis=0, keepdims=True) + EPS) * dn_ref[...]
    o_ref[...] = o * _silu(dz_ref[...])


def _gdn_sample_lanes(xc, dz, ba, sconv_t, state_t, conv_w, alog, dtb, dn):
    nseq = xc.shape[0]
    assert nseq == LANES
    lane = np.arange(DN_WIDTH)
    hsum = jnp.asarray((lane[:, None] // DN_DV == lane[None, :] // DN_DV).astype(np.float32), dtype=BF16)
    full = lambda a: pl.BlockSpec(a.shape, lambda i: (0,) * a.ndim)
    cm = jax.ShapeDtypeStruct((DN_WIDTH, nseq), F32)
    front_in = (xc, dz, ba, sconv_t, conv_w, alog, dtb, hsum)
    q_t, k_t, v_t, dz_t, gates_t = pl.pallas_call(
        _gdn_sample_front_kernel,
        grid=(1,),
        in_specs=[full(a) for a in front_in],
        out_specs=[pl.BlockSpec((DN_WIDTH, nseq), lambda i: (0, 0))] * 4
                  + [pl.BlockSpec((2 * LANES, nseq), lambda i: (0, 0))],
        out_shape=[cm, cm, cm, cm, jax.ShapeDtypeStruct((2 * LANES, nseq), F32)],
        compiler_params=_params("arbitrary"),
        name="gdn_sample_front",
    )(*front_in)
    dn_b = jnp.broadcast_to(dn.reshape(DN_DV, 1), (DN_DV, nseq))
    head = pl.BlockSpec((DN_DK, nseq), lambda h: (h, 0))
    st = pl.BlockSpec((1, DN_DK, DN_DV, nseq), lambda h: (h, 0, 0, 0))
    return pl.pallas_call(
        _gdn_sample_step_kernel,
        grid=(DN_HEADS,),
        in_specs=[head, head, head, head, full(gates_t), full(dn_b), st],
        out_specs=[head, st],
        out_shape=[cm, jax.ShapeDtypeStruct(state_t.shape, F32)],
        compiler_params=_params("parallel"),
        name="gdn_sample_step",
    )(q_t, k_t, v_t, dz_t, gates_t, dn_b, state_t)


def _route(xn, wr):
    logits = jnp.dot(xn, wr, preferred_element_type=F32)
    lane = lax.broadcasted_iota(jnp.int32, logits.shape, 1).astype(F32)
    first_at = lambda hit: jnp.min(jnp.where(hit, lane, float(LANES)), axis=-1, keepdims=True)
    glog = jnp.where(lane < N_GROUPS, logits, NEG_INF)
    gmax = jnp.max(glog, axis=-1, keepdims=True)
    gsel = first_at(glog == gmax)
    pgsel = 1.0 / jnp.sum(jnp.exp(glog - gmax), axis=-1, keepdims=True)
    lo = ROUTER_OFF + gsel * EXPERTS_PER_GROUP
    in_group = jnp.logical_and(lane >= lo, lane < lo + EXPERTS_PER_GROUP)
    elog = jnp.where(in_group, logits, NEG_INF)
    m1 = jnp.max(elog, axis=-1, keepdims=True)
    i1 = first_at(elog == m1)
    z = jnp.sum(jnp.exp(elog - m1), axis=-1, keepdims=True)
    elog2 = jnp.where(lane == i1, NEG_INF, elog)
    m2 = jnp.max(elog2, axis=-1, keepdims=True)
    i2 = first_at(elog2 == m2)
    p1 = 1.0 / z
    p2 = jnp.exp(m2 - m1) / z
    tot = p1 + p2
    return lane, i1, i2, p1 / tot * pgsel, p2 / tot * pgsel


def _outproj(x_ref, oa_ref, od_ref, wo_ref):
    return x_ref[...] + _mm(oa_ref[...], wo_ref[:ATT_WIDTH, :]) + _mm(od_ref[...], wo_ref[ATT_WIDTH:, :])


def _outproj_router_kernel(x_ref, oa_ref, od_t_ref, wo_ref, g_ref, wr_ref, h_ref, xn_ref, gate_ref):
    h = (x_ref[...] + _mm(oa_ref[...], wo_ref[:ATT_WIDTH, :])
         + _mm(od_t_ref[...].T, wo_ref[ATT_WIDTH:, :]))
    h_ref[...] = h
    xn = _rmsnorm(h, g_ref[...]).astype(BF16)
    xn_ref[...] = xn
    lane, i1, i2, g1, g2 = _route(xn, wr_ref[...])
    gate_ref[...] = jnp.where(lane == i1, g1, 0.0) + jnp.where(lane == i2, g2, 0.0)


def _outproj_router(x, oa, od_t, wo, g, wr):
    t = x.shape[0]
    tm = t
    row = lambda n: pl.BlockSpec((tm, n), lambda i: (i, 0))
    full = lambda a: pl.BlockSpec(a.shape, lambda i: (0,) * a.ndim)
    return pl.pallas_call(
        _outproj_router_kernel,
        grid=(t // tm,),
        in_specs=[row(D_MODEL), row(ATT_WIDTH), full(od_t), full(wo), full(g), full(wr)],
        out_specs=[row(D_MODEL), row(D_MODEL), row(LANES)],
        out_shape=[jax.ShapeDtypeStruct((t, D_MODEL), F32), jax.ShapeDtypeStruct((t, D_MODEL), BF16),
                   jax.ShapeDtypeStruct((t, LANES), F32)],
        compiler_params=_params("parallel"),
        name="outproj_router",
    )(x, oa, od_t, wo, g, wr)


def _moe_kernel(xn_ref, gate_ref, wg_ref, wu_ref, wd_ref, o_ref):
    e = pl.program_id(1)
    xn = xn_ref[...]
    lane = lax.broadcasted_iota(jnp.int32, gate_ref.shape, 1)
    gate = jnp.sum(jnp.where(lane == e + ROUTER_OFF, gate_ref[...], 0.0), axis=-1, keepdims=True)
    hg = jnp.dot(xn, wg_ref[...].astype(BF16), preferred_element_type=F32)
    hu = jnp.dot(xn, wu_ref[...].astype(BF16), preferred_element_type=F32)
    hm = _silu(hg) * hu * gate
    y = jnp.dot(hm.astype(BF16), wd_ref[...].astype(BF16), preferred_element_type=F32)

    @pl.when(e == 0)
    def _():
        o_ref[...] = y

    @pl.when(e > 0)
    def _():
        o_ref[...] += y


def _moe(xn, gates, wg, wu, wd):
    t = xn.shape[0]
    tm = min(t, 1024)
    return pl.pallas_call(
        _moe_kernel,
        grid=(t // tm, N_EXPERTS),
        in_specs=[pl.BlockSpec((tm, D_MODEL), lambda i, e: (i, 0)),
                  pl.BlockSpec((tm, LANES), lambda i, e: (i, 0)),
                  pl.BlockSpec((None, D_MODEL, D_EXPERT), lambda i, e: (e, 0, 0)),
                  pl.BlockSpec((None, D_MODEL, D_EXPERT), lambda i, e: (e, 0, 0)),
                  pl.BlockSpec((None, D_EXPERT, D_MODEL), lambda i, e: (e, 0, 0))],
        out_specs=pl.BlockSpec((tm, D_MODEL), lambda i, e: (i, 0)),
        out_shape=jax.ShapeDtypeStruct((t, D_MODEL), F32),
        compiler_params=_params("parallel", "arbitrary"),
        name="moe",
    )(xn, gates, wg, wu, wd)


MOE_TM = 512
POS_TM = 1024
INFO_G1, INFO_G2, INFO_E1, INFO_E2 = 0, 1, 2, 3
DMA_UNROLL = 8


def _moe_tiles(t):
    return (2 * t) // MOE_TM + N_EXPERTS


HALF = D_MODEL // 2
U32 = jnp.uint32


def _pack_rows(x):
    bits = lambda v: lax.bitcast_convert_type(v.astype(BF16).astype(F32), U32)
    return bits(x[:, HALF:]) | (bits(x[:, :HALF]) >> 16)


def _unpack_rows(w):
    lo = lax.bitcast_convert_type(w << 16, F32)
    hi = lax.bitcast_convert_type(w & jnp.uint32(0xFFFF0000), F32)
    return lo, hi


def _route_kernel(x_ref, oa_ref, od_ref, wo_ref, g_ref, wr_ref, h_ref, xn_ref, info_ref, cnt_ref, run_scr):
    h = _outproj(x_ref, oa_ref, od_ref, wo_ref)
    h_ref[...] = h
    xn = _rmsnorm(h, g_ref[...])
    xn_ref[...] = _pack_rows(xn)
    lane, i1, i2, g1, g2 = _route(xn.astype(BF16), wr_ref[...])
    info = jnp.where(lane == INFO_G1, g1, 0.0) + jnp.where(lane == INFO_G2, g2, 0.0)
    info = info + jnp.where(lane == INFO_E1, i1, 0.0) + jnp.where(lane == INFO_E2, i2, 0.0)
    info_ref[...] = info

    @pl.when(pl.program_id(0) == 0)
    def _():
        run_scr[...] = jnp.zeros(run_scr.shape, F32)
    picked = jnp.logical_or(lane == i1, lane == i2).astype(F32)
    run_scr[...] += jnp.sum(picked, axis=0, keepdims=True)
    cnt_ref[...] = run_scr[...]


def _route_sparse(x, oa, od, wo, g, wr):
    t = x.shape[0]
    tm = ROW_TM
    row = lambda n: pl.BlockSpec((tm, n), lambda i: (i, 0))
    full = lambda a: pl.BlockSpec(a.shape, lambda i: (0,) * a.ndim)
    return pl.pallas_call(
        _route_kernel,
        grid=(t // tm,),
        in_specs=[row(D_MODEL), row(ATT_WIDTH), row(DN_WIDTH), full(wo), full(g), full(wr)],
        out_specs=[row(D_MODEL), row(HALF), row(LANES), pl.BlockSpec((1, LANES), lambda i: (0, 0))],
        out_shape=[jax.ShapeDtypeStruct((t, D_MODEL), F32), jax.ShapeDtypeStruct((t, HALF), U32),
                   jax.ShapeDtypeStruct((t, LANES), F32), jax.ShapeDtypeStruct((1, LANES), F32)],
        scratch_shapes=[pltpu.VMEM((1, LANES), F32)],
        compiler_params=_params("arbitrary"),
        name="route",
    )(x, oa, od, wo, g, wr)


def _positions_kernel(info_ref, cnt_ref, ltri_ref, utri_ref, pos_ref, run_scr, off_scr):
    info = info_ref[...]
    lane = lax.broadcasted_iota(jnp.int32, info.shape, 1).astype(F32)
    hit1 = lane == info[:, INFO_E1:INFO_E1 + 1]
    hit2 = lane == info[:, INFO_E2:INFO_E2 + 1]
    onehot = jnp.logical_or(hit1, hit2).astype(F32)

    @pl.when(pl.program_id(0) == 0)
    def _():
        ln = lax.broadcasted_iota(jnp.int32, cnt_ref.shape, 1)
        is_expert = jnp.logical_and(ln >= ROUTER_OFF, ln < ROUTER_OFF + N_EXPERTS)
        tiles = jnp.where(is_expert, jnp.maximum(jnp.floor((cnt_ref[...] + (MOE_TM - 1)) * (1.0 / MOE_TM)), 1.0), 0.0)
        off_scr[...] = MOE_TM * jnp.dot(tiles.astype(BF16), utri_ref[...], preferred_element_type=F32)
        run_scr[...] = jnp.zeros(run_scr.shape, F32)

    before = (jnp.dot(ltri_ref[...], onehot.astype(BF16), preferred_element_type=F32)
              + run_scr[...] + off_scr[...])
    pos1 = jnp.sum(jnp.where(hit1, before, 0.0), axis=-1, keepdims=True)
    pos2 = jnp.sum(jnp.where(hit2, before, 0.0), axis=-1, keepdims=True)
    pos_ref[...] = (jnp.where(lane == 0, pos1, 0.0) + jnp.where(lane == 1, pos2, 0.0)).astype(jnp.int32)
    run_scr[...] += jnp.sum(onehot, axis=0, keepdims=True)


def _positions(info, cnt):
    t = info.shape[0]
    tm = min(t, POS_TM)
    tok = np.arange(tm)
    ltri = jnp.asarray((tok[:, None] > tok[None, :]).astype(np.float32), dtype=BF16)
    ln = np.arange(LANES)
    utri = jnp.asarray((ln[:, None] < ln[None, :]).astype(np.float32), dtype=BF16)
    full = lambda a: pl.BlockSpec(a.shape, lambda i: (0,) * a.ndim)
    return pl.pallas_call(
        _positions_kernel,
        grid=(t // tm,),
        in_specs=[pl.BlockSpec((tm, LANES), lambda i: (i, 0)), full(cnt), full(ltri), full(utri)],
        out_specs=pl.BlockSpec((tm, LANES), lambda i: (i, 0)),
        out_shape=jax.ShapeDtypeStruct((t, LANES), jnp.int32),
        scratch_shapes=[pltpu.VMEM((1, LANES), F32), pltpu.VMEM((1, LANES), F32)],
        compiler_params=_params("arbitrary"),
        name="positions",
    )(info, cnt, ltri, utri)


def _row_copy(src_hbm, src_row, dst_hbm, dst_row, sem):
    return pltpu.make_async_copy(src_hbm.at[pl.ds(src_row, 1)], dst_hbm.at[pl.ds(dst_row, 1)], sem)


SCATTER_SLOTS = 3


def _scatter_kernel(pos1_ref, pos2_ref, last_ref, used_ref, nt_ref, xn_hbm, zero_hbm, xs_hbm,
                    buf, lsem, sem, zsem, *, n_tok):
    max_tiles = xs_hbm.shape[0] // MOE_TM

    def zero_tile(tile):
        return pltpu.make_async_copy(zero_hbm, xs_hbm.at[pl.ds(tile * MOE_TM, MOE_TM)], zsem)

    def for_unused(fn):
        def body(tile, carry):
            fn(tile)
            return carry
        lax.fori_loop(nt_ref[0], max_tiles, body, 0)

    for e in range(N_EXPERTS):
        @pl.when(used_ref[e] > 0)
        def _():
            zero_tile(last_ref[e]).start()
    for_unused(lambda tile: zero_tile(tile).start())
    for e in range(N_EXPERTS):
        @pl.when(used_ref[e] > 0)
        def _():
            zero_tile(last_ref[e]).wait()
    for_unused(lambda tile: zero_tile(tile).wait())

    tm = buf.shape[1]
    n = n_tok // tm

    def load(i):
        return pltpu.make_async_copy(xn_hbm.at[pl.ds(i * tm, tm)], buf.at[i % SCATTER_SLOTS],
                                     lsem.at[i % SCATTER_SLOTS])

    def wait_rows(slot):
        pltpu.make_async_copy(xs_hbm.at[pl.ds(0, 2 * tm)], xs_hbm.at[pl.ds(0, 2 * tm)], sem.at[slot]).wait()

    load(0).start()
    load(1).start()

    def step(i, carry):
        slot = i % SCATTER_SLOTS
        load(i).wait()

        def body(j, c2):
            tok = i * tm + j
            src = buf.at[slot, pl.ds(j, 1)]
            pltpu.make_async_copy(src, xs_hbm.at[pl.ds(pos1_ref[tok], 1)], sem.at[slot]).start()
            pltpu.make_async_copy(src, xs_hbm.at[pl.ds(pos2_ref[tok], 1)], sem.at[slot]).start()
            return c2
        lax.fori_loop(0, tm, body, 0, unroll=DMA_UNROLL)

        @pl.when(i >= 1)
        def _():
            wait_rows((i - 1) % SCATTER_SLOTS)

        @pl.when(i + 2 < n)
        def _():
            load(i + 2).start()
        return carry
    lax.fori_loop(0, n, step, 0)
    wait_rows((n - 1) % SCATTER_SLOTS)


def _scatter_rows(xn, pos1, pos2, last_tile, used, n_tiles, n_rows):
    t = xn.shape[0]
    zero = jnp.zeros((MOE_TM, D_MODEL), F32)
    any_spec = pl.BlockSpec(memory_space=pl.ANY)
    return pl.pallas_call(
        functools.partial(_scatter_kernel, n_tok=t),
        grid_spec=pltpu.PrefetchScalarGridSpec(
            num_scalar_prefetch=5, grid=(1,),
            in_specs=[any_spec, any_spec], out_specs=any_spec,
            scratch_shapes=[pltpu.VMEM((SCATTER_SLOTS, MOE_TM, D_MODEL), F32),
                            pltpu.SemaphoreType.DMA((SCATTER_SLOTS,)),
                            pltpu.SemaphoreType.DMA((SCATTER_SLOTS,)),
                            pltpu.SemaphoreType.DMA]),
        out_shape=jax.ShapeDtypeStruct((n_rows, D_MODEL), F32),
        compiler_params=_params("arbitrary"),
        name="scatter_rows",
    )(pos1, pos2, last_tile, used, n_tiles, xn, zero)


def _experts_kernel(te_ref, tv_ref, nt_ref, xs_ref, wg_hbm, wu_hbm, wd_hbm, xn_new_ref, gate_new_ref,
                    ys_ref, moe_new_ref, wg_s, wu_s, wd_s, wg_f, wu_f, wd_f, wsem):
    i = pl.program_id(0)
    used = i < nt_ref[0]
    expert = te_ref[i]

    def fetch(e):
        slot = e % 2
        return [pltpu.make_async_copy(src.at[e], dst.at[slot], wsem.at[slot, j])
                for j, (src, dst) in enumerate(((wg_hbm, wg_f), (wu_hbm, wu_f), (wd_hbm, wd_f)))]

    @pl.when(jnp.logical_or(i == 0, expert != te_ref[jnp.maximum(i - 1, 0)]))
    def _():
        @pl.when(i == 0)
        def _():
            for c in fetch(expert):
                c.start()
        for c in fetch(expert):
            c.wait()

        @pl.when(expert + 1 < N_EXPERTS)
        def _():
            for c in fetch(expert + 1):
                c.start()
        slot = expert % 2
        wg_s[...] = wg_f[slot].astype(BF16)
        wu_s[...] = wu_f[slot].astype(BF16)
        wd_s[...] = wd_f[slot].astype(BF16)
        xn = xn_new_ref[...]
        lane = lax.broadcasted_iota(jnp.int32, gate_new_ref.shape, 1)
        gate = jnp.sum(jnp.where(lane == expert + ROUTER_OFF, gate_new_ref[...], 0.0), axis=-1, keepdims=True)
        hg = jnp.dot(xn, wg_s[...], preferred_element_type=F32)
        hu = jnp.dot(xn, wu_s[...], preferred_element_type=F32)
        hm = _silu(hg) * hu * gate
        y = jnp.dot(hm.astype(BF16), wd_s[...], preferred_element_type=F32)

        @pl.when(i == 0)
        def _():
            moe_new_ref[...] = y

        @pl.when(i > 0)
        def _():
            moe_new_ref[...] += y

    @pl.when(used)
    def _():
        row = lax.broadcasted_iota(jnp.int32, xs_ref.shape, 0)
        x_lo, x_hi = _unpack_rows(jnp.where(row < tv_ref[i], xs_ref[...], jnp.uint32(0)))
        x_lo = x_lo.astype(BF16)
        x_hi = x_hi.astype(BF16)
        up = lambda w_s: (jnp.dot(x_lo, w_s[:HALF, :], preferred_element_type=F32)
                          + jnp.dot(x_hi, w_s[HALF:, :], preferred_element_type=F32))
        hm = (_silu_tanh(up(wg_s)) * up(wu_s)).astype(BF16)
        ys_ref[...] = _pack_rows(jnp.dot(hm, wd_s[...], preferred_element_type=F32))

    @pl.when(jnp.logical_not(used))
    def _():
        ys_ref[...] = jnp.zeros(ys_ref.shape, U32)


def _experts(xs, tile_expert, tile_valid, n_tiles, wg, wu, wd, xn_new, gate_new):
    max_tiles = xs.shape[0] // MOE_TM
    rows = pl.BlockSpec((MOE_TM, HALF), lambda i, te, tv, nt: (i, 0))
    hbm = pl.BlockSpec(memory_space=pl.ANY)
    full = lambda a: pl.BlockSpec(a.shape, lambda i, te, tv, nt: (0,) * a.ndim)
    return pl.pallas_call(
        _experts_kernel,
        grid_spec=pltpu.PrefetchScalarGridSpec(
            num_scalar_prefetch=3, grid=(max_tiles,),
            in_specs=[rows, hbm, hbm, hbm, full(xn_new), full(gate_new)],
            out_specs=[rows, pl.BlockSpec(xn_new.shape, lambda i, te, tv, nt: (0, 0))],
            scratch_shapes=[pltpu.VMEM((D_MODEL, D_EXPERT), BF16), pltpu.VMEM((D_MODEL, D_EXPERT), BF16),
                            pltpu.VMEM((D_EXPERT, D_MODEL), BF16),
                            pltpu.VMEM((2, D_MODEL, D_EXPERT), F32), pltpu.VMEM((2, D_MODEL, D_EXPERT), F32),
                            pltpu.VMEM((2, D_EXPERT, D_MODEL), F32), pltpu.SemaphoreType.DMA((2, 3))]),
        out_shape=[jax.ShapeDtypeStruct(xs.shape, U32), jax.ShapeDtypeStruct(xn_new.shape, F32)],
        compiler_params=_params("arbitrary"),
        name="experts",
    )(tile_expert, tile_valid, n_tiles, xs, wg, wu, wd, xn_new, gate_new)


def _ple_gather_kernel(pos1_ref, pos2_ref, h_ref, info_ref, p_ref, wpp_ref, wpg_ref, gp_ref, gf_ref,
                       ys_hbm, y_ref, ybuf, sem):
    i = pl.program_id(0)
    n = pl.num_programs(0)
    tm = h_ref.shape[0]

    def issue(tile, slot):
        def body(j, carry):
            tok = tile * tm + j
            pltpu.make_async_copy(ys_hbm.at[pl.ds(pos1_ref[tok], 1)], ybuf.at[slot, 0, pl.ds(j, 1)],
                                  sem.at[slot]).start()
            pltpu.make_async_copy(ys_hbm.at[pl.ds(pos2_ref[tok], 1)], ybuf.at[slot, 1, pl.ds(j, 1)],
                                  sem.at[slot]).start()
            return carry
        lax.fori_loop(0, tm, body, 0, unroll=DMA_UNROLL)

    @pl.when(i == 0)
    def _():
        issue(0, 0)

    @pl.when(i + 1 < n)
    def _():
        issue(i + 1, (i + 1) % 2)

    slot = i % 2
    pltpu.make_async_copy(ybuf.at[slot], ybuf.at[slot], sem.at[slot]).wait()
    info = info_ref[...]
    moe = info[:, INFO_G1:INFO_G1 + 1] * ybuf[slot, 0] + info[:, INFO_G2:INFO_G2 + 1] * ybuf[slot, 1]
    h = h_ref[...] + moe
    hn = _rmsnorm(h, gp_ref[...])
    h = h + _mm(p_ref[...], wpp_ref[...]) * _sigmoid(_mm(hn, wpg_ref[...]))
    y_ref[...] = _rmsnorm(h, gf_ref[...])


def _ple_gather(h, info, p, ys, pos1, pos2, wpp, wpg, gp, gf):
    t = h.shape[0]
    tm = 256
    row = lambda n: pl.BlockSpec((tm, n), lambda i, p1, p2: (i, 0))
    full = lambda a: pl.BlockSpec(a.shape, lambda i, p1, p2: (0,) * a.ndim)
    return pl.pallas_call(
        _ple_gather_kernel,
        grid_spec=pltpu.PrefetchScalarGridSpec(
            num_scalar_prefetch=2, grid=(t // tm,),
            in_specs=[row(D_MODEL), row(LANES), row(PLE_DIM), full(wpp), full(wpg), full(gp), full(gf),
                      pl.BlockSpec(memory_space=pl.ANY)],
            out_specs=row(D_MODEL),
            scratch_shapes=[pltpu.VMEM((2, 2, tm, D_MODEL), F32), pltpu.SemaphoreType.DMA((2,))]),
        out_shape=jax.ShapeDtypeStruct((t, D_MODEL), F32),
        compiler_params=_params("arbitrary"),
        name="ple_gather",
    )(pos1, pos2, h, info, p, wpp, wpg, gp, gf, ys)


SC_IDX = 128
SC_ROWS = 64
SC_WORKERS = 32


def _sc_mesh():
    return plsc.VectorSubcoreMesh(core_axis_name="c", subcore_axis_name="s")


def _sc_windows(t, fn):
    per_worker = t // SC_WORKERS
    worker = lax.axis_index(("c", "s"))

    @pl.loop(0, per_worker // SC_IDX)
    def _(w):
        fn(worker * per_worker + w * SC_IDX)


def _sc_scatter_rows(xn, pos1, pos2, n_rows):
    t, d = xn.shape
    assert t % (SC_WORKERS * SC_IDX) == 0
    idx_t = pltpu.VMEM((1, SC_IDX), jnp.int32)

    @pl.kernel(out_type=jax.ShapeDtypeStruct((n_rows, d), xn.dtype), mesh=_sc_mesh(),
               scratch_types=[idx_t, idx_t, pltpu.VMEM((SC_ROWS, d), xn.dtype)])
    def scatter(x_hbm, p1_hbm, p2_hbm, o_hbm, i1_v, i2_v, buf):
        def window(base):
            pltpu.sync_copy(p1_hbm.at[:, pl.ds(base, SC_IDX)], i1_v)
            pltpu.sync_copy(p2_hbm.at[:, pl.ds(base, SC_IDX)], i2_v)
            for k in range(SC_IDX // SC_ROWS):
                pltpu.sync_copy(x_hbm.at[pl.ds(base + k * SC_ROWS, SC_ROWS)], buf)
                pltpu.sync_copy(buf, o_hbm.at[i1_v.at[0, pl.ds(k * SC_ROWS, SC_ROWS)]])
                pltpu.sync_copy(buf, o_hbm.at[i2_v.at[0, pl.ds(k * SC_ROWS, SC_ROWS)]])
        _sc_windows(t, window)

    return scatter(xn, pos1.reshape(1, t), pos2.reshape(1, t))


def _sc_gather_rows(ys, pos1, pos2):
    t = pos1.shape[0]
    d = ys.shape[1]
    assert t % (SC_WORKERS * SC_IDX) == 0
    idx_t = pltpu.VMEM((1, SC_IDX), jnp.int32)
    out = jax.ShapeDtypeStruct((t, d), ys.dtype)

    buf_t = pltpu.VMEM((SC_ROWS, d), ys.dtype)

    @pl.kernel(out_type=(out, out), mesh=_sc_mesh(),
               scratch_types=[idx_t, idx_t, buf_t, buf_t, pltpu.SemaphoreType.DMA((2,)),
                              pltpu.SemaphoreType.DMA((2,))])
    def gather(y_hbm, p1_hbm, p2_hbm, o1_hbm, o2_hbm, i1_v, i2_v, buf_a, buf_b, gsem, wsem):
        bufs = (buf_a, buf_b)

        def window(base):
            pltpu.sync_copy(p1_hbm.at[:, pl.ds(base, SC_IDX)], i1_v)
            pltpu.sync_copy(p2_hbm.at[:, pl.ds(base, SC_IDX)], i2_v)
            items = [(idx_v, o_hbm, k) for k in range(SC_IDX // SC_ROWS)
                     for idx_v, o_hbm in ((i1_v, o1_hbm), (i2_v, o2_hbm))]

            def read(n):
                idx_v, _, k = items[n]
                return pltpu.make_async_copy(y_hbm.at[idx_v.at[0, pl.ds(k * SC_ROWS, SC_ROWS)]],
                                             bufs[n % 2], gsem.at[n % 2])

            def write(n):
                _, o_hbm, k = items[n]
                return pltpu.make_async_copy(bufs[n % 2], o_hbm.at[pl.ds(base + k * SC_ROWS, SC_ROWS)],
                                             wsem.at[n % 2])

            read(0).start()
            for n in range(len(items)):
                read(n).wait()
                if n >= 1:
                    write(n - 1).wait()
                if n + 1 < len(items):
                    read(n + 1).start()
                write(n).start()
            write(len(items) - 1).wait()
        _sc_windows(t, window)

    return gather(ys, pos1.reshape(1, t), pos2.reshape(1, t))


def _ple_sparse_kernel(h_ref, info_ref, y1_ref, y2_ref, p_ref, wpp_ref, wpg_ref, gp_ref, gf_ref, y_ref):
    info = info_ref[...]
    g1 = info[:, INFO_G1:INFO_G1 + 1]
    g2 = info[:, INFO_G2:INFO_G2 + 1]
    y1_lo, y1_hi = _unpack_rows(y1_ref[...])
    y2_lo, y2_hi = _unpack_rows(y2_ref[...])
    moe = jnp.concatenate([g1 * y1_lo + g2 * y2_lo, g1 * y1_hi + g2 * y2_hi], axis=1)
    h = h_ref[...] + moe
    hn = _rmsnorm(h, gp_ref[...])
    h = h + _mm(p_ref[...], wpp_ref[...]) * _sigmoid(_mm(hn, wpg_ref[...]))
    y_ref[...] = _rmsnorm(h, gf_ref[...])


def _ple_sparse(h, info, y1, y2, p, wpp, wpg, gp, gf):
    t = h.shape[0]
    tm = ROW_TM
    row = lambda n: pl.BlockSpec((tm, n), lambda i: (i, 0))
    full = lambda a: pl.BlockSpec(a.shape, lambda i: (0,) * a.ndim)
    return pl.pallas_call(
        _ple_sparse_kernel,
        grid=(t // tm,),
        in_specs=[row(D_MODEL), row(LANES), row(HALF), row(HALF), row(PLE_DIM),
                  full(wpp), full(wpg), full(gp), full(gf)],
        out_specs=row(D_MODEL),
        out_shape=jax.ShapeDtypeStruct((t, D_MODEL), F32),
        compiler_params=_params("parallel"),
        name="ple_sparse",
    )(h, info, y1, y2, p, wpp, wpg, gp, gf)


def _tile_tables(cnt, max_tiles):
    tiles_e = jnp.maximum((cnt + (MOE_TM - 1)) // MOE_TM, 1)
    ends = jnp.cumsum(tiles_e)
    n_tiles = ends[-1]
    tile = jnp.arange(max_tiles, dtype=jnp.int32)
    idx = jnp.minimum(tile, n_tiles - 1)
    tile_expert = jnp.sum((idx[:, None] >= ends[None, :]).astype(jnp.int32), axis=1)
    mine = tile_expert[:, None] == jnp.arange(N_EXPERTS, dtype=jnp.int32)[None, :]
    of_mine = lambda v: jnp.sum(jnp.where(mine, v[None, :], 0), axis=1)
    valid = jnp.clip(of_mine(cnt) - (idx - of_mine(ends - tiles_e)) * MOE_TM, 0, MOE_TM)
    tile_valid = jnp.where(tile < n_tiles, valid, 0).astype(jnp.int32)
    return (tile_expert, tile_valid, n_tiles.reshape(1), (ends - 1).astype(jnp.int32),
            tiles_e.astype(jnp.int32))


def _ple_final_kernel(h_ref, m_ref, p_ref, wpp_ref, wpg_ref, gp_ref, gf_ref, y_ref):
    h = h_ref[...] + m_ref[...]
    hn = _rmsnorm(h, gp_ref[...])
    h = h + _mm(p_ref[...], wpp_ref[...]) * _sigmoid(_mm(hn, wpg_ref[...]))
    y_ref[...] = _rmsnorm(h, gf_ref[...])


def _ple_final(h, m, p, wpp, wpg, gp, gf):
    t = h.shape[0]
    tm = min(t, 256)
    row = lambda n: pl.BlockSpec((tm, n), lambda i: (i, 0))
    full = lambda a: pl.BlockSpec(a.shape, lambda i: (0,) * a.ndim)
    return pl.pallas_call(
        _ple_final_kernel,
        grid=(t // tm,),
        in_specs=[row(D_MODEL), row(D_MODEL), row(PLE_DIM), full(wpp), full(wpg), full(gp), full(gf)],
        out_specs=row(D_MODEL),
        out_shape=jax.ShapeDtypeStruct((t, D_MODEL), F32),
        compiler_params=_params("parallel"),
        name="ple_final",
    )(h, m, p, wpp, wpg, gp, gf)


def kernel(x_prompt, x_sample, p_prompt, p_sample, cache_k, cache_v, state_conv, state_S, rel_bias, norm_mix, w_in, att_sink, conv_w, dn_A_log, dn_dt_bias, dn_norm, w_out, norm_ffn, w_router_group, w_router_expert, w_gate, w_up, w_down, w_ple_proj, w_ple_gate, norm_ple, norm_final):
    batch, seq, _ = x_prompt.shape
    nseq = x_sample.shape[0]
    assert x_sample.shape[1] == 1 and norm_mix.shape[0] == 1 and cache_k.shape[2] == WINDOW
    assert seq % GDN_TB == 0 and seq % ATT_BLOCK == 0

    wi = w_in[0]
    o_db = ATT_COLS + CONV_CH
    w_in_re = jnp.concatenate(
        [wi[:, :o_db], wi[:, o_db + 2 * DN_HEADS:], wi[:, o_db:o_db + 2 * DN_HEADS],
         jnp.zeros((D_MODEL, LANES - 2 * DN_HEADS), F32)], axis=1).astype(BF16)
    row = lambda a: a.reshape(1, -1).astype(F32)
    pad_lanes = lambda a, off: jnp.zeros((1, LANES), F32).at[0, off:off + a.shape[0]].set(a)
    alog = pad_lanes(dn_A_log[0], DN_HEADS)
    dtb = pad_lanes(dn_dt_bias[0], DN_HEADS)
    dnx = jnp.tile(dn_norm[0], DN_HEADS).reshape(1, DN_WIDTH)
    w_router = jnp.concatenate(
        [w_router_group[0], w_router_expert[0],
         jnp.zeros((D_MODEL, LANES - N_GROUPS - N_EXPERTS), F32)], axis=1).astype(BF16)
    wo = w_out[0].astype(BF16)
    wg, wu, wd = w_gate[0], w_up[0], w_down[0]
    wpp, wpg = w_ple_proj[0].astype(BF16), w_ple_gate[0].astype(BF16)
    sink = att_sink[0]

    qi = np.arange(ATT_BLOCK)[:, None]
    kj = np.arange(2 * ATT_BLOCK)[None, :]
    bucket_p = jnp.asarray(_t5_bucket_np(qi + ATT_BLOCK - kj))
    bucket_s = jnp.asarray(_t5_bucket_np(WINDOW - np.arange(WINDOW)[None, :]))

    xp = x_prompt.reshape(batch * seq, D_MODEL)
    att_p, qkv_p, dz_p, ba_p, xc_tails = _inproj_conv(xp, row(norm_mix[0]), w_in_re, conv_w[0], seq)
    o_att_p = _attn_prompt(att_p, bucket_p, rel_bias, sink, batch, seq)
    o_dn_p, s_p = _gdn_prompt(qkv_p, dz_p, ba_p, alog, dtb, dnx, batch, seq)
    h1, xn2, info, cnt = _route_sparse(xp, o_att_p, o_dn_p, wo, row(norm_ffn[0]), w_router)
    pos = _positions(info, cnt)
    pos1, pos2 = pos[:, 0], pos[:, 1]
    max_tiles = _moe_tiles(batch * seq)
    cnt_e = cnt[0, ROUTER_OFF:ROUTER_OFF + N_EXPERTS].astype(jnp.int32)
    tile_expert, tile_valid, n_tiles, last_tile, used = _tile_tables(cnt_e, max_tiles)
    xs_sorted = _sc_scatter_rows(xn2, pos1, pos2, max_tiles * MOE_TM)

    xs = x_sample.reshape(nseq, D_MODEL)
    att_s, xc_s, dz_s, ba_s = _inproj(xs, row(norm_mix[0]), w_in_re)
    ck_t = jnp.transpose(cache_k[0], (0, 2, 3, 1))
    cv_t = jnp.transpose(cache_v[0], (0, 2, 3, 1))
    o_att_s, ks_t, vs_t = _attn_sample(att_s, ck_t, cv_t, bucket_s, rel_bias, sink)
    sconv_t = jnp.swapaxes(state_conv[0], 0, 1)
    o_dn_s_t, s_s_t = _gdn_sample_lanes(xc_s, dz_s, ba_s, sconv_t, jnp.transpose(state_S[0], (1, 2, 3, 0)),
                                        conv_w[0], alog, dtb, dn_norm[0])
    s_s = jnp.transpose(s_s_t, (3, 0, 1, 2))

    h1_s, xn2_s, gates_s = _outproj_router(xs, o_att_s, o_dn_s_t, wo, row(norm_ffn[0]), w_router)

    ys, moe_s = _experts(xs_sorted, tile_expert, tile_valid, n_tiles, wg, wu, wd, xn2_s, gates_s)
    y1, y2 = _sc_gather_rows(ys, pos1, pos2)
    y_s = _ple_final(h1_s, moe_s, p_sample[0].reshape(nseq, PLE_DIM), wpp, wpg, row(norm_ple[0]),
                     row(norm_final))
    y_p = _ple_sparse(h1, info, y1, y2, p_prompt[0].reshape(batch * seq, PLE_DIM),
                      wpp, wpg, row(norm_ple[0]), row(norm_final))

    att_p3 = att_p.reshape(batch, seq, ATT_COLS)
    kv_shape = (1, batch, WINDOW, ATT_KV_HEADS, HEAD_DIM)
    k_p = att_p3[:, seq - WINDOW:, ATT_WIDTH:ATT_WIDTH + KV_WIDTH].reshape(kv_shape)
    v_p = att_p3[:, seq - WINDOW:, ATT_WIDTH + KV_WIDTH:].reshape(kv_shape)
    conv_p = xc_tails.reshape(batch, -1, TAIL, CONV_CH)[:, -1, TAIL - (CONV_WIDTH - 1):][None]
    k_s = jnp.transpose(ks_t, (0, 3, 1, 2))[None]
    v_s = jnp.transpose(vs_t, (0, 3, 1, 2))[None]
    conv_s = jnp.concatenate([state_conv[0][:, 1:], xc_s[:, None, :]], axis=1)[None]
    return (y_p.reshape(batch, seq, D_MODEL), y_s.reshape(nseq, 1, D_MODEL),
            k_p, v_p, conv_p, s_p[None], k_s, v_s, conv_s, s_s[None])
```

```python
import functools
import math

import numpy as np
import jax
import jax.numpy as jnp
from jax import lax
from jax.experimental import pallas as pl
from jax.experimental.pallas import tpu as pltpu
from jax.experimental.pallas import tpu_sc as plsc

F32 = jnp.float32
BF16 = jnp.bfloat16

D_MODEL = 1024
ATT_HEADS = 8
ATT_KV_HEADS = 2
HEAD_DIM = 64
GQA = ATT_HEADS // ATT_KV_HEADS
WINDOW = 128
ATT_BLOCK = 128
N_BUCKETS = 32
DN_HEADS = 8
DN_DK = 64
DN_DV = 64
CONV_WIDTH = 4
DN_CHUNK = 64
ATT_WIDTH = ATT_HEADS * HEAD_DIM
KV_WIDTH = ATT_KV_HEADS * HEAD_DIM
DN_WIDTH = DN_HEADS * DN_DV
CONV_CH = 3 * DN_WIDTH
N_GROUPS = 4
EXPERTS_PER_GROUP = 8
N_EXPERTS = N_GROUPS * EXPERTS_PER_GROUP
D_EXPERT = 256
PLE_DIM = 256
EPS = 1e-6
NEG_INF = float("-inf")

ATT_COLS = ATT_WIDTH + 2 * KV_WIDTH
LANES = 128
IN_COLS = ATT_COLS + CONV_CH + DN_WIDTH + LANES
ROUTER_OFF = N_GROUPS
VMEM_LIMIT = 48 * 1024 * 1024
ROW_TM = 512


def _params(*sem):
    return pltpu.CompilerParams(dimension_semantics=sem, vmem_limit_bytes=VMEM_LIMIT)


def _mm(a, b):
    return jnp.dot(a.astype(BF16), b.astype(BF16), preferred_element_type=F32)


def _mm_nt(a, b):
    return lax.dot_general(a.astype(BF16), b.astype(BF16), (((1,), (1,)), ((), ())),
                           preferred_element_type=F32)


def _mm_tn(a, b):
    return lax.dot_general(a.astype(BF16), b.astype(BF16), (((0,), (0,)), ((), ())),
                           preferred_element_type=F32)


def _split3(x):
    h1 = x.astype(BF16)
    r1 = x - h1.astype(F32)
    h2 = r1.astype(BF16)
    h3 = (r1 - h2.astype(F32)).astype(BF16)
    return h1, h2, h3


def _mm_sel_rhs(x, sel):
    h1, h2, h3 = _split3(x)
    d = lambda h: jnp.dot(h, sel, preferred_element_type=F32)
    return d(h1) + d(h2) + d(h3)


def _mm_sel_lhs(sel, x):
    h1, h2, h3 = _split3(x)
    d = lambda h: jnp.dot(sel, h, preferred_element_type=F32)
    return d(h1) + d(h2) + d(h3)


def _mm3(a, b):
    ah = a.astype(BF16)
    al = (a - ah.astype(F32)).astype(BF16)
    bh = b.astype(BF16)
    bl = (b - bh.astype(F32)).astype(BF16)
    d = lambda u, v: jnp.dot(u, v, preferred_element_type=F32)
    return d(ah, bh) + d(ah, bl) + d(al, bh)


def _sigmoid(x):
    return 1.0 / (1.0 + jnp.exp(-x))


def _silu(x):
    return x * _sigmoid(x)


def _silu_tanh(x):
    return x * (0.5 * jnp.tanh(0.5 * x) + 0.5)


def _softplus(x):
    return jnp.maximum(x, 0.0) + jnp.log1p(jnp.exp(-jnp.abs(x)))


def _rmsnorm(x, g):
    return x * lax.rsqrt(jnp.mean(x * x, axis=-1, keepdims=True) + EPS) * g


def _t5_bucket_np(dist):
    max_exact = N_BUCKETS // 2
    d = np.maximum(dist, 0)
    ratio = (np.log(np.maximum(d, 1).astype(np.float32) / np.float32(max_exact))
             / np.float32(math.log(WINDOW / max_exact))).astype(np.float32)
    large = np.minimum(max_exact + (ratio * np.float32(N_BUCKETS - max_exact)).astype(np.int32),
                       N_BUCKETS - 1)
    return np.where(d < max_exact, d, large).astype(np.int32)


def _bias_lookup(bucket, rb_ref, h):
    acc = jnp.zeros(bucket.shape, F32)
    for t in range(N_BUCKETS):
        acc = jnp.where(bucket == t, rb_ref[t, h], acc)
    return acc


def _inproj_kernel(x_ref, g_ref, w_ref, att_ref, xc_ref, dz_ref, ba_ref):
    xn = _rmsnorm(x_ref[...], g_ref[...]).astype(BF16)
    o0, o1, o2 = ATT_COLS, ATT_COLS + CONV_CH, ATT_COLS + CONV_CH + DN_WIDTH
    att_ref[...] = jnp.dot(xn, w_ref[:, :o0], preferred_element_type=F32)
    xc_ref[...] = jnp.dot(xn, w_ref[:, o0:o1], preferred_element_type=F32)
    dz_ref[...] = jnp.dot(xn, w_ref[:, o1:o2], preferred_element_type=F32)
    ba_ref[...] = jnp.dot(xn, w_ref[:, o2:], preferred_element_type=F32)


def _inproj(x, g, w):
    t = x.shape[0]
    tm = min(t, ROW_TM)
    row = lambda n: pl.BlockSpec((tm, n), lambda i: (i, 0))
    full = lambda a: pl.BlockSpec(a.shape, lambda i: (0,) * a.ndim)
    return pl.pallas_call(
        _inproj_kernel,
        grid=(t // tm,),
        in_specs=[row(D_MODEL), full(g), full(w)],
        out_specs=[row(ATT_COLS), row(CONV_CH), row(DN_WIDTH), row(LANES)],
        out_shape=[jax.ShapeDtypeStruct((t, n), F32) for n in (ATT_COLS, CONV_CH, DN_WIDTH, LANES)],
        compiler_params=_params("parallel"),
        name="inproj",
    )(x, g, w)


TAIL = 8
PAIR = 2 * DN_DK
N_PAIRS = DN_WIDTH // PAIR


def _head_sums(z, pair_ones):
    hi = z.astype(BF16)
    lw = (z - hi.astype(F32)).astype(BF16)
    d = lambda a, p: jnp.dot(a[:, p * PAIR:(p + 1) * PAIR], pair_ones, preferred_element_type=F32)
    return jnp.concatenate([d(hi, p) + d(lw, p) for p in range(N_PAIRS)], axis=1)


def _inproj_conv_kernel(x_ref, g_ref, w_ref, cw_ref, ones_ref, att_ref, qkv_ref, dz_ref, ba_ref, tail_ref,
                        xp_scr, *, tiles_per_seq):
    tm = x_ref.shape[0]

    @pl.when(pl.program_id(0) % tiles_per_seq == 0)
    def _():
        xp_scr[...] = jnp.zeros((TAIL, CONV_CH), F32)

    xn = _rmsnorm(x_ref[...], g_ref[...]).astype(BF16)
    o0, o1, o2 = ATT_COLS, ATT_COLS + CONV_CH, ATT_COLS + CONV_CH + DN_WIDTH
    xc = jnp.dot(xn, w_ref[:, o0:o1], preferred_element_type=F32)
    att_ref[...] = jnp.dot(xn, w_ref[:, :o0], preferred_element_type=F32)
    dz_ref[...] = jnp.dot(xn, w_ref[:, o1:o2], preferred_element_type=F32)
    ba_ref[...] = jnp.dot(xn, w_ref[:, o2:], preferred_element_type=F32)

    head = jnp.concatenate([xp_scr[...], xc[:TAIL, :]], axis=0)

    def shifted(j):
        return jnp.concatenate([head[TAIL - j:2 * TAIL - j, :], pltpu.roll(xc, j, axis=0)[TAIL:, :]], axis=0)

    y = shifted(3) * cw_ref[0:1, :]
    y = y + shifted(2) * cw_ref[1:2, :]
    y = y + shifted(1) * cw_ref[2:3, :]
    y = y + xc * cw_ref[3:4, :]
    tail = xc[tm - TAIL:, :]
    xp_scr[...] = tail
    tail_ref[0] = tail
    y = _silu_tanh(y)
    q = y[:, :DN_WIDTH]
    k = y[:, DN_WIDTH:2 * DN_WIDTH]
    inv_norm = lax.rsqrt(_head_sums(jnp.concatenate([q * q, k * k], axis=0), ones_ref[...]) + EPS)
    qkv_ref[:, :DN_WIDTH] = q * inv_norm[:tm] * (DN_DK ** -0.5)
    qkv_ref[:, DN_WIDTH:2 * DN_WIDTH] = k * inv_norm[tm:]
    qkv_ref[:, 2 * DN_WIDTH:] = y[:, 2 * DN_WIDTH:]


def _pair_ones():
    lane = np.arange(PAIR)
    return jnp.asarray((lane[:, None] // DN_DV == lane[None, :] // DN_DV).astype(np.float32), dtype=BF16)


def _inproj_conv(x, g, w, conv_w, seq):
    t = x.shape[0]
    tm = ROW_TM
    assert seq % tm == 0
    ones = _pair_ones()
    row = lambda n: pl.BlockSpec((tm, n), lambda i: (i, 0))
    full = lambda a: pl.BlockSpec(a.shape, lambda i: (0,) * a.ndim)
    return pl.pallas_call(
        functools.partial(_inproj_conv_kernel, tiles_per_seq=seq // tm),
        grid=(t // tm,),
        in_specs=[row(D_MODEL), full(g), full(w), full(conv_w), full(ones)],
        out_specs=[row(ATT_COLS), row(CONV_CH), row(DN_WIDTH), row(LANES),
                   pl.BlockSpec((1, TAIL, CONV_CH), lambda i: (i, 0, 0))],
        out_shape=[jax.ShapeDtypeStruct((t, n), F32) for n in (ATT_COLS, CONV_CH, DN_WIDTH, LANES)]
                  + [jax.ShapeDtypeStruct((t // tm, TAIL, CONV_CH), F32)],
        scratch_shapes=[pltpu.VMEM((TAIL, CONV_CH), F32)],
        compiler_params=_params("arbitrary"),
        name="inproj_conv",
    )(x, g, w, conv_w, ones)


GROUP_ROWS = GQA * ATT_BLOCK


def _attn_prompt_kernel(cur_ref, prev_ref, bucket_ref, rb_ref, sink_ref, o_ref, bias_scr, sink_scr):
    i = pl.program_id(0)
    nseq = cur_ref.shape[0]

    @pl.when(i == 0)
    def _():
        qi = lax.broadcasted_iota(jnp.int32, (ATT_BLOCK, 2 * ATT_BLOCK), 0)
        kj = lax.broadcasted_iota(jnp.int32, (ATT_BLOCK, 2 * ATT_BLOCK), 1)
        dist = qi + ATT_BLOCK - kj
        band = jnp.logical_and(dist >= 0, dist < WINDOW)
        bucket = bucket_ref[...]
        hrow = lax.broadcasted_iota(jnp.int32, (GROUP_ROWS, 1), 0) // ATT_BLOCK
        for g in range(ATT_KV_HEADS):
            sink_col = jnp.zeros((GROUP_ROWS, 1), F32)
            for hh in range(GQA):
                h = g * GQA + hh
                bias = jnp.where(band, _bias_lookup(bucket, rb_ref, h), NEG_INF)
                bias_scr[0, g, hh * ATT_BLOCK:(hh + 1) * ATT_BLOCK, :] = bias
                bias_scr[1, g, hh * ATT_BLOCK:(hh + 1) * ATT_BLOCK, :] = jnp.where(kj >= ATT_BLOCK, bias, NEG_INF)
                sink_col = jnp.where(hrow == hh, sink_ref[h], sink_col)
            sink_scr[g] = sink_col

    first = (i == 0).astype(jnp.int32)
    probs = [(b, g) for b in range(nseq) for g in range(ATT_KV_HEADS)]
    scores = []
    for b, g in probs:
        cur = cur_ref[b]
        prev = prev_ref[b]
        q = jnp.concatenate([cur[:, (g * GQA + hh) * HEAD_DIM:(g * GQA + hh + 1) * HEAD_DIM]
                             for hh in range(GQA)], axis=0) * (HEAD_DIM ** -0.5)
        kcol = slice(ATT_WIDTH + g * HEAD_DIM, ATT_WIDTH + (g + 1) * HEAD_DIM)
        k2 = jnp.concatenate([prev[:, kcol], cur[:, kcol]], axis=0)
        scores.append(_mm_nt(q, k2) + bias_scr[first, g])
    probs_p, dens = [], []
    for (b, g), s in zip(probs, scores):
        sink = sink_scr[g]
        m = jnp.maximum(jnp.max(s, axis=-1, keepdims=True), sink)
        p = jnp.exp(s - m)
        dens.append(jnp.sum(p, axis=-1, keepdims=True) + jnp.exp(sink - m))
        probs_p.append(p)
    outs = {}
    for (b, g), p, den in zip(probs, probs_p, dens):
        vcol = slice(ATT_WIDTH + KV_WIDTH + g * HEAD_DIM, ATT_WIDTH + KV_WIDTH + (g + 1) * HEAD_DIM)
        v2 = jnp.concatenate([prev_ref[b][:, vcol], cur_ref[b][:, vcol]], axis=0)
        outs[b, g] = _mm(p, v2) / den
    for b in range(nseq):
        o_ref[b] = jnp.concatenate([outs[b, g][hh * ATT_BLOCK:(hh + 1) * ATT_BLOCK, :]
                                    for g in range(ATT_KV_HEADS) for hh in range(GQA)],
                                   axis=1).astype(o_ref.dtype)


def _attn_prompt(att, bucket, rel_bias, sink, batch, seq):
    nb = seq // ATT_BLOCK
    smem = pl.BlockSpec(memory_space=pltpu.SMEM)
    att3 = att.reshape(batch, seq, ATT_COLS)
    out = pl.pallas_call(
        _attn_prompt_kernel,
        grid=(nb,),
        in_specs=[
            pl.BlockSpec((batch, ATT_BLOCK, ATT_COLS), lambda i: (0, i, 0)),
            pl.BlockSpec((batch, ATT_BLOCK, ATT_COLS), lambda i: (0, jnp.maximum(i - 1, 0), 0)),
            pl.BlockSpec(bucket.shape, lambda i: (0, 0)),
            smem, smem,
        ],
        out_specs=pl.BlockSpec((batch, ATT_BLOCK, ATT_WIDTH), lambda i: (0, i, 0)),
        out_shape=jax.ShapeDtypeStruct((batch, seq, ATT_WIDTH), BF16),
        scratch_shapes=[pltpu.VMEM((2, ATT_KV_HEADS, GROUP_ROWS, 2 * ATT_BLOCK), F32),
                        pltpu.VMEM((ATT_KV_HEADS, GROUP_ROWS, 1), F32)],
        compiler_params=_params("arbitrary"),
        name="attn_prompt",
    )(att3, att3, bucket, rel_bias, sink)
    return out.reshape(batch * seq, ATT_WIDTH)


ATT_S_BB = 8


def _attn_sample_kernel(att_ref, ck_ref, cv_ref, bucket_ref, rb_ref, sink_ref, o_ref, ks_ref, vs_ref,
                        bias_scr, col_scr):
    hrow = lax.broadcasted_iota(jnp.int32, (ATT_HEADS, LANES), 0)
    lane = lax.broadcasted_iota(jnp.int32, (ATT_HEADS, LANES), 1)

    last = (lax.broadcasted_iota(jnp.int32, (3, WINDOW), 1) == WINDOW - 1).astype(BF16)
    is_last = lax.broadcasted_iota(jnp.int32, (KV_WIDTH, WINDOW), 1) == WINDOW - 1

    def shifted(cache_t, new_row):
        pieces = jnp.concatenate([p.astype(F32) for p in _split3(new_row)], axis=0).astype(BF16)
        col = lax.dot_general(pieces, last, (((0,), (0,)), ((), ())), preferred_element_type=F32)
        out = jnp.where(is_last, col, pltpu.roll(cache_t, WINDOW - 1, axis=1))
        return out.reshape(ATT_KV_HEADS, HEAD_DIM, WINDOW)

    for b in range(ATT_S_BB):
        row = att_ref[b:b + 1, :]
        ks_ref[b] = shifted(ck_ref[b].reshape(KV_WIDTH, WINDOW), row[:, ATT_WIDTH:ATT_WIDTH + KV_WIDTH])
        vs_ref[b] = shifted(cv_ref[b].reshape(KV_WIDTH, WINDOW), row[:, ATT_WIDTH + KV_WIDTH:])

    @pl.when(pl.program_id(0) == 0)
    def _():
        bucket = jnp.broadcast_to(bucket_ref[...], (ATT_HEADS, LANES))
        bias = jnp.zeros((ATT_HEADS, LANES), F32)
        cols = jnp.zeros((ATT_HEADS, LANES), F32)
        for h in range(ATT_HEADS):
            bias = jnp.where(hrow == h, _bias_lookup(bucket, rb_ref, h), bias)
            cols = jnp.where(jnp.logical_and(hrow == h, lane == 0), sink_ref[h], cols)
            cols = jnp.where(jnp.logical_and(hrow == h, lane == 1), rb_ref[0, h], cols)
        bias_scr[...] = jnp.where(lane >= 1, bias, NEG_INF)
        col_scr[...] = cols

    bias_c = bias_scr[...]
    sink = col_scr[:, 0:1]
    bias_n = col_scr[:, 1:2]
    same_group = (hrow // GQA) == (lane // HEAD_DIM)
    low_group = lax.broadcasted_iota(jnp.int32, (ATT_HEADS, HEAD_DIM), 0) < GQA
    rnd = lambda a: a.astype(BF16).astype(F32)
    seqs = range(ATT_S_BB)
    rows = [att_ref[b:b + 1, :] for b in seqs]
    q_bds = []
    for row in rows:
        q = row[:, :ATT_WIDTH] * (HEAD_DIM ** -0.5)
        qh = jnp.concatenate([q[:, h * HEAD_DIM:(h + 1) * HEAD_DIM] for h in range(ATT_HEADS)], axis=0)
        q_bds.append(jnp.where(same_group, jnp.concatenate([qh, qh], axis=1), 0.0))
    kv_t = lambda ref, b: ref[b].reshape(KV_WIDTH, WINDOW)
    s_cs = [_mm(q_bd, kv_t(ck_ref, b)) + bias_c for b, q_bd in zip(seqs, q_bds)]
    prs, pns = [], []
    for row, q_bd, s_c in zip(rows, q_bds, s_cs):
        kn = row[:, ATT_WIDTH:ATT_WIDTH + KV_WIDTH]
        s_n = jnp.sum(rnd(q_bd) * rnd(kn), axis=-1, keepdims=True) + bias_n
        m = jnp.maximum(jnp.maximum(jnp.max(s_c, axis=-1, keepdims=True), s_n), sink)
        p_c = jnp.exp(s_c - m)
        p_n = jnp.exp(s_n - m)
        den = jnp.sum(p_c, axis=-1, keepdims=True) + p_n + jnp.exp(sink - m)
        prs.append(p_c / den)
        pns.append(p_n / den)
    pvs = [_mm_nt(pr, kv_t(cv_ref, b)) for b, pr in zip(seqs, prs)]
    for b, row, pv, pn in zip(seqs, rows, pvs, pns):
        vn = row[:, ATT_WIDTH + KV_WIDTH:]
        o_full = pv + rnd(pn) * rnd(vn)
        o_sel = jnp.where(low_group, o_full[:, :HEAD_DIM], o_full[:, HEAD_DIM:])
        o_ref[b:b + 1, :] = jnp.concatenate([o_sel[h:h + 1, :] for h in range(ATT_HEADS)], axis=1)


def _attn_sample(att, ck, cv, bucket, rel_bias, sink):
    nseq = att.shape[0]
    smem = pl.BlockSpec(memory_space=pltpu.SMEM)
    cache = pl.BlockSpec((ATT_S_BB, ATT_KV_HEADS, HEAD_DIM, WINDOW), lambda i: (i, 0, 0, 0))
    return pl.pallas_call(
        _attn_sample_kernel,
        grid=(nseq // ATT_S_BB,),
        in_specs=[pl.BlockSpec((ATT_S_BB, ATT_COLS), lambda i: (i, 0)), cache, cache,
                  pl.BlockSpec(bucket.shape, lambda i: (0, 0)), smem, smem],
        out_specs=[pl.BlockSpec((ATT_S_BB, ATT_WIDTH), lambda i: (i, 0)), cache, cache],
        out_shape=[jax.ShapeDtypeStruct((nseq, ATT_WIDTH), F32),
                   jax.ShapeDtypeStruct(ck.shape, F32), jax.ShapeDtypeStruct(cv.shape, F32)],
        scratch_shapes=[pltpu.VMEM((ATT_HEADS, LANES), F32), pltpu.VMEM((ATT_HEADS, LANES), F32)],
        compiler_params=_params("arbitrary"),
        name="attn_sample",
    )(att, ck, cv, bucket, rel_bias, sink)


GDN_TB = 128
GDN_NC = GDN_TB // DN_CHUNK


def _gdn_gates(ba, alog, dtb):
    beta = _sigmoid(ba)
    g = -jnp.exp(alog) * _softplus(ba + dtb)
    return beta, g


def _pair_diag(x, lo):
    xb = x.astype(BF16)
    zero = jnp.zeros_like(xb)
    return jnp.concatenate([jnp.where(lo, xb, zero), jnp.where(lo, zero, xb)], axis=0)


def _gdn_prompt_kernel(qkv_ref, dz_ref, ba_ref, alog_ref, dtb_ref, dnx_ref,
                       hsum_ref, expb_ref, expg_ref, ltri_ref,
                       o_ref, s_out_ref, s_scr):
    i = pl.program_id(0)
    nb = qkv_ref.shape[0]

    @pl.when(i == 0)
    def _():
        s_scr[...] = jnp.zeros(s_scr.shape, F32)

    hsum = hsum_ref[...]
    ri = lax.broadcasted_iota(jnp.int32, (DN_CHUNK, PAIR), 0)
    ci = lax.broadcasted_iota(jnp.int32, (DN_CHUNK, PAIR), 1)
    lo = ci < DN_DK
    cj = jnp.where(lo, ci, ci - DN_DK)
    causal = ri >= cj
    strict = ri > cj
    eye = (ri == cj).astype(F32)

    def sel2(x, m):
        hi = x.astype(BF16)
        lw = (x - hi.astype(F32)).astype(BF16)
        return (jnp.dot(hi, m, preferred_element_type=F32) + jnp.dot(lw, m, preferred_element_type=F32))

    pre = []
    for b in range(nb):
        q = qkv_ref[b, :, :DN_WIDTH]
        k = qkv_ref[b, :, DN_WIDTH:2 * DN_WIDTH]
        v = qkv_ref[b, :, 2 * DN_WIDTH:]
        beta_c, g_c = _gdn_gates(ba_ref[b], alog_ref[...], dtb_ref[...])
        beta = sel2(beta_c, expb_ref[...])
        gam_c = _mm_sel_lhs(ltri_ref[...], g_c)
        gam = _mm_sel_rhs(gam_c, expg_ref[...])
        gam_t = gam_c.T
        kb = k * beta
        egam = jnp.exp(gam)
        pre.append(dict(q=q, k=k, kb=kb, vb=v * beta, qg=q * egam, wr=kb * egam, gam=gam, gam_t=gam_t))

    probs = [(b, p) for b in range(nb) for p in range(N_PAIRS)]
    pick = lambda m: jnp.where(lo, m[:DN_DK], m[DN_DK:])
    o_rows = [[] for _ in range(nb)]
    for c in range(GDN_NC):
        r0, r1 = c * DN_CHUNK, (c + 1) * DN_CHUNK
        sl = lambda name, b, p: pre[b][name][r0:r1, p * PAIR:(p + 1) * PAIR]
        raws = []
        for b, p in probs:
            k_p = sl("k", b, p)
            k_rows = jnp.concatenate([jnp.where(lo, k_p, 0.0), jnp.where(lo, 0.0, k_p)], axis=0)
            raws.append(_mm_nt(jnp.concatenate([sl("kb", b, p), sl("q", b, p)], axis=0), k_rows))
        pws, ts, qks = [], [], []
        for (b, p), raw in zip(probs, raws):
            gcol = sl("gam", b, p)
            h0 = DN_HEADS + 2 * p
            gam_t = pre[b]["gam_t"]
            grow = jnp.concatenate([gam_t[h0:h0 + 1, r0:r1], gam_t[h0 + 1:h0 + 2, r0:r1]], axis=1)
            decay = jnp.exp(jnp.where(causal, gcol - grow, NEG_INF))
            a = jnp.where(strict, raw[:DN_CHUNK] * decay, 0.0)
            qks.append(jnp.where(causal, raw[DN_CHUNK:] * decay, 0.0))
            pws.append(-a)
            ts.append(eye - a)
        pws = [_mm(pw, _pair_diag(pw, lo)) for pw in pws]
        for _ in range(4):
            rs = [_mm(jnp.concatenate([pw, t], axis=0), _pair_diag(pw, lo)) for pw, t in zip(pws, ts)]
            pws = [r[:DN_CHUNK] for r in rs]
            ts = [t + r[DN_CHUNK:] for t, r in zip(ts, rs)]
        rs = [_mm(t, _pair_diag(pw, lo)) for pw, t in zip(pws, ts)]
        ts = [t + r for t, r in zip(ts, rs)]
        sols = [_mm(t, jnp.concatenate([_pair_diag(sl("vb", b, p), lo), _pair_diag(sl("wr", b, p), lo)],
                                       axis=1)) for (b, p), t in zip(probs, ts)]
        qkuws = [_mm(qk, jnp.concatenate([_pair_diag(s[:, :PAIR], lo), _pair_diag(s[:, PAIR:], lo)], axis=1))
                 for qk, s in zip(qks, sols)]
        crosses, gls = [], []
        for (b, p), s in zip(probs, sols):
            gam_last = pre[b]["gam"][r1 - 1:r1, p * PAIR:(p + 1) * PAIR]
            kd = sl("k", b, p) * jnp.exp(gam_last - sl("gam", b, p))
            crosses.append(_mm_tn(kd, s))
            gls.append(jnp.exp(gam_last))
        lhs = [jnp.concatenate([pick(cr[:, PAIR:]), sl("qg", b, p) - qkuw[:, PAIR:]], axis=0)
               for (b, p), cr, qkuw in zip(probs, crosses, qkuws)]
        s_olds = [s_scr[b, p] for b, p in probs]
        rs = [_mm(l, _pair_diag(s_old, lo)) for l, s_old in zip(lhs, s_olds)]
        o_pairs = [[] for _ in range(nb)]
        for (b, p), r, s_old, gl, cr, qkuw in zip(probs, rs, s_olds, gls, crosses, qkuws):
            s_scr[b, p] = gl * s_old - r[:DN_DK] + pick(cr[:, :PAIR])
            o_pairs[b].append(r[DN_DK:] + qkuw[:, :PAIR])
        for b in range(nb):
            o_rows[b].append(jnp.concatenate(o_pairs[b], axis=1))

    o_all = jnp.concatenate([jnp.concatenate(rows, axis=0) for rows in o_rows], axis=0)
    inv_rms = lax.rsqrt(_head_sums(o_all * o_all, hsum) * (1.0 / DN_DV) + EPS)
    for b in range(nb):
        rows = slice(b * GDN_TB, (b + 1) * GDN_TB)
        o_ref[b] = (o_all[rows] * inv_rms[rows] * dnx_ref[...] * _silu_tanh(dz_ref[b])).astype(o_ref.dtype)

    @pl.when(i == pl.num_programs(0) - 1)
    def _():
        for b in range(nb):
            for p in range(N_PAIRS):
                s_p = s_scr[b, p]
                s_out_ref[b, 2 * p] = s_p[:, :DN_DV]
                s_out_ref[b, 2 * p + 1] = s_p[:, DN_DV:]


def _gdn_consts():
    lane = np.arange(DN_WIDTH)
    pl_lane = np.arange(PAIR)
    hsum = (pl_lane[:, None] // DN_DV == pl_lane[None, :] // DN_DV)
    src = np.arange(LANES)
    expb = (src[:, None] == lane[None, :] // DN_DV)
    expg = (src[:, None] == DN_HEADS + lane[None, :] // DN_DV)
    tok = np.arange(GDN_TB)
    ltri = np.logical_and(tok[:, None] >= tok[None, :],
                          tok[:, None] // DN_CHUNK == tok[None, :] // DN_CHUNK)
    as_bf16 = lambda m: jnp.asarray(m.astype(np.float32), dtype=BF16)
    return as_bf16(hsum), as_bf16(expb), as_bf16(expg), as_bf16(ltri)


def _gdn_prompt(xc, dz, ba, alog, dtb, dnx, batch, seq):
    nt = seq // GDN_TB
    hsum, expb, expg, ltri = _gdn_consts()
    row = lambda n: pl.BlockSpec((batch, GDN_TB, n), lambda i: (0, i, 0))
    full = lambda a: pl.BlockSpec(a.shape, lambda i: (0,) * a.ndim)
    consts = (alog, dtb, dnx, hsum, expb, expg, ltri)
    as3d = lambda a: a.reshape(batch, seq, a.shape[-1])
    o, s = pl.pallas_call(
        _gdn_prompt_kernel,
        grid=(nt,),
        in_specs=[row(CONV_CH), row(DN_WIDTH), row(LANES)] + [full(a) for a in consts],
        out_specs=[row(DN_WIDTH),
                   pl.BlockSpec((batch, DN_HEADS, DN_DK, DN_DV), lambda i: (0, 0, 0, 0))],
        out_shape=[jax.ShapeDtypeStruct((batch, seq, DN_WIDTH), BF16),
                   jax.ShapeDtypeStruct((batch, DN_HEADS, DN_DK, DN_DV), F32)],
        scratch_shapes=[pltpu.VMEM((batch, N_PAIRS, DN_DK, PAIR), F32)],
        compiler_params=_params("arbitrary"),
        name="gdn_prompt",
    )(as3d(xc), as3d(dz), as3d(ba), *consts)
    return o.reshape(batch * seq, DN_WIDTH), s


GDN_S_BB = 8


def _gdn_sample_kernel(xc_ref, dz_ref, ba_ref, sc_ref, s_ref, cw_ref, alog_ref, dtb_ref, dn_ref,
                       hsum_ref, eye_ref, hsel_ref, hrep3_ref, o_ref, s_out_ref):
    xc = xc_ref[...]
    y = sc_ref[0] * cw_ref[0:1, :]
    y = y + sc_ref[1] * cw_ref[1:2, :]
    y = y + sc_ref[2] * cw_ref[2:3, :]
    y = _silu(y + xc * cw_ref[3:4, :])
    hsum = hsum_ref[...]
    q = y[:, :DN_WIDTH]
    k = y[:, DN_WIDTH:2 * DN_WIDTH]
    v = y[:, 2 * DN_WIDTH:]
    q = q * lax.rsqrt(_mm_sel_rhs(q * q, hsum) + EPS) * (DN_DK ** -0.5)
    k = k * lax.rsqrt(_mm_sel_rhs(k * k, hsum) + EPS)
    beta_c, g_c = _gdn_gates(ba_ref[...], alog_ref[...], dtb_ref[...])
    eg_c = jnp.exp(g_c)
    eye = eye_ref[...]
    tr = lambda a: lax.dot_general(a, eye, (((0,), (0,)), ((), ())), precision=lax.Precision.HIGHEST,
                                   preferred_element_type=F32)
    gates_t = tr(jnp.concatenate([beta_c, eg_c], axis=1))
    beta_t = gates_t[:LANES]
    eg_t = gates_t[LANES:]
    dz = dz_ref[...]
    dn = dn_ref[...]
    split = lambda r: jnp.concatenate([r[:, h * DN_DV:(h + 1) * DN_DV] for h in range(DN_HEADS)], axis=0)
    own_head = hsel_ref[...].astype(F32)
    hrep3 = hrep3_ref[...]
    seqs = range(GDN_S_BB)
    dot = lambda a, b: jnp.dot(a.astype(BF16), b.astype(BF16), preferred_element_type=F32)

    def pieces(x):
        p1 = x.astype(BF16).astype(F32)
        r1 = x - p1
        p2 = r1.astype(BF16).astype(F32)
        return p1, p2, (r1 - p2).astype(BF16).astype(F32)

    heads = DN_HEADS
    k_pieces, kqs = [], []
    for b in seqs:
        kq_bd = jnp.concatenate([own_head * k[b:b + 1, :], own_head * q[b:b + 1, :]], axis=0)
        a1, a2, a3 = pieces(kq_bd)
        s1, s2, s3 = pieces(s_ref[b])
        r1 = dot(jnp.concatenate([a1, a2, a3], axis=0), s1)
        r2 = dot(jnp.concatenate([a1, a2], axis=0), s2)
        r3 = dot(a1, s3)
        n = 2 * heads
        kqs.append(((r3 + r2[n:] + r1[2 * n:]) + (r2[:n] + r1[n:2 * n])) + r1[:n])
        k_pieces.append((a1[:heads], a2[:heads], a3[:heads]))
    egs = [eg_t[DN_HEADS:2 * DN_HEADS, b:b + 1] for b in seqs]
    qks = [jnp.sum(split(q[b:b + 1, :]) * split(k[b:b + 1, :]), axis=-1, keepdims=True) for b in seqs]
    v_news = [beta_t[0:DN_HEADS, b:b + 1] * (split(v[b:b + 1, :]) - eg * kq[:heads])
              for b, eg, kq in zip(seqs, egs, kqs)]
    os_ = [eg * kq[heads:] + qk * v_new for eg, kq, qk, v_new in zip(egs, kqs, qks, v_news)]
    inv_rms = [lax.rsqrt(jnp.mean(o * o, axis=-1, keepdims=True) + EPS) for o in os_]
    for b, o, r in zip(seqs, os_, inv_rms):
        o_ref[b] = o * r * dn * _silu(split(dz[b:b + 1, :]))
    outers, egrows = [], []
    for (k1, k2, k3), v_new, eg in zip(k_pieces, v_news, egs):
        v1, v2, v3 = pieces(v_new)
        lhs = jnp.concatenate([k1, k1, k2, k1, k2, k3], axis=0).astype(BF16)
        rhs = jnp.concatenate([v1, v2, v1, v3, v2, v1], axis=0).astype(BF16)
        outers.append(lax.dot_general(lhs, rhs, (((0,), (0,)), ((), ())), preferred_element_type=F32))
        egrows.append(dot(hrep3, jnp.concatenate(pieces(jnp.broadcast_to(eg, (DN_HEADS, DN_DV))), axis=0)))
    for b, outer, egrow in zip(seqs, outers, egrows):
        s_out_ref[b] = s_ref[b] * egrow + outer


def _gdn_sample(xc, dz, ba, sconv_t, state, conv_w, alog, dtb, dn):
    nseq = xc.shape[0]
    lane = np.arange(DN_WIDTH)
    hsum = jnp.asarray((lane[:, None] // DN_DV == lane[None, :] // DN_DV).astype(np.float32), dtype=BF16)
    eye = jnp.eye(GDN_S_BB, dtype=F32)
    hsel_np = (np.arange(DN_HEADS)[:, None] == lane[None, :] // DN_DK).astype(np.float32)
    hsel = jnp.asarray(hsel_np, dtype=BF16)
    hrep3 = jnp.asarray(np.tile(hsel_np.T, (1, 3)), dtype=BF16)
    row = lambda n: pl.BlockSpec((GDN_S_BB, n), lambda i: (i, 0))
    full = lambda a: pl.BlockSpec(a.shape, lambda i: (0,) * a.ndim)
    st = pl.BlockSpec((GDN_S_BB, DN_HEADS * DN_DK, DN_DV), lambda i: (i, 0, 0))
    consts = (conv_w, alog, dtb, dn, hsum, eye, hsel, hrep3)
    return pl.pallas_call(
        _gdn_sample_kernel,
        grid=(nseq // GDN_S_BB,),
        in_specs=[row(CONV_CH), row(DN_WIDTH), row(LANES),
                  pl.BlockSpec((CONV_WIDTH - 1, GDN_S_BB, CONV_CH), lambda i: (0, i, 0)), st]
                 + [full(a) for a in consts],
        out_specs=[pl.BlockSpec((GDN_S_BB, DN_HEADS, DN_DV), lambda i: (i, 0, 0)), st],
        out_shape=[jax.ShapeDtypeStruct((nseq, DN_HEADS, DN_DV), F32),
                   jax.ShapeDtypeStruct(state.shape, F32)],
        compiler_params=_params("parallel"),
        name="gdn_sample",
    )(xc, dz, ba, sconv_t, state, *consts)


def _attn_sample_lanes_kernel(att_ref, ck_ref, cv_ref, bucket_ref, rb_ref, sink_ref, o_ref, s_scr):
    g = pl.program_id(0)
    nseq = att_ref.shape[0]
    rnd = lambda a: a.astype(BF16).astype(F32)
    att = att_ref[...]
    q_all_t = (att[:, :ATT_WIDTH] * (HEAD_DIM ** -0.5)).T
    kv_new_t = att[:, ATT_WIDTH:].T
    qsel = [jnp.where(g == 0, q_all_t[hh * HEAD_DIM:(hh + 1) * HEAD_DIM],
                      q_all_t[(GQA + hh) * HEAD_DIM:(GQA + hh + 1) * HEAD_DIM]) for hh in range(GQA)]
    qr = [rnd(q) for q in qsel]
    kn = rnd(jnp.where(g == 0, kv_new_t[0:HEAD_DIM], kv_new_t[HEAD_DIM:2 * HEAD_DIM]))
    vn = rnd(jnp.where(g == 0, kv_new_t[2 * HEAD_DIM:3 * HEAD_DIM], kv_new_t[3 * HEAD_DIM:]))

    def score_row(j, carry):
        kj = rnd(ck_ref[j, 0])
        for hh in range(GQA):
            s_scr[hh, pl.ds(j, 1), :] = jnp.sum(qr[hh] * kj, axis=0, keepdims=True)
        return carry
    lax.fori_loop(0, WINDOW, score_row, 0, unroll=2)

    bucket = bucket_ref[...]
    jrow = lax.broadcasted_iota(jnp.int32, (WINDOW, nseq), 0)
    prn = []
    for hh in range(GQA):
        h = g * GQA + hh
        bias = jnp.where(jrow >= 1, _bias_lookup(bucket, rb_ref, h), NEG_INF)
        s = s_scr[hh] + bias
        s_n = jnp.sum(qr[hh] * kn, axis=0, keepdims=True) + rb_ref[0, h]
        sink = sink_ref[h]
        m = jnp.maximum(jnp.maximum(jnp.max(s, axis=0, keepdims=True), s_n), sink)
        p = jnp.exp(s - m)
        p_n = jnp.exp(s_n - m)
        den = jnp.sum(p, axis=0, keepdims=True) + p_n + jnp.exp(sink - m)
        s_scr[hh] = rnd(p / den)
        prn.append(rnd(p_n / den))

    def value_row(j, acc):
        vj = rnd(cv_ref[j, 0])
        return tuple(acc[hh] + s_scr[hh, pl.ds(j, 1), :] * vj for hh in range(GQA))
    zero = jnp.zeros((HEAD_DIM, nseq), F32)
    acc = lax.fori_loop(0, WINDOW, value_row, (zero,) * GQA, unroll=2)
    for hh in range(GQA):
        o_ref[hh * HEAD_DIM:(hh + 1) * HEAD_DIM, :] = acc[hh] + prn[hh] * vn


def _attn_sample_lanes(att, ck_t, cv_t, rel_bias, sink):
    nseq = att.shape[0]
    assert nseq == LANES
    bucket = jnp.asarray(np.broadcast_to(_t5_bucket_np(WINDOW - np.arange(WINDOW))[:, None], (WINDOW, nseq)))
    smem = pl.BlockSpec(memory_space=pltpu.SMEM)
    cache = pl.BlockSpec((WINDOW, 1, HEAD_DIM, nseq), lambda g: (0, g, 0, 0))
    full = lambda a: pl.BlockSpec(a.shape, lambda g: (0,) * a.ndim)
    return pl.pallas_call(
        _attn_sample_lanes_kernel,
        grid=(ATT_KV_HEADS,),
        in_specs=[full(att), cache, cache, full(bucket), smem, smem],
        out_specs=pl.BlockSpec((GQA * HEAD_DIM, nseq), lambda g: (g, 0)),
        out_shape=jax.ShapeDtypeStruct((ATT_WIDTH, nseq), F32),
        scratch_shapes=[pltpu.VMEM((GQA, WINDOW, nseq), F32)],
        compiler_params=_params("arbitrary"),
        name="attn_sample_lanes",
    )(att, ck_t, cv_t, bucket, rel_bias, sink)


def _gdn_sample_front_kernel(xc_ref, dz_ref, ba_ref, sc_ref, cw_ref, alog_ref, dtb_ref, hsum_ref,
                             q_ref, k_ref, v_ref, dz_t_ref, gates_ref):
    xc = xc_ref[...]
    y = sc_ref[0] * cw_ref[0:1, :]
    y = y + sc_ref[1] * cw_ref[1:2, :]
    y = y + sc_ref[2] * cw_ref[2:3, :]
    y = _silu(y + xc * cw_ref[3:4, :])
    hsum = hsum_ref[...]
    q = y[:, :DN_WIDTH]
    k = y[:, DN_WIDTH:2 * DN_WIDTH]
    q = q * lax.rsqrt(_mm_sel_rhs(q * q, hsum) + EPS) * (DN_DK ** -0.5)
    k = k * lax.rsqrt(_mm_sel_rhs(k * k, hsum) + EPS)
    beta_c, g_c = _gdn_gates(ba_ref[...], alog_ref[...], dtb_ref[...])
    q_ref[...] = q.T
    k_ref[...] = k.T
    v_ref[...] = y[:, 2 * DN_WIDTH:].T
    dz_t_ref[...] = dz_ref[...].T
    gates_ref[0:LANES, :] = beta_c.T
    gates_ref[LANES:, :] = jnp.exp(g_c).T


def _gdn_sample_step_kernel(q_ref, k_ref, v_ref, dz_ref, gates_ref, dn_ref, s_ref, o_ref, s_out_ref):
    h = pl.program_id(0)
    beta = gates_ref[pl.ds(h, 1), :]
    eg = gates_ref[pl.ds(LANES + DN_HEADS + h, 1), :]
    q, k, v = q_ref[...], k_ref[...], v_ref[...]
    w = (k * beta) * eg
    qg = q * eg
    ws = jnp.zeros(v.shape, F32)
    qs = jnp.zeros(v.shape, F32)
    for dk in range(DN_DK):
        s_dk = s_ref[0, dk]
        ws = ws + w[dk:dk + 1, :] * s_dk
        qs = qs + qg[dk:dk + 1, :] * s_dk
    v_new = v * beta - ws
    qk = jnp.sum(q * k, axis=0, keepdims=True)
    o = qs + qk * v_new
    for dk in range(DN_DK):
        s_out_ref[0, dk] = s_ref[0, dk] * eg + k[dk:dk + 1, :] * v_new
    o = o * lax.rsqrt(jnp.mean(o * o, axis=0, keepdims=True) + EPS) * dn_ref[...]
    o_ref[...] = o * _silu(dz_ref[...])


def _gdn_sample_lanes(xc, dz, ba, sconv_t, state_t, conv_w, alog, dtb, dn):
    nseq = xc.shape[0]
    assert nseq == LANES
    lane = np.arange(DN_WIDTH)
    hsum = jnp.asarray((lane[:, None] // DN_DV == lane[None, :] // DN_DV).astype(np.float32), dtype=BF16)
    full = lambda a: pl.BlockSpec(a.shape, lambda i: (0,) * a.ndim)
    cm = jax.ShapeDtypeStruct((DN_WIDTH, nseq), F32)
    front_in = (xc, dz, ba, sconv_t, conv_w, alog, dtb, hsum)
    q_t, k_t, v_t, dz_t, gates_t = pl.pallas_call(
        _gdn_sample_front_kernel,
        grid=(1,),
        in_specs=[full(a) for a in front_in],
        out_specs=[pl.BlockSpec((DN_WIDTH, nseq), lambda i: (0, 0))] * 4
                  + [pl.BlockSpec((2 * LANES, nseq), lambda i: (0, 0))],
        out_shape=[cm, cm, cm, cm, jax.ShapeDtypeStruct((2 * LANES, nseq), F32)],
        compiler_params=_params("arbitrary"),
        name="gdn_sample_front",
    )(*front_in)
    dn_b = jnp.broadcast_to(dn.reshape(DN_DV, 1), (DN_DV, nseq))
    head = pl.BlockSpec((DN_DK, nseq), lambda h: (h, 0))
    st = pl.BlockSpec((1, DN_DK, DN_DV, nseq), lambda h: (h, 0, 0, 0))
    return pl.pallas_call(
        _gdn_sample_step_kernel,
        grid=(DN_HEADS,),
        in_specs=[head, head, head, head, full(gates_t), full(dn_b), st],
        out_specs=[head, st],
        out_shape=[cm, jax.ShapeDtypeStruct(state_t.shape, F32)],
        compiler_params=_params("parallel"),
        name="gdn_sample_step",
    )(q_t, k_t, v_t, dz_t, gates_t, dn_b, state_t)


def _route(xn, wr):
    logits = jnp.dot(xn, wr, preferred_element_type=F32)
    lane = lax.broadcasted_iota(jnp.int32, logits.shape, 1).astype(F32)
    first_at = lambda hit: jnp.min(jnp.where(hit, lane, float(LANES)), axis=-1, keepdims=True)
    glog = jnp.where(lane < N_GROUPS, logits, NEG_INF)
    gmax = jnp.max(glog, axis=-1, keepdims=True)
    gsel = first_at(glog == gmax)
    pgsel = 1.0 / jnp.sum(jnp.exp(glog - gmax), axis=-1, keepdims=True)
    lo = ROUTER_OFF + gsel * EXPERTS_PER_GROUP
    in_group = jnp.logical_and(lane >= lo, lane < lo + EXPERTS_PER_GROUP)
    elog = jnp.where(in_group, logits, NEG_INF)
    m1 = jnp.max(elog, axis=-1, keepdims=True)
    i1 = first_at(elog == m1)
    z = jnp.sum(jnp.exp(elog - m1), axis=-1, keepdims=True)
    elog2 = jnp.where(lane == i1, NEG_INF, elog)
    m2 = jnp.max(elog2, axis=-1, keepdims=True)
    i2 = first_at(elog2 == m2)
    p1 = 1.0 / z
    p2 = jnp.exp(m2 - m1) / z
    tot = p1 + p2
    return lane, i1, i2, p1 / tot * pgsel, p2 / tot * pgsel


def _outproj(x_ref, oa_ref, od_ref, wo_ref):
    return x_ref[...] + _mm(oa_ref[...], wo_ref[:ATT_WIDTH, :]) + _mm(od_ref[...], wo_ref[ATT_WIDTH:, :])


def _outproj_router_kernel(x_ref, oa_ref, od_t_ref, wo_ref, g_ref, wr_ref, h_ref, xn_ref, gate_ref):
    h = (x_ref[...] + _mm(oa_ref[...], wo_ref[:ATT_WIDTH, :])
         + _mm(od_t_ref[...].T, wo_ref[ATT_WIDTH:, :]))
    h_ref[...] = h
    xn = _rmsnorm(h, g_ref[...]).astype(BF16)
    xn_ref[...] = xn
    lane, i1, i2, g1, g2 = _route(xn, wr_ref[...])
    gate_ref[...] = jnp.where(lane == i1, g1, 0.0) + jnp.where(lane == i2, g2, 0.0)


def _outproj_router(x, oa, od_t, wo, g, wr):
    t = x.shape[0]
    tm = t
    row = lambda n: pl.BlockSpec((tm, n), lambda i: (i, 0))
    full = lambda a: pl.BlockSpec(a.shape, lambda i: (0,) * a.ndim)
    return pl.pallas_call(
        _outproj_router_kernel,
        grid=(t // tm,),
        in_specs=[row(D_MODEL), row(ATT_WIDTH), full(od_t), full(wo), full(g), full(wr)],
        out_specs=[row(D_MODEL), row(D_MODEL), row(LANES)],
        out_shape=[jax.ShapeDtypeStruct((t, D_MODEL), F32), jax.ShapeDtypeStruct((t, D_MODEL), BF16),
                   jax.ShapeDtypeStruct((t, LANES), F32)],
        compiler_params=_params("parallel"),
        name="outproj_router",
    )(x, oa, od_t, wo, g, wr)


def _moe_kernel(xn_ref, gate_ref, wg_ref, wu_ref, wd_ref, o_ref):
    e = pl.program_id(1)
    xn = xn_ref[...]
    lane = lax.broadcasted_iota(jnp.int32, gate_ref.shape, 1)
    gate = jnp.sum(jnp.where(lane == e + ROUTER_OFF, gate_ref[...], 0.0), axis=-1, keepdims=True)
    hg = jnp.dot(xn, wg_ref[...].astype(BF16), preferred_element_type=F32)
    hu = jnp.dot(xn, wu_ref[...].astype(BF16), preferred_element_type=F32)
    hm = _silu(hg) * hu * gate
    y = jnp.dot(hm.astype(BF16), wd_ref[...].astype(BF16), preferred_element_type=F32)

    @pl.when(e == 0)
    def _():
        o_ref[...] = y

    @pl.when(e > 0)
    def _():
        o_ref[...] += y


def _moe(xn, gates, wg, wu, wd):
    t = xn.shape[0]
    tm = min(t, 1024)
    return pl.pallas_call(
        _moe_kernel,
        grid=(t // tm, N_EXPERTS),
        in_specs=[pl.BlockSpec((tm, D_MODEL), lambda i, e: (i, 0)),
                  pl.BlockSpec((tm, LANES), lambda i, e: (i, 0)),
                  pl.BlockSpec((None, D_MODEL, D_EXPERT), lambda i, e: (e, 0, 0)),
                  pl.BlockSpec((None, D_MODEL, D_EXPERT), lambda i, e: (e, 0, 0)),
                  pl.BlockSpec((None, D_EXPERT, D_MODEL), lambda i, e: (e, 0, 0))],
        out_specs=pl.BlockSpec((tm, D_MODEL), lambda i, e: (i, 0)),
        out_shape=jax.ShapeDtypeStruct((t, D_MODEL), F32),
        compiler_params=_params("parallel", "arbitrary"),
        name="moe",
    )(xn, gates, wg, wu, wd)


MOE_TM = 512
POS_TM = 1024
INFO_G1, INFO_G2, INFO_E1, INFO_E2 = 0, 1, 2, 3
DMA_UNROLL = 8


def _moe_tiles(t):
    return (2 * t) // MOE_TM + N_EXPERTS


HALF = D_MODEL // 2
U32 = jnp.uint32


def _pack_rows(x):
    bits = lambda v: lax.bitcast_convert_type(v.astype(BF16).astype(F32), U32)
    return bits(x[:, HALF:]) | (bits(x[:, :HALF]) >> 16)


def _unpack_rows(w):
    lo = lax.bitcast_convert_type(w << 16, F32)
    hi = lax.bitcast_convert_type(w & jnp.uint32(0xFFFF0000), F32)
    return lo, hi


def _route_kernel(x_ref, oa_ref, od_ref, wo_ref, g_ref, wr_ref, h_ref, xn_ref, info_ref, cnt_ref, run_scr):
    h = _outproj(x_ref, oa_ref, od_ref, wo_ref)
    h_ref[...] = h
    xn = _rmsnorm(h, g_ref[...])
    xn_ref[...] = _pack_rows(xn)
    lane, i1, i2, g1, g2 = _route(xn.astype(BF16), wr_ref[...])
    info = jnp.where(lane == INFO_G1, g1, 0.0) + jnp.where(lane == INFO_G2, g2, 0.0)
    info = info + jnp.where(lane == INFO_E1, i1, 0.0) + jnp.where(lane == INFO_E2, i2, 0.0)
    info_ref[...] = info

    @pl.when(pl.program_id(0) == 0)
    def _():
        run_scr[...] = jnp.zeros(run_scr.shape, F32)
    picked = jnp.logical_or(lane == i1, lane == i2).astype(F32)
    run_scr[...] += jnp.sum(picked, axis=0, keepdims=True)
    cnt_ref[...] = run_scr[...]


def _route_sparse(x, oa, od, wo, g, wr):
    t = x.shape[0]
    tm = ROW_TM
    row = lambda n: pl.BlockSpec((tm, n), lambda i: (i, 0))
    full = lambda a: pl.BlockSpec(a.shape, lambda i: (0,) * a.ndim)
    return pl.pallas_call(
        _route_kernel,
        grid=(t // tm,),
        in_specs=[row(D_MODEL), row(ATT_WIDTH), row(DN_WIDTH), full(wo), full(g), full(wr)],
        out_specs=[row(D_MODEL), row(HALF), row(LANES), pl.BlockSpec((1, LANES), lambda i: (0, 0))],
        out_shape=[jax.ShapeDtypeStruct((t, D_MODEL), F32), jax.ShapeDtypeStruct((t, HALF), U32),
                   jax.ShapeDtypeStruct((t, LANES), F32), jax.ShapeDtypeStruct((1, LANES), F32)],
        scratch_shapes=[pltpu.VMEM((1, LANES), F32)],
        compiler_params=_params("arbitrary"),
        name="route",
    )(x, oa, od, wo, g, wr)


def _positions_kernel(info_ref, cnt_ref, ltri_ref, utri_ref, pos_ref, run_scr, off_scr):
    info = info_ref[...]
    lane = lax.broadcasted_iota(jnp.int32, info.shape, 1).astype(F32)
    hit1 = lane == info[:, INFO_E1:INFO_E1 + 1]
    hit2 = lane == info[:, INFO_E2:INFO_E2 + 1]
    onehot = jnp.logical_or(hit1, hit2).astype(F32)

    @pl.when(pl.program_id(0) == 0)
    def _():
        ln = lax.broadcasted_iota(jnp.int32, cnt_ref.shape, 1)
        is_expert = jnp.logical_and(ln >= ROUTER_OFF, ln < ROUTER_OFF + N_EXPERTS)
        tiles = jnp.where(is_expert, jnp.maximum(jnp.floor((cnt_ref[...] + (MOE_TM - 1)) * (1.0 / MOE_TM)), 1.0), 0.0)
        off_scr[...] = MOE_TM * jnp.dot(tiles.astype(BF16), utri_ref[...], preferred_element_type=F32)
        run_scr[...] = jnp.zeros(run_scr.shape, F32)

    before = (jnp.dot(ltri_ref[...], onehot.astype(BF16), preferred_element_type=F32)
              + run_scr[...] + off_scr[...])
    pos1 = jnp.sum(jnp.where(hit1, before, 0.0), axis=-1, keepdims=True)
    pos2 = jnp.sum(jnp.where(hit2, before, 0.0), axis=-1, keepdims=True)
    pos_ref[...] = (jnp.where(lane == 0, pos1, 0.0) + jnp.where(lane == 1, pos2, 0.0)).astype(jnp.int32)
    run_scr[...] += jnp.sum(onehot, axis=0, keepdims=True)


def _positions(info, cnt):
    t = info.shape[0]
    tm = min(t, POS_TM)
    tok = np.arange(tm)
    ltri = jnp.asarray((tok[:, None] > tok[None, :]).astype(np.float32), dtype=BF16)
    ln = np.arange(LANES)
    utri = jnp.asarray((ln[:, None] < ln[None, :]).astype(np.float32), dtype=BF16)
    full = lambda a: pl.BlockSpec(a.shape, lambda i: (0,) * a.ndim)
    return pl.pallas_call(
        _positions_kernel,
        grid=(t // tm,),
        in_specs=[pl.BlockSpec((tm, LANES), lambda i: (i, 0)), full(cnt), full(ltri), full(utri)],
        out_specs=pl.BlockSpec((tm, LANES), lambda i: (i, 0)),
        out_shape=jax.ShapeDtypeStruct((t, LANES), jnp.int32),
        scratch_shapes=[pltpu.VMEM((1, LANES), F32), pltpu.VMEM((1, LANES), F32)],
        compiler_params=_params("arbitrary"),
        name="positions",
    )(info, cnt, ltri, utri)


def _row_copy(src_hbm, src_row, dst_hbm, dst_row, sem):
    return pltpu.make_async_copy(src_hbm.at[pl.ds(src_row, 1)], dst_hbm.at[pl.ds(dst_row, 1)], sem)


SCATTER_SLOTS = 3


def _scatter_kernel(pos1_ref, pos2_ref, last_ref, used_ref, nt_ref, xn_hbm, zero_hbm, xs_hbm,
                    buf, lsem, sem, zsem, *, n_tok):
    max_tiles = xs_hbm.shape[0] // MOE_TM

    def zero_tile(tile):
        return pltpu.make_async_copy(zero_hbm, xs_hbm.at[pl.ds(tile * MOE_TM, MOE_TM)], zsem)

    def for_unused(fn):
        def body(tile, carry):
            fn(tile)
            return carry
        lax.fori_loop(nt_ref[0], max_tiles, body, 0)

    for e in range(N_EXPERTS):
        @pl.when(used_ref[e] > 0)
        def _():
            zero_tile(last_ref[e]).start()
    for_unused(lambda tile: zero_tile(tile).start())
    for e in range(N_EXPERTS):
        @pl.when(used_ref[e] > 0)
        def _():
            zero_tile(last_ref[e]).wait()
    for_unused(lambda tile: zero_tile(tile).wait())

    tm = buf.shape[1]
    n = n_tok // tm

    def load(i):
        return pltpu.make_async_copy(xn_hbm.at[pl.ds(i * tm, tm)], buf.at[i % SCATTER_SLOTS],
                                     lsem.at[i % SCATTER_SLOTS])

    def wait_rows(slot):
        pltpu.make_async_copy(xs_hbm.at[pl.ds(0, 2 * tm)], xs_hbm.at[pl.ds(0, 2 * tm)], sem.at[slot]).wait()

    load(0).start()
    load(1).start()

    def step(i, carry):
        slot = i % SCATTER_SLOTS
        load(i).wait()

        def body(j, c2):
            tok = i * tm + j
            src = buf.at[slot, pl.ds(j, 1)]
            pltpu.make_async_copy(src, xs_hbm.at[pl.ds(pos1_ref[tok], 1)], sem.at[slot]).start()
            pltpu.make_async_copy(src, xs_hbm.at[pl.ds(pos2_ref[tok], 1)], sem.at[slot]).start()
            return c2
        lax.fori_loop(0, tm, body, 0, unroll=DMA_UNROLL)

        @pl.when(i >= 1)
        def _():
            wait_rows((i - 1) % SCATTER_SLOTS)

        @pl.when(i + 2 < n)
        def _():
            load(i + 2).start()
        return carry
    lax.fori_loop(0, n, step, 0)
    wait_rows((n - 1) % SCATTER_SLOTS)


def _scatter_rows(xn, pos1, pos2, last_tile, used, n_tiles, n_rows):
    t = xn.shape[0]
    zero = jnp.zeros((MOE_TM, D_MODEL), F32)
    any_spec = pl.BlockSpec(memory_space=pl.ANY)
    return pl.pallas_call(
        functools.partial(_scatter_kernel, n_tok=t),
        grid_spec=pltpu.PrefetchScalarGridSpec(
            num_scalar_prefetch=5, grid=(1,),
            in_specs=[any_spec, any_spec], out_specs=any_spec,
            scratch_shapes=[pltpu.VMEM((SCATTER_SLOTS, MOE_TM, D_MODEL), F32),
                            pltpu.SemaphoreType.DMA((SCATTER_SLOTS,)),
                            pltpu.SemaphoreType.DMA((SCATTER_SLOTS,)),
                            pltpu.SemaphoreType.DMA]),
        out_shape=jax.ShapeDtypeStruct((n_rows, D_MODEL), F32),
        compiler_params=_params("arbitrary"),
        name="scatter_rows",
    )(pos1, pos2, last_tile, used, n_tiles, xn, zero)


def _experts_kernel(te_ref, tv_ref, nt_ref, xs_ref, wg_hbm, wu_hbm, wd_hbm, xn_new_ref, gate_new_ref,
                    ys_ref, moe_new_ref, wg_s, wu_s, wd_s, wg_f, wu_f, wd_f, wsem):
    i = pl.program_id(0)
    used = i < nt_ref[0]
    expert = te_ref[i]

    def fetch(e):
        slot = e % 2
        return [pltpu.make_async_copy(src.at[e], dst.at[slot], wsem.at[slot, j])
                for j, (src, dst) in enumerate(((wg_hbm, wg_f), (wu_hbm, wu_f), (wd_hbm, wd_f)))]

    @pl.when(jnp.logical_or(i == 0, expert != te_ref[jnp.maximum(i - 1, 0)]))
    def _():
        @pl.when(i == 0)
        def _():
            for c in fetch(expert):
                c.start()
        for c in fetch(expert):
            c.wait()

        @pl.when(expert + 1 < N_EXPERTS)
        def _():
            for c in fetch(expert + 1):
                c.start()
        slot = expert % 2
        wg_s[...] = wg_f[slot].astype(BF16)
        wu_s[...] = wu_f[slot].astype(BF16)
        wd_s[...] = wd_f[slot].astype(BF16)
        xn = xn_new_ref[...]
        lane = lax.broadcasted_iota(jnp.int32, gate_new_ref.shape, 1)
        gate = jnp.sum(jnp.where(lane == expert + ROUTER_OFF, gate_new_ref[...], 0.0), axis=-1, keepdims=True)
        hg = jnp.dot(xn, wg_s[...], preferred_element_type=F32)
        hu = jnp.dot(xn, wu_s[...], preferred_element_type=F32)
        hm = _silu(hg) * hu * gate
        y = jnp.dot(hm.astype(BF16), wd_s[...], preferred_element_type=F32)

        @pl.when(i == 0)
        def _():
            moe_new_ref[...] = y

        @pl.when(i > 0)
        def _():
            moe_new_ref[...] += y

    @pl.when(used)
    def _():
        row = lax.broadcasted_iota(jnp.int32, xs_ref.shape, 0)
        x_lo, x_hi = _unpack_rows(jnp.where(row < tv_ref[i], xs_ref[...], jnp.uint32(0)))
        x_lo = x_lo.astype(BF16)
        x_hi = x_hi.astype(BF16)
        up = lambda w_s: (jnp.dot(x_lo, w_s[:HALF, :], preferred_element_type=F32)
                          + jnp.dot(x_hi, w_s[HALF:, :], preferred_element_type=F32))
        hm = (_silu_tanh(up(wg_s)) * up(wu_s)).astype(BF16)
        ys_ref[...] = _pack_rows(jnp.dot(hm, wd_s[...], preferred_element_type=F32))

    @pl.when(jnp.logical_not(used))
    def _():
        ys_ref[...] = jnp.zeros(ys_ref.shape, U32)


def _experts(xs, tile_expert, tile_valid, n_tiles, wg, wu, wd, xn_new, gate_new):
    max_tiles = xs.shape[0] // MOE_TM
    rows = pl.BlockSpec((MOE_TM, HALF), lambda i, te, tv, nt: (i, 0))
    hbm = pl.BlockSpec(memory_space=pl.ANY)
    full = lambda a: pl.BlockSpec(a.shape, lambda i, te, tv, nt: (0,) * a.ndim)
    return pl.pallas_call(
        _experts_kernel,
        grid_spec=pltpu.PrefetchScalarGridSpec(
            num_scalar_prefetch=3, grid=(max_tiles,),
            in_specs=[rows, hbm, hbm, hbm, full(xn_new), full(gate_new)],
            out_specs=[rows, pl.BlockSpec(xn_new.shape, lambda i, te, tv, nt: (0, 0))],
            scratch_shapes=[pltpu.VMEM((D_MODEL, D_EXPERT), BF16), pltpu.VMEM((D_MODEL, D_EXPERT), BF16),
                            pltpu.VMEM((D_EXPERT, D_MODEL), BF16),
                            pltpu.VMEM((2, D_MODEL, D_EXPERT), F32), pltpu.VMEM((2, D_MODEL, D_EXPERT), F32),
                            pltpu.VMEM((2, D_EXPERT, D_MODEL), F32), pltpu.SemaphoreType.DMA((2, 3))]),
        out_shape=[jax.ShapeDtypeStruct(xs.shape, U32), jax.ShapeDtypeStruct(xn_new.shape, F32)],
        compiler_params=_params("arbitrary"),
        name="experts",
    )(tile_expert, tile_valid, n_tiles, xs, wg, wu, wd, xn_new, gate_new)


def _ple_gather_kernel(pos1_ref, pos2_ref, h_ref, info_ref, p_ref, wpp_ref, wpg_ref, gp_ref, gf_ref,
                       ys_hbm, y_ref, ybuf, sem):
    i = pl.program_id(0)
    n = pl.num_programs(0)
    tm = h_ref.shape[0]

    def issue(tile, slot):
        def body(j, carry):
            tok = tile * tm + j
            pltpu.make_async_copy(ys_hbm.at[pl.ds(pos1_ref[tok], 1)], ybuf.at[slot, 0, pl.ds(j, 1)],
                                  sem.at[slot]).start()
            pltpu.make_async_copy(ys_hbm.at[pl.ds(pos2_ref[tok], 1)], ybuf.at[slot, 1, pl.ds(j, 1)],
                                  sem.at[slot]).start()
            return carry
        lax.fori_loop(0, tm, body, 0, unroll=DMA_UNROLL)

    @pl.when(i == 0)
    def _():
        issue(0, 0)

    @pl.when(i + 1 < n)
    def _():
        issue(i + 1, (i + 1) % 2)

    slot = i % 2
    pltpu.make_async_copy(ybuf.at[slot], ybuf.at[slot], sem.at[slot]).wait()
    info = info_ref[...]
    moe = info[:, INFO_G1:INFO_G1 + 1] * ybuf[slot, 0] + info[:, INFO_G2:INFO_G2 + 1] * ybuf[slot, 1]
    h = h_ref[...] + moe
    hn = _rmsnorm(h, gp_ref[...])
    h = h + _mm(p_ref[...], wpp_ref[...]) * _sigmoid(_mm(hn, wpg_ref[...]))
    y_ref[...] = _rmsnorm(h, gf_ref[...])


def _ple_gather(h, info, p, ys, pos1, pos2, wpp, wpg, gp, gf):
    t = h.shape[0]
    tm = 256
    row = lambda n: pl.BlockSpec((tm, n), lambda i, p1, p2: (i, 0))
    full = lambda a: pl.BlockSpec(a.shape, lambda i, p1, p2: (0,) * a.ndim)
    return pl.pallas_call(
        _ple_gather_kernel,
        grid_spec=pltpu.PrefetchScalarGridSpec(
            num_scalar_prefetch=2, grid=(t // tm,),
            in_specs=[row(D_MODEL), row(LANES), row(PLE_DIM), full(wpp), full(wpg), full(gp), full(gf),
                      pl.BlockSpec(memory_space=pl.ANY)],
            out_specs=row(D_MODEL),
            scratch_shapes=[pltpu.VMEM((2, 2, tm, D_MODEL), F32), pltpu.SemaphoreType.DMA((2,))]),
        out_shape=jax.ShapeDtypeStruct((t, D_MODEL), F32),
        compiler_params=_params("arbitrary"),
        name="ple_gather",
    )(pos1, pos2, h, info, p, wpp, wpg, gp, gf, ys)


SC_IDX = 128
SC_ROWS = 64
SC_WORKERS = 32


def _sc_mesh():
    return plsc.VectorSubcoreMesh(core_axis_name="c", subcore_axis_name="s")


def _sc_windows(t, fn):
    per_worker = t // SC_WORKERS
    worker = lax.axis_index(("c", "s"))

    @pl.loop(0, per_worker // SC_IDX)
    def _(w):
        fn(worker * per_worker + w * SC_IDX)


def _sc_scatter_rows(xn, pos1, pos2, n_rows):
    t, d = xn.shape
    assert t % (SC_WORKERS * SC_IDX) == 0
    idx_t = pltpu.VMEM((1, SC_IDX), jnp.int32)

    @pl.kernel(out_type=jax.ShapeDtypeStruct((n_rows, d), xn.dtype), mesh=_sc_mesh(),
               scratch_types=[idx_t, idx_t, pltpu.VMEM((SC_ROWS, d), xn.dtype)])
    def scatter(x_hbm, p1_hbm, p2_hbm, o_hbm, i1_v, i2_v, buf):
        def window(base):
            pltpu.sync_copy(p1_hbm.at[:, pl.ds(base, SC_IDX)], i1_v)
            pltpu.sync_copy(p2_hbm.at[:, pl.ds(base, SC_IDX)], i2_v)
            for k in range(SC_IDX // SC_ROWS):
                pltpu.sync_copy(x_hbm.at[pl.ds(base + k * SC_ROWS, SC_ROWS)], buf)
                pltpu.sync_copy(buf, o_hbm.at[i1_v.at[0, pl.ds(k * SC_ROWS, SC_ROWS)]])
                pltpu.sync_copy(buf, o_hbm.at[i2_v.at[0, pl.ds(k * SC_ROWS, SC_ROWS)]])
        _sc_windows(t, window)

    return scatter(xn, pos1.reshape(1, t), pos2.reshape(1, t))


def _sc_gather_rows(ys, pos1, pos2):
    t = pos1.shape[0]
    d = ys.shape[1]
    assert t % (SC_WORKERS * SC_IDX) == 0
    idx_t = pltpu.VMEM((1, SC_IDX), jnp.int32)
    out = jax.ShapeDtypeStruct((t, d), ys.dtype)

    buf_t = pltpu.VMEM((SC_ROWS, d), ys.dtype)

    @pl.kernel(out_type=(out, out), mesh=_sc_mesh(),
               scratch_types=[idx_t, idx_t, buf_t, buf_t, pltpu.SemaphoreType.DMA((2,)),
                              pltpu.SemaphoreType.DMA((2,))])
    def gather(y_hbm, p1_hbm, p2_hbm, o1_hbm, o2_hbm, i1_v, i2_v, buf_a, buf_b, gsem, wsem):
        bufs = (buf_a, buf_b)

        def window(base):
            pltpu.sync_copy(p1_hbm.at[:, pl.ds(base, SC_IDX)], i1_v)
            pltpu.sync_copy(p2_hbm.at[:, pl.ds(base, SC_IDX)], i2_v)
            items = [(idx_v, o_hbm, k) for k in range(SC_IDX // SC_ROWS)
                     for idx_v, o_hbm in ((i1_v, o1_hbm), (i2_v, o2_hbm))]

            def read(n):
                idx_v, _, k = items[n]
                return pltpu.make_async_copy(y_hbm.at[idx_v.at[0, pl.ds(k * SC_ROWS, SC_ROWS)]],
                                             bufs[n % 2], gsem.at[n % 2])

            def write(n):
                _, o_hbm, k = items[n]
                return pltpu.make_async_copy(bufs[n % 2], o_hbm.at[pl.ds(base + k * SC_ROWS, SC_ROWS)],
                                             wsem.at[n % 2])

            read(0).start()
            for n in range(len(items)):
                read(n).wait()
                if n >= 1:
                    write(n - 1).wait()
                if n + 1 < len(items):
                    read(n + 1).start()
                write(n).start()
            write(len(items) - 1).wait()
        _sc_windows(t, window)

    return gather(ys, pos1.reshape(1, t), pos2.reshape(1, t))


def _ple_sparse_kernel(h_ref, info_ref, y1_ref, y2_ref, p_ref, wpp_ref, wpg_ref, gp_ref, gf_ref, y_ref):
    info = info_ref[...]
    g1 = info[:, INFO_G1:INFO_G1 + 1]
    g2 = info[:, INFO_G2:INFO_G2 + 1]
    y1_lo, y1_hi = _unpack_rows(y1_ref[...])
    y2_lo, y2_hi = _unpack_rows(y2_ref[...])
    moe = jnp.concatenate([g1 * y1_lo + g2 * y2_lo, g1 * y1_hi + g2 * y2_hi], axis=1)
    h = h_ref[...] + moe
    hn = _rmsnorm(h, gp_ref[...])
    h = h + _mm(p_ref[...], wpp_ref[...]) * _sigmoid(_mm(hn, wpg_ref[...]))
    y_ref[...] = _rmsnorm(h, gf_ref[...])


def _ple_sparse(h, info, y1, y2, p, wpp, wpg, gp, gf):
    t = h.shape[0]
    tm = ROW_TM
    row = lambda n: pl.BlockSpec((tm, n), lambda i: (i, 0))
    full = lambda a: pl.BlockSpec(a.shape, lambda i: (0,) * a.ndim)
    return pl.pallas_call(
        _ple_sparse_kernel,
        grid=(t // tm,),
        in_specs=[row(D_MODEL), row(LANES), row(HALF), row(HALF), row(PLE_DIM),
                  full(wpp), full(wpg), full(gp), full(gf)],
        out_specs=row(D_MODEL),
        out_shape=jax.ShapeDtypeStruct((t, D_MODEL), F32),
        compiler_params=_params("parallel"),
        name="ple_sparse",
    )(h, info, y1, y2, p, wpp, wpg, gp, gf)


def _tile_tables(cnt, max_tiles):
    tiles_e = jnp.maximum((cnt + (MOE_TM - 1)) // MOE_TM, 1)
    ends = jnp.cumsum(tiles_e)
    n_tiles = ends[-1]
    tile = jnp.arange(max_tiles, dtype=jnp.int32)
    idx = jnp.minimum(tile, n_tiles - 1)
    tile_expert = jnp.sum((idx[:, None] >= ends[None, :]).astype(jnp.int32), axis=1)
    mine = tile_expert[:, None] == jnp.arange(N_EXPERTS, dtype=jnp.int32)[None, :]
    of_mine = lambda v: jnp.sum(jnp.where(mine, v[None, :], 0), axis=1)
    valid = jnp.clip(of_mine(cnt) - (idx - of_mine(ends - tiles_e)) * MOE_TM, 0, MOE_TM)
    tile_valid = jnp.where(tile < n_tiles, valid, 0).astype(jnp.int32)
    return (tile_expert, tile_valid, n_tiles.reshape(1), (ends - 1).astype(jnp.int32),
            tiles_e.astype(jnp.int32))


def _ple_final_kernel(h_ref, m_ref, p_ref, wpp_ref, wpg_ref, gp_ref, gf_ref, y_ref):
    h = h_ref[...] + m_ref[...]
    hn = _rmsnorm(h, gp_ref[...])
    h = h + _mm(p_ref[...], wpp_ref[...]) * _sigmoid(_mm(hn, wpg_ref[...]))
    y_ref[...] = _rmsnorm(h, gf_ref[...])


def _ple_final(h, m, p, wpp, wpg, gp, gf):
    t = h.shape[0]
    tm = min(t, 256)
    row = lambda n: pl.BlockSpec((tm, n), lambda i: (i, 0))
    full = lambda a: pl.BlockSpec(a.shape, lambda i: (0,) * a.ndim)
    return pl.pallas_call(
        _ple_final_kernel,
        grid=(t // tm,),
        in_specs=[row(D_MODEL), row(D_MODEL), row(PLE_DIM), full(wpp), full(wpg), full(gp), full(gf)],
        out_specs=row(D_MODEL),
        out_shape=jax.ShapeDtypeStruct((t, D_MODEL), F32),
        compiler_params=_params("parallel"),
        name="ple_final",
    )(h, m, p, wpp, wpg, gp, gf)


def kernel(x_prompt, x_sample, p_prompt, p_sample, cache_k, cache_v, state_conv, state_S, rel_bias, norm_mix, w_in, att_sink, conv_w, dn_A_log, dn_dt_bias, dn_norm, w_out, norm_ffn, w_router_group, w_router_expert, w_gate, w_up, w_down, w_ple_proj, w_ple_gate, norm_ple, norm_final):
    batch, seq, _ = x_prompt.shape
    nseq = x_sample.shape[0]
    assert x_sample.shape[1] == 1 and norm_mix.shape[0] == 1 and cache_k.shape[2] == WINDOW
    assert seq % GDN_TB == 0 and seq % ATT_BLOCK == 0

    wi = w_in[0]
    o_db = ATT_COLS + CONV_CH
    w_in_re = jnp.concatenate(
        [wi[:, :o_db], wi[:, o_db + 2 * DN_HEADS:], wi[:, o_db:o_db + 2 * DN_HEADS],
         jnp.zeros((D_MODEL, LANES - 2 * DN_HEADS), F32)], axis=1).astype(BF16)
    row = lambda a: a.reshape(1, -1).astype(F32)
    pad_lanes = lambda a, off: jnp.zeros((1, LANES), F32).at[0, off:off + a.shape[0]].set(a)
    alog = pad_lanes(dn_A_log[0], DN_HEADS)
    dtb = pad_lanes(dn_dt_bias[0], DN_HEADS)
    dnx = jnp.tile(dn_norm[0], DN_HEADS).reshape(1, DN_WIDTH)
    w_router = jnp.concatenate(
        [w_router_group[0], w_router_expert[0],
         jnp.zeros((D_MODEL, LANES - N_GROUPS - N_EXPERTS), F32)], axis=1).astype(BF16)
    wo = w_out[0].astype(BF16)
    wg, wu, wd = w_gate[0], w_up[0], w_down[0]
    wpp, wpg = w_ple_proj[0].astype(BF16), w_ple_gate[0].astype(BF16)
    sink = att_sink[0]

    qi = np.arange(ATT_BLOCK)[:, None]
    kj = np.arange(2 * ATT_BLOCK)[None, :]
    bucket_p = jnp.asarray(_t5_bucket_np(qi + ATT_BLOCK - kj))
    bucket_s = jnp.asarray(_t5_bucket_np(WINDOW - np.arange(WINDOW)[None, :]))

    xp = x_prompt.reshape(batch * seq, D_MODEL)
    att_p, qkv_p, dz_p, ba_p, xc_tails = _inproj_conv(xp, row(norm_mix[0]), w_in_re, conv_w[0], seq)
    o_att_p = _attn_prompt(att_p, bucket_p, rel_bias, sink, batch, seq)
    o_dn_p, s_p = _gdn_prompt(qkv_p, dz_p, ba_p, alog, dtb, dnx, batch, seq)
    h1, xn2, info, cnt = _route_sparse(xp, o_att_p, o_dn_p, wo, row(norm_ffn[0]), w_router)
    pos = _positions(info, cnt)
    pos1, pos2 = pos[:, 0], pos[:, 1]
    max_tiles = _moe_tiles(batch * seq)
    cnt_e = cnt[0, ROUTER_OFF:ROUTER_OFF + N_EXPERTS].astype(jnp.int32)
    tile_expert, tile_valid, n_tiles, last_tile, used = _tile_tables(cnt_e, max_tiles)
    xs_sorted = _sc_scatter_rows(xn2, pos1, pos2, max_tiles * MOE_TM)

    xs = x_sample.reshape(nseq, D_MODEL)
    att_s, xc_s, dz_s, ba_s = _inproj(xs, row(norm_mix[0]), w_in_re)
    ck_t = jnp.transpose(cache_k[0], (0, 2, 3, 1))
    cv_t = jnp.transpose(cache_v[0], (0, 2, 3, 1))
    o_att_s, ks_t, vs_t = _attn_sample(att_s, ck_t, cv_t, bucket_s, rel_bias, sink)
    sconv_t = jnp.swapaxes(state_conv[0], 0, 1)
    o_dn_s_t, s_s_t = _gdn_sample_lanes(xc_s, dz_s, ba_s, sconv_t, jnp.transpose(state_S[0], (1, 2, 3, 0)),
                                        conv_w[0], alog, dtb, dn_norm[0])
    s_s = jnp.transpose(s_s_t, (3, 0, 1, 2))

    h1_s, xn2_s, gates_s = _outproj_router(xs, o_att_s, o_dn_s_t, wo, row(norm_ffn[0]), w_router)

    ys, moe_s = _experts(xs_sorted, tile_expert, tile_valid, n_tiles, wg, wu, wd, xn2_s, gates_s)
    y1, y2 = _sc_gather_rows(ys, pos1, pos2)
    y_s = _ple_final(h1_s, moe_s, p_sample[0].reshape(nseq, PLE_DIM), wpp, wpg, row(norm_ple[0]),
                     row(norm_final))
    y_p = _ple_sparse(h1, info, y1, y2, p_prompt[0].reshape(batch * seq, PLE_DIM),
                      wpp, wpg, row(norm_ple[0]), row(norm_final))

    att_p3 = att_p.reshape(batch, seq, ATT_COLS)
    kv_shape = (1, batch, WINDOW, ATT_KV_HEADS, HEAD_DIM)
    k_p = att_p3[:, seq - WINDOW:, ATT_WIDTH:ATT_WIDTH + KV_WIDTH].reshape(kv_shape)
    v_p = att_p3[:, seq - WINDOW:, ATT_WIDTH + KV_WIDTH:].reshape(kv_shape)
    conv_p = xc_tails.reshape(batch, -1, TAIL, CONV_CH)[:, -1, TAIL - (CONV_WIDTH - 1):][None]
    k_s = jnp.transpose(ks_t, (0, 3, 1, 2))[None]
    v_s = jnp.transpose(vs_t, (0, 3, 1, 2))[None]
    conv_s = jnp.concatenate([state_conv[0][:, 1:], xc_s[:, None, :]], axis=1)[None]
    return (y_p.reshape(batch, seq, D_MODEL), y_s.reshape(nseq, 1, D_MODEL),
            k_p, v_p, conv_p, s_p[None], k_s, v_s, conv_s, s_s[None])
```

```python
import functools
import math

import numpy as np
import jax
import jax.numpy as jnp
from jax import lax
from jax.experimental import pallas as pl
from jax.experimental.pallas import tpu as pltpu
from jax.experimental.pallas import tpu_sc as plsc

F32 = jnp.float32
BF16 = jnp.bfloat16

D_MODEL = 1024
ATT_HEADS = 8
ATT_KV_HEADS = 2
HEAD_DIM = 64
GQA = ATT_HEADS // ATT_KV_HEADS
WINDOW = 128
ATT_BLOCK = 128
N_BUCKETS = 32
DN_HEADS = 8
DN_DK = 64
DN_DV = 64
CONV_WIDTH = 4
DN_CHUNK = 64
ATT_WIDTH = ATT_HEADS * HEAD_DIM
KV_WIDTH = ATT_KV_HEADS * HEAD_DIM
DN_WIDTH = DN_HEADS * DN_DV
CONV_CH = 3 * DN_WIDTH
N_GROUPS = 4
EXPERTS_PER_GROUP = 8
N_EXPERTS = N_GROUPS * EXPERTS_PER_GROUP
D_EXPERT = 256
PLE_DIM = 256
EPS = 1e-6
NEG_INF = float("-inf")

ATT_COLS = ATT_WIDTH + 2 * KV_WIDTH
LANES = 128
IN_COLS = ATT_COLS + CONV_CH + DN_WIDTH + LANES
ROUTER_OFF = N_GROUPS
VMEM_LIMIT = 48 * 1024 * 1024
ROW_TM = 512


def _params(*sem):
    return pltpu.CompilerParams(dimension_semantics=sem, vmem_limit_bytes=VMEM_LIMIT)


def _mm(a, b):
    return jnp.dot(a.astype(BF16), b.astype(BF16), preferred_element_type=F32)


def _mm_nt(a, b):
    return lax.dot_general(a.astype(BF16), b.astype(BF16), (((1,), (1,)), ((), ())),
                           preferred_element_type=F32)


def _mm_tn(a, b):
    return lax.dot_general(a.astype(BF16), b.astype(BF16), (((0,), (0,)), ((), ())),
                           preferred_element_type=F32)


def _split3(x):
    h1 = x.astype(BF16)
    r1 = x - h1.astype(F32)
    h2 = r1.astype(BF16)
    h3 = (r1 - h2.astype(F32)).astype(BF16)
    return h1, h2, h3


def _mm_sel_rhs(x, sel):
    h1, h2, h3 = _split3(x)
    d = lambda h: jnp.dot(h, sel, preferred_element_type=F32)
    return d(h1) + d(h2) + d(h3)


def _mm_sel_lhs(sel, x):
    h1, h2, h3 = _split3(x)
    d = lambda h: jnp.dot(sel, h, preferred_element_type=F32)
    return d(h1) + d(h2) + d(h3)


def _mm3(a, b):
    ah = a.astype(BF16)
    al = (a - ah.astype(F32)).astype(BF16)
    bh = b.astype(BF16)
    bl = (b - bh.astype(F32)).astype(BF16)
    d = lambda u, v: jnp.dot(u, v, preferred_element_type=F32)
    return d(ah, bh) + d(ah, bl) + d(al, bh)


def _sigmoid(x):
    return 1.0 / (1.0 + jnp.exp(-x))


def _silu(x):
    return x * _sigmoid(x)


def _silu_tanh(x):
    return x * (0.5 * jnp.tanh(0.5 * x) + 0.5)


def _softplus(x):
    return jnp.maximum(x, 0.0) + jnp.log1p(jnp.exp(-jnp.abs(x)))


def _rmsnorm(x, g):
    return x * lax.rsqrt(jnp.mean(x * x, axis=-1, keepdims=True) + EPS) * g


def _t5_bucket_np(dist):
    max_exact = N_BUCKETS // 2
    d = np.maximum(dist, 0)
    ratio = (np.log(np.maximum(d, 1).astype(np.float32) / np.float32(max_exact))
             / np.float32(math.log(WINDOW / max_exact))).astype(np.float32)
    large = np.minimum(max_exact + (ratio * np.float32(N_BUCKETS - max_exact)).astype(np.int32),
                       N_BUCKETS - 1)
    return np.where(d < max_exact, d, large).astype(np.int32)


def _bias_lookup(bucket, rb_ref, h):
    acc = jnp.zeros(bucket.shape, F32)
    for t in range(N_BUCKETS):
        acc = jnp.where(bucket == t, rb_ref[t, h], acc)
    return acc


def _inproj_kernel(x_ref, g_ref, w_ref, att_ref, xc_ref, dz_ref, ba_ref):
    xn = _rmsnorm(x_ref[...], g_ref[...]).astype(BF16)
    o0, o1, o2 = ATT_COLS, ATT_COLS + CONV_CH, ATT_COLS + CONV_CH + DN_WIDTH
    att_ref[...] = jnp.dot(xn, w_ref[:, :o0], preferred_element_type=F32)
    xc_ref[...] = jnp.dot(xn, w_ref[:, o0:o1], preferred_element_type=F32)
    dz_ref[...] = jnp.dot(xn, w_ref[:, o1:o2], preferred_element_type=F32)
    ba_ref[...] = jnp.dot(xn, w_ref[:, o2:], preferred_element_type=F32)


def _inproj(x, g, w):
    t = x.shape[0]
    tm = min(t, ROW_TM)
    row = lambda n: pl.BlockSpec((tm, n), lambda i: (i, 0))
    full = lambda a: pl.BlockSpec(a.shape, lambda i: (0,) * a.ndim)
    return pl.pallas_call(
        _inproj_kernel,
        grid=(t // tm,),
        in_specs=[row(D_MODEL), full(g), full(w)],
        out_specs=[row(ATT_COLS), row(CONV_CH), row(DN_WIDTH), row(LANES)],
        out_shape=[jax.ShapeDtypeStruct((t, n), F32) for n in (ATT_COLS, CONV_CH, DN_WIDTH, LANES)],
        compiler_params=_params("parallel"),
        name="inproj",
    )(x, g, w)


TAIL = 8
PAIR = 2 * DN_DK
N_PAIRS = DN_WIDTH // PAIR


def _head_sums(z, pair_ones):
    hi = z.astype(BF16)
    lw = (z - hi.astype(F32)).astype(BF16)
    d = lambda a, p: jnp.dot(a[:, p * PAIR:(p + 1) * PAIR], pair_ones, preferred_element_type=F32)
    return jnp.concatenate([d(hi, p) + d(lw, p) for p in range(N_PAIRS)], axis=1)


def _inproj_conv_kernel(x_ref, g_ref, w_ref, cw_ref, ones_ref, att_ref, qkv_ref, dz_ref, ba_ref, tail_ref,
                        xp_scr, *, tiles_per_seq):
    tm = x_ref.shape[0]

    @pl.when(pl.program_id(0) % tiles_per_seq == 0)
    def _():
        xp_scr[...] = jnp.zeros((TAIL, CONV_CH), F32)

    xn = _rmsnorm(x_ref[...], g_ref[...]).astype(BF16)
    o0, o1, o2 = ATT_COLS, ATT_COLS + CONV_CH, ATT_COLS + CONV_CH + DN_WIDTH
    xc = jnp.dot(xn, w_ref[:, o0:o1], preferred_element_type=F32)
    att_ref[...] = jnp.dot(xn, w_ref[:, :o0], preferred_element_type=F32)
    dz_ref[...] = jnp.dot(xn, w_ref[:, o1:o2], preferred_element_type=F32)
    ba_ref[...] = jnp.dot(xn, w_ref[:, o2:], preferred_element_type=F32)

    head = jnp.concatenate([xp_scr[...], xc[:TAIL, :]], axis=0)

    def shifted(j):
        return jnp.concatenate([head[TAIL - j:2 * TAIL - j, :], pltpu.roll(xc, j, axis=0)[TAIL:, :]], axis=0)

    y = shifted(3) * cw_ref[0:1, :]
    y = y + shifted(2) * cw_ref[1:2, :]
    y = y + shifted(1) * cw_ref[2:3, :]
    y = y + xc * cw_ref[3:4, :]
    tail = xc[tm - TAIL:, :]
    xp_scr[...] = tail
    tail_ref[0] = tail
    y = _silu_tanh(y)
    q = y[:, :DN_WIDTH]
    k = y[:, DN_WIDTH:2 * DN_WIDTH]
    inv_norm = lax.rsqrt(_head_sums(jnp.concatenate([q * q, k * k], axis=0), ones_ref[...]) + EPS)
    qkv_ref[:, :DN_WIDTH] = q * inv_norm[:tm] * (DN_DK ** -0.5)
    qkv_ref[:, DN_WIDTH:2 * DN_WIDTH] = k * inv_norm[tm:]
    qkv_ref[:, 2 * DN_WIDTH:] = y[:, 2 * DN_WIDTH:]


def _pair_ones():
    lane = np.arange(PAIR)
    return jnp.asarray((lane[:, None] // DN_DV == lane[None, :] // DN_DV).astype(np.float32), dtype=BF16)


def _inproj_conv(x, g, w, conv_w, seq):
    t = x.shape[0]
    tm = ROW_TM
    assert seq % tm == 0
    ones = _pair_ones()
    row = lambda n: pl.BlockSpec((tm, n), lambda i: (i, 0))
    full = lambda a: pl.BlockSpec(a.shape, lambda i: (0,) * a.ndim)
    return pl.pallas_call(
        functools.partial(_inproj_conv_kernel, tiles_per_seq=seq // tm),
        grid=(t // tm,),
        in_specs=[row(D_MODEL), full(g), full(w), full(conv_w), full(ones)],
        out_specs=[row(ATT_COLS), row(CONV_CH), row(DN_WIDTH), row(LANES),
                   pl.BlockSpec((1, TAIL, CONV_CH), lambda i: (i, 0, 0))],
        out_shape=[jax.ShapeDtypeStruct((t, n), F32) for n in (ATT_COLS, CONV_CH, DN_WIDTH, LANES)]
                  + [jax.ShapeDtypeStruct((t // tm, TAIL, CONV_CH), F32)],
        scratch_shapes=[pltpu.VMEM((TAIL, CONV_CH), F32)],
        compiler_params=_params("arbitrary"),
        name="inproj_conv",
    )(x, g, w, conv_w, ones)


GROUP_ROWS = GQA * ATT_BLOCK


def _attn_prompt_kernel(cur_ref, prev_ref, bucket_ref, rb_ref, sink_ref, o_ref, bias_scr, sink_scr):
    i = pl.program_id(0)
    nseq = cur_ref.shape[0]

    @pl.when(i == 0)
    def _():
        qi = lax.broadcasted_iota(jnp.int32, (ATT_BLOCK, 2 * ATT_BLOCK), 0)
        kj = lax.broadcasted_iota(jnp.int32, (ATT_BLOCK, 2 * ATT_BLOCK), 1)
        dist = qi + ATT_BLOCK - kj
        band = jnp.logical_and(dist >= 0, dist < WINDOW)
        bucket = bucket_ref[...]
        hrow = lax.broadcasted_iota(jnp.int32, (GROUP_ROWS, 1), 0) // ATT_BLOCK
        for g in range(ATT_KV_HEADS):
            sink_col = jnp.zeros((GROUP_ROWS, 1), F32)
            for hh in range(GQA):
                h = g * GQA + hh
                bias = jnp.where(band, _bias_lookup(bucket, rb_ref, h), NEG_INF)
                bias_scr[0, g, hh * ATT_BLOCK:(hh + 1) * ATT_BLOCK, :] = bias
                bias_scr[1, g, hh * ATT_BLOCK:(hh + 1) * ATT_BLOCK, :] = jnp.where(kj >= ATT_BLOCK, bias, NEG_INF)
                sink_col = jnp.where(hrow == hh, sink_ref[h], sink_col)
            sink_scr[g] = sink_col

    first = (i == 0).astype(jnp.int32)
    probs = [(b, g) for b in range(nseq) for g in range(ATT_KV_HEADS)]
    scores = []
    for b, g in probs:
        cur = cur_ref[b]
        prev = prev_ref[b]
        q = jnp.concatenate([cur[:, (g * GQA + hh) * HEAD_DIM:(g * GQA + hh + 1) * HEAD_DIM]
                             for hh in range(GQA)], axis=0) * (HEAD_DIM ** -0.5)
        kcol = slice(ATT_WIDTH + g * HEAD_DIM, ATT_WIDTH + (g + 1) * HEAD_DIM)
        k2 = jnp.concatenate([prev[:, kcol], cur[:, kcol]], axis=0)
        scores.append(_mm_nt(q, k2) + bias_scr[first, g])
    probs_p, dens = [], []
    for (b, g), s in zip(probs, scores):
        sink = sink_scr[g]
        m = jnp.maximum(jnp.max(s, axis=-1, keepdims=True), sink)
        p = jnp.exp(s - m)
        dens.append(jnp.sum(p, axis=-1, keepdims=True) + jnp.exp(sink - m))
        probs_p.append(p)
    outs = {}
    for (b, g), p, den in zip(probs, probs_p, dens):
        vcol = slice(ATT_WIDTH + KV_WIDTH + g * HEAD_DIM, ATT_WIDTH + KV_WIDTH + (g + 1) * HEAD_DIM)
        v2 = jnp.concatenate([prev_ref[b][:, vcol], cur_ref[b][:, vcol]], axis=0)
        outs[b, g] = _mm(p, v2) / den
    for b in range(nseq):
        o_ref[b] = jnp.concatenate([outs[b, g][hh * ATT_BLOCK:(hh + 1) * ATT_BLOCK, :]
                                    for g in range(ATT_KV_HEADS) for hh in range(GQA)],
                                   axis=1).astype(o_ref.dtype)


def _attn_prompt(att, bucket, rel_bias, sink, batch, seq):
    nb = seq // ATT_BLOCK
    smem = pl.BlockSpec(memory_space=pltpu.SMEM)
    att3 = att.reshape(batch, seq, ATT_COLS)
    out = pl.pallas_call(
        _attn_prompt_kernel,
        grid=(nb,),
        in_specs=[
            pl.BlockSpec((batch, ATT_BLOCK, ATT_COLS), lambda i: (0, i, 0)),
            pl.BlockSpec((batch, ATT_BLOCK, ATT_COLS), lambda i: (0, jnp.maximum(i - 1, 0), 0)),
            pl.BlockSpec(bucket.shape, lambda i: (0, 0)),
            smem, smem,
        ],
        out_specs=pl.BlockSpec((batch, ATT_BLOCK, ATT_WIDTH), lambda i: (0, i, 0)),
        out_shape=jax.ShapeDtypeStruct((batch, seq, ATT_WIDTH), BF16),
        scratch_shapes=[pltpu.VMEM((2, ATT_KV_HEADS, GROUP_ROWS, 2 * ATT_BLOCK), F32),
                        pltpu.VMEM((ATT_KV_HEADS, GROUP_ROWS, 1), F32)],
        compiler_params=_params("arbitrary"),
        name="attn_prompt",
    )(att3, att3, bucket, rel_bias, sink)
    return out.reshape(batch * seq, ATT_WIDTH)


ATT_S_BB = 8


def _attn_sample_kernel(att_ref, ck_ref, cv_ref, bucket_ref, rb_ref, sink_ref, o_ref, ks_ref, vs_ref,
                        bias_scr, col_scr):
    hrow = lax.broadcasted_iota(jnp.int32, (ATT_HEADS, LANES), 0)
    lane = lax.broadcasted_iota(jnp.int32, (ATT_HEADS, LANES), 1)

    last = (lax.broadcasted_iota(jnp.int32, (3, WINDOW), 1) == WINDOW - 1).astype(BF16)
    is_last = lax.broadcasted_iota(jnp.int32, (KV_WIDTH, WINDOW), 1) == WINDOW - 1

    def shifted(cache_t, new_row):
        pieces = jnp.concatenate([p.astype(F32) for p in _split3(new_row)], axis=0).astype(BF16)
        col = lax.dot_general(pieces, last, (((0,), (0,)), ((), ())), preferred_element_type=F32)
        out = jnp.where(is_last, col, pltpu.roll(cache_t, WINDOW - 1, axis=1))
        return out.reshape(ATT_KV_HEADS, HEAD_DIM, WINDOW)

    for b in range(ATT_S_BB):
        row = att_ref[b:b + 1, :]
        ks_ref[b] = shifted(ck_ref[b].reshape(KV_WIDTH, WINDOW), row[:, ATT_WIDTH:ATT_WIDTH + KV_WIDTH])
        vs_ref[b] = shifted(cv_ref[b].reshape(KV_WIDTH, WINDOW), row[:, ATT_WIDTH + KV_WIDTH:])

    @pl.when(pl.program_id(0) == 0)
    def _():
        bucket = jnp.broadcast_to(bucket_ref[...], (ATT_HEADS, LANES))
        bias = jnp.zeros((ATT_HEADS, LANES), F32)
        cols = jnp.zeros((ATT_HEADS, LANES), F32)
        for h in range(ATT_HEADS):
            bias = jnp.where(hrow == h, _bias_lookup(bucket, rb_ref, h), bias)
            cols = jnp.where(jnp.logical_and(hrow == h, lane == 0), sink_ref[h], cols)
            cols = jnp.where(jnp.logical_and(hrow == h, lane == 1), rb_ref[0, h], cols)
        bias_scr[...] = jnp.where(lane >= 1, bias, NEG_INF)
        col_scr[...] = cols

    bias_c = bias_scr[...]
    sink = col_scr[:, 0:1]
    bias_n = col_scr[:, 1:2]
    same_group = (hrow // GQA) == (lane // HEAD_DIM)
    low_group = lax.broadcasted_iota(jnp.int32, (ATT_HEADS, HEAD_DIM), 0) < GQA
    rnd = lambda a: a.astype(BF16).astype(F32)
    seqs = range(ATT_S_BB)
    rows = [att_ref[b:b + 1, :] for b in seqs]
    q_bds = []
    for row in rows:
        q = row[:, :ATT_WIDTH] * (HEAD_DIM ** -0.5)
        qh = jnp.concatenate([q[:, h * HEAD_DIM:(h + 1) * HEAD_DIM] for h in range(ATT_HEADS)], axis=0)
        q_bds.append(jnp.where(same_group, jnp.concatenate([qh, qh], axis=1), 0.0))
    kv_t = lambda ref, b: ref[b].reshape(KV_WIDTH, WINDOW)
    s_cs = [_mm(q_bd, kv_t(ck_ref, b)) + bias_c for b, q_bd in zip(seqs, q_bds)]
    prs, pns = [], []
    for row, q_bd, s_c in zip(rows, q_bds, s_cs):
        kn = row[:, ATT_WIDTH:ATT_WIDTH + KV_WIDTH]
        s_n = jnp.sum(rnd(q_bd) * rnd(kn), axis=-1, keepdims=True) + bias_n
        m = jnp.maximum(jnp.maximum(jnp.max(s_c, axis=-1, keepdims=True), s_n), sink)
        p_c = jnp.exp(s_c - m)
        p_n = jnp.exp(s_n - m)
        den = jnp.sum(p_c, axis=-1, keepdims=True) + p_n + jnp.exp(sink - m)
        prs.append(p_c / den)
        pns.append(p_n / den)
    pvs = [_mm_nt(pr, kv_t(cv_ref, b)) for b, pr in zip(seqs, prs)]
    for b, row, pv, pn in zip(seqs, rows, pvs, pns):
        vn = row[:, ATT_WIDTH + KV_WIDTH:]
        o_full = pv + rnd(pn) * rnd(vn)
        o_sel = jnp.where(low_group, o_full[:, :HEAD_DIM], o_full[:, HEAD_DIM:])
        o_ref[b:b + 1, :] = jnp.concatenate([o_sel[h:h + 1, :] for h in range(ATT_HEADS)], axis=1)


def _attn_sample(att, ck, cv, bucket, rel_bias, sink):
    nseq = att.shape[0]
    smem = pl.BlockSpec(memory_space=pltpu.SMEM)
    cache = pl.BlockSpec((ATT_S_BB, ATT_KV_HEADS, HEAD_DIM, WINDOW), lambda i: (i, 0, 0, 0))
    return pl.pallas_call(
        _attn_sample_kernel,
        grid=(nseq // ATT_S_BB,),
        in_specs=[pl.BlockSpec((ATT_S_BB, ATT_COLS), lambda i: (i, 0)), cache, cache,
                  pl.BlockSpec(bucket.shape, lambda i: (0, 0)), smem, smem],
        out_specs=[pl.BlockSpec((ATT_S_BB, ATT_WIDTH), lambda i: (i, 0)), cache, cache],
        out_shape=[jax.ShapeDtypeStruct((nseq, ATT_WIDTH), F32),
                   jax.ShapeDtypeStruct(ck.shape, F32), jax.ShapeDtypeStruct(cv.shape, F32)],
        scratch_shapes=[pltpu.VMEM((ATT_HEADS, LANES), F32), pltpu.VMEM((ATT_HEADS, LANES), F32)],
        compiler_params=_params("arbitrary"),
        name="attn_sample",
    )(att, ck, cv, bucket, rel_bias, sink)


GDN_TB = 128
GDN_NC = GDN_TB // DN_CHUNK


def _gdn_gates(ba, alog, dtb):
    beta = _sigmoid(ba)
    g = -jnp.exp(alog) * _softplus(ba + dtb)
    return beta, g


def _pair_diag(x, lo):
    xb = x.astype(BF16)
    zero = jnp.zeros_like(xb)
    return jnp.concatenate([jnp.where(lo, xb, zero), jnp.where(lo, zero, xb)], axis=0)


def _gdn_prompt_kernel(qkv_ref, dz_ref, ba_ref, alog_ref, dtb_ref, dnx_ref,
                       hsum_ref, expb_ref, expg_ref, ltri_ref,
                       o_ref, s_out_ref, s_scr):
    i = pl.program_id(0)
    nb = qkv_ref.shape[0]

    @pl.when(i == 0)
    def _():
        s_scr[...] = jnp.zeros(s_scr.shape, F32)

    hsum = hsum_ref[...]
    ri = lax.broadcasted_iota(jnp.int32, (DN_CHUNK, PAIR), 0)
    ci = lax.broadcasted_iota(jnp.int32, (DN_CHUNK, PAIR), 1)
    lo = ci < DN_DK
    cj = jnp.where(lo, ci, ci - DN_DK)
    causal = ri >= cj
    strict = ri > cj
    eye = (ri == cj).astype(F32)

    def sel2(x, m):
        hi = x.astype(BF16)
        lw = (x - hi.astype(F32)).astype(BF16)
        return (jnp.dot(hi, m, preferred_element_type=F32) + jnp.dot(lw, m, preferred_element_type=F32))

    pre = []
    for b in range(nb):
        q = qkv_ref[b, :, :DN_WIDTH]
        k = qkv_ref[b, :, DN_WIDTH:2 * DN_WIDTH]
        v = qkv_ref[b, :, 2 * DN_WIDTH:]
        beta_c, g_c = _gdn_gates(ba_ref[b], alog_ref[...], dtb_ref[...])
        beta = sel2(beta_c, expb_ref[...])
        gam_c = _mm_sel_lhs(ltri_ref[...], g_c)
        gam = _mm_sel_rhs(gam_c, expg_ref[...])
        gam_t = gam_c.T
        kb = k * beta
        egam = jnp.exp(gam)
        pre.append(dict(q=q, k=k, kb=kb, vb=v * beta, qg=q * egam, wr=kb * egam, gam=gam, gam_t=gam_t))

    probs = [(c, b, p) for c in range(GDN_NC) for b in range(nb) for p in range(N_PAIRS)]
    pick = lambda m: jnp.where(lo, m[:DN_DK], m[DN_DK:])
    rows_of = lambda c: slice(c * DN_CHUNK, (c + 1) * DN_CHUNK)
    sl = lambda name, c, b, p: pre[b][name][rows_of(c), p * PAIR:(p + 1) * PAIR]
    raws = []
    for c, b, p in probs:
        k_p = sl("k", c, b, p)
        k_rows = jnp.concatenate([jnp.where(lo, k_p, 0.0), jnp.where(lo, 0.0, k_p)], axis=0)
        raws.append(_mm_nt(jnp.concatenate([sl("kb", c, b, p), sl("q", c, b, p)], axis=0), k_rows))
    pws, ts, qks = [], [], []
    for (c, b, p), raw in zip(probs, raws):
        gcol = sl("gam", c, b, p)
        h0 = DN_HEADS + 2 * p
        gam_t = pre[b]["gam_t"]
        grow = jnp.concatenate([gam_t[h0:h0 + 1, rows_of(c)], gam_t[h0 + 1:h0 + 2, rows_of(c)]], axis=1)
        decay = jnp.exp(jnp.where(causal, gcol - grow, NEG_INF))
        a = jnp.where(strict, raw[:DN_CHUNK] * decay, 0.0)
        qks.append(jnp.where(causal, raw[DN_CHUNK:] * decay, 0.0))
        pws.append(-a)
        ts.append(eye - a)
    pws = [_mm(pw, _pair_diag(pw, lo)) for pw in pws]
    for _ in range(4):
        rs = [_mm(jnp.concatenate([pw, t], axis=0), _pair_diag(pw, lo)) for pw, t in zip(pws, ts)]
        pws = [r[:DN_CHUNK] for r in rs]
        ts = [t + r[DN_CHUNK:] for t, r in zip(ts, rs)]
    rs = [_mm(t, _pair_diag(pw, lo)) for pw, t in zip(pws, ts)]
    ts = [t + r for t, r in zip(ts, rs)]
    sols = [_mm(t, jnp.concatenate([_pair_diag(sl("vb", c, b, p), lo), _pair_diag(sl("wr", c, b, p), lo)],
                                   axis=1)) for (c, b, p), t in zip(probs, ts)]
    qkuws = [_mm(qk, jnp.concatenate([_pair_diag(s[:, :PAIR], lo), _pair_diag(s[:, PAIR:], lo)], axis=1))
             for qk, s in zip(qks, sols)]
    crosses, gls = [], []
    for (c, b, p), s in zip(probs, sols):
        last = (c + 1) * DN_CHUNK - 1
        gam_last = pre[b]["gam"][last:last + 1, p * PAIR:(p + 1) * PAIR]
        kd = sl("k", c, b, p) * jnp.exp(gam_last - sl("gam", c, b, p))
        crosses.append(_mm_tn(kd, s))
        gls.append(jnp.exp(gam_last))
    lhs = [jnp.concatenate([pick(cr[:, PAIR:]), sl("qg", c, b, p) - qkuw[:, PAIR:]], axis=0)
           for (c, b, p), cr, qkuw in zip(probs, crosses, qkuws)]

    o_rows = [[] for _ in range(nb)]
    per_chunk = nb * N_PAIRS
    for c in range(GDN_NC):
        sel = slice(c * per_chunk, (c + 1) * per_chunk)
        s_olds = [s_scr[b, p] for _, b, p in probs[sel]]
        rs = [_mm(l, _pair_diag(s_old, lo)) for l, s_old in zip(lhs[sel], s_olds)]
        o_pairs = [[] for _ in range(nb)]
        for (_, b, p), r, s_old, gl, cr, qkuw in zip(probs[sel], rs, s_olds, gls[sel], crosses[sel], qkuws[sel]):
            s_scr[b, p] = gl * s_old - r[:DN_DK] + pick(cr[:, :PAIR])
            o_pairs[b].append(r[DN_DK:] + qkuw[:, :PAIR])
        for b in range(nb):
            o_rows[b].append(jnp.concatenate(o_pairs[b], axis=1))

    o_all = jnp.concatenate([jnp.concatenate(rows, axis=0) for rows in o_rows], axis=0)
    inv_rms = lax.rsqrt(_head_sums(o_all * o_all, hsum) * (1.0 / DN_DV) + EPS)
    for b in range(nb):
        rows = slice(b * GDN_TB, (b + 1) * GDN_TB)
        o_ref[b] = (o_all[rows] * inv_rms[rows] * dnx_ref[...] * _silu_tanh(dz_ref[b])).astype(o_ref.dtype)

    @pl.when(i == pl.num_programs(0) - 1)
    def _():
        for b in range(nb):
            for p in range(N_PAIRS):
                s_p = s_scr[b, p]
                s_out_ref[b, 2 * p] = s_p[:, :DN_DV]
                s_out_ref[b, 2 * p + 1] = s_p[:, DN_DV:]


def _gdn_consts():
    lane = np.arange(DN_WIDTH)
    pl_lane = np.arange(PAIR)
    hsum = (pl_lane[:, None] // DN_DV == pl_lane[None, :] // DN_DV)
    src = np.arange(LANES)
    expb = (src[:, None] == lane[None, :] // DN_DV)
    expg = (src[:, None] == DN_HEADS + lane[None, :] // DN_DV)
    tok = np.arange(GDN_TB)
    ltri = np.logical_and(tok[:, None] >= tok[None, :],
                          tok[:, None] // DN_CHUNK == tok[None, :] // DN_CHUNK)
    as_bf16 = lambda m: jnp.asarray(m.astype(np.float32), dtype=BF16)
    return as_bf16(hsum), as_bf16(expb), as_bf16(expg), as_bf16(ltri)


def _gdn_prompt(xc, dz, ba, alog, dtb, dnx, batch, seq):
    nt = seq // GDN_TB
    hsum, expb, expg, ltri = _gdn_consts()
    row = lambda n: pl.BlockSpec((batch, GDN_TB, n), lambda i: (0, i, 0))
    full = lambda a: pl.BlockSpec(a.shape, lambda i: (0,) * a.ndim)
    consts = (alog, dtb, dnx, hsum, expb, expg, ltri)
    as3d = lambda a: a.reshape(batch, seq, a.shape[-1])
    o, s = pl.pallas_call(
        _gdn_prompt_kernel,
        grid=(nt,),
        in_specs=[row(CONV_CH), row(DN_WIDTH), row(LANES)] + [full(a) for a in consts],
        out_specs=[row(DN_WIDTH),
                   pl.BlockSpec((batch, DN_HEADS, DN_DK, DN_DV), lambda i: (0, 0, 0, 0))],
        out_shape=[jax.ShapeDtypeStruct((batch, seq, DN_WIDTH), BF16),
                   jax.ShapeDtypeStruct((batch, DN_HEADS, DN_DK, DN_DV), F32)],
        scratch_shapes=[pltpu.VMEM((batch, N_PAIRS, DN_DK, PAIR), F32)],
        compiler_params=_params("arbitrary"),
        name="gdn_prompt",
    )(as3d(xc), as3d(dz), as3d(ba), *consts)
    return o.reshape(batch * seq, DN_WIDTH), s


GDN_S_BB = 8


def _gdn_sample_kernel(xc_ref, dz_ref, ba_ref, sc_ref, s_ref, cw_ref, alog_ref, dtb_ref, dn_ref,
                       hsum_ref, eye_ref, hsel_ref, hrep3_ref, o_ref, s_out_ref):
    xc = xc_ref[...]
    y = sc_ref[0] * cw_ref[0:1, :]
    y = y + sc_ref[1] * cw_ref[1:2, :]
    y = y + sc_ref[2] * cw_ref[2:3, :]
    y = _silu(y + xc * cw_ref[3:4, :])
    hsum = hsum_ref[...]
    q = y[:, :DN_WIDTH]
    k = y[:, DN_WIDTH:2 * DN_WIDTH]
    v = y[:, 2 * DN_WIDTH:]
    q = q * lax.rsqrt(_mm_sel_rhs(q * q, hsum) + EPS) * (DN_DK ** -0.5)
    k = k * lax.rsqrt(_mm_sel_rhs(k * k, hsum) + EPS)
    beta_c, g_c = _gdn_gates(ba_ref[...], alog_ref[...], dtb_ref[...])
    eg_c = jnp.exp(g_c)
    eye = eye_ref[...]
    tr = lambda a: lax.dot_general(a, eye, (((0,), (0,)), ((), ())), precision=lax.Precision.HIGHEST,
                                   preferred_element_type=F32)
    gates_t = tr(jnp.concatenate([beta_c, eg_c], axis=1))
    beta_t = gates_t[:LANES]
    eg_t = gates_t[LANES:]
    dz = dz_ref[...]
    dn = dn_ref[...]
    split = lambda r: jnp.concatenate([r[:, h * DN_DV:(h + 1) * DN_DV] for h in range(DN_HEADS)], axis=0)
    own_head = hsel_ref[...].astype(F32)
    hrep3 = hrep3_ref[...]
    seqs = range(GDN_S_BB)
    dot = lambda a, b: jnp.dot(a.astype(BF16), b.astype(BF16), preferred_element_type=F32)

    def pieces(x):
        p1 = x.astype(BF16).astype(F32)
        r1 = x - p1
        p2 = r1.astype(BF16).astype(F32)
        return p1, p2, (r1 - p2).astype(BF16).astype(F32)

    heads = DN_HEADS
    k_pieces, kqs = [], []
    for b in seqs:
        kq_bd = jnp.concatenate([own_head * k[b:b + 1, :], own_head * q[b:b + 1, :]], axis=0)
        a1, a2, a3 = pieces(kq_bd)
        s1, s2, s3 = pieces(s_ref[b])
        r1 = dot(jnp.concatenate([a1, a2, a3], axis=0), s1)
        r2 = dot(jnp.concatenate([a1, a2], axis=0), s2)
        r3 = dot(a1, s3)
        n = 2 * heads
        kqs.append(((r3 + r2[n:] + r1[2 * n:]) + (r2[:n] + r1[n:2 * n])) + r1[:n])
        k_pieces.append((a1[:heads], a2[:heads], a3[:heads]))
    egs = [eg_t[DN_HEADS:2 * DN_HEADS, b:b + 1] for b in seqs]
    qks = [jnp.sum(split(q[b:b + 1, :]) * split(k[b:b + 1, :]), axis=-1, keepdims=True) for b in seqs]
    v_news = [beta_t[0:DN_HEADS, b:b + 1] * (split(v[b:b + 1, :]) - eg * kq[:heads])
              for b, eg, kq in zip(seqs, egs, kqs)]
    os_ = [eg * kq[heads:] + qk * v_new for eg, kq, qk, v_new in zip(egs, kqs, qks, v_news)]
    inv_rms = [lax.rsqrt(jnp.mean(o * o, axis=-1, keepdims=True) + EPS) for o in os_]
    for b, o, r in zip(seqs, os_, inv_rms):
        o_ref[b] = o * r * dn * _silu(split(dz[b:b + 1, :]))
    outers, egrows = [], []
    for (k1, k2, k3), v_new, eg in zip(k_pieces, v_news, egs):
        v1, v2, v3 = pieces(v_new)
        lhs = jnp.concatenate([k1, k1, k2, k1, k2, k3], axis=0).astype(BF16)
        rhs = jnp.concatenate([v1, v2, v1, v3, v2, v1], axis=0).astype(BF16)
        outers.append(lax.dot_general(lhs, rhs, (((0,), (0,)), ((), ())), preferred_element_type=F32))
        egrows.append(dot(hrep3, jnp.concatenate(pieces(jnp.broadcast_to(eg, (DN_HEADS, DN_DV))), axis=0)))
    for b, outer, egrow in zip(seqs, outers, egrows):
        s_out_ref[b] = s_ref[b] * egrow + outer


def _gdn_sample(xc, dz, ba, sconv_t, state, conv_w, alog, dtb, dn):
    nseq = xc.shape[0]
    lane = np.arange(DN_WIDTH)
    hsum = jnp.asarray((lane[:, None] // DN_DV == lane[None, :] // DN_DV).astype(np.float32), dtype=BF16)
    eye = jnp.eye(GDN_S_BB, dtype=F32)
    hsel_np = (np.arange(DN_HEADS)[:, None] == lane[None, :] // DN_DK).astype(np.float32)
    hsel = jnp.asarray(hsel_np, dtype=BF16)
    hrep3 = jnp.asarray(np.tile(hsel_np.T, (1, 3)), dtype=BF16)
    row = lambda n: pl.BlockSpec((GDN_S_BB, n), lambda i: (i, 0))
    full = lambda a: pl.BlockSpec(a.shape, lambda i: (0,) * a.ndim)
    st = pl.BlockSpec((GDN_S_BB, DN_HEADS * DN_DK, DN_DV), lambda i: (i, 0, 0))
    consts = (conv_w, alog, dtb, dn, hsum, eye, hsel, hrep3)
    return pl.pallas_call(
        _gdn_sample_kernel,
        grid=(nseq // GDN_S_BB,),
        in_specs=[row(CONV_CH), row(DN_WIDTH), row(LANES),
                  pl.BlockSpec((CONV_WIDTH - 1, GDN_S_BB, CONV_CH), lambda i: (0, i, 0)), st]
                 + [full(a) for a in consts],
        out_specs=[pl.BlockSpec((GDN_S_BB, DN_HEADS, DN_DV), lambda i: (i, 0, 0)), st],
        out_shape=[jax.ShapeDtypeStruct((nseq, DN_HEADS, DN_DV), F32),
                   jax.ShapeDtypeStruct(state.shape, F32)],
        compiler_params=_params("parallel"),
        name="gdn_sample",
    )(xc, dz, ba, sconv_t, state, *consts)


def _attn_sample_lanes_kernel(att_ref, ck_ref, cv_ref, bucket_ref, rb_ref, sink_ref, o_ref, s_scr):
    g = pl.program_id(0)
    nseq = att_ref.shape[0]
    rnd = lambda a: a.astype(BF16).astype(F32)
    att = att_ref[...]
    q_all_t = (att[:, :ATT_WIDTH] * (HEAD_DIM ** -0.5)).T
    kv_new_t = att[:, ATT_WIDTH:].T
    qsel = [jnp.where(g == 0, q_all_t[hh * HEAD_DIM:(hh + 1) * HEAD_DIM],
                      q_all_t[(GQA + hh) * HEAD_DIM:(GQA + hh + 1) * HEAD_DIM]) for hh in range(GQA)]
    qr = [rnd(q) for q in qsel]
    kn = rnd(jnp.where(g == 0, kv_new_t[0:HEAD_DIM], kv_new_t[HEAD_DIM:2 * HEAD_DIM]))
    vn = rnd(jnp.where(g == 0, kv_new_t[2 * HEAD_DIM:3 * HEAD_DIM], kv_new_t[3 * HEAD_DIM:]))

    def score_row(j, carry):
        kj = rnd(ck_ref[j, 0])
        for hh in range(GQA):
            s_scr[hh, pl.ds(j, 1), :] = jnp.sum(qr[hh] * kj, axis=0, keepdims=True)
        return carry
    lax.fori_loop(0, WINDOW, score_row, 0, unroll=2)

    bucket = bucket_ref[...]
    jrow = lax.broadcasted_iota(jnp.int32, (WINDOW, nseq), 0)
    prn = []
    for hh in range(GQA):
        h = g * GQA + hh
        bias = jnp.where(jrow >= 1, _bias_lookup(bucket, rb_ref, h), NEG_INF)
        s = s_scr[hh] + bias
        s_n = jnp.sum(qr[hh] * kn, axis=0, keepdims=True) + rb_ref[0, h]
        sink = sink_ref[h]
        m = jnp.maximum(jnp.maximum(jnp.max(s, axis=0, keepdims=True), s_n), sink)
        p = jnp.exp(s - m)
        p_n = jnp.exp(s_n - m)
        den = jnp.sum(p, axis=0, keepdims=True) + p_n + jnp.exp(sink - m)
        s_scr[hh] = rnd(p / den)
        prn.append(rnd(p_n / den))

    def value_row(j, acc):
        vj = rnd(cv_ref[j, 0])
        return tuple(acc[hh] + s_scr[hh, pl.ds(j, 1), :] * vj for hh in range(GQA))
    zero = jnp.zeros((HEAD_DIM, nseq), F32)
    acc = lax.fori_loop(0, WINDOW, value_row, (zero,) * GQA, unroll=2)
    for hh in range(GQA):
        o_ref[hh * HEAD_DIM:(hh + 1) * HEAD_DIM, :] = acc[hh] + prn[hh] * vn


def _attn_sample_lanes(att, ck_t, cv_t, rel_bias, sink):
    nseq = att.shape[0]
    assert nseq == LANES
    bucket = jnp.asarray(np.broadcast_to(_t5_bucket_np(WINDOW - np.arange(WINDOW))[:, None], (WINDOW, nseq)))
    smem = pl.BlockSpec(memory_space=pltpu.SMEM)
    cache = pl.BlockSpec((WINDOW, 1, HEAD_DIM, nseq), lambda g: (0, g, 0, 0))
    full = lambda a: pl.BlockSpec(a.shape, lambda g: (0,) * a.ndim)
    return pl.pallas_call(
        _attn_sample_lanes_kernel,
        grid=(ATT_KV_HEADS,),
        in_specs=[full(att), cache, cache, full(bucket), smem, smem],
        out_specs=pl.BlockSpec((GQA * HEAD_DIM, nseq), lambda g: (g, 0)),
        out_shape=jax.ShapeDtypeStruct((ATT_WIDTH, nseq), F32),
        scratch_shapes=[pltpu.VMEM((GQA, WINDOW, nseq), F32)],
        compiler_params=_params("arbitrary"),
        name="attn_sample_lanes",
    )(att, ck_t, cv_t, bucket, rel_bias, sink)


def _gdn_sample_front_kernel(xc_ref, dz_ref, ba_ref, sc_ref, cw_ref, alog_ref, dtb_ref, hsum_ref,
                             q_ref, k_ref, v_ref, dz_t_ref, gates_ref):
    xc = xc_ref[...]
    y = sc_ref[0] * cw_ref[0:1, :]
    y = y + sc_ref[1] * cw_ref[1:2, :]
    y = y + sc_ref[2] * cw_ref[2:3, :]
    y = _silu(y + xc * cw_ref[3:4, :])
    hsum = hsum_ref[...]
    q = y[:, :DN_WIDTH]
    k = y[:, DN_WIDTH:2 * DN_WIDTH]
    q = q * lax.rsqrt(_mm_sel_rhs(q * q, hsum) + EPS) * (DN_DK ** -0.5)
    k = k * lax.rsqrt(_mm_sel_rhs(k * k, hsum) + EPS)
    beta_c, g_c = _gdn_gates(ba_ref[...], alog_ref[...], dtb_ref[...])
    q_ref[...] = q.T
    k_ref[...] = k.T
    v_ref[...] = y[:, 2 * DN_WIDTH:].T
    dz_t_ref[...] = dz_ref[...].T
    gates_ref[0:LANES, :] = beta_c.T
    gates_ref[LANES:, :] = jnp.exp(g_c).T


def _gdn_sample_step_kernel(q_ref, k_ref, v_ref, dz_ref, gates_ref, dn_ref, s_ref, o_ref, s_out_ref):
    h = pl.program_id(0)
    beta = gates_ref[pl.ds(h, 1), :]
    eg = gates_ref[pl.ds(LANES + DN_HEADS + h, 1), :]
    q, k, v = q_ref[...], k_ref[...], v_ref[...]
    w = (k * beta) * eg
    qg = q * eg
    ws = jnp.zeros(v.shape, F32)
    qs = jnp.zeros(v.shape, F32)
    for dk in range(DN_DK):
        s_dk = s_ref[0, dk]
        ws = ws + w[dk:dk + 1, :] * s_dk
        qs = qs + qg[dk:dk + 1, :] * s_dk
    v_new = v * beta - ws
    qk = jnp.sum(q * k, axis=0, keepdims=True)
    o = qs + qk * v_new
    for dk in range(DN_DK):
        s_out_ref[0, dk] = s_ref[0, dk] * eg + k[dk:dk + 1, :] * v_new
    o = o * lax.rsqrt(jnp.mean(o * o, axis=0, keepdims=True) + EPS) * dn_ref[...]
    o_ref[...] = o * _silu(dz_ref[...])


def _gdn_sample_lanes(xc, dz, ba, sconv_t, state_t, conv_w, alog, dtb, dn):
    nseq = xc.shape[0]
    assert nseq == LANES
    lane = np.arange(DN_WIDTH)
    hsum = jnp.asarray((lane[:, None] // DN_DV == lane[None, :] // DN_DV).astype(np.float32), dtype=BF16)
    full = lambda a: pl.BlockSpec(a.shape, lambda i: (0,) * a.ndim)
    cm = jax.ShapeDtypeStruct((DN_WIDTH, nseq), F32)
    front_in = (xc, dz, ba, sconv_t, conv_w, alog, dtb, hsum)
    q_t, k_t, v_t, dz_t, gates_t = pl.pallas_call(
        _gdn_sample_front_kernel,
        grid=(1,),
        in_specs=[full(a) for a in front_in],
        out_specs=[pl.BlockSpec((DN_WIDTH, nseq), lambda i: (0, 0))] * 4
                  + [pl.BlockSpec((2 * LANES, nseq), lambda i: (0, 0))],
        out_shape=[cm, cm, cm, cm, jax.ShapeDtypeStruct((2 * LANES, nseq), F32)],
        compiler_params=_params("arbitrary"),
        name="gdn_sample_front",
    )(*front_in)
    dn_b = jnp.broadcast_to(dn.reshape(DN_DV, 1), (DN_DV, nseq))
    head = pl.BlockSpec((DN_DK, nseq), lambda h: (h, 0))
    st = pl.BlockSpec((1, DN_DK, DN_DV, nseq), lambda h: (h, 0, 0, 0))
    return pl.pallas_call(
        _gdn_sample_step_kernel,
        grid=(DN_HEADS,),
        in_specs=[head, head, head, head, full(gates_t), full(dn_b), st],
        out_specs=[head, st],
        out_shape=[cm, jax.ShapeDtypeStruct(state_t.shape, F32)],
        compiler_params=_params("parallel"),
        name="gdn_sample_step",
    )(q_t, k_t, v_t, dz_t, gates_t, dn_b, state_t)


def _route(xn, wr):
    logits = jnp.dot(xn, wr, preferred_element_type=F32)
    lane = lax.broadcasted_iota(jnp.int32, logits.shape, 1).astype(F32)
    first_at = lambda hit: jnp.min(jnp.where(hit, lane, float(LANES)), axis=-1, keepdims=True)
    glog = jnp.where(lane < N_GROUPS, logits, NEG_INF)
    gmax = jnp.max(glog, axis=-1, keepdims=True)
    gsel = first_at(glog == gmax)
    pgsel = 1.0 / jnp.sum(jnp.exp(glog - gmax), axis=-1, keepdims=True)
    lo = ROUTER_OFF + gsel * EXPERTS_PER_GROUP
    in_group = jnp.logical_and(lane >= lo, lane < lo + EXPERTS_PER_GROUP)
    elog = jnp.where(in_group, logits, NEG_INF)
    m1 = jnp.max(elog, axis=-1, keepdims=True)
    i1 = first_at(elog == m1)
    z = jnp.sum(jnp.exp(elog - m1), axis=-1, keepdims=True)
    elog2 = jnp.where(lane == i1, NEG_INF, elog)
    m2 = jnp.max(elog2, axis=-1, keepdims=True)
    i2 = first_at(elog2 == m2)
    p1 = 1.0 / z
    p2 = jnp.exp(m2 - m1) / z
    tot = p1 + p2
    return lane, i1, i2, p1 / tot * pgsel, p2 / tot * pgsel


def _outproj(x_ref, oa_ref, od_ref, wo_ref):
    return x_ref[...] + _mm(oa_ref[...], wo_ref[:ATT_WIDTH, :]) + _mm(od_ref[...], wo_ref[ATT_WIDTH:, :])


def _outproj_router_kernel(x_ref, oa_ref, od_t_ref, wo_ref, g_ref, wr_ref, h_ref, xn_ref, gate_ref):
    h = (x_ref[...] + _mm(oa_ref[...], wo_ref[:ATT_WIDTH, :])
         + _mm(od_t_ref[...].T, wo_ref[ATT_WIDTH:, :]))
    h_ref[...] = h
    xn = _rmsnorm(h, g_ref[...]).astype(BF16)
    xn_ref[...] = xn
    lane, i1, i2, g1, g2 = _route(xn, wr_ref[...])
    gate_ref[...] = jnp.where(lane == i1, g1, 0.0) + jnp.where(lane == i2, g2, 0.0)


def _outproj_router(x, oa, od_t, wo, g, wr):
    t = x.shape[0]
    tm = t
    row = lambda n: pl.BlockSpec((tm, n), lambda i: (i, 0))
    full = lambda a: pl.BlockSpec(a.shape, lambda i: (0,) * a.ndim)
    return pl.pallas_call(
        _outproj_router_kernel,
        grid=(t // tm,),
        in_specs=[row(D_MODEL), row(ATT_WIDTH), full(od_t), full(wo), full(g), full(wr)],
        out_specs=[row(D_MODEL), row(D_MODEL), row(LANES)],
        out_shape=[jax.ShapeDtypeStruct((t, D_MODEL), F32), jax.ShapeDtypeStruct((t, D_MODEL), BF16),
                   jax.ShapeDtypeStruct((t, LANES), F32)],
        compiler_params=_params("parallel"),
        name="outproj_router",
    )(x, oa, od_t, wo, g, wr)


def _moe_kernel(xn_ref, gate_ref, wg_ref, wu_ref, wd_ref, o_ref):
    e = pl.program_id(1)
    xn = xn_ref[...]
    lane = lax.broadcasted_iota(jnp.int32, gate_ref.shape, 1)
    gate = jnp.sum(jnp.where(lane == e + ROUTER_OFF, gate_ref[...], 0.0), axis=-1, keepdims=True)
    hg = jnp.dot(xn, wg_ref[...].astype(BF16), preferred_element_type=F32)
    hu = jnp.dot(xn, wu_ref[...].astype(BF16), preferred_element_type=F32)
    hm = _silu(hg) * hu * gate
    y = jnp.dot(hm.astype(BF16), wd_ref[...].astype(BF16), preferred_element_type=F32)

    @pl.when(e == 0)
    def _():
        o_ref[...] = y

    @pl.when(e > 0)
    def _():
        o_ref[...] += y


def _moe(xn, gates, wg, wu, wd):
    t = xn.shape[0]
    tm = min(t, 1024)
    return pl.pallas_call(
        _moe_kernel,
        grid=(t // tm, N_EXPERTS),
        in_specs=[pl.BlockSpec((tm, D_MODEL), lambda i, e: (i, 0)),
                  pl.BlockSpec((tm, LANES), lambda i, e: (i, 0)),
                  pl.BlockSpec((None, D_MODEL, D_EXPERT), lambda i, e: (e, 0, 0)),
                  pl.BlockSpec((None, D_MODEL, D_EXPERT), lambda i, e: (e, 0, 0)),
                  pl.BlockSpec((None, D_EXPERT, D_MODEL), lambda i, e: (e, 0, 0))],
        out_specs=pl.BlockSpec((tm, D_MODEL), lambda i, e: (i, 0)),
        out_shape=jax.ShapeDtypeStruct((t, D_MODEL), F32),
        compiler_params=_params("parallel", "arbitrary"),
        name="moe",
    )(xn, gates, wg, wu, wd)


MOE_TM = 512
POS_TM = 1024
INFO_G1, INFO_G2, INFO_E1, INFO_E2 = 0, 1, 2, 3
DMA_UNROLL = 8


def _moe_tiles(t):
    return (2 * t) // MOE_TM + N_EXPERTS


HALF = D_MODEL // 2
U32 = jnp.uint32


def _pack_rows(x):
    bits = lambda v: lax.bitcast_convert_type(v.astype(BF16).astype(F32), U32)
    return bits(x[:, HALF:]) | (bits(x[:, :HALF]) >> 16)


def _unpack_rows(w):
    lo = lax.bitcast_convert_type(w << 16, F32)
    hi = lax.bitcast_convert_type(w & jnp.uint32(0xFFFF0000), F32)
    return lo, hi


def _route_kernel(x_ref, oa_ref, od_ref, wo_ref, g_ref, wr_ref, h_ref, xn_ref, info_ref, cnt_ref, run_scr):
    h = _outproj(x_ref, oa_ref, od_ref, wo_ref)
    h_ref[...] = h
    xn = _rmsnorm(h, g_ref[...])
    xn_ref[...] = _pack_rows(xn)
    lane, i1, i2, g1, g2 = _route(xn.astype(BF16), wr_ref[...])
    info = jnp.where(lane == INFO_G1, g1, 0.0) + jnp.where(lane == INFO_G2, g2, 0.0)
    info = info + jnp.where(lane == INFO_E1, i1, 0.0) + jnp.where(lane == INFO_E2, i2, 0.0)
    info_ref[...] = info

    @pl.when(pl.program_id(0) == 0)
    def _():
        run_scr[...] = jnp.zeros(run_scr.shape, F32)
    picked = jnp.logical_or(lane == i1, lane == i2).astype(F32)
    run_scr[...] += jnp.sum(picked, axis=0, keepdims=True)
    cnt_ref[...] = run_scr[...]


def _route_sparse(x, oa, od, wo, g, wr):
    t = x.shape[0]
    tm = ROW_TM
    row = lambda n: pl.BlockSpec((tm, n), lambda i: (i, 0))
    full = lambda a: pl.BlockSpec(a.shape, lambda i: (0,) * a.ndim)
    return pl.pallas_call(
        _route_kernel,
        grid=(t // tm,),
        in_specs=[row(D_MODEL), row(ATT_WIDTH), row(DN_WIDTH), full(wo), full(g), full(wr)],
        out_specs=[row(D_MODEL), row(HALF), row(LANES), pl.BlockSpec((1, LANES), lambda i: (0, 0))],
        out_shape=[jax.ShapeDtypeStruct((t, D_MODEL), F32), jax.ShapeDtypeStruct((t, HALF), U32),
                   jax.ShapeDtypeStruct((t, LANES), F32), jax.ShapeDtypeStruct((1, LANES), F32)],
        scratch_shapes=[pltpu.VMEM((1, LANES), F32)],
        compiler_params=_params("arbitrary"),
        name="route",
    )(x, oa, od, wo, g, wr)


def _positions_kernel(info_ref, cnt_ref, ltri_ref, utri_ref, pos_ref, run_scr, off_scr):
    info = info_ref[...]
    lane = lax.broadcasted_iota(jnp.int32, info.shape, 1).astype(F32)
    hit1 = lane == info[:, INFO_E1:INFO_E1 + 1]
    hit2 = lane == info[:, INFO_E2:INFO_E2 + 1]
    onehot = jnp.logical_or(hit1, hit2).astype(F32)

    @pl.when(pl.program_id(0) == 0)
    def _():
        ln = lax.broadcasted_iota(jnp.int32, cnt_ref.shape, 1)
        is_expert = jnp.logical_and(ln >= ROUTER_OFF, ln < ROUTER_OFF + N_EXPERTS)
        tiles = jnp.where(is_expert, jnp.maximum(jnp.floor((cnt_ref[...] + (MOE_TM - 1)) * (1.0 / MOE_TM)), 1.0), 0.0)
        off_scr[...] = MOE_TM * jnp.dot(tiles.astype(BF16), utri_ref[...], preferred_element_type=F32)
        run_scr[...] = jnp.zeros(run_scr.shape, F32)

    before = (jnp.dot(ltri_ref[...], onehot.astype(BF16), preferred_element_type=F32)
              + run_scr[...] + off_scr[...])
    pos1 = jnp.sum(jnp.where(hit1, before, 0.0), axis=-1, keepdims=True)
    pos2 = jnp.sum(jnp.where(hit2, before, 0.0), axis=-1, keepdims=True)
    pos_ref[...] = (jnp.where(lane == 0, pos1, 0.0) + jnp.where(lane == 1, pos2, 0.0)).astype(jnp.int32)
    run_scr[...] += jnp.sum(onehot, axis=0, keepdims=True)


def _positions(info, cnt):
    t = info.shape[0]
    tm = min(t, POS_TM)
    tok = np.arange(tm)
    ltri = jnp.asarray((tok[:, None] > tok[None, :]).astype(np.float32), dtype=BF16)
    ln = np.arange(LANES)
    utri = jnp.asarray((ln[:, None] < ln[None, :]).astype(np.float32), dtype=BF16)
    full = lambda a: pl.BlockSpec(a.shape, lambda i: (0,) * a.ndim)
    return pl.pallas_call(
        _positions_kernel,
        grid=(t // tm,),
        in_specs=[pl.BlockSpec((tm, LANES), lambda i: (i, 0)), full(cnt), full(ltri), full(utri)],
        out_specs=pl.BlockSpec((tm, LANES), lambda i: (i, 0)),
        out_shape=jax.ShapeDtypeStruct((t, LANES), jnp.int32),
        scratch_shapes=[pltpu.VMEM((1, LANES), F32), pltpu.VMEM((1, LANES), F32)],
        compiler_params=_params("arbitrary"),
        name="positions",
    )(info, cnt, ltri, utri)


def _row_copy(src_hbm, src_row, dst_hbm, dst_row, sem):
    return pltpu.make_async_copy(src_hbm.at[pl.ds(src_row, 1)], dst_hbm.at[pl.ds(dst_row, 1)], sem)


SCATTER_SLOTS = 3


def _scatter_kernel(pos1_ref, pos2_ref, last_ref, used_ref, nt_ref, xn_hbm, zero_hbm, xs_hbm,
                    buf, lsem, sem, zsem, *, n_tok):
    max_tiles = xs_hbm.shape[0] // MOE_TM

    def zero_tile(tile):
        return pltpu.make_async_copy(zero_hbm, xs_hbm.at[pl.ds(tile * MOE_TM, MOE_TM)], zsem)

    def for_unused(fn):
        def body(tile, carry):
            fn(tile)
            return carry
        lax.fori_loop(nt_ref[0], max_tiles, body, 0)

    for e in range(N_EXPERTS):
        @pl.when(used_ref[e] > 0)
        def _():
            zero_tile(last_ref[e]).start()
    for_unused(lambda tile: zero_tile(tile).start())
    for e in range(N_EXPERTS):
        @pl.when(used_ref[e] > 0)
        def _():
            zero_tile(last_ref[e]).wait()
    for_unused(lambda tile: zero_tile(tile).wait())

    tm = buf.shape[1]
    n = n_tok // tm

    def load(i):
        return pltpu.make_async_copy(xn_hbm.at[pl.ds(i * tm, tm)], buf.at[i % SCATTER_SLOTS],
                                     lsem.at[i % SCATTER_SLOTS])

    def wait_rows(slot):
        pltpu.make_async_copy(xs_hbm.at[pl.ds(0, 2 * tm)], xs_hbm.at[pl.ds(0, 2 * tm)], sem.at[slot]).wait()

    load(0).start()
    load(1).start()

    def step(i, carry):
        slot = i % SCATTER_SLOTS
        load(i).wait()

        def body(j, c2):
            tok = i * tm + j
            src = buf.at[slot, pl.ds(j, 1)]
            pltpu.make_async_copy(src, xs_hbm.at[pl.ds(pos1_ref[tok], 1)], sem.at[slot]).start()
            pltpu.make_async_copy(src, xs_hbm.at[pl.ds(pos2_ref[tok], 1)], sem.at[slot]).start()
            return c2
        lax.fori_loop(0, tm, body, 0, unroll=DMA_UNROLL)

        @pl.when(i >= 1)
        def _():
            wait_rows((i - 1) % SCATTER_SLOTS)

        @pl.when(i + 2 < n)
        def _():
            load(i + 2).start()
        return carry
    lax.fori_loop(0, n, step, 0)
    wait_rows((n - 1) % SCATTER_SLOTS)


def _scatter_rows(xn, pos1, pos2, last_tile, used, n_tiles, n_rows):
    t = xn.shape[0]
    zero = jnp.zeros((MOE_TM, D_MODEL), F32)
    any_spec = pl.BlockSpec(memory_space=pl.ANY)
    return pl.pallas_call(
        functools.partial(_scatter_kernel, n_tok=t),
        grid_spec=pltpu.PrefetchScalarGridSpec(
            num_scalar_prefetch=5, grid=(1,),
            in_specs=[any_spec, any_spec], out_specs=any_spec,
            scratch_shapes=[pltpu.VMEM((SCATTER_SLOTS, MOE_TM, D_MODEL), F32),
                            pltpu.SemaphoreType.DMA((SCATTER_SLOTS,)),
                            pltpu.SemaphoreType.DMA((SCATTER_SLOTS,)),
                            pltpu.SemaphoreType.DMA]),
        out_shape=jax.ShapeDtypeStruct((n_rows, D_MODEL), F32),
        compiler_params=_params("arbitrary"),
        name="scatter_rows",
    )(pos1, pos2, last_tile, used, n_tiles, xn, zero)


def _experts_kernel(te_ref, tv_ref, nt_ref, xs_ref, wg_hbm, wu_hbm, wd_hbm, xn_new_ref, gate_new_ref,
                    ys_ref, moe_new_ref, wg_s, wu_s, wd_s, wg_f, wu_f, wd_f, wsem):
    i = pl.program_id(0)
    used = i < nt_ref[0]
    expert = te_ref[i]

    def fetch(e):
        slot = e % 2
        return [pltpu.make_async_copy(src.at[e], dst.at[slot], wsem.at[slot, j])
                for j, (src, dst) in enumerate(((wg_hbm, wg_f), (wu_hbm, wu_f), (wd_hbm, wd_f)))]

    @pl.when(jnp.logical_or(i == 0, expert != te_ref[jnp.maximum(i - 1, 0)]))
    def _():
        @pl.when(i == 0)
        def _():
            for c in fetch(expert):
                c.start()
        for c in fetch(expert):
            c.wait()

        @pl.when(expert + 1 < N_EXPERTS)
        def _():
            for c in fetch(expert + 1):
                c.start()
        slot = expert % 2
        wg_s[...] = wg_f[slot].astype(BF16)
        wu_s[...] = wu_f[slot].astype(BF16)
        wd_s[...] = wd_f[slot].astype(BF16)
        xn = xn_new_ref[...]
        lane = lax.broadcasted_iota(jnp.int32, gate_new_ref.shape, 1)
        gate = jnp.sum(jnp.where(lane == expert + ROUTER_OFF, gate_new_ref[...], 0.0), axis=-1, keepdims=True)
        hg = jnp.dot(xn, wg_s[...], preferred_element_type=F32)
        hu = jnp.dot(xn, wu_s[...], preferred_element_type=F32)
        hm = _silu(hg) * hu * gate
        y = jnp.dot(hm.astype(BF16), wd_s[...], preferred_element_type=F32)

        @pl.when(i == 0)
        def _():
            moe_new_ref[...] = y

        @pl.when(i > 0)
        def _():
            moe_new_ref[...] += y

    @pl.when(used)
    def _():
        row = lax.broadcasted_iota(jnp.int32, xs_ref.shape, 0)
        x_lo, x_hi = _unpack_rows(jnp.where(row < tv_ref[i], xs_ref[...], jnp.uint32(0)))
        x_lo = x_lo.astype(BF16)
        x_hi = x_hi.astype(BF16)
        up = lambda w_s: (jnp.dot(x_lo, w_s[:HALF, :], preferred_element_type=F32)
                          + jnp.dot(x_hi, w_s[HALF:, :], preferred_element_type=F32))
        hm = (_silu_tanh(up(wg_s)) * up(wu_s)).astype(BF16)
        ys_ref[...] = _pack_rows(jnp.dot(hm, wd_s[...], preferred_element_type=F32))

    @pl.when(jnp.logical_not(used))
    def _():
        ys_ref[...] = jnp.zeros(ys_ref.shape, U32)


def _experts(xs, tile_expert, tile_valid, n_tiles, wg, wu, wd, xn_new, gate_new):
    max_tiles = xs.shape[0] // MOE_TM
    rows = pl.BlockSpec((MOE_TM, HALF), lambda i, te, tv, nt: (i, 0))
    hbm = pl.BlockSpec(memory_space=pl.ANY)
    full = lambda a: pl.BlockSpec(a.shape, lambda i, te, tv, nt: (0,) * a.ndim)
    return pl.pallas_call(
        _experts_kernel,
        grid_spec=pltpu.PrefetchScalarGridSpec(
            num_scalar_prefetch=3, grid=(max_tiles,),
            in_specs=[rows, hbm, hbm, hbm, full(xn_new), full(gate_new)],
            out_specs=[rows, pl.BlockSpec(xn_new.shape, lambda i, te, tv, nt: (0, 0))],
            scratch_shapes=[pltpu.VMEM((D_MODEL, D_EXPERT), BF16), pltpu.VMEM((D_MODEL, D_EXPERT), BF16),
                            pltpu.VMEM((D_EXPERT, D_MODEL), BF16),
                            pltpu.VMEM((2, D_MODEL, D_EXPERT), F32), pltpu.VMEM((2, D_MODEL, D_EXPERT), F32),
                            pltpu.VMEM((2, D_EXPERT, D_MODEL), F32), pltpu.SemaphoreType.DMA((2, 3))]),
        out_shape=[jax.ShapeDtypeStruct(xs.shape, U32), jax.ShapeDtypeStruct(xn_new.shape, F32)],
        compiler_params=_params("arbitrary"),
        name="experts",
    )(tile_expert, tile_valid, n_tiles, xs, wg, wu, wd, xn_new, gate_new)


def _ple_gather_kernel(pos1_ref, pos2_ref, h_ref, info_ref, p_ref, wpp_ref, wpg_ref, gp_ref, gf_ref,
                       ys_hbm, y_ref, ybuf, sem):
    i = pl.program_id(0)
    n = pl.num_programs(0)
    tm = h_ref.shape[0]

    def issue(tile, slot):
        def body(j, carry):
            tok = tile * tm + j
            pltpu.make_async_copy(ys_hbm.at[pl.ds(pos1_ref[tok], 1)], ybuf.at[slot, 0, pl.ds(j, 1)],
                                  sem.at[slot]).start()
            pltpu.make_async_copy(ys_hbm.at[pl.ds(pos2_ref[tok], 1)], ybuf.at[slot, 1, pl.ds(j, 1)],
                                  sem.at[slot]).start()
            return carry
        lax.fori_loop(0, tm, body, 0, unroll=DMA_UNROLL)

    @pl.when(i == 0)
    def _():
        issue(0, 0)

    @pl.when(i + 1 < n)
    def _():
        issue(i + 1, (i + 1) % 2)

    slot = i % 2
    pltpu.make_async_copy(ybuf.at[slot], ybuf.at[slot], sem.at[slot]).wait()
    info = info_ref[...]
    moe = info[:, INFO_G1:INFO_G1 + 1] * ybuf[slot, 0] + info[:, INFO_G2:INFO_G2 + 1] * ybuf[slot, 1]
    h = h_ref[...] + moe
    hn = _rmsnorm(h, gp_ref[...])
    h = h + _mm(p_ref[...], wpp_ref[...]) * _sigmoid(_mm(hn, wpg_ref[...]))
    y_ref[...] = _rmsnorm(h, gf_ref[...])


def _ple_gather(h, info, p, ys, pos1, pos2, wpp, wpg, gp, gf):
    t = h.shape[0]
    tm = 256
    row = lambda n: pl.BlockSpec((tm, n), lambda i, p1, p2: (i, 0))
    full = lambda a: pl.BlockSpec(a.shape, lambda i, p1, p2: (0,) * a.ndim)
    return pl.pallas_call(
        _ple_gather_kernel,
        grid_spec=pltpu.PrefetchScalarGridSpec(
            num_scalar_prefetch=2, grid=(t // tm,),
            in_specs=[row(D_MODEL), row(LANES), row(PLE_DIM), full(wpp), full(wpg), full(gp), full(gf),
                      pl.BlockSpec(memory_space=pl.ANY)],
            out_specs=row(D_MODEL),
            scratch_shapes=[pltpu.VMEM((2, 2, tm, D_MODEL), F32), pltpu.SemaphoreType.DMA((2,))]),
        out_shape=jax.ShapeDtypeStruct((t, D_MODEL), F32),
        compiler_params=_params("arbitrary"),
        name="ple_gather",
    )(pos1, pos2, h, info, p, wpp, wpg, gp, gf, ys)


SC_IDX = 128
SC_ROWS = 64
SC_WORKERS = 32


def _sc_mesh():
    return plsc.VectorSubcoreMesh(core_axis_name="c", subcore_axis_name="s")


def _sc_windows(t, fn):
    per_worker = t // SC_WORKERS
    worker = lax.axis_index(("c", "s"))

    @pl.loop(0, per_worker // SC_IDX)
    def _(w):
        fn(worker * per_worker + w * SC_IDX)


def _sc_scatter_rows(xn, pos1, pos2, n_rows):
    t, d = xn.shape
    assert t % (SC_WORKERS * SC_IDX) == 0
    idx_t = pltpu.VMEM((1, SC_IDX), jnp.int32)

    @pl.kernel(out_type=jax.ShapeDtypeStruct((n_rows, d), xn.dtype), mesh=_sc_mesh(),
               scratch_types=[idx_t, idx_t, pltpu.VMEM((SC_ROWS, d), xn.dtype)])
    def scatter(x_hbm, p1_hbm, p2_hbm, o_hbm, i1_v, i2_v, buf):
        def window(base):
            pltpu.sync_copy(p1_hbm.at[:, pl.ds(base, SC_IDX)], i1_v)
            pltpu.sync_copy(p2_hbm.at[:, pl.ds(base, SC_IDX)], i2_v)
            for k in range(SC_IDX // SC_ROWS):
                pltpu.sync_copy(x_hbm.at[pl.ds(base + k * SC_ROWS, SC_ROWS)], buf)
                pltpu.sync_copy(buf, o_hbm.at[i1_v.at[0, pl.ds(k * SC_ROWS, SC_ROWS)]])
                pltpu.sync_copy(buf, o_hbm.at[i2_v.at[0, pl.ds(k * SC_ROWS, SC_ROWS)]])
        _sc_windows(t, window)

    return scatter(xn, pos1.reshape(1, t), pos2.reshape(1, t))


def _sc_gather_rows(ys, pos1, pos2):
    t = pos1.shape[0]
    d = ys.shape[1]
    assert t % (SC_WORKERS * SC_IDX) == 0
    idx_t = pltpu.VMEM((1, SC_IDX), jnp.int32)
    out = jax.ShapeDtypeStruct((t, d), ys.dtype)

    buf_t = pltpu.VMEM((SC_ROWS, d), ys.dtype)

    @pl.kernel(out_type=(out, out), mesh=_sc_mesh(),
               scratch_types=[idx_t, idx_t, buf_t, buf_t, pltpu.SemaphoreType.DMA((2,)),
                              pltpu.SemaphoreType.DMA((2,))])
    def gather(y_hbm, p1_hbm, p2_hbm, o1_hbm, o2_hbm, i1_v, i2_v, buf_a, buf_b, gsem, wsem):
        bufs = (buf_a, buf_b)

        def window(base):
            pltpu.sync_copy(p1_hbm.at[:, pl.ds(base, SC_IDX)], i1_v)
            pltpu.sync_copy(p2_hbm.at[:, pl.ds(base, SC_IDX)], i2_v)
            items = [(idx_v, o_hbm, k) for k in range(SC_IDX // SC_ROWS)
                     for idx_v, o_hbm in ((i1_v, o1_hbm), (i2_v, o2_hbm))]

            def read(n):
                idx_v, _, k = items[n]
                return pltpu.make_async_copy(y_hbm.at[idx_v.at[0, pl.ds(k * SC_ROWS, SC_ROWS)]],
                                             bufs[n % 2], gsem.at[n % 2])

            def write(n):
                _, o_hbm, k = items[n]
                return pltpu.make_async_copy(bufs[n % 2], o_hbm.at[pl.ds(base + k * SC_ROWS, SC_ROWS)],
                                             wsem.at[n % 2])

            read(0).start()
            for n in range(len(items)):
                read(n).wait()
                if n >= 1:
                    write(n - 1).wait()
                if n + 1 < len(items):
                    read(n + 1).start()
                write(n).start()
            write(len(items) - 1).wait()
        _sc_windows(t, window)

    return gather(ys, pos1.reshape(1, t), pos2.reshape(1, t))


def _ple_sparse_kernel(h_ref, info_ref, y1_ref, y2_ref, p_ref, wpp_ref, wpg_ref, gp_ref, gf_ref, y_ref):
    info = info_ref[...]
    g1 = info[:, INFO_G1:INFO_G1 + 1]
    g2 = info[:, INFO_G2:INFO_G2 + 1]
    y1_lo, y1_hi = _unpack_rows(y1_ref[...])
    y2_lo, y2_hi = _unpack_rows(y2_ref[...])
    moe = jnp.concatenate([g1 * y1_lo + g2 * y2_lo, g1 * y1_hi + g2 * y2_hi], axis=1)
    h = h_ref[...] + moe
    hn = _rmsnorm(h, gp_ref[...])
    h = h + _mm(p_ref[...], wpp_ref[...]) * _sigmoid(_mm(hn, wpg_ref[...]))
    y_ref[...] = _rmsnorm(h, gf_ref[...])


def _ple_sparse(h, info, y1, y2, p, wpp, wpg, gp, gf):
    t = h.shape[0]
    tm = ROW_TM
    row = lambda n: pl.BlockSpec((tm, n), lambda i: (i, 0))
    full = lambda a: pl.BlockSpec(a.shape, lambda i: (0,) * a.ndim)
    return pl.pallas_call(
        _ple_sparse_kernel,
        grid=(t // tm,),
        in_specs=[row(D_MODEL), row(LANES), row(HALF), row(HALF), row(PLE_DIM),
                  full(wpp), full(wpg), full(gp), full(gf)],
        out_specs=row(D_MODEL),
        out_shape=jax.ShapeDtypeStruct((t, D_MODEL), F32),
        compiler_params=_params("parallel"),
        name="ple_sparse",
    )(h, info, y1, y2, p, wpp, wpg, gp, gf)


def _tile_tables(cnt, max_tiles):
    tiles_e = jnp.maximum((cnt + (MOE_TM - 1)) // MOE_TM, 1)
    ends = jnp.cumsum(tiles_e)
    n_tiles = ends[-1]
    tile = jnp.arange(max_tiles, dtype=jnp.int32)
    idx = jnp.minimum(tile, n_tiles - 1)
    tile_expert = jnp.sum((idx[:, None] >= ends[None, :]).astype(jnp.int32), axis=1)
    mine = tile_expert[:, None] == jnp.arange(N_EXPERTS, dtype=jnp.int32)[None, :]
    of_mine = lambda v: jnp.sum(jnp.where(mine, v[None, :], 0), axis=1)
    valid = jnp.clip(of_mine(cnt) - (idx - of_mine(ends - tiles_e)) * MOE_TM, 0, MOE_TM)
    tile_valid = jnp.where(tile < n_tiles, valid, 0).astype(jnp.int32)
    return (tile_expert, tile_valid, n_tiles.reshape(1), (ends - 1).astype(jnp.int32),
            tiles_e.astype(jnp.int32))


def _ple_final_kernel(h_ref, m_ref, p_ref, wpp_ref, wpg_ref, gp_ref, gf_ref, y_ref):
    h = h_ref[...] + m_ref[...]
    hn = _rmsnorm(h, gp_ref[...])
    h = h + _mm(p_ref[...], wpp_ref[...]) * _sigmoid(_mm(hn, wpg_ref[...]))
    y_ref[...] = _rmsnorm(h, gf_ref[...])


def _ple_final(h, m, p, wpp, wpg, gp, gf):
    t = h.shape[0]
    tm = min(t, 256)
    row = lambda n: pl.BlockSpec((tm, n), lambda i: (i, 0))
    full = lambda a: pl.BlockSpec(a.shape, lambda i: (0,) * a.ndim)
    return pl.pallas_call(
        _ple_final_kernel,
        grid=(t // tm,),
        in_specs=[row(D_MODEL), row(D_MODEL), row(PLE_DIM), full(wpp), full(wpg), full(gp), full(gf)],
        out_specs=row(D_MODEL),
        out_shape=jax.ShapeDtypeStruct((t, D_MODEL), F32),
        compiler_params=_params("parallel"),
        name="ple_final",
    )(h, m, p, wpp, wpg, gp, gf)


def kernel(x_prompt, x_sample, p_prompt, p_sample, cache_k, cache_v, state_conv, state_S, rel_bias, norm_mix, w_in, att_sink, conv_w, dn_A_log, dn_dt_bias, dn_norm, w_out, norm_ffn, w_router_group, w_router_expert, w_gate, w_up, w_down, w_ple_proj, w_ple_gate, norm_ple, norm_final):
    batch, seq, _ = x_prompt.shape
    nseq = x_sample.shape[0]
    assert x_sample.shape[1] == 1 and norm_mix.shape[0] == 1 and cache_k.shape[2] == WINDOW
    assert seq % GDN_TB == 0 and seq % ATT_BLOCK == 0

    wi = w_in[0]
    o_db = ATT_COLS + CONV_CH
    w_in_re = jnp.concatenate(
        [wi[:, :o_db], wi[:, o_db + 2 * DN_HEADS:], wi[:, o_db:o_db + 2 * DN_HEADS],
         jnp.zeros((D_MODEL, LANES - 2 * DN_HEADS), F32)], axis=1).astype(BF16)
    row = lambda a: a.reshape(1, -1).astype(F32)
    pad_lanes = lambda a, off: jnp.zeros((1, LANES), F32).at[0, off:off + a.shape[0]].set(a)
    alog = pad_lanes(dn_A_log[0], DN_HEADS)
    dtb = pad_lanes(dn_dt_bias[0], DN_HEADS)
    dnx = jnp.tile(dn_norm[0], DN_HEADS).reshape(1, DN_WIDTH)
    w_router = jnp.concatenate(
        [w_router_group[0], w_router_expert[0],
         jnp.zeros((D_MODEL, LANES - N_GROUPS - N_EXPERTS), F32)], axis=1).astype(BF16)
    wo = w_out[0].astype(BF16)
    wg, wu, wd = w_gate[0], w_up[0], w_down[0]
    wpp, wpg = w_ple_proj[0].astype(BF16), w_ple_gate[0].astype(BF16)
    sink = att_sink[0]

    qi = np.arange(ATT_BLOCK)[:, None]
    kj = np.arange(2 * ATT_BLOCK)[None, :]
    bucket_p = jnp.asarray(_t5_bucket_np(qi + ATT_BLOCK - kj))
    bucket_s = jnp.asarray(_t5_bucket_np(WINDOW - np.arange(WINDOW)[None, :]))

    xp = x_prompt.reshape(batch * seq, D_MODEL)
    att_p, qkv_p, dz_p, ba_p, xc_tails = _inproj_conv(xp, row(norm_mix[0]), w_in_re, conv_w[0], seq)
    o_att_p = _attn_prompt(att_p, bucket_p, rel_bias, sink, batch, seq)
    o_dn_p, s_p = _gdn_prompt(qkv_p, dz_p, ba_p, alog, dtb, dnx, batch, seq)
    h1, xn2, info, cnt = _route_sparse(xp, o_att_p, o_dn_p, wo, row(norm_ffn[0]), w_router)
    pos = _positions(info, cnt)
    pos1, pos2 = pos[:, 0], pos[:, 1]
    max_tiles = _moe_tiles(batch * seq)
    cnt_e = cnt[0, ROUTER_OFF:ROUTER_OFF + N_EXPERTS].astype(jnp.int32)
    tile_expert, tile_valid, n_tiles, last_tile, used = _tile_tables(cnt_e, max_tiles)
    xs_sorted = _sc_scatter_rows(xn2, pos1, pos2, max_tiles * MOE_TM)

    xs = x_sample.reshape(nseq, D_MODEL)
    att_s, xc_s, dz_s, ba_s = _inproj(xs, row(norm_mix[0]), w_in_re)
    ck_t = jnp.transpose(cache_k[0], (0, 2, 3, 1))
    cv_t = jnp.transpose(cache_v[0], (0, 2, 3, 1))
    o_att_s, ks_t, vs_t = _attn_sample(att_s, ck_t, cv_t, bucket_s, rel_bias, sink)
    sconv_t = jnp.swapaxes(state_conv[0], 0, 1)
    o_dn_s_t, s_s_t = _gdn_sample_lanes(xc_s, dz_s, ba_s, sconv_t, jnp.transpose(state_S[0], (1, 2, 3, 0)),
                                        conv_w[0], alog, dtb, dn_norm[0])
    s_s = jnp.transpose(s_s_t, (3, 0, 1, 2))

    h1_s, xn2_s, gates_s = _outproj_router(xs, o_att_s, o_dn_s_t, wo, row(norm_ffn[0]), w_router)

    ys, moe_s = _experts(xs_sorted, tile_expert, tile_valid, n_tiles, wg, wu, wd, xn2_s, gates_s)
    y1, y2 = _sc_gather_rows(ys, pos1, pos2)
    y_s = _ple_final(h1_s, moe_s, p_sample[0].reshape(nseq, PLE_DIM), wpp, wpg, row(norm_ple[0]),
                     row(norm_final))
    y_p = _ple_sparse(h1, info, y1, y2, p_prompt[0].reshape(batch * seq, PLE_DIM),
                      wpp, wpg, row(norm_ple[0]), row(norm_final))

    att_p3 = att_p.reshape(batch, seq, ATT_COLS)
    kv_shape = (1, batch, WINDOW, ATT_KV_HEADS, HEAD_DIM)
    k_p = att_p3[:, seq - WINDOW:, ATT_WIDTH:ATT_WIDTH + KV_WIDTH].reshape(kv_shape)
    v_p = att_p3[:, seq - WINDOW:, ATT_WIDTH + KV_WIDTH:].reshape(kv_shape)
    conv_p = xc_tails.reshape(batch, -1, TAIL, CONV_CH)[:, -1, TAIL - (CONV_WIDTH - 1):][None]
    k_s = jnp.transpose(ks_t, (0, 3, 1, 2))[None]
    v_s = jnp.transpose(vs_t, (0, 3, 1, 2))[None]
    conv_s = jnp.concatenate([state_conv[0][:, 1:], xc_s[:, None, :]], axis=1)[None]
    return (y_p.reshape(batch, seq, D_MODEL), y_s.reshape(nseq, 1, D_MODEL),
            k_p, v_p, conv_p, s_p[None], k_s, v_s, conv_s, s_s[None])
```

```python
import functools
import math

import numpy as np
import jax
import jax.numpy as jnp
from jax import lax
from jax.experimental import pallas as pl
from jax.experimental.pallas import tpu as pltpu
from jax.experimental.pallas import tpu_sc as plsc

F32 = jnp.float32
BF16 = jnp.bfloat16

D_MODEL = 1024
ATT_HEADS = 8
ATT_KV_HEADS = 2
HEAD_DIM = 64
GQA = ATT_HEADS // ATT_KV_HEADS
WINDOW = 128
ATT_BLOCK = 128
N_BUCKETS = 32
DN_HEADS = 8
DN_DK = 64
DN_DV = 64
CONV_WIDTH = 4
DN_CHUNK = 64
ATT_WIDTH = ATT_HEADS * HEAD_DIM
KV_WIDTH = ATT_KV_HEADS * HEAD_DIM
DN_WIDTH = DN_HEADS * DN_DV
CONV_CH = 3 * DN_WIDTH
N_GROUPS = 4
EXPERTS_PER_GROUP = 8
N_EXPERTS = N_GROUPS * EXPERTS_PER_GROUP
D_EXPERT = 256
PLE_DIM = 256
EPS = 1e-6
NEG_INF = float("-inf")

ATT_COLS = ATT_WIDTH + 2 * KV_WIDTH
LANES = 128
IN_COLS = ATT_COLS + CONV_CH + DN_WIDTH + LANES
ROUTER_OFF = N_GROUPS
VMEM_LIMIT = 48 * 1024 * 1024
ROW_TM = 512


def _params(*sem):
    return pltpu.CompilerParams(dimension_semantics=sem, vmem_limit_bytes=VMEM_LIMIT)


def _mm(a, b):
    return jnp.dot(a.astype(BF16), b.astype(BF16), preferred_element_type=F32)


def _mm_nt(a, b):
    return lax.dot_general(a.astype(BF16), b.astype(BF16), (((1,), (1,)), ((), ())),
                           preferred_element_type=F32)


def _mm_tn(a, b):
    return lax.dot_general(a.astype(BF16), b.astype(BF16), (((0,), (0,)), ((), ())),
                           preferred_element_type=F32)


def _split3(x):
    h1 = x.astype(BF16)
    r1 = x - h1.astype(F32)
    h2 = r1.astype(BF16)
    h3 = (r1 - h2.astype(F32)).astype(BF16)
    return h1, h2, h3


def _mm_sel_rhs(x, sel):
    h1, h2, h3 = _split3(x)
    d = lambda h: jnp.dot(h, sel, preferred_element_type=F32)
    return d(h1) + d(h2) + d(h3)


def _mm_sel_lhs(sel, x):
    h1, h2, h3 = _split3(x)
    d = lambda h: jnp.dot(sel, h, preferred_element_type=F32)
    return d(h1) + d(h2) + d(h3)


def _mm3(a, b):
    ah = a.astype(BF16)
    al = (a - ah.astype(F32)).astype(BF16)
    bh = b.astype(BF16)
    bl = (b - bh.astype(F32)).astype(BF16)
    d = lambda u, v: jnp.dot(u, v, preferred_element_type=F32)
    return d(ah, bh) + d(ah, bl) + d(al, bh)


def _sigmoid(x):
    return 1.0 / (1.0 + jnp.exp(-x))


def _silu(x):
    return x * _sigmoid(x)


def _silu_tanh(x):
    return x * (0.5 * jnp.tanh(0.5 * x) + 0.5)


def _softplus(x):
    return jnp.maximum(x, 0.0) + jnp.log1p(jnp.exp(-jnp.abs(x)))


def _rmsnorm(x, g):
    return x * lax.rsqrt(jnp.mean(x * x, axis=-1, keepdims=True) + EPS) * g


def _t5_bucket_np(dist):
    max_exact = N_BUCKETS // 2
    d = np.maximum(dist, 0)
    ratio = (np.log(np.maximum(d, 1).astype(np.float32) / np.float32(max_exact))
             / np.float32(math.log(WINDOW / max_exact))).astype(np.float32)
    large = np.minimum(max_exact + (ratio * np.float32(N_BUCKETS - max_exact)).astype(np.int32),
                       N_BUCKETS - 1)
    return np.where(d < max_exact, d, large).astype(np.int32)


def _bias_lookup(bucket, rb_ref, h):
    acc = jnp.zeros(bucket.shape, F32)
    for t in range(N_BUCKETS):
        acc = jnp.where(bucket == t, rb_ref[t, h], acc)
    return acc


def _inproj_kernel(x_ref, g_ref, w_ref, att_ref, xc_ref, dz_ref, ba_ref):
    xn = _rmsnorm(x_ref[...], g_ref[...]).astype(BF16)
    o0, o1, o2 = ATT_COLS, ATT_COLS + CONV_CH, ATT_COLS + CONV_CH + DN_WIDTH
    att_ref[...] = jnp.dot(xn, w_ref[:, :o0], preferred_element_type=F32)
    xc_ref[...] = jnp.dot(xn, w_ref[:, o0:o1], preferred_element_type=F32)
    dz_ref[...] = jnp.dot(xn, w_ref[:, o1:o2], preferred_element_type=F32)
    ba_ref[...] = jnp.dot(xn, w_ref[:, o2:], preferred_element_type=F32)


def _inproj(x, g, w):
    t = x.shape[0]
    tm = min(t, ROW_TM)
    row = lambda n: pl.BlockSpec((tm, n), lambda i: (i, 0))
    full = lambda a: pl.BlockSpec(a.shape, lambda i: (0,) * a.ndim)
    return pl.pallas_call(
        _inproj_kernel,
        grid=(t // tm,),
        in_specs=[row(D_MODEL), full(g), full(w)],
        out_specs=[row(ATT_COLS), row(CONV_CH), row(DN_WIDTH), row(LANES)],
        out_shape=[jax.ShapeDtypeStruct((t, n), F32) for n in (ATT_COLS, CONV_CH, DN_WIDTH, LANES)],
        compiler_params=_params("parallel"),
        name="inproj",
    )(x, g, w)


TAIL = 8
PAIR = 2 * DN_DK
N_PAIRS = DN_WIDTH // PAIR


def _head_sums(z, pair_ones):
    hi = z.astype(BF16)
    lw = (z - hi.astype(F32)).astype(BF16)
    d = lambda a, p: jnp.dot(a[:, p * PAIR:(p + 1) * PAIR], pair_ones, preferred_element_type=F32)
    return jnp.concatenate([d(hi, p) + d(lw, p) for p in range(N_PAIRS)], axis=1)


def _inproj_conv_kernel(x_ref, g_ref, w_ref, cw_ref, ones_ref, att_ref, qkv_ref, dz_ref, ba_ref, tail_ref,
                        xp_scr, *, tiles_per_seq):
    tm = x_ref.shape[0]

    @pl.when(pl.program_id(0) % tiles_per_seq == 0)
    def _():
        xp_scr[...] = jnp.zeros((TAIL, CONV_CH), F32)

    xn = _rmsnorm(x_ref[...], g_ref[...]).astype(BF16)
    o0, o1, o2 = ATT_COLS, ATT_COLS + CONV_CH, ATT_COLS + CONV_CH + DN_WIDTH
    xc = jnp.dot(xn, w_ref[:, o0:o1], preferred_element_type=F32)
    att_ref[...] = jnp.dot(xn, w_ref[:, :o0], preferred_element_type=F32)
    dz_ref[...] = jnp.dot(xn, w_ref[:, o1:o2], preferred_element_type=F32)
    ba_ref[...] = jnp.dot(xn, w_ref[:, o2:], preferred_element_type=F32)

    head = jnp.concatenate([xp_scr[...], xc[:TAIL, :]], axis=0)

    def shifted(j):
        return jnp.concatenate([head[TAIL - j:2 * TAIL - j, :], pltpu.roll(xc, j, axis=0)[TAIL:, :]], axis=0)

    y = shifted(3) * cw_ref[0:1, :]
    y = y + shifted(2) * cw_ref[1:2, :]
    y = y + shifted(1) * cw_ref[2:3, :]
    y = y + xc * cw_ref[3:4, :]
    tail = xc[tm - TAIL:, :]
    xp_scr[...] = tail
    tail_ref[0] = tail
    y = _silu_tanh(y)
    q = y[:, :DN_WIDTH]
    k = y[:, DN_WIDTH:2 * DN_WIDTH]
    inv_norm = lax.rsqrt(_head_sums(jnp.concatenate([q * q, k * k], axis=0), ones_ref[...]) + EPS)
    qkv_ref[:, :DN_WIDTH] = q * inv_norm[:tm] * (DN_DK ** -0.5)
    qkv_ref[:, DN_WIDTH:2 * DN_WIDTH] = k * inv_norm[tm:]
    qkv_ref[:, 2 * DN_WIDTH:] = y[:, 2 * DN_WIDTH:]


def _pair_ones():
    lane = np.arange(PAIR)
    return jnp.asarray((lane[:, None] // DN_DV == lane[None, :] // DN_DV).astype(np.float32), dtype=BF16)


def _inproj_conv(x, g, w, conv_w, seq):
    t = x.shape[0]
    tm = ROW_TM
    assert seq % tm == 0
    ones = _pair_ones()
    row = lambda n: pl.BlockSpec((tm, n), lambda i: (i, 0))
    full = lambda a: pl.BlockSpec(a.shape, lambda i: (0,) * a.ndim)
    return pl.pallas_call(
        functools.partial(_inproj_conv_kernel, tiles_per_seq=seq // tm),
        grid=(t // tm,),
        in_specs=[row(D_MODEL), full(g), full(w), full(conv_w), full(ones)],
        out_specs=[row(ATT_COLS), row(CONV_CH), row(DN_WIDTH), row(LANES),
                   pl.BlockSpec((1, TAIL, CONV_CH), lambda i: (i, 0, 0))],
        out_shape=[jax.ShapeDtypeStruct((t, n), F32) for n in (ATT_COLS, CONV_CH, DN_WIDTH, LANES)]
                  + [jax.ShapeDtypeStruct((t // tm, TAIL, CONV_CH), F32)],
        scratch_shapes=[pltpu.VMEM((TAIL, CONV_CH), F32)],
        compiler_params=_params("arbitrary"),
        name="inproj_conv",
    )(x, g, w, conv_w, ones)


GROUP_ROWS = GQA * ATT_BLOCK


def _attn_prompt_kernel(cur_ref, prev_ref, bucket_ref, rb_ref, sink_ref, o_ref, bias_scr, sink_scr):
    i = pl.program_id(0)
    nseq = cur_ref.shape[0]

    @pl.when(i == 0)
    def _():
        qi = lax.broadcasted_iota(jnp.int32, (ATT_BLOCK, 2 * ATT_BLOCK), 0)
        kj = lax.broadcasted_iota(jnp.int32, (ATT_BLOCK, 2 * ATT_BLOCK), 1)
        dist = qi + ATT_BLOCK - kj
        band = jnp.logical_and(dist >= 0, dist < WINDOW)
        bucket = bucket_ref[...]
        hrow = lax.broadcasted_iota(jnp.int32, (GROUP_ROWS, 1), 0) // ATT_BLOCK
        for g in range(ATT_KV_HEADS):
            sink_col = jnp.zeros((GROUP_ROWS, 1), F32)
            for hh in range(GQA):
                h = g * GQA + hh
                bias = jnp.where(band, _bias_lookup(bucket, rb_ref, h), NEG_INF)
                bias_scr[0, g, hh * ATT_BLOCK:(hh + 1) * ATT_BLOCK, :] = bias
                bias_scr[1, g, hh * ATT_BLOCK:(hh + 1) * ATT_BLOCK, :] = jnp.where(kj >= ATT_BLOCK, bias, NEG_INF)
                sink_col = jnp.where(hrow == hh, sink_ref[h], sink_col)
            sink_scr[g] = sink_col

    first = (i == 0).astype(jnp.int32)
    probs = [(b, g) for b in range(nseq) for g in range(ATT_KV_HEADS)]
    scores = []
    for b, g in probs:
        cur = cur_ref[b]
        prev = prev_ref[b]
        q = jnp.concatenate([cur[:, (g * GQA + hh) * HEAD_DIM:(g * GQA + hh + 1) * HEAD_DIM]
                             for hh in range(GQA)], axis=0) * (HEAD_DIM ** -0.5)
        kcol = slice(ATT_WIDTH + g * HEAD_DIM, ATT_WIDTH + (g + 1) * HEAD_DIM)
        k2 = jnp.concatenate([prev[:, kcol], cur[:, kcol]], axis=0)
        scores.append(_mm_nt(q, k2) + bias_scr[first, g])
    probs_p, dens = [], []
    for (b, g), s in zip(probs, scores):
        sink = sink_scr[g]
        m = jnp.maximum(jnp.max(s, axis=-1, keepdims=True), sink)
        p = jnp.exp(s - m)
        dens.append(jnp.sum(p, axis=-1, keepdims=True) + jnp.exp(sink - m))
        probs_p.append(p)
    outs = {}
    for (b, g), p, den in zip(probs, probs_p, dens):
        vcol = slice(ATT_WIDTH + KV_WIDTH + g * HEAD_DIM, ATT_WIDTH + KV_WIDTH + (g + 1) * HEAD_DIM)
        v2 = jnp.concatenate([prev_ref[b][:, vcol], cur_ref[b][:, vcol]], axis=0)
        outs[b, g] = _mm(p, v2) / den
    for b in range(nseq):
        o_ref[b] = jnp.concatenate([outs[b, g][hh * ATT_BLOCK:(hh + 1) * ATT_BLOCK, :]
                                    for g in range(ATT_KV_HEADS) for hh in range(GQA)],
                                   axis=1).astype(o_ref.dtype)


def _attn_prompt(att, bucket, rel_bias, sink, batch, seq):
    nb = seq // ATT_BLOCK
    smem = pl.BlockSpec(memory_space=pltpu.SMEM)
    att3 = att.reshape(batch, seq, ATT_COLS)
    out = pl.pallas_call(
        _attn_prompt_kernel,
        grid=(nb,),
        in_specs=[
            pl.BlockSpec((batch, ATT_BLOCK, ATT_COLS), lambda i: (0, i, 0)),
            pl.BlockSpec((batch, ATT_BLOCK, ATT_COLS), lambda i: (0, jnp.maximum(i - 1, 0), 0)),
            pl.BlockSpec(bucket.shape, lambda i: (0, 0)),
            smem, smem,
        ],
        out_specs=pl.BlockSpec((batch, ATT_BLOCK, ATT_WIDTH), lambda i: (0, i, 0)),
        out_shape=jax.ShapeDtypeStruct((batch, seq, ATT_WIDTH), BF16),
        scratch_shapes=[pltpu.VMEM((2, ATT_KV_HEADS, GROUP_ROWS, 2 * ATT_BLOCK), F32),
                        pltpu.VMEM((ATT_KV_HEADS, GROUP_ROWS, 1), F32)],
        compiler_params=_params("arbitrary"),
        name="attn_prompt",
    )(att3, att3, bucket, rel_bias, sink)
    return out.reshape(batch * seq, ATT_WIDTH)


ATT_S_BB = 8


def _attn_sample_kernel(att_ref, ck_ref, cv_ref, bucket_ref, rb_ref, sink_ref, o_ref, ks_ref, vs_ref,
                        bias_scr, col_scr):
    hrow = lax.broadcasted_iota(jnp.int32, (ATT_HEADS, LANES), 0)
    lane = lax.broadcasted_iota(jnp.int32, (ATT_HEADS, LANES), 1)

    last = (lax.broadcasted_iota(jnp.int32, (3, WINDOW), 1) == WINDOW - 1).astype(BF16)
    is_last = lax.broadcasted_iota(jnp.int32, (KV_WIDTH, WINDOW), 1) == WINDOW - 1

    def shifted(cache_t, new_row):
        pieces = jnp.concatenate([p.astype(F32) for p in _split3(new_row)], axis=0).astype(BF16)
        col = lax.dot_general(pieces, last, (((0,), (0,)), ((), ())), preferred_element_type=F32)
        out = jnp.where(is_last, col, pltpu.roll(cache_t, WINDOW - 1, axis=1))
        return out.reshape(ATT_KV_HEADS, HEAD_DIM, WINDOW)

    for b in range(ATT_S_BB):
        row = att_ref[b:b + 1, :]
        ks_ref[b] = shifted(ck_ref[b].reshape(KV_WIDTH, WINDOW), row[:, ATT_WIDTH:ATT_WIDTH + KV_WIDTH])
        vs_ref[b] = shifted(cv_ref[b].reshape(KV_WIDTH, WINDOW), row[:, ATT_WIDTH + KV_WIDTH:])

    @pl.when(pl.program_id(0) == 0)
    def _():
        bucket = jnp.broadcast_to(bucket_ref[...], (ATT_HEADS, LANES))
        bias = jnp.zeros((ATT_HEADS, LANES), F32)
        cols = jnp.zeros((ATT_HEADS, LANES), F32)
        for h in range(ATT_HEADS):
            bias = jnp.where(hrow == h, _bias_lookup(bucket, rb_ref, h), bias)
            cols = jnp.where(jnp.logical_and(hrow == h, lane == 0), sink_ref[h], cols)
            cols = jnp.where(jnp.logical_and(hrow == h, lane == 1), rb_ref[0, h], cols)
        bias_scr[...] = jnp.where(lane >= 1, bias, NEG_INF)
        col_scr[...] = cols

    bias_c = bias_scr[...]
    sink = col_scr[:, 0:1]
    bias_n = col_scr[:, 1:2]
    same_group = (hrow // GQA) == (lane // HEAD_DIM)
    low_group = lax.broadcasted_iota(jnp.int32, (ATT_HEADS, HEAD_DIM), 0) < GQA
    rnd = lambda a: a.astype(BF16).astype(F32)
    seqs = range(ATT_S_BB)
    rows = [att_ref[b:b + 1, :] for b in seqs]
    q_bds = []
    for row in rows:
        q = row[:, :ATT_WIDTH] * (HEAD_DIM ** -0.5)
        qh = jnp.concatenate([q[:, h * HEAD_DIM:(h + 1) * HEAD_DIM] for h in range(ATT_HEADS)], axis=0)
        q_bds.append(jnp.where(same_group, jnp.concatenate([qh, qh], axis=1), 0.0))
    kv_t = lambda ref, b: ref[b].reshape(KV_WIDTH, WINDOW)
    s_cs = [_mm(q_bd, kv_t(ck_ref, b)) + bias_c for b, q_bd in zip(seqs, q_bds)]
    prs, pns = [], []
    for row, q_bd, s_c in zip(rows, q_bds, s_cs):
        kn = row[:, ATT_WIDTH:ATT_WIDTH + KV_WIDTH]
        s_n = jnp.sum(rnd(q_bd) * rnd(kn), axis=-1, keepdims=True) + bias_n
        m = jnp.maximum(jnp.maximum(jnp.max(s_c, axis=-1, keepdims=True), s_n), sink)
        p_c = jnp.exp(s_c - m)
        p_n = jnp.exp(s_n - m)
        den = jnp.sum(p_c, axis=-1, keepdims=True) + p_n + jnp.exp(sink - m)
        prs.append(p_c / den)
        pns.append(p_n / den)
    pvs = [_mm_nt(pr, kv_t(cv_ref, b)) for b, pr in zip(seqs, prs)]
    for b, row, pv, pn in zip(seqs, rows, pvs, pns):
        vn = row[:, ATT_WIDTH + KV_WIDTH:]
        o_full = pv + rnd(pn) * rnd(vn)
        o_sel = jnp.where(low_group, o_full[:, :HEAD_DIM], o_full[:, HEAD_DIM:])
        o_ref[b:b + 1, :] = jnp.concatenate([o_sel[h:h + 1, :] for h in range(ATT_HEADS)], axis=1)


def _attn_sample(att, ck, cv, bucket, rel_bias, sink):
    nseq = att.shape[0]
    smem = pl.BlockSpec(memory_space=pltpu.SMEM)
    cache = pl.BlockSpec((ATT_S_BB, ATT_KV_HEADS, HEAD_DIM, WINDOW), lambda i: (i, 0, 0, 0))
    return pl.pallas_call(
        _attn_sample_kernel,
        grid=(nseq // ATT_S_BB,),
        in_specs=[pl.BlockSpec((ATT_S_BB, ATT_COLS), lambda i: (i, 0)), cache, cache,
                  pl.BlockSpec(bucket.shape, lambda i: (0, 0)), smem, smem],
        out_specs=[pl.BlockSpec((ATT_S_BB, ATT_WIDTH), lambda i: (i, 0)), cache, cache],
        out_shape=[jax.ShapeDtypeStruct((nseq, ATT_WIDTH), F32),
                   jax.ShapeDtypeStruct(ck.shape, F32), jax.ShapeDtypeStruct(cv.shape, F32)],
        scratch_shapes=[pltpu.VMEM((ATT_HEADS, LANES), F32), pltpu.VMEM((ATT_HEADS, LANES), F32)],
        compiler_params=_params("arbitrary"),
        name="attn_sample",
    )(att, ck, cv, bucket, rel_bias, sink)


GDN_TB = 256
GDN_NC = GDN_TB // DN_CHUNK


def _gdn_gates(ba, alog, dtb):
    beta = _sigmoid(ba)
    g = -jnp.exp(alog) * _softplus(ba + dtb)
    return beta, g


def _pair_diag(x, lo):
    xb = x.astype(BF16)
    zero = jnp.zeros_like(xb)
    return jnp.concatenate([jnp.where(lo, xb, zero), jnp.where(lo, zero, xb)], axis=0)


def _gdn_prompt_kernel(qkv_ref, dz_ref, ba_ref, alog_ref, dtb_ref, dnx_ref,
                       hsum_ref, expb_ref, expg_ref, ltri_ref,
                       o_ref, s_out_ref, s_scr):
    i = pl.program_id(0)
    nb = qkv_ref.shape[0]

    @pl.when(i == 0)
    def _():
        s_scr[...] = jnp.zeros(s_scr.shape, F32)

    hsum = hsum_ref[...]
    ri = lax.broadcasted_iota(jnp.int32, (DN_CHUNK, PAIR), 0)
    ci = lax.broadcasted_iota(jnp.int32, (DN_CHUNK, PAIR), 1)
    lo = ci < DN_DK
    cj = jnp.where(lo, ci, ci - DN_DK)
    causal = ri >= cj
    strict = ri > cj
    eye = (ri == cj).astype(F32)

    def sel2(x, m):
        hi = x.astype(BF16)
        lw = (x - hi.astype(F32)).astype(BF16)
        return (jnp.dot(hi, m, preferred_element_type=F32) + jnp.dot(lw, m, preferred_element_type=F32))

    pre = []
    for b in range(nb):
        q = qkv_ref[b, :, :DN_WIDTH]
        k = qkv_ref[b, :, DN_WIDTH:2 * DN_WIDTH]
        v = qkv_ref[b, :, 2 * DN_WIDTH:]
        beta_c, g_c = _gdn_gates(ba_ref[b], alog_ref[...], dtb_ref[...])
        beta = sel2(beta_c, expb_ref[...])
        gam_c = _mm_sel_lhs(ltri_ref[...], g_c)
        gam = _mm_sel_rhs(gam_c, expg_ref[...])
        gam_t = gam_c.T
        kb = k * beta
        egam = jnp.exp(gam)
        pre.append(dict(q=q, k=k, kb=kb, vb=v * beta, qg=q * egam, wr=kb * egam, gam=gam, gam_t=gam_t))

    probs = [(c, b, p) for c in range(GDN_NC) for b in range(nb) for p in range(N_PAIRS)]
    pick = lambda m: jnp.where(lo, m[:DN_DK], m[DN_DK:])
    rows_of = lambda c: slice(c * DN_CHUNK, (c + 1) * DN_CHUNK)
    sl = lambda name, c, b, p: pre[b][name][rows_of(c), p * PAIR:(p + 1) * PAIR]
    raws = []
    for c, b, p in probs:
        k_p = sl("k", c, b, p)
        k_rows = jnp.concatenate([jnp.where(lo, k_p, 0.0), jnp.where(lo, 0.0, k_p)], axis=0)
        raws.append(_mm_nt(jnp.concatenate([sl("kb", c, b, p), sl("q", c, b, p)], axis=0), k_rows))
    pws, ts, qks = [], [], []
    for (c, b, p), raw in zip(probs, raws):
        gcol = sl("gam", c, b, p)
        h0 = DN_HEADS + 2 * p
        gam_t = pre[b]["gam_t"]
        grow = jnp.concatenate([gam_t[h0:h0 + 1, rows_of(c)], gam_t[h0 + 1:h0 + 2, rows_of(c)]], axis=1)
        decay = jnp.exp(jnp.where(causal, gcol - grow, NEG_INF))
        a = jnp.where(strict, raw[:DN_CHUNK] * decay, 0.0)
        qks.append(jnp.where(causal, raw[DN_CHUNK:] * decay, 0.0))
        pws.append(-a)
        ts.append(eye - a)
    pws = [_mm(pw, _pair_diag(pw, lo)) for pw in pws]
    for _ in range(4):
        rs = [_mm(jnp.concatenate([pw, t], axis=0), _pair_diag(pw, lo)) for pw, t in zip(pws, ts)]
        pws = [r[:DN_CHUNK] for r in rs]
        ts = [t + r[DN_CHUNK:] for t, r in zip(ts, rs)]
    rs = [_mm(t, _pair_diag(pw, lo)) for pw, t in zip(pws, ts)]
    ts = [t + r for t, r in zip(ts, rs)]
    sols = [_mm(t, jnp.concatenate([_pair_diag(sl("vb", c, b, p), lo), _pair_diag(sl("wr", c, b, p), lo)],
                                   axis=1)) for (c, b, p), t in zip(probs, ts)]
    qkuws = [_mm(qk, jnp.concatenate([_pair_diag(s[:, :PAIR], lo), _pair_diag(s[:, PAIR:], lo)], axis=1))
             for qk, s in zip(qks, sols)]
    crosses, gls = [], []
    for (c, b, p), s in zip(probs, sols):
        last = (c + 1) * DN_CHUNK - 1
        gam_last = pre[b]["gam"][last:last + 1, p * PAIR:(p + 1) * PAIR]
        kd = sl("k", c, b, p) * jnp.exp(gam_last - sl("gam", c, b, p))
        crosses.append(_mm_tn(kd, s))
        gls.append(jnp.exp(gam_last))
    lhs = [jnp.concatenate([pick(cr[:, PAIR:]), sl("qg", c, b, p) - qkuw[:, PAIR:]], axis=0)
           for (c, b, p), cr, qkuw in zip(probs, crosses, qkuws)]

    o_rows = [[] for _ in range(nb)]
    per_chunk = nb * N_PAIRS
    for c in range(GDN_NC):
        sel = slice(c * per_chunk, (c + 1) * per_chunk)
        s_olds = [s_scr[b, p] for _, b, p in probs[sel]]
        rs = [_mm(l, _pair_diag(s_old, lo)) for l, s_old in zip(lhs[sel], s_olds)]
        o_pairs = [[] for _ in range(nb)]
        for (_, b, p), r, s_old, gl, cr, qkuw in zip(probs[sel], rs, s_olds, gls[sel], crosses[sel], qkuws[sel]):
            s_scr[b, p] = gl * s_old - r[:DN_DK] + pick(cr[:, :PAIR])
            o_pairs[b].append(r[DN_DK:] + qkuw[:, :PAIR])
        for b in range(nb):
            o_rows[b].append(jnp.concatenate(o_pairs[b], axis=1))

    o_all = jnp.concatenate([jnp.concatenate(rows, axis=0) for rows in o_rows], axis=0)
    inv_rms = lax.rsqrt(_head_sums(o_all * o_all, hsum) * (1.0 / DN_DV) + EPS)
    for b in range(nb):
        rows = slice(b * GDN_TB, (b + 1) * GDN_TB)
        o_ref[b] = (o_all[rows] * inv_rms[rows] * dnx_ref[...] * _silu_tanh(dz_ref[b])).astype(o_ref.dtype)

    @pl.when(i == pl.num_programs(0) - 1)
    def _():
        for b in range(nb):
            for p in range(N_PAIRS):
                s_p = s_scr[b, p]
                s_out_ref[b, 2 * p] = s_p[:, :DN_DV]
                s_out_ref[b, 2 * p + 1] = s_p[:, DN_DV:]


def _gdn_consts():
    lane = np.arange(DN_WIDTH)
    pl_lane = np.arange(PAIR)
    hsum = (pl_lane[:, None] // DN_DV == pl_lane[None, :] // DN_DV)
    src = np.arange(LANES)
    expb = (src[:, None] == lane[None, :] // DN_DV)
    expg = (src[:, None] == DN_HEADS + lane[None, :] // DN_DV)
    tok = np.arange(GDN_TB)
    ltri = np.logical_and(tok[:, None] >= tok[None, :],
                          tok[:, None] // DN_CHUNK == tok[None, :] // DN_CHUNK)
    as_bf16 = lambda m: jnp.asarray(m.astype(np.float32), dtype=BF16)
    return as_bf16(hsum), as_bf16(expb), as_bf16(expg), as_bf16(ltri)


def _gdn_prompt(xc, dz, ba, alog, dtb, dnx, batch, seq):
    nt = seq // GDN_TB
    hsum, expb, expg, ltri = _gdn_consts()
    row = lambda n: pl.BlockSpec((batch, GDN_TB, n), lambda i: (0, i, 0))
    full = lambda a: pl.BlockSpec(a.shape, lambda i: (0,) * a.ndim)
    consts = (alog, dtb, dnx, hsum, expb, expg, ltri)
    as3d = lambda a: a.reshape(batch, seq, a.shape[-1])
    o, s = pl.pallas_call(
        _gdn_prompt_kernel,
        grid=(nt,),
        in_specs=[row(CONV_CH), row(DN_WIDTH), row(LANES)] + [full(a) for a in consts],
        out_specs=[row(DN_WIDTH),
                   pl.BlockSpec((batch, DN_HEADS, DN_DK, DN_DV), lambda i: (0, 0, 0, 0))],
        out_shape=[jax.ShapeDtypeStruct((batch, seq, DN_WIDTH), BF16),
                   jax.ShapeDtypeStruct((batch, DN_HEADS, DN_DK, DN_DV), F32)],
        scratch_shapes=[pltpu.VMEM((batch, N_PAIRS, DN_DK, PAIR), F32)],
        compiler_params=_params("arbitrary"),
        name="gdn_prompt",
    )(as3d(xc), as3d(dz), as3d(ba), *consts)
    return o.reshape(batch * seq, DN_WIDTH), s


GDN_S_BB = 8


def _gdn_sample_kernel(xc_ref, dz_ref, ba_ref, sc_ref, s_ref, cw_ref, alog_ref, dtb_ref, dn_ref,
                       hsum_ref, eye_ref, hsel_ref, hrep3_ref, o_ref, s_out_ref):
    xc = xc_ref[...]
    y = sc_ref[0] * cw_ref[0:1, :]
    y = y + sc_ref[1] * cw_ref[1:2, :]
    y = y + sc_ref[2] * cw_ref[2:3, :]
    y = _silu(y + xc * cw_ref[3:4, :])
    hsum = hsum_ref[...]
    q = y[:, :DN_WIDTH]
    k = y[:, DN_WIDTH:2 * DN_WIDTH]
    v = y[:, 2 * DN_WIDTH:]
    q = q * lax.rsqrt(_mm_sel_rhs(q * q, hsum) + EPS) * (DN_DK ** -0.5)
    k = k * lax.rsqrt(_mm_sel_rhs(k * k, hsum) + EPS)
    beta_c, g_c = _gdn_gates(ba_ref[...], alog_ref[...], dtb_ref[...])
    eg_c = jnp.exp(g_c)
    eye = eye_ref[...]
    tr = lambda a: lax.dot_general(a, eye, (((0,), (0,)), ((), ())), precision=lax.Precision.HIGHEST,
                                   preferred_element_type=F32)
    gates_t = tr(jnp.concatenate([beta_c, eg_c], axis=1))
    beta_t = gates_t[:LANES]
    eg_t = gates_t[LANES:]
    dz = dz_ref[...]
    dn = dn_ref[...]
    split = lambda r: jnp.concatenate([r[:, h * DN_DV:(h + 1) * DN_DV] for h in range(DN_HEADS)], axis=0)
    own_head = hsel_ref[...].astype(F32)
    hrep3 = hrep3_ref[...]
    seqs = range(GDN_S_BB)
    dot = lambda a, b: jnp.dot(a.astype(BF16), b.astype(BF16), preferred_element_type=F32)

    def pieces(x):
        p1 = x.astype(BF16).astype(F32)
        r1 = x - p1
        p2 = r1.astype(BF16).astype(F32)
        return p1, p2, (r1 - p2).astype(BF16).astype(F32)

    heads = DN_HEADS
    k_pieces, kqs = [], []
    for b in seqs:
        kq_bd = jnp.concatenate([own_head * k[b:b + 1, :], own_head * q[b:b + 1, :]], axis=0)
        a1, a2, a3 = pieces(kq_bd)
        s1, s2, s3 = pieces(s_ref[b])
        r1 = dot(jnp.concatenate([a1, a2, a3], axis=0), s1)
        r2 = dot(jnp.concatenate([a1, a2], axis=0), s2)
        r3 = dot(a1, s3)
        n = 2 * heads
        kqs.append(((r3 + r2[n:] + r1[2 * n:]) + (r2[:n] + r1[n:2 * n])) + r1[:n])
        k_pieces.append((a1[:heads], a2[:heads], a3[:heads]))
    egs = [eg_t[DN_HEADS:2 * DN_HEADS, b:b + 1] for b in seqs]
    qks = [jnp.sum(split(q[b:b + 1, :]) * split(k[b:b + 1, :]), axis=-1, keepdims=True) for b in seqs]
    v_news = [beta_t[0:DN_HEADS, b:b + 1] * (split(v[b:b + 1, :]) - eg * kq[:heads])
              for b, eg, kq in zip(seqs, egs, kqs)]
    os_ = [eg * kq[heads:] + qk * v_new for eg, kq, qk, v_new in zip(egs, kqs, qks, v_news)]
    inv_rms = [lax.rsqrt(jnp.mean(o * o, axis=-1, keepdims=True) + EPS) for o in os_]
    for b, o, r in zip(seqs, os_, inv_rms):
        o_ref[b] = o * r * dn * _silu(split(dz[b:b + 1, :]))
    outers, egrows = [], []
    for (k1, k2, k3), v_new, eg in zip(k_pieces, v_news, egs):
        v1, v2, v3 = pieces(v_new)
        lhs = jnp.concatenate([k1, k1, k2, k1, k2, k3], axis=0).astype(BF16)
        rhs = jnp.concatenate([v1, v2, v1, v3, v2, v1], axis=0).astype(BF16)
        outers.append(lax.dot_general(lhs, rhs, (((0,), (0,)), ((), ())), preferred_element_type=F32))
        egrows.append(dot(hrep3, jnp.concatenate(pieces(jnp.broadcast_to(eg, (DN_HEADS, DN_DV))), axis=0)))
    for b, outer, egrow in zip(seqs, outers, egrows):
        s_out_ref[b] = s_ref[b] * egrow + outer


def _gdn_sample(xc, dz, ba, sconv_t, state, conv_w, alog, dtb, dn):
    nseq = xc.shape[0]
    lane = np.arange(DN_WIDTH)
    hsum = jnp.asarray((lane[:, None] // DN_DV == lane[None, :] // DN_DV).astype(np.float32), dtype=BF16)
    eye = jnp.eye(GDN_S_BB, dtype=F32)
    hsel_np = (np.arange(DN_HEADS)[:, None] == lane[None, :] // DN_DK).astype(np.float32)
    hsel = jnp.asarray(hsel_np, dtype=BF16)
    hrep3 = jnp.asarray(np.tile(hsel_np.T, (1, 3)), dtype=BF16)
    row = lambda n: pl.BlockSpec((GDN_S_BB, n), lambda i: (i, 0))
    full = lambda a: pl.BlockSpec(a.shape, lambda i: (0,) * a.ndim)
    st = pl.BlockSpec((GDN_S_BB, DN_HEADS * DN_DK, DN_DV), lambda i: (i, 0, 0))
    consts = (conv_w, alog, dtb, dn, hsum, eye, hsel, hrep3)
    return pl.pallas_call(
        _gdn_sample_kernel,
        grid=(nseq // GDN_S_BB,),
        in_specs=[row(CONV_CH), row(DN_WIDTH), row(LANES),
                  pl.BlockSpec((CONV_WIDTH - 1, GDN_S_BB, CONV_CH), lambda i: (0, i, 0)), st]
                 + [full(a) for a in consts],
        out_specs=[pl.BlockSpec((GDN_S_BB, DN_HEADS, DN_DV), lambda i: (i, 0, 0)), st],
        out_shape=[jax.ShapeDtypeStruct((nseq, DN_HEADS, DN_DV), F32),
                   jax.ShapeDtypeStruct(state.shape, F32)],
        compiler_params=_params("parallel"),
        name="gdn_sample",
    )(xc, dz, ba, sconv_t, state, *consts)


def _attn_sample_lanes_kernel(att_ref, ck_ref, cv_ref, bucket_ref, rb_ref, sink_ref, o_ref, s_scr):
    g = pl.program_id(0)
    nseq = att_ref.shape[0]
    rnd = lambda a: a.astype(BF16).astype(F32)
    att = att_ref[...]
    q_all_t = (att[:, :ATT_WIDTH] * (HEAD_DIM ** -0.5)).T
    kv_new_t = att[:, ATT_WIDTH:].T
    qsel = [jnp.where(g == 0, q_all_t[hh * HEAD_DIM:(hh + 1) * HEAD_DIM],
                      q_all_t[(GQA + hh) * HEAD_DIM:(GQA + hh + 1) * HEAD_DIM]) for hh in range(GQA)]
    qr = [rnd(q) for q in qsel]
    kn = rnd(jnp.where(g == 0, kv_new_t[0:HEAD_DIM], kv_new_t[HEAD_DIM:2 * HEAD_DIM]))
    vn = rnd(jnp.where(g == 0, kv_new_t[2 * HEAD_DIM:3 * HEAD_DIM], kv_new_t[3 * HEAD_DIM:]))

    def score_row(j, carry):
        kj = rnd(ck_ref[j, 0])
        for hh in range(GQA):
            s_scr[hh, pl.ds(j, 1), :] = jnp.sum(qr[hh] * kj, axis=0, keepdims=True)
        return carry
    lax.fori_loop(0, WINDOW, score_row, 0, unroll=2)

    bucket = bucket_ref[...]
    jrow = lax.broadcasted_iota(jnp.int32, (WINDOW, nseq), 0)
    prn = []
    for hh in range(GQA):
        h = g * GQA + hh
        bias = jnp.where(jrow >= 1, _bias_lookup(bucket, rb_ref, h), NEG_INF)
        s = s_scr[hh] + bias
        s_n = jnp.sum(qr[hh] * kn, axis=0, keepdims=True) + rb_ref[0, h]
        sink = sink_ref[h]
        m = jnp.maximum(jnp.maximum(jnp.max(s, axis=0, keepdims=True), s_n), sink)
        p = jnp.exp(s - m)
        p_n = jnp.exp(s_n - m)
        den = jnp.sum(p, axis=0, keepdims=True) + p_n + jnp.exp(sink - m)
        s_scr[hh] = rnd(p / den)
        prn.append(rnd(p_n / den))

    def value_row(j, acc):
        vj = rnd(cv_ref[j, 0])
        return tuple(acc[hh] + s_scr[hh, pl.ds(j, 1), :] * vj for hh in range(GQA))
    zero = jnp.zeros((HEAD_DIM, nseq), F32)
    acc = lax.fori_loop(0, WINDOW, value_row, (zero,) * GQA, unroll=2)
    for hh in range(GQA):
        o_ref[hh * HEAD_DIM:(hh + 1) * HEAD_DIM, :] = acc[hh] + prn[hh] * vn


def _attn_sample_lanes(att, ck_t, cv_t, rel_bias, sink):
    nseq = att.shape[0]
    assert nseq == LANES
    bucket = jnp.asarray(np.broadcast_to(_t5_bucket_np(WINDOW - np.arange(WINDOW))[:, None], (WINDOW, nseq)))
    smem = pl.BlockSpec(memory_space=pltpu.SMEM)
    cache = pl.BlockSpec((WINDOW, 1, HEAD_DIM, nseq), lambda g: (0, g, 0, 0))
    full = lambda a: pl.BlockSpec(a.shape, lambda g: (0,) * a.ndim)
    return pl.pallas_call(
        _attn_sample_lanes_kernel,
        grid=(ATT_KV_HEADS,),
        in_specs=[full(att), cache, cache, full(bucket), smem, smem],
        out_specs=pl.BlockSpec((GQA * HEAD_DIM, nseq), lambda g: (g, 0)),
        out_shape=jax.ShapeDtypeStruct((ATT_WIDTH, nseq), F32),
        scratch_shapes=[pltpu.VMEM((GQA, WINDOW, nseq), F32)],
        compiler_params=_params("arbitrary"),
        name="attn_sample_lanes",
    )(att, ck_t, cv_t, bucket, rel_bias, sink)


def _gdn_sample_front_kernel(xc_ref, dz_ref, ba_ref, sc_ref, cw_ref, alog_ref, dtb_ref, hsum_ref,
                             q_ref, k_ref, v_ref, dz_t_ref, gates_ref):
    xc = xc_ref[...]
    y = sc_ref[0] * cw_ref[0:1, :]
    y = y + sc_ref[1] * cw_ref[1:2, :]
    y = y + sc_ref[2] * cw_ref[2:3, :]
    y = _silu(y + xc * cw_ref[3:4, :])
    hsum = hsum_ref[...]
    q = y[:, :DN_WIDTH]
    k = y[:, DN_WIDTH:2 * DN_WIDTH]
    q = q * lax.rsqrt(_mm_sel_rhs(q * q, hsum) + EPS) * (DN_DK ** -0.5)
    k = k * lax.rsqrt(_mm_sel_rhs(k * k, hsum) + EPS)
    beta_c, g_c = _gdn_gates(ba_ref[...], alog_ref[...], dtb_ref[...])
    q_ref[...] = q.T
    k_ref[...] = k.T
    v_ref[...] = y[:, 2 * DN_WIDTH:].T
    dz_t_ref[...] = dz_ref[...].T
    gates_ref[0:LANES, :] = beta_c.T
    gates_ref[LANES:, :] = jnp.exp(g_c).T


def _gdn_sample_step_kernel(q_ref, k_ref, v_ref, dz_ref, gates_ref, dn_ref, s_ref, o_ref, s_out_ref):
    h = pl.program_id(0)
    beta = gates_ref[pl.ds(h, 1), :]
    eg = gates_ref[pl.ds(LANES + DN_HEADS + h, 1), :]
    q, k, v = q_ref[...], k_ref[...], v_ref[...]
    w = (k * beta) * eg
    qg = q * eg
    ws = jnp.zeros(v.shape, F32)
    qs = jnp.zeros(v.shape, F32)
    for dk in range(DN_DK):
        s_dk = s_ref[0, dk]
        ws = ws + w[dk:dk + 1, :] * s_dk
        qs = qs + qg[dk:dk + 1, :] * s_dk
    v_new = v * beta - ws
    qk = jnp.sum(q * k, axis=0, keepdims=True)
    o = qs + qk * v_new
    for dk in range(DN_DK):
        s_out_ref[0, dk] = s_ref[0, dk] * eg + k[dk:dk + 1, :] * v_new
    o = o * lax.rsqrt(jnp.mean(o * o, axis=0, keepdims=True) + EPS) * dn_ref[...]
    o_ref[...] = o * _silu(dz_ref[...])


def _gdn_sample_lanes(xc, dz, ba, sconv_t, state_t, conv_w, alog, dtb, dn):
    nseq = xc.shape[0]
    assert nseq == LANES
    lane = np.arange(DN_WIDTH)
    hsum = jnp.asarray((lane[:, None] // DN_DV == lane[None, :] // DN_DV).astype(np.float32), dtype=BF16)
    full = lambda a: pl.BlockSpec(a.shape, lambda i: (0,) * a.ndim)
    cm = jax.ShapeDtypeStruct((DN_WIDTH, nseq), F32)
    front_in = (xc, dz, ba, sconv_t, conv_w, alog, dtb, hsum)
    q_t, k_t, v_t, dz_t, gates_t = pl.pallas_call(
        _gdn_sample_front_kernel,
        grid=(1,),
        in_specs=[full(a) for a in front_in],
        out_specs=[pl.BlockSpec((DN_WIDTH, nseq), lambda i: (0, 0))] * 4
                  + [pl.BlockSpec((2 * LANES, nseq), lambda i: (0, 0))],
        out_shape=[cm, cm, cm, cm, jax.ShapeDtypeStruct((2 * LANES, nseq), F32)],
        compiler_params=_params("arbitrary"),
        name="gdn_sample_front",
    )(*front_in)
    dn_b = jnp.broadcast_to(dn.reshape(DN_DV, 1), (DN_DV, nseq))
    head = pl.BlockSpec((DN_DK, nseq), lambda h: (h, 0))
    st = pl.BlockSpec((1, DN_DK, DN_DV, nseq), lambda h: (h, 0, 0, 0))
    return pl.pallas_call(
        _gdn_sample_step_kernel,
        grid=(DN_HEADS,),
        in_specs=[head, head, head, head, full(gates_t), full(dn_b), st],
        out_specs=[head, st],
        out_shape=[cm, jax.ShapeDtypeStruct(state_t.shape, F32)],
        compiler_params=_params("parallel"),
        name="gdn_sample_step",
    )(q_t, k_t, v_t, dz_t, gates_t, dn_b, state_t)


def _route(xn, wr):
    logits = jnp.dot(xn, wr, preferred_element_type=F32)
    lane = lax.broadcasted_iota(jnp.int32, logits.shape, 1).astype(F32)
    first_at = lambda hit: jnp.min(jnp.where(hit, lane, float(LANES)), axis=-1, keepdims=True)
    glog = jnp.where(lane < N_GROUPS, logits, NEG_INF)
    gmax = jnp.max(glog, axis=-1, keepdims=True)
    gsel = first_at(glog == gmax)
    pgsel = 1.0 / jnp.sum(jnp.exp(glog - gmax), axis=-1, keepdims=True)
    lo = ROUTER_OFF + gsel * EXPERTS_PER_GROUP
    in_group = jnp.logical_and(lane >= lo, lane < lo + EXPERTS_PER_GROUP)
    elog = jnp.where(in_group, logits, NEG_INF)
    m1 = jnp.max(elog, axis=-1, keepdims=True)
    i1 = first_at(elog == m1)
    z = jnp.sum(jnp.exp(elog - m1), axis=-1, keepdims=True)
    elog2 = jnp.where(lane == i1, NEG_INF, elog)
    m2 = jnp.max(elog2, axis=-1, keepdims=True)
    i2 = first_at(elog2 == m2)
    p1 = 1.0 / z
    p2 = jnp.exp(m2 - m1) / z
    tot = p1 + p2
    return lane, i1, i2, p1 / tot * pgsel, p2 / tot * pgsel


def _outproj(x_ref, oa_ref, od_ref, wo_ref):
    return x_ref[...] + _mm(oa_ref[...], wo_ref[:ATT_WIDTH, :]) + _mm(od_ref[...], wo_ref[ATT_WIDTH:, :])


def _outproj_router_kernel(x_ref, oa_ref, od_t_ref, wo_ref, g_ref, wr_ref, h_ref, xn_ref, gate_ref):
    h = (x_ref[...] + _mm(oa_ref[...], wo_ref[:ATT_WIDTH, :])
         + _mm(od_t_ref[...].T, wo_ref[ATT_WIDTH:, :]))
    h_ref[...] = h
    xn = _rmsnorm(h, g_ref[...]).astype(BF16)
    xn_ref[...] = xn
    lane, i1, i2, g1, g2 = _route(xn, wr_ref[...])
    gate_ref[...] = jnp.where(lane == i1, g1, 0.0) + jnp.where(lane == i2, g2, 0.0)


def _outproj_router(x, oa, od_t, wo, g, wr):
    t = x.shape[0]
    tm = t
    row = lambda n: pl.BlockSpec((tm, n), lambda i: (i, 0))
    full = lambda a: pl.BlockSpec(a.shape, lambda i: (0,) * a.ndim)
    return pl.pallas_call(
        _outproj_router_kernel,
        grid=(t // tm,),
        in_specs=[row(D_MODEL), row(ATT_WIDTH), full(od_t), full(wo), full(g), full(wr)],
        out_specs=[row(D_MODEL), row(D_MODEL), row(LANES)],
        out_shape=[jax.ShapeDtypeStruct((t, D_MODEL), F32), jax.ShapeDtypeStruct((t, D_MODEL), BF16),
                   jax.ShapeDtypeStruct((t, LANES), F32)],
        compiler_params=_params("parallel"),
        name="outproj_router",
    )(x, oa, od_t, wo, g, wr)


def _moe_kernel(xn_ref, gate_ref, wg_ref, wu_ref, wd_ref, o_ref):
    e = pl.program_id(1)
    xn = xn_ref[...]
    lane = lax.broadcasted_iota(jnp.int32, gate_ref.shape, 1)
    gate = jnp.sum(jnp.where(lane == e + ROUTER_OFF, gate_ref[...], 0.0), axis=-1, keepdims=True)
    hg = jnp.dot(xn, wg_ref[...].astype(BF16), preferred_element_type=F32)
    hu = jnp.dot(xn, wu_ref[...].astype(BF16), preferred_element_type=F32)
    hm = _silu(hg) * hu * gate
    y = jnp.dot(hm.astype(BF16), wd_ref[...].astype(BF16), preferred_element_type=F32)

    @pl.when(e == 0)
    def _():
        o_ref[...] = y

    @pl.when(e > 0)
    def _():
        o_ref[...] += y


def _moe(xn, gates, wg, wu, wd):
    t = xn.shape[0]
    tm = min(t, 1024)
    return pl.pallas_call(
        _moe_kernel,
        grid=(t // tm, N_EXPERTS),
        in_specs=[pl.BlockSpec((tm, D_MODEL), lambda i, e: (i, 0)),
                  pl.BlockSpec((tm, LANES), lambda i, e: (i, 0)),
                  pl.BlockSpec((None, D_MODEL, D_EXPERT), lambda i, e: (e, 0, 0)),
                  pl.BlockSpec((None, D_MODEL, D_EXPERT), lambda i, e: (e, 0, 0)),
                  pl.BlockSpec((None, D_EXPERT, D_MODEL), lambda i, e: (e, 0, 0))],
        out_specs=pl.BlockSpec((tm, D_MODEL), lambda i, e: (i, 0)),
        out_shape=jax.ShapeDtypeStruct((t, D_MODEL), F32),
        compiler_params=_params("parallel", "arbitrary"),
        name="moe",
    )(xn, gates, wg, wu, wd)


MOE_TM = 512
POS_TM = 1024
INFO_G1, INFO_G2, INFO_E1, INFO_E2 = 0, 1, 2, 3
DMA_UNROLL = 8


def _moe_tiles(t):
    return (2 * t) // MOE_TM + N_EXPERTS


HALF = D_MODEL // 2
U32 = jnp.uint32


def _pack_rows(x):
    bits = lambda v: lax.bitcast_convert_type(v.astype(BF16).astype(F32), U32)
    return bits(x[:, HALF:]) | (bits(x[:, :HALF]) >> 16)


def _unpack_rows(w):
    lo = lax.bitcast_convert_type(w << 16, F32)
    hi = lax.bitcast_convert_type(w & jnp.uint32(0xFFFF0000), F32)
    return lo, hi


def _route_kernel(x_ref, oa_ref, od_ref, wo_ref, g_ref, wr_ref, h_ref, xn_ref, info_ref, cnt_ref, run_scr):
    h = _outproj(x_ref, oa_ref, od_ref, wo_ref)
    h_ref[...] = h
    xn = _rmsnorm(h, g_ref[...])
    xn_ref[...] = _pack_rows(xn)
    lane, i1, i2, g1, g2 = _route(xn.astype(BF16), wr_ref[...])
    info = jnp.where(lane == INFO_G1, g1, 0.0) + jnp.where(lane == INFO_G2, g2, 0.0)
    info = info + jnp.where(lane == INFO_E1, i1, 0.0) + jnp.where(lane == INFO_E2, i2, 0.0)
    info_ref[...] = info

    @pl.when(pl.program_id(0) == 0)
    def _():
        run_scr[...] = jnp.zeros(run_scr.shape, F32)
    picked = jnp.logical_or(lane == i1, lane == i2).astype(F32)
    run_scr[...] += jnp.sum(picked, axis=0, keepdims=True)
    cnt_ref[...] = run_scr[...]


def _route_sparse(x, oa, od, wo, g, wr):
    t = x.shape[0]
    tm = ROW_TM
    row = lambda n: pl.BlockSpec((tm, n), lambda i: (i, 0))
    full = lambda a: pl.BlockSpec(a.shape, lambda i: (0,) * a.ndim)
    return pl.pallas_call(
        _route_kernel,
        grid=(t // tm,),
        in_specs=[row(D_MODEL), row(ATT_WIDTH), row(DN_WIDTH), full(wo), full(g), full(wr)],
        out_specs=[row(D_MODEL), row(HALF), row(LANES), pl.BlockSpec((1, LANES), lambda i: (0, 0))],
        out_shape=[jax.ShapeDtypeStruct((t, D_MODEL), F32), jax.ShapeDtypeStruct((t, HALF), U32),
                   jax.ShapeDtypeStruct((t, LANES), F32), jax.ShapeDtypeStruct((1, LANES), F32)],
        scratch_shapes=[pltpu.VMEM((1, LANES), F32)],
        compiler_params=_params("arbitrary"),
        name="route",
    )(x, oa, od, wo, g, wr)


def _positions_kernel(info_ref, cnt_ref, ltri_ref, utri_ref, pos_ref, run_scr, off_scr):
    info = info_ref[...]
    lane = lax.broadcasted_iota(jnp.int32, info.shape, 1).astype(F32)
    hit1 = lane == info[:, INFO_E1:INFO_E1 + 1]
    hit2 = lane == info[:, INFO_E2:INFO_E2 + 1]
    onehot = jnp.logical_or(hit1, hit2).astype(F32)

    @pl.when(pl.program_id(0) == 0)
    def _():
        ln = lax.broadcasted_iota(jnp.int32, cnt_ref.shape, 1)
        is_expert = jnp.logical_and(ln >= ROUTER_OFF, ln < ROUTER_OFF + N_EXPERTS)
        tiles = jnp.where(is_expert, jnp.maximum(jnp.floor((cnt_ref[...] + (MOE_TM - 1)) * (1.0 / MOE_TM)), 1.0), 0.0)
        off_scr[...] = MOE_TM * jnp.dot(tiles.astype(BF16), utri_ref[...], preferred_element_type=F32)
        run_scr[...] = jnp.zeros(run_scr.shape, F32)

    before = (jnp.dot(ltri_ref[...], onehot.astype(BF16), preferred_element_type=F32)
              + run_scr[...] + off_scr[...])
    pos1 = jnp.sum(jnp.where(hit1, before, 0.0), axis=-1, keepdims=True)
    pos2 = jnp.sum(jnp.where(hit2, before, 0.0), axis=-1, keepdims=True)
    pos_ref[...] = (jnp.where(lane == 0, pos1, 0.0) + jnp.where(lane == 1, pos2, 0.0)).astype(jnp.int32)
    run_scr[...] += jnp.sum(onehot, axis=0, keepdims=True)


def _positions(info, cnt):
    t = info.shape[0]
    tm = min(t, POS_TM)
    tok = np.arange(tm)
    ltri = jnp.asarray((tok[:, None] > tok[None, :]).astype(np.float32), dtype=BF16)
    ln = np.arange(LANES)
    utri = jnp.asarray((ln[:, None] < ln[None, :]).astype(np.float32), dtype=BF16)
    full = lambda a: pl.BlockSpec(a.shape, lambda i: (0,) * a.ndim)
    return pl.pallas_call(
        _positions_kernel,
        grid=(t // tm,),
        in_specs=[pl.BlockSpec((tm, LANES), lambda i: (i, 0)), full(cnt), full(ltri), full(utri)],
        out_specs=pl.BlockSpec((tm, LANES), lambda i: (i, 0)),
        out_shape=jax.ShapeDtypeStruct((t, LANES), jnp.int32),
        scratch_shapes=[pltpu.VMEM((1, LANES), F32), pltpu.VMEM((1, LANES), F32)],
        compiler_params=_params("arbitrary"),
        name="positions",
    )(info, cnt, ltri, utri)


def _row_copy(src_hbm, src_row, dst_hbm, dst_row, sem):
    return pltpu.make_async_copy(src_hbm.at[pl.ds(src_row, 1)], dst_hbm.at[pl.ds(dst_row, 1)], sem)


SCATTER_SLOTS = 3


def _scatter_kernel(pos1_ref, pos2_ref, last_ref, used_ref, nt_ref, xn_hbm, zero_hbm, xs_hbm,
                    buf, lsem, sem, zsem, *, n_tok):
    max_tiles = xs_hbm.shape[0] // MOE_TM

    def zero_tile(tile):
        return pltpu.make_async_copy(zero_hbm, xs_hbm.at[pl.ds(tile * MOE_TM, MOE_TM)], zsem)

    def for_unused(fn):
        def body(tile, carry):
            fn(tile)
            return carry
        lax.fori_loop(nt_ref[0], max_tiles, body, 0)

    for e in range(N_EXPERTS):
        @pl.when(used_ref[e] > 0)
        def _():
            zero_tile(last_ref[e]).start()
    for_unused(lambda tile: zero_tile(tile).start())
    for e in range(N_EXPERTS):
        @pl.when(used_ref[e] > 0)
        def _():
            zero_tile(last_ref[e]).wait()
    for_unused(lambda tile: zero_tile(tile).wait())

    tm = buf.shape[1]
    n = n_tok // tm

    def load(i):
        return pltpu.make_async_copy(xn_hbm.at[pl.ds(i * tm, tm)], buf.at[i % SCATTER_SLOTS],
                                     lsem.at[i % SCATTER_SLOTS])

    def wait_rows(slot):
        pltpu.make_async_copy(xs_hbm.at[pl.ds(0, 2 * tm)], xs_hbm.at[pl.ds(0, 2 * tm)], sem.at[slot]).wait()

    load(0).start()
    load(1).start()

    def step(i, carry):
        slot = i % SCATTER_SLOTS
        load(i).wait()

        def body(j, c2):
            tok = i * tm + j
            src = buf.at[slot, pl.ds(j, 1)]
            pltpu.make_async_copy(src, xs_hbm.at[pl.ds(pos1_ref[tok], 1)], sem.at[slot]).start()
            pltpu.make_async_copy(src, xs_hbm.at[pl.ds(pos2_ref[tok], 1)], sem.at[slot]).start()
            return c2
        lax.fori_loop(0, tm, body, 0, unroll=DMA_UNROLL)

        @pl.when(i >= 1)
        def _():
            wait_rows((i - 1) % SCATTER_SLOTS)

        @pl.when(i + 2 < n)
        def _():
            load(i + 2).start()
        return carry
    lax.fori_loop(0, n, step, 0)
    wait_rows((n - 1) % SCATTER_SLOTS)


def _scatter_rows(xn, pos1, pos2, last_tile, used, n_tiles, n_rows):
    t = xn.shape[0]
    zero = jnp.zeros((MOE_TM, D_MODEL), F32)
    any_spec = pl.BlockSpec(memory_space=pl.ANY)
    return pl.pallas_call(
        functools.partial(_scatter_kernel, n_tok=t),
        grid_spec=pltpu.PrefetchScalarGridSpec(
            num_scalar_prefetch=5, grid=(1,),
            in_specs=[any_spec, any_spec], out_specs=any_spec,
            scratch_shapes=[pltpu.VMEM((SCATTER_SLOTS, MOE_TM, D_MODEL), F32),
                            pltpu.SemaphoreType.DMA((SCATTER_SLOTS,)),
                            pltpu.SemaphoreType.DMA((SCATTER_SLOTS,)),
                            pltpu.SemaphoreType.DMA]),
        out_shape=jax.ShapeDtypeStruct((n_rows, D_MODEL), F32),
        compiler_params=_params("arbitrary"),
        name="scatter_rows",
    )(pos1, pos2, last_tile, used, n_tiles, xn, zero)


def _experts_kernel(te_ref, tv_ref, nt_ref, xs_ref, wg_hbm, wu_hbm, wd_hbm, xn_new_ref, gate_new_ref,
                    ys_ref, moe_new_ref, wg_s, wu_s, wd_s, wg_f, wu_f, wd_f, wsem):
    i = pl.program_id(0)
    used = i < nt_ref[0]
    expert = te_ref[i]

    def fetch(e):
        slot = e % 2
        return [pltpu.make_async_copy(src.at[e], dst.at[slot], wsem.at[slot, j])
                for j, (src, dst) in enumerate(((wg_hbm, wg_f), (wu_hbm, wu_f), (wd_hbm, wd_f)))]

    @pl.when(jnp.logical_or(i == 0, expert != te_ref[jnp.maximum(i - 1, 0)]))
    def _():
        @pl.when(i == 0)
        def _():
            for c in fetch(expert):
                c.start()
        for c in fetch(expert):
            c.wait()

        @pl.when(expert + 1 < N_EXPERTS)
        def _():
            for c in fetch(expert + 1):
                c.start()
        slot = expert % 2
        wg_s[...] = wg_f[slot].astype(BF16)
        wu_s[...] = wu_f[slot].astype(BF16)
        wd_s[...] = wd_f[slot].astype(BF16)
        xn = xn_new_ref[...]
        lane = lax.broadcasted_iota(jnp.int32, gate_new_ref.shape, 1)
        gate = jnp.sum(jnp.where(lane == expert + ROUTER_OFF, gate_new_ref[...], 0.0), axis=-1, keepdims=True)
        hg = jnp.dot(xn, wg_s[...], preferred_element_type=F32)
        hu = jnp.dot(xn, wu_s[...], preferred_element_type=F32)
        hm = _silu(hg) * hu * gate
        y = jnp.dot(hm.astype(BF16), wd_s[...], preferred_element_type=F32)

        @pl.when(i == 0)
        def _():
            moe_new_ref[...] = y

        @pl.when(i > 0)
        def _():
            moe_new_ref[...] += y

    @pl.when(used)
    def _():
        row = lax.broadcasted_iota(jnp.int32, xs_ref.shape, 0)
        x_lo, x_hi = _unpack_rows(jnp.where(row < tv_ref[i], xs_ref[...], jnp.uint32(0)))
        x_lo = x_lo.astype(BF16)
        x_hi = x_hi.astype(BF16)
        up = lambda w_s: (jnp.dot(x_lo, w_s[:HALF, :], preferred_element_type=F32)
                          + jnp.dot(x_hi, w_s[HALF:, :], preferred_element_type=F32))
        hm = (_silu_tanh(up(wg_s)) * up(wu_s)).astype(BF16)
        ys_ref[...] = _pack_rows(jnp.dot(hm, wd_s[...], preferred_element_type=F32))

    @pl.when(jnp.logical_not(used))
    def _():
        ys_ref[...] = jnp.zeros(ys_ref.shape, U32)


def _experts(xs, tile_expert, tile_valid, n_tiles, wg, wu, wd, xn_new, gate_new):
    max_tiles = xs.shape[0] // MOE_TM
    rows = pl.BlockSpec((MOE_TM, HALF), lambda i, te, tv, nt: (i, 0))
    hbm = pl.BlockSpec(memory_space=pl.ANY)
    full = lambda a: pl.BlockSpec(a.shape, lambda i, te, tv, nt: (0,) * a.ndim)
    return pl.pallas_call(
        _experts_kernel,
        grid_spec=pltpu.PrefetchScalarGridSpec(
            num_scalar_prefetch=3, grid=(max_tiles,),
            in_specs=[rows, hbm, hbm, hbm, full(xn_new), full(gate_new)],
            out_specs=[rows, pl.BlockSpec(xn_new.shape, lambda i, te, tv, nt: (0, 0))],
            scratch_shapes=[pltpu.VMEM((D_MODEL, D_EXPERT), BF16), pltpu.VMEM((D_MODEL, D_EXPERT), BF16),
                            pltpu.VMEM((D_EXPERT, D_MODEL), BF16),
                            pltpu.VMEM((2, D_MODEL, D_EXPERT), F32), pltpu.VMEM((2, D_MODEL, D_EXPERT), F32),
                            pltpu.VMEM((2, D_EXPERT, D_MODEL), F32), pltpu.SemaphoreType.DMA((2, 3))]),
        out_shape=[jax.ShapeDtypeStruct(xs.shape, U32), jax.ShapeDtypeStruct(xn_new.shape, F32)],
        compiler_params=_params("arbitrary"),
        name="experts",
    )(tile_expert, tile_valid, n_tiles, xs, wg, wu, wd, xn_new, gate_new)


def _ple_gather_kernel(pos1_ref, pos2_ref, h_ref, info_ref, p_ref, wpp_ref, wpg_ref, gp_ref, gf_ref,
                       ys_hbm, y_ref, ybuf, sem):
    i = pl.program_id(0)
    n = pl.num_programs(0)
    tm = h_ref.shape[0]

    def issue(tile, slot):
        def body(j, carry):
            tok = tile * tm + j
            pltpu.make_async_copy(ys_hbm.at[pl.ds(pos1_ref[tok], 1)], ybuf.at[slot, 0, pl.ds(j, 1)],
                                  sem.at[slot]).start()
            pltpu.make_async_copy(ys_hbm.at[pl.ds(pos2_ref[tok], 1)], ybuf.at[slot, 1, pl.ds(j, 1)],
                                  sem.at[slot]).start()
            return carry
        lax.fori_loop(0, tm, body, 0, unroll=DMA_UNROLL)

    @pl.when(i == 0)
    def _():
        issue(0, 0)

    @pl.when(i + 1 < n)
    def _():
        issue(i + 1, (i + 1) % 2)

    slot = i % 2
    pltpu.make_async_copy(ybuf.at[slot], ybuf.at[slot], sem.at[slot]).wait()
    info = info_ref[...]
    moe = info[:, INFO_G1:INFO_G1 + 1] * ybuf[slot, 0] + info[:, INFO_G2:INFO_G2 + 1] * ybuf[slot, 1]
    h = h_ref[...] + moe
    hn = _rmsnorm(h, gp_ref[...])
    h = h + _mm(p_ref[...], wpp_ref[...]) * _sigmoid(_mm(hn, wpg_ref[...]))
    y_ref[...] = _rmsnorm(h, gf_ref[...])


def _ple_gather(h, info, p, ys, pos1, pos2, wpp, wpg, gp, gf):
    t = h.shape[0]
    tm = 256
    row = lambda n: pl.BlockSpec((tm, n), lambda i, p1, p2: (i, 0))
    full = lambda a: pl.BlockSpec(a.shape, lambda i, p1, p2: (0,) * a.ndim)
    return pl.pallas_call(
        _ple_gather_kernel,
        grid_spec=pltpu.PrefetchScalarGridSpec(
            num_scalar_prefetch=2, grid=(t // tm,),
            in_specs=[row(D_MODEL), row(LANES), row(PLE_DIM), full(wpp), full(wpg), full(gp), full(gf),
                      pl.BlockSpec(memory_space=pl.ANY)],
            out_specs=row(D_MODEL),
            scratch_shapes=[pltpu.VMEM((2, 2, tm, D_MODEL), F32), pltpu.SemaphoreType.DMA((2,))]),
        out_shape=jax.ShapeDtypeStruct((t, D_MODEL), F32),
        compiler_params=_params("arbitrary"),
        name="ple_gather",
    )(pos1, pos2, h, info, p, wpp, wpg, gp, gf, ys)


SC_IDX = 128
SC_ROWS = 64
SC_WORKERS = 32


def _sc_mesh():
    return plsc.VectorSubcoreMesh(core_axis_name="c", subcore_axis_name="s")


def _sc_windows(t, fn):
    per_worker = t // SC_WORKERS
    worker = lax.axis_index(("c", "s"))

    @pl.loop(0, per_worker // SC_IDX)
    def _(w):
        fn(worker * per_worker + w * SC_IDX)


def _sc_scatter_rows(xn, pos1, pos2, n_rows):
    t, d = xn.shape
    assert t % (SC_WORKERS * SC_IDX) == 0
    idx_t = pltpu.VMEM((1, SC_IDX), jnp.int32)

    @pl.kernel(out_type=jax.ShapeDtypeStruct((n_rows, d), xn.dtype), mesh=_sc_mesh(),
               scratch_types=[idx_t, idx_t, pltpu.VMEM((SC_ROWS, d), xn.dtype)])
    def scatter(x_hbm, p1_hbm, p2_hbm, o_hbm, i1_v, i2_v, buf):
        def window(base):
            pltpu.sync_copy(p1_hbm.at[:, pl.ds(base, SC_IDX)], i1_v)
            pltpu.sync_copy(p2_hbm.at[:, pl.ds(base, SC_IDX)], i2_v)
            for k in range(SC_IDX // SC_ROWS):
                pltpu.sync_copy(x_hbm.at[pl.ds(base + k * SC_ROWS, SC_ROWS)], buf)
                pltpu.sync_copy(buf, o_hbm.at[i1_v.at[0, pl.ds(k * SC_ROWS, SC_ROWS)]])
                pltpu.sync_copy(buf, o_hbm.at[i2_v.at[0, pl.ds(k * SC_ROWS, SC_ROWS)]])
        _sc_windows(t, window)

    return scatter(xn, pos1.reshape(1, t), pos2.reshape(1, t))


def _sc_gather_rows(ys, pos1, pos2):
    t = pos1.shape[0]
    d = ys.shape[1]
    assert t % (SC_WORKERS * SC_IDX) == 0
    idx_t = pltpu.VMEM((1, SC_IDX), jnp.int32)
    out = jax.ShapeDtypeStruct((t, d), ys.dtype)

    buf_t = pltpu.VMEM((SC_ROWS, d), ys.dtype)

    @pl.kernel(out_type=(out, out), mesh=_sc_mesh(),
               scratch_types=[idx_t, idx_t, buf_t, buf_t, pltpu.SemaphoreType.DMA((2,)),
                              pltpu.SemaphoreType.DMA((2,))])
    def gather(y_hbm, p1_hbm, p2_hbm, o1_hbm, o2_hbm, i1_v, i2_v, buf_a, buf_b, gsem, wsem):
        bufs = (buf_a, buf_b)

        def window(base):
            pltpu.sync_copy(p1_hbm.at[:, pl.ds(base, SC_IDX)], i1_v)
            pltpu.sync_copy(p2_hbm.at[:, pl.ds(base, SC_IDX)], i2_v)
            items = [(idx_v, o_hbm, k) for k in range(SC_IDX // SC_ROWS)
                     for idx_v, o_hbm in ((i1_v, o1_hbm), (i2_v, o2_hbm))]

            def read(n):
                idx_v, _, k = items[n]
                return pltpu.make_async_copy(y_hbm.at[idx_v.at[0, pl.ds(k * SC_ROWS, SC_ROWS)]],
                                             bufs[n % 2], gsem.at[n % 2])

            def write(n):
                _, o_hbm, k = items[n]
                return pltpu.make_async_copy(bufs[n % 2], o_hbm.at[pl.ds(base + k * SC_ROWS, SC_ROWS)],
                                             wsem.at[n % 2])

            read(0).start()
            for n in range(len(items)):
                read(n).wait()
                if n >= 1:
                    write(n - 1).wait()
                if n + 1 < len(items):
                    read(n + 1).start()
                write(n).start()
            write(len(items) - 1).wait()
        _sc_windows(t, window)

    return gather(ys, pos1.reshape(1, t), pos2.reshape(1, t))


def _ple_sparse_kernel(h_ref, info_ref, y1_ref, y2_ref, p_ref, wpp_ref, wpg_ref, gp_ref, gf_ref, y_ref):
    info = info_ref[...]
    g1 = info[:, INFO_G1:INFO_G1 + 1]
    g2 = info[:, INFO_G2:INFO_G2 + 1]
    y1_lo, y1_hi = _unpack_rows(y1_ref[...])
    y2_lo, y2_hi = _unpack_rows(y2_ref[...])
    moe = jnp.concatenate([g1 * y1_lo + g2 * y2_lo, g1 * y1_hi + g2 * y2_hi], axis=1)
    h = h_ref[...] + moe
    hn = _rmsnorm(h, gp_ref[...])
    h = h + _mm(p_ref[...], wpp_ref[...]) * _sigmoid(_mm(hn, wpg_ref[...]))
    y_ref[...] = _rmsnorm(h, gf_ref[...])


def _ple_sparse(h, info, y1, y2, p, wpp, wpg, gp, gf):
    t = h.shape[0]
    tm = ROW_TM
    row = lambda n: pl.BlockSpec((tm, n), lambda i: (i, 0))
    full = lambda a: pl.BlockSpec(a.shape, lambda i: (0,) * a.ndim)
    return pl.pallas_call(
        _ple_sparse_kernel,
        grid=(t // tm,),
        in_specs=[row(D_MODEL), row(LANES), row(HALF), row(HALF), row(PLE_DIM),
                  full(wpp), full(wpg), full(gp), full(gf)],
        out_specs=row(D_MODEL),
        out_shape=jax.ShapeDtypeStruct((t, D_MODEL), F32),
        compiler_params=_params("parallel"),
        name="ple_sparse",
    )(h, info, y1, y2, p, wpp, wpg, gp, gf)


def _tile_tables(cnt, max_tiles):
    tiles_e = jnp.maximum((cnt + (MOE_TM - 1)) // MOE_TM, 1)
    ends = jnp.cumsum(tiles_e)
    n_tiles = ends[-1]
    tile = jnp.arange(max_tiles, dtype=jnp.int32)
    idx = jnp.minimum(tile, n_tiles - 1)
    tile_expert = jnp.sum((idx[:, None] >= ends[None, :]).astype(jnp.int32), axis=1)
    mine = tile_expert[:, None] == jnp.arange(N_EXPERTS, dtype=jnp.int32)[None, :]
    of_mine = lambda v: jnp.sum(jnp.where(mine, v[None, :], 0), axis=1)
    valid = jnp.clip(of_mine(cnt) - (idx - of_mine(ends - tiles_e)) * MOE_TM, 0, MOE_TM)
    tile_valid = jnp.where(tile < n_tiles, valid, 0).astype(jnp.int32)
    return (tile_expert, tile_valid, n_tiles.reshape(1), (ends - 1).astype(jnp.int32),
            tiles_e.astype(jnp.int32))


def _ple_final_kernel(h_ref, m_ref, p_ref, wpp_ref, wpg_ref, gp_ref, gf_ref, y_ref):
    h = h_ref[...] + m_ref[...]
    hn = _rmsnorm(h, gp_ref[...])
    h = h + _mm(p_ref[...], wpp_ref[...]) * _sigmoid(_mm(hn, wpg_ref[...]))
    y_ref[...] = _rmsnorm(h, gf_ref[...])


def _ple_final(h, m, p, wpp, wpg, gp, gf):
    t = h.shape[0]
    tm = min(t, 256)
    row = lambda n: pl.BlockSpec((tm, n), lambda i: (i, 0))
    full = lambda a: pl.BlockSpec(a.shape, lambda i: (0,) * a.ndim)
    return pl.pallas_call(
        _ple_final_kernel,
        grid=(t // tm,),
        in_specs=[row(D_MODEL), row(D_MODEL), row(PLE_DIM), full(wpp), full(wpg), full(gp), full(gf)],
        out_specs=row(D_MODEL),
        out_shape=jax.ShapeDtypeStruct((t, D_MODEL), F32),
        compiler_params=_params("parallel"),
        name="ple_final",
    )(h, m, p, wpp, wpg, gp, gf)


def kernel(x_prompt, x_sample, p_prompt, p_sample, cache_k, cache_v, state_conv, state_S, rel_bias, norm_mix, w_in, att_sink, conv_w, dn_A_log, dn_dt_bias, dn_norm, w_out, norm_ffn, w_router_group, w_router_expert, w_gate, w_up, w_down, w_ple_proj, w_ple_gate, norm_ple, norm_final):
    batch, seq, _ = x_prompt.shape
    nseq = x_sample.shape[0]
    assert x_sample.shape[1] == 1 and norm_mix.shape[0] == 1 and cache_k.shape[2] == WINDOW
    assert seq % GDN_TB == 0 and seq % ATT_BLOCK == 0

    wi = w_in[0]
    o_db = ATT_COLS + CONV_CH
    w_in_re = jnp.concatenate(
        [wi[:, :o_db], wi[:, o_db + 2 * DN_HEADS:], wi[:, o_db:o_db + 2 * DN_HEADS],
         jnp.zeros((D_MODEL, LANES - 2 * DN_HEADS), F32)], axis=1).astype(BF16)
    row = lambda a: a.reshape(1, -1).astype(F32)
    pad_lanes = lambda a, off: jnp.zeros((1, LANES), F32).at[0, off:off + a.shape[0]].set(a)
    alog = pad_lanes(dn_A_log[0], DN_HEADS)
    dtb = pad_lanes(dn_dt_bias[0], DN_HEADS)
    dnx = jnp.tile(dn_norm[0], DN_HEADS).reshape(1, DN_WIDTH)
    w_router = jnp.concatenate(
        [w_router_group[0], w_router_expert[0],
         jnp.zeros((D_MODEL, LANES - N_GROUPS - N_EXPERTS), F32)], axis=1).astype(BF16)
    wo = w_out[0].astype(BF16)
    wg, wu, wd = w_gate[0], w_up[0], w_down[0]
    wpp, wpg = w_ple_proj[0].astype(BF16), w_ple_gate[0].astype(BF16)
    sink = att_sink[0]

    qi = np.arange(ATT_BLOCK)[:, None]
    kj = np.arange(2 * ATT_BLOCK)[None, :]
    bucket_p = jnp.asarray(_t5_bucket_np(qi + ATT_BLOCK - kj))
    bucket_s = jnp.asarray(_t5_bucket_np(WINDOW - np.arange(WINDOW)[None, :]))

    xp = x_prompt.reshape(batch * seq, D_MODEL)
    att_p, qkv_p, dz_p, ba_p, xc_tails = _inproj_conv(xp, row(norm_mix[0]), w_in_re, conv_w[0], seq)
    o_att_p = _attn_prompt(att_p, bucket_p, rel_bias, sink, batch, seq)
    o_dn_p, s_p = _gdn_prompt(qkv_p, dz_p, ba_p, alog, dtb, dnx, batch, seq)
    h1, xn2, info, cnt = _route_sparse(xp, o_att_p, o_dn_p, wo, row(norm_ffn[0]), w_router)
    pos = _positions(info, cnt)
    pos1, pos2 = pos[:, 0], pos[:, 1]
    max_tiles = _moe_tiles(batch * seq)
    cnt_e = cnt[0, ROUTER_OFF:ROUTER_OFF + N_EXPERTS].astype(jnp.int32)
    tile_expert, tile_valid, n_tiles, last_tile, used = _tile_tables(cnt_e, max_tiles)
    xs_sorted = _sc_scatter_rows(xn2, pos1, pos2, max_tiles * MOE_TM)

    xs = x_sample.reshape(nseq, D_MODEL)
    att_s, xc_s, dz_s, ba_s = _inproj(xs, row(norm_mix[0]), w_in_re)
    ck_t = jnp.transpose(cache_k[0], (0, 2, 3, 1))
    cv_t = jnp.transpose(cache_v[0], (0, 2, 3, 1))
    o_att_s, ks_t, vs_t = _attn_sample(att_s, ck_t, cv_t, bucket_s, rel_bias, sink)
    sconv_t = jnp.swapaxes(state_conv[0], 0, 1)
    o_dn_s_t, s_s_t = _gdn_sample_lanes(xc_s, dz_s, ba_s, sconv_t, jnp.transpose(state_S[0], (1, 2, 3, 0)),
                                        conv_w[0], alog, dtb, dn_norm[0])
    s_s = jnp.transpose(s_s_t, (3, 0, 1, 2))

    h1_s, xn2_s, gates_s = _outproj_router(xs, o_att_s, o_dn_s_t, wo, row(norm_ffn[0]), w_router)

    ys, moe_s = _experts(xs_sorted, tile_expert, tile_valid, n_tiles, wg, wu, wd, xn2_s, gates_s)
    y1, y2 = _sc_gather_rows(ys, pos1, pos2)
    y_s = _ple_final(h1_s, moe_s, p_sample[0].reshape(nseq, PLE_DIM), wpp, wpg, row(norm_ple[0]),
                     row(norm_final))
    y_p = _ple_sparse(h1, info, y1, y2, p_prompt[0].reshape(batch * seq, PLE_DIM),
                      wpp, wpg, row(norm_ple[0]), row(norm_final))

    att_p3 = att_p.reshape(batch, seq, ATT_COLS)
    kv_shape = (1, batch, WINDOW, ATT_KV_HEADS, HEAD_DIM)
    k_p = att_p3[:, seq - WINDOW:, ATT_WIDTH:ATT_WIDTH + KV_WIDTH].reshape(kv_shape)
    v_p = att_p3[:, seq - WINDOW:, ATT_WIDTH + KV_WIDTH:].reshape(kv_shape)
    conv_p = xc_tails.reshape(batch, -1, TAIL, CONV_CH)[:, -1, TAIL - (CONV_WIDTH - 1):][None]
    k_s = jnp.transpose(ks_t, (0, 3, 1, 2))[None]
    v_s = jnp.transpose(vs_t, (0, 3, 1, 2))[None]
    conv_s = jnp.concatenate([state_conv[0][:, 1:], xc_s[:, None, :]], axis=1)[None]
    return (y_p.reshape(batch, seq, D_MODEL), y_s.reshape(nseq, 1, D_MODEL),
            k_p, v_p, conv_p, s_p[None], k_s, v_s, conv_s, s_s[None])
```

```python
import functools
import math

import numpy as np
import jax
import jax.numpy as jnp
from jax import lax
from jax.experimental import pallas as pl
from jax.experimental.pallas import tpu as pltpu
from jax.experimental.pallas import tpu_sc as plsc

F32 = jnp.float32
BF16 = jnp.bfloat16

D_MODEL = 1024
ATT_HEADS = 8
ATT_KV_HEADS = 2
HEAD_DIM = 64
GQA = ATT_HEADS // ATT_KV_HEADS
WINDOW = 128
ATT_BLOCK = 128
N_BUCKETS = 32
DN_HEADS = 8
DN_DK = 64
DN_DV = 64
CONV_WIDTH = 4
DN_CHUNK = 64
ATT_WIDTH = ATT_HEADS * HEAD_DIM
KV_WIDTH = ATT_KV_HEADS * HEAD_DIM
DN_WIDTH = DN_HEADS * DN_DV
CONV_CH = 3 * DN_WIDTH
N_GROUPS = 4
EXPERTS_PER_GROUP = 8
N_EXPERTS = N_GROUPS * EXPERTS_PER_GROUP
D_EXPERT = 256
PLE_DIM = 256
EPS = 1e-6
NEG_INF = float("-inf")

ATT_COLS = ATT_WIDTH + 2 * KV_WIDTH
LANES = 128
IN_COLS = ATT_COLS + CONV_CH + DN_WIDTH + LANES
ROUTER_OFF = N_GROUPS
VMEM_LIMIT = 48 * 1024 * 1024
ROW_TM = 512


def _params(*sem):
    return pltpu.CompilerParams(dimension_semantics=sem, vmem_limit_bytes=VMEM_LIMIT)


def _mm(a, b):
    return jnp.dot(a.astype(BF16), b.astype(BF16), preferred_element_type=F32)


def _mm_nt(a, b):
    return lax.dot_general(a.astype(BF16), b.astype(BF16), (((1,), (1,)), ((), ())),
                           preferred_element_type=F32)


def _mm_tn(a, b):
    return lax.dot_general(a.astype(BF16), b.astype(BF16), (((0,), (0,)), ((), ())),
                           preferred_element_type=F32)


def _split3(x):
    h1 = x.astype(BF16)
    r1 = x - h1.astype(F32)
    h2 = r1.astype(BF16)
    h3 = (r1 - h2.astype(F32)).astype(BF16)
    return h1, h2, h3


def _mm_sel_rhs(x, sel):
    h1, h2, h3 = _split3(x)
    d = lambda h: jnp.dot(h, sel, preferred_element_type=F32)
    return d(h1) + d(h2) + d(h3)


def _mm_sel_lhs(sel, x):
    h1, h2, h3 = _split3(x)
    d = lambda h: jnp.dot(sel, h, preferred_element_type=F32)
    return d(h1) + d(h2) + d(h3)


def _mm3(a, b):
    ah = a.astype(BF16)
    al = (a - ah.astype(F32)).astype(BF16)
    bh = b.astype(BF16)
    bl = (b - bh.astype(F32)).astype(BF16)
    d = lambda u, v: jnp.dot(u, v, preferred_element_type=F32)
    return d(ah, bh) + d(ah, bl) + d(al, bh)


def _sigmoid(x):
    return 1.0 / (1.0 + jnp.exp(-x))


def _silu(x):
    return x * _sigmoid(x)


def _silu_tanh(x):
    return x * (0.5 * jnp.tanh(0.5 * x) + 0.5)


def _softplus(x):
    return jnp.maximum(x, 0.0) + jnp.log1p(jnp.exp(-jnp.abs(x)))


def _rmsnorm(x, g):
    return x * lax.rsqrt(jnp.mean(x * x, axis=-1, keepdims=True) + EPS) * g


def _t5_bucket_np(dist):
    max_exact = N_BUCKETS // 2
    d = np.maximum(dist, 0)
    ratio = (np.log(np.maximum(d, 1).astype(np.float32) / np.float32(max_exact))
             / np.float32(math.log(WINDOW / max_exact))).astype(np.float32)
    large = np.minimum(max_exact + (ratio * np.float32(N_BUCKETS - max_exact)).astype(np.int32),
                       N_BUCKETS - 1)
    return np.where(d < max_exact, d, large).astype(np.int32)


def _bias_lookup(bucket, rb_ref, h):
    acc = jnp.zeros(bucket.shape, F32)
    for t in range(N_BUCKETS):
        acc = jnp.where(bucket == t, rb_ref[t, h], acc)
    return acc


def _inproj_kernel(x_ref, g_ref, w_ref, att_ref, xc_ref, dz_ref, ba_ref):
    xn = _rmsnorm(x_ref[...], g_ref[...]).astype(BF16)
    o0, o1, o2 = ATT_COLS, ATT_COLS + CONV_CH, ATT_COLS + CONV_CH + DN_WIDTH
    att_ref[...] = jnp.dot(xn, w_ref[:, :o0], preferred_element_type=F32)
    xc_ref[...] = jnp.dot(xn, w_ref[:, o0:o1], preferred_element_type=F32)
    dz_ref[...] = jnp.dot(xn, w_ref[:, o1:o2], preferred_element_type=F32)
    ba_ref[...] = jnp.dot(xn, w_ref[:, o2:], preferred_element_type=F32)


def _inproj(x, g, w):
    t = x.shape[0]
    tm = min(t, ROW_TM)
    row = lambda n: pl.BlockSpec((tm, n), lambda i: (i, 0))
    full = lambda a: pl.BlockSpec(a.shape, lambda i: (0,) * a.ndim)
    return pl.pallas_call(
        _inproj_kernel,
        grid=(t // tm,),
        in_specs=[row(D_MODEL), full(g), full(w)],
        out_specs=[row(ATT_COLS), row(CONV_CH), row(DN_WIDTH), row(LANES)],
        out_shape=[jax.ShapeDtypeStruct((t, n), F32) for n in (ATT_COLS, CONV_CH, DN_WIDTH, LANES)],
        compiler_params=_params("parallel"),
        name="inproj",
    )(x, g, w)


TAIL = 8
PAIR = 2 * DN_DK
N_PAIRS = DN_WIDTH // PAIR


def _head_sums(z, pair_ones):
    hi = z.astype(BF16)
    lw = (z - hi.astype(F32)).astype(BF16)
    d = lambda a, p: jnp.dot(a[:, p * PAIR:(p + 1) * PAIR], pair_ones, preferred_element_type=F32)
    return jnp.concatenate([d(hi, p) + d(lw, p) for p in range(N_PAIRS)], axis=1)


def _inproj_conv_kernel(x_ref, g_ref, w_ref, cw_ref, ones_ref, att_ref, qkv_ref, dz_ref, ba_ref, tail_ref,
                        xp_scr, *, tiles_per_seq):
    tm = x_ref.shape[0]

    @pl.when(pl.program_id(0) % tiles_per_seq == 0)
    def _():
        xp_scr[...] = jnp.zeros((TAIL, CONV_CH), F32)

    xn = _rmsnorm(x_ref[...], g_ref[...]).astype(BF16)
    o0, o1, o2 = ATT_COLS, ATT_COLS + CONV_CH, ATT_COLS + CONV_CH + DN_WIDTH
    xc = jnp.dot(xn, w_ref[:, o0:o1], preferred_element_type=F32)
    att_ref[...] = jnp.dot(xn, w_ref[:, :o0], preferred_element_type=F32)
    dz_ref[...] = jnp.dot(xn, w_ref[:, o1:o2], preferred_element_type=F32)
    ba_ref[...] = jnp.dot(xn, w_ref[:, o2:], preferred_element_type=F32)

    head = jnp.concatenate([xp_scr[...], xc[:TAIL, :]], axis=0)

    def shifted(j):
        return jnp.concatenate([head[TAIL - j:2 * TAIL - j, :], pltpu.roll(xc, j, axis=0)[TAIL:, :]], axis=0)

    y = shifted(3) * cw_ref[0:1, :]
    y = y + shifted(2) * cw_ref[1:2, :]
    y = y + shifted(1) * cw_ref[2:3, :]
    y = y + xc * cw_ref[3:4, :]
    tail = xc[tm - TAIL:, :]
    xp_scr[...] = tail
    tail_ref[0] = tail
    y = _silu_tanh(y)
    q = y[:, :DN_WIDTH]
    k = y[:, DN_WIDTH:2 * DN_WIDTH]
    inv_norm = lax.rsqrt(_head_sums(jnp.concatenate([q * q, k * k], axis=0), ones_ref[...]) + EPS)
    qkv_ref[:, :DN_WIDTH] = q * inv_norm[:tm] * (DN_DK ** -0.5)
    qkv_ref[:, DN_WIDTH:2 * DN_WIDTH] = k * inv_norm[tm:]
    qkv_ref[:, 2 * DN_WIDTH:] = y[:, 2 * DN_WIDTH:]


def _pair_ones():
    lane = np.arange(PAIR)
    return jnp.asarray((lane[:, None] // DN_DV == lane[None, :] // DN_DV).astype(np.float32), dtype=BF16)


def _inproj_conv(x, g, w, conv_w, seq):
    t = x.shape[0]
    tm = ROW_TM
    assert seq % tm == 0
    ones = _pair_ones()
    row = lambda n: pl.BlockSpec((tm, n), lambda i: (i, 0))
    full = lambda a: pl.BlockSpec(a.shape, lambda i: (0,) * a.ndim)
    return pl.pallas_call(
        functools.partial(_inproj_conv_kernel, tiles_per_seq=seq // tm),
        grid=(t // tm,),
        in_specs=[row(D_MODEL), full(g), full(w), full(conv_w), full(ones)],
        out_specs=[row(ATT_COLS), row(CONV_CH), row(DN_WIDTH), row(LANES),
                   pl.BlockSpec((1, TAIL, CONV_CH), lambda i: (i, 0, 0))],
        out_shape=[jax.ShapeDtypeStruct((t, n), F32) for n in (ATT_COLS, CONV_CH, DN_WIDTH, LANES)]
                  + [jax.ShapeDtypeStruct((t // tm, TAIL, CONV_CH), F32)],
        scratch_shapes=[pltpu.VMEM((TAIL, CONV_CH), F32)],
        compiler_params=_params("arbitrary"),
        name="inproj_conv",
    )(x, g, w, conv_w, ones)


GROUP_ROWS = GQA * ATT_BLOCK


def _attn_prompt_kernel(cur_ref, prev_ref, bucket_ref, rb_ref, sink_ref, o_ref, bias_scr, sink_scr):
    i = pl.program_id(0)
    nseq = cur_ref.shape[0]

    @pl.when(i == 0)
    def _():
        qi = lax.broadcasted_iota(jnp.int32, (ATT_BLOCK, 2 * ATT_BLOCK), 0)
        kj = lax.broadcasted_iota(jnp.int32, (ATT_BLOCK, 2 * ATT_BLOCK), 1)
        dist = qi + ATT_BLOCK - kj
        band = jnp.logical_and(dist >= 0, dist < WINDOW)
        bucket = bucket_ref[...]
        hrow = lax.broadcasted_iota(jnp.int32, (GROUP_ROWS, 1), 0) // ATT_BLOCK
        for g in range(ATT_KV_HEADS):
            sink_col = jnp.zeros((GROUP_ROWS, 1), F32)
            for hh in range(GQA):
                h = g * GQA + hh
                bias = jnp.where(band, _bias_lookup(bucket, rb_ref, h), NEG_INF)
                bias_scr[0, g, hh * ATT_BLOCK:(hh + 1) * ATT_BLOCK, :] = bias
                bias_scr[1, g, hh * ATT_BLOCK:(hh + 1) * ATT_BLOCK, :] = jnp.where(kj >= ATT_BLOCK, bias, NEG_INF)
                sink_col = jnp.where(hrow == hh, sink_ref[h], sink_col)
            sink_scr[g] = sink_col

    first = (i == 0).astype(jnp.int32)
    probs = [(b, g) for b in range(nseq) for g in range(ATT_KV_HEADS)]
    scores = []
    for b, g in probs:
        cur = cur_ref[b]
        prev = prev_ref[b]
        q = jnp.concatenate([cur[:, (g * GQA + hh) * HEAD_DIM:(g * GQA + hh + 1) * HEAD_DIM]
                             for hh in range(GQA)], axis=0) * (HEAD_DIM ** -0.5)
        kcol = slice(ATT_WIDTH + g * HEAD_DIM, ATT_WIDTH + (g + 1) * HEAD_DIM)
        k2 = jnp.concatenate([prev[:, kcol], cur[:, kcol]], axis=0)
        scores.append(_mm_nt(q, k2) + bias_scr[first, g])
    probs_p, dens = [], []
    for (b, g), s in zip(probs, scores):
        sink = sink_scr[g]
        m = jnp.maximum(jnp.max(s, axis=-1, keepdims=True), sink)
        p = jnp.exp(s - m)
        dens.append(jnp.sum(p, axis=-1, keepdims=True) + jnp.exp(sink - m))
        probs_p.append(p)
    outs = {}
    for (b, g), p, den in zip(probs, probs_p, dens):
        vcol = slice(ATT_WIDTH + KV_WIDTH + g * HEAD_DIM, ATT_WIDTH + KV_WIDTH + (g + 1) * HEAD_DIM)
        v2 = jnp.concatenate([prev_ref[b][:, vcol], cur_ref[b][:, vcol]], axis=0)
        outs[b, g] = _mm(p, v2) / den
    for b in range(nseq):
        o_ref[b] = jnp.concatenate([outs[b, g][hh * ATT_BLOCK:(hh + 1) * ATT_BLOCK, :]
                                    for g in range(ATT_KV_HEADS) for hh in range(GQA)],
                                   axis=1).astype(o_ref.dtype)


def _attn_prompt(att, bucket, rel_bias, sink, batch, seq):
    nb = seq // ATT_BLOCK
    smem = pl.BlockSpec(memory_space=pltpu.SMEM)
    att3 = att.reshape(batch, seq, ATT_COLS)
    out = pl.pallas_call(
        _attn_prompt_kernel,
        grid=(nb,),
        in_specs=[
            pl.BlockSpec((batch, ATT_BLOCK, ATT_COLS), lambda i: (0, i, 0)),
            pl.BlockSpec((batch, ATT_BLOCK, ATT_COLS), lambda i: (0, jnp.maximum(i - 1, 0), 0)),
            pl.BlockSpec(bucket.shape, lambda i: (0, 0)),
            smem, smem,
        ],
        out_specs=pl.BlockSpec((batch, ATT_BLOCK, ATT_WIDTH), lambda i: (0, i, 0)),
        out_shape=jax.ShapeDtypeStruct((batch, seq, ATT_WIDTH), BF16),
        scratch_shapes=[pltpu.VMEM((2, ATT_KV_HEADS, GROUP_ROWS, 2 * ATT_BLOCK), F32),
                        pltpu.VMEM((ATT_KV_HEADS, GROUP_ROWS, 1), F32)],
        compiler_params=_params("arbitrary"),
        name="attn_prompt",
    )(att3, att3, bucket, rel_bias, sink)
    return out.reshape(batch * seq, ATT_WIDTH)


ATT_S_BB = 32


def _attn_sample_kernel(att_ref, ck_ref, cv_ref, bucket_ref, rb_ref, sink_ref, o_ref, ks_ref, vs_ref,
                        bias_scr, col_scr):
    hrow = lax.broadcasted_iota(jnp.int32, (ATT_HEADS, LANES), 0)
    lane = lax.broadcasted_iota(jnp.int32, (ATT_HEADS, LANES), 1)

    last = (lax.broadcasted_iota(jnp.int32, (3, WINDOW), 1) == WINDOW - 1).astype(BF16)
    is_last = lax.broadcasted_iota(jnp.int32, (KV_WIDTH, WINDOW), 1) == WINDOW - 1

    def shifted(cache_t, new_row):
        pieces = jnp.concatenate([p.astype(F32) for p in _split3(new_row)], axis=0).astype(BF16)
        col = lax.dot_general(pieces, last, (((0,), (0,)), ((), ())), preferred_element_type=F32)
        out = jnp.where(is_last, col, pltpu.roll(cache_t, WINDOW - 1, axis=1))
        return out.reshape(ATT_KV_HEADS, HEAD_DIM, WINDOW)

    for b in range(ATT_S_BB):
        row = att_ref[b:b + 1, :]
        ks_ref[b] = shifted(ck_ref[b].reshape(KV_WIDTH, WINDOW), row[:, ATT_WIDTH:ATT_WIDTH + KV_WIDTH])
        vs_ref[b] = shifted(cv_ref[b].reshape(KV_WIDTH, WINDOW), row[:, ATT_WIDTH + KV_WIDTH:])

    @pl.when(pl.program_id(0) == 0)
    def _():
        bucket = jnp.broadcast_to(bucket_ref[...], (ATT_HEADS, LANES))
        bias = jnp.zeros((ATT_HEADS, LANES), F32)
        cols = jnp.zeros((ATT_HEADS, LANES), F32)
        for h in range(ATT_HEADS):
            bias = jnp.where(hrow == h, _bias_lookup(bucket, rb_ref, h), bias)
            cols = jnp.where(jnp.logical_and(hrow == h, lane == 0), sink_ref[h], cols)
            cols = jnp.where(jnp.logical_and(hrow == h, lane == 1), rb_ref[0, h], cols)
        bias_scr[...] = jnp.where(lane >= 1, bias, NEG_INF)
        col_scr[...] = cols

    bias_c = bias_scr[...]
    sink = col_scr[:, 0:1]
    bias_n = col_scr[:, 1:2]
    same_group = (hrow // GQA) == (lane // HEAD_DIM)
    low_group = lax.broadcasted_iota(jnp.int32, (ATT_HEADS, HEAD_DIM), 0) < GQA
    rnd = lambda a: a.astype(BF16).astype(F32)
    seqs = range(ATT_S_BB)
    rows = [att_ref[b:b + 1, :] for b in seqs]
    q_bds = []
    for row in rows:
        q = row[:, :ATT_WIDTH] * (HEAD_DIM ** -0.5)
        qh = jnp.concatenate([q[:, h * HEAD_DIM:(h + 1) * HEAD_DIM] for h in range(ATT_HEADS)], axis=0)
        q_bds.append(jnp.where(same_group, jnp.concatenate([qh, qh], axis=1), 0.0))
    kv_t = lambda ref, b: ref[b].reshape(KV_WIDTH, WINDOW)
    s_cs = [_mm(q_bd, kv_t(ck_ref, b)) + bias_c for b, q_bd in zip(seqs, q_bds)]
    prs, pns = [], []
    for row, q_bd, s_c in zip(rows, q_bds, s_cs):
        kn = row[:, ATT_WIDTH:ATT_WIDTH + KV_WIDTH]
        s_n = jnp.sum(rnd(q_bd) * rnd(kn), axis=-1, keepdims=True) + bias_n
        m = jnp.maximum(jnp.maximum(jnp.max(s_c, axis=-1, keepdims=True), s_n), sink)
        p_c = jnp.exp(s_c - m)
        p_n = jnp.exp(s_n - m)
        den = jnp.sum(p_c, axis=-1, keepdims=True) + p_n + jnp.exp(sink - m)
        prs.append(p_c / den)
        pns.append(p_n / den)
    pvs = [_mm_nt(pr, kv_t(cv_ref, b)) for b, pr in zip(seqs, prs)]
    for b, row, pv, pn in zip(seqs, rows, pvs, pns):
        vn = row[:, ATT_WIDTH + KV_WIDTH:]
        o_full = pv + rnd(pn) * rnd(vn)
        o_sel = jnp.where(low_group, o_full[:, :HEAD_DIM], o_full[:, HEAD_DIM:])
        o_ref[b:b + 1, :] = jnp.concatenate([o_sel[h:h + 1, :] for h in range(ATT_HEADS)], axis=1)


def _attn_sample(att, ck, cv, bucket, rel_bias, sink):
    nseq = att.shape[0]
    smem = pl.BlockSpec(memory_space=pltpu.SMEM)
    cache = pl.BlockSpec((ATT_S_BB, ATT_KV_HEADS, HEAD_DIM, WINDOW), lambda i: (i, 0, 0, 0))
    return pl.pallas_call(
        _attn_sample_kernel,
        grid=(nseq // ATT_S_BB,),
        in_specs=[pl.BlockSpec((ATT_S_BB, ATT_COLS), lambda i: (i, 0)), cache, cache,
                  pl.BlockSpec(bucket.shape, lambda i: (0, 0)), smem, smem],
        out_specs=[pl.BlockSpec((ATT_S_BB, ATT_WIDTH), lambda i: (i, 0)), cache, cache],
        out_shape=[jax.ShapeDtypeStruct((nseq, ATT_WIDTH), F32),
                   jax.ShapeDtypeStruct(ck.shape, F32), jax.ShapeDtypeStruct(cv.shape, F32)],
        scratch_shapes=[pltpu.VMEM((ATT_HEADS, LANES), F32), pltpu.VMEM((ATT_HEADS, LANES), F32)],
        compiler_params=_params("arbitrary"),
        name="attn_sample",
    )(att, ck, cv, bucket, rel_bias, sink)


GDN_TB = 128
GDN_NC = GDN_TB // DN_CHUNK


def _gdn_gates(ba, alog, dtb):
    beta = _sigmoid(ba)
    g = -jnp.exp(alog) * _softplus(ba + dtb)
    return beta, g


def _pair_diag(x, lo):
    xb = x.astype(BF16)
    zero = jnp.zeros_like(xb)
    return jnp.concatenate([jnp.where(lo, xb, zero), jnp.where(lo, zero, xb)], axis=0)


def _gdn_prompt_kernel(qkv_ref, dz_ref, ba_ref, alog_ref, dtb_ref, dnx_ref,
                       hsum_ref, expb_ref, expg_ref, ltri_ref,
                       o_ref, s_out_ref, s_scr):
    i = pl.program_id(0)
    nb = qkv_ref.shape[0]

    @pl.when(i == 0)
    def _():
        s_scr[...] = jnp.zeros(s_scr.shape, F32)

    hsum = hsum_ref[...]
    ri = lax.broadcasted_iota(jnp.int32, (DN_CHUNK, PAIR), 0)
    ci = lax.broadcasted_iota(jnp.int32, (DN_CHUNK, PAIR), 1)
    lo = ci < DN_DK
    cj = jnp.where(lo, ci, ci - DN_DK)
    causal = ri >= cj
    strict = ri > cj
    eye = (ri == cj).astype(F32)

    def sel2(x, m):
        hi = x.astype(BF16)
        lw = (x - hi.astype(F32)).astype(BF16)
        return (jnp.dot(hi, m, preferred_element_type=F32) + jnp.dot(lw, m, preferred_element_type=F32))

    pre = []
    for b in range(nb):
        q = qkv_ref[b, :, :DN_WIDTH]
        k = qkv_ref[b, :, DN_WIDTH:2 * DN_WIDTH]
        v = qkv_ref[b, :, 2 * DN_WIDTH:]
        beta_c, g_c = _gdn_gates(ba_ref[b], alog_ref[...], dtb_ref[...])
        beta = sel2(beta_c, expb_ref[...])
        gam_c = _mm_sel_lhs(ltri_ref[...], g_c)
        gam = _mm_sel_rhs(gam_c, expg_ref[...])
        gam_t = gam_c.T
        kb = k * beta
        egam = jnp.exp(gam)
        pre.append(dict(q=q, k=k, kb=kb, vb=v * beta, qg=q * egam, wr=kb * egam, gam=gam, gam_t=gam_t))

    probs = [(c, b, p) for c in range(GDN_NC) for b in range(nb) for p in range(N_PAIRS)]
    pick = lambda m: jnp.where(lo, m[:DN_DK], m[DN_DK:])
    rows_of = lambda c: slice(c * DN_CHUNK, (c + 1) * DN_CHUNK)
    sl = lambda name, c, b, p: pre[b][name][rows_of(c), p * PAIR:(p + 1) * PAIR]
    raws = []
    for c, b, p in probs:
        k_p = sl("k", c, b, p)
        k_rows = jnp.concatenate([jnp.where(lo, k_p, 0.0), jnp.where(lo, 0.0, k_p)], axis=0)
        raws.append(_mm_nt(jnp.concatenate([sl("kb", c, b, p), sl("q", c, b, p)], axis=0), k_rows))
    pws, ts, qks = [], [], []
    for (c, b, p), raw in zip(probs, raws):
        gcol = sl("gam", c, b, p)
        h0 = DN_HEADS + 2 * p
        gam_t = pre[b]["gam_t"]
        grow = jnp.concatenate([gam_t[h0:h0 + 1, rows_of(c)], gam_t[h0 + 1:h0 + 2, rows_of(c)]], axis=1)
        decay = jnp.exp(jnp.where(causal, gcol - grow, NEG_INF))
        a = jnp.where(strict, raw[:DN_CHUNK] * decay, 0.0)
        qks.append(jnp.where(causal, raw[DN_CHUNK:] * decay, 0.0))
        pws.append(-a)
        ts.append(eye - a)
    pws = [_mm(pw, _pair_diag(pw, lo)) for pw in pws]
    for _ in range(4):
        rs = [_mm(jnp.concatenate([pw, t], axis=0), _pair_diag(pw, lo)) for pw, t in zip(pws, ts)]
        pws = [r[:DN_CHUNK] for r in rs]
        ts = [t + r[DN_CHUNK:] for t, r in zip(ts, rs)]
    rs = [_mm(t, _pair_diag(pw, lo)) for pw, t in zip(pws, ts)]
    ts = [t + r for t, r in zip(ts, rs)]
    sols = [_mm(t, jnp.concatenate([_pair_diag(sl("vb", c, b, p), lo), _pair_diag(sl("wr", c, b, p), lo)],
                                   axis=1)) for (c, b, p), t in zip(probs, ts)]
    qkuws = [_mm(qk, jnp.concatenate([_pair_diag(s[:, :PAIR], lo), _pair_diag(s[:, PAIR:], lo)], axis=1))
             for qk, s in zip(qks, sols)]
    crosses, gls = [], []
    for (c, b, p), s in zip(probs, sols):
        last = (c + 1) * DN_CHUNK - 1
        gam_last = pre[b]["gam"][last:last + 1, p * PAIR:(p + 1) * PAIR]
        kd = sl("k", c, b, p) * jnp.exp(gam_last - sl("gam", c, b, p))
        crosses.append(_mm_tn(kd, s))
        gls.append(jnp.exp(gam_last))
    lhs = [jnp.concatenate([pick(cr[:, PAIR:]), sl("qg", c, b, p) - qkuw[:, PAIR:]], axis=0)
           for (c, b, p), cr, qkuw in zip(probs, crosses, qkuws)]

    o_rows = [[] for _ in range(nb)]
    per_chunk = nb * N_PAIRS
    for c in range(GDN_NC):
        sel = slice(c * per_chunk, (c + 1) * per_chunk)
        s_olds = [s_scr[b, p] for _, b, p in probs[sel]]
        rs = [_mm(l, _pair_diag(s_old, lo)) for l, s_old in zip(lhs[sel], s_olds)]
        o_pairs = [[] for _ in range(nb)]
        for (_, b, p), r, s_old, gl, cr, qkuw in zip(probs[sel], rs, s_olds, gls[sel], crosses[sel], qkuws[sel]):
            s_scr[b, p] = gl * s_old - r[:DN_DK] + pick(cr[:, :PAIR])
            o_pairs[b].append(r[DN_DK:] + qkuw[:, :PAIR])
        for b in range(nb):
            o_rows[b].append(jnp.concatenate(o_pairs[b], axis=1))

    o_all = jnp.concatenate([jnp.concatenate(rows, axis=0) for rows in o_rows], axis=0)
    inv_rms = lax.rsqrt(_head_sums(o_all * o_all, hsum) * (1.0 / DN_DV) + EPS)
    for b in range(nb):
        rows = slice(b * GDN_TB, (b + 1) * GDN_TB)
        o_ref[b] = (o_all[rows] * inv_rms[rows] * dnx_ref[...] * _silu_tanh(dz_ref[b])).astype(o_ref.dtype)

    @pl.when(i == pl.num_programs(0) - 1)
    def _():
        for b in range(nb):
            for p in range(N_PAIRS):
                s_p = s_scr[b, p]
                s_out_ref[b, 2 * p] = s_p[:, :DN_DV]
                s_out_ref[b, 2 * p + 1] = s_p[:, DN_DV:]


def _gdn_consts():
    lane = np.arange(DN_WIDTH)
    pl_lane = np.arange(PAIR)
    hsum = (pl_lane[:, None] // DN_DV == pl_lane[None, :] // DN_DV)
    src = np.arange(LANES)
    expb = (src[:, None] == lane[None, :] // DN_DV)
    expg = (src[:, None] == DN_HEADS + lane[None, :] // DN_DV)
    tok = np.arange(GDN_TB)
    ltri = np.logical_and(tok[:, None] >= tok[None, :],
                          tok[:, None] // DN_CHUNK == tok[None, :] // DN_CHUNK)
    as_bf16 = lambda m: jnp.asarray(m.astype(np.float32), dtype=BF16)
    return as_bf16(hsum), as_bf16(expb), as_bf16(expg), as_bf16(ltri)


def _gdn_prompt(xc, dz, ba, alog, dtb, dnx, batch, seq):
    nt = seq // GDN_TB
    hsum, expb, expg, ltri = _gdn_consts()
    row = lambda n: pl.BlockSpec((batch, GDN_TB, n), lambda i: (0, i, 0))
    full = lambda a: pl.BlockSpec(a.shape, lambda i: (0,) * a.ndim)
    consts = (alog, dtb, dnx, hsum, expb, expg, ltri)
    as3d = lambda a: a.reshape(batch, seq, a.shape[-1])
    o, s = pl.pallas_call(
        _gdn_prompt_kernel,
        grid=(nt,),
        in_specs=[row(CONV_CH), row(DN_WIDTH), row(LANES)] + [full(a) for a in consts],
        out_specs=[row(DN_WIDTH),
                   pl.BlockSpec((batch, DN_HEADS, DN_DK, DN_DV), lambda i: (0, 0, 0, 0))],
        out_shape=[jax.ShapeDtypeStruct((batch, seq, DN_WIDTH), BF16),
                   jax.ShapeDtypeStruct((batch, DN_HEADS, DN_DK, DN_DV), F32)],
        scratch_shapes=[pltpu.VMEM((batch, N_PAIRS, DN_DK, PAIR), F32)],
        compiler_params=_params("arbitrary"),
        name="gdn_prompt",
    )(as3d(xc), as3d(dz), as3d(ba), *consts)
    return o.reshape(batch * seq, DN_WIDTH), s


GDN_S_BB = 8


def _gdn_sample_kernel(xc_ref, dz_ref, ba_ref, sc_ref, s_ref, cw_ref, alog_ref, dtb_ref, dn_ref,
                       hsum_ref, eye_ref, hsel_ref, hrep3_ref, o_ref, s_out_ref):
    xc = xc_ref[...]
    y = sc_ref[0] * cw_ref[0:1, :]
    y = y + sc_ref[1] * cw_ref[1:2, :]
    y = y + sc_ref[2] * cw_ref[2:3, :]
    y = _silu(y + xc * cw_ref[3:4, :])
    hsum = hsum_ref[...]
    q = y[:, :DN_WIDTH]
    k = y[:, DN_WIDTH:2 * DN_WIDTH]
    v = y[:, 2 * DN_WIDTH:]
    q = q * lax.rsqrt(_mm_sel_rhs(q * q, hsum) + EPS) * (DN_DK ** -0.5)
    k = k * lax.rsqrt(_mm_sel_rhs(k * k, hsum) + EPS)
    beta_c, g_c = _gdn_gates(ba_ref[...], alog_ref[...], dtb_ref[...])
    eg_c = jnp.exp(g_c)
    eye = eye_ref[...]
    tr = lambda a: lax.dot_general(a, eye, (((0,), (0,)), ((), ())), precision=lax.Precision.HIGHEST,
                                   preferred_element_type=F32)
    gates_t = tr(jnp.concatenate([beta_c, eg_c], axis=1))
    beta_t = gates_t[:LANES]
    eg_t = gates_t[LANES:]
    dz = dz_ref[...]
    dn = dn_ref[...]
    split = lambda r: jnp.concatenate([r[:, h * DN_DV:(h + 1) * DN_DV] for h in range(DN_HEADS)], axis=0)
    own_head = hsel_ref[...].astype(F32)
    hrep3 = hrep3_ref[...]
    seqs = range(GDN_S_BB)
    dot = lambda a, b: jnp.dot(a.astype(BF16), b.astype(BF16), preferred_element_type=F32)

    def pieces(x):
        p1 = x.astype(BF16).astype(F32)
        r1 = x - p1
        p2 = r1.astype(BF16).astype(F32)
        return p1, p2, (r1 - p2).astype(BF16).astype(F32)

    heads = DN_HEADS
    k_pieces, kqs = [], []
    for b in seqs:
        kq_bd = jnp.concatenate([own_head * k[b:b + 1, :], own_head * q[b:b + 1, :]], axis=0)
        a1, a2, a3 = pieces(kq_bd)
        s1, s2, s3 = pieces(s_ref[b])
        r1 = dot(jnp.concatenate([a1, a2, a3], axis=0), s1)
        r2 = dot(jnp.concatenate([a1, a2], axis=0), s2)
        r3 = dot(a1, s3)
        n = 2 * heads
        kqs.append(((r3 + r2[n:] + r1[2 * n:]) + (r2[:n] + r1[n:2 * n])) + r1[:n])
        k_pieces.append((a1[:heads], a2[:heads], a3[:heads]))
    egs = [eg_t[DN_HEADS:2 * DN_HEADS, b:b + 1] for b in seqs]
    qks = [jnp.sum(split(q[b:b + 1, :]) * split(k[b:b + 1, :]), axis=-1, keepdims=True) for b in seqs]
    v_news = [beta_t[0:DN_HEADS, b:b + 1] * (split(v[b:b + 1, :]) - eg * kq[:heads])
              for b, eg, kq in zip(seqs, egs, kqs)]
    os_ = [eg * kq[heads:] + qk * v_new for eg, kq, qk, v_new in zip(egs, kqs, qks, v_news)]
    inv_rms = [lax.rsqrt(jnp.mean(o * o, axis=-1, keepdims=True) + EPS) for o in os_]
    for b, o, r in zip(seqs, os_, inv_rms):
        o_ref[b] = o * r * dn * _silu(split(dz[b:b + 1, :]))
    outers, egrows = [], []
    for (k1, k2, k3), v_new, eg in zip(k_pieces, v_news, egs):
        v1, v2, v3 = pieces(v_new)
        lhs = jnp.concatenate([k1, k1, k2, k1, k2, k3], axis=0).astype(BF16)
        rhs = jnp.concatenate([v1, v2, v1, v3, v2, v1], axis=0).astype(BF16)
        outers.append(lax.dot_general(lhs, rhs, (((0,), (0,)), ((), ())), preferred_element_type=F32))
        egrows.append(dot(hrep3, jnp.concatenate(pieces(jnp.broadcast_to(eg, (DN_HEADS, DN_DV))), axis=0)))
    for b, outer, egrow in zip(seqs, outers, egrows):
        s_out_ref[b] = s_ref[b] * egrow + outer


def _gdn_sample(xc, dz, ba, sconv_t, state, conv_w, alog, dtb, dn):
    nseq = xc.shape[0]
    lane = np.arange(DN_WIDTH)
    hsum = jnp.asarray((lane[:, None] // DN_DV == lane[None, :] // DN_DV).astype(np.float32), dtype=BF16)
    eye = jnp.eye(GDN_S_BB, dtype=F32)
    hsel_np = (np.arange(DN_HEADS)[:, None] == lane[None, :] // DN_DK).astype(np.float32)
    hsel = jnp.asarray(hsel_np, dtype=BF16)
    hrep3 = jnp.asarray(np.tile(hsel_np.T, (1, 3)), dtype=BF16)
    row = lambda n: pl.BlockSpec((GDN_S_BB, n), lambda i: (i, 0))
    full = lambda a: pl.BlockSpec(a.shape, lambda i: (0,) * a.ndim)
    st = pl.BlockSpec((GDN_S_BB, DN_HEADS * DN_DK, DN_DV), lambda i: (i, 0, 0))
    consts = (conv_w, alog, dtb, dn, hsum, eye, hsel, hrep3)
    return pl.pallas_call(
        _gdn_sample_kernel,
        grid=(nseq // GDN_S_BB,),
        in_specs=[row(CONV_CH), row(DN_WIDTH), row(LANES),
                  pl.BlockSpec((CONV_WIDTH - 1, GDN_S_BB, CONV_CH), lambda i: (0, i, 0)), st]
                 + [full(a) for a in consts],
        out_specs=[pl.BlockSpec((GDN_S_BB, DN_HEADS, DN_DV), lambda i: (i, 0, 0)), st],
        out_shape=[jax.ShapeDtypeStruct((nseq, DN_HEADS, DN_DV), F32),
                   jax.ShapeDtypeStruct(state.shape, F32)],
        compiler_params=_params("parallel"),
        name="gdn_sample",
    )(xc, dz, ba, sconv_t, state, *consts)


def _attn_sample_lanes_kernel(att_ref, ck_ref, cv_ref, bucket_ref, rb_ref, sink_ref, o_ref, s_scr):
    g = pl.program_id(0)
    nseq = att_ref.shape[0]
    rnd = lambda a: a.astype(BF16).astype(F32)
    att = att_ref[...]
    q_all_t = (att[:, :ATT_WIDTH] * (HEAD_DIM ** -0.5)).T
    kv_new_t = att[:, ATT_WIDTH:].T
    qsel = [jnp.where(g == 0, q_all_t[hh * HEAD_DIM:(hh + 1) * HEAD_DIM],
                      q_all_t[(GQA + hh) * HEAD_DIM:(GQA + hh + 1) * HEAD_DIM]) for hh in range(GQA)]
    qr = [rnd(q) for q in qsel]
    kn = rnd(jnp.where(g == 0, kv_new_t[0:HEAD_DIM], kv_new_t[HEAD_DIM:2 * HEAD_DIM]))
    vn = rnd(jnp.where(g == 0, kv_new_t[2 * HEAD_DIM:3 * HEAD_DIM], kv_new_t[3 * HEAD_DIM:]))

    def score_row(j, carry):
        kj = rnd(ck_ref[j, 0])
        for hh in range(GQA):
            s_scr[hh, pl.ds(j, 1), :] = jnp.sum(qr[hh] * kj, axis=0, keepdims=True)
        return carry
    lax.fori_loop(0, WINDOW, score_row, 0, unroll=2)

    bucket = bucket_ref[...]
    jrow = lax.broadcasted_iota(jnp.int32, (WINDOW, nseq), 0)
    prn = []
    for hh in range(GQA):
        h = g * GQA + hh
        bias = jnp.where(jrow >= 1, _bias_lookup(bucket, rb_ref, h), NEG_INF)
        s = s_scr[hh] + bias
        s_n = jnp.sum(qr[hh] * kn, axis=0, keepdims=True) + rb_ref[0, h]
        sink = sink_ref[h]
        m = jnp.maximum(jnp.maximum(jnp.max(s, axis=0, keepdims=True), s_n), sink)
        p = jnp.exp(s - m)
        p_n = jnp.exp(s_n - m)
        den = jnp.sum(p, axis=0, keepdims=True) + p_n + jnp.exp(sink - m)
        s_scr[hh] = rnd(p / den)
        prn.append(rnd(p_n / den))

    def value_row(j, acc):
        vj = rnd(cv_ref[j, 0])
        return tuple(acc[hh] + s_scr[hh, pl.ds(j, 1), :] * vj for hh in range(GQA))
    zero = jnp.zeros((HEAD_DIM, nseq), F32)
    acc = lax.fori_loop(0, WINDOW, value_row, (zero,) * GQA, unroll=2)
    for hh in range(GQA):
        o_ref[hh * HEAD_DIM:(hh + 1) * HEAD_DIM, :] = acc[hh] + prn[hh] * vn


def _attn_sample_lanes(att, ck_t, cv_t, rel_bias, sink):
    nseq = att.shape[0]
    assert nseq == LANES
    bucket = jnp.asarray(np.broadcast_to(_t5_bucket_np(WINDOW - np.arange(WINDOW))[:, None], (WINDOW, nseq)))
    smem = pl.BlockSpec(memory_space=pltpu.SMEM)
    cache = pl.BlockSpec((WINDOW, 1, HEAD_DIM, nseq), lambda g: (0, g, 0, 0))
    full = lambda a: pl.BlockSpec(a.shape, lambda g: (0,) * a.ndim)
    return pl.pallas_call(
        _attn_sample_lanes_kernel,
        grid=(ATT_KV_HEADS,),
        in_specs=[full(att), cache, cache, full(bucket), smem, smem],
        out_specs=pl.BlockSpec((GQA * HEAD_DIM, nseq), lambda g: (g, 0)),
        out_shape=jax.ShapeDtypeStruct((ATT_WIDTH, nseq), F32),
        scratch_shapes=[pltpu.VMEM((GQA, WINDOW, nseq), F32)],
        compiler_params=_params("arbitrary"),
        name="attn_sample_lanes",
    )(att, ck_t, cv_t, bucket, rel_bias, sink)


def _gdn_sample_front_kernel(xc_ref, dz_ref, ba_ref, sc_ref, cw_ref, alog_ref, dtb_ref, hsum_ref,
                             q_ref, k_ref, v_ref, dz_t_ref, gates_ref):
    xc = xc_ref[...]
    y = sc_ref[0] * cw_ref[0:1, :]
    y = y + sc_ref[1] * cw_ref[1:2, :]
    y = y + sc_ref[2] * cw_ref[2:3, :]
    y = _silu(y + xc * cw_ref[3:4, :])
    hsum = hsum_ref[...]
    q = y[:, :DN_WIDTH]
    k = y[:, DN_WIDTH:2 * DN_WIDTH]
    q = q * lax.rsqrt(_mm_sel_rhs(q * q, hsum) + EPS) * (DN_DK ** -0.5)
    k = k * lax.rsqrt(_mm_sel_rhs(k * k, hsum) + EPS)
    beta_c, g_c = _gdn_gates(ba_ref[...], alog_ref[...], dtb_ref[...])
    q_ref[...] = q.T
    k_ref[...] = k.T
    v_ref[...] = y[:, 2 * DN_WIDTH:].T
    dz_t_ref[...] = dz_ref[...].T
    gates_ref[0:LANES, :] = beta_c.T
    gates_ref[LANES:, :] = jnp.exp(g_c).T


def _gdn_sample_step_kernel(q_ref, k_ref, v_ref, dz_ref, gates_ref, dn_ref, s_ref, o_ref, s_out_ref):
    h = pl.program_id(0)
    beta = gates_ref[pl.ds(h, 1), :]
    eg = gates_ref[pl.ds(LANES + DN_HEADS + h, 1), :]
    q, k, v = q_ref[...], k_ref[...], v_ref[...]
    w = (k * beta) * eg
    qg = q * eg
    ws = jnp.zeros(v.shape, F32)
    qs = jnp.zeros(v.shape, F32)
    for dk in range(DN_DK):
        s_dk = s_ref[0, dk]
        ws = ws + w[dk:dk + 1, :] * s_dk
        qs = qs + qg[dk:dk + 1, :] * s_dk
    v_new = v * beta - ws
    qk = jnp.sum(q * k, axis=0, keepdims=True)
    o = qs + qk * v_new
    for dk in range(DN_DK):
        s_out_ref[0, dk] = s_ref[0, dk] * eg + k[dk:dk + 1, :] * v_new
    o = o * lax.rsqrt(jnp.mean(o * o, axis=0, keepdims=True) + EPS) * dn_ref[...]
    o_ref[...] = o * _silu(dz_ref[...])


def _gdn_sample_lanes(xc, dz, ba, sconv_t, state_t, conv_w, alog, dtb, dn):
    nseq = xc.shape[0]
    assert nseq == LANES
    lane = np.arange(DN_WIDTH)
    hsum = jnp.asarray((lane[:, None] // DN_DV == lane[None, :] // DN_DV).astype(np.float32), dtype=BF16)
    full = lambda a: pl.BlockSpec(a.shape, lambda i: (0,) * a.ndim)
    cm = jax.ShapeDtypeStruct((DN_WIDTH, nseq), F32)
    front_in = (xc, dz, ba, sconv_t, conv_w, alog, dtb, hsum)
    q_t, k_t, v_t, dz_t, gates_t = pl.pallas_call(
        _gdn_sample_front_kernel,
        grid=(1,),
        in_specs=[full(a) for a in front_in],
        out_specs=[pl.BlockSpec((DN_WIDTH, nseq), lambda i: (0, 0))] * 4
                  + [pl.BlockSpec((2 * LANES, nseq), lambda i: (0, 0))],
        out_shape=[cm, cm, cm, cm, jax.ShapeDtypeStruct((2 * LANES, nseq), F32)],
        compiler_params=_params("arbitrary"),
        name="gdn_sample_front",
    )(*front_in)
    dn_b = jnp.broadcast_to(dn.reshape(DN_DV, 1), (DN_DV, nseq))
    head = pl.BlockSpec((DN_DK, nseq), lambda h: (h, 0))
    st = pl.BlockSpec((1, DN_DK, DN_DV, nseq), lambda h: (h, 0, 0, 0))
    return pl.pallas_call(
        _gdn_sample_step_kernel,
        grid=(DN_HEADS,),
        in_specs=[head, head, head, head, full(gates_t), full(dn_b), st],
        out_specs=[head, st],
        out_shape=[cm, jax.ShapeDtypeStruct(state_t.shape, F32)],
        compiler_params=_params("parallel"),
        name="gdn_sample_step",
    )(q_t, k_t, v_t, dz_t, gates_t, dn_b, state_t)


def _route(xn, wr):
    logits = jnp.dot(xn, wr, preferred_element_type=F32)
    lane = lax.broadcasted_iota(jnp.int32, logits.shape, 1).astype(F32)
    first_at = lambda hit: jnp.min(jnp.where(hit, lane, float(LANES)), axis=-1, keepdims=True)
    glog = jnp.where(lane < N_GROUPS, logits, NEG_INF)
    gmax = jnp.max(glog, axis=-1, keepdims=True)
    gsel = first_at(glog == gmax)
    pgsel = 1.0 / jnp.sum(jnp.exp(glog - gmax), axis=-1, keepdims=True)
    lo = ROUTER_OFF + gsel * EXPERTS_PER_GROUP
    in_group = jnp.logical_and(lane >= lo, lane < lo + EXPERTS_PER_GROUP)
    elog = jnp.where(in_group, logits, NEG_INF)
    m1 = jnp.max(elog, axis=-1, keepdims=True)
    i1 = first_at(elog == m1)
    z = jnp.sum(jnp.exp(elog - m1), axis=-1, keepdims=True)
    elog2 = jnp.where(lane == i1, NEG_INF, elog)
    m2 = jnp.max(elog2, axis=-1, keepdims=True)
    i2 = first_at(elog2 == m2)
    p1 = 1.0 / z
    p2 = jnp.exp(m2 - m1) / z
    tot = p1 + p2
    return lane, i1, i2, p1 / tot * pgsel, p2 / tot * pgsel


def _outproj(x_ref, oa_ref, od_ref, wo_ref):
    return x_ref[...] + _mm(oa_ref[...], wo_ref[:ATT_WIDTH, :]) + _mm(od_ref[...], wo_ref[ATT_WIDTH:, :])


def _outproj_router_kernel(x_ref, oa_ref, od_t_ref, wo_ref, g_ref, wr_ref, h_ref, xn_ref, gate_ref):
    h = (x_ref[...] + _mm(oa_ref[...], wo_ref[:ATT_WIDTH, :])
         + _mm(od_t_ref[...].T, wo_ref[ATT_WIDTH:, :]))
    h_ref[...] = h
    xn = _rmsnorm(h, g_ref[...]).astype(BF16)
    xn_ref[...] = xn
    lane, i1, i2, g1, g2 = _route(xn, wr_ref[...])
    gate_ref[...] = jnp.where(lane == i1, g1, 0.0) + jnp.where(lane == i2, g2, 0.0)


def _outproj_router(x, oa, od_t, wo, g, wr):
    t = x.shape[0]
    tm = t
    row = lambda n: pl.BlockSpec((tm, n), lambda i: (i, 0))
    full = lambda a: pl.BlockSpec(a.shape, lambda i: (0,) * a.ndim)
    return pl.pallas_call(
        _outproj_router_kernel,
        grid=(t // tm,),
        in_specs=[row(D_MODEL), row(ATT_WIDTH), full(od_t), full(wo), full(g), full(wr)],
        out_specs=[row(D_MODEL), row(D_MODEL), row(LANES)],
        out_shape=[jax.ShapeDtypeStruct((t, D_MODEL), F32), jax.ShapeDtypeStruct((t, D_MODEL), BF16),
                   jax.ShapeDtypeStruct((t, LANES), F32)],
        compiler_params=_params("parallel"),
        name="outproj_router",
    )(x, oa, od_t, wo, g, wr)


def _moe_kernel(xn_ref, gate_ref, wg_ref, wu_ref, wd_ref, o_ref):
    e = pl.program_id(1)
    xn = xn_ref[...]
    lane = lax.broadcasted_iota(jnp.int32, gate_ref.shape, 1)
    gate = jnp.sum(jnp.where(lane == e + ROUTER_OFF, gate_ref[...], 0.0), axis=-1, keepdims=True)
    hg = jnp.dot(xn, wg_ref[...].astype(BF16), preferred_element_type=F32)
    hu = jnp.dot(xn, wu_ref[...].astype(BF16), preferred_element_type=F32)
    hm = _silu(hg) * hu * gate
    y = jnp.dot(hm.astype(BF16), wd_ref[...].astype(BF16), preferred_element_type=F32)

    @pl.when(e == 0)
    def _():
        o_ref[...] = y

    @pl.when(e > 0)
    def _():
        o_ref[...] += y


def _moe(xn, gates, wg, wu, wd):
    t = xn.shape[0]
    tm = min(t, 1024)
    return pl.pallas_call(
        _moe_kernel,
        grid=(t // tm, N_EXPERTS),
        in_specs=[pl.BlockSpec((tm, D_MODEL), lambda i, e: (i, 0)),
                  pl.BlockSpec((tm, LANES), lambda i, e: (i, 0)),
                  pl.BlockSpec((None, D_MODEL, D_EXPERT), lambda i, e: (e, 0, 0)),
                  pl.BlockSpec((None, D_MODEL, D_EXPERT), lambda i, e: (e, 0, 0)),
                  pl.BlockSpec((None, D_EXPERT, D_MODEL), lambda i, e: (e, 0, 0))],
        out_specs=pl.BlockSpec((tm, D_MODEL), lambda i, e: (i, 0)),
        out_shape=jax.ShapeDtypeStruct((t, D_MODEL), F32),
        compiler_params=_params("parallel", "arbitrary"),
        name="moe",
    )(xn, gates, wg, wu, wd)


MOE_TM = 512
POS_TM = 1024
INFO_G1, INFO_G2, INFO_E1, INFO_E2 = 0, 1, 2, 3
DMA_UNROLL = 8


def _moe_tiles(t):
    return (2 * t) // MOE_TM + N_EXPERTS


HALF = D_MODEL // 2
U32 = jnp.uint32


def _pack_rows(x):
    bits = lambda v: lax.bitcast_convert_type(v.astype(BF16).astype(F32), U32)
    return bits(x[:, HALF:]) | (bits(x[:, :HALF]) >> 16)


def _unpack_rows(w):
    lo = lax.bitcast_convert_type(w << 16, F32)
    hi = lax.bitcast_convert_type(w & jnp.uint32(0xFFFF0000), F32)
    return lo, hi


def _route_kernel(x_ref, oa_ref, od_ref, wo_ref, g_ref, wr_ref, h_ref, xn_ref, info_ref, cnt_ref, run_scr):
    h = _outproj(x_ref, oa_ref, od_ref, wo_ref)
    h_ref[...] = h
    xn = _rmsnorm(h, g_ref[...])
    xn_ref[...] = _pack_rows(xn)
    lane, i1, i2, g1, g2 = _route(xn.astype(BF16), wr_ref[...])
    info = jnp.where(lane == INFO_G1, g1, 0.0) + jnp.where(lane == INFO_G2, g2, 0.0)
    info = info + jnp.where(lane == INFO_E1, i1, 0.0) + jnp.where(lane == INFO_E2, i2, 0.0)
    info_ref[...] = info

    @pl.when(pl.program_id(0) == 0)
    def _():
        run_scr[...] = jnp.zeros(run_scr.shape, F32)
    picked = jnp.logical_or(lane == i1, lane == i2).astype(F32)
    run_scr[...] += jnp.sum(picked, axis=0, keepdims=True)
    cnt_ref[...] = run_scr[...]


def _route_sparse(x, oa, od, wo, g, wr):
    t = x.shape[0]
    tm = ROW_TM
    row = lambda n: pl.BlockSpec((tm, n), lambda i: (i, 0))
    full = lambda a: pl.BlockSpec(a.shape, lambda i: (0,) * a.ndim)
    return pl.pallas_call(
        _route_kernel,
        grid=(t // tm,),
        in_specs=[row(D_MODEL), row(ATT_WIDTH), row(DN_WIDTH), full(wo), full(g), full(wr)],
        out_specs=[row(D_MODEL), row(HALF), row(LANES), pl.BlockSpec((1, LANES), lambda i: (0, 0))],
        out_shape=[jax.ShapeDtypeStruct((t, D_MODEL), F32), jax.ShapeDtypeStruct((t, HALF), U32),
                   jax.ShapeDtypeStruct((t, LANES), F32), jax.ShapeDtypeStruct((1, LANES), F32)],
        scratch_shapes=[pltpu.VMEM((1, LANES), F32)],
        compiler_params=_params("arbitrary"),
        name="route",
    )(x, oa, od, wo, g, wr)


def _positions_kernel(info_ref, cnt_ref, ltri_ref, utri_ref, pos_ref, run_scr, off_scr):
    info = info_ref[...]
    lane = lax.broadcasted_iota(jnp.int32, info.shape, 1).astype(F32)
    hit1 = lane == info[:, INFO_E1:INFO_E1 + 1]
    hit2 = lane == info[:, INFO_E2:INFO_E2 + 1]
    onehot = jnp.logical_or(hit1, hit2).astype(F32)

    @pl.when(pl.program_id(0) == 0)
    def _():
        ln = lax.broadcasted_iota(jnp.int32, cnt_ref.shape, 1)
        is_expert = jnp.logical_and(ln >= ROUTER_OFF, ln < ROUTER_OFF + N_EXPERTS)
        tiles = jnp.where(is_expert, jnp.maximum(jnp.floor((cnt_ref[...] + (MOE_TM - 1)) * (1.0 / MOE_TM)), 1.0), 0.0)
        off_scr[...] = MOE_TM * jnp.dot(tiles.astype(BF16), utri_ref[...], preferred_element_type=F32)
        run_scr[...] = jnp.zeros(run_scr.shape, F32)

    before = (jnp.dot(ltri_ref[...], onehot.astype(BF16), preferred_element_type=F32)
              + run_scr[...] + off_scr[...])
    pos1 = jnp.sum(jnp.where(hit1, before, 0.0), axis=-1, keepdims=True)
    pos2 = jnp.sum(jnp.where(hit2, before, 0.0), axis=-1, keepdims=True)
    both = jnp.where(lane == 0, pos1, 0.0) + jnp.where(lane == 1, pos2, 0.0)
    pos_ref[...] = both.T.astype(jnp.int32)
    run_scr[...] += jnp.sum(onehot, axis=0, keepdims=True)


def _positions(info, cnt):
    t = info.shape[0]
    tm = min(t, POS_TM)
    tok = np.arange(tm)
    ltri = jnp.asarray((tok[:, None] > tok[None, :]).astype(np.float32), dtype=BF16)
    ln = np.arange(LANES)
    utri = jnp.asarray((ln[:, None] < ln[None, :]).astype(np.float32), dtype=BF16)
    full = lambda a: pl.BlockSpec(a.shape, lambda i: (0,) * a.ndim)
    return pl.pallas_call(
        _positions_kernel,
        grid=(t // tm,),
        in_specs=[pl.BlockSpec((tm, LANES), lambda i: (i, 0)), full(cnt), full(ltri), full(utri)],
        out_specs=pl.BlockSpec((LANES, tm), lambda i: (0, i)),
        out_shape=jax.ShapeDtypeStruct((LANES, t), jnp.int32),
        scratch_shapes=[pltpu.VMEM((1, LANES), F32), pltpu.VMEM((1, LANES), F32)],
        compiler_params=_params("arbitrary"),
        name="positions",
    )(info, cnt, ltri, utri)


def _row_copy(src_hbm, src_row, dst_hbm, dst_row, sem):
    return pltpu.make_async_copy(src_hbm.at[pl.ds(src_row, 1)], dst_hbm.at[pl.ds(dst_row, 1)], sem)


SCATTER_SLOTS = 3


def _scatter_kernel(pos1_ref, pos2_ref, last_ref, used_ref, nt_ref, xn_hbm, zero_hbm, xs_hbm,
                    buf, lsem, sem, zsem, *, n_tok):
    max_tiles = xs_hbm.shape[0] // MOE_TM

    def zero_tile(tile):
        return pltpu.make_async_copy(zero_hbm, xs_hbm.at[pl.ds(tile * MOE_TM, MOE_TM)], zsem)

    def for_unused(fn):
        def body(tile, carry):
            fn(tile)
            return carry
        lax.fori_loop(nt_ref[0], max_tiles, body, 0)

    for e in range(N_EXPERTS):
        @pl.when(used_ref[e] > 0)
        def _():
            zero_tile(last_ref[e]).start()
    for_unused(lambda tile: zero_tile(tile).start())
    for e in range(N_EXPERTS):
        @pl.when(used_ref[e] > 0)
        def _():
            zero_tile(last_ref[e]).wait()
    for_unused(lambda tile: zero_tile(tile).wait())

    tm = buf.shape[1]
    n = n_tok // tm

    def load(i):
        return pltpu.make_async_copy(xn_hbm.at[pl.ds(i * tm, tm)], buf.at[i % SCATTER_SLOTS],
                                     lsem.at[i % SCATTER_SLOTS])

    def wait_rows(slot):
        pltpu.make_async_copy(xs_hbm.at[pl.ds(0, 2 * tm)], xs_hbm.at[pl.ds(0, 2 * tm)], sem.at[slot]).wait()

    load(0).start()
    load(1).start()

    def step(i, carry):
        slot = i % SCATTER_SLOTS
        load(i).wait()

        def body(j, c2):
            tok = i * tm + j
            src = buf.at[slot, pl.ds(j, 1)]
            pltpu.make_async_copy(src, xs_hbm.at[pl.ds(pos1_ref[tok], 1)], sem.at[slot]).start()
            pltpu.make_async_copy(src, xs_hbm.at[pl.ds(pos2_ref[tok], 1)], sem.at[slot]).start()
            return c2
        lax.fori_loop(0, tm, body, 0, unroll=DMA_UNROLL)

        @pl.when(i >= 1)
        def _():
            wait_rows((i - 1) % SCATTER_SLOTS)

        @pl.when(i + 2 < n)
        def _():
            load(i + 2).start()
        return carry
    lax.fori_loop(0, n, step, 0)
    wait_rows((n - 1) % SCATTER_SLOTS)


def _scatter_rows(xn, pos1, pos2, last_tile, used, n_tiles, n_rows):
    t = xn.shape[0]
    zero = jnp.zeros((MOE_TM, D_MODEL), F32)
    any_spec = pl.BlockSpec(memory_space=pl.ANY)
    return pl.pallas_call(
        functools.partial(_scatter_kernel, n_tok=t),
        grid_spec=pltpu.PrefetchScalarGridSpec(
            num_scalar_prefetch=5, grid=(1,),
            in_specs=[any_spec, any_spec], out_specs=any_spec,
            scratch_shapes=[pltpu.VMEM((SCATTER_SLOTS, MOE_TM, D_MODEL), F32),
                            pltpu.SemaphoreType.DMA((SCATTER_SLOTS,)),
                            pltpu.SemaphoreType.DMA((SCATTER_SLOTS,)),
                            pltpu.SemaphoreType.DMA]),
        out_shape=jax.ShapeDtypeStruct((n_rows, D_MODEL), F32),
        compiler_params=_params("arbitrary"),
        name="scatter_rows",
    )(pos1, pos2, last_tile, used, n_tiles, xn, zero)


def _experts_kernel(te_ref, tv_ref, nt_ref, xs_ref, wg_hbm, wu_hbm, wd_hbm, xn_new_ref, gate_new_ref,
                    ys_ref, moe_new_ref, wg_s, wu_s, wd_s, wg_f, wu_f, wd_f, wsem):
    i = pl.program_id(0)
    used = i < nt_ref[0]
    expert = te_ref[i]

    def fetch(e):
        slot = e % 2
        return [pltpu.make_async_copy(src.at[e], dst.at[slot], wsem.at[slot, j])
                for j, (src, dst) in enumerate(((wg_hbm, wg_f), (wu_hbm, wu_f), (wd_hbm, wd_f)))]

    @pl.when(jnp.logical_or(i == 0, expert != te_ref[jnp.maximum(i - 1, 0)]))
    def _():
        @pl.when(i == 0)
        def _():
            for c in fetch(expert):
                c.start()
        for c in fetch(expert):
            c.wait()

        @pl.when(expert + 1 < N_EXPERTS)
        def _():
            for c in fetch(expert + 1):
                c.start()
        slot = expert % 2
        wg_s[...] = wg_f[slot].astype(BF16)
        wu_s[...] = wu_f[slot].astype(BF16)
        wd_s[...] = wd_f[slot].astype(BF16)
        xn = xn_new_ref[...]
        lane = lax.broadcasted_iota(jnp.int32, gate_new_ref.shape, 1)
        gate = jnp.sum(jnp.where(lane == expert + ROUTER_OFF, gate_new_ref[...], 0.0), axis=-1, keepdims=True)
        hg = jnp.dot(xn, wg_s[...], preferred_element_type=F32)
        hu = jnp.dot(xn, wu_s[...], preferred_element_type=F32)
        hm = _silu(hg) * hu * gate
        y = jnp.dot(hm.astype(BF16), wd_s[...], preferred_element_type=F32)

        @pl.when(i == 0)
        def _():
            moe_new_ref[...] = y

        @pl.when(i > 0)
        def _():
            moe_new_ref[...] += y

    @pl.when(used)
    def _():
        row = lax.broadcasted_iota(jnp.int32, xs_ref.shape, 0)
        x_lo, x_hi = _unpack_rows(jnp.where(row < tv_ref[i], xs_ref[...], jnp.uint32(0)))
        x_lo = x_lo.astype(BF16)
        x_hi = x_hi.astype(BF16)
        up = lambda w_s: (jnp.dot(x_lo, w_s[:HALF, :], preferred_element_type=F32)
                          + jnp.dot(x_hi, w_s[HALF:, :], preferred_element_type=F32))
        hm = (_silu_tanh(up(wg_s)) * up(wu_s)).astype(BF16)
        ys_ref[...] = _pack_rows(jnp.dot(hm, wd_s[...], preferred_element_type=F32))

    @pl.when(jnp.logical_not(used))
    def _():
        ys_ref[...] = jnp.zeros(ys_ref.shape, U32)


def _experts(xs, tile_expert, tile_valid, n_tiles, wg, wu, wd, xn_new, gate_new):
    max_tiles = xs.shape[0] // MOE_TM
    rows = pl.BlockSpec((MOE_TM, HALF), lambda i, te, tv, nt: (i, 0))
    hbm = pl.BlockSpec(memory_space=pl.ANY)
    full = lambda a: pl.BlockSpec(a.shape, lambda i, te, tv, nt: (0,) * a.ndim)
    return pl.pallas_call(
        _experts_kernel,
        grid_spec=pltpu.PrefetchScalarGridSpec(
            num_scalar_prefetch=3, grid=(max_tiles,),
            in_specs=[rows, hbm, hbm, hbm, full(xn_new), full(gate_new)],
            out_specs=[rows, pl.BlockSpec(xn_new.shape, lambda i, te, tv, nt: (0, 0))],
            scratch_shapes=[pltpu.VMEM((D_MODEL, D_EXPERT), BF16), pltpu.VMEM((D_MODEL, D_EXPERT), BF16),
                            pltpu.VMEM((D_EXPERT, D_MODEL), BF16),
                            pltpu.VMEM((2, D_MODEL, D_EXPERT), F32), pltpu.VMEM((2, D_MODEL, D_EXPERT), F32),
                            pltpu.VMEM((2, D_EXPERT, D_MODEL), F32), pltpu.SemaphoreType.DMA((2, 3))]),
        out_shape=[jax.ShapeDtypeStruct(xs.shape, U32), jax.ShapeDtypeStruct(xn_new.shape, F32)],
        compiler_params=_params("arbitrary"),
        name="experts",
    )(tile_expert, tile_valid, n_tiles, xs, wg, wu, wd, xn_new, gate_new)


def _ple_gather_kernel(pos1_ref, pos2_ref, h_ref, info_ref, p_ref, wpp_ref, wpg_ref, gp_ref, gf_ref,
                       ys_hbm, y_ref, ybuf, sem):
    i = pl.program_id(0)
    n = pl.num_programs(0)
    tm = h_ref.shape[0]

    def issue(tile, slot):
        def body(j, carry):
            tok = tile * tm + j
            pltpu.make_async_copy(ys_hbm.at[pl.ds(pos1_ref[tok], 1)], ybuf.at[slot, 0, pl.ds(j, 1)],
                                  sem.at[slot]).start()
            pltpu.make_async_copy(ys_hbm.at[pl.ds(pos2_ref[tok], 1)], ybuf.at[slot, 1, pl.ds(j, 1)],
                                  sem.at[slot]).start()
            return carry
        lax.fori_loop(0, tm, body, 0, unroll=DMA_UNROLL)

    @pl.when(i == 0)
    def _():
        issue(0, 0)

    @pl.when(i + 1 < n)
    def _():
        issue(i + 1, (i + 1) % 2)

    slot = i % 2
    pltpu.make_async_copy(ybuf.at[slot], ybuf.at[slot], sem.at[slot]).wait()
    info = info_ref[...]
    moe = info[:, INFO_G1:INFO_G1 + 1] * ybuf[slot, 0] + info[:, INFO_G2:INFO_G2 + 1] * ybuf[slot, 1]
    h = h_ref[...] + moe
    hn = _rmsnorm(h, gp_ref[...])
    h = h + _mm(p_ref[...], wpp_ref[...]) * _sigmoid(_mm(hn, wpg_ref[...]))
    y_ref[...] = _rmsnorm(h, gf_ref[...])


def _ple_gather(h, info, p, ys, pos1, pos2, wpp, wpg, gp, gf):
    t = h.shape[0]
    tm = 256
    row = lambda n: pl.BlockSpec((tm, n), lambda i, p1, p2: (i, 0))
    full = lambda a: pl.BlockSpec(a.shape, lambda i, p1, p2: (0,) * a.ndim)
    return pl.pallas_call(
        _ple_gather_kernel,
        grid_spec=pltpu.PrefetchScalarGridSpec(
            num_scalar_prefetch=2, grid=(t // tm,),
            in_specs=[row(D_MODEL), row(LANES), row(PLE_DIM), full(wpp), full(wpg), full(gp), full(gf),
                      pl.BlockSpec(memory_space=pl.ANY)],
            out_specs=row(D_MODEL),
            scratch_shapes=[pltpu.VMEM((2, 2, tm, D_MODEL), F32), pltpu.SemaphoreType.DMA((2,))]),
        out_shape=jax.ShapeDtypeStruct((t, D_MODEL), F32),
        compiler_params=_params("arbitrary"),
        name="ple_gather",
    )(pos1, pos2, h, info, p, wpp, wpg, gp, gf, ys)


SC_IDX = 128
SC_ROWS = 64
SC_WORKERS = 32


def _sc_mesh():
    return plsc.VectorSubcoreMesh(core_axis_name="c", subcore_axis_name="s")


def _sc_windows(t, fn):
    per_worker = t // SC_WORKERS
    worker = lax.axis_index(("c", "s"))

    @pl.loop(0, per_worker // SC_IDX)
    def _(w):
        fn(worker * per_worker + w * SC_IDX)


def _sc_scatter_rows(xn, pos1, pos2, n_rows):
    t, d = xn.shape
    assert t % (SC_WORKERS * SC_IDX) == 0
    idx_t = pltpu.VMEM((1, SC_IDX), jnp.int32)

    @pl.kernel(out_type=jax.ShapeDtypeStruct((n_rows, d), xn.dtype), mesh=_sc_mesh(),
               scratch_types=[idx_t, idx_t, pltpu.VMEM((SC_ROWS, d), xn.dtype)])
    def scatter(x_hbm, p1_hbm, p2_hbm, o_hbm, i1_v, i2_v, buf):
        def window(base):
            pltpu.sync_copy(p1_hbm.at[:, pl.ds(base, SC_IDX)], i1_v)
            pltpu.sync_copy(p2_hbm.at[:, pl.ds(base, SC_IDX)], i2_v)
            for k in range(SC_IDX // SC_ROWS):
                pltpu.sync_copy(x_hbm.at[pl.ds(base + k * SC_ROWS, SC_ROWS)], buf)
                pltpu.sync_copy(buf, o_hbm.at[i1_v.at[0, pl.ds(k * SC_ROWS, SC_ROWS)]])
                pltpu.sync_copy(buf, o_hbm.at[i2_v.at[0, pl.ds(k * SC_ROWS, SC_ROWS)]])
        _sc_windows(t, window)

    return scatter(xn, pos1.reshape(1, t), pos2.reshape(1, t))


def _sc_gather_rows(ys, pos1, pos2):
    t = pos1.shape[0]
    d = ys.shape[1]
    assert t % (SC_WORKERS * SC_IDX) == 0
    idx_t = pltpu.VMEM((1, SC_IDX), jnp.int32)
    out = jax.ShapeDtypeStruct((t, d), ys.dtype)

    buf_t = pltpu.VMEM((SC_ROWS, d), ys.dtype)

    @pl.kernel(out_type=(out, out), mesh=_sc_mesh(),
               scratch_types=[idx_t, idx_t, buf_t, buf_t, pltpu.SemaphoreType.DMA((2,)),
                              pltpu.SemaphoreType.DMA((2,))])
    def gather(y_hbm, p1_hbm, p2_hbm, o1_hbm, o2_hbm, i1_v, i2_v, buf_a, buf_b, gsem, wsem):
        bufs = (buf_a, buf_b)

        def window(base):
            pltpu.sync_copy(p1_hbm.at[:, pl.ds(base, SC_IDX)], i1_v)
            pltpu.sync_copy(p2_hbm.at[:, pl.ds(base, SC_IDX)], i2_v)
            items = [(idx_v, o_hbm, k) for k in range(SC_IDX // SC_ROWS)
                     for idx_v, o_hbm in ((i1_v, o1_hbm), (i2_v, o2_hbm))]

            def read(n):
                idx_v, _, k = items[n]
                return pltpu.make_async_copy(y_hbm.at[idx_v.at[0, pl.ds(k * SC_ROWS, SC_ROWS)]],
                                             bufs[n % 2], gsem.at[n % 2])

            def write(n):
                _, o_hbm, k = items[n]
                return pltpu.make_async_copy(bufs[n % 2], o_hbm.at[pl.ds(base + k * SC_ROWS, SC_ROWS)],
                                             wsem.at[n % 2])

            read(0).start()
            for n in range(len(items)):
                read(n).wait()
                if n >= 1:
                    write(n - 1).wait()
                if n + 1 < len(items):
                    read(n + 1).start()
                write(n).start()
            write(len(items) - 1).wait()
        _sc_windows(t, window)

    return gather(ys, pos1.reshape(1, t), pos2.reshape(1, t))


def _ple_sparse_kernel(h_ref, info_ref, y1_ref, y2_ref, p_ref, wpp_ref, wpg_ref, gp_ref, gf_ref, y_ref):
    info = info_ref[...]
    g1 = info[:, INFO_G1:INFO_G1 + 1]
    g2 = info[:, INFO_G2:INFO_G2 + 1]
    y1_lo, y1_hi = _unpack_rows(y1_ref[...])
    y2_lo, y2_hi = _unpack_rows(y2_ref[...])
    moe = jnp.concatenate([g1 * y1_lo + g2 * y2_lo, g1 * y1_hi + g2 * y2_hi], axis=1)
    h = h_ref[...] + moe
    hn = _rmsnorm(h, gp_ref[...])
    h = h + _mm(p_ref[...], wpp_ref[...]) * _sigmoid(_mm(hn, wpg_ref[...]))
    y_ref[...] = _rmsnorm(h, gf_ref[...])


def _ple_sparse(h, info, y1, y2, p, wpp, wpg, gp, gf):
    t = h.shape[0]
    tm = ROW_TM
    row = lambda n: pl.BlockSpec((tm, n), lambda i: (i, 0))
    full = lambda a: pl.BlockSpec(a.shape, lambda i: (0,) * a.ndim)
    return pl.pallas_call(
        _ple_sparse_kernel,
        grid=(t // tm,),
        in_specs=[row(D_MODEL), row(LANES), row(HALF), row(HALF), row(PLE_DIM),
                  full(wpp), full(wpg), full(gp), full(gf)],
        out_specs=row(D_MODEL),
        out_shape=jax.ShapeDtypeStruct((t, D_MODEL), F32),
        compiler_params=_params("parallel"),
        name="ple_sparse",
    )(h, info, y1, y2, p, wpp, wpg, gp, gf)


def _tile_tables(cnt, max_tiles):
    tiles_e = jnp.maximum((cnt + (MOE_TM - 1)) // MOE_TM, 1)
    ends = jnp.cumsum(tiles_e)
    n_tiles = ends[-1]
    tile = jnp.arange(max_tiles, dtype=jnp.int32)
    idx = jnp.minimum(tile, n_tiles - 1)
    tile_expert = jnp.sum((idx[:, None] >= ends[None, :]).astype(jnp.int32), axis=1)
    mine = tile_expert[:, None] == jnp.arange(N_EXPERTS, dtype=jnp.int32)[None, :]
    of_mine = lambda v: jnp.sum(jnp.where(mine, v[None, :], 0), axis=1)
    valid = jnp.clip(of_mine(cnt) - (idx - of_mine(ends - tiles_e)) * MOE_TM, 0, MOE_TM)
    tile_valid = jnp.where(tile < n_tiles, valid, 0).astype(jnp.int32)
    return (tile_expert, tile_valid, n_tiles.reshape(1), (ends - 1).astype(jnp.int32),
            tiles_e.astype(jnp.int32))


def _ple_final_kernel(h_ref, m_ref, p_ref, wpp_ref, wpg_ref, gp_ref, gf_ref, y_ref):
    h = h_ref[...] + m_ref[...]
    hn = _rmsnorm(h, gp_ref[...])
    h = h + _mm(p_ref[...], wpp_ref[...]) * _sigmoid(_mm(hn, wpg_ref[...]))
    y_ref[...] = _rmsnorm(h, gf_ref[...])


def _ple_final(h, m, p, wpp, wpg, gp, gf):
    t = h.shape[0]
    tm = min(t, 256)
    row = lambda n: pl.BlockSpec((tm, n), lambda i: (i, 0))
    full = lambda a: pl.BlockSpec(a.shape, lambda i: (0,) * a.ndim)
    return pl.pallas_call(
        _ple_final_kernel,
        grid=(t // tm,),
        in_specs=[row(D_MODEL), row(D_MODEL), row(PLE_DIM), full(wpp), full(wpg), full(gp), full(gf)],
        out_specs=row(D_MODEL),
        out_shape=jax.ShapeDtypeStruct((t, D_MODEL), F32),
        compiler_params=_params("parallel"),
        name="ple_final",
    )(h, m, p, wpp, wpg, gp, gf)


def kernel(x_prompt, x_sample, p_prompt, p_sample, cache_k, cache_v, state_conv, state_S, rel_bias, norm_mix, w_in, att_sink, conv_w, dn_A_log, dn_dt_bias, dn_norm, w_out, norm_ffn, w_router_group, w_router_expert, w_gate, w_up, w_down, w_ple_proj, w_ple_gate, norm_ple, norm_final):
    batch, seq, _ = x_prompt.shape
    nseq = x_sample.shape[0]
    assert x_sample.shape[1] == 1 and norm_mix.shape[0] == 1 and cache_k.shape[2] == WINDOW
    assert seq % GDN_TB == 0 and seq % ATT_BLOCK == 0

    wi = w_in[0]
    o_db = ATT_COLS + CONV_CH
    w_in_re = jnp.concatenate(
        [wi[:, :o_db], wi[:, o_db + 2 * DN_HEADS:], wi[:, o_db:o_db + 2 * DN_HEADS],
         jnp.zeros((D_MODEL, LANES - 2 * DN_HEADS), F32)], axis=1).astype(BF16)
    row = lambda a: a.reshape(1, -1).astype(F32)
    pad_lanes = lambda a, off: jnp.zeros((1, LANES), F32).at[0, off:off + a.shape[0]].set(a)
    alog = pad_lanes(dn_A_log[0], DN_HEADS)
    dtb = pad_lanes(dn_dt_bias[0], DN_HEADS)
    dnx = jnp.tile(dn_norm[0], DN_HEADS).reshape(1, DN_WIDTH)
    w_router = jnp.concatenate(
        [w_router_group[0], w_router_expert[0],
         jnp.zeros((D_MODEL, LANES - N_GROUPS - N_EXPERTS), F32)], axis=1).astype(BF16)
    wo = w_out[0].astype(BF16)
    wg, wu, wd = w_gate[0], w_up[0], w_down[0]
    wpp, wpg = w_ple_proj[0].astype(BF16), w_ple_gate[0].astype(BF16)
    sink = att_sink[0]

    qi = np.arange(ATT_BLOCK)[:, None]
    kj = np.arange(2 * ATT_BLOCK)[None, :]
    bucket_p = jnp.asarray(_t5_bucket_np(qi + ATT_BLOCK - kj))
    bucket_s = jnp.asarray(_t5_bucket_np(WINDOW - np.arange(WINDOW)[None, :]))

    xp = x_prompt.reshape(batch * seq, D_MODEL)
    att_p, qkv_p, dz_p, ba_p, xc_tails = _inproj_conv(xp, row(norm_mix[0]), w_in_re, conv_w[0], seq)
    o_att_p = _attn_prompt(att_p, bucket_p, rel_bias, sink, batch, seq)
    o_dn_p, s_p = _gdn_prompt(qkv_p, dz_p, ba_p, alog, dtb, dnx, batch, seq)
    h1, xn2, info, cnt = _route_sparse(xp, o_att_p, o_dn_p, wo, row(norm_ffn[0]), w_router)
    pos = _positions(info, cnt)
    pos1, pos2 = pos[0], pos[1]
    max_tiles = _moe_tiles(batch * seq)
    cnt_e = cnt[0, ROUTER_OFF:ROUTER_OFF + N_EXPERTS].astype(jnp.int32)
    tile_expert, tile_valid, n_tiles, last_tile, used = _tile_tables(cnt_e, max_tiles)
    xs_sorted = _sc_scatter_rows(xn2, pos1, pos2, max_tiles * MOE_TM)

    xs = x_sample.reshape(nseq, D_MODEL)
    att_s, xc_s, dz_s, ba_s = _inproj(xs, row(norm_mix[0]), w_in_re)
    ck_t = jnp.transpose(cache_k[0], (0, 2, 3, 1))
    cv_t = jnp.transpose(cache_v[0], (0, 2, 3, 1))
    o_att_s, ks_t, vs_t = _attn_sample(att_s, ck_t, cv_t, bucket_s, rel_bias, sink)
    sconv_t = jnp.swapaxes(state_conv[0], 0, 1)
    o_dn_s_t, s_s_t = _gdn_sample_lanes(xc_s, dz_s, ba_s, sconv_t, jnp.transpose(state_S[0], (1, 2, 3, 0)),
                                        conv_w[0], alog, dtb, dn_norm[0])
    s_s = jnp.transpose(s_s_t, (3, 0, 1, 2))

    h1_s, xn2_s, gates_s = _outproj_router(xs, o_att_s, o_dn_s_t, wo, row(norm_ffn[0]), w_router)

    ys, moe_s = _experts(xs_sorted, tile_expert, tile_valid, n_tiles, wg, wu, wd, xn2_s, gates_s)
    y1, y2 = _sc_gather_rows(ys, pos1, pos2)
    y_s = _ple_final(h1_s, moe_s, p_sample[0].reshape(nseq, PLE_DIM), wpp, wpg, row(norm_ple[0]),
                     row(norm_final))
    y_p = _ple_sparse(h1, info, y1, y2, p_prompt[0].reshape(batch * seq, PLE_DIM),
                      wpp, wpg, row(norm_ple[0]), row(norm_final))

    att_p3 = att_p.reshape(batch, seq, ATT_COLS)
    kv_shape = (1, batch, WINDOW, ATT_KV_HEADS, HEAD_DIM)
    k_p = att_p3[:, seq - WINDOW:, ATT_WIDTH:ATT_WIDTH + KV_WIDTH].reshape(kv_shape)
    v_p = att_p3[:, seq - WINDOW:, ATT_WIDTH + KV_WIDTH:].reshape(kv_shape)
    conv_p = xc_tails.reshape(batch, -1, TAIL, CONV_CH)[:, -1, TAIL - (CONV_WIDTH - 1):][None]
    k_s = jnp.transpose(ks_t, (0, 3, 1, 2))[None]
    v_s = jnp.transpose(vs_t, (0, 3, 1, 2))[None]
    conv_s = jnp.concatenate([state_conv[0][:, 1:], xc_s[:, None, :]], axis=1)[None]
    return (y_p.reshape(batch, seq, D_MODEL), y_s.reshape(nseq, 1, D_MODEL),
            k_p, v_p, conv_p, s_p[None], k_s, v_s, conv_s, s_s[None])
```

```python
import functools
import math

import numpy as np
import jax
import jax.numpy as jnp
from jax import lax
from jax.experimental import pallas as pl
from jax.experimental.pallas import tpu as pltpu
from jax.experimental.pallas import tpu_sc as plsc

F32 = jnp.float32
BF16 = jnp.bfloat16

D_MODEL = 1024
ATT_HEADS = 8
ATT_KV_HEADS = 2
HEAD_DIM = 64
GQA = ATT_HEADS // ATT_KV_HEADS
WINDOW = 128
ATT_BLOCK = 128
N_BUCKETS = 32
DN_HEADS = 8
DN_DK = 64
DN_DV = 64
CONV_WIDTH = 4
DN_CHUNK = 64
ATT_WIDTH = ATT_HEADS * HEAD_DIM
KV_WIDTH = ATT_KV_HEADS * HEAD_DIM
DN_WIDTH = DN_HEADS * DN_DV
CONV_CH = 3 * DN_WIDTH
N_GROUPS = 4
EXPERTS_PER_GROUP = 8
N_EXPERTS = N_GROUPS * EXPERTS_PER_GROUP
D_EXPERT = 256
PLE_DIM = 256
EPS = 1e-6
NEG_INF = float("-inf")

ATT_COLS = ATT_WIDTH + 2 * KV_WIDTH
LANES = 128
IN_COLS = ATT_COLS + CONV_CH + DN_WIDTH + LANES
ROUTER_OFF = N_GROUPS
VMEM_LIMIT = 48 * 1024 * 1024
ROW_TM = 512


def _params(*sem):
    return pltpu.CompilerParams(dimension_semantics=sem, vmem_limit_bytes=VMEM_LIMIT)


def _mm(a, b):
    return jnp.dot(a.astype(BF16), b.astype(BF16), preferred_element_type=F32)


def _mm_nt(a, b):
    return lax.dot_general(a.astype(BF16), b.astype(BF16), (((1,), (1,)), ((), ())),
                           preferred_element_type=F32)


def _mm_tn(a, b):
    return lax.dot_general(a.astype(BF16), b.astype(BF16), (((0,), (0,)), ((), ())),
                           preferred_element_type=F32)


def _split3(x):
    h1 = x.astype(BF16)
    r1 = x - h1.astype(F32)
    h2 = r1.astype(BF16)
    h3 = (r1 - h2.astype(F32)).astype(BF16)
    return h1, h2, h3


def _mm_sel_rhs(x, sel):
    h1, h2, h3 = _split3(x)
    d = lambda h: jnp.dot(h, sel, preferred_element_type=F32)
    return d(h1) + d(h2) + d(h3)


def _mm_sel_lhs(sel, x):
    h1, h2, h3 = _split3(x)
    d = lambda h: jnp.dot(sel, h, preferred_element_type=F32)
    return d(h1) + d(h2) + d(h3)


def _mm3(a, b):
    ah = a.astype(BF16)
    al = (a - ah.astype(F32)).astype(BF16)
    bh = b.astype(BF16)
    bl = (b - bh.astype(F32)).astype(BF16)
    d = lambda u, v: jnp.dot(u, v, preferred_element_type=F32)
    return d(ah, bh) + d(ah, bl) + d(al, bh)


def _sigmoid(x):
    return 1.0 / (1.0 + jnp.exp(-x))


def _silu(x):
    return x * _sigmoid(x)


def _silu_tanh(x):
    return x * (0.5 * jnp.tanh(0.5 * x) + 0.5)


def _softplus(x):
    return jnp.maximum(x, 0.0) + jnp.log1p(jnp.exp(-jnp.abs(x)))


def _rmsnorm(x, g):
    return x * lax.rsqrt(jnp.mean(x * x, axis=-1, keepdims=True) + EPS) * g


def _t5_bucket_np(dist):
    max_exact = N_BUCKETS // 2
    d = np.maximum(dist, 0)
    ratio = (np.log(np.maximum(d, 1).astype(np.float32) / np.float32(max_exact))
             / np.float32(math.log(WINDOW / max_exact))).astype(np.float32)
    large = np.minimum(max_exact + (ratio * np.float32(N_BUCKETS - max_exact)).astype(np.int32),
                       N_BUCKETS - 1)
    return np.where(d < max_exact, d, large).astype(np.int32)


def _bias_lookup(bucket, rb_ref, h):
    acc = jnp.zeros(bucket.shape, F32)
    for t in range(N_BUCKETS):
        acc = jnp.where(bucket == t, rb_ref[t, h], acc)
    return acc


def _inproj_kernel(x_ref, g_ref, w_ref, att_ref, xc_ref, dz_ref, ba_ref):
    xn = _rmsnorm(x_ref[...], g_ref[...]).astype(BF16)
    o0, o1, o2 = ATT_COLS, ATT_COLS + CONV_CH, ATT_COLS + CONV_CH + DN_WIDTH
    att_ref[...] = jnp.dot(xn, w_ref[:, :o0], preferred_element_type=F32)
    xc_ref[...] = jnp.dot(xn, w_ref[:, o0:o1], preferred_element_type=F32)
    dz_ref[...] = jnp.dot(xn, w_ref[:, o1:o2], preferred_element_type=F32)
    ba_ref[...] = jnp.dot(xn, w_ref[:, o2:], preferred_element_type=F32)


def _inproj(x, g, w):
    t = x.shape[0]
    tm = min(t, ROW_TM)
    row = lambda n: pl.BlockSpec((tm, n), lambda i: (i, 0))
    full = lambda a: pl.BlockSpec(a.shape, lambda i: (0,) * a.ndim)
    return pl.pallas_call(
        _inproj_kernel,
        grid=(t // tm,),
        in_specs=[row(D_MODEL), full(g), full(w)],
        out_specs=[row(ATT_COLS), row(CONV_CH), row(DN_WIDTH), row(LANES)],
        out_shape=[jax.ShapeDtypeStruct((t, n), F32) for n in (ATT_COLS, CONV_CH, DN_WIDTH, LANES)],
        compiler_params=_params("parallel"),
        name="inproj",
    )(x, g, w)


TAIL = 8
PAIR = 2 * DN_DK
N_PAIRS = DN_WIDTH // PAIR


def _head_sums(z, pair_ones):
    hi = z.astype(BF16)
    lw = (z - hi.astype(F32)).astype(BF16)
    d = lambda a, p: jnp.dot(a[:, p * PAIR:(p + 1) * PAIR], pair_ones, preferred_element_type=F32)
    return jnp.concatenate([d(hi, p) + d(lw, p) for p in range(N_PAIRS)], axis=1)


def _inproj_conv_kernel(x_ref, g_ref, w_ref, cw_ref, ones_ref, att_ref, qkv_ref, dz_ref, ba_ref, tail_ref,
                        xp_scr, *, tiles_per_seq):
    tm = x_ref.shape[0]

    @pl.when(pl.program_id(0) % tiles_per_seq == 0)
    def _():
        xp_scr[...] = jnp.zeros((TAIL, CONV_CH), F32)

    xn = _rmsnorm(x_ref[...], g_ref[...]).astype(BF16)
    o0, o1, o2 = ATT_COLS, ATT_COLS + CONV_CH, ATT_COLS + CONV_CH + DN_WIDTH
    xc = jnp.dot(xn, w_ref[:, o0:o1], preferred_element_type=F32)
    att_ref[...] = jnp.dot(xn, w_ref[:, :o0], preferred_element_type=F32)
    dz_ref[...] = jnp.dot(xn, w_ref[:, o1:o2], preferred_element_type=F32)
    ba_ref[...] = jnp.dot(xn, w_ref[:, o2:], preferred_element_type=F32)

    head = jnp.concatenate([xp_scr[...], xc[:TAIL, :]], axis=0)

    def shifted(j):
        return jnp.concatenate([head[TAIL - j:2 * TAIL - j, :], pltpu.roll(xc, j, axis=0)[TAIL:, :]], axis=0)

    y = shifted(3) * cw_ref[0:1, :]
    y = y + shifted(2) * cw_ref[1:2, :]
    y = y + shifted(1) * cw_ref[2:3, :]
    y = y + xc * cw_ref[3:4, :]
    tail = xc[tm - TAIL:, :]
    xp_scr[...] = tail
    tail_ref[0] = tail
    y = _silu_tanh(y)
    q = y[:, :DN_WIDTH]
    k = y[:, DN_WIDTH:2 * DN_WIDTH]
    inv_norm = lax.rsqrt(_head_sums(jnp.concatenate([q * q, k * k], axis=0), ones_ref[...]) + EPS)
    qkv_ref[:, :DN_WIDTH] = q * inv_norm[:tm] * (DN_DK ** -0.5)
    qkv_ref[:, DN_WIDTH:2 * DN_WIDTH] = k * inv_norm[tm:]
    qkv_ref[:, 2 * DN_WIDTH:] = y[:, 2 * DN_WIDTH:]


def _pair_ones():
    lane = np.arange(PAIR)
    return jnp.asarray((lane[:, None] // DN_DV == lane[None, :] // DN_DV).astype(np.float32), dtype=BF16)


def _inproj_conv(x, g, w, conv_w, seq):
    t = x.shape[0]
    tm = ROW_TM
    assert seq % tm == 0
    ones = _pair_ones()
    row = lambda n: pl.BlockSpec((tm, n), lambda i: (i, 0))
    full = lambda a: pl.BlockSpec(a.shape, lambda i: (0,) * a.ndim)
    return pl.pallas_call(
        functools.partial(_inproj_conv_kernel, tiles_per_seq=seq // tm),
        grid=(t // tm,),
        in_specs=[row(D_MODEL), full(g), full(w), full(conv_w), full(ones)],
        out_specs=[row(ATT_COLS), row(CONV_CH), row(DN_WIDTH), row(LANES),
                   pl.BlockSpec((1, TAIL, CONV_CH), lambda i: (i, 0, 0))],
        out_shape=[jax.ShapeDtypeStruct((t, n), F32) for n in (ATT_COLS, CONV_CH, DN_WIDTH, LANES)]
                  + [jax.ShapeDtypeStruct((t // tm, TAIL, CONV_CH), F32)],
        scratch_shapes=[pltpu.VMEM((TAIL, CONV_CH), F32)],
        compiler_params=_params("arbitrary"),
        name="inproj_conv",
    )(x, g, w, conv_w, ones)


GROUP_ROWS = GQA * ATT_BLOCK


def _attn_prompt_kernel(cur_ref, prev_ref, bucket_ref, rb_ref, sink_ref, o_ref, bias_scr, sink_scr):
    i = pl.program_id(0)
    nseq = cur_ref.shape[0]

    @pl.when(i == 0)
    def _():
        qi = lax.broadcasted_iota(jnp.int32, (ATT_BLOCK, 2 * ATT_BLOCK), 0)
        kj = lax.broadcasted_iota(jnp.int32, (ATT_BLOCK, 2 * ATT_BLOCK), 1)
        dist = qi + ATT_BLOCK - kj
        band = jnp.logical_and(dist >= 0, dist < WINDOW)
        bucket = bucket_ref[...]
        hrow = lax.broadcasted_iota(jnp.int32, (GROUP_ROWS, 1), 0) // ATT_BLOCK
        for g in range(ATT_KV_HEADS):
            sink_col = jnp.zeros((GROUP_ROWS, 1), F32)
            for hh in range(GQA):
                h = g * GQA + hh
                bias = jnp.where(band, _bias_lookup(bucket, rb_ref, h), NEG_INF)
                bias_scr[0, g, hh * ATT_BLOCK:(hh + 1) * ATT_BLOCK, :] = bias
                bias_scr[1, g, hh * ATT_BLOCK:(hh + 1) * ATT_BLOCK, :] = jnp.where(kj >= ATT_BLOCK, bias, NEG_INF)
                sink_col = jnp.where(hrow == hh, sink_ref[h], sink_col)
            sink_scr[g] = sink_col

    first = (i == 0).astype(jnp.int32)
    probs = [(b, g) for b in range(nseq) for g in range(ATT_KV_HEADS)]
    scores = []
    for b, g in probs:
        cur = cur_ref[b]
        prev = prev_ref[b]
        q = jnp.concatenate([cur[:, (g * GQA + hh) * HEAD_DIM:(g * GQA + hh + 1) * HEAD_DIM]
                             for hh in range(GQA)], axis=0) * (HEAD_DIM ** -0.5)
        kcol = slice(ATT_WIDTH + g * HEAD_DIM, ATT_WIDTH + (g + 1) * HEAD_DIM)
        k2 = jnp.concatenate([prev[:, kcol], cur[:, kcol]], axis=0)
        scores.append(_mm_nt(q, k2) + bias_scr[first, g])
    probs_p, dens = [], []
    for (b, g), s in zip(probs, scores):
        sink = sink_scr[g]
        m = jnp.maximum(jnp.max(s, axis=-1, keepdims=True), sink)
        p = jnp.exp(s - m)
        dens.append(jnp.sum(p, axis=-1, keepdims=True) + jnp.exp(sink - m))
        probs_p.append(p.astype(BF16))
    outs = {}
    for (b, g), p, den in zip(probs, probs_p, dens):
        vcol = slice(ATT_WIDTH + KV_WIDTH + g * HEAD_DIM, ATT_WIDTH + KV_WIDTH + (g + 1) * HEAD_DIM)
        v2 = jnp.concatenate([prev_ref[b][:, vcol], cur_ref[b][:, vcol]], axis=0)
        outs[b, g] = _mm(p, v2) / den
    for b in range(nseq):
        o_ref[b] = jnp.concatenate([outs[b, g][hh * ATT_BLOCK:(hh + 1) * ATT_BLOCK, :]
                                    for g in range(ATT_KV_HEADS) for hh in range(GQA)],
                                   axis=1).astype(o_ref.dtype)


def _attn_prompt(att, bucket, rel_bias, sink, batch, seq):
    nb = seq // ATT_BLOCK
    smem = pl.BlockSpec(memory_space=pltpu.SMEM)
    att3 = att.reshape(batch, seq, ATT_COLS)
    out = pl.pallas_call(
        _attn_prompt_kernel,
        grid=(nb,),
        in_specs=[
            pl.BlockSpec((batch, ATT_BLOCK, ATT_COLS), lambda i: (0, i, 0)),
            pl.BlockSpec((batch, ATT_BLOCK, ATT_COLS), lambda i: (0, jnp.maximum(i - 1, 0), 0)),
            pl.BlockSpec(bucket.shape, lambda i: (0, 0)),
            smem, smem,
        ],
        out_specs=pl.BlockSpec((batch, ATT_BLOCK, ATT_WIDTH), lambda i: (0, i, 0)),
        out_shape=jax.ShapeDtypeStruct((batch, seq, ATT_WIDTH), BF16),
        scratch_shapes=[pltpu.VMEM((2, ATT_KV_HEADS, GROUP_ROWS, 2 * ATT_BLOCK), F32),
                        pltpu.VMEM((ATT_KV_HEADS, GROUP_ROWS, 1), F32)],
        compiler_params=_params("arbitrary"),
        name="attn_prompt",
    )(att3, att3, bucket, rel_bias, sink)
    return out.reshape(batch * seq, ATT_WIDTH)


ATT_S_BB = 8


def _attn_sample_kernel(att_ref, ck_ref, cv_ref, bucket_ref, rb_ref, sink_ref, o_ref, ks_ref, vs_ref,
                        bias_scr, col_scr):
    hrow = lax.broadcasted_iota(jnp.int32, (ATT_HEADS, LANES), 0)
    lane = lax.broadcasted_iota(jnp.int32, (ATT_HEADS, LANES), 1)

    last = (lax.broadcasted_iota(jnp.int32, (3, WINDOW), 1) == WINDOW - 1).astype(BF16)
    is_last = lax.broadcasted_iota(jnp.int32, (KV_WIDTH, WINDOW), 1) == WINDOW - 1

    def shifted(cache_t, new_row):
        pieces = jnp.concatenate([p.astype(F32) for p in _split3(new_row)], axis=0).astype(BF16)
        col = lax.dot_general(pieces, last, (((0,), (0,)), ((), ())), preferred_element_type=F32)
        out = jnp.where(is_last, col, pltpu.roll(cache_t, WINDOW - 1, axis=1))
        return out.reshape(ATT_KV_HEADS, HEAD_DIM, WINDOW)

    for b in range(ATT_S_BB):
        row = att_ref[b:b + 1, :]
        ks_ref[b] = shifted(ck_ref[b].reshape(KV_WIDTH, WINDOW), row[:, ATT_WIDTH:ATT_WIDTH + KV_WIDTH])
        vs_ref[b] = shifted(cv_ref[b].reshape(KV_WIDTH, WINDOW), row[:, ATT_WIDTH + KV_WIDTH:])

    @pl.when(pl.program_id(0) == 0)
    def _():
        bucket = jnp.broadcast_to(bucket_ref[...], (ATT_HEADS, LANES))
        bias = jnp.zeros((ATT_HEADS, LANES), F32)
        cols = jnp.zeros((ATT_HEADS, LANES), F32)
        for h in range(ATT_HEADS):
            bias = jnp.where(hrow == h, _bias_lookup(bucket, rb_ref, h), bias)
            cols = jnp.where(jnp.logical_and(hrow == h, lane == 0), sink_ref[h], cols)
            cols = jnp.where(jnp.logical_and(hrow == h, lane == 1), rb_ref[0, h], cols)
        bias_scr[...] = jnp.where(lane >= 1, bias, NEG_INF)
        col_scr[...] = cols

    bias_c = bias_scr[...]
    sink = col_scr[:, 0:1]
    bias_n = col_scr[:, 1:2]
    same_group = (hrow // GQA) == (lane // HEAD_DIM)
    low_group = lax.broadcasted_iota(jnp.int32, (ATT_HEADS, HEAD_DIM), 0) < GQA
    rnd = lambda a: a.astype(BF16).astype(F32)
    seqs = range(ATT_S_BB)
    rows = [att_ref[b:b + 1, :] for b in seqs]
    q_bds = []
    for row in rows:
        q = row[:, :ATT_WIDTH] * (HEAD_DIM ** -0.5)
        qh = jnp.concatenate([q[:, h * HEAD_DIM:(h + 1) * HEAD_DIM] for h in range(ATT_HEADS)], axis=0)
        q_bds.append(jnp.where(same_group, jnp.concatenate([qh, qh], axis=1), 0.0))
    kv_t = lambda ref, b: ref[b].reshape(KV_WIDTH, WINDOW)
    s_cs = [_mm(q_bd, kv_t(ck_ref, b)) + bias_c for b, q_bd in zip(seqs, q_bds)]
    prs, pns = [], []
    for row, q_bd, s_c in zip(rows, q_bds, s_cs):
        kn = row[:, ATT_WIDTH:ATT_WIDTH + KV_WIDTH]
        s_n = jnp.sum(rnd(q_bd) * rnd(kn), axis=-1, keepdims=True) + bias_n
        m = jnp.maximum(jnp.maximum(jnp.max(s_c, axis=-1, keepdims=True), s_n), sink)
        p_c = jnp.exp(s_c - m)
        p_n = jnp.exp(s_n - m)
        den = jnp.sum(p_c, axis=-1, keepdims=True) + p_n + jnp.exp(sink - m)
        prs.append(p_c / den)
        pns.append(p_n / den)
    pvs = [_mm_nt(pr, kv_t(cv_ref, b)) for b, pr in zip(seqs, prs)]
    for b, row, pv, pn in zip(seqs, rows, pvs, pns):
        vn = row[:, ATT_WIDTH + KV_WIDTH:]
        o_full = pv + rnd(pn) * rnd(vn)
        o_sel = jnp.where(low_group, o_full[:, :HEAD_DIM], o_full[:, HEAD_DIM:])
        o_ref[b:b + 1, :] = jnp.concatenate([o_sel[h:h + 1, :] for h in range(ATT_HEADS)], axis=1)


def _attn_sample(att, ck, cv, bucket, rel_bias, sink):
    nseq = att.shape[0]
    smem = pl.BlockSpec(memory_space=pltpu.SMEM)
    cache = pl.BlockSpec((ATT_S_BB, ATT_KV_HEADS, HEAD_DIM, WINDOW), lambda i: (i, 0, 0, 0))
    return pl.pallas_call(
        _attn_sample_kernel,
        grid=(nseq // ATT_S_BB,),
        in_specs=[pl.BlockSpec((ATT_S_BB, ATT_COLS), lambda i: (i, 0)), cache, cache,
                  pl.BlockSpec(bucket.shape, lambda i: (0, 0)), smem, smem],
        out_specs=[pl.BlockSpec((ATT_S_BB, ATT_WIDTH), lambda i: (i, 0)), cache, cache],
        out_shape=[jax.ShapeDtypeStruct((nseq, ATT_WIDTH), F32),
                   jax.ShapeDtypeStruct(ck.shape, F32), jax.ShapeDtypeStruct(cv.shape, F32)],
        scratch_shapes=[pltpu.VMEM((ATT_HEADS, LANES), F32), pltpu.VMEM((ATT_HEADS, LANES), F32)],
        compiler_params=_params("arbitrary"),
        name="attn_sample",
    )(att, ck, cv, bucket, rel_bias, sink)


GDN_TB = 128
GDN_NC = GDN_TB // DN_CHUNK


def _gdn_gates(ba, alog, dtb):
    beta = _sigmoid(ba)
    g = -jnp.exp(alog) * _softplus(ba + dtb)
    return beta, g


def _pair_diag(x, lo):
    xb = x.astype(BF16)
    zero = jnp.zeros_like(xb)
    return jnp.concatenate([jnp.where(lo, xb, zero), jnp.where(lo, zero, xb)], axis=0)


def _gdn_prompt_kernel(qkv_ref, dz_ref, ba_ref, alog_ref, dtb_ref, dnx_ref,
                       hsum_ref, expb_ref, expg_ref, ltri_ref,
                       o_ref, s_out_ref, s_scr):
    i = pl.program_id(0)
    nb = qkv_ref.shape[0]

    @pl.when(i == 0)
    def _():
        s_scr[...] = jnp.zeros(s_scr.shape, F32)

    hsum = hsum_ref[...]
    ri = lax.broadcasted_iota(jnp.int32, (DN_CHUNK, PAIR), 0)
    ci = lax.broadcasted_iota(jnp.int32, (DN_CHUNK, PAIR), 1)
    lo = ci < DN_DK
    cj = jnp.where(lo, ci, ci - DN_DK)
    causal = ri >= cj
    strict = ri > cj
    eye = (ri == cj).astype(F32)

    def sel2(x, m):
        hi = x.astype(BF16)
        lw = (x - hi.astype(F32)).astype(BF16)
        return (jnp.dot(hi, m, preferred_element_type=F32) + jnp.dot(lw, m, preferred_element_type=F32))

    pre = []
    for b in range(nb):
        q = qkv_ref[b, :, :DN_WIDTH]
        k = qkv_ref[b, :, DN_WIDTH:2 * DN_WIDTH]
        v = qkv_ref[b, :, 2 * DN_WIDTH:]
        beta_c, g_c = _gdn_gates(ba_ref[b], alog_ref[...], dtb_ref[...])
        beta = sel2(beta_c, expb_ref[...])
        gam_c = _mm_sel_lhs(ltri_ref[...], g_c)
        gam = _mm_sel_rhs(gam_c, expg_ref[...])
        gam_t = gam_c.T
        kb = k * beta
        egam = jnp.exp(gam)
        pre.append(dict(q=q, k=k, kb=kb, vb=v * beta, qg=q * egam, wr=kb * egam, gam=gam, gam_t=gam_t))

    probs = [(c, b, p) for c in range(GDN_NC) for b in range(nb) for p in range(N_PAIRS)]
    pick = lambda m: jnp.where(lo, m[:DN_DK], m[DN_DK:])
    rows_of = lambda c: slice(c * DN_CHUNK, (c + 1) * DN_CHUNK)
    sl = lambda name, c, b, p: pre[b][name][rows_of(c), p * PAIR:(p + 1) * PAIR]
    raws = []
    for c, b, p in probs:
        k_p = sl("k", c, b, p)
        k_rows = jnp.concatenate([jnp.where(lo, k_p, 0.0), jnp.where(lo, 0.0, k_p)], axis=0)
        raws.append(_mm_nt(jnp.concatenate([sl("kb", c, b, p), sl("q", c, b, p)], axis=0), k_rows))
    pws, ts, qks = [], [], []
    for (c, b, p), raw in zip(probs, raws):
        gcol = sl("gam", c, b, p)
        h0 = DN_HEADS + 2 * p
        gam_t = pre[b]["gam_t"]
        grow = jnp.concatenate([gam_t[h0:h0 + 1, rows_of(c)], gam_t[h0 + 1:h0 + 2, rows_of(c)]], axis=1)
        decay = jnp.exp(jnp.where(causal, gcol - grow, NEG_INF))
        a = jnp.where(strict, raw[:DN_CHUNK] * decay, 0.0)
        qks.append(jnp.where(causal, raw[DN_CHUNK:] * decay, 0.0))
        pws.append(-a)
        ts.append(eye - a)
    pws = [_mm(pw, _pair_diag(pw, lo)) for pw in pws]
    for _ in range(4):
        rs = [_mm(jnp.concatenate([pw, t], axis=0), _pair_diag(pw, lo)) for pw, t in zip(pws, ts)]
        pws = [r[:DN_CHUNK] for r in rs]
        ts = [t + r[DN_CHUNK:] for t, r in zip(ts, rs)]
    rs = [_mm(t, _pair_diag(pw, lo)) for pw, t in zip(pws, ts)]
    ts = [t + r for t, r in zip(ts, rs)]
    sols = [_mm(t, jnp.concatenate([_pair_diag(sl("vb", c, b, p), lo), _pair_diag(sl("wr", c, b, p), lo)],
                                   axis=1)) for (c, b, p), t in zip(probs, ts)]
    qkuws = [_mm(qk, jnp.concatenate([_pair_diag(s[:, :PAIR], lo), _pair_diag(s[:, PAIR:], lo)], axis=1))
             for qk, s in zip(qks, sols)]
    crosses, gls = [], []
    for (c, b, p), s in zip(probs, sols):
        last = (c + 1) * DN_CHUNK - 1
        gam_last = pre[b]["gam"][last:last + 1, p * PAIR:(p + 1) * PAIR]
        kd = sl("k", c, b, p) * jnp.exp(gam_last - sl("gam", c, b, p))
        crosses.append(_mm_tn(kd, s))
        gls.append(jnp.exp(gam_last))
    lhs = [jnp.concatenate([pick(cr[:, PAIR:]), sl("qg", c, b, p) - qkuw[:, PAIR:]], axis=0)
           for (c, b, p), cr, qkuw in zip(probs, crosses, qkuws)]

    o_rows = [[] for _ in range(nb)]
    per_chunk = nb * N_PAIRS
    for c in range(GDN_NC):
        sel = slice(c * per_chunk, (c + 1) * per_chunk)
        s_olds = [s_scr[b, p] for _, b, p in probs[sel]]
        rs = [_mm(l, _pair_diag(s_old, lo)) for l, s_old in zip(lhs[sel], s_olds)]
        o_pairs = [[] for _ in range(nb)]
        for (_, b, p), r, s_old, gl, cr, qkuw in zip(probs[sel], rs, s_olds, gls[sel], crosses[sel], qkuws[sel]):
            s_scr[b, p] = gl * s_old - r[:DN_DK] + pick(cr[:, :PAIR])
            o_pairs[b].append(r[DN_DK:] + qkuw[:, :PAIR])
        for b in range(nb):
            o_rows[b].append(jnp.concatenate(o_pairs[b], axis=1))

    o_all = jnp.concatenate([jnp.concatenate(rows, axis=0) for rows in o_rows], axis=0)
    inv_rms = lax.rsqrt(_head_sums(o_all * o_all, hsum) * (1.0 / DN_DV) + EPS)
    for b in range(nb):
        rows = slice(b * GDN_TB, (b + 1) * GDN_TB)
        o_ref[b] = (o_all[rows] * inv_rms[rows] * dnx_ref[...] * _silu_tanh(dz_ref[b])).astype(o_ref.dtype)

    @pl.when(i == pl.num_programs(0) - 1)
    def _():
        for b in range(nb):
            for p in range(N_PAIRS):
                s_p = s_scr[b, p]
                s_out_ref[b, 2 * p] = s_p[:, :DN_DV]
                s_out_ref[b, 2 * p + 1] = s_p[:, DN_DV:]


def _gdn_consts():
    lane = np.arange(DN_WIDTH)
    pl_lane = np.arange(PAIR)
    hsum = (pl_lane[:, None] // DN_DV == pl_lane[None, :] // DN_DV)
    src = np.arange(LANES)
    expb = (src[:, None] == lane[None, :] // DN_DV)
    expg = (src[:, None] == DN_HEADS + lane[None, :] // DN_DV)
    tok = np.arange(GDN_TB)
    ltri = np.logical_and(tok[:, None] >= tok[None, :],
                          tok[:, None] // DN_CHUNK == tok[None, :] // DN_CHUNK)
    as_bf16 = lambda m: jnp.asarray(m.astype(np.float32), dtype=BF16)
    return as_bf16(hsum), as_bf16(expb), as_bf16(expg), as_bf16(ltri)


def _gdn_prompt(xc, dz, ba, alog, dtb, dnx, batch, seq):
    nt = seq // GDN_TB
    hsum, expb, expg, ltri = _gdn_consts()
    row = lambda n: pl.BlockSpec((batch, GDN_TB, n), lambda i: (0, i, 0))
    full = lambda a: pl.BlockSpec(a.shape, lambda i: (0,) * a.ndim)
    consts = (alog, dtb, dnx, hsum, expb, expg, ltri)
    as3d = lambda a: a.reshape(batch, seq, a.shape[-1])
    o, s = pl.pallas_call(
        _gdn_prompt_kernel,
        grid=(nt,),
        in_specs=[row(CONV_CH), row(DN_WIDTH), row(LANES)] + [full(a) for a in consts],
        out_specs=[row(DN_WIDTH),
                   pl.BlockSpec((batch, DN_HEADS, DN_DK, DN_DV), lambda i: (0, 0, 0, 0))],
        out_shape=[jax.ShapeDtypeStruct((batch, seq, DN_WIDTH), BF16),
                   jax.ShapeDtypeStruct((batch, DN_HEADS, DN_DK, DN_DV), F32)],
        scratch_shapes=[pltpu.VMEM((batch, N_PAIRS, DN_DK, PAIR), F32)],
        compiler_params=_params("arbitrary"),
        name="gdn_prompt",
    )(as3d(xc), as3d(dz), as3d(ba), *consts)
    return o.reshape(batch * seq, DN_WIDTH), s


GDN_S_BB = 8


def _gdn_sample_kernel(xc_ref, dz_ref, ba_ref, sc_ref, s_ref, cw_ref, alog_ref, dtb_ref, dn_ref,
                       hsum_ref, eye_ref, hsel_ref, hrep3_ref, o_ref, s_out_ref):
    xc = xc_ref[...]
    y = sc_ref[0] * cw_ref[0:1, :]
    y = y + sc_ref[1] * cw_ref[1:2, :]
    y = y + sc_ref[2] * cw_ref[2:3, :]
    y = _silu(y + xc * cw_ref[3:4, :])
    hsum = hsum_ref[...]
    q = y[:, :DN_WIDTH]
    k = y[:, DN_WIDTH:2 * DN_WIDTH]
    v = y[:, 2 * DN_WIDTH:]
    q = q * lax.rsqrt(_mm_sel_rhs(q * q, hsum) + EPS) * (DN_DK ** -0.5)
    k = k * lax.rsqrt(_mm_sel_rhs(k * k, hsum) + EPS)
    beta_c, g_c = _gdn_gates(ba_ref[...], alog_ref[...], dtb_ref[...])
    eg_c = jnp.exp(g_c)
    eye = eye_ref[...]
    tr = lambda a: lax.dot_general(a, eye, (((0,), (0,)), ((), ())), precision=lax.Precision.HIGHEST,
                                   preferred_element_type=F32)
    gates_t = tr(jnp.concatenate([beta_c, eg_c], axis=1))
    beta_t = gates_t[:LANES]
    eg_t = gates_t[LANES:]
    dz = dz_ref[...]
    dn = dn_ref[...]
    split = lambda r: jnp.concatenate([r[:, h * DN_DV:(h + 1) * DN_DV] for h in range(DN_HEADS)], axis=0)
    own_head = hsel_ref[...].astype(F32)
    hrep3 = hrep3_ref[...]
    seqs = range(GDN_S_BB)
    dot = lambda a, b: jnp.dot(a.astype(BF16), b.astype(BF16), preferred_element_type=F32)

    def pieces(x):
        p1 = x.astype(BF16).astype(F32)
        r1 = x - p1
        p2 = r1.astype(BF16).astype(F32)
        return p1, p2, (r1 - p2).astype(BF16).astype(F32)

    heads = DN_HEADS
    k_pieces, kqs = [], []
    for b in seqs:
        kq_bd = jnp.concatenate([own_head * k[b:b + 1, :], own_head * q[b:b + 1, :]], axis=0)
        a1, a2, a3 = pieces(kq_bd)
        s1, s2, s3 = pieces(s_ref[b])
        r1 = dot(jnp.concatenate([a1, a2, a3], axis=0), s1)
        r2 = dot(jnp.concatenate([a1, a2], axis=0), s2)
        r3 = dot(a1, s3)
        n = 2 * heads
        kqs.append(((r3 + r2[n:] + r1[2 * n:]) + (r2[:n] + r1[n:2 * n])) + r1[:n])
        k_pieces.append((a1[:heads], a2[:heads], a3[:heads]))
    egs = [eg_t[DN_HEADS:2 * DN_HEADS, b:b + 1] for b in seqs]
    qks = [jnp.sum(split(q[b:b + 1, :]) * split(k[b:b + 1, :]), axis=-1, keepdims=True) for b in seqs]
    v_news = [beta_t[0:DN_HEADS, b:b + 1] * (split(v[b:b + 1, :]) - eg * kq[:heads])
              for b, eg, kq in zip(seqs, egs, kqs)]
    os_ = [eg * kq[heads:] + qk * v_new for eg, kq, qk, v_new in zip(egs, kqs, qks, v_news)]
    inv_rms = [lax.rsqrt(jnp.mean(o * o, axis=-1, keepdims=True) + EPS) for o in os_]
    for b, o, r in zip(seqs, os_, inv_rms):
        o_ref[b] = o * r * dn * _silu(split(dz[b:b + 1, :]))
    outers, egrows = [], []
    for (k1, k2, k3), v_new, eg in zip(k_pieces, v_news, egs):
        v1, v2, v3 = pieces(v_new)
        lhs = jnp.concatenate([k1, k1, k2, k1, k2, k3], axis=0).astype(BF16)
        rhs = jnp.concatenate([v1, v2, v1, v3, v2, v1], axis=0).astype(BF16)
        outers.append(lax.dot_general(lhs, rhs, (((0,), (0,)), ((), ())), preferred_element_type=F32))
        egrows.append(dot(hrep3, jnp.concatenate(pieces(jnp.broadcast_to(eg, (DN_HEADS, DN_DV))), axis=0)))
    for b, outer, egrow in zip(seqs, outers, egrows):
        s_out_ref[b] = s_ref[b] * egrow + outer


def _gdn_sample(xc, dz, ba, sconv_t, state, conv_w, alog, dtb, dn):
    nseq = xc.shape[0]
    lane = np.arange(DN_WIDTH)
    hsum = jnp.asarray((lane[:, None] // DN_DV == lane[None, :] // DN_DV).astype(np.float32), dtype=BF16)
    eye = jnp.eye(GDN_S_BB, dtype=F32)
    hsel_np = (np.arange(DN_HEADS)[:, None] == lane[None, :] // DN_DK).astype(np.float32)
    hsel = jnp.asarray(hsel_np, dtype=BF16)
    hrep3 = jnp.asarray(np.tile(hsel_np.T, (1, 3)), dtype=BF16)
    row = lambda n: pl.BlockSpec((GDN_S_BB, n), lambda i: (i, 0))
    full = lambda a: pl.BlockSpec(a.shape, lambda i: (0,) * a.ndim)
    st = pl.BlockSpec((GDN_S_BB, DN_HEADS * DN_DK, DN_DV), lambda i: (i, 0, 0))
    consts = (conv_w, alog, dtb, dn, hsum, eye, hsel, hrep3)
    return pl.pallas_call(
        _gdn_sample_kernel,
        grid=(nseq // GDN_S_BB,),
        in_specs=[row(CONV_CH), row(DN_WIDTH), row(LANES),
                  pl.BlockSpec((CONV_WIDTH - 1, GDN_S_BB, CONV_CH), lambda i: (0, i, 0)), st]
                 + [full(a) for a in consts],
        out_specs=[pl.BlockSpec((GDN_S_BB, DN_HEADS, DN_DV), lambda i: (i, 0, 0)), st],
        out_shape=[jax.ShapeDtypeStruct((nseq, DN_HEADS, DN_DV), F32),
                   jax.ShapeDtypeStruct(state.shape, F32)],
        compiler_params=_params("parallel"),
        name="gdn_sample",
    )(xc, dz, ba, sconv_t, state, *consts)


def _attn_sample_lanes_kernel(att_ref, ck_ref, cv_ref, bucket_ref, rb_ref, sink_ref, o_ref, s_scr):
    g = pl.program_id(0)
    nseq = att_ref.shape[0]
    rnd = lambda a: a.astype(BF16).astype(F32)
    att = att_ref[...]
    q_all_t = (att[:, :ATT_WIDTH] * (HEAD_DIM ** -0.5)).T
    kv_new_t = att[:, ATT_WIDTH:].T
    qsel = [jnp.where(g == 0, q_all_t[hh * HEAD_DIM:(hh + 1) * HEAD_DIM],
                      q_all_t[(GQA + hh) * HEAD_DIM:(GQA + hh + 1) * HEAD_DIM]) for hh in range(GQA)]
    qr = [rnd(q) for q in qsel]
    kn = rnd(jnp.where(g == 0, kv_new_t[0:HEAD_DIM], kv_new_t[HEAD_DIM:2 * HEAD_DIM]))
    vn = rnd(jnp.where(g == 0, kv_new_t[2 * HEAD_DIM:3 * HEAD_DIM], kv_new_t[3 * HEAD_DIM:]))

    def score_row(j, carry):
        kj = rnd(ck_ref[j, 0])
        for hh in range(GQA):
            s_scr[hh, pl.ds(j, 1), :] = jnp.sum(qr[hh] * kj, axis=0, keepdims=True)
        return carry
    lax.fori_loop(0, WINDOW, score_row, 0, unroll=2)

    bucket = bucket_ref[...]
    jrow = lax.broadcasted_iota(jnp.int32, (WINDOW, nseq), 0)
    prn = []
    for hh in range(GQA):
        h = g * GQA + hh
        bias = jnp.where(jrow >= 1, _bias_lookup(bucket, rb_ref, h), NEG_INF)
        s = s_scr[hh] + bias
        s_n = jnp.sum(qr[hh] * kn, axis=0, keepdims=True) + rb_ref[0, h]
        sink = sink_ref[h]
        m = jnp.maximum(jnp.maximum(jnp.max(s, axis=0, keepdims=True), s_n), sink)
        p = jnp.exp(s - m)
        p_n = jnp.exp(s_n - m)
        den = jnp.sum(p, axis=0, keepdims=True) + p_n + jnp.exp(sink - m)
        s_scr[hh] = rnd(p / den)
        prn.append(rnd(p_n / den))

    def value_row(j, acc):
        vj = rnd(cv_ref[j, 0])
        return tuple(acc[hh] + s_scr[hh, pl.ds(j, 1), :] * vj for hh in range(GQA))
    zero = jnp.zeros((HEAD_DIM, nseq), F32)
    acc = lax.fori_loop(0, WINDOW, value_row, (zero,) * GQA, unroll=2)
    for hh in range(GQA):
        o_ref[hh * HEAD_DIM:(hh + 1) * HEAD_DIM, :] = acc[hh] + prn[hh] * vn


def _attn_sample_lanes(att, ck_t, cv_t, rel_bias, sink):
    nseq = att.shape[0]
    assert nseq == LANES
    bucket = jnp.asarray(np.broadcast_to(_t5_bucket_np(WINDOW - np.arange(WINDOW))[:, None], (WINDOW, nseq)))
    smem = pl.BlockSpec(memory_space=pltpu.SMEM)
    cache = pl.BlockSpec((WINDOW, 1, HEAD_DIM, nseq), lambda g: (0, g, 0, 0))
    full = lambda a: pl.BlockSpec(a.shape, lambda g: (0,) * a.ndim)
    return pl.pallas_call(
        _attn_sample_lanes_kernel,
        grid=(ATT_KV_HEADS,),
        in_specs=[full(att), cache, cache, full(bucket), smem, smem],
        out_specs=pl.BlockSpec((GQA * HEAD_DIM, nseq), lambda g: (g, 0)),
        out_shape=jax.ShapeDtypeStruct((ATT_WIDTH, nseq), F32),
        scratch_shapes=[pltpu.VMEM((GQA, WINDOW, nseq), F32)],
        compiler_params=_params("arbitrary"),
        name="attn_sample_lanes",
    )(att, ck_t, cv_t, bucket, rel_bias, sink)


def _gdn_sample_front_kernel(xc_ref, dz_ref, ba_ref, sc_ref, cw_ref, alog_ref, dtb_ref, hsum_ref,
                             q_ref, k_ref, v_ref, dz_t_ref, gates_ref):
    xc = xc_ref[...]
    y = sc_ref[0] * cw_ref[0:1, :]
    y = y + sc_ref[1] * cw_ref[1:2, :]
    y = y + sc_ref[2] * cw_ref[2:3, :]
    y = _silu(y + xc * cw_ref[3:4, :])
    hsum = hsum_ref[...]
    q = y[:, :DN_WIDTH]
    k = y[:, DN_WIDTH:2 * DN_WIDTH]
    q = q * lax.rsqrt(_mm_sel_rhs(q * q, hsum) + EPS) * (DN_DK ** -0.5)
    k = k * lax.rsqrt(_mm_sel_rhs(k * k, hsum) + EPS)
    beta_c, g_c = _gdn_gates(ba_ref[...], alog_ref[...], dtb_ref[...])
    q_ref[...] = q.T
    k_ref[...] = k.T
    v_ref[...] = y[:, 2 * DN_WIDTH:].T
    dz_t_ref[...] = dz_ref[...].T
    gates_ref[0:LANES, :] = beta_c.T
    gates_ref[LANES:, :] = jnp.exp(g_c).T


def _gdn_sample_step_kernel(q_ref, k_ref, v_ref, dz_ref, gates_ref, dn_ref, s_ref, o_ref, s_out_ref):
    h = pl.program_id(0)
    beta = gates_ref[pl.ds(h, 1), :]
    eg = gates_ref[pl.ds(LANES + DN_HEADS + h, 1), :]
    q, k, v = q_ref[...], k_ref[...], v_ref[...]
    w = (k * beta) * eg
    qg = q * eg
    ws = jnp.zeros(v.shape, F32)
    qs = jnp.zeros(v.shape, F32)
    for dk in range(DN_DK):
        s_dk = s_ref[0, dk]
        ws = ws + w[dk:dk + 1, :] * s_dk
        qs = qs + qg[dk:dk + 1, :] * s_dk
    v_new = v * beta - ws
    qk = jnp.sum(q * k, axis=0, keepdims=True)
    o = qs + qk * v_new
    for dk in range(DN_DK):
        s_out_ref[0, dk] = s_ref[0, dk] * eg + k[dk:dk + 1, :] * v_new
    o = o * lax.rsqrt(jnp.mean(o * o, axis=0, keepdims=True) + EPS) * dn_ref[...]
    o_ref[...] = o * _silu(dz_ref[...])


def _gdn_sample_lanes(xc, dz, ba, sconv_t, state_t, conv_w, alog, dtb, dn):
    nseq = xc.shape[0]
    assert nseq == LANES
    lane = np.arange(DN_WIDTH)
    hsum = jnp.asarray((lane[:, None] // DN_DV == lane[None, :] // DN_DV).astype(np.float32), dtype=BF16)
    full = lambda a: pl.BlockSpec(a.shape, lambda i: (0,) * a.ndim)
    cm = jax.ShapeDtypeStruct((DN_WIDTH, nseq), F32)
    front_in = (xc, dz, ba, sconv_t, conv_w, alog, dtb, hsum)
    q_t, k_t, v_t, dz_t, gates_t = pl.pallas_call(
        _gdn_sample_front_kernel,
        grid=(1,),
        in_specs=[full(a) for a in front_in],
        out_specs=[pl.BlockSpec((DN_WIDTH, nseq), lambda i: (0, 0))] * 4
                  + [pl.BlockSpec((2 * LANES, nseq), lambda i: (0, 0))],
        out_shape=[cm, cm, cm, cm, jax.ShapeDtypeStruct((2 * LANES, nseq), F32)],
        compiler_params=_params("arbitrary"),
        name="gdn_sample_front",
    )(*front_in)
    dn_b = jnp.broadcast_to(dn.reshape(DN_DV, 1), (DN_DV, nseq))
    head = pl.BlockSpec((DN_DK, nseq), lambda h: (h, 0))
    st = pl.BlockSpec((1, DN_DK, DN_DV, nseq), lambda h: (h, 0, 0, 0))
    return pl.pallas_call(
        _gdn_sample_step_kernel,
        grid=(DN_HEADS,),
        in_specs=[head, head, head, head, full(gates_t), full(dn_b), st],
        out_specs=[head, st],
        out_shape=[cm, jax.ShapeDtypeStruct(state_t.shape, F32)],
        compiler_params=_params("parallel"),
        name="gdn_sample_step",
    )(q_t, k_t, v_t, dz_t, gates_t, dn_b, state_t)


def _route(xn, wr):
    logits = jnp.dot(xn, wr, preferred_element_type=F32)
    lane = lax.broadcasted_iota(jnp.int32, logits.shape, 1).astype(F32)
    first_at = lambda hit: jnp.min(jnp.where(hit, lane, float(LANES)), axis=-1, keepdims=True)
    glog = jnp.where(lane < N_GROUPS, logits, NEG_INF)
    gmax = jnp.max(glog, axis=-1, keepdims=True)
    gsel = first_at(glog == gmax)
    pgsel = 1.0 / jnp.sum(jnp.exp(glog - gmax), axis=-1, keepdims=True)
    lo = ROUTER_OFF + gsel * EXPERTS_PER_GROUP
    in_group = jnp.logical_and(lane >= lo, lane < lo + EXPERTS_PER_GROUP)
    elog = jnp.where(in_group, logits, NEG_INF)
    m1 = jnp.max(elog, axis=-1, keepdims=True)
    i1 = first_at(elog == m1)
    z = jnp.sum(jnp.exp(elog - m1), axis=-1, keepdims=True)
    elog2 = jnp.where(lane == i1, NEG_INF, elog)
    m2 = jnp.max(elog2, axis=-1, keepdims=True)
    i2 = first_at(elog2 == m2)
    p1 = 1.0 / z
    p2 = jnp.exp(m2 - m1) / z
    tot = p1 + p2
    return lane, i1, i2, p1 / tot * pgsel, p2 / tot * pgsel


def _outproj(x_ref, oa_ref, od_ref, wo_ref):
    return x_ref[...] + _mm(oa_ref[...], wo_ref[:ATT_WIDTH, :]) + _mm(od_ref[...], wo_ref[ATT_WIDTH:, :])


def _outproj_router_kernel(x_ref, oa_ref, od_t_ref, wo_ref, g_ref, wr_ref, h_ref, xn_ref, gate_ref):
    h = (x_ref[...] + _mm(oa_ref[...], wo_ref[:ATT_WIDTH, :])
         + _mm(od_t_ref[...].T, wo_ref[ATT_WIDTH:, :]))
    h_ref[...] = h
    xn = _rmsnorm(h, g_ref[...]).astype(BF16)
    xn_ref[...] = xn
    lane, i1, i2, g1, g2 = _route(xn, wr_ref[...])
    gate_ref[...] = jnp.where(lane == i1, g1, 0.0) + jnp.where(lane == i2, g2, 0.0)


def _outproj_router(x, oa, od_t, wo, g, wr):
    t = x.shape[0]
    tm = t
    row = lambda n: pl.BlockSpec((tm, n), lambda i: (i, 0))
    full = lambda a: pl.BlockSpec(a.shape, lambda i: (0,) * a.ndim)
    return pl.pallas_call(
        _outproj_router_kernel,
        grid=(t // tm,),
        in_specs=[row(D_MODEL), row(ATT_WIDTH), full(od_t), full(wo), full(g), full(wr)],
        out_specs=[row(D_MODEL), row(D_MODEL), row(LANES)],
        out_shape=[jax.ShapeDtypeStruct((t, D_MODEL), F32), jax.ShapeDtypeStruct((t, D_MODEL), BF16),
                   jax.ShapeDtypeStruct((t, LANES), F32)],
        compiler_params=_params("parallel"),
        name="outproj_router",
    )(x, oa, od_t, wo, g, wr)


def _moe_kernel(xn_ref, gate_ref, wg_ref, wu_ref, wd_ref, o_ref):
    e = pl.program_id(1)
    xn = xn_ref[...]
    lane = lax.broadcasted_iota(jnp.int32, gate_ref.shape, 1)
    gate = jnp.sum(jnp.where(lane == e + ROUTER_OFF, gate_ref[...], 0.0), axis=-1, keepdims=True)
    hg = jnp.dot(xn, wg_ref[...].astype(BF16), preferred_element_type=F32)
    hu = jnp.dot(xn, wu_ref[...].astype(BF16), preferred_element_type=F32)
    hm = _silu(hg) * hu * gate
    y = jnp.dot(hm.astype(BF16), wd_ref[...].astype(BF16), preferred_element_type=F32)

    @pl.when(e == 0)
    def _():
        o_ref[...] = y

    @pl.when(e > 0)
    def _():
        o_ref[...] += y


def _moe(xn, gates, wg, wu, wd):
    t = xn.shape[0]
    tm = min(t, 1024)
    return pl.pallas_call(
        _moe_kernel,
        grid=(t // tm, N_EXPERTS),
        in_specs=[pl.BlockSpec((tm, D_MODEL), lambda i, e: (i, 0)),
                  pl.BlockSpec((tm, LANES), lambda i, e: (i, 0)),
                  pl.BlockSpec((None, D_MODEL, D_EXPERT), lambda i, e: (e, 0, 0)),
                  pl.BlockSpec((None, D_MODEL, D_EXPERT), lambda i, e: (e, 0, 0)),
                  pl.BlockSpec((None, D_EXPERT, D_MODEL), lambda i, e: (e, 0, 0))],
        out_specs=pl.BlockSpec((tm, D_MODEL), lambda i, e: (i, 0)),
        out_shape=jax.ShapeDtypeStruct((t, D_MODEL), F32),
        compiler_params=_params("parallel", "arbitrary"),
        name="moe",
    )(xn, gates, wg, wu, wd)


MOE_TM = 512
POS_TM = 1024
INFO_G1, INFO_G2, INFO_E1, INFO_E2 = 0, 1, 2, 3
DMA_UNROLL = 8


def _moe_tiles(t):
    return (2 * t) // MOE_TM + N_EXPERTS


HALF = D_MODEL // 2
U32 = jnp.uint32


def _pack_rows(x):
    bits = lambda v: lax.bitcast_convert_type(v.astype(BF16).astype(F32), U32)
    return bits(x[:, HALF:]) | (bits(x[:, :HALF]) >> 16)


def _unpack_rows(w):
    lo = lax.bitcast_convert_type(w << 16, F32)
    hi = lax.bitcast_convert_type(w & jnp.uint32(0xFFFF0000), F32)
    return lo, hi


def _route_kernel(x_ref, oa_ref, od_ref, wo_ref, g_ref, wr_ref, h_ref, xn_ref, info_ref, cnt_ref, run_scr):
    h = _outproj(x_ref, oa_ref, od_ref, wo_ref)
    h_ref[...] = h
    xn = _rmsnorm(h, g_ref[...])
    xn_ref[...] = _pack_rows(xn)
    lane, i1, i2, g1, g2 = _route(xn.astype(BF16), wr_ref[...])
    info = jnp.where(lane == INFO_G1, g1, 0.0) + jnp.where(lane == INFO_G2, g2, 0.0)
    info = info + jnp.where(lane == INFO_E1, i1, 0.0) + jnp.where(lane == INFO_E2, i2, 0.0)
    info_ref[...] = info

    @pl.when(pl.program_id(0) == 0)
    def _():
        run_scr[...] = jnp.zeros(run_scr.shape, F32)
    picked = jnp.logical_or(lane == i1, lane == i2).astype(F32)
    run_scr[...] += jnp.sum(picked, axis=0, keepdims=True)
    cnt_ref[...] = run_scr[...]


def _route_sparse(x, oa, od, wo, g, wr):
    t = x.shape[0]
    tm = ROW_TM
    row = lambda n: pl.BlockSpec((tm, n), lambda i: (i, 0))
    full = lambda a: pl.BlockSpec(a.shape, lambda i: (0,) * a.ndim)
    return pl.pallas_call(
        _route_kernel,
        grid=(t // tm,),
        in_specs=[row(D_MODEL), row(ATT_WIDTH), row(DN_WIDTH), full(wo), full(g), full(wr)],
        out_specs=[row(D_MODEL), row(HALF), row(LANES), pl.BlockSpec((1, LANES), lambda i: (0, 0))],
        out_shape=[jax.ShapeDtypeStruct((t, D_MODEL), F32), jax.ShapeDtypeStruct((t, HALF), U32),
                   jax.ShapeDtypeStruct((t, LANES), F32), jax.ShapeDtypeStruct((1, LANES), F32)],
        scratch_shapes=[pltpu.VMEM((1, LANES), F32)],
        compiler_params=_params("arbitrary"),
        name="route",
    )(x, oa, od, wo, g, wr)


def _positions_kernel(info_ref, cnt_ref, ltri_ref, utri_ref, pos_ref, run_scr, off_scr):
    info = info_ref[...]
    lane = lax.broadcasted_iota(jnp.int32, info.shape, 1).astype(F32)
    hit1 = lane == info[:, INFO_E1:INFO_E1 + 1]
    hit2 = lane == info[:, INFO_E2:INFO_E2 + 1]
    onehot = jnp.logical_or(hit1, hit2).astype(F32)

    @pl.when(pl.program_id(0) == 0)
    def _():
        ln = lax.broadcasted_iota(jnp.int32, cnt_ref.shape, 1)
        is_expert = jnp.logical_and(ln >= ROUTER_OFF, ln < ROUTER_OFF + N_EXPERTS)
        tiles = jnp.where(is_expert, jnp.maximum(jnp.floor((cnt_ref[...] + (MOE_TM - 1)) * (1.0 / MOE_TM)), 1.0), 0.0)
        off_scr[...] = MOE_TM * jnp.dot(tiles.astype(BF16), utri_ref[...], preferred_element_type=F32)
        run_scr[...] = jnp.zeros(run_scr.shape, F32)

    before = (jnp.dot(ltri_ref[...], onehot.astype(BF16), preferred_element_type=F32)
              + run_scr[...] + off_scr[...])
    pos1 = jnp.sum(jnp.where(hit1, before, 0.0), axis=-1, keepdims=True)
    pos2 = jnp.sum(jnp.where(hit2, before, 0.0), axis=-1, keepdims=True)
    both = jnp.where(lane == 0, pos1, 0.0) + jnp.where(lane == 1, pos2, 0.0)
    pos_ref[...] = both.T.astype(jnp.int32)
    run_scr[...] += jnp.sum(onehot, axis=0, keepdims=True)


def _positions(info, cnt):
    t = info.shape[0]
    tm = min(t, POS_TM)
    tok = np.arange(tm)
    ltri = jnp.asarray((tok[:, None] > tok[None, :]).astype(np.float32), dtype=BF16)
    ln = np.arange(LANES)
    utri = jnp.asarray((ln[:, None] < ln[None, :]).astype(np.float32), dtype=BF16)
    full = lambda a: pl.BlockSpec(a.shape, lambda i: (0,) * a.ndim)
    return pl.pallas_call(
        _positions_kernel,
        grid=(t // tm,),
        in_specs=[pl.BlockSpec((tm, LANES), lambda i: (i, 0)), full(cnt), full(ltri), full(utri)],
        out_specs=pl.BlockSpec((LANES, tm), lambda i: (0, i)),
        out_shape=jax.ShapeDtypeStruct((LANES, t), jnp.int32),
        scratch_shapes=[pltpu.VMEM((1, LANES), F32), pltpu.VMEM((1, LANES), F32)],
        compiler_params=_params("arbitrary"),
        name="positions",
    )(info, cnt, ltri, utri)


def _row_copy(src_hbm, src_row, dst_hbm, dst_row, sem):
    return pltpu.make_async_copy(src_hbm.at[pl.ds(src_row, 1)], dst_hbm.at[pl.ds(dst_row, 1)], sem)


SCATTER_SLOTS = 3


def _scatter_kernel(pos1_ref, pos2_ref, last_ref, used_ref, nt_ref, xn_hbm, zero_hbm, xs_hbm,
                    buf, lsem, sem, zsem, *, n_tok):
    max_tiles = xs_hbm.shape[0] // MOE_TM

    def zero_tile(tile):
        return pltpu.make_async_copy(zero_hbm, xs_hbm.at[pl.ds(tile * MOE_TM, MOE_TM)], zsem)

    def for_unused(fn):
        def body(tile, carry):
            fn(tile)
            return carry
        lax.fori_loop(nt_ref[0], max_tiles, body, 0)

    for e in range(N_EXPERTS):
        @pl.when(used_ref[e] > 0)
        def _():
            zero_tile(last_ref[e]).start()
    for_unused(lambda tile: zero_tile(tile).start())
    for e in range(N_EXPERTS):
        @pl.when(used_ref[e] > 0)
        def _():
            zero_tile(last_ref[e]).wait()
    for_unused(lambda tile: zero_tile(tile).wait())

    tm = buf.shape[1]
    n = n_tok // tm

    def load(i):
        return pltpu.make_async_copy(xn_hbm.at[pl.ds(i * tm, tm)], buf.at[i % SCATTER_SLOTS],
                                     lsem.at[i % SCATTER_SLOTS])

    def wait_rows(slot):
        pltpu.make_async_copy(xs_hbm.at[pl.ds(0, 2 * tm)], xs_hbm.at[pl.ds(0, 2 * tm)], sem.at[slot]).wait()

    load(0).start()
    load(1).start()

    def step(i, carry):
        slot = i % SCATTER_SLOTS
        load(i).wait()

        def body(j, c2):
            tok = i * tm + j
            src = buf.at[slot, pl.ds(j, 1)]
            pltpu.make_async_copy(src, xs_hbm.at[pl.ds(pos1_ref[tok], 1)], sem.at[slot]).start()
            pltpu.make_async_copy(src, xs_hbm.at[pl.ds(pos2_ref[tok], 1)], sem.at[slot]).start()
            return c2
        lax.fori_loop(0, tm, body, 0, unroll=DMA_UNROLL)

        @pl.when(i >= 1)
        def _():
            wait_rows((i - 1) % SCATTER_SLOTS)

        @pl.when(i + 2 < n)
        def _():
            load(i + 2).start()
        return carry
    lax.fori_loop(0, n, step, 0)
    wait_rows((n - 1) % SCATTER_SLOTS)


def _scatter_rows(xn, pos1, pos2, last_tile, used, n_tiles, n_rows):
    t = xn.shape[0]
    zero = jnp.zeros((MOE_TM, D_MODEL), F32)
    any_spec = pl.BlockSpec(memory_space=pl.ANY)
    return pl.pallas_call(
        functools.partial(_scatter_kernel, n_tok=t),
        grid_spec=pltpu.PrefetchScalarGridSpec(
            num_scalar_prefetch=5, grid=(1,),
            in_specs=[any_spec, any_spec], out_specs=any_spec,
            scratch_shapes=[pltpu.VMEM((SCATTER_SLOTS, MOE_TM, D_MODEL), F32),
                            pltpu.SemaphoreType.DMA((SCATTER_SLOTS,)),
                            pltpu.SemaphoreType.DMA((SCATTER_SLOTS,)),
                            pltpu.SemaphoreType.DMA]),
        out_shape=jax.ShapeDtypeStruct((n_rows, D_MODEL), F32),
        compiler_params=_params("arbitrary"),
        name="scatter_rows",
    )(pos1, pos2, last_tile, used, n_tiles, xn, zero)


def _experts_kernel(te_ref, tv_ref, nt_ref, xs_ref, wg_hbm, wu_hbm, wd_hbm, xn_new_ref, gate_new_ref,
                    ys_ref, moe_new_ref, wg_s, wu_s, wd_s, wg_f, wu_f, wd_f, wsem):
    i = pl.program_id(0)
    used = i < nt_ref[0]
    expert = te_ref[i]

    def fetch(e):
        slot = e % 2
        return [pltpu.make_async_copy(src.at[e], dst.at[slot], wsem.at[slot, j])
                for j, (src, dst) in enumerate(((wg_hbm, wg_f), (wu_hbm, wu_f), (wd_hbm, wd_f)))]

    @pl.when(jnp.logical_or(i == 0, expert != te_ref[jnp.maximum(i - 1, 0)]))
    def _():
        @pl.when(i == 0)
        def _():
            for c in fetch(expert):
                c.start()
        for c in fetch(expert):
            c.wait()

        @pl.when(expert + 1 < N_EXPERTS)
        def _():
            for c in fetch(expert + 1):
                c.start()
        slot = expert % 2
        wg_s[...] = wg_f[slot].astype(BF16)
        wu_s[...] = wu_f[slot].astype(BF16)
        wd_s[...] = wd_f[slot].astype(BF16)
        xn = xn_new_ref[...]
        lane = lax.broadcasted_iota(jnp.int32, gate_new_ref.shape, 1)
        gate = jnp.sum(jnp.where(lane == expert + ROUTER_OFF, gate_new_ref[...], 0.0), axis=-1, keepdims=True)
        hg = jnp.dot(xn, wg_s[...], preferred_element_type=F32)
        hu = jnp.dot(xn, wu_s[...], preferred_element_type=F32)
        hm = _silu(hg) * hu * gate
        y = jnp.dot(hm.astype(BF16), wd_s[...], preferred_element_type=F32)

        @pl.when(i == 0)
        def _():
            moe_new_ref[...] = y

        @pl.when(i > 0)
        def _():
            moe_new_ref[...] += y

    @pl.when(used)
    def _():
        row = lax.broadcasted_iota(jnp.int32, xs_ref.shape, 0)
        x_lo, x_hi = _unpack_rows(jnp.where(row < tv_ref[i], xs_ref[...], jnp.uint32(0)))
        x_lo = x_lo.astype(BF16)
        x_hi = x_hi.astype(BF16)
        up = lambda w_s: (jnp.dot(x_lo, w_s[:HALF, :], preferred_element_type=F32)
                          + jnp.dot(x_hi, w_s[HALF:, :], preferred_element_type=F32))
        hm = (_silu_tanh(up(wg_s)) * up(wu_s)).astype(BF16)
        ys_ref[...] = _pack_rows(jnp.dot(hm, wd_s[...], preferred_element_type=F32))

    @pl.when(jnp.logical_not(used))
    def _():
        ys_ref[...] = jnp.zeros(ys_ref.shape, U32)


def _experts(xs, tile_expert, tile_valid, n_tiles, wg, wu, wd, xn_new, gate_new):
    max_tiles = xs.shape[0] // MOE_TM
    rows = pl.BlockSpec((MOE_TM, HALF), lambda i, te, tv, nt: (i, 0))
    hbm = pl.BlockSpec(memory_space=pl.ANY)
    full = lambda a: pl.BlockSpec(a.shape, lambda i, te, tv, nt: (0,) * a.ndim)
    return pl.pallas_call(
        _experts_kernel,
        grid_spec=pltpu.PrefetchScalarGridSpec(
            num_scalar_prefetch=3, grid=(max_tiles,),
            in_specs=[rows, hbm, hbm, hbm, full(xn_new), full(gate_new)],
            out_specs=[rows, pl.BlockSpec(xn_new.shape, lambda i, te, tv, nt: (0, 0))],
            scratch_shapes=[pltpu.VMEM((D_MODEL, D_EXPERT), BF16), pltpu.VMEM((D_MODEL, D_EXPERT), BF16),
                            pltpu.VMEM((D_EXPERT, D_MODEL), BF16),
                            pltpu.VMEM((2, D_MODEL, D_EXPERT), F32), pltpu.VMEM((2, D_MODEL, D_EXPERT), F32),
                            pltpu.VMEM((2, D_EXPERT, D_MODEL), F32), pltpu.SemaphoreType.DMA((2, 3))]),
        out_shape=[jax.ShapeDtypeStruct(xs.shape, U32), jax.ShapeDtypeStruct(xn_new.shape, F32)],
        compiler_params=_params("arbitrary"),
        name="experts",
    )(tile_expert, tile_valid, n_tiles, xs, wg, wu, wd, xn_new, gate_new)


def _ple_gather_kernel(pos1_ref, pos2_ref, h_ref, info_ref, p_ref, wpp_ref, wpg_ref, gp_ref, gf_ref,
                       ys_hbm, y_ref, ybuf, sem):
    i = pl.program_id(0)
    n = pl.num_programs(0)
    tm = h_ref.shape[0]

    def issue(tile, slot):
        def body(j, carry):
            tok = tile * tm + j
            pltpu.make_async_copy(ys_hbm.at[pl.ds(pos1_ref[tok], 1)], ybuf.at[slot, 0, pl.ds(j, 1)],
                                  sem.at[slot]).start()
            pltpu.make_async_copy(ys_hbm.at[pl.ds(pos2_ref[tok], 1)], ybuf.at[slot, 1, pl.ds(j, 1)],
                                  sem.at[slot]).start()
            return carry
        lax.fori_loop(0, tm, body, 0, unroll=DMA_UNROLL)

    @pl.when(i == 0)
    def _():
        issue(0, 0)

    @pl.when(i + 1 < n)
    def _():
        issue(i + 1, (i + 1) % 2)

    slot = i % 2
    pltpu.make_async_copy(ybuf.at[slot], ybuf.at[slot], sem.at[slot]).wait()
    info = info_ref[...]
    moe = info[:, INFO_G1:INFO_G1 + 1] * ybuf[slot, 0] + info[:, INFO_G2:INFO_G2 + 1] * ybuf[slot, 1]
    h = h_ref[...] + moe
    hn = _rmsnorm(h, gp_ref[...])
    h = h + _mm(p_ref[...], wpp_ref[...]) * _sigmoid(_mm(hn, wpg_ref[...]))
    y_ref[...] = _rmsnorm(h, gf_ref[...])


def _ple_gather(h, info, p, ys, pos1, pos2, wpp, wpg, gp, gf):
    t = h.shape[0]
    tm = 256
    row = lambda n: pl.BlockSpec((tm, n), lambda i, p1, p2: (i, 0))
    full = lambda a: pl.BlockSpec(a.shape, lambda i, p1, p2: (0,) * a.ndim)
    return pl.pallas_call(
        _ple_gather_kernel,
        grid_spec=pltpu.PrefetchScalarGridSpec(
            num_scalar_prefetch=2, grid=(t // tm,),
            in_specs=[row(D_MODEL), row(LANES), row(PLE_DIM), full(wpp), full(wpg), full(gp), full(gf),
                      pl.BlockSpec(memory_space=pl.ANY)],
            out_specs=row(D_MODEL),
            scratch_shapes=[pltpu.VMEM((2, 2, tm, D_MODEL), F32), pltpu.SemaphoreType.DMA((2,))]),
        out_shape=jax.ShapeDtypeStruct((t, D_MODEL), F32),
        compiler_params=_params("arbitrary"),
        name="ple_gather",
    )(pos1, pos2, h, info, p, wpp, wpg, gp, gf, ys)


SC_IDX = 128
SC_ROWS = 64
SC_WORKERS = 32


def _sc_mesh():
    return plsc.VectorSubcoreMesh(core_axis_name="c", subcore_axis_name="s")


def _sc_windows(t, fn):
    per_worker = t // SC_WORKERS
    worker = lax.axis_index(("c", "s"))

    @pl.loop(0, per_worker // SC_IDX)
    def _(w):
        fn(worker * per_worker + w * SC_IDX)


def _sc_scatter_rows(xn, pos1, pos2, n_rows):
    t, d = xn.shape
    assert t % (SC_WORKERS * SC_IDX) == 0
    idx_t = pltpu.VMEM((1, SC_IDX), jnp.int32)

    @pl.kernel(out_type=jax.ShapeDtypeStruct((n_rows, d), xn.dtype), mesh=_sc_mesh(),
               scratch_types=[idx_t, idx_t, pltpu.VMEM((SC_ROWS, d), xn.dtype)])
    def scatter(x_hbm, p1_hbm, p2_hbm, o_hbm, i1_v, i2_v, buf):
        def window(base):
            pltpu.sync_copy(p1_hbm.at[:, pl.ds(base, SC_IDX)], i1_v)
            pltpu.sync_copy(p2_hbm.at[:, pl.ds(base, SC_IDX)], i2_v)
            for k in range(SC_IDX // SC_ROWS):
                pltpu.sync_copy(x_hbm.at[pl.ds(base + k * SC_ROWS, SC_ROWS)], buf)
                pltpu.sync_copy(buf, o_hbm.at[i1_v.at[0, pl.ds(k * SC_ROWS, SC_ROWS)]])
                pltpu.sync_copy(buf, o_hbm.at[i2_v.at[0, pl.ds(k * SC_ROWS, SC_ROWS)]])
        _sc_windows(t, window)

    return scatter(xn, pos1.reshape(1, t), pos2.reshape(1, t))


def _sc_gather_rows(ys, pos1, pos2):
    t = pos1.shape[0]
    d = ys.shape[1]
    assert t % (SC_WORKERS * SC_IDX) == 0
    idx_t = pltpu.VMEM((1, SC_IDX), jnp.int32)
    out = jax.ShapeDtypeStruct((t, d), ys.dtype)

    buf_t = pltpu.VMEM((SC_ROWS, d), ys.dtype)

    @pl.kernel(out_type=(out, out), mesh=_sc_mesh(),
               scratch_types=[idx_t, idx_t, buf_t, buf_t, pltpu.SemaphoreType.DMA((2,)),
                              pltpu.SemaphoreType.DMA((2,))])
    def gather(y_hbm, p1_hbm, p2_hbm, o1_hbm, o2_hbm, i1_v, i2_v, buf_a, buf_b, gsem, wsem):
        bufs = (buf_a, buf_b)

        def window(base):
            pltpu.sync_copy(p1_hbm.at[:, pl.ds(base, SC_IDX)], i1_v)
            pltpu.sync_copy(p2_hbm.at[:, pl.ds(base, SC_IDX)], i2_v)
            items = [(idx_v, o_hbm, k) for k in range(SC_IDX // SC_ROWS)
                     for idx_v, o_hbm in ((i1_v, o1_hbm), (i2_v, o2_hbm))]

            def read(n):
                idx_v, _, k = items[n]
                return pltpu.make_async_copy(y_hbm.at[idx_v.at[0, pl.ds(k * SC_ROWS, SC_ROWS)]],
                                             bufs[n % 2], gsem.at[n % 2])

            def write(n):
                _, o_hbm, k = items[n]
                return pltpu.make_async_copy(bufs[n % 2], o_hbm.at[pl.ds(base + k * SC_ROWS, SC_ROWS)],
                                             wsem.at[n % 2])

            read(0).start()
            for n in range(len(items)):
                read(n).wait()
                if n >= 1:
                    write(n - 1).wait()
                if n + 1 < len(items):
                    read(n + 1).start()
                write(n).start()
            write(len(items) - 1).wait()
        _sc_windows(t, window)

    return gather(ys, pos1.reshape(1, t), pos2.reshape(1, t))


def _ple_sparse_kernel(h_ref, info_ref, y1_ref, y2_ref, p_ref, wpp_ref, wpg_ref, gp_ref, gf_ref, y_ref):
    info = info_ref[...]
    g1 = info[:, INFO_G1:INFO_G1 + 1]
    g2 = info[:, INFO_G2:INFO_G2 + 1]
    y1_lo, y1_hi = _unpack_rows(y1_ref[...])
    y2_lo, y2_hi = _unpack_rows(y2_ref[...])
    moe = jnp.concatenate([g1 * y1_lo + g2 * y2_lo, g1 * y1_hi + g2 * y2_hi], axis=1)
    h = h_ref[...] + moe
    hn = _rmsnorm(h, gp_ref[...])
    h = h + _mm(p_ref[...], wpp_ref[...]) * _sigmoid(_mm(hn, wpg_ref[...]))
    y_ref[...] = _rmsnorm(h, gf_ref[...])


def _ple_sparse(h, info, y1, y2, p, wpp, wpg, gp, gf):
    t = h.shape[0]
    tm = ROW_TM
    row = lambda n: pl.BlockSpec((tm, n), lambda i: (i, 0))
    full = lambda a: pl.BlockSpec(a.shape, lambda i: (0,) * a.ndim)
    return pl.pallas_call(
        _ple_sparse_kernel,
        grid=(t // tm,),
        in_specs=[row(D_MODEL), row(LANES), row(HALF), row(HALF), row(PLE_DIM),
                  full(wpp), full(wpg), full(gp), full(gf)],
        out_specs=row(D_MODEL),
        out_shape=jax.ShapeDtypeStruct((t, D_MODEL), F32),
        compiler_params=_params("parallel"),
        name="ple_sparse",
    )(h, info, y1, y2, p, wpp, wpg, gp, gf)


def _tile_tables(cnt, max_tiles):
    tiles_e = jnp.maximum((cnt + (MOE_TM - 1)) // MOE_TM, 1)
    ends = jnp.cumsum(tiles_e)
    n_tiles = ends[-1]
    tile = jnp.arange(max_tiles, dtype=jnp.int32)
    idx = jnp.minimum(tile, n_tiles - 1)
    tile_expert = jnp.sum((idx[:, None] >= ends[None, :]).astype(jnp.int32), axis=1)
    mine = tile_expert[:, None] == jnp.arange(N_EXPERTS, dtype=jnp.int32)[None, :]
    of_mine = lambda v: jnp.sum(jnp.where(mine, v[None, :], 0), axis=1)
    valid = jnp.clip(of_mine(cnt) - (idx - of_mine(ends - tiles_e)) * MOE_TM, 0, MOE_TM)
    tile_valid = jnp.where(tile < n_tiles, valid, 0).astype(jnp.int32)
    return (tile_expert, tile_valid, n_tiles.reshape(1), (ends - 1).astype(jnp.int32),
            tiles_e.astype(jnp.int32))


def _ple_final_kernel(h_ref, m_ref, p_ref, wpp_ref, wpg_ref, gp_ref, gf_ref, y_ref):
    h = h_ref[...] + m_ref[...]
    hn = _rmsnorm(h, gp_ref[...])
    h = h + _mm(p_ref[...], wpp_ref[...]) * _sigmoid(_mm(hn, wpg_ref[...]))
    y_ref[...] = _rmsnorm(h, gf_ref[...])


def _ple_final(h, m, p, wpp, wpg, gp, gf):
    t = h.shape[0]
    tm = min(t, 256)
    row = lambda n: pl.BlockSpec((tm, n), lambda i: (i, 0))
    full = lambda a: pl.BlockSpec(a.shape, lambda i: (0,) * a.ndim)
    return pl.pallas_call(
        _ple_final_kernel,
        grid=(t // tm,),
        in_specs=[row(D_MODEL), row(D_MODEL), row(PLE_DIM), full(wpp), full(wpg), full(gp), full(gf)],
        out_specs=row(D_MODEL),
        out_shape=jax.ShapeDtypeStruct((t, D_MODEL), F32),
        compiler_params=_params("parallel"),
        name="ple_final",
    )(h, m, p, wpp, wpg, gp, gf)


def kernel(x_prompt, x_sample, p_prompt, p_sample, cache_k, cache_v, state_conv, state_S, rel_bias, norm_mix, w_in, att_sink, conv_w, dn_A_log, dn_dt_bias, dn_norm, w_out, norm_ffn, w_router_group, w_router_expert, w_gate, w_up, w_down, w_ple_proj, w_ple_gate, norm_ple, norm_final):
    batch, seq, _ = x_prompt.shape
    nseq = x_sample.shape[0]
    assert x_sample.shape[1] == 1 and norm_mix.shape[0] == 1 and cache_k.shape[2] == WINDOW
    assert seq % GDN_TB == 0 and seq % ATT_BLOCK == 0

    wi = w_in[0]
    o_db = ATT_COLS + CONV_CH
    w_in_re = jnp.concatenate(
        [wi[:, :o_db], wi[:, o_db + 2 * DN_HEADS:], wi[:, o_db:o_db + 2 * DN_HEADS],
         jnp.zeros((D_MODEL, LANES - 2 * DN_HEADS), F32)], axis=1).astype(BF16)
    row = lambda a: a.reshape(1, -1).astype(F32)
    pad_lanes = lambda a, off: jnp.zeros((1, LANES), F32).at[0, off:off + a.shape[0]].set(a)
    alog = pad_lanes(dn_A_log[0], DN_HEADS)
    dtb = pad_lanes(dn_dt_bias[0], DN_HEADS)
    dnx = jnp.tile(dn_norm[0], DN_HEADS).reshape(1, DN_WIDTH)
    w_router = jnp.concatenate(
        [w_router_group[0], w_router_expert[0],
         jnp.zeros((D_MODEL, LANES - N_GROUPS - N_EXPERTS), F32)], axis=1).astype(BF16)
    wo = w_out[0].astype(BF16)
    wg, wu, wd = w_gate[0], w_up[0], w_down[0]
    wpp, wpg = w_ple_proj[0].astype(BF16), w_ple_gate[0].astype(BF16)
    sink = att_sink[0]

    qi = np.arange(ATT_BLOCK)[:, None]
    kj = np.arange(2 * ATT_BLOCK)[None, :]
    bucket_p = jnp.asarray(_t5_bucket_np(qi + ATT_BLOCK - kj))
    bucket_s = jnp.asarray(_t5_bucket_np(WINDOW - np.arange(WINDOW)[None, :]))

    xp = x_prompt.reshape(batch * seq, D_MODEL)
    att_p, qkv_p, dz_p, ba_p, xc_tails = _inproj_conv(xp, row(norm_mix[0]), w_in_re, conv_w[0], seq)
    o_att_p = _attn_prompt(att_p, bucket_p, rel_bias, sink, batch, seq)
    o_dn_p, s_p = _gdn_prompt(qkv_p, dz_p, ba_p, alog, dtb, dnx, batch, seq)
    h1, xn2, info, cnt = _route_sparse(xp, o_att_p, o_dn_p, wo, row(norm_ffn[0]), w_router)
    pos = _positions(info, cnt)
    pos1, pos2 = pos[0], pos[1]
    max_tiles = _moe_tiles(batch * seq)
    cnt_e = cnt[0, ROUTER_OFF:ROUTER_OFF + N_EXPERTS].astype(jnp.int32)
    tile_expert, tile_valid, n_tiles, last_tile, used = _tile_tables(cnt_e, max_tiles)
    xs_sorted = _sc_scatter_rows(xn2, pos1, pos2, max_tiles * MOE_TM)

    xs = x_sample.reshape(nseq, D_MODEL)
    att_s, xc_s, dz_s, ba_s = _inproj(xs, row(norm_mix[0]), w_in_re)
    ck_t = jnp.transpose(cache_k[0], (0, 2, 3, 1))
    cv_t = jnp.transpose(cache_v[0], (0, 2, 3, 1))
    o_att_s, ks_t, vs_t = _attn_sample(att_s, ck_t, cv_t, bucket_s, rel_bias, sink)
    sconv_t = jnp.swapaxes(state_conv[0], 0, 1)
    o_dn_s_t, s_s_t = _gdn_sample_lanes(xc_s, dz_s, ba_s, sconv_t, jnp.transpose(state_S[0], (1, 2, 3, 0)),
                                        conv_w[0], alog, dtb, dn_norm[0])
    s_s = jnp.transpose(s_s_t, (3, 0, 1, 2))

    h1_s, xn2_s, gates_s = _outproj_router(xs, o_att_s, o_dn_s_t, wo, row(norm_ffn[0]), w_router)

    ys, moe_s = _experts(xs_sorted, tile_expert, tile_valid, n_tiles, wg, wu, wd, xn2_s, gates_s)
    y1, y2 = _sc_gather_rows(ys, pos1, pos2)
    y_s = _ple_final(h1_s, moe_s, p_sample[0].reshape(nseq, PLE_DIM), wpp, wpg, row(norm_ple[0]),
                     row(norm_final))
    y_p = _ple_sparse(h1, info, y1, y2, p_prompt[0].reshape(batch * seq, PLE_DIM),
                      wpp, wpg, row(norm_ple[0]), row(norm_final))

    att_p3 = att_p.reshape(batch, seq, ATT_COLS)
    kv_shape = (1, batch, WINDOW, ATT_KV_HEADS, HEAD_DIM)
    k_p = att_p3[:, seq - WINDOW:, ATT_WIDTH:ATT_WIDTH + KV_WIDTH].reshape(kv_shape)
    v_p = att_p3[:, seq - WINDOW:, ATT_WIDTH + KV_WIDTH:].reshape(kv_shape)
    conv_p = xc_tails.reshape(batch, -1, TAIL, CONV_CH)[:, -1, TAIL - (CONV_WIDTH - 1):][None]
    k_s = jnp.transpose(ks_t, (0, 3, 1, 2))[None]
    v_s = jnp.transpose(vs_t, (0, 3, 1, 2))[None]
    conv_s = jnp.concatenate([state_conv[0][:, 1:], xc_s[:, None, :]], axis=1)[None]
    return (y_p.reshape(batch, seq, D_MODEL), y_s.reshape(nseq, 1, D_MODEL),
            k_p, v_p, conv_p, s_p[None], k_s, v_s, conv_s, s_s[None])
```

```python
import functools
import math

import numpy as np
import jax
import jax.numpy as jnp
from jax import lax
from jax.experimental import pallas as pl
from jax.experimental.pallas import tpu as pltpu
from jax.experimental.pallas import tpu_sc as plsc

F32 = jnp.float32
BF16 = jnp.bfloat16

D_MODEL = 1024
ATT_HEADS = 8
ATT_KV_HEADS = 2
HEAD_DIM = 64
GQA = ATT_HEADS // ATT_KV_HEADS
WINDOW = 128
ATT_BLOCK = 128
N_BUCKETS = 32
DN_HEADS = 8
DN_DK = 64
DN_DV = 64
CONV_WIDTH = 4
DN_CHUNK = 64
ATT_WIDTH = ATT_HEADS * HEAD_DIM
KV_WIDTH = ATT_KV_HEADS * HEAD_DIM
DN_WIDTH = DN_HEADS * DN_DV
CONV_CH = 3 * DN_WIDTH
N_GROUPS = 4
EXPERTS_PER_GROUP = 8
N_EXPERTS = N_GROUPS * EXPERTS_PER_GROUP
D_EXPERT = 256
PLE_DIM = 256
EPS = 1e-6
NEG_INF = float("-inf")

ATT_COLS = ATT_WIDTH + 2 * KV_WIDTH
LANES = 128
ROUTER_OFF = N_GROUPS
VMEM_LIMIT = 48 * 1024 * 1024
ROW_TM = 512


def _params(*sem):
    return pltpu.CompilerParams(dimension_semantics=sem, vmem_limit_bytes=VMEM_LIMIT)


def _mm(a, b):
    return jnp.dot(a.astype(BF16), b.astype(BF16), preferred_element_type=F32)


def _mm_nt(a, b):
    return lax.dot_general(a.astype(BF16), b.astype(BF16), (((1,), (1,)), ((), ())),
                           preferred_element_type=F32)


def _mm_tn(a, b):
    return lax.dot_general(a.astype(BF16), b.astype(BF16), (((0,), (0,)), ((), ())),
                           preferred_element_type=F32)


def _split3(x):
    h1 = x.astype(BF16)
    r1 = x - h1.astype(F32)
    h2 = r1.astype(BF16)
    h3 = (r1 - h2.astype(F32)).astype(BF16)
    return h1, h2, h3


def _mm_sel_rhs(x, sel):
    h1, h2, h3 = _split3(x)
    d = lambda h: jnp.dot(h, sel, preferred_element_type=F32)
    return d(h1) + d(h2) + d(h3)


def _mm_sel_lhs(sel, x):
    h1, h2, h3 = _split3(x)
    d = lambda h: jnp.dot(sel, h, preferred_element_type=F32)
    return d(h1) + d(h2) + d(h3)


def _sigmoid(x):
    return 1.0 / (1.0 + jnp.exp(-x))


def _silu(x):
    return x * _sigmoid(x)


def _silu_tanh(x):
    return x * (0.5 * jnp.tanh(0.5 * x) + 0.5)


def _softplus(x):
    return jnp.maximum(x, 0.0) + jnp.log1p(jnp.exp(-jnp.abs(x)))


def _rmsnorm(x, g):
    return x * lax.rsqrt(jnp.mean(x * x, axis=-1, keepdims=True) + EPS) * g


def _t5_bucket_np(dist):
    max_exact = N_BUCKETS // 2
    d = np.maximum(dist, 0)
    ratio = (np.log(np.maximum(d, 1).astype(np.float32) / np.float32(max_exact))
             / np.float32(math.log(WINDOW / max_exact))).astype(np.float32)
    large = np.minimum(max_exact + (ratio * np.float32(N_BUCKETS - max_exact)).astype(np.int32),
                       N_BUCKETS - 1)
    return np.where(d < max_exact, d, large).astype(np.int32)


def _bias_lookup(bucket, rb_ref, h):
    acc = jnp.zeros(bucket.shape, F32)
    for t in range(N_BUCKETS):
        acc = jnp.where(bucket == t, rb_ref[t, h], acc)
    return acc


def _inproj_kernel(x_ref, g_ref, w_ref, att_ref, xc_ref, dz_ref, ba_ref):
    xn = _rmsnorm(x_ref[...], g_ref[...]).astype(BF16)
    o0, o1, o2 = ATT_COLS, ATT_COLS + CONV_CH, ATT_COLS + CONV_CH + DN_WIDTH
    att_ref[...] = jnp.dot(xn, w_ref[:, :o0], preferred_element_type=F32)
    xc_ref[...] = jnp.dot(xn, w_ref[:, o0:o1], preferred_element_type=F32)
    dz_ref[...] = jnp.dot(xn, w_ref[:, o1:o2], preferred_element_type=F32)
    ba_ref[...] = jnp.dot(xn, w_ref[:, o2:], preferred_element_type=F32)


def _inproj(x, g, w):
    t = x.shape[0]
    tm = min(t, ROW_TM)
    row = lambda n: pl.BlockSpec((tm, n), lambda i: (i, 0))
    full = lambda a: pl.BlockSpec(a.shape, lambda i: (0,) * a.ndim)
    return pl.pallas_call(
        _inproj_kernel,
        grid=(t // tm,),
        in_specs=[row(D_MODEL), full(g), full(w)],
        out_specs=[row(ATT_COLS), row(CONV_CH), row(DN_WIDTH), row(LANES)],
        out_shape=[jax.ShapeDtypeStruct((t, n), F32) for n in (ATT_COLS, CONV_CH, DN_WIDTH, LANES)],
        compiler_params=_params("parallel"),
        name="inproj",
    )(x, g, w)


TAIL = 8
PAIR = 2 * DN_DK
N_PAIRS = DN_WIDTH // PAIR


def _head_sums(z, pair_ones):
    hi = z.astype(BF16)
    lw = (z - hi.astype(F32)).astype(BF16)
    d = lambda a, p: jnp.dot(a[:, p * PAIR:(p + 1) * PAIR], pair_ones, preferred_element_type=F32)
    return jnp.concatenate([d(hi, p) + d(lw, p) for p in range(N_PAIRS)], axis=1)


def _inproj_conv_kernel(x_ref, g_ref, w_ref, cw_ref, ones_ref, att_ref, qkv_ref, dz_ref, ba_ref, tail_ref,
                        xp_scr, *, tiles_per_seq):
    tm = x_ref.shape[0]

    @pl.when(pl.program_id(0) % tiles_per_seq == 0)
    def _():
        xp_scr[...] = jnp.zeros((TAIL, CONV_CH), F32)

    xn = _rmsnorm(x_ref[...], g_ref[...]).astype(BF16)
    o0, o1, o2 = ATT_COLS, ATT_COLS + CONV_CH, ATT_COLS + CONV_CH + DN_WIDTH
    xc = jnp.dot(xn, w_ref[:, o0:o1], preferred_element_type=F32)
    att_ref[...] = jnp.dot(xn, w_ref[:, :o0], preferred_element_type=F32)
    dz_ref[...] = jnp.dot(xn, w_ref[:, o1:o2], preferred_element_type=F32)
    ba_ref[...] = jnp.dot(xn, w_ref[:, o2:], preferred_element_type=F32)

    head = jnp.concatenate([xp_scr[...], xc[:TAIL, :]], axis=0)

    def shifted(j):
        return jnp.concatenate([head[TAIL - j:2 * TAIL - j, :], pltpu.roll(xc, j, axis=0)[TAIL:, :]], axis=0)

    y = shifted(3) * cw_ref[0:1, :]
    y = y + shifted(2) * cw_ref[1:2, :]
    y = y + shifted(1) * cw_ref[2:3, :]
    y = y + xc * cw_ref[3:4, :]
    tail = xc[tm - TAIL:, :]
    xp_scr[...] = tail
    tail_ref[0] = tail
    y = _silu_tanh(y)
    q = y[:, :DN_WIDTH]
    k = y[:, DN_WIDTH:2 * DN_WIDTH]
    inv_norm = lax.rsqrt(_head_sums(jnp.concatenate([q * q, k * k], axis=0), ones_ref[...]) + EPS)
    qkv_ref[:, :DN_WIDTH] = q * inv_norm[:tm] * (DN_DK ** -0.5)
    qkv_ref[:, DN_WIDTH:2 * DN_WIDTH] = k * inv_norm[tm:]
    qkv_ref[:, 2 * DN_WIDTH:] = y[:, 2 * DN_WIDTH:]


def _pair_ones():
    lane = np.arange(PAIR)
    return jnp.asarray((lane[:, None] // DN_DV == lane[None, :] // DN_DV).astype(np.float32), dtype=BF16)


def _inproj_conv(x, g, w, conv_w, seq):
    t = x.shape[0]
    tm = ROW_TM
    assert seq % tm == 0
    ones = _pair_ones()
    row = lambda n: pl.BlockSpec((tm, n), lambda i: (i, 0))
    full = lambda a: pl.BlockSpec(a.shape, lambda i: (0,) * a.ndim)
    return pl.pallas_call(
        functools.partial(_inproj_conv_kernel, tiles_per_seq=seq // tm),
        grid=(t // tm,),
        in_specs=[row(D_MODEL), full(g), full(w), full(conv_w), full(ones)],
        out_specs=[row(ATT_COLS), row(CONV_CH), row(DN_WIDTH), row(LANES),
                   pl.BlockSpec((1, TAIL, CONV_CH), lambda i: (i, 0, 0))],
        out_shape=[jax.ShapeDtypeStruct((t, n), F32) for n in (ATT_COLS, CONV_CH, DN_WIDTH, LANES)]
                  + [jax.ShapeDtypeStruct((t // tm, TAIL, CONV_CH), F32)],
        scratch_shapes=[pltpu.VMEM((TAIL, CONV_CH), F32)],
        compiler_params=_params("arbitrary"),
        name="inproj_conv",
    )(x, g, w, conv_w, ones)


GROUP_ROWS = GQA * ATT_BLOCK


def _attn_prompt_kernel(cur_ref, prev_ref, bucket_ref, rb_ref, sink_ref, o_ref, bias_scr, sink_scr):
    i = pl.program_id(0)
    nseq = cur_ref.shape[0]

    @pl.when(i == 0)
    def _():
        qi = lax.broadcasted_iota(jnp.int32, (ATT_BLOCK, 2 * ATT_BLOCK), 0)
        kj = lax.broadcasted_iota(jnp.int32, (ATT_BLOCK, 2 * ATT_BLOCK), 1)
        dist = qi + ATT_BLOCK - kj
        band = jnp.logical_and(dist >= 0, dist < WINDOW)
        bucket = bucket_ref[...]
        hrow = lax.broadcasted_iota(jnp.int32, (GROUP_ROWS, 1), 0) // ATT_BLOCK
        for g in range(ATT_KV_HEADS):
            sink_col = jnp.zeros((GROUP_ROWS, 1), F32)
            for hh in range(GQA):
                h = g * GQA + hh
                bias = jnp.where(band, _bias_lookup(bucket, rb_ref, h), NEG_INF)
                bias_scr[0, g, hh * ATT_BLOCK:(hh + 1) * ATT_BLOCK, :] = bias
                bias_scr[1, g, hh * ATT_BLOCK:(hh + 1) * ATT_BLOCK, :] = jnp.where(kj >= ATT_BLOCK, bias, NEG_INF)
                sink_col = jnp.where(hrow == hh, sink_ref[h], sink_col)
            sink_scr[g] = sink_col

    first = (i == 0).astype(jnp.int32)
    probs = [(b, g) for b in range(nseq) for g in range(ATT_KV_HEADS)]
    scores = []
    for b, g in probs:
        cur = cur_ref[b]
        prev = prev_ref[b]
        q = jnp.concatenate([cur[:, (g * GQA + hh) * HEAD_DIM:(g * GQA + hh + 1) * HEAD_DIM]
                             for hh in range(GQA)], axis=0) * (HEAD_DIM ** -0.5)
        kcol = slice(ATT_WIDTH + g * HEAD_DIM, ATT_WIDTH + (g + 1) * HEAD_DIM)
        k2 = jnp.concatenate([prev[:, kcol], cur[:, kcol]], axis=0)
        scores.append(_mm_nt(q, k2) + bias_scr[first, g])
    probs_p, dens = [], []
    for (b, g), s in zip(probs, scores):
        sink = sink_scr[g]
        m = jnp.maximum(jnp.max(s, axis=-1, keepdims=True), sink)
        p = jnp.exp(s - m)
        dens.append(jnp.sum(p, axis=-1, keepdims=True) + jnp.exp(sink - m))
        probs_p.append(p.astype(BF16))
    outs = {}
    for (b, g), p, den in zip(probs, probs_p, dens):
        vcol = slice(ATT_WIDTH + KV_WIDTH + g * HEAD_DIM, ATT_WIDTH + KV_WIDTH + (g + 1) * HEAD_DIM)
        v2 = jnp.concatenate([prev_ref[b][:, vcol], cur_ref[b][:, vcol]], axis=0)
        outs[b, g] = _mm(p, v2) / den
    for b in range(nseq):
        o_ref[b] = jnp.concatenate([outs[b, g][hh * ATT_BLOCK:(hh + 1) * ATT_BLOCK, :]
                                    for g in range(ATT_KV_HEADS) for hh in range(GQA)],
                                   axis=1).astype(o_ref.dtype)


def _attn_prompt(att, bucket, rel_bias, sink, batch, seq):
    nb = seq // ATT_BLOCK
    smem = pl.BlockSpec(memory_space=pltpu.SMEM)
    att3 = att.reshape(batch, seq, ATT_COLS)
    out = pl.pallas_call(
        _attn_prompt_kernel,
        grid=(nb,),
        in_specs=[
            pl.BlockSpec((batch, ATT_BLOCK, ATT_COLS), lambda i: (0, i, 0)),
            pl.BlockSpec((batch, ATT_BLOCK, ATT_COLS), lambda i: (0, jnp.maximum(i - 1, 0), 0)),
            pl.BlockSpec(bucket.shape, lambda i: (0, 0)),
            smem, smem,
        ],
        out_specs=pl.BlockSpec((batch, ATT_BLOCK, ATT_WIDTH), lambda i: (0, i, 0)),
        out_shape=jax.ShapeDtypeStruct((batch, seq, ATT_WIDTH), BF16),
        scratch_shapes=[pltpu.VMEM((2, ATT_KV_HEADS, GROUP_ROWS, 2 * ATT_BLOCK), F32),
                        pltpu.VMEM((ATT_KV_HEADS, GROUP_ROWS, 1), F32)],
        compiler_params=_params("arbitrary"),
        name="attn_prompt",
    )(att3, att3, bucket, rel_bias, sink)
    return out.reshape(batch * seq, ATT_WIDTH)


ATT_S_BB = 8


def _attn_sample_kernel(att_ref, ck_ref, cv_ref, bucket_ref, rb_ref, sink_ref, o_ref, ks_ref, vs_ref,
                        bias_scr, col_scr):
    hrow = lax.broadcasted_iota(jnp.int32, (ATT_HEADS, LANES), 0)
    lane = lax.broadcasted_iota(jnp.int32, (ATT_HEADS, LANES), 1)

    last = (lax.broadcasted_iota(jnp.int32, (3, WINDOW), 1) == WINDOW - 1).astype(BF16)
    is_last = lax.broadcasted_iota(jnp.int32, (KV_WIDTH, WINDOW), 1) == WINDOW - 1

    def shifted(cache_t, new_row):
        pieces = jnp.concatenate([p.astype(F32) for p in _split3(new_row)], axis=0).astype(BF16)
        col = lax.dot_general(pieces, last, (((0,), (0,)), ((), ())), preferred_element_type=F32)
        out = jnp.where(is_last, col, pltpu.roll(cache_t, WINDOW - 1, axis=1))
        return out.reshape(ATT_KV_HEADS, HEAD_DIM, WINDOW)

    for b in range(ATT_S_BB):
        row = att_ref[b:b + 1, :]
        ks_ref[b] = shifted(ck_ref[b].reshape(KV_WIDTH, WINDOW), row[:, ATT_WIDTH:ATT_WIDTH + KV_WIDTH])
        vs_ref[b] = shifted(cv_ref[b].reshape(KV_WIDTH, WINDOW), row[:, ATT_WIDTH + KV_WIDTH:])

    @pl.when(pl.program_id(0) == 0)
    def _():
        bucket = jnp.broadcast_to(bucket_ref[...], (ATT_HEADS, LANES))
        bias = jnp.zeros((ATT_HEADS, LANES), F32)
        cols = jnp.zeros((ATT_HEADS, LANES), F32)
        for h in range(ATT_HEADS):
            bias = jnp.where(hrow == h, _bias_lookup(bucket, rb_ref, h), bias)
            cols = jnp.where(jnp.logical_and(hrow == h, lane == 0), sink_ref[h], cols)
            cols = jnp.where(jnp.logical_and(hrow == h, lane == 1), rb_ref[0, h], cols)
        bias_scr[...] = jnp.where(lane >= 1, bias, NEG_INF)
        col_scr[...] = cols

    bias_c = bias_scr[...]
    sink = col_scr[:, 0:1]
    bias_n = col_scr[:, 1:2]
    same_group = (hrow // GQA) == (lane // HEAD_DIM)
    low_group = lax.broadcasted_iota(jnp.int32, (ATT_HEADS, HEAD_DIM), 0) < GQA
    rnd = lambda a: a.astype(BF16).astype(F32)
    seqs = range(ATT_S_BB)
    rows = [att_ref[b:b + 1, :] for b in seqs]
    q_bds = []
    for row in rows:
        q = row[:, :ATT_WIDTH] * (HEAD_DIM ** -0.5)
        qh = jnp.concatenate([q[:, h * HEAD_DIM:(h + 1) * HEAD_DIM] for h in range(ATT_HEADS)], axis=0)
        q_bds.append(jnp.where(same_group, jnp.concatenate([qh, qh], axis=1), 0.0))
    kv_t = lambda ref, b: ref[b].reshape(KV_WIDTH, WINDOW)
    s_cs = [_mm(q_bd, kv_t(ck_ref, b)) + bias_c for b, q_bd in zip(seqs, q_bds)]
    prs, pns = [], []
    for row, q_bd, s_c in zip(rows, q_bds, s_cs):
        kn = row[:, ATT_WIDTH:ATT_WIDTH + KV_WIDTH]
        s_n = jnp.sum(rnd(q_bd) * rnd(kn), axis=-1, keepdims=True) + bias_n
        m = jnp.maximum(jnp.maximum(jnp.max(s_c, axis=-1, keepdims=True), s_n), sink)
        p_c = jnp.exp(s_c - m)
        p_n = jnp.exp(s_n - m)
        den = jnp.sum(p_c, axis=-1, keepdims=True) + p_n + jnp.exp(sink - m)
        prs.append(p_c / den)
        pns.append(p_n / den)
    pvs = [_mm_nt(pr, kv_t(cv_ref, b)) for b, pr in zip(seqs, prs)]
    for b, row, pv, pn in zip(seqs, rows, pvs, pns):
        vn = row[:, ATT_WIDTH + KV_WIDTH:]
        o_full = pv + rnd(pn) * rnd(vn)
        o_sel = jnp.where(low_group, o_full[:, :HEAD_DIM], o_full[:, HEAD_DIM:])
        o_ref[b:b + 1, :] = jnp.concatenate([o_sel[h:h + 1, :] for h in range(ATT_HEADS)], axis=1)


def _attn_sample(att, ck, cv, bucket, rel_bias, sink):
    nseq = att.shape[0]
    smem = pl.BlockSpec(memory_space=pltpu.SMEM)
    cache = pl.BlockSpec((ATT_S_BB, ATT_KV_HEADS, HEAD_DIM, WINDOW), lambda i: (i, 0, 0, 0))
    return pl.pallas_call(
        _attn_sample_kernel,
        grid=(nseq // ATT_S_BB,),
        in_specs=[pl.BlockSpec((ATT_S_BB, ATT_COLS), lambda i: (i, 0)), cache, cache,
                  pl.BlockSpec(bucket.shape, lambda i: (0, 0)), smem, smem],
        out_specs=[pl.BlockSpec((ATT_S_BB, ATT_WIDTH), lambda i: (i, 0)), cache, cache],
        out_shape=[jax.ShapeDtypeStruct((nseq, ATT_WIDTH), F32),
                   jax.ShapeDtypeStruct(ck.shape, F32), jax.ShapeDtypeStruct(cv.shape, F32)],
        scratch_shapes=[pltpu.VMEM((ATT_HEADS, LANES), F32), pltpu.VMEM((ATT_HEADS, LANES), F32)],
        compiler_params=_params("arbitrary"),
        name="attn_sample",
    )(att, ck, cv, bucket, rel_bias, sink)


GDN_TB = 128
GDN_NC = GDN_TB // DN_CHUNK


def _gdn_gates(ba, alog, dtb):
    beta = _sigmoid(ba)
    g = -jnp.exp(alog) * _softplus(ba + dtb)
    return beta, g


def _pair_diag(x, lo):
    xb = x.astype(BF16)
    zero = jnp.zeros_like(xb)
    return jnp.concatenate([jnp.where(lo, xb, zero), jnp.where(lo, zero, xb)], axis=0)


def _gdn_prompt_kernel(qkv_ref, dz_ref, ba_ref, alog_ref, dtb_ref, dnx_ref,
                       hsum_ref, expb_ref, expg_ref, ltri_ref,
                       o_ref, s_out_ref, s_scr):
    i = pl.program_id(0)
    nb = qkv_ref.shape[0]

    @pl.when(i == 0)
    def _():
        s_scr[...] = jnp.zeros(s_scr.shape, F32)

    hsum = hsum_ref[...]
    ri = lax.broadcasted_iota(jnp.int32, (DN_CHUNK, PAIR), 0)
    ci = lax.broadcasted_iota(jnp.int32, (DN_CHUNK, PAIR), 1)
    lo = ci < DN_DK
    cj = jnp.where(lo, ci, ci - DN_DK)
    causal = ri >= cj
    strict = ri > cj
    eye = (ri == cj).astype(F32)

    def sel2(x, m):
        hi = x.astype(BF16)
        lw = (x - hi.astype(F32)).astype(BF16)
        return (jnp.dot(hi, m, preferred_element_type=F32) + jnp.dot(lw, m, preferred_element_type=F32))

    pre = []
    for b in range(nb):
        q = qkv_ref[b, :, :DN_WIDTH]
        k = qkv_ref[b, :, DN_WIDTH:2 * DN_WIDTH]
        v = qkv_ref[b, :, 2 * DN_WIDTH:]
        beta_c, g_c = _gdn_gates(ba_ref[b], alog_ref[...], dtb_ref[...])
        beta = sel2(beta_c, expb_ref[...])
        gam_c = _mm_sel_lhs(ltri_ref[...], g_c)
        gam = _mm_sel_rhs(gam_c, expg_ref[...])
        gam_t = gam_c.T
        kb = k * beta
        egam = jnp.exp(gam)
        pre.append(dict(q=q, k=k, kb=kb, vb=v * beta, qg=q * egam, wr=kb * egam, gam=gam, gam_t=gam_t))

    probs = [(c, b, p) for c in range(GDN_NC) for b in range(nb) for p in range(N_PAIRS)]
    pick = lambda m: jnp.where(lo, m[:DN_DK], m[DN_DK:])
    rows_of = lambda c: slice(c * DN_CHUNK, (c + 1) * DN_CHUNK)
    sl = lambda name, c, b, p: pre[b][name][rows_of(c), p * PAIR:(p + 1) * PAIR]
    raws = []
    for c, b, p in probs:
        k_p = sl("k", c, b, p)
        k_rows = jnp.concatenate([jnp.where(lo, k_p, 0.0), jnp.where(lo, 0.0, k_p)], axis=0)
        raws.append(_mm_nt(jnp.concatenate([sl("kb", c, b, p), sl("q", c, b, p)], axis=0), k_rows))
    pws, ts, qks = [], [], []
    for (c, b, p), raw in zip(probs, raws):
        gcol = sl("gam", c, b, p)
        h0 = DN_HEADS + 2 * p
        gam_t = pre[b]["gam_t"]
        grow = jnp.concatenate([gam_t[h0:h0 + 1, rows_of(c)], gam_t[h0 + 1:h0 + 2, rows_of(c)]], axis=1)
        decay = jnp.exp(jnp.where(causal, gcol - grow, NEG_INF))
        a = jnp.where(strict, raw[:DN_CHUNK] * decay, 0.0)
        qks.append(jnp.where(causal, raw[DN_CHUNK:] * decay, 0.0))
        pws.append(-a)
        ts.append(eye - a)
    pws = [_mm(pw, _pair_diag(pw, lo)) for pw in pws]
    for _ in range(4):
        rs = [_mm(jnp.concatenate([pw, t], axis=0), _pair_diag(pw, lo)) for pw, t in zip(pws, ts)]
        pws = [r[:DN_CHUNK] for r in rs]
        ts = [t + r[DN_CHUNK:] for t, r in zip(ts, rs)]
    rs = [_mm(t, _pair_diag(pw, lo)) for pw, t in zip(pws, ts)]
    ts = [t + r for t, r in zip(ts, rs)]
    sols = [_mm(t, jnp.concatenate([_pair_diag(sl("vb", c, b, p), lo), _pair_diag(sl("wr", c, b, p), lo)],
                                   axis=1)) for (c, b, p), t in zip(probs, ts)]
    qkuws = [_mm(qk, jnp.concatenate([_pair_diag(s[:, :PAIR], lo), _pair_diag(s[:, PAIR:], lo)], axis=1))
             for qk, s in zip(qks, sols)]
    crosses, gls = [], []
    for (c, b, p), s in zip(probs, sols):
        last = (c + 1) * DN_CHUNK - 1
        gam_last = pre[b]["gam"][last:last + 1, p * PAIR:(p + 1) * PAIR]
        kd = sl("k", c, b, p) * jnp.exp(gam_last - sl("gam", c, b, p))
        crosses.append(_mm_tn(kd, s))
        gls.append(jnp.exp(gam_last))
    lhs = [jnp.concatenate([pick(cr[:, PAIR:]), sl("qg", c, b, p) - qkuw[:, PAIR:]], axis=0)
           for (c, b, p), cr, qkuw in zip(probs, crosses, qkuws)]

    o_rows = [[] for _ in range(nb)]
    per_chunk = nb * N_PAIRS
    for c in range(GDN_NC):
        sel = slice(c * per_chunk, (c + 1) * per_chunk)
        s_olds = [s_scr[b, p] for _, b, p in probs[sel]]
        rs = [_mm(l, _pair_diag(s_old, lo)) for l, s_old in zip(lhs[sel], s_olds)]
        o_pairs = [[] for _ in range(nb)]
        for (_, b, p), r, s_old, gl, cr, qkuw in zip(probs[sel], rs, s_olds, gls[sel], crosses[sel], qkuws[sel]):
            s_scr[b, p] = gl * s_old - r[:DN_DK] + pick(cr[:, :PAIR])
            o_pairs[b].append(r[DN_DK:] + qkuw[:, :PAIR])
        for b in range(nb):
            o_rows[b].append(jnp.concatenate(o_pairs[b], axis=1))

    o_all = jnp.concatenate([jnp.concatenate(rows, axis=0) for rows in o_rows], axis=0)
    inv_rms = lax.rsqrt(_head_sums(o_all * o_all, hsum) * (1.0 / DN_DV) + EPS)
    for b in range(nb):
        rows = slice(b * GDN_TB, (b + 1) * GDN_TB)
        o_ref[b] = (o_all[rows] * inv_rms[rows] * dnx_ref[...] * _silu_tanh(dz_ref[b])).astype(o_ref.dtype)

    @pl.when(i == pl.num_programs(0) - 1)
    def _():
        for b in range(nb):
            for p in range(N_PAIRS):
                s_p = s_scr[b, p]
                s_out_ref[b, 2 * p] = s_p[:, :DN_DV]
                s_out_ref[b, 2 * p + 1] = s_p[:, DN_DV:]


def _gdn_consts():
    lane = np.arange(DN_WIDTH)
    pl_lane = np.arange(PAIR)
    hsum = (pl_lane[:, None] // DN_DV == pl_lane[None, :] // DN_DV)
    src = np.arange(LANES)
    expb = (src[:, None] == lane[None, :] // DN_DV)
    expg = (src[:, None] == DN_HEADS + lane[None, :] // DN_DV)
    tok = np.arange(GDN_TB)
    ltri = np.logical_and(tok[:, None] >= tok[None, :],
                          tok[:, None] // DN_CHUNK == tok[None, :] // DN_CHUNK)
    as_bf16 = lambda m: jnp.asarray(m.astype(np.float32), dtype=BF16)
    return as_bf16(hsum), as_bf16(expb), as_bf16(expg), as_bf16(ltri)


def _gdn_prompt(xc, dz, ba, alog, dtb, dnx, batch, seq):
    nt = seq // GDN_TB
    hsum, expb, expg, ltri = _gdn_consts()
    row = lambda n: pl.BlockSpec((batch, GDN_TB, n), lambda i: (0, i, 0))
    full = lambda a: pl.BlockSpec(a.shape, lambda i: (0,) * a.ndim)
    consts = (alog, dtb, dnx, hsum, expb, expg, ltri)
    as3d = lambda a: a.reshape(batch, seq, a.shape[-1])
    o, s = pl.pallas_call(
        _gdn_prompt_kernel,
        grid=(nt,),
        in_specs=[row(CONV_CH), row(DN_WIDTH), row(LANES)] + [full(a) for a in consts],
        out_specs=[row(DN_WIDTH),
                   pl.BlockSpec((batch, DN_HEADS, DN_DK, DN_DV), lambda i: (0, 0, 0, 0))],
        out_shape=[jax.ShapeDtypeStruct((batch, seq, DN_WIDTH), BF16),
                   jax.ShapeDtypeStruct((batch, DN_HEADS, DN_DK, DN_DV), F32)],
        scratch_shapes=[pltpu.VMEM((batch, N_PAIRS, DN_DK, PAIR), F32)],
        compiler_params=_params("arbitrary"),
        name="gdn_prompt",
    )(as3d(xc), as3d(dz), as3d(ba), *consts)
    return o.reshape(batch * seq, DN_WIDTH), s


def _gdn_sample_front_kernel(xc_ref, dz_ref, ba_ref, sc_ref, cw_ref, alog_ref, dtb_ref, hsum_ref,
                             q_ref, k_ref, v_ref, dz_t_ref, gates_ref):
    xc = xc_ref[...]
    y = sc_ref[0] * cw_ref[0:1, :]
    y = y + sc_ref[1] * cw_ref[1:2, :]
    y = y + sc_ref[2] * cw_ref[2:3, :]
    y = _silu(y + xc * cw_ref[3:4, :])
    hsum = hsum_ref[...]
    q = y[:, :DN_WIDTH]
    k = y[:, DN_WIDTH:2 * DN_WIDTH]
    q = q * lax.rsqrt(_mm_sel_rhs(q * q, hsum) + EPS) * (DN_DK ** -0.5)
    k = k * lax.rsqrt(_mm_sel_rhs(k * k, hsum) + EPS)
    beta_c, g_c = _gdn_gates(ba_ref[...], alog_ref[...], dtb_ref[...])
    q_ref[...] = q.T
    k_ref[...] = k.T
    v_ref[...] = y[:, 2 * DN_WIDTH:].T
    dz_t_ref[...] = dz_ref[...].T
    gates_ref[0:LANES, :] = beta_c.T
    gates_ref[LANES:, :] = jnp.exp(g_c).T


def _gdn_sample_step_kernel(q_ref, k_ref, v_ref, dz_ref, gates_ref, dn_ref, s_ref, o_ref, s_out_ref):
    h = pl.program_id(0)
    beta = gates_ref[pl.ds(h, 1), :]
    eg = gates_ref[pl.ds(LANES + DN_HEADS + h, 1), :]
    q, k, v = q_ref[...], k_ref[...], v_ref[...]
    w = (k * beta) * eg
    qg = q * eg
    ws = jnp.zeros(v.shape, F32)
    qs = jnp.zeros(v.shape, F32)
    for dk in range(DN_DK):
        s_dk = s_ref[0, dk]
        ws = ws + w[dk:dk + 1, :] * s_dk
        qs = qs + qg[dk:dk + 1, :] * s_dk
    v_new = v * beta - ws
    qk = jnp.sum(q * k, axis=0, keepdims=True)
    o = qs + qk * v_new
    for dk in range(DN_DK):
        s_out_ref[0, dk] = s_ref[0, dk] * eg + k[dk:dk + 1, :] * v_new
    o = o * lax.rsqrt(jnp.mean(o * o, axis=0, keepdims=True) + EPS) * dn_ref[...]
    o_ref[...] = o * _silu(dz_ref[...])


def _gdn_sample_lanes(xc, dz, ba, sconv_t, state_t, conv_w, alog, dtb, dn):
    nseq = xc.shape[0]
    assert nseq == LANES
    lane = np.arange(DN_WIDTH)
    hsum = jnp.asarray((lane[:, None] // DN_DV == lane[None, :] // DN_DV).astype(np.float32), dtype=BF16)
    full = lambda a: pl.BlockSpec(a.shape, lambda i: (0,) * a.ndim)
    cm = jax.ShapeDtypeStruct((DN_WIDTH, nseq), F32)
    front_in = (xc, dz, ba, sconv_t, conv_w, alog, dtb, hsum)
    q_t, k_t, v_t, dz_t, gates_t = pl.pallas_call(
        _gdn_sample_front_kernel,
        grid=(1,),
        in_specs=[full(a) for a in front_in],
        out_specs=[pl.BlockSpec((DN_WIDTH, nseq), lambda i: (0, 0))] * 4
                  + [pl.BlockSpec((2 * LANES, nseq), lambda i: (0, 0))],
        out_shape=[cm, cm, cm, cm, jax.ShapeDtypeStruct((2 * LANES, nseq), F32)],
        compiler_params=_params("arbitrary"),
        name="gdn_sample_front",
    )(*front_in)
    dn_b = jnp.broadcast_to(dn.reshape(DN_DV, 1), (DN_DV, nseq))
    head = pl.BlockSpec((DN_DK, nseq), lambda h: (h, 0))
    st = pl.BlockSpec((1, DN_DK, DN_DV, nseq), lambda h: (h, 0, 0, 0))
    return pl.pallas_call(
        _gdn_sample_step_kernel,
        grid=(DN_HEADS,),
        in_specs=[head, head, head, head, full(gates_t), full(dn_b), st],
        out_specs=[head, st],
        out_shape=[cm, jax.ShapeDtypeStruct(state_t.shape, F32)],
        compiler_params=_params("parallel"),
        name="gdn_sample_step",
    )(q_t, k_t, v_t, dz_t, gates_t, dn_b, state_t)


def _route(xn, wr):
    logits = jnp.dot(xn, wr, preferred_element_type=F32)
    lane = lax.broadcasted_iota(jnp.int32, logits.shape, 1).astype(F32)
    first_at = lambda hit: jnp.min(jnp.where(hit, lane, float(LANES)), axis=-1, keepdims=True)
    glog = jnp.where(lane < N_GROUPS, logits, NEG_INF)
    gmax = jnp.max(glog, axis=-1, keepdims=True)
    gsel = first_at(glog == gmax)
    pgsel = 1.0 / jnp.sum(jnp.exp(glog - gmax), axis=-1, keepdims=True)
    lo = ROUTER_OFF + gsel * EXPERTS_PER_GROUP
    in_group = jnp.logical_and(lane >= lo, lane < lo + EXPERTS_PER_GROUP)
    elog = jnp.where(in_group, logits, NEG_INF)
    m1 = jnp.max(elog, axis=-1, keepdims=True)
    i1 = first_at(elog == m1)
    z = jnp.sum(jnp.exp(elog - m1), axis=-1, keepdims=True)
    elog2 = jnp.where(lane == i1, NEG_INF, elog)
    m2 = jnp.max(elog2, axis=-1, keepdims=True)
    i2 = first_at(elog2 == m2)
    p1 = 1.0 / z
    p2 = jnp.exp(m2 - m1) / z
    tot = p1 + p2
    return lane, i1, i2, p1 / tot * pgsel, p2 / tot * pgsel


def _outproj(x_ref, oa_ref, od_ref, wo_ref):
    return x_ref[...] + _mm(oa_ref[...], wo_ref[:ATT_WIDTH, :]) + _mm(od_ref[...], wo_ref[ATT_WIDTH:, :])


def _outproj_router_kernel(x_ref, oa_ref, od_t_ref, wo_ref, g_ref, wr_ref, h_ref, xn_ref, gate_ref):
    h = (x_ref[...] + _mm(oa_ref[...], wo_ref[:ATT_WIDTH, :])
         + _mm(od_t_ref[...].T, wo_ref[ATT_WIDTH:, :]))
    h_ref[...] = h
    xn = _rmsnorm(h, g_ref[...]).astype(BF16)
    xn_ref[...] = xn
    lane, i1, i2, g1, g2 = _route(xn, wr_ref[...])
    gate_ref[...] = jnp.where(lane == i1, g1, 0.0) + jnp.where(lane == i2, g2, 0.0)


def _outproj_router(x, oa, od_t, wo, g, wr):
    t = x.shape[0]
    tm = t
    row = lambda n: pl.BlockSpec((tm, n), lambda i: (i, 0))
    full = lambda a: pl.BlockSpec(a.shape, lambda i: (0,) * a.ndim)
    return pl.pallas_call(
        _outproj_router_kernel,
        grid=(t // tm,),
        in_specs=[row(D_MODEL), row(ATT_WIDTH), full(od_t), full(wo), full(g), full(wr)],
        out_specs=[row(D_MODEL), row(D_MODEL), row(LANES)],
        out_shape=[jax.ShapeDtypeStruct((t, D_MODEL), F32), jax.ShapeDtypeStruct((t, D_MODEL), BF16),
                   jax.ShapeDtypeStruct((t, LANES), F32)],
        compiler_params=_params("parallel"),
        name="outproj_router",
    )(x, oa, od_t, wo, g, wr)


MOE_TM = 512
POS_TM = 1024
INFO_G1, INFO_G2, INFO_E1, INFO_E2 = 0, 1, 2, 3


def _moe_tiles(t):
    return (2 * t) // MOE_TM + N_EXPERTS


HALF = D_MODEL // 2
U32 = jnp.uint32


def _pack_rows(x):
    bits = lambda v: lax.bitcast_convert_type(v.astype(BF16).astype(F32), U32)
    return bits(x[:, HALF:]) | (bits(x[:, :HALF]) >> 16)


def _unpack_rows(w):
    lo = lax.bitcast_convert_type(w << 16, F32)
    hi = lax.bitcast_convert_type(w & jnp.uint32(0xFFFF0000), F32)
    return lo, hi


def _route_kernel(x_ref, oa_ref, od_ref, wo_ref, g_ref, wr_ref, h_ref, xn_ref, info_ref, cnt_ref, run_scr):
    h = _outproj(x_ref, oa_ref, od_ref, wo_ref)
    h_ref[...] = h
    xn = _rmsnorm(h, g_ref[...])
    xn_ref[...] = _pack_rows(xn)
    lane, i1, i2, g1, g2 = _route(xn.astype(BF16), wr_ref[...])
    info = jnp.where(lane == INFO_G1, g1, 0.0) + jnp.where(lane == INFO_G2, g2, 0.0)
    info = info + jnp.where(lane == INFO_E1, i1, 0.0) + jnp.where(lane == INFO_E2, i2, 0.0)
    info_ref[...] = info

    @pl.when(pl.program_id(0) == 0)
    def _():
        run_scr[...] = jnp.zeros(run_scr.shape, F32)
    picked = jnp.logical_or(lane == i1, lane == i2).astype(F32)
    run_scr[...] += jnp.sum(picked, axis=0, keepdims=True)
    cnt_ref[...] = run_scr[...]


def _route_sparse(x, oa, od, wo, g, wr):
    t = x.shape[0]
    tm = ROW_TM
    row = lambda n: pl.BlockSpec((tm, n), lambda i: (i, 0))
    full = lambda a: pl.BlockSpec(a.shape, lambda i: (0,) * a.ndim)
    return pl.pallas_call(
        _route_kernel,
        grid=(t // tm,),
        in_specs=[row(D_MODEL), row(ATT_WIDTH), row(DN_WIDTH), full(wo), full(g), full(wr)],
        out_specs=[row(D_MODEL), row(HALF), row(LANES), pl.BlockSpec((1, LANES), lambda i: (0, 0))],
        out_shape=[jax.ShapeDtypeStruct((t, D_MODEL), F32), jax.ShapeDtypeStruct((t, HALF), U32),
                   jax.ShapeDtypeStruct((t, LANES), F32), jax.ShapeDtypeStruct((1, LANES), F32)],
        scratch_shapes=[pltpu.VMEM((1, LANES), F32)],
        compiler_params=_params("arbitrary"),
        name="route",
    )(x, oa, od, wo, g, wr)


def _positions_kernel(info_ref, cnt_ref, ltri_ref, utri_ref, pos_ref, run_scr, off_scr):
    info = info_ref[...]
    lane = lax.broadcasted_iota(jnp.int32, info.shape, 1).astype(F32)
    hit1 = lane == info[:, INFO_E1:INFO_E1 + 1]
    hit2 = lane == info[:, INFO_E2:INFO_E2 + 1]
    onehot = jnp.logical_or(hit1, hit2).astype(F32)

    @pl.when(pl.program_id(0) == 0)
    def _():
        ln = lax.broadcasted_iota(jnp.int32, cnt_ref.shape, 1)
        is_expert = jnp.logical_and(ln >= ROUTER_OFF, ln < ROUTER_OFF + N_EXPERTS)
        tiles = jnp.where(is_expert, jnp.maximum(jnp.floor((cnt_ref[...] + (MOE_TM - 1)) * (1.0 / MOE_TM)), 1.0), 0.0)
        off_scr[...] = MOE_TM * jnp.dot(tiles.astype(BF16), utri_ref[...], preferred_element_type=F32)
        run_scr[...] = jnp.zeros(run_scr.shape, F32)

    before = (jnp.dot(ltri_ref[...], onehot.astype(BF16), preferred_element_type=F32)
              + run_scr[...] + off_scr[...])
    pos1 = jnp.sum(jnp.where(hit1, before, 0.0), axis=-1, keepdims=True)
    pos2 = jnp.sum(jnp.where(hit2, before, 0.0), axis=-1, keepdims=True)
    both = jnp.where(lane == 0, pos1, 0.0) + jnp.where(lane == 1, pos2, 0.0)
    pos_ref[...] = both.T.astype(jnp.int32)
    run_scr[...] += jnp.sum(onehot, axis=0, keepdims=True)


def _positions(info, cnt):
    t = info.shape[0]
    tm = min(t, POS_TM)
    tok = np.arange(tm)
    ltri = jnp.asarray((tok[:, None] > tok[None, :]).astype(np.float32), dtype=BF16)
    ln = np.arange(LANES)
    utri = jnp.asarray((ln[:, None] < ln[None, :]).astype(np.float32), dtype=BF16)
    full = lambda a: pl.BlockSpec(a.shape, lambda i: (0,) * a.ndim)
    return pl.pallas_call(
        _positions_kernel,
        grid=(t // tm,),
        in_specs=[pl.BlockSpec((tm, LANES), lambda i: (i, 0)), full(cnt), full(ltri), full(utri)],
        out_specs=pl.BlockSpec((LANES, tm), lambda i: (0, i)),
        out_shape=jax.ShapeDtypeStruct((LANES, t), jnp.int32),
        scratch_shapes=[pltpu.VMEM((1, LANES), F32), pltpu.VMEM((1, LANES), F32)],
        compiler_params=_params("arbitrary"),
        name="positions",
    )(info, cnt, ltri, utri)


def _experts_kernel(te_ref, tv_ref, nt_ref, xs_ref, wg_hbm, wu_hbm, wd_hbm, xn_new_ref, gate_new_ref,
                    ys_ref, moe_new_ref, wg_s, wu_s, wd_s, wg_f, wu_f, wd_f, wsem):
    i = pl.program_id(0)
    used = i < nt_ref[0]
    expert = te_ref[i]

    def fetch(e):
        slot = e % 2
        return [pltpu.make_async_copy(src.at[e], dst.at[slot], wsem.at[slot, j])
                for j, (src, dst) in enumerate(((wg_hbm, wg_f), (wu_hbm, wu_f), (wd_hbm, wd_f)))]

    @pl.when(jnp.logical_or(i == 0, expert != te_ref[jnp.maximum(i - 1, 0)]))
    def _():
        @pl.when(i == 0)
        def _():
            for c in fetch(expert):
                c.start()
        for c in fetch(expert):
            c.wait()

        @pl.when(expert + 1 < N_EXPERTS)
        def _():
            for c in fetch(expert + 1):
                c.start()
        slot = expert % 2
        wg_s[...] = wg_f[slot].astype(BF16)
        wu_s[...] = wu_f[slot].astype(BF16)
        wd_s[...] = wd_f[slot].astype(BF16)
        xn = xn_new_ref[...]
        lane = lax.broadcasted_iota(jnp.int32, gate_new_ref.shape, 1)
        gate = jnp.sum(jnp.where(lane == expert + ROUTER_OFF, gate_new_ref[...], 0.0), axis=-1, keepdims=True)
        hg = jnp.dot(xn, wg_s[...], preferred_element_type=F32)
        hu = jnp.dot(xn, wu_s[...], preferred_element_type=F32)
        hm = _silu(hg) * hu * gate
        y = jnp.dot(hm.astype(BF16), wd_s[...], preferred_element_type=F32)

        @pl.when(i == 0)
        def _():
            moe_new_ref[...] = y

        @pl.when(i > 0)
        def _():
            moe_new_ref[...] += y

    @pl.when(used)
    def _():
        row = lax.broadcasted_iota(jnp.int32, xs_ref.shape, 0)
        x_lo, x_hi = _unpack_rows(jnp.where(row < tv_ref[i], xs_ref[...], jnp.uint32(0)))
        x_lo = x_lo.astype(BF16)
        x_hi = x_hi.astype(BF16)
        up = lambda w_s: (jnp.dot(x_lo, w_s[:HALF, :], preferred_element_type=F32)
                          + jnp.dot(x_hi, w_s[HALF:, :], preferred_element_type=F32))
        hm = (_silu_tanh(up(wg_s)) * up(wu_s)).astype(BF16)
        ys_ref[...] = _pack_rows(jnp.dot(hm, wd_s[...], preferred_element_type=F32))

    @pl.when(jnp.logical_not(used))
    def _():
        ys_ref[...] = jnp.zeros(ys_ref.shape, U32)


def _experts(xs, tile_expert, tile_valid, n_tiles, wg, wu, wd, xn_new, gate_new):
    max_tiles = xs.shape[0] // MOE_TM
    rows = pl.BlockSpec((MOE_TM, HALF), lambda i, te, tv, nt: (i, 0))
    hbm = pl.BlockSpec(memory_space=pl.ANY)
    full = lambda a: pl.BlockSpec(a.shape, lambda i, te, tv, nt: (0,) * a.ndim)
    return pl.pallas_call(
        _experts_kernel,
        grid_spec=pltpu.PrefetchScalarGridSpec(
            num_scalar_prefetch=3, grid=(max_tiles,),
            in_specs=[rows, hbm, hbm, hbm, full(xn_new), full(gate_new)],
            out_specs=[rows, pl.BlockSpec(xn_new.shape, lambda i, te, tv, nt: (0, 0))],
            scratch_shapes=[pltpu.VMEM((D_MODEL, D_EXPERT), BF16), pltpu.VMEM((D_MODEL, D_EXPERT), BF16),
                            pltpu.VMEM((D_EXPERT, D_MODEL), BF16),
                            pltpu.VMEM((2, D_MODEL, D_EXPERT), F32), pltpu.VMEM((2, D_MODEL, D_EXPERT), F32),
                            pltpu.VMEM((2, D_EXPERT, D_MODEL), F32), pltpu.SemaphoreType.DMA((2, 3))]),
        out_shape=[jax.ShapeDtypeStruct(xs.shape, U32), jax.ShapeDtypeStruct(xn_new.shape, F32)],
        compiler_params=_params("arbitrary"),
        name="experts",
    )(tile_expert, tile_valid, n_tiles, xs, wg, wu, wd, xn_new, gate_new)


SC_IDX = 128
SC_ROWS = 64
SC_WORKERS = 32


def _sc_mesh():
    return plsc.VectorSubcoreMesh(core_axis_name="c", subcore_axis_name="s")


def _sc_windows(t, fn):
    per_worker = t // SC_WORKERS
    worker = lax.axis_index(("c", "s"))

    @pl.loop(0, per_worker // SC_IDX)
    def _(w):
        fn(worker * per_worker + w * SC_IDX)


def _sc_scatter_rows(xn, pos1, pos2, n_rows):
    t, d = xn.shape
    assert t % (SC_WORKERS * SC_IDX) == 0
    idx_t = pltpu.VMEM((1, SC_IDX), jnp.int32)

    @pl.kernel(out_type=jax.ShapeDtypeStruct((n_rows, d), xn.dtype), mesh=_sc_mesh(),
               scratch_types=[idx_t, idx_t, pltpu.VMEM((SC_ROWS, d), xn.dtype)])
    def scatter(x_hbm, p1_hbm, p2_hbm, o_hbm, i1_v, i2_v, buf):
        def window(base):
            pltpu.sync_copy(p1_hbm.at[:, pl.ds(base, SC_IDX)], i1_v)
            pltpu.sync_copy(p2_hbm.at[:, pl.ds(base, SC_IDX)], i2_v)
            for k in range(SC_IDX // SC_ROWS):
                pltpu.sync_copy(x_hbm.at[pl.ds(base + k * SC_ROWS, SC_ROWS)], buf)
                pltpu.sync_copy(buf, o_hbm.at[i1_v.at[0, pl.ds(k * SC_ROWS, SC_ROWS)]])
                pltpu.sync_copy(buf, o_hbm.at[i2_v.at[0, pl.ds(k * SC_ROWS, SC_ROWS)]])
        _sc_windows(t, window)

    return scatter(xn, pos1.reshape(1, t), pos2.reshape(1, t))


def _sc_gather_rows(ys, pos1, pos2):
    t = pos1.shape[0]
    d = ys.shape[1]
    assert t % (SC_WORKERS * SC_IDX) == 0
    idx_t = pltpu.VMEM((1, SC_IDX), jnp.int32)
    out = jax.ShapeDtypeStruct((t, d), ys.dtype)

    buf_t = pltpu.VMEM((SC_ROWS, d), ys.dtype)

    @pl.kernel(out_type=(out, out), mesh=_sc_mesh(),
               scratch_types=[idx_t, idx_t, buf_t, buf_t, pltpu.SemaphoreType.DMA((2,)),
                              pltpu.SemaphoreType.DMA((2,))])
    def gather(y_hbm, p1_hbm, p2_hbm, o1_hbm, o2_hbm, i1_v, i2_v, buf_a, buf_b, gsem, wsem):
        bufs = (buf_a, buf_b)

        def window(base):
            pltpu.sync_copy(p1_hbm.at[:, pl.ds(base, SC_IDX)], i1_v)
            pltpu.sync_copy(p2_hbm.at[:, pl.ds(base, SC_IDX)], i2_v)
            items = [(idx_v, o_hbm, k) for k in range(SC_IDX // SC_ROWS)
                     for idx_v, o_hbm in ((i1_v, o1_hbm), (i2_v, o2_hbm))]

            def read(n):
                idx_v, _, k = items[n]
                return pltpu.make_async_copy(y_hbm.at[idx_v.at[0, pl.ds(k * SC_ROWS, SC_ROWS)]],
                                             bufs[n % 2], gsem.at[n % 2])

            def write(n):
                _, o_hbm, k = items[n]
                return pltpu.make_async_copy(bufs[n % 2], o_hbm.at[pl.ds(base + k * SC_ROWS, SC_ROWS)],
                                             wsem.at[n % 2])

            read(0).start()
            for n in range(len(items)):
                read(n).wait()
                if n >= 1:
                    write(n - 1).wait()
                if n + 1 < len(items):
                    read(n + 1).start()
                write(n).start()
            write(len(items) - 1).wait()
        _sc_windows(t, window)

    return gather(ys, pos1.reshape(1, t), pos2.reshape(1, t))


def _ple_sparse_kernel(h_ref, info_ref, y1_ref, y2_ref, p_ref, wpp_ref, wpg_ref, gp_ref, gf_ref, y_ref):
    info = info_ref[...]
    g1 = info[:, INFO_G1:INFO_G1 + 1]
    g2 = info[:, INFO_G2:INFO_G2 + 1]
    y1_lo, y1_hi = _unpack_rows(y1_ref[...])
    y2_lo, y2_hi = _unpack_rows(y2_ref[...])
    moe = jnp.concatenate([g1 * y1_lo + g2 * y2_lo, g1 * y1_hi + g2 * y2_hi], axis=1)
    h = h_ref[...] + moe
    hn = _rmsnorm(h, gp_ref[...])
    h = h + _mm(p_ref[...], wpp_ref[...]) * _sigmoid(_mm(hn, wpg_ref[...]))
    y_ref[...] = _rmsnorm(h, gf_ref[...])


def _ple_sparse(h, info, y1, y2, p, wpp, wpg, gp, gf):
    t = h.shape[0]
    tm = ROW_TM
    row = lambda n: pl.BlockSpec((tm, n), lambda i: (i, 0))
    full = lambda a: pl.BlockSpec(a.shape, lambda i: (0,) * a.ndim)
    return pl.pallas_call(
        _ple_sparse_kernel,
        grid=(t // tm,),
        in_specs=[row(D_MODEL), row(LANES), row(HALF), row(HALF), row(PLE_DIM),
                  full(wpp), full(wpg), full(gp), full(gf)],
        out_specs=row(D_MODEL),
        out_shape=jax.ShapeDtypeStruct((t, D_MODEL), F32),
        compiler_params=_params("parallel"),
        name="ple_sparse",
    )(h, info, y1, y2, p, wpp, wpg, gp, gf)


def _tile_tables(cnt, max_tiles):
    tiles_e = jnp.maximum((cnt + (MOE_TM - 1)) // MOE_TM, 1)
    ends = jnp.cumsum(tiles_e)
    n_tiles = ends[-1]
    tile = jnp.arange(max_tiles, dtype=jnp.int32)
    idx = jnp.minimum(tile, n_tiles - 1)
    tile_expert = jnp.sum((idx[:, None] >= ends[None, :]).astype(jnp.int32), axis=1)
    mine = tile_expert[:, None] == jnp.arange(N_EXPERTS, dtype=jnp.int32)[None, :]
    of_mine = lambda v: jnp.sum(jnp.where(mine, v[None, :], 0), axis=1)
    valid = jnp.clip(of_mine(cnt) - (idx - of_mine(ends - tiles_e)) * MOE_TM, 0, MOE_TM)
    tile_valid = jnp.where(tile < n_tiles, valid, 0).astype(jnp.int32)
    return tile_expert, tile_valid, n_tiles.reshape(1)


def _ple_final_kernel(h_ref, m_ref, p_ref, wpp_ref, wpg_ref, gp_ref, gf_ref, y_ref):
    h = h_ref[...] + m_ref[...]
    hn = _rmsnorm(h, gp_ref[...])
    h = h + _mm(p_ref[...], wpp_ref[...]) * _sigmoid(_mm(hn, wpg_ref[...]))
    y_ref[...] = _rmsnorm(h, gf_ref[...])


def _ple_final(h, m, p, wpp, wpg, gp, gf):
    t = h.shape[0]
    tm = min(t, 256)
    row = lambda n: pl.BlockSpec((tm, n), lambda i: (i, 0))
    full = lambda a: pl.BlockSpec(a.shape, lambda i: (0,) * a.ndim)
    return pl.pallas_call(
        _ple_final_kernel,
        grid=(t // tm,),
        in_specs=[row(D_MODEL), row(D_MODEL), row(PLE_DIM), full(wpp), full(wpg), full(gp), full(gf)],
        out_specs=row(D_MODEL),
        out_shape=jax.ShapeDtypeStruct((t, D_MODEL), F32),
        compiler_params=_params("parallel"),
        name="ple_final",
    )(h, m, p, wpp, wpg, gp, gf)


def kernel(x_prompt, x_sample, p_prompt, p_sample, cache_k, cache_v, state_conv, state_S, rel_bias, norm_mix, w_in, att_sink, conv_w, dn_A_log, dn_dt_bias, dn_norm, w_out, norm_ffn, w_router_group, w_router_expert, w_gate, w_up, w_down, w_ple_proj, w_ple_gate, norm_ple, norm_final):
    batch, seq, _ = x_prompt.shape
    nseq = x_sample.shape[0]
    assert x_sample.shape[1] == 1 and norm_mix.shape[0] == 1 and cache_k.shape[2] == WINDOW
    assert seq % GDN_TB == 0 and seq % ATT_BLOCK == 0

    wi = w_in[0]
    o_db = ATT_COLS + CONV_CH
    w_in_re = jnp.concatenate(
        [wi[:, :o_db], wi[:, o_db + 2 * DN_HEADS:], wi[:, o_db:o_db + 2 * DN_HEADS],
         jnp.zeros((D_MODEL, LANES - 2 * DN_HEADS), F32)], axis=1).astype(BF16)
    row = lambda a: a.reshape(1, -1).astype(F32)
    pad_lanes = lambda a, off: jnp.zeros((1, LANES), F32).at[0, off:off + a.shape[0]].set(a)
    alog = pad_lanes(dn_A_log[0], DN_HEADS)
    dtb = pad_lanes(dn_dt_bias[0], DN_HEADS)
    dnx = jnp.tile(dn_norm[0], DN_HEADS).reshape(1, DN_WIDTH)
    w_router = jnp.concatenate(
        [w_router_group[0], w_router_expert[0],
         jnp.zeros((D_MODEL, LANES - N_GROUPS - N_EXPERTS), F32)], axis=1).astype(BF16)
    wo = w_out[0].astype(BF16)
    wg, wu, wd = w_gate[0], w_up[0], w_down[0]
    wpp, wpg = w_ple_proj[0].astype(BF16), w_ple_gate[0].astype(BF16)
    sink = att_sink[0]

    qi = np.arange(ATT_BLOCK)[:, None]
    kj = np.arange(2 * ATT_BLOCK)[None, :]
    bucket_p = jnp.asarray(_t5_bucket_np(qi + ATT_BLOCK - kj))
    bucket_s = jnp.asarray(_t5_bucket_np(WINDOW - np.arange(WINDOW)[None, :]))

    xp = x_prompt.reshape(batch * seq, D_MODEL)
    att_p, qkv_p, dz_p, ba_p, xc_tails = _inproj_conv(xp, row(norm_mix[0]), w_in_re, conv_w[0], seq)
    o_att_p = _attn_prompt(att_p, bucket_p, rel_bias, sink, batch, seq)
    o_dn_p, s_p = _gdn_prompt(qkv_p, dz_p, ba_p, alog, dtb, dnx, batch, seq)
    h1, xn2, info, cnt = _route_sparse(xp, o_att_p, o_dn_p, wo, row(norm_ffn[0]), w_router)
    pos = _positions(info, cnt)
    pos1, pos2 = pos[0], pos[1]
    max_tiles = _moe_tiles(batch * seq)
    cnt_e = cnt[0, ROUTER_OFF:ROUTER_OFF + N_EXPERTS].astype(jnp.int32)
    tile_expert, tile_valid, n_tiles = _tile_tables(cnt_e, max_tiles)
    xs_sorted = _sc_scatter_rows(xn2, pos1, pos2, max_tiles * MOE_TM)

    xs = x_sample.reshape(nseq, D_MODEL)
    att_s, xc_s, dz_s, ba_s = _inproj(xs, row(norm_mix[0]), w_in_re)
    ck_t = jnp.transpose(cache_k[0], (0, 2, 3, 1))
    cv_t = jnp.transpose(cache_v[0], (0, 2, 3, 1))
    o_att_s, ks_t, vs_t = _attn_sample(att_s, ck_t, cv_t, bucket_s, rel_bias, sink)
    sconv_t = jnp.swapaxes(state_conv[0], 0, 1)
    o_dn_s_t, s_s_t = _gdn_sample_lanes(xc_s, dz_s, ba_s, sconv_t, jnp.transpose(state_S[0], (1, 2, 3, 0)),
                                        conv_w[0], alog, dtb, dn_norm[0])
    s_s = jnp.transpose(s_s_t, (3, 0, 1, 2))

    h1_s, xn2_s, gates_s = _outproj_router(xs, o_att_s, o_dn_s_t, wo, row(norm_ffn[0]), w_router)

    ys, moe_s = _experts(xs_sorted, tile_expert, tile_valid, n_tiles, wg, wu, wd, xn2_s, gates_s)
    y1, y2 = _sc_gather_rows(ys, pos1, pos2)
    y_s = _ple_final(h1_s, moe_s, p_sample[0].reshape(nseq, PLE_DIM), wpp, wpg, row(norm_ple[0]),
                     row(norm_final))
    y_p = _ple_sparse(h1, info, y1, y2, p_prompt[0].reshape(batch * seq, PLE_DIM),
                      wpp, wpg, row(norm_ple[0]), row(norm_final))

    att_p3 = att_p.reshape(batch, seq, ATT_COLS)
    kv_shape = (1, batch, WINDOW, ATT_KV_HEADS, HEAD_DIM)
    k_p = att_p3[:, seq - WINDOW:, ATT_WIDTH:ATT_WIDTH + KV_WIDTH].reshape(kv_shape)
    v_p = att_p3[:, seq - WINDOW:, ATT_WIDTH + KV_WIDTH:].reshape(kv_shape)
    conv_p = xc_tails.reshape(batch, -1, TAIL, CONV_CH)[:, -1, TAIL - (CONV_WIDTH - 1):][None]
    k_s = jnp.transpose(ks_t, (0, 3, 1, 2))[None]
    v_s = jnp.transpose(vs_t, (0, 3, 1, 2))[None]
    conv_s = jnp.concatenate([state_conv[0][:, 1:], xc_s[:, None, :]], axis=1)[None]
    return (y_p.reshape(batch, seq, D_MODEL), y_s.reshape(nseq, 1, D_MODEL),
            k_p, v_p, conv_p, s_p[None], k_s, v_s, conv_s, s_s[None])
```

```python
import functools
import math

import numpy as np
import jax
import jax.numpy as jnp
from jax import lax
from jax.experimental import pallas as pl
from jax.experimental.pallas import tpu as pltpu
from jax.experimental.pallas import tpu_sc as plsc

F32 = jnp.float32
BF16 = jnp.bfloat16

D_MODEL = 1024
ATT_HEADS = 8
ATT_KV_HEADS = 2
HEAD_DIM = 64
GQA = ATT_HEADS // ATT_KV_HEADS
WINDOW = 128
ATT_BLOCK = 128
N_BUCKETS = 32
DN_HEADS = 8
DN_DK = 64
DN_DV = 64
CONV_WIDTH = 4
DN_CHUNK = 64
ATT_WIDTH = ATT_HEADS * HEAD_DIM
KV_WIDTH = ATT_KV_HEADS * HEAD_DIM
DN_WIDTH = DN_HEADS * DN_DV
CONV_CH = 3 * DN_WIDTH
N_GROUPS = 4
EXPERTS_PER_GROUP = 8
N_EXPERTS = N_GROUPS * EXPERTS_PER_GROUP
D_EXPERT = 256
PLE_DIM = 256
EPS = 1e-6
NEG_INF = float("-inf")

ATT_COLS = ATT_WIDTH + 2 * KV_WIDTH
LANES = 128
ROUTER_OFF = N_GROUPS
VMEM_LIMIT = 48 * 1024 * 1024
ROW_TM = 512
WIDE_TM = 1024


def _params(*sem):
    return pltpu.CompilerParams(dimension_semantics=sem, vmem_limit_bytes=VMEM_LIMIT)


def _mm(a, b):
    return jnp.dot(a.astype(BF16), b.astype(BF16), preferred_element_type=F32)


def _mm_nt(a, b):
    return lax.dot_general(a.astype(BF16), b.astype(BF16), (((1,), (1,)), ((), ())),
                           preferred_element_type=F32)


def _mm_tn(a, b):
    return lax.dot_general(a.astype(BF16), b.astype(BF16), (((0,), (0,)), ((), ())),
                           preferred_element_type=F32)


def _split3(x):
    h1 = x.astype(BF16)
    r1 = x - h1.astype(F32)
    h2 = r1.astype(BF16)
    h3 = (r1 - h2.astype(F32)).astype(BF16)
    return h1, h2, h3


def _mm_sel_rhs(x, sel):
    h1, h2, h3 = _split3(x)
    d = lambda h: jnp.dot(h, sel, preferred_element_type=F32)
    return d(h1) + d(h2) + d(h3)


def _mm_sel_lhs(sel, x):
    h1, h2, h3 = _split3(x)
    d = lambda h: jnp.dot(sel, h, preferred_element_type=F32)
    return d(h1) + d(h2) + d(h3)


def _sigmoid(x):
    return 1.0 / (1.0 + jnp.exp(-x))


def _silu(x):
    return x * _sigmoid(x)


def _silu_tanh(x):
    return x * (0.5 * jnp.tanh(0.5 * x) + 0.5)


def _softplus(x):
    return jnp.maximum(x, 0.0) + jnp.log1p(jnp.exp(-jnp.abs(x)))


def _rmsnorm(x, g):
    return x * lax.rsqrt(jnp.mean(x * x, axis=-1, keepdims=True) + EPS) * g


def _t5_bucket_np(dist):
    max_exact = N_BUCKETS // 2
    d = np.maximum(dist, 0)
    ratio = (np.log(np.maximum(d, 1).astype(np.float32) / np.float32(max_exact))
             / np.float32(math.log(WINDOW / max_exact))).astype(np.float32)
    large = np.minimum(max_exact + (ratio * np.float32(N_BUCKETS - max_exact)).astype(np.int32),
                       N_BUCKETS - 1)
    return np.where(d < max_exact, d, large).astype(np.int32)


def _bias_lookup(bucket, rb_ref, h):
    acc = jnp.zeros(bucket.shape, F32)
    for t in range(N_BUCKETS):
        acc = jnp.where(bucket == t, rb_ref[t, h], acc)
    return acc


def _inproj_kernel(x_ref, g_ref, w_ref, att_ref, xc_ref, dz_ref, ba_ref):
    xn = _rmsnorm(x_ref[...], g_ref[...]).astype(BF16)
    o0, o1, o2 = ATT_COLS, ATT_COLS + CONV_CH, ATT_COLS + CONV_CH + DN_WIDTH
    att_ref[...] = jnp.dot(xn, w_ref[:, :o0], preferred_element_type=F32)
    xc_ref[...] = jnp.dot(xn, w_ref[:, o0:o1], preferred_element_type=F32)
    dz_ref[...] = jnp.dot(xn, w_ref[:, o1:o2], preferred_element_type=F32)
    ba_ref[...] = jnp.dot(xn, w_ref[:, o2:], preferred_element_type=F32)


def _inproj(x, g, w):
    t = x.shape[0]
    tm = min(t, ROW_TM)
    row = lambda n: pl.BlockSpec((tm, n), lambda i: (i, 0))
    full = lambda a: pl.BlockSpec(a.shape, lambda i: (0,) * a.ndim)
    return pl.pallas_call(
        _inproj_kernel,
        grid=(t // tm,),
        in_specs=[row(D_MODEL), full(g), full(w)],
        out_specs=[row(ATT_COLS), row(CONV_CH), row(DN_WIDTH), row(LANES)],
        out_shape=[jax.ShapeDtypeStruct((t, n), F32) for n in (ATT_COLS, CONV_CH, DN_WIDTH, LANES)],
        compiler_params=_params("parallel"),
        name="inproj",
    )(x, g, w)


TAIL = 8
PAIR = 2 * DN_DK
N_PAIRS = DN_WIDTH // PAIR


def _head_sums(z, pair_ones):
    hi = z.astype(BF16)
    lw = (z - hi.astype(F32)).astype(BF16)
    d = lambda a, p: jnp.dot(a[:, p * PAIR:(p + 1) * PAIR], pair_ones, preferred_element_type=F32)
    return jnp.concatenate([d(hi, p) + d(lw, p) for p in range(N_PAIRS)], axis=1)


def _inproj_conv_kernel(x_ref, g_ref, w_ref, cw_ref, ones_ref, att_ref, qkv_ref, dz_ref, ba_ref, tail_ref,
                        xp_scr, *, tiles_per_seq):
    tm = x_ref.shape[0]

    @pl.when(pl.program_id(0) % tiles_per_seq == 0)
    def _():
        xp_scr[...] = jnp.zeros((TAIL, CONV_CH), F32)

    xn = _rmsnorm(x_ref[...], g_ref[...]).astype(BF16)
    o0, o1, o2 = ATT_COLS, ATT_COLS + CONV_CH, ATT_COLS + CONV_CH + DN_WIDTH
    xc = jnp.dot(xn, w_ref[:, o0:o1], preferred_element_type=F32)
    att_ref[...] = jnp.dot(xn, w_ref[:, :o0], preferred_element_type=F32)
    dz_ref[...] = jnp.dot(xn, w_ref[:, o1:o2], preferred_element_type=F32)
    ba_ref[...] = jnp.dot(xn, w_ref[:, o2:], preferred_element_type=F32)

    head = jnp.concatenate([xp_scr[...], xc[:TAIL, :]], axis=0)

    def shifted(j):
        return jnp.concatenate([head[TAIL - j:2 * TAIL - j, :], pltpu.roll(xc, j, axis=0)[TAIL:, :]], axis=0)

    y = shifted(3) * cw_ref[0:1, :]
    y = y + shifted(2) * cw_ref[1:2, :]
    y = y + shifted(1) * cw_ref[2:3, :]
    y = y + xc * cw_ref[3:4, :]
    tail = xc[tm - TAIL:, :]
    xp_scr[...] = tail
    tail_ref[0] = tail
    y = _silu_tanh(y)
    q = y[:, :DN_WIDTH]
    k = y[:, DN_WIDTH:2 * DN_WIDTH]
    inv_norm = lax.rsqrt(_head_sums(jnp.concatenate([q * q, k * k], axis=0), ones_ref[...]) + EPS)
    qkv_ref[:, :DN_WIDTH] = q * inv_norm[:tm] * (DN_DK ** -0.5)
    qkv_ref[:, DN_WIDTH:2 * DN_WIDTH] = k * inv_norm[tm:]
    qkv_ref[:, 2 * DN_WIDTH:] = y[:, 2 * DN_WIDTH:]


def _pair_ones():
    lane = np.arange(PAIR)
    return jnp.asarray((lane[:, None] // DN_DV == lane[None, :] // DN_DV).astype(np.float32), dtype=BF16)


def _inproj_conv(x, g, w, conv_w, seq):
    t = x.shape[0]
    tm = ROW_TM
    assert seq % tm == 0
    ones = _pair_ones()
    row = lambda n: pl.BlockSpec((tm, n), lambda i: (i, 0))
    full = lambda a: pl.BlockSpec(a.shape, lambda i: (0,) * a.ndim)
    return pl.pallas_call(
        functools.partial(_inproj_conv_kernel, tiles_per_seq=seq // tm),
        grid=(t // tm,),
        in_specs=[row(D_MODEL), full(g), full(w), full(conv_w), full(ones)],
        out_specs=[row(ATT_COLS), row(CONV_CH), row(DN_WIDTH), row(LANES),
                   pl.BlockSpec((1, TAIL, CONV_CH), lambda i: (i, 0, 0))],
        out_shape=[jax.ShapeDtypeStruct((t, n), F32) for n in (ATT_COLS, CONV_CH, DN_WIDTH, LANES)]
                  + [jax.ShapeDtypeStruct((t // tm, TAIL, CONV_CH), F32)],
        scratch_shapes=[pltpu.VMEM((TAIL, CONV_CH), F32)],
        compiler_params=_params("arbitrary"),
        name="inproj_conv",
    )(x, g, w, conv_w, ones)


GROUP_ROWS = GQA * ATT_BLOCK


def _attn_prompt_kernel(cur_ref, prev_ref, bucket_ref, rb_ref, sink_ref, o_ref, bias_scr, sink_scr):
    i = pl.program_id(0)
    nseq = cur_ref.shape[0]

    @pl.when(i == 0)
    def _():
        qi = lax.broadcasted_iota(jnp.int32, (ATT_BLOCK, 2 * ATT_BLOCK), 0)
        kj = lax.broadcasted_iota(jnp.int32, (ATT_BLOCK, 2 * ATT_BLOCK), 1)
        dist = qi + ATT_BLOCK - kj
        band = jnp.logical_and(dist >= 0, dist < WINDOW)
        bucket = bucket_ref[...]
        hrow = lax.broadcasted_iota(jnp.int32, (GROUP_ROWS, 1), 0) // ATT_BLOCK
        for g in range(ATT_KV_HEADS):
            sink_col = jnp.zeros((GROUP_ROWS, 1), F32)
            for hh in range(GQA):
                h = g * GQA + hh
                bias = jnp.where(band, _bias_lookup(bucket, rb_ref, h), NEG_INF)
                bias_scr[0, g, hh * ATT_BLOCK:(hh + 1) * ATT_BLOCK, :] = bias
                bias_scr[1, g, hh * ATT_BLOCK:(hh + 1) * ATT_BLOCK, :] = jnp.where(kj >= ATT_BLOCK, bias, NEG_INF)
                sink_col = jnp.where(hrow == hh, sink_ref[h], sink_col)
            sink_scr[g] = sink_col

    first = (i == 0).astype(jnp.int32)
    probs = [(b, g) for b in range(nseq) for g in range(ATT_KV_HEADS)]
    scores = []
    for b, g in probs:
        cur = cur_ref[b]
        prev = prev_ref[b]
        q = jnp.concatenate([cur[:, (g * GQA + hh) * HEAD_DIM:(g * GQA + hh + 1) * HEAD_DIM]
                             for hh in range(GQA)], axis=0) * (HEAD_DIM ** -0.5)
        kcol = slice(ATT_WIDTH + g * HEAD_DIM, ATT_WIDTH + (g + 1) * HEAD_DIM)
        k2 = jnp.concatenate([prev[:, kcol], cur[:, kcol]], axis=0)
        scores.append(_mm_nt(q, k2) + bias_scr[first, g])
    probs_p, dens = [], []
    for (b, g), s in zip(probs, scores):
        sink = sink_scr[g]
        m = jnp.maximum(jnp.max(s, axis=-1, keepdims=True), sink)
        p = jnp.exp(s - m)
        dens.append(jnp.sum(p, axis=-1, keepdims=True) + jnp.exp(sink - m))
        probs_p.append(p.astype(BF16))
    outs = {}
    for (b, g), p, den in zip(probs, probs_p, dens):
        vcol = slice(ATT_WIDTH + KV_WIDTH + g * HEAD_DIM, ATT_WIDTH + KV_WIDTH + (g + 1) * HEAD_DIM)
        v2 = jnp.concatenate([prev_ref[b][:, vcol], cur_ref[b][:, vcol]], axis=0)
        outs[b, g] = _mm(p, v2) / den
    for b in range(nseq):
        o_ref[b] = jnp.concatenate([outs[b, g][hh * ATT_BLOCK:(hh + 1) * ATT_BLOCK, :]
                                    for g in range(ATT_KV_HEADS) for hh in range(GQA)],
                                   axis=1).astype(o_ref.dtype)


def _attn_prompt(att, bucket, rel_bias, sink, batch, seq):
    nb = seq // ATT_BLOCK
    smem = pl.BlockSpec(memory_space=pltpu.SMEM)
    att3 = att.reshape(batch, seq, ATT_COLS)
    out = pl.pallas_call(
        _attn_prompt_kernel,
        grid=(nb,),
        in_specs=[
            pl.BlockSpec((batch, ATT_BLOCK, ATT_COLS), lambda i: (0, i, 0)),
            pl.BlockSpec((batch, ATT_BLOCK, ATT_COLS), lambda i: (0, jnp.maximum(i - 1, 0), 0)),
            pl.BlockSpec(bucket.shape, lambda i: (0, 0)),
            smem, smem,
        ],
        out_specs=pl.BlockSpec((batch, ATT_BLOCK, ATT_WIDTH), lambda i: (0, i, 0)),
        out_shape=jax.ShapeDtypeStruct((batch, seq, ATT_WIDTH), BF16),
        scratch_shapes=[pltpu.VMEM((2, ATT_KV_HEADS, GROUP_ROWS, 2 * ATT_BLOCK), F32),
                        pltpu.VMEM((ATT_KV_HEADS, GROUP_ROWS, 1), F32)],
        compiler_params=_params("arbitrary"),
        name="attn_prompt",
    )(att3, att3, bucket, rel_bias, sink)
    return out.reshape(batch * seq, ATT_WIDTH)


ATT_S_BB = 8


def _attn_sample_kernel(att_ref, ck_ref, cv_ref, bucket_ref, rb_ref, sink_ref, o_ref, ks_ref, vs_ref,
                        bias_scr, col_scr):
    hrow = lax.broadcasted_iota(jnp.int32, (ATT_HEADS, LANES), 0)
    lane = lax.broadcasted_iota(jnp.int32, (ATT_HEADS, LANES), 1)

    last = (lax.broadcasted_iota(jnp.int32, (3, WINDOW), 1) == WINDOW - 1).astype(BF16)
    is_last = lax.broadcasted_iota(jnp.int32, (KV_WIDTH, WINDOW), 1) == WINDOW - 1

    def shifted(cache_t, new_row):
        pieces = jnp.concatenate([p.astype(F32) for p in _split3(new_row)], axis=0).astype(BF16)
        col = lax.dot_general(pieces, last, (((0,), (0,)), ((), ())), preferred_element_type=F32)
        out = jnp.where(is_last, col, pltpu.roll(cache_t, WINDOW - 1, axis=1))
        return out.reshape(ATT_KV_HEADS, HEAD_DIM, WINDOW)

    for b in range(ATT_S_BB):
        row = att_ref[b:b + 1, :]
        ks_ref[b] = shifted(ck_ref[b].reshape(KV_WIDTH, WINDOW), row[:, ATT_WIDTH:ATT_WIDTH + KV_WIDTH])
        vs_ref[b] = shifted(cv_ref[b].reshape(KV_WIDTH, WINDOW), row[:, ATT_WIDTH + KV_WIDTH:])

    @pl.when(pl.program_id(0) == 0)
    def _():
        bucket = jnp.broadcast_to(bucket_ref[...], (ATT_HEADS, LANES))
        bias = jnp.zeros((ATT_HEADS, LANES), F32)
        cols = jnp.zeros((ATT_HEADS, LANES), F32)
        for h in range(ATT_HEADS):
            bias = jnp.where(hrow == h, _bias_lookup(bucket, rb_ref, h), bias)
            cols = jnp.where(jnp.logical_and(hrow == h, lane == 0), sink_ref[h], cols)
            cols = jnp.where(jnp.logical_and(hrow == h, lane == 1), rb_ref[0, h], cols)
        bias_scr[...] = jnp.where(lane >= 1, bias, NEG_INF)
        col_scr[...] = cols

    bias_c = bias_scr[...]
    sink = col_scr[:, 0:1]
    bias_n = col_scr[:, 1:2]
    same_group = (hrow // GQA) == (lane // HEAD_DIM)
    low_group = lax.broadcasted_iota(jnp.int32, (ATT_HEADS, HEAD_DIM), 0) < GQA
    rnd = lambda a: a.astype(BF16).astype(F32)
    seqs = range(ATT_S_BB)
    rows = [att_ref[b:b + 1, :] for b in seqs]
    q_bds = []
    for row in rows:
        q = row[:, :ATT_WIDTH] * (HEAD_DIM ** -0.5)
        qh = jnp.concatenate([q[:, h * HEAD_DIM:(h + 1) * HEAD_DIM] for h in range(ATT_HEADS)], axis=0)
        q_bds.append(jnp.where(same_group, jnp.concatenate([qh, qh], axis=1), 0.0))
    kv_t = lambda ref, b: ref[b].reshape(KV_WIDTH, WINDOW)
    s_cs = [_mm(q_bd, kv_t(ck_ref, b)) + bias_c for b, q_bd in zip(seqs, q_bds)]
    prs, pns = [], []
    for row, q_bd, s_c in zip(rows, q_bds, s_cs):
        kn = row[:, ATT_WIDTH:ATT_WIDTH + KV_WIDTH]
        s_n = jnp.sum(rnd(q_bd) * rnd(kn), axis=-1, keepdims=True) + bias_n
        m = jnp.maximum(jnp.maximum(jnp.max(s_c, axis=-1, keepdims=True), s_n), sink)
        p_c = jnp.exp(s_c - m)
        p_n = jnp.exp(s_n - m)
        den = jnp.sum(p_c, axis=-1, keepdims=True) + p_n + jnp.exp(sink - m)
        prs.append(p_c / den)
        pns.append(p_n / den)
    pvs = [_mm_nt(pr, kv_t(cv_ref, b)) for b, pr in zip(seqs, prs)]
    for b, row, pv, pn in zip(seqs, rows, pvs, pns):
        vn = row[:, ATT_WIDTH + KV_WIDTH:]
        o_full = pv + rnd(pn) * rnd(vn)
        o_sel = jnp.where(low_group, o_full[:, :HEAD_DIM], o_full[:, HEAD_DIM:])
        o_ref[b:b + 1, :] = jnp.concatenate([o_sel[h:h + 1, :] for h in range(ATT_HEADS)], axis=1)


def _attn_sample(att, ck, cv, bucket, rel_bias, sink):
    nseq = att.shape[0]
    smem = pl.BlockSpec(memory_space=pltpu.SMEM)
    cache = pl.BlockSpec((ATT_S_BB, ATT_KV_HEADS, HEAD_DIM, WINDOW), lambda i: (i, 0, 0, 0))
    return pl.pallas_call(
        _attn_sample_kernel,
        grid=(nseq // ATT_S_BB,),
        in_specs=[pl.BlockSpec((ATT_S_BB, ATT_COLS), lambda i: (i, 0)), cache, cache,
                  pl.BlockSpec(bucket.shape, lambda i: (0, 0)), smem, smem],
        out_specs=[pl.BlockSpec((ATT_S_BB, ATT_WIDTH), lambda i: (i, 0)), cache, cache],
        out_shape=[jax.ShapeDtypeStruct((nseq, ATT_WIDTH), F32),
                   jax.ShapeDtypeStruct(ck.shape, F32), jax.ShapeDtypeStruct(cv.shape, F32)],
        scratch_shapes=[pltpu.VMEM((ATT_HEADS, LANES), F32), pltpu.VMEM((ATT_HEADS, LANES), F32)],
        compiler_params=_params("arbitrary"),
        name="attn_sample",
    )(att, ck, cv, bucket, rel_bias, sink)


GDN_TB = 128
GDN_NC = GDN_TB // DN_CHUNK


def _gdn_gates(ba, alog, dtb):
    beta = _sigmoid(ba)
    g = -jnp.exp(alog) * _softplus(ba + dtb)
    return beta, g


def _pair_diag(x, lo):
    xb = x.astype(BF16)
    zero = jnp.zeros_like(xb)
    return jnp.concatenate([jnp.where(lo, xb, zero), jnp.where(lo, zero, xb)], axis=0)


def _gdn_prompt_kernel(qkv_ref, dz_ref, ba_ref, alog_ref, dtb_ref, dnx_ref,
                       hsum_ref, expb_ref, expg_ref, ltri_ref,
                       o_ref, s_out_ref, s_scr):
    i = pl.program_id(0)
    nb = qkv_ref.shape[0]

    @pl.when(i == 0)
    def _():
        s_scr[...] = jnp.zeros(s_scr.shape, F32)

    hsum = hsum_ref[...]
    ri = lax.broadcasted_iota(jnp.int32, (DN_CHUNK, PAIR), 0)
    ci = lax.broadcasted_iota(jnp.int32, (DN_CHUNK, PAIR), 1)
    lo = ci < DN_DK
    cj = jnp.where(lo, ci, ci - DN_DK)
    causal = ri >= cj
    strict = ri > cj
    eye = (ri == cj).astype(F32)

    def sel2(x, m):
        hi = x.astype(BF16)
        lw = (x - hi.astype(F32)).astype(BF16)
        return (jnp.dot(hi, m, preferred_element_type=F32) + jnp.dot(lw, m, preferred_element_type=F32))

    pre = []
    for b in range(nb):
        q = qkv_ref[b, :, :DN_WIDTH]
        k = qkv_ref[b, :, DN_WIDTH:2 * DN_WIDTH]
        v = qkv_ref[b, :, 2 * DN_WIDTH:]
        beta_c, g_c = _gdn_gates(ba_ref[b], alog_ref[...], dtb_ref[...])
        beta = sel2(beta_c, expb_ref[...])
        gam_c = _mm_sel_lhs(ltri_ref[...], g_c)
        gam = _mm_sel_rhs(gam_c, expg_ref[...])
        gam_t = gam_c.T
        kb = k * beta
        egam = jnp.exp(gam)
        pre.append(dict(q=q, k=k, kb=kb, vb=v * beta, qg=q * egam, wr=kb * egam, gam=gam, gam_t=gam_t))

    probs = [(c, b, p) for c in range(GDN_NC) for b in range(nb) for p in range(N_PAIRS)]
    pick = lambda m: jnp.where(lo, m[:DN_DK], m[DN_DK:])
    rows_of = lambda c: slice(c * DN_CHUNK, (c + 1) * DN_CHUNK)
    sl = lambda name, c, b, p: pre[b][name][rows_of(c), p * PAIR:(p + 1) * PAIR]
    raws = []
    for c, b, p in probs:
        k_p = sl("k", c, b, p)
        k_rows = jnp.concatenate([jnp.where(lo, k_p, 0.0), jnp.where(lo, 0.0, k_p)], axis=0)
        raws.append(_mm_nt(jnp.concatenate([sl("kb", c, b, p), sl("q", c, b, p)], axis=0), k_rows))
    pws, ts, qks = [], [], []
    for (c, b, p), raw in zip(probs, raws):
        gcol = sl("gam", c, b, p)
        h0 = DN_HEADS + 2 * p
        gam_t = pre[b]["gam_t"]
        grow = jnp.concatenate([gam_t[h0:h0 + 1, rows_of(c)], gam_t[h0 + 1:h0 + 2, rows_of(c)]], axis=1)
        decay = jnp.exp(jnp.where(causal, gcol - grow, NEG_INF))
        a = jnp.where(strict, raw[:DN_CHUNK] * decay, 0.0)
        qks.append(jnp.where(causal, raw[DN_CHUNK:] * decay, 0.0))
        pws.append(-a)
        ts.append(eye - a)
    pws = [_mm(pw, _pair_diag(pw, lo)) for pw in pws]
    for _ in range(4):
        rs = [_mm(jnp.concatenate([pw, t], axis=0), _pair_diag(pw, lo)) for pw, t in zip(pws, ts)]
        pws = [r[:DN_CHUNK] for r in rs]
        ts = [t + r[DN_CHUNK:] for t, r in zip(ts, rs)]
    rs = [_mm(t, _pair_diag(pw, lo)) for pw, t in zip(pws, ts)]
    ts = [t + r for t, r in zip(ts, rs)]
    sols = [_mm(t, jnp.concatenate([_pair_diag(sl("vb", c, b, p), lo), _pair_diag(sl("wr", c, b, p), lo)],
                                   axis=1)) for (c, b, p), t in zip(probs, ts)]
    qkuws = [_mm(qk, jnp.concatenate([_pair_diag(s[:, :PAIR], lo), _pair_diag(s[:, PAIR:], lo)], axis=1))
             for qk, s in zip(qks, sols)]
    crosses, gls = [], []
    for (c, b, p), s in zip(probs, sols):
        last = (c + 1) * DN_CHUNK - 1
        gam_last = pre[b]["gam"][last:last + 1, p * PAIR:(p + 1) * PAIR]
        kd = sl("k", c, b, p) * jnp.exp(gam_last - sl("gam", c, b, p))
        crosses.append(_mm_tn(kd, s))
        gls.append(jnp.exp(gam_last))
    lhs = [jnp.concatenate([pick(cr[:, PAIR:]), sl("qg", c, b, p) - qkuw[:, PAIR:]], axis=0)
           for (c, b, p), cr, qkuw in zip(probs, crosses, qkuws)]

    o_rows = [[] for _ in range(nb)]
    per_chunk = nb * N_PAIRS
    for c in range(GDN_NC):
        sel = slice(c * per_chunk, (c + 1) * per_chunk)
        s_olds = [s_scr[b, p] for _, b, p in probs[sel]]
        rs = [_mm(l, _pair_diag(s_old, lo)) for l, s_old in zip(lhs[sel], s_olds)]
        o_pairs = [[] for _ in range(nb)]
        for (_, b, p), r, s_old, gl, cr, qkuw in zip(probs[sel], rs, s_olds, gls[sel], crosses[sel], qkuws[sel]):
            s_scr[b, p] = gl * s_old - r[:DN_DK] + pick(cr[:, :PAIR])
            o_pairs[b].append(r[DN_DK:] + qkuw[:, :PAIR])
        for b in range(nb):
            o_rows[b].append(jnp.concatenate(o_pairs[b], axis=1))

    o_all = jnp.concatenate([jnp.concatenate(rows, axis=0) for rows in o_rows], axis=0)
    inv_rms = lax.rsqrt(_head_sums(o_all * o_all, hsum) * (1.0 / DN_DV) + EPS)
    for b in range(nb):
        rows = slice(b * GDN_TB, (b + 1) * GDN_TB)
        o_ref[b] = (o_all[rows] * inv_rms[rows] * dnx_ref[...] * _silu_tanh(dz_ref[b])).astype(o_ref.dtype)

    @pl.when(i == pl.num_programs(0) - 1)
    def _():
        for b in range(nb):
            for p in range(N_PAIRS):
                s_p = s_scr[b, p]
                s_out_ref[b, 2 * p] = s_p[:, :DN_DV]
                s_out_ref[b, 2 * p + 1] = s_p[:, DN_DV:]


def _gdn_consts():
    lane = np.arange(DN_WIDTH)
    pl_lane = np.arange(PAIR)
    hsum = (pl_lane[:, None] // DN_DV == pl_lane[None, :] // DN_DV)
    src = np.arange(LANES)
    expb = (src[:, None] == lane[None, :] // DN_DV)
    expg = (src[:, None] == DN_HEADS + lane[None, :] // DN_DV)
    tok = np.arange(GDN_TB)
    ltri = np.logical_and(tok[:, None] >= tok[None, :],
                          tok[:, None] // DN_CHUNK == tok[None, :] // DN_CHUNK)
    as_bf16 = lambda m: jnp.asarray(m.astype(np.float32), dtype=BF16)
    return as_bf16(hsum), as_bf16(expb), as_bf16(expg), as_bf16(ltri)


def _gdn_prompt(xc, dz, ba, alog, dtb, dnx, batch, seq):
    nt = seq // GDN_TB
    hsum, expb, expg, ltri = _gdn_consts()
    row = lambda n: pl.BlockSpec((batch, GDN_TB, n), lambda i: (0, i, 0))
    full = lambda a: pl.BlockSpec(a.shape, lambda i: (0,) * a.ndim)
    consts = (alog, dtb, dnx, hsum, expb, expg, ltri)
    as3d = lambda a: a.reshape(batch, seq, a.shape[-1])
    o, s = pl.pallas_call(
        _gdn_prompt_kernel,
        grid=(nt,),
        in_specs=[row(CONV_CH), row(DN_WIDTH), row(LANES)] + [full(a) for a in consts],
        out_specs=[row(DN_WIDTH),
                   pl.BlockSpec((batch, DN_HEADS, DN_DK, DN_DV), lambda i: (0, 0, 0, 0))],
        out_shape=[jax.ShapeDtypeStruct((batch, seq, DN_WIDTH), BF16),
                   jax.ShapeDtypeStruct((batch, DN_HEADS, DN_DK, DN_DV), F32)],
        scratch_shapes=[pltpu.VMEM((batch, N_PAIRS, DN_DK, PAIR), F32)],
        compiler_params=_params("arbitrary"),
        name="gdn_prompt",
    )(as3d(xc), as3d(dz), as3d(ba), *consts)
    return o.reshape(batch * seq, DN_WIDTH), s


def _gdn_sample_front_kernel(xc_ref, dz_ref, ba_ref, sc_ref, cw_ref, alog_ref, dtb_ref, hsum_ref,
                             q_ref, k_ref, v_ref, dz_t_ref, gates_ref):
    xc = xc_ref[...]
    y = sc_ref[0] * cw_ref[0:1, :]
    y = y + sc_ref[1] * cw_ref[1:2, :]
    y = y + sc_ref[2] * cw_ref[2:3, :]
    y = _silu(y + xc * cw_ref[3:4, :])
    hsum = hsum_ref[...]
    q = y[:, :DN_WIDTH]
    k = y[:, DN_WIDTH:2 * DN_WIDTH]
    q = q * lax.rsqrt(_mm_sel_rhs(q * q, hsum) + EPS) * (DN_DK ** -0.5)
    k = k * lax.rsqrt(_mm_sel_rhs(k * k, hsum) + EPS)
    beta_c, g_c = _gdn_gates(ba_ref[...], alog_ref[...], dtb_ref[...])
    q_ref[...] = q.T
    k_ref[...] = k.T
    v_ref[...] = y[:, 2 * DN_WIDTH:].T
    dz_t_ref[...] = dz_ref[...].T
    gates_ref[0:LANES, :] = beta_c.T
    gates_ref[LANES:, :] = jnp.exp(g_c).T


def _gdn_sample_step_kernel(q_ref, k_ref, v_ref, dz_ref, gates_ref, dn_ref, s_ref, o_ref, s_out_ref):
    h = pl.program_id(0)
    beta = gates_ref[pl.ds(h, 1), :]
    eg = gates_ref[pl.ds(LANES + DN_HEADS + h, 1), :]
    q, k, v = q_ref[...], k_ref[...], v_ref[...]
    w = (k * beta) * eg
    qg = q * eg
    ws = jnp.zeros(v.shape, F32)
    qs = jnp.zeros(v.shape, F32)
    for dk in range(DN_DK):
        s_dk = s_ref[0, dk]
        ws = ws + w[dk:dk + 1, :] * s_dk
        qs = qs + qg[dk:dk + 1, :] * s_dk
    v_new = v * beta - ws
    qk = jnp.sum(q * k, axis=0, keepdims=True)
    o = qs + qk * v_new
    for dk in range(DN_DK):
        s_out_ref[0, dk] = s_ref[0, dk] * eg + k[dk:dk + 1, :] * v_new
    o = o * lax.rsqrt(jnp.mean(o * o, axis=0, keepdims=True) + EPS) * dn_ref[...]
    o_ref[...] = o * _silu(dz_ref[...])


def _gdn_sample_lanes(xc, dz, ba, sconv_t, state_t, conv_w, alog, dtb, dn):
    nseq = xc.shape[0]
    assert nseq == LANES
    lane = np.arange(DN_WIDTH)
    hsum = jnp.asarray((lane[:, None] // DN_DV == lane[None, :] // DN_DV).astype(np.float32), dtype=BF16)
    full = lambda a: pl.BlockSpec(a.shape, lambda i: (0,) * a.ndim)
    cm = jax.ShapeDtypeStruct((DN_WIDTH, nseq), F32)
    front_in = (xc, dz, ba, sconv_t, conv_w, alog, dtb, hsum)
    q_t, k_t, v_t, dz_t, gates_t = pl.pallas_call(
        _gdn_sample_front_kernel,
        grid=(1,),
        in_specs=[full(a) for a in front_in],
        out_specs=[pl.BlockSpec((DN_WIDTH, nseq), lambda i: (0, 0))] * 4
                  + [pl.BlockSpec((2 * LANES, nseq), lambda i: (0, 0))],
        out_shape=[cm, cm, cm, cm, jax.ShapeDtypeStruct((2 * LANES, nseq), F32)],
        compiler_params=_params("arbitrary"),
        name="gdn_sample_front",
    )(*front_in)
    dn_b = jnp.broadcast_to(dn.reshape(DN_DV, 1), (DN_DV, nseq))
    head = pl.BlockSpec((DN_DK, nseq), lambda h: (h, 0))
    st = pl.BlockSpec((1, DN_DK, DN_DV, nseq), lambda h: (h, 0, 0, 0))
    return pl.pallas_call(
        _gdn_sample_step_kernel,
        grid=(DN_HEADS,),
        in_specs=[head, head, head, head, full(gates_t), full(dn_b), st],
        out_specs=[head, st],
        out_shape=[cm, jax.ShapeDtypeStruct(state_t.shape, F32)],
        compiler_params=_params("parallel"),
        name="gdn_sample_step",
    )(q_t, k_t, v_t, dz_t, gates_t, dn_b, state_t)


def _route(xn, wr):
    logits = jnp.dot(xn, wr, preferred_element_type=F32)
    lane = lax.broadcasted_iota(jnp.int32, logits.shape, 1).astype(F32)
    first_at = lambda hit: jnp.min(jnp.where(hit, lane, float(LANES)), axis=-1, keepdims=True)
    glog = jnp.where(lane < N_GROUPS, logits, NEG_INF)
    gmax = jnp.max(glog, axis=-1, keepdims=True)
    gsel = first_at(glog == gmax)
    pgsel = 1.0 / jnp.sum(jnp.exp(glog - gmax), axis=-1, keepdims=True)
    lo = ROUTER_OFF + gsel * EXPERTS_PER_GROUP
    in_group = jnp.logical_and(lane >= lo, lane < lo + EXPERTS_PER_GROUP)
    elog = jnp.where(in_group, logits, NEG_INF)
    m1 = jnp.max(elog, axis=-1, keepdims=True)
    i1 = first_at(elog == m1)
    z = jnp.sum(jnp.exp(elog - m1), axis=-1, keepdims=True)
    elog2 = jnp.where(lane == i1, NEG_INF, elog)
    m2 = jnp.max(elog2, axis=-1, keepdims=True)
    i2 = first_at(elog2 == m2)
    p1 = 1.0 / z
    p2 = jnp.exp(m2 - m1) / z
    tot = p1 + p2
    return lane, i1, i2, p1 / tot * pgsel, p2 / tot * pgsel


def _outproj(x_ref, oa_ref, od_ref, wo_ref):
    return x_ref[...] + _mm(oa_ref[...], wo_ref[:ATT_WIDTH, :]) + _mm(od_ref[...], wo_ref[ATT_WIDTH:, :])


def _outproj_router_kernel(x_ref, oa_ref, od_t_ref, wo_ref, g_ref, wr_ref, h_ref, xn_ref, gate_ref):
    h = (x_ref[...] + _mm(oa_ref[...], wo_ref[:ATT_WIDTH, :])
         + _mm(od_t_ref[...].T, wo_ref[ATT_WIDTH:, :]))
    h_ref[...] = h
    xn = _rmsnorm(h, g_ref[...]).astype(BF16)
    xn_ref[...] = xn
    lane, i1, i2, g1, g2 = _route(xn, wr_ref[...])
    gate_ref[...] = jnp.where(lane == i1, g1, 0.0) + jnp.where(lane == i2, g2, 0.0)


def _outproj_router(x, oa, od_t, wo, g, wr):
    t = x.shape[0]
    tm = t
    row = lambda n: pl.BlockSpec((tm, n), lambda i: (i, 0))
    full = lambda a: pl.BlockSpec(a.shape, lambda i: (0,) * a.ndim)
    return pl.pallas_call(
        _outproj_router_kernel,
        grid=(t // tm,),
        in_specs=[row(D_MODEL), row(ATT_WIDTH), full(od_t), full(wo), full(g), full(wr)],
        out_specs=[row(D_MODEL), row(D_MODEL), row(LANES)],
        out_shape=[jax.ShapeDtypeStruct((t, D_MODEL), F32), jax.ShapeDtypeStruct((t, D_MODEL), BF16),
                   jax.ShapeDtypeStruct((t, LANES), F32)],
        compiler_params=_params("parallel"),
        name="outproj_router",
    )(x, oa, od_t, wo, g, wr)


MOE_TM = 512
POS_TM = 1024
INFO_G1, INFO_G2, INFO_E1, INFO_E2 = 0, 1, 2, 3


def _moe_tiles(t):
    return (2 * t) // MOE_TM + N_EXPERTS


HALF = D_MODEL // 2
U32 = jnp.uint32


def _pack_rows(x):
    bits = lambda v: lax.bitcast_convert_type(v.astype(BF16).astype(F32), U32)
    return bits(x[:, HALF:]) | (bits(x[:, :HALF]) >> 16)


def _unpack_rows(w):
    lo = lax.bitcast_convert_type(w << 16, F32)
    hi = lax.bitcast_convert_type(w & jnp.uint32(0xFFFF0000), F32)
    return lo, hi


def _route_kernel(x_ref, oa_ref, od_ref, wo_ref, g_ref, wr_ref, h_ref, xn_ref, info_ref, cnt_ref, run_scr):
    h = _outproj(x_ref, oa_ref, od_ref, wo_ref)
    h_ref[...] = h
    xn = _rmsnorm(h, g_ref[...])
    xn_ref[...] = _pack_rows(xn)
    lane, i1, i2, g1, g2 = _route(xn.astype(BF16), wr_ref[...])
    info = jnp.where(lane == INFO_G1, g1, 0.0) + jnp.where(lane == INFO_G2, g2, 0.0)
    info = info + jnp.where(lane == INFO_E1, i1, 0.0) + jnp.where(lane == INFO_E2, i2, 0.0)
    info_ref[...] = info

    @pl.when(pl.program_id(0) == 0)
    def _():
        run_scr[...] = jnp.zeros(run_scr.shape, F32)
    picked = jnp.logical_or(lane == i1, lane == i2).astype(F32)
    run_scr[...] += jnp.sum(picked, axis=0, keepdims=True)
    cnt_ref[...] = run_scr[...]


def _route_sparse(x, oa, od, wo, g, wr):
    t = x.shape[0]
    tm = WIDE_TM
    row = lambda n: pl.BlockSpec((tm, n), lambda i: (i, 0))
    full = lambda a: pl.BlockSpec(a.shape, lambda i: (0,) * a.ndim)
    return pl.pallas_call(
        _route_kernel,
        grid=(t // tm,),
        in_specs=[row(D_MODEL), row(ATT_WIDTH), row(DN_WIDTH), full(wo), full(g), full(wr)],
        out_specs=[row(D_MODEL), row(HALF), row(LANES), pl.BlockSpec((1, LANES), lambda i: (0, 0))],
        out_shape=[jax.ShapeDtypeStruct((t, D_MODEL), F32), jax.ShapeDtypeStruct((t, HALF), U32),
                   jax.ShapeDtypeStruct((t, LANES), F32), jax.ShapeDtypeStruct((1, LANES), F32)],
        scratch_shapes=[pltpu.VMEM((1, LANES), F32)],
        compiler_params=_params("arbitrary"),
        name="route",
    )(x, oa, od, wo, g, wr)


def _positions_kernel(info_ref, cnt_ref, ltri_ref, utri_ref, pos_ref, run_scr, off_scr):
    info = info_ref[...]
    lane = lax.broadcasted_iota(jnp.int32, info.shape, 1).astype(F32)
    hit1 = lane == info[:, INFO_E1:INFO_E1 + 1]
    hit2 = lane == info[:, INFO_E2:INFO_E2 + 1]
    onehot = jnp.logical_or(hit1, hit2).astype(F32)

    @pl.when(pl.program_id(0) == 0)
    def _():
        ln = lax.broadcasted_iota(jnp.int32, cnt_ref.shape, 1)
        is_expert = jnp.logical_and(ln >= ROUTER_OFF, ln < ROUTER_OFF + N_EXPERTS)
        tiles = jnp.where(is_expert, jnp.maximum(jnp.floor((cnt_ref[...] + (MOE_TM - 1)) * (1.0 / MOE_TM)), 1.0), 0.0)
        off_scr[...] = MOE_TM * jnp.dot(tiles.astype(BF16), utri_ref[...], preferred_element_type=F32)
        run_scr[...] = jnp.zeros(run_scr.shape, F32)

    before = (jnp.dot(ltri_ref[...], onehot.astype(BF16), preferred_element_type=F32)
              + run_scr[...] + off_scr[...])
    pos1 = jnp.sum(jnp.where(hit1, before, 0.0), axis=-1, keepdims=True)
    pos2 = jnp.sum(jnp.where(hit2, before, 0.0), axis=-1, keepdims=True)
    both = jnp.where(lane == 0, pos1, 0.0) + jnp.where(lane == 1, pos2, 0.0)
    pos_ref[...] = both.T.astype(jnp.int32)
    run_scr[...] += jnp.sum(onehot, axis=0, keepdims=True)


def _positions(info, cnt):
    t = info.shape[0]
    tm = min(t, POS_TM)
    tok = np.arange(tm)
    ltri = jnp.asarray((tok[:, None] > tok[None, :]).astype(np.float32), dtype=BF16)
    ln = np.arange(LANES)
    utri = jnp.asarray((ln[:, None] < ln[None, :]).astype(np.float32), dtype=BF16)
    full = lambda a: pl.BlockSpec(a.shape, lambda i: (0,) * a.ndim)
    return pl.pallas_call(
        _positions_kernel,
        grid=(t // tm,),
        in_specs=[pl.BlockSpec((tm, LANES), lambda i: (i, 0)), full(cnt), full(ltri), full(utri)],
        out_specs=pl.BlockSpec((LANES, tm), lambda i: (0, i)),
        out_shape=jax.ShapeDtypeStruct((LANES, t), jnp.int32),
        scratch_shapes=[pltpu.VMEM((1, LANES), F32), pltpu.VMEM((1, LANES), F32)],
        compiler_params=_params("arbitrary"),
        name="positions",
    )(info, cnt, ltri, utri)


def _experts_kernel(te_ref, tv_ref, nt_ref, xs_ref, wg_hbm, wu_hbm, wd_hbm, xn_new_ref, gate_new_ref,
                    ys_ref, moe_new_ref, wg_s, wu_s, wd_s, wg_f, wu_f, wd_f, wsem):
    i = pl.program_id(0)
    used = i < nt_ref[0]
    expert = te_ref[i]

    def fetch(e):
        slot = e % 2
        return [pltpu.make_async_copy(src.at[e], dst.at[slot], wsem.at[slot, j])
                for j, (src, dst) in enumerate(((wg_hbm, wg_f), (wu_hbm, wu_f), (wd_hbm, wd_f)))]

    @pl.when(jnp.logical_or(i == 0, expert != te_ref[jnp.maximum(i - 1, 0)]))
    def _():
        @pl.when(i == 0)
        def _():
            for c in fetch(expert):
                c.start()
        for c in fetch(expert):
            c.wait()

        @pl.when(expert + 1 < N_EXPERTS)
        def _():
            for c in fetch(expert + 1):
                c.start()
        slot = expert % 2
        wg_s[...] = wg_f[slot].astype(BF16)
        wu_s[...] = wu_f[slot].astype(BF16)
        wd_s[...] = wd_f[slot].astype(BF16)
        xn = xn_new_ref[...]
        lane = lax.broadcasted_iota(jnp.int32, gate_new_ref.shape, 1)
        gate = jnp.sum(jnp.where(lane == expert + ROUTER_OFF, gate_new_ref[...], 0.0), axis=-1, keepdims=True)
        hg = jnp.dot(xn, wg_s[...], preferred_element_type=F32)
        hu = jnp.dot(xn, wu_s[...], preferred_element_type=F32)
        hm = _silu(hg) * hu * gate
        y = jnp.dot(hm.astype(BF16), wd_s[...], preferred_element_type=F32)

        @pl.when(i == 0)
        def _():
            moe_new_ref[...] = y

        @pl.when(i > 0)
        def _():
            moe_new_ref[...] += y

    @pl.when(used)
    def _():
        row = lax.broadcasted_iota(jnp.int32, xs_ref.shape, 0)
        x_lo, x_hi = _unpack_rows(jnp.where(row < tv_ref[i], xs_ref[...], jnp.uint32(0)))
        x_lo = x_lo.astype(BF16)
        x_hi = x_hi.astype(BF16)
        up = lambda w_s: (jnp.dot(x_lo, w_s[:HALF, :], preferred_element_type=F32)
                          + jnp.dot(x_hi, w_s[HALF:, :], preferred_element_type=F32))
        hm = (_silu_tanh(up(wg_s)) * up(wu_s)).astype(BF16)
        ys_ref[...] = _pack_rows(jnp.dot(hm, wd_s[...], preferred_element_type=F32))

    @pl.when(jnp.logical_not(used))
    def _():
        ys_ref[...] = jnp.zeros(ys_ref.shape, U32)


def _experts(xs, tile_expert, tile_valid, n_tiles, wg, wu, wd, xn_new, gate_new):
    max_tiles = xs.shape[0] // MOE_TM
    rows = pl.BlockSpec((MOE_TM, HALF), lambda i, te, tv, nt: (i, 0))
    hbm = pl.BlockSpec(memory_space=pl.ANY)
    full = lambda a: pl.BlockSpec(a.shape, lambda i, te, tv, nt: (0,) * a.ndim)
    return pl.pallas_call(
        _experts_kernel,
        grid_spec=pltpu.PrefetchScalarGridSpec(
            num_scalar_prefetch=3, grid=(max_tiles,),
            in_specs=[rows, hbm, hbm, hbm, full(xn_new), full(gate_new)],
            out_specs=[rows, pl.BlockSpec(xn_new.shape, lambda i, te, tv, nt: (0, 0))],
            scratch_shapes=[pltpu.VMEM((D_MODEL, D_EXPERT), BF16), pltpu.VMEM((D_MODEL, D_EXPERT), BF16),
                            pltpu.VMEM((D_EXPERT, D_MODEL), BF16),
                            pltpu.VMEM((2, D_MODEL, D_EXPERT), F32), pltpu.VMEM((2, D_MODEL, D_EXPERT), F32),
                            pltpu.VMEM((2, D_EXPERT, D_MODEL), F32), pltpu.SemaphoreType.DMA((2, 3))]),
        out_shape=[jax.ShapeDtypeStruct(xs.shape, U32), jax.ShapeDtypeStruct(xn_new.shape, F32)],
        compiler_params=_params("arbitrary"),
        name="experts",
    )(tile_expert, tile_valid, n_tiles, xs, wg, wu, wd, xn_new, gate_new)


SC_IDX = 128
SC_ROWS = 64
SC_WORKERS = 32


def _sc_mesh():
    return plsc.VectorSubcoreMesh(core_axis_name="c", subcore_axis_name="s")


def _sc_windows(t, fn):
    per_worker = t // SC_WORKERS
    worker = lax.axis_index(("c", "s"))

    @pl.loop(0, per_worker // SC_IDX)
    def _(w):
        fn(worker * per_worker + w * SC_IDX)


def _sc_scatter_rows(xn, pos1, pos2, n_rows):
    t, d = xn.shape
    assert t % (SC_WORKERS * SC_IDX) == 0
    idx_t = pltpu.VMEM((1, SC_IDX), jnp.int32)

    @pl.kernel(out_type=jax.ShapeDtypeStruct((n_rows, d), xn.dtype), mesh=_sc_mesh(),
               scratch_types=[idx_t, idx_t, pltpu.VMEM((SC_ROWS, d), xn.dtype)])
    def scatter(x_hbm, p1_hbm, p2_hbm, o_hbm, i1_v, i2_v, buf):
        def window(base):
            pltpu.sync_copy(p1_hbm.at[:, pl.ds(base, SC_IDX)], i1_v)
            pltpu.sync_copy(p2_hbm.at[:, pl.ds(base, SC_IDX)], i2_v)
            for k in range(SC_IDX // SC_ROWS):
                pltpu.sync_copy(x_hbm.at[pl.ds(base + k * SC_ROWS, SC_ROWS)], buf)
                pltpu.sync_copy(buf, o_hbm.at[i1_v.at[0, pl.ds(k * SC_ROWS, SC_ROWS)]])
                pltpu.sync_copy(buf, o_hbm.at[i2_v.at[0, pl.ds(k * SC_ROWS, SC_ROWS)]])
        _sc_windows(t, window)

    return scatter(xn, pos1.reshape(1, t), pos2.reshape(1, t))


def _sc_gather_rows(ys, pos1, pos2):
    t = pos1.shape[0]
    d = ys.shape[1]
    assert t % (SC_WORKERS * SC_IDX) == 0
    idx_t = pltpu.VMEM((1, SC_IDX), jnp.int32)
    out = jax.ShapeDtypeStruct((t, d), ys.dtype)

    buf_t = pltpu.VMEM((SC_ROWS, d), ys.dtype)

    @pl.kernel(out_type=(out, out), mesh=_sc_mesh(),
               scratch_types=[idx_t, idx_t, buf_t, buf_t, pltpu.SemaphoreType.DMA((2,)),
                              pltpu.SemaphoreType.DMA((2,))])
    def gather(y_hbm, p1_hbm, p2_hbm, o1_hbm, o2_hbm, i1_v, i2_v, buf_a, buf_b, gsem, wsem):
        bufs = (buf_a, buf_b)

        def window(base):
            pltpu.sync_copy(p1_hbm.at[:, pl.ds(base, SC_IDX)], i1_v)
            pltpu.sync_copy(p2_hbm.at[:, pl.ds(base, SC_IDX)], i2_v)
            items = [(idx_v, o_hbm, k) for k in range(SC_IDX // SC_ROWS)
                     for idx_v, o_hbm in ((i1_v, o1_hbm), (i2_v, o2_hbm))]

            def read(n):
                idx_v, _, k = items[n]
                return pltpu.make_async_copy(y_hbm.at[idx_v.at[0, pl.ds(k * SC_ROWS, SC_ROWS)]],
                                             bufs[n % 2], gsem.at[n % 2])

            def write(n):
                _, o_hbm, k = items[n]
                return pltpu.make_async_copy(bufs[n % 2], o_hbm.at[pl.ds(base + k * SC_ROWS, SC_ROWS)],
                                             wsem.at[n % 2])

            read(0).start()
            for n in range(len(items)):
                read(n).wait()
                if n >= 1:
                    write(n - 1).wait()
                if n + 1 < len(items):
                    read(n + 1).start()
                write(n).start()
            write(len(items) - 1).wait()
        _sc_windows(t, window)

    return gather(ys, pos1.reshape(1, t), pos2.reshape(1, t))


def _ple_sparse_kernel(h_ref, info_ref, y1_ref, y2_ref, p_ref, wpp_ref, wpg_ref, gp_ref, gf_ref, y_ref):
    info = info_ref[...]
    g1 = info[:, INFO_G1:INFO_G1 + 1]
    g2 = info[:, INFO_G2:INFO_G2 + 1]
    y1_lo, y1_hi = _unpack_rows(y1_ref[...])
    y2_lo, y2_hi = _unpack_rows(y2_ref[...])
    moe = jnp.concatenate([g1 * y1_lo + g2 * y2_lo, g1 * y1_hi + g2 * y2_hi], axis=1)
    h = h_ref[...] + moe
    hn = _rmsnorm(h, gp_ref[...])
    h = h + _mm(p_ref[...], wpp_ref[...]) * _sigmoid(_mm(hn, wpg_ref[...]))
    y_ref[...] = _rmsnorm(h, gf_ref[...])


def _ple_sparse(h, info, y1, y2, p, wpp, wpg, gp, gf):
    t = h.shape[0]
    tm = WIDE_TM
    row = lambda n: pl.BlockSpec((tm, n), lambda i: (i, 0))
    full = lambda a: pl.BlockSpec(a.shape, lambda i: (0,) * a.ndim)
    return pl.pallas_call(
        _ple_sparse_kernel,
        grid=(t // tm,),
        in_specs=[row(D_MODEL), row(LANES), row(HALF), row(HALF), row(PLE_DIM),
                  full(wpp), full(wpg), full(gp), full(gf)],
        out_specs=row(D_MODEL),
        out_shape=jax.ShapeDtypeStruct((t, D_MODEL), F32),
        compiler_params=_params("parallel"),
        name="ple_sparse",
    )(h, info, y1, y2, p, wpp, wpg, gp, gf)


def _tile_tables(cnt, max_tiles):
    tiles_e = jnp.maximum((cnt + (MOE_TM - 1)) // MOE_TM, 1)
    ends = jnp.cumsum(tiles_e)
    n_tiles = ends[-1]
    tile = jnp.arange(max_tiles, dtype=jnp.int32)
    idx = jnp.minimum(tile, n_tiles - 1)
    tile_expert = jnp.sum((idx[:, None] >= ends[None, :]).astype(jnp.int32), axis=1)
    mine = tile_expert[:, None] == jnp.arange(N_EXPERTS, dtype=jnp.int32)[None, :]
    of_mine = lambda v: jnp.sum(jnp.where(mine, v[None, :], 0), axis=1)
    valid = jnp.clip(of_mine(cnt) - (idx - of_mine(ends - tiles_e)) * MOE_TM, 0, MOE_TM)
    tile_valid = jnp.where(tile < n_tiles, valid, 0).astype(jnp.int32)
    return tile_expert, tile_valid, n_tiles.reshape(1)


def _ple_final_kernel(h_ref, m_ref, p_ref, wpp_ref, wpg_ref, gp_ref, gf_ref, y_ref):
    h = h_ref[...] + m_ref[...]
    hn = _rmsnorm(h, gp_ref[...])
    h = h + _mm(p_ref[...], wpp_ref[...]) * _sigmoid(_mm(hn, wpg_ref[...]))
    y_ref[...] = _rmsnorm(h, gf_ref[...])


def _ple_final(h, m, p, wpp, wpg, gp, gf):
    t = h.shape[0]
    tm = min(t, 256)
    row = lambda n: pl.BlockSpec((tm, n), lambda i: (i, 0))
    full = lambda a: pl.BlockSpec(a.shape, lambda i: (0,) * a.ndim)
    return pl.pallas_call(
        _ple_final_kernel,
        grid=(t // tm,),
        in_specs=[row(D_MODEL), row(D_MODEL), row(PLE_DIM), full(wpp), full(wpg), full(gp), full(gf)],
        out_specs=row(D_MODEL),
        out_shape=jax.ShapeDtypeStruct((t, D_MODEL), F32),
        compiler_params=_params("parallel"),
        name="ple_final",
    )(h, m, p, wpp, wpg, gp, gf)


def kernel(x_prompt, x_sample, p_prompt, p_sample, cache_k, cache_v, state_conv, state_S, rel_bias, norm_mix, w_in, att_sink, conv_w, dn_A_log, dn_dt_bias, dn_norm, w_out, norm_ffn, w_router_group, w_router_expert, w_gate, w_up, w_down, w_ple_proj, w_ple_gate, norm_ple, norm_final):
    batch, seq, _ = x_prompt.shape
    nseq = x_sample.shape[0]
    assert x_sample.shape[1] == 1 and norm_mix.shape[0] == 1 and cache_k.shape[2] == WINDOW
    assert seq % GDN_TB == 0 and seq % ATT_BLOCK == 0

    wi = w_in[0]
    o_db = ATT_COLS + CONV_CH
    w_in_re = jnp.concatenate(
        [wi[:, :o_db], wi[:, o_db + 2 * DN_HEADS:], wi[:, o_db:o_db + 2 * DN_HEADS],
         jnp.zeros((D_MODEL, LANES - 2 * DN_HEADS), F32)], axis=1).astype(BF16)
    row = lambda a: a.reshape(1, -1).astype(F32)
    pad_lanes = lambda a, off: jnp.zeros((1, LANES), F32).at[0, off:off + a.shape[0]].set(a)
    alog = pad_lanes(dn_A_log[0], DN_HEADS)
    dtb = pad_lanes(dn_dt_bias[0], DN_HEADS)
    dnx = jnp.tile(dn_norm[0], DN_HEADS).reshape(1, DN_WIDTH)
    w_router = jnp.concatenate(
        [w_router_group[0], w_router_expert[0],
         jnp.zeros((D_MODEL, LANES - N_GROUPS - N_EXPERTS), F32)], axis=1).astype(BF16)
    wo = w_out[0].astype(BF16)
    wg, wu, wd = w_gate[0], w_up[0], w_down[0]
    wpp, wpg = w_ple_proj[0].astype(BF16), w_ple_gate[0].astype(BF16)
    sink = att_sink[0]

    qi = np.arange(ATT_BLOCK)[:, None]
    kj = np.arange(2 * ATT_BLOCK)[None, :]
    bucket_p = jnp.asarray(_t5_bucket_np(qi + ATT_BLOCK - kj))
    bucket_s = jnp.asarray(_t5_bucket_np(WINDOW - np.arange(WINDOW)[None, :]))

    xp = x_prompt.reshape(batch * seq, D_MODEL)
    att_p, qkv_p, dz_p, ba_p, xc_tails = _inproj_conv(xp, row(norm_mix[0]), w_in_re, conv_w[0], seq)
    o_att_p = _attn_prompt(att_p, bucket_p, rel_bias, sink, batch, seq)
    o_dn_p, s_p = _gdn_prompt(qkv_p, dz_p, ba_p, alog, dtb, dnx, batch, seq)
    h1, xn2, info, cnt = _route_sparse(xp, o_att_p, o_dn_p, wo, row(norm_ffn[0]), w_router)
    pos = _positions(info, cnt)
    pos1, pos2 = pos[0], pos[1]
    max_tiles = _moe_tiles(batch * seq)
    cnt_e = cnt[0, ROUTER_OFF:ROUTER_OFF + N_EXPERTS].astype(jnp.int32)
    tile_expert, tile_valid, n_tiles = _tile_tables(cnt_e, max_tiles)
    xs_sorted = _sc_scatter_rows(xn2, pos1, pos2, max_tiles * MOE_TM)

    xs = x_sample.reshape(nseq, D_MODEL)
    att_s, xc_s, dz_s, ba_s = _inproj(xs, row(norm_mix[0]), w_in_re)
    ck_t = jnp.transpose(cache_k[0], (0, 2, 3, 1))
    cv_t = jnp.transpose(cache_v[0], (0, 2, 3, 1))
    o_att_s, ks_t, vs_t = _attn_sample(att_s, ck_t, cv_t, bucket_s, rel_bias, sink)
    sconv_t = jnp.swapaxes(state_conv[0], 0, 1)
    o_dn_s_t, s_s_t = _gdn_sample_lanes(xc_s, dz_s, ba_s, sconv_t, jnp.transpose(state_S[0], (1, 2, 3, 0)),
                                        conv_w[0], alog, dtb, dn_norm[0])
    s_s = jnp.transpose(s_s_t, (3, 0, 1, 2))

    h1_s, xn2_s, gates_s = _outproj_router(xs, o_att_s, o_dn_s_t, wo, row(norm_ffn[0]), w_router)

    ys, moe_s = _experts(xs_sorted, tile_expert, tile_valid, n_tiles, wg, wu, wd, xn2_s, gates_s)
    y1, y2 = _sc_gather_rows(ys, pos1, pos2)
    y_s = _ple_final(h1_s, moe_s, p_sample[0].reshape(nseq, PLE_DIM), wpp, wpg, row(norm_ple[0]),
                     row(norm_final))
    y_p = _ple_sparse(h1, info, y1, y2, p_prompt[0].reshape(batch * seq, PLE_DIM),
                      wpp, wpg, row(norm_ple[0]), row(norm_final))

    att_p3 = att_p.reshape(batch, seq, ATT_COLS)
    kv_shape = (1, batch, WINDOW, ATT_KV_HEADS, HEAD_DIM)
    k_p = att_p3[:, seq - WINDOW:, ATT_WIDTH:ATT_WIDTH + KV_WIDTH].reshape(kv_shape)
    v_p = att_p3[:, seq - WINDOW:, ATT_WIDTH + KV_WIDTH:].reshape(kv_shape)
    conv_p = xc_tails.reshape(batch, -1, TAIL, CONV_CH)[:, -1, TAIL - (CONV_WIDTH - 1):][None]
    k_s = jnp.transpose(ks_t, (0, 3, 1, 2))[None]
    v_s = jnp.transpose(vs_t, (0, 3, 1, 2))[None]
    conv_s = jnp.concatenate([state_conv[0][:, 1:], xc_s[:, None, :]], axis=1)[None]
    return (y_p.reshape(batch, seq, D_MODEL), y_s.reshape(nseq, 1, D_MODEL),
            k_p, v_p, conv_p, s_p[None], k_s, v_s, conv_s, s_s[None])
```

```python
import functools
import math

import numpy as np
import jax
import jax.numpy as jnp
from jax import lax
from jax.experimental import pallas as pl
from jax.experimental.pallas import tpu as pltpu
from jax.experimental.pallas import tpu_sc as plsc

F32 = jnp.float32
BF16 = jnp.bfloat16

D_MODEL = 1024
ATT_HEADS = 8
ATT_KV_HEADS = 2
HEAD_DIM = 64
GQA = ATT_HEADS // ATT_KV_HEADS
WINDOW = 128
ATT_BLOCK = 128
N_BUCKETS = 32
DN_HEADS = 8
DN_DK = 64
DN_DV = 64
CONV_WIDTH = 4
DN_CHUNK = 64
ATT_WIDTH = ATT_HEADS * HEAD_DIM
KV_WIDTH = ATT_KV_HEADS * HEAD_DIM
DN_WIDTH = DN_HEADS * DN_DV
CONV_CH = 3 * DN_WIDTH
N_GROUPS = 4
EXPERTS_PER_GROUP = 8
N_EXPERTS = N_GROUPS * EXPERTS_PER_GROUP
D_EXPERT = 256
PLE_DIM = 256
EPS = 1e-6
NEG_INF = float("-inf")

ATT_COLS = ATT_WIDTH + 2 * KV_WIDTH
LANES = 128
ROUTER_OFF = N_GROUPS
VMEM_LIMIT = 48 * 1024 * 1024
ROW_TM = 512
WIDE_TM = 1024


def _params(*sem):
    return pltpu.CompilerParams(dimension_semantics=sem, vmem_limit_bytes=VMEM_LIMIT)


def _mm(a, b):
    return jnp.dot(a.astype(BF16), b.astype(BF16), preferred_element_type=F32)


def _mm_nt(a, b):
    return lax.dot_general(a.astype(BF16), b.astype(BF16), (((1,), (1,)), ((), ())),
                           preferred_element_type=F32)


def _mm_tn(a, b):
    return lax.dot_general(a.astype(BF16), b.astype(BF16), (((0,), (0,)), ((), ())),
                           preferred_element_type=F32)


def _split3(x):
    h1 = x.astype(BF16)
    r1 = x - h1.astype(F32)
    h2 = r1.astype(BF16)
    h3 = (r1 - h2.astype(F32)).astype(BF16)
    return h1, h2, h3


def _mm_sel_rhs(x, sel):
    h1, h2, h3 = _split3(x)
    d = lambda h: jnp.dot(h, sel, preferred_element_type=F32)
    return d(h1) + d(h2) + d(h3)


def _mm_sel_lhs(sel, x):
    h1, h2, h3 = _split3(x)
    d = lambda h: jnp.dot(sel, h, preferred_element_type=F32)
    return d(h1) + d(h2) + d(h3)


def _sigmoid(x):
    return 1.0 / (1.0 + jnp.exp(-x))


def _silu(x):
    return x * _sigmoid(x)


def _silu_tanh(x):
    return x * (0.5 * jnp.tanh(0.5 * x) + 0.5)


def _softplus(x):
    return jnp.maximum(x, 0.0) + jnp.log1p(jnp.exp(-jnp.abs(x)))


def _rmsnorm(x, g):
    return x * lax.rsqrt(jnp.mean(x * x, axis=-1, keepdims=True) + EPS) * g


def _t5_bucket_np(dist):
    max_exact = N_BUCKETS // 2
    d = np.maximum(dist, 0)
    ratio = (np.log(np.maximum(d, 1).astype(np.float32) / np.float32(max_exact))
             / np.float32(math.log(WINDOW / max_exact))).astype(np.float32)
    large = np.minimum(max_exact + (ratio * np.float32(N_BUCKETS - max_exact)).astype(np.int32),
                       N_BUCKETS - 1)
    return np.where(d < max_exact, d, large).astype(np.int32)


def _bias_lookup(bucket, rb_ref, h):
    acc = jnp.zeros(bucket.shape, F32)
    for t in range(N_BUCKETS):
        acc = jnp.where(bucket == t, rb_ref[t, h], acc)
    return acc


def _inproj_kernel(x_ref, g_ref, wa_ref, wz_ref, wb_ref, att_ref, xc_ref, dz_ref, ba_ref):
    xn = _rmsnorm(x_ref[...], g_ref[...]).astype(BF16)
    att_ref[...] = jnp.dot(xn, wa_ref[:, :ATT_COLS], preferred_element_type=F32)
    xc_ref[...] = jnp.dot(xn, wa_ref[:, ATT_COLS:], preferred_element_type=F32)
    dz_ref[...] = jnp.dot(xn, wz_ref[...], preferred_element_type=F32)
    ba_ref[...] = jnp.dot(xn, wb_ref[...], preferred_element_type=F32)


def _inproj(x, g, w):
    t = x.shape[0]
    tm = min(t, ROW_TM)
    row = lambda n: pl.BlockSpec((tm, n), lambda i: (i, 0))
    full = lambda a: pl.BlockSpec(a.shape, lambda i: (0,) * a.ndim)
    return pl.pallas_call(
        _inproj_kernel,
        grid=(t // tm,),
        in_specs=[row(D_MODEL), full(g)] + [full(a) for a in w],
        out_specs=[row(ATT_COLS), row(CONV_CH), row(DN_WIDTH), row(LANES)],
        out_shape=[jax.ShapeDtypeStruct((t, n), F32) for n in (ATT_COLS, CONV_CH, DN_WIDTH, LANES)],
        compiler_params=_params("parallel"),
        name="inproj",
    )(x, g, *w)


TAIL = 8
PAIR = 2 * DN_DK
N_PAIRS = DN_WIDTH // PAIR


def _head_sums(z, pair_ones):
    hi = z.astype(BF16)
    lw = (z - hi.astype(F32)).astype(BF16)
    d = lambda a, p: jnp.dot(a[:, p * PAIR:(p + 1) * PAIR], pair_ones, preferred_element_type=F32)
    return jnp.concatenate([d(hi, p) + d(lw, p) for p in range(N_PAIRS)], axis=1)


def _inproj_conv_kernel(x_ref, g_ref, wa_ref, wz_ref, wb_ref, cw_ref, ones_ref,
                        att_ref, qkv_ref, dz_ref, ba_ref, tail_ref, xp_scr, *, tiles_per_seq):
    tm = x_ref.shape[0]

    @pl.when(pl.program_id(0) % tiles_per_seq == 0)
    def _():
        xp_scr[...] = jnp.zeros((TAIL, CONV_CH), F32)

    xn = _rmsnorm(x_ref[...], g_ref[...]).astype(BF16)
    xc = jnp.dot(xn, wa_ref[:, ATT_COLS:], preferred_element_type=F32)
    att_ref[...] = jnp.dot(xn, wa_ref[:, :ATT_COLS], preferred_element_type=F32)
    dz_ref[...] = jnp.dot(xn, wz_ref[...], preferred_element_type=F32)
    ba_ref[...] = jnp.dot(xn, wb_ref[...], preferred_element_type=F32)

    head = jnp.concatenate([xp_scr[...], xc[:TAIL, :]], axis=0)

    def shifted(j):
        return jnp.concatenate([head[TAIL - j:2 * TAIL - j, :], pltpu.roll(xc, j, axis=0)[TAIL:, :]], axis=0)

    y = shifted(3) * cw_ref[0:1, :]
    y = y + shifted(2) * cw_ref[1:2, :]
    y = y + shifted(1) * cw_ref[2:3, :]
    y = y + xc * cw_ref[3:4, :]
    tail = xc[tm - TAIL:, :]
    xp_scr[...] = tail
    tail_ref[0] = tail
    y = _silu_tanh(y)
    q = y[:, :DN_WIDTH]
    k = y[:, DN_WIDTH:2 * DN_WIDTH]
    inv_norm = lax.rsqrt(_head_sums(jnp.concatenate([q * q, k * k], axis=0), ones_ref[...]) + EPS)
    qkv_ref[:, :DN_WIDTH] = q * inv_norm[:tm] * (DN_DK ** -0.5)
    qkv_ref[:, DN_WIDTH:2 * DN_WIDTH] = k * inv_norm[tm:]
    qkv_ref[:, 2 * DN_WIDTH:] = y[:, 2 * DN_WIDTH:]


def _pair_ones():
    lane = np.arange(PAIR)
    return jnp.asarray((lane[:, None] // DN_DV == lane[None, :] // DN_DV).astype(np.float32), dtype=BF16)


def _inproj_conv(x, g, w, conv_w, seq):
    t = x.shape[0]
    tm = ROW_TM
    assert seq % tm == 0
    ones = _pair_ones()
    row = lambda n: pl.BlockSpec((tm, n), lambda i: (i, 0))
    full = lambda a: pl.BlockSpec(a.shape, lambda i: (0,) * a.ndim)
    return pl.pallas_call(
        functools.partial(_inproj_conv_kernel, tiles_per_seq=seq // tm),
        grid=(t // tm,),
        in_specs=[row(D_MODEL), full(g)] + [full(a) for a in w] + [full(conv_w), full(ones)],
        out_specs=[row(ATT_COLS), row(CONV_CH), row(DN_WIDTH), row(LANES),
                   pl.BlockSpec((1, TAIL, CONV_CH), lambda i: (i, 0, 0))],
        out_shape=[jax.ShapeDtypeStruct((t, n), F32) for n in (ATT_COLS, CONV_CH, DN_WIDTH, LANES)]
                  + [jax.ShapeDtypeStruct((t // tm, TAIL, CONV_CH), F32)],
        scratch_shapes=[pltpu.VMEM((TAIL, CONV_CH), F32)],
        compiler_params=_params("arbitrary"),
        name="inproj_conv",
    )(x, g, *w, conv_w, ones)


GROUP_ROWS = GQA * ATT_BLOCK


def _attn_prompt_kernel(cur_ref, prev_ref, bucket_ref, rb_ref, sink_ref, o_ref, bias_scr, sink_scr):
    i = pl.program_id(0)
    nseq = cur_ref.shape[0]

    @pl.when(i == 0)
    def _():
        qi = lax.broadcasted_iota(jnp.int32, (ATT_BLOCK, 2 * ATT_BLOCK), 0)
        kj = lax.broadcasted_iota(jnp.int32, (ATT_BLOCK, 2 * ATT_BLOCK), 1)
        dist = qi + ATT_BLOCK - kj
        band = jnp.logical_and(dist >= 0, dist < WINDOW)
        bucket = bucket_ref[...]
        hrow = lax.broadcasted_iota(jnp.int32, (GROUP_ROWS, 1), 0) // ATT_BLOCK
        for g in range(ATT_KV_HEADS):
            sink_col = jnp.zeros((GROUP_ROWS, 1), F32)
            for hh in range(GQA):
                h = g * GQA + hh
                bias = jnp.where(band, _bias_lookup(bucket, rb_ref, h), NEG_INF)
                bias_scr[0, g, hh * ATT_BLOCK:(hh + 1) * ATT_BLOCK, :] = bias
                bias_scr[1, g, hh * ATT_BLOCK:(hh + 1) * ATT_BLOCK, :] = jnp.where(kj >= ATT_BLOCK, bias, NEG_INF)
                sink_col = jnp.where(hrow == hh, sink_ref[h], sink_col)
            sink_scr[g] = sink_col

    first = (i == 0).astype(jnp.int32)
    probs = [(b, g) for b in range(nseq) for g in range(ATT_KV_HEADS)]
    scores = []
    for b, g in probs:
        cur = cur_ref[b]
        prev = prev_ref[b]
        q = jnp.concatenate([cur[:, (g * GQA + hh) * HEAD_DIM:(g * GQA + hh + 1) * HEAD_DIM]
                             for hh in range(GQA)], axis=0) * (HEAD_DIM ** -0.5)
        kcol = slice(ATT_WIDTH + g * HEAD_DIM, ATT_WIDTH + (g + 1) * HEAD_DIM)
        k2 = jnp.concatenate([prev[:, kcol], cur[:, kcol]], axis=0)
        scores.append(_mm_nt(q, k2) + bias_scr[first, g])
    probs_p, dens = [], []
    for (b, g), s in zip(probs, scores):
        sink = sink_scr[g]
        m = jnp.maximum(jnp.max(s, axis=-1, keepdims=True), sink)
        p = jnp.exp(s - m)
        dens.append(jnp.sum(p, axis=-1, keepdims=True) + jnp.exp(sink - m))
        probs_p.append(p.astype(BF16))
    outs = {}
    for (b, g), p, den in zip(probs, probs_p, dens):
        vcol = slice(ATT_WIDTH + KV_WIDTH + g * HEAD_DIM, ATT_WIDTH + KV_WIDTH + (g + 1) * HEAD_DIM)
        v2 = jnp.concatenate([prev_ref[b][:, vcol], cur_ref[b][:, vcol]], axis=0)
        outs[b, g] = _mm(p, v2) / den
    for b in range(nseq):
        o_ref[b] = jnp.concatenate([outs[b, g][hh * ATT_BLOCK:(hh + 1) * ATT_BLOCK, :]
                                    for g in range(ATT_KV_HEADS) for hh in range(GQA)],
                                   axis=1).astype(o_ref.dtype)


def _attn_prompt(att, bucket, rel_bias, sink, batch, seq):
    nb = seq // ATT_BLOCK
    smem = pl.BlockSpec(memory_space=pltpu.SMEM)
    att3 = att.reshape(batch, seq, ATT_COLS)
    out = pl.pallas_call(
        _attn_prompt_kernel,
        grid=(nb,),
        in_specs=[
            pl.BlockSpec((batch, ATT_BLOCK, ATT_COLS), lambda i: (0, i, 0)),
            pl.BlockSpec((batch, ATT_BLOCK, ATT_COLS), lambda i: (0, jnp.maximum(i - 1, 0), 0)),
            pl.BlockSpec(bucket.shape, lambda i: (0, 0)),
            smem, smem,
        ],
        out_specs=pl.BlockSpec((batch, ATT_BLOCK, ATT_WIDTH), lambda i: (0, i, 0)),
        out_shape=jax.ShapeDtypeStruct((batch, seq, ATT_WIDTH), BF16),
        scratch_shapes=[pltpu.VMEM((2, ATT_KV_HEADS, GROUP_ROWS, 2 * ATT_BLOCK), F32),
                        pltpu.VMEM((ATT_KV_HEADS, GROUP_ROWS, 1), F32)],
        compiler_params=_params("arbitrary"),
        name="attn_prompt",
    )(att3, att3, bucket, rel_bias, sink)
    return out.reshape(batch * seq, ATT_WIDTH)


ATT_S_BB = 8


def _attn_sample_kernel(att_ref, ck_ref, cv_ref, bucket_ref, rb_ref, sink_ref, o_ref, ks_ref, vs_ref,
                        bias_scr, col_scr):
    hrow = lax.broadcasted_iota(jnp.int32, (ATT_HEADS, LANES), 0)
    lane = lax.broadcasted_iota(jnp.int32, (ATT_HEADS, LANES), 1)

    last = (lax.broadcasted_iota(jnp.int32, (3, WINDOW), 1) == WINDOW - 1).astype(BF16)
    is_last = lax.broadcasted_iota(jnp.int32, (KV_WIDTH, WINDOW), 1) == WINDOW - 1

    def shifted(cache_t, new_row):
        pieces = jnp.concatenate([p.astype(F32) for p in _split3(new_row)], axis=0).astype(BF16)
        col = lax.dot_general(pieces, last, (((0,), (0,)), ((), ())), preferred_element_type=F32)
        out = jnp.where(is_last, col, pltpu.roll(cache_t, WINDOW - 1, axis=1))
        return out.reshape(ATT_KV_HEADS, HEAD_DIM, WINDOW)

    for b in range(ATT_S_BB):
        row = att_ref[b:b + 1, :]
        ks_ref[b] = shifted(ck_ref[b].reshape(KV_WIDTH, WINDOW), row[:, ATT_WIDTH:ATT_WIDTH + KV_WIDTH])
        vs_ref[b] = shifted(cv_ref[b].reshape(KV_WIDTH, WINDOW), row[:, ATT_WIDTH + KV_WIDTH:])

    @pl.when(pl.program_id(0) == 0)
    def _():
        bucket = jnp.broadcast_to(bucket_ref[...], (ATT_HEADS, LANES))
        bias = jnp.zeros((ATT_HEADS, LANES), F32)
        cols = jnp.zeros((ATT_HEADS, LANES), F32)
        for h in range(ATT_HEADS):
            bias = jnp.where(hrow == h, _bias_lookup(bucket, rb_ref, h), bias)
            cols = jnp.where(jnp.logical_and(hrow == h, lane == 0), sink_ref[h], cols)
            cols = jnp.where(jnp.logical_and(hrow == h, lane == 1), rb_ref[0, h], cols)
        bias_scr[...] = jnp.where(lane >= 1, bias, NEG_INF)
        col_scr[...] = cols

    bias_c = bias_scr[...]
    sink = col_scr[:, 0:1]
    bias_n = col_scr[:, 1:2]
    same_group = (hrow // GQA) == (lane // HEAD_DIM)
    low_group = lax.broadcasted_iota(jnp.int32, (ATT_HEADS, HEAD_DIM), 0) < GQA
    rnd = lambda a: a.astype(BF16).astype(F32)
    seqs = range(ATT_S_BB)
    rows = [att_ref[b:b + 1, :] for b in seqs]
    q_bds = []
    for row in rows:
        q = row[:, :ATT_WIDTH] * (HEAD_DIM ** -0.5)
        qh = jnp.concatenate([q[:, h * HEAD_DIM:(h + 1) * HEAD_DIM] for h in range(ATT_HEADS)], axis=0)
        q_bds.append(jnp.where(same_group, jnp.concatenate([qh, qh], axis=1), 0.0))
    kv_t = lambda ref, b: ref[b].reshape(KV_WIDTH, WINDOW)
    s_cs = [_mm(q_bd, kv_t(ck_ref, b)) + bias_c for b, q_bd in zip(seqs, q_bds)]
    prs, pns = [], []
    for row, q_bd, s_c in zip(rows, q_bds, s_cs):
        kn = row[:, ATT_WIDTH:ATT_WIDTH + KV_WIDTH]
        s_n = jnp.sum(rnd(q_bd) * rnd(kn), axis=-1, keepdims=True) + bias_n
        m = jnp.maximum(jnp.maximum(jnp.max(s_c, axis=-1, keepdims=True), s_n), sink)
        p_c = jnp.exp(s_c - m)
        p_n = jnp.exp(s_n - m)
        den = jnp.sum(p_c, axis=-1, keepdims=True) + p_n + jnp.exp(sink - m)
        prs.append(p_c / den)
        pns.append(p_n / den)
    pvs = [_mm_nt(pr, kv_t(cv_ref, b)) for b, pr in zip(seqs, prs)]
    for b, row, pv, pn in zip(seqs, rows, pvs, pns):
        vn = row[:, ATT_WIDTH + KV_WIDTH:]
        o_full = pv + rnd(pn) * rnd(vn)
        o_sel = jnp.where(low_group, o_full[:, :HEAD_DIM], o_full[:, HEAD_DIM:])
        o_ref[b:b + 1, :] = jnp.concatenate([o_sel[h:h + 1, :] for h in range(ATT_HEADS)], axis=1)


def _attn_sample(att, ck, cv, bucket, rel_bias, sink):
    nseq = att.shape[0]
    smem = pl.BlockSpec(memory_space=pltpu.SMEM)
    cache = pl.BlockSpec((ATT_S_BB, ATT_KV_HEADS, HEAD_DIM, WINDOW), lambda i: (i, 0, 0, 0))
    return pl.pallas_call(
        _attn_sample_kernel,
        grid=(nseq // ATT_S_BB,),
        in_specs=[pl.BlockSpec((ATT_S_BB, ATT_COLS), lambda i: (i, 0)), cache, cache,
                  pl.BlockSpec(bucket.shape, lambda i: (0, 0)), smem, smem],
        out_specs=[pl.BlockSpec((ATT_S_BB, ATT_WIDTH), lambda i: (i, 0)), cache, cache],
        out_shape=[jax.ShapeDtypeStruct((nseq, ATT_WIDTH), F32),
                   jax.ShapeDtypeStruct(ck.shape, F32), jax.ShapeDtypeStruct(cv.shape, F32)],
        scratch_shapes=[pltpu.VMEM((ATT_HEADS, LANES), F32), pltpu.VMEM((ATT_HEADS, LANES), F32)],
        compiler_params=_params("arbitrary"),
        name="attn_sample",
    )(att, ck, cv, bucket, rel_bias, sink)


GDN_TB = 128
GDN_NC = GDN_TB // DN_CHUNK


def _gdn_gates(ba, alog, dtb):
    beta = _sigmoid(ba)
    g = -jnp.exp(alog) * _softplus(ba + dtb)
    return beta, g


def _pair_diag(x, lo):
    xb = x.astype(BF16)
    zero = jnp.zeros_like(xb)
    return jnp.concatenate([jnp.where(lo, xb, zero), jnp.where(lo, zero, xb)], axis=0)


def _gdn_prompt_kernel(qkv_ref, dz_ref, ba_ref, alog_ref, dtb_ref, dnx_ref,
                       hsum_ref, expb_ref, expg_ref, ltri_ref,
                       o_ref, s_out_ref, s_scr):
    i = pl.program_id(0)
    nb = qkv_ref.shape[0]

    @pl.when(i == 0)
    def _():
        s_scr[...] = jnp.zeros(s_scr.shape, F32)

    hsum = hsum_ref[...]
    ri = lax.broadcasted_iota(jnp.int32, (DN_CHUNK, PAIR), 0)
    ci = lax.broadcasted_iota(jnp.int32, (DN_CHUNK, PAIR), 1)
    lo = ci < DN_DK
    cj = jnp.where(lo, ci, ci - DN_DK)
    causal = ri >= cj
    strict = ri > cj
    eye = (ri == cj).astype(F32)

    def sel2(x, m):
        hi = x.astype(BF16)
        lw = (x - hi.astype(F32)).astype(BF16)
        return (jnp.dot(hi, m, preferred_element_type=F32) + jnp.dot(lw, m, preferred_element_type=F32))

    pre = []
    for b in range(nb):
        q = qkv_ref[b, :, :DN_WIDTH]
        k = qkv_ref[b, :, DN_WIDTH:2 * DN_WIDTH]
        v = qkv_ref[b, :, 2 * DN_WIDTH:]
        beta_c, g_c = _gdn_gates(ba_ref[b], alog_ref[...], dtb_ref[...])
        beta = sel2(beta_c, expb_ref[...])
        gam_c = _mm_sel_lhs(ltri_ref[...], g_c)
        gam = _mm_sel_rhs(gam_c, expg_ref[...])
        gam_t = gam_c.T
        kb = k * beta
        egam = jnp.exp(gam)
        pre.append(dict(q=q, k=k, kb=kb, vb=v * beta, qg=q * egam, wr=kb * egam, gam=gam, gam_t=gam_t))

    probs = [(c, b, p) for c in range(GDN_NC) for b in range(nb) for p in range(N_PAIRS)]
    pick = lambda m: jnp.where(lo, m[:DN_DK], m[DN_DK:])
    rows_of = lambda c: slice(c * DN_CHUNK, (c + 1) * DN_CHUNK)
    sl = lambda name, c, b, p: pre[b][name][rows_of(c), p * PAIR:(p + 1) * PAIR]
    raws = []
    for c, b, p in probs:
        k_p = sl("k", c, b, p)
        k_rows = jnp.concatenate([jnp.where(lo, k_p, 0.0), jnp.where(lo, 0.0, k_p)], axis=0)
        raws.append(_mm_nt(jnp.concatenate([sl("kb", c, b, p), sl("q", c, b, p)], axis=0), k_rows))
    pws, ts, qks = [], [], []
    for (c, b, p), raw in zip(probs, raws):
        gcol = sl("gam", c, b, p)
        h0 = DN_HEADS + 2 * p
        gam_t = pre[b]["gam_t"]
        grow = jnp.concatenate([gam_t[h0:h0 + 1, rows_of(c)], gam_t[h0 + 1:h0 + 2, rows_of(c)]], axis=1)
        decay = jnp.exp(jnp.where(causal, gcol - grow, NEG_INF))
        a = jnp.where(strict, raw[:DN_CHUNK] * decay, 0.0)
        qks.append(jnp.where(causal, raw[DN_CHUNK:] * decay, 0.0))
        pws.append(-a)
        ts.append(eye - a)
    pws = [_mm(pw, _pair_diag(pw, lo)) for pw in pws]
    for _ in range(4):
        rs = [_mm(jnp.concatenate([pw, t], axis=0), _pair_diag(pw, lo)) for pw, t in zip(pws, ts)]
        pws = [r[:DN_CHUNK] for r in rs]
        ts = [t + r[DN_CHUNK:] for t, r in zip(ts, rs)]
    rs = [_mm(t, _pair_diag(pw, lo)) for pw, t in zip(pws, ts)]
    ts = [t + r for t, r in zip(ts, rs)]
    sols = [_mm(t, jnp.concatenate([_pair_diag(sl("vb", c, b, p), lo), _pair_diag(sl("wr", c, b, p), lo)],
                                   axis=1)) for (c, b, p), t in zip(probs, ts)]
    qkuws = [_mm(qk, jnp.concatenate([_pair_diag(s[:, :PAIR], lo), _pair_diag(s[:, PAIR:], lo)], axis=1))
             for qk, s in zip(qks, sols)]
    crosses, gls = [], []
    for (c, b, p), s in zip(probs, sols):
        last = (c + 1) * DN_CHUNK - 1
        gam_last = pre[b]["gam"][last:last + 1, p * PAIR:(p + 1) * PAIR]
        kd = sl("k", c, b, p) * jnp.exp(gam_last - sl("gam", c, b, p))
        crosses.append(_mm_tn(kd, s))
        gls.append(jnp.exp(gam_last))
    lhs = [jnp.concatenate([pick(cr[:, PAIR:]), sl("qg", c, b, p) - qkuw[:, PAIR:]], axis=0)
           for (c, b, p), cr, qkuw in zip(probs, crosses, qkuws)]

    o_rows = [[] for _ in range(nb)]
    per_chunk = nb * N_PAIRS
    for c in range(GDN_NC):
        sel = slice(c * per_chunk, (c + 1) * per_chunk)
        s_olds = [s_scr[b, p] for _, b, p in probs[sel]]
        rs = [_mm(l, _pair_diag(s_old, lo)) for l, s_old in zip(lhs[sel], s_olds)]
        o_pairs = [[] for _ in range(nb)]
        for (_, b, p), r, s_old, gl, cr, qkuw in zip(probs[sel], rs, s_olds, gls[sel], crosses[sel], qkuws[sel]):
            s_scr[b, p] = gl * s_old - r[:DN_DK] + pick(cr[:, :PAIR])
            o_pairs[b].append(r[DN_DK:] + qkuw[:, :PAIR])
        for b in range(nb):
            o_rows[b].append(jnp.concatenate(o_pairs[b], axis=1))

    o_all = jnp.concatenate([jnp.concatenate(rows, axis=0) for rows in o_rows], axis=0)
    inv_rms = lax.rsqrt(_head_sums(o_all * o_all, hsum) * (1.0 / DN_DV) + EPS)
    for b in range(nb):
        rows = slice(b * GDN_TB, (b + 1) * GDN_TB)
        o_ref[b] = (o_all[rows] * inv_rms[rows] * dnx_ref[...] * _silu_tanh(dz_ref[b])).astype(o_ref.dtype)

    @pl.when(i == pl.num_programs(0) - 1)
    def _():
        for b in range(nb):
            for p in range(N_PAIRS):
                s_p = s_scr[b, p]
                s_out_ref[b, 2 * p] = s_p[:, :DN_DV]
                s_out_ref[b, 2 * p + 1] = s_p[:, DN_DV:]


def _gdn_consts():
    lane = np.arange(DN_WIDTH)
    pl_lane = np.arange(PAIR)
    hsum = (pl_lane[:, None] // DN_DV == pl_lane[None, :] // DN_DV)
    src = np.arange(LANES)
    expb = (src[:, None] == lane[None, :] // DN_DV)
    expg = (src[:, None] == DN_HEADS + lane[None, :] // DN_DV)
    tok = np.arange(GDN_TB)
    ltri = np.logical_and(tok[:, None] >= tok[None, :],
                          tok[:, None] // DN_CHUNK == tok[None, :] // DN_CHUNK)
    as_bf16 = lambda m: jnp.asarray(m.astype(np.float32), dtype=BF16)
    return as_bf16(hsum), as_bf16(expb), as_bf16(expg), as_bf16(ltri)


def _gdn_prompt(xc, dz, ba, alog, dtb, dnx, batch, seq):
    nt = seq // GDN_TB
    hsum, expb, expg, ltri = _gdn_consts()
    row = lambda n: pl.BlockSpec((batch, GDN_TB, n), lambda i: (0, i, 0))
    full = lambda a: pl.BlockSpec(a.shape, lambda i: (0,) * a.ndim)
    consts = (alog, dtb, dnx, hsum, expb, expg, ltri)
    as3d = lambda a: a.reshape(batch, seq, a.shape[-1])
    o, s = pl.pallas_call(
        _gdn_prompt_kernel,
        grid=(nt,),
        in_specs=[row(CONV_CH), row(DN_WIDTH), row(LANES)] + [full(a) for a in consts],
        out_specs=[row(DN_WIDTH),
                   pl.BlockSpec((batch, DN_HEADS, DN_DK, DN_DV), lambda i: (0, 0, 0, 0))],
        out_shape=[jax.ShapeDtypeStruct((batch, seq, DN_WIDTH), BF16),
                   jax.ShapeDtypeStruct((batch, DN_HEADS, DN_DK, DN_DV), F32)],
        scratch_shapes=[pltpu.VMEM((batch, N_PAIRS, DN_DK, PAIR), F32)],
        compiler_params=_params("arbitrary"),
        name="gdn_prompt",
    )(as3d(xc), as3d(dz), as3d(ba), *consts)
    return o.reshape(batch * seq, DN_WIDTH), s


def _gdn_sample_front_kernel(xc_ref, dz_ref, ba_ref, sc_ref, cw_ref, alog_ref, dtb_ref, hsum_ref,
                             q_ref, k_ref, v_ref, dz_t_ref, gates_ref):
    xc = xc_ref[...]
    y = sc_ref[0] * cw_ref[0:1, :]
    y = y + sc_ref[1] * cw_ref[1:2, :]
    y = y + sc_ref[2] * cw_ref[2:3, :]
    y = _silu(y + xc * cw_ref[3:4, :])
    hsum = hsum_ref[...]
    q = y[:, :DN_WIDTH]
    k = y[:, DN_WIDTH:2 * DN_WIDTH]
    q = q * lax.rsqrt(_mm_sel_rhs(q * q, hsum) + EPS) * (DN_DK ** -0.5)
    k = k * lax.rsqrt(_mm_sel_rhs(k * k, hsum) + EPS)
    beta_c, g_c = _gdn_gates(ba_ref[...], alog_ref[...], dtb_ref[...])
    q_ref[...] = q.T
    k_ref[...] = k.T
    v_ref[...] = y[:, 2 * DN_WIDTH:].T
    dz_t_ref[...] = dz_ref[...].T
    gates_ref[0:LANES, :] = beta_c.T
    gates_ref[LANES:, :] = jnp.exp(g_c).T


def _gdn_sample_step_kernel(q_ref, k_ref, v_ref, dz_ref, gates_ref, dn_ref, s_ref, o_ref, s_out_ref):
    h = pl.program_id(0)
    beta = gates_ref[pl.ds(h, 1), :]
    eg = gates_ref[pl.ds(LANES + DN_HEADS + h, 1), :]
    q, k, v = q_ref[...], k_ref[...], v_ref[...]
    w = (k * beta) * eg
    qg = q * eg
    ws = jnp.zeros(v.shape, F32)
    qs = jnp.zeros(v.shape, F32)
    for dk in range(DN_DK):
        s_dk = s_ref[0, dk]
        ws = ws + w[dk:dk + 1, :] * s_dk
        qs = qs + qg[dk:dk + 1, :] * s_dk
    v_new = v * beta - ws
    qk = jnp.sum(q * k, axis=0, keepdims=True)
    o = qs + qk * v_new
    for dk in range(DN_DK):
        s_out_ref[0, dk] = s_ref[0, dk] * eg + k[dk:dk + 1, :] * v_new
    o = o * lax.rsqrt(jnp.mean(o * o, axis=0, keepdims=True) + EPS) * dn_ref[...]
    o_ref[...] = o * _silu(dz_ref[...])


def _gdn_sample_lanes(xc, dz, ba, sconv_t, state_t, conv_w, alog, dtb, dn):
    nseq = xc.shape[0]
    assert nseq == LANES
    lane = np.arange(DN_WIDTH)
    hsum = jnp.asarray((lane[:, None] // DN_DV == lane[None, :] // DN_DV).astype(np.float32), dtype=BF16)
    full = lambda a: pl.BlockSpec(a.shape, lambda i: (0,) * a.ndim)
    cm = jax.ShapeDtypeStruct((DN_WIDTH, nseq), F32)
    front_in = (xc, dz, ba, sconv_t, conv_w, alog, dtb, hsum)
    q_t, k_t, v_t, dz_t, gates_t = pl.pallas_call(
        _gdn_sample_front_kernel,
        grid=(1,),
        in_specs=[full(a) for a in front_in],
        out_specs=[pl.BlockSpec((DN_WIDTH, nseq), lambda i: (0, 0))] * 4
                  + [pl.BlockSpec((2 * LANES, nseq), lambda i: (0, 0))],
        out_shape=[cm, cm, cm, cm, jax.ShapeDtypeStruct((2 * LANES, nseq), F32)],
        compiler_params=_params("arbitrary"),
        name="gdn_sample_front",
    )(*front_in)
    dn_b = jnp.broadcast_to(dn.reshape(DN_DV, 1), (DN_DV, nseq))
    head = pl.BlockSpec((DN_DK, nseq), lambda h: (h, 0))
    st = pl.BlockSpec((1, DN_DK, DN_DV, nseq), lambda h: (h, 0, 0, 0))
    return pl.pallas_call(
        _gdn_sample_step_kernel,
        grid=(DN_HEADS,),
        in_specs=[head, head, head, head, full(gates_t), full(dn_b), st],
        out_specs=[head, st],
        out_shape=[cm, jax.ShapeDtypeStruct(state_t.shape, F32)],
        compiler_params=_params("parallel"),
        name="gdn_sample_step",
    )(q_t, k_t, v_t, dz_t, gates_t, dn_b, state_t)


def _route(xn, wr):
    logits = jnp.dot(xn, wr, preferred_element_type=F32)
    lane = lax.broadcasted_iota(jnp.int32, logits.shape, 1).astype(F32)
    first_at = lambda hit: jnp.min(jnp.where(hit, lane, float(LANES)), axis=-1, keepdims=True)
    glog = jnp.where(lane < N_GROUPS, logits, NEG_INF)
    gmax = jnp.max(glog, axis=-1, keepdims=True)
    gsel = first_at(glog == gmax)
    pgsel = 1.0 / jnp.sum(jnp.exp(glog - gmax), axis=-1, keepdims=True)
    lo = ROUTER_OFF + gsel * EXPERTS_PER_GROUP
    in_group = jnp.logical_and(lane >= lo, lane < lo + EXPERTS_PER_GROUP)
    elog = jnp.where(in_group, logits, NEG_INF)
    m1 = jnp.max(elog, axis=-1, keepdims=True)
    i1 = first_at(elog == m1)
    z = jnp.sum(jnp.exp(elog - m1), axis=-1, keepdims=True)
    elog2 = jnp.where(lane == i1, NEG_INF, elog)
    m2 = jnp.max(elog2, axis=-1, keepdims=True)
    i2 = first_at(elog2 == m2)
    p1 = 1.0 / z
    p2 = jnp.exp(m2 - m1) / z
    tot = p1 + p2
    return lane, i1, i2, p1 / tot * pgsel, p2 / tot * pgsel


def _outproj(x_ref, oa_ref, od_ref, wo_ref):
    return x_ref[...] + _mm(oa_ref[...], wo_ref[:ATT_WIDTH, :]) + _mm(od_ref[...], wo_ref[ATT_WIDTH:, :])


def _outproj_router_kernel(x_ref, oa_ref, od_t_ref, wo_ref, g_ref, wr_ref, h_ref, xn_ref, gate_ref):
    h = (x_ref[...] + _mm(oa_ref[...], wo_ref[:ATT_WIDTH, :])
         + _mm(od_t_ref[...].T, wo_ref[ATT_WIDTH:, :]))
    h_ref[...] = h
    xn = _rmsnorm(h, g_ref[...]).astype(BF16)
    xn_ref[...] = xn
    lane, i1, i2, g1, g2 = _route(xn, wr_ref[...])
    gate_ref[...] = jnp.where(lane == i1, g1, 0.0) + jnp.where(lane == i2, g2, 0.0)


def _outproj_router(x, oa, od_t, wo, g, wr):
    t = x.shape[0]
    tm = t
    row = lambda n: pl.BlockSpec((tm, n), lambda i: (i, 0))
    full = lambda a: pl.BlockSpec(a.shape, lambda i: (0,) * a.ndim)
    return pl.pallas_call(
        _outproj_router_kernel,
        grid=(t // tm,),
        in_specs=[row(D_MODEL), row(ATT_WIDTH), full(od_t), full(wo), full(g), full(wr)],
        out_specs=[row(D_MODEL), row(D_MODEL), row(LANES)],
        out_shape=[jax.ShapeDtypeStruct((t, D_MODEL), F32), jax.ShapeDtypeStruct((t, D_MODEL), BF16),
                   jax.ShapeDtypeStruct((t, LANES), F32)],
        compiler_params=_params("parallel"),
        name="outproj_router",
    )(x, oa, od_t, wo, g, wr)


MOE_TM = 512
POS_TM = 1024
INFO_G1, INFO_G2, INFO_E1, INFO_E2 = 0, 1, 2, 3


def _moe_tiles(t):
    return (2 * t) // MOE_TM + N_EXPERTS


HALF = D_MODEL // 2
U32 = jnp.uint32


def _pack_rows(x):
    bits = lambda v: lax.bitcast_convert_type(v.astype(BF16).astype(F32), U32)
    return bits(x[:, HALF:]) | (bits(x[:, :HALF]) >> 16)


def _unpack_rows(w):
    lo = lax.bitcast_convert_type(w << 16, F32)
    hi = lax.bitcast_convert_type(w & jnp.uint32(0xFFFF0000), F32)
    return lo, hi


def _route_kernel(x_ref, oa_ref, od_ref, wo_ref, g_ref, wr_ref, h_ref, xn_ref, info_ref, cnt_ref, run_scr):
    h = _outproj(x_ref, oa_ref, od_ref, wo_ref)
    h_ref[...] = h
    xn = _rmsnorm(h, g_ref[...])
    xn_ref[...] = _pack_rows(xn)
    lane, i1, i2, g1, g2 = _route(xn.astype(BF16), wr_ref[...])
    info = jnp.where(lane == INFO_G1, g1, 0.0) + jnp.where(lane == INFO_G2, g2, 0.0)
    info = info + jnp.where(lane == INFO_E1, i1, 0.0) + jnp.where(lane == INFO_E2, i2, 0.0)
    info_ref[...] = info

    @pl.when(pl.program_id(0) == 0)
    def _():
        run_scr[...] = jnp.zeros(run_scr.shape, F32)
    picked = jnp.logical_or(lane == i1, lane == i2).astype(F32)
    run_scr[...] += jnp.sum(picked, axis=0, keepdims=True)
    cnt_ref[...] = run_scr[...]


def _route_sparse(x, oa, od, wo, g, wr):
    t = x.shape[0]
    tm = WIDE_TM
    row = lambda n: pl.BlockSpec((tm, n), lambda i: (i, 0))
    full = lambda a: pl.BlockSpec(a.shape, lambda i: (0,) * a.ndim)
    return pl.pallas_call(
        _route_kernel,
        grid=(t // tm,),
        in_specs=[row(D_MODEL), row(ATT_WIDTH), row(DN_WIDTH), full(wo), full(g), full(wr)],
        out_specs=[row(D_MODEL), row(HALF), row(LANES), pl.BlockSpec((1, LANES), lambda i: (0, 0))],
        out_shape=[jax.ShapeDtypeStruct((t, D_MODEL), F32), jax.ShapeDtypeStruct((t, HALF), U32),
                   jax.ShapeDtypeStruct((t, LANES), F32), jax.ShapeDtypeStruct((1, LANES), F32)],
        scratch_shapes=[pltpu.VMEM((1, LANES), F32)],
        compiler_params=_params("arbitrary"),
        name="route",
    )(x, oa, od, wo, g, wr)


def _positions_kernel(info_ref, cnt_ref, ltri_ref, utri_ref, pos_ref, run_scr, off_scr):
    info = info_ref[...]
    lane = lax.broadcasted_iota(jnp.int32, info.shape, 1).astype(F32)
    hit1 = lane == info[:, INFO_E1:INFO_E1 + 1]
    hit2 = lane == info[:, INFO_E2:INFO_E2 + 1]
    onehot = jnp.logical_or(hit1, hit2).astype(F32)

    @pl.when(pl.program_id(0) == 0)
    def _():
        ln = lax.broadcasted_iota(jnp.int32, cnt_ref.shape, 1)
        is_expert = jnp.logical_and(ln >= ROUTER_OFF, ln < ROUTER_OFF + N_EXPERTS)
        tiles = jnp.where(is_expert, jnp.maximum(jnp.floor((cnt_ref[...] + (MOE_TM - 1)) * (1.0 / MOE_TM)), 1.0), 0.0)
        off_scr[...] = MOE_TM * jnp.dot(tiles.astype(BF16), utri_ref[...], preferred_element_type=F32)
        run_scr[...] = jnp.zeros(run_scr.shape, F32)

    before = (jnp.dot(ltri_ref[...], onehot.astype(BF16), preferred_element_type=F32)
              + run_scr[...] + off_scr[...])
    pos1 = jnp.sum(jnp.where(hit1, before, 0.0), axis=-1, keepdims=True)
    pos2 = jnp.sum(jnp.where(hit2, before, 0.0), axis=-1, keepdims=True)
    both = jnp.where(lane == 0, pos1, 0.0) + jnp.where(lane == 1, pos2, 0.0)
    pos_ref[...] = both.T.astype(jnp.int32)
    run_scr[...] += jnp.sum(onehot, axis=0, keepdims=True)


def _positions(info, cnt):
    t = info.shape[0]
    tm = min(t, POS_TM)
    tok = np.arange(tm)
    ltri = jnp.asarray((tok[:, None] > tok[None, :]).astype(np.float32), dtype=BF16)
    ln = np.arange(LANES)
    utri = jnp.asarray((ln[:, None] < ln[None, :]).astype(np.float32), dtype=BF16)
    full = lambda a: pl.BlockSpec(a.shape, lambda i: (0,) * a.ndim)
    return pl.pallas_call(
        _positions_kernel,
        grid=(t // tm,),
        in_specs=[pl.BlockSpec((tm, LANES), lambda i: (i, 0)), full(cnt), full(ltri), full(utri)],
        out_specs=pl.BlockSpec((LANES, tm), lambda i: (0, i)),
        out_shape=jax.ShapeDtypeStruct((LANES, t), jnp.int32),
        scratch_shapes=[pltpu.VMEM((1, LANES), F32), pltpu.VMEM((1, LANES), F32)],
        compiler_params=_params("arbitrary"),
        name="positions",
    )(info, cnt, ltri, utri)


def _experts_kernel(te_ref, tv_ref, nt_ref, xs_ref, wg_hbm, wu_hbm, wd_hbm, xn_new_ref, gate_new_ref,
                    ys_ref, moe_new_ref, wg_s, wu_s, wd_s, wg_f, wu_f, wd_f, wsem):
    i = pl.program_id(0)
    used = i < nt_ref[0]
    expert = te_ref[i]

    def fetch(e):
        slot = e % 2
        return [pltpu.make_async_copy(src.at[e], dst.at[slot], wsem.at[slot, j])
                for j, (src, dst) in enumerate(((wg_hbm, wg_f), (wu_hbm, wu_f), (wd_hbm, wd_f)))]

    @pl.when(jnp.logical_or(i == 0, expert != te_ref[jnp.maximum(i - 1, 0)]))
    def _():
        @pl.when(i == 0)
        def _():
            for c in fetch(expert):
                c.start()
        for c in fetch(expert):
            c.wait()

        @pl.when(expert + 1 < N_EXPERTS)
        def _():
            for c in fetch(expert + 1):
                c.start()
        slot = expert % 2
        wg_s[...] = wg_f[slot].astype(BF16)
        wu_s[...] = wu_f[slot].astype(BF16)
        wd_s[...] = wd_f[slot].astype(BF16)
        xn = xn_new_ref[...]
        lane = lax.broadcasted_iota(jnp.int32, gate_new_ref.shape, 1)
        gate = jnp.sum(jnp.where(lane == expert + ROUTER_OFF, gate_new_ref[...], 0.0), axis=-1, keepdims=True)
        hg = jnp.dot(xn, wg_s[...], preferred_element_type=F32)
        hu = jnp.dot(xn, wu_s[...], preferred_element_type=F32)
        hm = _silu(hg) * hu * gate
        y = jnp.dot(hm.astype(BF16), wd_s[...], preferred_element_type=F32)

        @pl.when(i == 0)
        def _():
            moe_new_ref[...] = y

        @pl.when(i > 0)
        def _():
            moe_new_ref[...] += y

    @pl.when(used)
    def _():
        row = lax.broadcasted_iota(jnp.int32, xs_ref.shape, 0)
        x_lo, x_hi = _unpack_rows(jnp.where(row < tv_ref[i], xs_ref[...], jnp.uint32(0)))
        x_lo = x_lo.astype(BF16)
        x_hi = x_hi.astype(BF16)
        up = lambda w_s: (jnp.dot(x_lo, w_s[:HALF, :], preferred_element_type=F32)
                          + jnp.dot(x_hi, w_s[HALF:, :], preferred_element_type=F32))
        hm = (_silu_tanh(up(wg_s)) * up(wu_s)).astype(BF16)
        ys_ref[...] = _pack_rows(jnp.dot(hm, wd_s[...], preferred_element_type=F32))

    @pl.when(jnp.logical_not(used))
    def _():
        ys_ref[...] = jnp.zeros(ys_ref.shape, U32)


def _experts(xs, tile_expert, tile_valid, n_tiles, wg, wu, wd, xn_new, gate_new):
    max_tiles = xs.shape[0] // MOE_TM
    rows = pl.BlockSpec((MOE_TM, HALF), lambda i, te, tv, nt: (i, 0))
    hbm = pl.BlockSpec(memory_space=pl.ANY)
    full = lambda a: pl.BlockSpec(a.shape, lambda i, te, tv, nt: (0,) * a.ndim)
    return pl.pallas_call(
        _experts_kernel,
        grid_spec=pltpu.PrefetchScalarGridSpec(
            num_scalar_prefetch=3, grid=(max_tiles,),
            in_specs=[rows, hbm, hbm, hbm, full(xn_new), full(gate_new)],
            out_specs=[rows, pl.BlockSpec(xn_new.shape, lambda i, te, tv, nt: (0, 0))],
            scratch_shapes=[pltpu.VMEM((D_MODEL, D_EXPERT), BF16), pltpu.VMEM((D_MODEL, D_EXPERT), BF16),
                            pltpu.VMEM((D_EXPERT, D_MODEL), BF16),
                            pltpu.VMEM((2, D_MODEL, D_EXPERT), F32), pltpu.VMEM((2, D_MODEL, D_EXPERT), F32),
                            pltpu.VMEM((2, D_EXPERT, D_MODEL), F32), pltpu.SemaphoreType.DMA((2, 3))]),
        out_shape=[jax.ShapeDtypeStruct(xs.shape, U32), jax.ShapeDtypeStruct(xn_new.shape, F32)],
        compiler_params=_params("arbitrary"),
        name="experts",
    )(tile_expert, tile_valid, n_tiles, xs, wg, wu, wd, xn_new, gate_new)


SC_IDX = 128
SC_ROWS = 64
SC_WORKERS = 32


def _sc_mesh():
    return plsc.VectorSubcoreMesh(core_axis_name="c", subcore_axis_name="s")


def _sc_windows(t, fn):
    per_worker = t // SC_WORKERS
    worker = lax.axis_index(("c", "s"))

    @pl.loop(0, per_worker // SC_IDX)
    def _(w):
        fn(worker * per_worker + w * SC_IDX)


def _sc_scatter_rows(xn, pos1, pos2, n_rows):
    t, d = xn.shape
    assert t % (SC_WORKERS * SC_IDX) == 0
    idx_t = pltpu.VMEM((1, SC_IDX), jnp.int32)

    @pl.kernel(out_type=jax.ShapeDtypeStruct((n_rows, d), xn.dtype), mesh=_sc_mesh(),
               scratch_types=[idx_t, idx_t, pltpu.VMEM((SC_ROWS, d), xn.dtype)])
    def scatter(x_hbm, p1_hbm, p2_hbm, o_hbm, i1_v, i2_v, buf):
        def window(base):
            pltpu.sync_copy(p1_hbm.at[:, pl.ds(base, SC_IDX)], i1_v)
            pltpu.sync_copy(p2_hbm.at[:, pl.ds(base, SC_IDX)], i2_v)
            for k in range(SC_IDX // SC_ROWS):
                pltpu.sync_copy(x_hbm.at[pl.ds(base + k * SC_ROWS, SC_ROWS)], buf)
                pltpu.sync_copy(buf, o_hbm.at[i1_v.at[0, pl.ds(k * SC_ROWS, SC_ROWS)]])
                pltpu.sync_copy(buf, o_hbm.at[i2_v.at[0, pl.ds(k * SC_ROWS, SC_ROWS)]])
        _sc_windows(t, window)

    return scatter(xn, pos1.reshape(1, t), pos2.reshape(1, t))


def _sc_gather_rows(ys, pos1, pos2):
    t = pos1.shape[0]
    d = ys.shape[1]
    assert t % (SC_WORKERS * SC_IDX) == 0
    idx_t = pltpu.VMEM((1, SC_IDX), jnp.int32)
    out = jax.ShapeDtypeStruct((t, d), ys.dtype)

    buf_t = pltpu.VMEM((SC_ROWS, d), ys.dtype)

    @pl.kernel(out_type=(out, out), mesh=_sc_mesh(),
               scratch_types=[idx_t, idx_t, buf_t, buf_t, pltpu.SemaphoreType.DMA((2,)),
                              pltpu.SemaphoreType.DMA((2,))])
    def gather(y_hbm, p1_hbm, p2_hbm, o1_hbm, o2_hbm, i1_v, i2_v, buf_a, buf_b, gsem, wsem):
        bufs = (buf_a, buf_b)

        def window(base):
            pltpu.sync_copy(p1_hbm.at[:, pl.ds(base, SC_IDX)], i1_v)
            pltpu.sync_copy(p2_hbm.at[:, pl.ds(base, SC_IDX)], i2_v)
            items = [(idx_v, o_hbm, k) for k in range(SC_IDX // SC_ROWS)
                     for idx_v, o_hbm in ((i1_v, o1_hbm), (i2_v, o2_hbm))]

            def read(n):
                idx_v, _, k = items[n]
                return pltpu.make_async_copy(y_hbm.at[idx_v.at[0, pl.ds(k * SC_ROWS, SC_ROWS)]],
                                             bufs[n % 2], gsem.at[n % 2])

            def write(n):
                _, o_hbm, k = items[n]
                return pltpu.make_async_copy(bufs[n % 2], o_hbm.at[pl.ds(base + k * SC_ROWS, SC_ROWS)],
                                             wsem.at[n % 2])

            read(0).start()
            for n in range(len(items)):
                read(n).wait()
                if n >= 1:
                    write(n - 1).wait()
                if n + 1 < len(items):
                    read(n + 1).start()
                write(n).start()
            write(len(items) - 1).wait()
        _sc_windows(t, window)

    return gather(ys, pos1.reshape(1, t), pos2.reshape(1, t))


def _ple_sparse_kernel(h_ref, info_ref, y1_ref, y2_ref, p_ref, wpp_ref, wpg_ref, gp_ref, gf_ref, y_ref):
    info = info_ref[...]
    g1 = info[:, INFO_G1:INFO_G1 + 1]
    g2 = info[:, INFO_G2:INFO_G2 + 1]
    y1_lo, y1_hi = _unpack_rows(y1_ref[...])
    y2_lo, y2_hi = _unpack_rows(y2_ref[...])
    moe = jnp.concatenate([g1 * y1_lo + g2 * y2_lo, g1 * y1_hi + g2 * y2_hi], axis=1)
    h = h_ref[...] + moe
    hn = _rmsnorm(h, gp_ref[...])
    h = h + _mm(p_ref[...], wpp_ref[...]) * _sigmoid(_mm(hn, wpg_ref[...]))
    y_ref[...] = _rmsnorm(h, gf_ref[...])


def _ple_sparse(h, info, y1, y2, p, wpp, wpg, gp, gf):
    t = h.shape[0]
    tm = WIDE_TM
    row = lambda n: pl.BlockSpec((tm, n), lambda i: (i, 0))
    full = lambda a: pl.BlockSpec(a.shape, lambda i: (0,) * a.ndim)
    return pl.pallas_call(
        _ple_sparse_kernel,
        grid=(t // tm,),
        in_specs=[row(D_MODEL), row(LANES), row(HALF), row(HALF), row(PLE_DIM),
                  full(wpp), full(wpg), full(gp), full(gf)],
        out_specs=row(D_MODEL),
        out_shape=jax.ShapeDtypeStruct((t, D_MODEL), F32),
        compiler_params=_params("parallel"),
        name="ple_sparse",
    )(h, info, y1, y2, p, wpp, wpg, gp, gf)


def _tile_tables(cnt, max_tiles):
    tiles_e = jnp.maximum((cnt + (MOE_TM - 1)) // MOE_TM, 1)
    ends = jnp.cumsum(tiles_e)
    n_tiles = ends[-1]
    tile = jnp.arange(max_tiles, dtype=jnp.int32)
    idx = jnp.minimum(tile, n_tiles - 1)
    tile_expert = jnp.sum((idx[:, None] >= ends[None, :]).astype(jnp.int32), axis=1)
    mine = tile_expert[:, None] == jnp.arange(N_EXPERTS, dtype=jnp.int32)[None, :]
    of_mine = lambda v: jnp.sum(jnp.where(mine, v[None, :], 0), axis=1)
    valid = jnp.clip(of_mine(cnt) - (idx - of_mine(ends - tiles_e)) * MOE_TM, 0, MOE_TM)
    tile_valid = jnp.where(tile < n_tiles, valid, 0).astype(jnp.int32)
    return tile_expert, tile_valid, n_tiles.reshape(1)


def _ple_final_kernel(h_ref, m_ref, p_ref, wpp_ref, wpg_ref, gp_ref, gf_ref, y_ref):
    h = h_ref[...] + m_ref[...]
    hn = _rmsnorm(h, gp_ref[...])
    h = h + _mm(p_ref[...], wpp_ref[...]) * _sigmoid(_mm(hn, wpg_ref[...]))
    y_ref[...] = _rmsnorm(h, gf_ref[...])


def _ple_final(h, m, p, wpp, wpg, gp, gf):
    t = h.shape[0]
    tm = min(t, 256)
    row = lambda n: pl.BlockSpec((tm, n), lambda i: (i, 0))
    full = lambda a: pl.BlockSpec(a.shape, lambda i: (0,) * a.ndim)
    return pl.pallas_call(
        _ple_final_kernel,
        grid=(t // tm,),
        in_specs=[row(D_MODEL), row(D_MODEL), row(PLE_DIM), full(wpp), full(wpg), full(gp), full(gf)],
        out_specs=row(D_MODEL),
        out_shape=jax.ShapeDtypeStruct((t, D_MODEL), F32),
        compiler_params=_params("parallel"),
        name="ple_final",
    )(h, m, p, wpp, wpg, gp, gf)


def kernel(x_prompt, x_sample, p_prompt, p_sample, cache_k, cache_v, state_conv, state_S, rel_bias, norm_mix, w_in, att_sink, conv_w, dn_A_log, dn_dt_bias, dn_norm, w_out, norm_ffn, w_router_group, w_router_expert, w_gate, w_up, w_down, w_ple_proj, w_ple_gate, norm_ple, norm_final):
    batch, seq, _ = x_prompt.shape
    nseq = x_sample.shape[0]
    assert x_sample.shape[1] == 1 and norm_mix.shape[0] == 1 and cache_k.shape[2] == WINDOW
    assert seq % GDN_TB == 0 and seq % ATT_BLOCK == 0

    wi = w_in[0]
    o_db = ATT_COLS + CONV_CH
    w_in_re = (wi[:, :o_db].astype(BF16), wi[:, o_db + 2 * DN_HEADS:].astype(BF16),
               jnp.pad(wi[:, o_db:o_db + 2 * DN_HEADS], ((0, 0), (0, LANES - 2 * DN_HEADS))).astype(BF16))
    row = lambda a: a.reshape(1, -1).astype(F32)
    pad_lanes = lambda a, off: jnp.zeros((1, LANES), F32).at[0, off:off + a.shape[0]].set(a)
    alog = pad_lanes(dn_A_log[0], DN_HEADS)
    dtb = pad_lanes(dn_dt_bias[0], DN_HEADS)
    dnx = jnp.tile(dn_norm[0], DN_HEADS).reshape(1, DN_WIDTH)
    w_router = jnp.concatenate(
        [w_router_group[0], w_router_expert[0],
         jnp.zeros((D_MODEL, LANES - N_GROUPS - N_EXPERTS), F32)], axis=1).astype(BF16)
    wo = w_out[0].astype(BF16)
    wg, wu, wd = w_gate[0], w_up[0], w_down[0]
    wpp, wpg = w_ple_proj[0].astype(BF16), w_ple_gate[0].astype(BF16)
    sink = att_sink[0]

    qi = np.arange(ATT_BLOCK)[:, None]
    kj = np.arange(2 * ATT_BLOCK)[None, :]
    bucket_p = jnp.asarray(_t5_bucket_np(qi + ATT_BLOCK - kj))
    bucket_s = jnp.asarray(_t5_bucket_np(WINDOW - np.arange(WINDOW)[None, :]))

    xp = x_prompt.reshape(batch * seq, D_MODEL)
    att_p, qkv_p, dz_p, ba_p, xc_tails = _inproj_conv(xp, row(norm_mix[0]), w_in_re, conv_w[0], seq)
    o_att_p = _attn_prompt(att_p, bucket_p, rel_bias, sink, batch, seq)
    o_dn_p, s_p = _gdn_prompt(qkv_p, dz_p, ba_p, alog, dtb, dnx, batch, seq)
    h1, xn2, info, cnt = _route_sparse(xp, o_att_p, o_dn_p, wo, row(norm_ffn[0]), w_router)
    pos = _positions(info, cnt)
    pos1, pos2 = pos[0], pos[1]
    max_tiles = _moe_tiles(batch * seq)
    cnt_e = cnt[0, ROUTER_OFF:ROUTER_OFF + N_EXPERTS].astype(jnp.int32)
    tile_expert, tile_valid, n_tiles = _tile_tables(cnt_e, max_tiles)
    xs_sorted = _sc_scatter_rows(xn2, pos1, pos2, max_tiles * MOE_TM)

    xs = x_sample.reshape(nseq, D_MODEL)
    att_s, xc_s, dz_s, ba_s = _inproj(xs, row(norm_mix[0]), w_in_re)
    ck_t = jnp.transpose(cache_k[0], (0, 2, 3, 1))
    cv_t = jnp.transpose(cache_v[0], (0, 2, 3, 1))
    o_att_s, ks_t, vs_t = _attn_sample(att_s, ck_t, cv_t, bucket_s, rel_bias, sink)
    sconv_t = jnp.swapaxes(state_conv[0], 0, 1)
    o_dn_s_t, s_s_t = _gdn_sample_lanes(xc_s, dz_s, ba_s, sconv_t, jnp.transpose(state_S[0], (1, 2, 3, 0)),
                                        conv_w[0], alog, dtb, dn_norm[0])
    s_s = jnp.transpose(s_s_t, (3, 0, 1, 2))

    h1_s, xn2_s, gates_s = _outproj_router(xs, o_att_s, o_dn_s_t, wo, row(norm_ffn[0]), w_router)

    ys, moe_s = _experts(xs_sorted, tile_expert, tile_valid, n_tiles, wg, wu, wd, xn2_s, gates_s)
    y1, y2 = _sc_gather_rows(ys, pos1, pos2)
    y_s = _ple_final(h1_s, moe_s, p_sample[0].reshape(nseq, PLE_DIM), wpp, wpg, row(norm_ple[0]),
                     row(norm_final))
    y_p = _ple_sparse(h1, info, y1, y2, p_prompt[0].reshape(batch * seq, PLE_DIM),
                      wpp, wpg, row(norm_ple[0]), row(norm_final))

    att_p3 = att_p.reshape(batch, seq, ATT_COLS)
    kv_shape = (1, batch, WINDOW, ATT_KV_HEADS, HEAD_DIM)
    k_p = att_p3[:, seq - WINDOW:, ATT_WIDTH:ATT_WIDTH + KV_WIDTH].reshape(kv_shape)
    v_p = att_p3[:, seq - WINDOW:, ATT_WIDTH + KV_WIDTH:].reshape(kv_shape)
    conv_p = xc_tails.reshape(batch, -1, TAIL, CONV_CH)[:, -1, TAIL - (CONV_WIDTH - 1):][None]
    k_s = jnp.transpose(ks_t, (0, 3, 1, 2))[None]
    v_s = jnp.transpose(vs_t, (0, 3, 1, 2))[None]
    conv_s = jnp.concatenate([state_conv[0][:, 1:], xc_s[:, None, :]], axis=1)[None]
    return (y_p.reshape(batch, seq, D_MODEL), y_s.reshape(nseq, 1, D_MODEL),
            k_p, v_p, conv_p, s_p[None], k_s, v_s, conv_s, s_s[None])
```

```python
import functools
import math

import numpy as np
import jax
import jax.numpy as jnp
from jax import lax
from jax.experimental import pallas as pl
from jax.experimental.pallas import tpu as pltpu
from jax.experimental.pallas import tpu_sc as plsc

F32 = jnp.float32
BF16 = jnp.bfloat16

D_MODEL = 1024
ATT_HEADS = 8
ATT_KV_HEADS = 2
HEAD_DIM = 64
GQA = ATT_HEADS // ATT_KV_HEADS
WINDOW = 128
ATT_BLOCK = 128
N_BUCKETS = 32
DN_HEADS = 8
DN_DK = 64
DN_DV = 64
CONV_WIDTH = 4
DN_CHUNK = 64
ATT_WIDTH = ATT_HEADS * HEAD_DIM
KV_WIDTH = ATT_KV_HEADS * HEAD_DIM
DN_WIDTH = DN_HEADS * DN_DV
CONV_CH = 3 * DN_WIDTH
N_GROUPS = 4
EXPERTS_PER_GROUP = 8
N_EXPERTS = N_GROUPS * EXPERTS_PER_GROUP
D_EXPERT = 256
PLE_DIM = 256
EPS = 1e-6
NEG_INF = float("-inf")

ATT_COLS = ATT_WIDTH + 2 * KV_WIDTH
LANES = 128
ROUTER_OFF = N_GROUPS
VMEM_LIMIT = 48 * 1024 * 1024
ROW_TM = 512
WIDE_TM = 1024


def _params(*sem):
    return pltpu.CompilerParams(dimension_semantics=sem, vmem_limit_bytes=VMEM_LIMIT)


def _mm(a, b):
    return jnp.dot(a.astype(BF16), b.astype(BF16), preferred_element_type=F32)


def _mm_nt(a, b):
    return lax.dot_general(a.astype(BF16), b.astype(BF16), (((1,), (1,)), ((), ())),
                           preferred_element_type=F32)


def _mm_tn(a, b):
    return lax.dot_general(a.astype(BF16), b.astype(BF16), (((0,), (0,)), ((), ())),
                           preferred_element_type=F32)


def _split3(x):
    h1 = x.astype(BF16)
    r1 = x - h1.astype(F32)
    h2 = r1.astype(BF16)
    h3 = (r1 - h2.astype(F32)).astype(BF16)
    return h1, h2, h3


def _mm_sel_rhs(x, sel):
    h1, h2, h3 = _split3(x)
    d = lambda h: jnp.dot(h, sel, preferred_element_type=F32)
    return d(h1) + d(h2) + d(h3)


def _mm_sel_lhs(sel, x):
    h1, h2, h3 = _split3(x)
    d = lambda h: jnp.dot(sel, h, preferred_element_type=F32)
    return d(h1) + d(h2) + d(h3)


def _sigmoid(x):
    return 1.0 / (1.0 + jnp.exp(-x))


def _silu(x):
    return x * _sigmoid(x)


def _silu_tanh(x):
    return x * (0.5 * jnp.tanh(0.5 * x) + 0.5)


def _softplus(x):
    return jnp.maximum(x, 0.0) + jnp.log1p(jnp.exp(-jnp.abs(x)))


def _rmsnorm(x, g):
    return x * lax.rsqrt(jnp.mean(x * x, axis=-1, keepdims=True) + EPS) * g


def _t5_bucket_np(dist):
    max_exact = N_BUCKETS // 2
    d = np.maximum(dist, 0)
    ratio = (np.log(np.maximum(d, 1).astype(np.float32) / np.float32(max_exact))
             / np.float32(math.log(WINDOW / max_exact))).astype(np.float32)
    large = np.minimum(max_exact + (ratio * np.float32(N_BUCKETS - max_exact)).astype(np.int32),
                       N_BUCKETS - 1)
    return np.where(d < max_exact, d, large).astype(np.int32)


def _bias_lookup(bucket, rb_ref, h):
    acc = jnp.zeros(bucket.shape, F32)
    for t in range(N_BUCKETS):
        acc = jnp.where(bucket == t, rb_ref[t, h], acc)
    return acc


def _inproj_kernel(x_ref, g_ref, wa_ref, wz_ref, wb_ref, att_ref, xc_ref, dz_ref, ba_ref):
    xn = _rmsnorm(x_ref[...], g_ref[...]).astype(BF16)
    att_ref[...] = jnp.dot(xn, wa_ref[:, :ATT_COLS], preferred_element_type=F32)
    xc_ref[...] = jnp.dot(xn, wa_ref[:, ATT_COLS:], preferred_element_type=F32)
    dz_ref[...] = jnp.dot(xn, wz_ref[...], preferred_element_type=F32)
    ba_ref[...] = jnp.dot(xn, wb_ref[...], preferred_element_type=F32)


def _inproj(x, g, w):
    t = x.shape[0]
    tm = min(t, ROW_TM)
    row = lambda n: pl.BlockSpec((tm, n), lambda i: (i, 0))
    full = lambda a: pl.BlockSpec(a.shape, lambda i: (0,) * a.ndim)
    return pl.pallas_call(
        _inproj_kernel,
        grid=(t // tm,),
        in_specs=[row(D_MODEL), full(g)] + [full(a) for a in w],
        out_specs=[row(ATT_COLS), row(CONV_CH), row(DN_WIDTH), row(LANES)],
        out_shape=[jax.ShapeDtypeStruct((t, n), F32) for n in (ATT_COLS, CONV_CH, DN_WIDTH, LANES)],
        compiler_params=_params("parallel"),
        name="inproj",
    )(x, g, *w)


TAIL = 8
PAIR = 2 * DN_DK
N_PAIRS = DN_WIDTH // PAIR


def _head_sums(z, pair_ones):
    hi = z.astype(BF16)
    lw = (z - hi.astype(F32)).astype(BF16)
    d = lambda a, p: jnp.dot(a[:, p * PAIR:(p + 1) * PAIR], pair_ones, preferred_element_type=F32)
    return jnp.concatenate([d(hi, p) + d(lw, p) for p in range(N_PAIRS)], axis=1)


def _inproj_conv_kernel(x_ref, g_ref, wa_ref, wz_ref, wb_ref, cw_ref, ones_ref,
                        att_ref, qkv_ref, dz_ref, ba_ref, tail_ref, xp_scr, *, tiles_per_seq):
    tm = x_ref.shape[0]

    @pl.when(pl.program_id(0) % tiles_per_seq == 0)
    def _():
        xp_scr[...] = jnp.zeros((TAIL, CONV_CH), F32)

    xn = _rmsnorm(x_ref[...], g_ref[...]).astype(BF16)
    xc = jnp.dot(xn, wa_ref[:, ATT_COLS:], preferred_element_type=F32)
    att_ref[...] = jnp.dot(xn, wa_ref[:, :ATT_COLS], preferred_element_type=F32)
    dz_ref[...] = jnp.dot(xn, wz_ref[...], preferred_element_type=F32)
    ba_ref[...] = jnp.dot(xn, wb_ref[...], preferred_element_type=F32)

    head = jnp.concatenate([xp_scr[...], xc[:TAIL, :]], axis=0)

    def shifted(j):
        return jnp.concatenate([head[TAIL - j:2 * TAIL - j, :], pltpu.roll(xc, j, axis=0)[TAIL:, :]], axis=0)

    y = shifted(3) * cw_ref[0:1, :]
    y = y + shifted(2) * cw_ref[1:2, :]
    y = y + shifted(1) * cw_ref[2:3, :]
    y = y + xc * cw_ref[3:4, :]
    tail = xc[tm - TAIL:, :]
    xp_scr[...] = tail
    tail_ref[0] = tail
    y = _silu_tanh(y)
    q = y[:, :DN_WIDTH]
    k = y[:, DN_WIDTH:2 * DN_WIDTH]
    inv_norm = lax.rsqrt(_head_sums(jnp.concatenate([q * q, k * k], axis=0), ones_ref[...]) + EPS)
    qkv_ref[:, :DN_WIDTH] = q * inv_norm[:tm] * (DN_DK ** -0.5)
    qkv_ref[:, DN_WIDTH:2 * DN_WIDTH] = k * inv_norm[tm:]
    qkv_ref[:, 2 * DN_WIDTH:] = y[:, 2 * DN_WIDTH:]


def _pair_ones():
    lane = np.arange(PAIR)
    return jnp.asarray((lane[:, None] // DN_DV == lane[None, :] // DN_DV).astype(np.float32), dtype=BF16)


def _inproj_conv(x, g, w, conv_w, seq):
    t = x.shape[0]
    tm = ROW_TM
    assert seq % tm == 0
    ones = _pair_ones()
    row = lambda n: pl.BlockSpec((tm, n), lambda i: (i, 0))
    full = lambda a: pl.BlockSpec(a.shape, lambda i: (0,) * a.ndim)
    return pl.pallas_call(
        functools.partial(_inproj_conv_kernel, tiles_per_seq=seq // tm),
        grid=(t // tm,),
        in_specs=[row(D_MODEL), full(g)] + [full(a) for a in w] + [full(conv_w), full(ones)],
        out_specs=[row(ATT_COLS), row(CONV_CH), row(DN_WIDTH), row(LANES),
                   pl.BlockSpec((1, TAIL, CONV_CH), lambda i: (i, 0, 0))],
        out_shape=[jax.ShapeDtypeStruct((t, n), F32) for n in (ATT_COLS, CONV_CH, DN_WIDTH, LANES)]
                  + [jax.ShapeDtypeStruct((t // tm, TAIL, CONV_CH), F32)],
        scratch_shapes=[pltpu.VMEM((TAIL, CONV_CH), F32)],
        compiler_params=_params("arbitrary"),
        name="inproj_conv",
    )(x, g, *w, conv_w, ones)


GROUP_ROWS = GQA * ATT_BLOCK


def _attn_prompt_kernel(cur_ref, prev_ref, bucket_ref, rb_ref, sink_ref, o_ref, bias_scr, sink_scr):
    i = pl.program_id(0)
    nseq = cur_ref.shape[0]

    @pl.when(i == 0)
    def _():
        qi = lax.broadcasted_iota(jnp.int32, (ATT_BLOCK, 2 * ATT_BLOCK), 0)
        kj = lax.broadcasted_iota(jnp.int32, (ATT_BLOCK, 2 * ATT_BLOCK), 1)
        dist = qi + ATT_BLOCK - kj
        band = jnp.logical_and(dist >= 0, dist < WINDOW)
        bucket = bucket_ref[...]
        hrow = lax.broadcasted_iota(jnp.int32, (GROUP_ROWS, 1), 0) // ATT_BLOCK
        for g in range(ATT_KV_HEADS):
            sink_col = jnp.zeros((GROUP_ROWS, 1), F32)
            for hh in range(GQA):
                h = g * GQA + hh
                bias = jnp.where(band, _bias_lookup(bucket, rb_ref, h), NEG_INF)
                bias_scr[0, g, hh * ATT_BLOCK:(hh + 1) * ATT_BLOCK, :] = bias
                bias_scr[1, g, hh * ATT_BLOCK:(hh + 1) * ATT_BLOCK, :] = jnp.where(kj >= ATT_BLOCK, bias, NEG_INF)
                sink_col = jnp.where(hrow == hh, sink_ref[h], sink_col)
            sink_scr[g] = sink_col

    first = (i == 0).astype(jnp.int32)
    probs = [(b, g) for b in range(nseq) for g in range(ATT_KV_HEADS)]
    scores = []
    for b, g in probs:
        cur = cur_ref[b]
        prev = prev_ref[b]
        q = jnp.concatenate([cur[:, (g * GQA + hh) * HEAD_DIM:(g * GQA + hh + 1) * HEAD_DIM]
                             for hh in range(GQA)], axis=0) * (HEAD_DIM ** -0.5)
        kcol = slice(ATT_WIDTH + g * HEAD_DIM, ATT_WIDTH + (g + 1) * HEAD_DIM)
        k2 = jnp.concatenate([prev[:, kcol], cur[:, kcol]], axis=0)
        scores.append(_mm_nt(q, k2) + bias_scr[first, g])
    probs_p, dens = [], []
    for (b, g), s in zip(probs, scores):
        sink = sink_scr[g]
        m = jnp.maximum(jnp.max(s, axis=-1, keepdims=True), sink)
        p = jnp.exp(s - m)
        dens.append(jnp.sum(p, axis=-1, keepdims=True) + jnp.exp(sink - m))
        probs_p.append(p.astype(BF16))
    outs = {}
    for (b, g), p, den in zip(probs, probs_p, dens):
        vcol = slice(ATT_WIDTH + KV_WIDTH + g * HEAD_DIM, ATT_WIDTH + KV_WIDTH + (g + 1) * HEAD_DIM)
        v2 = jnp.concatenate([prev_ref[b][:, vcol], cur_ref[b][:, vcol]], axis=0)
        outs[b, g] = _mm(p, v2) / den
    for b in range(nseq):
        o_ref[b] = jnp.concatenate([outs[b, g][hh * ATT_BLOCK:(hh + 1) * ATT_BLOCK, :]
                                    for g in range(ATT_KV_HEADS) for hh in range(GQA)],
                                   axis=1).astype(o_ref.dtype)


def _attn_prompt(att, bucket, rel_bias, sink, batch, seq):
    nb = seq // ATT_BLOCK
    smem = pl.BlockSpec(memory_space=pltpu.SMEM)
    att3 = att.reshape(batch, seq, ATT_COLS)
    out = pl.pallas_call(
        _attn_prompt_kernel,
        grid=(nb,),
        in_specs=[
            pl.BlockSpec((batch, ATT_BLOCK, ATT_COLS), lambda i: (0, i, 0)),
            pl.BlockSpec((batch, ATT_BLOCK, ATT_COLS), lambda i: (0, jnp.maximum(i - 1, 0), 0)),
            pl.BlockSpec(bucket.shape, lambda i: (0, 0)),
            smem, smem,
        ],
        out_specs=pl.BlockSpec((batch, ATT_BLOCK, ATT_WIDTH), lambda i: (0, i, 0)),
        out_shape=jax.ShapeDtypeStruct((batch, seq, ATT_WIDTH), BF16),
        scratch_shapes=[pltpu.VMEM((2, ATT_KV_HEADS, GROUP_ROWS, 2 * ATT_BLOCK), F32),
                        pltpu.VMEM((ATT_KV_HEADS, GROUP_ROWS, 1), F32)],
        compiler_params=_params("arbitrary"),
        name="attn_prompt",
    )(att3, att3, bucket, rel_bias, sink)
    return out.reshape(batch * seq, ATT_WIDTH)


ATT_S_BB = 8


def _attn_sample_kernel(att_ref, ck_ref, cv_ref, bucket_ref, rb_ref, sink_ref, o_ref, ks_ref, vs_ref,
                        bias_scr, col_scr):
    hrow = lax.broadcasted_iota(jnp.int32, (ATT_HEADS, LANES), 0)
    lane = lax.broadcasted_iota(jnp.int32, (ATT_HEADS, LANES), 1)

    last = (lax.broadcasted_iota(jnp.int32, (3, WINDOW), 1) == WINDOW - 1).astype(BF16)
    is_last = lax.broadcasted_iota(jnp.int32, (KV_WIDTH, WINDOW), 1) == WINDOW - 1

    def shifted(cache_t, new_row):
        pieces = jnp.concatenate([p.astype(F32) for p in _split3(new_row)], axis=0).astype(BF16)
        col = lax.dot_general(pieces, last, (((0,), (0,)), ((), ())), preferred_element_type=F32)
        out = jnp.where(is_last, col, pltpu.roll(cache_t, WINDOW - 1, axis=1))
        return out.reshape(ATT_KV_HEADS, HEAD_DIM, WINDOW)

    for b in range(ATT_S_BB):
        row = att_ref[b:b + 1, :]
        ks_ref[b] = shifted(ck_ref[b].reshape(KV_WIDTH, WINDOW), row[:, ATT_WIDTH:ATT_WIDTH + KV_WIDTH])
        vs_ref[b] = shifted(cv_ref[b].reshape(KV_WIDTH, WINDOW), row[:, ATT_WIDTH + KV_WIDTH:])

    @pl.when(pl.program_id(0) == 0)
    def _():
        bucket = jnp.broadcast_to(bucket_ref[...], (ATT_HEADS, LANES))
        bias = jnp.zeros((ATT_HEADS, LANES), F32)
        cols = jnp.zeros((ATT_HEADS, LANES), F32)
        for h in range(ATT_HEADS):
            bias = jnp.where(hrow == h, _bias_lookup(bucket, rb_ref, h), bias)
            cols = jnp.where(jnp.logical_and(hrow == h, lane == 0), sink_ref[h], cols)
            cols = jnp.where(jnp.logical_and(hrow == h, lane == 1), rb_ref[0, h], cols)
        bias_scr[...] = jnp.where(lane >= 1, bias, NEG_INF)
        col_scr[...] = cols

    bias_c = bias_scr[...]
    sink = col_scr[:, 0:1]
    bias_n = col_scr[:, 1:2]
    same_group = (hrow // GQA) == (lane // HEAD_DIM)
    low_group = lax.broadcasted_iota(jnp.int32, (ATT_HEADS, HEAD_DIM), 0) < GQA
    rnd = lambda a: a.astype(BF16).astype(F32)
    seqs = range(ATT_S_BB)
    rows = [att_ref[b:b + 1, :] for b in seqs]
    q_bds = []
    for row in rows:
        q = row[:, :ATT_WIDTH] * (HEAD_DIM ** -0.5)
        qh = jnp.concatenate([q[:, h * HEAD_DIM:(h + 1) * HEAD_DIM] for h in range(ATT_HEADS)], axis=0)
        q_bds.append(jnp.where(same_group, jnp.concatenate([qh, qh], axis=1), 0.0))
    kv_t = lambda ref, b: ref[b].reshape(KV_WIDTH, WINDOW)
    s_cs = [_mm(q_bd, kv_t(ck_ref, b)) + bias_c for b, q_bd in zip(seqs, q_bds)]
    prs, pns = [], []
    for row, q_bd, s_c in zip(rows, q_bds, s_cs):
        kn = row[:, ATT_WIDTH:ATT_WIDTH + KV_WIDTH]
        s_n = jnp.sum(rnd(q_bd) * rnd(kn), axis=-1, keepdims=True) + bias_n
        m = jnp.maximum(jnp.maximum(jnp.max(s_c, axis=-1, keepdims=True), s_n), sink)
        p_c = jnp.exp(s_c - m)
        p_n = jnp.exp(s_n - m)
        den = jnp.sum(p_c, axis=-1, keepdims=True) + p_n + jnp.exp(sink - m)
        prs.append(p_c / den)
        pns.append(p_n / den)
    pvs = [_mm_nt(pr, kv_t(cv_ref, b)) for b, pr in zip(seqs, prs)]
    for b, row, pv, pn in zip(seqs, rows, pvs, pns):
        vn = row[:, ATT_WIDTH + KV_WIDTH:]
        o_full = pv + rnd(pn) * rnd(vn)
        o_sel = jnp.where(low_group, o_full[:, :HEAD_DIM], o_full[:, HEAD_DIM:])
        o_ref[b:b + 1, :] = jnp.concatenate([o_sel[h:h + 1, :] for h in range(ATT_HEADS)], axis=1)


def _attn_sample(att, ck, cv, bucket, rel_bias, sink):
    nseq = att.shape[0]
    smem = pl.BlockSpec(memory_space=pltpu.SMEM)
    cache = pl.BlockSpec((ATT_S_BB, ATT_KV_HEADS, HEAD_DIM, WINDOW), lambda i: (i, 0, 0, 0))
    return pl.pallas_call(
        _attn_sample_kernel,
        grid=(nseq // ATT_S_BB,),
        in_specs=[pl.BlockSpec((ATT_S_BB, ATT_COLS), lambda i: (i, 0)), cache, cache,
                  pl.BlockSpec(bucket.shape, lambda i: (0, 0)), smem, smem],
        out_specs=[pl.BlockSpec((ATT_S_BB, ATT_WIDTH), lambda i: (i, 0)), cache, cache],
        out_shape=[jax.ShapeDtypeStruct((nseq, ATT_WIDTH), F32),
                   jax.ShapeDtypeStruct(ck.shape, F32), jax.ShapeDtypeStruct(cv.shape, F32)],
        scratch_shapes=[pltpu.VMEM((ATT_HEADS, LANES), F32), pltpu.VMEM((ATT_HEADS, LANES), F32)],
        compiler_params=_params("arbitrary"),
        name="attn_sample",
    )(att, ck, cv, bucket, rel_bias, sink)


GDN_TB = 128
GDN_NC = GDN_TB // DN_CHUNK


def _gdn_gates(ba, alog, dtb):
    beta = _sigmoid(ba)
    g = -jnp.exp(alog) * _softplus(ba + dtb)
    return beta, g


def _pair_diag(x, lo):
    xb = x.astype(BF16)
    zero = jnp.zeros_like(xb)
    return jnp.concatenate([jnp.where(lo, xb, zero), jnp.where(lo, zero, xb)], axis=0)


def _gdn_prompt_kernel(qkv_ref, dz_ref, ba_ref, alog_ref, dtb_ref, dnx_ref,
                       hsum_ref, expb_ref, expg_ref, ltri_ref,
                       o_ref, s_out_ref, s_scr):
    i = pl.program_id(0)
    nb = qkv_ref.shape[0]

    @pl.when(i == 0)
    def _():
        s_scr[...] = jnp.zeros(s_scr.shape, F32)

    hsum = hsum_ref[...]
    ri = lax.broadcasted_iota(jnp.int32, (DN_CHUNK, PAIR), 0)
    ci = lax.broadcasted_iota(jnp.int32, (DN_CHUNK, PAIR), 1)
    lo = ci < DN_DK
    cj = jnp.where(lo, ci, ci - DN_DK)
    causal = ri >= cj
    strict = ri > cj
    eye = (ri == cj).astype(F32)

    def sel2(x, m):
        hi = x.astype(BF16)
        lw = (x - hi.astype(F32)).astype(BF16)
        return (jnp.dot(hi, m, preferred_element_type=F32) + jnp.dot(lw, m, preferred_element_type=F32))

    pre = []
    for b in range(nb):
        q = qkv_ref[b, :, :DN_WIDTH]
        k = qkv_ref[b, :, DN_WIDTH:2 * DN_WIDTH]
        v = qkv_ref[b, :, 2 * DN_WIDTH:]
        beta_c, g_c = _gdn_gates(ba_ref[b], alog_ref[...], dtb_ref[...])
        beta = sel2(beta_c, expb_ref[...])
        gam_c = _mm_sel_lhs(ltri_ref[...], g_c)
        gam = _mm_sel_rhs(gam_c, expg_ref[...])
        gam_t = gam_c.T
        kb = k * beta
        egam = jnp.exp(gam)
        pre.append(dict(q=q, k=k, kb=kb, vb=v * beta, qg=q * egam, wr=kb * egam, gam=gam, gam_t=gam_t))

    probs = [(c, b, p) for c in range(GDN_NC) for b in range(nb) for p in range(N_PAIRS)]
    pick = lambda m: jnp.where(lo, m[:DN_DK], m[DN_DK:])
    rows_of = lambda c: slice(c * DN_CHUNK, (c + 1) * DN_CHUNK)
    sl = lambda name, c, b, p: pre[b][name][rows_of(c), p * PAIR:(p + 1) * PAIR]
    raws = []
    for c, b, p in probs:
        k_p = sl("k", c, b, p)
        k_rows = jnp.concatenate([jnp.where(lo, k_p, 0.0), jnp.where(lo, 0.0, k_p)], axis=0)
        raws.append(_mm_nt(jnp.concatenate([sl("kb", c, b, p), sl("q", c, b, p)], axis=0), k_rows))
    pws, ts, qks = [], [], []
    for (c, b, p), raw in zip(probs, raws):
        gcol = sl("gam", c, b, p)
        h0 = DN_HEADS + 2 * p
        gam_t = pre[b]["gam_t"]
        grow = jnp.concatenate([gam_t[h0:h0 + 1, rows_of(c)], gam_t[h0 + 1:h0 + 2, rows_of(c)]], axis=1)
        decay = jnp.exp(jnp.where(causal, gcol - grow, NEG_INF))
        a = jnp.where(strict, raw[:DN_CHUNK] * decay, 0.0)
        qks.append(jnp.where(causal, raw[DN_CHUNK:] * decay, 0.0))
        pws.append(-a)
        ts.append(eye - a)
    pws = [_mm(pw, _pair_diag(pw, lo)) for pw in pws]
    for _ in range(4):
        rs = [_mm(jnp.concatenate([pw, t], axis=0), _pair_diag(pw, lo)) for pw, t in zip(pws, ts)]
        pws = [r[:DN_CHUNK] for r in rs]
        ts = [t + r[DN_CHUNK:] for t, r in zip(ts, rs)]
    rs = [_mm(t, _pair_diag(pw, lo)) for pw, t in zip(pws, ts)]
    ts = [t + r for t, r in zip(ts, rs)]
    sols = [_mm(t, jnp.concatenate([_pair_diag(sl("vb", c, b, p), lo), _pair_diag(sl("wr", c, b, p), lo)],
                                   axis=1)) for (c, b, p), t in zip(probs, ts)]
    qkuws = [_mm(qk, jnp.concatenate([_pair_diag(s[:, :PAIR], lo), _pair_diag(s[:, PAIR:], lo)], axis=1))
             for qk, s in zip(qks, sols)]
    crosses, gls = [], []
    for (c, b, p), s in zip(probs, sols):
        last = (c + 1) * DN_CHUNK - 1
        gam_last = pre[b]["gam"][last:last + 1, p * PAIR:(p + 1) * PAIR]
        kd = sl("k", c, b, p) * jnp.exp(gam_last - sl("gam", c, b, p))
        crosses.append(_mm_tn(kd, s))
        gls.append(jnp.exp(gam_last))
    lhs = [jnp.concatenate([pick(cr[:, PAIR:]), sl("qg", c, b, p) - qkuw[:, PAIR:]], axis=0)
           for (c, b, p), cr, qkuw in zip(probs, crosses, qkuws)]

    o_rows = [[] for _ in range(nb)]
    per_chunk = nb * N_PAIRS
    for c in range(GDN_NC):
        sel = slice(c * per_chunk, (c + 1) * per_chunk)
        s_olds = [s_scr[b, p] for _, b, p in probs[sel]]
        rs = [_mm(l, _pair_diag(s_old, lo)) for l, s_old in zip(lhs[sel], s_olds)]
        o_pairs = [[] for _ in range(nb)]
        for (_, b, p), r, s_old, gl, cr, qkuw in zip(probs[sel], rs, s_olds, gls[sel], crosses[sel], qkuws[sel]):
            s_scr[b, p] = gl * s_old - r[:DN_DK] + pick(cr[:, :PAIR])
            o_pairs[b].append(r[DN_DK:] + qkuw[:, :PAIR])
        for b in range(nb):
            o_rows[b].append(jnp.concatenate(o_pairs[b], axis=1))

    o_all = jnp.concatenate([jnp.concatenate(rows, axis=0) for rows in o_rows], axis=0)
    inv_rms = lax.rsqrt(_head_sums(o_all * o_all, hsum) * (1.0 / DN_DV) + EPS)
    for b in range(nb):
        rows = slice(b * GDN_TB, (b + 1) * GDN_TB)
        o_ref[b] = (o_all[rows] * inv_rms[rows] * dnx_ref[...] * _silu_tanh(dz_ref[b])).astype(o_ref.dtype)

    @pl.when(i == pl.num_programs(0) - 1)
    def _():
        for b in range(nb):
            for p in range(N_PAIRS):
                s_p = s_scr[b, p]
                s_out_ref[b, 2 * p] = s_p[:, :DN_DV]
                s_out_ref[b, 2 * p + 1] = s_p[:, DN_DV:]


def _gdn_consts():
    lane = np.arange(DN_WIDTH)
    pl_lane = np.arange(PAIR)
    hsum = (pl_lane[:, None] // DN_DV == pl_lane[None, :] // DN_DV)
    src = np.arange(LANES)
    expb = (src[:, None] == lane[None, :] // DN_DV)
    expg = (src[:, None] == DN_HEADS + lane[None, :] // DN_DV)
    tok = np.arange(GDN_TB)
    ltri = np.logical_and(tok[:, None] >= tok[None, :],
                          tok[:, None] // DN_CHUNK == tok[None, :] // DN_CHUNK)
    as_bf16 = lambda m: jnp.asarray(m.astype(np.float32), dtype=BF16)
    return as_bf16(hsum), as_bf16(expb), as_bf16(expg), as_bf16(ltri)


def _gdn_prompt(xc, dz, ba, alog, dtb, dnx, batch, seq):
    nt = seq // GDN_TB
    hsum, expb, expg, ltri = _gdn_consts()
    row = lambda n: pl.BlockSpec((batch, GDN_TB, n), lambda i: (0, i, 0))
    full = lambda a: pl.BlockSpec(a.shape, lambda i: (0,) * a.ndim)
    consts = (alog, dtb, dnx, hsum, expb, expg, ltri)
    as3d = lambda a: a.reshape(batch, seq, a.shape[-1])
    o, s = pl.pallas_call(
        _gdn_prompt_kernel,
        grid=(nt,),
        in_specs=[row(CONV_CH), row(DN_WIDTH), row(LANES)] + [full(a) for a in consts],
        out_specs=[row(DN_WIDTH),
                   pl.BlockSpec((batch, DN_HEADS, DN_DK, DN_DV), lambda i: (0, 0, 0, 0))],
        out_shape=[jax.ShapeDtypeStruct((batch, seq, DN_WIDTH), BF16),
                   jax.ShapeDtypeStruct((batch, DN_HEADS, DN_DK, DN_DV), F32)],
        scratch_shapes=[pltpu.VMEM((batch, N_PAIRS, DN_DK, PAIR), F32)],
        compiler_params=_params("arbitrary"),
        name="gdn_prompt",
    )(as3d(xc), as3d(dz), as3d(ba), *consts)
    return o.reshape(batch * seq, DN_WIDTH), s


def _gdn_sample_front_kernel(xc_ref, dz_ref, ba_ref, sc_ref, cw_ref, alog_ref, dtb_ref, hsum_ref,
                             q_ref, k_ref, v_ref, dz_t_ref, gates_ref):
    xc = xc_ref[...]
    y = sc_ref[0] * cw_ref[0:1, :]
    y = y + sc_ref[1] * cw_ref[1:2, :]
    y = y + sc_ref[2] * cw_ref[2:3, :]
    y = _silu(y + xc * cw_ref[3:4, :])
    hsum = hsum_ref[...]
    q = y[:, :DN_WIDTH]
    k = y[:, DN_WIDTH:2 * DN_WIDTH]
    q = q * lax.rsqrt(_mm_sel_rhs(q * q, hsum) + EPS) * (DN_DK ** -0.5)
    k = k * lax.rsqrt(_mm_sel_rhs(k * k, hsum) + EPS)
    beta_c, g_c = _gdn_gates(ba_ref[...], alog_ref[...], dtb_ref[...])
    q_ref[...] = q.T
    k_ref[...] = k.T
    v_ref[...] = y[:, 2 * DN_WIDTH:].T
    dz_t_ref[...] = dz_ref[...].T
    gates_ref[0:LANES, :] = beta_c.T
    gates_ref[LANES:, :] = jnp.exp(g_c).T


def _gdn_sample_step_kernel(q_ref, k_ref, v_ref, dz_ref, gates_ref, dn_ref, s_ref, o_ref, s_out_ref):
    h = pl.program_id(0)
    beta = gates_ref[pl.ds(h, 1), :]
    eg = gates_ref[pl.ds(LANES + DN_HEADS + h, 1), :]
    q, k, v = q_ref[...], k_ref[...], v_ref[...]
    w = (k * beta) * eg
    qg = q * eg
    ws = jnp.zeros(v.shape, F32)
    qs = jnp.zeros(v.shape, F32)
    for dk in range(DN_DK):
        s_dk = s_ref[0, dk]
        ws = ws + w[dk:dk + 1, :] * s_dk
        qs = qs + qg[dk:dk + 1, :] * s_dk
    v_new = v * beta - ws
    qk = jnp.sum(q * k, axis=0, keepdims=True)
    o = qs + qk * v_new
    for dk in range(DN_DK):
        s_out_ref[0, dk] = s_ref[0, dk] * eg + k[dk:dk + 1, :] * v_new
    o = o * lax.rsqrt(jnp.mean(o * o, axis=0, keepdims=True) + EPS) * dn_ref[...]
    o_ref[...] = o * _silu(dz_ref[...])


def _gdn_sample_lanes(xc, dz, ba, sconv_t, state_t, conv_w, alog, dtb, dn):
    nseq = xc.shape[0]
    assert nseq == LANES
    lane = np.arange(DN_WIDTH)
    hsum = jnp.asarray((lane[:, None] // DN_DV == lane[None, :] // DN_DV).astype(np.float32), dtype=BF16)
    full = lambda a: pl.BlockSpec(a.shape, lambda i: (0,) * a.ndim)
    cm = jax.ShapeDtypeStruct((DN_WIDTH, nseq), F32)
    front_in = (xc, dz, ba, sconv_t, conv_w, alog, dtb, hsum)
    q_t, k_t, v_t, dz_t, gates_t = pl.pallas_call(
        _gdn_sample_front_kernel,
        grid=(1,),
        in_specs=[full(a) for a in front_in],
        out_specs=[pl.BlockSpec((DN_WIDTH, nseq), lambda i: (0, 0))] * 4
                  + [pl.BlockSpec((2 * LANES, nseq), lambda i: (0, 0))],
        out_shape=[cm, cm, cm, cm, jax.ShapeDtypeStruct((2 * LANES, nseq), F32)],
        compiler_params=_params("arbitrary"),
        name="gdn_sample_front",
    )(*front_in)
    dn_b = jnp.broadcast_to(dn.reshape(DN_DV, 1), (DN_DV, nseq))
    head = pl.BlockSpec((DN_DK, nseq), lambda h: (h, 0))
    st = pl.BlockSpec((1, DN_DK, DN_DV, nseq), lambda h: (h, 0, 0, 0))
    return pl.pallas_call(
        _gdn_sample_step_kernel,
        grid=(DN_HEADS,),
        in_specs=[head, head, head, head, full(gates_t), full(dn_b), st],
        out_specs=[head, st],
        out_shape=[cm, jax.ShapeDtypeStruct(state_t.shape, F32)],
        compiler_params=_params("parallel"),
        name="gdn_sample_step",
    )(q_t, k_t, v_t, dz_t, gates_t, dn_b, state_t)


def _route(xn, wr):
    logits = jnp.dot(xn, wr, preferred_element_type=F32)
    lane = lax.broadcasted_iota(jnp.int32, logits.shape, 1).astype(F32)
    first_at = lambda hit: jnp.min(jnp.where(hit, lane, float(LANES)), axis=-1, keepdims=True)
    glog = jnp.where(lane < N_GROUPS, logits, NEG_INF)
    gmax = jnp.max(glog, axis=-1, keepdims=True)
    gsel = first_at(glog == gmax)
    pgsel = 1.0 / jnp.sum(jnp.exp(glog - gmax), axis=-1, keepdims=True)
    lo = ROUTER_OFF + gsel * EXPERTS_PER_GROUP
    in_group = jnp.logical_and(lane >= lo, lane < lo + EXPERTS_PER_GROUP)
    elog = jnp.where(in_group, logits, NEG_INF)
    m1 = jnp.max(elog, axis=-1, keepdims=True)
    i1 = first_at(elog == m1)
    z = jnp.sum(jnp.exp(elog - m1), axis=-1, keepdims=True)
    elog2 = jnp.where(lane == i1, NEG_INF, elog)
    m2 = jnp.max(elog2, axis=-1, keepdims=True)
    i2 = first_at(elog2 == m2)
    p1 = 1.0 / z
    p2 = jnp.exp(m2 - m1) / z
    tot = p1 + p2
    return lane, i1, i2, p1 / tot * pgsel, p2 / tot * pgsel


def _outproj(x_ref, oa_ref, od_ref, wo_ref):
    return x_ref[...] + _mm(oa_ref[...], wo_ref[:ATT_WIDTH, :]) + _mm(od_ref[...], wo_ref[ATT_WIDTH:, :])


def _outproj_router_kernel(x_ref, oa_ref, od_t_ref, wo_ref, g_ref, wr_ref, h_ref, xn_ref, gate_ref):
    h = (x_ref[...] + _mm(oa_ref[...], wo_ref[:ATT_WIDTH, :])
         + _mm(od_t_ref[...].T, wo_ref[ATT_WIDTH:, :]))
    h_ref[...] = h
    xn = _rmsnorm(h, g_ref[...]).astype(BF16)
    xn_ref[...] = xn
    lane, i1, i2, g1, g2 = _route(xn, wr_ref[...])
    gate_ref[...] = jnp.where(lane == i1, g1, 0.0) + jnp.where(lane == i2, g2, 0.0)


def _outproj_router(x, oa, od_t, wo, g, wr):
    t = x.shape[0]
    tm = t
    row = lambda n: pl.BlockSpec((tm, n), lambda i: (i, 0))
    full = lambda a: pl.BlockSpec(a.shape, lambda i: (0,) * a.ndim)
    return pl.pallas_call(
        _outproj_router_kernel,
        grid=(t // tm,),
        in_specs=[row(D_MODEL), row(ATT_WIDTH), full(od_t), full(wo), full(g), full(wr)],
        out_specs=[row(D_MODEL), row(D_MODEL), row(LANES)],
        out_shape=[jax.ShapeDtypeStruct((t, D_MODEL), F32), jax.ShapeDtypeStruct((t, D_MODEL), BF16),
                   jax.ShapeDtypeStruct((t, LANES), F32)],
        compiler_params=_params("parallel"),
        name="outproj_router",
    )(x, oa, od_t, wo, g, wr)


MOE_TM = 512
POS_TM = 2048
INFO_G1, INFO_G2, INFO_E1, INFO_E2 = 0, 1, 2, 3


def _moe_tiles(t):
    return (2 * t) // MOE_TM + N_EXPERTS


HALF = D_MODEL // 2
U32 = jnp.uint32


def _pack_rows(x):
    bits = lambda v: lax.bitcast_convert_type(v.astype(BF16).astype(F32), U32)
    return bits(x[:, HALF:]) | (bits(x[:, :HALF]) >> 16)


def _unpack_rows(w):
    lo = lax.bitcast_convert_type(w << 16, F32)
    hi = lax.bitcast_convert_type(w & jnp.uint32(0xFFFF0000), F32)
    return lo, hi


def _route_kernel(x_ref, oa_ref, od_ref, wo_ref, g_ref, wr_ref, h_ref, xn_ref, info_ref, cnt_ref, run_scr):
    h = _outproj(x_ref, oa_ref, od_ref, wo_ref)
    h_ref[...] = h
    xn = _rmsnorm(h, g_ref[...])
    xn_ref[...] = _pack_rows(xn)
    lane, i1, i2, g1, g2 = _route(xn.astype(BF16), wr_ref[...])
    info = jnp.where(lane == INFO_G1, g1, 0.0) + jnp.where(lane == INFO_G2, g2, 0.0)
    info = info + jnp.where(lane == INFO_E1, i1, 0.0) + jnp.where(lane == INFO_E2, i2, 0.0)
    info_ref[...] = info

    @pl.when(pl.program_id(0) == 0)
    def _():
        run_scr[...] = jnp.zeros(run_scr.shape, F32)
    picked = jnp.logical_or(lane == i1, lane == i2).astype(F32)
    run_scr[...] += jnp.sum(picked, axis=0, keepdims=True)
    cnt_ref[...] = run_scr[...]


def _route_sparse(x, oa, od, wo, g, wr):
    t = x.shape[0]
    tm = WIDE_TM
    row = lambda n: pl.BlockSpec((tm, n), lambda i: (i, 0))
    full = lambda a: pl.BlockSpec(a.shape, lambda i: (0,) * a.ndim)
    return pl.pallas_call(
        _route_kernel,
        grid=(t // tm,),
        in_specs=[row(D_MODEL), row(ATT_WIDTH), row(DN_WIDTH), full(wo), full(g), full(wr)],
        out_specs=[row(D_MODEL), row(HALF), row(LANES), pl.BlockSpec((1, LANES), lambda i: (0, 0))],
        out_shape=[jax.ShapeDtypeStruct((t, D_MODEL), F32), jax.ShapeDtypeStruct((t, HALF), U32),
                   jax.ShapeDtypeStruct((t, LANES), F32), jax.ShapeDtypeStruct((1, LANES), F32)],
        scratch_shapes=[pltpu.VMEM((1, LANES), F32)],
        compiler_params=_params("arbitrary"),
        name="route",
    )(x, oa, od, wo, g, wr)


def _positions_kernel(info_ref, cnt_ref, ltri_ref, utri_ref, pos_ref, run_scr, off_scr):
    info = info_ref[...]
    lane = lax.broadcasted_iota(jnp.int32, info.shape, 1).astype(F32)
    hit1 = lane == info[:, INFO_E1:INFO_E1 + 1]
    hit2 = lane == info[:, INFO_E2:INFO_E2 + 1]
    onehot = jnp.logical_or(hit1, hit2).astype(F32)

    @pl.when(pl.program_id(0) == 0)
    def _():
        ln = lax.broadcasted_iota(jnp.int32, cnt_ref.shape, 1)
        is_expert = jnp.logical_and(ln >= ROUTER_OFF, ln < ROUTER_OFF + N_EXPERTS)
        tiles = jnp.where(is_expert, jnp.maximum(jnp.floor((cnt_ref[...] + (MOE_TM - 1)) * (1.0 / MOE_TM)), 1.0), 0.0)
        off_scr[...] = MOE_TM * jnp.dot(tiles.astype(BF16), utri_ref[...], preferred_element_type=F32)
        run_scr[...] = jnp.zeros(run_scr.shape, F32)

    before = (jnp.dot(ltri_ref[...], onehot.astype(BF16), preferred_element_type=F32)
              + run_scr[...] + off_scr[...])
    pos1 = jnp.sum(jnp.where(hit1, before, 0.0), axis=-1, keepdims=True)
    pos2 = jnp.sum(jnp.where(hit2, before, 0.0), axis=-1, keepdims=True)
    both = jnp.where(lane == 0, pos1, 0.0) + jnp.where(lane == 1, pos2, 0.0)
    pos_ref[...] = both.T.astype(jnp.int32)
    run_scr[...] += jnp.sum(onehot, axis=0, keepdims=True)


def _positions(info, cnt):
    t = info.shape[0]
    tm = min(t, POS_TM)
    tok = np.arange(tm)
    ltri = jnp.asarray((tok[:, None] > tok[None, :]).astype(np.float32), dtype=BF16)
    ln = np.arange(LANES)
    utri = jnp.asarray((ln[:, None] < ln[None, :]).astype(np.float32), dtype=BF16)
    full = lambda a: pl.BlockSpec(a.shape, lambda i: (0,) * a.ndim)
    return pl.pallas_call(
        _positions_kernel,
        grid=(t // tm,),
        in_specs=[pl.BlockSpec((tm, LANES), lambda i: (i, 0)), full(cnt), full(ltri), full(utri)],
        out_specs=pl.BlockSpec((LANES, tm), lambda i: (0, i)),
        out_shape=jax.ShapeDtypeStruct((LANES, t), jnp.int32),
        scratch_shapes=[pltpu.VMEM((1, LANES), F32), pltpu.VMEM((1, LANES), F32)],
        compiler_params=_params("arbitrary"),
        name="positions",
    )(info, cnt, ltri, utri)


def _experts_kernel(te_ref, tv_ref, nt_ref, xs_ref, wg_hbm, wu_hbm, wd_hbm, xn_new_ref, gate_new_ref,
                    ys_ref, moe_new_ref, wg_s, wu_s, wd_s, wg_f, wu_f, wd_f, wsem):
    i = pl.program_id(0)
    used = i < nt_ref[0]
    expert = te_ref[i]

    def fetch(e):
        slot = e % 2
        return [pltpu.make_async_copy(src.at[e], dst.at[slot], wsem.at[slot, j])
                for j, (src, dst) in enumerate(((wg_hbm, wg_f), (wu_hbm, wu_f), (wd_hbm, wd_f)))]

    @pl.when(jnp.logical_or(i == 0, expert != te_ref[jnp.maximum(i - 1, 0)]))
    def _():
        @pl.when(i == 0)
        def _():
            for c in fetch(expert):
                c.start()
        for c in fetch(expert):
            c.wait()

        @pl.when(expert + 1 < N_EXPERTS)
        def _():
            for c in fetch(expert + 1):
                c.start()
        slot = expert % 2
        wg_s[...] = wg_f[slot].astype(BF16)
        wu_s[...] = wu_f[slot].astype(BF16)
        wd_s[...] = wd_f[slot].astype(BF16)
        xn = xn_new_ref[...]
        lane = lax.broadcasted_iota(jnp.int32, gate_new_ref.shape, 1)
        gate = jnp.sum(jnp.where(lane == expert + ROUTER_OFF, gate_new_ref[...], 0.0), axis=-1, keepdims=True)
        hg = jnp.dot(xn, wg_s[...], preferred_element_type=F32)
        hu = jnp.dot(xn, wu_s[...], preferred_element_type=F32)
        hm = _silu(hg) * hu * gate
        y = jnp.dot(hm.astype(BF16), wd_s[...], preferred_element_type=F32)

        @pl.when(i == 0)
        def _():
            moe_new_ref[...] = y

        @pl.when(i > 0)
        def _():
            moe_new_ref[...] += y

    @pl.when(used)
    def _():
        row = lax.broadcasted_iota(jnp.int32, xs_ref.shape, 0)
        x_lo, x_hi = _unpack_rows(jnp.where(row < tv_ref[i], xs_ref[...], jnp.uint32(0)))
        x_lo = x_lo.astype(BF16)
        x_hi = x_hi.astype(BF16)
        up = lambda w_s: (jnp.dot(x_lo, w_s[:HALF, :], preferred_element_type=F32)
                          + jnp.dot(x_hi, w_s[HALF:, :], preferred_element_type=F32))
        hm = (_silu_tanh(up(wg_s)) * up(wu_s)).astype(BF16)
        ys_ref[...] = _pack_rows(jnp.dot(hm, wd_s[...], preferred_element_type=F32))

    @pl.when(jnp.logical_not(used))
    def _():
        ys_ref[...] = jnp.zeros(ys_ref.shape, U32)


def _experts(xs, tile_expert, tile_valid, n_tiles, wg, wu, wd, xn_new, gate_new):
    max_tiles = xs.shape[0] // MOE_TM
    rows = pl.BlockSpec((MOE_TM, HALF), lambda i, te, tv, nt: (i, 0))
    hbm = pl.BlockSpec(memory_space=pl.ANY)
    full = lambda a: pl.BlockSpec(a.shape, lambda i, te, tv, nt: (0,) * a.ndim)
    return pl.pallas_call(
        _experts_kernel,
        grid_spec=pltpu.PrefetchScalarGridSpec(
            num_scalar_prefetch=3, grid=(max_tiles,),
            in_specs=[rows, hbm, hbm, hbm, full(xn_new), full(gate_new)],
            out_specs=[rows, pl.BlockSpec(xn_new.shape, lambda i, te, tv, nt: (0, 0))],
            scratch_shapes=[pltpu.VMEM((D_MODEL, D_EXPERT), BF16), pltpu.VMEM((D_MODEL, D_EXPERT), BF16),
                            pltpu.VMEM((D_EXPERT, D_MODEL), BF16),
                            pltpu.VMEM((2, D_MODEL, D_EXPERT), F32), pltpu.VMEM((2, D_MODEL, D_EXPERT), F32),
                            pltpu.VMEM((2, D_EXPERT, D_MODEL), F32), pltpu.SemaphoreType.DMA((2, 3))]),
        out_shape=[jax.ShapeDtypeStruct(xs.shape, U32), jax.ShapeDtypeStruct(xn_new.shape, F32)],
        compiler_params=_params("arbitrary"),
        name="experts",
    )(tile_expert, tile_valid, n_tiles, xs, wg, wu, wd, xn_new, gate_new)


SC_IDX = 128
SC_ROWS = 64
SC_WORKERS = 32


def _sc_mesh():
    return plsc.VectorSubcoreMesh(core_axis_name="c", subcore_axis_name="s")


def _sc_windows(t, fn):
    per_worker = t // SC_WORKERS
    worker = lax.axis_index(("c", "s"))

    @pl.loop(0, per_worker // SC_IDX)
    def _(w):
        fn(worker * per_worker + w * SC_IDX)


def _sc_scatter_rows(xn, pos1, pos2, n_rows):
    t, d = xn.shape
    assert t % (SC_WORKERS * SC_IDX) == 0
    idx_t = pltpu.VMEM((1, SC_IDX), jnp.int32)

    @pl.kernel(out_type=jax.ShapeDtypeStruct((n_rows, d), xn.dtype), mesh=_sc_mesh(),
               scratch_types=[idx_t, idx_t, pltpu.VMEM((SC_ROWS, d), xn.dtype)])
    def scatter(x_hbm, p1_hbm, p2_hbm, o_hbm, i1_v, i2_v, buf):
        def window(base):
            pltpu.sync_copy(p1_hbm.at[:, pl.ds(base, SC_IDX)], i1_v)
            pltpu.sync_copy(p2_hbm.at[:, pl.ds(base, SC_IDX)], i2_v)
            for k in range(SC_IDX // SC_ROWS):
                pltpu.sync_copy(x_hbm.at[pl.ds(base + k * SC_ROWS, SC_ROWS)], buf)
                pltpu.sync_copy(buf, o_hbm.at[i1_v.at[0, pl.ds(k * SC_ROWS, SC_ROWS)]])
                pltpu.sync_copy(buf, o_hbm.at[i2_v.at[0, pl.ds(k * SC_ROWS, SC_ROWS)]])
        _sc_windows(t, window)

    return scatter(xn, pos1.reshape(1, t), pos2.reshape(1, t))


def _sc_gather_rows(ys, pos1, pos2):
    t = pos1.shape[0]
    d = ys.shape[1]
    assert t % (SC_WORKERS * SC_IDX) == 0
    idx_t = pltpu.VMEM((1, SC_IDX), jnp.int32)
    out = jax.ShapeDtypeStruct((t, d), ys.dtype)

    buf_t = pltpu.VMEM((SC_ROWS, d), ys.dtype)

    @pl.kernel(out_type=(out, out), mesh=_sc_mesh(),
               scratch_types=[idx_t, idx_t, buf_t, buf_t, pltpu.SemaphoreType.DMA((2,)),
                              pltpu.SemaphoreType.DMA((2,))])
    def gather(y_hbm, p1_hbm, p2_hbm, o1_hbm, o2_hbm, i1_v, i2_v, buf_a, buf_b, gsem, wsem):
        bufs = (buf_a, buf_b)

        def window(base):
            pltpu.sync_copy(p1_hbm.at[:, pl.ds(base, SC_IDX)], i1_v)
            pltpu.sync_copy(p2_hbm.at[:, pl.ds(base, SC_IDX)], i2_v)
            items = [(idx_v, o_hbm, k) for k in range(SC_IDX // SC_ROWS)
                     for idx_v, o_hbm in ((i1_v, o1_hbm), (i2_v, o2_hbm))]

            def read(n):
                idx_v, _, k = items[n]
                return pltpu.make_async_copy(y_hbm.at[idx_v.at[0, pl.ds(k * SC_ROWS, SC_ROWS)]],
                                             bufs[n % 2], gsem.at[n % 2])

            def write(n):
                _, o_hbm, k = items[n]
                return pltpu.make_async_copy(bufs[n % 2], o_hbm.at[pl.ds(base + k * SC_ROWS, SC_ROWS)],
                                             wsem.at[n % 2])

            read(0).start()
            for n in range(len(items)):
                read(n).wait()
                if n >= 1:
                    write(n - 1).wait()
                if n + 1 < len(items):
                    read(n + 1).start()
                write(n).start()
            write(len(items) - 1).wait()
        _sc_windows(t, window)

    return gather(ys, pos1.reshape(1, t), pos2.reshape(1, t))


def _ple_sparse_kernel(h_ref, info_ref, y1_ref, y2_ref, p_ref, wpp_ref, wpg_ref, gp_ref, gf_ref, y_ref):
    info = info_ref[...]
    g1 = info[:, INFO_G1:INFO_G1 + 1]
    g2 = info[:, INFO_G2:INFO_G2 + 1]
    y1_lo, y1_hi = _unpack_rows(y1_ref[...])
    y2_lo, y2_hi = _unpack_rows(y2_ref[...])
    moe = jnp.concatenate([g1 * y1_lo + g2 * y2_lo, g1 * y1_hi + g2 * y2_hi], axis=1)
    h = h_ref[...] + moe
    hn = _rmsnorm(h, gp_ref[...])
    h = h + _mm(p_ref[...], wpp_ref[...]) * _sigmoid(_mm(hn, wpg_ref[...]))
    y_ref[...] = _rmsnorm(h, gf_ref[...])


def _ple_sparse(h, info, y1, y2, p, wpp, wpg, gp, gf):
    t = h.shape[0]
    tm = WIDE_TM
    row = lambda n: pl.BlockSpec((tm, n), lambda i: (i, 0))
    full = lambda a: pl.BlockSpec(a.shape, lambda i: (0,) * a.ndim)
    return pl.pallas_call(
        _ple_sparse_kernel,
        grid=(t // tm,),
        in_specs=[row(D_MODEL), row(LANES), row(HALF), row(HALF), row(PLE_DIM),
                  full(wpp), full(wpg), full(gp), full(gf)],
        out_specs=row(D_MODEL),
        out_shape=jax.ShapeDtypeStruct((t, D_MODEL), F32),
        compiler_params=_params("parallel"),
        name="ple_sparse",
    )(h, info, y1, y2, p, wpp, wpg, gp, gf)


def _tile_tables(cnt, max_tiles):
    tiles_e = jnp.maximum((cnt + (MOE_TM - 1)) // MOE_TM, 1)
    ends = jnp.cumsum(tiles_e)
    n_tiles = ends[-1]
    tile = jnp.arange(max_tiles, dtype=jnp.int32)
    idx = jnp.minimum(tile, n_tiles - 1)
    tile_expert = jnp.sum((idx[:, None] >= ends[None, :]).astype(jnp.int32), axis=1)
    mine = tile_expert[:, None] == jnp.arange(N_EXPERTS, dtype=jnp.int32)[None, :]
    of_mine = lambda v: jnp.sum(jnp.where(mine, v[None, :], 0), axis=1)
    valid = jnp.clip(of_mine(cnt) - (idx - of_mine(ends - tiles_e)) * MOE_TM, 0, MOE_TM)
    tile_valid = jnp.where(tile < n_tiles, valid, 0).astype(jnp.int32)
    return tile_expert, tile_valid, n_tiles.reshape(1)


def _ple_final_kernel(h_ref, m_ref, p_ref, wpp_ref, wpg_ref, gp_ref, gf_ref, y_ref):
    h = h_ref[...] + m_ref[...]
    hn = _rmsnorm(h, gp_ref[...])
    h = h + _mm(p_ref[...], wpp_ref[...]) * _sigmoid(_mm(hn, wpg_ref[...]))
    y_ref[...] = _rmsnorm(h, gf_ref[...])


def _ple_final(h, m, p, wpp, wpg, gp, gf):
    t = h.shape[0]
    tm = min(t, 256)
    row = lambda n: pl.BlockSpec((tm, n), lambda i: (i, 0))
    full = lambda a: pl.BlockSpec(a.shape, lambda i: (0,) * a.ndim)
    return pl.pallas_call(
        _ple_final_kernel,
        grid=(t // tm,),
        in_specs=[row(D_MODEL), row(D_MODEL), row(PLE_DIM), full(wpp), full(wpg), full(gp), full(gf)],
        out_specs=row(D_MODEL),
        out_shape=jax.ShapeDtypeStruct((t, D_MODEL), F32),
        compiler_params=_params("parallel"),
        name="ple_final",
    )(h, m, p, wpp, wpg, gp, gf)


def kernel(x_prompt, x_sample, p_prompt, p_sample, cache_k, cache_v, state_conv, state_S, rel_bias, norm_mix, w_in, att_sink, conv_w, dn_A_log, dn_dt_bias, dn_norm, w_out, norm_ffn, w_router_group, w_router_expert, w_gate, w_up, w_down, w_ple_proj, w_ple_gate, norm_ple, norm_final):
    batch, seq, _ = x_prompt.shape
    nseq = x_sample.shape[0]
    assert x_sample.shape[1] == 1 and norm_mix.shape[0] == 1 and cache_k.shape[2] == WINDOW
    assert seq % GDN_TB == 0 and seq % ATT_BLOCK == 0

    wi = w_in[0]
    o_db = ATT_COLS + CONV_CH
    w_in_re = (wi[:, :o_db].astype(BF16), wi[:, o_db + 2 * DN_HEADS:].astype(BF16),
               jnp.pad(wi[:, o_db:o_db + 2 * DN_HEADS], ((0, 0), (0, LANES - 2 * DN_HEADS))).astype(BF16))
    row = lambda a: a.reshape(1, -1).astype(F32)
    pad_lanes = lambda a, off: jnp.zeros((1, LANES), F32).at[0, off:off + a.shape[0]].set(a)
    alog = pad_lanes(dn_A_log[0], DN_HEADS)
    dtb = pad_lanes(dn_dt_bias[0], DN_HEADS)
    dnx = jnp.tile(dn_norm[0], DN_HEADS).reshape(1, DN_WIDTH)
    w_router = jnp.concatenate(
        [w_router_group[0], w_router_expert[0],
         jnp.zeros((D_MODEL, LANES - N_GROUPS - N_EXPERTS), F32)], axis=1).astype(BF16)
    wo = w_out[0].astype(BF16)
    wg, wu, wd = w_gate[0], w_up[0], w_down[0]
    wpp, wpg = w_ple_proj[0].astype(BF16), w_ple_gate[0].astype(BF16)
    sink = att_sink[0]

    qi = np.arange(ATT_BLOCK)[:, None]
    kj = np.arange(2 * ATT_BLOCK)[None, :]
    bucket_p = jnp.asarray(_t5_bucket_np(qi + ATT_BLOCK - kj))
    bucket_s = jnp.asarray(_t5_bucket_np(WINDOW - np.arange(WINDOW)[None, :]))

    xp = x_prompt.reshape(batch * seq, D_MODEL)
    att_p, qkv_p, dz_p, ba_p, xc_tails = _inproj_conv(xp, row(norm_mix[0]), w_in_re, conv_w[0], seq)
    o_att_p = _attn_prompt(att_p, bucket_p, rel_bias, sink, batch, seq)
    o_dn_p, s_p = _gdn_prompt(qkv_p, dz_p, ba_p, alog, dtb, dnx, batch, seq)
    h1, xn2, info, cnt = _route_sparse(xp, o_att_p, o_dn_p, wo, row(norm_ffn[0]), w_router)
    pos = _positions(info, cnt)
    pos1, pos2 = pos[0], pos[1]
    max_tiles = _moe_tiles(batch * seq)
    cnt_e = cnt[0, ROUTER_OFF:ROUTER_OFF + N_EXPERTS].astype(jnp.int32)
    tile_expert, tile_valid, n_tiles = _tile_tables(cnt_e, max_tiles)
    xs_sorted = _sc_scatter_rows(xn2, pos1, pos2, max_tiles * MOE_TM)

    xs = x_sample.reshape(nseq, D_MODEL)
    att_s, xc_s, dz_s, ba_s = _inproj(xs, row(norm_mix[0]), w_in_re)
    ck_t = jnp.transpose(cache_k[0], (0, 2, 3, 1))
    cv_t = jnp.transpose(cache_v[0], (0, 2, 3, 1))
    o_att_s, ks_t, vs_t = _attn_sample(att_s, ck_t, cv_t, bucket_s, rel_bias, sink)
    sconv_t = jnp.swapaxes(state_conv[0], 0, 1)
    o_dn_s_t, s_s_t = _gdn_sample_lanes(xc_s, dz_s, ba_s, sconv_t, jnp.transpose(state_S[0], (1, 2, 3, 0)),
                                        conv_w[0], alog, dtb, dn_norm[0])
    s_s = jnp.transpose(s_s_t, (3, 0, 1, 2))

    h1_s, xn2_s, gates_s = _outproj_router(xs, o_att_s, o_dn_s_t, wo, row(norm_ffn[0]), w_router)

    ys, moe_s = _experts(xs_sorted, tile_expert, tile_valid, n_tiles, wg, wu, wd, xn2_s, gates_s)
    y1, y2 = _sc_gather_rows(ys, pos1, pos2)
    y_s = _ple_final(h1_s, moe_s, p_sample[0].reshape(nseq, PLE_DIM), wpp, wpg, row(norm_ple[0]),
                     row(norm_final))
    y_p = _ple_sparse(h1, info, y1, y2, p_prompt[0].reshape(batch * seq, PLE_DIM),
                      wpp, wpg, row(norm_ple[0]), row(norm_final))

    att_p3 = att_p.reshape(batch, seq, ATT_COLS)
    kv_shape = (1, batch, WINDOW, ATT_KV_HEADS, HEAD_DIM)
    k_p = att_p3[:, seq - WINDOW:, ATT_WIDTH:ATT_WIDTH + KV_WIDTH].reshape(kv_shape)
    v_p = att_p3[:, seq - WINDOW:, ATT_WIDTH + KV_WIDTH:].reshape(kv_shape)
    conv_p = xc_tails.reshape(batch, -1, TAIL, CONV_CH)[:, -1, TAIL - (CONV_WIDTH - 1):][None]
    k_s = jnp.transpose(ks_t, (0, 3, 1, 2))[None]
    v_s = jnp.transpose(vs_t, (0, 3, 1, 2))[None]
    conv_s = jnp.concatenate([state_conv[0][:, 1:], xc_s[:, None, :]], axis=1)[None]
    return (y_p.reshape(batch, seq, D_MODEL), y_s.reshape(nseq, 1, D_MODEL),
            k_p, v_p, conv_p, s_p[None], k_s, v_s, conv_s, s_s[None])
```

```python
import functools
import math

import numpy as np
import jax
import jax.numpy as jnp
from jax import lax
from jax.experimental import pallas as pl
from jax.experimental.pallas import tpu as pltpu
from jax.experimental.pallas import tpu_sc as plsc

F32 = jnp.float32
BF16 = jnp.bfloat16

D_MODEL = 1024
ATT_HEADS = 8
ATT_KV_HEADS = 2
HEAD_DIM = 64
GQA = ATT_HEADS // ATT_KV_HEADS
WINDOW = 128
ATT_BLOCK = 128
N_BUCKETS = 32
DN_HEADS = 8
DN_DK = 64
DN_DV = 64
CONV_WIDTH = 4
DN_CHUNK = 64
ATT_WIDTH = ATT_HEADS * HEAD_DIM
KV_WIDTH = ATT_KV_HEADS * HEAD_DIM
DN_WIDTH = DN_HEADS * DN_DV
CONV_CH = 3 * DN_WIDTH
N_GROUPS = 4
EXPERTS_PER_GROUP = 8
N_EXPERTS = N_GROUPS * EXPERTS_PER_GROUP
D_EXPERT = 256
PLE_DIM = 256
EPS = 1e-6
NEG_INF = float("-inf")

ATT_COLS = ATT_WIDTH + 2 * KV_WIDTH
LANES = 128
ROUTER_OFF = N_GROUPS
VMEM_LIMIT = 48 * 1024 * 1024
ROW_TM = 512
WIDE_TM = 1024


def _params(*sem):
    return pltpu.CompilerParams(dimension_semantics=sem, vmem_limit_bytes=VMEM_LIMIT)


def _mm(a, b):
    return jnp.dot(a.astype(BF16), b.astype(BF16), preferred_element_type=F32)


def _mm_nt(a, b):
    return lax.dot_general(a.astype(BF16), b.astype(BF16), (((1,), (1,)), ((), ())),
                           preferred_element_type=F32)


def _mm_tn(a, b):
    return lax.dot_general(a.astype(BF16), b.astype(BF16), (((0,), (0,)), ((), ())),
                           preferred_element_type=F32)


def _split3(x):
    h1 = x.astype(BF16)
    r1 = x - h1.astype(F32)
    h2 = r1.astype(BF16)
    h3 = (r1 - h2.astype(F32)).astype(BF16)
    return h1, h2, h3


def _mm_sel_rhs(x, sel):
    h1, h2, h3 = _split3(x)
    d = lambda h: jnp.dot(h, sel, preferred_element_type=F32)
    return d(h1) + d(h2) + d(h3)


def _mm_sel_lhs(sel, x):
    h1, h2, h3 = _split3(x)
    d = lambda h: jnp.dot(sel, h, preferred_element_type=F32)
    return d(h1) + d(h2) + d(h3)


def _sigmoid(x):
    return 1.0 / (1.0 + jnp.exp(-x))


def _silu(x):
    return x * _sigmoid(x)


def _silu_tanh(x):
    return x * (0.5 * jnp.tanh(0.5 * x) + 0.5)


def _softplus(x):
    return jnp.maximum(x, 0.0) + jnp.log1p(jnp.exp(-jnp.abs(x)))


def _rmsnorm(x, g):
    return x * lax.rsqrt(jnp.mean(x * x, axis=-1, keepdims=True) + EPS) * g


def _t5_bucket_np(dist):
    max_exact = N_BUCKETS // 2
    d = np.maximum(dist, 0)
    ratio = (np.log(np.maximum(d, 1).astype(np.float32) / np.float32(max_exact))
             / np.float32(math.log(WINDOW / max_exact))).astype(np.float32)
    large = np.minimum(max_exact + (ratio * np.float32(N_BUCKETS - max_exact)).astype(np.int32),
                       N_BUCKETS - 1)
    return np.where(d < max_exact, d, large).astype(np.int32)


def _bias_lookup(bucket, rb_ref, h):
    acc = jnp.zeros(bucket.shape, F32)
    for t in range(N_BUCKETS):
        acc = jnp.where(bucket == t, rb_ref[t, h], acc)
    return acc


def _inproj_kernel(x_ref, g_ref, wa_ref, wz_ref, wb_ref, att_ref, xc_ref, dz_ref, ba_ref):
    xn = _rmsnorm(x_ref[...], g_ref[...]).astype(BF16)
    att_ref[...] = jnp.dot(xn, wa_ref[:, :ATT_COLS], preferred_element_type=F32)
    xc_ref[...] = jnp.dot(xn, wa_ref[:, ATT_COLS:], preferred_element_type=F32)
    dz_ref[...] = jnp.dot(xn, wz_ref[...], preferred_element_type=F32)
    ba_ref[...] = jnp.dot(xn, wb_ref[...], preferred_element_type=F32)


def _inproj(x, g, w):
    t = x.shape[0]
    tm = min(t, ROW_TM)
    row = lambda n: pl.BlockSpec((tm, n), lambda i: (i, 0))
    full = lambda a: pl.BlockSpec(a.shape, lambda i: (0,) * a.ndim)
    return pl.pallas_call(
        _inproj_kernel,
        grid=(t // tm,),
        in_specs=[row(D_MODEL), full(g)] + [full(a) for a in w],
        out_specs=[row(ATT_COLS), row(CONV_CH), row(DN_WIDTH), row(LANES)],
        out_shape=[jax.ShapeDtypeStruct((t, n), F32) for n in (ATT_COLS, CONV_CH, DN_WIDTH, LANES)],
        compiler_params=_params("parallel"),
        name="inproj",
    )(x, g, *w)


TAIL = 8
PAIR = 2 * DN_DK
N_PAIRS = DN_WIDTH // PAIR


def _head_sums(z, pair_ones):
    hi = z.astype(BF16)
    lw = (z - hi.astype(F32)).astype(BF16)
    d = lambda a, p: jnp.dot(a[:, p * PAIR:(p + 1) * PAIR], pair_ones, preferred_element_type=F32)
    return jnp.concatenate([d(hi, p) + d(lw, p) for p in range(N_PAIRS)], axis=1)


def _inproj_conv_kernel(x_ref, g_ref, wa_ref, wz_ref, wb_ref, cw_ref, ones_ref,
                        att_ref, qkv_ref, dz_ref, ba_ref, tail_ref, xp_scr, *, tiles_per_seq):
    tm = x_ref.shape[0]

    @pl.when(pl.program_id(0) % tiles_per_seq == 0)
    def _():
        xp_scr[...] = jnp.zeros((TAIL, CONV_CH), F32)

    xn = _rmsnorm(x_ref[...], g_ref[...]).astype(BF16)
    xc = jnp.dot(xn, wa_ref[:, ATT_COLS:], preferred_element_type=F32)
    att_ref[...] = jnp.dot(xn, wa_ref[:, :ATT_COLS], preferred_element_type=F32)
    dz_ref[...] = jnp.dot(xn, wz_ref[...], preferred_element_type=F32)
    ba_ref[...] = jnp.dot(xn, wb_ref[...], preferred_element_type=F32)

    head = jnp.concatenate([xp_scr[...], xc[:TAIL, :]], axis=0)

    def shifted(j):
        return jnp.concatenate([head[TAIL - j:2 * TAIL - j, :], pltpu.roll(xc, j, axis=0)[TAIL:, :]], axis=0)

    y = shifted(3) * cw_ref[0:1, :]
    y = y + shifted(2) * cw_ref[1:2, :]
    y = y + shifted(1) * cw_ref[2:3, :]
    y = y + xc * cw_ref[3:4, :]
    tail = xc[tm - TAIL:, :]
    xp_scr[...] = tail
    tail_ref[0] = tail
    y = _silu_tanh(y)
    q = y[:, :DN_WIDTH]
    k = y[:, DN_WIDTH:2 * DN_WIDTH]
    inv_norm = lax.rsqrt(_head_sums(jnp.concatenate([q * q, k * k], axis=0), ones_ref[...]) + EPS)
    qkv_ref[:, :DN_WIDTH] = q * inv_norm[:tm] * (DN_DK ** -0.5)
    qkv_ref[:, DN_WIDTH:2 * DN_WIDTH] = k * inv_norm[tm:]
    qkv_ref[:, 2 * DN_WIDTH:] = y[:, 2 * DN_WIDTH:]


def _pair_ones():
    lane = np.arange(PAIR)
    return jnp.asarray((lane[:, None] // DN_DV == lane[None, :] // DN_DV).astype(np.float32), dtype=BF16)


def _inproj_conv(x, g, w, conv_w, seq):
    t = x.shape[0]
    tm = ROW_TM
    assert seq % tm == 0
    ones = _pair_ones()
    row = lambda n: pl.BlockSpec((tm, n), lambda i: (i, 0))
    full = lambda a: pl.BlockSpec(a.shape, lambda i: (0,) * a.ndim)
    return pl.pallas_call(
        functools.partial(_inproj_conv_kernel, tiles_per_seq=seq // tm),
        grid=(t // tm,),
        in_specs=[row(D_MODEL), full(g)] + [full(a) for a in w] + [full(conv_w), full(ones)],
        out_specs=[row(ATT_COLS), row(CONV_CH), row(DN_WIDTH), row(LANES),
                   pl.BlockSpec((1, TAIL, CONV_CH), lambda i: (i, 0, 0))],
        out_shape=[jax.ShapeDtypeStruct((t, n), F32) for n in (ATT_COLS, CONV_CH, DN_WIDTH, LANES)]
                  + [jax.ShapeDtypeStruct((t // tm, TAIL, CONV_CH), F32)],
        scratch_shapes=[pltpu.VMEM((TAIL, CONV_CH), F32)],
        compiler_params=_params("arbitrary"),
        name="inproj_conv",
    )(x, g, *w, conv_w, ones)


GROUP_ROWS = GQA * ATT_BLOCK


def _attn_prompt_kernel(cur_ref, prev_ref, bucket_ref, rb_ref, sink_ref, o_ref, bias_scr, sink_scr):
    i = pl.program_id(0)
    nseq = cur_ref.shape[0]

    @pl.when(i == 0)
    def _():
        qi = lax.broadcasted_iota(jnp.int32, (ATT_BLOCK, 2 * ATT_BLOCK), 0)
        kj = lax.broadcasted_iota(jnp.int32, (ATT_BLOCK, 2 * ATT_BLOCK), 1)
        dist = qi + ATT_BLOCK - kj
        band = jnp.logical_and(dist >= 0, dist < WINDOW)
        bucket = bucket_ref[...]
        hrow = lax.broadcasted_iota(jnp.int32, (GROUP_ROWS, 1), 0) // ATT_BLOCK
        for g in range(ATT_KV_HEADS):
            sink_col = jnp.zeros((GROUP_ROWS, 1), F32)
            for hh in range(GQA):
                h = g * GQA + hh
                bias = jnp.where(band, _bias_lookup(bucket, rb_ref, h), NEG_INF)
                bias_scr[0, g, hh * ATT_BLOCK:(hh + 1) * ATT_BLOCK, :] = bias
                bias_scr[1, g, hh * ATT_BLOCK:(hh + 1) * ATT_BLOCK, :] = jnp.where(kj >= ATT_BLOCK, bias, NEG_INF)
                sink_col = jnp.where(hrow == hh, sink_ref[h], sink_col)
            sink_scr[g] = sink_col

    first = (i == 0).astype(jnp.int32)
    probs = [(b, g) for b in range(nseq) for g in range(ATT_KV_HEADS)]
    scores = []
    for b, g in probs:
        cur = cur_ref[b]
        prev = prev_ref[b]
        q = jnp.concatenate([cur[:, (g * GQA + hh) * HEAD_DIM:(g * GQA + hh + 1) * HEAD_DIM]
                             for hh in range(GQA)], axis=0) * (HEAD_DIM ** -0.5)
        kcol = slice(ATT_WIDTH + g * HEAD_DIM, ATT_WIDTH + (g + 1) * HEAD_DIM)
        k2 = jnp.concatenate([prev[:, kcol], cur[:, kcol]], axis=0)
        scores.append(_mm_nt(q, k2) + bias_scr[first, g])
    probs_p, dens = [], []
    for (b, g), s in zip(probs, scores):
        sink = sink_scr[g]
        m = jnp.maximum(jnp.max(s, axis=-1, keepdims=True), sink)
        p = jnp.exp(s - m)
        dens.append(jnp.sum(p, axis=-1, keepdims=True) + jnp.exp(sink - m))
        probs_p.append(p.astype(BF16))
    outs = {}
    for (b, g), p, den in zip(probs, probs_p, dens):
        vcol = slice(ATT_WIDTH + KV_WIDTH + g * HEAD_DIM, ATT_WIDTH + KV_WIDTH + (g + 1) * HEAD_DIM)
        v2 = jnp.concatenate([prev_ref[b][:, vcol], cur_ref[b][:, vcol]], axis=0)
        outs[b, g] = _mm(p, v2) / den
    for b in range(nseq):
        o_ref[b] = jnp.concatenate([outs[b, g][hh * ATT_BLOCK:(hh + 1) * ATT_BLOCK, :]
                                    for g in range(ATT_KV_HEADS) for hh in range(GQA)],
                                   axis=1).astype(o_ref.dtype)


def _attn_prompt(att, bucket, rel_bias, sink, batch, seq):
    nb = seq // ATT_BLOCK
    smem = pl.BlockSpec(memory_space=pltpu.SMEM)
    att3 = att.reshape(batch, seq, ATT_COLS)
    out = pl.pallas_call(
        _attn_prompt_kernel,
        grid=(nb,),
        in_specs=[
            pl.BlockSpec((batch, ATT_BLOCK, ATT_COLS), lambda i: (0, i, 0)),
            pl.BlockSpec((batch, ATT_BLOCK, ATT_COLS), lambda i: (0, jnp.maximum(i - 1, 0), 0)),
            pl.BlockSpec(bucket.shape, lambda i: (0, 0)),
            smem, smem,
        ],
        out_specs=pl.BlockSpec((batch, ATT_BLOCK, ATT_WIDTH), lambda i: (0, i, 0)),
        out_shape=jax.ShapeDtypeStruct((batch, seq, ATT_WIDTH), BF16),
        scratch_shapes=[pltpu.VMEM((2, ATT_KV_HEADS, GROUP_ROWS, 2 * ATT_BLOCK), F32),
                        pltpu.VMEM((ATT_KV_HEADS, GROUP_ROWS, 1), F32)],
        compiler_params=_params("arbitrary"),
        name="attn_prompt",
    )(att3, att3, bucket, rel_bias, sink)
    return out.reshape(batch * seq, ATT_WIDTH)


ATT_S_BB = 8


def _attn_sample_kernel(att_ref, ck_ref, cv_ref, bucket_ref, rb_ref, sink_ref, o_ref, ks_ref, vs_ref,
                        bias_scr, col_scr):
    hrow = lax.broadcasted_iota(jnp.int32, (ATT_HEADS, LANES), 0)
    lane = lax.broadcasted_iota(jnp.int32, (ATT_HEADS, LANES), 1)

    last = (lax.broadcasted_iota(jnp.int32, (3, WINDOW), 1) == WINDOW - 1).astype(BF16)
    is_last = lax.broadcasted_iota(jnp.int32, (KV_WIDTH, WINDOW), 1) == WINDOW - 1

    def shifted(cache_t, new_row):
        pieces = jnp.concatenate([p.astype(F32) for p in _split3(new_row)], axis=0).astype(BF16)
        col = lax.dot_general(pieces, last, (((0,), (0,)), ((), ())), preferred_element_type=F32)
        out = jnp.where(is_last, col, pltpu.roll(cache_t, WINDOW - 1, axis=1))
        return out.reshape(ATT_KV_HEADS, HEAD_DIM, WINDOW)

    for b in range(ATT_S_BB):
        row = att_ref[b:b + 1, :]
        ks_ref[b] = shifted(ck_ref[b].reshape(KV_WIDTH, WINDOW), row[:, ATT_WIDTH:ATT_WIDTH + KV_WIDTH])
        vs_ref[b] = shifted(cv_ref[b].reshape(KV_WIDTH, WINDOW), row[:, ATT_WIDTH + KV_WIDTH:])

    @pl.when(pl.program_id(0) == 0)
    def _():
        bucket = jnp.broadcast_to(bucket_ref[...], (ATT_HEADS, LANES))
        bias = jnp.zeros((ATT_HEADS, LANES), F32)
        cols = jnp.zeros((ATT_HEADS, LANES), F32)
        for h in range(ATT_HEADS):
            bias = jnp.where(hrow == h, _bias_lookup(bucket, rb_ref, h), bias)
            cols = jnp.where(jnp.logical_and(hrow == h, lane == 0), sink_ref[h], cols)
            cols = jnp.where(jnp.logical_and(hrow == h, lane == 1), rb_ref[0, h], cols)
        bias_scr[...] = jnp.where(lane >= 1, bias, NEG_INF)
        col_scr[...] = cols

    bias_c = bias_scr[...]
    sink = col_scr[:, 0:1]
    bias_n = col_scr[:, 1:2]
    same_group = (hrow // GQA) == (lane // HEAD_DIM)
    low_group = lax.broadcasted_iota(jnp.int32, (ATT_HEADS, HEAD_DIM), 0) < GQA
    rnd = lambda a: a.astype(BF16).astype(F32)
    seqs = range(ATT_S_BB)
    rows = [att_ref[b:b + 1, :] for b in seqs]
    q_bds = []
    for row in rows:
        q = row[:, :ATT_WIDTH] * (HEAD_DIM ** -0.5)
        qh = jnp.concatenate([q[:, h * HEAD_DIM:(h + 1) * HEAD_DIM] for h in range(ATT_HEADS)], axis=0)
        q_bds.append(jnp.where(same_group, jnp.concatenate([qh, qh], axis=1), 0.0))
    kv_t = lambda ref, b: ref[b].reshape(KV_WIDTH, WINDOW)
    s_cs = [_mm(q_bd, kv_t(ck_ref, b)) + bias_c for b, q_bd in zip(seqs, q_bds)]
    prs, pns = [], []
    for row, q_bd, s_c in zip(rows, q_bds, s_cs):
        kn = row[:, ATT_WIDTH:ATT_WIDTH + KV_WIDTH]
        s_n = jnp.sum(rnd(q_bd) * rnd(kn), axis=-1, keepdims=True) + bias_n
        m = jnp.maximum(jnp.maximum(jnp.max(s_c, axis=-1, keepdims=True), s_n), sink)
        p_c = jnp.exp(s_c - m)
        p_n = jnp.exp(s_n - m)
        den = jnp.sum(p_c, axis=-1, keepdims=True) + p_n + jnp.exp(sink - m)
        prs.append(p_c / den)
        pns.append(p_n / den)
    pvs = [_mm_nt(pr, kv_t(cv_ref, b)) for b, pr in zip(seqs, prs)]
    for b, row, pv, pn in zip(seqs, rows, pvs, pns):
        vn = row[:, ATT_WIDTH + KV_WIDTH:]
        o_full = pv + rnd(pn) * rnd(vn)
        o_sel = jnp.where(low_group, o_full[:, :HEAD_DIM], o_full[:, HEAD_DIM:])
        o_ref[b:b + 1, :] = jnp.concatenate([o_sel[h:h + 1, :] for h in range(ATT_HEADS)], axis=1)


def _attn_sample(att, ck, cv, bucket, rel_bias, sink):
    nseq = att.shape[0]
    smem = pl.BlockSpec(memory_space=pltpu.SMEM)
    cache = pl.BlockSpec((ATT_S_BB, ATT_KV_HEADS, HEAD_DIM, WINDOW), lambda i: (i, 0, 0, 0))
    return pl.pallas_call(
        _attn_sample_kernel,
        grid=(nseq // ATT_S_BB,),
        in_specs=[pl.BlockSpec((ATT_S_BB, ATT_COLS), lambda i: (i, 0)), cache, cache,
                  pl.BlockSpec(bucket.shape, lambda i: (0, 0)), smem, smem],
        out_specs=[pl.BlockSpec((ATT_S_BB, ATT_WIDTH), lambda i: (i, 0)), cache, cache],
        out_shape=[jax.ShapeDtypeStruct((nseq, ATT_WIDTH), F32),
                   jax.ShapeDtypeStruct(ck.shape, F32), jax.ShapeDtypeStruct(cv.shape, F32)],
        scratch_shapes=[pltpu.VMEM((ATT_HEADS, LANES), F32), pltpu.VMEM((ATT_HEADS, LANES), F32)],
        compiler_params=_params("arbitrary"),
        name="attn_sample",
    )(att, ck, cv, bucket, rel_bias, sink)


GDN_TB = 128
GDN_NC = GDN_TB // DN_CHUNK


def _gdn_gates(ba, alog, dtb):
    beta = _sigmoid(ba)
    g = -jnp.exp(alog) * _softplus(ba + dtb)
    return beta, g


def _pair_diag(x, lo):
    xb = x.astype(BF16)
    zero = jnp.zeros_like(xb)
    return jnp.concatenate([jnp.where(lo, xb, zero), jnp.where(lo, zero, xb)], axis=0)


def _gdn_prompt_kernel(qkv_ref, dz_ref, ba_ref, alog_ref, dtb_ref, dnx_ref,
                       hsum_ref, expb_ref, expg_ref, ltri_ref,
                       o_ref, s_out_ref, s_scr):
    i = pl.program_id(0)
    nb = qkv_ref.shape[0]

    @pl.when(i == 0)
    def _():
        s_scr[...] = jnp.zeros(s_scr.shape, F32)

    hsum = hsum_ref[...]
    ri = lax.broadcasted_iota(jnp.int32, (DN_CHUNK, PAIR), 0)
    ci = lax.broadcasted_iota(jnp.int32, (DN_CHUNK, PAIR), 1)
    lo = ci < DN_DK
    cj = jnp.where(lo, ci, ci - DN_DK)
    causal = ri >= cj
    strict = ri > cj
    eye = (ri == cj).astype(F32)

    def sel2(x, m):
        hi = x.astype(BF16)
        lw = (x - hi.astype(F32)).astype(BF16)
        return (jnp.dot(hi, m, preferred_element_type=F32) + jnp.dot(lw, m, preferred_element_type=F32))

    pre = []
    for b in range(nb):
        q = qkv_ref[b, :, :DN_WIDTH]
        k = qkv_ref[b, :, DN_WIDTH:2 * DN_WIDTH]
        v = qkv_ref[b, :, 2 * DN_WIDTH:]
        beta_c, g_c = _gdn_gates(ba_ref[b], alog_ref[...], dtb_ref[...])
        beta = sel2(beta_c, expb_ref[...])
        gam_c = _mm_sel_lhs(ltri_ref[...], g_c)
        gam = _mm_sel_rhs(gam_c, expg_ref[...])
        gam_t = gam_c.T
        kb = k * beta
        egam = jnp.exp(gam)
        pre.append(dict(q=q, k=k, kb=kb, vb=v * beta, qg=q * egam, wr=kb * egam, gam=gam, gam_t=gam_t))

    probs = [(c, b, p) for c in range(GDN_NC) for b in range(nb) for p in range(N_PAIRS)]
    pick = lambda m: jnp.where(lo, m[:DN_DK], m[DN_DK:])
    rows_of = lambda c: slice(c * DN_CHUNK, (c + 1) * DN_CHUNK)
    sl = lambda name, c, b, p: pre[b][name][rows_of(c), p * PAIR:(p + 1) * PAIR]
    raws = []
    for c, b, p in probs:
        k_p = sl("k", c, b, p)
        k_rows = jnp.concatenate([jnp.where(lo, k_p, 0.0), jnp.where(lo, 0.0, k_p)], axis=0)
        raws.append(_mm_nt(jnp.concatenate([sl("kb", c, b, p), sl("q", c, b, p)], axis=0), k_rows))
    pws, ts, qks = [], [], []
    for (c, b, p), raw in zip(probs, raws):
        gcol = sl("gam", c, b, p)
        h0 = DN_HEADS + 2 * p
        gam_t = pre[b]["gam_t"]
        grow = jnp.concatenate([gam_t[h0:h0 + 1, rows_of(c)], gam_t[h0 + 1:h0 + 2, rows_of(c)]], axis=1)
        decay = jnp.exp(jnp.where(causal, gcol - grow, NEG_INF))
        a = jnp.where(strict, raw[:DN_CHUNK] * decay, 0.0)
        qks.append(jnp.where(causal, raw[DN_CHUNK:] * decay, 0.0))
        pws.append(-a)
        ts.append(eye - a)
    pws = [_mm(pw, _pair_diag(pw, lo)) for pw in pws]
    for _ in range(4):
        rs = [_mm(jnp.concatenate([pw, t], axis=0), _pair_diag(pw, lo)) for pw, t in zip(pws, ts)]
        pws = [r[:DN_CHUNK] for r in rs]
        ts = [t + r[DN_CHUNK:] for t, r in zip(ts, rs)]
    rs = [_mm(t, _pair_diag(pw, lo)) for pw, t in zip(pws, ts)]
    ts = [t + r for t, r in zip(ts, rs)]
    sols = [_mm(t, jnp.concatenate([_pair_diag(sl("vb", c, b, p), lo), _pair_diag(sl("wr", c, b, p), lo)],
                                   axis=1)) for (c, b, p), t in zip(probs, ts)]
    qkuws = [_mm(qk, jnp.concatenate([_pair_diag(s[:, :PAIR], lo), _pair_diag(s[:, PAIR:], lo)], axis=1))
             for qk, s in zip(qks, sols)]
    crosses, gls = [], []
    for (c, b, p), s in zip(probs, sols):
        last = (c + 1) * DN_CHUNK - 1
        gam_last = pre[b]["gam"][last:last + 1, p * PAIR:(p + 1) * PAIR]
        kd = sl("k", c, b, p) * jnp.exp(gam_last - sl("gam", c, b, p))
        crosses.append(_mm_tn(kd, s))
        gls.append(jnp.exp(gam_last))
    lhs = [jnp.concatenate([pick(cr[:, PAIR:]), sl("qg", c, b, p) - qkuw[:, PAIR:]], axis=0)
           for (c, b, p), cr, qkuw in zip(probs, crosses, qkuws)]

    o_rows = [[] for _ in range(nb)]
    per_chunk = nb * N_PAIRS
    for c in range(GDN_NC):
        sel = slice(c * per_chunk, (c + 1) * per_chunk)
        s_olds = [s_scr[b, p] for _, b, p in probs[sel]]
        rs = [_mm(l, _pair_diag(s_old, lo)) for l, s_old in zip(lhs[sel], s_olds)]
        o_pairs = [[] for _ in range(nb)]
        for (_, b, p), r, s_old, gl, cr, qkuw in zip(probs[sel], rs, s_olds, gls[sel], crosses[sel], qkuws[sel]):
            s_scr[b, p] = gl * s_old - r[:DN_DK] + pick(cr[:, :PAIR])
            o_pairs[b].append(r[DN_DK:] + qkuw[:, :PAIR])
        for b in range(nb):
            o_rows[b].append(jnp.concatenate(o_pairs[b], axis=1))

    o_all = jnp.concatenate([jnp.concatenate(rows, axis=0) for rows in o_rows], axis=0)
    inv_rms = lax.rsqrt(_head_sums(o_all * o_all, hsum) * (1.0 / DN_DV) + EPS)
    for b in range(nb):
        rows = slice(b * GDN_TB, (b + 1) * GDN_TB)
        o_ref[b] = (o_all[rows] * inv_rms[rows] * dnx_ref[...] * _silu_tanh(dz_ref[b])).astype(o_ref.dtype)

    @pl.when(i == pl.num_programs(0) - 1)
    def _():
        for b in range(nb):
            for p in range(N_PAIRS):
                s_p = s_scr[b, p]
                s_out_ref[b, 2 * p] = s_p[:, :DN_DV]
                s_out_ref[b, 2 * p + 1] = s_p[:, DN_DV:]


def _gdn_consts():
    lane = np.arange(DN_WIDTH)
    pl_lane = np.arange(PAIR)
    hsum = (pl_lane[:, None] // DN_DV == pl_lane[None, :] // DN_DV)
    src = np.arange(LANES)
    expb = (src[:, None] == lane[None, :] // DN_DV)
    expg = (src[:, None] == DN_HEADS + lane[None, :] // DN_DV)
    tok = np.arange(GDN_TB)
    ltri = np.logical_and(tok[:, None] >= tok[None, :],
                          tok[:, None] // DN_CHUNK == tok[None, :] // DN_CHUNK)
    as_bf16 = lambda m: jnp.asarray(m.astype(np.float32), dtype=BF16)
    return as_bf16(hsum), as_bf16(expb), as_bf16(expg), as_bf16(ltri)


def _gdn_prompt(xc, dz, ba, alog, dtb, dnx, batch, seq):
    nt = seq // GDN_TB
    hsum, expb, expg, ltri = _gdn_consts()
    row = lambda n: pl.BlockSpec((batch, GDN_TB, n), lambda i: (0, i, 0))
    full = lambda a: pl.BlockSpec(a.shape, lambda i: (0,) * a.ndim)
    consts = (alog, dtb, dnx, hsum, expb, expg, ltri)
    as3d = lambda a: a.reshape(batch, seq, a.shape[-1])
    o, s = pl.pallas_call(
        _gdn_prompt_kernel,
        grid=(nt,),
        in_specs=[row(CONV_CH), row(DN_WIDTH), row(LANES)] + [full(a) for a in consts],
        out_specs=[row(DN_WIDTH),
                   pl.BlockSpec((batch, DN_HEADS, DN_DK, DN_DV), lambda i: (0, 0, 0, 0))],
        out_shape=[jax.ShapeDtypeStruct((batch, seq, DN_WIDTH), BF16),
                   jax.ShapeDtypeStruct((batch, DN_HEADS, DN_DK, DN_DV), F32)],
        scratch_shapes=[pltpu.VMEM((batch, N_PAIRS, DN_DK, PAIR), F32)],
        compiler_params=_params("arbitrary"),
        name="gdn_prompt",
    )(as3d(xc), as3d(dz), as3d(ba), *consts)
    return o.reshape(batch * seq, DN_WIDTH), s


def _gdn_sample_front_kernel(xc_ref, dz_ref, ba_ref, sc_ref, cw_ref, alog_ref, dtb_ref, hsum_ref,
                             q_ref, k_ref, v_ref, dz_t_ref, gates_ref):
    xc = xc_ref[...]
    y = sc_ref[0] * cw_ref[0:1, :]
    y = y + sc_ref[1] * cw_ref[1:2, :]
    y = y + sc_ref[2] * cw_ref[2:3, :]
    y = _silu(y + xc * cw_ref[3:4, :])
    hsum = hsum_ref[...]
    q = y[:, :DN_WIDTH]
    k = y[:, DN_WIDTH:2 * DN_WIDTH]
    q = q * lax.rsqrt(_mm_sel_rhs(q * q, hsum) + EPS) * (DN_DK ** -0.5)
    k = k * lax.rsqrt(_mm_sel_rhs(k * k, hsum) + EPS)
    beta_c, g_c = _gdn_gates(ba_ref[...], alog_ref[...], dtb_ref[...])
    q_ref[...] = q.T
    k_ref[...] = k.T
    v_ref[...] = y[:, 2 * DN_WIDTH:].T
    dz_t_ref[...] = dz_ref[...].T
    gates_ref[0:LANES, :] = beta_c.T
    gates_ref[LANES:, :] = jnp.exp(g_c).T


def _gdn_sample_step_kernel(q_ref, k_ref, v_ref, dz_ref, gates_ref, dn_ref, s_ref, o_ref, s_out_ref):
    h = pl.program_id(0)
    beta = gates_ref[pl.ds(h, 1), :]
    eg = gates_ref[pl.ds(LANES + DN_HEADS + h, 1), :]
    q, k, v = q_ref[...], k_ref[...], v_ref[...]
    w = (k * beta) * eg
    qg = q * eg
    ws = jnp.zeros(v.shape, F32)
    qs = jnp.zeros(v.shape, F32)
    for dk in range(DN_DK):
        s_dk = s_ref[0, dk]
        ws = ws + w[dk:dk + 1, :] * s_dk
        qs = qs + qg[dk:dk + 1, :] * s_dk
    v_new = v * beta - ws
    qk = jnp.sum(q * k, axis=0, keepdims=True)
    o = qs + qk * v_new
    for dk in range(DN_DK):
        s_out_ref[0, dk] = s_ref[0, dk] * eg + k[dk:dk + 1, :] * v_new
    o = o * lax.rsqrt(jnp.mean(o * o, axis=0, keepdims=True) + EPS) * dn_ref[...]
    o_ref[...] = o * _silu(dz_ref[...])


def _gdn_sample_lanes(xc, dz, ba, sconv_t, state_t, conv_w, alog, dtb, dn):
    nseq = xc.shape[0]
    assert nseq == LANES
    lane = np.arange(DN_WIDTH)
    hsum = jnp.asarray((lane[:, None] // DN_DV == lane[None, :] // DN_DV).astype(np.float32), dtype=BF16)
    full = lambda a: pl.BlockSpec(a.shape, lambda i: (0,) * a.ndim)
    cm = jax.ShapeDtypeStruct((DN_WIDTH, nseq), F32)
    front_in = (xc, dz, ba, sconv_t, conv_w, alog, dtb, hsum)
    q_t, k_t, v_t, dz_t, gates_t = pl.pallas_call(
        _gdn_sample_front_kernel,
        grid=(1,),
        in_specs=[full(a) for a in front_in],
        out_specs=[pl.BlockSpec((DN_WIDTH, nseq), lambda i: (0, 0))] * 4
                  + [pl.BlockSpec((2 * LANES, nseq), lambda i: (0, 0))],
        out_shape=[cm, cm, cm, cm, jax.ShapeDtypeStruct((2 * LANES, nseq), F32)],
        compiler_params=_params("arbitrary"),
        name="gdn_sample_front",
    )(*front_in)
    dn_b = jnp.broadcast_to(dn.reshape(DN_DV, 1), (DN_DV, nseq))
    head = pl.BlockSpec((DN_DK, nseq), lambda h: (h, 0))
    st = pl.BlockSpec((1, DN_DK, DN_DV, nseq), lambda h: (h, 0, 0, 0))
    return pl.pallas_call(
        _gdn_sample_step_kernel,
        grid=(DN_HEADS,),
        in_specs=[head, head, head, head, full(gates_t), full(dn_b), st],
        out_specs=[head, st],
        out_shape=[cm, jax.ShapeDtypeStruct(state_t.shape, F32)],
        compiler_params=_params("parallel"),
        name="gdn_sample_step",
    )(q_t, k_t, v_t, dz_t, gates_t, dn_b, state_t)


def _route(xn, wr):
    logits = jnp.dot(xn, wr, preferred_element_type=F32)
    lane = lax.broadcasted_iota(jnp.int32, logits.shape, 1).astype(F32)
    first_at = lambda hit: jnp.min(jnp.where(hit, lane, float(LANES)), axis=-1, keepdims=True)
    glog = jnp.where(lane < N_GROUPS, logits, NEG_INF)
    gmax = jnp.max(glog, axis=-1, keepdims=True)
    gsel = first_at(glog == gmax)
    pgsel = 1.0 / jnp.sum(jnp.exp(glog - gmax), axis=-1, keepdims=True)
    lo = ROUTER_OFF + gsel * EXPERTS_PER_GROUP
    in_group = jnp.logical_and(lane >= lo, lane < lo + EXPERTS_PER_GROUP)
    elog = jnp.where(in_group, logits, NEG_INF)
    m1 = jnp.max(elog, axis=-1, keepdims=True)
    i1 = first_at(elog == m1)
    z = jnp.sum(jnp.exp(elog - m1), axis=-1, keepdims=True)
    elog2 = jnp.where(lane == i1, NEG_INF, elog)
    m2 = jnp.max(elog2, axis=-1, keepdims=True)
    i2 = first_at(elog2 == m2)
    p1 = 1.0 / z
    p2 = jnp.exp(m2 - m1) / z
    tot = p1 + p2
    return lane, i1, i2, p1 / tot * pgsel, p2 / tot * pgsel


def _outproj(x_ref, oa_ref, od_ref, wo_ref):
    return x_ref[...] + _mm(oa_ref[...], wo_ref[:ATT_WIDTH, :]) + _mm(od_ref[...], wo_ref[ATT_WIDTH:, :])


def _outproj_router_kernel(x_ref, oa_ref, od_t_ref, wo_ref, g_ref, wr_ref, h_ref, xn_ref, gate_ref):
    h = (x_ref[...] + _mm(oa_ref[...], wo_ref[:ATT_WIDTH, :])
         + _mm(od_t_ref[...].T, wo_ref[ATT_WIDTH:, :]))
    h_ref[...] = h
    xn = _rmsnorm(h, g_ref[...]).astype(BF16)
    xn_ref[...] = xn
    lane, i1, i2, g1, g2 = _route(xn, wr_ref[...])
    gate_ref[...] = jnp.where(lane == i1, g1, 0.0) + jnp.where(lane == i2, g2, 0.0)


def _outproj_router(x, oa, od_t, wo, g, wr):
    t = x.shape[0]
    tm = t
    row = lambda n: pl.BlockSpec((tm, n), lambda i: (i, 0))
    full = lambda a: pl.BlockSpec(a.shape, lambda i: (0,) * a.ndim)
    return pl.pallas_call(
        _outproj_router_kernel,
        grid=(t // tm,),
        in_specs=[row(D_MODEL), row(ATT_WIDTH), full(od_t), full(wo), full(g), full(wr)],
        out_specs=[row(D_MODEL), row(D_MODEL), row(LANES)],
        out_shape=[jax.ShapeDtypeStruct((t, D_MODEL), F32), jax.ShapeDtypeStruct((t, D_MODEL), BF16),
                   jax.ShapeDtypeStruct((t, LANES), F32)],
        compiler_params=_params("parallel"),
        name="outproj_router",
    )(x, oa, od_t, wo, g, wr)


MOE_TM = 512
POS_TM = 1024
INFO_G1, INFO_G2, INFO_E1, INFO_E2 = 0, 1, 2, 3


def _moe_tiles(t):
    return (2 * t) // MOE_TM + N_EXPERTS


HALF = D_MODEL // 2
U32 = jnp.uint32


def _pack_rows(x):
    bits = lambda v: lax.bitcast_convert_type(v.astype(BF16).astype(F32), U32)
    return bits(x[:, HALF:]) | (bits(x[:, :HALF]) >> 16)


def _unpack_rows(w):
    lo = lax.bitcast_convert_type(w << 16, F32)
    hi = lax.bitcast_convert_type(w & jnp.uint32(0xFFFF0000), F32)
    return lo, hi


def _route_kernel(x_ref, oa_ref, od_ref, wo_ref, g_ref, wr_ref, h_ref, xn_ref, info_ref, cnt_ref, run_scr):
    h = _outproj(x_ref, oa_ref, od_ref, wo_ref)
    h_ref[...] = h
    xn = _rmsnorm(h, g_ref[...])
    xn_ref[...] = _pack_rows(xn)
    lane, i1, i2, g1, g2 = _route(xn.astype(BF16), wr_ref[...])
    info = jnp.where(lane == INFO_G1, g1, 0.0) + jnp.where(lane == INFO_G2, g2, 0.0)
    info = info + jnp.where(lane == INFO_E1, i1, 0.0) + jnp.where(lane == INFO_E2, i2, 0.0)
    info_ref[...] = info

    @pl.when(pl.program_id(0) == 0)
    def _():
        run_scr[...] = jnp.zeros(run_scr.shape, F32)
    picked = jnp.logical_or(lane == i1, lane == i2).astype(F32)
    run_scr[...] += jnp.sum(picked, axis=0, keepdims=True)
    cnt_ref[...] = run_scr[...]


def _route_sparse(x, oa, od, wo, g, wr):
    t = x.shape[0]
    tm = WIDE_TM
    row = lambda n: pl.BlockSpec((tm, n), lambda i: (i, 0))
    full = lambda a: pl.BlockSpec(a.shape, lambda i: (0,) * a.ndim)
    return pl.pallas_call(
        _route_kernel,
        grid=(t // tm,),
        in_specs=[row(D_MODEL), row(ATT_WIDTH), row(DN_WIDTH), full(wo), full(g), full(wr)],
        out_specs=[row(D_MODEL), row(HALF), row(LANES), pl.BlockSpec((1, LANES), lambda i: (0, 0))],
        out_shape=[jax.ShapeDtypeStruct((t, D_MODEL), F32), jax.ShapeDtypeStruct((t, HALF), U32),
                   jax.ShapeDtypeStruct((t, LANES), F32), jax.ShapeDtypeStruct((1, LANES), F32)],
        scratch_shapes=[pltpu.VMEM((1, LANES), F32)],
        compiler_params=_params("arbitrary"),
        name="route",
    )(x, oa, od, wo, g, wr)


def _positions_kernel(info_ref, cnt_ref, ltri_ref, utri_ref, pos_ref, run_scr, off_scr):
    info = info_ref[...]
    lane = lax.broadcasted_iota(jnp.int32, info.shape, 1).astype(F32)
    hit1 = lane == info[:, INFO_E1:INFO_E1 + 1]
    hit2 = lane == info[:, INFO_E2:INFO_E2 + 1]
    onehot = jnp.logical_or(hit1, hit2).astype(F32)

    @pl.when(pl.program_id(0) == 0)
    def _():
        ln = lax.broadcasted_iota(jnp.int32, cnt_ref.shape, 1)
        is_expert = jnp.logical_and(ln >= ROUTER_OFF, ln < ROUTER_OFF + N_EXPERTS)
        tiles = jnp.where(is_expert, jnp.maximum(jnp.floor((cnt_ref[...] + (MOE_TM - 1)) * (1.0 / MOE_TM)), 1.0), 0.0)
        off_scr[...] = MOE_TM * jnp.dot(tiles.astype(BF16), utri_ref[...], preferred_element_type=F32)
        run_scr[...] = jnp.zeros(run_scr.shape, F32)

    before = (jnp.dot(ltri_ref[...], onehot.astype(BF16), preferred_element_type=F32)
              + run_scr[...] + off_scr[...])
    pos1 = jnp.sum(jnp.where(hit1, before, 0.0), axis=-1, keepdims=True)
    pos2 = jnp.sum(jnp.where(hit2, before, 0.0), axis=-1, keepdims=True)
    both = jnp.where(lane == 0, pos1, 0.0) + jnp.where(lane == 1, pos2, 0.0)
    pos_ref[...] = both.T.astype(jnp.int32)
    run_scr[...] += jnp.sum(onehot, axis=0, keepdims=True)


def _positions(info, cnt):
    t = info.shape[0]
    tm = min(t, POS_TM)
    tok = np.arange(tm)
    ltri = jnp.asarray((tok[:, None] > tok[None, :]).astype(np.float32), dtype=BF16)
    ln = np.arange(LANES)
    utri = jnp.asarray((ln[:, None] < ln[None, :]).astype(np.float32), dtype=BF16)
    full = lambda a: pl.BlockSpec(a.shape, lambda i: (0,) * a.ndim)
    return pl.pallas_call(
        _positions_kernel,
        grid=(t // tm,),
        in_specs=[pl.BlockSpec((tm, LANES), lambda i: (i, 0)), full(cnt), full(ltri), full(utri)],
        out_specs=pl.BlockSpec((LANES, tm), lambda i: (0, i)),
        out_shape=jax.ShapeDtypeStruct((LANES, t), jnp.int32),
        scratch_shapes=[pltpu.VMEM((1, LANES), F32), pltpu.VMEM((1, LANES), F32)],
        compiler_params=_params("arbitrary"),
        name="positions",
    )(info, cnt, ltri, utri)


def _experts_kernel(te_ref, tv_ref, nt_ref, xs_ref, wg_hbm, wu_hbm, wd_hbm, xn_new_ref, gate_new_ref,
                    ys_ref, moe_new_ref, wg_s, wu_s, wd_s, wg_f, wu_f, wd_f, wsem):
    i = pl.program_id(0)
    used = i < nt_ref[0]
    expert = te_ref[i]

    def fetch(e):
        slot = e % 2
        return [pltpu.make_async_copy(src.at[e], dst.at[slot], wsem.at[slot, j])
                for j, (src, dst) in enumerate(((wg_hbm, wg_f), (wu_hbm, wu_f), (wd_hbm, wd_f)))]

    @pl.when(jnp.logical_or(i == 0, expert != te_ref[jnp.maximum(i - 1, 0)]))
    def _():
        @pl.when(i == 0)
        def _():
            for c in fetch(expert):
                c.start()
        for c in fetch(expert):
            c.wait()

        @pl.when(expert + 1 < N_EXPERTS)
        def _():
            for c in fetch(expert + 1):
                c.start()
        slot = expert % 2
        wg_s[...] = wg_f[slot].astype(BF16)
        wu_s[...] = wu_f[slot].astype(BF16)
        wd_s[...] = wd_f[slot].astype(BF16)
        xn = xn_new_ref[...]
        lane = lax.broadcasted_iota(jnp.int32, gate_new_ref.shape, 1)
        gate = jnp.sum(jnp.where(lane == expert + ROUTER_OFF, gate_new_ref[...], 0.0), axis=-1, keepdims=True)
        hg = jnp.dot(xn, wg_s[...], preferred_element_type=F32)
        hu = jnp.dot(xn, wu_s[...], preferred_element_type=F32)
        hm = _silu(hg) * hu * gate
        y = jnp.dot(hm.astype(BF16), wd_s[...], preferred_element_type=F32)

        @pl.when(i == 0)
        def _():
            moe_new_ref[...] = y

        @pl.when(i > 0)
        def _():
            moe_new_ref[...] += y

    @pl.when(used)
    def _():
        row = lax.broadcasted_iota(jnp.int32, xs_ref.shape, 0)
        x_lo, x_hi = _unpack_rows(jnp.where(row < tv_ref[i], xs_ref[...], jnp.uint32(0)))
        x_lo = x_lo.astype(BF16)
        x_hi = x_hi.astype(BF16)
        up = lambda w_s: (jnp.dot(x_lo, w_s[:HALF, :], preferred_element_type=F32)
                          + jnp.dot(x_hi, w_s[HALF:, :], preferred_element_type=F32))
        hm = (_silu_tanh(up(wg_s)) * up(wu_s)).astype(BF16)
        ys_ref[...] = _pack_rows(jnp.dot(hm, wd_s[...], preferred_element_type=F32))

    @pl.when(jnp.logical_not(used))
    def _():
        ys_ref[...] = jnp.zeros(ys_ref.shape, U32)


def _experts(xs, tile_expert, tile_valid, n_tiles, wg, wu, wd, xn_new, gate_new):
    max_tiles = xs.shape[0] // MOE_TM
    rows = pl.BlockSpec((MOE_TM, HALF), lambda i, te, tv, nt: (i, 0))
    hbm = pl.BlockSpec(memory_space=pl.ANY)
    full = lambda a: pl.BlockSpec(a.shape, lambda i, te, tv, nt: (0,) * a.ndim)
    return pl.pallas_call(
        _experts_kernel,
        grid_spec=pltpu.PrefetchScalarGridSpec(
            num_scalar_prefetch=3, grid=(max_tiles,),
            in_specs=[rows, hbm, hbm, hbm, full(xn_new), full(gate_new)],
            out_specs=[rows, pl.BlockSpec(xn_new.shape, lambda i, te, tv, nt: (0, 0))],
            scratch_shapes=[pltpu.VMEM((D_MODEL, D_EXPERT), BF16), pltpu.VMEM((D_MODEL, D_EXPERT), BF16),
                            pltpu.VMEM((D_EXPERT, D_MODEL), BF16),
                            pltpu.VMEM((2, D_MODEL, D_EXPERT), F32), pltpu.VMEM((2, D_MODEL, D_EXPERT), F32),
                            pltpu.VMEM((2, D_EXPERT, D_MODEL), F32), pltpu.SemaphoreType.DMA((2, 3))]),
        out_shape=[jax.ShapeDtypeStruct(xs.shape, U32), jax.ShapeDtypeStruct(xn_new.shape, F32)],
        compiler_params=_params("arbitrary"),
        name="experts",
    )(tile_expert, tile_valid, n_tiles, xs, wg, wu, wd, xn_new, gate_new)


SC_IDX = 128
SC_ROWS = 64
SC_WORKERS = 32
SC_GATHER_ROWS = 32
SC_GATHER_BUFS = 4


def _sc_mesh():
    return plsc.VectorSubcoreMesh(core_axis_name="c", subcore_axis_name="s")


def _sc_windows(t, fn):
    per_worker = t // SC_WORKERS
    worker = lax.axis_index(("c", "s"))

    @pl.loop(0, per_worker // SC_IDX)
    def _(w):
        fn(worker * per_worker + w * SC_IDX)


def _sc_scatter_rows(xn, pos1, pos2, n_rows):
    t, d = xn.shape
    assert t % (SC_WORKERS * SC_IDX) == 0
    idx_t = pltpu.VMEM((1, SC_IDX), jnp.int32)

    @pl.kernel(out_type=jax.ShapeDtypeStruct((n_rows, d), xn.dtype), mesh=_sc_mesh(),
               scratch_types=[idx_t, idx_t, pltpu.VMEM((SC_ROWS, d), xn.dtype)])
    def scatter(x_hbm, p1_hbm, p2_hbm, o_hbm, i1_v, i2_v, buf):
        def window(base):
            pltpu.sync_copy(p1_hbm.at[:, pl.ds(base, SC_IDX)], i1_v)
            pltpu.sync_copy(p2_hbm.at[:, pl.ds(base, SC_IDX)], i2_v)
            for k in range(SC_IDX // SC_ROWS):
                pltpu.sync_copy(x_hbm.at[pl.ds(base + k * SC_ROWS, SC_ROWS)], buf)
                pltpu.sync_copy(buf, o_hbm.at[i1_v.at[0, pl.ds(k * SC_ROWS, SC_ROWS)]])
                pltpu.sync_copy(buf, o_hbm.at[i2_v.at[0, pl.ds(k * SC_ROWS, SC_ROWS)]])
        _sc_windows(t, window)

    return scatter(xn, pos1.reshape(1, t), pos2.reshape(1, t))


def _sc_gather_rows(ys, pos1, pos2):
    t = pos1.shape[0]
    d = ys.shape[1]
    assert t % (SC_WORKERS * SC_IDX) == 0
    idx_t = pltpu.VMEM((1, SC_IDX), jnp.int32)
    out = jax.ShapeDtypeStruct((t, d), ys.dtype)

    nbuf, rows = SC_GATHER_BUFS, SC_GATHER_ROWS
    buf_t = pltpu.VMEM((rows, d), ys.dtype)

    @pl.kernel(out_type=(out, out), mesh=_sc_mesh(),
               scratch_types=[idx_t, idx_t] + [buf_t] * nbuf
                             + [pltpu.SemaphoreType.DMA((nbuf,)), pltpu.SemaphoreType.DMA((nbuf,))])
    def gather(y_hbm, p1_hbm, p2_hbm, o1_hbm, o2_hbm, i1_v, i2_v, *rest):
        bufs, (gsem, wsem) = rest[:nbuf], rest[nbuf:]

        def window(base):
            pltpu.sync_copy(p1_hbm.at[:, pl.ds(base, SC_IDX)], i1_v)
            pltpu.sync_copy(p2_hbm.at[:, pl.ds(base, SC_IDX)], i2_v)
            items = [(idx_v, o_hbm, k) for k in range(SC_IDX // rows)
                     for idx_v, o_hbm in ((i1_v, o1_hbm), (i2_v, o2_hbm))]
            n_items = len(items)

            def read(n):
                idx_v, _, k = items[n]
                return pltpu.make_async_copy(y_hbm.at[idx_v.at[0, pl.ds(k * rows, rows)]],
                                             bufs[n % nbuf], gsem.at[n % nbuf])

            def write(n):
                _, o_hbm, k = items[n]
                return pltpu.make_async_copy(bufs[n % nbuf], o_hbm.at[pl.ds(base + k * rows, rows)],
                                             wsem.at[n % nbuf])

            for n in range(min(nbuf - 1, n_items)):
                read(n).start()
            waited = 0
            for n in range(n_items):
                read(n).wait()
                write(n).start()
                ahead = n + nbuf - 1
                if ahead < n_items:
                    if n >= 1:
                        write(n - 1).wait()
                        waited = n
                    read(ahead).start()
            for n in range(waited, n_items):
                write(n).wait()
        _sc_windows(t, window)

    return gather(ys, pos1.reshape(1, t), pos2.reshape(1, t))


def _ple_sparse_kernel(h_ref, info_ref, y1_ref, y2_ref, p_ref, wpp_ref, wpg_ref, gp_ref, gf_ref, y_ref):
    info = info_ref[...]
    g1 = info[:, INFO_G1:INFO_G1 + 1]
    g2 = info[:, INFO_G2:INFO_G2 + 1]
    y1_lo, y1_hi = _unpack_rows(y1_ref[...])
    y2_lo, y2_hi = _unpack_rows(y2_ref[...])
    moe = jnp.concatenate([g1 * y1_lo + g2 * y2_lo, g1 * y1_hi + g2 * y2_hi], axis=1)
    h = h_ref[...] + moe
    hn = _rmsnorm(h, gp_ref[...])
    h = h + _mm(p_ref[...], wpp_ref[...]) * _sigmoid(_mm(hn, wpg_ref[...]))
    y_ref[...] = _rmsnorm(h, gf_ref[...])


def _ple_sparse(h, info, y1, y2, p, wpp, wpg, gp, gf):
    t = h.shape[0]
    tm = WIDE_TM
    row = lambda n: pl.BlockSpec((tm, n), lambda i: (i, 0))
    full = lambda a: pl.BlockSpec(a.shape, lambda i: (0,) * a.ndim)
    return pl.pallas_call(
        _ple_sparse_kernel,
        grid=(t // tm,),
        in_specs=[row(D_MODEL), row(LANES), row(HALF), row(HALF), row(PLE_DIM),
                  full(wpp), full(wpg), full(gp), full(gf)],
        out_specs=row(D_MODEL),
        out_shape=jax.ShapeDtypeStruct((t, D_MODEL), F32),
        compiler_params=_params("parallel"),
        name="ple_sparse",
    )(h, info, y1, y2, p, wpp, wpg, gp, gf)


def _tile_tables(cnt, max_tiles):
    tiles_e = jnp.maximum((cnt + (MOE_TM - 1)) // MOE_TM, 1)
    ends = jnp.cumsum(tiles_e)
    n_tiles = ends[-1]
    tile = jnp.arange(max_tiles, dtype=jnp.int32)
    idx = jnp.minimum(tile, n_tiles - 1)
    tile_expert = jnp.sum((idx[:, None] >= ends[None, :]).astype(jnp.int32), axis=1)
    mine = tile_expert[:, None] == jnp.arange(N_EXPERTS, dtype=jnp.int32)[None, :]
    of_mine = lambda v: jnp.sum(jnp.where(mine, v[None, :], 0), axis=1)
    valid = jnp.clip(of_mine(cnt) - (idx - of_mine(ends - tiles_e)) * MOE_TM, 0, MOE_TM)
    tile_valid = jnp.where(tile < n_tiles, valid, 0).astype(jnp.int32)
    return tile_expert, tile_valid, n_tiles.reshape(1)


def _ple_final_kernel(h_ref, m_ref, p_ref, wpp_ref, wpg_ref, gp_ref, gf_ref, y_ref):
    h = h_ref[...] + m_ref[...]
    hn = _rmsnorm(h, gp_ref[...])
    h = h + _mm(p_ref[...], wpp_ref[...]) * _sigmoid(_mm(hn, wpg_ref[...]))
    y_ref[...] = _rmsnorm(h, gf_ref[...])


def _ple_final(h, m, p, wpp, wpg, gp, gf):
    t = h.shape[0]
    tm = min(t, 256)
    row = lambda n: pl.BlockSpec((tm, n), lambda i: (i, 0))
    full = lambda a: pl.BlockSpec(a.shape, lambda i: (0,) * a.ndim)
    return pl.pallas_call(
        _ple_final_kernel,
        grid=(t // tm,),
        in_specs=[row(D_MODEL), row(D_MODEL), row(PLE_DIM), full(wpp), full(wpg), full(gp), full(gf)],
        out_specs=row(D_MODEL),
        out_shape=jax.ShapeDtypeStruct((t, D_MODEL), F32),
        compiler_params=_params("parallel"),
        name="ple_final",
    )(h, m, p, wpp, wpg, gp, gf)


def kernel(x_prompt, x_sample, p_prompt, p_sample, cache_k, cache_v, state_conv, state_S, rel_bias, norm_mix, w_in, att_sink, conv_w, dn_A_log, dn_dt_bias, dn_norm, w_out, norm_ffn, w_router_group, w_router_expert, w_gate, w_up, w_down, w_ple_proj, w_ple_gate, norm_ple, norm_final):
    batch, seq, _ = x_prompt.shape
    nseq = x_sample.shape[0]
    assert x_sample.shape[1] == 1 and norm_mix.shape[0] == 1 and cache_k.shape[2] == WINDOW
    assert seq % GDN_TB == 0 and seq % ATT_BLOCK == 0

    wi = w_in[0]
    o_db = ATT_COLS + CONV_CH
    w_in_re = (wi[:, :o_db].astype(BF16), wi[:, o_db + 2 * DN_HEADS:].astype(BF16),
               jnp.pad(wi[:, o_db:o_db + 2 * DN_HEADS], ((0, 0), (0, LANES - 2 * DN_HEADS))).astype(BF16))
    row = lambda a: a.reshape(1, -1).astype(F32)
    pad_lanes = lambda a, off: jnp.zeros((1, LANES), F32).at[0, off:off + a.shape[0]].set(a)
    alog = pad_lanes(dn_A_log[0], DN_HEADS)
    dtb = pad_lanes(dn_dt_bias[0], DN_HEADS)
    dnx = jnp.tile(dn_norm[0], DN_HEADS).reshape(1, DN_WIDTH)
    w_router = jnp.concatenate(
        [w_router_group[0], w_router_expert[0],
         jnp.zeros((D_MODEL, LANES - N_GROUPS - N_EXPERTS), F32)], axis=1).astype(BF16)
    wo = w_out[0].astype(BF16)
    wg, wu, wd = w_gate[0], w_up[0], w_down[0]
    wpp, wpg = w_ple_proj[0].astype(BF16), w_ple_gate[0].astype(BF16)
    sink = att_sink[0]

    qi = np.arange(ATT_BLOCK)[:, None]
    kj = np.arange(2 * ATT_BLOCK)[None, :]
    bucket_p = jnp.asarray(_t5_bucket_np(qi + ATT_BLOCK - kj))
    bucket_s = jnp.asarray(_t5_bucket_np(WINDOW - np.arange(WINDOW)[None, :]))

    xp = x_prompt.reshape(batch * seq, D_MODEL)
    att_p, qkv_p, dz_p, ba_p, xc_tails = _inproj_conv(xp, row(norm_mix[0]), w_in_re, conv_w[0], seq)
    o_att_p = _attn_prompt(att_p, bucket_p, rel_bias, sink, batch, seq)
    o_dn_p, s_p = _gdn_prompt(qkv_p, dz_p, ba_p, alog, dtb, dnx, batch, seq)
    h1, xn2, info, cnt = _route_sparse(xp, o_att_p, o_dn_p, wo, row(norm_ffn[0]), w_router)
    pos = _positions(info, cnt)
    pos1, pos2 = pos[0], pos[1]
    max_tiles = _moe_tiles(batch * seq)
    cnt_e = cnt[0, ROUTER_OFF:ROUTER_OFF + N_EXPERTS].astype(jnp.int32)
    tile_expert, tile_valid, n_tiles = _tile_tables(cnt_e, max_tiles)
    xs_sorted = _sc_scatter_rows(xn2, pos1, pos2, max_tiles * MOE_TM)

    xs = x_sample.reshape(nseq, D_MODEL)
    att_s, xc_s, dz_s, ba_s = _inproj(xs, row(norm_mix[0]), w_in_re)
    ck_t = jnp.transpose(cache_k[0], (0, 2, 3, 1))
    cv_t = jnp.transpose(cache_v[0], (0, 2, 3, 1))
    o_att_s, ks_t, vs_t = _attn_sample(att_s, ck_t, cv_t, bucket_s, rel_bias, sink)
    sconv_t = jnp.swapaxes(state_conv[0], 0, 1)
    o_dn_s_t, s_s_t = _gdn_sample_lanes(xc_s, dz_s, ba_s, sconv_t, jnp.transpose(state_S[0], (1, 2, 3, 0)),
                                        conv_w[0], alog, dtb, dn_norm[0])
    s_s = jnp.transpose(s_s_t, (3, 0, 1, 2))

    h1_s, xn2_s, gates_s = _outproj_router(xs, o_att_s, o_dn_s_t, wo, row(norm_ffn[0]), w_router)

    ys, moe_s = _experts(xs_sorted, tile_expert, tile_valid, n_tiles, wg, wu, wd, xn2_s, gates_s)
    y1, y2 = _sc_gather_rows(ys, pos1, pos2)
    y_s = _ple_final(h1_s, moe_s, p_sample[0].reshape(nseq, PLE_DIM), wpp, wpg, row(norm_ple[0]),
                     row(norm_final))
    y_p = _ple_sparse(h1, info, y1, y2, p_prompt[0].reshape(batch * seq, PLE_DIM),
                      wpp, wpg, row(norm_ple[0]), row(norm_final))

    att_p3 = att_p.reshape(batch, seq, ATT_COLS)
    kv_shape = (1, batch, WINDOW, ATT_KV_HEADS, HEAD_DIM)
    k_p = att_p3[:, seq - WINDOW:, ATT_WIDTH:ATT_WIDTH + KV_WIDTH].reshape(kv_shape)
    v_p = att_p3[:, seq - WINDOW:, ATT_WIDTH + KV_WIDTH:].reshape(kv_shape)
    conv_p = xc_tails.reshape(batch, -1, TAIL, CONV_CH)[:, -1, TAIL - (CONV_WIDTH - 1):][None]
    k_s = jnp.transpose(ks_t, (0, 3, 1, 2))[None]
    v_s = jnp.transpose(vs_t, (0, 3, 1, 2))[None]
    conv_s = jnp.concatenate([state_conv[0][:, 1:], xc_s[:, None, :]], axis=1)[None]
    return (y_p.reshape(batch, seq, D_MODEL), y_s.reshape(nseq, 1, D_MODEL),
            k_p, v_p, conv_p, s_p[None], k_s, v_s, conv_s, s_s[None])
```

```python
import functools
import math

import numpy as np
import jax
import jax.numpy as jnp
from jax import lax
from jax.experimental import pallas as pl
from jax.experimental.pallas import tpu as pltpu
from jax.experimental.pallas import tpu_sc as plsc

F32 = jnp.float32
BF16 = jnp.bfloat16

D_MODEL = 1024
ATT_HEADS = 8
ATT_KV_HEADS = 2
HEAD_DIM = 64
GQA = ATT_HEADS // ATT_KV_HEADS
WINDOW = 128
ATT_BLOCK = 128
N_BUCKETS = 32
DN_HEADS = 8
DN_DK = 64
DN_DV = 64
CONV_WIDTH = 4
DN_CHUNK = 64
ATT_WIDTH = ATT_HEADS * HEAD_DIM
KV_WIDTH = ATT_KV_HEADS * HEAD_DIM
DN_WIDTH = DN_HEADS * DN_DV
CONV_CH = 3 * DN_WIDTH
N_GROUPS = 4
EXPERTS_PER_GROUP = 8
N_EXPERTS = N_GROUPS * EXPERTS_PER_GROUP
D_EXPERT = 256
PLE_DIM = 256
EPS = 1e-6
NEG_INF = float("-inf")

ATT_COLS = ATT_WIDTH + 2 * KV_WIDTH
LANES = 128
ROUTER_OFF = N_GROUPS
VMEM_LIMIT = 48 * 1024 * 1024
ROW_TM = 512
WIDE_TM = 1024


def _params(*sem):
    return pltpu.CompilerParams(dimension_semantics=sem, vmem_limit_bytes=VMEM_LIMIT)


def _mm(a, b):
    return jnp.dot(a.astype(BF16), b.astype(BF16), preferred_element_type=F32)


def _mm_nt(a, b):
    return lax.dot_general(a.astype(BF16), b.astype(BF16), (((1,), (1,)), ((), ())),
                           preferred_element_type=F32)


def _mm_tn(a, b):
    return lax.dot_general(a.astype(BF16), b.astype(BF16), (((0,), (0,)), ((), ())),
                           preferred_element_type=F32)


def _split3(x):
    h1 = x.astype(BF16)
    r1 = x - h1.astype(F32)
    h2 = r1.astype(BF16)
    h3 = (r1 - h2.astype(F32)).astype(BF16)
    return h1, h2, h3


def _mm_sel_rhs(x, sel):
    h1, h2, h3 = _split3(x)
    d = lambda h: jnp.dot(h, sel, preferred_element_type=F32)
    return d(h1) + d(h2) + d(h3)


def _mm_sel_lhs(sel, x):
    h1, h2, h3 = _split3(x)
    d = lambda h: jnp.dot(sel, h, preferred_element_type=F32)
    return d(h1) + d(h2) + d(h3)


def _sigmoid(x):
    return 1.0 / (1.0 + jnp.exp(-x))


def _silu(x):
    return x * _sigmoid(x)


def _silu_tanh(x):
    return x * (0.5 * jnp.tanh(0.5 * x) + 0.5)


def _softplus(x):
    return jnp.maximum(x, 0.0) + jnp.log1p(jnp.exp(-jnp.abs(x)))


def _rmsnorm(x, g):
    return x * lax.rsqrt(jnp.mean(x * x, axis=-1, keepdims=True) + EPS) * g


def _t5_bucket_np(dist):
    max_exact = N_BUCKETS // 2
    d = np.maximum(dist, 0)
    ratio = (np.log(np.maximum(d, 1).astype(np.float32) / np.float32(max_exact))
             / np.float32(math.log(WINDOW / max_exact))).astype(np.float32)
    large = np.minimum(max_exact + (ratio * np.float32(N_BUCKETS - max_exact)).astype(np.int32),
                       N_BUCKETS - 1)
    return np.where(d < max_exact, d, large).astype(np.int32)


def _bias_lookup(bucket, rb_ref, h):
    acc = jnp.zeros(bucket.shape, F32)
    for t in range(N_BUCKETS):
        acc = jnp.where(bucket == t, rb_ref[t, h], acc)
    return acc


def _inproj_kernel(x_ref, g_ref, wa_ref, wz_ref, wb_ref, att_ref, xc_ref, dz_ref, ba_ref):
    xn = _rmsnorm(x_ref[...], g_ref[...]).astype(BF16)
    att_ref[...] = jnp.dot(xn, wa_ref[:, :ATT_COLS], preferred_element_type=F32)
    xc_ref[...] = jnp.dot(xn, wa_ref[:, ATT_COLS:], preferred_element_type=F32)
    dz_ref[...] = jnp.dot(xn, wz_ref[...], preferred_element_type=F32)
    ba_ref[...] = jnp.dot(xn, wb_ref[...], preferred_element_type=F32)


def _inproj(x, g, w):
    t = x.shape[0]
    tm = min(t, ROW_TM)
    row = lambda n: pl.BlockSpec((tm, n), lambda i: (i, 0))
    full = lambda a: pl.BlockSpec(a.shape, lambda i: (0,) * a.ndim)
    return pl.pallas_call(
        _inproj_kernel,
        grid=(t // tm,),
        in_specs=[row(D_MODEL), full(g)] + [full(a) for a in w],
        out_specs=[row(ATT_COLS), row(CONV_CH), row(DN_WIDTH), row(LANES)],
        out_shape=[jax.ShapeDtypeStruct((t, n), F32) for n in (ATT_COLS, CONV_CH, DN_WIDTH, LANES)],
        compiler_params=_params("parallel"),
        name="inproj",
    )(x, g, *w)


TAIL = 8
PAIR = 2 * DN_DK
N_PAIRS = DN_WIDTH // PAIR


def _head_sums(z, pair_ones):
    hi = z.astype(BF16)
    lw = (z - hi.astype(F32)).astype(BF16)
    d = lambda a, p: jnp.dot(a[:, p * PAIR:(p + 1) * PAIR], pair_ones, preferred_element_type=F32)
    return jnp.concatenate([d(hi, p) + d(lw, p) for p in range(N_PAIRS)], axis=1)


def _inproj_conv_kernel(x_ref, g_ref, wa_ref, wz_ref, wb_ref, cw_ref, ones_ref,
                        att_ref, qkv_ref, dz_ref, ba_ref, tail_ref, xp_scr, *, tiles_per_seq):
    tm = x_ref.shape[0]

    @pl.when(pl.program_id(0) % tiles_per_seq == 0)
    def _():
        xp_scr[...] = jnp.zeros((TAIL, CONV_CH), F32)

    xn = _rmsnorm(x_ref[...], g_ref[...]).astype(BF16)
    xc = jnp.dot(xn, wa_ref[:, ATT_COLS:], preferred_element_type=F32)
    att_ref[...] = jnp.dot(xn, wa_ref[:, :ATT_COLS], preferred_element_type=F32)
    dz_ref[...] = jnp.dot(xn, wz_ref[...], preferred_element_type=F32)
    ba_ref[...] = jnp.dot(xn, wb_ref[...], preferred_element_type=F32)

    head = jnp.concatenate([xp_scr[...], xc[:TAIL, :]], axis=0)

    def shifted(j):
        return jnp.concatenate([head[TAIL - j:2 * TAIL - j, :], pltpu.roll(xc, j, axis=0)[TAIL:, :]], axis=0)

    y = shifted(3) * cw_ref[0:1, :]
    y = y + shifted(2) * cw_ref[1:2, :]
    y = y + shifted(1) * cw_ref[2:3, :]
    y = y + xc * cw_ref[3:4, :]
    tail = xc[tm - TAIL:, :]
    xp_scr[...] = tail
    tail_ref[0] = tail
    y = _silu_tanh(y)
    q = y[:, :DN_WIDTH]
    k = y[:, DN_WIDTH:2 * DN_WIDTH]
    inv_norm = lax.rsqrt(_head_sums(jnp.concatenate([q * q, k * k], axis=0), ones_ref[...]) + EPS)
    qkv_ref[:, :DN_WIDTH] = q * inv_norm[:tm] * (DN_DK ** -0.5)
    qkv_ref[:, DN_WIDTH:2 * DN_WIDTH] = k * inv_norm[tm:]
    qkv_ref[:, 2 * DN_WIDTH:] = y[:, 2 * DN_WIDTH:]


def _pair_ones():
    lane = np.arange(PAIR)
    return jnp.asarray((lane[:, None] // DN_DV == lane[None, :] // DN_DV).astype(np.float32), dtype=BF16)


def _inproj_conv(x, g, w, conv_w, seq):
    t = x.shape[0]
    tm = ROW_TM
    assert seq % tm == 0
    ones = _pair_ones()
    row = lambda n: pl.BlockSpec((tm, n), lambda i: (i, 0))
    full = lambda a: pl.BlockSpec(a.shape, lambda i: (0,) * a.ndim)
    return pl.pallas_call(
        functools.partial(_inproj_conv_kernel, tiles_per_seq=seq // tm),
        grid=(t // tm,),
        in_specs=[row(D_MODEL), full(g)] + [full(a) for a in w] + [full(conv_w), full(ones)],
        out_specs=[row(ATT_COLS), row(CONV_CH), row(DN_WIDTH), row(LANES),
                   pl.BlockSpec((1, TAIL, CONV_CH), lambda i: (i, 0, 0))],
        out_shape=[jax.ShapeDtypeStruct((t, n), F32) for n in (ATT_COLS, CONV_CH, DN_WIDTH, LANES)]
                  + [jax.ShapeDtypeStruct((t // tm, TAIL, CONV_CH), F32)],
        scratch_shapes=[pltpu.VMEM((TAIL, CONV_CH), F32)],
        compiler_params=_params("arbitrary"),
        name="inproj_conv",
    )(x, g, *w, conv_w, ones)


GROUP_ROWS = GQA * ATT_BLOCK


def _attn_prompt_kernel(cur_ref, prev_ref, bucket_ref, rb_ref, sink_ref, o_ref, bias_scr, sink_scr):
    i = pl.program_id(0)
    nseq = cur_ref.shape[0]

    @pl.when(i == 0)
    def _():
        qi = lax.broadcasted_iota(jnp.int32, (ATT_BLOCK, 2 * ATT_BLOCK), 0)
        kj = lax.broadcasted_iota(jnp.int32, (ATT_BLOCK, 2 * ATT_BLOCK), 1)
        dist = qi + ATT_BLOCK - kj
        band = jnp.logical_and(dist >= 0, dist < WINDOW)
        bucket = bucket_ref[...]
        hrow = lax.broadcasted_iota(jnp.int32, (GROUP_ROWS, 1), 0) // ATT_BLOCK
        for g in range(ATT_KV_HEADS):
            sink_col = jnp.zeros((GROUP_ROWS, 1), F32)
            for hh in range(GQA):
                h = g * GQA + hh
                bias = jnp.where(band, _bias_lookup(bucket, rb_ref, h), NEG_INF)
                bias_scr[0, g, hh * ATT_BLOCK:(hh + 1) * ATT_BLOCK, :] = bias
                bias_scr[1, g, hh * ATT_BLOCK:(hh + 1) * ATT_BLOCK, :] = jnp.where(kj >= ATT_BLOCK, bias, NEG_INF)
                sink_col = jnp.where(hrow == hh, sink_ref[h], sink_col)
            sink_scr[g] = sink_col

    first = (i == 0).astype(jnp.int32)
    probs = [(b, g) for b in range(nseq) for g in range(ATT_KV_HEADS)]
    scores = []
    for b, g in probs:
        cur = cur_ref[b]
        prev = prev_ref[b]
        q = jnp.concatenate([cur[:, (g * GQA + hh) * HEAD_DIM:(g * GQA + hh + 1) * HEAD_DIM]
                             for hh in range(GQA)], axis=0) * (HEAD_DIM ** -0.5)
        kcol = slice(ATT_WIDTH + g * HEAD_DIM, ATT_WIDTH + (g + 1) * HEAD_DIM)
        k2 = jnp.concatenate([prev[:, kcol], cur[:, kcol]], axis=0)
        scores.append(_mm_nt(q, k2) + bias_scr[first, g])
    probs_p, dens = [], []
    for (b, g), s in zip(probs, scores):
        sink = sink_scr[g]
        m = jnp.maximum(jnp.max(s, axis=-1, keepdims=True), sink)
        p = jnp.exp(s - m)
        dens.append(jnp.sum(p, axis=-1, keepdims=True) + jnp.exp(sink - m))
        probs_p.append(p.astype(BF16))
    outs = {}
    for (b, g), p, den in zip(probs, probs_p, dens):
        vcol = slice(ATT_WIDTH + KV_WIDTH + g * HEAD_DIM, ATT_WIDTH + KV_WIDTH + (g + 1) * HEAD_DIM)
        v2 = jnp.concatenate([prev_ref[b][:, vcol], cur_ref[b][:, vcol]], axis=0)
        outs[b, g] = _mm(p, v2) / den
    for b in range(nseq):
        o_ref[b] = jnp.concatenate([outs[b, g][hh * ATT_BLOCK:(hh + 1) * ATT_BLOCK, :]
                                    for g in range(ATT_KV_HEADS) for hh in range(GQA)],
                                   axis=1).astype(o_ref.dtype)


def _attn_prompt(att, bucket, rel_bias, sink, batch, seq):
    nb = seq // ATT_BLOCK
    smem = pl.BlockSpec(memory_space=pltpu.SMEM)
    att3 = att.reshape(batch, seq, ATT_COLS)
    out = pl.pallas_call(
        _attn_prompt_kernel,
        grid=(nb,),
        in_specs=[
            pl.BlockSpec((batch, ATT_BLOCK, ATT_COLS), lambda i: (0, i, 0)),
            pl.BlockSpec((batch, ATT_BLOCK, ATT_COLS), lambda i: (0, jnp.maximum(i - 1, 0), 0)),
            pl.BlockSpec(bucket.shape, lambda i: (0, 0)),
            smem, smem,
        ],
        out_specs=pl.BlockSpec((batch, ATT_BLOCK, ATT_WIDTH), lambda i: (0, i, 0)),
        out_shape=jax.ShapeDtypeStruct((batch, seq, ATT_WIDTH), BF16),
        scratch_shapes=[pltpu.VMEM((2, ATT_KV_HEADS, GROUP_ROWS, 2 * ATT_BLOCK), F32),
                        pltpu.VMEM((ATT_KV_HEADS, GROUP_ROWS, 1), F32)],
        compiler_params=_params("arbitrary"),
        name="attn_prompt",
    )(att3, att3, bucket, rel_bias, sink)
    return out.reshape(batch * seq, ATT_WIDTH)


ATT_S_BB = 8


def _attn_sample_kernel(att_ref, ck_ref, cv_ref, bucket_ref, rb_ref, sink_ref, o_ref, ks_ref, vs_ref,
                        bias_scr, col_scr):
    hrow = lax.broadcasted_iota(jnp.int32, (ATT_HEADS, LANES), 0)
    lane = lax.broadcasted_iota(jnp.int32, (ATT_HEADS, LANES), 1)

    last = (lax.broadcasted_iota(jnp.int32, (3, WINDOW), 1) == WINDOW - 1).astype(BF16)
    is_last = lax.broadcasted_iota(jnp.int32, (KV_WIDTH, WINDOW), 1) == WINDOW - 1

    def shifted(cache_t, new_row):
        pieces = jnp.concatenate([p.astype(F32) for p in _split3(new_row)], axis=0).astype(BF16)
        col = lax.dot_general(pieces, last, (((0,), (0,)), ((), ())), preferred_element_type=F32)
        out = jnp.where(is_last, col, pltpu.roll(cache_t, WINDOW - 1, axis=1))
        return out.reshape(ATT_KV_HEADS, HEAD_DIM, WINDOW)

    for b in range(ATT_S_BB):
        row = att_ref[b:b + 1, :]
        ks_ref[b] = shifted(ck_ref[b].reshape(KV_WIDTH, WINDOW), row[:, ATT_WIDTH:ATT_WIDTH + KV_WIDTH])
        vs_ref[b] = shifted(cv_ref[b].reshape(KV_WIDTH, WINDOW), row[:, ATT_WIDTH + KV_WIDTH:])

    @pl.when(pl.program_id(0) == 0)
    def _():
        bucket = jnp.broadcast_to(bucket_ref[...], (ATT_HEADS, LANES))
        bias = jnp.zeros((ATT_HEADS, LANES), F32)
        cols = jnp.zeros((ATT_HEADS, LANES), F32)
        for h in range(ATT_HEADS):
            bias = jnp.where(hrow == h, _bias_lookup(bucket, rb_ref, h), bias)
            cols = jnp.where(jnp.logical_and(hrow == h, lane == 0), sink_ref[h], cols)
            cols = jnp.where(jnp.logical_and(hrow == h, lane == 1), rb_ref[0, h], cols)
        bias_scr[...] = jnp.where(lane >= 1, bias, NEG_INF)
        col_scr[...] = cols

    bias_c = bias_scr[...]
    sink = col_scr[:, 0:1]
    bias_n = col_scr[:, 1:2]
    same_group = (hrow // GQA) == (lane // HEAD_DIM)
    low_group = lax.broadcasted_iota(jnp.int32, (ATT_HEADS, HEAD_DIM), 0) < GQA
    rnd = lambda a: a.astype(BF16).astype(F32)
    seqs = range(ATT_S_BB)
    rows = [att_ref[b:b + 1, :] for b in seqs]
    q_bds = []
    for row in rows:
        q = row[:, :ATT_WIDTH] * (HEAD_DIM ** -0.5)
        qh = jnp.concatenate([q[:, h * HEAD_DIM:(h + 1) * HEAD_DIM] for h in range(ATT_HEADS)], axis=0)
        q_bds.append(jnp.where(same_group, jnp.concatenate([qh, qh], axis=1), 0.0))
    kv_t = lambda ref, b: ref[b].reshape(KV_WIDTH, WINDOW)
    s_cs = [_mm(q_bd, kv_t(ck_ref, b)) + bias_c for b, q_bd in zip(seqs, q_bds)]
    prs, pns = [], []
    for row, q_bd, s_c in zip(rows, q_bds, s_cs):
        kn = row[:, ATT_WIDTH:ATT_WIDTH + KV_WIDTH]
        s_n = jnp.sum(rnd(q_bd) * rnd(kn), axis=-1, keepdims=True) + bias_n
        m = jnp.maximum(jnp.maximum(jnp.max(s_c, axis=-1, keepdims=True), s_n), sink)
        p_c = jnp.exp(s_c - m)
        p_n = jnp.exp(s_n - m)
        den = jnp.sum(p_c, axis=-1, keepdims=True) + p_n + jnp.exp(sink - m)
        prs.append(p_c / den)
        pns.append(p_n / den)
    pvs = [_mm_nt(pr, kv_t(cv_ref, b)) for b, pr in zip(seqs, prs)]
    for b, row, pv, pn in zip(seqs, rows, pvs, pns):
        vn = row[:, ATT_WIDTH + KV_WIDTH:]
        o_full = pv + rnd(pn) * rnd(vn)
        o_sel = jnp.where(low_group, o_full[:, :HEAD_DIM], o_full[:, HEAD_DIM:])
        o_ref[b:b + 1, :] = jnp.concatenate([o_sel[h:h + 1, :] for h in range(ATT_HEADS)], axis=1)


def _attn_sample(att, ck, cv, bucket, rel_bias, sink):
    nseq = att.shape[0]
    smem = pl.BlockSpec(memory_space=pltpu.SMEM)
    cache = pl.BlockSpec((ATT_S_BB, ATT_KV_HEADS, HEAD_DIM, WINDOW), lambda i: (i, 0, 0, 0))
    return pl.pallas_call(
        _attn_sample_kernel,
        grid=(nseq // ATT_S_BB,),
        in_specs=[pl.BlockSpec((ATT_S_BB, ATT_COLS), lambda i: (i, 0)), cache, cache,
                  pl.BlockSpec(bucket.shape, lambda i: (0, 0)), smem, smem],
        out_specs=[pl.BlockSpec((ATT_S_BB, ATT_WIDTH), lambda i: (i, 0)), cache, cache],
        out_shape=[jax.ShapeDtypeStruct((nseq, ATT_WIDTH), F32),
                   jax.ShapeDtypeStruct(ck.shape, F32), jax.ShapeDtypeStruct(cv.shape, F32)],
        scratch_shapes=[pltpu.VMEM((ATT_HEADS, LANES), F32), pltpu.VMEM((ATT_HEADS, LANES), F32)],
        compiler_params=_params("arbitrary"),
        name="attn_sample",
    )(att, ck, cv, bucket, rel_bias, sink)


GDN_TB = 128
GDN_NC = GDN_TB // DN_CHUNK


def _gdn_gates(ba, alog, dtb):
    beta = _sigmoid(ba)
    g = -jnp.exp(alog) * _softplus(ba + dtb)
    return beta, g


def _pair_diag(x, lo):
    xb = x.astype(BF16)
    zero = jnp.zeros_like(xb)
    return jnp.concatenate([jnp.where(lo, xb, zero), jnp.where(lo, zero, xb)], axis=0)


def _gdn_prompt_kernel(qkv_ref, dz_ref, ba_ref, alog_ref, dtb_ref, dnx_ref,
                       hsum_ref, expb_ref, expg_ref, ltri_ref,
                       o_ref, s_out_ref, s_scr):
    i = pl.program_id(0)
    nb = qkv_ref.shape[0]

    @pl.when(i == 0)
    def _():
        s_scr[...] = jnp.zeros(s_scr.shape, F32)

    hsum = hsum_ref[...]
    ri = lax.broadcasted_iota(jnp.int32, (DN_CHUNK, PAIR), 0)
    ci = lax.broadcasted_iota(jnp.int32, (DN_CHUNK, PAIR), 1)
    lo = ci < DN_DK
    cj = jnp.where(lo, ci, ci - DN_DK)
    causal = ri >= cj
    strict = ri > cj
    eye = (ri == cj).astype(F32)

    def sel2(x, m):
        hi = x.astype(BF16)
        lw = (x - hi.astype(F32)).astype(BF16)
        return (jnp.dot(hi, m, preferred_element_type=F32) + jnp.dot(lw, m, preferred_element_type=F32))

    pre = []
    for b in range(nb):
        q = qkv_ref[b, :, :DN_WIDTH]
        k = qkv_ref[b, :, DN_WIDTH:2 * DN_WIDTH]
        v = qkv_ref[b, :, 2 * DN_WIDTH:]
        beta_c, g_c = _gdn_gates(ba_ref[b], alog_ref[...], dtb_ref[...])
        beta = sel2(beta_c, expb_ref[...])
        gam_c = _mm_sel_lhs(ltri_ref[...], g_c)
        gam = _mm_sel_rhs(gam_c, expg_ref[...])
        gam_t = gam_c.T
        kb = k * beta
        egam = jnp.exp(gam)
        pre.append(dict(q=q, k=k, kb=kb, vb=v * beta, qg=q * egam, wr=kb * egam, gam=gam, gam_t=gam_t))

    probs = [(c, b, p) for c in range(GDN_NC) for b in range(nb) for p in range(N_PAIRS)]
    pick = lambda m: jnp.where(lo, m[:DN_DK], m[DN_DK:])
    rows_of = lambda c: slice(c * DN_CHUNK, (c + 1) * DN_CHUNK)
    sl = lambda name, c, b, p: pre[b][name][rows_of(c), p * PAIR:(p + 1) * PAIR]
    raws = []
    for c, b, p in probs:
        k_p = sl("k", c, b, p)
        k_rows = jnp.concatenate([jnp.where(lo, k_p, 0.0), jnp.where(lo, 0.0, k_p)], axis=0)
        raws.append(_mm_nt(jnp.concatenate([sl("kb", c, b, p), sl("q", c, b, p)], axis=0), k_rows))
    pws, ts, qks = [], [], []
    for (c, b, p), raw in zip(probs, raws):
        gcol = sl("gam", c, b, p)
        h0 = DN_HEADS + 2 * p
        gam_t = pre[b]["gam_t"]
        grow = jnp.concatenate([gam_t[h0:h0 + 1, rows_of(c)], gam_t[h0 + 1:h0 + 2, rows_of(c)]], axis=1)
        decay = jnp.exp(jnp.where(causal, gcol - grow, NEG_INF))
        a = jnp.where(strict, raw[:DN_CHUNK] * decay, 0.0)
        qks.append(jnp.where(causal, raw[DN_CHUNK:] * decay, 0.0))
        pws.append(-a)
        ts.append(eye - a)
    pws = [_mm(pw, _pair_diag(pw, lo)) for pw in pws]
    for _ in range(4):
        rs = [_mm(jnp.concatenate([pw, t], axis=0), _pair_diag(pw, lo)) for pw, t in zip(pws, ts)]
        pws = [r[:DN_CHUNK] for r in rs]
        ts = [t + r[DN_CHUNK:] for t, r in zip(ts, rs)]
    rs = [_mm(t, _pair_diag(pw, lo)) for pw, t in zip(pws, ts)]
    ts = [t + r for t, r in zip(ts, rs)]
    sols = [_mm(t, jnp.concatenate([_pair_diag(sl("vb", c, b, p), lo), _pair_diag(sl("wr", c, b, p), lo)],
                                   axis=1)) for (c, b, p), t in zip(probs, ts)]
    qkuws = [_mm(qk, jnp.concatenate([_pair_diag(s[:, :PAIR], lo), _pair_diag(s[:, PAIR:], lo)], axis=1))
             for qk, s in zip(qks, sols)]
    crosses, gls = [], []
    for (c, b, p), s in zip(probs, sols):
        last = (c + 1) * DN_CHUNK - 1
        gam_last = pre[b]["gam"][last:last + 1, p * PAIR:(p + 1) * PAIR]
        kd = sl("k", c, b, p) * jnp.exp(gam_last - sl("gam", c, b, p))
        crosses.append(_mm_tn(kd, s))
        gls.append(jnp.exp(gam_last))
    lhs = [jnp.concatenate([pick(cr[:, PAIR:]), sl("qg", c, b, p) - qkuw[:, PAIR:]], axis=0)
           for (c, b, p), cr, qkuw in zip(probs, crosses, qkuws)]

    o_rows = [[] for _ in range(nb)]
    per_chunk = nb * N_PAIRS
    for c in range(GDN_NC):
        sel = slice(c * per_chunk, (c + 1) * per_chunk)
        s_olds = [s_scr[b, p] for _, b, p in probs[sel]]
        rs = [_mm(l, _pair_diag(s_old, lo)) for l, s_old in zip(lhs[sel], s_olds)]
        o_pairs = [[] for _ in range(nb)]
        for (_, b, p), r, s_old, gl, cr, qkuw in zip(probs[sel], rs, s_olds, gls[sel], crosses[sel], qkuws[sel]):
            s_scr[b, p] = gl * s_old - r[:DN_DK] + pick(cr[:, :PAIR])
            o_pairs[b].append(r[DN_DK:] + qkuw[:, :PAIR])
        for b in range(nb):
            o_rows[b].append(jnp.concatenate(o_pairs[b], axis=1))

    o_all = jnp.concatenate([jnp.concatenate(rows, axis=0) for rows in o_rows], axis=0)
    inv_rms = lax.rsqrt(_head_sums(o_all * o_all, hsum) * (1.0 / DN_DV) + EPS)
    for b in range(nb):
        rows = slice(b * GDN_TB, (b + 1) * GDN_TB)
        o_ref[b] = (o_all[rows] * inv_rms[rows] * dnx_ref[...] * _silu_tanh(dz_ref[b])).astype(o_ref.dtype)

    @pl.when(i == pl.num_programs(0) - 1)
    def _():
        for b in range(nb):
            for p in range(N_PAIRS):
                s_p = s_scr[b, p]
                s_out_ref[b, 2 * p] = s_p[:, :DN_DV]
                s_out_ref[b, 2 * p + 1] = s_p[:, DN_DV:]


def _gdn_consts():
    lane = np.arange(DN_WIDTH)
    pl_lane = np.arange(PAIR)
    hsum = (pl_lane[:, None] // DN_DV == pl_lane[None, :] // DN_DV)
    src = np.arange(LANES)
    expb = (src[:, None] == lane[None, :] // DN_DV)
    expg = (src[:, None] == DN_HEADS + lane[None, :] // DN_DV)
    tok = np.arange(GDN_TB)
    ltri = np.logical_and(tok[:, None] >= tok[None, :],
                          tok[:, None] // DN_CHUNK == tok[None, :] // DN_CHUNK)
    as_bf16 = lambda m: jnp.asarray(m.astype(np.float32), dtype=BF16)
    return as_bf16(hsum), as_bf16(expb), as_bf16(expg), as_bf16(ltri)


def _gdn_prompt(xc, dz, ba, alog, dtb, dnx, batch, seq):
    nt = seq // GDN_TB
    hsum, expb, expg, ltri = _gdn_consts()
    row = lambda n: pl.BlockSpec((batch, GDN_TB, n), lambda i: (0, i, 0))
    full = lambda a: pl.BlockSpec(a.shape, lambda i: (0,) * a.ndim)
    consts = (alog, dtb, dnx, hsum, expb, expg, ltri)
    as3d = lambda a: a.reshape(batch, seq, a.shape[-1])
    o, s = pl.pallas_call(
        _gdn_prompt_kernel,
        grid=(nt,),
        in_specs=[row(CONV_CH), row(DN_WIDTH), row(LANES)] + [full(a) for a in consts],
        out_specs=[row(DN_WIDTH),
                   pl.BlockSpec((batch, DN_HEADS, DN_DK, DN_DV), lambda i: (0, 0, 0, 0))],
        out_shape=[jax.ShapeDtypeStruct((batch, seq, DN_WIDTH), BF16),
                   jax.ShapeDtypeStruct((batch, DN_HEADS, DN_DK, DN_DV), F32)],
        scratch_shapes=[pltpu.VMEM((batch, N_PAIRS, DN_DK, PAIR), F32)],
        compiler_params=_params("arbitrary"),
        name="gdn_prompt",
    )(as3d(xc), as3d(dz), as3d(ba), *consts)
    return o.reshape(batch * seq, DN_WIDTH), s


def _gdn_sample_front_kernel(xc_ref, dz_ref, ba_ref, sc_ref, cw_ref, alog_ref, dtb_ref, hsum_ref,
                             q_ref, k_ref, v_ref, dz_t_ref, gates_ref):
    xc = xc_ref[...]
    y = sc_ref[0] * cw_ref[0:1, :]
    y = y + sc_ref[1] * cw_ref[1:2, :]
    y = y + sc_ref[2] * cw_ref[2:3, :]
    y = _silu(y + xc * cw_ref[3:4, :])
    hsum = hsum_ref[...]
    q = y[:, :DN_WIDTH]
    k = y[:, DN_WIDTH:2 * DN_WIDTH]
    q = q * lax.rsqrt(_mm_sel_rhs(q * q, hsum) + EPS) * (DN_DK ** -0.5)
    k = k * lax.rsqrt(_mm_sel_rhs(k * k, hsum) + EPS)
    beta_c, g_c = _gdn_gates(ba_ref[...], alog_ref[...], dtb_ref[...])
    q_ref[...] = q.T
    k_ref[...] = k.T
    v_ref[...] = y[:, 2 * DN_WIDTH:].T
    dz_t_ref[...] = dz_ref[...].T
    gates_ref[0:LANES, :] = beta_c.T
    gates_ref[LANES:, :] = jnp.exp(g_c).T


def _gdn_sample_step_kernel(q_ref, k_ref, v_ref, dz_ref, gates_ref, dn_ref, s_ref, o_ref, s_out_ref):
    h = pl.program_id(0)
    beta = gates_ref[pl.ds(h, 1), :]
    eg = gates_ref[pl.ds(LANES + DN_HEADS + h, 1), :]
    q, k, v = q_ref[...], k_ref[...], v_ref[...]
    w = (k * beta) * eg
    qg = q * eg
    ws = jnp.zeros(v.shape, F32)
    qs = jnp.zeros(v.shape, F32)
    for dk in range(DN_DK):
        s_dk = s_ref[0, dk]
        ws = ws + w[dk:dk + 1, :] * s_dk
        qs = qs + qg[dk:dk + 1, :] * s_dk
    v_new = v * beta - ws
    qk = jnp.sum(q * k, axis=0, keepdims=True)
    o = qs + qk * v_new
    for dk in range(DN_DK):
        s_out_ref[0, dk] = s_ref[0, dk] * eg + k[dk:dk + 1, :] * v_new
    o = o * lax.rsqrt(jnp.mean(o * o, axis=0, keepdims=True) + EPS) * dn_ref[...]
    o_ref[...] = o * _silu(dz_ref[...])


def _gdn_sample_lanes(xc, dz, ba, sconv_t, state_t, conv_w, alog, dtb, dn):
    nseq = xc.shape[0]
    assert nseq == LANES
    lane = np.arange(DN_WIDTH)
    hsum = jnp.asarray((lane[:, None] // DN_DV == lane[None, :] // DN_DV).astype(np.float32), dtype=BF16)
    full = lambda a: pl.BlockSpec(a.shape, lambda i: (0,) * a.ndim)
    cm = jax.ShapeDtypeStruct((DN_WIDTH, nseq), F32)
    front_in = (xc, dz, ba, sconv_t, conv_w, alog, dtb, hsum)
    q_t, k_t, v_t, dz_t, gates_t = pl.pallas_call(
        _gdn_sample_front_kernel,
        grid=(1,),
        in_specs=[full(a) for a in front_in],
        out_specs=[pl.BlockSpec((DN_WIDTH, nseq), lambda i: (0, 0))] * 4
                  + [pl.BlockSpec((2 * LANES, nseq), lambda i: (0, 0))],
        out_shape=[cm, cm, cm, cm, jax.ShapeDtypeStruct((2 * LANES, nseq), F32)],
        compiler_params=_params("arbitrary"),
        name="gdn_sample_front",
    )(*front_in)
    dn_b = jnp.broadcast_to(dn.reshape(DN_DV, 1), (DN_DV, nseq))
    head = pl.BlockSpec((DN_DK, nseq), lambda h: (h, 0))
    st = pl.BlockSpec((1, DN_DK, DN_DV, nseq), lambda h: (h, 0, 0, 0))
    return pl.pallas_call(
        _gdn_sample_step_kernel,
        grid=(DN_HEADS,),
        in_specs=[head, head, head, head, full(gates_t), full(dn_b), st],
        out_specs=[head, st],
        out_shape=[cm, jax.ShapeDtypeStruct(state_t.shape, F32)],
        compiler_params=_params("parallel"),
        name="gdn_sample_step",
    )(q_t, k_t, v_t, dz_t, gates_t, dn_b, state_t)


def _route(xn, wr):
    logits = jnp.dot(xn, wr, preferred_element_type=F32)
    lane = lax.broadcasted_iota(jnp.int32, logits.shape, 1).astype(F32)
    first_at = lambda hit: jnp.min(jnp.where(hit, lane, float(LANES)), axis=-1, keepdims=True)
    glog = jnp.where(lane < N_GROUPS, logits, NEG_INF)
    gmax = jnp.max(glog, axis=-1, keepdims=True)
    gsel = first_at(glog == gmax)
    pgsel = 1.0 / jnp.sum(jnp.exp(glog - gmax), axis=-1, keepdims=True)
    lo = ROUTER_OFF + gsel * EXPERTS_PER_GROUP
    in_group = jnp.logical_and(lane >= lo, lane < lo + EXPERTS_PER_GROUP)
    elog = jnp.where(in_group, logits, NEG_INF)
    m1 = jnp.max(elog, axis=-1, keepdims=True)
    i1 = first_at(elog == m1)
    z = jnp.sum(jnp.exp(elog - m1), axis=-1, keepdims=True)
    elog2 = jnp.where(lane == i1, NEG_INF, elog)
    m2 = jnp.max(elog2, axis=-1, keepdims=True)
    i2 = first_at(elog2 == m2)
    p1 = 1.0 / z
    p2 = jnp.exp(m2 - m1) / z
    tot = p1 + p2
    return lane, i1, i2, p1 / tot * pgsel, p2 / tot * pgsel


def _outproj(x_ref, oa_ref, od_ref, wo_ref):
    return x_ref[...] + _mm(oa_ref[...], wo_ref[:ATT_WIDTH, :]) + _mm(od_ref[...], wo_ref[ATT_WIDTH:, :])


def _outproj_router_kernel(x_ref, oa_ref, od_t_ref, wo_ref, g_ref, wr_ref, h_ref, xn_ref, gate_ref):
    h = (x_ref[...] + _mm(oa_ref[...], wo_ref[:ATT_WIDTH, :])
         + _mm(od_t_ref[...].T, wo_ref[ATT_WIDTH:, :]))
    h_ref[...] = h
    xn = _rmsnorm(h, g_ref[...]).astype(BF16)
    xn_ref[...] = xn
    lane, i1, i2, g1, g2 = _route(xn, wr_ref[...])
    gate_ref[...] = jnp.where(lane == i1, g1, 0.0) + jnp.where(lane == i2, g2, 0.0)


def _outproj_router(x, oa, od_t, wo, g, wr):
    t = x.shape[0]
    tm = t
    row = lambda n: pl.BlockSpec((tm, n), lambda i: (i, 0))
    full = lambda a: pl.BlockSpec(a.shape, lambda i: (0,) * a.ndim)
    return pl.pallas_call(
        _outproj_router_kernel,
        grid=(t // tm,),
        in_specs=[row(D_MODEL), row(ATT_WIDTH), full(od_t), full(wo), full(g), full(wr)],
        out_specs=[row(D_MODEL), row(D_MODEL), row(LANES)],
        out_shape=[jax.ShapeDtypeStruct((t, D_MODEL), F32), jax.ShapeDtypeStruct((t, D_MODEL), BF16),
                   jax.ShapeDtypeStruct((t, LANES), F32)],
        compiler_params=_params("parallel"),
        name="outproj_router",
    )(x, oa, od_t, wo, g, wr)


MOE_TM = 512
POS_TM = 1024
INFO_G1, INFO_G2, INFO_E1, INFO_E2 = 0, 1, 2, 3


def _moe_tiles(t):
    return (2 * t) // MOE_TM + N_EXPERTS


HALF = D_MODEL // 2
U32 = jnp.uint32


def _pack_rows(x):
    bits = lambda v: lax.bitcast_convert_type(v.astype(BF16).astype(F32), U32)
    return bits(x[:, HALF:]) | (bits(x[:, :HALF]) >> 16)


def _unpack_rows(w):
    lo = lax.bitcast_convert_type(w << 16, F32)
    hi = lax.bitcast_convert_type(w & jnp.uint32(0xFFFF0000), F32)
    return lo, hi


def _route_kernel(x_ref, oa_ref, od_ref, wo_ref, g_ref, wr_ref, h_ref, xn_ref, info_ref, cnt_ref, run_scr):
    h = _outproj(x_ref, oa_ref, od_ref, wo_ref)
    h_ref[...] = h
    xn = _rmsnorm(h, g_ref[...])
    xn_ref[...] = _pack_rows(xn)
    lane, i1, i2, g1, g2 = _route(xn.astype(BF16), wr_ref[...])
    info = jnp.where(lane == INFO_G1, g1, 0.0) + jnp.where(lane == INFO_G2, g2, 0.0)
    info = info + jnp.where(lane == INFO_E1, i1, 0.0) + jnp.where(lane == INFO_E2, i2, 0.0)
    info_ref[...] = info

    @pl.when(pl.program_id(0) == 0)
    def _():
        run_scr[...] = jnp.zeros(run_scr.shape, F32)
    picked = jnp.logical_or(lane == i1, lane == i2).astype(F32)
    run_scr[...] += jnp.sum(picked, axis=0, keepdims=True)
    cnt_ref[...] = run_scr[...]


def _route_sparse(x, oa, od, wo, g, wr):
    t = x.shape[0]
    tm = WIDE_TM
    row = lambda n: pl.BlockSpec((tm, n), lambda i: (i, 0))
    full = lambda a: pl.BlockSpec(a.shape, lambda i: (0,) * a.ndim)
    return pl.pallas_call(
        _route_kernel,
        grid=(t // tm,),
        in_specs=[row(D_MODEL), row(ATT_WIDTH), row(DN_WIDTH), full(wo), full(g), full(wr)],
        out_specs=[row(D_MODEL), row(HALF), row(LANES), pl.BlockSpec((1, LANES), lambda i: (0, 0))],
        out_shape=[jax.ShapeDtypeStruct((t, D_MODEL), F32), jax.ShapeDtypeStruct((t, HALF), U32),
                   jax.ShapeDtypeStruct((t, LANES), F32), jax.ShapeDtypeStruct((1, LANES), F32)],
        scratch_shapes=[pltpu.VMEM((1, LANES), F32)],
        compiler_params=_params("arbitrary"),
        name="route",
    )(x, oa, od, wo, g, wr)


def _positions_kernel(info_ref, cnt_ref, ltri_ref, utri_ref, pos_ref, run_scr, off_scr):
    info = info_ref[...]
    lane = lax.broadcasted_iota(jnp.int32, info.shape, 1).astype(F32)
    hit1 = lane == info[:, INFO_E1:INFO_E1 + 1]
    hit2 = lane == info[:, INFO_E2:INFO_E2 + 1]
    onehot = jnp.logical_or(hit1, hit2).astype(F32)

    @pl.when(pl.program_id(0) == 0)
    def _():
        ln = lax.broadcasted_iota(jnp.int32, cnt_ref.shape, 1)
        is_expert = jnp.logical_and(ln >= ROUTER_OFF, ln < ROUTER_OFF + N_EXPERTS)
        tiles = jnp.where(is_expert, jnp.maximum(jnp.floor((cnt_ref[...] + (MOE_TM - 1)) * (1.0 / MOE_TM)), 1.0), 0.0)
        off_scr[...] = MOE_TM * jnp.dot(tiles.astype(BF16), utri_ref[...], preferred_element_type=F32)
        run_scr[...] = jnp.zeros(run_scr.shape, F32)

    before = (jnp.dot(ltri_ref[...], onehot.astype(BF16), preferred_element_type=F32)
              + run_scr[...] + off_scr[...])
    pos1 = jnp.sum(jnp.where(hit1, before, 0.0), axis=-1, keepdims=True)
    pos2 = jnp.sum(jnp.where(hit2, before, 0.0), axis=-1, keepdims=True)
    both = jnp.where(lane == 0, pos1, 0.0) + jnp.where(lane == 1, pos2, 0.0)
    pos_ref[...] = both.T.astype(jnp.int32)
    run_scr[...] += jnp.sum(onehot, axis=0, keepdims=True)


def _positions(info, cnt):
    t = info.shape[0]
    tm = min(t, POS_TM)
    tok = np.arange(tm)
    ltri = jnp.asarray((tok[:, None] > tok[None, :]).astype(np.float32), dtype=BF16)
    ln = np.arange(LANES)
    utri = jnp.asarray((ln[:, None] < ln[None, :]).astype(np.float32), dtype=BF16)
    full = lambda a: pl.BlockSpec(a.shape, lambda i: (0,) * a.ndim)
    return pl.pallas_call(
        _positions_kernel,
        grid=(t // tm,),
        in_specs=[pl.BlockSpec((tm, LANES), lambda i: (i, 0)), full(cnt), full(ltri), full(utri)],
        out_specs=pl.BlockSpec((LANES, tm), lambda i: (0, i)),
        out_shape=jax.ShapeDtypeStruct((LANES, t), jnp.int32),
        scratch_shapes=[pltpu.VMEM((1, LANES), F32), pltpu.VMEM((1, LANES), F32)],
        compiler_params=_params("arbitrary"),
        name="positions",
    )(info, cnt, ltri, utri)


def _experts_kernel(te_ref, tv_ref, nt_ref, xs_ref, wg_hbm, wu_hbm, wd_hbm, xn_new_ref, gate_new_ref,
                    ys_ref, moe_new_ref, wg_s, wu_s, wd_s, wg_f, wu_f, wd_f, wsem):
    i = pl.program_id(0)
    used = i < nt_ref[0]
    expert = te_ref[i]

    def fetch(e):
        slot = e % 2
        return [pltpu.make_async_copy(src.at[e], dst.at[slot], wsem.at[slot, j])
                for j, (src, dst) in enumerate(((wg_hbm, wg_f), (wu_hbm, wu_f), (wd_hbm, wd_f)))]

    @pl.when(jnp.logical_or(i == 0, expert != te_ref[jnp.maximum(i - 1, 0)]))
    def _():
        @pl.when(i == 0)
        def _():
            for c in fetch(expert):
                c.start()
        for c in fetch(expert):
            c.wait()

        @pl.when(expert + 1 < N_EXPERTS)
        def _():
            for c in fetch(expert + 1):
                c.start()
        slot = expert % 2
        wg_s[...] = wg_f[slot].astype(BF16)
        wu_s[...] = wu_f[slot].astype(BF16)
        wd_s[...] = wd_f[slot].astype(BF16)
        xn = xn_new_ref[...]
        lane = lax.broadcasted_iota(jnp.int32, gate_new_ref.shape, 1)
        gate = jnp.sum(jnp.where(lane == expert + ROUTER_OFF, gate_new_ref[...], 0.0), axis=-1, keepdims=True)
        hg = jnp.dot(xn, wg_s[...], preferred_element_type=F32)
        hu = jnp.dot(xn, wu_s[...], preferred_element_type=F32)
        hm = _silu(hg) * hu * gate
        y = jnp.dot(hm.astype(BF16), wd_s[...], preferred_element_type=F32)

        @pl.when(i == 0)
        def _():
            moe_new_ref[...] = y

        @pl.when(i > 0)
        def _():
            moe_new_ref[...] += y

    @pl.when(used)
    def _():
        row = lax.broadcasted_iota(jnp.int32, xs_ref.shape, 0)
        x_lo, x_hi = _unpack_rows(jnp.where(row < tv_ref[i], xs_ref[...], jnp.uint32(0)))
        x_lo = x_lo.astype(BF16)
        x_hi = x_hi.astype(BF16)
        up = lambda w_s: (jnp.dot(x_lo, w_s[:HALF, :], preferred_element_type=F32)
                          + jnp.dot(x_hi, w_s[HALF:, :], preferred_element_type=F32))
        hm = (_silu_tanh(up(wg_s)) * up(wu_s)).astype(BF16)
        ys_ref[...] = _pack_rows(jnp.dot(hm, wd_s[...], preferred_element_type=F32))

    @pl.when(jnp.logical_not(used))
    def _():
        ys_ref[...] = jnp.zeros(ys_ref.shape, U32)


def _experts(xs, tile_expert, tile_valid, n_tiles, wg, wu, wd, xn_new, gate_new):
    max_tiles = xs.shape[0] // MOE_TM
    rows = pl.BlockSpec((MOE_TM, HALF), lambda i, te, tv, nt: (i, 0))
    hbm = pl.BlockSpec(memory_space=pl.ANY)
    full = lambda a: pl.BlockSpec(a.shape, lambda i, te, tv, nt: (0,) * a.ndim)
    return pl.pallas_call(
        _experts_kernel,
        grid_spec=pltpu.PrefetchScalarGridSpec(
            num_scalar_prefetch=3, grid=(max_tiles,),
            in_specs=[rows, hbm, hbm, hbm, full(xn_new), full(gate_new)],
            out_specs=[rows, pl.BlockSpec(xn_new.shape, lambda i, te, tv, nt: (0, 0))],
            scratch_shapes=[pltpu.VMEM((D_MODEL, D_EXPERT), BF16), pltpu.VMEM((D_MODEL, D_EXPERT), BF16),
                            pltpu.VMEM((D_EXPERT, D_MODEL), BF16),
                            pltpu.VMEM((2, D_MODEL, D_EXPERT), F32), pltpu.VMEM((2, D_MODEL, D_EXPERT), F32),
                            pltpu.VMEM((2, D_EXPERT, D_MODEL), F32), pltpu.SemaphoreType.DMA((2, 3))]),
        out_shape=[jax.ShapeDtypeStruct(xs.shape, U32), jax.ShapeDtypeStruct(xn_new.shape, F32)],
        compiler_params=_params("arbitrary"),
        name="experts",
    )(tile_expert, tile_valid, n_tiles, xs, wg, wu, wd, xn_new, gate_new)


SC_IDX = 128
SC_ROWS = 64
SC_WORKERS = 32
SC_GATHER_ROWS = 32
SC_GATHER_BUFS = 4


def _sc_mesh():
    return plsc.VectorSubcoreMesh(core_axis_name="c", subcore_axis_name="s")


def _sc_windows(t, fn):
    per_worker = t // SC_WORKERS
    worker = lax.axis_index(("c", "s"))

    @pl.loop(0, per_worker // SC_IDX)
    def _(w):
        fn(worker * per_worker + w * SC_IDX)


def _sc_scatter_rows(xn, pos1, pos2, n_rows):
    t, d = xn.shape
    assert t % (SC_WORKERS * SC_IDX) == 0
    idx_t = pltpu.VMEM((1, SC_IDX), jnp.int32)

    @pl.kernel(out_type=jax.ShapeDtypeStruct((n_rows, d), xn.dtype), mesh=_sc_mesh(),
               scratch_types=[idx_t, idx_t, pltpu.VMEM((SC_ROWS, d), xn.dtype)])
    def scatter(x_hbm, p1_hbm, p2_hbm, o_hbm, i1_v, i2_v, buf):
        def window(base):
            pltpu.sync_copy(p1_hbm.at[:, pl.ds(base, SC_IDX)], i1_v)
            pltpu.sync_copy(p2_hbm.at[:, pl.ds(base, SC_IDX)], i2_v)
            for k in range(SC_IDX // SC_ROWS):
                pltpu.sync_copy(x_hbm.at[pl.ds(base + k * SC_ROWS, SC_ROWS)], buf)
                pltpu.sync_copy(buf, o_hbm.at[i1_v.at[0, pl.ds(k * SC_ROWS, SC_ROWS)]])
                pltpu.sync_copy(buf, o_hbm.at[i2_v.at[0, pl.ds(k * SC_ROWS, SC_ROWS)]])
        _sc_windows(t, window)

    return scatter(xn, pos1.reshape(1, t), pos2.reshape(1, t))


def _sc_gather_rows(ys, pos1, pos2):
    t = pos1.shape[0]
    d = ys.shape[1]
    assert t % (SC_WORKERS * SC_IDX) == 0
    per_worker = t // SC_WORKERS
    idx_t = pltpu.VMEM((1, per_worker), jnp.int32)
    out = jax.ShapeDtypeStruct((t, d), ys.dtype)

    nbuf, rows = SC_GATHER_BUFS, SC_GATHER_ROWS
    buf_t = pltpu.VMEM((rows, d), ys.dtype)

    @pl.kernel(out_type=(out, out), mesh=_sc_mesh(),
               scratch_types=[idx_t, idx_t] + [buf_t] * nbuf
                             + [pltpu.SemaphoreType.DMA((nbuf,)), pltpu.SemaphoreType.DMA((nbuf,))])
    def gather(y_hbm, p1_hbm, p2_hbm, o1_hbm, o2_hbm, i1_v, i2_v, *rest):
        bufs, (gsem, wsem) = rest[:nbuf], rest[nbuf:]
        base = lax.axis_index(("c", "s")) * per_worker
        pltpu.sync_copy(p1_hbm.at[:, pl.ds(base, per_worker)], i1_v)
        pltpu.sync_copy(p2_hbm.at[:, pl.ds(base, per_worker)], i2_v)
        items = [(idx_v, o_hbm, k) for k in range(per_worker // rows)
                 for idx_v, o_hbm in ((i1_v, o1_hbm), (i2_v, o2_hbm))]
        n_items = len(items)

        def read(n):
            idx_v, _, k = items[n]
            return pltpu.make_async_copy(y_hbm.at[idx_v.at[0, pl.ds(k * rows, rows)]],
                                         bufs[n % nbuf], gsem.at[n % nbuf])

        def write(n):
            _, o_hbm, k = items[n]
            return pltpu.make_async_copy(bufs[n % nbuf], o_hbm.at[pl.ds(base + k * rows, rows)],
                                         wsem.at[n % nbuf])

        for n in range(min(nbuf - 1, n_items)):
            read(n).start()
        waited = 0
        for n in range(n_items):
            read(n).wait()
            write(n).start()
            ahead = n + nbuf - 1
            if ahead < n_items:
                if n >= 1:
                    write(n - 1).wait()
                    waited = n
                read(ahead).start()
        for n in range(waited, n_items):
            write(n).wait()

    return gather(ys, pos1.reshape(1, t), pos2.reshape(1, t))


def _ple_sparse_kernel(h_ref, info_ref, y1_ref, y2_ref, p_ref, wpp_ref, wpg_ref, gp_ref, gf_ref, y_ref):
    info = info_ref[...]
    g1 = info[:, INFO_G1:INFO_G1 + 1]
    g2 = info[:, INFO_G2:INFO_G2 + 1]
    y1_lo, y1_hi = _unpack_rows(y1_ref[...])
    y2_lo, y2_hi = _unpack_rows(y2_ref[...])
    moe = jnp.concatenate([g1 * y1_lo + g2 * y2_lo, g1 * y1_hi + g2 * y2_hi], axis=1)
    h = h_ref[...] + moe
    hn = _rmsnorm(h, gp_ref[...])
    h = h + _mm(p_ref[...], wpp_ref[...]) * _sigmoid(_mm(hn, wpg_ref[...]))
    y_ref[...] = _rmsnorm(h, gf_ref[...])


def _ple_sparse(h, info, y1, y2, p, wpp, wpg, gp, gf):
    t = h.shape[0]
    tm = WIDE_TM
    row = lambda n: pl.BlockSpec((tm, n), lambda i: (i, 0))
    full = lambda a: pl.BlockSpec(a.shape, lambda i: (0,) * a.ndim)
    return pl.pallas_call(
        _ple_sparse_kernel,
        grid=(t // tm,),
        in_specs=[row(D_MODEL), row(LANES), row(HALF), row(HALF), row(PLE_DIM),
                  full(wpp), full(wpg), full(gp), full(gf)],
        out_specs=row(D_MODEL),
        out_shape=jax.ShapeDtypeStruct((t, D_MODEL), F32),
        compiler_params=_params("parallel"),
        name="ple_sparse",
    )(h, info, y1, y2, p, wpp, wpg, gp, gf)


def _tile_tables(cnt, max_tiles):
    tiles_e = jnp.maximum((cnt + (MOE_TM - 1)) // MOE_TM, 1)
    ends = jnp.cumsum(tiles_e)
    n_tiles = ends[-1]
    tile = jnp.arange(max_tiles, dtype=jnp.int32)
    idx = jnp.minimum(tile, n_tiles - 1)
    tile_expert = jnp.sum((idx[:, None] >= ends[None, :]).astype(jnp.int32), axis=1)
    mine = tile_expert[:, None] == jnp.arange(N_EXPERTS, dtype=jnp.int32)[None, :]
    of_mine = lambda v: jnp.sum(jnp.where(mine, v[None, :], 0), axis=1)
    valid = jnp.clip(of_mine(cnt) - (idx - of_mine(ends - tiles_e)) * MOE_TM, 0, MOE_TM)
    tile_valid = jnp.where(tile < n_tiles, valid, 0).astype(jnp.int32)
    return tile_expert, tile_valid, n_tiles.reshape(1)


def _ple_final_kernel(h_ref, m_ref, p_ref, wpp_ref, wpg_ref, gp_ref, gf_ref, y_ref):
    h = h_ref[...] + m_ref[...]
    hn = _rmsnorm(h, gp_ref[...])
    h = h + _mm(p_ref[...], wpp_ref[...]) * _sigmoid(_mm(hn, wpg_ref[...]))
    y_ref[...] = _rmsnorm(h, gf_ref[...])


def _ple_final(h, m, p, wpp, wpg, gp, gf):
    t = h.shape[0]
    tm = min(t, 256)
    row = lambda n: pl.BlockSpec((tm, n), lambda i: (i, 0))
    full = lambda a: pl.BlockSpec(a.shape, lambda i: (0,) * a.ndim)
    return pl.pallas_call(
        _ple_final_kernel,
        grid=(t // tm,),
        in_specs=[row(D_MODEL), row(D_MODEL), row(PLE_DIM), full(wpp), full(wpg), full(gp), full(gf)],
        out_specs=row(D_MODEL),
        out_shape=jax.ShapeDtypeStruct((t, D_MODEL), F32),
        compiler_params=_params("parallel"),
        name="ple_final",
    )(h, m, p, wpp, wpg, gp, gf)


def kernel(x_prompt, x_sample, p_prompt, p_sample, cache_k, cache_v, state_conv, state_S, rel_bias, norm_mix, w_in, att_sink, conv_w, dn_A_log, dn_dt_bias, dn_norm, w_out, norm_ffn, w_router_group, w_router_expert, w_gate, w_up, w_down, w_ple_proj, w_ple_gate, norm_ple, norm_final):
    batch, seq, _ = x_prompt.shape
    nseq = x_sample.shape[0]
    assert x_sample.shape[1] == 1 and norm_mix.shape[0] == 1 and cache_k.shape[2] == WINDOW
    assert seq % GDN_TB == 0 and seq % ATT_BLOCK == 0

    wi = w_in[0]
    o_db = ATT_COLS + CONV_CH
    w_in_re = (wi[:, :o_db].astype(BF16), wi[:, o_db + 2 * DN_HEADS:].astype(BF16),
               jnp.pad(wi[:, o_db:o_db + 2 * DN_HEADS], ((0, 0), (0, LANES - 2 * DN_HEADS))).astype(BF16))
    row = lambda a: a.reshape(1, -1).astype(F32)
    pad_lanes = lambda a, off: jnp.zeros((1, LANES), F32).at[0, off:off + a.shape[0]].set(a)
    alog = pad_lanes(dn_A_log[0], DN_HEADS)
    dtb = pad_lanes(dn_dt_bias[0], DN_HEADS)
    dnx = jnp.tile(dn_norm[0], DN_HEADS).reshape(1, DN_WIDTH)
    w_router = jnp.concatenate(
        [w_router_group[0], w_router_expert[0],
         jnp.zeros((D_MODEL, LANES - N_GROUPS - N_EXPERTS), F32)], axis=1).astype(BF16)
    wo = w_out[0].astype(BF16)
    wg, wu, wd = w_gate[0], w_up[0], w_down[0]
    wpp, wpg = w_ple_proj[0].astype(BF16), w_ple_gate[0].astype(BF16)
    sink = att_sink[0]

    qi = np.arange(ATT_BLOCK)[:, None]
    kj = np.arange(2 * ATT_BLOCK)[None, :]
    bucket_p = jnp.asarray(_t5_bucket_np(qi + ATT_BLOCK - kj))
    bucket_s = jnp.asarray(_t5_bucket_np(WINDOW - np.arange(WINDOW)[None, :]))

    xp = x_prompt.reshape(batch * seq, D_MODEL)
    att_p, qkv_p, dz_p, ba_p, xc_tails = _inproj_conv(xp, row(norm_mix[0]), w_in_re, conv_w[0], seq)
    o_att_p = _attn_prompt(att_p, bucket_p, rel_bias, sink, batch, seq)
    o_dn_p, s_p = _gdn_prompt(qkv_p, dz_p, ba_p, alog, dtb, dnx, batch, seq)
    h1, xn2, info, cnt = _route_sparse(xp, o_att_p, o_dn_p, wo, row(norm_ffn[0]), w_router)
    pos = _positions(info, cnt)
    pos1, pos2 = pos[0], pos[1]
    max_tiles = _moe_tiles(batch * seq)
    cnt_e = cnt[0, ROUTER_OFF:ROUTER_OFF + N_EXPERTS].astype(jnp.int32)
    tile_expert, tile_valid, n_tiles = _tile_tables(cnt_e, max_tiles)
    xs_sorted = _sc_scatter_rows(xn2, pos1, pos2, max_tiles * MOE_TM)

    xs = x_sample.reshape(nseq, D_MODEL)
    att_s, xc_s, dz_s, ba_s = _inproj(xs, row(norm_mix[0]), w_in_re)
    ck_t = jnp.transpose(cache_k[0], (0, 2, 3, 1))
    cv_t = jnp.transpose(cache_v[0], (0, 2, 3, 1))
    o_att_s, ks_t, vs_t = _attn_sample(att_s, ck_t, cv_t, bucket_s, rel_bias, sink)
    sconv_t = jnp.swapaxes(state_conv[0], 0, 1)
    o_dn_s_t, s_s_t = _gdn_sample_lanes(xc_s, dz_s, ba_s, sconv_t, jnp.transpose(state_S[0], (1, 2, 3, 0)),
                                        conv_w[0], alog, dtb, dn_norm[0])
    s_s = jnp.transpose(s_s_t, (3, 0, 1, 2))

    h1_s, xn2_s, gates_s = _outproj_router(xs, o_att_s, o_dn_s_t, wo, row(norm_ffn[0]), w_router)

    ys, moe_s = _experts(xs_sorted, tile_expert, tile_valid, n_tiles, wg, wu, wd, xn2_s, gates_s)
    y1, y2 = _sc_gather_rows(ys, pos1, pos2)
    y_s = _ple_final(h1_s, moe_s, p_sample[0].reshape(nseq, PLE_DIM), wpp, wpg, row(norm_ple[0]),
                     row(norm_final))
    y_p = _ple_sparse(h1, info, y1, y2, p_prompt[0].reshape(batch * seq, PLE_DIM),
                      wpp, wpg, row(norm_ple[0]), row(norm_final))

    att_p3 = att_p.reshape(batch, seq, ATT_COLS)
    kv_shape = (1, batch, WINDOW, ATT_KV_HEADS, HEAD_DIM)
    k_p = att_p3[:, seq - WINDOW:, ATT_WIDTH:ATT_WIDTH + KV_WIDTH].reshape(kv_shape)
    v_p = att_p3[:, seq - WINDOW:, ATT_WIDTH + KV_WIDTH:].reshape(kv_shape)
    conv_p = xc_tails.reshape(batch, -1, TAIL, CONV_CH)[:, -1, TAIL - (CONV_WIDTH - 1):][None]
    k_s = jnp.transpose(ks_t, (0, 3, 1, 2))[None]
    v_s = jnp.transpose(vs_t, (0, 3, 1, 2))[None]
    conv_s = jnp.concatenate([state_conv[0][:, 1:], xc_s[:, None, :]], axis=1)[None]
    return (y_p.reshape(batch, seq, D_MODEL), y_s.reshape(nseq, 1, D_MODEL),
            k_p, v_p, conv_p, s_p[None], k_s, v_s, conv_s, s_s[None])
```

```python
import functools
import math

import numpy as np
import jax
import jax.numpy as jnp
from jax import lax
from jax.experimental import pallas as pl
from jax.experimental.pallas import tpu as pltpu
from jax.experimental.pallas import tpu_sc as plsc

F32 = jnp.float32
BF16 = jnp.bfloat16

D_MODEL = 1024
ATT_HEADS = 8
ATT_KV_HEADS = 2
HEAD_DIM = 64
GQA = ATT_HEADS // ATT_KV_HEADS
WINDOW = 128
ATT_BLOCK = 128
N_BUCKETS = 32
DN_HEADS = 8
DN_DK = 64
DN_DV = 64
CONV_WIDTH = 4
DN_CHUNK = 64
ATT_WIDTH = ATT_HEADS * HEAD_DIM
KV_WIDTH = ATT_KV_HEADS * HEAD_DIM
DN_WIDTH = DN_HEADS * DN_DV
CONV_CH = 3 * DN_WIDTH
N_GROUPS = 4
EXPERTS_PER_GROUP = 8
N_EXPERTS = N_GROUPS * EXPERTS_PER_GROUP
D_EXPERT = 256
PLE_DIM = 256
EPS = 1e-6
NEG_INF = float("-inf")

ATT_COLS = ATT_WIDTH + 2 * KV_WIDTH
LANES = 128
ROUTER_OFF = N_GROUPS
VMEM_LIMIT = 48 * 1024 * 1024
ROW_TM = 512
WIDE_TM = 1024


def _params(*sem):
    return pltpu.CompilerParams(dimension_semantics=sem, vmem_limit_bytes=VMEM_LIMIT)


def _mm(a, b):
    return jnp.dot(a.astype(BF16), b.astype(BF16), preferred_element_type=F32)


def _mm_nt(a, b):
    return lax.dot_general(a.astype(BF16), b.astype(BF16), (((1,), (1,)), ((), ())),
                           preferred_element_type=F32)


def _mm_tn(a, b):
    return lax.dot_general(a.astype(BF16), b.astype(BF16), (((0,), (0,)), ((), ())),
                           preferred_element_type=F32)


def _split3(x):
    h1 = x.astype(BF16)
    r1 = x - h1.astype(F32)
    h2 = r1.astype(BF16)
    h3 = (r1 - h2.astype(F32)).astype(BF16)
    return h1, h2, h3


def _mm_sel_rhs(x, sel):
    h1, h2, h3 = _split3(x)
    d = lambda h: jnp.dot(h, sel, preferred_element_type=F32)
    return d(h1) + d(h2) + d(h3)


def _mm_sel_lhs(sel, x):
    h1, h2, h3 = _split3(x)
    d = lambda h: jnp.dot(sel, h, preferred_element_type=F32)
    return d(h1) + d(h2) + d(h3)


def _sigmoid(x):
    return 1.0 / (1.0 + jnp.exp(-x))


def _silu(x):
    return x * _sigmoid(x)


def _silu_tanh(x):
    return x * (0.5 * jnp.tanh(0.5 * x) + 0.5)


def _softplus(x):
    return jnp.maximum(x, 0.0) + jnp.log1p(jnp.exp(-jnp.abs(x)))


def _rmsnorm(x, g):
    return x * lax.rsqrt(jnp.mean(x * x, axis=-1, keepdims=True) + EPS) * g


def _t5_bucket_np(dist):
    max_exact = N_BUCKETS // 2
    d = np.maximum(dist, 0)
    ratio = (np.log(np.maximum(d, 1).astype(np.float32) / np.float32(max_exact))
             / np.float32(math.log(WINDOW / max_exact))).astype(np.float32)
    large = np.minimum(max_exact + (ratio * np.float32(N_BUCKETS - max_exact)).astype(np.int32),
                       N_BUCKETS - 1)
    return np.where(d < max_exact, d, large).astype(np.int32)


def _bias_lookup(bucket, rb_ref, h):
    acc = jnp.zeros(bucket.shape, F32)
    for t in range(N_BUCKETS):
        acc = jnp.where(bucket == t, rb_ref[t, h], acc)
    return acc


def _inproj_kernel(x_ref, g_ref, wa_ref, wz_ref, wb_ref, att_ref, xc_ref, dz_ref, ba_ref):
    xn = _rmsnorm(x_ref[...], g_ref[...]).astype(BF16)
    att_ref[...] = jnp.dot(xn, wa_ref[:, :ATT_COLS], preferred_element_type=F32)
    xc_ref[...] = jnp.dot(xn, wa_ref[:, ATT_COLS:], preferred_element_type=F32)
    dz_ref[...] = jnp.dot(xn, wz_ref[...], preferred_element_type=F32)
    ba_ref[...] = jnp.dot(xn, wb_ref[...], preferred_element_type=F32)


def _inproj(x, g, w):
    t = x.shape[0]
    tm = min(t, ROW_TM)
    row = lambda n: pl.BlockSpec((tm, n), lambda i: (i, 0))
    full = lambda a: pl.BlockSpec(a.shape, lambda i: (0,) * a.ndim)
    return pl.pallas_call(
        _inproj_kernel,
        grid=(t // tm,),
        in_specs=[row(D_MODEL), full(g)] + [full(a) for a in w],
        out_specs=[row(ATT_COLS), row(CONV_CH), row(DN_WIDTH), row(LANES)],
        out_shape=[jax.ShapeDtypeStruct((t, n), F32) for n in (ATT_COLS, CONV_CH, DN_WIDTH, LANES)],
        compiler_params=_params("parallel"),
        name="inproj",
    )(x, g, *w)


TAIL = 8
PAIR = 2 * DN_DK
N_PAIRS = DN_WIDTH // PAIR


def _head_sums(z, pair_ones):
    hi = z.astype(BF16)
    lw = (z - hi.astype(F32)).astype(BF16)
    d = lambda a, p: jnp.dot(a[:, p * PAIR:(p + 1) * PAIR], pair_ones, preferred_element_type=F32)
    return jnp.concatenate([d(hi, p) + d(lw, p) for p in range(N_PAIRS)], axis=1)


def _inproj_conv_kernel(x_ref, g_ref, wa_ref, wz_ref, wb_ref, cw_ref, ones_ref,
                        att_ref, qkv_ref, dz_ref, ba_ref, tail_ref, xp_scr, *, tiles_per_seq):
    tm = x_ref.shape[0]

    @pl.when(pl.program_id(0) % tiles_per_seq == 0)
    def _():
        xp_scr[...] = jnp.zeros((TAIL, CONV_CH), F32)

    xn = _rmsnorm(x_ref[...], g_ref[...]).astype(BF16)
    xc = jnp.dot(xn, wa_ref[:, ATT_COLS:], preferred_element_type=F32)
    att_ref[...] = jnp.dot(xn, wa_ref[:, :ATT_COLS], preferred_element_type=F32)
    dz_ref[...] = jnp.dot(xn, wz_ref[...], preferred_element_type=F32)
    ba_ref[...] = jnp.dot(xn, wb_ref[...], preferred_element_type=F32)

    head = jnp.concatenate([xp_scr[...], xc[:TAIL, :]], axis=0)

    def shifted(j):
        return jnp.concatenate([head[TAIL - j:2 * TAIL - j, :], pltpu.roll(xc, j, axis=0)[TAIL:, :]], axis=0)

    y = shifted(3) * cw_ref[0:1, :]
    y = y + shifted(2) * cw_ref[1:2, :]
    y = y + shifted(1) * cw_ref[2:3, :]
    y = y + xc * cw_ref[3:4, :]
    tail = xc[tm - TAIL:, :]
    xp_scr[...] = tail
    tail_ref[0] = tail
    y = _silu_tanh(y)
    q = y[:, :DN_WIDTH]
    k = y[:, DN_WIDTH:2 * DN_WIDTH]
    inv_norm = lax.rsqrt(_head_sums(jnp.concatenate([q * q, k * k], axis=0), ones_ref[...]) + EPS)
    qkv_ref[:, :DN_WIDTH] = q * inv_norm[:tm] * (DN_DK ** -0.5)
    qkv_ref[:, DN_WIDTH:2 * DN_WIDTH] = k * inv_norm[tm:]
    qkv_ref[:, 2 * DN_WIDTH:] = y[:, 2 * DN_WIDTH:]


def _pair_ones():
    lane = np.arange(PAIR)
    return jnp.asarray((lane[:, None] // DN_DV == lane[None, :] // DN_DV).astype(np.float32), dtype=BF16)


def _inproj_conv(x, g, w, conv_w, seq):
    t = x.shape[0]
    tm = ROW_TM
    assert seq % tm == 0
    ones = _pair_ones()
    row = lambda n: pl.BlockSpec((tm, n), lambda i: (i, 0))
    full = lambda a: pl.BlockSpec(a.shape, lambda i: (0,) * a.ndim)
    return pl.pallas_call(
        functools.partial(_inproj_conv_kernel, tiles_per_seq=seq // tm),
        grid=(t // tm,),
        in_specs=[row(D_MODEL), full(g)] + [full(a) for a in w] + [full(conv_w), full(ones)],
        out_specs=[row(ATT_COLS), row(CONV_CH), row(DN_WIDTH), row(LANES),
                   pl.BlockSpec((1, TAIL, CONV_CH), lambda i: (i, 0, 0))],
        out_shape=[jax.ShapeDtypeStruct((t, n), F32) for n in (ATT_COLS, CONV_CH, DN_WIDTH, LANES)]
                  + [jax.ShapeDtypeStruct((t // tm, TAIL, CONV_CH), F32)],
        scratch_shapes=[pltpu.VMEM((TAIL, CONV_CH), F32)],
        compiler_params=_params("arbitrary"),
        name="inproj_conv",
    )(x, g, *w, conv_w, ones)


GROUP_ROWS = GQA * ATT_BLOCK


def _attn_prompt_kernel(cur_ref, prev_ref, bucket_ref, rb_ref, sink_ref, o_ref, bias_scr, sink_scr):
    i = pl.program_id(0)
    nseq = cur_ref.shape[0]

    @pl.when(i == 0)
    def _():
        qi = lax.broadcasted_iota(jnp.int32, (ATT_BLOCK, 2 * ATT_BLOCK), 0)
        kj = lax.broadcasted_iota(jnp.int32, (ATT_BLOCK, 2 * ATT_BLOCK), 1)
        dist = qi + ATT_BLOCK - kj
        band = jnp.logical_and(dist >= 0, dist < WINDOW)
        bucket = bucket_ref[...]
        hrow = lax.broadcasted_iota(jnp.int32, (GROUP_ROWS, 1), 0) // ATT_BLOCK
        for g in range(ATT_KV_HEADS):
            sink_col = jnp.zeros((GROUP_ROWS, 1), F32)
            for hh in range(GQA):
                h = g * GQA + hh
                bias = jnp.where(band, _bias_lookup(bucket, rb_ref, h), NEG_INF)
                bias_scr[0, g, hh * ATT_BLOCK:(hh + 1) * ATT_BLOCK, :] = bias
                bias_scr[1, g, hh * ATT_BLOCK:(hh + 1) * ATT_BLOCK, :] = jnp.where(kj >= ATT_BLOCK, bias, NEG_INF)
                sink_col = jnp.where(hrow == hh, sink_ref[h], sink_col)
            sink_scr[g] = sink_col

    first = (i == 0).astype(jnp.int32)
    probs = [(b, g) for b in range(nseq) for g in range(ATT_KV_HEADS)]
    scores = []
    for b, g in probs:
        cur = cur_ref[b]
        prev = prev_ref[b]
        q = jnp.concatenate([cur[:, (g * GQA + hh) * HEAD_DIM:(g * GQA + hh + 1) * HEAD_DIM]
                             for hh in range(GQA)], axis=0) * (HEAD_DIM ** -0.5)
        kcol = slice(ATT_WIDTH + g * HEAD_DIM, ATT_WIDTH + (g + 1) * HEAD_DIM)
        k2 = jnp.concatenate([prev[:, kcol], cur[:, kcol]], axis=0)
        scores.append(_mm_nt(q, k2) + bias_scr[first, g])
    probs_p, dens = [], []
    for (b, g), s in zip(probs, scores):
        sink = sink_scr[g]
        m = jnp.maximum(jnp.max(s, axis=-1, keepdims=True), sink)
        p = jnp.exp(s - m)
        dens.append(jnp.sum(p, axis=-1, keepdims=True) + jnp.exp(sink - m))
        probs_p.append(p.astype(BF16))
    outs = {}
    for (b, g), p, den in zip(probs, probs_p, dens):
        vcol = slice(ATT_WIDTH + KV_WIDTH + g * HEAD_DIM, ATT_WIDTH + KV_WIDTH + (g + 1) * HEAD_DIM)
        v2 = jnp.concatenate([prev_ref[b][:, vcol], cur_ref[b][:, vcol]], axis=0)
        outs[b, g] = _mm(p, v2) / den
    for b in range(nseq):
        o_ref[b] = jnp.concatenate([outs[b, g][hh * ATT_BLOCK:(hh + 1) * ATT_BLOCK, :]
                                    for g in range(ATT_KV_HEADS) for hh in range(GQA)],
                                   axis=1).astype(o_ref.dtype)


def _attn_prompt(att, bucket, rel_bias, sink, batch, seq):
    nb = seq // ATT_BLOCK
    smem = pl.BlockSpec(memory_space=pltpu.SMEM)
    att3 = att.reshape(batch, seq, ATT_COLS)
    out = pl.pallas_call(
        _attn_prompt_kernel,
        grid=(nb,),
        in_specs=[
            pl.BlockSpec((batch, ATT_BLOCK, ATT_COLS), lambda i: (0, i, 0)),
            pl.BlockSpec((batch, ATT_BLOCK, ATT_COLS), lambda i: (0, jnp.maximum(i - 1, 0), 0)),
            pl.BlockSpec(bucket.shape, lambda i: (0, 0)),
            smem, smem,
        ],
        out_specs=pl.BlockSpec((batch, ATT_BLOCK, ATT_WIDTH), lambda i: (0, i, 0)),
        out_shape=jax.ShapeDtypeStruct((batch, seq, ATT_WIDTH), BF16),
        scratch_shapes=[pltpu.VMEM((2, ATT_KV_HEADS, GROUP_ROWS, 2 * ATT_BLOCK), F32),
                        pltpu.VMEM((ATT_KV_HEADS, GROUP_ROWS, 1), F32)],
        compiler_params=_params("arbitrary"),
        name="attn_prompt",
    )(att3, att3, bucket, rel_bias, sink)
    return out.reshape(batch * seq, ATT_WIDTH)


ATT_S_BB = 8


def _attn_sample_kernel(att_ref, ck_ref, cv_ref, bucket_ref, rb_ref, sink_ref, o_ref, ks_ref, vs_ref,
                        bias_scr, col_scr):
    hrow = lax.broadcasted_iota(jnp.int32, (ATT_HEADS, LANES), 0)
    lane = lax.broadcasted_iota(jnp.int32, (ATT_HEADS, LANES), 1)

    last = (lax.broadcasted_iota(jnp.int32, (3, WINDOW), 1) == WINDOW - 1).astype(BF16)
    is_last = lax.broadcasted_iota(jnp.int32, (KV_WIDTH, WINDOW), 1) == WINDOW - 1

    def shifted(cache_t, new_row):
        pieces = jnp.concatenate([p.astype(F32) for p in _split3(new_row)], axis=0).astype(BF16)
        col = lax.dot_general(pieces, last, (((0,), (0,)), ((), ())), preferred_element_type=F32)
        out = jnp.where(is_last, col, pltpu.roll(cache_t, WINDOW - 1, axis=1))
        return out.reshape(ATT_KV_HEADS, HEAD_DIM, WINDOW)

    for b in range(ATT_S_BB):
        row = att_ref[b:b + 1, :]
        ks_ref[b] = shifted(ck_ref[b].reshape(KV_WIDTH, WINDOW), row[:, ATT_WIDTH:ATT_WIDTH + KV_WIDTH])
        vs_ref[b] = shifted(cv_ref[b].reshape(KV_WIDTH, WINDOW), row[:, ATT_WIDTH + KV_WIDTH:])

    @pl.when(pl.program_id(0) == 0)
    def _():
        bucket = jnp.broadcast_to(bucket_ref[...], (ATT_HEADS, LANES))
        bias = jnp.zeros((ATT_HEADS, LANES), F32)
        cols = jnp.zeros((ATT_HEADS, LANES), F32)
        for h in range(ATT_HEADS):
            bias = jnp.where(hrow == h, _bias_lookup(bucket, rb_ref, h), bias)
            cols = jnp.where(jnp.logical_and(hrow == h, lane == 0), sink_ref[h], cols)
            cols = jnp.where(jnp.logical_and(hrow == h, lane == 1), rb_ref[0, h], cols)
        bias_scr[...] = jnp.where(lane >= 1, bias, NEG_INF)
        col_scr[...] = cols

    bias_c = bias_scr[...]
    sink = col_scr[:, 0:1]
    bias_n = col_scr[:, 1:2]
    same_group = (hrow // GQA) == (lane // HEAD_DIM)
    low_group = lax.broadcasted_iota(jnp.int32, (ATT_HEADS, HEAD_DIM), 0) < GQA
    rnd = lambda a: a.astype(BF16).astype(F32)
    seqs = range(ATT_S_BB)
    rows = [att_ref[b:b + 1, :] for b in seqs]
    q_bds = []
    for row in rows:
        q = row[:, :ATT_WIDTH] * (HEAD_DIM ** -0.5)
        qh = jnp.concatenate([q[:, h * HEAD_DIM:(h + 1) * HEAD_DIM] for h in range(ATT_HEADS)], axis=0)
        q_bds.append(jnp.where(same_group, jnp.concatenate([qh, qh], axis=1), 0.0))
    kv_t = lambda ref, b: ref[b].reshape(KV_WIDTH, WINDOW)
    s_cs = [_mm(q_bd, kv_t(ck_ref, b)) + bias_c for b, q_bd in zip(seqs, q_bds)]
    prs, pns = [], []
    for row, q_bd, s_c in zip(rows, q_bds, s_cs):
        kn = row[:, ATT_WIDTH:ATT_WIDTH + KV_WIDTH]
        s_n = jnp.sum(rnd(q_bd) * rnd(kn), axis=-1, keepdims=True) + bias_n
        m = jnp.maximum(jnp.maximum(jnp.max(s_c, axis=-1, keepdims=True), s_n), sink)
        p_c = jnp.exp(s_c - m)
        p_n = jnp.exp(s_n - m)
        den = jnp.sum(p_c, axis=-1, keepdims=True) + p_n + jnp.exp(sink - m)
        prs.append(p_c / den)
        pns.append(p_n / den)
    pvs = [_mm_nt(pr, kv_t(cv_ref, b)) for b, pr in zip(seqs, prs)]
    for b, row, pv, pn in zip(seqs, rows, pvs, pns):
        vn = row[:, ATT_WIDTH + KV_WIDTH:]
        o_full = pv + rnd(pn) * rnd(vn)
        o_sel = jnp.where(low_group, o_full[:, :HEAD_DIM], o_full[:, HEAD_DIM:])
        o_ref[b:b + 1, :] = jnp.concatenate([o_sel[h:h + 1, :] for h in range(ATT_HEADS)], axis=1)


def _attn_sample(att, ck, cv, bucket, rel_bias, sink):
    nseq = att.shape[0]
    smem = pl.BlockSpec(memory_space=pltpu.SMEM)
    cache = pl.BlockSpec((ATT_S_BB, ATT_KV_HEADS, HEAD_DIM, WINDOW), lambda i: (i, 0, 0, 0))
    return pl.pallas_call(
        _attn_sample_kernel,
        grid=(nseq // ATT_S_BB,),
        in_specs=[pl.BlockSpec((ATT_S_BB, ATT_COLS), lambda i: (i, 0)), cache, cache,
                  pl.BlockSpec(bucket.shape, lambda i: (0, 0)), smem, smem],
        out_specs=[pl.BlockSpec((ATT_S_BB, ATT_WIDTH), lambda i: (i, 0)), cache, cache],
        out_shape=[jax.ShapeDtypeStruct((nseq, ATT_WIDTH), F32),
                   jax.ShapeDtypeStruct(ck.shape, F32), jax.ShapeDtypeStruct(cv.shape, F32)],
        scratch_shapes=[pltpu.VMEM((ATT_HEADS, LANES), F32), pltpu.VMEM((ATT_HEADS, LANES), F32)],
        compiler_params=_params("arbitrary"),
        name="attn_sample",
    )(att, ck, cv, bucket, rel_bias, sink)


GDN_TB = 128
GDN_NC = GDN_TB // DN_CHUNK


def _gdn_gates(ba, alog, dtb):
    beta = _sigmoid(ba)
    g = -jnp.exp(alog) * _softplus(ba + dtb)
    return beta, g


def _pair_diag(x, lo):
    xb = x.astype(BF16)
    zero = jnp.zeros_like(xb)
    return jnp.concatenate([jnp.where(lo, xb, zero), jnp.where(lo, zero, xb)], axis=0)


def _gdn_prompt_kernel(qkv_ref, dz_ref, ba_ref, alog_ref, dtb_ref, dnx_ref,
                       hsum_ref, expb_ref, expg_ref, ltri_ref,
                       o_ref, s_out_ref, s_scr):
    i = pl.program_id(0)
    nb = qkv_ref.shape[0]

    @pl.when(i == 0)
    def _():
        s_scr[...] = jnp.zeros(s_scr.shape, F32)

    hsum = hsum_ref[...]
    ri = lax.broadcasted_iota(jnp.int32, (DN_CHUNK, PAIR), 0)
    ci = lax.broadcasted_iota(jnp.int32, (DN_CHUNK, PAIR), 1)
    lo = ci < DN_DK
    cj = jnp.where(lo, ci, ci - DN_DK)
    causal = ri >= cj
    strict = ri > cj
    eye = (ri == cj).astype(F32)

    def sel2(x, m):
        hi = x.astype(BF16)
        lw = (x - hi.astype(F32)).astype(BF16)
        return (jnp.dot(hi, m, preferred_element_type=F32) + jnp.dot(lw, m, preferred_element_type=F32))

    pre = []
    for b in range(nb):
        q = qkv_ref[b, :, :DN_WIDTH]
        k = qkv_ref[b, :, DN_WIDTH:2 * DN_WIDTH]
        v = qkv_ref[b, :, 2 * DN_WIDTH:]
        beta_c, g_c = _gdn_gates(ba_ref[b], alog_ref[...], dtb_ref[...])
        beta = sel2(beta_c, expb_ref[...])
        gam_c = _mm_sel_lhs(ltri_ref[...], g_c)
        gam = _mm_sel_rhs(gam_c, expg_ref[...])
        gam_t = gam_c.T
        kb = k * beta
        egam = jnp.exp(gam)
        pre.append(dict(q=q, k=k, kb=kb, vb=v * beta, qg=q * egam, wr=kb * egam, gam=gam, gam_t=gam_t))

    probs = [(c, b, p) for c in range(GDN_NC) for b in range(nb) for p in range(N_PAIRS)]
    pick = lambda m: jnp.where(lo, m[:DN_DK], m[DN_DK:])
    rows_of = lambda c: slice(c * DN_CHUNK, (c + 1) * DN_CHUNK)
    sl = lambda name, c, b, p: pre[b][name][rows_of(c), p * PAIR:(p + 1) * PAIR]
    raws = []
    for c, b, p in probs:
        k_p = sl("k", c, b, p)
        k_rows = jnp.concatenate([jnp.where(lo, k_p, 0.0), jnp.where(lo, 0.0, k_p)], axis=0)
        raws.append(_mm_nt(jnp.concatenate([sl("kb", c, b, p), sl("q", c, b, p)], axis=0), k_rows))
    pws, ts, qks = [], [], []
    for (c, b, p), raw in zip(probs, raws):
        gcol = sl("gam", c, b, p)
        h0 = DN_HEADS + 2 * p
        gam_t = pre[b]["gam_t"]
        grow = jnp.concatenate([gam_t[h0:h0 + 1, rows_of(c)], gam_t[h0 + 1:h0 + 2, rows_of(c)]], axis=1)
        decay = jnp.exp(jnp.where(causal, gcol - grow, NEG_INF))
        a = jnp.where(strict, raw[:DN_CHUNK] * decay, 0.0)
        qks.append(jnp.where(causal, raw[DN_CHUNK:] * decay, 0.0))
        pws.append(-a)
        ts.append(eye - a)
    pws = [_mm(pw, _pair_diag(pw, lo)) for pw in pws]
    for _ in range(4):
        rs = [_mm(jnp.concatenate([pw, t], axis=0), _pair_diag(pw, lo)) for pw, t in zip(pws, ts)]
        pws = [r[:DN_CHUNK] for r in rs]
        ts = [t + r[DN_CHUNK:] for t, r in zip(ts, rs)]
    rs = [_mm(t, _pair_diag(pw, lo)) for pw, t in zip(pws, ts)]
    ts = [t + r for t, r in zip(ts, rs)]
    sols = [_mm(t, jnp.concatenate([_pair_diag(sl("vb", c, b, p), lo), _pair_diag(sl("wr", c, b, p), lo)],
                                   axis=1)) for (c, b, p), t in zip(probs, ts)]
    qkuws = [_mm(qk, jnp.concatenate([_pair_diag(s[:, :PAIR], lo), _pair_diag(s[:, PAIR:], lo)], axis=1))
             for qk, s in zip(qks, sols)]
    crosses, gls = [], []
    for (c, b, p), s in zip(probs, sols):
        last = (c + 1) * DN_CHUNK - 1
        gam_last = pre[b]["gam"][last:last + 1, p * PAIR:(p + 1) * PAIR]
        kd = sl("k", c, b, p) * jnp.exp(gam_last - sl("gam", c, b, p))
        crosses.append(_mm_tn(kd, s))
        gls.append(jnp.exp(gam_last))
    lhs = [jnp.concatenate([pick(cr[:, PAIR:]), sl("qg", c, b, p) - qkuw[:, PAIR:]], axis=0)
           for (c, b, p), cr, qkuw in zip(probs, crosses, qkuws)]

    o_rows = [[] for _ in range(nb)]
    per_chunk = nb * N_PAIRS
    for c in range(GDN_NC):
        sel = slice(c * per_chunk, (c + 1) * per_chunk)
        s_olds = [s_scr[b, p] for _, b, p in probs[sel]]
        rs = [_mm(l, _pair_diag(s_old, lo)) for l, s_old in zip(lhs[sel], s_olds)]
        o_pairs = [[] for _ in range(nb)]
        for (_, b, p), r, s_old, gl, cr, qkuw in zip(probs[sel], rs, s_olds, gls[sel], crosses[sel], qkuws[sel]):
            s_scr[b, p] = gl * s_old - r[:DN_DK] + pick(cr[:, :PAIR])
            o_pairs[b].append(r[DN_DK:] + qkuw[:, :PAIR])
        for b in range(nb):
            o_rows[b].append(jnp.concatenate(o_pairs[b], axis=1))

    o_all = jnp.concatenate([jnp.concatenate(rows, axis=0) for rows in o_rows], axis=0)
    inv_rms = lax.rsqrt(_head_sums(o_all * o_all, hsum) * (1.0 / DN_DV) + EPS)
    for b in range(nb):
        rows = slice(b * GDN_TB, (b + 1) * GDN_TB)
        o_ref[b] = (o_all[rows] * inv_rms[rows] * dnx_ref[...] * _silu_tanh(dz_ref[b])).astype(o_ref.dtype)

    @pl.when(i == pl.num_programs(0) - 1)
    def _():
        for b in range(nb):
            for p in range(N_PAIRS):
                s_p = s_scr[b, p]
                s_out_ref[b, 2 * p] = s_p[:, :DN_DV]
                s_out_ref[b, 2 * p + 1] = s_p[:, DN_DV:]


def _gdn_consts():
    lane = np.arange(DN_WIDTH)
    pl_lane = np.arange(PAIR)
    hsum = (pl_lane[:, None] // DN_DV == pl_lane[None, :] // DN_DV)
    src = np.arange(LANES)
    expb = (src[:, None] == lane[None, :] // DN_DV)
    expg = (src[:, None] == DN_HEADS + lane[None, :] // DN_DV)
    tok = np.arange(GDN_TB)
    ltri = np.logical_and(tok[:, None] >= tok[None, :],
                          tok[:, None] // DN_CHUNK == tok[None, :] // DN_CHUNK)
    as_bf16 = lambda m: jnp.asarray(m.astype(np.float32), dtype=BF16)
    return as_bf16(hsum), as_bf16(expb), as_bf16(expg), as_bf16(ltri)


def _gdn_prompt(xc, dz, ba, alog, dtb, dnx, batch, seq):
    nt = seq // GDN_TB
    hsum, expb, expg, ltri = _gdn_consts()
    row = lambda n: pl.BlockSpec((batch, GDN_TB, n), lambda i: (0, i, 0))
    full = lambda a: pl.BlockSpec(a.shape, lambda i: (0,) * a.ndim)
    consts = (alog, dtb, dnx, hsum, expb, expg, ltri)
    as3d = lambda a: a.reshape(batch, seq, a.shape[-1])
    o, s = pl.pallas_call(
        _gdn_prompt_kernel,
        grid=(nt,),
        in_specs=[row(CONV_CH), row(DN_WIDTH), row(LANES)] + [full(a) for a in consts],
        out_specs=[row(DN_WIDTH),
                   pl.BlockSpec((batch, DN_HEADS, DN_DK, DN_DV), lambda i: (0, 0, 0, 0))],
        out_shape=[jax.ShapeDtypeStruct((batch, seq, DN_WIDTH), BF16),
                   jax.ShapeDtypeStruct((batch, DN_HEADS, DN_DK, DN_DV), F32)],
        scratch_shapes=[pltpu.VMEM((batch, N_PAIRS, DN_DK, PAIR), F32)],
        compiler_params=_params("arbitrary"),
        name="gdn_prompt",
    )(as3d(xc), as3d(dz), as3d(ba), *consts)
    return o.reshape(batch * seq, DN_WIDTH), s


def _gdn_sample_front_kernel(xc_ref, dz_ref, ba_ref, sc_ref, cw_ref, alog_ref, dtb_ref, hsum_ref,
                             q_ref, k_ref, v_ref, dz_t_ref, gates_ref):
    xc = xc_ref[...]
    y = sc_ref[0] * cw_ref[0:1, :]
    y = y + sc_ref[1] * cw_ref[1:2, :]
    y = y + sc_ref[2] * cw_ref[2:3, :]
    y = _silu(y + xc * cw_ref[3:4, :])
    hsum = hsum_ref[...]
    q = y[:, :DN_WIDTH]
    k = y[:, DN_WIDTH:2 * DN_WIDTH]
    q = q * lax.rsqrt(_mm_sel_rhs(q * q, hsum) + EPS) * (DN_DK ** -0.5)
    k = k * lax.rsqrt(_mm_sel_rhs(k * k, hsum) + EPS)
    beta_c, g_c = _gdn_gates(ba_ref[...], alog_ref[...], dtb_ref[...])
    q_ref[...] = q.T
    k_ref[...] = k.T
    v_ref[...] = y[:, 2 * DN_WIDTH:].T
    dz_t_ref[...] = dz_ref[...].T
    gates_ref[0:LANES, :] = beta_c.T
    gates_ref[LANES:, :] = jnp.exp(g_c).T


def _gdn_sample_step_kernel(q_ref, k_ref, v_ref, dz_ref, gates_ref, dn_ref, s_ref, o_ref, s_out_ref):
    h = pl.program_id(0)
    beta = gates_ref[pl.ds(h, 1), :]
    eg = gates_ref[pl.ds(LANES + DN_HEADS + h, 1), :]
    q, k, v = q_ref[...], k_ref[...], v_ref[...]
    w = (k * beta) * eg
    qg = q * eg
    ws = jnp.zeros(v.shape, F32)
    qs = jnp.zeros(v.shape, F32)
    for dk in range(DN_DK):
        s_dk = s_ref[0, dk]
        ws = ws + w[dk:dk + 1, :] * s_dk
        qs = qs + qg[dk:dk + 1, :] * s_dk
    v_new = v * beta - ws
    qk = jnp.sum(q * k, axis=0, keepdims=True)
    o = qs + qk * v_new
    for dk in range(DN_DK):
        s_out_ref[0, dk] = s_ref[0, dk] * eg + k[dk:dk + 1, :] * v_new
    o = o * lax.rsqrt(jnp.mean(o * o, axis=0, keepdims=True) + EPS) * dn_ref[...]
    o_ref[...] = o * _silu(dz_ref[...])


def _gdn_sample_lanes(xc, dz, ba, sconv_t, state_t, conv_w, alog, dtb, dn):
    nseq = xc.shape[0]
    assert nseq == LANES
    lane = np.arange(DN_WIDTH)
    hsum = jnp.asarray((lane[:, None] // DN_DV == lane[None, :] // DN_DV).astype(np.float32), dtype=BF16)
    full = lambda a: pl.BlockSpec(a.shape, lambda i: (0,) * a.ndim)
    cm = jax.ShapeDtypeStruct((DN_WIDTH, nseq), F32)
    front_in = (xc, dz, ba, sconv_t, conv_w, alog, dtb, hsum)
    q_t, k_t, v_t, dz_t, gates_t = pl.pallas_call(
        _gdn_sample_front_kernel,
        grid=(1,),
        in_specs=[full(a) for a in front_in],
        out_specs=[pl.BlockSpec((DN_WIDTH, nseq), lambda i: (0, 0))] * 4
                  + [pl.BlockSpec((2 * LANES, nseq), lambda i: (0, 0))],
        out_shape=[cm, cm, cm, cm, jax.ShapeDtypeStruct((2 * LANES, nseq), F32)],
        compiler_params=_params("arbitrary"),
        name="gdn_sample_front",
    )(*front_in)
    dn_b = jnp.broadcast_to(dn.reshape(DN_DV, 1), (DN_DV, nseq))
    head = pl.BlockSpec((DN_DK, nseq), lambda h: (h, 0))
    st = pl.BlockSpec((1, DN_DK, DN_DV, nseq), lambda h: (h, 0, 0, 0))
    return pl.pallas_call(
        _gdn_sample_step_kernel,
        grid=(DN_HEADS,),
        in_specs=[head, head, head, head, full(gates_t), full(dn_b), st],
        out_specs=[head, st],
        out_shape=[cm, jax.ShapeDtypeStruct(state_t.shape, F32)],
        compiler_params=_params("parallel"),
        name="gdn_sample_step",
    )(q_t, k_t, v_t, dz_t, gates_t, dn_b, state_t)


def _route(xn, wr):
    logits = jnp.dot(xn, wr, preferred_element_type=F32)
    lane = lax.broadcasted_iota(jnp.int32, logits.shape, 1).astype(F32)
    first_at = lambda hit: jnp.min(jnp.where(hit, lane, float(LANES)), axis=-1, keepdims=True)
    glog = jnp.where(lane < N_GROUPS, logits, NEG_INF)
    gmax = jnp.max(glog, axis=-1, keepdims=True)
    gsel = first_at(glog == gmax)
    pgsel = 1.0 / jnp.sum(jnp.exp(glog - gmax), axis=-1, keepdims=True)
    lo = ROUTER_OFF + gsel * EXPERTS_PER_GROUP
    in_group = jnp.logical_and(lane >= lo, lane < lo + EXPERTS_PER_GROUP)
    elog = jnp.where(in_group, logits, NEG_INF)
    m1 = jnp.max(elog, axis=-1, keepdims=True)
    i1 = first_at(elog == m1)
    z = jnp.sum(jnp.exp(elog - m1), axis=-1, keepdims=True)
    elog2 = jnp.where(lane == i1, NEG_INF, elog)
    m2 = jnp.max(elog2, axis=-1, keepdims=True)
    i2 = first_at(elog2 == m2)
    p1 = 1.0 / z
    p2 = jnp.exp(m2 - m1) / z
    tot = p1 + p2
    return lane, i1, i2, p1 / tot * pgsel, p2 / tot * pgsel


def _outproj(x_ref, oa_ref, od_ref, wo_ref):
    return x_ref[...] + _mm(oa_ref[...], wo_ref[:ATT_WIDTH, :]) + _mm(od_ref[...], wo_ref[ATT_WIDTH:, :])


def _outproj_router_kernel(x_ref, oa_ref, od_t_ref, wo_ref, g_ref, wr_ref, h_ref, xn_ref, gate_ref):
    h = (x_ref[...] + _mm(oa_ref[...], wo_ref[:ATT_WIDTH, :])
         + _mm(od_t_ref[...].T, wo_ref[ATT_WIDTH:, :]))
    h_ref[...] = h
    xn = _rmsnorm(h, g_ref[...]).astype(BF16)
    xn_ref[...] = xn
    lane, i1, i2, g1, g2 = _route(xn, wr_ref[...])
    gate_ref[...] = jnp.where(lane == i1, g1, 0.0) + jnp.where(lane == i2, g2, 0.0)


def _outproj_router(x, oa, od_t, wo, g, wr):
    t = x.shape[0]
    tm = t
    row = lambda n: pl.BlockSpec((tm, n), lambda i: (i, 0))
    full = lambda a: pl.BlockSpec(a.shape, lambda i: (0,) * a.ndim)
    return pl.pallas_call(
        _outproj_router_kernel,
        grid=(t // tm,),
        in_specs=[row(D_MODEL), row(ATT_WIDTH), full(od_t), full(wo), full(g), full(wr)],
        out_specs=[row(D_MODEL), row(D_MODEL), row(LANES)],
        out_shape=[jax.ShapeDtypeStruct((t, D_MODEL), F32), jax.ShapeDtypeStruct((t, D_MODEL), BF16),
                   jax.ShapeDtypeStruct((t, LANES), F32)],
        compiler_params=_params("parallel"),
        name="outproj_router",
    )(x, oa, od_t, wo, g, wr)


MOE_TM = 512
POS_TM = 1024
INFO_G1, INFO_G2, INFO_E1, INFO_E2 = 0, 1, 2, 3


def _moe_tiles(t):
    return (2 * t) // MOE_TM + N_EXPERTS


HALF = D_MODEL // 2
U32 = jnp.uint32


def _pack_rows(x):
    bits = lambda v: lax.bitcast_convert_type(v.astype(BF16).astype(F32), U32)
    return bits(x[:, HALF:]) | (bits(x[:, :HALF]) >> 16)


def _unpack_rows(w):
    lo = lax.bitcast_convert_type(w << 16, F32)
    hi = lax.bitcast_convert_type(w & jnp.uint32(0xFFFF0000), F32)
    return lo, hi


def _route_kernel(x_ref, oa_ref, od_ref, wo_ref, g_ref, wr_ref, h_ref, xn_ref, info_ref, cnt_ref, run_scr):
    h = _outproj(x_ref, oa_ref, od_ref, wo_ref)
    h_ref[...] = h
    xn = _rmsnorm(h, g_ref[...])
    xn_ref[...] = _pack_rows(xn)
    lane, i1, i2, g1, g2 = _route(xn.astype(BF16), wr_ref[...])
    info = jnp.where(lane == INFO_G1, g1, 0.0) + jnp.where(lane == INFO_G2, g2, 0.0)
    info = info + jnp.where(lane == INFO_E1, i1, 0.0) + jnp.where(lane == INFO_E2, i2, 0.0)
    info_ref[...] = info

    @pl.when(pl.program_id(0) == 0)
    def _():
        run_scr[...] = jnp.zeros(run_scr.shape, F32)
    picked = jnp.logical_or(lane == i1, lane == i2).astype(F32)
    run_scr[...] += jnp.sum(picked, axis=0, keepdims=True)
    cnt_ref[...] = run_scr[...]


def _route_sparse(x, oa, od, wo, g, wr):
    t = x.shape[0]
    tm = WIDE_TM
    row = lambda n: pl.BlockSpec((tm, n), lambda i: (i, 0))
    full = lambda a: pl.BlockSpec(a.shape, lambda i: (0,) * a.ndim)
    return pl.pallas_call(
        _route_kernel,
        grid=(t // tm,),
        in_specs=[row(D_MODEL), row(ATT_WIDTH), row(DN_WIDTH), full(wo), full(g), full(wr)],
        out_specs=[row(D_MODEL), row(HALF), row(LANES), pl.BlockSpec((1, LANES), lambda i: (0, 0))],
        out_shape=[jax.ShapeDtypeStruct((t, D_MODEL), F32), jax.ShapeDtypeStruct((t, HALF), U32),
                   jax.ShapeDtypeStruct((t, LANES), F32), jax.ShapeDtypeStruct((1, LANES), F32)],
        scratch_shapes=[pltpu.VMEM((1, LANES), F32)],
        compiler_params=_params("arbitrary"),
        name="route",
    )(x, oa, od, wo, g, wr)


def _positions_kernel(info_ref, cnt_ref, ltri_ref, utri_ref, pos_ref, run_scr, off_scr):
    info = info_ref[...]
    lane = lax.broadcasted_iota(jnp.int32, info.shape, 1).astype(F32)
    hit1 = lane == info[:, INFO_E1:INFO_E1 + 1]
    hit2 = lane == info[:, INFO_E2:INFO_E2 + 1]
    onehot = jnp.logical_or(hit1, hit2).astype(F32)

    @pl.when(pl.program_id(0) == 0)
    def _():
        ln = lax.broadcasted_iota(jnp.int32, cnt_ref.shape, 1)
        is_expert = jnp.logical_and(ln >= ROUTER_OFF, ln < ROUTER_OFF + N_EXPERTS)
        tiles = jnp.where(is_expert, jnp.maximum(jnp.floor((cnt_ref[...] + (MOE_TM - 1)) * (1.0 / MOE_TM)), 1.0), 0.0)
        off_scr[...] = MOE_TM * jnp.dot(tiles.astype(BF16), utri_ref[...], preferred_element_type=F32)
        run_scr[...] = jnp.zeros(run_scr.shape, F32)

    before = (jnp.dot(ltri_ref[...], onehot.astype(BF16), preferred_element_type=F32)
              + run_scr[...] + off_scr[...])
    pos1 = jnp.sum(jnp.where(hit1, before, 0.0), axis=-1, keepdims=True)
    pos2 = jnp.sum(jnp.where(hit2, before, 0.0), axis=-1, keepdims=True)
    both = jnp.where(lane == 0, pos1, 0.0) + jnp.where(lane == 1, pos2, 0.0)
    pos_ref[...] = both.T.astype(jnp.int32)
    run_scr[...] += jnp.sum(onehot, axis=0, keepdims=True)


def _positions(info, cnt):
    t = info.shape[0]
    tm = min(t, POS_TM)
    tok = np.arange(tm)
    ltri = jnp.asarray((tok[:, None] > tok[None, :]).astype(np.float32), dtype=BF16)
    ln = np.arange(LANES)
    utri = jnp.asarray((ln[:, None] < ln[None, :]).astype(np.float32), dtype=BF16)
    full = lambda a: pl.BlockSpec(a.shape, lambda i: (0,) * a.ndim)
    return pl.pallas_call(
        _positions_kernel,
        grid=(t // tm,),
        in_specs=[pl.BlockSpec((tm, LANES), lambda i: (i, 0)), full(cnt), full(ltri), full(utri)],
        out_specs=pl.BlockSpec((LANES, tm), lambda i: (0, i)),
        out_shape=jax.ShapeDtypeStruct((LANES, t), jnp.int32),
        scratch_shapes=[pltpu.VMEM((1, LANES), F32), pltpu.VMEM((1, LANES), F32)],
        compiler_params=_params("arbitrary"),
        name="positions",
    )(info, cnt, ltri, utri)


def _experts_kernel(te_ref, tv_ref, nt_ref, xs_ref, wg_hbm, wu_hbm, wd_hbm, xn_new_ref, gate_new_ref,
                    ys_ref, moe_new_ref, wg_s, wu_s, wd_s, wg_f, wu_f, wd_f, wsem):
    i = pl.program_id(0)
    used = i < nt_ref[0]
    expert = te_ref[i]

    def fetch(e):
        slot = e % 2
        return [pltpu.make_async_copy(src.at[e], dst.at[slot], wsem.at[slot, j])
                for j, (src, dst) in enumerate(((wg_hbm, wg_f), (wu_hbm, wu_f), (wd_hbm, wd_f)))]

    @pl.when(jnp.logical_or(i == 0, expert != te_ref[jnp.maximum(i - 1, 0)]))
    def _():
        @pl.when(i == 0)
        def _():
            for c in fetch(expert):
                c.start()
        for c in fetch(expert):
            c.wait()

        @pl.when(expert + 1 < N_EXPERTS)
        def _():
            for c in fetch(expert + 1):
                c.start()
        slot = expert % 2
        wg_s[...] = wg_f[slot].astype(BF16)
        wu_s[...] = wu_f[slot].astype(BF16)
        wd_s[...] = wd_f[slot].astype(BF16)
        xn = xn_new_ref[...]
        lane = lax.broadcasted_iota(jnp.int32, gate_new_ref.shape, 1)
        gate = jnp.sum(jnp.where(lane == expert + ROUTER_OFF, gate_new_ref[...], 0.0), axis=-1, keepdims=True)
        hg = jnp.dot(xn, wg_s[...], preferred_element_type=F32)
        hu = jnp.dot(xn, wu_s[...], preferred_element_type=F32)
        hm = _silu(hg) * hu * gate
        y = jnp.dot(hm.astype(BF16), wd_s[...], preferred_element_type=F32)

        @pl.when(i == 0)
        def _():
            moe_new_ref[...] = y

        @pl.when(i > 0)
        def _():
            moe_new_ref[...] += y

    @pl.when(used)
    def _():
        row = lax.broadcasted_iota(jnp.int32, xs_ref.shape, 0)
        x_lo, x_hi = _unpack_rows(jnp.where(row < tv_ref[i], xs_ref[...], jnp.uint32(0)))
        x_lo = x_lo.astype(BF16)
        x_hi = x_hi.astype(BF16)
        up = lambda w_s: (jnp.dot(x_lo, w_s[:HALF, :], preferred_element_type=F32)
                          + jnp.dot(x_hi, w_s[HALF:, :], preferred_element_type=F32))
        hm = (_silu_tanh(up(wg_s)) * up(wu_s)).astype(BF16)
        ys_ref[...] = _pack_rows(jnp.dot(hm, wd_s[...], preferred_element_type=F32))

    @pl.when(jnp.logical_not(used))
    def _():
        ys_ref[...] = jnp.zeros(ys_ref.shape, U32)


def _experts(xs, tile_expert, tile_valid, n_tiles, wg, wu, wd, xn_new, gate_new):
    max_tiles = xs.shape[0] // MOE_TM
    rows = pl.BlockSpec((MOE_TM, HALF), lambda i, te, tv, nt: (i, 0))
    hbm = pl.BlockSpec(memory_space=pl.ANY)
    full = lambda a: pl.BlockSpec(a.shape, lambda i, te, tv, nt: (0,) * a.ndim)
    return pl.pallas_call(
        _experts_kernel,
        grid_spec=pltpu.PrefetchScalarGridSpec(
            num_scalar_prefetch=3, grid=(max_tiles,),
            in_specs=[rows, hbm, hbm, hbm, full(xn_new), full(gate_new)],
            out_specs=[rows, pl.BlockSpec(xn_new.shape, lambda i, te, tv, nt: (0, 0))],
            scratch_shapes=[pltpu.VMEM((D_MODEL, D_EXPERT), BF16), pltpu.VMEM((D_MODEL, D_EXPERT), BF16),
                            pltpu.VMEM((D_EXPERT, D_MODEL), BF16),
                            pltpu.VMEM((2, D_MODEL, D_EXPERT), F32), pltpu.VMEM((2, D_MODEL, D_EXPERT), F32),
                            pltpu.VMEM((2, D_EXPERT, D_MODEL), F32), pltpu.SemaphoreType.DMA((2, 3))]),
        out_shape=[jax.ShapeDtypeStruct(xs.shape, U32), jax.ShapeDtypeStruct(xn_new.shape, F32)],
        compiler_params=_params("arbitrary"),
        name="experts",
    )(tile_expert, tile_valid, n_tiles, xs, wg, wu, wd, xn_new, gate_new)


SC_IDX = 128
SC_ROWS = 64
SC_WORKERS = 32
SC_GATHER_ROWS = 16
SC_GATHER_BUFS = 8


def _sc_mesh():
    return plsc.VectorSubcoreMesh(core_axis_name="c", subcore_axis_name="s")


def _sc_windows(t, fn):
    per_worker = t // SC_WORKERS
    worker = lax.axis_index(("c", "s"))

    @pl.loop(0, per_worker // SC_IDX)
    def _(w):
        fn(worker * per_worker + w * SC_IDX)


def _sc_scatter_rows(xn, pos1, pos2, n_rows):
    t, d = xn.shape
    assert t % (SC_WORKERS * SC_IDX) == 0
    idx_t = pltpu.VMEM((1, SC_IDX), jnp.int32)

    @pl.kernel(out_type=jax.ShapeDtypeStruct((n_rows, d), xn.dtype), mesh=_sc_mesh(),
               scratch_types=[idx_t, idx_t, pltpu.VMEM((SC_ROWS, d), xn.dtype)])
    def scatter(x_hbm, p1_hbm, p2_hbm, o_hbm, i1_v, i2_v, buf):
        def window(base):
            pltpu.sync_copy(p1_hbm.at[:, pl.ds(base, SC_IDX)], i1_v)
            pltpu.sync_copy(p2_hbm.at[:, pl.ds(base, SC_IDX)], i2_v)
            for k in range(SC_IDX // SC_ROWS):
                pltpu.sync_copy(x_hbm.at[pl.ds(base + k * SC_ROWS, SC_ROWS)], buf)
                pltpu.sync_copy(buf, o_hbm.at[i1_v.at[0, pl.ds(k * SC_ROWS, SC_ROWS)]])
                pltpu.sync_copy(buf, o_hbm.at[i2_v.at[0, pl.ds(k * SC_ROWS, SC_ROWS)]])
        _sc_windows(t, window)

    return scatter(xn, pos1.reshape(1, t), pos2.reshape(1, t))


def _sc_gather_rows(ys, pos1, pos2):
    t = pos1.shape[0]
    d = ys.shape[1]
    assert t % (SC_WORKERS * SC_IDX) == 0
    per_worker = t // SC_WORKERS
    idx_t = pltpu.VMEM((1, per_worker), jnp.int32)
    out = jax.ShapeDtypeStruct((t, d), ys.dtype)

    nbuf, rows = SC_GATHER_BUFS, SC_GATHER_ROWS
    buf_t = pltpu.VMEM((rows, d), ys.dtype)

    @pl.kernel(out_type=(out, out), mesh=_sc_mesh(),
               scratch_types=[idx_t, idx_t] + [buf_t] * nbuf
                             + [pltpu.SemaphoreType.DMA((nbuf,)), pltpu.SemaphoreType.DMA((nbuf,))])
    def gather(y_hbm, p1_hbm, p2_hbm, o1_hbm, o2_hbm, i1_v, i2_v, *rest):
        bufs, (gsem, wsem) = rest[:nbuf], rest[nbuf:]
        base = lax.axis_index(("c", "s")) * per_worker
        pltpu.sync_copy(p1_hbm.at[:, pl.ds(base, per_worker)], i1_v)
        pltpu.sync_copy(p2_hbm.at[:, pl.ds(base, per_worker)], i2_v)
        items = [(idx_v, o_hbm, k) for k in range(per_worker // rows)
                 for idx_v, o_hbm in ((i1_v, o1_hbm), (i2_v, o2_hbm))]
        n_items = len(items)

        def read(n):
            idx_v, _, k = items[n]
            return pltpu.make_async_copy(y_hbm.at[idx_v.at[0, pl.ds(k * rows, rows)]],
                                         bufs[n % nbuf], gsem.at[n % nbuf])

        def write(n):
            _, o_hbm, k = items[n]
            return pltpu.make_async_copy(bufs[n % nbuf], o_hbm.at[pl.ds(base + k * rows, rows)],
                                         wsem.at[n % nbuf])

        for n in range(min(nbuf - 1, n_items)):
            read(n).start()
        waited = 0
        for n in range(n_items):
            read(n).wait()
            write(n).start()
            ahead = n + nbuf - 1
            if ahead < n_items:
                if n >= 1:
                    write(n - 1).wait()
                    waited = n
                read(ahead).start()
        for n in range(waited, n_items):
            write(n).wait()

    return gather(ys, pos1.reshape(1, t), pos2.reshape(1, t))


def _ple_sparse_kernel(h_ref, info_ref, y1_ref, y2_ref, p_ref, wpp_ref, wpg_ref, gp_ref, gf_ref, y_ref):
    info = info_ref[...]
    g1 = info[:, INFO_G1:INFO_G1 + 1]
    g2 = info[:, INFO_G2:INFO_G2 + 1]
    y1_lo, y1_hi = _unpack_rows(y1_ref[...])
    y2_lo, y2_hi = _unpack_rows(y2_ref[...])
    moe = jnp.concatenate([g1 * y1_lo + g2 * y2_lo, g1 * y1_hi + g2 * y2_hi], axis=1)
    h = h_ref[...] + moe
    hn = _rmsnorm(h, gp_ref[...])
    h = h + _mm(p_ref[...], wpp_ref[...]) * _sigmoid(_mm(hn, wpg_ref[...]))
    y_ref[...] = _rmsnorm(h, gf_ref[...])


def _ple_sparse(h, info, y1, y2, p, wpp, wpg, gp, gf):
    t = h.shape[0]
    tm = WIDE_TM
    row = lambda n: pl.BlockSpec((tm, n), lambda i: (i, 0))
    full = lambda a: pl.BlockSpec(a.shape, lambda i: (0,) * a.ndim)
    return pl.pallas_call(
        _ple_sparse_kernel,
        grid=(t // tm,),
        in_specs=[row(D_MODEL), row(LANES), row(HALF), row(HALF), row(PLE_DIM),
                  full(wpp), full(wpg), full(gp), full(gf)],
        out_specs=row(D_MODEL),
        out_shape=jax.ShapeDtypeStruct((t, D_MODEL), F32),
        compiler_params=_params("parallel"),
        name="ple_sparse",
    )(h, info, y1, y2, p, wpp, wpg, gp, gf)


def _tile_tables(cnt, max_tiles):
    tiles_e = jnp.maximum((cnt + (MOE_TM - 1)) // MOE_TM, 1)
    ends = jnp.cumsum(tiles_e)
    n_tiles = ends[-1]
    tile = jnp.arange(max_tiles, dtype=jnp.int32)
    idx = jnp.minimum(tile, n_tiles - 1)
    tile_expert = jnp.sum((idx[:, None] >= ends[None, :]).astype(jnp.int32), axis=1)
    mine = tile_expert[:, None] == jnp.arange(N_EXPERTS, dtype=jnp.int32)[None, :]
    of_mine = lambda v: jnp.sum(jnp.where(mine, v[None, :], 0), axis=1)
    valid = jnp.clip(of_mine(cnt) - (idx - of_mine(ends - tiles_e)) * MOE_TM, 0, MOE_TM)
    tile_valid = jnp.where(tile < n_tiles, valid, 0).astype(jnp.int32)
    return tile_expert, tile_valid, n_tiles.reshape(1)


def _ple_final_kernel(h_ref, m_ref, p_ref, wpp_ref, wpg_ref, gp_ref, gf_ref, y_ref):
    h = h_ref[...] + m_ref[...]
    hn = _rmsnorm(h, gp_ref[...])
    h = h + _mm(p_ref[...], wpp_ref[...]) * _sigmoid(_mm(hn, wpg_ref[...]))
    y_ref[...] = _rmsnorm(h, gf_ref[...])


def _ple_final(h, m, p, wpp, wpg, gp, gf):
    t = h.shape[0]
    tm = min(t, 256)
    row = lambda n: pl.BlockSpec((tm, n), lambda i: (i, 0))
    full = lambda a: pl.BlockSpec(a.shape, lambda i: (0,) * a.ndim)
    return pl.pallas_call(
        _ple_final_kernel,
        grid=(t // tm,),
        in_specs=[row(D_MODEL), row(D_MODEL), row(PLE_DIM), full(wpp), full(wpg), full(gp), full(gf)],
        out_specs=row(D_MODEL),
        out_shape=jax.ShapeDtypeStruct((t, D_MODEL), F32),
        compiler_params=_params("parallel"),
        name="ple_final",
    )(h, m, p, wpp, wpg, gp, gf)


def kernel(x_prompt, x_sample, p_prompt, p_sample, cache_k, cache_v, state_conv, state_S, rel_bias, norm_mix, w_in, att_sink, conv_w, dn_A_log, dn_dt_bias, dn_norm, w_out, norm_ffn, w_router_group, w_router_expert, w_gate, w_up, w_down, w_ple_proj, w_ple_gate, norm_ple, norm_final):
    batch, seq, _ = x_prompt.shape
    nseq = x_sample.shape[0]
    assert x_sample.shape[1] == 1 and norm_mix.shape[0] == 1 and cache_k.shape[2] == WINDOW
    assert seq % GDN_TB == 0 and seq % ATT_BLOCK == 0

    wi = w_in[0]
    o_db = ATT_COLS + CONV_CH
    w_in_re = (wi[:, :o_db].astype(BF16), wi[:, o_db + 2 * DN_HEADS:].astype(BF16),
               jnp.pad(wi[:, o_db:o_db + 2 * DN_HEADS], ((0, 0), (0, LANES - 2 * DN_HEADS))).astype(BF16))
    row = lambda a: a.reshape(1, -1).astype(F32)
    pad_lanes = lambda a, off: jnp.zeros((1, LANES), F32).at[0, off:off + a.shape[0]].set(a)
    alog = pad_lanes(dn_A_log[0], DN_HEADS)
    dtb = pad_lanes(dn_dt_bias[0], DN_HEADS)
    dnx = jnp.tile(dn_norm[0], DN_HEADS).reshape(1, DN_WIDTH)
    w_router = jnp.concatenate(
        [w_router_group[0], w_router_expert[0],
         jnp.zeros((D_MODEL, LANES - N_GROUPS - N_EXPERTS), F32)], axis=1).astype(BF16)
    wo = w_out[0].astype(BF16)
    wg, wu, wd = w_gate[0], w_up[0], w_down[0]
    wpp, wpg = w_ple_proj[0].astype(BF16), w_ple_gate[0].astype(BF16)
    sink = att_sink[0]

    qi = np.arange(ATT_BLOCK)[:, None]
    kj = np.arange(2 * ATT_BLOCK)[None, :]
    bucket_p = jnp.asarray(_t5_bucket_np(qi + ATT_BLOCK - kj))
    bucket_s = jnp.asarray(_t5_bucket_np(WINDOW - np.arange(WINDOW)[None, :]))

    xp = x_prompt.reshape(batch * seq, D_MODEL)
    att_p, qkv_p, dz_p, ba_p, xc_tails = _inproj_conv(xp, row(norm_mix[0]), w_in_re, conv_w[0], seq)
    o_att_p = _attn_prompt(att_p, bucket_p, rel_bias, sink, batch, seq)
    o_dn_p, s_p = _gdn_prompt(qkv_p, dz_p, ba_p, alog, dtb, dnx, batch, seq)
    h1, xn2, info, cnt = _route_sparse(xp, o_att_p, o_dn_p, wo, row(norm_ffn[0]), w_router)
    pos = _positions(info, cnt)
    pos1, pos2 = pos[0], pos[1]
    max_tiles = _moe_tiles(batch * seq)
    cnt_e = cnt[0, ROUTER_OFF:ROUTER_OFF + N_EXPERTS].astype(jnp.int32)
    tile_expert, tile_valid, n_tiles = _tile_tables(cnt_e, max_tiles)
    xs_sorted = _sc_scatter_rows(xn2, pos1, pos2, max_tiles * MOE_TM)

    xs = x_sample.reshape(nseq, D_MODEL)
    att_s, xc_s, dz_s, ba_s = _inproj(xs, row(norm_mix[0]), w_in_re)
    ck_t = jnp.transpose(cache_k[0], (0, 2, 3, 1))
    cv_t = jnp.transpose(cache_v[0], (0, 2, 3, 1))
    o_att_s, ks_t, vs_t = _attn_sample(att_s, ck_t, cv_t, bucket_s, rel_bias, sink)
    sconv_t = jnp.swapaxes(state_conv[0], 0, 1)
    o_dn_s_t, s_s_t = _gdn_sample_lanes(xc_s, dz_s, ba_s, sconv_t, jnp.transpose(state_S[0], (1, 2, 3, 0)),
                                        conv_w[0], alog, dtb, dn_norm[0])
    s_s = jnp.transpose(s_s_t, (3, 0, 1, 2))

    h1_s, xn2_s, gates_s = _outproj_router(xs, o_att_s, o_dn_s_t, wo, row(norm_ffn[0]), w_router)

    ys, moe_s = _experts(xs_sorted, tile_expert, tile_valid, n_tiles, wg, wu, wd, xn2_s, gates_s)
    y1, y2 = _sc_gather_rows(ys, pos1, pos2)
    y_s = _ple_final(h1_s, moe_s, p_sample[0].reshape(nseq, PLE_DIM), wpp, wpg, row(norm_ple[0]),
                     row(norm_final))
    y_p = _ple_sparse(h1, info, y1, y2, p_prompt[0].reshape(batch * seq, PLE_DIM),
                      wpp, wpg, row(norm_ple[0]), row(norm_final))

    att_p3 = att_p.reshape(batch, seq, ATT_COLS)
    kv_shape = (1, batch, WINDOW, ATT_KV_HEADS, HEAD_DIM)
    k_p = att_p3[:, seq - WINDOW:, ATT_WIDTH:ATT_WIDTH + KV_WIDTH].reshape(kv_shape)
    v_p = att_p3[:, seq - WINDOW:, ATT_WIDTH + KV_WIDTH:].reshape(kv_shape)
    conv_p = xc_tails.reshape(batch, -1, TAIL, CONV_CH)[:, -1, TAIL - (CONV_WIDTH - 1):][None]
    k_s = jnp.transpose(ks_t, (0, 3, 1, 2))[None]
    v_s = jnp.transpose(vs_t, (0, 3, 1, 2))[None]
    conv_s = jnp.concatenate([state_conv[0][:, 1:], xc_s[:, None, :]], axis=1)[None]
    return (y_p.reshape(batch, seq, D_MODEL), y_s.reshape(nseq, 1, D_MODEL),
            k_p, v_p, conv_p, s_p[None], k_s, v_s, conv_s, s_s[None])
```

```python
import functools
import math

import numpy as np
import jax
import jax.numpy as jnp
from jax import lax
from jax.experimental import pallas as pl
from jax.experimental.pallas import tpu as pltpu
from jax.experimental.pallas import tpu_sc as plsc

F32 = jnp.float32
BF16 = jnp.bfloat16

D_MODEL = 1024
ATT_HEADS = 8
ATT_KV_HEADS = 2
HEAD_DIM = 64
GQA = ATT_HEADS // ATT_KV_HEADS
WINDOW = 128
ATT_BLOCK = 128
N_BUCKETS = 32
DN_HEADS = 8
DN_DK = 64
DN_DV = 64
CONV_WIDTH = 4
DN_CHUNK = 64
ATT_WIDTH = ATT_HEADS * HEAD_DIM
KV_WIDTH = ATT_KV_HEADS * HEAD_DIM
DN_WIDTH = DN_HEADS * DN_DV
CONV_CH = 3 * DN_WIDTH
N_GROUPS = 4
EXPERTS_PER_GROUP = 8
N_EXPERTS = N_GROUPS * EXPERTS_PER_GROUP
D_EXPERT = 256
PLE_DIM = 256
EPS = 1e-6
NEG_INF = float("-inf")

ATT_COLS = ATT_WIDTH + 2 * KV_WIDTH
LANES = 128
ROUTER_OFF = N_GROUPS
VMEM_LIMIT = 48 * 1024 * 1024
ROW_TM = 512
WIDE_TM = 1024


def _params(*sem):
    return pltpu.CompilerParams(dimension_semantics=sem, vmem_limit_bytes=VMEM_LIMIT)


def _mm(a, b):
    return jnp.dot(a.astype(BF16), b.astype(BF16), preferred_element_type=F32)


def _mm_nt(a, b):
    return lax.dot_general(a.astype(BF16), b.astype(BF16), (((1,), (1,)), ((), ())),
                           preferred_element_type=F32)


def _mm_tn(a, b):
    return lax.dot_general(a.astype(BF16), b.astype(BF16), (((0,), (0,)), ((), ())),
                           preferred_element_type=F32)


def _split3(x):
    h1 = x.astype(BF16)
    r1 = x - h1.astype(F32)
    h2 = r1.astype(BF16)
    h3 = (r1 - h2.astype(F32)).astype(BF16)
    return h1, h2, h3


def _mm_sel_rhs(x, sel):
    h1, h2, h3 = _split3(x)
    d = lambda h: jnp.dot(h, sel, preferred_element_type=F32)
    return d(h1) + d(h2) + d(h3)


def _mm_sel_lhs(sel, x):
    h1, h2, h3 = _split3(x)
    d = lambda h: jnp.dot(sel, h, preferred_element_type=F32)
    return d(h1) + d(h2) + d(h3)


def _sigmoid(x):
    return 1.0 / (1.0 + jnp.exp(-x))


def _silu(x):
    return x * _sigmoid(x)


def _silu_tanh(x):
    return x * (0.5 * jnp.tanh(0.5 * x) + 0.5)


def _softplus(x):
    return jnp.maximum(x, 0.0) + jnp.log1p(jnp.exp(-jnp.abs(x)))


def _rmsnorm(x, g):
    return x * lax.rsqrt(jnp.mean(x * x, axis=-1, keepdims=True) + EPS) * g


def _t5_bucket_np(dist):
    max_exact = N_BUCKETS // 2
    d = np.maximum(dist, 0)
    ratio = (np.log(np.maximum(d, 1).astype(np.float32) / np.float32(max_exact))
             / np.float32(math.log(WINDOW / max_exact))).astype(np.float32)
    large = np.minimum(max_exact + (ratio * np.float32(N_BUCKETS - max_exact)).astype(np.int32),
                       N_BUCKETS - 1)
    return np.where(d < max_exact, d, large).astype(np.int32)


def _bias_lookup(bucket, rb_ref, h):
    acc = jnp.zeros(bucket.shape, F32)
    for t in range(N_BUCKETS):
        acc = jnp.where(bucket == t, rb_ref[t, h], acc)
    return acc


def _inproj_kernel(x_ref, g_ref, wa_ref, wz_ref, wb_ref, att_ref, xc_ref, dz_ref, ba_ref):
    xn = _rmsnorm(x_ref[...], g_ref[...]).astype(BF16)
    att_ref[...] = _mm_nt(xn, wa_ref[:ATT_COLS, :])
    xc_ref[...] = _mm_nt(xn, wa_ref[ATT_COLS:, :])
    dz_ref[...] = _mm_nt(xn, wz_ref[...])
    ba_ref[...] = _mm_nt(xn, wb_ref[...])


def _inproj(x, g, w):
    t = x.shape[0]
    tm = min(t, ROW_TM)
    row = lambda n: pl.BlockSpec((tm, n), lambda i: (i, 0))
    full = lambda a: pl.BlockSpec(a.shape, lambda i: (0,) * a.ndim)
    return pl.pallas_call(
        _inproj_kernel,
        grid=(t // tm,),
        in_specs=[row(D_MODEL), full(g)] + [full(a) for a in w],
        out_specs=[row(ATT_COLS), row(CONV_CH), row(DN_WIDTH), row(LANES)],
        out_shape=[jax.ShapeDtypeStruct((t, n), F32) for n in (ATT_COLS, CONV_CH, DN_WIDTH, LANES)],
        compiler_params=_params("parallel"),
        name="inproj",
    )(x, g, *w)


TAIL = 8
PAIR = 2 * DN_DK
N_PAIRS = DN_WIDTH // PAIR


def _head_sums(z, pair_ones):
    hi = z.astype(BF16)
    lw = (z - hi.astype(F32)).astype(BF16)
    d = lambda a, p: jnp.dot(a[:, p * PAIR:(p + 1) * PAIR], pair_ones, preferred_element_type=F32)
    return jnp.concatenate([d(hi, p) + d(lw, p) for p in range(N_PAIRS)], axis=1)


W_T_CHUNK = 256


def _inproj_conv_kernel(x_ref, g_ref, wat_ref, wzt_ref, wbt_ref, cw_ref, ones_ref,
                        att_ref, qkv_ref, dz_ref, ba_ref, tail_ref, xp_scr, wa_ref, wz_ref, wb_ref,
                        *, tiles_per_seq):
    tm = x_ref.shape[0]

    @pl.when(pl.program_id(0) == 0)
    def _():
        for src, dst in ((wat_ref, wa_ref), (wzt_ref, wz_ref), (wbt_ref, wb_ref)):
            for c in range(0, src.shape[0], W_T_CHUNK):
                n = min(W_T_CHUNK, src.shape[0] - c)
                dst[:, c:c + n] = src[c:c + n, :].T

    @pl.when(pl.program_id(0) % tiles_per_seq == 0)
    def _():
        xp_scr[...] = jnp.zeros((TAIL, CONV_CH), F32)

    xn = _rmsnorm(x_ref[...], g_ref[...]).astype(BF16)
    xc = jnp.dot(xn, wa_ref[:, ATT_COLS:], preferred_element_type=F32)
    att_ref[...] = jnp.dot(xn, wa_ref[:, :ATT_COLS], preferred_element_type=F32)
    dz_ref[...] = jnp.dot(xn, wz_ref[...], preferred_element_type=F32)
    ba_ref[...] = jnp.dot(xn, wb_ref[...], preferred_element_type=F32)

    head = jnp.concatenate([xp_scr[...], xc[:TAIL, :]], axis=0)

    def shifted(j):
        return jnp.concatenate([head[TAIL - j:2 * TAIL - j, :], pltpu.roll(xc, j, axis=0)[TAIL:, :]], axis=0)

    y = shifted(3) * cw_ref[0:1, :]
    y = y + shifted(2) * cw_ref[1:2, :]
    y = y + shifted(1) * cw_ref[2:3, :]
    y = y + xc * cw_ref[3:4, :]
    tail = xc[tm - TAIL:, :]
    xp_scr[...] = tail
    tail_ref[0] = tail
    y = _silu_tanh(y)
    q = y[:, :DN_WIDTH]
    k = y[:, DN_WIDTH:2 * DN_WIDTH]
    inv_norm = lax.rsqrt(_head_sums(jnp.concatenate([q * q, k * k], axis=0), ones_ref[...]) + EPS)
    qkv_ref[:, :DN_WIDTH] = q * inv_norm[:tm] * (DN_DK ** -0.5)
    qkv_ref[:, DN_WIDTH:2 * DN_WIDTH] = k * inv_norm[tm:]
    qkv_ref[:, 2 * DN_WIDTH:] = y[:, 2 * DN_WIDTH:]


def _pair_ones():
    lane = np.arange(PAIR)
    return jnp.asarray((lane[:, None] // DN_DV == lane[None, :] // DN_DV).astype(np.float32), dtype=BF16)


def _inproj_conv(x, g, w, conv_w, seq):
    t = x.shape[0]
    tm = ROW_TM
    assert seq % tm == 0
    ones = _pair_ones()
    row = lambda n: pl.BlockSpec((tm, n), lambda i: (i, 0))
    full = lambda a: pl.BlockSpec(a.shape, lambda i: (0,) * a.ndim)
    return pl.pallas_call(
        functools.partial(_inproj_conv_kernel, tiles_per_seq=seq // tm),
        grid=(t // tm,),
        in_specs=[row(D_MODEL), full(g)] + [full(a) for a in w] + [full(conv_w), full(ones)],
        out_specs=[row(ATT_COLS), row(CONV_CH), row(DN_WIDTH), row(LANES),
                   pl.BlockSpec((1, TAIL, CONV_CH), lambda i: (i, 0, 0))],
        out_shape=[jax.ShapeDtypeStruct((t, n), F32) for n in (ATT_COLS, CONV_CH, DN_WIDTH, LANES)]
                  + [jax.ShapeDtypeStruct((t // tm, TAIL, CONV_CH), F32)],
        scratch_shapes=[pltpu.VMEM((TAIL, CONV_CH), F32)]
                       + [pltpu.VMEM((D_MODEL, a.shape[0]), BF16) for a in w],
        compiler_params=_params("arbitrary"),
        name="inproj_conv",
    )(x, g, *w, conv_w, ones)


GROUP_ROWS = GQA * ATT_BLOCK


def _attn_prompt_kernel(cur_ref, prev_ref, bucket_ref, rb_ref, sink_ref, o_ref, bias_scr, sink_scr):
    i = pl.program_id(0)
    nseq = cur_ref.shape[0]

    @pl.when(i == 0)
    def _():
        qi = lax.broadcasted_iota(jnp.int32, (ATT_BLOCK, 2 * ATT_BLOCK), 0)
        kj = lax.broadcasted_iota(jnp.int32, (ATT_BLOCK, 2 * ATT_BLOCK), 1)
        dist = qi + ATT_BLOCK - kj
        band = jnp.logical_and(dist >= 0, dist < WINDOW)
        bucket = bucket_ref[...]
        hrow = lax.broadcasted_iota(jnp.int32, (GROUP_ROWS, 1), 0) // ATT_BLOCK
        for g in range(ATT_KV_HEADS):
            sink_col = jnp.zeros((GROUP_ROWS, 1), F32)
            for hh in range(GQA):
                h = g * GQA + hh
                bias = jnp.where(band, _bias_lookup(bucket, rb_ref, h), NEG_INF)
                bias_scr[0, g, hh * ATT_BLOCK:(hh + 1) * ATT_BLOCK, :] = bias
                bias_scr[1, g, hh * ATT_BLOCK:(hh + 1) * ATT_BLOCK, :] = jnp.where(kj >= ATT_BLOCK, bias, NEG_INF)
                sink_col = jnp.where(hrow == hh, sink_ref[h], sink_col)
            sink_scr[g] = sink_col

    first = (i == 0).astype(jnp.int32)
    probs = [(b, g) for b in range(nseq) for g in range(ATT_KV_HEADS)]
    scores = []
    for b, g in probs:
        cur = cur_ref[b]
        prev = prev_ref[b]
        q = jnp.concatenate([cur[:, (g * GQA + hh) * HEAD_DIM:(g * GQA + hh + 1) * HEAD_DIM]
                             for hh in range(GQA)], axis=0) * (HEAD_DIM ** -0.5)
        kcol = slice(ATT_WIDTH + g * HEAD_DIM, ATT_WIDTH + (g + 1) * HEAD_DIM)
        k2 = jnp.concatenate([prev[:, kcol], cur[:, kcol]], axis=0)
        scores.append(_mm_nt(q, k2) + bias_scr[first, g])
    probs_p, dens = [], []
    for (b, g), s in zip(probs, scores):
        sink = sink_scr[g]
        m = jnp.maximum(jnp.max(s, axis=-1, keepdims=True), sink)
        p = jnp.exp(s - m)
        dens.append(jnp.sum(p, axis=-1, keepdims=True) + jnp.exp(sink - m))
        probs_p.append(p.astype(BF16))
    outs = {}
    for (b, g), p, den in zip(probs, probs_p, dens):
        vcol = slice(ATT_WIDTH + KV_WIDTH + g * HEAD_DIM, ATT_WIDTH + KV_WIDTH + (g + 1) * HEAD_DIM)
        v2 = jnp.concatenate([prev_ref[b][:, vcol], cur_ref[b][:, vcol]], axis=0)
        outs[b, g] = _mm(p, v2) / den
    for b in range(nseq):
        o_ref[b] = jnp.concatenate([outs[b, g][hh * ATT_BLOCK:(hh + 1) * ATT_BLOCK, :]
                                    for g in range(ATT_KV_HEADS) for hh in range(GQA)],
                                   axis=1).astype(o_ref.dtype)


def _attn_prompt(att, bucket, rel_bias, sink, batch, seq):
    nb = seq // ATT_BLOCK
    smem = pl.BlockSpec(memory_space=pltpu.SMEM)
    att3 = att.reshape(batch, seq, ATT_COLS)
    out = pl.pallas_call(
        _attn_prompt_kernel,
        grid=(nb,),
        in_specs=[
            pl.BlockSpec((batch, ATT_BLOCK, ATT_COLS), lambda i: (0, i, 0)),
            pl.BlockSpec((batch, ATT_BLOCK, ATT_COLS), lambda i: (0, jnp.maximum(i - 1, 0), 0)),
            pl.BlockSpec(bucket.shape, lambda i: (0, 0)),
            smem, smem,
        ],
        out_specs=pl.BlockSpec((batch, ATT_BLOCK, ATT_WIDTH), lambda i: (0, i, 0)),
        out_shape=jax.ShapeDtypeStruct((batch, seq, ATT_WIDTH), BF16),
        scratch_shapes=[pltpu.VMEM((2, ATT_KV_HEADS, GROUP_ROWS, 2 * ATT_BLOCK), F32),
                        pltpu.VMEM((ATT_KV_HEADS, GROUP_ROWS, 1), F32)],
        compiler_params=_params("arbitrary"),
        name="attn_prompt",
    )(att3, att3, bucket, rel_bias, sink)
    return out.reshape(batch * seq, ATT_WIDTH)


ATT_S_BB = 8


def _attn_sample_kernel(att_ref, ck_ref, cv_ref, bucket_ref, rb_ref, sink_ref, o_ref, ks_ref, vs_ref,
                        bias_scr, col_scr):
    hrow = lax.broadcasted_iota(jnp.int32, (ATT_HEADS, LANES), 0)
    lane = lax.broadcasted_iota(jnp.int32, (ATT_HEADS, LANES), 1)

    last = (lax.broadcasted_iota(jnp.int32, (3, WINDOW), 1) == WINDOW - 1).astype(BF16)
    is_last = lax.broadcasted_iota(jnp.int32, (KV_WIDTH, WINDOW), 1) == WINDOW - 1

    def shifted(cache_t, new_row):
        pieces = jnp.concatenate([p.astype(F32) for p in _split3(new_row)], axis=0).astype(BF16)
        col = lax.dot_general(pieces, last, (((0,), (0,)), ((), ())), preferred_element_type=F32)
        out = jnp.where(is_last, col, pltpu.roll(cache_t, WINDOW - 1, axis=1))
        return out.reshape(ATT_KV_HEADS, HEAD_DIM, WINDOW)

    for b in range(ATT_S_BB):
        row = att_ref[b:b + 1, :]
        ks_ref[b] = shifted(ck_ref[b].reshape(KV_WIDTH, WINDOW), row[:, ATT_WIDTH:ATT_WIDTH + KV_WIDTH])
        vs_ref[b] = shifted(cv_ref[b].reshape(KV_WIDTH, WINDOW), row[:, ATT_WIDTH + KV_WIDTH:])

    @pl.when(pl.program_id(0) == 0)
    def _():
        bucket = jnp.broadcast_to(bucket_ref[...], (ATT_HEADS, LANES))
        bias = jnp.zeros((ATT_HEADS, LANES), F32)
        cols = jnp.zeros((ATT_HEADS, LANES), F32)
        for h in range(ATT_HEADS):
            bias = jnp.where(hrow == h, _bias_lookup(bucket, rb_ref, h), bias)
            cols = jnp.where(jnp.logical_and(hrow == h, lane == 0), sink_ref[h], cols)
            cols = jnp.where(jnp.logical_and(hrow == h, lane == 1), rb_ref[0, h], cols)
        bias_scr[...] = jnp.where(lane >= 1, bias, NEG_INF)
        col_scr[...] = cols

    bias_c = bias_scr[...]
    sink = col_scr[:, 0:1]
    bias_n = col_scr[:, 1:2]
    same_group = (hrow // GQA) == (lane // HEAD_DIM)
    low_group = lax.broadcasted_iota(jnp.int32, (ATT_HEADS, HEAD_DIM), 0) < GQA
    rnd = lambda a: a.astype(BF16).astype(F32)
    seqs = range(ATT_S_BB)
    rows = [att_ref[b:b + 1, :] for b in seqs]
    q_bds = []
    for row in rows:
        q = row[:, :ATT_WIDTH] * (HEAD_DIM ** -0.5)
        qh = jnp.concatenate([q[:, h * HEAD_DIM:(h + 1) * HEAD_DIM] for h in range(ATT_HEADS)], axis=0)
        q_bds.append(jnp.where(same_group, jnp.concatenate([qh, qh], axis=1), 0.0))
    kv_t = lambda ref, b: ref[b].reshape(KV_WIDTH, WINDOW)
    s_cs = [_mm(q_bd, kv_t(ck_ref, b)) + bias_c for b, q_bd in zip(seqs, q_bds)]
    prs, pns = [], []
    for row, q_bd, s_c in zip(rows, q_bds, s_cs):
        kn = row[:, ATT_WIDTH:ATT_WIDTH + KV_WIDTH]
        s_n = jnp.sum(rnd(q_bd) * rnd(kn), axis=-1, keepdims=True) + bias_n
        m = jnp.maximum(jnp.maximum(jnp.max(s_c, axis=-1, keepdims=True), s_n), sink)
        p_c = jnp.exp(s_c - m)
        p_n = jnp.exp(s_n - m)
        den = jnp.sum(p_c, axis=-1, keepdims=True) + p_n + jnp.exp(sink - m)
        prs.append(p_c / den)
        pns.append(p_n / den)
    pvs = [_mm_nt(pr, kv_t(cv_ref, b)) for b, pr in zip(seqs, prs)]
    for b, row, pv, pn in zip(seqs, rows, pvs, pns):
        vn = row[:, ATT_WIDTH + KV_WIDTH:]
        o_full = pv + rnd(pn) * rnd(vn)
        o_sel = jnp.where(low_group, o_full[:, :HEAD_DIM], o_full[:, HEAD_DIM:])
        o_ref[b:b + 1, :] = jnp.concatenate([o_sel[h:h + 1, :] for h in range(ATT_HEADS)], axis=1)


def _attn_sample(att, ck, cv, bucket, rel_bias, sink):
    nseq = att.shape[0]
    smem = pl.BlockSpec(memory_space=pltpu.SMEM)
    cache = pl.BlockSpec((ATT_S_BB, ATT_KV_HEADS, HEAD_DIM, WINDOW), lambda i: (i, 0, 0, 0))
    return pl.pallas_call(
        _attn_sample_kernel,
        grid=(nseq // ATT_S_BB,),
        in_specs=[pl.BlockSpec((ATT_S_BB, ATT_COLS), lambda i: (i, 0)), cache, cache,
                  pl.BlockSpec(bucket.shape, lambda i: (0, 0)), smem, smem],
        out_specs=[pl.BlockSpec((ATT_S_BB, ATT_WIDTH), lambda i: (i, 0)), cache, cache],
        out_shape=[jax.ShapeDtypeStruct((nseq, ATT_WIDTH), F32),
                   jax.ShapeDtypeStruct(ck.shape, F32), jax.ShapeDtypeStruct(cv.shape, F32)],
        scratch_shapes=[pltpu.VMEM((ATT_HEADS, LANES), F32), pltpu.VMEM((ATT_HEADS, LANES), F32)],
        compiler_params=_params("arbitrary"),
        name="attn_sample",
    )(att, ck, cv, bucket, rel_bias, sink)


GDN_TB = 128
GDN_NC = GDN_TB // DN_CHUNK


def _gdn_gates(ba, alog, dtb):
    beta = _sigmoid(ba)
    g = -jnp.exp(alog) * _softplus(ba + dtb)
    return beta, g


def _pair_diag(x, lo):
    xb = x.astype(BF16)
    zero = jnp.zeros_like(xb)
    return jnp.concatenate([jnp.where(lo, xb, zero), jnp.where(lo, zero, xb)], axis=0)


def _gdn_prompt_kernel(qkv_ref, dz_ref, ba_ref, alog_ref, dtb_ref, dnx_ref,
                       hsum_ref, expb_ref, expg_ref, ltri_ref,
                       o_ref, s_out_ref, s_scr):
    i = pl.program_id(0)
    nb = qkv_ref.shape[0]

    @pl.when(i == 0)
    def _():
        s_scr[...] = jnp.zeros(s_scr.shape, F32)

    hsum = hsum_ref[...]
    ri = lax.broadcasted_iota(jnp.int32, (DN_CHUNK, PAIR), 0)
    ci = lax.broadcasted_iota(jnp.int32, (DN_CHUNK, PAIR), 1)
    lo = ci < DN_DK
    cj = jnp.where(lo, ci, ci - DN_DK)
    causal = ri >= cj
    strict = ri > cj
    eye = (ri == cj).astype(F32)

    def sel2(x, m):
        hi = x.astype(BF16)
        lw = (x - hi.astype(F32)).astype(BF16)
        return (jnp.dot(hi, m, preferred_element_type=F32) + jnp.dot(lw, m, preferred_element_type=F32))

    pre = []
    for b in range(nb):
        q = qkv_ref[b, :, :DN_WIDTH]
        k = qkv_ref[b, :, DN_WIDTH:2 * DN_WIDTH]
        v = qkv_ref[b, :, 2 * DN_WIDTH:]
        beta_c, g_c = _gdn_gates(ba_ref[b], alog_ref[...], dtb_ref[...])
        beta = sel2(beta_c, expb_ref[...])
        gam_c = _mm_sel_lhs(ltri_ref[...], g_c)
        gam = _mm_sel_rhs(gam_c, expg_ref[...])
        gam_t = gam_c.T
        kb = k * beta
        egam = jnp.exp(gam)
        pre.append(dict(q=q, k=k, kb=kb, vb=v * beta, qg=q * egam, wr=kb * egam, gam=gam, gam_t=gam_t))

    probs = [(c, b, p) for c in range(GDN_NC) for b in range(nb) for p in range(N_PAIRS)]
    pick = lambda m: jnp.where(lo, m[:DN_DK], m[DN_DK:])
    rows_of = lambda c: slice(c * DN_CHUNK, (c + 1) * DN_CHUNK)
    sl = lambda name, c, b, p: pre[b][name][rows_of(c), p * PAIR:(p + 1) * PAIR]
    raws = []
    for c, b, p in probs:
        k_p = sl("k", c, b, p)
        k_rows = jnp.concatenate([jnp.where(lo, k_p, 0.0), jnp.where(lo, 0.0, k_p)], axis=0)
        raws.append(_mm_nt(jnp.concatenate([sl("kb", c, b, p), sl("q", c, b, p)], axis=0), k_rows))
    pws, ts, qks = [], [], []
    for (c, b, p), raw in zip(probs, raws):
        gcol = sl("gam", c, b, p)
        h0 = DN_HEADS + 2 * p
        gam_t = pre[b]["gam_t"]
        grow = jnp.concatenate([gam_t[h0:h0 + 1, rows_of(c)], gam_t[h0 + 1:h0 + 2, rows_of(c)]], axis=1)
        decay = jnp.exp(jnp.where(causal, gcol - grow, NEG_INF))
        a = jnp.where(strict, raw[:DN_CHUNK] * decay, 0.0)
        qks.append(jnp.where(causal, raw[DN_CHUNK:] * decay, 0.0))
        pws.append(-a)
        ts.append(eye - a)
    pws = [_mm(pw, _pair_diag(pw, lo)) for pw in pws]
    for _ in range(4):
        rs = [_mm(jnp.concatenate([pw, t], axis=0), _pair_diag(pw, lo)) for pw, t in zip(pws, ts)]
        pws = [r[:DN_CHUNK] for r in rs]
        ts = [t + r[DN_CHUNK:] for t, r in zip(ts, rs)]
    rs = [_mm(t, _pair_diag(pw, lo)) for pw, t in zip(pws, ts)]
    ts = [t + r for t, r in zip(ts, rs)]
    sols = [_mm(t, jnp.concatenate([_pair_diag(sl("vb", c, b, p), lo), _pair_diag(sl("wr", c, b, p), lo)],
                                   axis=1)) for (c, b, p), t in zip(probs, ts)]
    qkuws = [_mm(qk, jnp.concatenate([_pair_diag(s[:, :PAIR], lo), _pair_diag(s[:, PAIR:], lo)], axis=1))
             for qk, s in zip(qks, sols)]
    crosses, gls = [], []
    for (c, b, p), s in zip(probs, sols):
        last = (c + 1) * DN_CHUNK - 1
        gam_last = pre[b]["gam"][last:last + 1, p * PAIR:(p + 1) * PAIR]
        kd = sl("k", c, b, p) * jnp.exp(gam_last - sl("gam", c, b, p))
        crosses.append(_mm_tn(kd, s))
        gls.append(jnp.exp(gam_last))
    lhs = [jnp.concatenate([pick(cr[:, PAIR:]), sl("qg", c, b, p) - qkuw[:, PAIR:]], axis=0)
           for (c, b, p), cr, qkuw in zip(probs, crosses, qkuws)]

    o_rows = [[] for _ in range(nb)]
    per_chunk = nb * N_PAIRS
    for c in range(GDN_NC):
        sel = slice(c * per_chunk, (c + 1) * per_chunk)
        s_olds = [s_scr[b, p] for _, b, p in probs[sel]]
        rs = [_mm(l, _pair_diag(s_old, lo)) for l, s_old in zip(lhs[sel], s_olds)]
        o_pairs = [[] for _ in range(nb)]
        for (_, b, p), r, s_old, gl, cr, qkuw in zip(probs[sel], rs, s_olds, gls[sel], crosses[sel], qkuws[sel]):
            s_scr[b, p] = gl * s_old - r[:DN_DK] + pick(cr[:, :PAIR])
            o_pairs[b].append(r[DN_DK:] + qkuw[:, :PAIR])
        for b in range(nb):
            o_rows[b].append(jnp.concatenate(o_pairs[b], axis=1))

    o_all = jnp.concatenate([jnp.concatenate(rows, axis=0) for rows in o_rows], axis=0)
    inv_rms = lax.rsqrt(_head_sums(o_all * o_all, hsum) * (1.0 / DN_DV) + EPS)
    for b in range(nb):
        rows = slice(b * GDN_TB, (b + 1) * GDN_TB)
        o_ref[b] = (o_all[rows] * inv_rms[rows] * dnx_ref[...] * _silu_tanh(dz_ref[b])).astype(o_ref.dtype)

    @pl.when(i == pl.num_programs(0) - 1)
    def _():
        for b in range(nb):
            for p in range(N_PAIRS):
                s_p = s_scr[b, p]
                s_out_ref[b, 2 * p] = s_p[:, :DN_DV]
                s_out_ref[b, 2 * p + 1] = s_p[:, DN_DV:]


def _gdn_consts():
    lane = np.arange(DN_WIDTH)
    pl_lane = np.arange(PAIR)
    hsum = (pl_lane[:, None] // DN_DV == pl_lane[None, :] // DN_DV)
    src = np.arange(LANES)
    expb = (src[:, None] == lane[None, :] // DN_DV)
    expg = (src[:, None] == DN_HEADS + lane[None, :] // DN_DV)
    tok = np.arange(GDN_TB)
    ltri = np.logical_and(tok[:, None] >= tok[None, :],
                          tok[:, None] // DN_CHUNK == tok[None, :] // DN_CHUNK)
    as_bf16 = lambda m: jnp.asarray(m.astype(np.float32), dtype=BF16)
    return as_bf16(hsum), as_bf16(expb), as_bf16(expg), as_bf16(ltri)


def _gdn_prompt(xc, dz, ba, alog, dtb, dnx, batch, seq):
    nt = seq // GDN_TB
    hsum, expb, expg, ltri = _gdn_consts()
    row = lambda n: pl.BlockSpec((batch, GDN_TB, n), lambda i: (0, i, 0))
    full = lambda a: pl.BlockSpec(a.shape, lambda i: (0,) * a.ndim)
    consts = (alog, dtb, dnx, hsum, expb, expg, ltri)
    as3d = lambda a: a.reshape(batch, seq, a.shape[-1])
    o, s = pl.pallas_call(
        _gdn_prompt_kernel,
        grid=(nt,),
        in_specs=[row(CONV_CH), row(DN_WIDTH), row(LANES)] + [full(a) for a in consts],
        out_specs=[row(DN_WIDTH),
                   pl.BlockSpec((batch, DN_HEADS, DN_DK, DN_DV), lambda i: (0, 0, 0, 0))],
        out_shape=[jax.ShapeDtypeStruct((batch, seq, DN_WIDTH), BF16),
                   jax.ShapeDtypeStruct((batch, DN_HEADS, DN_DK, DN_DV), F32)],
        scratch_shapes=[pltpu.VMEM((batch, N_PAIRS, DN_DK, PAIR), F32)],
        compiler_params=_params("arbitrary"),
        name="gdn_prompt",
    )(as3d(xc), as3d(dz), as3d(ba), *consts)
    return o.reshape(batch * seq, DN_WIDTH), s


def _gdn_sample_front_kernel(xc_ref, dz_ref, ba_ref, sc_ref, cw_ref, alog_ref, dtb_ref, hsum_ref,
                             q_ref, k_ref, v_ref, dz_t_ref, gates_ref):
    xc = xc_ref[...]
    y = sc_ref[0] * cw_ref[0:1, :]
    y = y + sc_ref[1] * cw_ref[1:2, :]
    y = y + sc_ref[2] * cw_ref[2:3, :]
    y = _silu(y + xc * cw_ref[3:4, :])
    hsum = hsum_ref[...]
    q = y[:, :DN_WIDTH]
    k = y[:, DN_WIDTH:2 * DN_WIDTH]
    q = q * lax.rsqrt(_mm_sel_rhs(q * q, hsum) + EPS) * (DN_DK ** -0.5)
    k = k * lax.rsqrt(_mm_sel_rhs(k * k, hsum) + EPS)
    beta_c, g_c = _gdn_gates(ba_ref[...], alog_ref[...], dtb_ref[...])
    q_ref[...] = q.T
    k_ref[...] = k.T
    v_ref[...] = y[:, 2 * DN_WIDTH:].T
    dz_t_ref[...] = dz_ref[...].T
    gates_ref[0:LANES, :] = beta_c.T
    gates_ref[LANES:, :] = jnp.exp(g_c).T


def _gdn_sample_step_kernel(q_ref, k_ref, v_ref, dz_ref, gates_ref, dn_ref, s_ref, o_ref, s_out_ref):
    h = pl.program_id(0)
    beta = gates_ref[pl.ds(h, 1), :]
    eg = gates_ref[pl.ds(LANES + DN_HEADS + h, 1), :]
    q, k, v = q_ref[...], k_ref[...], v_ref[...]
    w = (k * beta) * eg
    qg = q * eg
    ws = jnp.zeros(v.shape, F32)
    qs = jnp.zeros(v.shape, F32)
    for dk in range(DN_DK):
        s_dk = s_ref[0, dk]
        ws = ws + w[dk:dk + 1, :] * s_dk
        qs = qs + qg[dk:dk + 1, :] * s_dk
    v_new = v * beta - ws
    qk = jnp.sum(q * k, axis=0, keepdims=True)
    o = qs + qk * v_new
    for dk in range(DN_DK):
        s_out_ref[0, dk] = s_ref[0, dk] * eg + k[dk:dk + 1, :] * v_new
    o = o * lax.rsqrt(jnp.mean(o * o, axis=0, keepdims=True) + EPS) * dn_ref[...]
    o_ref[...] = o * _silu(dz_ref[...])


def _gdn_sample_lanes(xc, dz, ba, sconv_t, state_t, conv_w, alog, dtb, dn):
    nseq = xc.shape[0]
    assert nseq == LANES
    lane = np.arange(DN_WIDTH)
    hsum = jnp.asarray((lane[:, None] // DN_DV == lane[None, :] // DN_DV).astype(np.float32), dtype=BF16)
    full = lambda a: pl.BlockSpec(a.shape, lambda i: (0,) * a.ndim)
    cm = jax.ShapeDtypeStruct((DN_WIDTH, nseq), F32)
    front_in = (xc, dz, ba, sconv_t, conv_w, alog, dtb, hsum)
    q_t, k_t, v_t, dz_t, gates_t = pl.pallas_call(
        _gdn_sample_front_kernel,
        grid=(1,),
        in_specs=[full(a) for a in front_in],
        out_specs=[pl.BlockSpec((DN_WIDTH, nseq), lambda i: (0, 0))] * 4
                  + [pl.BlockSpec((2 * LANES, nseq), lambda i: (0, 0))],
        out_shape=[cm, cm, cm, cm, jax.ShapeDtypeStruct((2 * LANES, nseq), F32)],
        compiler_params=_params("arbitrary"),
        name="gdn_sample_front",
    )(*front_in)
    dn_b = jnp.broadcast_to(dn.reshape(DN_DV, 1), (DN_DV, nseq))
    head = pl.BlockSpec((DN_DK, nseq), lambda h: (h, 0))
    st = pl.BlockSpec((1, DN_DK, DN_DV, nseq), lambda h: (h, 0, 0, 0))
    return pl.pallas_call(
        _gdn_sample_step_kernel,
        grid=(DN_HEADS,),
        in_specs=[head, head, head, head, full(gates_t), full(dn_b), st],
        out_specs=[head, st],
        out_shape=[cm, jax.ShapeDtypeStruct(state_t.shape, F32)],
        compiler_params=_params("parallel"),
        name="gdn_sample_step",
    )(q_t, k_t, v_t, dz_t, gates_t, dn_b, state_t)


def _route(xn, wr):
    logits = jnp.dot(xn, wr, preferred_element_type=F32)
    lane = lax.broadcasted_iota(jnp.int32, logits.shape, 1).astype(F32)
    first_at = lambda hit: jnp.min(jnp.where(hit, lane, float(LANES)), axis=-1, keepdims=True)
    glog = jnp.where(lane < N_GROUPS, logits, NEG_INF)
    gmax = jnp.max(glog, axis=-1, keepdims=True)
    gsel = first_at(glog == gmax)
    pgsel = 1.0 / jnp.sum(jnp.exp(glog - gmax), axis=-1, keepdims=True)
    lo = ROUTER_OFF + gsel * EXPERTS_PER_GROUP
    in_group = jnp.logical_and(lane >= lo, lane < lo + EXPERTS_PER_GROUP)
    elog = jnp.where(in_group, logits, NEG_INF)
    m1 = jnp.max(elog, axis=-1, keepdims=True)
    i1 = first_at(elog == m1)
    z = jnp.sum(jnp.exp(elog - m1), axis=-1, keepdims=True)
    elog2 = jnp.where(lane == i1, NEG_INF, elog)
    m2 = jnp.max(elog2, axis=-1, keepdims=True)
    i2 = first_at(elog2 == m2)
    p1 = 1.0 / z
    p2 = jnp.exp(m2 - m1) / z
    tot = p1 + p2
    return lane, i1, i2, p1 / tot * pgsel, p2 / tot * pgsel


def _outproj(x_ref, oa_ref, od_ref, wo_ref):
    return x_ref[...] + _mm(oa_ref[...], wo_ref[:ATT_WIDTH, :]) + _mm(od_ref[...], wo_ref[ATT_WIDTH:, :])


def _outproj_router_kernel(x_ref, oa_ref, od_t_ref, wo_ref, g_ref, wr_ref, h_ref, xn_ref, gate_ref):
    h = (x_ref[...] + _mm(oa_ref[...], wo_ref[:ATT_WIDTH, :])
         + _mm(od_t_ref[...].T, wo_ref[ATT_WIDTH:, :]))
    h_ref[...] = h
    xn = _rmsnorm(h, g_ref[...]).astype(BF16)
    xn_ref[...] = xn
    lane, i1, i2, g1, g2 = _route(xn, wr_ref[...])
    gate_ref[...] = jnp.where(lane == i1, g1, 0.0) + jnp.where(lane == i2, g2, 0.0)


def _outproj_router(x, oa, od_t, wo, g, wr):
    t = x.shape[0]
    tm = t
    row = lambda n: pl.BlockSpec((tm, n), lambda i: (i, 0))
    full = lambda a: pl.BlockSpec(a.shape, lambda i: (0,) * a.ndim)
    return pl.pallas_call(
        _outproj_router_kernel,
        grid=(t // tm,),
        in_specs=[row(D_MODEL), row(ATT_WIDTH), full(od_t), full(wo), full(g), full(wr)],
        out_specs=[row(D_MODEL), row(D_MODEL), row(LANES)],
        out_shape=[jax.ShapeDtypeStruct((t, D_MODEL), F32), jax.ShapeDtypeStruct((t, D_MODEL), BF16),
                   jax.ShapeDtypeStruct((t, LANES), F32)],
        compiler_params=_params("parallel"),
        name="outproj_router",
    )(x, oa, od_t, wo, g, wr)


MOE_TM = 512
POS_TM = 1024
INFO_G1, INFO_G2, INFO_E1, INFO_E2 = 0, 1, 2, 3


def _moe_tiles(t):
    return (2 * t) // MOE_TM + N_EXPERTS


HALF = D_MODEL // 2
U32 = jnp.uint32


def _pack_rows(x):
    bits = lambda v: lax.bitcast_convert_type(v.astype(BF16).astype(F32), U32)
    return bits(x[:, HALF:]) | (bits(x[:, :HALF]) >> 16)


def _unpack_rows(w):
    lo = lax.bitcast_convert_type(w << 16, F32)
    hi = lax.bitcast_convert_type(w & jnp.uint32(0xFFFF0000), F32)
    return lo, hi


def _route_kernel(x_ref, oa_ref, od_ref, wo_ref, g_ref, wr_ref, h_ref, xn_ref, info_ref, cnt_ref, run_scr):
    h = _outproj(x_ref, oa_ref, od_ref, wo_ref)
    h_ref[...] = h
    xn = _rmsnorm(h, g_ref[...])
    xn_ref[...] = _pack_rows(xn)
    lane, i1, i2, g1, g2 = _route(xn.astype(BF16), wr_ref[...])
    info = jnp.where(lane == INFO_G1, g1, 0.0) + jnp.where(lane == INFO_G2, g2, 0.0)
    info = info + jnp.where(lane == INFO_E1, i1, 0.0) + jnp.where(lane == INFO_E2, i2, 0.0)
    info_ref[...] = info

    @pl.when(pl.program_id(0) == 0)
    def _():
        run_scr[...] = jnp.zeros(run_scr.shape, F32)
    picked = jnp.logical_or(lane == i1, lane == i2).astype(F32)
    run_scr[...] += jnp.sum(picked, axis=0, keepdims=True)
    cnt_ref[...] = run_scr[...]


def _route_sparse(x, oa, od, wo, g, wr):
    t = x.shape[0]
    tm = WIDE_TM
    row = lambda n: pl.BlockSpec((tm, n), lambda i: (i, 0))
    full = lambda a: pl.BlockSpec(a.shape, lambda i: (0,) * a.ndim)
    return pl.pallas_call(
        _route_kernel,
        grid=(t // tm,),
        in_specs=[row(D_MODEL), row(ATT_WIDTH), row(DN_WIDTH), full(wo), full(g), full(wr)],
        out_specs=[row(D_MODEL), row(HALF), row(LANES), pl.BlockSpec((1, LANES), lambda i: (0, 0))],
        out_shape=[jax.ShapeDtypeStruct((t, D_MODEL), F32), jax.ShapeDtypeStruct((t, HALF), U32),
                   jax.ShapeDtypeStruct((t, LANES), F32), jax.ShapeDtypeStruct((1, LANES), F32)],
        scratch_shapes=[pltpu.VMEM((1, LANES), F32)],
        compiler_params=_params("arbitrary"),
        name="route",
    )(x, oa, od, wo, g, wr)


def _positions_kernel(info_ref, cnt_ref, ltri_ref, utri_ref, pos_ref, run_scr, off_scr):
    info = info_ref[...]
    lane = lax.broadcasted_iota(jnp.int32, info.shape, 1).astype(F32)
    hit1 = lane == info[:, INFO_E1:INFO_E1 + 1]
    hit2 = lane == info[:, INFO_E2:INFO_E2 + 1]
    onehot = jnp.logical_or(hit1, hit2).astype(F32)

    @pl.when(pl.program_id(0) == 0)
    def _():
        ln = lax.broadcasted_iota(jnp.int32, cnt_ref.shape, 1)
        is_expert = jnp.logical_and(ln >= ROUTER_OFF, ln < ROUTER_OFF + N_EXPERTS)
        tiles = jnp.where(is_expert, jnp.maximum(jnp.floor((cnt_ref[...] + (MOE_TM - 1)) * (1.0 / MOE_TM)), 1.0), 0.0)
        off_scr[...] = MOE_TM * jnp.dot(tiles.astype(BF16), utri_ref[...], preferred_element_type=F32)
        run_scr[...] = jnp.zeros(run_scr.shape, F32)

    before = (jnp.dot(ltri_ref[...], onehot.astype(BF16), preferred_element_type=F32)
              + run_scr[...] + off_scr[...])
    pos1 = jnp.sum(jnp.where(hit1, before, 0.0), axis=-1, keepdims=True)
    pos2 = jnp.sum(jnp.where(hit2, before, 0.0), axis=-1, keepdims=True)
    both = jnp.where(lane == 0, pos1, 0.0) + jnp.where(lane == 1, pos2, 0.0)
    pos_ref[...] = both.T.astype(jnp.int32)
    run_scr[...] += jnp.sum(onehot, axis=0, keepdims=True)


def _positions(info, cnt):
    t = info.shape[0]
    tm = min(t, POS_TM)
    tok = np.arange(tm)
    ltri = jnp.asarray((tok[:, None] > tok[None, :]).astype(np.float32), dtype=BF16)
    ln = np.arange(LANES)
    utri = jnp.asarray((ln[:, None] < ln[None, :]).astype(np.float32), dtype=BF16)
    full = lambda a: pl.BlockSpec(a.shape, lambda i: (0,) * a.ndim)
    return pl.pallas_call(
        _positions_kernel,
        grid=(t // tm,),
        in_specs=[pl.BlockSpec((tm, LANES), lambda i: (i, 0)), full(cnt), full(ltri), full(utri)],
        out_specs=pl.BlockSpec((LANES, tm), lambda i: (0, i)),
        out_shape=jax.ShapeDtypeStruct((LANES, t), jnp.int32),
        scratch_shapes=[pltpu.VMEM((1, LANES), F32), pltpu.VMEM((1, LANES), F32)],
        compiler_params=_params("arbitrary"),
        name="positions",
    )(info, cnt, ltri, utri)


def _experts_kernel(te_ref, tv_ref, nt_ref, xs_ref, wg_hbm, wu_hbm, wd_hbm, xn_new_ref, gate_new_ref,
                    ys_ref, moe_new_ref, wg_s, wu_s, wd_s, wg_f, wu_f, wd_f, wsem):
    i = pl.program_id(0)
    used = i < nt_ref[0]
    expert = te_ref[i]

    def fetch(e):
        slot = e % 2
        return [pltpu.make_async_copy(src.at[e], dst.at[slot], wsem.at[slot, j])
                for j, (src, dst) in enumerate(((wg_hbm, wg_f), (wu_hbm, wu_f), (wd_hbm, wd_f)))]

    @pl.when(jnp.logical_or(i == 0, expert != te_ref[jnp.maximum(i - 1, 0)]))
    def _():
        @pl.when(i == 0)
        def _():
            for c in fetch(expert):
                c.start()
        for c in fetch(expert):
            c.wait()

        @pl.when(expert + 1 < N_EXPERTS)
        def _():
            for c in fetch(expert + 1):
                c.start()
        slot = expert % 2
        wg_s[...] = wg_f[slot].astype(BF16)
        wu_s[...] = wu_f[slot].astype(BF16)
        wd_s[...] = wd_f[slot].astype(BF16)
        xn = xn_new_ref[...]
        lane = lax.broadcasted_iota(jnp.int32, gate_new_ref.shape, 1)
        gate = jnp.sum(jnp.where(lane == expert + ROUTER_OFF, gate_new_ref[...], 0.0), axis=-1, keepdims=True)
        hg = jnp.dot(xn, wg_s[...], preferred_element_type=F32)
        hu = jnp.dot(xn, wu_s[...], preferred_element_type=F32)
        hm = _silu(hg) * hu * gate
        y = jnp.dot(hm.astype(BF16), wd_s[...], preferred_element_type=F32)

        @pl.when(i == 0)
        def _():
            moe_new_ref[...] = y

        @pl.when(i > 0)
        def _():
            moe_new_ref[...] += y

    @pl.when(used)
    def _():
        row = lax.broadcasted_iota(jnp.int32, xs_ref.shape, 0)
        x_lo, x_hi = _unpack_rows(jnp.where(row < tv_ref[i], xs_ref[...], jnp.uint32(0)))
        x_lo = x_lo.astype(BF16)
        x_hi = x_hi.astype(BF16)
        up = lambda w_s: (jnp.dot(x_lo, w_s[:HALF, :], preferred_element_type=F32)
                          + jnp.dot(x_hi, w_s[HALF:, :], preferred_element_type=F32))
        hm = (_silu_tanh(up(wg_s)) * up(wu_s)).astype(BF16)
        ys_ref[...] = _pack_rows(jnp.dot(hm, wd_s[...], preferred_element_type=F32))

    @pl.when(jnp.logical_not(used))
    def _():
        ys_ref[...] = jnp.zeros(ys_ref.shape, U32)


def _experts(xs, tile_expert, tile_valid, n_tiles, wg, wu, wd, xn_new, gate_new):
    max_tiles = xs.shape[0] // MOE_TM
    rows = pl.BlockSpec((MOE_TM, HALF), lambda i, te, tv, nt: (i, 0))
    hbm = pl.BlockSpec(memory_space=pl.ANY)
    full = lambda a: pl.BlockSpec(a.shape, lambda i, te, tv, nt: (0,) * a.ndim)
    return pl.pallas_call(
        _experts_kernel,
        grid_spec=pltpu.PrefetchScalarGridSpec(
            num_scalar_prefetch=3, grid=(max_tiles,),
            in_specs=[rows, hbm, hbm, hbm, full(xn_new), full(gate_new)],
            out_specs=[rows, pl.BlockSpec(xn_new.shape, lambda i, te, tv, nt: (0, 0))],
            scratch_shapes=[pltpu.VMEM((D_MODEL, D_EXPERT), BF16), pltpu.VMEM((D_MODEL, D_EXPERT), BF16),
                            pltpu.VMEM((D_EXPERT, D_MODEL), BF16),
                            pltpu.VMEM((2, D_MODEL, D_EXPERT), F32), pltpu.VMEM((2, D_MODEL, D_EXPERT), F32),
                            pltpu.VMEM((2, D_EXPERT, D_MODEL), F32), pltpu.SemaphoreType.DMA((2, 3))]),
        out_shape=[jax.ShapeDtypeStruct(xs.shape, U32), jax.ShapeDtypeStruct(xn_new.shape, F32)],
        compiler_params=_params("arbitrary"),
        name="experts",
    )(tile_expert, tile_valid, n_tiles, xs, wg, wu, wd, xn_new, gate_new)


SC_IDX = 128
SC_ROWS = 64
SC_WORKERS = 32
SC_GATHER_ROWS = 32
SC_GATHER_BUFS = 4


def _sc_mesh():
    return plsc.VectorSubcoreMesh(core_axis_name="c", subcore_axis_name="s")


def _sc_windows(t, fn):
    per_worker = t // SC_WORKERS
    worker = lax.axis_index(("c", "s"))

    @pl.loop(0, per_worker // SC_IDX)
    def _(w):
        fn(worker * per_worker + w * SC_IDX)


def _sc_scatter_rows(xn, pos1, pos2, n_rows):
    t, d = xn.shape
    assert t % (SC_WORKERS * SC_IDX) == 0
    idx_t = pltpu.VMEM((1, SC_IDX), jnp.int32)

    @pl.kernel(out_type=jax.ShapeDtypeStruct((n_rows, d), xn.dtype), mesh=_sc_mesh(),
               scratch_types=[idx_t, idx_t, pltpu.VMEM((SC_ROWS, d), xn.dtype)])
    def scatter(x_hbm, p1_hbm, p2_hbm, o_hbm, i1_v, i2_v, buf):
        def window(base):
            pltpu.sync_copy(p1_hbm.at[:, pl.ds(base, SC_IDX)], i1_v)
            pltpu.sync_copy(p2_hbm.at[:, pl.ds(base, SC_IDX)], i2_v)
            for k in range(SC_IDX // SC_ROWS):
                pltpu.sync_copy(x_hbm.at[pl.ds(base + k * SC_ROWS, SC_ROWS)], buf)
                pltpu.sync_copy(buf, o_hbm.at[i1_v.at[0, pl.ds(k * SC_ROWS, SC_ROWS)]])
                pltpu.sync_copy(buf, o_hbm.at[i2_v.at[0, pl.ds(k * SC_ROWS, SC_ROWS)]])
        _sc_windows(t, window)

    return scatter(xn, pos1.reshape(1, t), pos2.reshape(1, t))


def _sc_gather_rows(ys, pos1, pos2):
    t = pos1.shape[0]
    d = ys.shape[1]
    assert t % (SC_WORKERS * SC_IDX) == 0
    per_worker = t // SC_WORKERS
    idx_t = pltpu.VMEM((1, per_worker), jnp.int32)
    out = jax.ShapeDtypeStruct((t, d), ys.dtype)

    nbuf, rows = SC_GATHER_BUFS, SC_GATHER_ROWS
    buf_t = pltpu.VMEM((rows, d), ys.dtype)

    @pl.kernel(out_type=(out, out), mesh=_sc_mesh(),
               scratch_types=[idx_t, idx_t] + [buf_t] * nbuf
                             + [pltpu.SemaphoreType.DMA((nbuf,)), pltpu.SemaphoreType.DMA((nbuf,))])
    def gather(y_hbm, p1_hbm, p2_hbm, o1_hbm, o2_hbm, i1_v, i2_v, *rest):
        bufs, (gsem, wsem) = rest[:nbuf], rest[nbuf:]
        base = lax.axis_index(("c", "s")) * per_worker
        pltpu.sync_copy(p1_hbm.at[:, pl.ds(base, per_worker)], i1_v)
        pltpu.sync_copy(p2_hbm.at[:, pl.ds(base, per_worker)], i2_v)
        items = [(idx_v, o_hbm, k) for k in range(per_worker // rows)
                 for idx_v, o_hbm in ((i1_v, o1_hbm), (i2_v, o2_hbm))]
        n_items = len(items)

        def read(n):
            idx_v, _, k = items[n]
            return pltpu.make_async_copy(y_hbm.at[idx_v.at[0, pl.ds(k * rows, rows)]],
                                         bufs[n % nbuf], gsem.at[n % nbuf])

        def write(n):
            _, o_hbm, k = items[n]
            return pltpu.make_async_copy(bufs[n % nbuf], o_hbm.at[pl.ds(base + k * rows, rows)],
                                         wsem.at[n % nbuf])

        for n in range(min(nbuf - 1, n_items)):
            read(n).start()
        waited = 0
        for n in range(n_items):
            read(n).wait()
            write(n).start()
            ahead = n + nbuf - 1
            if ahead < n_items:
                if n >= 1:
                    write(n - 1).wait()
                    waited = n
                read(ahead).start()
        for n in range(waited, n_items):
            write(n).wait()

    return gather(ys, pos1.reshape(1, t), pos2.reshape(1, t))


def _ple_sparse_kernel(h_ref, info_ref, y1_ref, y2_ref, p_ref, wpp_ref, wpg_ref, gp_ref, gf_ref, y_ref):
    info = info_ref[...]
    g1 = info[:, INFO_G1:INFO_G1 + 1]
    g2 = info[:, INFO_G2:INFO_G2 + 1]
    y1_lo, y1_hi = _unpack_rows(y1_ref[...])
    y2_lo, y2_hi = _unpack_rows(y2_ref[...])
    moe = jnp.concatenate([g1 * y1_lo + g2 * y2_lo, g1 * y1_hi + g2 * y2_hi], axis=1)
    h = h_ref[...] + moe
    hn = _rmsnorm(h, gp_ref[...])
    h = h + _mm(p_ref[...], wpp_ref[...]) * _sigmoid(_mm(hn, wpg_ref[...]))
    y_ref[...] = _rmsnorm(h, gf_ref[...])


def _ple_sparse(h, info, y1, y2, p, wpp, wpg, gp, gf):
    t = h.shape[0]
    tm = WIDE_TM
    row = lambda n: pl.BlockSpec((tm, n), lambda i: (i, 0))
    full = lambda a: pl.BlockSpec(a.shape, lambda i: (0,) * a.ndim)
    return pl.pallas_call(
        _ple_sparse_kernel,
        grid=(t // tm,),
        in_specs=[row(D_MODEL), row(LANES), row(HALF), row(HALF), row(PLE_DIM),
                  full(wpp), full(wpg), full(gp), full(gf)],
        out_specs=row(D_MODEL),
        out_shape=jax.ShapeDtypeStruct((t, D_MODEL), F32),
        compiler_params=_params("parallel"),
        name="ple_sparse",
    )(h, info, y1, y2, p, wpp, wpg, gp, gf)


def _tile_tables(cnt, max_tiles):
    tiles_e = jnp.maximum((cnt + (MOE_TM - 1)) // MOE_TM, 1)
    ends = jnp.cumsum(tiles_e)
    n_tiles = ends[-1]
    tile = jnp.arange(max_tiles, dtype=jnp.int32)
    idx = jnp.minimum(tile, n_tiles - 1)
    tile_expert = jnp.sum((idx[:, None] >= ends[None, :]).astype(jnp.int32), axis=1)
    mine = tile_expert[:, None] == jnp.arange(N_EXPERTS, dtype=jnp.int32)[None, :]
    of_mine = lambda v: jnp.sum(jnp.where(mine, v[None, :], 0), axis=1)
    valid = jnp.clip(of_mine(cnt) - (idx - of_mine(ends - tiles_e)) * MOE_TM, 0, MOE_TM)
    tile_valid = jnp.where(tile < n_tiles, valid, 0).astype(jnp.int32)
    return tile_expert, tile_valid, n_tiles.reshape(1)


def _ple_final_kernel(h_ref, m_ref, p_ref, wpp_ref, wpg_ref, gp_ref, gf_ref, y_ref):
    h = h_ref[...] + m_ref[...]
    hn = _rmsnorm(h, gp_ref[...])
    h = h + _mm(p_ref[...], wpp_ref[...]) * _sigmoid(_mm(hn, wpg_ref[...]))
    y_ref[...] = _rmsnorm(h, gf_ref[...])


def _ple_final(h, m, p, wpp, wpg, gp, gf):
    t = h.shape[0]
    tm = min(t, 256)
    row = lambda n: pl.BlockSpec((tm, n), lambda i: (i, 0))
    full = lambda a: pl.BlockSpec(a.shape, lambda i: (0,) * a.ndim)
    return pl.pallas_call(
        _ple_final_kernel,
        grid=(t // tm,),
        in_specs=[row(D_MODEL), row(D_MODEL), row(PLE_DIM), full(wpp), full(wpg), full(gp), full(gf)],
        out_specs=row(D_MODEL),
        out_shape=jax.ShapeDtypeStruct((t, D_MODEL), F32),
        compiler_params=_params("parallel"),
        name="ple_final",
    )(h, m, p, wpp, wpg, gp, gf)


def kernel(x_prompt, x_sample, p_prompt, p_sample, cache_k, cache_v, state_conv, state_S, rel_bias, norm_mix, w_in, att_sink, conv_w, dn_A_log, dn_dt_bias, dn_norm, w_out, norm_ffn, w_router_group, w_router_expert, w_gate, w_up, w_down, w_ple_proj, w_ple_gate, norm_ple, norm_final):
    batch, seq, _ = x_prompt.shape
    nseq = x_sample.shape[0]
    assert x_sample.shape[1] == 1 and norm_mix.shape[0] == 1 and cache_k.shape[2] == WINDOW
    assert seq % GDN_TB == 0 and seq % ATT_BLOCK == 0

    wt = jnp.swapaxes(w_in[0], 0, 1)
    o_db = ATT_COLS + CONV_CH
    w_in_re = (wt[:o_db].astype(BF16), wt[o_db + 2 * DN_HEADS:].astype(BF16),
               jnp.pad(wt[o_db:o_db + 2 * DN_HEADS], ((0, LANES - 2 * DN_HEADS), (0, 0))).astype(BF16))
    row = lambda a: a.reshape(1, -1).astype(F32)
    pad_lanes = lambda a, off: jnp.zeros((1, LANES), F32).at[0, off:off + a.shape[0]].set(a)
    alog = pad_lanes(dn_A_log[0], DN_HEADS)
    dtb = pad_lanes(dn_dt_bias[0], DN_HEADS)
    dnx = jnp.tile(dn_norm[0], DN_HEADS).reshape(1, DN_WIDTH)
    w_router = jnp.concatenate(
        [w_router_group[0], w_router_expert[0],
         jnp.zeros((D_MODEL, LANES - N_GROUPS - N_EXPERTS), F32)], axis=1).astype(BF16)
    wo = w_out[0].astype(BF16)
    wg, wu, wd = w_gate[0], w_up[0], w_down[0]
    wpp, wpg = w_ple_proj[0].astype(BF16), w_ple_gate[0].astype(BF16)
    sink = att_sink[0]

    qi = np.arange(ATT_BLOCK)[:, None]
    kj = np.arange(2 * ATT_BLOCK)[None, :]
    bucket_p = jnp.asarray(_t5_bucket_np(qi + ATT_BLOCK - kj))
    bucket_s = jnp.asarray(_t5_bucket_np(WINDOW - np.arange(WINDOW)[None, :]))

    xp = x_prompt.reshape(batch * seq, D_MODEL)
    att_p, qkv_p, dz_p, ba_p, xc_tails = _inproj_conv(xp, row(norm_mix[0]), w_in_re, conv_w[0], seq)
    o_att_p = _attn_prompt(att_p, bucket_p, rel_bias, sink, batch, seq)
    o_dn_p, s_p = _gdn_prompt(qkv_p, dz_p, ba_p, alog, dtb, dnx, batch, seq)
    h1, xn2, info, cnt = _route_sparse(xp, o_att_p, o_dn_p, wo, row(norm_ffn[0]), w_router)
    pos = _positions(info, cnt)
    pos1, pos2 = pos[0], pos[1]
    max_tiles = _moe_tiles(batch * seq)
    cnt_e = cnt[0, ROUTER_OFF:ROUTER_OFF + N_EXPERTS].astype(jnp.int32)
    tile_expert, tile_valid, n_tiles = _tile_tables(cnt_e, max_tiles)
    xs_sorted = _sc_scatter_rows(xn2, pos1, pos2, max_tiles * MOE_TM)

    xs = x_sample.reshape(nseq, D_MODEL)
    att_s, xc_s, dz_s, ba_s = _inproj(xs, row(norm_mix[0]), w_in_re)
    ck_t = jnp.transpose(cache_k[0], (0, 2, 3, 1))
    cv_t = jnp.transpose(cache_v[0], (0, 2, 3, 1))
    o_att_s, ks_t, vs_t = _attn_sample(att_s, ck_t, cv_t, bucket_s, rel_bias, sink)
    sconv_t = jnp.swapaxes(state_conv[0], 0, 1)
    o_dn_s_t, s_s_t = _gdn_sample_lanes(xc_s, dz_s, ba_s, sconv_t, jnp.transpose(state_S[0], (1, 2, 3, 0)),
                                        conv_w[0], alog, dtb, dn_norm[0])
    s_s = jnp.transpose(s_s_t, (3, 0, 1, 2))

    h1_s, xn2_s, gates_s = _outproj_router(xs, o_att_s, o_dn_s_t, wo, row(norm_ffn[0]), w_router)

    ys, moe_s = _experts(xs_sorted, tile_expert, tile_valid, n_tiles, wg, wu, wd, xn2_s, gates_s)
    y1, y2 = _sc_gather_rows(ys, pos1, pos2)
    y_s = _ple_final(h1_s, moe_s, p_sample[0].reshape(nseq, PLE_DIM), wpp, wpg, row(norm_ple[0]),
                     row(norm_final))
    y_p = _ple_sparse(h1, info, y1, y2, p_prompt[0].reshape(batch * seq, PLE_DIM),
                      wpp, wpg, row(norm_ple[0]), row(norm_final))

    att_p3 = att_p.reshape(batch, seq, ATT_COLS)
    kv_shape = (1, batch, WINDOW, ATT_KV_HEADS, HEAD_DIM)
    k_p = att_p3[:, seq - WINDOW:, ATT_WIDTH:ATT_WIDTH + KV_WIDTH].reshape(kv_shape)
    v_p = att_p3[:, seq - WINDOW:, ATT_WIDTH + KV_WIDTH:].reshape(kv_shape)
    conv_p = xc_tails.reshape(batch, -1, TAIL, CONV_CH)[:, -1, TAIL - (CONV_WIDTH - 1):][None]
    k_s = jnp.transpose(ks_t, (0, 3, 1, 2))[None]
    v_s = jnp.transpose(vs_t, (0, 3, 1, 2))[None]
    conv_s = jnp.concatenate([state_conv[0][:, 1:], xc_s[:, None, :]], axis=1)[None]
    return (y_p.reshape(batch, seq, D_MODEL), y_s.reshape(nseq, 1, D_MODEL),
            k_p, v_p, conv_p, s_p[None], k_s, v_s, conv_s, s_s[None])
```

```python
import functools
import math

import numpy as np
import jax
import jax.numpy as jnp
from jax import lax
from jax.experimental import pallas as pl
from jax.experimental.pallas import tpu as pltpu
from jax.experimental.pallas import tpu_sc as plsc

F32 = jnp.float32
BF16 = jnp.bfloat16

D_MODEL = 1024
ATT_HEADS = 8
ATT_KV_HEADS = 2
HEAD_DIM = 64
GQA = ATT_HEADS // ATT_KV_HEADS
WINDOW = 128
ATT_BLOCK = 128
N_BUCKETS = 32
DN_HEADS = 8
DN_DK = 64
DN_DV = 64
CONV_WIDTH = 4
DN_CHUNK = 64
ATT_WIDTH = ATT_HEADS * HEAD_DIM
KV_WIDTH = ATT_KV_HEADS * HEAD_DIM
DN_WIDTH = DN_HEADS * DN_DV
CONV_CH = 3 * DN_WIDTH
N_GROUPS = 4
EXPERTS_PER_GROUP = 8
N_EXPERTS = N_GROUPS * EXPERTS_PER_GROUP
D_EXPERT = 256
PLE_DIM = 256
EPS = 1e-6
NEG_INF = float("-inf")

ATT_COLS = ATT_WIDTH + 2 * KV_WIDTH
LANES = 128
ROUTER_OFF = N_GROUPS
VMEM_LIMIT = 48 * 1024 * 1024
ROW_TM = 512
WIDE_TM = 1024


def _params(*sem):
    return pltpu.CompilerParams(dimension_semantics=sem, vmem_limit_bytes=VMEM_LIMIT)


def _mm(a, b):
    return jnp.dot(a.astype(BF16), b.astype(BF16), preferred_element_type=F32)


def _mm_nt(a, b):
    return lax.dot_general(a.astype(BF16), b.astype(BF16), (((1,), (1,)), ((), ())),
                           preferred_element_type=F32)


def _mm_tn(a, b):
    return lax.dot_general(a.astype(BF16), b.astype(BF16), (((0,), (0,)), ((), ())),
                           preferred_element_type=F32)


def _split3(x):
    h1 = x.astype(BF16)
    r1 = x - h1.astype(F32)
    h2 = r1.astype(BF16)
    h3 = (r1 - h2.astype(F32)).astype(BF16)
    return h1, h2, h3


def _mm_sel_rhs(x, sel):
    h1, h2, h3 = _split3(x)
    d = lambda h: jnp.dot(h, sel, preferred_element_type=F32)
    return d(h1) + d(h2) + d(h3)


def _mm_sel_lhs(sel, x):
    h1, h2, h3 = _split3(x)
    d = lambda h: jnp.dot(sel, h, preferred_element_type=F32)
    return d(h1) + d(h2) + d(h3)


def _sigmoid(x):
    return 1.0 / (1.0 + jnp.exp(-x))


def _silu(x):
    return x * _sigmoid(x)


def _sigmoid_tanh(x):
    return 0.5 * jnp.tanh(0.5 * x) + 0.5


def _silu_tanh(x):
    return x * _sigmoid_tanh(x)


def _softplus(x):
    return jnp.maximum(x, 0.0) + jnp.log1p(jnp.exp(-jnp.abs(x)))


def _rmsnorm(x, g):
    return x * lax.rsqrt(jnp.mean(x * x, axis=-1, keepdims=True) + EPS) * g


def _t5_bucket_np(dist):
    max_exact = N_BUCKETS // 2
    d = np.maximum(dist, 0)
    ratio = (np.log(np.maximum(d, 1).astype(np.float32) / np.float32(max_exact))
             / np.float32(math.log(WINDOW / max_exact))).astype(np.float32)
    large = np.minimum(max_exact + (ratio * np.float32(N_BUCKETS - max_exact)).astype(np.int32),
                       N_BUCKETS - 1)
    return np.where(d < max_exact, d, large).astype(np.int32)


def _bias_lookup(bucket, rb_ref, h):
    acc = jnp.zeros(bucket.shape, F32)
    for t in range(N_BUCKETS):
        acc = jnp.where(bucket == t, rb_ref[t, h], acc)
    return acc


def _inproj_kernel(x_ref, g_ref, wa_ref, wz_ref, wb_ref, att_ref, xc_ref, dz_ref, ba_ref):
    xn = _rmsnorm(x_ref[...], g_ref[...]).astype(BF16)
    att_ref[...] = _mm_nt(xn, wa_ref[:ATT_COLS, :])
    xc_ref[...] = _mm_nt(xn, wa_ref[ATT_COLS:, :])
    dz_ref[...] = _mm_nt(xn, wz_ref[...])
    ba_ref[...] = _mm_nt(xn, wb_ref[...])


def _inproj(x, g, w):
    t = x.shape[0]
    tm = min(t, ROW_TM)
    row = lambda n: pl.BlockSpec((tm, n), lambda i: (i, 0))
    full = lambda a: pl.BlockSpec(a.shape, lambda i: (0,) * a.ndim)
    return pl.pallas_call(
        _inproj_kernel,
        grid=(t // tm,),
        in_specs=[row(D_MODEL), full(g)] + [full(a) for a in w],
        out_specs=[row(ATT_COLS), row(CONV_CH), row(DN_WIDTH), row(LANES)],
        out_shape=[jax.ShapeDtypeStruct((t, n), F32) for n in (ATT_COLS, CONV_CH, DN_WIDTH, LANES)],
        compiler_params=_params("parallel"),
        name="inproj",
    )(x, g, *w)


TAIL = 8
PAIR = 2 * DN_DK
N_PAIRS = DN_WIDTH // PAIR


def _head_sums(z, pair_ones):
    hi = z.astype(BF16)
    lw = (z - hi.astype(F32)).astype(BF16)
    d = lambda a, p: jnp.dot(a[:, p * PAIR:(p + 1) * PAIR], pair_ones, preferred_element_type=F32)
    return jnp.concatenate([d(hi, p) + d(lw, p) for p in range(N_PAIRS)], axis=1)


W_T_CHUNK = 256


def _inproj_conv_kernel(x_ref, g_ref, wat_ref, wzt_ref, wbt_ref, cw_ref, ones_ref,
                        att_ref, qkv_ref, dz_ref, ba_ref, tail_ref, xp_scr, wa_ref, wz_ref, wb_ref,
                        *, tiles_per_seq):
    tm = x_ref.shape[0]

    @pl.when(pl.program_id(0) == 0)
    def _():
        for src, dst in ((wat_ref, wa_ref), (wzt_ref, wz_ref), (wbt_ref, wb_ref)):
            for c in range(0, src.shape[0], W_T_CHUNK):
                n = min(W_T_CHUNK, src.shape[0] - c)
                dst[:, c:c + n] = src[c:c + n, :].T

    @pl.when(pl.program_id(0) % tiles_per_seq == 0)
    def _():
        xp_scr[...] = jnp.zeros((TAIL, CONV_CH), F32)

    xn = _rmsnorm(x_ref[...], g_ref[...]).astype(BF16)
    xc = jnp.dot(xn, wa_ref[:, ATT_COLS:], preferred_element_type=F32)
    att_ref[...] = jnp.dot(xn, wa_ref[:, :ATT_COLS], preferred_element_type=F32)
    dz_ref[...] = jnp.dot(xn, wz_ref[...], preferred_element_type=F32)
    ba_ref[...] = jnp.dot(xn, wb_ref[...], preferred_element_type=F32)

    head = jnp.concatenate([xp_scr[...], xc[:TAIL, :]], axis=0)

    def shifted(j):
        return jnp.concatenate([head[TAIL - j:2 * TAIL - j, :], pltpu.roll(xc, j, axis=0)[TAIL:, :]], axis=0)

    y = shifted(3) * cw_ref[0:1, :]
    y = y + shifted(2) * cw_ref[1:2, :]
    y = y + shifted(1) * cw_ref[2:3, :]
    y = y + xc * cw_ref[3:4, :]
    tail = xc[tm - TAIL:, :]
    xp_scr[...] = tail
    tail_ref[0] = tail
    y = _silu_tanh(y)
    q = y[:, :DN_WIDTH]
    k = y[:, DN_WIDTH:2 * DN_WIDTH]
    inv_norm = lax.rsqrt(_head_sums(jnp.concatenate([q * q, k * k], axis=0), ones_ref[...]) + EPS)
    qkv_ref[:, :DN_WIDTH] = q * inv_norm[:tm] * (DN_DK ** -0.5)
    qkv_ref[:, DN_WIDTH:2 * DN_WIDTH] = k * inv_norm[tm:]
    qkv_ref[:, 2 * DN_WIDTH:] = y[:, 2 * DN_WIDTH:]


def _pair_ones():
    lane = np.arange(PAIR)
    return jnp.asarray((lane[:, None] // DN_DV == lane[None, :] // DN_DV).astype(np.float32), dtype=BF16)


def _inproj_conv(x, g, w, conv_w, seq):
    t = x.shape[0]
    tm = ROW_TM
    assert seq % tm == 0
    ones = _pair_ones()
    row = lambda n: pl.BlockSpec((tm, n), lambda i: (i, 0))
    full = lambda a: pl.BlockSpec(a.shape, lambda i: (0,) * a.ndim)
    return pl.pallas_call(
        functools.partial(_inproj_conv_kernel, tiles_per_seq=seq // tm),
        grid=(t // tm,),
        in_specs=[row(D_MODEL), full(g)] + [full(a) for a in w] + [full(conv_w), full(ones)],
        out_specs=[row(ATT_COLS), row(CONV_CH), row(DN_WIDTH), row(LANES),
                   pl.BlockSpec((1, TAIL, CONV_CH), lambda i: (i, 0, 0))],
        out_shape=[jax.ShapeDtypeStruct((t, n), F32) for n in (ATT_COLS, CONV_CH, DN_WIDTH, LANES)]
                  + [jax.ShapeDtypeStruct((t // tm, TAIL, CONV_CH), F32)],
        scratch_shapes=[pltpu.VMEM((TAIL, CONV_CH), F32)]
                       + [pltpu.VMEM((D_MODEL, a.shape[0]), BF16) for a in w],
        compiler_params=_params("arbitrary"),
        name="inproj_conv",
    )(x, g, *w, conv_w, ones)


GROUP_ROWS = GQA * ATT_BLOCK


def _attn_prompt_kernel(cur_ref, prev_ref, bucket_ref, rb_ref, sink_ref, o_ref, bias_scr, sink_scr):
    i = pl.program_id(0)
    nseq = cur_ref.shape[0]

    @pl.when(i == 0)
    def _():
        qi = lax.broadcasted_iota(jnp.int32, (ATT_BLOCK, 2 * ATT_BLOCK), 0)
        kj = lax.broadcasted_iota(jnp.int32, (ATT_BLOCK, 2 * ATT_BLOCK), 1)
        dist = qi + ATT_BLOCK - kj
        band = jnp.logical_and(dist >= 0, dist < WINDOW)
        bucket = bucket_ref[...]
        hrow = lax.broadcasted_iota(jnp.int32, (GROUP_ROWS, 1), 0) // ATT_BLOCK
        for g in range(ATT_KV_HEADS):
            sink_col = jnp.zeros((GROUP_ROWS, 1), F32)
            for hh in range(GQA):
                h = g * GQA + hh
                bias = jnp.where(band, _bias_lookup(bucket, rb_ref, h), NEG_INF)
                bias_scr[0, g, hh * ATT_BLOCK:(hh + 1) * ATT_BLOCK, :] = bias
                bias_scr[1, g, hh * ATT_BLOCK:(hh + 1) * ATT_BLOCK, :] = jnp.where(kj >= ATT_BLOCK, bias, NEG_INF)
                sink_col = jnp.where(hrow == hh, sink_ref[h], sink_col)
            sink_scr[g] = sink_col

    first = (i == 0).astype(jnp.int32)
    probs = [(b, g) for b in range(nseq) for g in range(ATT_KV_HEADS)]
    scores = []
    for b, g in probs:
        cur = cur_ref[b]
        prev = prev_ref[b]
        q = jnp.concatenate([cur[:, (g * GQA + hh) * HEAD_DIM:(g * GQA + hh + 1) * HEAD_DIM]
                             for hh in range(GQA)], axis=0) * (HEAD_DIM ** -0.5)
        kcol = slice(ATT_WIDTH + g * HEAD_DIM, ATT_WIDTH + (g + 1) * HEAD_DIM)
        k2 = jnp.concatenate([prev[:, kcol], cur[:, kcol]], axis=0)
        scores.append(_mm_nt(q, k2) + bias_scr[first, g])
    probs_p, dens = [], []
    for (b, g), s in zip(probs, scores):
        sink = sink_scr[g]
        m = jnp.maximum(jnp.max(s, axis=-1, keepdims=True), sink)
        p = jnp.exp(s - m)
        dens.append(jnp.sum(p, axis=-1, keepdims=True) + jnp.exp(sink - m))
        probs_p.append(p.astype(BF16))
    outs = {}
    for (b, g), p, den in zip(probs, probs_p, dens):
        vcol = slice(ATT_WIDTH + KV_WIDTH + g * HEAD_DIM, ATT_WIDTH + KV_WIDTH + (g + 1) * HEAD_DIM)
        v2 = jnp.concatenate([prev_ref[b][:, vcol], cur_ref[b][:, vcol]], axis=0)
        outs[b, g] = _mm(p, v2) / den
    for b in range(nseq):
        o_ref[b] = jnp.concatenate([outs[b, g][hh * ATT_BLOCK:(hh + 1) * ATT_BLOCK, :]
                                    for g in range(ATT_KV_HEADS) for hh in range(GQA)],
                                   axis=1).astype(o_ref.dtype)


def _attn_prompt(att, bucket, rel_bias, sink, batch, seq):
    nb = seq // ATT_BLOCK
    smem = pl.BlockSpec(memory_space=pltpu.SMEM)
    att3 = att.reshape(batch, seq, ATT_COLS)
    out = pl.pallas_call(
        _attn_prompt_kernel,
        grid=(nb,),
        in_specs=[
            pl.BlockSpec((batch, ATT_BLOCK, ATT_COLS), lambda i: (0, i, 0)),
            pl.BlockSpec((batch, ATT_BLOCK, ATT_COLS), lambda i: (0, jnp.maximum(i - 1, 0), 0)),
            pl.BlockSpec(bucket.shape, lambda i: (0, 0)),
            smem, smem,
        ],
        out_specs=pl.BlockSpec((batch, ATT_BLOCK, ATT_WIDTH), lambda i: (0, i, 0)),
        out_shape=jax.ShapeDtypeStruct((batch, seq, ATT_WIDTH), BF16),
        scratch_shapes=[pltpu.VMEM((2, ATT_KV_HEADS, GROUP_ROWS, 2 * ATT_BLOCK), F32),
                        pltpu.VMEM((ATT_KV_HEADS, GROUP_ROWS, 1), F32)],
        compiler_params=_params("arbitrary"),
        name="attn_prompt",
    )(att3, att3, bucket, rel_bias, sink)
    return out.reshape(batch * seq, ATT_WIDTH)


ATT_S_BB = 8


def _attn_sample_kernel(att_ref, ck_ref, cv_ref, bucket_ref, rb_ref, sink_ref, o_ref, ks_ref, vs_ref,
                        bias_scr, col_scr):
    hrow = lax.broadcasted_iota(jnp.int32, (ATT_HEADS, LANES), 0)
    lane = lax.broadcasted_iota(jnp.int32, (ATT_HEADS, LANES), 1)

    last = (lax.broadcasted_iota(jnp.int32, (3, WINDOW), 1) == WINDOW - 1).astype(BF16)
    is_last = lax.broadcasted_iota(jnp.int32, (KV_WIDTH, WINDOW), 1) == WINDOW - 1

    def shifted(cache_t, new_row):
        pieces = jnp.concatenate([p.astype(F32) for p in _split3(new_row)], axis=0).astype(BF16)
        col = lax.dot_general(pieces, last, (((0,), (0,)), ((), ())), preferred_element_type=F32)
        out = jnp.where(is_last, col, pltpu.roll(cache_t, WINDOW - 1, axis=1))
        return out.reshape(ATT_KV_HEADS, HEAD_DIM, WINDOW)

    for b in range(ATT_S_BB):
        row = att_ref[b:b + 1, :]
        ks_ref[b] = shifted(ck_ref[b].reshape(KV_WIDTH, WINDOW), row[:, ATT_WIDTH:ATT_WIDTH + KV_WIDTH])
        vs_ref[b] = shifted(cv_ref[b].reshape(KV_WIDTH, WINDOW), row[:, ATT_WIDTH + KV_WIDTH:])

    @pl.when(pl.program_id(0) == 0)
    def _():
        bucket = jnp.broadcast_to(bucket_ref[...], (ATT_HEADS, LANES))
        bias = jnp.zeros((ATT_HEADS, LANES), F32)
        cols = jnp.zeros((ATT_HEADS, LANES), F32)
        for h in range(ATT_HEADS):
            bias = jnp.where(hrow == h, _bias_lookup(bucket, rb_ref, h), bias)
            cols = jnp.where(jnp.logical_and(hrow == h, lane == 0), sink_ref[h], cols)
            cols = jnp.where(jnp.logical_and(hrow == h, lane == 1), rb_ref[0, h], cols)
        bias_scr[...] = jnp.where(lane >= 1, bias, NEG_INF)
        col_scr[...] = cols

    bias_c = bias_scr[...]
    sink = col_scr[:, 0:1]
    bias_n = col_scr[:, 1:2]
    same_group = (hrow // GQA) == (lane // HEAD_DIM)
    low_group = lax.broadcasted_iota(jnp.int32, (ATT_HEADS, HEAD_DIM), 0) < GQA
    rnd = lambda a: a.astype(BF16).astype(F32)
    seqs = range(ATT_S_BB)
    rows = [att_ref[b:b + 1, :] for b in seqs]
    q_bds = []
    for row in rows:
        q = row[:, :ATT_WIDTH] * (HEAD_DIM ** -0.5)
        qh = jnp.concatenate([q[:, h * HEAD_DIM:(h + 1) * HEAD_DIM] for h in range(ATT_HEADS)], axis=0)
        q_bds.append(jnp.where(same_group, jnp.concatenate([qh, qh], axis=1), 0.0))
    kv_t = lambda ref, b: ref[b].reshape(KV_WIDTH, WINDOW)
    s_cs = [_mm(q_bd, kv_t(ck_ref, b)) + bias_c for b, q_bd in zip(seqs, q_bds)]
    prs, pns = [], []
    for row, q_bd, s_c in zip(rows, q_bds, s_cs):
        kn = row[:, ATT_WIDTH:ATT_WIDTH + KV_WIDTH]
        s_n = jnp.sum(rnd(q_bd) * rnd(kn), axis=-1, keepdims=True) + bias_n
        m = jnp.maximum(jnp.maximum(jnp.max(s_c, axis=-1, keepdims=True), s_n), sink)
        p_c = jnp.exp(s_c - m)
        p_n = jnp.exp(s_n - m)
        den = jnp.sum(p_c, axis=-1, keepdims=True) + p_n + jnp.exp(sink - m)
        prs.append(p_c / den)
        pns.append(p_n / den)
    pvs = [_mm_nt(pr, kv_t(cv_ref, b)) for b, pr in zip(seqs, prs)]
    for b, row, pv, pn in zip(seqs, rows, pvs, pns):
        vn = row[:, ATT_WIDTH + KV_WIDTH:]
        o_full = pv + rnd(pn) * rnd(vn)
        o_sel = jnp.where(low_group, o_full[:, :HEAD_DIM], o_full[:, HEAD_DIM:])
        o_ref[b:b + 1, :] = jnp.concatenate([o_sel[h:h + 1, :] for h in range(ATT_HEADS)], axis=1)


def _attn_sample(att, ck, cv, bucket, rel_bias, sink):
    nseq = att.shape[0]
    smem = pl.BlockSpec(memory_space=pltpu.SMEM)
    cache = pl.BlockSpec((ATT_S_BB, ATT_KV_HEADS, HEAD_DIM, WINDOW), lambda i: (i, 0, 0, 0))
    return pl.pallas_call(
        _attn_sample_kernel,
        grid=(nseq // ATT_S_BB,),
        in_specs=[pl.BlockSpec((ATT_S_BB, ATT_COLS), lambda i: (i, 0)), cache, cache,
                  pl.BlockSpec(bucket.shape, lambda i: (0, 0)), smem, smem],
        out_specs=[pl.BlockSpec((ATT_S_BB, ATT_WIDTH), lambda i: (i, 0)), cache, cache],
        out_shape=[jax.ShapeDtypeStruct((nseq, ATT_WIDTH), F32),
                   jax.ShapeDtypeStruct(ck.shape, F32), jax.ShapeDtypeStruct(cv.shape, F32)],
        scratch_shapes=[pltpu.VMEM((ATT_HEADS, LANES), F32), pltpu.VMEM((ATT_HEADS, LANES), F32)],
        compiler_params=_params("arbitrary"),
        name="attn_sample",
    )(att, ck, cv, bucket, rel_bias, sink)


GDN_TB = 128
GDN_NC = GDN_TB // DN_CHUNK


def _gdn_gates(ba, alog, dtb):
    beta = _sigmoid(ba)
    g = -jnp.exp(alog) * _softplus(ba + dtb)
    return beta, g


def _pair_diag(x, lo):
    xb = x.astype(BF16)
    zero = jnp.zeros_like(xb)
    return jnp.concatenate([jnp.where(lo, xb, zero), jnp.where(lo, zero, xb)], axis=0)


def _gdn_prompt_kernel(qkv_ref, dz_ref, ba_ref, alog_ref, dtb_ref, dnx_ref,
                       hsum_ref, expb_ref, expg_ref, ltri_ref,
                       o_ref, s_out_ref, s_scr):
    i = pl.program_id(0)
    nb = qkv_ref.shape[0]

    @pl.when(i == 0)
    def _():
        s_scr[...] = jnp.zeros(s_scr.shape, F32)

    hsum = hsum_ref[...]
    ri = lax.broadcasted_iota(jnp.int32, (DN_CHUNK, PAIR), 0)
    ci = lax.broadcasted_iota(jnp.int32, (DN_CHUNK, PAIR), 1)
    lo = ci < DN_DK
    cj = jnp.where(lo, ci, ci - DN_DK)
    causal = ri >= cj
    strict = ri > cj
    eye = (ri == cj).astype(F32)

    def sel2(x, m):
        hi = x.astype(BF16)
        lw = (x - hi.astype(F32)).astype(BF16)
        return (jnp.dot(hi, m, preferred_element_type=F32) + jnp.dot(lw, m, preferred_element_type=F32))

    pre = []
    for b in range(nb):
        q = qkv_ref[b, :, :DN_WIDTH]
        k = qkv_ref[b, :, DN_WIDTH:2 * DN_WIDTH]
        v = qkv_ref[b, :, 2 * DN_WIDTH:]
        beta_c, g_c = _gdn_gates(ba_ref[b], alog_ref[...], dtb_ref[...])
        beta = sel2(beta_c, expb_ref[...])
        gam_c = _mm_sel_lhs(ltri_ref[...], g_c)
        gam = _mm_sel_rhs(gam_c, expg_ref[...])
        gam_t = gam_c.T
        kb = k * beta
        egam = jnp.exp(gam)
        pre.append(dict(q=q, k=k, kb=kb, vb=v * beta, qg=q * egam, wr=kb * egam, gam=gam, gam_t=gam_t))

    probs = [(c, b, p) for c in range(GDN_NC) for b in range(nb) for p in range(N_PAIRS)]
    pick = lambda m: jnp.where(lo, m[:DN_DK], m[DN_DK:])
    rows_of = lambda c: slice(c * DN_CHUNK, (c + 1) * DN_CHUNK)
    sl = lambda name, c, b, p: pre[b][name][rows_of(c), p * PAIR:(p + 1) * PAIR]
    raws = []
    for c, b, p in probs:
        k_p = sl("k", c, b, p)
        k_rows = jnp.concatenate([jnp.where(lo, k_p, 0.0), jnp.where(lo, 0.0, k_p)], axis=0)
        raws.append(_mm_nt(jnp.concatenate([sl("kb", c, b, p), sl("q", c, b, p)], axis=0), k_rows))
    pws, ts, qks = [], [], []
    for (c, b, p), raw in zip(probs, raws):
        gcol = sl("gam", c, b, p)
        h0 = DN_HEADS + 2 * p
        gam_t = pre[b]["gam_t"]
        grow = jnp.concatenate([gam_t[h0:h0 + 1, rows_of(c)], gam_t[h0 + 1:h0 + 2, rows_of(c)]], axis=1)
        decay = jnp.exp(jnp.where(causal, gcol - grow, NEG_INF))
        a = jnp.where(strict, raw[:DN_CHUNK] * decay, 0.0)
        qks.append(jnp.where(causal, raw[DN_CHUNK:] * decay, 0.0))
        pws.append(-a)
        ts.append(eye - a)
    pws = [_mm(pw, _pair_diag(pw, lo)) for pw in pws]
    for _ in range(4):
        rs = [_mm(jnp.concatenate([pw, t], axis=0), _pair_diag(pw, lo)) for pw, t in zip(pws, ts)]
        pws = [r[:DN_CHUNK] for r in rs]
        ts = [t + r[DN_CHUNK:] for t, r in zip(ts, rs)]
    rs = [_mm(t, _pair_diag(pw, lo)) for pw, t in zip(pws, ts)]
    ts = [t + r for t, r in zip(ts, rs)]
    sols = [_mm(t, jnp.concatenate([_pair_diag(sl("vb", c, b, p), lo), _pair_diag(sl("wr", c, b, p), lo)],
                                   axis=1)) for (c, b, p), t in zip(probs, ts)]
    qkuws = [_mm(qk, jnp.concatenate([_pair_diag(s[:, :PAIR], lo), _pair_diag(s[:, PAIR:], lo)], axis=1))
             for qk, s in zip(qks, sols)]
    crosses, gls = [], []
    for (c, b, p), s in zip(probs, sols):
        last = (c + 1) * DN_CHUNK - 1
        gam_last = pre[b]["gam"][last:last + 1, p * PAIR:(p + 1) * PAIR]
        kd = sl("k", c, b, p) * jnp.exp(gam_last - sl("gam", c, b, p))
        crosses.append(_mm_tn(kd, s))
        gls.append(jnp.exp(gam_last))
    lhs = [jnp.concatenate([pick(cr[:, PAIR:]), sl("qg", c, b, p) - qkuw[:, PAIR:]], axis=0)
           for (c, b, p), cr, qkuw in zip(probs, crosses, qkuws)]

    o_rows = [[] for _ in range(nb)]
    per_chunk = nb * N_PAIRS
    for c in range(GDN_NC):
        sel = slice(c * per_chunk, (c + 1) * per_chunk)
        s_olds = [s_scr[b, p] for _, b, p in probs[sel]]
        rs = [_mm(l, _pair_diag(s_old, lo)) for l, s_old in zip(lhs[sel], s_olds)]
        o_pairs = [[] for _ in range(nb)]
        for (_, b, p), r, s_old, gl, cr, qkuw in zip(probs[sel], rs, s_olds, gls[sel], crosses[sel], qkuws[sel]):
            s_scr[b, p] = gl * s_old - r[:DN_DK] + pick(cr[:, :PAIR])
            o_pairs[b].append(r[DN_DK:] + qkuw[:, :PAIR])
        for b in range(nb):
            o_rows[b].append(jnp.concatenate(o_pairs[b], axis=1))

    o_all = jnp.concatenate([jnp.concatenate(rows, axis=0) for rows in o_rows], axis=0)
    inv_rms = lax.rsqrt(_head_sums(o_all * o_all, hsum) * (1.0 / DN_DV) + EPS)
    for b in range(nb):
        rows = slice(b * GDN_TB, (b + 1) * GDN_TB)
        o_ref[b] = (o_all[rows] * inv_rms[rows] * dnx_ref[...] * _silu_tanh(dz_ref[b])).astype(o_ref.dtype)

    @pl.when(i == pl.num_programs(0) - 1)
    def _():
        for b in range(nb):
            for p in range(N_PAIRS):
                s_p = s_scr[b, p]
                s_out_ref[b, 2 * p] = s_p[:, :DN_DV]
                s_out_ref[b, 2 * p + 1] = s_p[:, DN_DV:]


def _gdn_consts():
    lane = np.arange(DN_WIDTH)
    pl_lane = np.arange(PAIR)
    hsum = (pl_lane[:, None] // DN_DV == pl_lane[None, :] // DN_DV)
    src = np.arange(LANES)
    expb = (src[:, None] == lane[None, :] // DN_DV)
    expg = (src[:, None] == DN_HEADS + lane[None, :] // DN_DV)
    tok = np.arange(GDN_TB)
    ltri = np.logical_and(tok[:, None] >= tok[None, :],
                          tok[:, None] // DN_CHUNK == tok[None, :] // DN_CHUNK)
    as_bf16 = lambda m: jnp.asarray(m.astype(np.float32), dtype=BF16)
    return as_bf16(hsum), as_bf16(expb), as_bf16(expg), as_bf16(ltri)


def _gdn_prompt(xc, dz, ba, alog, dtb, dnx, batch, seq):
    nt = seq // GDN_TB
    hsum, expb, expg, ltri = _gdn_consts()
    row = lambda n: pl.BlockSpec((batch, GDN_TB, n), lambda i: (0, i, 0))
    full = lambda a: pl.BlockSpec(a.shape, lambda i: (0,) * a.ndim)
    consts = (alog, dtb, dnx, hsum, expb, expg, ltri)
    as3d = lambda a: a.reshape(batch, seq, a.shape[-1])
    o, s = pl.pallas_call(
        _gdn_prompt_kernel,
        grid=(nt,),
        in_specs=[row(CONV_CH), row(DN_WIDTH), row(LANES)] + [full(a) for a in consts],
        out_specs=[row(DN_WIDTH),
                   pl.BlockSpec((batch, DN_HEADS, DN_DK, DN_DV), lambda i: (0, 0, 0, 0))],
        out_shape=[jax.ShapeDtypeStruct((batch, seq, DN_WIDTH), BF16),
                   jax.ShapeDtypeStruct((batch, DN_HEADS, DN_DK, DN_DV), F32)],
        scratch_shapes=[pltpu.VMEM((batch, N_PAIRS, DN_DK, PAIR), F32)],
        compiler_params=_params("arbitrary"),
        name="gdn_prompt",
    )(as3d(xc), as3d(dz), as3d(ba), *consts)
    return o.reshape(batch * seq, DN_WIDTH), s


def _gdn_sample_front_kernel(xc_ref, dz_ref, ba_ref, sc_ref, cw_ref, alog_ref, dtb_ref, hsum_ref,
                             q_ref, k_ref, v_ref, dz_t_ref, gates_ref):
    xc = xc_ref[...]
    y = sc_ref[0] * cw_ref[0:1, :]
    y = y + sc_ref[1] * cw_ref[1:2, :]
    y = y + sc_ref[2] * cw_ref[2:3, :]
    y = _silu(y + xc * cw_ref[3:4, :])
    hsum = hsum_ref[...]
    q = y[:, :DN_WIDTH]
    k = y[:, DN_WIDTH:2 * DN_WIDTH]
    q = q * lax.rsqrt(_mm_sel_rhs(q * q, hsum) + EPS) * (DN_DK ** -0.5)
    k = k * lax.rsqrt(_mm_sel_rhs(k * k, hsum) + EPS)
    beta_c, g_c = _gdn_gates(ba_ref[...], alog_ref[...], dtb_ref[...])
    q_ref[...] = q.T
    k_ref[...] = k.T
    v_ref[...] = y[:, 2 * DN_WIDTH:].T
    dz_t_ref[...] = dz_ref[...].T
    gates_ref[0:LANES, :] = beta_c.T
    gates_ref[LANES:, :] = jnp.exp(g_c).T


def _gdn_sample_step_kernel(q_ref, k_ref, v_ref, dz_ref, gates_ref, dn_ref, s_ref, o_ref, s_out_ref):
    h = pl.program_id(0)
    beta = gates_ref[pl.ds(h, 1), :]
    eg = gates_ref[pl.ds(LANES + DN_HEADS + h, 1), :]
    q, k, v = q_ref[...], k_ref[...], v_ref[...]
    w = (k * beta) * eg
    qg = q * eg
    ws = jnp.zeros(v.shape, F32)
    qs = jnp.zeros(v.shape, F32)
    for dk in range(DN_DK):
        s_dk = s_ref[0, dk]
        ws = ws + w[dk:dk + 1, :] * s_dk
        qs = qs + qg[dk:dk + 1, :] * s_dk
    v_new = v * beta - ws
    qk = jnp.sum(q * k, axis=0, keepdims=True)
    o = qs + qk * v_new
    for dk in range(DN_DK):
        s_out_ref[0, dk] = s_ref[0, dk] * eg + k[dk:dk + 1, :] * v_new
    o = o * lax.rsqrt(jnp.mean(o * o, axis=0, keepdims=True) + EPS) * dn_ref[...]
    o_ref[...] = o * _silu(dz_ref[...])


def _gdn_sample_lanes(xc, dz, ba, sconv_t, state_t, conv_w, alog, dtb, dn):
    nseq = xc.shape[0]
    assert nseq == LANES
    lane = np.arange(DN_WIDTH)
    hsum = jnp.asarray((lane[:, None] // DN_DV == lane[None, :] // DN_DV).astype(np.float32), dtype=BF16)
    full = lambda a: pl.BlockSpec(a.shape, lambda i: (0,) * a.ndim)
    cm = jax.ShapeDtypeStruct((DN_WIDTH, nseq), F32)
    front_in = (xc, dz, ba, sconv_t, conv_w, alog, dtb, hsum)
    q_t, k_t, v_t, dz_t, gates_t = pl.pallas_call(
        _gdn_sample_front_kernel,
        grid=(1,),
        in_specs=[full(a) for a in front_in],
        out_specs=[pl.BlockSpec((DN_WIDTH, nseq), lambda i: (0, 0))] * 4
                  + [pl.BlockSpec((2 * LANES, nseq), lambda i: (0, 0))],
        out_shape=[cm, cm, cm, cm, jax.ShapeDtypeStruct((2 * LANES, nseq), F32)],
        compiler_params=_params("arbitrary"),
        name="gdn_sample_front",
    )(*front_in)
    dn_b = jnp.broadcast_to(dn.reshape(DN_DV, 1), (DN_DV, nseq))
    head = pl.BlockSpec((DN_DK, nseq), lambda h: (h, 0))
    st = pl.BlockSpec((1, DN_DK, DN_DV, nseq), lambda h: (h, 0, 0, 0))
    return pl.pallas_call(
        _gdn_sample_step_kernel,
        grid=(DN_HEADS,),
        in_specs=[head, head, head, head, full(gates_t), full(dn_b), st],
        out_specs=[head, st],
        out_shape=[cm, jax.ShapeDtypeStruct(state_t.shape, F32)],
        compiler_params=_params("parallel"),
        name="gdn_sample_step",
    )(q_t, k_t, v_t, dz_t, gates_t, dn_b, state_t)


def _route(xn, wr):
    logits = jnp.dot(xn, wr, preferred_element_type=F32)
    lane = lax.broadcasted_iota(jnp.int32, logits.shape, 1).astype(F32)
    first_at = lambda hit: jnp.min(jnp.where(hit, lane, float(LANES)), axis=-1, keepdims=True)
    glog = jnp.where(lane < N_GROUPS, logits, NEG_INF)
    gmax = jnp.max(glog, axis=-1, keepdims=True)
    gsel = first_at(glog == gmax)
    pgsel = 1.0 / jnp.sum(jnp.exp(glog - gmax), axis=-1, keepdims=True)
    lo = ROUTER_OFF + gsel * EXPERTS_PER_GROUP
    in_group = jnp.logical_and(lane >= lo, lane < lo + EXPERTS_PER_GROUP)
    elog = jnp.where(in_group, logits, NEG_INF)
    m1 = jnp.max(elog, axis=-1, keepdims=True)
    i1 = first_at(elog == m1)
    z = jnp.sum(jnp.exp(elog - m1), axis=-1, keepdims=True)
    elog2 = jnp.where(lane == i1, NEG_INF, elog)
    m2 = jnp.max(elog2, axis=-1, keepdims=True)
    i2 = first_at(elog2 == m2)
    p1 = 1.0 / z
    p2 = jnp.exp(m2 - m1) / z
    tot = p1 + p2
    return lane, i1, i2, p1 / tot * pgsel, p2 / tot * pgsel


def _outproj_router_kernel(x_ref, oa_ref, od_t_ref, wo_ref, g_ref, wr_ref, h_ref, xn_ref, gate_ref):
    h = (x_ref[...] + _mm(oa_ref[...], wo_ref[:ATT_WIDTH, :])
         + _mm(od_t_ref[...].T, wo_ref[ATT_WIDTH:, :]))
    h_ref[...] = h
    xn = _rmsnorm(h, g_ref[...]).astype(BF16)
    xn_ref[...] = xn
    lane, i1, i2, g1, g2 = _route(xn, wr_ref[...])
    gate_ref[...] = jnp.where(lane == i1, g1, 0.0) + jnp.where(lane == i2, g2, 0.0)


def _outproj_router(x, oa, od_t, wo, g, wr):
    t = x.shape[0]
    tm = t
    row = lambda n: pl.BlockSpec((tm, n), lambda i: (i, 0))
    full = lambda a: pl.BlockSpec(a.shape, lambda i: (0,) * a.ndim)
    return pl.pallas_call(
        _outproj_router_kernel,
        grid=(t // tm,),
        in_specs=[row(D_MODEL), row(ATT_WIDTH), full(od_t), full(wo), full(g), full(wr)],
        out_specs=[row(D_MODEL), row(D_MODEL), row(LANES)],
        out_shape=[jax.ShapeDtypeStruct((t, D_MODEL), F32), jax.ShapeDtypeStruct((t, D_MODEL), BF16),
                   jax.ShapeDtypeStruct((t, LANES), F32)],
        compiler_params=_params("parallel"),
        name="outproj_router",
    )(x, oa, od_t, wo, g, wr)


MOE_TM = 512
POS_TM = 1024
ROUTE_CHUNK = 512
INFO_G1, INFO_G2, INFO_E1, INFO_E2 = 0, 1, 2, 3


def _moe_tiles(t):
    return (2 * t) // MOE_TM + N_EXPERTS


HALF = D_MODEL // 2
U32 = jnp.uint32


def _pack_rows(x):
    bits = lambda v: lax.bitcast_convert_type(v.astype(BF16).astype(F32), U32)
    return bits(x[:, HALF:]) | (bits(x[:, :HALF]) >> 16)


def _unpack_rows(w):
    lo = lax.bitcast_convert_type(w << 16, F32)
    hi = lax.bitcast_convert_type(w & jnp.uint32(0xFFFF0000), F32)
    return lo, hi


def _route_kernel(x_ref, oa_ref, od_ref, wo_ref, g_ref, wr_ref, h_ref, xn_ref, info_ref, cnt_ref, run_scr):
    @pl.when(pl.program_id(0) == 0)
    def _():
        run_scr[...] = jnp.zeros(run_scr.shape, F32)

    seen = jnp.zeros(run_scr.shape, F32)
    for c in range(0, x_ref.shape[0], ROUTE_CHUNK):
        rows = pl.ds(c, ROUTE_CHUNK)
        h = (x_ref[rows, :] + _mm(oa_ref[rows, :], wo_ref[:ATT_WIDTH, :])
             + _mm(od_ref[rows, :], wo_ref[ATT_WIDTH:, :]))
        h_ref[rows, :] = h
        xn = _rmsnorm(h, g_ref[...])
        xn_ref[rows, :] = _pack_rows(xn)
        lane, i1, i2, g1, g2 = _route(xn.astype(BF16), wr_ref[...])
        info = jnp.where(lane == INFO_G1, g1, 0.0) + jnp.where(lane == INFO_G2, g2, 0.0)
        info = info + jnp.where(lane == INFO_E1, i1, 0.0) + jnp.where(lane == INFO_E2, i2, 0.0)
        info_ref[rows, :] = info
        picked = jnp.logical_or(lane == i1, lane == i2).astype(F32)
        seen = seen + jnp.sum(picked, axis=0, keepdims=True)
    run_scr[...] += seen
    cnt_ref[...] = run_scr[...]


def _route_sparse(x, oa, od, wo, g, wr):
    t = x.shape[0]
    tm = WIDE_TM
    row = lambda n: pl.BlockSpec((tm, n), lambda i: (i, 0))
    full = lambda a: pl.BlockSpec(a.shape, lambda i: (0,) * a.ndim)
    return pl.pallas_call(
        _route_kernel,
        grid=(t // tm,),
        in_specs=[row(D_MODEL), row(ATT_WIDTH), row(DN_WIDTH), full(wo), full(g), full(wr)],
        out_specs=[row(D_MODEL), row(HALF), row(LANES), pl.BlockSpec((1, LANES), lambda i: (0, 0))],
        out_shape=[jax.ShapeDtypeStruct((t, D_MODEL), F32), jax.ShapeDtypeStruct((t, HALF), U32),
                   jax.ShapeDtypeStruct((t, LANES), F32), jax.ShapeDtypeStruct((1, LANES), F32)],
        scratch_shapes=[pltpu.VMEM((1, LANES), F32)],
        compiler_params=_params("arbitrary"),
        name="route",
    )(x, oa, od, wo, g, wr)


def _positions_kernel(info_ref, cnt_ref, ltri_ref, utri_ref, pos_ref, run_scr, off_scr):
    info = info_ref[...]
    lane = lax.broadcasted_iota(jnp.int32, info.shape, 1).astype(F32)
    hit1 = lane == info[:, INFO_E1:INFO_E1 + 1]
    hit2 = lane == info[:, INFO_E2:INFO_E2 + 1]
    onehot = jnp.logical_or(hit1, hit2).astype(F32)

    @pl.when(pl.program_id(0) == 0)
    def _():
        ln = lax.broadcasted_iota(jnp.int32, cnt_ref.shape, 1)
        is_expert = jnp.logical_and(ln >= ROUTER_OFF, ln < ROUTER_OFF + N_EXPERTS)
        tiles = jnp.where(is_expert, jnp.maximum(jnp.floor((cnt_ref[...] + (MOE_TM - 1)) * (1.0 / MOE_TM)), 1.0), 0.0)
        off_scr[...] = MOE_TM * jnp.dot(tiles.astype(BF16), utri_ref[...], preferred_element_type=F32)
        run_scr[...] = jnp.zeros(run_scr.shape, F32)

    before = (jnp.dot(ltri_ref[...], onehot.astype(BF16), preferred_element_type=F32)
              + run_scr[...] + off_scr[...])
    pos1 = jnp.sum(jnp.where(hit1, before, 0.0), axis=-1, keepdims=True)
    pos2 = jnp.sum(jnp.where(hit2, before, 0.0), axis=-1, keepdims=True)
    both = jnp.where(lane == 0, pos1, 0.0) + jnp.where(lane == 1, pos2, 0.0)
    pos_ref[...] = both.T.astype(jnp.int32)
    run_scr[...] += jnp.sum(onehot, axis=0, keepdims=True)


def _positions(info, cnt):
    t = info.shape[0]
    tm = min(t, POS_TM)
    tok = np.arange(tm)
    ltri = jnp.asarray((tok[:, None] > tok[None, :]).astype(np.float32), dtype=BF16)
    ln = np.arange(LANES)
    utri = jnp.asarray((ln[:, None] < ln[None, :]).astype(np.float32), dtype=BF16)
    full = lambda a: pl.BlockSpec(a.shape, lambda i: (0,) * a.ndim)
    return pl.pallas_call(
        _positions_kernel,
        grid=(t // tm,),
        in_specs=[pl.BlockSpec((tm, LANES), lambda i: (i, 0)), full(cnt), full(ltri), full(utri)],
        out_specs=pl.BlockSpec((LANES, tm), lambda i: (0, i)),
        out_shape=jax.ShapeDtypeStruct((LANES, t), jnp.int32),
        scratch_shapes=[pltpu.VMEM((1, LANES), F32), pltpu.VMEM((1, LANES), F32)],
        compiler_params=_params("arbitrary"),
        name="positions",
    )(info, cnt, ltri, utri)


def _experts_kernel(te_ref, tv_ref, nt_ref, xs_ref, wg_hbm, wu_hbm, wd_hbm, xn_new_ref, gate_new_ref,
                    ys_ref, moe_new_ref, wg_s, wu_s, wd_s, wg_f, wu_f, wd_f, wsem):
    i = pl.program_id(0)
    used = i < nt_ref[0]
    expert = te_ref[i]

    def fetch(e):
        slot = e % 2
        return [pltpu.make_async_copy(src.at[e], dst.at[slot], wsem.at[slot, j])
                for j, (src, dst) in enumerate(((wg_hbm, wg_f), (wu_hbm, wu_f), (wd_hbm, wd_f)))]

    @pl.when(jnp.logical_or(i == 0, expert != te_ref[jnp.maximum(i - 1, 0)]))
    def _():
        @pl.when(i == 0)
        def _():
            for c in fetch(expert):
                c.start()
        for c in fetch(expert):
            c.wait()

        @pl.when(expert + 1 < N_EXPERTS)
        def _():
            for c in fetch(expert + 1):
                c.start()
        slot = expert % 2
        wg_s[...] = wg_f[slot].astype(BF16)
        wu_s[...] = wu_f[slot].astype(BF16)
        wd_s[...] = wd_f[slot].astype(BF16)
        xn = xn_new_ref[...]
        lane = lax.broadcasted_iota(jnp.int32, gate_new_ref.shape, 1)
        gate = jnp.sum(jnp.where(lane == expert + ROUTER_OFF, gate_new_ref[...], 0.0), axis=-1, keepdims=True)
        hg = jnp.dot(xn, wg_s[...], preferred_element_type=F32)
        hu = jnp.dot(xn, wu_s[...], preferred_element_type=F32)
        hm = _silu(hg) * hu * gate
        y = jnp.dot(hm.astype(BF16), wd_s[...], preferred_element_type=F32)

        @pl.when(i == 0)
        def _():
            moe_new_ref[...] = y

        @pl.when(i > 0)
        def _():
            moe_new_ref[...] += y

    @pl.when(used)
    def _():
        row = lax.broadcasted_iota(jnp.int32, xs_ref.shape, 0)
        x_lo, x_hi = _unpack_rows(jnp.where(row < tv_ref[i], xs_ref[...], jnp.uint32(0)))
        x_lo = x_lo.astype(BF16)
        x_hi = x_hi.astype(BF16)
        up = lambda w_s: (jnp.dot(x_lo, w_s[:HALF, :], preferred_element_type=F32)
                          + jnp.dot(x_hi, w_s[HALF:, :], preferred_element_type=F32))
        hm = (_silu_tanh(up(wg_s)) * up(wu_s)).astype(BF16)
        ys_ref[...] = _pack_rows(jnp.dot(hm, wd_s[...], preferred_element_type=F32))

    @pl.when(jnp.logical_not(used))
    def _():
        ys_ref[...] = jnp.zeros(ys_ref.shape, U32)


def _experts(xs, tile_expert, tile_valid, n_tiles, wg, wu, wd, xn_new, gate_new):
    max_tiles = xs.shape[0] // MOE_TM
    rows = pl.BlockSpec((MOE_TM, HALF), lambda i, te, tv, nt: (i, 0))
    hbm = pl.BlockSpec(memory_space=pl.ANY)
    full = lambda a: pl.BlockSpec(a.shape, lambda i, te, tv, nt: (0,) * a.ndim)
    return pl.pallas_call(
        _experts_kernel,
        grid_spec=pltpu.PrefetchScalarGridSpec(
            num_scalar_prefetch=3, grid=(max_tiles,),
            in_specs=[rows, hbm, hbm, hbm, full(xn_new), full(gate_new)],
            out_specs=[rows, pl.BlockSpec(xn_new.shape, lambda i, te, tv, nt: (0, 0))],
            scratch_shapes=[pltpu.VMEM((D_MODEL, D_EXPERT), BF16), pltpu.VMEM((D_MODEL, D_EXPERT), BF16),
                            pltpu.VMEM((D_EXPERT, D_MODEL), BF16),
                            pltpu.VMEM((2, D_MODEL, D_EXPERT), F32), pltpu.VMEM((2, D_MODEL, D_EXPERT), F32),
                            pltpu.VMEM((2, D_EXPERT, D_MODEL), F32), pltpu.SemaphoreType.DMA((2, 3))]),
        out_shape=[jax.ShapeDtypeStruct(xs.shape, U32), jax.ShapeDtypeStruct(xn_new.shape, F32)],
        compiler_params=_params("arbitrary"),
        name="experts",
    )(tile_expert, tile_valid, n_tiles, xs, wg, wu, wd, xn_new, gate_new)


SC_IDX = 128
SC_ROWS = 64
SC_WORKERS = 32
SC_GATHER_ROWS = 32
SC_GATHER_BUFS = 4


def _sc_mesh():
    return plsc.VectorSubcoreMesh(core_axis_name="c", subcore_axis_name="s")


def _sc_windows(t, fn):
    per_worker = t // SC_WORKERS
    worker = lax.axis_index(("c", "s"))

    @pl.loop(0, per_worker // SC_IDX)
    def _(w):
        fn(worker * per_worker + w * SC_IDX)


def _sc_scatter_rows(xn, pos1, pos2, n_rows):
    t, d = xn.shape
    assert t % (SC_WORKERS * SC_IDX) == 0
    idx_t = pltpu.VMEM((1, SC_IDX), jnp.int32)

    @pl.kernel(out_type=jax.ShapeDtypeStruct((n_rows, d), xn.dtype), mesh=_sc_mesh(),
               scratch_types=[idx_t, idx_t, pltpu.VMEM((SC_ROWS, d), xn.dtype)])
    def scatter(x_hbm, p1_hbm, p2_hbm, o_hbm, i1_v, i2_v, buf):
        def window(base):
            pltpu.sync_copy(p1_hbm.at[:, pl.ds(base, SC_IDX)], i1_v)
            pltpu.sync_copy(p2_hbm.at[:, pl.ds(base, SC_IDX)], i2_v)
            for k in range(SC_IDX // SC_ROWS):
                pltpu.sync_copy(x_hbm.at[pl.ds(base + k * SC_ROWS, SC_ROWS)], buf)
                pltpu.sync_copy(buf, o_hbm.at[i1_v.at[0, pl.ds(k * SC_ROWS, SC_ROWS)]])
                pltpu.sync_copy(buf, o_hbm.at[i2_v.at[0, pl.ds(k * SC_ROWS, SC_ROWS)]])
        _sc_windows(t, window)

    return scatter(xn, pos1.reshape(1, t), pos2.reshape(1, t))


def _sc_gather_rows(ys, pos1, pos2):
    t = pos1.shape[0]
    d = ys.shape[1]
    assert t % (SC_WORKERS * SC_IDX) == 0
    per_worker = t // SC_WORKERS
    idx_t = pltpu.VMEM((1, per_worker), jnp.int32)
    out = jax.ShapeDtypeStruct((t, d), ys.dtype)

    nbuf, rows = SC_GATHER_BUFS, SC_GATHER_ROWS
    buf_t = pltpu.VMEM((rows, d), ys.dtype)

    @pl.kernel(out_type=(out, out), mesh=_sc_mesh(),
               scratch_types=[idx_t, idx_t] + [buf_t] * nbuf
                             + [pltpu.SemaphoreType.DMA((nbuf,)), pltpu.SemaphoreType.DMA((nbuf,))])
    def gather(y_hbm, p1_hbm, p2_hbm, o1_hbm, o2_hbm, i1_v, i2_v, *rest):
        bufs, (gsem, wsem) = rest[:nbuf], rest[nbuf:]
        base = lax.axis_index(("c", "s")) * per_worker
        pltpu.sync_copy(p1_hbm.at[:, pl.ds(base, per_worker)], i1_v)
        pltpu.sync_copy(p2_hbm.at[:, pl.ds(base, per_worker)], i2_v)
        items = [(idx_v, o_hbm, k) for k in range(per_worker // rows)
                 for idx_v, o_hbm in ((i1_v, o1_hbm), (i2_v, o2_hbm))]
        n_items = len(items)

        def read(n):
            idx_v, _, k = items[n]
            return pltpu.make_async_copy(y_hbm.at[idx_v.at[0, pl.ds(k * rows, rows)]],
                                         bufs[n % nbuf], gsem.at[n % nbuf])

        def write(n):
            _, o_hbm, k = items[n]
            return pltpu.make_async_copy(bufs[n % nbuf], o_hbm.at[pl.ds(base + k * rows, rows)],
                                         wsem.at[n % nbuf])

        for n in range(min(nbuf - 1, n_items)):
            read(n).start()
        waited = 0
        for n in range(n_items):
            read(n).wait()
            write(n).start()
            ahead = n + nbuf - 1
            if ahead < n_items:
                if n >= 1:
                    write(n - 1).wait()
                    waited = n
                read(ahead).start()
        for n in range(waited, n_items):
            write(n).wait()

    return gather(ys, pos1.reshape(1, t), pos2.reshape(1, t))


TAIL_CHUNK = 256


def _ple_sparse_kernel(h_ref, info_ref, y1_ref, y2_ref, p_ref, wpp_ref, wpg_ref, gp_ref, gf_ref, y_ref):
    for c in range(0, h_ref.shape[0], TAIL_CHUNK):
        rows = pl.ds(c, TAIL_CHUNK)
        info = info_ref[rows, :]
        g1 = info[:, INFO_G1:INFO_G1 + 1]
        g2 = info[:, INFO_G2:INFO_G2 + 1]
        y1_lo, y1_hi = _unpack_rows(y1_ref[rows, :])
        y2_lo, y2_hi = _unpack_rows(y2_ref[rows, :])
        moe = jnp.concatenate([g1 * y1_lo + g2 * y2_lo, g1 * y1_hi + g2 * y2_hi], axis=1)
        h = h_ref[rows, :] + moe
        hn = _rmsnorm(h, gp_ref[...])
        h = h + _mm(p_ref[rows, :], wpp_ref[...]) * _sigmoid_tanh(_mm(hn, wpg_ref[...]))
        y_ref[rows, :] = _rmsnorm(h, gf_ref[...])


def _ple_sparse(h, info, y1, y2, p, wpp, wpg, gp, gf):
    t = h.shape[0]
    tm = WIDE_TM
    row = lambda n: pl.BlockSpec((tm, n), lambda i: (i, 0))
    full = lambda a: pl.BlockSpec(a.shape, lambda i: (0,) * a.ndim)
    return pl.pallas_call(
        _ple_sparse_kernel,
        grid=(t // tm,),
        in_specs=[row(D_MODEL), row(LANES), row(HALF), row(HALF), row(PLE_DIM),
                  full(wpp), full(wpg), full(gp), full(gf)],
        out_specs=row(D_MODEL),
        out_shape=jax.ShapeDtypeStruct((t, D_MODEL), F32),
        compiler_params=_params("parallel"),
        name="ple_sparse",
    )(h, info, y1, y2, p, wpp, wpg, gp, gf)


def _tile_tables(cnt, max_tiles):
    tiles_e = jnp.maximum((cnt + (MOE_TM - 1)) // MOE_TM, 1)
    ends = jnp.cumsum(tiles_e)
    n_tiles = ends[-1]
    tile = jnp.arange(max_tiles, dtype=jnp.int32)
    idx = jnp.minimum(tile, n_tiles - 1)
    tile_expert = jnp.sum((idx[:, None] >= ends[None, :]).astype(jnp.int32), axis=1)
    mine = tile_expert[:, None] == jnp.arange(N_EXPERTS, dtype=jnp.int32)[None, :]
    of_mine = lambda v: jnp.sum(jnp.where(mine, v[None, :], 0), axis=1)
    valid = jnp.clip(of_mine(cnt) - (idx - of_mine(ends - tiles_e)) * MOE_TM, 0, MOE_TM)
    tile_valid = jnp.where(tile < n_tiles, valid, 0).astype(jnp.int32)
    return tile_expert, tile_valid, n_tiles.reshape(1)


def _ple_final_kernel(h_ref, m_ref, p_ref, wpp_ref, wpg_ref, gp_ref, gf_ref, y_ref):
    h = h_ref[...] + m_ref[...]
    hn = _rmsnorm(h, gp_ref[...])
    h = h + _mm(p_ref[...], wpp_ref[...]) * _sigmoid(_mm(hn, wpg_ref[...]))
    y_ref[...] = _rmsnorm(h, gf_ref[...])


def _ple_final(h, m, p, wpp, wpg, gp, gf):
    t = h.shape[0]
    tm = min(t, 256)
    row = lambda n: pl.BlockSpec((tm, n), lambda i: (i, 0))
    full = lambda a: pl.BlockSpec(a.shape, lambda i: (0,) * a.ndim)
    return pl.pallas_call(
        _ple_final_kernel,
        grid=(t // tm,),
        in_specs=[row(D_MODEL), row(D_MODEL), row(PLE_DIM), full(wpp), full(wpg), full(gp), full(gf)],
        out_specs=row(D_MODEL),
        out_shape=jax.ShapeDtypeStruct((t, D_MODEL), F32),
        compiler_params=_params("parallel"),
        name="ple_final",
    )(h, m, p, wpp, wpg, gp, gf)


def kernel(x_prompt, x_sample, p_prompt, p_sample, cache_k, cache_v, state_conv, state_S, rel_bias, norm_mix, w_in, att_sink, conv_w, dn_A_log, dn_dt_bias, dn_norm, w_out, norm_ffn, w_router_group, w_router_expert, w_gate, w_up, w_down, w_ple_proj, w_ple_gate, norm_ple, norm_final):
    batch, seq, _ = x_prompt.shape
    nseq = x_sample.shape[0]
    assert x_sample.shape[1] == 1 and norm_mix.shape[0] == 1 and cache_k.shape[2] == WINDOW
    assert seq % GDN_TB == 0 and seq % ATT_BLOCK == 0

    wt = jnp.swapaxes(w_in[0], 0, 1)
    o_db = ATT_COLS + CONV_CH
    w_in_re = (wt[:o_db].astype(BF16), wt[o_db + 2 * DN_HEADS:].astype(BF16),
               jnp.pad(wt[o_db:o_db + 2 * DN_HEADS], ((0, LANES - 2 * DN_HEADS), (0, 0))).astype(BF16))
    row = lambda a: a.reshape(1, -1).astype(F32)
    pad_lanes = lambda a, off: jnp.zeros((1, LANES), F32).at[0, off:off + a.shape[0]].set(a)
    alog = pad_lanes(dn_A_log[0], DN_HEADS)
    dtb = pad_lanes(dn_dt_bias[0], DN_HEADS)
    dnx = jnp.tile(dn_norm[0], DN_HEADS).reshape(1, DN_WIDTH)
    w_router = jnp.concatenate(
        [w_router_group[0], w_router_expert[0],
         jnp.zeros((D_MODEL, LANES - N_GROUPS - N_EXPERTS), F32)], axis=1).astype(BF16)
    wo = w_out[0].astype(BF16)
    wg, wu, wd = w_gate[0], w_up[0], w_down[0]
    wpp, wpg = w_ple_proj[0].astype(BF16), w_ple_gate[0].astype(BF16)
    sink = att_sink[0]

    qi = np.arange(ATT_BLOCK)[:, None]
    kj = np.arange(2 * ATT_BLOCK)[None, :]
    bucket_p = jnp.asarray(_t5_bucket_np(qi + ATT_BLOCK - kj))
    bucket_s = jnp.asarray(_t5_bucket_np(WINDOW - np.arange(WINDOW)[None, :]))

    xp = x_prompt.reshape(batch * seq, D_MODEL)
    att_p, qkv_p, dz_p, ba_p, xc_tails = _inproj_conv(xp, row(norm_mix[0]), w_in_re, conv_w[0], seq)
    o_att_p = _attn_prompt(att_p, bucket_p, rel_bias, sink, batch, seq)
    o_dn_p, s_p = _gdn_prompt(qkv_p, dz_p, ba_p, alog, dtb, dnx, batch, seq)
    h1, xn2, info, cnt = _route_sparse(xp, o_att_p, o_dn_p, wo, row(norm_ffn[0]), w_router)
    pos = _positions(info, cnt)
    pos1, pos2 = pos[0], pos[1]
    max_tiles = _moe_tiles(batch * seq)
    cnt_e = cnt[0, ROUTER_OFF:ROUTER_OFF + N_EXPERTS].astype(jnp.int32)
    tile_expert, tile_valid, n_tiles = _tile_tables(cnt_e, max_tiles)
    xs_sorted = _sc_scatter_rows(xn2, pos1, pos2, max_tiles * MOE_TM)

    xs = x_sample.reshape(nseq, D_MODEL)
    att_s, xc_s, dz_s, ba_s = _inproj(xs, row(norm_mix[0]), w_in_re)
    ck_t = jnp.transpose(cache_k[0], (0, 2, 3, 1))
    cv_t = jnp.transpose(cache_v[0], (0, 2, 3, 1))
    o_att_s, ks_t, vs_t = _attn_sample(att_s, ck_t, cv_t, bucket_s, rel_bias, sink)
    sconv_t = jnp.swapaxes(state_conv[0], 0, 1)
    o_dn_s_t, s_s_t = _gdn_sample_lanes(xc_s, dz_s, ba_s, sconv_t, jnp.transpose(state_S[0], (1, 2, 3, 0)),
                                        conv_w[0], alog, dtb, dn_norm[0])
    s_s = jnp.transpose(s_s_t, (3, 0, 1, 2))

    h1_s, xn2_s, gates_s = _outproj_router(xs, o_att_s, o_dn_s_t, wo, row(norm_ffn[0]), w_router)

    ys, moe_s = _experts(xs_sorted, tile_expert, tile_valid, n_tiles, wg, wu, wd, xn2_s, gates_s)
    y1, y2 = _sc_gather_rows(ys, pos1, pos2)
    y_s = _ple_final(h1_s, moe_s, p_sample[0].reshape(nseq, PLE_DIM), wpp, wpg, row(norm_ple[0]),
                     row(norm_final))
    y_p = _ple_sparse(h1, info, y1, y2, p_prompt[0].reshape(batch * seq, PLE_DIM),
                      wpp, wpg, row(norm_ple[0]), row(norm_final))

    att_p3 = att_p.reshape(batch, seq, ATT_COLS)
    kv_shape = (1, batch, WINDOW, ATT_KV_HEADS, HEAD_DIM)
    k_p = att_p3[:, seq - WINDOW:, ATT_WIDTH:ATT_WIDTH + KV_WIDTH].reshape(kv_shape)
    v_p = att_p3[:, seq - WINDOW:, ATT_WIDTH + KV_WIDTH:].reshape(kv_shape)
    conv_p = xc_tails.reshape(batch, -1, TAIL, CONV_CH)[:, -1, TAIL - (CONV_WIDTH - 1):][None]
    k_s = jnp.transpose(ks_t, (0, 3, 1, 2))[None]
    v_s = jnp.transpose(vs_t, (0, 3, 1, 2))[None]
    conv_s = jnp.concatenate([state_conv[0][:, 1:], xc_s[:, None, :]], axis=1)[None]
    return (y_p.reshape(batch, seq, D_MODEL), y_s.reshape(nseq, 1, D_MODEL),
            k_p, v_p, conv_p, s_p[None], k_s, v_s, conv_s, s_s[None])
```

```python
import functools
import math

import numpy as np
import jax
import jax.numpy as jnp
from jax import lax
from jax.experimental import pallas as pl
from jax.experimental.pallas import tpu as pltpu
from jax.experimental.pallas import tpu_sc as plsc

F32 = jnp.float32
BF16 = jnp.bfloat16

D_MODEL = 1024
ATT_HEADS = 8
ATT_KV_HEADS = 2
HEAD_DIM = 64
GQA = ATT_HEADS // ATT_KV_HEADS
WINDOW = 128
ATT_BLOCK = 128
N_BUCKETS = 32
DN_HEADS = 8
DN_DK = 64
DN_DV = 64
CONV_WIDTH = 4
DN_CHUNK = 64
ATT_WIDTH = ATT_HEADS * HEAD_DIM
KV_WIDTH = ATT_KV_HEADS * HEAD_DIM
DN_WIDTH = DN_HEADS * DN_DV
CONV_CH = 3 * DN_WIDTH
N_GROUPS = 4
EXPERTS_PER_GROUP = 8
N_EXPERTS = N_GROUPS * EXPERTS_PER_GROUP
D_EXPERT = 256
PLE_DIM = 256
EPS = 1e-6
NEG_INF = float("-inf")

ATT_COLS = ATT_WIDTH + 2 * KV_WIDTH
LANES = 128
ROUTER_OFF = N_GROUPS
VMEM_LIMIT = 48 * 1024 * 1024
ROW_TM = 512
WIDE_TM = 1024


def _params(*sem):
    return pltpu.CompilerParams(dimension_semantics=sem, vmem_limit_bytes=VMEM_LIMIT)


def _mm(a, b):
    return jnp.dot(a.astype(BF16), b.astype(BF16), preferred_element_type=F32)


def _mm_nt(a, b):
    return lax.dot_general(a.astype(BF16), b.astype(BF16), (((1,), (1,)), ((), ())),
                           preferred_element_type=F32)


def _mm_tn(a, b):
    return lax.dot_general(a.astype(BF16), b.astype(BF16), (((0,), (0,)), ((), ())),
                           preferred_element_type=F32)


def _split3(x):
    h1 = x.astype(BF16)
    r1 = x - h1.astype(F32)
    h2 = r1.astype(BF16)
    h3 = (r1 - h2.astype(F32)).astype(BF16)
    return h1, h2, h3


def _mm_sel_rhs(x, sel):
    h1, h2, h3 = _split3(x)
    d = lambda h: jnp.dot(h, sel, preferred_element_type=F32)
    return d(h1) + d(h2) + d(h3)


def _mm_sel_lhs(sel, x):
    h1, h2, h3 = _split3(x)
    d = lambda h: jnp.dot(sel, h, preferred_element_type=F32)
    return d(h1) + d(h2) + d(h3)


def _sigmoid(x):
    return 1.0 / (1.0 + jnp.exp(-x))


def _silu(x):
    return x * _sigmoid(x)


def _sigmoid_tanh(x):
    return 0.5 * jnp.tanh(0.5 * x) + 0.5


def _silu_tanh(x):
    return x * _sigmoid_tanh(x)


def _softplus(x):
    return jnp.maximum(x, 0.0) + jnp.log1p(jnp.exp(-jnp.abs(x)))


def _rmsnorm(x, g):
    return x * lax.rsqrt(jnp.mean(x * x, axis=-1, keepdims=True) + EPS) * g


def _t5_bucket_np(dist):
    max_exact = N_BUCKETS // 2
    d = np.maximum(dist, 0)
    ratio = (np.log(np.maximum(d, 1).astype(np.float32) / np.float32(max_exact))
             / np.float32(math.log(WINDOW / max_exact))).astype(np.float32)
    large = np.minimum(max_exact + (ratio * np.float32(N_BUCKETS - max_exact)).astype(np.int32),
                       N_BUCKETS - 1)
    return np.where(d < max_exact, d, large).astype(np.int32)


def _bias_lookup(bucket, rb_ref, h):
    acc = jnp.zeros(bucket.shape, F32)
    for t in range(N_BUCKETS):
        acc = jnp.where(bucket == t, rb_ref[t, h], acc)
    return acc


def _inproj_kernel(x_ref, g_ref, wa_ref, wz_ref, wb_ref, att_ref, xc_ref, dz_ref, ba_ref):
    xn = _rmsnorm(x_ref[...], g_ref[...]).astype(BF16)
    att_ref[...] = _mm_nt(xn, wa_ref[:ATT_COLS, :])
    xc_ref[...] = _mm_nt(xn, wa_ref[ATT_COLS:, :])
    dz_ref[...] = _mm_nt(xn, wz_ref[...])
    ba_ref[...] = _mm_nt(xn, wb_ref[...])


def _inproj(x, g, w):
    t = x.shape[0]
    tm = min(t, ROW_TM)
    row = lambda n: pl.BlockSpec((tm, n), lambda i: (i, 0))
    full = lambda a: pl.BlockSpec(a.shape, lambda i: (0,) * a.ndim)
    return pl.pallas_call(
        _inproj_kernel,
        grid=(t // tm,),
        in_specs=[row(D_MODEL), full(g)] + [full(a) for a in w],
        out_specs=[row(ATT_COLS), row(CONV_CH), row(DN_WIDTH), row(LANES)],
        out_shape=[jax.ShapeDtypeStruct((t, n), F32) for n in (ATT_COLS, CONV_CH, DN_WIDTH, LANES)],
        compiler_params=_params("parallel"),
        name="inproj",
    )(x, g, *w)


TAIL = 8
PAIR = 2 * DN_DK
N_PAIRS = DN_WIDTH // PAIR


def _head_sums(z, pair_ones):
    hi = z.astype(BF16)
    lw = (z - hi.astype(F32)).astype(BF16)
    d = lambda a, p: jnp.dot(a[:, p * PAIR:(p + 1) * PAIR], pair_ones, preferred_element_type=F32)
    return jnp.concatenate([d(hi, p) + d(lw, p) for p in range(N_PAIRS)], axis=1)


W_T_CHUNK = 256


def _inproj_conv_kernel(x_ref, g_ref, wat_ref, wzt_ref, wbt_ref, cw_ref, ones_ref,
                        att_ref, qkv_ref, dz_ref, ba_ref, tail_ref, xp_scr, wa_ref, wz_ref, wb_ref,
                        *, tiles_per_seq):
    tm = x_ref.shape[0]

    @pl.when(pl.program_id(0) == 0)
    def _():
        for src, dst in ((wat_ref, wa_ref), (wzt_ref, wz_ref), (wbt_ref, wb_ref)):
            for c in range(0, src.shape[0], W_T_CHUNK):
                n = min(W_T_CHUNK, src.shape[0] - c)
                dst[:, c:c + n] = src[c:c + n, :].T

    @pl.when(pl.program_id(0) % tiles_per_seq == 0)
    def _():
        xp_scr[...] = jnp.zeros((TAIL, CONV_CH), F32)

    xn = _rmsnorm(x_ref[...], g_ref[...]).astype(BF16)
    xc = jnp.dot(xn, wa_ref[:, ATT_COLS:], preferred_element_type=F32)
    att_ref[...] = jnp.dot(xn, wa_ref[:, :ATT_COLS], preferred_element_type=F32)
    dz_ref[...] = jnp.dot(xn, wz_ref[...], preferred_element_type=F32)
    ba_ref[...] = jnp.dot(xn, wb_ref[...], preferred_element_type=F32)

    head = jnp.concatenate([xp_scr[...], xc[:TAIL, :]], axis=0)

    def shifted(j):
        return jnp.concatenate([head[TAIL - j:2 * TAIL - j, :], pltpu.roll(xc, j, axis=0)[TAIL:, :]], axis=0)

    y = shifted(3) * cw_ref[0:1, :]
    y = y + shifted(2) * cw_ref[1:2, :]
    y = y + shifted(1) * cw_ref[2:3, :]
    y = y + xc * cw_ref[3:4, :]
    tail = xc[tm - TAIL:, :]
    xp_scr[...] = tail
    tail_ref[0] = tail
    y = _silu_tanh(y)
    q = y[:, :DN_WIDTH]
    k = y[:, DN_WIDTH:2 * DN_WIDTH]
    inv_norm = lax.rsqrt(_head_sums(jnp.concatenate([q * q, k * k], axis=0), ones_ref[...]) + EPS)
    qkv_ref[:, :DN_WIDTH] = q * inv_norm[:tm] * (DN_DK ** -0.5)
    qkv_ref[:, DN_WIDTH:2 * DN_WIDTH] = k * inv_norm[tm:]
    qkv_ref[:, 2 * DN_WIDTH:] = y[:, 2 * DN_WIDTH:]


def _pair_ones():
    lane = np.arange(PAIR)
    return jnp.asarray((lane[:, None] // DN_DV == lane[None, :] // DN_DV).astype(np.float32), dtype=BF16)


def _inproj_conv(x, g, w, conv_w, seq):
    t = x.shape[0]
    tm = ROW_TM
    assert seq % tm == 0
    ones = _pair_ones()
    row = lambda n: pl.BlockSpec((tm, n), lambda i: (i, 0))
    full = lambda a: pl.BlockSpec(a.shape, lambda i: (0,) * a.ndim)
    return pl.pallas_call(
        functools.partial(_inproj_conv_kernel, tiles_per_seq=seq // tm),
        grid=(t // tm,),
        in_specs=[row(D_MODEL), full(g)] + [full(a) for a in w] + [full(conv_w), full(ones)],
        out_specs=[row(ATT_COLS), row(CONV_CH), row(DN_WIDTH), row(LANES),
                   pl.BlockSpec((1, TAIL, CONV_CH), lambda i: (i, 0, 0))],
        out_shape=[jax.ShapeDtypeStruct((t, n), F32) for n in (ATT_COLS, CONV_CH, DN_WIDTH, LANES)]
                  + [jax.ShapeDtypeStruct((t // tm, TAIL, CONV_CH), F32)],
        scratch_shapes=[pltpu.VMEM((TAIL, CONV_CH), F32)]
                       + [pltpu.VMEM((D_MODEL, a.shape[0]), BF16) for a in w],
        compiler_params=_params("arbitrary"),
        name="inproj_conv",
    )(x, g, *w, conv_w, ones)


GROUP_ROWS = GQA * ATT_BLOCK


def _attn_prompt_kernel(cur_ref, prev_ref, bucket_ref, rb_ref, sink_ref, o_ref, bias_scr, sink_scr):
    i = pl.program_id(0)
    nseq = cur_ref.shape[0]

    @pl.when(i == 0)
    def _():
        qi = lax.broadcasted_iota(jnp.int32, (ATT_BLOCK, 2 * ATT_BLOCK), 0)
        kj = lax.broadcasted_iota(jnp.int32, (ATT_BLOCK, 2 * ATT_BLOCK), 1)
        dist = qi + ATT_BLOCK - kj
        band = jnp.logical_and(dist >= 0, dist < WINDOW)
        bucket = bucket_ref[...]
        hrow = lax.broadcasted_iota(jnp.int32, (GROUP_ROWS, 1), 0) // ATT_BLOCK
        for g in range(ATT_KV_HEADS):
            sink_col = jnp.zeros((GROUP_ROWS, 1), F32)
            for hh in range(GQA):
                h = g * GQA + hh
                bias = jnp.where(band, _bias_lookup(bucket, rb_ref, h), NEG_INF)
                bias_scr[0, g, hh * ATT_BLOCK:(hh + 1) * ATT_BLOCK, :] = bias
                bias_scr[1, g, hh * ATT_BLOCK:(hh + 1) * ATT_BLOCK, :] = jnp.where(kj >= ATT_BLOCK, bias, NEG_INF)
                sink_col = jnp.where(hrow == hh, sink_ref[h], sink_col)
            sink_scr[g] = sink_col

    first = (i == 0).astype(jnp.int32)
    probs = [(b, g) for b in range(nseq) for g in range(ATT_KV_HEADS)]
    scores = []
    for b, g in probs:
        cur = cur_ref[b]
        prev = prev_ref[b]
        q = jnp.concatenate([cur[:, (g * GQA + hh) * HEAD_DIM:(g * GQA + hh + 1) * HEAD_DIM]
                             for hh in range(GQA)], axis=0) * (HEAD_DIM ** -0.5)
        kcol = slice(ATT_WIDTH + g * HEAD_DIM, ATT_WIDTH + (g + 1) * HEAD_DIM)
        k2 = jnp.concatenate([prev[:, kcol], cur[:, kcol]], axis=0)
        scores.append(_mm_nt(q, k2) + bias_scr[first, g])
    probs_p, dens = [], []
    for (b, g), s in zip(probs, scores):
        sink = sink_scr[g]
        m = jnp.maximum(jnp.max(s, axis=-1, keepdims=True), sink)
        p = jnp.exp(s - m)
        dens.append(jnp.sum(p, axis=-1, keepdims=True) + jnp.exp(sink - m))
        probs_p.append(p.astype(BF16))
    outs = {}
    for (b, g), p, den in zip(probs, probs_p, dens):
        vcol = slice(ATT_WIDTH + KV_WIDTH + g * HEAD_DIM, ATT_WIDTH + KV_WIDTH + (g + 1) * HEAD_DIM)
        v2 = jnp.concatenate([prev_ref[b][:, vcol], cur_ref[b][:, vcol]], axis=0)
        outs[b, g] = _mm(p, v2) / den
    for b in range(nseq):
        o_ref[b] = jnp.concatenate([outs[b, g][hh * ATT_BLOCK:(hh + 1) * ATT_BLOCK, :]
                                    for g in range(ATT_KV_HEADS) for hh in range(GQA)],
                                   axis=1).astype(o_ref.dtype)


def _attn_prompt(att, bucket, rel_bias, sink, batch, seq):
    nb = seq // ATT_BLOCK
    smem = pl.BlockSpec(memory_space=pltpu.SMEM)
    att3 = att.reshape(batch, seq, ATT_COLS)
    out = pl.pallas_call(
        _attn_prompt_kernel,
        grid=(nb,),
        in_specs=[
            pl.BlockSpec((batch, ATT_BLOCK, ATT_COLS), lambda i: (0, i, 0)),
            pl.BlockSpec((batch, ATT_BLOCK, ATT_COLS), lambda i: (0, jnp.maximum(i - 1, 0), 0)),
            pl.BlockSpec(bucket.shape, lambda i: (0, 0)),
            smem, smem,
        ],
        out_specs=pl.BlockSpec((batch, ATT_BLOCK, ATT_WIDTH), lambda i: (0, i, 0)),
        out_shape=jax.ShapeDtypeStruct((batch, seq, ATT_WIDTH), BF16),
        scratch_shapes=[pltpu.VMEM((2, ATT_KV_HEADS, GROUP_ROWS, 2 * ATT_BLOCK), F32),
                        pltpu.VMEM((ATT_KV_HEADS, GROUP_ROWS, 1), F32)],
        compiler_params=_params("arbitrary"),
        name="attn_prompt",
    )(att3, att3, bucket, rel_bias, sink)
    return out.reshape(batch * seq, ATT_WIDTH)


ATT_S_BB = 8


def _attn_sample_kernel(att_ref, ck_ref, cv_ref, bucket_ref, rb_ref, sink_ref, o_ref, ks_ref, vs_ref,
                        bias_scr, col_scr):
    hrow = lax.broadcasted_iota(jnp.int32, (ATT_HEADS, LANES), 0)
    lane = lax.broadcasted_iota(jnp.int32, (ATT_HEADS, LANES), 1)

    last = (lax.broadcasted_iota(jnp.int32, (3, WINDOW), 1) == WINDOW - 1).astype(BF16)
    is_last = lax.broadcasted_iota(jnp.int32, (KV_WIDTH, WINDOW), 1) == WINDOW - 1

    def shifted(cache_t, new_row):
        pieces = jnp.concatenate([p.astype(F32) for p in _split3(new_row)], axis=0).astype(BF16)
        col = lax.dot_general(pieces, last, (((0,), (0,)), ((), ())), preferred_element_type=F32)
        out = jnp.where(is_last, col, pltpu.roll(cache_t, WINDOW - 1, axis=1))
        return out.reshape(ATT_KV_HEADS, HEAD_DIM, WINDOW)

    for b in range(ATT_S_BB):
        row = att_ref[b:b + 1, :]
        ks_ref[b] = shifted(ck_ref[b].reshape(KV_WIDTH, WINDOW), row[:, ATT_WIDTH:ATT_WIDTH + KV_WIDTH])
        vs_ref[b] = shifted(cv_ref[b].reshape(KV_WIDTH, WINDOW), row[:, ATT_WIDTH + KV_WIDTH:])

    @pl.when(pl.program_id(0) == 0)
    def _():
        bucket = jnp.broadcast_to(bucket_ref[...], (ATT_HEADS, LANES))
        bias = jnp.zeros((ATT_HEADS, LANES), F32)
        cols = jnp.zeros((ATT_HEADS, LANES), F32)
        for h in range(ATT_HEADS):
            bias = jnp.where(hrow == h, _bias_lookup(bucket, rb_ref, h), bias)
            cols = jnp.where(jnp.logical_and(hrow == h, lane == 0), sink_ref[h], cols)
            cols = jnp.where(jnp.logical_and(hrow == h, lane == 1), rb_ref[0, h], cols)
        bias_scr[...] = jnp.where(lane >= 1, bias, NEG_INF)
        col_scr[...] = cols

    bias_c = bias_scr[...]
    sink = col_scr[:, 0:1]
    bias_n = col_scr[:, 1:2]
    same_group = (hrow // GQA) == (lane // HEAD_DIM)
    low_group = lax.broadcasted_iota(jnp.int32, (ATT_HEADS, HEAD_DIM), 0) < GQA
    rnd = lambda a: a.astype(BF16).astype(F32)
    seqs = range(ATT_S_BB)
    rows = [att_ref[b:b + 1, :] for b in seqs]
    q_bds = []
    for row in rows:
        q = row[:, :ATT_WIDTH] * (HEAD_DIM ** -0.5)
        qh = jnp.concatenate([q[:, h * HEAD_DIM:(h + 1) * HEAD_DIM] for h in range(ATT_HEADS)], axis=0)
        q_bds.append(jnp.where(same_group, jnp.concatenate([qh, qh], axis=1), 0.0))
    kv_t = lambda ref, b: ref[b].reshape(KV_WIDTH, WINDOW)
    s_cs = [_mm(q_bd, kv_t(ck_ref, b)) + bias_c for b, q_bd in zip(seqs, q_bds)]
    prs, pns = [], []
    for row, q_bd, s_c in zip(rows, q_bds, s_cs):
        kn = row[:, ATT_WIDTH:ATT_WIDTH + KV_WIDTH]
        s_n = jnp.sum(rnd(q_bd) * rnd(kn), axis=-1, keepdims=True) + bias_n
        m = jnp.maximum(jnp.maximum(jnp.max(s_c, axis=-1, keepdims=True), s_n), sink)
        p_c = jnp.exp(s_c - m)
        p_n = jnp.exp(s_n - m)
        den = jnp.sum(p_c, axis=-1, keepdims=True) + p_n + jnp.exp(sink - m)
        prs.append(p_c / den)
        pns.append(p_n / den)
    pvs = [_mm_nt(pr, kv_t(cv_ref, b)) for b, pr in zip(seqs, prs)]
    for b, row, pv, pn in zip(seqs, rows, pvs, pns):
        vn = row[:, ATT_WIDTH + KV_WIDTH:]
        o_full = pv + rnd(pn) * rnd(vn)
        o_sel = jnp.where(low_group, o_full[:, :HEAD_DIM], o_full[:, HEAD_DIM:])
        o_ref[b:b + 1, :] = jnp.concatenate([o_sel[h:h + 1, :] for h in range(ATT_HEADS)], axis=1)


def _attn_sample(att, ck, cv, bucket, rel_bias, sink):
    nseq = att.shape[0]
    smem = pl.BlockSpec(memory_space=pltpu.SMEM)
    cache = pl.BlockSpec((ATT_S_BB, ATT_KV_HEADS, HEAD_DIM, WINDOW), lambda i: (i, 0, 0, 0))
    return pl.pallas_call(
        _attn_sample_kernel,
        grid=(nseq // ATT_S_BB,),
        in_specs=[pl.BlockSpec((ATT_S_BB, ATT_COLS), lambda i: (i, 0)), cache, cache,
                  pl.BlockSpec(bucket.shape, lambda i: (0, 0)), smem, smem],
        out_specs=[pl.BlockSpec((ATT_S_BB, ATT_WIDTH), lambda i: (i, 0)), cache, cache],
        out_shape=[jax.ShapeDtypeStruct((nseq, ATT_WIDTH), F32),
                   jax.ShapeDtypeStruct(ck.shape, F32), jax.ShapeDtypeStruct(cv.shape, F32)],
        scratch_shapes=[pltpu.VMEM((ATT_HEADS, LANES), F32), pltpu.VMEM((ATT_HEADS, LANES), F32)],
        compiler_params=_params("arbitrary"),
        name="attn_sample",
    )(att, ck, cv, bucket, rel_bias, sink)


GDN_TB = 128
GDN_NC = GDN_TB // DN_CHUNK


def _gdn_gates(ba, alog, dtb):
    beta = _sigmoid(ba)
    g = -jnp.exp(alog) * _softplus(ba + dtb)
    return beta, g


def _pair_diag(x, lo):
    xb = x.astype(BF16)
    zero = jnp.zeros_like(xb)
    return jnp.concatenate([jnp.where(lo, xb, zero), jnp.where(lo, zero, xb)], axis=0)


def _gdn_prompt_kernel(qkv_ref, dz_ref, ba_ref, alog_ref, dtb_ref, dnx_ref,
                       hsum_ref, expb_ref, expg_ref, ltri_ref,
                       o_ref, s_out_ref, s_scr):
    i = pl.program_id(0)
    nb = qkv_ref.shape[0]

    @pl.when(i == 0)
    def _():
        s_scr[...] = jnp.zeros(s_scr.shape, F32)

    hsum = hsum_ref[...]
    ri = lax.broadcasted_iota(jnp.int32, (DN_CHUNK, PAIR), 0)
    ci = lax.broadcasted_iota(jnp.int32, (DN_CHUNK, PAIR), 1)
    lo = ci < DN_DK
    cj = jnp.where(lo, ci, ci - DN_DK)
    causal = ri >= cj
    strict = ri > cj
    eye = (ri == cj).astype(F32)

    def sel2(x, m):
        hi = x.astype(BF16)
        lw = (x - hi.astype(F32)).astype(BF16)
        return (jnp.dot(hi, m, preferred_element_type=F32) + jnp.dot(lw, m, preferred_element_type=F32))

    pre = []
    for b in range(nb):
        q = qkv_ref[b, :, :DN_WIDTH]
        k = qkv_ref[b, :, DN_WIDTH:2 * DN_WIDTH]
        v = qkv_ref[b, :, 2 * DN_WIDTH:]
        beta_c, g_c = _gdn_gates(ba_ref[b], alog_ref[...], dtb_ref[...])
        beta = sel2(beta_c, expb_ref[...])
        gam_c = _mm_sel_lhs(ltri_ref[...], g_c)
        gam = _mm_sel_rhs(gam_c, expg_ref[...])
        gam_t = gam_c.T
        kb = k * beta
        egam = jnp.exp(gam)
        pre.append(dict(q=q, k=k, kb=kb, vb=v * beta, qg=q * egam, wr=kb * egam, gam=gam, gam_t=gam_t))

    probs = [(c, b, p) for c in range(GDN_NC) for b in range(nb) for p in range(N_PAIRS)]
    pick = lambda m: jnp.where(lo, m[:DN_DK], m[DN_DK:])
    rows_of = lambda c: slice(c * DN_CHUNK, (c + 1) * DN_CHUNK)
    sl = lambda name, c, b, p: pre[b][name][rows_of(c), p * PAIR:(p + 1) * PAIR]
    raws = []
    for c, b, p in probs:
        k_p = sl("k", c, b, p)
        k_rows = jnp.concatenate([jnp.where(lo, k_p, 0.0), jnp.where(lo, 0.0, k_p)], axis=0)
        raws.append(_mm_nt(jnp.concatenate([sl("kb", c, b, p), sl("q", c, b, p)], axis=0), k_rows))
    pws, ts, qks = [], [], []
    for (c, b, p), raw in zip(probs, raws):
        gcol = sl("gam", c, b, p)
        h0 = DN_HEADS + 2 * p
        gam_t = pre[b]["gam_t"]
        grow = jnp.concatenate([gam_t[h0:h0 + 1, rows_of(c)], gam_t[h0 + 1:h0 + 2, rows_of(c)]], axis=1)
        decay = jnp.exp(jnp.where(causal, gcol - grow, NEG_INF))
        a = jnp.where(strict, raw[:DN_CHUNK] * decay, 0.0)
        qks.append(jnp.where(causal, raw[DN_CHUNK:] * decay, 0.0))
        pws.append(-a)
        ts.append(eye - a)
    pws = [_mm(pw, _pair_diag(pw, lo)) for pw in pws]
    for _ in range(4):
        rs = [_mm(jnp.concatenate([pw, t], axis=0), _pair_diag(pw, lo)) for pw, t in zip(pws, ts)]
        pws = [r[:DN_CHUNK] for r in rs]
        ts = [t + r[DN_CHUNK:] for t, r in zip(ts, rs)]
    rs = [_mm(t, _pair_diag(pw, lo)) for pw, t in zip(pws, ts)]
    ts = [t + r for t, r in zip(ts, rs)]
    sols = [_mm(t, jnp.concatenate([_pair_diag(sl("vb", c, b, p), lo), _pair_diag(sl("wr", c, b, p), lo)],
                                   axis=1)) for (c, b, p), t in zip(probs, ts)]
    qkuws = [_mm(qk, jnp.concatenate([_pair_diag(s[:, :PAIR], lo), _pair_diag(s[:, PAIR:], lo)], axis=1))
             for qk, s in zip(qks, sols)]
    crosses, gls = [], []
    for (c, b, p), s in zip(probs, sols):
        last = (c + 1) * DN_CHUNK - 1
        gam_last = pre[b]["gam"][last:last + 1, p * PAIR:(p + 1) * PAIR]
        kd = sl("k", c, b, p) * jnp.exp(gam_last - sl("gam", c, b, p))
        crosses.append(_mm_tn(kd, s))
        gls.append(jnp.exp(gam_last))
    lhs = [jnp.concatenate([pick(cr[:, PAIR:]), sl("qg", c, b, p) - qkuw[:, PAIR:]], axis=0)
           for (c, b, p), cr, qkuw in zip(probs, crosses, qkuws)]

    o_rows = [[] for _ in range(nb)]
    per_chunk = nb * N_PAIRS
    for c in range(GDN_NC):
        sel = slice(c * per_chunk, (c + 1) * per_chunk)
        s_olds = [s_scr[b, p] for _, b, p in probs[sel]]
        rs = [_mm(l, _pair_diag(s_old, lo)) for l, s_old in zip(lhs[sel], s_olds)]
        o_pairs = [[] for _ in range(nb)]
        for (_, b, p), r, s_old, gl, cr, qkuw in zip(probs[sel], rs, s_olds, gls[sel], crosses[sel], qkuws[sel]):
            s_scr[b, p] = gl * s_old - r[:DN_DK] + pick(cr[:, :PAIR])
            o_pairs[b].append(r[DN_DK:] + qkuw[:, :PAIR])
        for b in range(nb):
            o_rows[b].append(jnp.concatenate(o_pairs[b], axis=1))

    o_all = jnp.concatenate([jnp.concatenate(rows, axis=0) for rows in o_rows], axis=0)
    inv_rms = lax.rsqrt(_head_sums(o_all * o_all, hsum) * (1.0 / DN_DV) + EPS)
    for b in range(nb):
        rows = slice(b * GDN_TB, (b + 1) * GDN_TB)
        o_ref[b] = (o_all[rows] * inv_rms[rows] * dnx_ref[...] * _silu_tanh(dz_ref[b])).astype(o_ref.dtype)

    @pl.when(i == pl.num_programs(0) - 1)
    def _():
        for b in range(nb):
            for p in range(N_PAIRS):
                s_p = s_scr[b, p]
                s_out_ref[b, 2 * p] = s_p[:, :DN_DV]
                s_out_ref[b, 2 * p + 1] = s_p[:, DN_DV:]


def _gdn_consts():
    lane = np.arange(DN_WIDTH)
    pl_lane = np.arange(PAIR)
    hsum = (pl_lane[:, None] // DN_DV == pl_lane[None, :] // DN_DV)
    src = np.arange(LANES)
    expb = (src[:, None] == lane[None, :] // DN_DV)
    expg = (src[:, None] == DN_HEADS + lane[None, :] // DN_DV)
    tok = np.arange(GDN_TB)
    ltri = np.logical_and(tok[:, None] >= tok[None, :],
                          tok[:, None] // DN_CHUNK == tok[None, :] // DN_CHUNK)
    as_bf16 = lambda m: jnp.asarray(m.astype(np.float32), dtype=BF16)
    return as_bf16(hsum), as_bf16(expb), as_bf16(expg), as_bf16(ltri)


def _gdn_prompt(xc, dz, ba, alog, dtb, dnx, batch, seq):
    nt = seq // GDN_TB
    hsum, expb, expg, ltri = _gdn_consts()
    row = lambda n: pl.BlockSpec((batch, GDN_TB, n), lambda i: (0, i, 0))
    full = lambda a: pl.BlockSpec(a.shape, lambda i: (0,) * a.ndim)
    consts = (alog, dtb, dnx, hsum, expb, expg, ltri)
    as3d = lambda a: a.reshape(batch, seq, a.shape[-1])
    o, s = pl.pallas_call(
        _gdn_prompt_kernel,
        grid=(nt,),
        in_specs=[row(CONV_CH), row(DN_WIDTH), row(LANES)] + [full(a) for a in consts],
        out_specs=[row(DN_WIDTH),
                   pl.BlockSpec((batch, DN_HEADS, DN_DK, DN_DV), lambda i: (0, 0, 0, 0))],
        out_shape=[jax.ShapeDtypeStruct((batch, seq, DN_WIDTH), BF16),
                   jax.ShapeDtypeStruct((batch, DN_HEADS, DN_DK, DN_DV), F32)],
        scratch_shapes=[pltpu.VMEM((batch, N_PAIRS, DN_DK, PAIR), F32)],
        compiler_params=_params("arbitrary"),
        name="gdn_prompt",
    )(as3d(xc), as3d(dz), as3d(ba), *consts)
    return o.reshape(batch * seq, DN_WIDTH), s


def _gdn_sample_front_kernel(xc_ref, dz_ref, ba_ref, sc_ref, cw_ref, alog_ref, dtb_ref, hsum_ref,
                             q_ref, k_ref, v_ref, dz_t_ref, gates_ref):
    xc = xc_ref[...]
    y = sc_ref[0] * cw_ref[0:1, :]
    y = y + sc_ref[1] * cw_ref[1:2, :]
    y = y + sc_ref[2] * cw_ref[2:3, :]
    y = _silu(y + xc * cw_ref[3:4, :])
    hsum = hsum_ref[...]
    q = y[:, :DN_WIDTH]
    k = y[:, DN_WIDTH:2 * DN_WIDTH]
    q = q * lax.rsqrt(_mm_sel_rhs(q * q, hsum) + EPS) * (DN_DK ** -0.5)
    k = k * lax.rsqrt(_mm_sel_rhs(k * k, hsum) + EPS)
    beta_c, g_c = _gdn_gates(ba_ref[...], alog_ref[...], dtb_ref[...])
    q_ref[...] = q.T
    k_ref[...] = k.T
    v_ref[...] = y[:, 2 * DN_WIDTH:].T
    dz_t_ref[...] = dz_ref[...].T
    gates_ref[0:LANES, :] = beta_c.T
    gates_ref[LANES:, :] = jnp.exp(g_c).T


def _gdn_sample_step_kernel(q_ref, k_ref, v_ref, dz_ref, gates_ref, dn_ref, s_ref, o_ref, s_out_ref):
    h = pl.program_id(0)
    beta = gates_ref[pl.ds(h, 1), :]
    eg = gates_ref[pl.ds(LANES + DN_HEADS + h, 1), :]
    q, k, v = q_ref[...], k_ref[...], v_ref[...]
    w = (k * beta) * eg
    qg = q * eg
    ws = jnp.zeros(v.shape, F32)
    qs = jnp.zeros(v.shape, F32)
    for dk in range(DN_DK):
        s_dk = s_ref[0, dk]
        ws = ws + w[dk:dk + 1, :] * s_dk
        qs = qs + qg[dk:dk + 1, :] * s_dk
    v_new = v * beta - ws
    qk = jnp.sum(q * k, axis=0, keepdims=True)
    o = qs + qk * v_new
    for dk in range(DN_DK):
        s_out_ref[0, dk] = s_ref[0, dk] * eg + k[dk:dk + 1, :] * v_new
    o = o * lax.rsqrt(jnp.mean(o * o, axis=0, keepdims=True) + EPS) * dn_ref[...]
    o_ref[...] = o * _silu(dz_ref[...])


def _gdn_sample_lanes(xc, dz, ba, sconv_t, state_t, conv_w, alog, dtb, dn):
    nseq = xc.shape[0]
    assert nseq == LANES
    lane = np.arange(DN_WIDTH)
    hsum = jnp.asarray((lane[:, None] // DN_DV == lane[None, :] // DN_DV).astype(np.float32), dtype=BF16)
    full = lambda a: pl.BlockSpec(a.shape, lambda i: (0,) * a.ndim)
    cm = jax.ShapeDtypeStruct((DN_WIDTH, nseq), F32)
    front_in = (xc, dz, ba, sconv_t, conv_w, alog, dtb, hsum)
    q_t, k_t, v_t, dz_t, gates_t = pl.pallas_call(
        _gdn_sample_front_kernel,
        grid=(1,),
        in_specs=[full(a) for a in front_in],
        out_specs=[pl.BlockSpec((DN_WIDTH, nseq), lambda i: (0, 0))] * 4
                  + [pl.BlockSpec((2 * LANES, nseq), lambda i: (0, 0))],
        out_shape=[cm, cm, cm, cm, jax.ShapeDtypeStruct((2 * LANES, nseq), F32)],
        compiler_params=_params("arbitrary"),
        name="gdn_sample_front",
    )(*front_in)
    dn_b = jnp.broadcast_to(dn.reshape(DN_DV, 1), (DN_DV, nseq))
    head = pl.BlockSpec((DN_DK, nseq), lambda h: (h, 0))
    st = pl.BlockSpec((1, DN_DK, DN_DV, nseq), lambda h: (h, 0, 0, 0))
    return pl.pallas_call(
        _gdn_sample_step_kernel,
        grid=(DN_HEADS,),
        in_specs=[head, head, head, head, full(gates_t), full(dn_b), st],
        out_specs=[head, st],
        out_shape=[cm, jax.ShapeDtypeStruct(state_t.shape, F32)],
        compiler_params=_params("parallel"),
        name="gdn_sample_step",
    )(q_t, k_t, v_t, dz_t, gates_t, dn_b, state_t)


def _route(xn, wr):
    logits = jnp.dot(xn, wr, preferred_element_type=F32)
    lane = lax.broadcasted_iota(jnp.int32, logits.shape, 1).astype(F32)
    first_at = lambda hit: jnp.min(jnp.where(hit, lane, float(LANES)), axis=-1, keepdims=True)
    glog = jnp.where(lane < N_GROUPS, logits, NEG_INF)
    gmax = jnp.max(glog, axis=-1, keepdims=True)
    gsel = first_at(glog == gmax)
    pgsel = 1.0 / jnp.sum(jnp.exp(glog - gmax), axis=-1, keepdims=True)
    lo = ROUTER_OFF + gsel * EXPERTS_PER_GROUP
    in_group = jnp.logical_and(lane >= lo, lane < lo + EXPERTS_PER_GROUP)
    elog = jnp.where(in_group, logits, NEG_INF)
    m1 = jnp.max(elog, axis=-1, keepdims=True)
    i1 = first_at(elog == m1)
    z = jnp.sum(jnp.exp(elog - m1), axis=-1, keepdims=True)
    elog2 = jnp.where(lane == i1, NEG_INF, elog)
    m2 = jnp.max(elog2, axis=-1, keepdims=True)
    i2 = first_at(elog2 == m2)
    p1 = 1.0 / z
    p2 = jnp.exp(m2 - m1) / z
    tot = p1 + p2
    return lane, i1, i2, p1 / tot * pgsel, p2 / tot * pgsel


def _outproj_router_kernel(x_ref, oa_ref, od_t_ref, wo_ref, g_ref, wr_ref, h_ref, xn_ref, gate_ref):
    h = (x_ref[...] + _mm(oa_ref[...], wo_ref[:ATT_WIDTH, :])
         + _mm(od_t_ref[...].T, wo_ref[ATT_WIDTH:, :]))
    h_ref[...] = h
    xn = _rmsnorm(h, g_ref[...]).astype(BF16)
    xn_ref[...] = xn
    lane, i1, i2, g1, g2 = _route(xn, wr_ref[...])
    gate_ref[...] = jnp.where(lane == i1, g1, 0.0) + jnp.where(lane == i2, g2, 0.0)


def _outproj_router(x, oa, od_t, wo, g, wr):
    t = x.shape[0]
    tm = t
    row = lambda n: pl.BlockSpec((tm, n), lambda i: (i, 0))
    full = lambda a: pl.BlockSpec(a.shape, lambda i: (0,) * a.ndim)
    return pl.pallas_call(
        _outproj_router_kernel,
        grid=(t // tm,),
        in_specs=[row(D_MODEL), row(ATT_WIDTH), full(od_t), full(wo), full(g), full(wr)],
        out_specs=[row(D_MODEL), row(D_MODEL), row(LANES)],
        out_shape=[jax.ShapeDtypeStruct((t, D_MODEL), F32), jax.ShapeDtypeStruct((t, D_MODEL), BF16),
                   jax.ShapeDtypeStruct((t, LANES), F32)],
        compiler_params=_params("parallel"),
        name="outproj_router",
    )(x, oa, od_t, wo, g, wr)


MOE_TM = 512
POS_TM = 1024
ROUTE_CHUNK = 512
INFO_G1, INFO_G2, INFO_E1, INFO_E2 = 0, 1, 2, 3


def _moe_tiles(t):
    return (2 * t) // MOE_TM + N_EXPERTS


HALF = D_MODEL // 2
U32 = jnp.uint32


def _pack_rows(x):
    bits = lambda v: lax.bitcast_convert_type(v.astype(BF16).astype(F32), U32)
    return bits(x[:, HALF:]) | (bits(x[:, :HALF]) >> 16)


def _unpack_rows(w):
    lo = lax.bitcast_convert_type(w << 16, F32)
    hi = lax.bitcast_convert_type(w & jnp.uint32(0xFFFF0000), F32)
    return lo, hi


def _route_kernel(x_ref, oa_ref, od_ref, wo_ref, g_ref, wr_ref, h_ref, xn_ref, info_ref, cnt_ref, run_scr):
    @pl.when(pl.program_id(0) == 0)
    def _():
        run_scr[...] = jnp.zeros(run_scr.shape, F32)

    seen = jnp.zeros(run_scr.shape, F32)
    for c in range(0, x_ref.shape[0], ROUTE_CHUNK):
        rows = pl.ds(c, ROUTE_CHUNK)
        h = (x_ref[rows, :] + _mm(oa_ref[rows, :], wo_ref[:ATT_WIDTH, :])
             + _mm(od_ref[rows, :], wo_ref[ATT_WIDTH:, :]))
        h_ref[rows, :] = h
        xn = _rmsnorm(h, g_ref[...])
        xn_ref[rows, :] = _pack_rows(xn)
        lane, i1, i2, g1, g2 = _route(xn.astype(BF16), wr_ref[...])
        info = jnp.where(lane == INFO_G1, g1, 0.0) + jnp.where(lane == INFO_G2, g2, 0.0)
        info = info + jnp.where(lane == INFO_E1, i1, 0.0) + jnp.where(lane == INFO_E2, i2, 0.0)
        info_ref[rows, :] = info
        picked = jnp.logical_or(lane == i1, lane == i2).astype(F32)
        seen = seen + jnp.sum(picked, axis=0, keepdims=True)
    run_scr[...] += seen
    cnt_ref[...] = run_scr[...]


def _route_sparse(x, oa, od, wo, g, wr):
    t = x.shape[0]
    tm = WIDE_TM
    row = lambda n: pl.BlockSpec((tm, n), lambda i: (i, 0))
    full = lambda a: pl.BlockSpec(a.shape, lambda i: (0,) * a.ndim)
    return pl.pallas_call(
        _route_kernel,
        grid=(t // tm,),
        in_specs=[row(D_MODEL), row(ATT_WIDTH), row(DN_WIDTH), full(wo), full(g), full(wr)],
        out_specs=[row(D_MODEL), row(HALF), row(LANES), pl.BlockSpec((1, LANES), lambda i: (0, 0))],
        out_shape=[jax.ShapeDtypeStruct((t, D_MODEL), F32), jax.ShapeDtypeStruct((t, HALF), U32),
                   jax.ShapeDtypeStruct((t, LANES), F32), jax.ShapeDtypeStruct((1, LANES), F32)],
        scratch_shapes=[pltpu.VMEM((1, LANES), F32)],
        compiler_params=_params("arbitrary"),
        name="route",
    )(x, oa, od, wo, g, wr)


def _positions_kernel(info_ref, cnt_ref, ltri_ref, utri_ref, pos_ref, run_scr, off_scr):
    info = info_ref[...]
    lane = lax.broadcasted_iota(jnp.int32, info.shape, 1).astype(F32)
    hit1 = lane == info[:, INFO_E1:INFO_E1 + 1]
    hit2 = lane == info[:, INFO_E2:INFO_E2 + 1]
    onehot = jnp.logical_or(hit1, hit2).astype(F32)

    @pl.when(pl.program_id(0) == 0)
    def _():
        ln = lax.broadcasted_iota(jnp.int32, cnt_ref.shape, 1)
        is_expert = jnp.logical_and(ln >= ROUTER_OFF, ln < ROUTER_OFF + N_EXPERTS)
        tiles = jnp.where(is_expert, jnp.maximum(jnp.floor((cnt_ref[...] + (MOE_TM - 1)) * (1.0 / MOE_TM)), 1.0), 0.0)
        off_scr[...] = MOE_TM * jnp.dot(tiles.astype(BF16), utri_ref[...], preferred_element_type=F32)
        run_scr[...] = jnp.zeros(run_scr.shape, F32)

    before = (jnp.dot(ltri_ref[...], onehot.astype(BF16), preferred_element_type=F32)
              + run_scr[...] + off_scr[...])
    pos1 = jnp.sum(jnp.where(hit1, before, 0.0), axis=-1, keepdims=True)
    pos2 = jnp.sum(jnp.where(hit2, before, 0.0), axis=-1, keepdims=True)
    both = jnp.where(lane == 0, pos1, 0.0) + jnp.where(lane == 1, pos2, 0.0)
    pos_ref[...] = both.T.astype(jnp.int32)
    run_scr[...] += jnp.sum(onehot, axis=0, keepdims=True)


def _positions(info, cnt):
    t = info.shape[0]
    tm = min(t, POS_TM)
    tok = np.arange(tm)
    ltri = jnp.asarray((tok[:, None] > tok[None, :]).astype(np.float32), dtype=BF16)
    ln = np.arange(LANES)
    utri = jnp.asarray((ln[:, None] < ln[None, :]).astype(np.float32), dtype=BF16)
    full = lambda a: pl.BlockSpec(a.shape, lambda i: (0,) * a.ndim)
    return pl.pallas_call(
        _positions_kernel,
        grid=(t // tm,),
        in_specs=[pl.BlockSpec((tm, LANES), lambda i: (i, 0)), full(cnt), full(ltri), full(utri)],
        out_specs=pl.BlockSpec((LANES, tm), lambda i: (0, i)),
        out_shape=jax.ShapeDtypeStruct((LANES, t), jnp.int32),
        scratch_shapes=[pltpu.VMEM((1, LANES), F32), pltpu.VMEM((1, LANES), F32)],
        compiler_params=_params("arbitrary"),
        name="positions",
    )(info, cnt, ltri, utri)


def _experts_kernel(te_ref, tv_ref, nt_ref, xs_ref, wg_hbm, wu_hbm, wd_hbm, xn_new_ref, gate_new_ref,
                    ys_ref, moe_new_ref, wg_s, wu_s, wd_s, wg_f, wu_f, wd_f, wsem):
    i = pl.program_id(0)
    used = i < nt_ref[0]
    expert = te_ref[i]

    def fetch(e):
        slot = e % 2
        return [pltpu.make_async_copy(src.at[e], dst.at[slot], wsem.at[slot, j])
                for j, (src, dst) in enumerate(((wg_hbm, wg_f), (wu_hbm, wu_f), (wd_hbm, wd_f)))]

    @pl.when(jnp.logical_or(i == 0, expert != te_ref[jnp.maximum(i - 1, 0)]))
    def _():
        @pl.when(i == 0)
        def _():
            for c in fetch(expert):
                c.start()
        for c in fetch(expert):
            c.wait()

        @pl.when(expert + 1 < N_EXPERTS)
        def _():
            for c in fetch(expert + 1):
                c.start()
        slot = expert % 2
        wg_s[...] = wg_f[slot].astype(BF16)
        wu_s[...] = wu_f[slot].astype(BF16)
        wd_s[...] = wd_f[slot].astype(BF16)
        xn = xn_new_ref[...]
        lane = lax.broadcasted_iota(jnp.int32, gate_new_ref.shape, 1)
        gate = jnp.sum(jnp.where(lane == expert + ROUTER_OFF, gate_new_ref[...], 0.0), axis=-1, keepdims=True)
        hg = jnp.dot(xn, wg_s[...], preferred_element_type=F32)
        hu = jnp.dot(xn, wu_s[...], preferred_element_type=F32)
        hm = _silu(hg) * hu * gate
        y = jnp.dot(hm.astype(BF16), wd_s[...], preferred_element_type=F32)

        @pl.when(i == 0)
        def _():
            moe_new_ref[...] = y

        @pl.when(i > 0)
        def _():
            moe_new_ref[...] += y

    @pl.when(used)
    def _():
        row = lax.broadcasted_iota(jnp.int32, xs_ref.shape, 0)
        x_lo, x_hi = _unpack_rows(jnp.where(row < tv_ref[i], xs_ref[...], jnp.uint32(0)))
        x_lo = x_lo.astype(BF16)
        x_hi = x_hi.astype(BF16)
        up = lambda w_s: (jnp.dot(x_lo, w_s[:HALF, :], preferred_element_type=F32)
                          + jnp.dot(x_hi, w_s[HALF:, :], preferred_element_type=F32))
        hm = (_silu_tanh(up(wg_s)) * up(wu_s)).astype(BF16)
        ys_ref[...] = _pack_rows(jnp.dot(hm, wd_s[...], preferred_element_type=F32))

    @pl.when(jnp.logical_not(used))
    def _():
        ys_ref[...] = jnp.zeros(ys_ref.shape, U32)


def _experts(xs, tile_expert, tile_valid, n_tiles, wg, wu, wd, xn_new, gate_new):
    max_tiles = xs.shape[0] // MOE_TM
    rows = pl.BlockSpec((MOE_TM, HALF), lambda i, te, tv, nt: (i, 0))
    hbm = pl.BlockSpec(memory_space=pl.ANY)
    full = lambda a: pl.BlockSpec(a.shape, lambda i, te, tv, nt: (0,) * a.ndim)
    return pl.pallas_call(
        _experts_kernel,
        grid_spec=pltpu.PrefetchScalarGridSpec(
            num_scalar_prefetch=3, grid=(max_tiles,),
            in_specs=[rows, hbm, hbm, hbm, full(xn_new), full(gate_new)],
            out_specs=[rows, pl.BlockSpec(xn_new.shape, lambda i, te, tv, nt: (0, 0))],
            scratch_shapes=[pltpu.VMEM((D_MODEL, D_EXPERT), BF16), pltpu.VMEM((D_MODEL, D_EXPERT), BF16),
                            pltpu.VMEM((D_EXPERT, D_MODEL), BF16),
                            pltpu.VMEM((2, D_MODEL, D_EXPERT), F32), pltpu.VMEM((2, D_MODEL, D_EXPERT), F32),
                            pltpu.VMEM((2, D_EXPERT, D_MODEL), F32), pltpu.SemaphoreType.DMA((2, 3))]),
        out_shape=[jax.ShapeDtypeStruct(xs.shape, U32), jax.ShapeDtypeStruct(xn_new.shape, F32)],
        compiler_params=_params("arbitrary"),
        name="experts",
    )(tile_expert, tile_valid, n_tiles, xs, wg, wu, wd, xn_new, gate_new)


SC_IDX = 128
SC_ROWS = 64
SC_WORKERS = 32
SC_GATHER_ROWS = 32
SC_GATHER_BUFS = 4


def _sc_mesh():
    return plsc.VectorSubcoreMesh(core_axis_name="c", subcore_axis_name="s")


def _sc_windows(t, fn):
    per_worker = t // SC_WORKERS
    worker = lax.axis_index(("c", "s"))

    @pl.loop(0, per_worker // SC_IDX)
    def _(w):
        fn(worker * per_worker + w * SC_IDX)


def _sc_scatter_rows(xn, pos1, pos2, n_rows):
    t, d = xn.shape
    assert t % (SC_WORKERS * SC_IDX) == 0
    idx_t = pltpu.VMEM((1, SC_IDX), jnp.int32)

    @pl.kernel(out_type=jax.ShapeDtypeStruct((n_rows, d), xn.dtype), mesh=_sc_mesh(),
               scratch_types=[idx_t, idx_t, pltpu.VMEM((SC_ROWS, d), xn.dtype)])
    def scatter(x_hbm, p1_hbm, p2_hbm, o_hbm, i1_v, i2_v, buf):
        def window(base):
            pltpu.sync_copy(p1_hbm.at[:, pl.ds(base, SC_IDX)], i1_v)
            pltpu.sync_copy(p2_hbm.at[:, pl.ds(base, SC_IDX)], i2_v)
            for k in range(SC_IDX // SC_ROWS):
                pltpu.sync_copy(x_hbm.at[pl.ds(base + k * SC_ROWS, SC_ROWS)], buf)
                pltpu.sync_copy(buf, o_hbm.at[i1_v.at[0, pl.ds(k * SC_ROWS, SC_ROWS)]])
                pltpu.sync_copy(buf, o_hbm.at[i2_v.at[0, pl.ds(k * SC_ROWS, SC_ROWS)]])
        _sc_windows(t, window)

    return scatter(xn, pos1.reshape(1, t), pos2.reshape(1, t))


def _sc_gather_rows(ys, pos1, pos2):
    t = pos1.shape[0]
    d = ys.shape[1]
    assert t % (SC_WORKERS * SC_IDX) == 0
    per_worker = t // SC_WORKERS
    idx_t = pltpu.VMEM((1, per_worker), jnp.int32)
    out = jax.ShapeDtypeStruct((t, d), ys.dtype)

    nbuf, rows = SC_GATHER_BUFS, SC_GATHER_ROWS
    buf_t = pltpu.VMEM((rows, d), ys.dtype)

    @pl.kernel(out_type=(out, out), mesh=_sc_mesh(),
               scratch_types=[idx_t, idx_t] + [buf_t] * nbuf
                             + [pltpu.SemaphoreType.DMA((nbuf,)), pltpu.SemaphoreType.DMA((nbuf,))])
    def gather(y_hbm, p1_hbm, p2_hbm, o1_hbm, o2_hbm, i1_v, i2_v, *rest):
        bufs, (gsem, wsem) = rest[:nbuf], rest[nbuf:]
        base = lax.axis_index(("c", "s")) * per_worker
        pltpu.sync_copy(p1_hbm.at[:, pl.ds(base, per_worker)], i1_v)
        pltpu.sync_copy(p2_hbm.at[:, pl.ds(base, per_worker)], i2_v)
        items = [(idx_v, o_hbm, k) for k in range(per_worker // rows)
                 for idx_v, o_hbm in ((i1_v, o1_hbm), (i2_v, o2_hbm))]
        n_items = len(items)

        def read(n):
            idx_v, _, k = items[n]
            return pltpu.make_async_copy(y_hbm.at[idx_v.at[0, pl.ds(k * rows, rows)]],
                                         bufs[n % nbuf], gsem.at[n % nbuf])

        def write(n):
            _, o_hbm, k = items[n]
            return pltpu.make_async_copy(bufs[n % nbuf], o_hbm.at[pl.ds(base + k * rows, rows)],
                                         wsem.at[n % nbuf])

        for n in range(min(nbuf - 1, n_items)):
            read(n).start()
        waited = 0
        for n in range(n_items):
            read(n).wait()
            write(n).start()
            ahead = n + nbuf - 1
            if ahead < n_items:
                if n >= 1:
                    write(n - 1).wait()
                    waited = n
                read(ahead).start()
        for n in range(waited, n_items):
            write(n).wait()

    return gather(ys, pos1.reshape(1, t), pos2.reshape(1, t))


TAIL_CHUNK = 256


def _ple_sparse_kernel(h_ref, info_ref, y1_ref, y2_ref, p_ref, wpp_ref, wpg_ref, gp_ref, gf_ref, y_ref):
    for c in range(0, h_ref.shape[0], TAIL_CHUNK):
        rows = pl.ds(c, TAIL_CHUNK)
        info = info_ref[rows, :]
        g1 = info[:, INFO_G1:INFO_G1 + 1]
        g2 = info[:, INFO_G2:INFO_G2 + 1]
        y1_lo, y1_hi = _unpack_rows(y1_ref[rows, :])
        y2_lo, y2_hi = _unpack_rows(y2_ref[rows, :])
        moe = jnp.concatenate([g1 * y1_lo + g2 * y2_lo, g1 * y1_hi + g2 * y2_hi], axis=1)
        h = h_ref[rows, :] + moe
        hn = _rmsnorm(h, gp_ref[...])
        pp_half = _mm(p_ref[rows, :], wpp_ref[...])
        h = h + pp_half * jnp.tanh(_mm(hn, wpg_ref[...])) + pp_half
        y_ref[rows, :] = _rmsnorm(h, gf_ref[...])


def _ple_sparse(h, info, y1, y2, p, wpp, wpg, gp, gf):
    t = h.shape[0]
    tm = WIDE_TM
    row = lambda n: pl.BlockSpec((tm, n), lambda i: (i, 0))
    full = lambda a: pl.BlockSpec(a.shape, lambda i: (0,) * a.ndim)
    return pl.pallas_call(
        _ple_sparse_kernel,
        grid=(t // tm,),
        in_specs=[row(D_MODEL), row(LANES), row(HALF), row(HALF), row(PLE_DIM),
                  full(wpp), full(wpg), full(gp), full(gf)],
        out_specs=row(D_MODEL),
        out_shape=jax.ShapeDtypeStruct((t, D_MODEL), F32),
        compiler_params=_params("parallel"),
        name="ple_sparse",
    )(h, info, y1, y2, p, wpp, wpg, gp, gf)


def _tile_tables(cnt, max_tiles):
    tiles_e = jnp.maximum((cnt + (MOE_TM - 1)) // MOE_TM, 1)
    ends = jnp.cumsum(tiles_e)
    n_tiles = ends[-1]
    tile = jnp.arange(max_tiles, dtype=jnp.int32)
    idx = jnp.minimum(tile, n_tiles - 1)
    tile_expert = jnp.sum((idx[:, None] >= ends[None, :]).astype(jnp.int32), axis=1)
    mine = tile_expert[:, None] == jnp.arange(N_EXPERTS, dtype=jnp.int32)[None, :]
    of_mine = lambda v: jnp.sum(jnp.where(mine, v[None, :], 0), axis=1)
    valid = jnp.clip(of_mine(cnt) - (idx - of_mine(ends - tiles_e)) * MOE_TM, 0, MOE_TM)
    tile_valid = jnp.where(tile < n_tiles, valid, 0).astype(jnp.int32)
    return tile_expert, tile_valid, n_tiles.reshape(1)


def _ple_final_kernel(h_ref, m_ref, p_ref, wpp_ref, wpg_ref, gp_ref, gf_ref, y_ref):
    h = h_ref[...] + m_ref[...]
    hn = _rmsnorm(h, gp_ref[...])
    h = h + _mm(p_ref[...], wpp_ref[...]) * _sigmoid(_mm(hn, wpg_ref[...]))
    y_ref[...] = _rmsnorm(h, gf_ref[...])


def _ple_final(h, m, p, wpp, wpg, gp, gf):
    t = h.shape[0]
    tm = min(t, 256)
    row = lambda n: pl.BlockSpec((tm, n), lambda i: (i, 0))
    full = lambda a: pl.BlockSpec(a.shape, lambda i: (0,) * a.ndim)
    return pl.pallas_call(
        _ple_final_kernel,
        grid=(t // tm,),
        in_specs=[row(D_MODEL), row(D_MODEL), row(PLE_DIM), full(wpp), full(wpg), full(gp), full(gf)],
        out_specs=row(D_MODEL),
        out_shape=jax.ShapeDtypeStruct((t, D_MODEL), F32),
        compiler_params=_params("parallel"),
        name="ple_final",
    )(h, m, p, wpp, wpg, gp, gf)


def kernel(x_prompt, x_sample, p_prompt, p_sample, cache_k, cache_v, state_conv, state_S, rel_bias, norm_mix, w_in, att_sink, conv_w, dn_A_log, dn_dt_bias, dn_norm, w_out, norm_ffn, w_router_group, w_router_expert, w_gate, w_up, w_down, w_ple_proj, w_ple_gate, norm_ple, norm_final):
    batch, seq, _ = x_prompt.shape
    nseq = x_sample.shape[0]
    assert x_sample.shape[1] == 1 and norm_mix.shape[0] == 1 and cache_k.shape[2] == WINDOW
    assert seq % GDN_TB == 0 and seq % ATT_BLOCK == 0

    wt = jnp.swapaxes(w_in[0], 0, 1)
    o_db = ATT_COLS + CONV_CH
    w_in_re = (wt[:o_db].astype(BF16), wt[o_db + 2 * DN_HEADS:].astype(BF16),
               jnp.pad(wt[o_db:o_db + 2 * DN_HEADS], ((0, LANES - 2 * DN_HEADS), (0, 0))).astype(BF16))
    row = lambda a: a.reshape(1, -1).astype(F32)
    pad_lanes = lambda a, off: jnp.zeros((1, LANES), F32).at[0, off:off + a.shape[0]].set(a)
    alog = pad_lanes(dn_A_log[0], DN_HEADS)
    dtb = pad_lanes(dn_dt_bias[0], DN_HEADS)
    dnx = jnp.tile(dn_norm[0], DN_HEADS).reshape(1, DN_WIDTH)
    w_router = jnp.concatenate(
        [w_router_group[0], w_router_expert[0],
         jnp.zeros((D_MODEL, LANES - N_GROUPS - N_EXPERTS), F32)], axis=1).astype(BF16)
    wo = w_out[0].astype(BF16)
    wg, wu, wd = w_gate[0], w_up[0], w_down[0]
    wpp, wpg = w_ple_proj[0].astype(BF16), w_ple_gate[0].astype(BF16)
    sink = att_sink[0]

    qi = np.arange(ATT_BLOCK)[:, None]
    kj = np.arange(2 * ATT_BLOCK)[None, :]
    bucket_p = jnp.asarray(_t5_bucket_np(qi + ATT_BLOCK - kj))
    bucket_s = jnp.asarray(_t5_bucket_np(WINDOW - np.arange(WINDOW)[None, :]))

    xp = x_prompt.reshape(batch * seq, D_MODEL)
    att_p, qkv_p, dz_p, ba_p, xc_tails = _inproj_conv(xp, row(norm_mix[0]), w_in_re, conv_w[0], seq)
    o_att_p = _attn_prompt(att_p, bucket_p, rel_bias, sink, batch, seq)
    o_dn_p, s_p = _gdn_prompt(qkv_p, dz_p, ba_p, alog, dtb, dnx, batch, seq)
    h1, xn2, info, cnt = _route_sparse(xp, o_att_p, o_dn_p, wo, row(norm_ffn[0]), w_router)
    pos = _positions(info, cnt)
    pos1, pos2 = pos[0], pos[1]
    max_tiles = _moe_tiles(batch * seq)
    cnt_e = cnt[0, ROUTER_OFF:ROUTER_OFF + N_EXPERTS].astype(jnp.int32)
    tile_expert, tile_valid, n_tiles = _tile_tables(cnt_e, max_tiles)
    xs_sorted = _sc_scatter_rows(xn2, pos1, pos2, max_tiles * MOE_TM)

    xs = x_sample.reshape(nseq, D_MODEL)
    att_s, xc_s, dz_s, ba_s = _inproj(xs, row(norm_mix[0]), w_in_re)
    ck_t = jnp.transpose(cache_k[0], (0, 2, 3, 1))
    cv_t = jnp.transpose(cache_v[0], (0, 2, 3, 1))
    o_att_s, ks_t, vs_t = _attn_sample(att_s, ck_t, cv_t, bucket_s, rel_bias, sink)
    sconv_t = jnp.swapaxes(state_conv[0], 0, 1)
    o_dn_s_t, s_s_t = _gdn_sample_lanes(xc_s, dz_s, ba_s, sconv_t, jnp.transpose(state_S[0], (1, 2, 3, 0)),
                                        conv_w[0], alog, dtb, dn_norm[0])
    s_s = jnp.transpose(s_s_t, (3, 0, 1, 2))

    h1_s, xn2_s, gates_s = _outproj_router(xs, o_att_s, o_dn_s_t, wo, row(norm_ffn[0]), w_router)

    ys, moe_s = _experts(xs_sorted, tile_expert, tile_valid, n_tiles, wg, wu, wd, xn2_s, gates_s)
    y1, y2 = _sc_gather_rows(ys, pos1, pos2)
    y_s = _ple_final(h1_s, moe_s, p_sample[0].reshape(nseq, PLE_DIM), wpp, wpg, row(norm_ple[0]),
                     row(norm_final))
    y_p = _ple_sparse(h1, info, y1, y2, p_prompt[0].reshape(batch * seq, PLE_DIM),
                      (0.5 * w_ple_proj[0]).astype(BF16), (0.5 * w_ple_gate[0]).astype(BF16),
                      row(norm_ple[0]), row(norm_final))

    att_p3 = att_p.reshape(batch, seq, ATT_COLS)
    kv_shape = (1, batch, WINDOW, ATT_KV_HEADS, HEAD_DIM)
    k_p = att_p3[:, seq - WINDOW:, ATT_WIDTH:ATT_WIDTH + KV_WIDTH].reshape(kv_shape)
    v_p = att_p3[:, seq - WINDOW:, ATT_WIDTH + KV_WIDTH:].reshape(kv_shape)
    conv_p = xc_tails.reshape(batch, -1, TAIL, CONV_CH)[:, -1, TAIL - (CONV_WIDTH - 1):][None]
    k_s = jnp.transpose(ks_t, (0, 3, 1, 2))[None]
    v_s = jnp.transpose(vs_t, (0, 3, 1, 2))[None]
    conv_s = jnp.concatenate([state_conv[0][:, 1:], xc_s[:, None, :]], axis=1)[None]
    return (y_p.reshape(batch, seq, D_MODEL), y_s.reshape(nseq, 1, D_MODEL),
            k_p, v_p, conv_p, s_p[None], k_s, v_s, conv_s, s_s[None])
```

```python
import functools
import math

import numpy as np
import jax
import jax.numpy as jnp
from jax import lax
from jax.experimental import pallas as pl
from jax.experimental.pallas import tpu as pltpu
from jax.experimental.pallas import tpu_sc as plsc

F32 = jnp.float32
BF16 = jnp.bfloat16

D_MODEL = 1024
ATT_HEADS = 8
ATT_KV_HEADS = 2
HEAD_DIM = 64
GQA = ATT_HEADS // ATT_KV_HEADS
WINDOW = 128
ATT_BLOCK = 128
N_BUCKETS = 32
DN_HEADS = 8
DN_DK = 64
DN_DV = 64
CONV_WIDTH = 4
DN_CHUNK = 64
ATT_WIDTH = ATT_HEADS * HEAD_DIM
KV_WIDTH = ATT_KV_HEADS * HEAD_DIM
DN_WIDTH = DN_HEADS * DN_DV
CONV_CH = 3 * DN_WIDTH
N_GROUPS = 4
EXPERTS_PER_GROUP = 8
N_EXPERTS = N_GROUPS * EXPERTS_PER_GROUP
D_EXPERT = 256
PLE_DIM = 256
EPS = 1e-6
NEG_INF = float("-inf")

ATT_COLS = ATT_WIDTH + 2 * KV_WIDTH
LANES = 128
ROUTER_OFF = N_GROUPS
VMEM_LIMIT = 48 * 1024 * 1024
ROW_TM = 512
WIDE_TM = 1024


def _params(*sem):
    return pltpu.CompilerParams(dimension_semantics=sem, vmem_limit_bytes=VMEM_LIMIT)


def _mm(a, b):
    return jnp.dot(a.astype(BF16), b.astype(BF16), preferred_element_type=F32)


def _mm_nt(a, b):
    return lax.dot_general(a.astype(BF16), b.astype(BF16), (((1,), (1,)), ((), ())),
                           preferred_element_type=F32)


def _mm_tn(a, b):
    return lax.dot_general(a.astype(BF16), b.astype(BF16), (((0,), (0,)), ((), ())),
                           preferred_element_type=F32)


def _split3(x):
    h1 = x.astype(BF16)
    r1 = x - h1.astype(F32)
    h2 = r1.astype(BF16)
    h3 = (r1 - h2.astype(F32)).astype(BF16)
    return h1, h2, h3


def _mm_sel_rhs(x, sel):
    h1, h2, h3 = _split3(x)
    d = lambda h: jnp.dot(h, sel, preferred_element_type=F32)
    return d(h1) + d(h2) + d(h3)


def _mm_sel_lhs(sel, x):
    h1, h2, h3 = _split3(x)
    d = lambda h: jnp.dot(sel, h, preferred_element_type=F32)
    return d(h1) + d(h2) + d(h3)


def _sigmoid(x):
    return 1.0 / (1.0 + jnp.exp(-x))


def _silu(x):
    return x * _sigmoid(x)


def _sigmoid_tanh(x):
    return 0.5 * jnp.tanh(0.5 * x) + 0.5


def _silu_tanh(x):
    return x * _sigmoid_tanh(x)


def _softplus(x):
    return jnp.maximum(x, 0.0) + jnp.log1p(jnp.exp(-jnp.abs(x)))


def _rmsnorm(x, g):
    return x * lax.rsqrt(jnp.mean(x * x, axis=-1, keepdims=True) + EPS) * g


def _t5_bucket_np(dist):
    max_exact = N_BUCKETS // 2
    d = np.maximum(dist, 0)
    ratio = (np.log(np.maximum(d, 1).astype(np.float32) / np.float32(max_exact))
             / np.float32(math.log(WINDOW / max_exact))).astype(np.float32)
    large = np.minimum(max_exact + (ratio * np.float32(N_BUCKETS - max_exact)).astype(np.int32),
                       N_BUCKETS - 1)
    return np.where(d < max_exact, d, large).astype(np.int32)


def _bias_lookup(bucket, rb_ref, h):
    acc = jnp.zeros(bucket.shape, F32)
    for t in range(N_BUCKETS):
        acc = jnp.where(bucket == t, rb_ref[t, h], acc)
    return acc


def _inproj_kernel(x_ref, g_ref, wa_ref, wz_ref, wb_ref, att_ref, xc_ref, dz_ref, ba_ref):
    xn = _rmsnorm(x_ref[...], g_ref[...]).astype(BF16)
    att_ref[...] = _mm_nt(xn, wa_ref[:ATT_COLS, :])
    xc_ref[...] = _mm_nt(xn, wa_ref[ATT_COLS:, :])
    dz_ref[...] = _mm_nt(xn, wz_ref[...])
    ba_ref[...] = _mm_nt(xn, wb_ref[...])


def _inproj(x, g, w):
    t = x.shape[0]
    tm = min(t, ROW_TM)
    row = lambda n: pl.BlockSpec((tm, n), lambda i: (i, 0))
    full = lambda a: pl.BlockSpec(a.shape, lambda i: (0,) * a.ndim)
    return pl.pallas_call(
        _inproj_kernel,
        grid=(t // tm,),
        in_specs=[row(D_MODEL), full(g)] + [full(a) for a in w],
        out_specs=[row(ATT_COLS), row(CONV_CH), row(DN_WIDTH), row(LANES)],
        out_shape=[jax.ShapeDtypeStruct((t, n), F32) for n in (ATT_COLS, CONV_CH, DN_WIDTH, LANES)],
        compiler_params=_params("parallel"),
        name="inproj",
    )(x, g, *w)


TAIL = 8
PAIR = 2 * DN_DK
N_PAIRS = DN_WIDTH // PAIR


def _head_sums(z, pair_ones):
    hi = z.astype(BF16)
    lw = (z - hi.astype(F32)).astype(BF16)
    d = lambda a, p: jnp.dot(a[:, p * PAIR:(p + 1) * PAIR], pair_ones, preferred_element_type=F32)
    return jnp.concatenate([d(hi, p) + d(lw, p) for p in range(N_PAIRS)], axis=1)


W_T_CHUNK = 256


def _inproj_conv_kernel(x_ref, g_ref, wat_ref, wzt_ref, wbt_ref, cw_ref, ones_ref,
                        att_ref, qkv_ref, dz_ref, ba_ref, tail_ref, xp_scr, wa_ref, wz_ref, wb_ref,
                        *, tiles_per_seq):
    tm = x_ref.shape[0]

    @pl.when(pl.program_id(0) == 0)
    def _():
        for src, dst in ((wat_ref, wa_ref), (wzt_ref, wz_ref), (wbt_ref, wb_ref)):
            for c in range(0, src.shape[0], W_T_CHUNK):
                n = min(W_T_CHUNK, src.shape[0] - c)
                dst[:, c:c + n] = src[c:c + n, :].T

    @pl.when(pl.program_id(0) % tiles_per_seq == 0)
    def _():
        xp_scr[...] = jnp.zeros((TAIL, CONV_CH), F32)

    xn = _rmsnorm(x_ref[...], g_ref[...]).astype(BF16)
    xc = jnp.dot(xn, wa_ref[:, ATT_COLS:], preferred_element_type=F32)
    att_ref[...] = jnp.dot(xn, wa_ref[:, :ATT_COLS], preferred_element_type=F32)
    dz_ref[...] = jnp.dot(xn, wz_ref[...], preferred_element_type=F32)
    ba_ref[...] = jnp.dot(xn, wb_ref[...], preferred_element_type=F32)

    head = jnp.concatenate([xp_scr[...], xc[:TAIL, :]], axis=0)

    def shifted(j):
        return jnp.concatenate([head[TAIL - j:2 * TAIL - j, :], pltpu.roll(xc, j, axis=0)[TAIL:, :]], axis=0)

    y = shifted(3) * cw_ref[0:1, :]
    y = y + shifted(2) * cw_ref[1:2, :]
    y = y + shifted(1) * cw_ref[2:3, :]
    y = y + xc * cw_ref[3:4, :]
    tail = xc[tm - TAIL:, :]
    xp_scr[...] = tail
    tail_ref[0] = tail
    y = _silu_tanh(y)
    q = y[:, :DN_WIDTH]
    k = y[:, DN_WIDTH:2 * DN_WIDTH]
    inv_norm = lax.rsqrt(_head_sums(jnp.concatenate([q * q, k * k], axis=0), ones_ref[...]) + EPS)
    qkv_ref[:, :DN_WIDTH] = q * inv_norm[:tm] * (DN_DK ** -0.5)
    qkv_ref[:, DN_WIDTH:2 * DN_WIDTH] = k * inv_norm[tm:]
    qkv_ref[:, 2 * DN_WIDTH:] = y[:, 2 * DN_WIDTH:]


def _pair_ones():
    lane = np.arange(PAIR)
    return jnp.asarray((lane[:, None] // DN_DV == lane[None, :] // DN_DV).astype(np.float32), dtype=BF16)


def _inproj_conv(x, g, w, conv_w, seq):
    t = x.shape[0]
    tm = ROW_TM
    assert seq % tm == 0
    ones = _pair_ones()
    row = lambda n: pl.BlockSpec((tm, n), lambda i: (i, 0))
    full = lambda a: pl.BlockSpec(a.shape, lambda i: (0,) * a.ndim)
    return pl.pallas_call(
        functools.partial(_inproj_conv_kernel, tiles_per_seq=seq // tm),
        grid=(t // tm,),
        in_specs=[row(D_MODEL), full(g)] + [full(a) for a in w] + [full(conv_w), full(ones)],
        out_specs=[row(ATT_COLS), row(CONV_CH), row(DN_WIDTH), row(LANES),
                   pl.BlockSpec((1, TAIL, CONV_CH), lambda i: (i, 0, 0))],
        out_shape=[jax.ShapeDtypeStruct((t, n), F32) for n in (ATT_COLS, CONV_CH, DN_WIDTH, LANES)]
                  + [jax.ShapeDtypeStruct((t // tm, TAIL, CONV_CH), F32)],
        scratch_shapes=[pltpu.VMEM((TAIL, CONV_CH), F32)]
                       + [pltpu.VMEM((D_MODEL, a.shape[0]), BF16) for a in w],
        compiler_params=_params("arbitrary"),
        name="inproj_conv",
    )(x, g, *w, conv_w, ones)


GROUP_ROWS = GQA * ATT_BLOCK


def _attn_prompt_kernel(cur_ref, prev_ref, bucket_ref, rb_ref, sink_ref, o_ref, bias_scr, sink_scr):
    i = pl.program_id(0)
    nseq = cur_ref.shape[0]

    @pl.when(i == 0)
    def _():
        qi = lax.broadcasted_iota(jnp.int32, (ATT_BLOCK, 2 * ATT_BLOCK), 0)
        kj = lax.broadcasted_iota(jnp.int32, (ATT_BLOCK, 2 * ATT_BLOCK), 1)
        dist = qi + ATT_BLOCK - kj
        band = jnp.logical_and(dist >= 0, dist < WINDOW)
        bucket = bucket_ref[...]
        hrow = lax.broadcasted_iota(jnp.int32, (GROUP_ROWS, 1), 0) // ATT_BLOCK
        for g in range(ATT_KV_HEADS):
            sink_col = jnp.zeros((GROUP_ROWS, 1), F32)
            for hh in range(GQA):
                h = g * GQA + hh
                bias = jnp.where(band, _bias_lookup(bucket, rb_ref, h), NEG_INF)
                bias_scr[0, g, hh * ATT_BLOCK:(hh + 1) * ATT_BLOCK, :] = bias
                bias_scr[1, g, hh * ATT_BLOCK:(hh + 1) * ATT_BLOCK, :] = jnp.where(kj >= ATT_BLOCK, bias, NEG_INF)
                sink_col = jnp.where(hrow == hh, sink_ref[h], sink_col)
            sink_scr[g] = sink_col

    first = (i == 0).astype(jnp.int32)
    probs = [(b, g) for b in range(nseq) for g in range(ATT_KV_HEADS)]
    scores = []
    for b, g in probs:
        cur = cur_ref[b]
        prev = prev_ref[b]
        q = jnp.concatenate([cur[:, (g * GQA + hh) * HEAD_DIM:(g * GQA + hh + 1) * HEAD_DIM]
                             for hh in range(GQA)], axis=0) * (HEAD_DIM ** -0.5)
        kcol = slice(ATT_WIDTH + g * HEAD_DIM, ATT_WIDTH + (g + 1) * HEAD_DIM)
        k2 = jnp.concatenate([prev[:, kcol], cur[:, kcol]], axis=0)
        scores.append(_mm_nt(q, k2) + bias_scr[first, g])
    probs_p, dens = [], []
    for (b, g), s in zip(probs, scores):
        sink = sink_scr[g]
        m = jnp.maximum(jnp.max(s, axis=-1, keepdims=True), sink)
        p = jnp.exp(s - m)
        dens.append(jnp.sum(p, axis=-1, keepdims=True) + jnp.exp(sink - m))
        probs_p.append(p.astype(BF16))
    outs = {}
    for (b, g), p, den in zip(probs, probs_p, dens):
        vcol = slice(ATT_WIDTH + KV_WIDTH + g * HEAD_DIM, ATT_WIDTH + KV_WIDTH + (g + 1) * HEAD_DIM)
        v2 = jnp.concatenate([prev_ref[b][:, vcol], cur_ref[b][:, vcol]], axis=0)
        outs[b, g] = _mm(p, v2) / den
    for b in range(nseq):
        o_ref[b] = jnp.concatenate([outs[b, g][hh * ATT_BLOCK:(hh + 1) * ATT_BLOCK, :]
                                    for g in range(ATT_KV_HEADS) for hh in range(GQA)],
                                   axis=1).astype(o_ref.dtype)


def _attn_prompt(att, bucket, rel_bias, sink, batch, seq):
    nb = seq // ATT_BLOCK
    smem = pl.BlockSpec(memory_space=pltpu.SMEM)
    att3 = att.reshape(batch, seq, ATT_COLS)
    out = pl.pallas_call(
        _attn_prompt_kernel,
        grid=(nb,),
        in_specs=[
            pl.BlockSpec((batch, ATT_BLOCK, ATT_COLS), lambda i: (0, i, 0)),
            pl.BlockSpec((batch, ATT_BLOCK, ATT_COLS), lambda i: (0, jnp.maximum(i - 1, 0), 0)),
            pl.BlockSpec(bucket.shape, lambda i: (0, 0)),
            smem, smem,
        ],
        out_specs=pl.BlockSpec((batch, ATT_BLOCK, ATT_WIDTH), lambda i: (0, i, 0)),
        out_shape=jax.ShapeDtypeStruct((batch, seq, ATT_WIDTH), BF16),
        scratch_shapes=[pltpu.VMEM((2, ATT_KV_HEADS, GROUP_ROWS, 2 * ATT_BLOCK), F32),
                        pltpu.VMEM((ATT_KV_HEADS, GROUP_ROWS, 1), F32)],
        compiler_params=_params("arbitrary"),
        name="attn_prompt",
    )(att3, att3, bucket, rel_bias, sink)
    return out.reshape(batch * seq, ATT_WIDTH)


ATT_S_BB = 8


def _attn_sample_kernel(att_ref, ck_ref, cv_ref, bucket_ref, rb_ref, sink_ref, o_ref, ks_ref, vs_ref,
                        bias_scr, col_scr):
    hrow = lax.broadcasted_iota(jnp.int32, (ATT_HEADS, LANES), 0)
    lane = lax.broadcasted_iota(jnp.int32, (ATT_HEADS, LANES), 1)

    last = (lax.broadcasted_iota(jnp.int32, (3, WINDOW), 1) == WINDOW - 1).astype(BF16)
    is_last = lax.broadcasted_iota(jnp.int32, (KV_WIDTH, WINDOW), 1) == WINDOW - 1

    def shifted(cache_t, new_row):
        pieces = jnp.concatenate([p.astype(F32) for p in _split3(new_row)], axis=0).astype(BF16)
        col = lax.dot_general(pieces, last, (((0,), (0,)), ((), ())), preferred_element_type=F32)
        out = jnp.where(is_last, col, pltpu.roll(cache_t, WINDOW - 1, axis=1))
        return out.reshape(ATT_KV_HEADS, HEAD_DIM, WINDOW)

    for b in range(ATT_S_BB):
        row = att_ref[b:b + 1, :]
        ks_ref[b] = shifted(ck_ref[b].reshape(KV_WIDTH, WINDOW), row[:, ATT_WIDTH:ATT_WIDTH + KV_WIDTH])
        vs_ref[b] = shifted(cv_ref[b].reshape(KV_WIDTH, WINDOW), row[:, ATT_WIDTH + KV_WIDTH:])

    @pl.when(pl.program_id(0) == 0)
    def _():
        bucket = jnp.broadcast_to(bucket_ref[...], (ATT_HEADS, LANES))
        bias = jnp.zeros((ATT_HEADS, LANES), F32)
        cols = jnp.zeros((ATT_HEADS, LANES), F32)
        for h in range(ATT_HEADS):
            bias = jnp.where(hrow == h, _bias_lookup(bucket, rb_ref, h), bias)
            cols = jnp.where(jnp.logical_and(hrow == h, lane == 0), sink_ref[h], cols)
            cols = jnp.where(jnp.logical_and(hrow == h, lane == 1), rb_ref[0, h], cols)
        bias_scr[...] = jnp.where(lane >= 1, bias, NEG_INF)
        col_scr[...] = cols

    bias_c = bias_scr[...]
    sink = col_scr[:, 0:1]
    bias_n = col_scr[:, 1:2]
    same_group = (hrow // GQA) == (lane // HEAD_DIM)
    low_group = lax.broadcasted_iota(jnp.int32, (ATT_HEADS, HEAD_DIM), 0) < GQA
    rnd = lambda a: a.astype(BF16).astype(F32)
    seqs = range(ATT_S_BB)
    rows = [att_ref[b:b + 1, :] for b in seqs]
    q_bds = []
    for row in rows:
        q = row[:, :ATT_WIDTH] * (HEAD_DIM ** -0.5)
        qh = jnp.concatenate([q[:, h * HEAD_DIM:(h + 1) * HEAD_DIM] for h in range(ATT_HEADS)], axis=0)
        q_bds.append(jnp.where(same_group, jnp.concatenate([qh, qh], axis=1), 0.0))
    kv_t = lambda ref, b: ref[b].reshape(KV_WIDTH, WINDOW)
    s_cs = [_mm(q_bd, kv_t(ck_ref, b)) + bias_c for b, q_bd in zip(seqs, q_bds)]
    prs, pns = [], []
    for row, q_bd, s_c in zip(rows, q_bds, s_cs):
        kn = row[:, ATT_WIDTH:ATT_WIDTH + KV_WIDTH]
        s_n = jnp.sum(rnd(q_bd) * rnd(kn), axis=-1, keepdims=True) + bias_n
        m = jnp.maximum(jnp.maximum(jnp.max(s_c, axis=-1, keepdims=True), s_n), sink)
        p_c = jnp.exp(s_c - m)
        p_n = jnp.exp(s_n - m)
        den = jnp.sum(p_c, axis=-1, keepdims=True) + p_n + jnp.exp(sink - m)
        prs.append(p_c / den)
        pns.append(p_n / den)
    pvs = [_mm_nt(pr, kv_t(cv_ref, b)) for b, pr in zip(seqs, prs)]
    for b, row, pv, pn in zip(seqs, rows, pvs, pns):
        vn = row[:, ATT_WIDTH + KV_WIDTH:]
        o_full = pv + rnd(pn) * rnd(vn)
        o_sel = jnp.where(low_group, o_full[:, :HEAD_DIM], o_full[:, HEAD_DIM:])
        o_ref[b:b + 1, :] = jnp.concatenate([o_sel[h:h + 1, :] for h in range(ATT_HEADS)], axis=1)


def _attn_sample(att, ck, cv, bucket, rel_bias, sink):
    nseq = att.shape[0]
    smem = pl.BlockSpec(memory_space=pltpu.SMEM)
    cache = pl.BlockSpec((ATT_S_BB, ATT_KV_HEADS, HEAD_DIM, WINDOW), lambda i: (i, 0, 0, 0))
    return pl.pallas_call(
        _attn_sample_kernel,
        grid=(nseq // ATT_S_BB,),
        in_specs=[pl.BlockSpec((ATT_S_BB, ATT_COLS), lambda i: (i, 0)), cache, cache,
                  pl.BlockSpec(bucket.shape, lambda i: (0, 0)), smem, smem],
        out_specs=[pl.BlockSpec((ATT_S_BB, ATT_WIDTH), lambda i: (i, 0)), cache, cache],
        out_shape=[jax.ShapeDtypeStruct((nseq, ATT_WIDTH), F32),
                   jax.ShapeDtypeStruct(ck.shape, F32), jax.ShapeDtypeStruct(cv.shape, F32)],
        scratch_shapes=[pltpu.VMEM((ATT_HEADS, LANES), F32), pltpu.VMEM((ATT_HEADS, LANES), F32)],
        compiler_params=_params("arbitrary"),
        name="attn_sample",
    )(att, ck, cv, bucket, rel_bias, sink)


GDN_TB = 128
GDN_NC = GDN_TB // DN_CHUNK


def _gdn_gates(ba, alog, dtb):
    beta = _sigmoid(ba)
    g = -jnp.exp(alog) * _softplus(ba + dtb)
    return beta, g


def _pair_diag(x, lo):
    xb = x.astype(BF16)
    zero = jnp.zeros_like(xb)
    return jnp.concatenate([jnp.where(lo, xb, zero), jnp.where(lo, zero, xb)], axis=0)


def _gdn_prompt_kernel(qkv_ref, dz_ref, ba_ref, alog_ref, dtb_ref, dnx_ref,
                       hsum_ref, expb_ref, expg_ref, ltri_ref,
                       o_ref, s_out_ref, s_scr):
    i = pl.program_id(0)
    nb = qkv_ref.shape[0]

    @pl.when(i == 0)
    def _():
        s_scr[...] = jnp.zeros(s_scr.shape, F32)

    hsum = hsum_ref[...]
    ri = lax.broadcasted_iota(jnp.int32, (DN_CHUNK, PAIR), 0)
    ci = lax.broadcasted_iota(jnp.int32, (DN_CHUNK, PAIR), 1)
    lo = ci < DN_DK
    cj = jnp.where(lo, ci, ci - DN_DK)
    causal = ri >= cj
    strict = ri > cj
    eye = (ri == cj).astype(F32)

    def sel2(x, m):
        hi = x.astype(BF16)
        lw = (x - hi.astype(F32)).astype(BF16)
        return (jnp.dot(hi, m, preferred_element_type=F32) + jnp.dot(lw, m, preferred_element_type=F32))

    pre = []
    for b in range(nb):
        q = qkv_ref[b, :, :DN_WIDTH]
        k = qkv_ref[b, :, DN_WIDTH:2 * DN_WIDTH]
        v = qkv_ref[b, :, 2 * DN_WIDTH:]
        beta_c, g_c = _gdn_gates(ba_ref[b], alog_ref[...], dtb_ref[...])
        beta = sel2(beta_c, expb_ref[...])
        gam_c = _mm_sel_lhs(ltri_ref[...], g_c)
        gam = _mm_sel_rhs(gam_c, expg_ref[...])
        gam_t = gam_c.T
        kb = k * beta
        egam = jnp.exp(gam)
        pre.append(dict(q=q, k=k, kb=kb, vb=v * beta, qg=q * egam, wr=kb * egam, gam=gam, gam_t=gam_t))

    probs = [(c, b, p) for c in range(GDN_NC) for b in range(nb) for p in range(N_PAIRS)]
    pick = lambda m: jnp.where(lo, m[:DN_DK], m[DN_DK:])
    rows_of = lambda c: slice(c * DN_CHUNK, (c + 1) * DN_CHUNK)
    sl = lambda name, c, b, p: pre[b][name][rows_of(c), p * PAIR:(p + 1) * PAIR]
    raws = []
    for c, b, p in probs:
        k_p = sl("k", c, b, p)
        k_rows = jnp.concatenate([jnp.where(lo, k_p, 0.0), jnp.where(lo, 0.0, k_p)], axis=0)
        raws.append(_mm_nt(jnp.concatenate([sl("kb", c, b, p), sl("q", c, b, p)], axis=0), k_rows))
    pws, ts, qks = [], [], []
    for (c, b, p), raw in zip(probs, raws):
        gcol = sl("gam", c, b, p)
        h0 = DN_HEADS + 2 * p
        gam_t = pre[b]["gam_t"]
        grow = jnp.concatenate([gam_t[h0:h0 + 1, rows_of(c)], gam_t[h0 + 1:h0 + 2, rows_of(c)]], axis=1)
        decay = jnp.exp(jnp.where(causal, gcol - grow, NEG_INF))
        a = jnp.where(strict, raw[:DN_CHUNK] * decay, 0.0)
        qks.append(jnp.where(causal, raw[DN_CHUNK:] * decay, 0.0))
        pws.append(-a)
        ts.append(eye - a)
    pws = [_mm(pw, _pair_diag(pw, lo)) for pw in pws]
    for _ in range(4):
        rs = [_mm(jnp.concatenate([pw, t], axis=0), _pair_diag(pw, lo)) for pw, t in zip(pws, ts)]
        pws = [r[:DN_CHUNK] for r in rs]
        ts = [t + r[DN_CHUNK:] for t, r in zip(ts, rs)]
    rs = [_mm(t, _pair_diag(pw, lo)) for pw, t in zip(pws, ts)]
    ts = [t + r for t, r in zip(ts, rs)]
    sols = [_mm(t, jnp.concatenate([_pair_diag(sl("vb", c, b, p), lo), _pair_diag(sl("wr", c, b, p), lo)],
                                   axis=1)) for (c, b, p), t in zip(probs, ts)]
    qkuws = [_mm(qk, jnp.concatenate([_pair_diag(s[:, :PAIR], lo), _pair_diag(s[:, PAIR:], lo)], axis=1))
             for qk, s in zip(qks, sols)]
    crosses, gls = [], []
    for (c, b, p), s in zip(probs, sols):
        last = (c + 1) * DN_CHUNK - 1
        gam_last = pre[b]["gam"][last:last + 1, p * PAIR:(p + 1) * PAIR]
        kd = sl("k", c, b, p) * jnp.exp(gam_last - sl("gam", c, b, p))
        crosses.append(_mm_tn(kd, s))
        gls.append(jnp.exp(gam_last))
    lhs = [jnp.concatenate([pick(cr[:, PAIR:]), sl("qg", c, b, p) - qkuw[:, PAIR:]], axis=0)
           for (c, b, p), cr, qkuw in zip(probs, crosses, qkuws)]

    o_rows = [[] for _ in range(nb)]
    per_chunk = nb * N_PAIRS
    for c in range(GDN_NC):
        sel = slice(c * per_chunk, (c + 1) * per_chunk)
        s_olds = [s_scr[b, p] for _, b, p in probs[sel]]
        rs = [_mm(l, _pair_diag(s_old, lo)) for l, s_old in zip(lhs[sel], s_olds)]
        o_pairs = [[] for _ in range(nb)]
        for (_, b, p), r, s_old, gl, cr, qkuw in zip(probs[sel], rs, s_olds, gls[sel], crosses[sel], qkuws[sel]):
            s_scr[b, p] = gl * s_old - r[:DN_DK] + pick(cr[:, :PAIR])
            o_pairs[b].append(r[DN_DK:] + qkuw[:, :PAIR])
        for b in range(nb):
            o_rows[b].append(jnp.concatenate(o_pairs[b], axis=1))

    o_all = jnp.concatenate([jnp.concatenate(rows, axis=0) for rows in o_rows], axis=0)
    inv_rms = lax.rsqrt(_head_sums(o_all * o_all, hsum) * (1.0 / DN_DV) + EPS)
    for b in range(nb):
        rows = slice(b * GDN_TB, (b + 1) * GDN_TB)
        o_ref[b] = (o_all[rows] * inv_rms[rows] * dnx_ref[...] * _silu_tanh(dz_ref[b])).astype(o_ref.dtype)

    @pl.when(i == pl.num_programs(0) - 1)
    def _():
        for b in range(nb):
            for p in range(N_PAIRS):
                s_p = s_scr[b, p]
                s_out_ref[b, 2 * p] = s_p[:, :DN_DV]
                s_out_ref[b, 2 * p + 1] = s_p[:, DN_DV:]


def _gdn_consts():
    lane = np.arange(DN_WIDTH)
    pl_lane = np.arange(PAIR)
    hsum = (pl_lane[:, None] // DN_DV == pl_lane[None, :] // DN_DV)
    src = np.arange(LANES)
    expb = (src[:, None] == lane[None, :] // DN_DV)
    expg = (src[:, None] == DN_HEADS + lane[None, :] // DN_DV)
    tok = np.arange(GDN_TB)
    ltri = np.logical_and(tok[:, None] >= tok[None, :],
                          tok[:, None] // DN_CHUNK == tok[None, :] // DN_CHUNK)
    as_bf16 = lambda m: jnp.asarray(m.astype(np.float32), dtype=BF16)
    return as_bf16(hsum), as_bf16(expb), as_bf16(expg), as_bf16(ltri)


def _gdn_prompt(xc, dz, ba, alog, dtb, dnx, batch, seq):
    nt = seq // GDN_TB
    hsum, expb, expg, ltri = _gdn_consts()
    row = lambda n: pl.BlockSpec((batch, GDN_TB, n), lambda i: (0, i, 0))
    full = lambda a: pl.BlockSpec(a.shape, lambda i: (0,) * a.ndim)
    consts = (alog, dtb, dnx, hsum, expb, expg, ltri)
    as3d = lambda a: a.reshape(batch, seq, a.shape[-1])
    o, s = pl.pallas_call(
        _gdn_prompt_kernel,
        grid=(nt,),
        in_specs=[row(CONV_CH), row(DN_WIDTH), row(LANES)] + [full(a) for a in consts],
        out_specs=[row(DN_WIDTH),
                   pl.BlockSpec((batch, DN_HEADS, DN_DK, DN_DV), lambda i: (0, 0, 0, 0))],
        out_shape=[jax.ShapeDtypeStruct((batch, seq, DN_WIDTH), BF16),
                   jax.ShapeDtypeStruct((batch, DN_HEADS, DN_DK, DN_DV), F32)],
        scratch_shapes=[pltpu.VMEM((batch, N_PAIRS, DN_DK, PAIR), F32)],
        compiler_params=_params("arbitrary"),
        name="gdn_prompt",
    )(as3d(xc), as3d(dz), as3d(ba), *consts)
    return o.reshape(batch * seq, DN_WIDTH), s


def _gdn_sample_front_kernel(xc_ref, dz_ref, ba_ref, sc_ref, cw_ref, alog_ref, dtb_ref, hsum_ref,
                             q_ref, k_ref, v_ref, dz_t_ref, gates_ref):
    xc = xc_ref[...]
    y = sc_ref[0] * cw_ref[0:1, :]
    y = y + sc_ref[1] * cw_ref[1:2, :]
    y = y + sc_ref[2] * cw_ref[2:3, :]
    y = _silu(y + xc * cw_ref[3:4, :])
    hsum = hsum_ref[...]
    q = y[:, :DN_WIDTH]
    k = y[:, DN_WIDTH:2 * DN_WIDTH]
    q = q * lax.rsqrt(_mm_sel_rhs(q * q, hsum) + EPS) * (DN_DK ** -0.5)
    k = k * lax.rsqrt(_mm_sel_rhs(k * k, hsum) + EPS)
    beta_c, g_c = _gdn_gates(ba_ref[...], alog_ref[...], dtb_ref[...])
    q_ref[...] = q.T
    k_ref[...] = k.T
    v_ref[...] = y[:, 2 * DN_WIDTH:].T
    dz_t_ref[...] = dz_ref[...].T
    gates_ref[0:LANES, :] = beta_c.T
    gates_ref[LANES:, :] = jnp.exp(g_c).T


def _gdn_sample_step_kernel(q_ref, k_ref, v_ref, dz_ref, gates_ref, dn_ref, s_ref, o_ref, s_out_ref):
    h = pl.program_id(0)
    beta = gates_ref[pl.ds(h, 1), :]
    eg = gates_ref[pl.ds(LANES + DN_HEADS + h, 1), :]
    q, k, v = q_ref[...], k_ref[...], v_ref[...]
    w = (k * beta) * eg
    qg = q * eg
    ws = jnp.zeros(v.shape, F32)
    qs = jnp.zeros(v.shape, F32)
    for dk in range(DN_DK):
        s_dk = s_ref[0, dk]
        ws = ws + w[dk:dk + 1, :] * s_dk
        qs = qs + qg[dk:dk + 1, :] * s_dk
    v_new = v * beta - ws
    qk = jnp.sum(q * k, axis=0, keepdims=True)
    o = qs + qk * v_new
    for dk in range(DN_DK):
        s_out_ref[0, dk] = s_ref[0, dk] * eg + k[dk:dk + 1, :] * v_new
    o = o * lax.rsqrt(jnp.mean(o * o, axis=0, keepdims=True) + EPS) * dn_ref[...]
    o_ref[...] = o * _silu(dz_ref[...])


def _gdn_sample_lanes(xc, dz, ba, sconv_t, state_t, conv_w, alog, dtb, dn):
    nseq = xc.shape[0]
    assert nseq == LANES
    lane = np.arange(DN_WIDTH)
    hsum = jnp.asarray((lane[:, None] // DN_DV == lane[None, :] // DN_DV).astype(np.float32), dtype=BF16)
    full = lambda a: pl.BlockSpec(a.shape, lambda i: (0,) * a.ndim)
    cm = jax.ShapeDtypeStruct((DN_WIDTH, nseq), F32)
    front_in = (xc, dz, ba, sconv_t, conv_w, alog, dtb, hsum)
    q_t, k_t, v_t, dz_t, gates_t = pl.pallas_call(
        _gdn_sample_front_kernel,
        grid=(1,),
        in_specs=[full(a) for a in front_in],
        out_specs=[pl.BlockSpec((DN_WIDTH, nseq), lambda i: (0, 0))] * 4
                  + [pl.BlockSpec((2 * LANES, nseq), lambda i: (0, 0))],
        out_shape=[cm, cm, cm, cm, jax.ShapeDtypeStruct((2 * LANES, nseq), F32)],
        compiler_params=_params("arbitrary"),
        name="gdn_sample_front",
    )(*front_in)
    dn_b = jnp.broadcast_to(dn.reshape(DN_DV, 1), (DN_DV, nseq))
    head = pl.BlockSpec((DN_DK, nseq), lambda h: (h, 0))
    st = pl.BlockSpec((1, DN_DK, DN_DV, nseq), lambda h: (h, 0, 0, 0))
    return pl.pallas_call(
        _gdn_sample_step_kernel,
        grid=(DN_HEADS,),
        in_specs=[head, head, head, head, full(gates_t), full(dn_b), st],
        out_specs=[head, st],
        out_shape=[cm, jax.ShapeDtypeStruct(state_t.shape, F32)],
        compiler_params=_params("parallel"),
        name="gdn_sample_step",
    )(q_t, k_t, v_t, dz_t, gates_t, dn_b, state_t)


def _route(xn, wr):
    logits = jnp.dot(xn, wr, preferred_element_type=F32)
    lane = lax.broadcasted_iota(jnp.int32, logits.shape, 1).astype(F32)
    first_at = lambda hit: jnp.min(jnp.where(hit, lane, float(LANES)), axis=-1, keepdims=True)
    glog = jnp.where(lane < N_GROUPS, logits, NEG_INF)
    gmax = jnp.max(glog, axis=-1, keepdims=True)
    gsel = first_at(glog == gmax)
    pgsel = 1.0 / jnp.sum(jnp.exp(glog - gmax), axis=-1, keepdims=True)
    lo = ROUTER_OFF + gsel * EXPERTS_PER_GROUP
    in_group = jnp.logical_and(lane >= lo, lane < lo + EXPERTS_PER_GROUP)
    elog = jnp.where(in_group, logits, NEG_INF)
    m1 = jnp.max(elog, axis=-1, keepdims=True)
    i1 = first_at(elog == m1)
    z = jnp.sum(jnp.exp(elog - m1), axis=-1, keepdims=True)
    elog2 = jnp.where(lane == i1, NEG_INF, elog)
    m2 = jnp.max(elog2, axis=-1, keepdims=True)
    i2 = first_at(elog2 == m2)
    p1 = 1.0 / z
    p2 = jnp.exp(m2 - m1) / z
    tot = p1 + p2
    return lane, i1, i2, p1 / tot * pgsel, p2 / tot * pgsel


def _outproj_router_kernel(x_ref, oa_ref, od_t_ref, wo_ref, g_ref, wr_ref, h_ref, xn_ref, gate_ref):
    h = (x_ref[...] + _mm(oa_ref[...], wo_ref[:ATT_WIDTH, :])
         + _mm(od_t_ref[...].T, wo_ref[ATT_WIDTH:, :]))
    h_ref[...] = h
    xn = _rmsnorm(h, g_ref[...]).astype(BF16)
    xn_ref[...] = xn
    lane, i1, i2, g1, g2 = _route(xn, wr_ref[...])
    gate_ref[...] = jnp.where(lane == i1, g1, 0.0) + jnp.where(lane == i2, g2, 0.0)


def _outproj_router(x, oa, od_t, wo, g, wr):
    t = x.shape[0]
    tm = t
    row = lambda n: pl.BlockSpec((tm, n), lambda i: (i, 0))
    full = lambda a: pl.BlockSpec(a.shape, lambda i: (0,) * a.ndim)
    return pl.pallas_call(
        _outproj_router_kernel,
        grid=(t // tm,),
        in_specs=[row(D_MODEL), row(ATT_WIDTH), full(od_t), full(wo), full(g), full(wr)],
        out_specs=[row(D_MODEL), row(D_MODEL), row(LANES)],
        out_shape=[jax.ShapeDtypeStruct((t, D_MODEL), F32), jax.ShapeDtypeStruct((t, D_MODEL), BF16),
                   jax.ShapeDtypeStruct((t, LANES), F32)],
        compiler_params=_params("parallel"),
        name="outproj_router",
    )(x, oa, od_t, wo, g, wr)


MOE_TM = 512
POS_TM = 1024
ROUTE_CHUNK = 512
INFO_G1, INFO_G2, INFO_E1, INFO_E2 = 0, 1, 2, 3


def _moe_tiles(t):
    return (2 * t) // MOE_TM + N_EXPERTS


HALF = D_MODEL // 2
U32 = jnp.uint32


def _pack_rows(x):
    bits = lambda v: lax.bitcast_convert_type(v.astype(BF16).astype(F32), U32)
    return bits(x[:, HALF:]) | (bits(x[:, :HALF]) >> 16)


def _unpack_rows(w):
    lo = lax.bitcast_convert_type(w << 16, F32)
    hi = lax.bitcast_convert_type(w & jnp.uint32(0xFFFF0000), F32)
    return lo, hi


def _route_kernel(x_ref, oa_ref, od_ref, wo_ref, g_ref, wr_ref, h_ref, xn_ref, info_ref, cnt_ref, run_scr):
    @pl.when(pl.program_id(0) == 0)
    def _():
        run_scr[...] = jnp.zeros(run_scr.shape, F32)

    seen = jnp.zeros(run_scr.shape, F32)
    for c in range(0, x_ref.shape[0], ROUTE_CHUNK):
        rows = pl.ds(c, ROUTE_CHUNK)
        h = (x_ref[rows, :] + _mm(oa_ref[rows, :], wo_ref[:ATT_WIDTH, :])
             + _mm(od_ref[rows, :], wo_ref[ATT_WIDTH:, :]))
        h_ref[rows, :] = h
        xn = _rmsnorm(h, g_ref[...])
        xn_ref[rows, :] = _pack_rows(xn)
        lane, i1, i2, g1, g2 = _route(xn.astype(BF16), wr_ref[...])
        info = jnp.where(lane == INFO_G1, g1, 0.0) + jnp.where(lane == INFO_G2, g2, 0.0)
        info = info + jnp.where(lane == INFO_E1, i1, 0.0) + jnp.where(lane == INFO_E2, i2, 0.0)
        info_ref[rows, :] = info
        picked = jnp.logical_or(lane == i1, lane == i2).astype(F32)
        seen = seen + jnp.sum(picked, axis=0, keepdims=True)
    run_scr[...] += seen
    cnt_ref[...] = run_scr[...]


def _route_sparse(x, oa, od, wo, g, wr):
    t = x.shape[0]
    tm = WIDE_TM
    row = lambda n: pl.BlockSpec((tm, n), lambda i: (i, 0))
    full = lambda a: pl.BlockSpec(a.shape, lambda i: (0,) * a.ndim)
    return pl.pallas_call(
        _route_kernel,
        grid=(t // tm,),
        in_specs=[row(D_MODEL), row(ATT_WIDTH), row(DN_WIDTH), full(wo), full(g), full(wr)],
        out_specs=[row(D_MODEL), row(HALF), row(LANES), pl.BlockSpec((1, LANES), lambda i: (0, 0))],
        out_shape=[jax.ShapeDtypeStruct((t, D_MODEL), F32), jax.ShapeDtypeStruct((t, HALF), U32),
                   jax.ShapeDtypeStruct((t, LANES), F32), jax.ShapeDtypeStruct((1, LANES), F32)],
        scratch_shapes=[pltpu.VMEM((1, LANES), F32)],
        compiler_params=_params("arbitrary"),
        name="route",
    )(x, oa, od, wo, g, wr)


def _positions_kernel(info_ref, cnt_ref, ltri_ref, utri_ref, pos_ref, run_scr, off_scr):
    info = info_ref[...]
    lane = lax.broadcasted_iota(jnp.int32, info.shape, 1).astype(F32)
    hit1 = lane == info[:, INFO_E1:INFO_E1 + 1]
    hit2 = lane == info[:, INFO_E2:INFO_E2 + 1]
    onehot = jnp.logical_or(hit1, hit2).astype(F32)

    @pl.when(pl.program_id(0) == 0)
    def _():
        ln = lax.broadcasted_iota(jnp.int32, cnt_ref.shape, 1)
        is_expert = jnp.logical_and(ln >= ROUTER_OFF, ln < ROUTER_OFF + N_EXPERTS)
        tiles = jnp.where(is_expert, jnp.maximum(jnp.floor((cnt_ref[...] + (MOE_TM - 1)) * (1.0 / MOE_TM)), 1.0), 0.0)
        off_scr[...] = MOE_TM * jnp.dot(tiles.astype(BF16), utri_ref[...], preferred_element_type=F32)
        run_scr[...] = jnp.zeros(run_scr.shape, F32)

    before = (jnp.dot(ltri_ref[...], onehot.astype(BF16), preferred_element_type=F32)
              + run_scr[...] + off_scr[...])
    pos1 = jnp.sum(jnp.where(hit1, before, 0.0), axis=-1, keepdims=True)
    pos2 = jnp.sum(jnp.where(hit2, before, 0.0), axis=-1, keepdims=True)
    both = jnp.where(lane == 0, pos1, 0.0) + jnp.where(lane == 1, pos2, 0.0)
    pos_ref[...] = both.T.astype(jnp.int32)
    run_scr[...] += jnp.sum(onehot, axis=0, keepdims=True)


def _positions(info, cnt):
    t = info.shape[0]
    tm = min(t, POS_TM)
    tok = np.arange(tm)
    ltri = jnp.asarray((tok[:, None] > tok[None, :]).astype(np.float32), dtype=BF16)
    ln = np.arange(LANES)
    utri = jnp.asarray((ln[:, None] < ln[None, :]).astype(np.float32), dtype=BF16)
    full = lambda a: pl.BlockSpec(a.shape, lambda i: (0,) * a.ndim)
    return pl.pallas_call(
        _positions_kernel,
        grid=(t // tm,),
        in_specs=[pl.BlockSpec((tm, LANES), lambda i: (i, 0)), full(cnt), full(ltri), full(utri)],
        out_specs=pl.BlockSpec((LANES, tm), lambda i: (0, i)),
        out_shape=jax.ShapeDtypeStruct((LANES, t), jnp.int32),
        scratch_shapes=[pltpu.VMEM((1, LANES), F32), pltpu.VMEM((1, LANES), F32)],
        compiler_params=_params("arbitrary"),
        name="positions",
    )(info, cnt, ltri, utri)


def _experts_kernel(te_ref, tv_ref, nt_ref, xs_ref, wg_hbm, wu_hbm, wd_hbm, xn_new_ref, gate_new_ref,
                    ys_ref, moe_new_ref, wg_s, wu_s, wd_s, wg_f, wu_f, wd_f, wsem):
    i = pl.program_id(0)
    used = i < nt_ref[0]
    expert = te_ref[i]

    def fetch(e):
        slot = e % 2
        return [pltpu.make_async_copy(src.at[e], dst.at[slot], wsem.at[slot, j])
                for j, (src, dst) in enumerate(((wg_hbm, wg_f), (wu_hbm, wu_f), (wd_hbm, wd_f)))]

    @pl.when(jnp.logical_or(i == 0, expert != te_ref[jnp.maximum(i - 1, 0)]))
    def _():
        @pl.when(i == 0)
        def _():
            for c in fetch(expert):
                c.start()
        for c in fetch(expert):
            c.wait()

        @pl.when(expert + 1 < N_EXPERTS)
        def _():
            for c in fetch(expert + 1):
                c.start()
        slot = expert % 2
        wg_s[...] = wg_f[slot].astype(BF16)
        wu_s[...] = wu_f[slot].astype(BF16)
        wd_s[...] = wd_f[slot].astype(BF16)
        xn = xn_new_ref[...]
        lane = lax.broadcasted_iota(jnp.int32, gate_new_ref.shape, 1)
        gate = jnp.sum(jnp.where(lane == expert + ROUTER_OFF, gate_new_ref[...], 0.0), axis=-1, keepdims=True)
        hg = jnp.dot(xn, wg_s[...], preferred_element_type=F32)
        hu = jnp.dot(xn, wu_s[...], preferred_element_type=F32)
        hm = _silu(hg) * hu * gate
        y = jnp.dot(hm.astype(BF16), wd_s[...], preferred_element_type=F32)

        @pl.when(i == 0)
        def _():
            moe_new_ref[...] = y

        @pl.when(i > 0)
        def _():
            moe_new_ref[...] += y

    @pl.when(used)
    def _():
        row = lax.broadcasted_iota(jnp.int32, xs_ref.shape, 0)
        x_lo, x_hi = _unpack_rows(jnp.where(row < tv_ref[i], xs_ref[...], jnp.uint32(0)))
        x_lo = x_lo.astype(BF16)
        x_hi = x_hi.astype(BF16)
        up = lambda w_s: (jnp.dot(x_lo, w_s[:HALF, :], preferred_element_type=F32)
                          + jnp.dot(x_hi, w_s[HALF:, :], preferred_element_type=F32))
        hm = (_silu_tanh(up(wg_s)) * up(wu_s)).astype(BF16)
        ys_ref[...] = _pack_rows(jnp.dot(hm, wd_s[...], preferred_element_type=F32))

    @pl.when(jnp.logical_not(used))
    def _():
        ys_ref[...] = jnp.zeros(ys_ref.shape, U32)


def _experts(xs, tile_expert, tile_valid, n_tiles, wg, wu, wd, xn_new, gate_new):
    max_tiles = xs.shape[0] // MOE_TM
    rows = pl.BlockSpec((MOE_TM, HALF), lambda i, te, tv, nt: (i, 0))
    hbm = pl.BlockSpec(memory_space=pl.ANY)
    full = lambda a: pl.BlockSpec(a.shape, lambda i, te, tv, nt: (0,) * a.ndim)
    return pl.pallas_call(
        _experts_kernel,
        grid_spec=pltpu.PrefetchScalarGridSpec(
            num_scalar_prefetch=3, grid=(max_tiles,),
            in_specs=[rows, hbm, hbm, hbm, full(xn_new), full(gate_new)],
            out_specs=[rows, pl.BlockSpec(xn_new.shape, lambda i, te, tv, nt: (0, 0))],
            scratch_shapes=[pltpu.VMEM((D_MODEL, D_EXPERT), BF16), pltpu.VMEM((D_MODEL, D_EXPERT), BF16),
                            pltpu.VMEM((D_EXPERT, D_MODEL), BF16),
                            pltpu.VMEM((2, D_MODEL, D_EXPERT), F32), pltpu.VMEM((2, D_MODEL, D_EXPERT), F32),
                            pltpu.VMEM((2, D_EXPERT, D_MODEL), F32), pltpu.SemaphoreType.DMA((2, 3))]),
        out_shape=[jax.ShapeDtypeStruct(xs.shape, U32), jax.ShapeDtypeStruct(xn_new.shape, F32)],
        compiler_params=_params("arbitrary"),
        name="experts",
    )(tile_expert, tile_valid, n_tiles, xs, wg, wu, wd, xn_new, gate_new)


SC_IDX = 128
SC_ROWS = 64
SC_WORKERS = 32
SC_GATHER_ROWS = 64
SC_GATHER_BUFS = 3


def _sc_mesh():
    return plsc.VectorSubcoreMesh(core_axis_name="c", subcore_axis_name="s")


def _sc_windows(t, fn):
    per_worker = t // SC_WORKERS
    worker = lax.axis_index(("c", "s"))

    @pl.loop(0, per_worker // SC_IDX)
    def _(w):
        fn(worker * per_worker + w * SC_IDX)


def _sc_scatter_rows(xn, pos1, pos2, n_rows):
    t, d = xn.shape
    assert t % (SC_WORKERS * SC_IDX) == 0
    idx_t = pltpu.VMEM((1, SC_IDX), jnp.int32)

    @pl.kernel(out_type=jax.ShapeDtypeStruct((n_rows, d), xn.dtype), mesh=_sc_mesh(),
               scratch_types=[idx_t, idx_t, pltpu.VMEM((SC_ROWS, d), xn.dtype)])
    def scatter(x_hbm, p1_hbm, p2_hbm, o_hbm, i1_v, i2_v, buf):
        def window(base):
            pltpu.sync_copy(p1_hbm.at[:, pl.ds(base, SC_IDX)], i1_v)
            pltpu.sync_copy(p2_hbm.at[:, pl.ds(base, SC_IDX)], i2_v)
            for k in range(SC_IDX // SC_ROWS):
                pltpu.sync_copy(x_hbm.at[pl.ds(base + k * SC_ROWS, SC_ROWS)], buf)
                pltpu.sync_copy(buf, o_hbm.at[i1_v.at[0, pl.ds(k * SC_ROWS, SC_ROWS)]])
                pltpu.sync_copy(buf, o_hbm.at[i2_v.at[0, pl.ds(k * SC_ROWS, SC_ROWS)]])
        _sc_windows(t, window)

    return scatter(xn, pos1.reshape(1, t), pos2.reshape(1, t))


def _sc_gather_rows(ys, pos1, pos2):
    t = pos1.shape[0]
    d = ys.shape[1]
    assert t % (SC_WORKERS * SC_IDX) == 0
    per_worker = t // SC_WORKERS
    idx_t = pltpu.VMEM((1, per_worker), jnp.int32)
    out = jax.ShapeDtypeStruct((t, d), ys.dtype)

    nbuf, rows = SC_GATHER_BUFS, SC_GATHER_ROWS
    buf_t = pltpu.VMEM((rows, d), ys.dtype)

    @pl.kernel(out_type=(out, out), mesh=_sc_mesh(),
               scratch_types=[idx_t, idx_t] + [buf_t] * nbuf
                             + [pltpu.SemaphoreType.DMA((nbuf,)), pltpu.SemaphoreType.DMA((nbuf,))])
    def gather(y_hbm, p1_hbm, p2_hbm, o1_hbm, o2_hbm, i1_v, i2_v, *rest):
        bufs, (gsem, wsem) = rest[:nbuf], rest[nbuf:]
        base = lax.axis_index(("c", "s")) * per_worker
        pltpu.sync_copy(p1_hbm.at[:, pl.ds(base, per_worker)], i1_v)
        pltpu.sync_copy(p2_hbm.at[:, pl.ds(base, per_worker)], i2_v)
        items = [(idx_v, o_hbm, k) for k in range(per_worker // rows)
                 for idx_v, o_hbm in ((i1_v, o1_hbm), (i2_v, o2_hbm))]
        n_items = len(items)

        def read(n):
            idx_v, _, k = items[n]
            return pltpu.make_async_copy(y_hbm.at[idx_v.at[0, pl.ds(k * rows, rows)]],
                                         bufs[n % nbuf], gsem.at[n % nbuf])

        def write(n):
            _, o_hbm, k = items[n]
            return pltpu.make_async_copy(bufs[n % nbuf], o_hbm.at[pl.ds(base + k * rows, rows)],
                                         wsem.at[n % nbuf])

        for n in range(min(nbuf - 1, n_items)):
            read(n).start()
        waited = 0
        for n in range(n_items):
            read(n).wait()
            write(n).start()
            ahead = n + nbuf - 1
            if ahead < n_items:
                if n >= 1:
                    write(n - 1).wait()
                    waited = n
                read(ahead).start()
        for n in range(waited, n_items):
            write(n).wait()

    return gather(ys, pos1.reshape(1, t), pos2.reshape(1, t))


TAIL_CHUNK = 256


def _ple_sparse_kernel(h_ref, info_ref, y1_ref, y2_ref, p_ref, wpp_ref, wpg_ref, gp_ref, gf_ref, y_ref):
    for c in range(0, h_ref.shape[0], TAIL_CHUNK):
        rows = pl.ds(c, TAIL_CHUNK)
        info = info_ref[rows, :]
        g1 = info[:, INFO_G1:INFO_G1 + 1]
        g2 = info[:, INFO_G2:INFO_G2 + 1]
        y1_lo, y1_hi = _unpack_rows(y1_ref[rows, :])
        y2_lo, y2_hi = _unpack_rows(y2_ref[rows, :])
        moe = jnp.concatenate([g1 * y1_lo + g2 * y2_lo, g1 * y1_hi + g2 * y2_hi], axis=1)
        h = h_ref[rows, :] + moe
        hn = _rmsnorm(h, gp_ref[...])
        pp_half = _mm(p_ref[rows, :], wpp_ref[...])
        h = h + pp_half * jnp.tanh(_mm(hn, wpg_ref[...])) + pp_half
        y_ref[rows, :] = _rmsnorm(h, gf_ref[...])


def _ple_sparse(h, info, y1, y2, p, wpp, wpg, gp, gf):
    t = h.shape[0]
    tm = WIDE_TM
    row = lambda n: pl.BlockSpec((tm, n), lambda i: (i, 0))
    full = lambda a: pl.BlockSpec(a.shape, lambda i: (0,) * a.ndim)
    return pl.pallas_call(
        _ple_sparse_kernel,
        grid=(t // tm,),
        in_specs=[row(D_MODEL), row(LANES), row(HALF), row(HALF), row(PLE_DIM),
                  full(wpp), full(wpg), full(gp), full(gf)],
        out_specs=row(D_MODEL),
        out_shape=jax.ShapeDtypeStruct((t, D_MODEL), F32),
        compiler_params=_params("parallel"),
        name="ple_sparse",
    )(h, info, y1, y2, p, wpp, wpg, gp, gf)


def _tile_tables(cnt, max_tiles):
    tiles_e = jnp.maximum((cnt + (MOE_TM - 1)) // MOE_TM, 1)
    ends = jnp.cumsum(tiles_e)
    n_tiles = ends[-1]
    tile = jnp.arange(max_tiles, dtype=jnp.int32)
    idx = jnp.minimum(tile, n_tiles - 1)
    tile_expert = jnp.sum((idx[:, None] >= ends[None, :]).astype(jnp.int32), axis=1)
    mine = tile_expert[:, None] == jnp.arange(N_EXPERTS, dtype=jnp.int32)[None, :]
    of_mine = lambda v: jnp.sum(jnp.where(mine, v[None, :], 0), axis=1)
    valid = jnp.clip(of_mine(cnt) - (idx - of_mine(ends - tiles_e)) * MOE_TM, 0, MOE_TM)
    tile_valid = jnp.where(tile < n_tiles, valid, 0).astype(jnp.int32)
    return tile_expert, tile_valid, n_tiles.reshape(1)


def _ple_final_kernel(h_ref, m_ref, p_ref, wpp_ref, wpg_ref, gp_ref, gf_ref, y_ref):
    h = h_ref[...] + m_ref[...]
    hn = _rmsnorm(h, gp_ref[...])
    h = h + _mm(p_ref[...], wpp_ref[...]) * _sigmoid(_mm(hn, wpg_ref[...]))
    y_ref[...] = _rmsnorm(h, gf_ref[...])


def _ple_final(h, m, p, wpp, wpg, gp, gf):
    t = h.shape[0]
    tm = min(t, 256)
    row = lambda n: pl.BlockSpec((tm, n), lambda i: (i, 0))
    full = lambda a: pl.BlockSpec(a.shape, lambda i: (0,) * a.ndim)
    return pl.pallas_call(
        _ple_final_kernel,
        grid=(t // tm,),
        in_specs=[row(D_MODEL), row(D_MODEL), row(PLE_DIM), full(wpp), full(wpg), full(gp), full(gf)],
        out_specs=row(D_MODEL),
        out_shape=jax.ShapeDtypeStruct((t, D_MODEL), F32),
        compiler_params=_params("parallel"),
        name="ple_final",
    )(h, m, p, wpp, wpg, gp, gf)


def kernel(x_prompt, x_sample, p_prompt, p_sample, cache_k, cache_v, state_conv, state_S, rel_bias, norm_mix, w_in, att_sink, conv_w, dn_A_log, dn_dt_bias, dn_norm, w_out, norm_ffn, w_router_group, w_router_expert, w_gate, w_up, w_down, w_ple_proj, w_ple_gate, norm_ple, norm_final):
    batch, seq, _ = x_prompt.shape
    nseq = x_sample.shape[0]
    assert x_sample.shape[1] == 1 and norm_mix.shape[0] == 1 and cache_k.shape[2] == WINDOW
    assert seq % GDN_TB == 0 and seq % ATT_BLOCK == 0

    wt = jnp.swapaxes(w_in[0], 0, 1)
    o_db = ATT_COLS + CONV_CH
    w_in_re = (wt[:o_db].astype(BF16), wt[o_db + 2 * DN_HEADS:].astype(BF16),
               jnp.pad(wt[o_db:o_db + 2 * DN_HEADS], ((0, LANES - 2 * DN_HEADS), (0, 0))).astype(BF16))
    row = lambda a: a.reshape(1, -1).astype(F32)
    pad_lanes = lambda a, off: jnp.zeros((1, LANES), F32).at[0, off:off + a.shape[0]].set(a)
    alog = pad_lanes(dn_A_log[0], DN_HEADS)
    dtb = pad_lanes(dn_dt_bias[0], DN_HEADS)
    dnx = jnp.tile(dn_norm[0], DN_HEADS).reshape(1, DN_WIDTH)
    w_router = jnp.concatenate(
        [w_router_group[0], w_router_expert[0],
         jnp.zeros((D_MODEL, LANES - N_GROUPS - N_EXPERTS), F32)], axis=1).astype(BF16)
    wo = w_out[0].astype(BF16)
    wg, wu, wd = w_gate[0], w_up[0], w_down[0]
    wpp, wpg = w_ple_proj[0].astype(BF16), w_ple_gate[0].astype(BF16)
    sink = att_sink[0]

    qi = np.arange(ATT_BLOCK)[:, None]
    kj = np.arange(2 * ATT_BLOCK)[None, :]
    bucket_p = jnp.asarray(_t5_bucket_np(qi + ATT_BLOCK - kj))
    bucket_s = jnp.asarray(_t5_bucket_np(WINDOW - np.arange(WINDOW)[None, :]))

    xp = x_prompt.reshape(batch * seq, D_MODEL)
    att_p, qkv_p, dz_p, ba_p, xc_tails = _inproj_conv(xp, row(norm_mix[0]), w_in_re, conv_w[0], seq)
    o_att_p = _attn_prompt(att_p, bucket_p, rel_bias, sink, batch, seq)
    o_dn_p, s_p = _gdn_prompt(qkv_p, dz_p, ba_p, alog, dtb, dnx, batch, seq)
    h1, xn2, info, cnt = _route_sparse(xp, o_att_p, o_dn_p, wo, row(norm_ffn[0]), w_router)
    pos = _positions(info, cnt)
    pos1, pos2 = pos[0], pos[1]
    max_tiles = _moe_tiles(batch * seq)
    cnt_e = cnt[0, ROUTER_OFF:ROUTER_OFF + N_EXPERTS].astype(jnp.int32)
    tile_expert, tile_valid, n_tiles = _tile_tables(cnt_e, max_tiles)
    xs_sorted = _sc_scatter_rows(xn2, pos1, pos2, max_tiles * MOE_TM)

    xs = x_sample.reshape(nseq, D_MODEL)
    att_s, xc_s, dz_s, ba_s = _inproj(xs, row(norm_mix[0]), w_in_re)
    ck_t = jnp.transpose(cache_k[0], (0, 2, 3, 1))
    cv_t = jnp.transpose(cache_v[0], (0, 2, 3, 1))
    o_att_s, ks_t, vs_t = _attn_sample(att_s, ck_t, cv_t, bucket_s, rel_bias, sink)
    sconv_t = jnp.swapaxes(state_conv[0], 0, 1)
    o_dn_s_t, s_s_t = _gdn_sample_lanes(xc_s, dz_s, ba_s, sconv_t, jnp.transpose(state_S[0], (1, 2, 3, 0)),
                                        conv_w[0], alog, dtb, dn_norm[0])
    s_s = jnp.transpose(s_s_t, (3, 0, 1, 2))

    h1_s, xn2_s, gates_s = _outproj_router(xs, o_att_s, o_dn_s_t, wo, row(norm_ffn[0]), w_router)

    ys, moe_s = _experts(xs_sorted, tile_expert, tile_valid, n_tiles, wg, wu, wd, xn2_s, gates_s)
    y1, y2 = _sc_gather_rows(ys, pos1, pos2)
    y_s = _ple_final(h1_s, moe_s, p_sample[0].reshape(nseq, PLE_DIM), wpp, wpg, row(norm_ple[0]),
                     row(norm_final))
    y_p = _ple_sparse(h1, info, y1, y2, p_prompt[0].reshape(batch * seq, PLE_DIM),
                      (0.5 * w_ple_proj[0]).astype(BF16), (0.5 * w_ple_gate[0]).astype(BF16),
                      row(norm_ple[0]), row(norm_final))

    att_p3 = att_p.reshape(batch, seq, ATT_COLS)
    kv_shape = (1, batch, WINDOW, ATT_KV_HEADS, HEAD_DIM)
    k_p = att_p3[:, seq - WINDOW:, ATT_WIDTH:ATT_WIDTH + KV_WIDTH].reshape(kv_shape)
    v_p = att_p3[:, seq - WINDOW:, ATT_WIDTH + KV_WIDTH:].reshape(kv_shape)
    conv_p = xc_tails.reshape(batch, -1, TAIL, CONV_CH)[:, -1, TAIL - (CONV_WIDTH - 1):][None]
    k_s = jnp.transpose(ks_t, (0, 3, 1, 2))[None]
    v_s = jnp.transpose(vs_t, (0, 3, 1, 2))[None]
    conv_s = jnp.concatenate([state_conv[0][:, 1:], xc_s[:, None, :]], axis=1)[None]
    return (y_p.reshape(batch, seq, D_MODEL), y_s.reshape(nseq, 1, D_MODEL),
            k_p, v_p, conv_p, s_p[None], k_s, v_s, conv_s, s_s[None])
```

```python
import functools
import math

import numpy as np
import jax
import jax.numpy as jnp
from jax import lax
from jax.experimental import pallas as pl
from jax.experimental.pallas import tpu as pltpu
from jax.experimental.pallas import tpu_sc as plsc

F32 = jnp.float32
BF16 = jnp.bfloat16

D_MODEL = 1024
ATT_HEADS = 8
ATT_KV_HEADS = 2
HEAD_DIM = 64
GQA = ATT_HEADS // ATT_KV_HEADS
WINDOW = 128
ATT_BLOCK = 128
N_BUCKETS = 32
DN_HEADS = 8
DN_DK = 64
DN_DV = 64
CONV_WIDTH = 4
DN_CHUNK = 64
ATT_WIDTH = ATT_HEADS * HEAD_DIM
KV_WIDTH = ATT_KV_HEADS * HEAD_DIM
DN_WIDTH = DN_HEADS * DN_DV
CONV_CH = 3 * DN_WIDTH
N_GROUPS = 4
EXPERTS_PER_GROUP = 8
N_EXPERTS = N_GROUPS * EXPERTS_PER_GROUP
D_EXPERT = 256
PLE_DIM = 256
EPS = 1e-6
NEG_INF = float("-inf")

ATT_COLS = ATT_WIDTH + 2 * KV_WIDTH
LANES = 128
ROUTER_OFF = N_GROUPS
VMEM_LIMIT = 48 * 1024 * 1024
ROW_TM = 512
WIDE_TM = 1024


def _params(*sem):
    return pltpu.CompilerParams(dimension_semantics=sem, vmem_limit_bytes=VMEM_LIMIT)


def _mm(a, b):
    return jnp.dot(a.astype(BF16), b.astype(BF16), preferred_element_type=F32)


def _mm_nt(a, b):
    return lax.dot_general(a.astype(BF16), b.astype(BF16), (((1,), (1,)), ((), ())),
                           preferred_element_type=F32)


def _mm_tn(a, b):
    return lax.dot_general(a.astype(BF16), b.astype(BF16), (((0,), (0,)), ((), ())),
                           preferred_element_type=F32)


def _split3(x):
    h1 = x.astype(BF16)
    r1 = x - h1.astype(F32)
    h2 = r1.astype(BF16)
    h3 = (r1 - h2.astype(F32)).astype(BF16)
    return h1, h2, h3


def _mm_sel_rhs(x, sel):
    h1, h2, h3 = _split3(x)
    d = lambda h: jnp.dot(h, sel, preferred_element_type=F32)
    return d(h1) + d(h2) + d(h3)


def _mm_sel_lhs(sel, x):
    h1, h2, h3 = _split3(x)
    d = lambda h: jnp.dot(sel, h, preferred_element_type=F32)
    return d(h1) + d(h2) + d(h3)


def _sigmoid(x):
    return 1.0 / (1.0 + jnp.exp(-x))


def _silu(x):
    return x * _sigmoid(x)


def _sigmoid_tanh(x):
    return 0.5 * jnp.tanh(0.5 * x) + 0.5


def _silu_tanh(x):
    return x * _sigmoid_tanh(x)


def _softplus(x):
    return jnp.maximum(x, 0.0) + jnp.log1p(jnp.exp(-jnp.abs(x)))


def _rmsnorm(x, g):
    return x * lax.rsqrt(jnp.mean(x * x, axis=-1, keepdims=True) + EPS) * g


def _t5_bucket_np(dist):
    max_exact = N_BUCKETS // 2
    d = np.maximum(dist, 0)
    ratio = (np.log(np.maximum(d, 1).astype(np.float32) / np.float32(max_exact))
             / np.float32(math.log(WINDOW / max_exact))).astype(np.float32)
    large = np.minimum(max_exact + (ratio * np.float32(N_BUCKETS - max_exact)).astype(np.int32),
                       N_BUCKETS - 1)
    return np.where(d < max_exact, d, large).astype(np.int32)


def _bias_lookup(bucket, rb_ref, h):
    acc = jnp.zeros(bucket.shape, F32)
    for t in range(N_BUCKETS):
        acc = jnp.where(bucket == t, rb_ref[t, h], acc)
    return acc


def _inproj_kernel(x_ref, g_ref, wa_ref, wz_ref, wb_ref, att_ref, xc_ref, dz_ref, ba_ref):
    xn = _rmsnorm(x_ref[...], g_ref[...]).astype(BF16)
    att_ref[...] = _mm_nt(xn, wa_ref[:ATT_COLS, :])
    xc_ref[...] = _mm_nt(xn, wa_ref[ATT_COLS:, :])
    dz_ref[...] = _mm_nt(xn, wz_ref[...])
    ba_ref[...] = _mm_nt(xn, wb_ref[...])


def _inproj(x, g, w):
    t = x.shape[0]
    tm = min(t, ROW_TM)
    row = lambda n: pl.BlockSpec((tm, n), lambda i: (i, 0))
    full = lambda a: pl.BlockSpec(a.shape, lambda i: (0,) * a.ndim)
    return pl.pallas_call(
        _inproj_kernel,
        grid=(t // tm,),
        in_specs=[row(D_MODEL), full(g)] + [full(a) for a in w],
        out_specs=[row(ATT_COLS), row(CONV_CH), row(DN_WIDTH), row(LANES)],
        out_shape=[jax.ShapeDtypeStruct((t, n), F32) for n in (ATT_COLS, CONV_CH, DN_WIDTH, LANES)],
        compiler_params=_params("parallel"),
        name="inproj",
    )(x, g, *w)


TAIL = 8
PAIR = 2 * DN_DK
N_PAIRS = DN_WIDTH // PAIR


def _head_sums(z, pair_ones):
    hi = z.astype(BF16)
    lw = (z - hi.astype(F32)).astype(BF16)
    d = lambda a, p: jnp.dot(a[:, p * PAIR:(p + 1) * PAIR], pair_ones, preferred_element_type=F32)
    return jnp.concatenate([d(hi, p) + d(lw, p) for p in range(N_PAIRS)], axis=1)


W_T_CHUNK = 256


def _inproj_conv_kernel(x_ref, g_ref, wat_ref, wzt_ref, wbt_ref, cw_ref, ones_ref,
                        att_ref, qkv_ref, dz_ref, ba_ref, tail_ref, xp_scr, wa_ref, wz_ref, wb_ref,
                        *, tiles_per_seq):
    tm = x_ref.shape[0]

    @pl.when(pl.program_id(0) == 0)
    def _():
        for src, dst in ((wat_ref, wa_ref), (wzt_ref, wz_ref), (wbt_ref, wb_ref)):
            for c in range(0, src.shape[0], W_T_CHUNK):
                n = min(W_T_CHUNK, src.shape[0] - c)
                dst[:, c:c + n] = src[c:c + n, :].T

    @pl.when(pl.program_id(0) % tiles_per_seq == 0)
    def _():
        xp_scr[...] = jnp.zeros((TAIL, CONV_CH), F32)

    xn = _rmsnorm(x_ref[...], g_ref[...]).astype(BF16)
    xc = jnp.dot(xn, wa_ref[:, ATT_COLS:], preferred_element_type=F32)
    att_ref[...] = jnp.dot(xn, wa_ref[:, :ATT_COLS], preferred_element_type=F32)
    dz_ref[...] = jnp.dot(xn, wz_ref[...], preferred_element_type=F32)
    ba_ref[...] = jnp.dot(xn, wb_ref[...], preferred_element_type=F32)

    head = jnp.concatenate([xp_scr[...], xc[:TAIL, :]], axis=0)

    def shifted(j):
        return jnp.concatenate([head[TAIL - j:2 * TAIL - j, :], pltpu.roll(xc, j, axis=0)[TAIL:, :]], axis=0)

    y = shifted(3) * cw_ref[0:1, :]
    y = y + shifted(2) * cw_ref[1:2, :]
    y = y + shifted(1) * cw_ref[2:3, :]
    y = y + xc * cw_ref[3:4, :]
    tail = xc[tm - TAIL:, :]
    xp_scr[...] = tail
    tail_ref[0] = tail
    y = _silu_tanh(y)
    q = y[:, :DN_WIDTH]
    k = y[:, DN_WIDTH:2 * DN_WIDTH]
    inv_norm = lax.rsqrt(_head_sums(jnp.concatenate([q * q, k * k], axis=0), ones_ref[...]) + EPS)
    qkv_ref[:, :DN_WIDTH] = q * inv_norm[:tm] * (DN_DK ** -0.5)
    qkv_ref[:, DN_WIDTH:2 * DN_WIDTH] = k * inv_norm[tm:]
    qkv_ref[:, 2 * DN_WIDTH:] = y[:, 2 * DN_WIDTH:]


def _pair_ones():
    lane = np.arange(PAIR)
    return jnp.asarray((lane[:, None] // DN_DV == lane[None, :] // DN_DV).astype(np.float32), dtype=BF16)


def _inproj_conv(x, g, w, conv_w, seq):
    t = x.shape[0]
    tm = ROW_TM
    assert seq % tm == 0
    ones = _pair_ones()
    row = lambda n: pl.BlockSpec((tm, n), lambda i: (i, 0))
    full = lambda a: pl.BlockSpec(a.shape, lambda i: (0,) * a.ndim)
    return pl.pallas_call(
        functools.partial(_inproj_conv_kernel, tiles_per_seq=seq // tm),
        grid=(t // tm,),
        in_specs=[row(D_MODEL), full(g)] + [full(a) for a in w] + [full(conv_w), full(ones)],
        out_specs=[row(ATT_COLS), row(CONV_CH), row(DN_WIDTH), row(LANES),
                   pl.BlockSpec((1, TAIL, CONV_CH), lambda i: (i, 0, 0))],
        out_shape=[jax.ShapeDtypeStruct((t, n), F32) for n in (ATT_COLS, CONV_CH, DN_WIDTH, LANES)]
                  + [jax.ShapeDtypeStruct((t // tm, TAIL, CONV_CH), F32)],
        scratch_shapes=[pltpu.VMEM((TAIL, CONV_CH), F32)]
                       + [pltpu.VMEM((D_MODEL, a.shape[0]), BF16) for a in w],
        compiler_params=_params("arbitrary"),
        name="inproj_conv",
    )(x, g, *w, conv_w, ones)


GROUP_ROWS = GQA * ATT_BLOCK


def _attn_prompt_kernel(cur_ref, prev_ref, bucket_ref, rb_ref, sink_ref, o_ref, bias_scr, sink_scr):
    i = pl.program_id(0)
    nseq = cur_ref.shape[0]

    @pl.when(i == 0)
    def _():
        qi = lax.broadcasted_iota(jnp.int32, (ATT_BLOCK, 2 * ATT_BLOCK), 0)
        kj = lax.broadcasted_iota(jnp.int32, (ATT_BLOCK, 2 * ATT_BLOCK), 1)
        dist = qi + ATT_BLOCK - kj
        band = jnp.logical_and(dist >= 0, dist < WINDOW)
        bucket = bucket_ref[...]
        hrow = lax.broadcasted_iota(jnp.int32, (GROUP_ROWS, 1), 0) // ATT_BLOCK
        for g in range(ATT_KV_HEADS):
            sink_col = jnp.zeros((GROUP_ROWS, 1), F32)
            for hh in range(GQA):
                h = g * GQA + hh
                bias = jnp.where(band, _bias_lookup(bucket, rb_ref, h), NEG_INF)
                bias_scr[0, g, hh * ATT_BLOCK:(hh + 1) * ATT_BLOCK, :] = bias
                bias_scr[1, g, hh * ATT_BLOCK:(hh + 1) * ATT_BLOCK, :] = jnp.where(kj >= ATT_BLOCK, bias, NEG_INF)
                sink_col = jnp.where(hrow == hh, sink_ref[h], sink_col)
            sink_scr[g] = sink_col

    first = (i == 0).astype(jnp.int32)
    probs = [(b, g) for b in range(nseq) for g in range(ATT_KV_HEADS)]
    scores = []
    for b, g in probs:
        cur = cur_ref[b]
        prev = prev_ref[b]
        q = jnp.concatenate([cur[:, (g * GQA + hh) * HEAD_DIM:(g * GQA + hh + 1) * HEAD_DIM]
                             for hh in range(GQA)], axis=0) * (HEAD_DIM ** -0.5)
        kcol = slice(ATT_WIDTH + g * HEAD_DIM, ATT_WIDTH + (g + 1) * HEAD_DIM)
        k2 = jnp.concatenate([prev[:, kcol], cur[:, kcol]], axis=0)
        scores.append(_mm_nt(q, k2) + bias_scr[first, g])
    probs_p, dens = [], []
    for (b, g), s in zip(probs, scores):
        sink = sink_scr[g]
        m = jnp.maximum(jnp.max(s, axis=-1, keepdims=True), sink)
        p = jnp.exp(s - m)
        dens.append(jnp.sum(p, axis=-1, keepdims=True) + jnp.exp(sink - m))
        probs_p.append(p.astype(BF16))
    outs = {}
    for (b, g), p, den in zip(probs, probs_p, dens):
        vcol = slice(ATT_WIDTH + KV_WIDTH + g * HEAD_DIM, ATT_WIDTH + KV_WIDTH + (g + 1) * HEAD_DIM)
        v2 = jnp.concatenate([prev_ref[b][:, vcol], cur_ref[b][:, vcol]], axis=0)
        outs[b, g] = _mm(p, v2) / den
    for b in range(nseq):
        o_ref[b] = jnp.concatenate([outs[b, g][hh * ATT_BLOCK:(hh + 1) * ATT_BLOCK, :]
                                    for g in range(ATT_KV_HEADS) for hh in range(GQA)],
                                   axis=1).astype(o_ref.dtype)


def _attn_prompt(att, bucket, rel_bias, sink, batch, seq):
    nb = seq // ATT_BLOCK
    smem = pl.BlockSpec(memory_space=pltpu.SMEM)
    att3 = att.reshape(batch, seq, ATT_COLS)
    out = pl.pallas_call(
        _attn_prompt_kernel,
        grid=(nb,),
        in_specs=[
            pl.BlockSpec((batch, ATT_BLOCK, ATT_COLS), lambda i: (0, i, 0)),
            pl.BlockSpec((batch, ATT_BLOCK, ATT_COLS), lambda i: (0, jnp.maximum(i - 1, 0), 0)),
            pl.BlockSpec(bucket.shape, lambda i: (0, 0)),
            smem, smem,
        ],
        out_specs=pl.BlockSpec((batch, ATT_BLOCK, ATT_WIDTH), lambda i: (0, i, 0)),
        out_shape=jax.ShapeDtypeStruct((batch, seq, ATT_WIDTH), BF16),
        scratch_shapes=[pltpu.VMEM((2, ATT_KV_HEADS, GROUP_ROWS, 2 * ATT_BLOCK), F32),
                        pltpu.VMEM((ATT_KV_HEADS, GROUP_ROWS, 1), F32)],
        compiler_params=_params("arbitrary"),
        name="attn_prompt",
    )(att3, att3, bucket, rel_bias, sink)
    return out.reshape(batch * seq, ATT_WIDTH)


ATT_S_BB = 8


def _attn_sample_kernel(att_ref, ck_ref, cv_ref, bucket_ref, rb_ref, sink_ref, o_ref, ks_ref, vs_ref,
                        bias_scr, col_scr):
    hrow = lax.broadcasted_iota(jnp.int32, (ATT_HEADS, LANES), 0)
    lane = lax.broadcasted_iota(jnp.int32, (ATT_HEADS, LANES), 1)

    last = (lax.broadcasted_iota(jnp.int32, (3, WINDOW), 1) == WINDOW - 1).astype(BF16)
    is_last = lax.broadcasted_iota(jnp.int32, (KV_WIDTH, WINDOW), 1) == WINDOW - 1

    def shifted(cache_t, new_row):
        pieces = jnp.concatenate([p.astype(F32) for p in _split3(new_row)], axis=0).astype(BF16)
        col = lax.dot_general(pieces, last, (((0,), (0,)), ((), ())), preferred_element_type=F32)
        out = jnp.where(is_last, col, pltpu.roll(cache_t, WINDOW - 1, axis=1))
        return out.reshape(ATT_KV_HEADS, HEAD_DIM, WINDOW)

    for b in range(ATT_S_BB):
        row = att_ref[b:b + 1, :]
        ks_ref[b] = shifted(ck_ref[b].reshape(KV_WIDTH, WINDOW), row[:, ATT_WIDTH:ATT_WIDTH + KV_WIDTH])
        vs_ref[b] = shifted(cv_ref[b].reshape(KV_WIDTH, WINDOW), row[:, ATT_WIDTH + KV_WIDTH:])

    @pl.when(pl.program_id(0) == 0)
    def _():
        bucket = jnp.broadcast_to(bucket_ref[...], (ATT_HEADS, LANES))
        bias = jnp.zeros((ATT_HEADS, LANES), F32)
        cols = jnp.zeros((ATT_HEADS, LANES), F32)
        for h in range(ATT_HEADS):
            bias = jnp.where(hrow == h, _bias_lookup(bucket, rb_ref, h), bias)
            cols = jnp.where(jnp.logical_and(hrow == h, lane == 0), sink_ref[h], cols)
            cols = jnp.where(jnp.logical_and(hrow == h, lane == 1), rb_ref[0, h], cols)
        bias_scr[...] = jnp.where(lane >= 1, bias, NEG_INF)
        col_scr[...] = cols

    bias_c = bias_scr[...]
    sink = col_scr[:, 0:1]
    bias_n = col_scr[:, 1:2]
    same_group = (hrow // GQA) == (lane // HEAD_DIM)
    low_group = lax.broadcasted_iota(jnp.int32, (ATT_HEADS, HEAD_DIM), 0) < GQA
    rnd = lambda a: a.astype(BF16).astype(F32)
    seqs = range(ATT_S_BB)
    rows = [att_ref[b:b + 1, :] for b in seqs]
    q_bds = []
    for row in rows:
        q = row[:, :ATT_WIDTH] * (HEAD_DIM ** -0.5)
        qh = jnp.concatenate([q[:, h * HEAD_DIM:(h + 1) * HEAD_DIM] for h in range(ATT_HEADS)], axis=0)
        q_bds.append(jnp.where(same_group, jnp.concatenate([qh, qh], axis=1), 0.0))
    kv_t = lambda ref, b: ref[b].reshape(KV_WIDTH, WINDOW)
    s_cs = [_mm(q_bd, kv_t(ck_ref, b)) + bias_c for b, q_bd in zip(seqs, q_bds)]
    prs, pns = [], []
    for row, q_bd, s_c in zip(rows, q_bds, s_cs):
        kn = row[:, ATT_WIDTH:ATT_WIDTH + KV_WIDTH]
        s_n = jnp.sum(rnd(q_bd) * rnd(kn), axis=-1, keepdims=True) + bias_n
        m = jnp.maximum(jnp.maximum(jnp.max(s_c, axis=-1, keepdims=True), s_n), sink)
        p_c = jnp.exp(s_c - m)
        p_n = jnp.exp(s_n - m)
        den = jnp.sum(p_c, axis=-1, keepdims=True) + p_n + jnp.exp(sink - m)
        prs.append(p_c / den)
        pns.append(p_n / den)
    pvs = [_mm_nt(pr, kv_t(cv_ref, b)) for b, pr in zip(seqs, prs)]
    for b, row, pv, pn in zip(seqs, rows, pvs, pns):
        vn = row[:, ATT_WIDTH + KV_WIDTH:]
        o_full = pv + rnd(pn) * rnd(vn)
        o_sel = jnp.where(low_group, o_full[:, :HEAD_DIM], o_full[:, HEAD_DIM:])
        o_ref[b:b + 1, :] = jnp.concatenate([o_sel[h:h + 1, :] for h in range(ATT_HEADS)], axis=1)


def _attn_sample(att, ck, cv, bucket, rel_bias, sink):
    nseq = att.shape[0]
    smem = pl.BlockSpec(memory_space=pltpu.SMEM)
    cache = pl.BlockSpec((ATT_S_BB, ATT_KV_HEADS, HEAD_DIM, WINDOW), lambda i: (i, 0, 0, 0))
    return pl.pallas_call(
        _attn_sample_kernel,
        grid=(nseq // ATT_S_BB,),
        in_specs=[pl.BlockSpec((ATT_S_BB, ATT_COLS), lambda i: (i, 0)), cache, cache,
                  pl.BlockSpec(bucket.shape, lambda i: (0, 0)), smem, smem],
        out_specs=[pl.BlockSpec((ATT_S_BB, ATT_WIDTH), lambda i: (i, 0)), cache, cache],
        out_shape=[jax.ShapeDtypeStruct((nseq, ATT_WIDTH), F32),
                   jax.ShapeDtypeStruct(ck.shape, F32), jax.ShapeDtypeStruct(cv.shape, F32)],
        scratch_shapes=[pltpu.VMEM((ATT_HEADS, LANES), F32), pltpu.VMEM((ATT_HEADS, LANES), F32)],
        compiler_params=_params("arbitrary"),
        name="attn_sample",
    )(att, ck, cv, bucket, rel_bias, sink)


GDN_TB = 128
GDN_NC = GDN_TB // DN_CHUNK


def _gdn_gates(ba, alog, dtb):
    beta = _sigmoid(ba)
    g = -jnp.exp(alog) * _softplus(ba + dtb)
    return beta, g


def _pair_diag(x, lo):
    xb = x.astype(BF16)
    zero = jnp.zeros_like(xb)
    return jnp.concatenate([jnp.where(lo, xb, zero), jnp.where(lo, zero, xb)], axis=0)


def _gdn_prompt_kernel(qkv_ref, dz_ref, ba_ref, alog_ref, dtb_ref, dnx_ref,
                       hsum_ref, expb_ref, expg_ref, ltri_ref,
                       o_ref, s_out_ref, s_scr):
    i = pl.program_id(0)
    nb = qkv_ref.shape[0]

    @pl.when(i == 0)
    def _():
        s_scr[...] = jnp.zeros(s_scr.shape, F32)

    hsum = hsum_ref[...]
    ri = lax.broadcasted_iota(jnp.int32, (DN_CHUNK, PAIR), 0)
    ci = lax.broadcasted_iota(jnp.int32, (DN_CHUNK, PAIR), 1)
    lo = ci < DN_DK
    cj = jnp.where(lo, ci, ci - DN_DK)
    causal = ri >= cj
    strict = ri > cj
    eye = (ri == cj).astype(F32)

    def sel2(x, m):
        hi = x.astype(BF16)
        lw = (x - hi.astype(F32)).astype(BF16)
        return (jnp.dot(hi, m, preferred_element_type=F32) + jnp.dot(lw, m, preferred_element_type=F32))

    pre = []
    for b in range(nb):
        q = qkv_ref[b, :, :DN_WIDTH]
        k = qkv_ref[b, :, DN_WIDTH:2 * DN_WIDTH]
        v = qkv_ref[b, :, 2 * DN_WIDTH:]
        beta_c, g_c = _gdn_gates(ba_ref[b], alog_ref[...], dtb_ref[...])
        beta = sel2(beta_c, expb_ref[...])
        gam_c = _mm_sel_lhs(ltri_ref[...], g_c)
        gam = _mm_sel_rhs(gam_c, expg_ref[...])
        gam_t = gam_c.T
        kb = k * beta
        egam = jnp.exp(gam)
        pre.append(dict(q=q, k=k, kb=kb, vb=v * beta, qg=q * egam, wr=kb * egam, gam=gam, gam_t=gam_t))

    probs = [(c, b, p) for c in range(GDN_NC) for b in range(nb) for p in range(N_PAIRS)]
    pick = lambda m: jnp.where(lo, m[:DN_DK], m[DN_DK:])
    rows_of = lambda c: slice(c * DN_CHUNK, (c + 1) * DN_CHUNK)
    sl = lambda name, c, b, p: pre[b][name][rows_of(c), p * PAIR:(p + 1) * PAIR]
    raws = []
    for c, b, p in probs:
        k_p = sl("k", c, b, p)
        k_rows = jnp.concatenate([jnp.where(lo, k_p, 0.0), jnp.where(lo, 0.0, k_p)], axis=0)
        raws.append(_mm_nt(jnp.concatenate([sl("kb", c, b, p), sl("q", c, b, p)], axis=0), k_rows))
    pws, ts, qks = [], [], []
    for (c, b, p), raw in zip(probs, raws):
        gcol = sl("gam", c, b, p)
        h0 = DN_HEADS + 2 * p
        gam_t = pre[b]["gam_t"]
        grow = jnp.concatenate([gam_t[h0:h0 + 1, rows_of(c)], gam_t[h0 + 1:h0 + 2, rows_of(c)]], axis=1)
        decay = jnp.exp(jnp.where(causal, gcol - grow, NEG_INF))
        a = jnp.where(strict, raw[:DN_CHUNK] * decay, 0.0)
        qks.append(jnp.where(causal, raw[DN_CHUNK:] * decay, 0.0))
        pws.append(-a)
        ts.append(eye - a)
    pws = [_mm(pw, _pair_diag(pw, lo)) for pw in pws]
    for _ in range(4):
        rs = [_mm(jnp.concatenate([pw, t], axis=0), _pair_diag(pw, lo)) for pw, t in zip(pws, ts)]
        pws = [r[:DN_CHUNK] for r in rs]
        ts = [t + r[DN_CHUNK:] for t, r in zip(ts, rs)]
    rs = [_mm(t, _pair_diag(pw, lo)) for pw, t in zip(pws, ts)]
    ts = [t + r for t, r in zip(ts, rs)]
    sols = [_mm(t, jnp.concatenate([_pair_diag(sl("vb", c, b, p), lo), _pair_diag(sl("wr", c, b, p), lo)],
                                   axis=1)) for (c, b, p), t in zip(probs, ts)]
    qkuws = [_mm(qk, jnp.concatenate([_pair_diag(s[:, :PAIR], lo), _pair_diag(s[:, PAIR:], lo)], axis=1))
             for qk, s in zip(qks, sols)]
    crosses, gls = [], []
    for (c, b, p), s in zip(probs, sols):
        last = (c + 1) * DN_CHUNK - 1
        gam_last = pre[b]["gam"][last:last + 1, p * PAIR:(p + 1) * PAIR]
        kd = sl("k", c, b, p) * jnp.exp(gam_last - sl("gam", c, b, p))
        crosses.append(_mm_tn(kd, s))
        gls.append(jnp.exp(gam_last))
    lhs = [jnp.concatenate([pick(cr[:, PAIR:]), sl("qg", c, b, p) - qkuw[:, PAIR:]], axis=0)
           for (c, b, p), cr, qkuw in zip(probs, crosses, qkuws)]

    o_rows = [[] for _ in range(nb)]
    per_chunk = nb * N_PAIRS
    for c in range(GDN_NC):
        sel = slice(c * per_chunk, (c + 1) * per_chunk)
        s_olds = [s_scr[b, p] for _, b, p in probs[sel]]
        rs = [_mm(l, _pair_diag(s_old, lo)) for l, s_old in zip(lhs[sel], s_olds)]
        o_pairs = [[] for _ in range(nb)]
        for (_, b, p), r, s_old, gl, cr, qkuw in zip(probs[sel], rs, s_olds, gls[sel], crosses[sel], qkuws[sel]):
            s_scr[b, p] = gl * s_old - r[:DN_DK] + pick(cr[:, :PAIR])
            o_pairs[b].append(r[DN_DK:] + qkuw[:, :PAIR])
        for b in range(nb):
            o_rows[b].append(jnp.concatenate(o_pairs[b], axis=1))

    o_all = jnp.concatenate([jnp.concatenate(rows, axis=0) for rows in o_rows], axis=0)
    inv_rms = lax.rsqrt(_head_sums(o_all * o_all, hsum) * (1.0 / DN_DV) + EPS)
    for b in range(nb):
        rows = slice(b * GDN_TB, (b + 1) * GDN_TB)
        o_ref[b] = (o_all[rows] * inv_rms[rows] * dnx_ref[...] * _silu_tanh(dz_ref[b])).astype(o_ref.dtype)

    @pl.when(i == pl.num_programs(0) - 1)
    def _():
        for b in range(nb):
            for p in range(N_PAIRS):
                s_p = s_scr[b, p]
                s_out_ref[b, 2 * p] = s_p[:, :DN_DV]
                s_out_ref[b, 2 * p + 1] = s_p[:, DN_DV:]


def _gdn_consts():
    lane = np.arange(DN_WIDTH)
    pl_lane = np.arange(PAIR)
    hsum = (pl_lane[:, None] // DN_DV == pl_lane[None, :] // DN_DV)
    src = np.arange(LANES)
    expb = (src[:, None] == lane[None, :] // DN_DV)
    expg = (src[:, None] == DN_HEADS + lane[None, :] // DN_DV)
    tok = np.arange(GDN_TB)
    ltri = np.logical_and(tok[:, None] >= tok[None, :],
                          tok[:, None] // DN_CHUNK == tok[None, :] // DN_CHUNK)
    as_bf16 = lambda m: jnp.asarray(m.astype(np.float32), dtype=BF16)
    return as_bf16(hsum), as_bf16(expb), as_bf16(expg), as_bf16(ltri)


def _gdn_prompt(xc, dz, ba, alog, dtb, dnx, batch, seq):
    nt = seq // GDN_TB
    hsum, expb, expg, ltri = _gdn_consts()
    row = lambda n: pl.BlockSpec((batch, GDN_TB, n), lambda i: (0, i, 0))
    full = lambda a: pl.BlockSpec(a.shape, lambda i: (0,) * a.ndim)
    consts = (alog, dtb, dnx, hsum, expb, expg, ltri)
    as3d = lambda a: a.reshape(batch, seq, a.shape[-1])
    o, s = pl.pallas_call(
        _gdn_prompt_kernel,
        grid=(nt,),
        in_specs=[row(CONV_CH), row(DN_WIDTH), row(LANES)] + [full(a) for a in consts],
        out_specs=[row(DN_WIDTH),
                   pl.BlockSpec((batch, DN_HEADS, DN_DK, DN_DV), lambda i: (0, 0, 0, 0))],
        out_shape=[jax.ShapeDtypeStruct((batch, seq, DN_WIDTH), BF16),
                   jax.ShapeDtypeStruct((batch, DN_HEADS, DN_DK, DN_DV), F32)],
        scratch_shapes=[pltpu.VMEM((batch, N_PAIRS, DN_DK, PAIR), F32)],
        compiler_params=_params("arbitrary"),
        name="gdn_prompt",
    )(as3d(xc), as3d(dz), as3d(ba), *consts)
    return o.reshape(batch * seq, DN_WIDTH), s


def _gdn_sample_front_kernel(xc_ref, dz_ref, ba_ref, sc_ref, cw_ref, alog_ref, dtb_ref, hsum_ref,
                             q_ref, k_ref, v_ref, dz_t_ref, gates_ref):
    xc = xc_ref[...]
    y = sc_ref[0] * cw_ref[0:1, :]
    y = y + sc_ref[1] * cw_ref[1:2, :]
    y = y + sc_ref[2] * cw_ref[2:3, :]
    y = _silu(y + xc * cw_ref[3:4, :])
    hsum = hsum_ref[...]
    q = y[:, :DN_WIDTH]
    k = y[:, DN_WIDTH:2 * DN_WIDTH]
    q = q * lax.rsqrt(_mm_sel_rhs(q * q, hsum) + EPS) * (DN_DK ** -0.5)
    k = k * lax.rsqrt(_mm_sel_rhs(k * k, hsum) + EPS)
    beta_c, g_c = _gdn_gates(ba_ref[...], alog_ref[...], dtb_ref[...])
    q_ref[...] = q.T
    k_ref[...] = k.T
    v_ref[...] = y[:, 2 * DN_WIDTH:].T
    dz_t_ref[...] = dz_ref[...].T
    gates_ref[0:LANES, :] = beta_c.T
    gates_ref[LANES:, :] = jnp.exp(g_c).T


def _gdn_sample_step_kernel(q_ref, k_ref, v_ref, dz_ref, gates_ref, dn_ref, s_ref, o_ref, s_out_ref):
    h = pl.program_id(0)
    beta = gates_ref[pl.ds(h, 1), :]
    eg = gates_ref[pl.ds(LANES + DN_HEADS + h, 1), :]
    q, k, v = q_ref[...], k_ref[...], v_ref[...]
    w = (k * beta) * eg
    qg = q * eg
    ws = jnp.zeros(v.shape, F32)
    qs = jnp.zeros(v.shape, F32)
    for dk in range(DN_DK):
        s_dk = s_ref[0, dk]
        ws = ws + w[dk:dk + 1, :] * s_dk
        qs = qs + qg[dk:dk + 1, :] * s_dk
    v_new = v * beta - ws
    qk = jnp.sum(q * k, axis=0, keepdims=True)
    o = qs + qk * v_new
    for dk in range(DN_DK):
        s_out_ref[0, dk] = s_ref[0, dk] * eg + k[dk:dk + 1, :] * v_new
    o = o * lax.rsqrt(jnp.mean(o * o, axis=0, keepdims=True) + EPS) * dn_ref[...]
    o_ref[...] = o * _silu(dz_ref[...])


def _gdn_sample_lanes(xc, dz, ba, sconv_t, state_t, conv_w, alog, dtb, dn):
    nseq = xc.shape[0]
    assert nseq == LANES
    lane = np.arange(DN_WIDTH)
    hsum = jnp.asarray((lane[:, None] // DN_DV == lane[None, :] // DN_DV).astype(np.float32), dtype=BF16)
    full = lambda a: pl.BlockSpec(a.shape, lambda i: (0,) * a.ndim)
    cm = jax.ShapeDtypeStruct((DN_WIDTH, nseq), F32)
    front_in = (xc, dz, ba, sconv_t, conv_w, alog, dtb, hsum)
    q_t, k_t, v_t, dz_t, gates_t = pl.pallas_call(
        _gdn_sample_front_kernel,
        grid=(1,),
        in_specs=[full(a) for a in front_in],
        out_specs=[pl.BlockSpec((DN_WIDTH, nseq), lambda i: (0, 0))] * 4
                  + [pl.BlockSpec((2 * LANES, nseq), lambda i: (0, 0))],
        out_shape=[cm, cm, cm, cm, jax.ShapeDtypeStruct((2 * LANES, nseq), F32)],
        compiler_params=_params("arbitrary"),
        name="gdn_sample_front",
    )(*front_in)
    dn_b = jnp.broadcast_to(dn.reshape(DN_DV, 1), (DN_DV, nseq))
    head = pl.BlockSpec((DN_DK, nseq), lambda h: (h, 0))
    st = pl.BlockSpec((1, DN_DK, DN_DV, nseq), lambda h: (h, 0, 0, 0))
    return pl.pallas_call(
        _gdn_sample_step_kernel,
        grid=(DN_HEADS,),
        in_specs=[head, head, head, head, full(gates_t), full(dn_b), st],
        out_specs=[head, st],
        out_shape=[cm, jax.ShapeDtypeStruct(state_t.shape, F32)],
        compiler_params=_params("parallel"),
        name="gdn_sample_step",
    )(q_t, k_t, v_t, dz_t, gates_t, dn_b, state_t)


def _route(xn, wr):
    logits = jnp.dot(xn, wr, preferred_element_type=F32)
    lane = lax.broadcasted_iota(jnp.int32, logits.shape, 1).astype(F32)
    first_at = lambda hit: jnp.min(jnp.where(hit, lane, float(LANES)), axis=-1, keepdims=True)
    glog = jnp.where(lane < N_GROUPS, logits, NEG_INF)
    gmax = jnp.max(glog, axis=-1, keepdims=True)
    gsel = first_at(glog == gmax)
    pgsel = 1.0 / jnp.sum(jnp.exp(glog - gmax), axis=-1, keepdims=True)
    lo = ROUTER_OFF + gsel * EXPERTS_PER_GROUP
    in_group = jnp.logical_and(lane >= lo, lane < lo + EXPERTS_PER_GROUP)
    elog = jnp.where(in_group, logits, NEG_INF)
    m1 = jnp.max(elog, axis=-1, keepdims=True)
    i1 = first_at(elog == m1)
    z = jnp.sum(jnp.exp(elog - m1), axis=-1, keepdims=True)
    elog2 = jnp.where(lane == i1, NEG_INF, elog)
    m2 = jnp.max(elog2, axis=-1, keepdims=True)
    i2 = first_at(elog2 == m2)
    p1 = 1.0 / z
    p2 = jnp.exp(m2 - m1) / z
    tot = p1 + p2
    return lane, i1, i2, p1 / tot * pgsel, p2 / tot * pgsel


def _outproj_router_kernel(x_ref, oa_ref, od_t_ref, wo_ref, g_ref, wr_ref, h_ref, xn_ref, gate_ref):
    h = (x_ref[...] + _mm(oa_ref[...], wo_ref[:ATT_WIDTH, :])
         + _mm(od_t_ref[...].T, wo_ref[ATT_WIDTH:, :]))
    h_ref[...] = h
    xn = _rmsnorm(h, g_ref[...]).astype(BF16)
    xn_ref[...] = xn
    lane, i1, i2, g1, g2 = _route(xn, wr_ref[...])
    gate_ref[...] = jnp.where(lane == i1, g1, 0.0) + jnp.where(lane == i2, g2, 0.0)


def _outproj_router(x, oa, od_t, wo, g, wr):
    t = x.shape[0]
    tm = t
    row = lambda n: pl.BlockSpec((tm, n), lambda i: (i, 0))
    full = lambda a: pl.BlockSpec(a.shape, lambda i: (0,) * a.ndim)
    return pl.pallas_call(
        _outproj_router_kernel,
        grid=(t // tm,),
        in_specs=[row(D_MODEL), row(ATT_WIDTH), full(od_t), full(wo), full(g), full(wr)],
        out_specs=[row(D_MODEL), row(D_MODEL), row(LANES)],
        out_shape=[jax.ShapeDtypeStruct((t, D_MODEL), F32), jax.ShapeDtypeStruct((t, D_MODEL), BF16),
                   jax.ShapeDtypeStruct((t, LANES), F32)],
        compiler_params=_params("parallel"),
        name="outproj_router",
    )(x, oa, od_t, wo, g, wr)


MOE_TM = 512
POS_TM = 1024
ROUTE_CHUNK = 512
INFO_G1, INFO_G2, INFO_E1, INFO_E2 = 0, 1, 2, 3


def _moe_tiles(t):
    return (2 * t) // MOE_TM + N_EXPERTS


HALF = D_MODEL // 2
U32 = jnp.uint32


def _pack_rows(x):
    bits = lambda v: lax.bitcast_convert_type(v.astype(BF16).astype(F32), U32)
    return bits(x[:, HALF:]) | (bits(x[:, :HALF]) >> 16)


def _unpack_rows(w):
    lo = lax.bitcast_convert_type(w << 16, F32)
    hi = lax.bitcast_convert_type(w & jnp.uint32(0xFFFF0000), F32)
    return lo, hi


def _route_kernel(x_ref, oa_ref, od_ref, wo_ref, g_ref, wr_ref, h_ref, xn_ref, info_ref, cnt_ref, run_scr):
    @pl.when(pl.program_id(0) == 0)
    def _():
        run_scr[...] = jnp.zeros(run_scr.shape, F32)

    seen = jnp.zeros(run_scr.shape, F32)
    for c in range(0, x_ref.shape[0], ROUTE_CHUNK):
        rows = pl.ds(c, ROUTE_CHUNK)
        h = (x_ref[rows, :] + _mm(oa_ref[rows, :], wo_ref[:ATT_WIDTH, :])
             + _mm(od_ref[rows, :], wo_ref[ATT_WIDTH:, :]))
        h_ref[rows, :] = h
        xn = _rmsnorm(h, g_ref[...])
        xn_ref[rows, :] = _pack_rows(xn)
        lane, i1, i2, g1, g2 = _route(xn.astype(BF16), wr_ref[...])
        info = jnp.where(lane == INFO_G1, g1, 0.0) + jnp.where(lane == INFO_G2, g2, 0.0)
        info = info + jnp.where(lane == INFO_E1, i1, 0.0) + jnp.where(lane == INFO_E2, i2, 0.0)
        info_ref[rows, :] = info
        picked = jnp.logical_or(lane == i1, lane == i2).astype(F32)
        seen = seen + jnp.sum(picked, axis=0, keepdims=True)
    run_scr[...] += seen
    cnt_ref[...] = run_scr[...]


def _route_sparse(x, oa, od, wo, g, wr):
    t = x.shape[0]
    tm = WIDE_TM
    row = lambda n: pl.BlockSpec((tm, n), lambda i: (i, 0))
    full = lambda a: pl.BlockSpec(a.shape, lambda i: (0,) * a.ndim)
    return pl.pallas_call(
        _route_kernel,
        grid=(t // tm,),
        in_specs=[row(D_MODEL), row(ATT_WIDTH), row(DN_WIDTH), full(wo), full(g), full(wr)],
        out_specs=[row(D_MODEL), row(HALF), row(LANES), pl.BlockSpec((1, LANES), lambda i: (0, 0))],
        out_shape=[jax.ShapeDtypeStruct((t, D_MODEL), F32), jax.ShapeDtypeStruct((t, HALF), U32),
                   jax.ShapeDtypeStruct((t, LANES), F32), jax.ShapeDtypeStruct((1, LANES), F32)],
        scratch_shapes=[pltpu.VMEM((1, LANES), F32)],
        compiler_params=_params("arbitrary"),
        name="route",
    )(x, oa, od, wo, g, wr)


def _positions_kernel(info_ref, cnt_ref, ltri_ref, utri_ref, pos_ref, run_scr, off_scr):
    info = info_ref[...]
    lane = lax.broadcasted_iota(jnp.int32, info.shape, 1).astype(F32)
    hit1 = lane == info[:, INFO_E1:INFO_E1 + 1]
    hit2 = lane == info[:, INFO_E2:INFO_E2 + 1]
    onehot = jnp.logical_or(hit1, hit2).astype(F32)

    @pl.when(pl.program_id(0) == 0)
    def _():
        ln = lax.broadcasted_iota(jnp.int32, cnt_ref.shape, 1)
        is_expert = jnp.logical_and(ln >= ROUTER_OFF, ln < ROUTER_OFF + N_EXPERTS)
        tiles = jnp.where(is_expert, jnp.maximum(jnp.floor((cnt_ref[...] + (MOE_TM - 1)) * (1.0 / MOE_TM)), 1.0), 0.0)
        off_scr[...] = MOE_TM * jnp.dot(tiles.astype(BF16), utri_ref[...], preferred_element_type=F32)
        run_scr[...] = jnp.zeros(run_scr.shape, F32)

    before = (jnp.dot(ltri_ref[...], onehot.astype(BF16), preferred_element_type=F32)
              + run_scr[...] + off_scr[...])
    pos1 = jnp.sum(jnp.where(hit1, before, 0.0), axis=-1, keepdims=True)
    pos2 = jnp.sum(jnp.where(hit2, before, 0.0), axis=-1, keepdims=True)
    both = jnp.where(lane == 0, pos1, 0.0) + jnp.where(lane == 1, pos2, 0.0)
    pos_ref[...] = both.T.astype(jnp.int32)
    run_scr[...] += jnp.sum(onehot, axis=0, keepdims=True)


def _positions(info, cnt):
    t = info.shape[0]
    tm = min(t, POS_TM)
    tok = np.arange(tm)
    ltri = jnp.asarray((tok[:, None] > tok[None, :]).astype(np.float32), dtype=BF16)
    ln = np.arange(LANES)
    utri = jnp.asarray((ln[:, None] < ln[None, :]).astype(np.float32), dtype=BF16)
    full = lambda a: pl.BlockSpec(a.shape, lambda i: (0,) * a.ndim)
    return pl.pallas_call(
        _positions_kernel,
        grid=(t // tm,),
        in_specs=[pl.BlockSpec((tm, LANES), lambda i: (i, 0)), full(cnt), full(ltri), full(utri)],
        out_specs=pl.BlockSpec((LANES, tm), lambda i: (0, i)),
        out_shape=jax.ShapeDtypeStruct((LANES, t), jnp.int32),
        scratch_shapes=[pltpu.VMEM((1, LANES), F32), pltpu.VMEM((1, LANES), F32)],
        compiler_params=_params("arbitrary"),
        name="positions",
    )(info, cnt, ltri, utri)


def _experts_kernel(te_ref, tv_ref, nt_ref, xs_ref, wg_hbm, wu_hbm, wd_hbm, xn_new_ref, gate_new_ref,
                    ys_ref, moe_new_ref, wg_s, wu_s, wd_s, wg_f, wu_f, wd_f, wsem):
    i = pl.program_id(0)
    used = i < nt_ref[0]
    expert = te_ref[i]

    def fetch(e):
        slot = e % 2
        return [pltpu.make_async_copy(src.at[e], dst.at[slot], wsem.at[slot, j])
                for j, (src, dst) in enumerate(((wg_hbm, wg_f), (wu_hbm, wu_f), (wd_hbm, wd_f)))]

    @pl.when(jnp.logical_or(i == 0, expert != te_ref[jnp.maximum(i - 1, 0)]))
    def _():
        @pl.when(i == 0)
        def _():
            for c in fetch(expert):
                c.start()
        for c in fetch(expert):
            c.wait()

        @pl.when(expert + 1 < N_EXPERTS)
        def _():
            for c in fetch(expert + 1):
                c.start()
        slot = expert % 2
        wg_s[...] = wg_f[slot].astype(BF16)
        wu_s[...] = wu_f[slot].astype(BF16)
        wd_s[...] = wd_f[slot].astype(BF16)
        xn = xn_new_ref[...]
        lane = lax.broadcasted_iota(jnp.int32, gate_new_ref.shape, 1)
        gate = jnp.sum(jnp.where(lane == expert + ROUTER_OFF, gate_new_ref[...], 0.0), axis=-1, keepdims=True)
        hg = jnp.dot(xn, wg_s[...], preferred_element_type=F32)
        hu = jnp.dot(xn, wu_s[...], preferred_element_type=F32)
        hm = _silu(hg) * hu * gate
        y = jnp.dot(hm.astype(BF16), wd_s[...], preferred_element_type=F32)

        @pl.when(i == 0)
        def _():
            moe_new_ref[...] = y

        @pl.when(i > 0)
        def _():
            moe_new_ref[...] += y

    @pl.when(used)
    def _():
        row = lax.broadcasted_iota(jnp.int32, xs_ref.shape, 0)
        x_lo, x_hi = _unpack_rows(jnp.where(row < tv_ref[i], xs_ref[...], jnp.uint32(0)))
        x_lo = x_lo.astype(BF16)
        x_hi = x_hi.astype(BF16)
        up = lambda w_s: (jnp.dot(x_lo, w_s[:HALF, :], preferred_element_type=F32)
                          + jnp.dot(x_hi, w_s[HALF:, :], preferred_element_type=F32))
        hm = (_silu_tanh(up(wg_s)) * up(wu_s)).astype(BF16)
        ys_ref[...] = _pack_rows(jnp.dot(hm, wd_s[...], preferred_element_type=F32))

    @pl.when(jnp.logical_not(used))
    def _():
        ys_ref[...] = jnp.zeros(ys_ref.shape, U32)


def _experts(xs, tile_expert, tile_valid, n_tiles, wg, wu, wd, xn_new, gate_new):
    max_tiles = xs.shape[0] // MOE_TM
    rows = pl.BlockSpec((MOE_TM, HALF), lambda i, te, tv, nt: (i, 0))
    hbm = pl.BlockSpec(memory_space=pl.ANY)
    full = lambda a: pl.BlockSpec(a.shape, lambda i, te, tv, nt: (0,) * a.ndim)
    return pl.pallas_call(
        _experts_kernel,
        grid_spec=pltpu.PrefetchScalarGridSpec(
            num_scalar_prefetch=3, grid=(max_tiles,),
            in_specs=[rows, hbm, hbm, hbm, full(xn_new), full(gate_new)],
            out_specs=[rows, pl.BlockSpec(xn_new.shape, lambda i, te, tv, nt: (0, 0))],
            scratch_shapes=[pltpu.VMEM((D_MODEL, D_EXPERT), BF16), pltpu.VMEM((D_MODEL, D_EXPERT), BF16),
                            pltpu.VMEM((D_EXPERT, D_MODEL), BF16),
                            pltpu.VMEM((2, D_MODEL, D_EXPERT), F32), pltpu.VMEM((2, D_MODEL, D_EXPERT), F32),
                            pltpu.VMEM((2, D_EXPERT, D_MODEL), F32), pltpu.SemaphoreType.DMA((2, 3))]),
        out_shape=[jax.ShapeDtypeStruct(xs.shape, U32), jax.ShapeDtypeStruct(xn_new.shape, F32)],
        compiler_params=_params("arbitrary"),
        name="experts",
    )(tile_expert, tile_valid, n_tiles, xs, wg, wu, wd, xn_new, gate_new)


SC_IDX = 128
SC_ROWS = 64
SC_WORKERS = 32
SC_GATHER_ROWS = 64
SC_GATHER_BUFS = 3


def _sc_mesh():
    return plsc.VectorSubcoreMesh(core_axis_name="c", subcore_axis_name="s")


def _sc_windows(t, fn):
    per_worker = t // SC_WORKERS
    worker = lax.axis_index(("c", "s"))

    @pl.loop(0, per_worker // SC_IDX)
    def _(w):
        fn(worker * per_worker + w * SC_IDX)


def _sc_scatter_rows(xn, pos1, pos2, n_rows):
    t, d = xn.shape
    assert t % (SC_WORKERS * SC_IDX) == 0
    idx_t = pltpu.VMEM((1, SC_IDX), jnp.int32)

    @pl.kernel(out_type=jax.ShapeDtypeStruct((n_rows, d), xn.dtype), mesh=_sc_mesh(),
               scratch_types=[idx_t, idx_t, pltpu.VMEM((SC_ROWS, d), xn.dtype)])
    def scatter(x_hbm, p1_hbm, p2_hbm, o_hbm, i1_v, i2_v, buf):
        def window(base):
            pltpu.sync_copy(p1_hbm.at[:, pl.ds(base, SC_IDX)], i1_v)
            pltpu.sync_copy(p2_hbm.at[:, pl.ds(base, SC_IDX)], i2_v)
            for k in range(SC_IDX // SC_ROWS):
                pltpu.sync_copy(x_hbm.at[pl.ds(base + k * SC_ROWS, SC_ROWS)], buf)
                pltpu.sync_copy(buf, o_hbm.at[i1_v.at[0, pl.ds(k * SC_ROWS, SC_ROWS)]])
                pltpu.sync_copy(buf, o_hbm.at[i2_v.at[0, pl.ds(k * SC_ROWS, SC_ROWS)]])
        _sc_windows(t, window)

    return scatter(xn, pos1.reshape(1, t), pos2.reshape(1, t))


def _sc_gather_rows(ys, pos1, pos2):
    t = pos1.shape[0]
    d = ys.shape[1]
    assert t % (SC_WORKERS * SC_IDX) == 0
    per_worker = t // SC_WORKERS
    idx_t = pltpu.VMEM((1, per_worker), jnp.int32)
    out = jax.ShapeDtypeStruct((t, d), ys.dtype)

    nbuf, rows = SC_GATHER_BUFS, SC_GATHER_ROWS
    buf_t = pltpu.VMEM((rows, d), ys.dtype)

    @pl.kernel(out_type=(out, out), mesh=_sc_mesh(),
               scratch_types=[idx_t, idx_t] + [buf_t] * nbuf
                             + [pltpu.SemaphoreType.DMA((nbuf,)), pltpu.SemaphoreType.DMA((nbuf,))])
    def gather(y_hbm, p1_hbm, p2_hbm, o1_hbm, o2_hbm, i1_v, i2_v, *rest):
        bufs, (gsem, wsem) = rest[:nbuf], rest[nbuf:]
        base = lax.axis_index(("c", "s")) * per_worker
        pltpu.sync_copy(p1_hbm.at[:, pl.ds(base, per_worker)], i1_v)
        pltpu.sync_copy(p2_hbm.at[:, pl.ds(base, per_worker)], i2_v)
        items = [(idx_v, o_hbm, k) for k in range(per_worker // rows)
                 for idx_v, o_hbm in ((i1_v, o1_hbm), (i2_v, o2_hbm))]
        n_items = len(items)

        def read(n):
            idx_v, _, k = items[n]
            return pltpu.make_async_copy(y_hbm.at[idx_v.at[0, pl.ds(k * rows, rows)]],
                                         bufs[n % nbuf], gsem.at[n % nbuf])

        def write(n):
            _, o_hbm, k = items[n]
            return pltpu.make_async_copy(bufs[n % nbuf], o_hbm.at[pl.ds(base + k * rows, rows)],
                                         wsem.at[n % nbuf])

        for n in range(min(nbuf - 1, n_items)):
            read(n).start()
        waited = 0
        for n in range(n_items):
            read(n).wait()
            write(n).start()
            ahead = n + nbuf - 1
            if ahead < n_items:
                if n >= 1:
                    write(n - 1).wait()
                    waited = n
                read(ahead).start()
        for n in range(waited, n_items):
            write(n).wait()

    return gather(ys, pos1.reshape(1, t), pos2.reshape(1, t))


TAIL_CHUNK = 256


H_BUFS = 3


def _ple_sparse_kernel(h_hbm, info_ref, y1_ref, y2_ref, p_ref, wpp_ref, wpg_ref, gp_ref, gf_ref, y_ref,
                       h_buf, h_sem):
    i = pl.program_id(0)
    n = pl.num_programs(0)
    tm = y_ref.shape[0]

    def fetch(step):
        slot = step % H_BUFS
        return pltpu.make_async_copy(h_hbm.at[pl.ds(pl.multiple_of(step * tm, tm), tm), :],
                                     h_buf.at[slot], h_sem.at[slot])

    @pl.when(i == 0)
    def _():
        for s in range(H_BUFS - 1):
            fetch(s).start()

    @pl.when(i + (H_BUFS - 1) < n)
    def _():
        fetch(i + (H_BUFS - 1)).start()

    fetch(i).wait()
    h_ref = h_buf.at[i % H_BUFS]
    for c in range(0, tm, TAIL_CHUNK):
        rows = pl.ds(c, TAIL_CHUNK)
        info = info_ref[rows, :]
        g1 = info[:, INFO_G1:INFO_G1 + 1]
        g2 = info[:, INFO_G2:INFO_G2 + 1]
        y1_lo, y1_hi = _unpack_rows(y1_ref[rows, :])
        y2_lo, y2_hi = _unpack_rows(y2_ref[rows, :])
        moe = jnp.concatenate([g1 * y1_lo + g2 * y2_lo, g1 * y1_hi + g2 * y2_hi], axis=1)
        h = h_ref[rows, :] + moe
        hn = _rmsnorm(h, gp_ref[...])
        pp_half = _mm(p_ref[rows, :], wpp_ref[...])
        h = h + pp_half * jnp.tanh(_mm(hn, wpg_ref[...])) + pp_half
        y_ref[rows, :] = _rmsnorm(h, gf_ref[...])


def _ple_sparse(h, info, y1, y2, p, wpp, wpg, gp, gf):
    t = h.shape[0]
    tm = WIDE_TM
    row = lambda n: pl.BlockSpec((tm, n), lambda i: (i, 0))
    full = lambda a: pl.BlockSpec(a.shape, lambda i: (0,) * a.ndim)
    assert t // tm >= H_BUFS - 1
    return pl.pallas_call(
        _ple_sparse_kernel,
        grid=(t // tm,),
        in_specs=[pl.BlockSpec(memory_space=pl.ANY), row(LANES), row(HALF), row(HALF), row(PLE_DIM),
                  full(wpp), full(wpg), full(gp), full(gf)],
        out_specs=row(D_MODEL),
        out_shape=jax.ShapeDtypeStruct((t, D_MODEL), F32),
        scratch_shapes=[pltpu.VMEM((H_BUFS, tm, D_MODEL), F32), pltpu.SemaphoreType.DMA((H_BUFS,))],
        compiler_params=_params("arbitrary"),
        name="ple_sparse",
    )(h, info, y1, y2, p, wpp, wpg, gp, gf)


def _tile_tables(cnt, max_tiles):
    tiles_e = jnp.maximum((cnt + (MOE_TM - 1)) // MOE_TM, 1)
    ends = jnp.cumsum(tiles_e)
    n_tiles = ends[-1]
    tile = jnp.arange(max_tiles, dtype=jnp.int32)
    idx = jnp.minimum(tile, n_tiles - 1)
    tile_expert = jnp.sum((idx[:, None] >= ends[None, :]).astype(jnp.int32), axis=1)
    mine = tile_expert[:, None] == jnp.arange(N_EXPERTS, dtype=jnp.int32)[None, :]
    of_mine = lambda v: jnp.sum(jnp.where(mine, v[None, :], 0), axis=1)
    valid = jnp.clip(of_mine(cnt) - (idx - of_mine(ends - tiles_e)) * MOE_TM, 0, MOE_TM)
    tile_valid = jnp.where(tile < n_tiles, valid, 0).astype(jnp.int32)
    return tile_expert, tile_valid, n_tiles.reshape(1)


def _ple_final_kernel(h_ref, m_ref, p_ref, wpp_ref, wpg_ref, gp_ref, gf_ref, y_ref):
    h = h_ref[...] + m_ref[...]
    hn = _rmsnorm(h, gp_ref[...])
    h = h + _mm(p_ref[...], wpp_ref[...]) * _sigmoid(_mm(hn, wpg_ref[...]))
    y_ref[...] = _rmsnorm(h, gf_ref[...])


def _ple_final(h, m, p, wpp, wpg, gp, gf):
    t = h.shape[0]
    tm = min(t, 256)
    row = lambda n: pl.BlockSpec((tm, n), lambda i: (i, 0))
    full = lambda a: pl.BlockSpec(a.shape, lambda i: (0,) * a.ndim)
    return pl.pallas_call(
        _ple_final_kernel,
        grid=(t // tm,),
        in_specs=[row(D_MODEL), row(D_MODEL), row(PLE_DIM), full(wpp), full(wpg), full(gp), full(gf)],
        out_specs=row(D_MODEL),
        out_shape=jax.ShapeDtypeStruct((t, D_MODEL), F32),
        compiler_params=_params("parallel"),
        name="ple_final",
    )(h, m, p, wpp, wpg, gp, gf)


def kernel(x_prompt, x_sample, p_prompt, p_sample, cache_k, cache_v, state_conv, state_S, rel_bias, norm_mix, w_in, att_sink, conv_w, dn_A_log, dn_dt_bias, dn_norm, w_out, norm_ffn, w_router_group, w_router_expert, w_gate, w_up, w_down, w_ple_proj, w_ple_gate, norm_ple, norm_final):
    batch, seq, _ = x_prompt.shape
    nseq = x_sample.shape[0]
    assert x_sample.shape[1] == 1 and norm_mix.shape[0] == 1 and cache_k.shape[2] == WINDOW
    assert seq % GDN_TB == 0 and seq % ATT_BLOCK == 0

    wt = jnp.swapaxes(w_in[0], 0, 1)
    o_db = ATT_COLS + CONV_CH
    w_in_re = (wt[:o_db].astype(BF16), wt[o_db + 2 * DN_HEADS:].astype(BF16),
               jnp.pad(wt[o_db:o_db + 2 * DN_HEADS], ((0, LANES - 2 * DN_HEADS), (0, 0))).astype(BF16))
    row = lambda a: a.reshape(1, -1).astype(F32)
    pad_lanes = lambda a, off: jnp.zeros((1, LANES), F32).at[0, off:off + a.shape[0]].set(a)
    alog = pad_lanes(dn_A_log[0], DN_HEADS)
    dtb = pad_lanes(dn_dt_bias[0], DN_HEADS)
    dnx = jnp.tile(dn_norm[0], DN_HEADS).reshape(1, DN_WIDTH)
    w_router = jnp.concatenate(
        [w_router_group[0], w_router_expert[0],
         jnp.zeros((D_MODEL, LANES - N_GROUPS - N_EXPERTS), F32)], axis=1).astype(BF16)
    wo = w_out[0].astype(BF16)
    wg, wu, wd = w_gate[0], w_up[0], w_down[0]
    wpp, wpg = w_ple_proj[0].astype(BF16), w_ple_gate[0].astype(BF16)
    sink = att_sink[0]

    qi = np.arange(ATT_BLOCK)[:, None]
    kj = np.arange(2 * ATT_BLOCK)[None, :]
    bucket_p = jnp.asarray(_t5_bucket_np(qi + ATT_BLOCK - kj))
    bucket_s = jnp.asarray(_t5_bucket_np(WINDOW - np.arange(WINDOW)[None, :]))

    xp = x_prompt.reshape(batch * seq, D_MODEL)
    att_p, qkv_p, dz_p, ba_p, xc_tails = _inproj_conv(xp, row(norm_mix[0]), w_in_re, conv_w[0], seq)
    o_att_p = _attn_prompt(att_p, bucket_p, rel_bias, sink, batch, seq)
    o_dn_p, s_p = _gdn_prompt(qkv_p, dz_p, ba_p, alog, dtb, dnx, batch, seq)
    h1, xn2, info, cnt = _route_sparse(xp, o_att_p, o_dn_p, wo, row(norm_ffn[0]), w_router)
    pos = _positions(info, cnt)
    pos1, pos2 = pos[0], pos[1]
    max_tiles = _moe_tiles(batch * seq)
    cnt_e = cnt[0, ROUTER_OFF:ROUTER_OFF + N_EXPERTS].astype(jnp.int32)
    tile_expert, tile_valid, n_tiles = _tile_tables(cnt_e, max_tiles)
    xs_sorted = _sc_scatter_rows(xn2, pos1, pos2, max_tiles * MOE_TM)

    xs = x_sample.reshape(nseq, D_MODEL)
    att_s, xc_s, dz_s, ba_s = _inproj(xs, row(norm_mix[0]), w_in_re)
    ck_t = jnp.transpose(cache_k[0], (0, 2, 3, 1))
    cv_t = jnp.transpose(cache_v[0], (0, 2, 3, 1))
    o_att_s, ks_t, vs_t = _attn_sample(att_s, ck_t, cv_t, bucket_s, rel_bias, sink)
    sconv_t = jnp.swapaxes(state_conv[0], 0, 1)
    o_dn_s_t, s_s_t = _gdn_sample_lanes(xc_s, dz_s, ba_s, sconv_t, jnp.transpose(state_S[0], (1, 2, 3, 0)),
                                        conv_w[0], alog, dtb, dn_norm[0])
    s_s = jnp.transpose(s_s_t, (3, 0, 1, 2))

    h1_s, xn2_s, gates_s = _outproj_router(xs, o_att_s, o_dn_s_t, wo, row(norm_ffn[0]), w_router)

    ys, moe_s = _experts(xs_sorted, tile_expert, tile_valid, n_tiles, wg, wu, wd, xn2_s, gates_s)
    y1, y2 = _sc_gather_rows(ys, pos1, pos2)
    y_s = _ple_final(h1_s, moe_s, p_sample[0].reshape(nseq, PLE_DIM), wpp, wpg, row(norm_ple[0]),
                     row(norm_final))
    y_p = _ple_sparse(h1, info, y1, y2, p_prompt[0].reshape(batch * seq, PLE_DIM),
                      (0.5 * w_ple_proj[0]).astype(BF16), (0.5 * w_ple_gate[0]).astype(BF16),
                      row(norm_ple[0]), row(norm_final))

    att_p3 = att_p.reshape(batch, seq, ATT_COLS)
    kv_shape = (1, batch, WINDOW, ATT_KV_HEADS, HEAD_DIM)
    k_p = att_p3[:, seq - WINDOW:, ATT_WIDTH:ATT_WIDTH + KV_WIDTH].reshape(kv_shape)
    v_p = att_p3[:, seq - WINDOW:, ATT_WIDTH + KV_WIDTH:].reshape(kv_shape)
    conv_p = xc_tails.reshape(batch, -1, TAIL, CONV_CH)[:, -1, TAIL - (CONV_WIDTH - 1):][None]
    k_s = jnp.transpose(ks_t, (0, 3, 1, 2))[None]
    v_s = jnp.transpose(vs_t, (0, 3, 1, 2))[None]
    conv_s = jnp.concatenate([state_conv[0][:, 1:], xc_s[:, None, :]], axis=1)[None]
    return (y_p.reshape(batch, seq, D_MODEL), y_s.reshape(nseq, 1, D_MODEL),
            k_p, v_p, conv_p, s_p[None], k_s, v_s, conv_s, s_s[None])
```
